```python
import math
import jax, jax.numpy as jnp
from jax import lax
import numpy as np

D_MODEL = 1024
BATCH = 32
SEQ = 2048
DEPTH = 1

HEAD_DIM = 64
DIL_GROUPS = ((128, 1), (512, 4), (2048, 16))
A_HEADS_PER_GROUP = 4
A_HEADS = A_HEADS_PER_GROUP * len(DIL_GROUPS)
A_OUT = A_HEADS_PER_GROUP * HEAD_DIM
B_Q_HEADS = 8
B_KV_HEADS = 2
B_WINDOW = 128
B_OUT = B_Q_HEADS * HEAD_DIM
BLOCK = 128
N_BUCKETS = 32
MAX_DISTANCE = 2048
TOTAL_HEADS = A_HEADS + B_Q_HEADS
D_FF = 2816
EPS = 1e-6
A_QKV_W = A_HEADS * HEAD_DIM
B_KV_W = B_KV_HEADS * HEAD_DIM
SPLITS = (A_QKV_W, A_QKV_W, A_QKV_W, B_OUT, B_KV_W, B_KV_W, D_MODEL, D_MODEL)
D_IN = sum(SPLITS)

kernel_name = "hybrid_dilated_swa_sink_macaron"


def rms_norm(x, g):
    x32 = x.astype(jnp.float32)
    y = x32 * lax.rsqrt(jnp.mean(x32 * x32, axis=-1, keepdims=True) + EPS)
    return (y * g.astype(jnp.float32)).astype(x.dtype)


def swiglu(x, w_gate, w_up, w_down):
    return (jax.nn.silu(x @ w_gate) * (x @ w_up)) @ w_down


def t5_bucket(dist):
    max_exact = N_BUCKETS // 2
    n = jnp.maximum(dist, 0)
    nf = jnp.maximum(n, 1).astype(jnp.float32)
    large = max_exact + (jnp.log(nf / max_exact) / math.log(MAX_DISTANCE / max_exact)
                         * (N_BUCKETS - max_exact)).astype(jnp.int32)
    large = jnp.minimum(large, N_BUCKETS - 1)
    return jnp.where(n < max_exact, n, large)


def band_geometry(max_steps):
    qi = jnp.arange(BLOCK)[:, None]
    ki = jnp.arange(2 * BLOCK)[None, :]
    dist = qi + BLOCK - ki
    band = (dist >= 0) & (dist <= max_steps)
    return dist, band


def rel_bias_block(table, dist, stride, h0, h1):
    b = table[t5_bucket(dist * stride)][..., h0:h1]
    return jnp.transpose(b, (2, 0, 1))


def banded_attention(q, k, v, bias, max_steps, sinks=None):
    N, L, H, hd = q.shape
    hkv = k.shape[2]
    g = H // hkv
    nb = -(-L // BLOCK)
    lp = nb * BLOCK
    pad = ((0, 0), (0, lp - L), (0, 0), (0, 0))
    qb = jnp.pad(q, pad).reshape(N, nb, BLOCK, hkv, g, hd)
    kb = jnp.pad(k, pad).reshape(N, nb, BLOCK, hkv, hd)
    vb = jnp.pad(v, pad).reshape(N, nb, BLOCK, hkv, hd)

    def with_prev(t):
        prev = jnp.pad(t, ((0, 0), (1, 0), (0, 0), (0, 0), (0, 0)))[:, :-1]
        return jnp.concatenate([prev, t], axis=2)

    kc, vc = with_prev(kb), with_prev(vb)
    _, band = band_geometry(max_steps)
    first = (jnp.arange(nb) == 0)[:, None, None]
    before_start = (jnp.arange(2 * BLOCK) < BLOCK)[None, None, :]
    valid = band[None] & ~(first & before_start)

    logits = jnp.einsum('nbqhgd,nbkhd->nbhgqk', qb, kc).astype(jnp.float32) * (hd ** -0.5)
    logits = logits + bias.astype(jnp.float32).reshape(hkv, g, BLOCK, 2 * BLOCK)
    logits = jnp.where(valid[None, :, None, None], logits, -jnp.inf)
    m = jnp.max(logits, axis=-1)
    if sinks is not None:
        s = sinks.astype(jnp.float32).reshape(hkv, g)[:, :, None]
        m = jnp.maximum(m, s)
    p = jnp.exp(logits - m[..., None])
    denom = jnp.sum(p, axis=-1)
    if sinks is not None:
        denom = denom + jnp.exp(s - m)
    out = jnp.einsum('nbhgqk,nbkhd->nbqhgd', p.astype(vc.dtype), vc)
    out = (out / jnp.transpose(denom, (0, 1, 4, 2, 3))[..., None]).astype(q.dtype)
    out = out.reshape(N, lp, H, hd)[:, :L]
    lse = jnp.transpose(m + jnp.log(denom), (0, 1, 4, 2, 3)).reshape(N, lp, H)[:, :L]
    return out, lse


def fold_residues(t, d):
    B, S, H, hd = t.shape
    t = t.reshape(B, S // d, d, H, hd)
    return jnp.transpose(t, (0, 2, 1, 3, 4)).reshape(B * d, S // d, H, hd)


def unfold_residues(t, B, d):
    Bd, Ld = t.shape[:2]
    rest = t.shape[2:]
    t = t.reshape((B, d, Ld) + rest)
    perm = (0, 2, 1) + tuple(range(3, t.ndim))
    return jnp.transpose(t, perm).reshape((B, d * Ld) + rest)


def dilated_attention(q, k, v, table):
    B, S = q.shape[:2]
    outs, lses = [], []
    for gi, (window, d) in enumerate(DIL_GROUPS):
        h0, h1 = gi * A_HEADS_PER_GROUP, (gi + 1) * A_HEADS_PER_GROUP
        steps = window // d
        dist, _ = band_geometry(steps)
        bias = rel_bias_block(table, dist, d, h0, h1)
        o, lse = banded_attention(fold_residues(q[:, :, h0:h1], d), fold_residues(k[:, :, h0:h1], d),
                                  fold_residues(v[:, :, h0:h1], d), bias, steps)
        outs.append(unfold_residues(o, B, d))
        lses.append(unfold_residues(lse, B, d))
    alpha = jax.nn.softmax(jnp.stack(lses, axis=0), axis=0)
    y = jnp.sum(alpha[..., None] * jnp.stack(outs, axis=0).astype(jnp.float32), axis=0)
    return y.reshape(B, S, A_OUT).astype(q.dtype)


def sliding_window_sink_attention(q, k, v, sinks, table):
    B, S = q.shape[:2]
    dist, _ = band_geometry(B_WINDOW - 1)
    bias = rel_bias_block(table, dist, 1, A_HEADS, TOTAL_HEADS)
    o, _ = banded_attention(q, k, v, bias, B_WINDOW - 1, sinks)
    return o.reshape(B, S, B_OUT)


def hybrid_layer(x, ffn1_norm, ffn1_w_gate, ffn1_w_up, ffn1_w_down, mix_norm, w_in, b_in,
                 w_branch_a, w_branch_b, w_out, sinks, rel_bias,
                 ffn2_norm, ffn2_w_gate, ffn2_w_up, ffn2_w_down):
    B, S, _ = x.shape
    h = x + 0.5 * swiglu(rms_norm(x, ffn1_norm), ffn1_w_gate, ffn1_w_up, ffn1_w_down)
    u = rms_norm(h, mix_norm)
    z = u @ w_in + b_in
    aq, ak, av, bq, bk, bv, ga, gb = jnp.split(z, np.cumsum(SPLITS)[:-1].tolist(), axis=-1)
    heads = lambda t: t.reshape(B, S, -1, HEAD_DIM)
    ya = dilated_attention(heads(aq), heads(ak), heads(av), rel_bias)
    yb = sliding_window_sink_attention(heads(bq), heads(bk), heads(bv), sinks, rel_bias)
    merged = jax.nn.sigmoid(ga) * (ya @ w_branch_a) + jax.nn.sigmoid(gb) * (yb @ w_branch_b)
    h = h + merged @ w_out
    h = h + 0.5 * swiglu(rms_norm(h, ffn2_norm), ffn2_w_gate, ffn2_w_up, ffn2_w_down)
    return h


def _fwd_setup_inputs(seed: int = 0) -> dict:
    key = jax.random.key(seed)
    ks = jax.random.split(key, 20)
    L = DEPTH

    def w(k, shape, fan_in):
        return jax.random.normal(k, shape, jnp.float32) * (fan_in ** -0.5)

    def gain(k, shape):
        return 1.0 + 0.01 * jax.random.normal(k, shape, jnp.float32)

    return {
        "x": jax.random.normal(ks[0], (BATCH, SEQ, D_MODEL), jnp.float32),
        "ffn1_norm": gain(ks[1], (L, D_MODEL)),
        "ffn1_w_gate": w(ks[2], (L, D_MODEL, D_FF), D_MODEL),
        "ffn1_w_up": w(ks[3], (L, D_MODEL, D_FF), D_MODEL),
        "ffn1_w_down": w(ks[4], (L, D_FF, D_MODEL), D_FF),
        "mix_norm": gain(ks[5], (L, D_MODEL)),
        "w_in": w(ks[6], (L, D_MODEL, D_IN), D_MODEL),
        "b_in": 0.01 * jax.random.normal(ks[7], (L, D_IN), jnp.float32),
        "w_branch_a": w(ks[8], (L, A_OUT, D_MODEL), A_OUT),
        "w_branch_b": w(ks[9], (L, B_OUT, D_MODEL), B_OUT),
        "w_out": w(ks[10], (L, D_MODEL, D_MODEL), D_MODEL),
        "sinks": 0.5 * jax.random.normal(ks[11], (L, B_Q_HEADS), jnp.float32),
        "rel_bias": 0.5 * jax.random.normal(ks[12], (N_BUCKETS, TOTAL_HEADS), jnp.float32),
        "ffn2_norm": gain(ks[13], (L, D_MODEL)),
        "ffn2_w_gate": w(ks[14], (L, D_MODEL, D_FF), D_MODEL),
        "ffn2_w_up": w(ks[15], (L, D_MODEL, D_FF), D_MODEL),
        "ffn2_w_down": w(ks[16], (L, D_FF, D_MODEL), D_FF),
        "final_norm": gain(ks[17], (D_MODEL,)),
    }


def _fwd_reference(x, ffn1_norm, ffn1_w_gate, ffn1_w_up, ffn1_w_down, mix_norm, w_in, b_in,
              w_branch_a, w_branch_b, w_out, sinks, rel_bias,
              ffn2_norm, ffn2_w_gate, ffn2_w_up, ffn2_w_down, final_norm):
    h = x
    for l in range(DEPTH):
        h = hybrid_layer(h, ffn1_norm[l], ffn1_w_gate[l], ffn1_w_up[l], ffn1_w_down[l], mix_norm[l],
                         w_in[l], b_in[l], w_branch_a[l], w_branch_b[l], w_out[l], sinks[l], rel_bias,
                         ffn2_norm[l], ffn2_w_gate[l], ffn2_w_up[l], ffn2_w_down[l])
    return rms_norm(h, final_norm)


import jax as _jax
import jax.numpy as _jnp

TWIN_FORMAT = 'train_step'
FWD_PARAMS = ['x', 'ffn1_norm', 'ffn1_w_gate', 'ffn1_w_up', 'ffn1_w_down', 'mix_norm', 'w_in', 'b_in', 'w_branch_a', 'w_branch_b', 'w_out', 'sinks', 'rel_bias', 'ffn2_norm', 'ffn2_w_gate', 'ffn2_w_up', 'ffn2_w_down', 'final_norm']
TWIN_WEIGHTS = ['ffn1_norm', 'ffn1_w_gate', 'ffn1_w_up', 'ffn1_w_down', 'mix_norm', 'w_in', 'b_in', 'w_branch_a', 'w_branch_b', 'w_out', 'sinks', 'rel_bias', 'ffn2_norm', 'ffn2_w_gate', 'ffn2_w_up', 'ffn2_w_down', 'final_norm']
TWIN_DIFF_INPUT = 'x'
TWIN_INPUTS = ['x', 'ffn1_norm', 'ffn1_w_gate', 'ffn1_w_up', 'ffn1_w_down', 'mix_norm', 'w_in', 'b_in', 'w_branch_a', 'w_branch_b', 'w_out', 'sinks', 'rel_bias', 'ffn2_norm', 'ffn2_w_gate', 'ffn2_w_up', 'ffn2_w_down', 'final_norm', 'loss_target', 'm_ffn1_norm', 'm_ffn1_w_gate', 'm_ffn1_w_up', 'm_ffn1_w_down', 'm_mix_norm', 'm_w_in', 'm_b_in', 'm_w_branch_a', 'm_w_branch_b', 'm_w_out', 'm_sinks', 'm_rel_bias', 'm_ffn2_norm', 'm_ffn2_w_gate', 'm_ffn2_w_up', 'm_ffn2_w_down', 'm_final_norm', 'v_ffn1_norm', 'v_ffn1_w_gate', 'v_ffn1_w_up', 'v_ffn1_w_down', 'v_mix_norm', 'v_w_in', 'v_b_in', 'v_w_branch_a', 'v_w_branch_b', 'v_w_out', 'v_sinks', 'v_rel_bias', 'v_ffn2_norm', 'v_ffn2_w_gate', 'v_ffn2_w_up', 'v_ffn2_w_down', 'v_final_norm']
TWIN_OUTPUTS = ['loss', 'grad_x', 'grad_ffn1_norm', 'grad_ffn1_w_gate', 'grad_ffn1_w_up', 'grad_ffn1_w_down', 'grad_mix_norm', 'grad_w_in', 'grad_b_in', 'grad_w_branch_a', 'grad_w_branch_b', 'grad_w_out', 'grad_sinks', 'grad_rel_bias', 'grad_ffn2_norm', 'grad_ffn2_w_gate', 'grad_ffn2_w_up', 'grad_ffn2_w_down', 'grad_final_norm', 'delta_ffn1_norm', 'delta_ffn1_w_gate', 'delta_ffn1_w_up', 'delta_ffn1_w_down', 'delta_mix_norm', 'delta_w_in', 'delta_b_in', 'delta_w_branch_a', 'delta_w_branch_b', 'delta_w_out', 'delta_sinks', 'delta_rel_bias', 'delta_ffn2_norm', 'delta_ffn2_w_gate', 'delta_ffn2_w_up', 'delta_ffn2_w_down', 'delta_final_norm', 'new_m_ffn1_norm', 'new_m_ffn1_w_gate', 'new_m_ffn1_w_up', 'new_m_ffn1_w_down', 'new_m_mix_norm', 'new_m_w_in', 'new_m_b_in', 'new_m_w_branch_a', 'new_m_w_branch_b', 'new_m_w_out', 'new_m_sinks', 'new_m_rel_bias', 'new_m_ffn2_norm', 'new_m_ffn2_w_gate', 'new_m_ffn2_w_up', 'new_m_ffn2_w_down', 'new_m_final_norm', 'new_v_ffn1_norm', 'new_v_ffn1_w_gate', 'new_v_ffn1_w_up', 'new_v_ffn1_w_down', 'new_v_mix_norm', 'new_v_w_in', 'new_v_b_in', 'new_v_w_branch_a', 'new_v_w_branch_b', 'new_v_w_out', 'new_v_sinks', 'new_v_rel_bias', 'new_v_ffn2_norm', 'new_v_ffn2_w_gate', 'new_v_ffn2_w_up', 'new_v_ffn2_w_down', 'new_v_final_norm']
TWIN_LEAF_KINDS = {'loss': 'loss', 'grad_x': 'grad_x', 'grad_ffn1_norm': 'grad_w', 'grad_ffn1_w_gate': 'grad_w', 'grad_ffn1_w_up': 'grad_w', 'grad_ffn1_w_down': 'grad_w', 'grad_mix_norm': 'grad_w', 'grad_w_in': 'grad_w', 'grad_b_in': 'grad_w', 'grad_w_branch_a': 'grad_w', 'grad_w_branch_b': 'grad_w', 'grad_w_out': 'grad_w', 'grad_sinks': 'grad_w', 'grad_rel_bias': 'grad_w', 'grad_ffn2_norm': 'grad_w', 'grad_ffn2_w_gate': 'grad_w', 'grad_ffn2_w_up': 'grad_w', 'grad_ffn2_w_down': 'grad_w', 'grad_final_norm': 'grad_w', 'delta_ffn1_norm': 'delta_w', 'delta_ffn1_w_gate': 'delta_w', 'delta_ffn1_w_up': 'delta_w', 'delta_ffn1_w_down': 'delta_w', 'delta_mix_norm': 'delta_w', 'delta_w_in': 'delta_w', 'delta_b_in': 'delta_w', 'delta_w_branch_a': 'delta_w', 'delta_w_branch_b': 'delta_w', 'delta_w_out': 'delta_w', 'delta_sinks': 'delta_w', 'delta_rel_bias': 'delta_w', 'delta_ffn2_norm': 'delta_w', 'delta_ffn2_w_gate': 'delta_w', 'delta_ffn2_w_up': 'delta_w', 'delta_ffn2_w_down': 'delta_w', 'delta_final_norm': 'delta_w', 'new_m_ffn1_norm': 'new_m', 'new_m_ffn1_w_gate': 'new_m', 'new_m_ffn1_w_up': 'new_m', 'new_m_ffn1_w_down': 'new_m', 'new_m_mix_norm': 'new_m', 'new_m_w_in': 'new_m', 'new_m_b_in': 'new_m', 'new_m_w_branch_a': 'new_m', 'new_m_w_branch_b': 'new_m', 'new_m_w_out': 'new_m', 'new_m_sinks': 'new_m', 'new_m_rel_bias': 'new_m', 'new_m_ffn2_norm': 'new_m', 'new_m_ffn2_w_gate': 'new_m', 'new_m_ffn2_w_up': 'new_m', 'new_m_ffn2_w_down': 'new_m', 'new_m_final_norm': 'new_m', 'new_v_ffn1_norm': 'new_v', 'new_v_ffn1_w_gate': 'new_v', 'new_v_ffn1_w_up': 'new_v', 'new_v_ffn1_w_down': 'new_v', 'new_v_mix_norm': 'new_v', 'new_v_w_in': 'new_v', 'new_v_b_in': 'new_v', 'new_v_w_branch_a': 'new_v', 'new_v_w_branch_b': 'new_v', 'new_v_w_out': 'new_v', 'new_v_sinks': 'new_v', 'new_v_rel_bias': 'new_v', 'new_v_ffn2_norm': 'new_v', 'new_v_ffn2_w_gate': 'new_v', 'new_v_ffn2_w_up': 'new_v', 'new_v_ffn2_w_down': 'new_v', 'new_v_final_norm': 'new_v'}


def _forward(args):
    return _fwd_reference(*[args[k] for k in FWD_PARAMS])


def _output_shape():
    out = _jax.eval_shape(lambda: _forward(_fwd_setup_inputs(0)))
    return out.shape, out.dtype

N_MICROBATCH = 1
ADAM_LR = 0.001
ADAM_B1 = 0.9
ADAM_B2 = 0.999
ADAM_EPS = 1e-08
ADAM_WD = 0.01
ADAM_STEP = 10
PER_EXAMPLE_BATCH_AXIS = {'x': 0, 'loss_target': 0}
SHARED_INPUTS = []
_WEIGHT_DTYPES = {'ffn1_norm': _jnp.float32, 'ffn1_w_gate': _jnp.float32, 'ffn1_w_up': _jnp.float32, 'ffn1_w_down': _jnp.float32, 'mix_norm': _jnp.float32, 'w_in': _jnp.float32, 'b_in': _jnp.float32, 'w_branch_a': _jnp.float32, 'w_branch_b': _jnp.float32, 'w_out': _jnp.float32, 'sinks': _jnp.float32, 'rel_bias': _jnp.float32, 'ffn2_norm': _jnp.float32, 'ffn2_w_gate': _jnp.float32, 'ffn2_w_up': _jnp.float32, 'ffn2_w_down': _jnp.float32, 'final_norm': _jnp.float32}
MOMENT_SCALE = {'ffn1_norm': 1.236235e-01, 'ffn1_w_gate': 4.717447e-02, 'ffn1_w_up': 4.561864e-02, 'ffn1_w_down': 7.567674e-02, 'mix_norm': 6.542182e-02, 'w_in': 2.868342e-02, 'b_in': 7.900715e-02, 'w_branch_a': 2.676285e-02, 'w_branch_b': 2.920754e-02, 'w_out': 3.932685e-02, 'sinks': 2.000145e-02, 'rel_bias': 3.814828e-02, 'ffn2_norm': 1.017682e-01, 'ffn2_w_gate': 4.297533e-02, 'ffn2_w_up': 4.169186e-02, 'ffn2_w_down': 6.928271e-02, 'final_norm': 6.389807e+01}


def _to_microbatches(a, axis):
    t = _jnp.moveaxis(a, axis, 0)
    t = t.reshape((N_MICROBATCH, t.shape[0] // N_MICROBATCH) + t.shape[1:])
    return _jnp.moveaxis(t, 1, axis + 1)


def setup_inputs(seed: int = 0) -> dict:
    inp = _fwd_setup_inputs(seed)
    key = _jax.random.fold_in(_jax.random.key(seed), 7919)
    shape, _ = _output_shape()
    out = dict(inp)
    out["loss_target"] = _jax.random.normal(_jax.random.fold_in(key, 0), shape, _jnp.float32)
    for i, name in enumerate(TWIN_WEIGHTS):
        w = inp[name].astype(_jnp.float32)
        if MOMENT_SCALE is None:
            s = _jnp.sqrt(_jnp.mean(_jnp.square(w)) + 1e-30)
        else:
            s = MOMENT_SCALE[name]
        km, kv = _jax.random.split(_jax.random.fold_in(key, i + 1))
        out[name] = w
        out["m_" + name] = s * _jax.random.normal(km, w.shape, _jnp.float32)
        out["v_" + name] = (s * s) * _jax.random.uniform(kv, w.shape, _jnp.float32, 0.5, 1.5)
    if N_MICROBATCH > 1:
        for name, axis in PER_EXAMPLE_BATCH_AXIS.items():
            out[name] = _to_microbatches(out[name], axis)
    return {'x': out['x'], 'ffn1_norm': out['ffn1_norm'], 'ffn1_w_gate': out['ffn1_w_gate'], 'ffn1_w_up': out['ffn1_w_up'], 'ffn1_w_down': out['ffn1_w_down'], 'mix_norm': out['mix_norm'], 'w_in': out['w_in'], 'b_in': out['b_in'], 'w_branch_a': out['w_branch_a'], 'w_branch_b': out['w_branch_b'], 'w_out': out['w_out'], 'sinks': out['sinks'], 'rel_bias': out['rel_bias'], 'ffn2_norm': out['ffn2_norm'], 'ffn2_w_gate': out['ffn2_w_gate'], 'ffn2_w_up': out['ffn2_w_up'], 'ffn2_w_down': out['ffn2_w_down'], 'final_norm': out['final_norm'], 'loss_target': out['loss_target'], 'm_ffn1_norm': out['m_ffn1_norm'], 'm_ffn1_w_gate': out['m_ffn1_w_gate'], 'm_ffn1_w_up': out['m_ffn1_w_up'], 'm_ffn1_w_down': out['m_ffn1_w_down'], 'm_mix_norm': out['m_mix_norm'], 'm_w_in': out['m_w_in'], 'm_b_in': out['m_b_in'], 'm_w_branch_a': out['m_w_branch_a'], 'm_w_branch_b': out['m_w_branch_b'], 'm_w_out': out['m_w_out'], 'm_sinks': out['m_sinks'], 'm_rel_bias': out['m_rel_bias'], 'm_ffn2_norm': out['m_ffn2_norm'], 'm_ffn2_w_gate': out['m_ffn2_w_gate'], 'm_ffn2_w_up': out['m_ffn2_w_up'], 'm_ffn2_w_down': out['m_ffn2_w_down'], 'm_final_norm': out['m_final_norm'], 'v_ffn1_norm': out['v_ffn1_norm'], 'v_ffn1_w_gate': out['v_ffn1_w_gate'], 'v_ffn1_w_up': out['v_ffn1_w_up'], 'v_ffn1_w_down': out['v_ffn1_w_down'], 'v_mix_norm': out['v_mix_norm'], 'v_w_in': out['v_w_in'], 'v_b_in': out['v_b_in'], 'v_w_branch_a': out['v_w_branch_a'], 'v_w_branch_b': out['v_w_branch_b'], 'v_w_out': out['v_w_out'], 'v_sinks': out['v_sinks'], 'v_rel_bias': out['v_rel_bias'], 'v_ffn2_norm': out['v_ffn2_norm'], 'v_ffn2_w_gate': out['v_ffn2_w_gate'], 'v_ffn2_w_up': out['v_ffn2_w_up'], 'v_ffn2_w_down': out['v_ffn2_w_down'], 'v_final_norm': out['v_final_norm']}


def _loss(weights, diff, rest, loss_target):
    with _jax.named_scope("forward"):
        args = {**rest, TWIN_DIFF_INPUT: diff, **{k: w.astype(_WEIGHT_DTYPES[k]) for k, w in weights.items()}}
        y = _forward(args)
    with _jax.named_scope("loss_head"):
        err = _jnp.square(y.astype(_jnp.float32) - loss_target)
        return 0.5 * _jnp.sum(_jnp.mean(err, axis=-1)) if err.ndim else 0.5 * err


def _adamw(w, g, m, v):
    m = ADAM_B1 * m + (1.0 - ADAM_B1) * g
    v = ADAM_B2 * v + (1.0 - ADAM_B2) * _jnp.square(g)
    m_hat = m / (1.0 - ADAM_B1 ** ADAM_STEP)
    v_hat = v / (1.0 - ADAM_B2 ** ADAM_STEP)
    delta = -ADAM_LR * (m_hat / (_jnp.sqrt(v_hat) + ADAM_EPS) + ADAM_WD * w)
    return delta, m, v


def reference(x, ffn1_norm, ffn1_w_gate, ffn1_w_up, ffn1_w_down, mix_norm, w_in, b_in, w_branch_a, w_branch_b, w_out, sinks, rel_bias, ffn2_norm, ffn2_w_gate, ffn2_w_up, ffn2_w_down, final_norm, loss_target, m_ffn1_norm, m_ffn1_w_gate, m_ffn1_w_up, m_ffn1_w_down, m_mix_norm, m_w_in, m_b_in, m_w_branch_a, m_w_branch_b, m_w_out, m_sinks, m_rel_bias, m_ffn2_norm, m_ffn2_w_gate, m_ffn2_w_up, m_ffn2_w_down, m_final_norm, v_ffn1_norm, v_ffn1_w_gate, v_ffn1_w_up, v_ffn1_w_down, v_mix_norm, v_w_in, v_b_in, v_w_branch_a, v_w_branch_b, v_w_out, v_sinks, v_rel_bias, v_ffn2_norm, v_ffn2_w_gate, v_ffn2_w_up, v_ffn2_w_down, v_final_norm):
    given = dict(x=x, ffn1_norm=ffn1_norm, ffn1_w_gate=ffn1_w_gate, ffn1_w_up=ffn1_w_up, ffn1_w_down=ffn1_w_down, mix_norm=mix_norm, w_in=w_in, b_in=b_in, w_branch_a=w_branch_a, w_branch_b=w_branch_b, w_out=w_out, sinks=sinks, rel_bias=rel_bias, ffn2_norm=ffn2_norm, ffn2_w_gate=ffn2_w_gate, ffn2_w_up=ffn2_w_up, ffn2_w_down=ffn2_w_down, final_norm=final_norm, loss_target=loss_target, m_ffn1_norm=m_ffn1_norm, m_ffn1_w_gate=m_ffn1_w_gate, m_ffn1_w_up=m_ffn1_w_up, m_ffn1_w_down=m_ffn1_w_down, m_mix_norm=m_mix_norm, m_w_in=m_w_in, m_b_in=m_b_in, m_w_branch_a=m_w_branch_a, m_w_branch_b=m_w_branch_b, m_w_out=m_w_out, m_sinks=m_sinks, m_rel_bias=m_rel_bias, m_ffn2_norm=m_ffn2_norm, m_ffn2_w_gate=m_ffn2_w_gate, m_ffn2_w_up=m_ffn2_w_up, m_ffn2_w_down=m_ffn2_w_down, m_final_norm=m_final_norm, v_ffn1_norm=v_ffn1_norm, v_ffn1_w_gate=v_ffn1_w_gate, v_ffn1_w_up=v_ffn1_w_up, v_ffn1_w_down=v_ffn1_w_down, v_mix_norm=v_mix_norm, v_w_in=v_w_in, v_b_in=v_b_in, v_w_branch_a=v_w_branch_a, v_w_branch_b=v_w_branch_b, v_w_out=v_w_out, v_sinks=v_sinks, v_rel_bias=v_rel_bias, v_ffn2_norm=v_ffn2_norm, v_ffn2_w_gate=v_ffn2_w_gate, v_ffn2_w_up=v_ffn2_w_up, v_ffn2_w_down=v_ffn2_w_down, v_final_norm=v_final_norm)
    weights = {n: given[n] for n in TWIN_WEIGHTS}
    shared = {n: given[n] for n in SHARED_INPUTS}
    per_example = {n: given[n] for n in ['x']}
    grad_fn = _jax.value_and_grad(_loss, argnums=(0, 1))

    def one_microbatch(ex, loss_target):
        ex = dict(ex)
        diff = ex.pop(TWIN_DIFF_INPUT)
        return grad_fn(weights, diff, {**shared, **ex}, loss_target)

    if N_MICROBATCH == 1:
        loss, (grad_w, grad_x) = one_microbatch(per_example, given["loss_target"])
    else:
        def body(carry, xs):
            loss_sum, grad_sum = carry
            l_k, (gw_k, gx_k) = one_microbatch(xs[0], xs[1])
            with _jax.named_scope("update"):
                return (loss_sum + l_k, _jax.tree.map(_jnp.add, grad_sum, gw_k)), gx_k

        init = (_jnp.zeros((), _jnp.float32), _jax.tree.map(_jnp.zeros_like, weights))
        (loss, grad_w), grad_x = _jax.lax.scan(body, init, (per_example, given["loss_target"]))
    with _jax.named_scope("update"):
        delta_w, new_m, new_v = {}, {}, {}
        for n in TWIN_WEIGHTS:
            delta_w[n], new_m[n], new_v[n] = _adamw(weights[n], grad_w[n], given["m_" + n], given["v_" + n])
    return (loss, grad_x, *[grad_w[n] for n in TWIN_WEIGHTS], *[delta_w[n] for n in TWIN_WEIGHTS],
            *[new_m[n] for n in TWIN_WEIGHTS], *[new_v[n] for n in TWIN_WEIGHTS])
```

```python
import functools
import math

import jax
import jax.numpy as jnp
from jax import lax
from jax.experimental import pallas as pl
from jax.experimental.pallas import tpu as pltpu

F32, BF16 = jnp.float32, jnp.bfloat16
MESH = pl.DeviceIdType.MESH

D_MODEL = 1024
D_FF = 2816
D_IN = 5120
HEAD_DIM = 64
BLOCK = 128
DIL_GROUPS = ((128, 1), (512, 4), (2048, 16))
B_WINDOW = 128
N_BUCKETS = 32
MAX_DISTANCE = 2048
EPS = 1e-6
N_CHIPS = 4
GW = 256
NEG = -1e30

ADAM_LR, ADAM_B1, ADAM_B2, ADAM_EPS, ADAM_WD, ADAM_STEP = 0.001, 0.9, 0.999, 1e-08, 0.01, 10

VMEM_BIG = 56 * 1024 * 1024
TM = 512
TM_BWD = 256
FF_CHUNKS = 2


def _dot(a, b):
    return jnp.dot(a, b, preferred_element_type=F32)


def _dot_nt(a, b):
    return lax.dot_general(a, b, (((1,), (1,)), ((), ())), preferred_element_type=F32)


def _dot_tn(a, b):
    return lax.dot_general(a, b, (((0,), (0,)), ((), ())), preferred_element_type=F32)


def _sigmoid(x):
    return 1.0 / (1.0 + jnp.exp(-x))


def _params(sem, vmem=None):
    return pltpu.CompilerParams(dimension_semantics=sem, vmem_limit_bytes=vmem)


ANY = pl.BlockSpec(memory_space=pl.ANY)


def _me():
    return lax.axis_index("x"), lax.axis_index("y"), lax.axis_index("c")


_CHIP_RELS = ((1, 0), (0, 1), (1, 1))


def _flip(v, f):
    return 1 - v if f else v


def _gather_weights(pack_a, pack_b):
    def body(a_ref, b_ref, ga_ref, gb_ref, send_sems, recv_sems, loc_sems):
        x, y, c = _me()
        j = 2 * x + y
        la = pltpu.make_async_copy(a_ref, ga_ref.at[j], loc_sems.at[0])
        lb = pltpu.make_async_copy(b_ref, gb_ref.at[j], loc_sems.at[1])
        la.start()
        lb.start()
        copies = []
        for k, (fx, fy) in enumerate(_CHIP_RELS):
            peer = (_flip(x, fx), _flip(y, fy), c)
            for t, (src, dst) in enumerate(((a_ref, ga_ref), (b_ref, gb_ref))):
                cp = pltpu.make_async_remote_copy(
                    src_ref=src, dst_ref=dst.at[j], send_sem=send_sems.at[2 * k + t],
                    recv_sem=recv_sems.at[2 * k + t], device_id=peer, device_id_type=MESH)
                cp.start()
                copies.append(cp)
        for cp in copies:
            cp.wait()
        la.wait()
        lb.wait()

    return pl.pallas_call(
        body, name="gather_weights",
        out_shape=(jax.ShapeDtypeStruct((N_CHIPS,) + pack_a.shape, pack_a.dtype),
                   jax.ShapeDtypeStruct((N_CHIPS,) + pack_b.shape, pack_b.dtype)),
        in_specs=[ANY, ANY], out_specs=(ANY, ANY),
        scratch_shapes=[pltpu.SemaphoreType.DMA((6,)), pltpu.SemaphoreType.DMA((6,)), pltpu.SemaphoreType.DMA((2,))],
    )(pack_a, pack_b)


def _pair_exchange(pa, pb):
    def body(a_ref, b_ref, ma_ref, ra_ref, mb_ref, rb_ref, send_sems, recv_sems, loc_sems):
        x, y, c = _me()
        sib = (x, y, 1 - c)
        la = pltpu.make_async_copy(a_ref.at[c], ma_ref, loc_sems.at[0])
        lb = pltpu.make_async_copy(b_ref.at[c], mb_ref, loc_sems.at[1])
        la.start()
        lb.start()
        ca = pltpu.make_async_remote_copy(src_ref=a_ref.at[1 - c], dst_ref=ra_ref, send_sem=send_sems.at[0],
                                          recv_sem=recv_sems.at[0], device_id=sib, device_id_type=MESH)
        cb = pltpu.make_async_remote_copy(src_ref=b_ref.at[1 - c], dst_ref=rb_ref, send_sem=send_sems.at[1],
                                          recv_sem=recv_sems.at[1], device_id=sib, device_id_type=MESH)
        ca.start()
        cb.start()
        ca.wait()
        cb.wait()
        la.wait()
        lb.wait()

    sa = jax.ShapeDtypeStruct(pa.shape[1:], pa.dtype)
    sb = jax.ShapeDtypeStruct(pb.shape[1:], pb.dtype)
    return pl.pallas_call(
        body, name="grad_pair_exchange", out_shape=(sa, sa, sb, sb),
        in_specs=[ANY, ANY], out_specs=(ANY, ANY, ANY, ANY),
        scratch_shapes=[pltpu.SemaphoreType.DMA((2,)), pltpu.SemaphoreType.DMA((2,)), pltpu.SemaphoreType.DMA((2,))],
    )(pa, pb)


def _chip_exchange(sa, sb):
    def body(a_ref, b_ref, oa_ref, ra_ref, ob_ref, rb_ref, send_sems, recv_sems, loc_sems):
        x, y, c = _me()
        j = 2 * x + y
        la = pltpu.make_async_copy(a_ref.at[j], oa_ref, loc_sems.at[0])
        lb = pltpu.make_async_copy(b_ref.at[j], ob_ref, loc_sems.at[1])
        la.start()
        lb.start()
        copies = []
        for k, (fx, fy) in enumerate(_CHIP_RELS):
            px, py = _flip(x, fx), _flip(y, fy)
            pj = 2 * px + py
            for t, (src, dst) in enumerate(((a_ref, ra_ref), (b_ref, rb_ref))):
                cp = pltpu.make_async_remote_copy(
                    src_ref=src.at[pj], dst_ref=dst.at[k], send_sem=send_sems.at[2 * k + t],
                    recv_sem=recv_sems.at[2 * k + t], device_id=(px, py, c), device_id_type=MESH)
                cp.start()
                copies.append(cp)
        for cp in copies:
            cp.wait()
        la.wait()
        lb.wait()

    def shp(s):
        return (jax.ShapeDtypeStruct(s.shape[1:], s.dtype), jax.ShapeDtypeStruct((3,) + s.shape[1:], s.dtype))

    return pl.pallas_call(
        body, name="grad_chip_exchange", out_shape=shp(sa) + shp(sb),
        in_specs=[ANY, ANY], out_specs=(ANY, ANY, ANY, ANY),
        scratch_shapes=[pltpu.SemaphoreType.DMA((6,)), pltpu.SemaphoreType.DMA((6,)), pltpu.SemaphoreType.DMA((2,))],
    )(sa, sb)


def _half_exchange(fa, fb):
    def body(a_ref, b_ref, oa_ref, ob_ref, send_sems, recv_sems, loc_sems):
        x, y, c = _me()
        sib = (x, y, 1 - c)
        la = pltpu.make_async_copy(a_ref, oa_ref.at[c], loc_sems.at[0])
        lb = pltpu.make_async_copy(b_ref, ob_ref.at[c], loc_sems.at[1])
        la.start()
        lb.start()
        ca = pltpu.make_async_remote_copy(src_ref=a_ref, dst_ref=oa_ref.at[c], send_sem=send_sems.at[0],
                                          recv_sem=recv_sems.at[0], device_id=sib, device_id_type=MESH)
        cb = pltpu.make_async_remote_copy(src_ref=b_ref, dst_ref=ob_ref.at[c], send_sem=send_sems.at[1],
                                          recv_sem=recv_sems.at[1], device_id=sib, device_id_type=MESH)
        ca.start()
        cb.start()
        ca.wait()
        cb.wait()
        la.wait()
        lb.wait()

    return pl.pallas_call(
        body, name="grad_half_exchange",
        out_shape=(jax.ShapeDtypeStruct((2,) + fa.shape, fa.dtype), jax.ShapeDtypeStruct((2,) + fb.shape, fb.dtype)),
        in_specs=[ANY, ANY], out_specs=(ANY, ANY),
        scratch_shapes=[pltpu.SemaphoreType.DMA((2,)), pltpu.SemaphoreType.DMA((2,)), pltpu.SemaphoreType.DMA((2,))],
    )(fa, fb)


def _allreduce_small(vec):
    def body(v_ref, o_ref, buf, send_sems, recv_sems):
        x, y, c = _me()
        me = 4 * x + 2 * y + c
        buf[me] = v_ref[...]
        copies = []
        for k in range(1, 8):
            peer = (_flip(x, (k >> 2) & 1), _flip(y, (k >> 1) & 1), _flip(c, k & 1))
            cp = pltpu.make_async_remote_copy(
                src_ref=v_ref, dst_ref=buf.at[me], send_sem=send_sems.at[k - 1], recv_sem=recv_sems.at[k - 1],
                device_id=peer, device_id_type=MESH)
            cp.start()
            copies.append(cp)
        for cp in copies:
            cp.wait()
        acc = buf[0]
        for i in range(1, 8):
            acc = acc + buf[i]
        o_ref[...] = acc

    vm = pl.BlockSpec(memory_space=pltpu.VMEM)
    return pl.pallas_call(
        body, name="allreduce_small", out_shape=jax.ShapeDtypeStruct(vec.shape, vec.dtype),
        in_specs=[vm], out_specs=vm,
        scratch_shapes=[pltpu.VMEM((8,) + vec.shape, vec.dtype), pltpu.SemaphoreType.DMA((7,)),
                        pltpu.SemaphoreType.DMA((7,))],
    )(vec)


def _sum_call(arrs, out_dtype, tr, name):
    r, c = arrs[0].shape
    n = len(arrs)

    def body(*refs):
        acc = refs[0][...].astype(F32)
        for i in range(1, n):
            acc = acc + refs[i][...].astype(F32)
        refs[n][...] = acc.astype(out_dtype)

    spec = pl.BlockSpec((tr, c), lambda i: (i, 0))
    return pl.pallas_call(
        body, name=name, grid=(r // tr,), out_shape=jax.ShapeDtypeStruct((r, c), out_dtype),
        in_specs=[spec] * n, out_specs=spec, compiler_params=_params(("parallel",)),
    )(*arrs)


def _ffn_fwd(h, gain, wg, wu, wd):
    t = h.shape[0]
    fc = D_FF // FF_CHUNKS

    def body(h_ref, gain_ref, wg_hbm, wu_hbm, wd_hbm, hout_ref, n_ref, g_ref, u_ref, wg_v, wu_v, wd_v):
        @pl.when(pl.program_id(0) == 0)
        def _():
            pltpu.sync_copy(wg_hbm, wg_v)
            pltpu.sync_copy(wu_hbm, wu_v)
            pltpu.sync_copy(wd_hbm, wd_v)

        hh = h_ref[...]
        r = lax.rsqrt(jnp.mean(hh * hh, axis=-1, keepdims=True) + EPS)
        n = (hh * r * gain_ref[...]).astype(BF16)
        n_ref[...] = n
        acc = jnp.zeros((TM, D_MODEL), F32)
        for ci in range(FF_CHUNKS):
            sl = slice(ci * fc, (ci + 1) * fc)
            g = _dot(n, wg_v[:, sl])
            u = _dot(n, wu_v[:, sl])
            g_ref[:, sl] = g.astype(BF16)
            u_ref[:, sl] = u.astype(BF16)
            a = (g * _sigmoid(g) * u).astype(BF16)
            acc = acc + _dot(a, wd_v[sl, :])
        hout_ref[...] = hh + 0.5 * acc

    row = lambda w: pl.BlockSpec((TM, w), lambda i: (i, 0))
    return pl.pallas_call(
        body, name="ffn_fwd", grid=(t // TM,),
        out_shape=(jax.ShapeDtypeStruct((t, D_MODEL), F32), jax.ShapeDtypeStruct((t, D_MODEL), BF16),
                   jax.ShapeDtypeStruct((t, D_FF), BF16), jax.ShapeDtypeStruct((t, D_FF), BF16)),
        in_specs=[row(D_MODEL), pl.BlockSpec((1, D_MODEL), lambda i: (0, 0)), ANY, ANY, ANY],
        out_specs=(row(D_MODEL), row(D_MODEL), row(D_FF), row(D_FF)),
        scratch_shapes=[pltpu.VMEM((D_MODEL, D_FF), BF16), pltpu.VMEM((D_MODEL, D_FF), BF16),
                        pltpu.VMEM((D_FF, D_MODEL), BF16)],
        compiler_params=_params(("arbitrary",), VMEM_BIG),
    )(h, gain, wg, wu, wd)


def _ffn_bwd(dhout, h, gain, g, u, wg, wu, wd):
    t = h.shape[0]
    tm = TM_BWD
    fc = D_FF // FF_CHUNKS

    def body(dho_ref, h_ref, gain_ref, g_ref, u_ref, wg_hbm, wu_hbm, wd_hbm,
             dh_ref, dg_ref, du_ref, a_ref, df_ref, gg_ref, wg_v, wu_v, wd_v):
        @pl.when(pl.program_id(0) == 0)
        def _():
            pltpu.sync_copy(wg_hbm, wg_v)
            pltpu.sync_copy(wu_hbm, wu_v)
            pltpu.sync_copy(wd_hbm, wd_v)
            gg_ref[...] = jnp.zeros_like(gg_ref)

        dho = dho_ref[...]
        df = (0.5 * dho).astype(BF16)
        df_ref[...] = df
        dn = jnp.zeros((tm, D_MODEL), F32)
        for ci in range(FF_CHUNKS):
            sl = slice(ci * fc, (ci + 1) * fc)
            da = _dot_nt(df, wd_v[sl, :])
            gv = g_ref[:, sl].astype(F32)
            uv = u_ref[:, sl].astype(F32)
            sg = _sigmoid(gv)
            silu = gv * sg
            dg = (da * uv * (sg * (1.0 + gv * (1.0 - sg)))).astype(BF16)
            du = (da * silu).astype(BF16)
            dg_ref[:, sl] = dg
            du_ref[:, sl] = du
            a_ref[:, sl] = (silu * uv).astype(BF16)
            dn = dn + _dot_nt(dg, wg_v[:, sl]) + _dot_nt(du, wu_v[:, sl])
        hh = h_ref[...]
        r = lax.rsqrt(jnp.mean(hh * hh, axis=-1, keepdims=True) + EPS)
        hn = hh * r
        gg_ref[...] += jnp.sum(dn * hn, axis=0, keepdims=True)
        dng = dn * gain_ref[...]
        dh_ref[...] = dho + r * (dng - hn * jnp.mean(dng * hn, axis=-1, keepdims=True))

    row = lambda w: pl.BlockSpec((tm, w), lambda i: (i, 0))
    vec = pl.BlockSpec((1, D_MODEL), lambda i: (0, 0))
    return pl.pallas_call(
        body, name="ffn_bwd", grid=(t // tm,),
        out_shape=(jax.ShapeDtypeStruct((t, D_MODEL), F32), jax.ShapeDtypeStruct((t, D_FF), BF16),
                   jax.ShapeDtypeStruct((t, D_FF), BF16), jax.ShapeDtypeStruct((t, D_FF), BF16),
                   jax.ShapeDtypeStruct((t, D_MODEL), BF16), jax.ShapeDtypeStruct((1, D_MODEL), F32)),
        in_specs=[row(D_MODEL), row(D_MODEL), vec, row(D_FF), row(D_FF), ANY, ANY, ANY],
        out_specs=(row(D_MODEL), row(D_FF), row(D_FF), row(D_FF), row(D_MODEL), vec),
        scratch_shapes=[pltpu.VMEM((D_MODEL, D_FF), BF16), pltpu.VMEM((D_MODEL, D_FF), BF16),
                        pltpu.VMEM((D_FF, D_MODEL), BF16)],
        compiler_params=_params(("arbitrary",), VMEM_BIG),
    )(dhout, h, gain, g, u, wg, wu, wd)


def _wgrad(lhs, rhs, tn, with_colsum=False, name="wgrad"):
    t, k = lhs.shape
    n = rhs.shape[1]
    tk = 512
    nt = t // tk

    def body(l_ref, r_ref, o_ref, *rest):
        ti = pl.program_id(1)

        @pl.when(ti == 0)
        def _():
            o_ref[...] = jnp.zeros_like(o_ref)
            if with_colsum:
                rest[0][...] = jnp.zeros_like(rest[0])

        o_ref[...] += _dot_tn(l_ref[...], r_ref[...])
        if with_colsum:
            rest[0][...] += jnp.sum(r_ref[...].astype(F32), axis=0, keepdims=True)

    out_shape = [jax.ShapeDtypeStruct((k, n), F32)]
    out_specs = [pl.BlockSpec((k, tn), lambda j, i: (0, j))]
    if with_colsum:
        out_shape.append(jax.ShapeDtypeStruct((1, n), F32))
        out_specs.append(pl.BlockSpec((1, tn), lambda j, i: (0, j)))
    res = pl.pallas_call(
        body, name=name, grid=(n // tn, nt), out_shape=tuple(out_shape),
        in_specs=[pl.BlockSpec((tk, k), lambda j, i: (i, 0)), pl.BlockSpec((tk, tn), lambda j, i: (i, j))],
        out_specs=tuple(out_specs),
        compiler_params=_params(("arbitrary", "arbitrary"), VMEM_BIG),
    )(lhs, rhs)
    return res if with_colsum else res[0]


def _inproj_fwd(h, gain, w_in, b_in):
    t = h.shape[0]
    nc = 5
    cw = D_IN // nc

    def body(h_ref, gain_ref, w_hbm, b_ref, u_ref, z_ref, w_v):
        @pl.when(pl.program_id(0) == 0)
        def _():
            pltpu.sync_copy(w_hbm, w_v)

        hh = h_ref[...]
        r = lax.rsqrt(jnp.mean(hh * hh, axis=-1, keepdims=True) + EPS)
        un = (hh * r * gain_ref[...]).astype(BF16)
        u_ref[...] = un
        for ci in range(nc):
            sl = slice(ci * cw, (ci + 1) * cw)
            z_ref[:, sl] = (_dot(un, w_v[:, sl]) + b_ref[:, sl]).astype(BF16)

    row = lambda w: pl.BlockSpec((TM, w), lambda i: (i, 0))
    return pl.pallas_call(
        body, name="inproj_fwd", grid=(t // TM,),
        out_shape=(jax.ShapeDtypeStruct((t, D_MODEL), BF16), jax.ShapeDtypeStruct((t, D_IN), BF16)),
        in_specs=[row(D_MODEL), pl.BlockSpec((1, D_MODEL), lambda i: (0, 0)), ANY,
                  pl.BlockSpec((1, D_IN), lambda i: (0, 0))],
        out_specs=(row(D_MODEL), row(D_IN)),
        scratch_shapes=[pltpu.VMEM((D_MODEL, D_IN), BF16)],
        compiler_params=_params(("arbitrary",), VMEM_BIG),
    )(h, gain, w_in, b_in)


def _inproj_bwd(dz, dh2, h, gain, w_in):
    t = h.shape[0]
    nc = 5
    cw = D_IN // nc

    def body(dz_ref, dh2_ref, h_ref, gain_ref, w_hbm, dh_ref, gg_ref, w_v):
        @pl.when(pl.program_id(0) == 0)
        def _():
            pltpu.sync_copy(w_hbm, w_v)
            gg_ref[...] = jnp.zeros_like(gg_ref)

        du = jnp.zeros((TM, D_MODEL), F32)
        for ci in range(nc):
            sl = slice(ci * cw, (ci + 1) * cw)
            du = du + _dot_nt(dz_ref[:, sl], w_v[:, sl])
        hh = h_ref[...]
        r = lax.rsqrt(jnp.mean(hh * hh, axis=-1, keepdims=True) + EPS)
        hn = hh * r
        gg_ref[...] += jnp.sum(du * hn, axis=0, keepdims=True)
        dng = du * gain_ref[...]
        dh_ref[...] = dh2_ref[...] + r * (dng - hn * jnp.mean(dng * hn, axis=-1, keepdims=True))

    row = lambda w: pl.BlockSpec((TM, w), lambda i: (i, 0))
    vec = pl.BlockSpec((1, D_MODEL), lambda i: (0, 0))
    return pl.pallas_call(
        body, name="inproj_bwd", grid=(t // TM,),
        out_shape=(jax.ShapeDtypeStruct((t, D_MODEL), F32), jax.ShapeDtypeStruct((1, D_MODEL), F32)),
        in_specs=[row(D_IN), row(D_MODEL), row(D_MODEL), vec, ANY],
        out_specs=(row(D_MODEL), vec),
        scratch_shapes=[pltpu.VMEM((D_MODEL, D_IN), BF16)],
        compiler_params=_params(("arbitrary",), VMEM_BIG),
    )(dz, dh2, h, gain, w_in)


def _head_sum_matrix(w):
    i = lax.broadcasted_iota(jnp.int32, (w, w), 0) // HEAD_DIM
    j = lax.broadcasted_iota(jnp.int32, (w, w), 1) // HEAD_DIM
    return (i == j).astype(F32)


def _merge_fwd(o0, o1, o2, l0, l1, l2, yb, z, h1, wa, wb, wout):
    t = h1.shape[0]

    def body(o0_ref, o1_ref, o2_ref, l0_ref, l1_ref, l2_ref, yb_ref, ga_ref, gb_ref, h1_ref, wa_ref, wb_ref, wo_ref,
             h2_ref, y_ref, lt_ref, pa_ref, pb_ref, mg_ref):
        la, lb, lc = l0_ref[...], l1_ref[...], l2_ref[...]
        mx = jnp.maximum(jnp.maximum(la, lb), lc)
        ea, eb, ec = jnp.exp(la - mx), jnp.exp(lb - mx), jnp.exp(lc - mx)
        den = ea + eb + ec
        y = (ea * o0_ref[...] + eb * o1_ref[...] + ec * o2_ref[...]) / den
        lt_ref[...] = mx + jnp.log(den)
        yb16 = y.astype(BF16)
        y_ref[...] = yb16
        pa = _dot(yb16, wa_ref[...])
        pb = _dot(yb_ref[...], wb_ref[...])
        pa_ref[...] = pa.astype(BF16)
        pb_ref[...] = pb.astype(BF16)
        mg = (_sigmoid(ga_ref[...].astype(F32)) * pa + _sigmoid(gb_ref[...].astype(F32)) * pb).astype(BF16)
        mg_ref[...] = mg
        h2_ref[...] = h1_ref[...] + _dot(mg, wo_ref[...])

    row = lambda w: pl.BlockSpec((TM, w), lambda i: (i, 0))
    full = lambda a: pl.BlockSpec(a.shape, lambda i: (0, 0))
    gate = lambda cb: pl.BlockSpec((TM, D_MODEL), lambda i: (i, cb))
    return pl.pallas_call(
        body, name="merge_fwd", grid=(t // TM,),
        out_shape=(jax.ShapeDtypeStruct((t, D_MODEL), F32), jax.ShapeDtypeStruct((t, GW), BF16),
                   jax.ShapeDtypeStruct((t, GW), F32), jax.ShapeDtypeStruct((t, D_MODEL), BF16),
                   jax.ShapeDtypeStruct((t, D_MODEL), BF16), jax.ShapeDtypeStruct((t, D_MODEL), BF16)),
        in_specs=[row(GW)] * 6 + [row(2 * GW), gate(3), gate(4), row(D_MODEL), full(wa), full(wb), full(wout)],
        out_specs=(row(D_MODEL), row(GW), row(GW), row(D_MODEL), row(D_MODEL), row(D_MODEL)),
        compiler_params=_params(("parallel",), VMEM_BIG),
    )(o0, o1, o2, l0, l1, l2, yb, z, z, h1, wa, wb, wout)


def _merge_bwd(dh2, pa, pb, z, y, yb, wa, wb, wout):
    t = dh2.shape[0]

    def body(dh2_ref, pa_ref, pb_ref, ga_ref, gb_ref, y_ref, yb_ref, wa_ref, wb_ref, wo_ref,
             dpa_ref, dpb_ref, dga_ref, dgb_ref, dya_ref, dyb_ref, dh2b_ref, ca_ref, cb_ref):
        d16 = dh2_ref[...].astype(BF16)
        dh2b_ref[...] = d16
        dm = _dot_nt(d16, wo_ref[...])
        sa = _sigmoid(ga_ref[...].astype(F32))
        sb = _sigmoid(gb_ref[...].astype(F32))
        dpa = (dm * sa).astype(BF16)
        dpb = (dm * sb).astype(BF16)
        dpa_ref[...] = dpa
        dpb_ref[...] = dpb
        dga_ref[...] = (dm * pa_ref[...].astype(F32) * sa * (1.0 - sa)).astype(BF16)
        dgb_ref[...] = (dm * pb_ref[...].astype(F32) * sb * (1.0 - sb)).astype(BF16)
        dya = _dot_nt(dpa, wa_ref[...])
        dyb = _dot_nt(dpb, wb_ref[...])
        dya_ref[...] = dya.astype(BF16)
        dyb_ref[...] = dyb.astype(BF16)
        hp = lax.Precision.HIGHEST
        ca_ref[...] = jnp.dot(dya * y_ref[...].astype(F32), _head_sum_matrix(GW), precision=hp,
                              preferred_element_type=F32)
        cb_ref[...] = jnp.dot(dyb * yb_ref[...].astype(F32), _head_sum_matrix(2 * GW), precision=hp,
                              preferred_element_type=F32)

    row = lambda w: pl.BlockSpec((TM, w), lambda i: (i, 0))
    full = lambda a: pl.BlockSpec(a.shape, lambda i: (0, 0))
    gate = lambda cb: pl.BlockSpec((TM, D_MODEL), lambda i: (i, cb))
    bf = lambda w: jax.ShapeDtypeStruct((t, w), BF16)
    return pl.pallas_call(
        body, name="merge_bwd", grid=(t // TM,),
        out_shape=(bf(D_MODEL), bf(D_MODEL), bf(D_MODEL), bf(D_MODEL), bf(GW), bf(2 * GW), bf(D_MODEL),
                   jax.ShapeDtypeStruct((t, GW), F32), jax.ShapeDtypeStruct((t, 2 * GW), F32)),
        in_specs=[row(D_MODEL), row(D_MODEL), row(D_MODEL), gate(3), gate(4), row(GW), row(2 * GW),
                  full(wa), full(wb), full(wout)],
        out_specs=(row(D_MODEL), row(D_MODEL), row(D_MODEL), row(D_MODEL), row(GW), row(2 * GW), row(D_MODEL),
                   row(GW), row(2 * GW)),
        compiler_params=_params(("parallel",), VMEM_BIG),
    )(dh2, pa, pb, z, z, y, yb, wa, wb, wout)


def _loss_head(h3, gain, tgt):
    t = h3.shape[0]

    def body(h_ref, gain_ref, t_ref, dh_ref, loss_ref, gg_ref):
        @pl.when(pl.program_id(0) == 0)
        def _():
            loss_ref[...] = jnp.zeros_like(loss_ref)
            gg_ref[...] = jnp.zeros_like(gg_ref)

        hh = h_ref[...]
        r = lax.rsqrt(jnp.mean(hh * hh, axis=-1, keepdims=True) + EPS)
        hn = hh * r
        err = hn * gain_ref[...] - t_ref[...]
        part = jnp.sum(jnp.sum(err * err, axis=1, keepdims=True), axis=0, keepdims=True)
        loss_ref[...] += (0.5 / D_MODEL) * part
        dy = err * (1.0 / D_MODEL)
        gg_ref[...] += jnp.sum(dy * hn, axis=0, keepdims=True)
        dng = dy * gain_ref[...]
        dh_ref[...] = r * (dng - hn * jnp.mean(dng * hn, axis=-1, keepdims=True))

    row = pl.BlockSpec((TM, D_MODEL), lambda i: (i, 0))
    vec = pl.BlockSpec((1, D_MODEL), lambda i: (0, 0))
    return pl.pallas_call(
        body, name="loss_head", grid=(t // TM,),
        out_shape=(jax.ShapeDtypeStruct((t, D_MODEL), F32), jax.ShapeDtypeStruct((8, 128), F32),
                   jax.ShapeDtypeStruct((1, D_MODEL), F32)),
        in_specs=[row, vec, row], out_specs=(row, pl.BlockSpec((8, 128), lambda i: (0, 0)), vec),
        compiler_params=_params(("arbitrary",)),
    )(h3, gain, tgt)


def _lane_head(rows):
    return lax.broadcasted_iota(jnp.int32, (rows, GW), 1) // HEAD_DIM


def _kv_expand_matrix(r):
    ci = lax.broadcasted_iota(jnp.int32, (2 * HEAD_DIM, GW), 0)
    ji = lax.broadcasted_iota(jnp.int32, (2 * HEAD_DIM, GW), 1)
    return (ci == (ji % HEAD_DIM) + HEAD_DIM * r).astype(BF16)


def _attn_fwd(q_arr, k_arr, v_arr, bias, sink, *, grid, seq, kvw, q_map, k_map, v_map, bias_map, sink_map,
              has_sink, o_shape, o_dtype, name):
    nb = seq // BLOCK

    def body(q_ref, k_ref, v_ref, bias_ref, sink_ref, o_ref, lse_ref):
        lane_head = _lane_head(BLOCK)
        expand = _kv_expand_matrix(pl.program_id(1)) if kvw != GW else None

        def block(i, first):
            r0 = pl.multiple_of(i * BLOCK, BLOCK)
            qi = q_ref[0, pl.ds(r0, BLOCK), :]
            if first:
                kc = k_ref[0, pl.ds(0, BLOCK), :]
                vc = v_ref[0, pl.ds(0, BLOCK), :]
            else:
                k0 = pl.multiple_of(r0 - BLOCK, BLOCK)
                kc = k_ref[0, pl.ds(k0, 2 * BLOCK), :]
                vc = v_ref[0, pl.ds(k0, 2 * BLOCK), :]
            if expand is not None:
                kc = _dot(kc, expand).astype(BF16)
                vc = _dot(vc, expand).astype(BF16)
            acc_o = jnp.zeros((BLOCK, GW), F32)
            acc_l = jnp.zeros((BLOCK, GW), F32)
            for h in range(4):
                qh = jnp.where(lane_head == h, qi, jnp.zeros_like(qi))
                bias_h = bias_ref[h, :, BLOCK:] if first else bias_ref[h]
                s = _dot_nt(qh, kc) * (HEAD_DIM ** -0.5) + bias_h
                m = jnp.max(s, axis=-1, keepdims=True)
                if has_sink:
                    sk = sink_ref[0, h:h + 1, 0:1]
                    m = jnp.maximum(m, sk)
                p = jnp.exp(s - m)
                l = jnp.sum(p, axis=-1, keepdims=True)
                if has_sink:
                    l = l + jnp.exp(sk - m)
                oh = _dot(p.astype(BF16), vc) / l
                acc_o = jnp.where(lane_head == h, oh, acc_o)
                acc_l = jnp.where(lane_head == h, m + jnp.log(l), acc_l)
            o_ref[0, pl.ds(r0, BLOCK), :] = acc_o.astype(o_dtype)
            lse_ref[0, pl.ds(r0, BLOCK), :] = acc_l

        block(0, True)
        if nb > 1:
            def step(i, carry):
                block(i, False)
                return carry
            lax.fori_loop(1, nb, step, 0)

    blk = lambda w, m: pl.BlockSpec((1, seq, w), m)
    o_map = lambda n, r: (n, 0, r)
    return pl.pallas_call(
        body, name=name, grid=grid,
        out_shape=(jax.ShapeDtypeStruct(o_shape, o_dtype), jax.ShapeDtypeStruct(o_shape, F32)),
        in_specs=[blk(GW, q_map), blk(kvw, k_map), blk(kvw, v_map),
                  pl.BlockSpec((4, BLOCK, 2 * BLOCK), bias_map), pl.BlockSpec((1, 4, 128), sink_map)],
        out_specs=(blk(GW, o_map), blk(GW, o_map)),
        compiler_params=_params(("parallel", "arbitrary"), VMEM_BIG),
    )(q_arr, k_arr, v_arr, bias, sink)


def _attn_bwd(q_arr, k_arr, v_arr, bias, sink, dy, cc, lse, *, grid, seq, kvw, q_map, k_map, v_map, bias_map,
              sink_map, has_sink, n_bias, dq_shape, dkv_shape, name):
    nb = seq // BLOCK
    scale = HEAD_DIM ** -0.5

    def body(q_ref, k_ref, v_ref, bias_ref, sink_ref, dy_ref, c_ref, lse_ref,
             dq_ref, dk_ref, dv_ref, db_ref, dsk_ref, dk_acc, dv_acc, dk_half, dv_half):
        rr = pl.program_id(1)

        @pl.when((pl.program_id(0) == 0) & (rr == 0))
        def _():
            db_ref[...] = jnp.zeros_like(db_ref)
            dsk_ref[...] = jnp.zeros_like(dsk_ref)

        dk_acc[...] = jnp.zeros_like(dk_acc)
        dv_acc[...] = jnp.zeros_like(dv_acc)
        lane_head = _lane_head(BLOCK)
        expand = _kv_expand_matrix(rr) if kvw != GW else None
        hb = 4 * rr if n_bias == 8 else 0

        def block(i, first):
            r0 = pl.multiple_of(i * BLOCK, BLOCK)
            rows = pl.ds(r0, BLOCK)
            qi = q_ref[0, rows, :]
            dyi = dy_ref[0, rows, :]
            ci = c_ref[0, rows, :]
            li = lse_ref[0, rows, :]
            if first:
                krows = pl.ds(0, BLOCK)
            else:
                krows = pl.ds(pl.multiple_of(r0 - BLOCK, BLOCK), 2 * BLOCK)
            kc = k_ref[0, krows, :]
            vc = v_ref[0, krows, :]
            if expand is not None:
                kc = _dot(kc, expand).astype(BF16)
                vc = _dot(vc, expand).astype(BF16)
            nk = BLOCK if first else 2 * BLOCK
            dq = jnp.zeros((BLOCK, GW), F32)
            dkc = jnp.zeros((nk, GW), F32)
            dvc = jnp.zeros((nk, GW), F32)
            for h in range(4):
                sel = lane_head == h
                qh = jnp.where(sel, qi, jnp.zeros_like(qi))
                dyh = jnp.where(sel, dyi, jnp.zeros_like(dyi))
                bias_h = bias_ref[h, :, BLOCK:] if first else bias_ref[h]
                s = _dot_nt(qh, kc) * scale + bias_h
                lh = li[:, h * HEAD_DIM:h * HEAD_DIM + 1]
                ch = ci[:, h * HEAD_DIM:h * HEAD_DIM + 1]
                p = jnp.exp(s - lh)
                p16 = p.astype(BF16)
                dvc = dvc + _dot_tn(p16, dyh)
                dp = _dot_nt(dyh, vc)
                ds = p * (dp - ch)
                if first:
                    db_ref[hb + h, :, BLOCK:] += ds
                else:
                    db_ref[hb + h] += ds
                ds16 = ds.astype(BF16)
                dq = dq + jnp.where(sel, _dot(ds16, kc), 0.0)
                dkc = dkc + _dot_tn(ds16, qh)
                if has_sink:
                    sk = sink_ref[0, h:h + 1, 0:1]
                    val = -jnp.sum(jnp.exp(sk - lh) * ch, axis=0, keepdims=True)
                    dsk_ref[hb + h] += jnp.broadcast_to(val, (8, 128))
            dq_ref[0, rows, :] = (dq * scale).astype(BF16)
            dk_acc[krows, :] += dkc * scale
            dv_acc[krows, :] += dvc

        block(0, True)
        if nb > 1:
            def step(i, carry):
                block(i, False)
                return carry
            lax.fori_loop(1, nb, step, 0)

        if kvw == GW:
            dk_ref[0] = dk_acc[...].astype(BF16)
            dv_ref[0] = dv_acc[...].astype(BF16)
        else:
            def fold(acc):
                t2 = acc[:, :2 * HEAD_DIM] + acc[:, 2 * HEAD_DIM:]
                t2 = t2 + pltpu.roll(t2, HEAD_DIM, 1)
                lane = lax.broadcasted_iota(jnp.int32, t2.shape, 1) // HEAD_DIM
                return jnp.where(lane == rr, t2, 0.0)

            @pl.when(rr == 0)
            def _():
                dk_half[...] = fold(dk_acc[...])
                dv_half[...] = fold(dv_acc[...])

            @pl.when(rr == 1)
            def _():
                dk_ref[0] = (dk_half[...] + fold(dk_acc[...])).astype(BF16)
                dv_ref[0] = (dv_half[...] + fold(dv_acc[...])).astype(BF16)

    blk = lambda w, m: pl.BlockSpec((1, seq, w), m)
    o_map = lambda n, r: (n, 0, r)
    kv_out_map = o_map if kvw == GW else (lambda n, r: (n, 0, 0))
    return pl.pallas_call(
        body, name=name, grid=grid,
        out_shape=(jax.ShapeDtypeStruct(dq_shape, BF16), jax.ShapeDtypeStruct(dkv_shape, BF16),
                   jax.ShapeDtypeStruct(dkv_shape, BF16), jax.ShapeDtypeStruct((n_bias, BLOCK, 2 * BLOCK), F32),
                   jax.ShapeDtypeStruct((8, 8, 128), F32)),
        in_specs=[blk(GW, q_map), blk(kvw, k_map), blk(kvw, v_map),
                  pl.BlockSpec((4, BLOCK, 2 * BLOCK), bias_map), pl.BlockSpec((1, 4, 128), sink_map),
                  blk(GW, o_map), blk(GW, o_map), blk(GW, o_map)],
        out_specs=(blk(GW, o_map), blk(kvw, kv_out_map), blk(kvw, kv_out_map),
                   pl.BlockSpec((n_bias, BLOCK, 2 * BLOCK), lambda n, r: (0, 0, 0)),
                   pl.BlockSpec((8, 8, 128), lambda n, r: (0, 0, 0))),
        scratch_shapes=[pltpu.VMEM((seq, GW), F32), pltpu.VMEM((seq, GW), F32),
                        pltpu.VMEM((seq, 2 * HEAD_DIM), F32), pltpu.VMEM((seq, 2 * HEAD_DIM), F32)],
        compiler_params=_params(("arbitrary", "arbitrary"), VMEM_BIG),
    )(q_arr, k_arr, v_arr, bias, sink, dy, cc, lse)


def _bias_grad(ds_all, buckets):
    def body(ds_ref, bk_ref, o_ref):
        rows = lax.broadcasted_iota(jnp.int32, (N_BUCKETS, 128), 0)
        cols = lax.broadcasted_iota(jnp.int32, (N_BUCKETS, 128), 1)

        def per_bucket(b, acc):
            for h in range(20):
                gi = h // 4 if h < 12 else 3
                v = jnp.where(bk_ref[gi] == b, ds_ref[h], 0.0)
                v = jnp.sum(jnp.sum(v, axis=1, keepdims=True), axis=0, keepdims=True)
                acc = jnp.where((rows == b) & (cols == h), v, acc)
            return acc

        o_ref[...] = lax.fori_loop(0, N_BUCKETS, per_bucket, jnp.zeros((N_BUCKETS, 128), F32))

    vm = pl.BlockSpec(memory_space=pltpu.VMEM)
    return pl.pallas_call(body, name="bias_grad", out_shape=jax.ShapeDtypeStruct((N_BUCKETS, 128), F32),
                          in_specs=[vm, vm], out_specs=vm)(ds_all, buckets)


def _adamw(w, g, m, v, name):
    r, c = w.shape
    tr = r
    for cand in (256, 176, 128, 64, 32, 16, 8):
        if r % cand == 0:
            tr = cand
            break
    bc1 = 1.0 - ADAM_B1 ** ADAM_STEP
    bc2 = 1.0 - ADAM_B2 ** ADAM_STEP

    def body(w_ref, g_ref, m_ref, v_ref, d_ref, nm_ref, nv_ref):
        gv = g_ref[...]
        nm = ADAM_B1 * m_ref[...] + (1.0 - ADAM_B1) * gv
        nv = ADAM_B2 * v_ref[...] + (1.0 - ADAM_B2) * (gv * gv)
        nm_ref[...] = nm
        nv_ref[...] = nv
        d_ref[...] = -ADAM_LR * ((nm / bc1) / (jnp.sqrt(nv / bc2) + ADAM_EPS) + ADAM_WD * w_ref[...])

    spec = pl.BlockSpec((tr, c), lambda i: (i, 0))
    shp = jax.ShapeDtypeStruct((r, c), F32)
    return pl.pallas_call(body, name=name, grid=(r // tr,), out_shape=(shp, shp, shp),
                          in_specs=[spec] * 4, out_specs=(spec, spec, spec),
                          compiler_params=_params(("parallel",)))(w, g, m, v)


def _t5_bucket(dist):
    max_exact = N_BUCKETS // 2
    n = jnp.maximum(dist, 0)
    nf = jnp.maximum(n, 1).astype(F32)
    large = max_exact + (jnp.log(nf / max_exact) / math.log(MAX_DISTANCE / max_exact)
                         * (N_BUCKETS - max_exact)).astype(jnp.int32)
    large = jnp.minimum(large, N_BUCKETS - 1)
    return jnp.where(n < max_exact, n, large)


def _bias_tables(rel_bias):
    qi = jnp.arange(BLOCK)[:, None]
    ki = jnp.arange(2 * BLOCK)[None, :]
    dist = qi + BLOCK - ki
    specs = [(d, w // d, 4 * gi, 4 * gi + 4) for gi, (w, d) in enumerate(DIL_GROUPS)] + [(1, B_WINDOW - 1, 12, 20)]
    biases, buckets = [], []
    for stride, steps, h0, h1 in specs:
        valid = (dist >= 0) & (dist <= steps)
        bk = _t5_bucket(dist * stride)
        b = jnp.transpose(rel_bias[bk][..., h0:h1], (2, 0, 1))
        biases.append(jnp.where(valid[None], b, NEG))
        buckets.append(jnp.where(valid, bk, -1).astype(jnp.int32))
    return jnp.concatenate(biases, axis=0), jnp.stack(buckets, axis=0)


def _cols_to_chunks(w):
    r, c4 = w.shape
    return jnp.transpose(w.reshape(r, N_CHIPS, c4 // N_CHIPS), (1, 0, 2))


def _chunks_to_cols(w):
    n, r, c = w.shape
    return jnp.transpose(w, (1, 0, 2)).reshape(r, n * c)


_A_WIDTHS = (704, 704, 704, 704, 1280)
_B_ROWS = (704, 704, 256, 64, 128)


def _local_step(x, tgt, W, S):
    nseq, seq, _ = x.shape
    t = nseq * seq
    xf = x.reshape(t, D_MODEL)
    bias_all, buckets = _bias_tables(S["rel_bias"])
    sink_b = jnp.broadcast_to(S["sinks"].reshape(2, 4, 1), (2, 4, 128)).astype(F32)
    sink_0 = jnp.zeros((1, 4, 128), F32)

    h1, n1, g1, u1 = _ffn_fwd(xf, S["ffn1_norm"], W["wg1"], W["wu1"], W["wd1"])
    un, z = _inproj_fwd(h1, S["mix_norm"], W["w_in"], S["b_in"])

    a_cfg = []
    outs, lses = [], []
    for gi, (_, d) in enumerate(DIL_GROUPS):
        ln = seq // d
        cpr = D_IN // GW
        cfg = dict(grid=(nseq, d), seq=ln, kvw=GW,
                   q_map=lambda n, r, gi=gi, cpr=cpr: (n, 0, r * cpr + gi),
                   k_map=lambda n, r, gi=gi, cpr=cpr: (n, 0, r * cpr + 3 + gi),
                   v_map=lambda n, r, gi=gi, cpr=cpr: (n, 0, r * cpr + 6 + gi),
                   bias_map=lambda n, r: (0, 0, 0), sink_map=lambda n, r: (0, 0, 0), has_sink=False)
        a_cfg.append(cfg)
        zr = z.reshape(nseq, ln, d * D_IN)
        o, lse = _attn_fwd(zr, zr, zr, bias_all[4 * gi:4 * gi + 4], sink_0, o_shape=(nseq, ln, d * GW),
                           o_dtype=F32, name=f"attn_a{gi}_fwd", **cfg)
        outs.append(o.reshape(t, GW))
        lses.append(lse.reshape(t, GW))
    z3 = z.reshape(nseq, seq, D_IN)
    b_cfg = dict(grid=(nseq, 2), seq=seq, kvw=2 * HEAD_DIM,
                 q_map=lambda n, r: (n, 0, 9 + r), k_map=lambda n, r: (n, 0, 22), v_map=lambda n, r: (n, 0, 23),
                 bias_map=lambda n, r: (r, 0, 0), sink_map=lambda n, r: (r, 0, 0), has_sink=True)
    yb, lse_b = _attn_fwd(z3, z3, z3, bias_all[12:20], sink_b, o_shape=(nseq, seq, 2 * GW), o_dtype=BF16,
                          name="attn_b_fwd", **b_cfg)
    yb = yb.reshape(t, 2 * GW)

    h2, y, lse_tot, pa, pb, merged = _merge_fwd(outs[0], outs[1], outs[2], lses[0], lses[1], lses[2], yb, z, h1,
                                                W["wa"], W["wb"], W["wout"])
    h3, n2, g2, u2 = _ffn_fwd(h2, S["ffn2_norm"], W["wg2"], W["wu2"], W["wd2"])
    dh3, loss_part, g_final = _loss_head(h3, S["final_norm"].reshape(1, D_MODEL), tgt.reshape(t, D_MODEL))

    G, GS = {}, {}
    GS["final_norm"] = g_final
    dh2, dg2, du2, a2, df2, GS["ffn2_norm"] = _ffn_bwd(dh3, h2, S["ffn2_norm"], g2, u2, W["wg2"], W["wu2"], W["wd2"])
    G["wg2"] = _wgrad(n2, dg2, D_FF, name="wgrad_gate2")
    G["wu2"] = _wgrad(n2, du2, D_FF, name="wgrad_up2")
    G["wd2"] = _wgrad(a2, df2, D_MODEL, name="wgrad_down2")

    dpa, dpb, dga, dgb, dya, dyb, dh2b, ca, cb = _merge_bwd(dh2, pa, pb, z, y, yb, W["wa"], W["wb"], W["wout"])
    G["wout"] = _wgrad(merged, dh2b, D_MODEL, name="wgrad_out")
    G["wa"] = _wgrad(y, dpa, D_MODEL, name="wgrad_branch_a")
    G["wb"] = _wgrad(yb, dpb, D_MODEL, name="wgrad_branch_b")

    dqs, dks, dvs, dbs = [], [], [], []
    for gi, (_, d) in enumerate(DIL_GROUPS):
        ln = seq // d
        zr = z.reshape(nseq, ln, d * D_IN)
        fold = lambda a, ln=ln, d=d: a.reshape(nseq, ln, d * GW)
        shp = (nseq, ln, d * GW)
        dq, dk, dv, db, _ = _attn_bwd(zr, zr, zr, bias_all[4 * gi:4 * gi + 4], sink_0, fold(dya), fold(ca),
                                      fold(lse_tot), n_bias=4, dq_shape=shp, dkv_shape=shp,
                                      name=f"attn_a{gi}_bwd", **a_cfg[gi])
        dqs.append(dq.reshape(t, GW))
        dks.append(dk.reshape(t, GW))
        dvs.append(dv.reshape(t, GW))
        dbs.append(db)
    r3 = lambda a: a.reshape(nseq, seq, a.shape[-1])
    dqb, dkb, dvb, dbb, dsink = _attn_bwd(z3, z3, z3, bias_all[12:20], sink_b, r3(dyb), r3(cb), lse_b, n_bias=8,
                                          dq_shape=(nseq, seq, 2 * GW), dkv_shape=(nseq, seq, 2 * HEAD_DIM),
                                          name="attn_b_bwd", **b_cfg)
    dz = jnp.concatenate(dqs + dks + dvs + [dqb.reshape(t, 2 * GW), dkb.reshape(t, 2 * HEAD_DIM),
                                            dvb.reshape(t, 2 * HEAD_DIM), dga, dgb], axis=-1)
    gb_tab = _bias_grad(jnp.concatenate(dbs + [dbb], axis=0), buckets)
    GS["rel_bias"] = gb_tab[:, :20]
    GS["sinks"] = dsink[:, 0, 0].reshape(1, 8)

    G["w_in"], GS["b_in"] = _wgrad(un, dz, D_IN // 2, with_colsum=True, name="wgrad_in")
    dh1, GS["mix_norm"] = _inproj_bwd(dz, dh2, h1, S["mix_norm"], W["w_in"])

    dx, dg1, du1, a1, df1, GS["ffn1_norm"] = _ffn_bwd(dh1, xf, S["ffn1_norm"], g1, u1, W["wg1"], W["wu1"], W["wd1"])
    G["wg1"] = _wgrad(n1, dg1, D_FF, name="wgrad_gate1")
    G["wu1"] = _wgrad(n1, du1, D_FF, name="wgrad_up1")
    G["wd1"] = _wgrad(a1, df1, D_MODEL, name="wgrad_down1")
    return loss_part, dx.reshape(x.shape), G, GS


_BIG = ("ffn1_w_gate", "ffn1_w_up", "ffn1_w_down", "w_in", "w_branch_a", "w_branch_b", "w_out",
        "ffn2_w_gate", "ffn2_w_up", "ffn2_w_down")
_SMALL = ("ffn1_norm", "mix_norm", "ffn2_norm", "final_norm", "b_in", "sinks", "rel_bias")
_ORDER = ("ffn1_norm", "ffn1_w_gate", "ffn1_w_up", "ffn1_w_down", "mix_norm", "w_in", "b_in", "w_branch_a",
          "w_branch_b", "w_out", "sinks", "rel_bias", "ffn2_norm", "ffn2_w_gate", "ffn2_w_up", "ffn2_w_down",
          "final_norm")


def _pack_small(p, extra=None):
    last = [p["sinks"].reshape(8), p["rel_bias"].reshape(640)]
    used = 648
    if extra is not None:
        last.append(extra.reshape(1))
        used += 1
    last.append(jnp.zeros((D_MODEL - used,), F32))
    rows = [p["ffn1_norm"].reshape(1, D_MODEL), p["mix_norm"].reshape(1, D_MODEL), p["ffn2_norm"].reshape(1, D_MODEL),
            p["final_norm"].reshape(1, D_MODEL), p["b_in"].reshape(5, D_MODEL), jnp.concatenate(last).reshape(1, D_MODEL),
            jnp.zeros((6, D_MODEL), F32)]
    return jnp.concatenate(rows, axis=0)


def _unpack_small(a):
    return {"ffn1_norm": a[0:1], "mix_norm": a[1:2], "ffn2_norm": a[2:3], "final_norm": a[3],
            "b_in": a[4:9].reshape(1, D_IN), "sinks": a[9, 0:8].reshape(1, 8), "rel_bias": a[9, 8:648].reshape(32, 20)}


def kernel(x, ffn1_norm, ffn1_w_gate, ffn1_w_up, ffn1_w_down, mix_norm, w_in, b_in, w_branch_a, w_branch_b, w_out, sinks, rel_bias, ffn2_norm, ffn2_w_gate, ffn2_w_up, ffn2_w_down, final_norm, loss_target, m_ffn1_norm, m_ffn1_w_gate, m_ffn1_w_up, m_ffn1_w_down, m_mix_norm, m_w_in, m_b_in, m_w_branch_a, m_w_branch_b, m_w_out, m_sinks, m_rel_bias, m_ffn2_norm, m_ffn2_w_gate, m_ffn2_w_up, m_ffn2_w_down, m_final_norm, v_ffn1_norm, v_ffn1_w_gate, v_ffn1_w_up, v_ffn1_w_down, v_mix_norm, v_w_in, v_b_in, v_w_branch_a, v_w_branch_b, v_w_out, v_sinks, v_rel_bias, v_ffn2_norm, v_ffn2_w_gate, v_ffn2_w_up, v_ffn2_w_down, v_final_norm):
    args = dict(locals())
    w = {n: args[n] for n in _ORDER}
    m = {n: args["m_" + n] for n in _ORDER}
    v = {n: args["v_" + n] for n in _ORDER}

    b16 = lambda a: a[0].astype(BF16)
    pack_a = jnp.concatenate([b16(w["ffn1_w_gate"]), b16(w["ffn1_w_up"]), b16(w["ffn2_w_gate"]), b16(w["ffn2_w_up"]),
                              b16(w["w_in"])], axis=1)
    pack_b = jnp.concatenate([b16(w["ffn1_w_down"]), b16(w["ffn2_w_down"]), b16(w["w_out"]),
                              b16(w["w_branch_a"]).reshape(64, D_MODEL), b16(w["w_branch_b"]).reshape(128, D_MODEL)],
                             axis=0)
    ga, gb = _gather_weights(pack_a, pack_b)
    ca_off = [0]
    for wd_ in _A_WIDTHS:
        ca_off.append(ca_off[-1] + wd_)
    rb_off = [0]
    for rw in _B_ROWS:
        rb_off.append(rb_off[-1] + rw)
    acol = lambda i: _chunks_to_cols(ga[:, :, ca_off[i]:ca_off[i + 1]])
    brow = lambda i: gb[:, rb_off[i]:rb_off[i + 1], :]
    W = {"wg1": acol(0), "wu1": acol(1), "wg2": acol(2), "wu2": acol(3), "w_in": acol(4),
         "wd1": brow(0).reshape(D_FF, D_MODEL), "wd2": brow(1).reshape(D_FF, D_MODEL),
         "wout": brow(2).reshape(D_MODEL, D_MODEL),
         "wa": _chunks_to_cols(brow(3).reshape(N_CHIPS, 256, 256)),
         "wb": _chunks_to_cols(brow(4).reshape(N_CHIPS, 512, 256))}
    S = {n: w[n] for n in _SMALL}

    loss_part, grad_x, G, GS = _local_step(x, loss_target, W, S)

    def halves(a):
        n, r, c = a.shape
        return jnp.transpose(a.reshape(n, 2, r // 2, c), (1, 0, 2, 3)).astype(BF16)

    pa_g = halves(jnp.concatenate([_cols_to_chunks(G["wg1"]), _cols_to_chunks(G["wu1"]), _cols_to_chunks(G["wg2"]),
                                   _cols_to_chunks(G["wu2"]), _cols_to_chunks(G["w_in"])], axis=2))
    pb_g = halves(jnp.concatenate([G["wd1"].reshape(N_CHIPS, 704, D_MODEL), G["wd2"].reshape(N_CHIPS, 704, D_MODEL),
                                   G["wout"].reshape(N_CHIPS, 256, D_MODEL),
                                   _cols_to_chunks(G["wa"]).reshape(N_CHIPS, 64, D_MODEL),
                                   _cols_to_chunks(G["wb"]).reshape(N_CHIPS, 128, D_MODEL)], axis=1))
    mine_a, got_a, mine_b, got_b = _pair_exchange(pa_g, pb_g)
    flat = lambda a: a.reshape(-1, a.shape[-1])
    sa = _sum_call([flat(mine_a), flat(got_a)], BF16, 128, "pair_sum_a").reshape(mine_a.shape)
    sb = _sum_call([flat(mine_b), flat(got_b)], BF16, 128, "pair_sum_b").reshape(mine_b.shape)
    own_a, rec_a, own_b, rec_b = _chip_exchange(sa, sb)
    fa = _sum_call([own_a, rec_a[0], rec_a[1], rec_a[2]], F32, 128, "chip_sum_a")
    fb = _sum_call([own_b, rec_b[0], rec_b[1], rec_b[2]], F32, 464, "chip_sum_b")
    full_a, full_b = _half_exchange(fa, fb)
    full_a = full_a.reshape(D_MODEL, 4096)
    full_b = full_b.reshape(1856, D_MODEL)
    grads = {"ffn1_w_gate": full_a[:, ca_off[0]:ca_off[1]], "ffn1_w_up": full_a[:, ca_off[1]:ca_off[2]],
             "ffn2_w_gate": full_a[:, ca_off[2]:ca_off[3]], "ffn2_w_up": full_a[:, ca_off[3]:ca_off[4]],
             "w_in": full_a[:, ca_off[4]:ca_off[5]],
             "ffn1_w_down": full_b[rb_off[0]:rb_off[1]], "ffn2_w_down": full_b[rb_off[1]:rb_off[2]],
             "w_out": full_b[rb_off[2]:rb_off[3]], "w_branch_a": full_b[rb_off[3]:rb_off[4]].reshape(256, 256),
             "w_branch_b": full_b[rb_off[4]:rb_off[5]].reshape(512, 256)}

    small = _allreduce_small(_pack_small(GS, extra=loss_part[0, 0]))
    loss = small[9, 648]

    out_g, out_d, out_m, out_v = {}, {}, {}, {}
    for n in _BIG:
        d_, nm_, nv_ = _adamw(w[n][0], grads[n], m[n][0], v[n][0], "adamw_" + n)
        out_g[n], out_d[n], out_m[n], out_v[n] = grads[n][None], d_[None], nm_[None], nv_[None]
    d_s, m_s, v_s = _adamw(_pack_small(w), small, _pack_small(m), _pack_small(v), "adamw_small")
    for dst, src in ((out_g, small), (out_d, d_s), (out_m, m_s), (out_v, v_s)):
        dst.update(_unpack_small(src))

    return (loss, grad_x, *[out_g[n] for n in _ORDER], *[out_d[n] for n in _ORDER],
            *[out_m[n] for n in _ORDER], *[out_v[n] for n in _ORDER])
```

```python
import math

import jax
import jax.numpy as jnp
from jax import lax
from jax.experimental import pallas as pl
from jax.experimental.pallas import tpu as pltpu

F32, BF16 = jnp.float32, jnp.bfloat16
MESH = pl.DeviceIdType.MESH

D_MODEL = 1024
D_FF = 2816
D_IN = 5120
HEAD_DIM = 64
BLOCK = 128
DIL_GROUPS = ((128, 1), (512, 4), (2048, 16))
B_WINDOW = 128
N_BUCKETS = 32
MAX_DISTANCE = 2048
EPS = 1e-6
N_CHIPS = 4
GW = 256
NEG = -1e30

ADAM_LR, ADAM_B1, ADAM_B2, ADAM_EPS, ADAM_WD, ADAM_STEP = 0.001, 0.9, 0.999, 1e-08, 0.01, 10

VMEM_BIG = 56 * 1024 * 1024
TM = 512
TM_BWD = 256
FF_CHUNKS = 2
DMA_SPLIT = 8


def _dot(a, b):
    return jnp.dot(a, b, preferred_element_type=F32)


def _dot_nt(a, b):
    return lax.dot_general(a, b, (((1,), (1,)), ((), ())), preferred_element_type=F32)


def _dot_tn(a, b):
    return lax.dot_general(a, b, (((0,), (0,)), ((), ())), preferred_element_type=F32)


def _sigmoid(x):
    return 1.0 / (1.0 + jnp.exp(-x))


def _params(sem, vmem=None):
    return pltpu.CompilerParams(dimension_semantics=sem, vmem_limit_bytes=vmem)


ANY = pl.BlockSpec(memory_space=pl.ANY)


def _me():
    return lax.axis_index("x"), lax.axis_index("y"), lax.axis_index("c")


_CHIP_RELS = ((1, 0), (0, 1), (1, 1))


def _flip(v, f):
    return 1 - v if f else v


def _remote(src, dst, ssem, rsem, peer):
    return pltpu.make_async_remote_copy(src_ref=src, dst_ref=dst, send_sem=ssem, recv_sem=rsem,
                                        device_id=peer, device_id_type=MESH)


def _row_pieces(rows, n):
    step = max(16, -(-rows // n) // 16 * 16)
    out, s = [], 0
    while s < rows:
        out.append((s, min(step, rows - s)))
        s += step
    return out


def _gather_rows(shards):
    nt = len(shards)
    rows = [s.shape[0] for s in shards]

    def body(*refs):
        srcs, outs = refs[:nt], refs[nt:2 * nt]
        ici_s, ici_r, d2d_s, d2d_r, loc = refs[2 * nt:]
        x, y, c = _me()
        j = 2 * x + y
        sib = (x, y, 1 - c)
        local = [pltpu.make_async_copy(srcs[t], outs[t].at[j], loc.at[t]) for t in range(nt)]
        for cp in local:
            cp.start()
        sends = []
        for k, (fx, fy) in enumerate(_CHIP_RELS):
            peer = (_flip(x, fx), _flip(y, fy), c)
            for t in range(nt):
                half = pl.ds(c * (rows[t] // 2), rows[t] // 2)
                cp = _remote(srcs[t].at[half], outs[t].at[j, half], ici_s.at[3 * t + k], ici_r.at[3 * t + k], peer)
                cp.start()
                sends.append(cp)
        fwds = []
        for k, (fx, fy) in enumerate(_CHIP_RELS):
            pj = 2 * _flip(x, fx) + _flip(y, fy)
            for t in range(nt):
                half = pl.ds(c * (rows[t] // 2), rows[t] // 2)
                blk = outs[t].at[pj, half]
                _remote(blk, blk, ici_s.at[3 * t + k], ici_r.at[3 * t + k], sib).wait_recv()
                cp = _remote(blk, blk, d2d_s.at[3 * t + k], d2d_r.at[3 * t + k], sib)
                cp.start()
                fwds.append(cp)
        for cp in fwds:
            cp.wait()
        for cp in sends:
            cp.wait_send()
        for cp in local:
            cp.wait()

    sems = [pltpu.SemaphoreType.DMA((3 * nt,)) for _ in range(4)] + [pltpu.SemaphoreType.DMA((nt,))]
    return pl.pallas_call(
        body, name="gather_weights",
        out_shape=tuple(jax.ShapeDtypeStruct((N_CHIPS,) + s.shape, s.dtype) for s in shards),
        in_specs=[ANY] * nt, out_specs=tuple([ANY] * nt), scratch_shapes=sems,
    )(*shards)


def _pair_exchange(grads):
    nt = len(grads)
    r2 = [g.shape[2] for g in grads]
    off = [sum(r2[:t]) for t in range(nt)]
    tot = sum(r2)

    def body(*refs):
        gs = refs[:nt]
        mine, got, ssem, rsem, lsem = refs[nt:]
        x, y, c = _me()
        sib = (x, y, 1 - c)
        for t in range(nt):
            for k in range(N_CHIPS):
                dst = pl.ds(off[t], r2[t])
                pltpu.make_async_copy(gs[t].at[k, c], mine.at[k, dst], lsem).start()
                _remote(gs[t].at[k, 1 - c], got.at[k, dst], ssem, rsem, sib).start()
        _remote(got, got, ssem, rsem, sib).wait()
        pltpu.make_async_copy(mine, mine, lsem).wait()

    shp = jax.ShapeDtypeStruct((N_CHIPS, tot, D_MODEL), BF16)
    return pl.pallas_call(
        body, name="grad_pair_exchange", out_shape=(shp, shp), in_specs=[ANY] * nt, out_specs=(ANY, ANY),
        scratch_shapes=[pltpu.SemaphoreType.DMA(()), pltpu.SemaphoreType.DMA(()), pltpu.SemaphoreType.DMA(())],
    )(*grads)


def _chip_exchange(s):
    r2 = s.shape[1]
    pieces = _row_pieces(r2, DMA_SPLIT // 2)

    def body(s_ref, own_ref, rec_ref, ssems, rsems, lsem):
        x, y, c = _me()
        j = 2 * x + y
        lc = pltpu.make_async_copy(s_ref.at[j], own_ref, lsem)
        lc.start()
        for k, (fx, fy) in enumerate(_CHIP_RELS):
            px, py = _flip(x, fx), _flip(y, fy)
            for st, sz in pieces:
                rows = pl.ds(st, sz)
                _remote(s_ref.at[2 * px + py, rows], rec_ref.at[k, rows], ssems.at[k], rsems.at[k],
                        (px, py, c)).start()
        for k in range(3):
            _remote(rec_ref.at[k], rec_ref.at[k], ssems.at[k], rsems.at[k], (x, y, c)).wait()
        lc.wait()

    return pl.pallas_call(
        body, name="grad_chip_exchange",
        out_shape=(jax.ShapeDtypeStruct(s.shape[1:], s.dtype), jax.ShapeDtypeStruct((3,) + s.shape[1:], s.dtype)),
        in_specs=[ANY], out_specs=(ANY, ANY),
        scratch_shapes=[pltpu.SemaphoreType.DMA((3,)), pltpu.SemaphoreType.DMA((3,)), pltpu.SemaphoreType.DMA(())],
    )(s)


def _half_exchange(f):
    pieces = _row_pieces(f.shape[0], DMA_SPLIT)

    def body(f_ref, o_ref, ssem, rsem, lsem):
        x, y, c = _me()
        sib = (x, y, 1 - c)
        lc = pltpu.make_async_copy(f_ref, o_ref.at[c], lsem)
        lc.start()
        for st, sz in pieces:
            rows = pl.ds(st, sz)
            _remote(f_ref.at[rows], o_ref.at[c, rows], ssem, rsem, sib).start()
        _remote(f_ref, o_ref.at[c], ssem, rsem, sib).wait()
        lc.wait()

    return pl.pallas_call(
        body, name="grad_half_exchange", out_shape=jax.ShapeDtypeStruct((2,) + f.shape, f.dtype),
        in_specs=[ANY], out_specs=ANY,
        scratch_shapes=[pltpu.SemaphoreType.DMA(()), pltpu.SemaphoreType.DMA(()), pltpu.SemaphoreType.DMA(())],
    )(f)


def _allreduce_small(vec):
    def body(v_ref, o_ref, buf, send_sems, recv_sems):
        x, y, c = _me()
        me = 4 * x + 2 * y + c
        buf[me] = v_ref[...]
        copies = []
        for k in range(1, 8):
            peer = (_flip(x, (k >> 2) & 1), _flip(y, (k >> 1) & 1), _flip(c, k & 1))
            cp = _remote(v_ref, buf.at[me], send_sems.at[k - 1], recv_sems.at[k - 1], peer)
            cp.start()
            copies.append(cp)
        for cp in copies:
            cp.wait()
        acc = buf[0]
        for i in range(1, 8):
            acc = acc + buf[i]
        o_ref[...] = acc

    vm = pl.BlockSpec(memory_space=pltpu.VMEM)
    return pl.pallas_call(
        body, name="allreduce_small", out_shape=jax.ShapeDtypeStruct(vec.shape, vec.dtype),
        in_specs=[vm], out_specs=vm,
        scratch_shapes=[pltpu.VMEM((8,) + vec.shape, vec.dtype), pltpu.SemaphoreType.DMA((7,)),
                        pltpu.SemaphoreType.DMA((7,))],
    )(vec)


def _sum_call(arrs, out_dtype, tr, name):
    r, c = arrs[0].shape
    n = len(arrs)

    def body(*refs):
        acc = refs[0][...].astype(F32)
        for i in range(1, n):
            acc = acc + refs[i][...].astype(F32)
        refs[n][...] = acc.astype(out_dtype)

    spec = pl.BlockSpec((tr, c), lambda i: (i, 0))
    return pl.pallas_call(
        body, name=name, grid=(r // tr,), out_shape=jax.ShapeDtypeStruct((r, c), out_dtype),
        in_specs=[spec] * n, out_specs=spec, compiler_params=_params(("parallel",)),
    )(*arrs)


def _ffn_fwd(h, gain, wgt, wut, wd):
    t = h.shape[0]
    fc = D_FF // FF_CHUNKS

    def body(h_ref, gain_ref, wg_hbm, wu_hbm, wd_hbm, hout_ref, n_ref, g_ref, u_ref, wg_v, wu_v, wd_v):
        @pl.when(pl.program_id(0) == 0)
        def _():
            pltpu.sync_copy(wg_hbm, wg_v)
            pltpu.sync_copy(wu_hbm, wu_v)
            pltpu.sync_copy(wd_hbm, wd_v)

        hh = h_ref[...]
        r = lax.rsqrt(jnp.mean(hh * hh, axis=-1, keepdims=True) + EPS)
        n = (hh * r * gain_ref[...]).astype(BF16)
        n_ref[...] = n
        acc = jnp.zeros((TM, D_MODEL), F32)
        for ci in range(FF_CHUNKS):
            sl = slice(ci * fc, (ci + 1) * fc)
            g = _dot_nt(n, wg_v[sl, :])
            u = _dot_nt(n, wu_v[sl, :])
            g_ref[:, sl] = g.astype(BF16)
            u_ref[:, sl] = u.astype(BF16)
            a = (g * _sigmoid(g) * u).astype(BF16)
            acc = acc + _dot(a, wd_v[sl, :])
        hout_ref[...] = hh + 0.5 * acc

    row = lambda w: pl.BlockSpec((TM, w), lambda i: (i, 0))
    wv = pltpu.VMEM((D_FF, D_MODEL), BF16)
    return pl.pallas_call(
        body, name="ffn_fwd", grid=(t // TM,),
        out_shape=(jax.ShapeDtypeStruct((t, D_MODEL), F32), jax.ShapeDtypeStruct((t, D_MODEL), BF16),
                   jax.ShapeDtypeStruct((t, D_FF), BF16), jax.ShapeDtypeStruct((t, D_FF), BF16)),
        in_specs=[row(D_MODEL), pl.BlockSpec((1, D_MODEL), lambda i: (0, 0)), ANY, ANY, ANY],
        out_specs=(row(D_MODEL), row(D_MODEL), row(D_FF), row(D_FF)),
        scratch_shapes=[wv, wv, wv],
        compiler_params=_params(("arbitrary",), VMEM_BIG),
    )(h, gain, wgt, wut, wd)


def _ffn_bwd(dhout, h, gain, g, u, wgt, wut, wd):
    t = h.shape[0]
    tm = TM_BWD
    fc = D_FF // FF_CHUNKS

    def body(dho_ref, h_ref, gain_ref, g_ref, u_ref, wg_hbm, wu_hbm, wd_hbm,
             dh_ref, dg_ref, du_ref, a_ref, df_ref, gg_ref, wg_v, wu_v, wd_v):
        @pl.when(pl.program_id(0) == 0)
        def _():
            pltpu.sync_copy(wg_hbm, wg_v)
            pltpu.sync_copy(wu_hbm, wu_v)
            pltpu.sync_copy(wd_hbm, wd_v)
            gg_ref[...] = jnp.zeros_like(gg_ref)

        dho = dho_ref[...]
        df = (0.5 * dho).astype(BF16)
        df_ref[...] = df
        dn = jnp.zeros((tm, D_MODEL), F32)
        for ci in range(FF_CHUNKS):
            sl = slice(ci * fc, (ci + 1) * fc)
            da = _dot_nt(df, wd_v[sl, :])
            gv = g_ref[:, sl].astype(F32)
            uv = u_ref[:, sl].astype(F32)
            sg = _sigmoid(gv)
            silu = gv * sg
            dg = (da * uv * (sg * (1.0 + gv * (1.0 - sg)))).astype(BF16)
            du = (da * silu).astype(BF16)
            dg_ref[:, sl] = dg
            du_ref[:, sl] = du
            a_ref[:, sl] = (silu * uv).astype(BF16)
            dn = dn + _dot(dg, wg_v[sl, :]) + _dot(du, wu_v[sl, :])
        hh = h_ref[...]
        r = lax.rsqrt(jnp.mean(hh * hh, axis=-1, keepdims=True) + EPS)
        hn = hh * r
        gg_ref[...] += jnp.sum(dn * hn, axis=0, keepdims=True)
        dng = dn * gain_ref[...]
        dh_ref[...] = dho + r * (dng - hn * jnp.mean(dng * hn, axis=-1, keepdims=True))

    row = lambda w: pl.BlockSpec((tm, w), lambda i: (i, 0))
    vec = pl.BlockSpec((1, D_MODEL), lambda i: (0, 0))
    wv = pltpu.VMEM((D_FF, D_MODEL), BF16)
    return pl.pallas_call(
        body, name="ffn_bwd", grid=(t // tm,),
        out_shape=(jax.ShapeDtypeStruct((t, D_MODEL), F32), jax.ShapeDtypeStruct((t, D_FF), BF16),
                   jax.ShapeDtypeStruct((t, D_FF), BF16), jax.ShapeDtypeStruct((t, D_FF), BF16),
                   jax.ShapeDtypeStruct((t, D_MODEL), BF16), jax.ShapeDtypeStruct((1, D_MODEL), F32)),
        in_specs=[row(D_MODEL), row(D_MODEL), vec, row(D_FF), row(D_FF), ANY, ANY, ANY],
        out_specs=(row(D_MODEL), row(D_FF), row(D_FF), row(D_FF), row(D_MODEL), vec),
        scratch_shapes=[wv, wv, wv],
        compiler_params=_params(("arbitrary",), VMEM_BIG),
    )(dhout, h, gain, g, u, wgt, wut, wd)


def _wgrad(lhs, rhs, rb, with_colsum=False, name="wgrad"):
    t, k = lhs.shape
    n = rhs.shape[1]
    tk = 512
    nt = t // tk

    def body(l_ref, r_ref, o_ref, *rest):
        acc = rest[-1]
        ti = pl.program_id(1)

        @pl.when(ti == 0)
        def _():
            acc[...] = jnp.zeros_like(acc)
            if with_colsum:
                rest[0][...] = jnp.zeros_like(rest[0])

        acc[...] += _dot_tn(l_ref[...], r_ref[...])
        if with_colsum:
            rest[0][...] += jnp.sum(l_ref[...].astype(F32), axis=0, keepdims=True)

        @pl.when(ti == nt - 1)
        def _():
            o_ref[...] = acc[...].astype(BF16)

    out_shape = [jax.ShapeDtypeStruct((k, n), BF16)]
    out_specs = [pl.BlockSpec((rb, n), lambda j, i: (j, 0))]
    if with_colsum:
        out_shape.append(jax.ShapeDtypeStruct((1, k), F32))
        out_specs.append(pl.BlockSpec((1, rb), lambda j, i: (0, j)))
    res = pl.pallas_call(
        body, name=name, grid=(k // rb, nt), out_shape=tuple(out_shape),
        in_specs=[pl.BlockSpec((tk, rb), lambda j, i: (i, j)), pl.BlockSpec((tk, n), lambda j, i: (i, 0))],
        out_specs=tuple(out_specs), scratch_shapes=[pltpu.VMEM((rb, n), F32)],
        compiler_params=_params(("arbitrary", "arbitrary"), VMEM_BIG),
    )(lhs, rhs)
    return res if with_colsum else res[0]


def _inproj_fwd(h, gain, wint, b_in):
    t = h.shape[0]
    nc = 5
    cw = D_IN // nc

    def body(h_ref, gain_ref, w_hbm, b_ref, u_ref, z_ref, w_v):
        @pl.when(pl.program_id(0) == 0)
        def _():
            pltpu.sync_copy(w_hbm, w_v)

        hh = h_ref[...]
        r = lax.rsqrt(jnp.mean(hh * hh, axis=-1, keepdims=True) + EPS)
        un = (hh * r * gain_ref[...]).astype(BF16)
        u_ref[...] = un
        for ci in range(nc):
            sl = slice(ci * cw, (ci + 1) * cw)
            z_ref[:, sl] = (_dot_nt(un, w_v[sl, :]) + b_ref[:, sl]).astype(BF16)

    row = lambda w: pl.BlockSpec((TM, w), lambda i: (i, 0))
    return pl.pallas_call(
        body, name="inproj_fwd", grid=(t // TM,),
        out_shape=(jax.ShapeDtypeStruct((t, D_MODEL), BF16), jax.ShapeDtypeStruct((t, D_IN), BF16)),
        in_specs=[row(D_MODEL), pl.BlockSpec((1, D_MODEL), lambda i: (0, 0)), ANY,
                  pl.BlockSpec((1, D_IN), lambda i: (0, 0))],
        out_specs=(row(D_MODEL), row(D_IN)),
        scratch_shapes=[pltpu.VMEM((D_IN, D_MODEL), BF16)],
        compiler_params=_params(("arbitrary",), VMEM_BIG),
    )(h, gain, wint, b_in)


def _inproj_bwd(dz, dh2, h, gain, wint):
    t = h.shape[0]
    nc = 5
    cw = D_IN // nc

    def body(dz_ref, dh2_ref, h_ref, gain_ref, w_hbm, dh_ref, gg_ref, w_v):
        @pl.when(pl.program_id(0) == 0)
        def _():
            pltpu.sync_copy(w_hbm, w_v)
            gg_ref[...] = jnp.zeros_like(gg_ref)

        du = jnp.zeros((TM, D_MODEL), F32)
        for ci in range(nc):
            sl = slice(ci * cw, (ci + 1) * cw)
            du = du + _dot(dz_ref[:, sl], w_v[sl, :])
        hh = h_ref[...]
        r = lax.rsqrt(jnp.mean(hh * hh, axis=-1, keepdims=True) + EPS)
        hn = hh * r
        gg_ref[...] += jnp.sum(du * hn, axis=0, keepdims=True)
        dng = du * gain_ref[...]
        dh_ref[...] = dh2_ref[...] + r * (dng - hn * jnp.mean(dng * hn, axis=-1, keepdims=True))

    row = lambda w: pl.BlockSpec((TM, w), lambda i: (i, 0))
    vec = pl.BlockSpec((1, D_MODEL), lambda i: (0, 0))
    return pl.pallas_call(
        body, name="inproj_bwd", grid=(t // TM,),
        out_shape=(jax.ShapeDtypeStruct((t, D_MODEL), F32), jax.ShapeDtypeStruct((1, D_MODEL), F32)),
        in_specs=[row(D_IN), row(D_MODEL), row(D_MODEL), vec, ANY],
        out_specs=(row(D_MODEL), vec),
        scratch_shapes=[pltpu.VMEM((D_IN, D_MODEL), BF16)],
        compiler_params=_params(("arbitrary",), VMEM_BIG),
    )(dz, dh2, h, gain, wint)


def _head_sum_matrix(w):
    i = lax.broadcasted_iota(jnp.int32, (w, w), 0) // HEAD_DIM
    j = lax.broadcasted_iota(jnp.int32, (w, w), 1) // HEAD_DIM
    return (i == j).astype(F32)


def _merge_fwd(o0, o1, o2, l0, l1, l2, yb, z, h1, wat, wbt, wout):
    t = h1.shape[0]

    def body(o0_ref, o1_ref, o2_ref, l0_ref, l1_ref, l2_ref, yb_ref, ga_ref, gb_ref, h1_ref, wa_ref, wb_ref, wo_ref,
             h2_ref, y_ref, lt_ref, pa_ref, pb_ref, mg_ref):
        la, lb, lc = l0_ref[...], l1_ref[...], l2_ref[...]
        mx = jnp.maximum(jnp.maximum(la, lb), lc)
        ea, eb, ec = jnp.exp(la - mx), jnp.exp(lb - mx), jnp.exp(lc - mx)
        den = ea + eb + ec
        y = (ea * o0_ref[...] + eb * o1_ref[...] + ec * o2_ref[...]) / den
        lt_ref[...] = mx + jnp.log(den)
        yb16 = y.astype(BF16)
        y_ref[...] = yb16
        pa = _dot_nt(yb16, wa_ref[...])
        pb = _dot_nt(yb_ref[...], wb_ref[...])
        pa_ref[...] = pa.astype(BF16)
        pb_ref[...] = pb.astype(BF16)
        mg = (_sigmoid(ga_ref[...].astype(F32)) * pa + _sigmoid(gb_ref[...].astype(F32)) * pb).astype(BF16)
        mg_ref[...] = mg
        h2_ref[...] = h1_ref[...] + _dot(mg, wo_ref[...])

    row = lambda w: pl.BlockSpec((TM, w), lambda i: (i, 0))
    full = lambda a: pl.BlockSpec(a.shape, lambda i: (0, 0))
    gate = lambda cb: pl.BlockSpec((TM, D_MODEL), lambda i: (i, cb))
    return pl.pallas_call(
        body, name="merge_fwd", grid=(t // TM,),
        out_shape=(jax.ShapeDtypeStruct((t, D_MODEL), F32), jax.ShapeDtypeStruct((t, GW), BF16),
                   jax.ShapeDtypeStruct((t, GW), F32), jax.ShapeDtypeStruct((t, D_MODEL), BF16),
                   jax.ShapeDtypeStruct((t, D_MODEL), BF16), jax.ShapeDtypeStruct((t, D_MODEL), BF16)),
        in_specs=[row(GW)] * 6 + [row(2 * GW), gate(3), gate(4), row(D_MODEL), full(wat), full(wbt), full(wout)],
        out_specs=(row(D_MODEL), row(GW), row(GW), row(D_MODEL), row(D_MODEL), row(D_MODEL)),
        compiler_params=_params(("parallel",), VMEM_BIG),
    )(o0, o1, o2, l0, l1, l2, yb, z, z, h1, wat, wbt, wout)


def _merge_bwd(dh2, pa, pb, z, y, yb, wat, wbt, wout):
    t = dh2.shape[0]

    def body(dh2_ref, pa_ref, pb_ref, ga_ref, gb_ref, y_ref, yb_ref, wa_ref, wb_ref, wo_ref,
             dpa_ref, dpb_ref, dga_ref, dgb_ref, dya_ref, dyb_ref, dh2b_ref, ca_ref, cb_ref):
        d16 = dh2_ref[...].astype(BF16)
        dh2b_ref[...] = d16
        dm = _dot_nt(d16, wo_ref[...])
        sa = _sigmoid(ga_ref[...].astype(F32))
        sb = _sigmoid(gb_ref[...].astype(F32))
        dpa = (dm * sa).astype(BF16)
        dpb = (dm * sb).astype(BF16)
        dpa_ref[...] = dpa
        dpb_ref[...] = dpb
        dga_ref[...] = (dm * pa_ref[...].astype(F32) * sa * (1.0 - sa)).astype(BF16)
        dgb_ref[...] = (dm * pb_ref[...].astype(F32) * sb * (1.0 - sb)).astype(BF16)
        dya = _dot(dpa, wa_ref[...])
        dyb = _dot(dpb, wb_ref[...])
        dya_ref[...] = dya.astype(BF16)
        dyb_ref[...] = dyb.astype(BF16)
        hp = lax.Precision.HIGHEST
        ca_ref[...] = jnp.dot(dya * y_ref[...].astype(F32), _head_sum_matrix(GW), precision=hp,
                              preferred_element_type=F32)
        cb_ref[...] = jnp.dot(dyb * yb_ref[...].astype(F32), _head_sum_matrix(2 * GW), precision=hp,
                              preferred_element_type=F32)

    row = lambda w: pl.BlockSpec((TM, w), lambda i: (i, 0))
    full = lambda a: pl.BlockSpec(a.shape, lambda i: (0, 0))
    gate = lambda cb: pl.BlockSpec((TM, D_MODEL), lambda i: (i, cb))
    bf = lambda w: jax.ShapeDtypeStruct((t, w), BF16)
    return pl.pallas_call(
        body, name="merge_bwd", grid=(t // TM,),
        out_shape=(bf(D_MODEL), bf(D_MODEL), bf(D_MODEL), bf(D_MODEL), bf(GW), bf(2 * GW), bf(D_MODEL),
                   jax.ShapeDtypeStruct((t, GW), F32), jax.ShapeDtypeStruct((t, 2 * GW), F32)),
        in_specs=[row(D_MODEL), row(D_MODEL), row(D_MODEL), gate(3), gate(4), row(GW), row(2 * GW),
                  full(wat), full(wbt), full(wout)],
        out_specs=(row(D_MODEL), row(D_MODEL), row(D_MODEL), row(D_MODEL), row(GW), row(2 * GW), row(D_MODEL),
                   row(GW), row(2 * GW)),
        compiler_params=_params(("parallel",), VMEM_BIG),
    )(dh2, pa, pb, z, z, y, yb, wat, wbt, wout)


def _loss_head(h3, gain, tgt):
    t = h3.shape[0]

    def body(h_ref, gain_ref, t_ref, dh_ref, loss_ref, gg_ref):
        @pl.when(pl.program_id(0) == 0)
        def _():
            loss_ref[...] = jnp.zeros_like(loss_ref)
            gg_ref[...] = jnp.zeros_like(gg_ref)

        hh = h_ref[...]
        r = lax.rsqrt(jnp.mean(hh * hh, axis=-1, keepdims=True) + EPS)
        hn = hh * r
        err = hn * gain_ref[...] - t_ref[...]
        part = jnp.sum(jnp.sum(err * err, axis=1, keepdims=True), axis=0, keepdims=True)
        loss_ref[...] += (0.5 / D_MODEL) * part
        dy = err * (1.0 / D_MODEL)
        gg_ref[...] += jnp.sum(dy * hn, axis=0, keepdims=True)
        dng = dy * gain_ref[...]
        dh_ref[...] = r * (dng - hn * jnp.mean(dng * hn, axis=-1, keepdims=True))

    row = pl.BlockSpec((TM, D_MODEL), lambda i: (i, 0))
    vec = pl.BlockSpec((1, D_MODEL), lambda i: (0, 0))
    return pl.pallas_call(
        body, name="loss_head", grid=(t // TM,),
        out_shape=(jax.ShapeDtypeStruct((t, D_MODEL), F32), jax.ShapeDtypeStruct((8, 128), F32),
                   jax.ShapeDtypeStruct((1, D_MODEL), F32)),
        in_specs=[row, vec, row], out_specs=(row, pl.BlockSpec((8, 128), lambda i: (0, 0)), vec),
        compiler_params=_params(("arbitrary",)),
    )(h3, gain, tgt)


def _lane_head(rows):
    return lax.broadcasted_iota(jnp.int32, (rows, GW), 1) // HEAD_DIM


def _kv_expand_matrix(r):
    ci = lax.broadcasted_iota(jnp.int32, (2 * HEAD_DIM, GW), 0)
    ji = lax.broadcasted_iota(jnp.int32, (2 * HEAD_DIM, GW), 1)
    return (ci == (ji % HEAD_DIM) + HEAD_DIM * r).astype(BF16)


def _attn_fwd(q_arr, k_arr, v_arr, bias, sink, *, grid, seq, kvw, q_map, k_map, v_map, bias_map, sink_map,
              has_sink, o_shape, o_dtype, name):
    nb = seq // BLOCK

    def body(q_ref, k_ref, v_ref, bias_ref, sink_ref, o_ref, lse_ref):
        lane_head = _lane_head(BLOCK)
        expand = _kv_expand_matrix(pl.program_id(1)) if kvw != GW else None

        def block(i, first):
            r0 = pl.multiple_of(i * BLOCK, BLOCK)
            qi = q_ref[0, pl.ds(r0, BLOCK), :]
            if first:
                kc = k_ref[0, pl.ds(0, BLOCK), :]
                vc = v_ref[0, pl.ds(0, BLOCK), :]
            else:
                k0 = pl.multiple_of(r0 - BLOCK, BLOCK)
                kc = k_ref[0, pl.ds(k0, 2 * BLOCK), :]
                vc = v_ref[0, pl.ds(k0, 2 * BLOCK), :]
            if expand is not None:
                kc = _dot(kc, expand).astype(BF16)
                vc = _dot(vc, expand).astype(BF16)
            acc_o = jnp.zeros((BLOCK, GW), F32)
            acc_l = jnp.zeros((BLOCK, GW), F32)
            for h in range(4):
                qh = jnp.where(lane_head == h, qi, jnp.zeros_like(qi))
                bias_h = bias_ref[h, :, BLOCK:] if first else bias_ref[h]
                s = _dot_nt(qh, kc) * (HEAD_DIM ** -0.5) + bias_h
                m = jnp.max(s, axis=-1, keepdims=True)
                if has_sink:
                    sk = sink_ref[0, h:h + 1, 0:1]
                    m = jnp.maximum(m, sk)
                p = jnp.exp(s - m)
                l = jnp.sum(p, axis=-1, keepdims=True)
                if has_sink:
                    l = l + jnp.exp(sk - m)
                oh = _dot(p.astype(BF16), vc) / l
                acc_o = jnp.where(lane_head == h, oh, acc_o)
                acc_l = jnp.where(lane_head == h, m + jnp.log(l), acc_l)
            o_ref[0, pl.ds(r0, BLOCK), :] = acc_o.astype(o_dtype)
            lse_ref[0, pl.ds(r0, BLOCK), :] = acc_l

        block(0, True)
        if nb > 1:
            def step(i, carry):
                block(i, False)
                return carry
            lax.fori_loop(1, nb, step, 0)

    blk = lambda w, m: pl.BlockSpec((1, seq, w), m)
    o_map = lambda n, r: (n, 0, r)
    return pl.pallas_call(
        body, name=name, grid=grid,
        out_shape=(jax.ShapeDtypeStruct(o_shape, o_dtype), jax.ShapeDtypeStruct(o_shape, F32)),
        in_specs=[blk(GW, q_map), blk(kvw, k_map), blk(kvw, v_map),
                  pl.BlockSpec((4, BLOCK, 2 * BLOCK), bias_map), pl.BlockSpec((1, 4, 128), sink_map)],
        out_specs=(blk(GW, o_map), blk(GW, o_map)),
        compiler_params=_params(("parallel", "arbitrary"), VMEM_BIG),
    )(q_arr, k_arr, v_arr, bias, sink)


def _attn_bwd(q_arr, k_arr, v_arr, bias, sink, dy, cc, lse, *, grid, seq, kvw, q_map, k_map, v_map, bias_map,
              sink_map, has_sink, n_bias, dq_shape, dkv_shape, name):
    nb = seq // BLOCK
    scale = HEAD_DIM ** -0.5

    def body(q_ref, k_ref, v_ref, bias_ref, sink_ref, dy_ref, c_ref, lse_ref,
             dq_ref, dk_ref, dv_ref, db_ref, dsk_ref, dk_acc, dv_acc, dk_half, dv_half):
        rr = pl.program_id(1)

        @pl.when((pl.program_id(0) == 0) & (rr == 0))
        def _():
            db_ref[...] = jnp.zeros_like(db_ref)
            dsk_ref[...] = jnp.zeros_like(dsk_ref)

        dk_acc[...] = jnp.zeros_like(dk_acc)
        dv_acc[...] = jnp.zeros_like(dv_acc)
        lane_head = _lane_head(BLOCK)
        expand = _kv_expand_matrix(rr) if kvw != GW else None
        hb = 4 * rr if n_bias == 8 else 0

        def block(i, first):
            r0 = pl.multiple_of(i * BLOCK, BLOCK)
            rows = pl.ds(r0, BLOCK)
            qi = q_ref[0, rows, :]
            dyi = dy_ref[0, rows, :]
            ci = c_ref[0, rows, :]
            li = lse_ref[0, rows, :]
            if first:
                krows = pl.ds(0, BLOCK)
            else:
                krows = pl.ds(pl.multiple_of(r0 - BLOCK, BLOCK), 2 * BLOCK)
            kc = k_ref[0, krows, :]
            vc = v_ref[0, krows, :]
            if expand is not None:
                kc = _dot(kc, expand).astype(BF16)
                vc = _dot(vc, expand).astype(BF16)
            nk = BLOCK if first else 2 * BLOCK
            dq = jnp.zeros((BLOCK, GW), F32)
            dkc = jnp.zeros((nk, GW), F32)
            dvc = jnp.zeros((nk, GW), F32)
            for h in range(4):
                sel = lane_head == h
                qh = jnp.where(sel, qi, jnp.zeros_like(qi))
                dyh = jnp.where(sel, dyi, jnp.zeros_like(dyi))
                bias_h = bias_ref[h, :, BLOCK:] if first else bias_ref[h]
                s = _dot_nt(qh, kc) * scale + bias_h
                lh = li[:, h * HEAD_DIM:h * HEAD_DIM + 1]
                ch = ci[:, h * HEAD_DIM:h * HEAD_DIM + 1]
                p = jnp.exp(s - lh)
                p16 = p.astype(BF16)
                dvc = dvc + _dot_tn(p16, dyh)
                dp = _dot_nt(dyh, vc)
                ds = p * (dp - ch)
                if first:
                    db_ref[hb + h, :, BLOCK:] += ds
                else:
                    db_ref[hb + h] += ds
                ds16 = ds.astype(BF16)
                dq = dq + jnp.where(sel, _dot(ds16, kc), 0.0)
                dkc = dkc + _dot_tn(ds16, qh)
                if has_sink:
                    sk = sink_ref[0, h:h + 1, 0:1]
                    val = -jnp.sum(jnp.exp(sk - lh) * ch, axis=0, keepdims=True)
                    dsk_ref[hb + h] += jnp.broadcast_to(val, (8, 128))
            dq_ref[0, rows, :] = (dq * scale).astype(BF16)
            dk_acc[krows, :] += dkc * scale
            dv_acc[krows, :] += dvc

        block(0, True)
        if nb > 1:
            def step(i, carry):
                block(i, False)
                return carry
            lax.fori_loop(1, nb, step, 0)

        if kvw == GW:
            dk_ref[0] = dk_acc[...].astype(BF16)
            dv_ref[0] = dv_acc[...].astype(BF16)
        else:
            def fold(acc):
                t2 = acc[:, :2 * HEAD_DIM] + acc[:, 2 * HEAD_DIM:]
                t2 = t2 + pltpu.roll(t2, HEAD_DIM, 1)
                lane = lax.broadcasted_iota(jnp.int32, t2.shape, 1) // HEAD_DIM
                return jnp.where(lane == rr, t2, 0.0)

            @pl.when(rr == 0)
            def _():
                dk_half[...] = fold(dk_acc[...])
                dv_half[...] = fold(dv_acc[...])

            @pl.when(rr == 1)
            def _():
                dk_ref[0] = (dk_half[...] + fold(dk_acc[...])).astype(BF16)
                dv_ref[0] = (dv_half[...] + fold(dv_acc[...])).astype(BF16)

    blk = lambda w, m: pl.BlockSpec((1, seq, w), m)
    o_map = lambda n, r: (n, 0, r)
    kv_out_map = o_map if kvw == GW else (lambda n, r: (n, 0, 0))
    return pl.pallas_call(
        body, name=name, grid=grid,
        out_shape=(jax.ShapeDtypeStruct(dq_shape, BF16), jax.ShapeDtypeStruct(dkv_shape, BF16),
                   jax.ShapeDtypeStruct(dkv_shape, BF16), jax.ShapeDtypeStruct((n_bias, BLOCK, 2 * BLOCK), F32),
                   jax.ShapeDtypeStruct((8, 8, 128), F32)),
        in_specs=[blk(GW, q_map), blk(kvw, k_map), blk(kvw, v_map),
                  pl.BlockSpec((4, BLOCK, 2 * BLOCK), bias_map), pl.BlockSpec((1, 4, 128), sink_map),
                  blk(GW, o_map), blk(GW, o_map), blk(GW, o_map)],
        out_specs=(blk(GW, o_map), blk(kvw, kv_out_map), blk(kvw, kv_out_map),
                   pl.BlockSpec((n_bias, BLOCK, 2 * BLOCK), lambda n, r: (0, 0, 0)),
                   pl.BlockSpec((8, 8, 128), lambda n, r: (0, 0, 0))),
        scratch_shapes=[pltpu.VMEM((seq, GW), F32), pltpu.VMEM((seq, GW), F32),
                        pltpu.VMEM((seq, 2 * HEAD_DIM), F32), pltpu.VMEM((seq, 2 * HEAD_DIM), F32)],
        compiler_params=_params(("arbitrary", "arbitrary"), VMEM_BIG),
    )(q_arr, k_arr, v_arr, bias, sink, dy, cc, lse)


def _bias_grad(ds_all, buckets):
    def body(ds_ref, bk_ref, o_ref):
        rows = lax.broadcasted_iota(jnp.int32, (N_BUCKETS, 128), 0)
        cols = lax.broadcasted_iota(jnp.int32, (N_BUCKETS, 128), 1)

        def per_bucket(b, acc):
            for h in range(20):
                gi = h // 4 if h < 12 else 3
                v = jnp.where(bk_ref[gi] == b, ds_ref[h], 0.0)
                v = jnp.sum(jnp.sum(v, axis=1, keepdims=True), axis=0, keepdims=True)
                acc = jnp.where((rows == b) & (cols == h), v, acc)
            return acc

        o_ref[...] = lax.fori_loop(0, N_BUCKETS, per_bucket, jnp.zeros((N_BUCKETS, 128), F32))

    vm = pl.BlockSpec(memory_space=pltpu.VMEM)
    return pl.pallas_call(body, name="bias_grad", out_shape=jax.ShapeDtypeStruct((N_BUCKETS, 128), F32),
                          in_specs=[vm, vm], out_specs=vm)(ds_all, buckets)


def _adamw(w, g, m, v, name):
    r, c = w.shape
    tr = r
    for cand in (256, 176, 128, 64, 32, 16, 8):
        if r % cand == 0:
            tr = cand
            break
    bc1 = 1.0 - ADAM_B1 ** ADAM_STEP
    bc2 = 1.0 - ADAM_B2 ** ADAM_STEP

    def body(w_ref, g_ref, m_ref, v_ref, d_ref, nm_ref, nv_ref):
        gv = g_ref[...]
        nm = ADAM_B1 * m_ref[...] + (1.0 - ADAM_B1) * gv
        nv = ADAM_B2 * v_ref[...] + (1.0 - ADAM_B2) * (gv * gv)
        nm_ref[...] = nm
        nv_ref[...] = nv
        d_ref[...] = -ADAM_LR * ((nm / bc1) / (jnp.sqrt(nv / bc2) + ADAM_EPS) + ADAM_WD * w_ref[...])

    spec = pl.BlockSpec((tr, c), lambda i: (i, 0))
    shp = jax.ShapeDtypeStruct((r, c), F32)
    return pl.pallas_call(body, name=name, grid=(r // tr,), out_shape=(shp, shp, shp),
                          in_specs=[spec] * 4, out_specs=(spec, spec, spec),
                          compiler_params=_params(("parallel",)))(w, g, m, v)


def _t5_bucket(dist):
    max_exact = N_BUCKETS // 2
    n = jnp.maximum(dist, 0)
    nf = jnp.maximum(n, 1).astype(F32)
    large = max_exact + (jnp.log(nf / max_exact) / math.log(MAX_DISTANCE / max_exact)
                         * (N_BUCKETS - max_exact)).astype(jnp.int32)
    large = jnp.minimum(large, N_BUCKETS - 1)
    return jnp.where(n < max_exact, n, large)


def _bias_tables(rel_bias):
    qi = jnp.arange(BLOCK)[:, None]
    ki = jnp.arange(2 * BLOCK)[None, :]
    dist = qi + BLOCK - ki
    specs = [(d, w // d, 4 * gi, 4 * gi + 4) for gi, (w, d) in enumerate(DIL_GROUPS)] + [(1, B_WINDOW - 1, 12, 20)]
    biases, buckets = [], []
    for stride, steps, h0, h1 in specs:
        valid = (dist >= 0) & (dist <= steps)
        bk = jnp.where(valid, _t5_bucket(dist * stride), -1).astype(jnp.int32)
        onehot = (bk[None, :, :] == jnp.arange(N_BUCKETS, dtype=jnp.int32)[:, None, None]).astype(F32)
        b = jnp.einsum("bqk,bh->hqk", onehot, rel_bias[:, h0:h1], precision=lax.Precision.HIGHEST)
        biases.append(jnp.where(valid[None], b, NEG))
        buckets.append(bk)
    return jnp.concatenate(biases, axis=0), jnp.stack(buckets, axis=0)


def _local_step(x, tgt, W, S):
    nseq, seq, _ = x.shape
    t = nseq * seq
    xf = x.reshape(t, D_MODEL)
    bias_all, buckets = _bias_tables(S["rel_bias"])
    sink_b = jnp.broadcast_to(S["sinks"].reshape(2, 4, 1), (2, 4, 128)).astype(F32)
    sink_0 = jnp.zeros((1, 4, 128), F32)

    h1, n1, g1, u1 = _ffn_fwd(xf, S["ffn1_norm"], W["wgt1"], W["wut1"], W["wd1"])
    un, z = _inproj_fwd(h1, S["mix_norm"], W["wint"], S["b_in"])

    a_cfg = []
    outs, lses = [], []
    for gi, (_, d) in enumerate(DIL_GROUPS):
        ln = seq // d
        cpr = D_IN // GW
        cfg = dict(grid=(nseq, d), seq=ln, kvw=GW,
                   q_map=lambda n, r, gi=gi, cpr=cpr: (n, 0, r * cpr + gi),
                   k_map=lambda n, r, gi=gi, cpr=cpr: (n, 0, r * cpr + 3 + gi),
                   v_map=lambda n, r, gi=gi, cpr=cpr: (n, 0, r * cpr + 6 + gi),
                   bias_map=lambda n, r: (0, 0, 0), sink_map=lambda n, r: (0, 0, 0), has_sink=False)
        a_cfg.append(cfg)
        zr = z.reshape(nseq, ln, d * D_IN)
        o, lse = _attn_fwd(zr, zr, zr, bias_all[4 * gi:4 * gi + 4], sink_0, o_shape=(nseq, ln, d * GW),
                           o_dtype=F32, name=f"attn_a{gi}_fwd", **cfg)
        outs.append(o.reshape(t, GW))
        lses.append(lse.reshape(t, GW))
    z3 = z.reshape(nseq, seq, D_IN)
    b_cfg = dict(grid=(nseq, 2), seq=seq, kvw=2 * HEAD_DIM,
                 q_map=lambda n, r: (n, 0, 9 + r), k_map=lambda n, r: (n, 0, 22), v_map=lambda n, r: (n, 0, 23),
                 bias_map=lambda n, r: (r, 0, 0), sink_map=lambda n, r: (r, 0, 0), has_sink=True)
    yb, lse_b = _attn_fwd(z3, z3, z3, bias_all[12:20], sink_b, o_shape=(nseq, seq, 2 * GW), o_dtype=BF16,
                          name="attn_b_fwd", **b_cfg)
    yb = yb.reshape(t, 2 * GW)

    h2, y, lse_tot, pa, pb, merged = _merge_fwd(outs[0], outs[1], outs[2], lses[0], lses[1], lses[2], yb, z, h1,
                                                W["wat"], W["wbt"], W["wout"])
    h3, n2, g2, u2 = _ffn_fwd(h2, S["ffn2_norm"], W["wgt2"], W["wut2"], W["wd2"])
    dh3, loss_part, g_final = _loss_head(h3, S["final_norm"].reshape(1, D_MODEL), tgt.reshape(t, D_MODEL))

    G, GS = {}, {}
    GS["final_norm"] = g_final
    dh2, dg2, du2, a2, df2, GS["ffn2_norm"] = _ffn_bwd(dh3, h2, S["ffn2_norm"], g2, u2, W["wgt2"], W["wut2"], W["wd2"])
    G["wgt2"] = _wgrad(dg2, n2, D_FF, name="wgrad_gate2")
    G["wut2"] = _wgrad(du2, n2, D_FF, name="wgrad_up2")
    G["wd2"] = _wgrad(a2, df2, D_FF, name="wgrad_down2")

    dpa, dpb, dga, dgb, dya, dyb, dh2b, ca, cb = _merge_bwd(dh2, pa, pb, z, y, yb, W["wat"], W["wbt"], W["wout"])
    G["wout"] = _wgrad(merged, dh2b, D_MODEL, name="wgrad_out")
    G["wat"] = _wgrad(dpa, y, D_MODEL, name="wgrad_branch_a")
    G["wbt"] = _wgrad(dpb, yb, D_MODEL, name="wgrad_branch_b")

    dqs, dks, dvs, dbs = [], [], [], []
    for gi, (_, d) in enumerate(DIL_GROUPS):
        ln = seq // d
        zr = z.reshape(nseq, ln, d * D_IN)
        fold = lambda a, ln=ln, d=d: a.reshape(nseq, ln, d * GW)
        shp = (nseq, ln, d * GW)
        dq, dk, dv, db, _ = _attn_bwd(zr, zr, zr, bias_all[4 * gi:4 * gi + 4], sink_0, fold(dya), fold(ca),
                                      fold(lse_tot), n_bias=4, dq_shape=shp, dkv_shape=shp,
                                      name=f"attn_a{gi}_bwd", **a_cfg[gi])
        dqs.append(dq.reshape(t, GW))
        dks.append(dk.reshape(t, GW))
        dvs.append(dv.reshape(t, GW))
        dbs.append(db)
    r3 = lambda a: a.reshape(nseq, seq, a.shape[-1])
    dqb, dkb, dvb, dbb, dsink = _attn_bwd(z3, z3, z3, bias_all[12:20], sink_b, r3(dyb), r3(cb), lse_b, n_bias=8,
                                          dq_shape=(nseq, seq, 2 * GW), dkv_shape=(nseq, seq, 2 * HEAD_DIM),
                                          name="attn_b_bwd", **b_cfg)
    dz = jnp.concatenate(dqs + dks + dvs + [dqb.reshape(t, 2 * GW), dkb.reshape(t, 2 * HEAD_DIM),
                                            dvb.reshape(t, 2 * HEAD_DIM), dga, dgb], axis=-1)
    gb_tab = _bias_grad(jnp.concatenate(dbs + [dbb], axis=0), buckets)
    GS["rel_bias"] = gb_tab[:, :20]
    GS["sinks"] = dsink[:, 0, 0].reshape(1, 8)

    G["wint"], GS["b_in"] = _wgrad(dz, un, D_IN // 2, with_colsum=True, name="wgrad_in")
    dh1, GS["mix_norm"] = _inproj_bwd(dz, dh2, h1, S["mix_norm"], W["wint"])

    dx, dg1, du1, a1, df1, GS["ffn1_norm"] = _ffn_bwd(dh1, xf, S["ffn1_norm"], g1, u1, W["wgt1"], W["wut1"], W["wd1"])
    G["wgt1"] = _wgrad(dg1, n1, D_FF, name="wgrad_gate1")
    G["wut1"] = _wgrad(du1, n1, D_FF, name="wgrad_up1")
    G["wd1"] = _wgrad(a1, df1, D_FF, name="wgrad_down1")
    return loss_part, dx.reshape(x.shape), G, GS


_SMALL = ("ffn1_norm", "mix_norm", "ffn2_norm", "final_norm", "b_in", "sinks", "rel_bias")
_ORDER = ("ffn1_norm", "ffn1_w_gate", "ffn1_w_up", "ffn1_w_down", "mix_norm", "w_in", "b_in", "w_branch_a",
          "w_branch_b", "w_out", "sinks", "rel_bias", "ffn2_norm", "ffn2_w_gate", "ffn2_w_up", "ffn2_w_down",
          "final_norm")
_BIG = (("wgt1", "ffn1_w_gate", True, 704), ("wut1", "ffn1_w_up", True, 704), ("wd1", "ffn1_w_down", False, 704),
        ("wint", "w_in", True, 1280), ("wout", "w_out", False, 256), ("wat", "w_branch_a", True, 64),
        ("wbt", "w_branch_b", True, 128), ("wgt2", "ffn2_w_gate", True, 704), ("wut2", "ffn2_w_up", True, 704),
        ("wd2", "ffn2_w_down", False, 704))
_FULL_SHAPE = {"wat": (D_MODEL, GW), "wbt": (D_MODEL, 2 * GW)}


def _pack_small(p, extra=None):
    last = [p["sinks"].reshape(8), p["rel_bias"].reshape(640)]
    used = 648
    if extra is not None:
        last.append(extra.reshape(1))
        used += 1
    last.append(jnp.zeros((D_MODEL - used,), F32))
    rows = [p["ffn1_norm"].reshape(1, D_MODEL), p["mix_norm"].reshape(1, D_MODEL), p["ffn2_norm"].reshape(1, D_MODEL),
            p["final_norm"].reshape(1, D_MODEL), p["b_in"].reshape(5, D_MODEL), jnp.concatenate(last).reshape(1, D_MODEL),
            jnp.zeros((6, D_MODEL), F32)]
    return jnp.concatenate(rows, axis=0)


def _unpack_small(a):
    return {"ffn1_norm": a[0:1], "mix_norm": a[1:2], "ffn2_norm": a[2:3], "final_norm": a[3],
            "b_in": a[4:9].reshape(1, D_IN), "sinks": a[9, 0:8].reshape(1, 8), "rel_bias": a[9, 8:648].reshape(32, 20)}


def kernel(x, ffn1_norm, ffn1_w_gate, ffn1_w_up, ffn1_w_down, mix_norm, w_in, b_in, w_branch_a, w_branch_b, w_out, sinks, rel_bias, ffn2_norm, ffn2_w_gate, ffn2_w_up, ffn2_w_down, final_norm, loss_target, m_ffn1_norm, m_ffn1_w_gate, m_ffn1_w_up, m_ffn1_w_down, m_mix_norm, m_w_in, m_b_in, m_w_branch_a, m_w_branch_b, m_w_out, m_sinks, m_rel_bias, m_ffn2_norm, m_ffn2_w_gate, m_ffn2_w_up, m_ffn2_w_down, m_final_norm, v_ffn1_norm, v_ffn1_w_gate, v_ffn1_w_up, v_ffn1_w_down, v_mix_norm, v_w_in, v_b_in, v_w_branch_a, v_w_branch_b, v_w_out, v_sinks, v_rel_bias, v_ffn2_norm, v_ffn2_w_gate, v_ffn2_w_up, v_ffn2_w_down, v_final_norm):
    args = dict(locals())
    w = {n: args[n] for n in _ORDER}
    m = {n: args["m_" + n] for n in _ORDER}
    v = {n: args["v_" + n] for n in _ORDER}

    shards = []
    for key, name, transposed, rows in _BIG:
        a = w[name][0]
        a = (a.T if transposed else a).astype(BF16)
        shards.append(a.reshape(rows, D_MODEL))
    gathered = _gather_rows(shards)
    W = {}
    for (key, name, transposed, rows), g in zip(_BIG, gathered):
        W[key] = g.reshape(_FULL_SHAPE.get(key, (N_CHIPS * rows, D_MODEL)))
    S = {n: w[n] for n in _SMALL}

    loss_part, grad_x, G, GS = _local_step(x, loss_target, W, S)

    parts = [G[key].reshape(N_CHIPS, 2, rows // 2, D_MODEL) for key, _, _, rows in _BIG]
    mine, got = _pair_exchange(parts)
    flat = lambda a: a.reshape(-1, a.shape[-1])
    pair = _sum_call([flat(mine), flat(got)], BF16, 128, "pair_sum").reshape(mine.shape)
    own, rec = _chip_exchange(pair)
    final_half = _sum_call([own, rec[0], rec[1], rec[2]], F32, 496, "chip_sum")
    full = _half_exchange(final_half)
    grads, off = {}, 0
    for key, name, transposed, rows in _BIG:
        g = full[:, off:off + rows // 2].reshape(rows, D_MODEL)
        off += rows // 2
        nat = w[name][0].shape
        grads[name] = g.reshape(nat[1], nat[0]).T if transposed else g.reshape(nat)

    small = _allreduce_small(_pack_small(GS, extra=loss_part[0, 0]))
    loss = small[9, 648]

    out_g, out_d, out_m, out_v = {}, {}, {}, {}
    for _, n, _, _ in _BIG:
        d_, nm_, nv_ = _adamw(w[n][0], grads[n], m[n][0], v[n][0], "adamw_" + n)
        out_g[n], out_d[n], out_m[n], out_v[n] = grads[n][None], d_[None], nm_[None], nv_[None]
    d_s, m_s, v_s = _adamw(_pack_small(w), small, _pack_small(m), _pack_small(v), "adamw_small")
    for dst, src in ((out_g, small), (out_d, d_s), (out_m, m_s), (out_v, v_s)):
        dst.update(_unpack_small(src))

    return (loss, grad_x, *[out_g[n] for n in _ORDER], *[out_d[n] for n in _ORDER],
            *[out_m[n] for n in _ORDER], *[out_v[n] for n in _ORDER])
```

```python
import math

import jax
import jax.numpy as jnp
from jax import lax
from jax.experimental import pallas as pl
from jax.experimental.pallas import tpu as pltpu

F32, BF16 = jnp.float32, jnp.bfloat16
MESH = pl.DeviceIdType.MESH

D_MODEL = 1024
D_FF = 2816
D_IN = 5120
HEAD_DIM = 64
BLOCK = 128
DIL_GROUPS = ((128, 1), (512, 4), (2048, 16))
B_WINDOW = 128
N_BUCKETS = 32
MAX_DISTANCE = 2048
EPS = 1e-6
N_CHIPS = 4
GW = 256
NEG = -1e30

ADAM_LR, ADAM_B1, ADAM_B2, ADAM_EPS, ADAM_WD, ADAM_STEP = 0.001, 0.9, 0.999, 1e-08, 0.01, 10

VMEM_BIG = 56 * 1024 * 1024
TM = 512
TM_BWD = 256
FF_CHUNKS = 2
DMA_SPLIT = 8


def _dot(a, b):
    return jnp.dot(a, b, preferred_element_type=F32)


def _dot_nt(a, b):
    return lax.dot_general(a, b, (((1,), (1,)), ((), ())), preferred_element_type=F32)


def _dot_tn(a, b):
    return lax.dot_general(a, b, (((0,), (0,)), ((), ())), preferred_element_type=F32)


def _sigmoid(x):
    return 1.0 / (1.0 + jnp.exp(-x))


def _params(sem, vmem=None):
    return pltpu.CompilerParams(dimension_semantics=sem, vmem_limit_bytes=vmem)


ANY = pl.BlockSpec(memory_space=pl.ANY)


def _me():
    return lax.axis_index("x"), lax.axis_index("y"), lax.axis_index("c")


_CHIP_RELS = ((1, 0), (0, 1), (1, 1))


def _flip(v, f):
    return 1 - v if f else v


def _remote(src, dst, ssem, rsem, peer):
    return pltpu.make_async_remote_copy(src_ref=src, dst_ref=dst, send_sem=ssem, recv_sem=rsem,
                                        device_id=peer, device_id_type=MESH)


def _row_pieces(rows, n):
    step = max(16, -(-rows // n) // 16 * 16)
    out, s = [], 0
    while s < rows:
        out.append((s, min(step, rows - s)))
        s += step
    return out


def _gather_rows(shards):
    nt = len(shards)
    rows = [s.shape[0] for s in shards]

    def body(*refs):
        srcs, outs = refs[:nt], refs[nt:2 * nt]
        ici_s, ici_r, d2d_s, d2d_r, loc = refs[2 * nt:]
        x, y, c = _me()
        j = 2 * x + y
        sib = (x, y, 1 - c)
        local = [pltpu.make_async_copy(srcs[t], outs[t].at[j], loc.at[t]) for t in range(nt)]
        for cp in local:
            cp.start()
        sends = []
        for k, (fx, fy) in enumerate(_CHIP_RELS):
            peer = (_flip(x, fx), _flip(y, fy), c)
            for t in range(nt):
                half = pl.ds(c * (rows[t] // 2), rows[t] // 2)
                cp = _remote(srcs[t].at[half], outs[t].at[j, half], ici_s.at[3 * t + k], ici_r.at[3 * t + k], peer)
                cp.start()
                sends.append(cp)
        fwds = []
        for k, (fx, fy) in enumerate(_CHIP_RELS):
            pj = 2 * _flip(x, fx) + _flip(y, fy)
            for t in range(nt):
                half = pl.ds(c * (rows[t] // 2), rows[t] // 2)
                blk = outs[t].at[pj, half]
                _remote(blk, blk, ici_s.at[3 * t + k], ici_r.at[3 * t + k], sib).wait_recv()
                cp = _remote(blk, blk, d2d_s.at[3 * t + k], d2d_r.at[3 * t + k], sib)
                cp.start()
                fwds.append(cp)
        for cp in fwds:
            cp.wait()
        for cp in sends:
            cp.wait_send()
        for cp in local:
            cp.wait()

    sems = [pltpu.SemaphoreType.DMA((3 * nt,)) for _ in range(4)] + [pltpu.SemaphoreType.DMA((nt,))]
    return pl.pallas_call(
        body, name="gather_weights",
        out_shape=tuple(jax.ShapeDtypeStruct((N_CHIPS,) + s.shape, s.dtype) for s in shards),
        in_specs=[ANY] * nt, out_specs=tuple([ANY] * nt), scratch_shapes=sems,
    )(*shards)


VMEM_WHOLE = pl.BlockSpec(memory_space=pltpu.VMEM)


def _pair_reduce(grads, name):
    nt = len(grads)
    r2 = [g.shape[2] for g in grads]
    off = [sum(r2[:t]) for t in range(nt)]
    tot = sum(r2)

    def body(*refs):
        gs = refs[:nt]
        s_ref, got, ssem, rsem = refs[nt:]
        x, y, c = _me()
        sib = (x, y, 1 - c)
        for t in range(nt):
            for k in range(N_CHIPS):
                _remote(gs[t].at[k, 1 - c], got.at[k, pl.ds(off[t], r2[t])], ssem, rsem, sib).start()
        _remote(got, got, ssem, rsem, sib).wait()
        for t in range(nt):
            for k in range(N_CHIPS):
                rows = slice(off[t], off[t] + r2[t])
                s_ref[k, rows, :] = (gs[t][k, c].astype(F32) + got[k, rows, :].astype(F32)).astype(BF16)

    shp = jax.ShapeDtypeStruct((N_CHIPS, tot, D_MODEL), BF16)
    return pl.pallas_call(
        body, name=name, out_shape=shp, in_specs=[VMEM_WHOLE] * nt, out_specs=VMEM_WHOLE,
        scratch_shapes=[pltpu.VMEM((N_CHIPS, tot, D_MODEL), BF16), pltpu.SemaphoreType.DMA(()),
                        pltpu.SemaphoreType.DMA(())],
        compiler_params=pltpu.CompilerParams(vmem_limit_bytes=VMEM_BIG),
    )(*grads)


def _chip_exchange(parts):
    ng = len(parts)
    r2 = [p.shape[1] for p in parts]
    off = [sum(r2[:g]) for g in range(ng)]
    tot = sum(r2)

    def body(*refs):
        ps = refs[:ng]
        own_ref, rec_ref, ssems, rsems, lsem = refs[ng:]
        x, y, c = _me()
        j = 2 * x + y
        for g in range(ng):
            pltpu.make_async_copy(ps[g].at[j], own_ref.at[pl.ds(off[g], r2[g])], lsem).start()
        for k, (fx, fy) in enumerate(_CHIP_RELS):
            px, py = _flip(x, fx), _flip(y, fy)
            for g in range(ng):
                for st, sz in _row_pieces(r2[g], 2):
                    _remote(ps[g].at[2 * px + py, pl.ds(st, sz)], rec_ref.at[k, pl.ds(off[g] + st, sz)],
                            ssems.at[k], rsems.at[k], (px, py, c)).start()
        for k in range(3):
            _remote(rec_ref.at[k], rec_ref.at[k], ssems.at[k], rsems.at[k], (x, y, c)).wait()
        pltpu.make_async_copy(own_ref, own_ref, lsem).wait()

    return pl.pallas_call(
        body, name="grad_chip_exchange",
        out_shape=(jax.ShapeDtypeStruct((tot, D_MODEL), BF16), jax.ShapeDtypeStruct((3, tot, D_MODEL), BF16)),
        in_specs=[ANY] * ng, out_specs=(ANY, ANY),
        scratch_shapes=[pltpu.SemaphoreType.DMA((3,)), pltpu.SemaphoreType.DMA((3,)), pltpu.SemaphoreType.DMA(())],
    )(*parts)


def _final_reduce(own, rec):
    r2 = own.shape[0]
    pieces = _row_pieces(r2, DMA_SPLIT)

    def body(own_ref, rec_ref, o_ref, fbuf, ssem, rsem, lsem):
        x, y, c = _me()
        sib = (x, y, 1 - c)
        for st, sz in pieces:
            rows = slice(st, st + sz)
            fbuf[rows, :] = (own_ref[rows, :].astype(F32) + rec_ref[0, rows, :].astype(F32)
                             + rec_ref[1, rows, :].astype(F32) + rec_ref[2, rows, :].astype(F32))
            pltpu.make_async_copy(fbuf.at[pl.ds(st, sz)], o_ref.at[c, pl.ds(st, sz)], lsem).start()
            _remote(fbuf.at[pl.ds(st, sz)], o_ref.at[c, pl.ds(st, sz)], ssem, rsem, sib).start()
        _remote(fbuf, o_ref.at[c], ssem, rsem, sib).wait()
        pltpu.make_async_copy(fbuf, o_ref.at[c], lsem).wait()

    return pl.pallas_call(
        body, name="grad_final_reduce", out_shape=jax.ShapeDtypeStruct((2, r2, D_MODEL), F32),
        in_specs=[VMEM_WHOLE, VMEM_WHOLE], out_specs=ANY,
        scratch_shapes=[pltpu.VMEM((r2, D_MODEL), F32), pltpu.SemaphoreType.DMA(()), pltpu.SemaphoreType.DMA(()),
                        pltpu.SemaphoreType.DMA(())],
        compiler_params=pltpu.CompilerParams(vmem_limit_bytes=VMEM_BIG),
    )(own, rec)


def _allreduce_small(vec):
    def body(v_ref, o_ref, buf, send_sems, recv_sems):
        x, y, c = _me()
        me = 4 * x + 2 * y + c
        buf[me] = v_ref[...]
        copies = []
        for k in range(1, 8):
            peer = (_flip(x, (k >> 2) & 1), _flip(y, (k >> 1) & 1), _flip(c, k & 1))
            cp = _remote(v_ref, buf.at[me], send_sems.at[k - 1], recv_sems.at[k - 1], peer)
            cp.start()
            copies.append(cp)
        for cp in copies:
            cp.wait()
        acc = buf[0]
        for i in range(1, 8):
            acc = acc + buf[i]
        o_ref[...] = acc

    vm = pl.BlockSpec(memory_space=pltpu.VMEM)
    return pl.pallas_call(
        body, name="allreduce_small", out_shape=jax.ShapeDtypeStruct(vec.shape, vec.dtype),
        in_specs=[vm], out_specs=vm,
        scratch_shapes=[pltpu.VMEM((8,) + vec.shape, vec.dtype), pltpu.SemaphoreType.DMA((7,)),
                        pltpu.SemaphoreType.DMA((7,))],
    )(vec)


def _sum_call(arrs, out_dtype, tr, name):
    r, c = arrs[0].shape
    n = len(arrs)

    def body(*refs):
        acc = refs[0][...].astype(F32)
        for i in range(1, n):
            acc = acc + refs[i][...].astype(F32)
        refs[n][...] = acc.astype(out_dtype)

    spec = pl.BlockSpec((tr, c), lambda i: (i, 0))
    return pl.pallas_call(
        body, name=name, grid=(r // tr,), out_shape=jax.ShapeDtypeStruct((r, c), out_dtype),
        in_specs=[spec] * n, out_specs=spec, compiler_params=_params(("parallel",)),
    )(*arrs)


def _ffn_fwd(h, gain, wgt, wut, wd):
    t = h.shape[0]
    fc = D_FF // FF_CHUNKS

    def body(h_ref, gain_ref, wg_hbm, wu_hbm, wd_hbm, hout_ref, n_ref, g_ref, u_ref, wg_v, wu_v, wd_v):
        @pl.when(pl.program_id(0) == 0)
        def _():
            pltpu.sync_copy(wg_hbm, wg_v)
            pltpu.sync_copy(wu_hbm, wu_v)
            pltpu.sync_copy(wd_hbm, wd_v)

        hh = h_ref[...]
        r = lax.rsqrt(jnp.mean(hh * hh, axis=-1, keepdims=True) + EPS)
        n = (hh * r * gain_ref[...]).astype(BF16)
        n_ref[...] = n
        acc = jnp.zeros((TM, D_MODEL), F32)
        for ci in range(FF_CHUNKS):
            sl = slice(ci * fc, (ci + 1) * fc)
            g = _dot_nt(n, wg_v[sl, :])
            u = _dot_nt(n, wu_v[sl, :])
            g_ref[:, sl] = g.astype(BF16)
            u_ref[:, sl] = u.astype(BF16)
            a = (g * _sigmoid(g) * u).astype(BF16)
            acc = acc + _dot(a, wd_v[sl, :])
        hout_ref[...] = hh + 0.5 * acc

    row = lambda w: pl.BlockSpec((TM, w), lambda i: (i, 0))
    wv = pltpu.VMEM((D_FF, D_MODEL), BF16)
    return pl.pallas_call(
        body, name="ffn_fwd", grid=(t // TM,),
        out_shape=(jax.ShapeDtypeStruct((t, D_MODEL), F32), jax.ShapeDtypeStruct((t, D_MODEL), BF16),
                   jax.ShapeDtypeStruct((t, D_FF), BF16), jax.ShapeDtypeStruct((t, D_FF), BF16)),
        in_specs=[row(D_MODEL), pl.BlockSpec((1, D_MODEL), lambda i: (0, 0)), ANY, ANY, ANY],
        out_specs=(row(D_MODEL), row(D_MODEL), row(D_FF), row(D_FF)),
        scratch_shapes=[wv, wv, wv],
        compiler_params=_params(("arbitrary",), VMEM_BIG),
    )(h, gain, wgt, wut, wd)


def _ffn_bwd(dhout, h, gain, g, u, wgt, wut, wd):
    t = h.shape[0]
    tm = TM_BWD
    fc = D_FF // FF_CHUNKS

    def body(dho_ref, h_ref, gain_ref, g_ref, u_ref, wg_hbm, wu_hbm, wd_hbm,
             dh_ref, dg_ref, du_ref, a_ref, df_ref, gg_ref, wg_v, wu_v, wd_v):
        @pl.when(pl.program_id(0) == 0)
        def _():
            pltpu.sync_copy(wg_hbm, wg_v)
            pltpu.sync_copy(wu_hbm, wu_v)
            pltpu.sync_copy(wd_hbm, wd_v)
            gg_ref[...] = jnp.zeros_like(gg_ref)

        dho = dho_ref[...]
        df = (0.5 * dho).astype(BF16)
        df_ref[...] = df
        dn = jnp.zeros((tm, D_MODEL), F32)
        for ci in range(FF_CHUNKS):
            sl = slice(ci * fc, (ci + 1) * fc)
            da = _dot_nt(df, wd_v[sl, :])
            gv = g_ref[:, sl].astype(F32)
            uv = u_ref[:, sl].astype(F32)
            sg = _sigmoid(gv)
            silu = gv * sg
            dg = (da * uv * (sg * (1.0 + gv * (1.0 - sg)))).astype(BF16)
            du = (da * silu).astype(BF16)
            dg_ref[:, sl] = dg
            du_ref[:, sl] = du
            a_ref[:, sl] = (silu * uv).astype(BF16)
            dn = dn + _dot(dg, wg_v[sl, :]) + _dot(du, wu_v[sl, :])
        hh = h_ref[...]
        r = lax.rsqrt(jnp.mean(hh * hh, axis=-1, keepdims=True) + EPS)
        hn = hh * r
        gg_ref[...] += jnp.sum(dn * hn, axis=0, keepdims=True)
        dng = dn * gain_ref[...]
        dh_ref[...] = dho + r * (dng - hn * jnp.mean(dng * hn, axis=-1, keepdims=True))

    row = lambda w: pl.BlockSpec((tm, w), lambda i: (i, 0))
    vec = pl.BlockSpec((1, D_MODEL), lambda i: (0, 0))
    wv = pltpu.VMEM((D_FF, D_MODEL), BF16)
    return pl.pallas_call(
        body, name="ffn_bwd", grid=(t // tm,),
        out_shape=(jax.ShapeDtypeStruct((t, D_MODEL), F32), jax.ShapeDtypeStruct((t, D_FF), BF16),
                   jax.ShapeDtypeStruct((t, D_FF), BF16), jax.ShapeDtypeStruct((t, D_FF), BF16),
                   jax.ShapeDtypeStruct((t, D_MODEL), BF16), jax.ShapeDtypeStruct((1, D_MODEL), F32)),
        in_specs=[row(D_MODEL), row(D_MODEL), vec, row(D_FF), row(D_FF), ANY, ANY, ANY],
        out_specs=(row(D_MODEL), row(D_FF), row(D_FF), row(D_FF), row(D_MODEL), vec),
        scratch_shapes=[wv, wv, wv],
        compiler_params=_params(("arbitrary",), VMEM_BIG),
    )(dhout, h, gain, g, u, wgt, wut, wd)


def _wgrad(lhs, rhs, rb, with_colsum=False, name="wgrad"):
    t, k = lhs.shape
    n = rhs.shape[1]
    tk = 512
    nt = t // tk

    def body(l_ref, r_ref, o_ref, *rest):
        acc = rest[-1]
        ti = pl.program_id(1)

        @pl.when(ti == 0)
        def _():
            acc[...] = jnp.zeros_like(acc)
            if with_colsum:
                rest[0][...] = jnp.zeros_like(rest[0])

        acc[...] += _dot_tn(l_ref[...], r_ref[...])
        if with_colsum:
            rest[0][...] += jnp.sum(l_ref[...].astype(F32), axis=0, keepdims=True)

        @pl.when(ti == nt - 1)
        def _():
            o_ref[...] = acc[...].astype(BF16)

    out_shape = [jax.ShapeDtypeStruct((k, n), BF16)]
    out_specs = [pl.BlockSpec((rb, n), lambda j, i: (j, 0))]
    if with_colsum:
        out_shape.append(jax.ShapeDtypeStruct((1, k), F32))
        out_specs.append(pl.BlockSpec((1, rb), lambda j, i: (0, j)))
    res = pl.pallas_call(
        body, name=name, grid=(k // rb, nt), out_shape=tuple(out_shape),
        in_specs=[pl.BlockSpec((tk, rb), lambda j, i: (i, j)), pl.BlockSpec((tk, n), lambda j, i: (i, 0))],
        out_specs=tuple(out_specs), scratch_shapes=[pltpu.VMEM((rb, n), F32)],
        compiler_params=_params(("arbitrary", "arbitrary"), VMEM_BIG),
    )(lhs, rhs)
    return res if with_colsum else res[0]


def _inproj_fwd(h, gain, wint, b_in):
    t = h.shape[0]
    nc = 5
    cw = D_IN // nc

    def body(h_ref, gain_ref, w_hbm, b_ref, u_ref, z_ref, w_v):
        @pl.when(pl.program_id(0) == 0)
        def _():
            pltpu.sync_copy(w_hbm, w_v)

        hh = h_ref[...]
        r = lax.rsqrt(jnp.mean(hh * hh, axis=-1, keepdims=True) + EPS)
        un = (hh * r * gain_ref[...]).astype(BF16)
        u_ref[...] = un
        for ci in range(nc):
            sl = slice(ci * cw, (ci + 1) * cw)
            z_ref[:, sl] = (_dot_nt(un, w_v[sl, :]) + b_ref[:, sl]).astype(BF16)

    row = lambda w: pl.BlockSpec((TM, w), lambda i: (i, 0))
    return pl.pallas_call(
        body, name="inproj_fwd", grid=(t // TM,),
        out_shape=(jax.ShapeDtypeStruct((t, D_MODEL), BF16), jax.ShapeDtypeStruct((t, D_IN), BF16)),
        in_specs=[row(D_MODEL), pl.BlockSpec((1, D_MODEL), lambda i: (0, 0)), ANY,
                  pl.BlockSpec((1, D_IN), lambda i: (0, 0))],
        out_specs=(row(D_MODEL), row(D_IN)),
        scratch_shapes=[pltpu.VMEM((D_IN, D_MODEL), BF16)],
        compiler_params=_params(("arbitrary",), VMEM_BIG),
    )(h, gain, wint, b_in)


def _inproj_bwd(dz, dh2, h, gain, wint):
    t = h.shape[0]
    nc = 5
    cw = D_IN // nc

    def body(dz_ref, dh2_ref, h_ref, gain_ref, w_hbm, dh_ref, gg_ref, w_v):
        @pl.when(pl.program_id(0) == 0)
        def _():
            pltpu.sync_copy(w_hbm, w_v)
            gg_ref[...] = jnp.zeros_like(gg_ref)

        du = jnp.zeros((TM, D_MODEL), F32)
        for ci in range(nc):
            sl = slice(ci * cw, (ci + 1) * cw)
            du = du + _dot(dz_ref[:, sl], w_v[sl, :])
        hh = h_ref[...]
        r = lax.rsqrt(jnp.mean(hh * hh, axis=-1, keepdims=True) + EPS)
        hn = hh * r
        gg_ref[...] += jnp.sum(du * hn, axis=0, keepdims=True)
        dng = du * gain_ref[...]
        dh_ref[...] = dh2_ref[...] + r * (dng - hn * jnp.mean(dng * hn, axis=-1, keepdims=True))

    row = lambda w: pl.BlockSpec((TM, w), lambda i: (i, 0))
    vec = pl.BlockSpec((1, D_MODEL), lambda i: (0, 0))
    return pl.pallas_call(
        body, name="inproj_bwd", grid=(t // TM,),
        out_shape=(jax.ShapeDtypeStruct((t, D_MODEL), F32), jax.ShapeDtypeStruct((1, D_MODEL), F32)),
        in_specs=[row(D_IN), row(D_MODEL), row(D_MODEL), vec, ANY],
        out_specs=(row(D_MODEL), vec),
        scratch_shapes=[pltpu.VMEM((D_IN, D_MODEL), BF16)],
        compiler_params=_params(("arbitrary",), VMEM_BIG),
    )(dz, dh2, h, gain, wint)


def _head_sum_matrix(w):
    i = lax.broadcasted_iota(jnp.int32, (w, w), 0) // HEAD_DIM
    j = lax.broadcasted_iota(jnp.int32, (w, w), 1) // HEAD_DIM
    return (i == j).astype(F32)


def _merge_fwd(o0, o1, o2, l0, l1, l2, yb, z, h1, wat, wbt, wout):
    t = h1.shape[0]

    def body(o0_ref, o1_ref, o2_ref, l0_ref, l1_ref, l2_ref, yb_ref, ga_ref, gb_ref, h1_ref, wa_ref, wb_ref, wo_ref,
             h2_ref, y_ref, lt_ref, pa_ref, pb_ref, mg_ref):
        la, lb, lc = l0_ref[...], l1_ref[...], l2_ref[...]
        mx = jnp.maximum(jnp.maximum(la, lb), lc)
        ea, eb, ec = jnp.exp(la - mx), jnp.exp(lb - mx), jnp.exp(lc - mx)
        den = ea + eb + ec
        y = (ea * o0_ref[...] + eb * o1_ref[...] + ec * o2_ref[...]) / den
        lt_ref[...] = mx + jnp.log(den)
        yb16 = y.astype(BF16)
        y_ref[...] = yb16
        pa = _dot_nt(yb16, wa_ref[...])
        pb = _dot_nt(yb_ref[...], wb_ref[...])
        pa_ref[...] = pa.astype(BF16)
        pb_ref[...] = pb.astype(BF16)
        mg = (_sigmoid(ga_ref[...].astype(F32)) * pa + _sigmoid(gb_ref[...].astype(F32)) * pb).astype(BF16)
        mg_ref[...] = mg
        h2_ref[...] = h1_ref[...] + _dot(mg, wo_ref[...])

    row = lambda w: pl.BlockSpec((TM, w), lambda i: (i, 0))
    full = lambda a: pl.BlockSpec(a.shape, lambda i: (0, 0))
    gate = lambda cb: pl.BlockSpec((TM, D_MODEL), lambda i: (i, cb))
    return pl.pallas_call(
        body, name="merge_fwd", grid=(t // TM,),
        out_shape=(jax.ShapeDtypeStruct((t, D_MODEL), F32), jax.ShapeDtypeStruct((t, GW), BF16),
                   jax.ShapeDtypeStruct((t, GW), F32), jax.ShapeDtypeStruct((t, D_MODEL), BF16),
                   jax.ShapeDtypeStruct((t, D_MODEL), BF16), jax.ShapeDtypeStruct((t, D_MODEL), BF16)),
        in_specs=[row(GW)] * 6 + [row(2 * GW), gate(3), gate(4), row(D_MODEL), full(wat), full(wbt), full(wout)],
        out_specs=(row(D_MODEL), row(GW), row(GW), row(D_MODEL), row(D_MODEL), row(D_MODEL)),
        compiler_params=_params(("parallel",), VMEM_BIG),
    )(o0, o1, o2, l0, l1, l2, yb, z, z, h1, wat, wbt, wout)


def _merge_bwd(dh2, pa, pb, z, y, yb, wat, wbt, wout):
    t = dh2.shape[0]

    def body(dh2_ref, pa_ref, pb_ref, ga_ref, gb_ref, y_ref, yb_ref, wa_ref, wb_ref, wo_ref,
             dpa_ref, dpb_ref, dga_ref, dgb_ref, dya_ref, dyb_ref, dh2b_ref, ca_ref, cb_ref):
        d16 = dh2_ref[...].astype(BF16)
        dh2b_ref[...] = d16
        dm = _dot_nt(d16, wo_ref[...])
        sa = _sigmoid(ga_ref[...].astype(F32))
        sb = _sigmoid(gb_ref[...].astype(F32))
        dpa = (dm * sa).astype(BF16)
        dpb = (dm * sb).astype(BF16)
        dpa_ref[...] = dpa
        dpb_ref[...] = dpb
        dga_ref[...] = (dm * pa_ref[...].astype(F32) * sa * (1.0 - sa)).astype(BF16)
        dgb_ref[...] = (dm * pb_ref[...].astype(F32) * sb * (1.0 - sb)).astype(BF16)
        dya = _dot(dpa, wa_ref[...])
        dyb = _dot(dpb, wb_ref[...])
        dya_ref[...] = dya.astype(BF16)
        dyb_ref[...] = dyb.astype(BF16)
        hp = lax.Precision.HIGHEST
        ca_ref[...] = jnp.dot(dya * y_ref[...].astype(F32), _head_sum_matrix(GW), precision=hp,
                              preferred_element_type=F32)
        cb_ref[...] = jnp.dot(dyb * yb_ref[...].astype(F32), _head_sum_matrix(2 * GW), precision=hp,
                              preferred_element_type=F32)

    row = lambda w: pl.BlockSpec((TM, w), lambda i: (i, 0))
    full = lambda a: pl.BlockSpec(a.shape, lambda i: (0, 0))
    gate = lambda cb: pl.BlockSpec((TM, D_MODEL), lambda i: (i, cb))
    bf = lambda w: jax.ShapeDtypeStruct((t, w), BF16)
    return pl.pallas_call(
        body, name="merge_bwd", grid=(t // TM,),
        out_shape=(bf(D_MODEL), bf(D_MODEL), bf(D_MODEL), bf(D_MODEL), bf(GW), bf(2 * GW), bf(D_MODEL),
                   jax.ShapeDtypeStruct((t, GW), F32), jax.ShapeDtypeStruct((t, 2 * GW), F32)),
        in_specs=[row(D_MODEL), row(D_MODEL), row(D_MODEL), gate(3), gate(4), row(GW), row(2 * GW),
                  full(wat), full(wbt), full(wout)],
        out_specs=(row(D_MODEL), row(D_MODEL), row(D_MODEL), row(D_MODEL), row(GW), row(2 * GW), row(D_MODEL),
                   row(GW), row(2 * GW)),
        compiler_params=_params(("parallel",), VMEM_BIG),
    )(dh2, pa, pb, z, z, y, yb, wat, wbt, wout)


def _loss_head(h3, gain, tgt):
    t = h3.shape[0]

    def body(h_ref, gain_ref, t_ref, dh_ref, loss_ref, gg_ref):
        @pl.when(pl.program_id(0) == 0)
        def _():
            loss_ref[...] = jnp.zeros_like(loss_ref)
            gg_ref[...] = jnp.zeros_like(gg_ref)

        hh = h_ref[...]
        r = lax.rsqrt(jnp.mean(hh * hh, axis=-1, keepdims=True) + EPS)
        hn = hh * r
        err = hn * gain_ref[...] - t_ref[...]
        part = jnp.sum(jnp.sum(err * err, axis=1, keepdims=True), axis=0, keepdims=True)
        loss_ref[...] += (0.5 / D_MODEL) * part
        dy = err * (1.0 / D_MODEL)
        gg_ref[...] += jnp.sum(dy * hn, axis=0, keepdims=True)
        dng = dy * gain_ref[...]
        dh_ref[...] = r * (dng - hn * jnp.mean(dng * hn, axis=-1, keepdims=True))

    row = pl.BlockSpec((TM, D_MODEL), lambda i: (i, 0))
    vec = pl.BlockSpec((1, D_MODEL), lambda i: (0, 0))
    return pl.pallas_call(
        body, name="loss_head", grid=(t // TM,),
        out_shape=(jax.ShapeDtypeStruct((t, D_MODEL), F32), jax.ShapeDtypeStruct((8, 128), F32),
                   jax.ShapeDtypeStruct((1, D_MODEL), F32)),
        in_specs=[row, vec, row], out_specs=(row, pl.BlockSpec((8, 128), lambda i: (0, 0)), vec),
        compiler_params=_params(("arbitrary",)),
    )(h3, gain, tgt)


def _lane_head(rows):
    return lax.broadcasted_iota(jnp.int32, (rows, GW), 1) // HEAD_DIM


def _kv_expand_matrix(r):
    ci = lax.broadcasted_iota(jnp.int32, (2 * HEAD_DIM, GW), 0)
    ji = lax.broadcasted_iota(jnp.int32, (2 * HEAD_DIM, GW), 1)
    return (ci == (ji % HEAD_DIM) + HEAD_DIM * r).astype(BF16)


def _attn_fwd(q_arr, k_arr, v_arr, bias, sink, *, grid, seq, kvw, q_map, k_map, v_map, bias_map, sink_map,
              has_sink, o_shape, o_dtype, name):
    nb = seq // BLOCK

    def body(q_ref, k_ref, v_ref, bias_ref, sink_ref, o_ref, lse_ref):
        lane_head = _lane_head(BLOCK)
        expand = _kv_expand_matrix(pl.program_id(1)) if kvw != GW else None

        def block(i, first):
            r0 = pl.multiple_of(i * BLOCK, BLOCK)
            qi = q_ref[0, pl.ds(r0, BLOCK), :]
            if first:
                kc = k_ref[0, pl.ds(0, BLOCK), :]
                vc = v_ref[0, pl.ds(0, BLOCK), :]
            else:
                k0 = pl.multiple_of(r0 - BLOCK, BLOCK)
                kc = k_ref[0, pl.ds(k0, 2 * BLOCK), :]
                vc = v_ref[0, pl.ds(k0, 2 * BLOCK), :]
            if expand is not None:
                kc = _dot(kc, expand).astype(BF16)
                vc = _dot(vc, expand).astype(BF16)
            acc_o = jnp.zeros((BLOCK, GW), F32)
            acc_l = jnp.zeros((BLOCK, GW), F32)
            for h in range(4):
                qh = jnp.where(lane_head == h, qi, jnp.zeros_like(qi))
                bias_h = bias_ref[h, :, BLOCK:] if first else bias_ref[h]
                s = _dot_nt(qh, kc) * (HEAD_DIM ** -0.5) + bias_h
                m = jnp.max(s, axis=-1, keepdims=True)
                if has_sink:
                    sk = sink_ref[0, h:h + 1, 0:1]
                    m = jnp.maximum(m, sk)
                p = jnp.exp(s - m)
                l = jnp.sum(p, axis=-1, keepdims=True)
                if has_sink:
                    l = l + jnp.exp(sk - m)
                oh = _dot(p.astype(BF16), vc) / l
                acc_o = jnp.where(lane_head == h, oh, acc_o)
                acc_l = jnp.where(lane_head == h, m + jnp.log(l), acc_l)
            o_ref[0, pl.ds(r0, BLOCK), :] = acc_o.astype(o_dtype)
            lse_ref[0, pl.ds(r0, BLOCK), :] = acc_l

        block(0, True)
        if nb > 1:
            def step(i, carry):
                block(i, False)
                return carry
            lax.fori_loop(1, nb, step, 0)

    blk = lambda w, m: pl.BlockSpec((1, seq, w), m)
    o_map = lambda n, r: (n, 0, r)
    return pl.pallas_call(
        body, name=name, grid=grid,
        out_shape=(jax.ShapeDtypeStruct(o_shape, o_dtype), jax.ShapeDtypeStruct(o_shape, F32)),
        in_specs=[blk(GW, q_map), blk(kvw, k_map), blk(kvw, v_map),
                  pl.BlockSpec((4, BLOCK, 2 * BLOCK), bias_map), pl.BlockSpec((1, 4, 128), sink_map)],
        out_specs=(blk(GW, o_map), blk(GW, o_map)),
        compiler_params=_params(("parallel", "arbitrary"), VMEM_BIG),
    )(q_arr, k_arr, v_arr, bias, sink)


def _attn_bwd(q_arr, k_arr, v_arr, bias, sink, dy, cc, lse, *, grid, seq, kvw, q_map, k_map, v_map, bias_map,
              sink_map, has_sink, n_bias, dq_shape, dkv_shape, name):
    nb = seq // BLOCK
    scale = HEAD_DIM ** -0.5

    def body(q_ref, k_ref, v_ref, bias_ref, sink_ref, dy_ref, c_ref, lse_ref,
             dq_ref, dk_ref, dv_ref, db_ref, dsk_ref, dk_acc, dv_acc, dk_half, dv_half):
        rr = pl.program_id(1)

        @pl.when((pl.program_id(0) == 0) & (rr == 0))
        def _():
            db_ref[...] = jnp.zeros_like(db_ref)
            dsk_ref[...] = jnp.zeros_like(dsk_ref)

        dk_acc[...] = jnp.zeros_like(dk_acc)
        dv_acc[...] = jnp.zeros_like(dv_acc)
        lane_head = _lane_head(BLOCK)
        expand = _kv_expand_matrix(rr) if kvw != GW else None
        hb = 4 * rr if n_bias == 8 else 0

        def block(i, first):
            r0 = pl.multiple_of(i * BLOCK, BLOCK)
            rows = pl.ds(r0, BLOCK)
            qi = q_ref[0, rows, :]
            dyi = dy_ref[0, rows, :]
            ci = c_ref[0, rows, :]
            li = lse_ref[0, rows, :]
            if first:
                krows = pl.ds(0, BLOCK)
            else:
                krows = pl.ds(pl.multiple_of(r0 - BLOCK, BLOCK), 2 * BLOCK)
            kc = k_ref[0, krows, :]
            vc = v_ref[0, krows, :]
            if expand is not None:
                kc = _dot(kc, expand).astype(BF16)
                vc = _dot(vc, expand).astype(BF16)
            nk = BLOCK if first else 2 * BLOCK
            dq = jnp.zeros((BLOCK, GW), F32)
            dkc = jnp.zeros((nk, GW), F32)
            dvc = jnp.zeros((nk, GW), F32)
            for h in range(4):
                sel = lane_head == h
                qh = jnp.where(sel, qi, jnp.zeros_like(qi))
                dyh = jnp.where(sel, dyi, jnp.zeros_like(dyi))
                bias_h = bias_ref[h, :, BLOCK:] if first else bias_ref[h]
                s = _dot_nt(qh, kc) * scale + bias_h
                lh = li[:, h * HEAD_DIM:h * HEAD_DIM + 1]
                ch = ci[:, h * HEAD_DIM:h * HEAD_DIM + 1]
                p = jnp.exp(s - lh)
                p16 = p.astype(BF16)
                dvc = dvc + _dot_tn(p16, dyh)
                dp = _dot_nt(dyh, vc)
                ds = p * (dp - ch)
                if first:
                    db_ref[hb + h, :, BLOCK:] += ds
                else:
                    db_ref[hb + h] += ds
                ds16 = ds.astype(BF16)
                dq = dq + jnp.where(sel, _dot(ds16, kc), 0.0)
                dkc = dkc + _dot_tn(ds16, qh)
                if has_sink:
                    sk = sink_ref[0, h:h + 1, 0:1]
                    val = -jnp.sum(jnp.exp(sk - lh) * ch, axis=0, keepdims=True)
                    dsk_ref[hb + h] += jnp.broadcast_to(val, (8, 128))
            dq_ref[0, rows, :] = (dq * scale).astype(BF16)
            dk_acc[krows, :] += dkc * scale
            dv_acc[krows, :] += dvc

        block(0, True)
        if nb > 1:
            def step(i, carry):
                block(i, False)
                return carry
            lax.fori_loop(1, nb, step, 0)

        if kvw == GW:
            dk_ref[0] = dk_acc[...].astype(BF16)
            dv_ref[0] = dv_acc[...].astype(BF16)
        else:
            def fold(acc):
                t2 = acc[:, :2 * HEAD_DIM] + acc[:, 2 * HEAD_DIM:]
                t2 = t2 + pltpu.roll(t2, HEAD_DIM, 1)
                lane = lax.broadcasted_iota(jnp.int32, t2.shape, 1) // HEAD_DIM
                return jnp.where(lane == rr, t2, 0.0)

            @pl.when(rr == 0)
            def _():
                dk_half[...] = fold(dk_acc[...])
                dv_half[...] = fold(dv_acc[...])

            @pl.when(rr == 1)
            def _():
                dk_ref[0] = (dk_half[...] + fold(dk_acc[...])).astype(BF16)
                dv_ref[0] = (dv_half[...] + fold(dv_acc[...])).astype(BF16)

    blk = lambda w, m: pl.BlockSpec((1, seq, w), m)
    o_map = lambda n, r: (n, 0, r)
    kv_out_map = o_map if kvw == GW else (lambda n, r: (n, 0, 0))
    return pl.pallas_call(
        body, name=name, grid=grid,
        out_shape=(jax.ShapeDtypeStruct(dq_shape, BF16), jax.ShapeDtypeStruct(dkv_shape, BF16),
                   jax.ShapeDtypeStruct(dkv_shape, BF16), jax.ShapeDtypeStruct((n_bias, BLOCK, 2 * BLOCK), F32),
                   jax.ShapeDtypeStruct((8, 8, 128), F32)),
        in_specs=[blk(GW, q_map), blk(kvw, k_map), blk(kvw, v_map),
                  pl.BlockSpec((4, BLOCK, 2 * BLOCK), bias_map), pl.BlockSpec((1, 4, 128), sink_map),
                  blk(GW, o_map), blk(GW, o_map), blk(GW, o_map)],
        out_specs=(blk(GW, o_map), blk(kvw, kv_out_map), blk(kvw, kv_out_map),
                   pl.BlockSpec((n_bias, BLOCK, 2 * BLOCK), lambda n, r: (0, 0, 0)),
                   pl.BlockSpec((8, 8, 128), lambda n, r: (0, 0, 0))),
        scratch_shapes=[pltpu.VMEM((seq, GW), F32), pltpu.VMEM((seq, GW), F32),
                        pltpu.VMEM((seq, 2 * HEAD_DIM), F32), pltpu.VMEM((seq, 2 * HEAD_DIM), F32)],
        compiler_params=_params(("arbitrary", "arbitrary"), VMEM_BIG),
    )(q_arr, k_arr, v_arr, bias, sink, dy, cc, lse)


def _bias_grad(ds_all, buckets):
    def body(ds_ref, bk_ref, o_ref):
        rows = lax.broadcasted_iota(jnp.int32, (N_BUCKETS, 128), 0)
        cols = lax.broadcasted_iota(jnp.int32, (N_BUCKETS, 128), 1)

        def per_bucket(b, acc):
            for h in range(20):
                gi = h // 4 if h < 12 else 3
                v = jnp.where(bk_ref[gi] == b, ds_ref[h], 0.0)
                v = jnp.sum(jnp.sum(v, axis=1, keepdims=True), axis=0, keepdims=True)
                acc = jnp.where((rows == b) & (cols == h), v, acc)
            return acc

        o_ref[...] = lax.fori_loop(0, N_BUCKETS, per_bucket, jnp.zeros((N_BUCKETS, 128), F32))

    vm = pl.BlockSpec(memory_space=pltpu.VMEM)
    return pl.pallas_call(body, name="bias_grad", out_shape=jax.ShapeDtypeStruct((N_BUCKETS, 128), F32),
                          in_specs=[vm, vm], out_specs=vm)(ds_all, buckets)


def _adamw(w, g, m, v, name):
    r, c = w.shape
    tr = r
    for cand in (256, 176, 128, 64, 32, 16, 8):
        if r % cand == 0:
            tr = cand
            break
    bc1 = 1.0 - ADAM_B1 ** ADAM_STEP
    bc2 = 1.0 - ADAM_B2 ** ADAM_STEP

    def body(w_ref, g_ref, m_ref, v_ref, d_ref, nm_ref, nv_ref):
        gv = g_ref[...]
        nm = ADAM_B1 * m_ref[...] + (1.0 - ADAM_B1) * gv
        nv = ADAM_B2 * v_ref[...] + (1.0 - ADAM_B2) * (gv * gv)
        nm_ref[...] = nm
        nv_ref[...] = nv
        d_ref[...] = -ADAM_LR * ((nm / bc1) / (jnp.sqrt(nv / bc2) + ADAM_EPS) + ADAM_WD * w_ref[...])

    spec = pl.BlockSpec((tr, c), lambda i: (i, 0))
    shp = jax.ShapeDtypeStruct((r, c), F32)
    return pl.pallas_call(body, name=name, grid=(r // tr,), out_shape=(shp, shp, shp),
                          in_specs=[spec] * 4, out_specs=(spec, spec, spec),
                          compiler_params=_params(("parallel",)))(w, g, m, v)


def _t5_bucket(dist):
    max_exact = N_BUCKETS // 2
    n = jnp.maximum(dist, 0)
    nf = jnp.maximum(n, 1).astype(F32)
    large = max_exact + (jnp.log(nf / max_exact) / math.log(MAX_DISTANCE / max_exact)
                         * (N_BUCKETS - max_exact)).astype(jnp.int32)
    large = jnp.minimum(large, N_BUCKETS - 1)
    return jnp.where(n < max_exact, n, large)


def _bias_tables(rel_bias):
    qi = jnp.arange(BLOCK)[:, None]
    ki = jnp.arange(2 * BLOCK)[None, :]
    dist = qi + BLOCK - ki
    specs = [(d, w // d, 4 * gi, 4 * gi + 4) for gi, (w, d) in enumerate(DIL_GROUPS)] + [(1, B_WINDOW - 1, 12, 20)]
    biases, buckets = [], []
    for stride, steps, h0, h1 in specs:
        valid = (dist >= 0) & (dist <= steps)
        bk = jnp.where(valid, _t5_bucket(dist * stride), -1).astype(jnp.int32)
        onehot = (bk[None, :, :] == jnp.arange(N_BUCKETS, dtype=jnp.int32)[:, None, None]).astype(F32)
        b = jnp.einsum("bqk,bh->hqk", onehot, rel_bias[:, h0:h1], precision=lax.Precision.HIGHEST)
        biases.append(jnp.where(valid[None], b, NEG))
        buckets.append(bk)
    return jnp.concatenate(biases, axis=0), jnp.stack(buckets, axis=0)


def _local_step(x, tgt, W, S):
    nseq, seq, _ = x.shape
    t = nseq * seq
    xf = x.reshape(t, D_MODEL)
    bias_all, buckets = _bias_tables(S["rel_bias"])
    sink_b = jnp.broadcast_to(S["sinks"].reshape(2, 4, 1), (2, 4, 128)).astype(F32)
    sink_0 = jnp.zeros((1, 4, 128), F32)

    h1, n1, g1, u1 = _ffn_fwd(xf, S["ffn1_norm"], W["wgt1"], W["wut1"], W["wd1"])
    un, z = _inproj_fwd(h1, S["mix_norm"], W["wint"], S["b_in"])

    a_cfg = []
    outs, lses = [], []
    for gi, (_, d) in enumerate(DIL_GROUPS):
        ln = seq // d
        cpr = D_IN // GW
        cfg = dict(grid=(nseq, d), seq=ln, kvw=GW,
                   q_map=lambda n, r, gi=gi, cpr=cpr: (n, 0, r * cpr + gi),
                   k_map=lambda n, r, gi=gi, cpr=cpr: (n, 0, r * cpr + 3 + gi),
                   v_map=lambda n, r, gi=gi, cpr=cpr: (n, 0, r * cpr + 6 + gi),
                   bias_map=lambda n, r: (0, 0, 0), sink_map=lambda n, r: (0, 0, 0), has_sink=False)
        a_cfg.append(cfg)
        zr = z.reshape(nseq, ln, d * D_IN)
        o, lse = _attn_fwd(zr, zr, zr, bias_all[4 * gi:4 * gi + 4], sink_0, o_shape=(nseq, ln, d * GW),
                           o_dtype=F32, name=f"attn_a{gi}_fwd", **cfg)
        outs.append(o.reshape(t, GW))
        lses.append(lse.reshape(t, GW))
    z3 = z.reshape(nseq, seq, D_IN)
    b_cfg = dict(grid=(nseq, 2), seq=seq, kvw=2 * HEAD_DIM,
                 q_map=lambda n, r: (n, 0, 9 + r), k_map=lambda n, r: (n, 0, 22), v_map=lambda n, r: (n, 0, 23),
                 bias_map=lambda n, r: (r, 0, 0), sink_map=lambda n, r: (r, 0, 0), has_sink=True)
    yb, lse_b = _attn_fwd(z3, z3, z3, bias_all[12:20], sink_b, o_shape=(nseq, seq, 2 * GW), o_dtype=BF16,
                          name="attn_b_fwd", **b_cfg)
    yb = yb.reshape(t, 2 * GW)

    h2, y, lse_tot, pa, pb, merged = _merge_fwd(outs[0], outs[1], outs[2], lses[0], lses[1], lses[2], yb, z, h1,
                                                W["wat"], W["wbt"], W["wout"])
    h3, n2, g2, u2 = _ffn_fwd(h2, S["ffn2_norm"], W["wgt2"], W["wut2"], W["wd2"])
    dh3, loss_part, g_final = _loss_head(h3, S["final_norm"].reshape(1, D_MODEL), tgt.reshape(t, D_MODEL))

    G, GS = {}, {}
    GS["final_norm"] = g_final
    dh2, dg2, du2, a2, df2, GS["ffn2_norm"] = _ffn_bwd(dh3, h2, S["ffn2_norm"], g2, u2, W["wgt2"], W["wut2"], W["wd2"])
    G["wgt2"] = _wgrad(dg2, n2, D_FF, name="wgrad_gate2")
    G["wut2"] = _wgrad(du2, n2, D_FF, name="wgrad_up2")
    G["wd2"] = _wgrad(a2, df2, D_FF, name="wgrad_down2")

    dpa, dpb, dga, dgb, dya, dyb, dh2b, ca, cb = _merge_bwd(dh2, pa, pb, z, y, yb, W["wat"], W["wbt"], W["wout"])
    G["wout"] = _wgrad(merged, dh2b, D_MODEL, name="wgrad_out")
    G["wat"] = _wgrad(dpa, y, D_MODEL, name="wgrad_branch_a")
    G["wbt"] = _wgrad(dpb, yb, D_MODEL, name="wgrad_branch_b")

    dqs, dks, dvs, dbs = [], [], [], []
    for gi, (_, d) in enumerate(DIL_GROUPS):
        ln = seq // d
        zr = z.reshape(nseq, ln, d * D_IN)
        fold = lambda a, ln=ln, d=d: a.reshape(nseq, ln, d * GW)
        shp = (nseq, ln, d * GW)
        dq, dk, dv, db, _ = _attn_bwd(zr, zr, zr, bias_all[4 * gi:4 * gi + 4], sink_0, fold(dya), fold(ca),
                                      fold(lse_tot), n_bias=4, dq_shape=shp, dkv_shape=shp,
                                      name=f"attn_a{gi}_bwd", **a_cfg[gi])
        dqs.append(dq.reshape(t, GW))
        dks.append(dk.reshape(t, GW))
        dvs.append(dv.reshape(t, GW))
        dbs.append(db)
    r3 = lambda a: a.reshape(nseq, seq, a.shape[-1])
    dqb, dkb, dvb, dbb, dsink = _attn_bwd(z3, z3, z3, bias_all[12:20], sink_b, r3(dyb), r3(cb), lse_b, n_bias=8,
                                          dq_shape=(nseq, seq, 2 * GW), dkv_shape=(nseq, seq, 2 * HEAD_DIM),
                                          name="attn_b_bwd", **b_cfg)
    dz = jnp.concatenate(dqs + dks + dvs + [dqb.reshape(t, 2 * GW), dkb.reshape(t, 2 * HEAD_DIM),
                                            dvb.reshape(t, 2 * HEAD_DIM), dga, dgb], axis=-1)
    gb_tab = _bias_grad(jnp.concatenate(dbs + [dbb], axis=0), buckets)
    GS["rel_bias"] = gb_tab[:, :20]
    GS["sinks"] = dsink[:, 0, 0].reshape(1, 8)

    G["wint"], GS["b_in"] = _wgrad(dz, un, D_IN // 2, with_colsum=True, name="wgrad_in")
    dh1, GS["mix_norm"] = _inproj_bwd(dz, dh2, h1, S["mix_norm"], W["wint"])

    dx, dg1, du1, a1, df1, GS["ffn1_norm"] = _ffn_bwd(dh1, xf, S["ffn1_norm"], g1, u1, W["wgt1"], W["wut1"], W["wd1"])
    G["wgt1"] = _wgrad(dg1, n1, D_FF, name="wgrad_gate1")
    G["wut1"] = _wgrad(du1, n1, D_FF, name="wgrad_up1")
    G["wd1"] = _wgrad(a1, df1, D_FF, name="wgrad_down1")
    return loss_part, dx.reshape(x.shape), G, GS


_SMALL = ("ffn1_norm", "mix_norm", "ffn2_norm", "final_norm", "b_in", "sinks", "rel_bias")
_ORDER = ("ffn1_norm", "ffn1_w_gate", "ffn1_w_up", "ffn1_w_down", "mix_norm", "w_in", "b_in", "w_branch_a",
          "w_branch_b", "w_out", "sinks", "rel_bias", "ffn2_norm", "ffn2_w_gate", "ffn2_w_up", "ffn2_w_down",
          "final_norm")
_BIG = (("wgt1", "ffn1_w_gate", True, 704), ("wut1", "ffn1_w_up", True, 704), ("wd1", "ffn1_w_down", False, 704),
        ("wint", "w_in", True, 1280), ("wout", "w_out", False, 256), ("wat", "w_branch_a", True, 64),
        ("wbt", "w_branch_b", True, 128), ("wgt2", "ffn2_w_gate", True, 704), ("wut2", "ffn2_w_up", True, 704),
        ("wd2", "ffn2_w_down", False, 704))
_FULL_SHAPE = {"wat": (D_MODEL, GW), "wbt": (D_MODEL, 2 * GW)}
_REDUCE_GROUPS = ((0, 3), (3, 7), (7, 10))


def _pack_small(p, extra=None):
    last = [p["sinks"].reshape(8), p["rel_bias"].reshape(640)]
    used = 648
    if extra is not None:
        last.append(extra.reshape(1))
        used += 1
    last.append(jnp.zeros((D_MODEL - used,), F32))
    rows = [p["ffn1_norm"].reshape(1, D_MODEL), p["mix_norm"].reshape(1, D_MODEL), p["ffn2_norm"].reshape(1, D_MODEL),
            p["final_norm"].reshape(1, D_MODEL), p["b_in"].reshape(5, D_MODEL), jnp.concatenate(last).reshape(1, D_MODEL),
            jnp.zeros((6, D_MODEL), F32)]
    return jnp.concatenate(rows, axis=0)


def _unpack_small(a):
    return {"ffn1_norm": a[0:1], "mix_norm": a[1:2], "ffn2_norm": a[2:3], "final_norm": a[3],
            "b_in": a[4:9].reshape(1, D_IN), "sinks": a[9, 0:8].reshape(1, 8), "rel_bias": a[9, 8:648].reshape(32, 20)}


def kernel(x, ffn1_norm, ffn1_w_gate, ffn1_w_up, ffn1_w_down, mix_norm, w_in, b_in, w_branch_a, w_branch_b, w_out, sinks, rel_bias, ffn2_norm, ffn2_w_gate, ffn2_w_up, ffn2_w_down, final_norm, loss_target, m_ffn1_norm, m_ffn1_w_gate, m_ffn1_w_up, m_ffn1_w_down, m_mix_norm, m_w_in, m_b_in, m_w_branch_a, m_w_branch_b, m_w_out, m_sinks, m_rel_bias, m_ffn2_norm, m_ffn2_w_gate, m_ffn2_w_up, m_ffn2_w_down, m_final_norm, v_ffn1_norm, v_ffn1_w_gate, v_ffn1_w_up, v_ffn1_w_down, v_mix_norm, v_w_in, v_b_in, v_w_branch_a, v_w_branch_b, v_w_out, v_sinks, v_rel_bias, v_ffn2_norm, v_ffn2_w_gate, v_ffn2_w_up, v_ffn2_w_down, v_final_norm):
    args = dict(locals())
    w = {n: args[n] for n in _ORDER}
    m = {n: args["m_" + n] for n in _ORDER}
    v = {n: args["v_" + n] for n in _ORDER}

    shards = []
    for key, name, transposed, rows in _BIG:
        a = w[name][0]
        a = (a.T if transposed else a).astype(BF16)
        shards.append(a.reshape(rows, D_MODEL))
    gathered = _gather_rows(shards)
    W = {}
    for (key, name, transposed, rows), g in zip(_BIG, gathered):
        W[key] = g.reshape(_FULL_SHAPE.get(key, (N_CHIPS * rows, D_MODEL)))
    S = {n: w[n] for n in _SMALL}

    loss_part, grad_x, G, GS = _local_step(x, loss_target, W, S)

    parts = [G[key].reshape(N_CHIPS, 2, rows // 2, D_MODEL) for key, _, _, rows in _BIG]
    pair = [_pair_reduce(parts[a:b], f"grad_pair_reduce_{a}") for a, b in _REDUCE_GROUPS]
    own, rec = _chip_exchange(pair)
    full = _final_reduce(own, rec)
    grads, off = {}, 0
    for key, name, transposed, rows in _BIG:
        g = full[:, off:off + rows // 2].reshape(rows, D_MODEL)
        off += rows // 2
        nat = w[name][0].shape
        grads[name] = g.reshape(nat[1], nat[0]).T if transposed else g.reshape(nat)

    small = _allreduce_small(_pack_small(GS, extra=loss_part[0, 0]))
    loss = small[9, 648]

    out_g, out_d, out_m, out_v = {}, {}, {}, {}
    for _, n, _, _ in _BIG:
        d_, nm_, nv_ = _adamw(w[n][0], grads[n], m[n][0], v[n][0], "adamw_" + n)
        out_g[n], out_d[n], out_m[n], out_v[n] = grads[n][None], d_[None], nm_[None], nv_[None]
    d_s, m_s, v_s = _adamw(_pack_small(w), small, _pack_small(m), _pack_small(v), "adamw_small")
    for dst, src in ((out_g, small), (out_d, d_s), (out_m, m_s), (out_v, v_s)):
        dst.update(_unpack_small(src))

    return (loss, grad_x, *[out_g[n] for n in _ORDER], *[out_d[n] for n in _ORDER],
            *[out_m[n] for n in _ORDER], *[out_v[n] for n in _ORDER])
```

```python
import math

import jax
import jax.numpy as jnp
from jax import lax
from jax.experimental import pallas as pl
from jax.experimental.pallas import tpu as pltpu

F32, BF16 = jnp.float32, jnp.bfloat16
MESH = pl.DeviceIdType.MESH

D_MODEL = 1024
D_FF = 2816
D_IN = 5120
HEAD_DIM = 64
BLOCK = 128
DIL_GROUPS = ((128, 1), (512, 4), (2048, 16))
B_WINDOW = 128
N_BUCKETS = 32
MAX_DISTANCE = 2048
EPS = 1e-6
N_CHIPS = 4
GW = 256
NEG = -1e30

ADAM_LR, ADAM_B1, ADAM_B2, ADAM_EPS, ADAM_WD, ADAM_STEP = 0.001, 0.9, 0.999, 1e-08, 0.01, 10

VMEM_BIG = 56 * 1024 * 1024
TM = 512
TM_BWD = 256
FF_CHUNKS = 2
DMA_SPLIT = 8


def _dot(a, b):
    return jnp.dot(a, b, preferred_element_type=F32)


def _dot_nt(a, b):
    return lax.dot_general(a, b, (((1,), (1,)), ((), ())), preferred_element_type=F32)


def _dot_tn(a, b):
    return lax.dot_general(a, b, (((0,), (0,)), ((), ())), preferred_element_type=F32)


def _sigmoid(x):
    return 1.0 / (1.0 + jnp.exp(-x))


def _params(sem, vmem=None):
    return pltpu.CompilerParams(dimension_semantics=sem, vmem_limit_bytes=vmem)


ANY = pl.BlockSpec(memory_space=pl.ANY)


def _me():
    return lax.axis_index("x"), lax.axis_index("y"), lax.axis_index("c")


_CHIP_RELS = ((1, 0), (0, 1), (1, 1))


def _flip(v, f):
    return 1 - v if f else v


def _remote(src, dst, ssem, rsem, peer):
    return pltpu.make_async_remote_copy(src_ref=src, dst_ref=dst, send_sem=ssem, recv_sem=rsem,
                                        device_id=peer, device_id_type=MESH)


def _row_pieces(rows, n):
    step = max(16, -(-rows // n) // 16 * 16)
    out, s = [], 0
    while s < rows:
        out.append((s, min(step, rows - s)))
        s += step
    return out


def _gather_rows(shards):
    nt = len(shards)
    rows = [s.shape[0] for s in shards]

    def body(*refs):
        srcs, outs = refs[:nt], refs[nt:2 * nt]
        ici_s, ici_r, d2d_s, d2d_r, loc = refs[2 * nt:]
        x, y, c = _me()
        j = 2 * x + y
        sib = (x, y, 1 - c)
        local = [pltpu.make_async_copy(srcs[t], outs[t].at[j], loc.at[t]) for t in range(nt)]
        for cp in local:
            cp.start()
        sends = []
        for k, (fx, fy) in enumerate(_CHIP_RELS):
            peer = (_flip(x, fx), _flip(y, fy), c)
            for t in range(nt):
                half = pl.ds(c * (rows[t] // 2), rows[t] // 2)
                cp = _remote(srcs[t].at[half], outs[t].at[j, half], ici_s.at[3 * t + k], ici_r.at[3 * t + k], peer)
                cp.start()
                sends.append(cp)
        fwds = []
        for k, (fx, fy) in enumerate(_CHIP_RELS):
            pj = 2 * _flip(x, fx) + _flip(y, fy)
            for t in range(nt):
                half = pl.ds(c * (rows[t] // 2), rows[t] // 2)
                blk = outs[t].at[pj, half]
                _remote(blk, blk, ici_s.at[3 * t + k], ici_r.at[3 * t + k], sib).wait_recv()
                cp = _remote(blk, blk, d2d_s.at[3 * t + k], d2d_r.at[3 * t + k], sib)
                cp.start()
                fwds.append(cp)
        for cp in fwds:
            cp.wait()
        for cp in sends:
            cp.wait_send()
        for cp in local:
            cp.wait()

    sems = [pltpu.SemaphoreType.DMA((3 * nt,)) for _ in range(4)] + [pltpu.SemaphoreType.DMA((nt,))]
    return pl.pallas_call(
        body, name="gather_weights",
        out_shape=tuple(jax.ShapeDtypeStruct((N_CHIPS,) + s.shape, s.dtype) for s in shards),
        in_specs=[ANY] * nt, out_specs=tuple([ANY] * nt), scratch_shapes=sems,
    )(*shards)


VMEM_WHOLE = pl.BlockSpec(memory_space=pltpu.VMEM)


def _pair_reduce(grads, name):
    nt = len(grads)
    r2 = [g.shape[2] for g in grads]
    off = [sum(r2[:t]) for t in range(nt)]
    tot = sum(r2)

    def body(*refs):
        gs = refs[:nt]
        s_ref, got, ssem, rsem = refs[nt:]
        x, y, c = _me()
        sib = (x, y, 1 - c)
        for t in range(nt):
            for k in range(N_CHIPS):
                _remote(gs[t].at[k, 1 - c], got.at[k, pl.ds(off[t], r2[t])], ssem, rsem, sib).start()
        _remote(got, got, ssem, rsem, sib).wait()
        for t in range(nt):
            for k in range(N_CHIPS):
                rows = slice(off[t], off[t] + r2[t])
                s_ref[k, rows, :] = (gs[t][k, c].astype(F32) + got[k, rows, :].astype(F32)).astype(BF16)

    shp = jax.ShapeDtypeStruct((N_CHIPS, tot, D_MODEL), BF16)
    return pl.pallas_call(
        body, name=name, out_shape=shp, in_specs=[VMEM_WHOLE] * nt, out_specs=VMEM_WHOLE,
        scratch_shapes=[pltpu.VMEM((N_CHIPS, tot, D_MODEL), BF16), pltpu.SemaphoreType.DMA(()),
                        pltpu.SemaphoreType.DMA(())],
        compiler_params=pltpu.CompilerParams(vmem_limit_bytes=VMEM_BIG),
    )(*grads)


def _chip_exchange(parts):
    ng = len(parts)
    r2 = [p.shape[1] for p in parts]
    off = [sum(r2[:g]) for g in range(ng)]
    tot = sum(r2)

    def body(*refs):
        ps = refs[:ng]
        own_ref, rec_ref, ssems, rsems, lsem = refs[ng:]
        x, y, c = _me()
        j = 2 * x + y
        for g in range(ng):
            pltpu.make_async_copy(ps[g].at[j], own_ref.at[pl.ds(off[g], r2[g])], lsem).start()
        for k, (fx, fy) in enumerate(_CHIP_RELS):
            px, py = _flip(x, fx), _flip(y, fy)
            for g in range(ng):
                for st, sz in _row_pieces(r2[g], 2):
                    _remote(ps[g].at[2 * px + py, pl.ds(st, sz)], rec_ref.at[k, pl.ds(off[g] + st, sz)],
                            ssems.at[k], rsems.at[k], (px, py, c)).start()
        for k in range(3):
            _remote(rec_ref.at[k], rec_ref.at[k], ssems.at[k], rsems.at[k], (x, y, c)).wait()
        pltpu.make_async_copy(own_ref, own_ref, lsem).wait()

    return pl.pallas_call(
        body, name="grad_chip_exchange",
        out_shape=(jax.ShapeDtypeStruct((tot, D_MODEL), BF16), jax.ShapeDtypeStruct((3, tot, D_MODEL), BF16)),
        in_specs=[ANY] * ng, out_specs=(ANY, ANY),
        scratch_shapes=[pltpu.SemaphoreType.DMA((3,)), pltpu.SemaphoreType.DMA((3,)), pltpu.SemaphoreType.DMA(())],
    )(*parts)


def _final_reduce(own, rec, name):
    r2 = own.shape[0]
    pieces = _row_pieces(r2, DMA_SPLIT)

    def body(own_ref, rec_ref, o_ref, fbuf, ssem, rsem, lsem):
        x, y, c = _me()
        sib = (x, y, 1 - c)
        for st, sz in pieces:
            rows = slice(st, st + sz)
            fbuf[rows, :] = (own_ref[rows, :].astype(F32) + rec_ref[0, rows, :].astype(F32)
                             + rec_ref[1, rows, :].astype(F32) + rec_ref[2, rows, :].astype(F32))
            pltpu.make_async_copy(fbuf.at[pl.ds(st, sz)], o_ref.at[c, pl.ds(st, sz)], lsem).start()
            _remote(fbuf.at[pl.ds(st, sz)], o_ref.at[c, pl.ds(st, sz)], ssem, rsem, sib).start()
        _remote(fbuf, o_ref.at[c], ssem, rsem, sib).wait()
        pltpu.make_async_copy(fbuf, o_ref.at[c], lsem).wait()

    return pl.pallas_call(
        body, name=name, out_shape=jax.ShapeDtypeStruct((2, r2, D_MODEL), F32),
        in_specs=[VMEM_WHOLE, VMEM_WHOLE], out_specs=ANY,
        scratch_shapes=[pltpu.VMEM((r2, D_MODEL), F32), pltpu.SemaphoreType.DMA(()), pltpu.SemaphoreType.DMA(()),
                        pltpu.SemaphoreType.DMA(())],
        compiler_params=pltpu.CompilerParams(vmem_limit_bytes=VMEM_BIG),
    )(own, rec)


def _allreduce_small(vec):
    def body(v_ref, o_ref, buf, send_sems, recv_sems):
        x, y, c = _me()
        me = 4 * x + 2 * y + c
        buf[me] = v_ref[...]
        copies = []
        for k in range(1, 8):
            peer = (_flip(x, (k >> 2) & 1), _flip(y, (k >> 1) & 1), _flip(c, k & 1))
            cp = _remote(v_ref, buf.at[me], send_sems.at[k - 1], recv_sems.at[k - 1], peer)
            cp.start()
            copies.append(cp)
        for cp in copies:
            cp.wait()
        acc = buf[0]
        for i in range(1, 8):
            acc = acc + buf[i]
        o_ref[...] = acc

    vm = pl.BlockSpec(memory_space=pltpu.VMEM)
    return pl.pallas_call(
        body, name="allreduce_small", out_shape=jax.ShapeDtypeStruct(vec.shape, vec.dtype),
        in_specs=[vm], out_specs=vm,
        scratch_shapes=[pltpu.VMEM((8,) + vec.shape, vec.dtype), pltpu.SemaphoreType.DMA((7,)),
                        pltpu.SemaphoreType.DMA((7,))],
    )(vec)


class _GatherRider:
    def __init__(self, shards):
        self.inputs = list(shards)
        nt = len(shards)
        self.out_shape = [jax.ShapeDtypeStruct((N_CHIPS,) + s.shape, s.dtype) for s in shards]
        self.scratch = [pltpu.SemaphoreType.DMA((3 * nt,)), pltpu.SemaphoreType.DMA((3 * nt,)),
                        pltpu.SemaphoreType.DMA((nt,))]

    def _copies(self, srcs, outs, sems):
        ici_s, ici_r, loc = sems
        x, y, c = _me()
        j = 2 * x + y
        local = [pltpu.make_async_copy(srcs[t], outs[t].at[j], loc.at[t]) for t in range(len(srcs))]
        remote = []
        for k, (fx, fy) in enumerate(_CHIP_RELS):
            peer = (_flip(x, fx), _flip(y, fy), c)
            for t in range(len(srcs)):
                remote.append(_remote(srcs[t], outs[t].at[j], ici_s.at[3 * t + k], ici_r.at[3 * t + k], peer))
        return local, remote

    def start(self, srcs, outs, sems):
        local, remote = self._copies(srcs, outs, sems)
        for cp in local + remote:
            cp.start()

    def finish(self, srcs, outs, sems):
        local, remote = self._copies(srcs, outs, sems)
        for cp in remote + local:
            cp.wait()


class _ExchangeRider:
    def __init__(self, parts):
        self.inputs = list(parts)
        self.r2 = [p.shape[1] for p in parts]
        self.off = [sum(self.r2[:g]) for g in range(len(parts))]
        tot = sum(self.r2)
        self.out_shape = [jax.ShapeDtypeStruct((tot, D_MODEL), BF16), jax.ShapeDtypeStruct((3, tot, D_MODEL), BF16)]
        self.scratch = [pltpu.SemaphoreType.DMA((3,)), pltpu.SemaphoreType.DMA((3,)), pltpu.SemaphoreType.DMA(())]

    def start(self, ps, outs, sems):
        own_ref, rec_ref = outs
        ssems, rsems, lsem = sems
        x, y, c = _me()
        j = 2 * x + y
        for g in range(len(ps)):
            pltpu.make_async_copy(ps[g].at[j], own_ref.at[pl.ds(self.off[g], self.r2[g])], lsem).start()
        for k, (fx, fy) in enumerate(_CHIP_RELS):
            px, py = _flip(x, fx), _flip(y, fy)
            for g in range(len(ps)):
                for st, sz in _row_pieces(self.r2[g], 2):
                    _remote(ps[g].at[2 * px + py, pl.ds(st, sz)], rec_ref.at[k, pl.ds(self.off[g] + st, sz)],
                            ssems.at[k], rsems.at[k], (px, py, c)).start()

    def finish(self, ps, outs, sems):
        own_ref, rec_ref = outs
        ssems, rsems, lsem = sems
        x, y, c = _me()
        for k in range(3):
            _remote(rec_ref.at[k], rec_ref.at[k], ssems.at[k], rsems.at[k], (x, y, c)).wait()
        pltpu.make_async_copy(own_ref, own_ref, lsem).wait()


def _pallas(body, args, *, name, grid, in_specs, out_specs, out_shape, scratch_shapes=(), sem=None, vmem=None,
            rider=None):
    if rider is None:
        res = pl.pallas_call(body, name=name, grid=grid, in_specs=list(in_specs), out_specs=tuple(out_specs),
                             out_shape=tuple(out_shape), scratch_shapes=list(scratch_shapes),
                             compiler_params=_params(sem, vmem))(*args)
        return tuple(res), ()
    n_in, n_out, n_sc = len(in_specs), len(out_shape), len(scratch_shapes)
    r_in, r_out = len(rider.inputs), len(rider.out_shape)

    def wrapped(*refs):
        ins, rins = refs[:n_in], refs[n_in:n_in + r_in]
        p = n_in + r_in
        outs, routs = refs[p:p + n_out], refs[p + n_out:p + n_out + r_out]
        p += n_out + r_out
        scr, rsems = refs[p:p + n_sc], refs[p + n_sc:]
        first = pl.program_id(0) == 0
        last = pl.program_id(0) == grid[0] - 1
        for a in range(1, len(grid)):
            first = first & (pl.program_id(a) == 0)
            last = last & (pl.program_id(a) == grid[a] - 1)

        @pl.when(first)
        def _():
            rider.start(rins, routs, rsems)

        body(*ins, *outs, *scr)

        @pl.when(last)
        def _():
            rider.finish(rins, routs, rsems)

    res = pl.pallas_call(
        wrapped, name=name, grid=grid, in_specs=list(in_specs) + [ANY] * r_in,
        out_specs=tuple(out_specs) + (ANY,) * r_out, out_shape=tuple(out_shape) + tuple(rider.out_shape),
        scratch_shapes=list(scratch_shapes) + rider.scratch,
        compiler_params=_params(("arbitrary",) * len(grid), vmem))(*args, *rider.inputs)
    return tuple(res[:n_out]), tuple(res[n_out:])


def _ffn_fwd(h, gain, wgt, wut, wd, rider=None):
    t = h.shape[0]
    fc = D_FF // FF_CHUNKS

    def body(h_ref, gain_ref, wg_hbm, wu_hbm, wd_hbm, hout_ref, n_ref, g_ref, u_ref, wg_v, wu_v, wd_v):
        @pl.when(pl.program_id(0) == 0)
        def _():
            pltpu.sync_copy(wg_hbm, wg_v)
            pltpu.sync_copy(wu_hbm, wu_v)
            pltpu.sync_copy(wd_hbm, wd_v)

        hh = h_ref[...]
        r = lax.rsqrt(jnp.mean(hh * hh, axis=-1, keepdims=True) + EPS)
        n = (hh * r * gain_ref[...]).astype(BF16)
        n_ref[...] = n
        acc = jnp.zeros((TM, D_MODEL), F32)
        for ci in range(FF_CHUNKS):
            sl = slice(ci * fc, (ci + 1) * fc)
            g = _dot_nt(n, wg_v[sl, :])
            u = _dot_nt(n, wu_v[sl, :])
            g_ref[:, sl] = g.astype(BF16)
            u_ref[:, sl] = u.astype(BF16)
            a = (g * _sigmoid(g) * u).astype(BF16)
            acc = acc + _dot(a, wd_v[sl, :])
        hout_ref[...] = hh + 0.5 * acc

    row = lambda w: pl.BlockSpec((TM, w), lambda i: (i, 0))
    wv = pltpu.VMEM((D_FF, D_MODEL), BF16)
    return _pallas(
        body, (h, gain, wgt, wut, wd), name="ffn_fwd", grid=(t // TM,),
        out_shape=(jax.ShapeDtypeStruct((t, D_MODEL), F32), jax.ShapeDtypeStruct((t, D_MODEL), BF16),
                   jax.ShapeDtypeStruct((t, D_FF), BF16), jax.ShapeDtypeStruct((t, D_FF), BF16)),
        in_specs=[row(D_MODEL), pl.BlockSpec((1, D_MODEL), lambda i: (0, 0)), ANY, ANY, ANY],
        out_specs=(row(D_MODEL), row(D_MODEL), row(D_FF), row(D_FF)),
        scratch_shapes=[wv, wv, wv], sem=("arbitrary",), vmem=VMEM_BIG, rider=rider)


def _ffn_bwd(dhout, h, gain, g, u, wgt, wut, wd):
    t = h.shape[0]
    tm = TM_BWD
    fc = D_FF // FF_CHUNKS

    def body(dho_ref, h_ref, gain_ref, g_ref, u_ref, wg_hbm, wu_hbm, wd_hbm,
             dh_ref, dg_ref, du_ref, a_ref, df_ref, gg_ref, wg_v, wu_v, wd_v):
        @pl.when(pl.program_id(0) == 0)
        def _():
            pltpu.sync_copy(wg_hbm, wg_v)
            pltpu.sync_copy(wu_hbm, wu_v)
            pltpu.sync_copy(wd_hbm, wd_v)
            gg_ref[...] = jnp.zeros_like(gg_ref)

        dho = dho_ref[...]
        df = (0.5 * dho).astype(BF16)
        df_ref[...] = df
        dn = jnp.zeros((tm, D_MODEL), F32)
        for ci in range(FF_CHUNKS):
            sl = slice(ci * fc, (ci + 1) * fc)
            da = _dot_nt(df, wd_v[sl, :])
            gv = g_ref[:, sl].astype(F32)
            uv = u_ref[:, sl].astype(F32)
            sg = _sigmoid(gv)
            silu = gv * sg
            dg = (da * uv * (sg * (1.0 + gv * (1.0 - sg)))).astype(BF16)
            du = (da * silu).astype(BF16)
            dg_ref[:, sl] = dg
            du_ref[:, sl] = du
            a_ref[:, sl] = (silu * uv).astype(BF16)
            dn = dn + _dot(dg, wg_v[sl, :]) + _dot(du, wu_v[sl, :])
        hh = h_ref[...]
        r = lax.rsqrt(jnp.mean(hh * hh, axis=-1, keepdims=True) + EPS)
        hn = hh * r
        gg_ref[...] += jnp.sum(dn * hn, axis=0, keepdims=True)
        dng = dn * gain_ref[...]
        dh_ref[...] = dho + r * (dng - hn * jnp.mean(dng * hn, axis=-1, keepdims=True))

    row = lambda w: pl.BlockSpec((tm, w), lambda i: (i, 0))
    vec = pl.BlockSpec((1, D_MODEL), lambda i: (0, 0))
    wv = pltpu.VMEM((D_FF, D_MODEL), BF16)
    return pl.pallas_call(
        body, name="ffn_bwd", grid=(t // tm,),
        out_shape=(jax.ShapeDtypeStruct((t, D_MODEL), F32), jax.ShapeDtypeStruct((t, D_FF), BF16),
                   jax.ShapeDtypeStruct((t, D_FF), BF16), jax.ShapeDtypeStruct((t, D_FF), BF16),
                   jax.ShapeDtypeStruct((t, D_MODEL), BF16), jax.ShapeDtypeStruct((1, D_MODEL), F32)),
        in_specs=[row(D_MODEL), row(D_MODEL), vec, row(D_FF), row(D_FF), ANY, ANY, ANY],
        out_specs=(row(D_MODEL), row(D_FF), row(D_FF), row(D_FF), row(D_MODEL), vec),
        scratch_shapes=[wv, wv, wv],
        compiler_params=_params(("arbitrary",), VMEM_BIG),
    )(dhout, h, gain, g, u, wgt, wut, wd)


def _wgrad(lhs, rhs, rb, with_colsum=False, name="wgrad"):
    t, k = lhs.shape
    n = rhs.shape[1]
    tk = 512
    nt = t // tk

    def body(l_ref, r_ref, o_ref, *rest):
        acc = rest[-1]
        ti = pl.program_id(1)

        @pl.when(ti == 0)
        def _():
            acc[...] = jnp.zeros_like(acc)
            if with_colsum:
                rest[0][...] = jnp.zeros_like(rest[0])

        acc[...] += _dot_tn(l_ref[...], r_ref[...])
        if with_colsum:
            rest[0][...] += jnp.sum(l_ref[...].astype(F32), axis=0, keepdims=True)

        @pl.when(ti == nt - 1)
        def _():
            o_ref[...] = acc[...].astype(BF16)

    out_shape = [jax.ShapeDtypeStruct((k, n), BF16)]
    out_specs = [pl.BlockSpec((rb, n), lambda j, i: (j, 0))]
    if with_colsum:
        out_shape.append(jax.ShapeDtypeStruct((1, k), F32))
        out_specs.append(pl.BlockSpec((1, rb), lambda j, i: (0, j)))
    res = pl.pallas_call(
        body, name=name, grid=(k // rb, nt), out_shape=tuple(out_shape),
        in_specs=[pl.BlockSpec((tk, rb), lambda j, i: (i, j)), pl.BlockSpec((tk, n), lambda j, i: (i, 0))],
        out_specs=tuple(out_specs), scratch_shapes=[pltpu.VMEM((rb, n), F32)],
        compiler_params=_params(("arbitrary", "arbitrary"), VMEM_BIG),
    )(lhs, rhs)
    return res if with_colsum else res[0]


def _inproj_fwd(h, gain, wint, b_in, rider=None):
    t = h.shape[0]
    nc = 5
    cw = D_IN // nc

    def body(h_ref, gain_ref, w_hbm, b_ref, u_ref, z_ref, w_v):
        @pl.when(pl.program_id(0) == 0)
        def _():
            pltpu.sync_copy(w_hbm, w_v)

        hh = h_ref[...]
        r = lax.rsqrt(jnp.mean(hh * hh, axis=-1, keepdims=True) + EPS)
        un = (hh * r * gain_ref[...]).astype(BF16)
        u_ref[...] = un
        for ci in range(nc):
            sl = slice(ci * cw, (ci + 1) * cw)
            z_ref[:, sl] = (_dot_nt(un, w_v[sl, :]) + b_ref[:, sl]).astype(BF16)

    row = lambda w: pl.BlockSpec((TM, w), lambda i: (i, 0))
    return _pallas(
        body, (h, gain, wint, b_in), name="inproj_fwd", grid=(t // TM,),
        out_shape=(jax.ShapeDtypeStruct((t, D_MODEL), BF16), jax.ShapeDtypeStruct((t, D_IN), BF16)),
        in_specs=[row(D_MODEL), pl.BlockSpec((1, D_MODEL), lambda i: (0, 0)), ANY,
                  pl.BlockSpec((1, D_IN), lambda i: (0, 0))],
        out_specs=(row(D_MODEL), row(D_IN)),
        scratch_shapes=[pltpu.VMEM((D_IN, D_MODEL), BF16)], sem=("arbitrary",), vmem=VMEM_BIG, rider=rider)


def _inproj_bwd(dz, dh2, h, gain, wint, rider=None):
    t = h.shape[0]
    nc = 5
    cw = D_IN // nc

    def body(dz_ref, dh2_ref, h_ref, gain_ref, w_hbm, dh_ref, gg_ref, w_v):
        @pl.when(pl.program_id(0) == 0)
        def _():
            pltpu.sync_copy(w_hbm, w_v)
            gg_ref[...] = jnp.zeros_like(gg_ref)

        du = jnp.zeros((TM, D_MODEL), F32)
        for ci in range(nc):
            sl = slice(ci * cw, (ci + 1) * cw)
            du = du + _dot(dz_ref[:, sl], w_v[sl, :])
        hh = h_ref[...]
        r = lax.rsqrt(jnp.mean(hh * hh, axis=-1, keepdims=True) + EPS)
        hn = hh * r
        gg_ref[...] += jnp.sum(du * hn, axis=0, keepdims=True)
        dng = du * gain_ref[...]
        dh_ref[...] = dh2_ref[...] + r * (dng - hn * jnp.mean(dng * hn, axis=-1, keepdims=True))

    row = lambda w: pl.BlockSpec((TM, w), lambda i: (i, 0))
    vec = pl.BlockSpec((1, D_MODEL), lambda i: (0, 0))
    return _pallas(
        body, (dz, dh2, h, gain, wint), name="inproj_bwd", grid=(t // TM,),
        out_shape=(jax.ShapeDtypeStruct((t, D_MODEL), F32), jax.ShapeDtypeStruct((1, D_MODEL), F32)),
        in_specs=[row(D_IN), row(D_MODEL), row(D_MODEL), vec, ANY],
        out_specs=(row(D_MODEL), vec),
        scratch_shapes=[pltpu.VMEM((D_IN, D_MODEL), BF16)], sem=("arbitrary",), vmem=VMEM_BIG, rider=rider)


def _head_sum_matrix(w):
    i = lax.broadcasted_iota(jnp.int32, (w, w), 0) // HEAD_DIM
    j = lax.broadcasted_iota(jnp.int32, (w, w), 1) // HEAD_DIM
    return (i == j).astype(F32)


def _merge_fwd(o0, o1, o2, l0, l1, l2, yb, z, h1, wat, wbt, wout):
    t = h1.shape[0]

    def body(o0_ref, o1_ref, o2_ref, l0_ref, l1_ref, l2_ref, yb_ref, ga_ref, gb_ref, h1_ref, wa_ref, wb_ref, wo_ref,
             h2_ref, y_ref, lt_ref, pa_ref, pb_ref, mg_ref):
        la, lb, lc = l0_ref[...], l1_ref[...], l2_ref[...]
        mx = jnp.maximum(jnp.maximum(la, lb), lc)
        ea, eb, ec = jnp.exp(la - mx), jnp.exp(lb - mx), jnp.exp(lc - mx)
        den = ea + eb + ec
        y = (ea * o0_ref[...] + eb * o1_ref[...] + ec * o2_ref[...]) / den
        lt_ref[...] = mx + jnp.log(den)
        yb16 = y.astype(BF16)
        y_ref[...] = yb16
        pa = _dot_nt(yb16, wa_ref[...])
        pb = _dot_nt(yb_ref[...], wb_ref[...])
        pa_ref[...] = pa.astype(BF16)
        pb_ref[...] = pb.astype(BF16)
        mg = (_sigmoid(ga_ref[...].astype(F32)) * pa + _sigmoid(gb_ref[...].astype(F32)) * pb).astype(BF16)
        mg_ref[...] = mg
        h2_ref[...] = h1_ref[...] + _dot(mg, wo_ref[...])

    row = lambda w: pl.BlockSpec((TM, w), lambda i: (i, 0))
    full = lambda a: pl.BlockSpec(a.shape, lambda i: (0, 0))
    gate = lambda cb: pl.BlockSpec((TM, D_MODEL), lambda i: (i, cb))
    return pl.pallas_call(
        body, name="merge_fwd", grid=(t // TM,),
        out_shape=(jax.ShapeDtypeStruct((t, D_MODEL), F32), jax.ShapeDtypeStruct((t, GW), BF16),
                   jax.ShapeDtypeStruct((t, GW), F32), jax.ShapeDtypeStruct((t, D_MODEL), BF16),
                   jax.ShapeDtypeStruct((t, D_MODEL), BF16), jax.ShapeDtypeStruct((t, D_MODEL), BF16)),
        in_specs=[row(GW)] * 6 + [row(2 * GW), gate(3), gate(4), row(D_MODEL), full(wat), full(wbt), full(wout)],
        out_specs=(row(D_MODEL), row(GW), row(GW), row(D_MODEL), row(D_MODEL), row(D_MODEL)),
        compiler_params=_params(("parallel",), VMEM_BIG),
    )(o0, o1, o2, l0, l1, l2, yb, z, z, h1, wat, wbt, wout)


def _merge_bwd(dh2, pa, pb, z, y, yb, wat, wbt, wout, rider=None):
    t = dh2.shape[0]

    def body(dh2_ref, pa_ref, pb_ref, ga_ref, gb_ref, y_ref, yb_ref, wa_ref, wb_ref, wo_ref,
             dpa_ref, dpb_ref, dga_ref, dgb_ref, dya_ref, dyb_ref, dh2b_ref, ca_ref, cb_ref):
        d16 = dh2_ref[...].astype(BF16)
        dh2b_ref[...] = d16
        dm = _dot_nt(d16, wo_ref[...])
        sa = _sigmoid(ga_ref[...].astype(F32))
        sb = _sigmoid(gb_ref[...].astype(F32))
        dpa = (dm * sa).astype(BF16)
        dpb = (dm * sb).astype(BF16)
        dpa_ref[...] = dpa
        dpb_ref[...] = dpb
        dga_ref[...] = (dm * pa_ref[...].astype(F32) * sa * (1.0 - sa)).astype(BF16)
        dgb_ref[...] = (dm * pb_ref[...].astype(F32) * sb * (1.0 - sb)).astype(BF16)
        dya = _dot(dpa, wa_ref[...])
        dyb = _dot(dpb, wb_ref[...])
        dya_ref[...] = dya.astype(BF16)
        dyb_ref[...] = dyb.astype(BF16)
        hp = lax.Precision.HIGHEST
        ca_ref[...] = jnp.dot(dya * y_ref[...].astype(F32), _head_sum_matrix(GW), precision=hp,
                              preferred_element_type=F32)
        cb_ref[...] = jnp.dot(dyb * yb_ref[...].astype(F32), _head_sum_matrix(2 * GW), precision=hp,
                              preferred_element_type=F32)

    row = lambda w: pl.BlockSpec((TM, w), lambda i: (i, 0))
    full = lambda a: pl.BlockSpec(a.shape, lambda i: (0, 0))
    gate = lambda cb: pl.BlockSpec((TM, D_MODEL), lambda i: (i, cb))
    bf = lambda w: jax.ShapeDtypeStruct((t, w), BF16)
    return _pallas(
        body, (dh2, pa, pb, z, z, y, yb, wat, wbt, wout), name="merge_bwd", grid=(t // TM,),
        out_shape=(bf(D_MODEL), bf(D_MODEL), bf(D_MODEL), bf(D_MODEL), bf(GW), bf(2 * GW), bf(D_MODEL),
                   jax.ShapeDtypeStruct((t, GW), F32), jax.ShapeDtypeStruct((t, 2 * GW), F32)),
        in_specs=[row(D_MODEL), row(D_MODEL), row(D_MODEL), gate(3), gate(4), row(GW), row(2 * GW),
                  full(wat), full(wbt), full(wout)],
        out_specs=(row(D_MODEL), row(D_MODEL), row(D_MODEL), row(D_MODEL), row(GW), row(2 * GW), row(D_MODEL),
                   row(GW), row(2 * GW)),
        sem=("parallel",), vmem=VMEM_BIG, rider=rider)


def _loss_head(h3, gain, tgt):
    t = h3.shape[0]

    def body(h_ref, gain_ref, t_ref, dh_ref, loss_ref, gg_ref):
        @pl.when(pl.program_id(0) == 0)
        def _():
            loss_ref[...] = jnp.zeros_like(loss_ref)
            gg_ref[...] = jnp.zeros_like(gg_ref)

        hh = h_ref[...]
        r = lax.rsqrt(jnp.mean(hh * hh, axis=-1, keepdims=True) + EPS)
        hn = hh * r
        err = hn * gain_ref[...] - t_ref[...]
        part = jnp.sum(jnp.sum(err * err, axis=1, keepdims=True), axis=0, keepdims=True)
        loss_ref[...] += (0.5 / D_MODEL) * part
        dy = err * (1.0 / D_MODEL)
        gg_ref[...] += jnp.sum(dy * hn, axis=0, keepdims=True)
        dng = dy * gain_ref[...]
        dh_ref[...] = r * (dng - hn * jnp.mean(dng * hn, axis=-1, keepdims=True))

    row = pl.BlockSpec((TM, D_MODEL), lambda i: (i, 0))
    vec = pl.BlockSpec((1, D_MODEL), lambda i: (0, 0))
    return pl.pallas_call(
        body, name="loss_head", grid=(t // TM,),
        out_shape=(jax.ShapeDtypeStruct((t, D_MODEL), F32), jax.ShapeDtypeStruct((8, 128), F32),
                   jax.ShapeDtypeStruct((1, D_MODEL), F32)),
        in_specs=[row, vec, row], out_specs=(row, pl.BlockSpec((8, 128), lambda i: (0, 0)), vec),
        compiler_params=_params(("arbitrary",)),
    )(h3, gain, tgt)


def _lane_head(rows):
    return lax.broadcasted_iota(jnp.int32, (rows, GW), 1) // HEAD_DIM


def _kv_expand_matrix(r):
    ci = lax.broadcasted_iota(jnp.int32, (2 * HEAD_DIM, GW), 0)
    ji = lax.broadcasted_iota(jnp.int32, (2 * HEAD_DIM, GW), 1)
    return (ci == (ji % HEAD_DIM) + HEAD_DIM * r).astype(BF16)


def _attn_fwd(q_arr, k_arr, v_arr, bias, sink, *, grid, seq, kvw, q_map, k_map, v_map, bias_map, sink_map,
              has_sink, o_shape, o_dtype, name, rider=None):
    nb = seq // BLOCK

    def body(q_ref, k_ref, v_ref, bias_ref, sink_ref, o_ref, lse_ref):
        lane_head = _lane_head(BLOCK)
        expand = _kv_expand_matrix(pl.program_id(1)) if kvw != GW else None

        def block(i, first):
            r0 = pl.multiple_of(i * BLOCK, BLOCK)
            qi = q_ref[0, pl.ds(r0, BLOCK), :]
            if first:
                kc = k_ref[0, pl.ds(0, BLOCK), :]
                vc = v_ref[0, pl.ds(0, BLOCK), :]
            else:
                k0 = pl.multiple_of(r0 - BLOCK, BLOCK)
                kc = k_ref[0, pl.ds(k0, 2 * BLOCK), :]
                vc = v_ref[0, pl.ds(k0, 2 * BLOCK), :]
            if expand is not None:
                kc = _dot(kc, expand).astype(BF16)
                vc = _dot(vc, expand).astype(BF16)
            acc_o = jnp.zeros((BLOCK, GW), F32)
            acc_l = jnp.zeros((BLOCK, GW), F32)
            for h in range(4):
                qh = jnp.where(lane_head == h, qi, jnp.zeros_like(qi))
                bias_h = bias_ref[h, :, BLOCK:] if first else bias_ref[h]
                s = _dot_nt(qh, kc) * (HEAD_DIM ** -0.5) + bias_h
                m = jnp.max(s, axis=-1, keepdims=True)
                if has_sink:
                    sk = sink_ref[0, h:h + 1, 0:1]
                    m = jnp.maximum(m, sk)
                p = jnp.exp(s - m)
                l = jnp.sum(p, axis=-1, keepdims=True)
                if has_sink:
                    l = l + jnp.exp(sk - m)
                oh = _dot(p.astype(BF16), vc) / l
                acc_o = jnp.where(lane_head == h, oh, acc_o)
                acc_l = jnp.where(lane_head == h, m + jnp.log(l), acc_l)
            o_ref[0, pl.ds(r0, BLOCK), :] = acc_o.astype(o_dtype)
            lse_ref[0, pl.ds(r0, BLOCK), :] = acc_l

        block(0, True)
        if nb > 1:
            def step(i, carry):
                block(i, False)
                return carry
            lax.fori_loop(1, nb, step, 0)

    blk = lambda w, m: pl.BlockSpec((1, seq, w), m)
    o_map = lambda n, r: (n, 0, r)
    return _pallas(
        body, (q_arr, k_arr, v_arr, bias, sink), name=name, grid=grid,
        out_shape=(jax.ShapeDtypeStruct(o_shape, o_dtype), jax.ShapeDtypeStruct(o_shape, F32)),
        in_specs=[blk(GW, q_map), blk(kvw, k_map), blk(kvw, v_map),
                  pl.BlockSpec((4, BLOCK, 2 * BLOCK), bias_map), pl.BlockSpec((1, 4, 128), sink_map)],
        out_specs=(blk(GW, o_map), blk(GW, o_map)),
        sem=("parallel", "arbitrary"), vmem=VMEM_BIG, rider=rider)


def _attn_bwd(q_arr, k_arr, v_arr, bias, sink, dy, cc, lse, *, grid, seq, kvw, q_map, k_map, v_map, bias_map,
              sink_map, has_sink, n_bias, dq_shape, dkv_shape, name):
    nb = seq // BLOCK
    scale = HEAD_DIM ** -0.5

    def body(q_ref, k_ref, v_ref, bias_ref, sink_ref, dy_ref, c_ref, lse_ref,
             dq_ref, dk_ref, dv_ref, db_ref, dsk_ref, dk_acc, dv_acc, dk_half, dv_half):
        rr = pl.program_id(1)

        @pl.when((pl.program_id(0) == 0) & (rr == 0))
        def _():
            db_ref[...] = jnp.zeros_like(db_ref)
            dsk_ref[...] = jnp.zeros_like(dsk_ref)

        dk_acc[...] = jnp.zeros_like(dk_acc)
        dv_acc[...] = jnp.zeros_like(dv_acc)
        lane_head = _lane_head(BLOCK)
        expand = _kv_expand_matrix(rr) if kvw != GW else None
        hb = 4 * rr if n_bias == 8 else 0

        def block(i, first):
            r0 = pl.multiple_of(i * BLOCK, BLOCK)
            rows = pl.ds(r0, BLOCK)
            qi = q_ref[0, rows, :]
            dyi = dy_ref[0, rows, :]
            ci = c_ref[0, rows, :]
            li = lse_ref[0, rows, :]
            if first:
                krows = pl.ds(0, BLOCK)
            else:
                krows = pl.ds(pl.multiple_of(r0 - BLOCK, BLOCK), 2 * BLOCK)
            kc = k_ref[0, krows, :]
            vc = v_ref[0, krows, :]
            if expand is not None:
                kc = _dot(kc, expand).astype(BF16)
                vc = _dot(vc, expand).astype(BF16)
            nk = BLOCK if first else 2 * BLOCK
            dq = jnp.zeros((BLOCK, GW), F32)
            dkc = jnp.zeros((nk, GW), F32)
            dvc = jnp.zeros((nk, GW), F32)
            for h in range(4):
                sel = lane_head == h
                qh = jnp.where(sel, qi, jnp.zeros_like(qi))
                dyh = jnp.where(sel, dyi, jnp.zeros_like(dyi))
                bias_h = bias_ref[h, :, BLOCK:] if first else bias_ref[h]
                s = _dot_nt(qh, kc) * scale + bias_h
                lh = li[:, h * HEAD_DIM:h * HEAD_DIM + 1]
                ch = ci[:, h * HEAD_DIM:h * HEAD_DIM + 1]
                p = jnp.exp(s - lh)
                p16 = p.astype(BF16)
                dvc = dvc + _dot_tn(p16, dyh)
                dp = _dot_nt(dyh, vc)
                ds = p * (dp - ch)
                if first:
                    db_ref[hb + h, :, BLOCK:] += ds
                else:
                    db_ref[hb + h] += ds
                ds16 = ds.astype(BF16)
                dq = dq + jnp.where(sel, _dot(ds16, kc), 0.0)
                dkc = dkc + _dot_tn(ds16, qh)
                if has_sink:
                    sk = sink_ref[0, h:h + 1, 0:1]
                    val = -jnp.sum(jnp.exp(sk - lh) * ch, axis=0, keepdims=True)
                    dsk_ref[hb + h] += jnp.broadcast_to(val, (8, 128))
            dq_ref[0, rows, :] = (dq * scale).astype(BF16)
            dk_acc[krows, :] += dkc * scale
            dv_acc[krows, :] += dvc

        block(0, True)
        if nb > 1:
            def step(i, carry):
                block(i, False)
                return carry
            lax.fori_loop(1, nb, step, 0)

        if kvw == GW:
            dk_ref[0] = dk_acc[...].astype(BF16)
            dv_ref[0] = dv_acc[...].astype(BF16)
        else:
            def fold(acc):
                t2 = acc[:, :2 * HEAD_DIM] + acc[:, 2 * HEAD_DIM:]
                t2 = t2 + pltpu.roll(t2, HEAD_DIM, 1)
                lane = lax.broadcasted_iota(jnp.int32, t2.shape, 1) // HEAD_DIM
                return jnp.where(lane == rr, t2, 0.0)

            @pl.when(rr == 0)
            def _():
                dk_half[...] = fold(dk_acc[...])
                dv_half[...] = fold(dv_acc[...])

            @pl.when(rr == 1)
            def _():
                dk_ref[0] = (dk_half[...] + fold(dk_acc[...])).astype(BF16)
                dv_ref[0] = (dv_half[...] + fold(dv_acc[...])).astype(BF16)

    blk = lambda w, m: pl.BlockSpec((1, seq, w), m)
    o_map = lambda n, r: (n, 0, r)
    kv_out_map = o_map if kvw == GW else (lambda n, r: (n, 0, 0))
    return pl.pallas_call(
        body, name=name, grid=grid,
        out_shape=(jax.ShapeDtypeStruct(dq_shape, BF16), jax.ShapeDtypeStruct(dkv_shape, BF16),
                   jax.ShapeDtypeStruct(dkv_shape, BF16), jax.ShapeDtypeStruct((n_bias, BLOCK, 2 * BLOCK), F32),
                   jax.ShapeDtypeStruct((8, 8, 128), F32)),
        in_specs=[blk(GW, q_map), blk(kvw, k_map), blk(kvw, v_map),
                  pl.BlockSpec((4, BLOCK, 2 * BLOCK), bias_map), pl.BlockSpec((1, 4, 128), sink_map),
                  blk(GW, o_map), blk(GW, o_map), blk(GW, o_map)],
        out_specs=(blk(GW, o_map), blk(kvw, kv_out_map), blk(kvw, kv_out_map),
                   pl.BlockSpec((n_bias, BLOCK, 2 * BLOCK), lambda n, r: (0, 0, 0)),
                   pl.BlockSpec((8, 8, 128), lambda n, r: (0, 0, 0))),
        scratch_shapes=[pltpu.VMEM((seq, GW), F32), pltpu.VMEM((seq, GW), F32),
                        pltpu.VMEM((seq, 2 * HEAD_DIM), F32), pltpu.VMEM((seq, 2 * HEAD_DIM), F32)],
        compiler_params=_params(("arbitrary", "arbitrary"), VMEM_BIG),
    )(q_arr, k_arr, v_arr, bias, sink, dy, cc, lse)


def _bias_grad(ds_all, buckets):
    def body(ds_ref, bk_ref, o_ref):
        rows = lax.broadcasted_iota(jnp.int32, (N_BUCKETS, 128), 0)
        cols = lax.broadcasted_iota(jnp.int32, (N_BUCKETS, 128), 1)

        def per_bucket(b, acc):
            for h in range(20):
                gi = h // 4 if h < 12 else 3
                v = jnp.where(bk_ref[gi] == b, ds_ref[h], 0.0)
                v = jnp.sum(jnp.sum(v, axis=1, keepdims=True), axis=0, keepdims=True)
                acc = jnp.where((rows == b) & (cols == h), v, acc)
            return acc

        o_ref[...] = lax.fori_loop(0, N_BUCKETS, per_bucket, jnp.zeros((N_BUCKETS, 128), F32))

    vm = pl.BlockSpec(memory_space=pltpu.VMEM)
    return pl.pallas_call(body, name="bias_grad", out_shape=jax.ShapeDtypeStruct((N_BUCKETS, 128), F32),
                          in_specs=[vm, vm], out_specs=vm)(ds_all, buckets)


def _adamw(w, g, m, v, name):
    r, c = w.shape
    tr = r
    for cand in (256, 176, 128, 64, 32, 16, 8):
        if r % cand == 0:
            tr = cand
            break
    bc1 = 1.0 - ADAM_B1 ** ADAM_STEP
    bc2 = 1.0 - ADAM_B2 ** ADAM_STEP

    def body(w_ref, g_ref, m_ref, v_ref, d_ref, nm_ref, nv_ref):
        gv = g_ref[...]
        nm = ADAM_B1 * m_ref[...] + (1.0 - ADAM_B1) * gv
        nv = ADAM_B2 * v_ref[...] + (1.0 - ADAM_B2) * (gv * gv)
        nm_ref[...] = nm
        nv_ref[...] = nv
        d_ref[...] = -ADAM_LR * ((nm / bc1) / (jnp.sqrt(nv / bc2) + ADAM_EPS) + ADAM_WD * w_ref[...])

    spec = pl.BlockSpec((tr, c), lambda i: (i, 0))
    shp = jax.ShapeDtypeStruct((r, c), F32)
    return pl.pallas_call(body, name=name, grid=(r // tr,), out_shape=(shp, shp, shp),
                          in_specs=[spec] * 4, out_specs=(spec, spec, spec),
                          compiler_params=_params(("parallel",)))(w, g, m, v)


def _t5_bucket(dist):
    max_exact = N_BUCKETS // 2
    n = jnp.maximum(dist, 0)
    nf = jnp.maximum(n, 1).astype(F32)
    large = max_exact + (jnp.log(nf / max_exact) / math.log(MAX_DISTANCE / max_exact)
                         * (N_BUCKETS - max_exact)).astype(jnp.int32)
    large = jnp.minimum(large, N_BUCKETS - 1)
    return jnp.where(n < max_exact, n, large)


def _bias_tables(rel_bias):
    qi = jnp.arange(BLOCK)[:, None]
    ki = jnp.arange(2 * BLOCK)[None, :]
    dist = qi + BLOCK - ki
    specs = [(d, w // d, 4 * gi, 4 * gi + 4) for gi, (w, d) in enumerate(DIL_GROUPS)] + [(1, B_WINDOW - 1, 12, 20)]
    biases, buckets = [], []
    for stride, steps, h0, h1 in specs:
        valid = (dist >= 0) & (dist <= steps)
        bk = jnp.where(valid, _t5_bucket(dist * stride), -1).astype(jnp.int32)
        onehot = (bk[None, :, :] == jnp.arange(N_BUCKETS, dtype=jnp.int32)[:, None, None]).astype(F32)
        b = jnp.einsum("bqk,bh->hqk", onehot, rel_bias[:, h0:h1], precision=lax.Precision.HIGHEST)
        biases.append(jnp.where(valid[None], b, NEG))
        buckets.append(bk)
    return jnp.concatenate(biases, axis=0), jnp.stack(buckets, axis=0)


def _local_step(x, tgt, W, S, shards=None):
    nseq, seq, _ = x.shape
    t = nseq * seq
    xf = x.reshape(t, D_MODEL)
    bias_all, buckets = _bias_tables(S["rel_bias"])
    sink_b = jnp.broadcast_to(S["sinks"].reshape(2, 4, 1), (2, 4, 128)).astype(F32)
    sink_0 = jnp.zeros((1, 4, 128), F32)
    dist = shards is not None
    W = dict(W)
    G, GS, reduced = {}, {}, {}

    def put(keys, gathered):
        for k, g in zip(keys, gathered):
            W[k] = g.reshape(_FULL_SHAPE.get(k, (N_CHIPS * shards[k].shape[0], D_MODEL)))

    def gather_rider(keys):
        return _GatherRider([shards[k] for k in keys]) if dist else None

    def pair(keys):
        return _pair_reduce([G[k].reshape(N_CHIPS, 2, shards[k].shape[0] // 2, D_MODEL) for k in keys],
                            "grad_pair_reduce_" + keys[0])

    def finish(keys, own, rec):
        full = _final_reduce(own, rec, "grad_final_reduce_" + keys[0])
        off = 0
        for k in keys:
            r = shards[k].shape[0]
            reduced[k] = full[:, off:off + r // 2].reshape(r, D_MODEL)
            off += r // 2

    if dist:
        first = ("wgt1", "wut1", "wd1")
        put(first, _gather_rows([shards[k] for k in first]))
    keys = ("wint",)
    (h1, n1, g1, u1), ro = _ffn_fwd(xf, S["ffn1_norm"], W["wgt1"], W["wut1"], W["wd1"], rider=gather_rider(keys))
    put(keys, ro)
    keys = ("wout", "wat", "wbt")
    (un, z), ro = _inproj_fwd(h1, S["mix_norm"], W["wint"], S["b_in"], rider=gather_rider(keys))
    put(keys, ro)

    a_cfg = []
    outs, lses = [], []
    for gi, (_, d) in enumerate(DIL_GROUPS):
        ln = seq // d
        cpr = D_IN // GW
        cfg = dict(grid=(nseq, d), seq=ln, kvw=GW,
                   q_map=lambda n, r, gi=gi, cpr=cpr: (n, 0, r * cpr + gi),
                   k_map=lambda n, r, gi=gi, cpr=cpr: (n, 0, r * cpr + 3 + gi),
                   v_map=lambda n, r, gi=gi, cpr=cpr: (n, 0, r * cpr + 6 + gi),
                   bias_map=lambda n, r: (0, 0, 0), sink_map=lambda n, r: (0, 0, 0), has_sink=False)
        a_cfg.append(cfg)
        zr = z.reshape(nseq, ln, d * D_IN)
        (o, lse), _ = _attn_fwd(zr, zr, zr, bias_all[4 * gi:4 * gi + 4], sink_0, o_shape=(nseq, ln, d * GW),
                                o_dtype=F32, name=f"attn_a{gi}_fwd", **cfg)
        outs.append(o.reshape(t, GW))
        lses.append(lse.reshape(t, GW))
    z3 = z.reshape(nseq, seq, D_IN)
    b_cfg = dict(grid=(nseq, 2), seq=seq, kvw=2 * HEAD_DIM,
                 q_map=lambda n, r: (n, 0, 9 + r), k_map=lambda n, r: (n, 0, 22), v_map=lambda n, r: (n, 0, 23),
                 bias_map=lambda n, r: (r, 0, 0), sink_map=lambda n, r: (r, 0, 0), has_sink=True)
    keys = ("wgt2", "wut2", "wd2")
    (yb, lse_b), ro = _attn_fwd(z3, z3, z3, bias_all[12:20], sink_b, o_shape=(nseq, seq, 2 * GW), o_dtype=BF16,
                                name="attn_b_fwd", rider=gather_rider(keys), **b_cfg)
    put(keys, ro)
    yb = yb.reshape(t, 2 * GW)

    h2, y, lse_tot, pa, pb, merged = _merge_fwd(outs[0], outs[1], outs[2], lses[0], lses[1], lses[2], yb, z, h1,
                                                W["wat"], W["wbt"], W["wout"])
    (h3, n2, g2, u2), _ = _ffn_fwd(h2, S["ffn2_norm"], W["wgt2"], W["wut2"], W["wd2"])
    dh3, loss_part, g_final = _loss_head(h3, S["final_norm"].reshape(1, D_MODEL), tgt.reshape(t, D_MODEL))

    GS["final_norm"] = g_final
    dh2, dg2, du2, a2, df2, GS["ffn2_norm"] = _ffn_bwd(dh3, h2, S["ffn2_norm"], g2, u2, W["wgt2"], W["wut2"], W["wd2"])
    G["wgt2"] = _wgrad(dg2, n2, D_FF, name="wgrad_gate2")
    G["wut2"] = _wgrad(du2, n2, D_FF, name="wgrad_up2")
    G["wd2"] = _wgrad(a2, df2, D_FF, name="wgrad_down2")

    keys = ("wgt2", "wut2", "wd2")
    rider = _ExchangeRider([pair(keys)]) if dist else None
    (dpa, dpb, dga, dgb, dya, dyb, dh2b, ca, cb), ro = _merge_bwd(dh2, pa, pb, z, y, yb, W["wat"], W["wbt"], W["wout"],
                                                                  rider=rider)
    if dist:
        finish(keys, *ro)
    G["wout"] = _wgrad(merged, dh2b, D_MODEL, name="wgrad_out")
    G["wat"] = _wgrad(dpa, y, D_MODEL, name="wgrad_branch_a")
    G["wbt"] = _wgrad(dpb, yb, D_MODEL, name="wgrad_branch_b")

    dqs, dks, dvs, dbs = [], [], [], []
    for gi, (_, d) in enumerate(DIL_GROUPS):
        ln = seq // d
        zr = z.reshape(nseq, ln, d * D_IN)
        fold = lambda a, ln=ln, d=d: a.reshape(nseq, ln, d * GW)
        shp = (nseq, ln, d * GW)
        dq, dk, dv, db, _ = _attn_bwd(zr, zr, zr, bias_all[4 * gi:4 * gi + 4], sink_0, fold(dya), fold(ca),
                                      fold(lse_tot), n_bias=4, dq_shape=shp, dkv_shape=shp,
                                      name=f"attn_a{gi}_bwd", **a_cfg[gi])
        dqs.append(dq.reshape(t, GW))
        dks.append(dk.reshape(t, GW))
        dvs.append(dv.reshape(t, GW))
        dbs.append(db)
    r3 = lambda a: a.reshape(nseq, seq, a.shape[-1])
    dqb, dkb, dvb, dbb, dsink = _attn_bwd(z3, z3, z3, bias_all[12:20], sink_b, r3(dyb), r3(cb), lse_b, n_bias=8,
                                          dq_shape=(nseq, seq, 2 * GW), dkv_shape=(nseq, seq, 2 * HEAD_DIM),
                                          name="attn_b_bwd", **b_cfg)
    dz = jnp.concatenate(dqs + dks + dvs + [dqb.reshape(t, 2 * GW), dkb.reshape(t, 2 * HEAD_DIM),
                                            dvb.reshape(t, 2 * HEAD_DIM), dga, dgb], axis=-1)
    gb_tab = _bias_grad(jnp.concatenate(dbs + [dbb], axis=0), buckets)
    GS["rel_bias"] = gb_tab[:, :20]
    GS["sinks"] = dsink[:, 0, 0].reshape(1, 8)

    G["wint"], GS["b_in"] = _wgrad(dz, un, D_IN // 2, with_colsum=True, name="wgrad_in")
    keys = ("wint", "wout", "wat", "wbt")
    rider = _ExchangeRider([pair(keys)]) if dist else None
    (dh1, GS["mix_norm"]), ro = _inproj_bwd(dz, dh2, h1, S["mix_norm"], W["wint"], rider=rider)
    if dist:
        finish(keys, *ro)

    dx, dg1, du1, a1, df1, GS["ffn1_norm"] = _ffn_bwd(dh1, xf, S["ffn1_norm"], g1, u1, W["wgt1"], W["wut1"], W["wd1"])
    G["wgt1"] = _wgrad(dg1, n1, D_FF, name="wgrad_gate1")
    G["wut1"] = _wgrad(du1, n1, D_FF, name="wgrad_up1")
    G["wd1"] = _wgrad(a1, df1, D_FF, name="wgrad_down1")
    if dist:
        keys = ("wgt1", "wut1", "wd1")
        finish(keys, *_chip_exchange([pair(keys)]))
    return loss_part, dx.reshape(x.shape), (reduced if dist else G), GS


_SMALL = ("ffn1_norm", "mix_norm", "ffn2_norm", "final_norm", "b_in", "sinks", "rel_bias")
_ORDER = ("ffn1_norm", "ffn1_w_gate", "ffn1_w_up", "ffn1_w_down", "mix_norm", "w_in", "b_in", "w_branch_a",
          "w_branch_b", "w_out", "sinks", "rel_bias", "ffn2_norm", "ffn2_w_gate", "ffn2_w_up", "ffn2_w_down",
          "final_norm")
_BIG = (("wgt1", "ffn1_w_gate", True, 704), ("wut1", "ffn1_w_up", True, 704), ("wd1", "ffn1_w_down", False, 704),
        ("wint", "w_in", True, 1280), ("wout", "w_out", False, 256), ("wat", "w_branch_a", True, 64),
        ("wbt", "w_branch_b", True, 128), ("wgt2", "ffn2_w_gate", True, 704), ("wut2", "ffn2_w_up", True, 704),
        ("wd2", "ffn2_w_down", False, 704))
_FULL_SHAPE = {"wat": (D_MODEL, GW), "wbt": (D_MODEL, 2 * GW)}


def _pack_small(p, extra=None):
    last = [p["sinks"].reshape(8), p["rel_bias"].reshape(640)]
    used = 648
    if extra is not None:
        last.append(extra.reshape(1))
        used += 1
    last.append(jnp.zeros((D_MODEL - used,), F32))
    rows = [p["ffn1_norm"].reshape(1, D_MODEL), p["mix_norm"].reshape(1, D_MODEL), p["ffn2_norm"].reshape(1, D_MODEL),
            p["final_norm"].reshape(1, D_MODEL), p["b_in"].reshape(5, D_MODEL), jnp.concatenate(last).reshape(1, D_MODEL),
            jnp.zeros((6, D_MODEL), F32)]
    return jnp.concatenate(rows, axis=0)


def _unpack_small(a):
    return {"ffn1_norm": a[0:1], "mix_norm": a[1:2], "ffn2_norm": a[2:3], "final_norm": a[3],
            "b_in": a[4:9].reshape(1, D_IN), "sinks": a[9, 0:8].reshape(1, 8), "rel_bias": a[9, 8:648].reshape(32, 20)}


def kernel(x, ffn1_norm, ffn1_w_gate, ffn1_w_up, ffn1_w_down, mix_norm, w_in, b_in, w_branch_a, w_branch_b, w_out, sinks, rel_bias, ffn2_norm, ffn2_w_gate, ffn2_w_up, ffn2_w_down, final_norm, loss_target, m_ffn1_norm, m_ffn1_w_gate, m_ffn1_w_up, m_ffn1_w_down, m_mix_norm, m_w_in, m_b_in, m_w_branch_a, m_w_branch_b, m_w_out, m_sinks, m_rel_bias, m_ffn2_norm, m_ffn2_w_gate, m_ffn2_w_up, m_ffn2_w_down, m_final_norm, v_ffn1_norm, v_ffn1_w_gate, v_ffn1_w_up, v_ffn1_w_down, v_mix_norm, v_w_in, v_b_in, v_w_branch_a, v_w_branch_b, v_w_out, v_sinks, v_rel_bias, v_ffn2_norm, v_ffn2_w_gate, v_ffn2_w_up, v_ffn2_w_down, v_final_norm):
    args = dict(locals())
    w = {n: args[n] for n in _ORDER}
    m = {n: args["m_" + n] for n in _ORDER}
    v = {n: args["v_" + n] for n in _ORDER}

    shards = {}
    for key, name, transposed, rows in _BIG:
        a = w[name][0]
        a = (a.T if transposed else a).astype(BF16)
        shards[key] = a.reshape(rows, D_MODEL)
    S = {n: w[n] for n in _SMALL}

    loss_part, grad_x, reduced, GS = _local_step(x, loss_target, {}, S, shards)

    grads = {}
    for key, name, transposed, rows in _BIG:
        nat = w[name][0].shape
        grads[name] = reduced[key].reshape(nat[1], nat[0]).T if transposed else reduced[key].reshape(nat)

    small = _allreduce_small(_pack_small(GS, extra=loss_part[0, 0]))
    loss = small[9, 648]

    out_g, out_d, out_m, out_v = {}, {}, {}, {}
    for _, n, _, _ in _BIG:
        d_, nm_, nv_ = _adamw(w[n][0], grads[n], m[n][0], v[n][0], "adamw_" + n)
        out_g[n], out_d[n], out_m[n], out_v[n] = grads[n][None], d_[None], nm_[None], nv_[None]
    d_s, m_s, v_s = _adamw(_pack_small(w), small, _pack_small(m), _pack_small(v), "adamw_small")
    for dst, src in ((out_g, small), (out_d, d_s), (out_m, m_s), (out_v, v_s)):
        dst.update(_unpack_small(src))

    return (loss, grad_x, *[out_g[n] for n in _ORDER], *[out_d[n] for n in _ORDER],
            *[out_m[n] for n in _ORDER], *[out_v[n] for n in _ORDER])
```

```python
import math

import jax
import jax.numpy as jnp
from jax import lax
from jax.experimental import pallas as pl
from jax.experimental.pallas import tpu as pltpu

F32, BF16 = jnp.float32, jnp.bfloat16
MESH = pl.DeviceIdType.MESH

D_MODEL = 1024
D_FF = 2816
D_IN = 5120
HEAD_DIM = 64
BLOCK = 128
DIL_GROUPS = ((128, 1), (512, 4), (2048, 16))
B_WINDOW = 128
N_BUCKETS = 32
MAX_DISTANCE = 2048
EPS = 1e-6
N_CHIPS = 4
GW = 256
ZA_W = 2304
ZB_W = 768
NEG = -1e30

ADAM_LR, ADAM_B1, ADAM_B2, ADAM_EPS, ADAM_WD, ADAM_STEP = 0.001, 0.9, 0.999, 1e-08, 0.01, 10

VMEM_BIG = 56 * 1024 * 1024
TM = 512
TM_BWD = 256
FF_CHUNKS = 2
DMA_SPLIT = 8


def _dot(a, b):
    return jnp.dot(a, b, preferred_element_type=F32)


def _dot_nt(a, b):
    return lax.dot_general(a, b, (((1,), (1,)), ((), ())), preferred_element_type=F32)


def _dot_tn(a, b):
    return lax.dot_general(a, b, (((0,), (0,)), ((), ())), preferred_element_type=F32)


def _sigmoid(x):
    return 1.0 / (1.0 + jnp.exp(-x))


def _params(sem, vmem=None):
    return pltpu.CompilerParams(dimension_semantics=sem, vmem_limit_bytes=vmem)


ANY = pl.BlockSpec(memory_space=pl.ANY)


def _me():
    return lax.axis_index("x"), lax.axis_index("y"), lax.axis_index("c")


_CHIP_RELS = ((1, 0), (0, 1), (1, 1))


def _flip(v, f):
    return 1 - v if f else v


def _remote(src, dst, ssem, rsem, peer):
    return pltpu.make_async_remote_copy(src_ref=src, dst_ref=dst, send_sem=ssem, recv_sem=rsem,
                                        device_id=peer, device_id_type=MESH)


def _row_pieces(rows, n):
    step = max(16, -(-rows // n) // 16 * 16)
    out, s = [], 0
    while s < rows:
        out.append((s, min(step, rows - s)))
        s += step
    return out


def _gather_rows(shards):
    nt = len(shards)
    rows = [s.shape[0] for s in shards]

    def body(*refs):
        srcs, outs = refs[:nt], refs[nt:2 * nt]
        ici_s, ici_r, d2d_s, d2d_r, loc = refs[2 * nt:]
        x, y, c = _me()
        j = 2 * x + y
        sib = (x, y, 1 - c)
        local = [pltpu.make_async_copy(srcs[t], outs[t].at[j], loc.at[t]) for t in range(nt)]
        for cp in local:
            cp.start()
        sends = []
        for k, (fx, fy) in enumerate(_CHIP_RELS):
            peer = (_flip(x, fx), _flip(y, fy), c)
            for t in range(nt):
                half = pl.ds(c * (rows[t] // 2), rows[t] // 2)
                cp = _remote(srcs[t].at[half], outs[t].at[j, half], ici_s.at[3 * t + k], ici_r.at[3 * t + k], peer)
                cp.start()
                sends.append(cp)
        fwds = []
        for k, (fx, fy) in enumerate(_CHIP_RELS):
            pj = 2 * _flip(x, fx) + _flip(y, fy)
            for t in range(nt):
                half = pl.ds(c * (rows[t] // 2), rows[t] // 2)
                blk = outs[t].at[pj, half]
                _remote(blk, blk, ici_s.at[3 * t + k], ici_r.at[3 * t + k], sib).wait_recv()
                cp = _remote(blk, blk, d2d_s.at[3 * t + k], d2d_r.at[3 * t + k], sib)
                cp.start()
                fwds.append(cp)
        for cp in fwds:
            cp.wait()
        for cp in sends:
            cp.wait_send()
        for cp in local:
            cp.wait()

    sems = [pltpu.SemaphoreType.DMA((3 * nt,)) for _ in range(4)] + [pltpu.SemaphoreType.DMA((nt,))]
    return pl.pallas_call(
        body, name="gather_weights",
        out_shape=tuple(jax.ShapeDtypeStruct((N_CHIPS,) + s.shape, s.dtype) for s in shards),
        in_specs=[ANY] * nt, out_specs=tuple([ANY] * nt), scratch_shapes=sems,
    )(*shards)


VMEM_WHOLE = pl.BlockSpec(memory_space=pltpu.VMEM)


def _pair_reduce(grads, name):
    nt = len(grads)
    r2 = [g.shape[2] for g in grads]
    off = [sum(r2[:t]) for t in range(nt)]
    tot = sum(r2)

    def body(*refs):
        gs = refs[:nt]
        s_ref, got, ssem, rsem = refs[nt:]
        x, y, c = _me()
        sib = (x, y, 1 - c)
        for t in range(nt):
            for k in range(N_CHIPS):
                _remote(gs[t].at[k, 1 - c], got.at[k, pl.ds(off[t], r2[t])], ssem, rsem, sib).start()
        _remote(got, got, ssem, rsem, sib).wait()
        for t in range(nt):
            for k in range(N_CHIPS):
                rows = slice(off[t], off[t] + r2[t])
                s_ref[k, rows, :] = (gs[t][k, c].astype(F32) + got[k, rows, :].astype(F32)).astype(BF16)

    shp = jax.ShapeDtypeStruct((N_CHIPS, tot, D_MODEL), BF16)
    return pl.pallas_call(
        body, name=name, out_shape=shp, in_specs=[VMEM_WHOLE] * nt, out_specs=VMEM_WHOLE,
        scratch_shapes=[pltpu.VMEM((N_CHIPS, tot, D_MODEL), BF16), pltpu.SemaphoreType.DMA(()),
                        pltpu.SemaphoreType.DMA(())],
        compiler_params=pltpu.CompilerParams(vmem_limit_bytes=VMEM_BIG),
    )(*grads)


def _chip_exchange(parts):
    ng = len(parts)
    r2 = [p.shape[1] for p in parts]
    off = [sum(r2[:g]) for g in range(ng)]
    tot = sum(r2)

    def body(*refs):
        ps = refs[:ng]
        own_ref, rec_ref, ssems, rsems, lsem = refs[ng:]
        x, y, c = _me()
        j = 2 * x + y
        for g in range(ng):
            pltpu.make_async_copy(ps[g].at[j], own_ref.at[pl.ds(off[g], r2[g])], lsem).start()
        for k, (fx, fy) in enumerate(_CHIP_RELS):
            px, py = _flip(x, fx), _flip(y, fy)
            for g in range(ng):
                for st, sz in _row_pieces(r2[g], 2):
                    _remote(ps[g].at[2 * px + py, pl.ds(st, sz)], rec_ref.at[k, pl.ds(off[g] + st, sz)],
                            ssems.at[k], rsems.at[k], (px, py, c)).start()
        for k in range(3):
            _remote(rec_ref.at[k], rec_ref.at[k], ssems.at[k], rsems.at[k], (x, y, c)).wait()
        pltpu.make_async_copy(own_ref, own_ref, lsem).wait()

    return pl.pallas_call(
        body, name="grad_chip_exchange",
        out_shape=(jax.ShapeDtypeStruct((tot, D_MODEL), BF16), jax.ShapeDtypeStruct((3, tot, D_MODEL), BF16)),
        in_specs=[ANY] * ng, out_specs=(ANY, ANY),
        scratch_shapes=[pltpu.SemaphoreType.DMA((3,)), pltpu.SemaphoreType.DMA((3,)), pltpu.SemaphoreType.DMA(())],
    )(*parts)


def _final_reduce(own, rec, name):
    r2 = own.shape[0]
    pieces = _row_pieces(r2, DMA_SPLIT)

    def body(own_ref, rec_ref, o_ref, fbuf, ssem, rsem, lsem):
        x, y, c = _me()
        sib = (x, y, 1 - c)
        for st, sz in pieces:
            rows = slice(st, st + sz)
            fbuf[rows, :] = (own_ref[rows, :].astype(F32) + rec_ref[0, rows, :].astype(F32)
                             + rec_ref[1, rows, :].astype(F32) + rec_ref[2, rows, :].astype(F32))
            pltpu.make_async_copy(fbuf.at[pl.ds(st, sz)], o_ref.at[c, pl.ds(st, sz)], lsem).start()
            _remote(fbuf.at[pl.ds(st, sz)], o_ref.at[c, pl.ds(st, sz)], ssem, rsem, sib).start()
        _remote(fbuf, o_ref.at[c], ssem, rsem, sib).wait()
        pltpu.make_async_copy(fbuf, o_ref.at[c], lsem).wait()

    return pl.pallas_call(
        body, name=name, out_shape=jax.ShapeDtypeStruct((2, r2, D_MODEL), F32),
        in_specs=[VMEM_WHOLE, VMEM_WHOLE], out_specs=ANY,
        scratch_shapes=[pltpu.VMEM((r2, D_MODEL), F32), pltpu.SemaphoreType.DMA(()), pltpu.SemaphoreType.DMA(()),
                        pltpu.SemaphoreType.DMA(())],
        compiler_params=pltpu.CompilerParams(vmem_limit_bytes=VMEM_BIG),
    )(own, rec)


def _allreduce_small(vec):
    def body(v_ref, o_ref, buf, send_sems, recv_sems):
        x, y, c = _me()
        me = 4 * x + 2 * y + c
        buf[me] = v_ref[...]
        copies = []
        for k in range(1, 8):
            peer = (_flip(x, (k >> 2) & 1), _flip(y, (k >> 1) & 1), _flip(c, k & 1))
            cp = _remote(v_ref, buf.at[me], send_sems.at[k - 1], recv_sems.at[k - 1], peer)
            cp.start()
            copies.append(cp)
        for cp in copies:
            cp.wait()
        acc = buf[0]
        for i in range(1, 8):
            acc = acc + buf[i]
        o_ref[...] = acc

    vm = pl.BlockSpec(memory_space=pltpu.VMEM)
    return pl.pallas_call(
        body, name="allreduce_small", out_shape=jax.ShapeDtypeStruct(vec.shape, vec.dtype),
        in_specs=[vm], out_specs=vm,
        scratch_shapes=[pltpu.VMEM((8,) + vec.shape, vec.dtype), pltpu.SemaphoreType.DMA((7,)),
                        pltpu.SemaphoreType.DMA((7,))],
    )(vec)


class _GatherRider:
    def __init__(self, shards):
        self.inputs = list(shards)
        nt = len(shards)
        self.out_shape = [jax.ShapeDtypeStruct((N_CHIPS,) + s.shape, s.dtype) for s in shards]
        self.scratch = [pltpu.SemaphoreType.DMA((3 * nt,)), pltpu.SemaphoreType.DMA((3 * nt,)),
                        pltpu.SemaphoreType.DMA((nt,))]

    def _copies(self, srcs, outs, sems):
        ici_s, ici_r, loc = sems
        x, y, c = _me()
        j = 2 * x + y
        local = [pltpu.make_async_copy(srcs[t], outs[t].at[j], loc.at[t]) for t in range(len(srcs))]
        remote = []
        for k, (fx, fy) in enumerate(_CHIP_RELS):
            peer = (_flip(x, fx), _flip(y, fy), c)
            for t in range(len(srcs)):
                remote.append(_remote(srcs[t], outs[t].at[j], ici_s.at[3 * t + k], ici_r.at[3 * t + k], peer))
        return local, remote

    def start(self, srcs, outs, sems):
        local, remote = self._copies(srcs, outs, sems)
        for cp in local + remote:
            cp.start()

    def finish(self, srcs, outs, sems):
        local, remote = self._copies(srcs, outs, sems)
        for cp in remote + local:
            cp.wait()


class _ExchangeRider:
    def __init__(self, parts):
        self.inputs = list(parts)
        self.r2 = [p.shape[1] for p in parts]
        self.off = [sum(self.r2[:g]) for g in range(len(parts))]
        tot = sum(self.r2)
        self.out_shape = [jax.ShapeDtypeStruct((tot, D_MODEL), BF16), jax.ShapeDtypeStruct((3, tot, D_MODEL), BF16)]
        self.scratch = [pltpu.SemaphoreType.DMA((3,)), pltpu.SemaphoreType.DMA((3,)), pltpu.SemaphoreType.DMA(())]

    def start(self, ps, outs, sems):
        own_ref, rec_ref = outs
        ssems, rsems, lsem = sems
        x, y, c = _me()
        j = 2 * x + y
        for g in range(len(ps)):
            pltpu.make_async_copy(ps[g].at[j], own_ref.at[pl.ds(self.off[g], self.r2[g])], lsem).start()
        for k, (fx, fy) in enumerate(_CHIP_RELS):
            px, py = _flip(x, fx), _flip(y, fy)
            for g in range(len(ps)):
                for st, sz in _row_pieces(self.r2[g], 2):
                    _remote(ps[g].at[2 * px + py, pl.ds(st, sz)], rec_ref.at[k, pl.ds(self.off[g] + st, sz)],
                            ssems.at[k], rsems.at[k], (px, py, c)).start()

    def finish(self, ps, outs, sems):
        own_ref, rec_ref = outs
        ssems, rsems, lsem = sems
        x, y, c = _me()
        for k in range(3):
            _remote(rec_ref.at[k], rec_ref.at[k], ssems.at[k], rsems.at[k], (x, y, c)).wait()
        pltpu.make_async_copy(own_ref, own_ref, lsem).wait()


def _pallas(body, args, *, name, grid, in_specs, out_specs, out_shape, scratch_shapes=(), sem=None, vmem=None,
            rider=None):
    if rider is None:
        res = pl.pallas_call(body, name=name, grid=grid, in_specs=list(in_specs), out_specs=tuple(out_specs),
                             out_shape=tuple(out_shape), scratch_shapes=list(scratch_shapes),
                             compiler_params=_params(sem, vmem))(*args)
        return tuple(res), ()
    n_in, n_out, n_sc = len(in_specs), len(out_shape), len(scratch_shapes)
    r_in, r_out = len(rider.inputs), len(rider.out_shape)

    def wrapped(*refs):
        ins, rins = refs[:n_in], refs[n_in:n_in + r_in]
        p = n_in + r_in
        outs, routs = refs[p:p + n_out], refs[p + n_out:p + n_out + r_out]
        p += n_out + r_out
        scr, rsems = refs[p:p + n_sc], refs[p + n_sc:]
        first = pl.program_id(0) == 0
        last = pl.program_id(0) == grid[0] - 1
        for a in range(1, len(grid)):
            first = first & (pl.program_id(a) == 0)
            last = last & (pl.program_id(a) == grid[a] - 1)

        @pl.when(first)
        def _():
            rider.start(rins, routs, rsems)

        body(*ins, *outs, *scr)

        @pl.when(last)
        def _():
            rider.finish(rins, routs, rsems)

    res = pl.pallas_call(
        wrapped, name=name, grid=grid, in_specs=list(in_specs) + [ANY] * r_in,
        out_specs=tuple(out_specs) + (ANY,) * r_out, out_shape=tuple(out_shape) + tuple(rider.out_shape),
        scratch_shapes=list(scratch_shapes) + rider.scratch,
        compiler_params=_params(("arbitrary",) * len(grid), vmem))(*args, *rider.inputs)
    return tuple(res[:n_out]), tuple(res[n_out:])


def _ffn_fwd(h, gain, wgt, wut, wd, rider=None):
    t = h.shape[0]
    fc = D_FF // FF_CHUNKS

    def body(h_ref, gain_ref, wg_hbm, wu_hbm, wd_hbm, hout_ref, n_ref, g_ref, u_ref, wg_v, wu_v, wd_v):
        @pl.when(pl.program_id(0) == 0)
        def _():
            pltpu.sync_copy(wg_hbm, wg_v)
            pltpu.sync_copy(wu_hbm, wu_v)
            pltpu.sync_copy(wd_hbm, wd_v)

        hh = h_ref[...]
        r = lax.rsqrt(jnp.mean(hh * hh, axis=-1, keepdims=True) + EPS)
        n = (hh * r * gain_ref[...]).astype(BF16)
        n_ref[...] = n
        acc = jnp.zeros((TM, D_MODEL), F32)
        for ci in range(FF_CHUNKS):
            sl = slice(ci * fc, (ci + 1) * fc)
            g = _dot_nt(n, wg_v[sl, :])
            u = _dot_nt(n, wu_v[sl, :])
            g_ref[:, sl] = g.astype(BF16)
            u_ref[:, sl] = u.astype(BF16)
            a = (g * _sigmoid(g) * u).astype(BF16)
            acc = acc + _dot(a, wd_v[sl, :])
        hout_ref[...] = hh + 0.5 * acc

    row = lambda w: pl.BlockSpec((TM, w), lambda i: (i, 0))
    wv = pltpu.VMEM((D_FF, D_MODEL), BF16)
    return _pallas(
        body, (h, gain, wgt, wut, wd), name="ffn_fwd", grid=(t // TM,),
        out_shape=(jax.ShapeDtypeStruct((t, D_MODEL), F32), jax.ShapeDtypeStruct((t, D_MODEL), BF16),
                   jax.ShapeDtypeStruct((t, D_FF), BF16), jax.ShapeDtypeStruct((t, D_FF), BF16)),
        in_specs=[row(D_MODEL), pl.BlockSpec((1, D_MODEL), lambda i: (0, 0)), ANY, ANY, ANY],
        out_specs=(row(D_MODEL), row(D_MODEL), row(D_FF), row(D_FF)),
        scratch_shapes=[wv, wv, wv], sem=("arbitrary",), vmem=VMEM_BIG, rider=rider)


def _ffn_bwd(dhout, h, gain, g, u, wgt, wut, wd):
    t = h.shape[0]
    tm = TM_BWD
    fc = D_FF // FF_CHUNKS

    def body(dho_ref, h_ref, gain_ref, g_ref, u_ref, wg_hbm, wu_hbm, wd_hbm,
             dh_ref, dg_ref, du_ref, a_ref, df_ref, gg_ref, wg_v, wu_v, wd_v):
        @pl.when(pl.program_id(0) == 0)
        def _():
            pltpu.sync_copy(wg_hbm, wg_v)
            pltpu.sync_copy(wu_hbm, wu_v)
            pltpu.sync_copy(wd_hbm, wd_v)
            gg_ref[...] = jnp.zeros_like(gg_ref)

        dho = dho_ref[...]
        df = (0.5 * dho).astype(BF16)
        df_ref[...] = df
        dn = jnp.zeros((tm, D_MODEL), F32)
        for ci in range(FF_CHUNKS):
            sl = slice(ci * fc, (ci + 1) * fc)
            da = _dot_nt(df, wd_v[sl, :])
            gv = g_ref[:, sl].astype(F32)
            uv = u_ref[:, sl].astype(F32)
            sg = _sigmoid(gv)
            silu = gv * sg
            dg = (da * uv * (sg * (1.0 + gv * (1.0 - sg)))).astype(BF16)
            du = (da * silu).astype(BF16)
            dg_ref[:, sl] = dg
            du_ref[:, sl] = du
            a_ref[:, sl] = (silu * uv).astype(BF16)
            dn = dn + _dot(dg, wg_v[sl, :]) + _dot(du, wu_v[sl, :])
        hh = h_ref[...]
        r = lax.rsqrt(jnp.mean(hh * hh, axis=-1, keepdims=True) + EPS)
        hn = hh * r
        gg_ref[...] += jnp.sum(dn * hn, axis=0, keepdims=True)
        dng = dn * gain_ref[...]
        dh_ref[...] = dho + r * (dng - hn * jnp.mean(dng * hn, axis=-1, keepdims=True))

    row = lambda w: pl.BlockSpec((tm, w), lambda i: (i, 0))
    vec = pl.BlockSpec((1, D_MODEL), lambda i: (0, 0))
    wv = pltpu.VMEM((D_FF, D_MODEL), BF16)
    return pl.pallas_call(
        body, name="ffn_bwd", grid=(t // tm,),
        out_shape=(jax.ShapeDtypeStruct((t, D_MODEL), F32), jax.ShapeDtypeStruct((t, D_FF), BF16),
                   jax.ShapeDtypeStruct((t, D_FF), BF16), jax.ShapeDtypeStruct((t, D_FF), BF16),
                   jax.ShapeDtypeStruct((t, D_MODEL), BF16), jax.ShapeDtypeStruct((1, D_MODEL), F32)),
        in_specs=[row(D_MODEL), row(D_MODEL), vec, row(D_FF), row(D_FF), ANY, ANY, ANY],
        out_specs=(row(D_MODEL), row(D_FF), row(D_FF), row(D_FF), row(D_MODEL), vec),
        scratch_shapes=[wv, wv, wv],
        compiler_params=_params(("arbitrary",), VMEM_BIG),
    )(dhout, h, gain, g, u, wgt, wut, wd)


def _wgrad(lhs, rhs, rb, with_colsum=False, name="wgrad"):
    t, k = lhs.shape
    n = rhs.shape[1]
    tk = 512
    nt = t // tk

    def body(l_ref, r_ref, o_ref, *rest):
        acc = rest[-1]
        ti = pl.program_id(1)

        @pl.when(ti == 0)
        def _():
            acc[...] = jnp.zeros_like(acc)
            if with_colsum:
                rest[0][...] = jnp.zeros_like(rest[0])

        acc[...] += _dot_tn(l_ref[...], r_ref[...])
        if with_colsum:
            rest[0][...] += jnp.sum(l_ref[...].astype(F32), axis=0, keepdims=True)

        @pl.when(ti == nt - 1)
        def _():
            o_ref[...] = acc[...].astype(BF16)

    out_shape = [jax.ShapeDtypeStruct((k, n), BF16)]
    out_specs = [pl.BlockSpec((rb, n), lambda j, i: (j, 0))]
    if with_colsum:
        out_shape.append(jax.ShapeDtypeStruct((1, k), F32))
        out_specs.append(pl.BlockSpec((1, rb), lambda j, i: (0, j)))
    res = pl.pallas_call(
        body, name=name, grid=(k // rb, nt), out_shape=tuple(out_shape),
        in_specs=[pl.BlockSpec((tk, rb), lambda j, i: (i, j)), pl.BlockSpec((tk, n), lambda j, i: (i, 0))],
        out_specs=tuple(out_specs), scratch_shapes=[pltpu.VMEM((rb, n), F32)],
        compiler_params=_params(("arbitrary", "arbitrary"), VMEM_BIG),
    )(lhs, rhs)
    return res if with_colsum else res[0]


def _lane_blocks(nseq, seq, nblk, tm=TM):
    spt = seq // tm
    return pl.BlockSpec((1, nblk, tm, 128), lambda i: (i // spt, 0, i % spt, 0))


def _inproj_fwd(h, gain, wint, b_in, nseq, rider=None):
    t = h.shape[0]
    seq = t // nseq
    half_a = ZA_W // 2
    pieces = ((0, half_a, 0, 0), (half_a, half_a, 0, half_a), (ZA_W, ZB_W, 1, 0), (ZA_W + ZB_W, 1024, 2, 0),
              (ZA_W + ZB_W + 1024, 1024, 2, 1024))

    def body(h_ref, gain_ref, w_hbm, b_ref, u_ref, za_ref, zb_ref, zg_ref, w_v):
        @pl.when(pl.program_id(0) == 0)
        def _():
            pltpu.sync_copy(w_hbm, w_v)

        hh = h_ref[...]
        r = lax.rsqrt(jnp.mean(hh * hh, axis=-1, keepdims=True) + EPS)
        un = (hh * r * gain_ref[...]).astype(BF16)
        u_ref[...] = un
        outs = (None, zb_ref, zg_ref)
        for c0, cw, oi, o0 in pieces:
            val = _dot_nt(un, w_v[c0:c0 + cw, :]) + b_ref[:, c0:c0 + cw]
            if oi == 0:
                for cb in range(cw // 128):
                    za_ref[0, o0 // 128 + cb] = val[:, cb * 128:(cb + 1) * 128]
            else:
                outs[oi][:, o0:o0 + cw] = val.astype(BF16)

    row = lambda w: pl.BlockSpec((TM, w), lambda i: (i, 0))
    return _pallas(
        body, (h, gain, wint, b_in), name="inproj_fwd", grid=(t // TM,),
        out_shape=(jax.ShapeDtypeStruct((t, D_MODEL), BF16), jax.ShapeDtypeStruct((nseq, ZA_W // 128, seq, 128), F32),
                   jax.ShapeDtypeStruct((t, ZB_W), BF16), jax.ShapeDtypeStruct((t, 2 * D_MODEL), BF16)),
        in_specs=[row(D_MODEL), pl.BlockSpec((1, D_MODEL), lambda i: (0, 0)), ANY,
                  pl.BlockSpec((1, D_IN), lambda i: (0, 0))],
        out_specs=(row(D_MODEL), _lane_blocks(nseq, seq, ZA_W // 128), row(ZB_W), row(2 * D_MODEL)),
        scratch_shapes=[pltpu.VMEM((D_IN, D_MODEL), BF16)], sem=("arbitrary",), vmem=VMEM_BIG, rider=rider)


def _inproj_bwd(dz, dh2, h, gain, wint, rider=None):
    t = h.shape[0]
    nc = 5
    cw = D_IN // nc

    def body(dz_ref, dh2_ref, h_ref, gain_ref, w_hbm, dh_ref, gg_ref, w_v):
        @pl.when(pl.program_id(0) == 0)
        def _():
            pltpu.sync_copy(w_hbm, w_v)
            gg_ref[...] = jnp.zeros_like(gg_ref)

        du = jnp.zeros((TM, D_MODEL), F32)
        for ci in range(nc):
            sl = slice(ci * cw, (ci + 1) * cw)
            du = du + _dot(dz_ref[:, sl], w_v[sl, :])
        hh = h_ref[...]
        r = lax.rsqrt(jnp.mean(hh * hh, axis=-1, keepdims=True) + EPS)
        hn = hh * r
        gg_ref[...] += jnp.sum(du * hn, axis=0, keepdims=True)
        dng = du * gain_ref[...]
        dh_ref[...] = dh2_ref[...] + r * (dng - hn * jnp.mean(dng * hn, axis=-1, keepdims=True))

    row = lambda w: pl.BlockSpec((TM, w), lambda i: (i, 0))
    vec = pl.BlockSpec((1, D_MODEL), lambda i: (0, 0))
    return _pallas(
        body, (dz, dh2, h, gain, wint), name="inproj_bwd", grid=(t // TM,),
        out_shape=(jax.ShapeDtypeStruct((t, D_MODEL), F32), jax.ShapeDtypeStruct((1, D_MODEL), F32)),
        in_specs=[row(D_IN), row(D_MODEL), row(D_MODEL), vec, ANY],
        out_specs=(row(D_MODEL), vec),
        scratch_shapes=[pltpu.VMEM((D_IN, D_MODEL), BF16)], sem=("arbitrary",), vmem=VMEM_BIG, rider=rider)


def _head_sum_matrix(w):
    i = lax.broadcasted_iota(jnp.int32, (w, w), 0) // HEAD_DIM
    j = lax.broadcasted_iota(jnp.int32, (w, w), 1) // HEAD_DIM
    return (i == j).astype(F32)


def _merge_fwd(o0, o1, o2, l0, l1, l2, yb, zg, h1, wat, wbt, wout):
    t = h1.shape[0]
    nseq, _, seq, _ = o0.shape

    def body(o0_ref, o1_ref, o2_ref, l0_ref, l1_ref, l2_ref, yb_ref, ga_ref, gb_ref, h1_ref, wa_ref, wb_ref, wo_ref,
             h2_ref, y_ref, lt_ref, pa_ref, pb_ref, mg_ref):
        wide = lambda ref: jnp.concatenate([ref[0, 0], ref[0, 1]], axis=1)
        la, lb, lc = wide(l0_ref), wide(l1_ref), wide(l2_ref)
        mx = jnp.maximum(jnp.maximum(la, lb), lc)
        ea, eb, ec = jnp.exp(la - mx), jnp.exp(lb - mx), jnp.exp(lc - mx)
        den = ea + eb + ec
        y = (ea * wide(o0_ref) + eb * wide(o1_ref) + ec * wide(o2_ref)) / den
        lt = mx + jnp.log(den)
        lt_ref[0, 0] = lt[:, :128]
        lt_ref[0, 1] = lt[:, 128:]
        yb16 = y.astype(BF16)
        y_ref[...] = yb16
        pa = _dot_nt(yb16, wa_ref[...])
        pb = _dot_nt(yb_ref[...], wb_ref[...])
        pa_ref[...] = pa.astype(BF16)
        pb_ref[...] = pb.astype(BF16)
        mg = (_sigmoid(ga_ref[...].astype(F32)) * pa + _sigmoid(gb_ref[...].astype(F32)) * pb).astype(BF16)
        mg_ref[...] = mg
        h2_ref[...] = h1_ref[...] + _dot(mg, wo_ref[...])

    row = lambda w: pl.BlockSpec((TM, w), lambda i: (i, 0))
    full = lambda a: pl.BlockSpec(a.shape, lambda i: (0, 0))
    gate = lambda cb: pl.BlockSpec((TM, D_MODEL), lambda i: (i, cb))
    return pl.pallas_call(
        body, name="merge_fwd", grid=(t // TM,),
        out_shape=(jax.ShapeDtypeStruct((t, D_MODEL), F32), jax.ShapeDtypeStruct((t, GW), BF16),
                   jax.ShapeDtypeStruct((nseq, 2, seq, 128), F32), jax.ShapeDtypeStruct((t, D_MODEL), BF16),
                   jax.ShapeDtypeStruct((t, D_MODEL), BF16), jax.ShapeDtypeStruct((t, D_MODEL), BF16)),
        in_specs=[_lane_blocks(nseq, seq, 2)] * 6 + [row(2 * GW), gate(0), gate(1), row(D_MODEL), full(wat), full(wbt),
                                                     full(wout)],
        out_specs=(row(D_MODEL), row(GW), _lane_blocks(nseq, seq, 2), row(D_MODEL), row(D_MODEL), row(D_MODEL)),
        compiler_params=_params(("parallel",), VMEM_BIG),
    )(o0, o1, o2, l0, l1, l2, yb, zg, zg, h1, wat, wbt, wout)


def _merge_bwd(dh2, pa, pb, zg, y, yb, wat, wbt, wout, nseq, rider=None):
    t = dh2.shape[0]

    def body(dh2_ref, pa_ref, pb_ref, ga_ref, gb_ref, y_ref, yb_ref, wa_ref, wb_ref, wo_ref,
             dpa_ref, dpb_ref, dga_ref, dgb_ref, dya_ref, dyb_ref, dh2b_ref, ca_ref, cb_ref):
        d16 = dh2_ref[...].astype(BF16)
        dh2b_ref[...] = d16
        dm = _dot_nt(d16, wo_ref[...])
        sa = _sigmoid(ga_ref[...].astype(F32))
        sb = _sigmoid(gb_ref[...].astype(F32))
        dpa = (dm * sa).astype(BF16)
        dpb = (dm * sb).astype(BF16)
        dpa_ref[...] = dpa
        dpb_ref[...] = dpb
        dga_ref[...] = (dm * pa_ref[...].astype(F32) * sa * (1.0 - sa)).astype(BF16)
        dgb_ref[...] = (dm * pb_ref[...].astype(F32) * sb * (1.0 - sb)).astype(BF16)
        dya = _dot(dpa, wa_ref[...])
        dyb = _dot(dpb, wb_ref[...])
        dya_ref[0, 0] = dya[:, :128]
        dya_ref[0, 1] = dya[:, 128:]
        dyb_ref[...] = dyb.astype(BF16)
        hp = lax.Precision.HIGHEST
        ca = jnp.dot(dya * y_ref[...].astype(F32), _head_sum_matrix(GW), precision=hp, preferred_element_type=F32)
        ca_ref[0, 0] = ca[:, :128]
        ca_ref[0, 1] = ca[:, 128:]
        cb_ref[...] = jnp.dot(dyb * yb_ref[...].astype(F32), _head_sum_matrix(2 * GW), precision=hp,
                              preferred_element_type=F32)

    row = lambda w: pl.BlockSpec((TM, w), lambda i: (i, 0))
    full = lambda a: pl.BlockSpec(a.shape, lambda i: (0, 0))
    gate = lambda cb: pl.BlockSpec((TM, D_MODEL), lambda i: (i, cb))
    bf = lambda w: jax.ShapeDtypeStruct((t, w), BF16)
    lanes = jax.ShapeDtypeStruct((nseq, 2, t // nseq, 128), F32)
    lane_spec = _lane_blocks(nseq, t // nseq, 2)
    return _pallas(
        body, (dh2, pa, pb, zg, zg, y, yb, wat, wbt, wout), name="merge_bwd", grid=(t // TM,),
        out_shape=(bf(D_MODEL), bf(D_MODEL), bf(D_MODEL), bf(D_MODEL), lanes, bf(2 * GW), bf(D_MODEL),
                   lanes, jax.ShapeDtypeStruct((t, 2 * GW), F32)),
        in_specs=[row(D_MODEL), row(D_MODEL), row(D_MODEL), gate(0), gate(1), row(GW), row(2 * GW),
                  full(wat), full(wbt), full(wout)],
        out_specs=(row(D_MODEL), row(D_MODEL), row(D_MODEL), row(D_MODEL), lane_spec, row(2 * GW), row(D_MODEL),
                   lane_spec, row(2 * GW)),
        sem=("parallel",), vmem=VMEM_BIG, rider=rider)


def _loss_head(h3, gain, tgt):
    t = h3.shape[0]

    def body(h_ref, gain_ref, t_ref, dh_ref, loss_ref, gg_ref):
        @pl.when(pl.program_id(0) == 0)
        def _():
            loss_ref[...] = jnp.zeros_like(loss_ref)
            gg_ref[...] = jnp.zeros_like(gg_ref)

        hh = h_ref[...]
        r = lax.rsqrt(jnp.mean(hh * hh, axis=-1, keepdims=True) + EPS)
        hn = hh * r
        err = hn * gain_ref[...] - t_ref[...]
        part = jnp.sum(jnp.sum(err * err, axis=1, keepdims=True), axis=0, keepdims=True)
        loss_ref[...] += (0.5 / D_MODEL) * part
        dy = err * (1.0 / D_MODEL)
        gg_ref[...] += jnp.sum(dy * hn, axis=0, keepdims=True)
        dng = dy * gain_ref[...]
        dh_ref[...] = r * (dng - hn * jnp.mean(dng * hn, axis=-1, keepdims=True))

    row = pl.BlockSpec((TM, D_MODEL), lambda i: (i, 0))
    vec = pl.BlockSpec((1, D_MODEL), lambda i: (0, 0))
    return pl.pallas_call(
        body, name="loss_head", grid=(t // TM,),
        out_shape=(jax.ShapeDtypeStruct((t, D_MODEL), F32), jax.ShapeDtypeStruct((8, 128), F32),
                   jax.ShapeDtypeStruct((1, D_MODEL), F32)),
        in_specs=[row, vec, row], out_specs=(row, pl.BlockSpec((8, 128), lambda i: (0, 0)), vec),
        compiler_params=_params(("arbitrary",)),
    )(h3, gain, tgt)


def _lane_head(rows):
    return lax.broadcasted_iota(jnp.int32, (rows, GW), 1) // HEAD_DIM


def _kv_expand_matrix(r):
    ci = lax.broadcasted_iota(jnp.int32, (2 * HEAD_DIM, GW), 0)
    ji = lax.broadcasted_iota(jnp.int32, (2 * HEAD_DIM, GW), 1)
    return (ci == (ji % HEAD_DIM) + HEAD_DIM * r).astype(BF16)


def _block_rows(row0, stride, ib):
    start = row0 + (stride * BLOCK) * ib
    if stride > 1:
        return pl.ds(start, BLOCK, stride=stride)
    return pl.ds(pl.multiple_of(start, BLOCK), BLOCK)


def _stack_heads(x, lane_head):
    return jnp.concatenate([jnp.where(lane_head == h, x, jnp.zeros_like(x)) for h in range(4)], axis=0)


def _unstack_heads(x4, lane_head):
    out = jnp.zeros((BLOCK, GW), F32)
    for h in range(4):
        out = jnp.where(lane_head == h, x4[h * BLOCK:(h + 1) * BLOCK], out)
    return out


def _load_rows(ref, rows, split):
    if split:
        return jnp.concatenate([ref[0, 0, rows, :], ref[0, 1, rows, :]], axis=1)
    return ref[0, rows, :]


def _store_rows(ref, rows, val, split):
    if split:
        ref[0, 0, rows, :] = val[:, :128]
        ref[0, 1, rows, :] = val[:, 128:]
    else:
        ref[0, rows, :] = val


def _attn_fwd(q_arr, k_arr, v_arr, bias, sink, *, grid, seq, stride, kvw, split, q_spec, k_spec, v_spec, bias_map,
              sink_map, o_spec, has_sink, o_shape, o_dtype, name, rider=None):
    nb = seq // stride // BLOCK
    scale = HEAD_DIM ** -0.5

    def body(q_ref, k_ref, v_ref, bias_ref, sink_ref, o_ref, lse_ref):
        rr = pl.program_id(1)
        row0 = rr if stride > 1 else 0
        lane_head = _lane_head(BLOCK)
        expand = _kv_expand_matrix(rr) if kvw != GW else None
        if has_sink:
            sk4 = jnp.concatenate([jnp.broadcast_to(sink_ref[0, h:h + 1, 0:1], (BLOCK, 1)) for h in range(4)], axis=0)

        def load(ref, ib):
            return _load_rows(ref, _block_rows(row0, stride, ib), split).astype(BF16)

        def block(ib, first):
            q4 = _stack_heads(load(q_ref, ib), lane_head)
            if first:
                kc, vc = load(k_ref, ib), load(v_ref, ib)
                b4 = bias_ref[:, :, BLOCK:].reshape(4 * BLOCK, BLOCK)
            else:
                kc = jnp.concatenate([load(k_ref, ib - 1), load(k_ref, ib)], axis=0)
                vc = jnp.concatenate([load(v_ref, ib - 1), load(v_ref, ib)], axis=0)
                b4 = bias_ref[...].reshape(4 * BLOCK, 2 * BLOCK)
            if expand is not None:
                kc = _dot(kc, expand).astype(BF16)
                vc = _dot(vc, expand).astype(BF16)
            s = _dot_nt(q4, kc) * scale + b4
            m = jnp.max(s, axis=-1, keepdims=True)
            if has_sink:
                m = jnp.maximum(m, sk4)
            p = jnp.exp(s - m)
            l = jnp.sum(p, axis=-1, keepdims=True)
            if has_sink:
                l = l + jnp.exp(sk4 - m)
            o4 = _dot(p.astype(BF16), vc) / l
            rows = _block_rows(row0, stride, ib)
            _store_rows(o_ref, rows, _unstack_heads(o4, lane_head).astype(o_dtype), split)
            _store_rows(lse_ref, rows, _unstack_heads(m + jnp.log(l), lane_head), split)

        block(0, True)
        if nb > 1:
            def step(i, carry):
                block(i, False)
                return carry
            lax.fori_loop(1, nb, step, 0)

    return _pallas(
        body, (q_arr, k_arr, v_arr, bias, sink), name=name, grid=grid,
        out_shape=(jax.ShapeDtypeStruct(o_shape, o_dtype), jax.ShapeDtypeStruct(o_shape, F32)),
        in_specs=[q_spec, k_spec, v_spec,
                  pl.BlockSpec((4, BLOCK, 2 * BLOCK), bias_map), pl.BlockSpec((1, 4, 128), sink_map)],
        out_specs=(o_spec, o_spec),
        sem=("arbitrary", "arbitrary"), vmem=VMEM_BIG, rider=rider)


def _attn_bwd(q_arr, k_arr, v_arr, bias, sink, dy, cc, lse, *, grid, seq, stride, kvw, split, q_spec, k_spec, v_spec,
              bias_map, sink_map, o_spec, kv_out_spec, has_sink, n_bias, dq_shape, dkv_shape, g_dtype, name):
    ln = seq // stride
    nb = ln // BLOCK
    scale = HEAD_DIM ** -0.5

    def body(q_ref, k_ref, v_ref, bias_ref, sink_ref, dy_ref, c_ref, lse_ref,
             dq_ref, dk_ref, dv_ref, db_ref, dsk_ref, dk_acc, dv_acc, dk_half, dv_half):
        rr = pl.program_id(1)
        row0 = rr if stride > 1 else 0

        @pl.when((pl.program_id(0) == 0) & (rr == 0))
        def _():
            db_ref[...] = jnp.zeros_like(db_ref)
            dsk_ref[...] = jnp.zeros_like(dsk_ref)

        dk_acc[...] = jnp.zeros_like(dk_acc)
        dv_acc[...] = jnp.zeros_like(dv_acc)
        lane_head = _lane_head(BLOCK)
        expand = _kv_expand_matrix(rr) if kvw != GW else None
        hb = 4 * rr if n_bias == 8 else 0

        def load(ref, ib):
            return _load_rows(ref, _block_rows(row0, stride, ib), split)

        def head_col(x):
            return jnp.concatenate([x[:, h * HEAD_DIM:h * HEAD_DIM + 1] for h in range(4)], axis=0)

        def block(ib, first):
            q4 = _stack_heads(load(q_ref, ib).astype(BF16), lane_head)
            dy4 = _stack_heads(load(dy_ref, ib).astype(BF16), lane_head)
            c4 = head_col(load(c_ref, ib))
            l4 = head_col(load(lse_ref, ib))
            if first:
                kc, vc = load(k_ref, ib).astype(BF16), load(v_ref, ib).astype(BF16)
                b4 = bias_ref[:, :, BLOCK:].reshape(4 * BLOCK, BLOCK)
                krows = pl.ds(0, BLOCK)
            else:
                kc = jnp.concatenate([load(k_ref, ib - 1), load(k_ref, ib)], axis=0).astype(BF16)
                vc = jnp.concatenate([load(v_ref, ib - 1), load(v_ref, ib)], axis=0).astype(BF16)
                b4 = bias_ref[...].reshape(4 * BLOCK, 2 * BLOCK)
                krows = pl.ds(pl.multiple_of((ib - 1) * BLOCK, BLOCK), 2 * BLOCK)
            if expand is not None:
                kc = _dot(kc, expand).astype(BF16)
                vc = _dot(vc, expand).astype(BF16)
            nk = BLOCK if first else 2 * BLOCK
            p = jnp.exp(_dot_nt(q4, kc) * scale + b4 - l4)
            ds = p * (_dot_nt(dy4, vc) - c4)
            ds3 = ds.reshape(4, BLOCK, nk)
            if n_bias == 8:
                if first:
                    db_ref[pl.ds(hb, 4), :, BLOCK:] += ds3
                else:
                    db_ref[pl.ds(hb, 4)] += ds3
            elif first:
                db_ref[:, :, BLOCK:] += ds3
            else:
                db_ref[...] += ds3
            ds16 = ds.astype(BF16)
            dq = _unstack_heads(_dot(ds16, kc), lane_head) * scale
            _store_rows(dq_ref, _block_rows(row0, stride, ib), dq.astype(g_dtype), split)
            dk_acc[krows, :] += _dot_tn(ds16, q4) * scale
            dv_acc[krows, :] += _dot_tn(p.astype(BF16), dy4)
            if has_sink:
                for h in range(4):
                    hs = slice(h * BLOCK, (h + 1) * BLOCK)
                    sk = sink_ref[0, h:h + 1, 0:1]
                    val = -jnp.sum(jnp.exp(sk - l4[hs]) * c4[hs], axis=0, keepdims=True)
                    dsk_ref[hb + h] += jnp.broadcast_to(val, (8, 128))

        block(0, True)
        if nb > 1:
            def step(i, carry):
                block(i, False)
                return carry
            lax.fori_loop(1, nb, step, 0)

        if kvw == GW:
            all_rows = pl.ds(row0, ln, stride=stride) if stride > 1 else pl.ds(0, ln)
            _store_rows(dk_ref, all_rows, dk_acc[...].astype(g_dtype), split)
            _store_rows(dv_ref, all_rows, dv_acc[...].astype(g_dtype), split)
        else:
            def fold(acc):
                t2 = acc[:, :2 * HEAD_DIM] + acc[:, 2 * HEAD_DIM:]
                t2 = t2 + pltpu.roll(t2, HEAD_DIM, 1)
                lane = lax.broadcasted_iota(jnp.int32, t2.shape, 1) // HEAD_DIM
                return jnp.where(lane == rr, t2, 0.0)

            @pl.when(rr == 0)
            def _():
                dk_half[...] = fold(dk_acc[...])
                dv_half[...] = fold(dv_acc[...])

            @pl.when(rr == 1)
            def _():
                dk_ref[0] = (dk_half[...] + fold(dk_acc[...])).astype(g_dtype)
                dv_ref[0] = (dv_half[...] + fold(dv_acc[...])).astype(g_dtype)

    return pl.pallas_call(
        body, name=name, grid=grid,
        out_shape=(jax.ShapeDtypeStruct(dq_shape, g_dtype), jax.ShapeDtypeStruct(dkv_shape, g_dtype),
                   jax.ShapeDtypeStruct(dkv_shape, g_dtype), jax.ShapeDtypeStruct((n_bias, BLOCK, 2 * BLOCK), F32),
                   jax.ShapeDtypeStruct((8, 8, 128), F32)),
        in_specs=[q_spec, k_spec, v_spec,
                  pl.BlockSpec((4, BLOCK, 2 * BLOCK), bias_map), pl.BlockSpec((1, 4, 128), sink_map),
                  o_spec, o_spec, o_spec],
        out_specs=(o_spec, kv_out_spec, kv_out_spec,
                   pl.BlockSpec((n_bias, BLOCK, 2 * BLOCK), lambda n, r: (0, 0, 0)),
                   pl.BlockSpec((8, 8, 128), lambda n, r: (0, 0, 0))),
        scratch_shapes=[pltpu.VMEM((ln, GW), F32), pltpu.VMEM((ln, GW), F32),
                        pltpu.VMEM((ln, 2 * HEAD_DIM), F32), pltpu.VMEM((ln, 2 * HEAD_DIM), F32)],
        compiler_params=_params(("arbitrary", "arbitrary"), VMEM_BIG),
    )(q_arr, k_arr, v_arr, bias, sink, dy, cc, lse)


def _bias_grad(ds_all, buckets):
    def body(ds_ref, bk_ref, o_ref):
        rows = lax.broadcasted_iota(jnp.int32, (N_BUCKETS, 128), 0)
        cols = lax.broadcasted_iota(jnp.int32, (N_BUCKETS, 128), 1)

        def per_bucket(b, acc):
            for h in range(20):
                gi = h // 4 if h < 12 else 3
                v = jnp.where(bk_ref[gi] == b, ds_ref[h], 0.0)
                v = jnp.sum(jnp.sum(v, axis=1, keepdims=True), axis=0, keepdims=True)
                acc = jnp.where((rows == b) & (cols == h), v, acc)
            return acc

        o_ref[...] = lax.fori_loop(0, N_BUCKETS, per_bucket, jnp.zeros((N_BUCKETS, 128), F32))

    vm = pl.BlockSpec(memory_space=pltpu.VMEM)
    return pl.pallas_call(body, name="bias_grad", out_shape=jax.ShapeDtypeStruct((N_BUCKETS, 128), F32),
                          in_specs=[vm, vm], out_specs=vm)(ds_all, buckets)


def _adamw(w, g, m, v, name):
    r, c = w.shape
    tr = r
    for cand in (256, 176, 128, 64, 32, 16, 8):
        if r % cand == 0:
            tr = cand
            break
    bc1 = 1.0 - ADAM_B1 ** ADAM_STEP
    bc2 = 1.0 - ADAM_B2 ** ADAM_STEP

    def body(w_ref, g_ref, m_ref, v_ref, d_ref, nm_ref, nv_ref):
        gv = g_ref[...]
        nm = ADAM_B1 * m_ref[...] + (1.0 - ADAM_B1) * gv
        nv = ADAM_B2 * v_ref[...] + (1.0 - ADAM_B2) * (gv * gv)
        nm_ref[...] = nm
        nv_ref[...] = nv
        d_ref[...] = -ADAM_LR * ((nm / bc1) / (jnp.sqrt(nv / bc2) + ADAM_EPS) + ADAM_WD * w_ref[...])

    spec = pl.BlockSpec((tr, c), lambda i: (i, 0))
    shp = jax.ShapeDtypeStruct((r, c), F32)
    return pl.pallas_call(body, name=name, grid=(r // tr,), out_shape=(shp, shp, shp),
                          in_specs=[spec] * 4, out_specs=(spec, spec, spec),
                          compiler_params=_params(("parallel",)))(w, g, m, v)


def _t5_bucket(dist):
    max_exact = N_BUCKETS // 2
    n = jnp.maximum(dist, 0)
    nf = jnp.maximum(n, 1).astype(F32)
    large = max_exact + (jnp.log(nf / max_exact) / math.log(MAX_DISTANCE / max_exact)
                         * (N_BUCKETS - max_exact)).astype(jnp.int32)
    large = jnp.minimum(large, N_BUCKETS - 1)
    return jnp.where(n < max_exact, n, large)


def _bias_tables(rel_bias):
    qi = jnp.arange(BLOCK)[:, None]
    ki = jnp.arange(2 * BLOCK)[None, :]
    dist = qi + BLOCK - ki
    specs = [(d, w // d, 4 * gi, 4 * gi + 4) for gi, (w, d) in enumerate(DIL_GROUPS)] + [(1, B_WINDOW - 1, 12, 20)]
    biases, buckets = [], []
    for stride, steps, h0, h1 in specs:
        valid = (dist >= 0) & (dist <= steps)
        bk = jnp.where(valid, _t5_bucket(dist * stride), -1).astype(jnp.int32)
        onehot = (bk[None, :, :] == jnp.arange(N_BUCKETS, dtype=jnp.int32)[:, None, None]).astype(F32)
        b = jnp.einsum("bqk,bh->hqk", onehot, rel_bias[:, h0:h1], precision=lax.Precision.HIGHEST)
        biases.append(jnp.where(valid[None], b, NEG))
        buckets.append(bk)
    return jnp.concatenate(biases, axis=0), jnp.stack(buckets, axis=0)


def _local_step(x, tgt, W, S, shards=None):
    nseq, seq, _ = x.shape
    t = nseq * seq
    xf = x.reshape(t, D_MODEL)
    bias_all, buckets = _bias_tables(S["rel_bias"])
    sink_b = jnp.broadcast_to(S["sinks"].reshape(2, 4, 1), (2, 4, 128)).astype(F32)
    sink_0 = jnp.zeros((1, 4, 128), F32)
    dist = shards is not None
    W = dict(W)
    G, GS, reduced = {}, {}, {}

    def put(keys, gathered):
        for k, g in zip(keys, gathered):
            W[k] = g.reshape(_FULL_SHAPE.get(k, (N_CHIPS * shards[k].shape[0], D_MODEL)))

    def gather_rider(keys):
        return _GatherRider([shards[k] for k in keys]) if dist else None

    def pair(keys):
        return _pair_reduce([G[k].reshape(N_CHIPS, 2, shards[k].shape[0] // 2, D_MODEL) for k in keys],
                            "grad_pair_reduce_" + keys[0])

    def finish(keys, own, rec):
        full = _final_reduce(own, rec, "grad_final_reduce_" + keys[0])
        off = 0
        for k in keys:
            r = shards[k].shape[0]
            reduced[k] = full[:, off:off + r // 2].reshape(r, D_MODEL)
            off += r // 2

    if dist:
        first = ("wgt1", "wut1", "wd1")
        put(first, _gather_rows([shards[k] for k in first]))
    keys = ("wint",)
    (h1, n1, g1, u1), ro = _ffn_fwd(xf, S["ffn1_norm"], W["wgt1"], W["wut1"], W["wd1"], rider=gather_rider(keys))
    put(keys, ro)
    keys = ("wout", "wat", "wbt")
    (un, za, zb, zg), ro = _inproj_fwd(h1, S["mix_norm"], W["wint"], S["b_in"], nseq, rider=gather_rider(keys))
    put(keys, ro)

    seq3 = lambda a: a.reshape(nseq, seq, a.shape[-1])
    zb3 = seq3(zb)
    pair_blk = lambda cb: pl.BlockSpec((1, 2, seq, 128), lambda n, r, cb=cb: (n, cb, 0, 0))
    a_cfg = []
    outs, lses = [], []
    for gi, (_, d) in enumerate(DIL_GROUPS):
        cfg = dict(grid=(nseq, d), seq=seq, stride=d, kvw=GW, split=True,
                   q_spec=pair_blk(gi), k_spec=pair_blk(3 + gi), v_spec=pair_blk(6 + gi), o_spec=pair_blk(0),
                   bias_map=lambda n, r: (0, 0, 0), sink_map=lambda n, r: (0, 0, 0), has_sink=False)
        a_cfg.append(cfg)
        (o, lse), _ = _attn_fwd(za, za, za, bias_all[4 * gi:4 * gi + 4], sink_0, o_shape=(nseq, 2, seq, 128),
                                o_dtype=F32, name=f"attn_a{gi}_fwd", **cfg)
        outs.append(o)
        lses.append(lse)
    wide_blk = lambda w, cmap: pl.BlockSpec((1, seq, w), cmap)
    b_cfg = dict(grid=(nseq, 2), seq=seq, stride=1, kvw=2 * HEAD_DIM, split=False,
                 q_spec=wide_blk(GW, lambda n, r: (n, 0, r)), k_spec=wide_blk(2 * HEAD_DIM, lambda n, r: (n, 0, 4)),
                 v_spec=wide_blk(2 * HEAD_DIM, lambda n, r: (n, 0, 5)), o_spec=wide_blk(GW, lambda n, r: (n, 0, r)),
                 bias_map=lambda n, r: (r, 0, 0), sink_map=lambda n, r: (r, 0, 0), has_sink=True)
    keys = ("wgt2", "wut2", "wd2")
    (yb, lse_b), ro = _attn_fwd(zb3, zb3, zb3, bias_all[12:20], sink_b, o_shape=(nseq, seq, 2 * GW), o_dtype=BF16,
                                name="attn_b_fwd", rider=gather_rider(keys), **b_cfg)
    put(keys, ro)
    yb = yb.reshape(t, 2 * GW)

    h2, y, lse_tot, pa, pb, merged = _merge_fwd(outs[0], outs[1], outs[2], lses[0], lses[1], lses[2], yb, zg, h1,
                                                W["wat"], W["wbt"], W["wout"])
    (h3, n2, g2, u2), _ = _ffn_fwd(h2, S["ffn2_norm"], W["wgt2"], W["wut2"], W["wd2"])
    dh3, loss_part, g_final = _loss_head(h3, S["final_norm"].reshape(1, D_MODEL), tgt.reshape(t, D_MODEL))

    GS["final_norm"] = g_final
    dh2, dg2, du2, a2, df2, GS["ffn2_norm"] = _ffn_bwd(dh3, h2, S["ffn2_norm"], g2, u2, W["wgt2"], W["wut2"], W["wd2"])
    G["wgt2"] = _wgrad(dg2, n2, D_FF, name="wgrad_gate2")
    G["wut2"] = _wgrad(du2, n2, D_FF, name="wgrad_up2")
    G["wd2"] = _wgrad(a2, df2, D_FF, name="wgrad_down2")

    keys = ("wgt2", "wut2", "wd2")
    rider = _ExchangeRider([pair(keys)]) if dist else None
    (dpa, dpb, dga, dgb, dya, dyb, dh2b, ca, cb), ro = _merge_bwd(dh2, pa, pb, zg, y, yb, W["wat"], W["wbt"], W["wout"],
                                                                  nseq, rider=rider)
    if dist:
        finish(keys, *ro)
    G["wout"] = _wgrad(merged, dh2b, D_MODEL, name="wgrad_out")
    G["wat"] = _wgrad(dpa, y, D_MODEL, name="wgrad_branch_a")
    G["wbt"] = _wgrad(dpb, yb, D_MODEL, name="wgrad_branch_b")

    dqs, dks, dvs, dbs = [], [], [], []
    shp = (nseq, 2, seq, 128)
    halves = lambda a: [a[:, hf].reshape(t, 128).astype(BF16) for hf in range(2)]
    for gi in range(len(DIL_GROUPS)):
        dq, dk, dv, db, _ = _attn_bwd(za, za, za, bias_all[4 * gi:4 * gi + 4], sink_0, dya, ca, lse_tot,
                                      n_bias=4, dq_shape=shp, dkv_shape=shp, g_dtype=F32,
                                      kv_out_spec=a_cfg[gi]["o_spec"], name=f"attn_a{gi}_bwd", **a_cfg[gi])
        dqs += halves(dq)
        dks += halves(dk)
        dvs += halves(dv)
        dbs.append(db)
    dqb, dkb, dvb, dbb, dsink = _attn_bwd(zb3, zb3, zb3, bias_all[12:20], sink_b, seq3(dyb), seq3(cb), lse_b,
                                          n_bias=8, dq_shape=(nseq, seq, 2 * GW),
                                          dkv_shape=(nseq, seq, 2 * HEAD_DIM), g_dtype=BF16,
                                          kv_out_spec=wide_blk(2 * HEAD_DIM, lambda n, r: (n, 0, 0)),
                                          name="attn_b_bwd", **b_cfg)
    dz = jnp.concatenate(dqs + dks + dvs + [dqb.reshape(t, 2 * GW), dkb.reshape(t, 2 * HEAD_DIM),
                                            dvb.reshape(t, 2 * HEAD_DIM), dga, dgb], axis=-1)
    gb_tab = _bias_grad(jnp.concatenate(dbs + [dbb], axis=0), buckets)
    GS["rel_bias"] = gb_tab[:, :20]
    GS["sinks"] = dsink[:, 0, 0].reshape(1, 8)

    G["wint"], GS["b_in"] = _wgrad(dz, un, D_IN // 2, with_colsum=True, name="wgrad_in")
    keys = ("wint", "wout", "wat", "wbt")
    rider = _ExchangeRider([pair(keys)]) if dist else None
    (dh1, GS["mix_norm"]), ro = _inproj_bwd(dz, dh2, h1, S["mix_norm"], W["wint"], rider=rider)
    if dist:
        finish(keys, *ro)

    dx, dg1, du1, a1, df1, GS["ffn1_norm"] = _ffn_bwd(dh1, xf, S["ffn1_norm"], g1, u1, W["wgt1"], W["wut1"], W["wd1"])
    G["wgt1"] = _wgrad(dg1, n1, D_FF, name="wgrad_gate1")
    G["wut1"] = _wgrad(du1, n1, D_FF, name="wgrad_up1")
    G["wd1"] = _wgrad(a1, df1, D_FF, name="wgrad_down1")
    if dist:
        keys = ("wgt1", "wut1", "wd1")
        finish(keys, *_chip_exchange([pair(keys)]))
    return loss_part, dx.reshape(x.shape), (reduced if dist else G), GS


_SMALL = ("ffn1_norm", "mix_norm", "ffn2_norm", "final_norm", "b_in", "sinks", "rel_bias")
_ORDER = ("ffn1_norm", "ffn1_w_gate", "ffn1_w_up", "ffn1_w_down", "mix_norm", "w_in", "b_in", "w_branch_a",
          "w_branch_b", "w_out", "sinks", "rel_bias", "ffn2_norm", "ffn2_w_gate", "ffn2_w_up", "ffn2_w_down",
          "final_norm")
_BIG = (("wgt1", "ffn1_w_gate", True, 704), ("wut1", "ffn1_w_up", True, 704), ("wd1", "ffn1_w_down", False, 704),
        ("wint", "w_in", True, 1280), ("wout", "w_out", False, 256), ("wat", "w_branch_a", True, 64),
        ("wbt", "w_branch_b", True, 128), ("wgt2", "ffn2_w_gate", True, 704), ("wut2", "ffn2_w_up", True, 704),
        ("wd2", "ffn2_w_down", False, 704))
_FULL_SHAPE = {"wat": (D_MODEL, GW), "wbt": (D_MODEL, 2 * GW)}


def _pack_small(p, extra=None):
    last = [p["sinks"].reshape(8), p["rel_bias"].reshape(640)]
    used = 648
    if extra is not None:
        last.append(extra.reshape(1))
        used += 1
    last.append(jnp.zeros((D_MODEL - used,), F32))
    rows = [p["ffn1_norm"].reshape(1, D_MODEL), p["mix_norm"].reshape(1, D_MODEL), p["ffn2_norm"].reshape(1, D_MODEL),
            p["final_norm"].reshape(1, D_MODEL), p["b_in"].reshape(5, D_MODEL), jnp.concatenate(last).reshape(1, D_MODEL),
            jnp.zeros((6, D_MODEL), F32)]
    return jnp.concatenate(rows, axis=0)


def _unpack_small(a):
    return {"ffn1_norm": a[0:1], "mix_norm": a[1:2], "ffn2_norm": a[2:3], "final_norm": a[3],
            "b_in": a[4:9].reshape(1, D_IN), "sinks": a[9, 0:8].reshape(1, 8), "rel_bias": a[9, 8:648].reshape(32, 20)}


def kernel(x, ffn1_norm, ffn1_w_gate, ffn1_w_up, ffn1_w_down, mix_norm, w_in, b_in, w_branch_a, w_branch_b, w_out, sinks, rel_bias, ffn2_norm, ffn2_w_gate, ffn2_w_up, ffn2_w_down, final_norm, loss_target, m_ffn1_norm, m_ffn1_w_gate, m_ffn1_w_up, m_ffn1_w_down, m_mix_norm, m_w_in, m_b_in, m_w_branch_a, m_w_branch_b, m_w_out, m_sinks, m_rel_bias, m_ffn2_norm, m_ffn2_w_gate, m_ffn2_w_up, m_ffn2_w_down, m_final_norm, v_ffn1_norm, v_ffn1_w_gate, v_ffn1_w_up, v_ffn1_w_down, v_mix_norm, v_w_in, v_b_in, v_w_branch_a, v_w_branch_b, v_w_out, v_sinks, v_rel_bias, v_ffn2_norm, v_ffn2_w_gate, v_ffn2_w_up, v_ffn2_w_down, v_final_norm):
    args = dict(locals())
    w = {n: args[n] for n in _ORDER}
    m = {n: args["m_" + n] for n in _ORDER}
    v = {n: args["v_" + n] for n in _ORDER}

    shards = {}
    for key, name, transposed, rows in _BIG:
        a = w[name][0]
        a = (a.T if transposed else a).astype(BF16)
        shards[key] = a.reshape(rows, D_MODEL)
    S = {n: w[n] for n in _SMALL}

    loss_part, grad_x, reduced, GS = _local_step(x, loss_target, {}, S, shards)

    grads = {}
    for key, name, transposed, rows in _BIG:
        nat = w[name][0].shape
        grads[name] = reduced[key].reshape(nat[1], nat[0]).T if transposed else reduced[key].reshape(nat)

    small = _allreduce_small(_pack_small(GS, extra=loss_part[0, 0]))
    loss = small[9, 648]

    out_g, out_d, out_m, out_v = {}, {}, {}, {}
    for _, n, _, _ in _BIG:
        d_, nm_, nv_ = _adamw(w[n][0], grads[n], m[n][0], v[n][0], "adamw_" + n)
        out_g[n], out_d[n], out_m[n], out_v[n] = grads[n][None], d_[None], nm_[None], nv_[None]
    d_s, m_s, v_s = _adamw(_pack_small(w), small, _pack_small(m), _pack_small(v), "adamw_small")
    for dst, src in ((out_g, small), (out_d, d_s), (out_m, m_s), (out_v, v_s)):
        dst.update(_unpack_small(src))

    return (loss, grad_x, *[out_g[n] for n in _ORDER], *[out_d[n] for n in _ORDER],
            *[out_m[n] for n in _ORDER], *[out_v[n] for n in _ORDER])
```

```python
import math

import jax
import jax.numpy as jnp
from jax import lax
from jax.experimental import pallas as pl
from jax.experimental.pallas import tpu as pltpu

F32, BF16 = jnp.float32, jnp.bfloat16
MESH = pl.DeviceIdType.MESH

D_MODEL = 1024
D_FF = 2816
D_IN = 5120
HEAD_DIM = 64
BLOCK = 128
DIL_GROUPS = ((128, 1), (512, 4), (2048, 16))
B_WINDOW = 128
N_BUCKETS = 32
MAX_DISTANCE = 2048
EPS = 1e-6
N_CHIPS = 4
GW = 256
ZA_W = 2304
ZB_W = 768
NEG = -1e30

ADAM_LR, ADAM_B1, ADAM_B2, ADAM_EPS, ADAM_WD, ADAM_STEP = 0.001, 0.9, 0.999, 1e-08, 0.01, 10

VMEM_BIG = 56 * 1024 * 1024
TM = 512
TM_BWD = 256
FF_CHUNKS = 2
DMA_SPLIT = 8


def _dot(a, b):
    return jnp.dot(a, b, preferred_element_type=F32)


def _dot_nt(a, b):
    return lax.dot_general(a, b, (((1,), (1,)), ((), ())), preferred_element_type=F32)


def _dot_tn(a, b):
    return lax.dot_general(a, b, (((0,), (0,)), ((), ())), preferred_element_type=F32)


def _sigmoid(x):
    return 1.0 / (1.0 + jnp.exp(-x))


def _params(sem, vmem=None):
    return pltpu.CompilerParams(dimension_semantics=sem, vmem_limit_bytes=vmem)


ANY = pl.BlockSpec(memory_space=pl.ANY)


def _me():
    return lax.axis_index("x"), lax.axis_index("y"), lax.axis_index("c")


_CHIP_RELS = ((1, 0), (0, 1), (1, 1))


def _flip(v, f):
    return 1 - v if f else v


def _remote(src, dst, ssem, rsem, peer):
    return pltpu.make_async_remote_copy(src_ref=src, dst_ref=dst, send_sem=ssem, recv_sem=rsem,
                                        device_id=peer, device_id_type=MESH)


def _row_pieces(rows, n):
    step = max(16, -(-rows // n) // 16 * 16)
    out, s = [], 0
    while s < rows:
        out.append((s, min(step, rows - s)))
        s += step
    return out


def _gather_rows(shards):
    nt = len(shards)
    rows = [s.shape[0] for s in shards]

    def body(*refs):
        srcs, outs = refs[:nt], refs[nt:2 * nt]
        ici_s, ici_r, d2d_s, d2d_r, loc = refs[2 * nt:]
        x, y, c = _me()
        j = 2 * x + y
        sib = (x, y, 1 - c)
        local = [pltpu.make_async_copy(srcs[t], outs[t].at[j], loc.at[t]) for t in range(nt)]
        for cp in local:
            cp.start()
        sends = []
        for k, (fx, fy) in enumerate(_CHIP_RELS):
            peer = (_flip(x, fx), _flip(y, fy), c)
            for t in range(nt):
                half = pl.ds(c * (rows[t] // 2), rows[t] // 2)
                cp = _remote(srcs[t].at[half], outs[t].at[j, half], ici_s.at[3 * t + k], ici_r.at[3 * t + k], peer)
                cp.start()
                sends.append(cp)
        fwds = []
        for k, (fx, fy) in enumerate(_CHIP_RELS):
            pj = 2 * _flip(x, fx) + _flip(y, fy)
            for t in range(nt):
                half = pl.ds(c * (rows[t] // 2), rows[t] // 2)
                blk = outs[t].at[pj, half]
                _remote(blk, blk, ici_s.at[3 * t + k], ici_r.at[3 * t + k], sib).wait_recv()
                cp = _remote(blk, blk, d2d_s.at[3 * t + k], d2d_r.at[3 * t + k], sib)
                cp.start()
                fwds.append(cp)
        for cp in fwds:
            cp.wait()
        for cp in sends:
            cp.wait_send()
        for cp in local:
            cp.wait()

    sems = [pltpu.SemaphoreType.DMA((3 * nt,)) for _ in range(4)] + [pltpu.SemaphoreType.DMA((nt,))]
    return pl.pallas_call(
        body, name="gather_weights",
        out_shape=tuple(jax.ShapeDtypeStruct((N_CHIPS,) + s.shape, s.dtype) for s in shards),
        in_specs=[ANY] * nt, out_specs=tuple([ANY] * nt), scratch_shapes=sems,
    )(*shards)


VMEM_WHOLE = pl.BlockSpec(memory_space=pltpu.VMEM)


def _pair_reduce(grads, name):
    nt = len(grads)
    r2 = [g.shape[2] for g in grads]
    off = [sum(r2[:t]) for t in range(nt)]
    tot = sum(r2)

    def body(*refs):
        gs = refs[:nt]
        s_ref, got, ssem, rsem = refs[nt:]
        x, y, c = _me()
        sib = (x, y, 1 - c)
        for t in range(nt):
            for k in range(N_CHIPS):
                _remote(gs[t].at[k, 1 - c], got.at[k, pl.ds(off[t], r2[t])], ssem, rsem, sib).start()
        _remote(got, got, ssem, rsem, sib).wait()
        for t in range(nt):
            for k in range(N_CHIPS):
                rows = slice(off[t], off[t] + r2[t])
                s_ref[k, rows, :] = (gs[t][k, c].astype(F32) + got[k, rows, :].astype(F32)).astype(BF16)

    shp = jax.ShapeDtypeStruct((N_CHIPS, tot, D_MODEL), BF16)
    return pl.pallas_call(
        body, name=name, out_shape=shp, in_specs=[VMEM_WHOLE] * nt, out_specs=VMEM_WHOLE,
        scratch_shapes=[pltpu.VMEM((N_CHIPS, tot, D_MODEL), BF16), pltpu.SemaphoreType.DMA(()),
                        pltpu.SemaphoreType.DMA(())],
        compiler_params=pltpu.CompilerParams(vmem_limit_bytes=VMEM_BIG),
    )(*grads)


def _chip_exchange(parts):
    ng = len(parts)
    r2 = [p.shape[1] for p in parts]
    off = [sum(r2[:g]) for g in range(ng)]
    tot = sum(r2)

    def body(*refs):
        ps = refs[:ng]
        own_ref, rec_ref, ssems, rsems, lsem = refs[ng:]
        x, y, c = _me()
        j = 2 * x + y
        for g in range(ng):
            pltpu.make_async_copy(ps[g].at[j], own_ref.at[pl.ds(off[g], r2[g])], lsem).start()
        for k, (fx, fy) in enumerate(_CHIP_RELS):
            px, py = _flip(x, fx), _flip(y, fy)
            for g in range(ng):
                for st, sz in _row_pieces(r2[g], 2):
                    _remote(ps[g].at[2 * px + py, pl.ds(st, sz)], rec_ref.at[k, pl.ds(off[g] + st, sz)],
                            ssems.at[k], rsems.at[k], (px, py, c)).start()
        for k in range(3):
            _remote(rec_ref.at[k], rec_ref.at[k], ssems.at[k], rsems.at[k], (x, y, c)).wait()
        pltpu.make_async_copy(own_ref, own_ref, lsem).wait()

    return pl.pallas_call(
        body, name="grad_chip_exchange",
        out_shape=(jax.ShapeDtypeStruct((tot, D_MODEL), BF16), jax.ShapeDtypeStruct((3, tot, D_MODEL), BF16)),
        in_specs=[ANY] * ng, out_specs=(ANY, ANY),
        scratch_shapes=[pltpu.SemaphoreType.DMA((3,)), pltpu.SemaphoreType.DMA((3,)), pltpu.SemaphoreType.DMA(())],
    )(*parts)


def _final_reduce(own, rec, name):
    r2 = own.shape[0]
    pieces = _row_pieces(r2, DMA_SPLIT)

    def body(own_ref, rec_ref, o_ref, fbuf, ssem, rsem, lsem):
        x, y, c = _me()
        sib = (x, y, 1 - c)
        for st, sz in pieces:
            rows = slice(st, st + sz)
            fbuf[rows, :] = (own_ref[rows, :].astype(F32) + rec_ref[0, rows, :].astype(F32)
                             + rec_ref[1, rows, :].astype(F32) + rec_ref[2, rows, :].astype(F32))
            pltpu.make_async_copy(fbuf.at[pl.ds(st, sz)], o_ref.at[c, pl.ds(st, sz)], lsem).start()
            _remote(fbuf.at[pl.ds(st, sz)], o_ref.at[c, pl.ds(st, sz)], ssem, rsem, sib).start()
        _remote(fbuf, o_ref.at[c], ssem, rsem, sib).wait()
        pltpu.make_async_copy(fbuf, o_ref.at[c], lsem).wait()

    return pl.pallas_call(
        body, name=name, out_shape=jax.ShapeDtypeStruct((2, r2, D_MODEL), F32),
        in_specs=[VMEM_WHOLE, VMEM_WHOLE], out_specs=ANY,
        scratch_shapes=[pltpu.VMEM((r2, D_MODEL), F32), pltpu.SemaphoreType.DMA(()), pltpu.SemaphoreType.DMA(()),
                        pltpu.SemaphoreType.DMA(())],
        compiler_params=pltpu.CompilerParams(vmem_limit_bytes=VMEM_BIG),
    )(own, rec)


def _allreduce_small(vec):
    def body(v_ref, o_ref, buf, send_sems, recv_sems):
        x, y, c = _me()
        me = 4 * x + 2 * y + c
        buf[me] = v_ref[...]
        copies = []
        for k in range(1, 8):
            peer = (_flip(x, (k >> 2) & 1), _flip(y, (k >> 1) & 1), _flip(c, k & 1))
            cp = _remote(v_ref, buf.at[me], send_sems.at[k - 1], recv_sems.at[k - 1], peer)
            cp.start()
            copies.append(cp)
        for cp in copies:
            cp.wait()
        acc = buf[0]
        for i in range(1, 8):
            acc = acc + buf[i]
        o_ref[...] = acc

    vm = pl.BlockSpec(memory_space=pltpu.VMEM)
    return pl.pallas_call(
        body, name="allreduce_small", out_shape=jax.ShapeDtypeStruct(vec.shape, vec.dtype),
        in_specs=[vm], out_specs=vm,
        scratch_shapes=[pltpu.VMEM((8,) + vec.shape, vec.dtype), pltpu.SemaphoreType.DMA((7,)),
                        pltpu.SemaphoreType.DMA((7,))],
    )(vec)


class _GatherRider:
    def __init__(self, shards):
        self.inputs = list(shards)
        nt = len(shards)
        self.out_shape = [jax.ShapeDtypeStruct((N_CHIPS,) + s.shape, s.dtype) for s in shards]
        self.scratch = [pltpu.SemaphoreType.DMA((3 * nt,)), pltpu.SemaphoreType.DMA((3 * nt,)),
                        pltpu.SemaphoreType.DMA((nt,))]

    def _copies(self, srcs, outs, sems):
        ici_s, ici_r, loc = sems
        x, y, c = _me()
        j = 2 * x + y
        local = [pltpu.make_async_copy(srcs[t], outs[t].at[j], loc.at[t]) for t in range(len(srcs))]
        remote = []
        for k, (fx, fy) in enumerate(_CHIP_RELS):
            peer = (_flip(x, fx), _flip(y, fy), c)
            for t in range(len(srcs)):
                remote.append(_remote(srcs[t], outs[t].at[j], ici_s.at[3 * t + k], ici_r.at[3 * t + k], peer))
        return local, remote

    def start(self, srcs, outs, sems):
        local, remote = self._copies(srcs, outs, sems)
        for cp in local + remote:
            cp.start()

    def finish(self, srcs, outs, sems):
        local, remote = self._copies(srcs, outs, sems)
        for cp in remote + local:
            cp.wait()


class _ExchangeRider:
    def __init__(self, parts):
        self.inputs = list(parts)
        self.r2 = [p.shape[1] for p in parts]
        self.off = [sum(self.r2[:g]) for g in range(len(parts))]
        tot = sum(self.r2)
        self.out_shape = [jax.ShapeDtypeStruct((tot, D_MODEL), BF16), jax.ShapeDtypeStruct((3, tot, D_MODEL), BF16)]
        self.scratch = [pltpu.SemaphoreType.DMA((3,)), pltpu.SemaphoreType.DMA((3,)), pltpu.SemaphoreType.DMA(())]

    def start(self, ps, outs, sems):
        own_ref, rec_ref = outs
        ssems, rsems, lsem = sems
        x, y, c = _me()
        j = 2 * x + y
        for g in range(len(ps)):
            pltpu.make_async_copy(ps[g].at[j], own_ref.at[pl.ds(self.off[g], self.r2[g])], lsem).start()
        for k, (fx, fy) in enumerate(_CHIP_RELS):
            px, py = _flip(x, fx), _flip(y, fy)
            for g in range(len(ps)):
                for st, sz in _row_pieces(self.r2[g], 2):
                    _remote(ps[g].at[2 * px + py, pl.ds(st, sz)], rec_ref.at[k, pl.ds(self.off[g] + st, sz)],
                            ssems.at[k], rsems.at[k], (px, py, c)).start()

    def finish(self, ps, outs, sems):
        own_ref, rec_ref = outs
        ssems, rsems, lsem = sems
        x, y, c = _me()
        for k in range(3):
            _remote(rec_ref.at[k], rec_ref.at[k], ssems.at[k], rsems.at[k], (x, y, c)).wait()
        pltpu.make_async_copy(own_ref, own_ref, lsem).wait()


def _pallas(body, args, *, name, grid, in_specs, out_specs, out_shape, scratch_shapes=(), sem=None, vmem=None,
            rider=None):
    if rider is None:
        res = pl.pallas_call(body, name=name, grid=grid, in_specs=list(in_specs), out_specs=tuple(out_specs),
                             out_shape=tuple(out_shape), scratch_shapes=list(scratch_shapes),
                             compiler_params=_params(sem, vmem))(*args)
        return tuple(res), ()
    n_in, n_out, n_sc = len(in_specs), len(out_shape), len(scratch_shapes)
    r_in, r_out = len(rider.inputs), len(rider.out_shape)

    def wrapped(*refs):
        ins, rins = refs[:n_in], refs[n_in:n_in + r_in]
        p = n_in + r_in
        outs, routs = refs[p:p + n_out], refs[p + n_out:p + n_out + r_out]
        p += n_out + r_out
        scr, rsems = refs[p:p + n_sc], refs[p + n_sc:]
        first = pl.program_id(0) == 0
        last = pl.program_id(0) == grid[0] - 1
        for a in range(1, len(grid)):
            first = first & (pl.program_id(a) == 0)
            last = last & (pl.program_id(a) == grid[a] - 1)

        @pl.when(first)
        def _():
            rider.start(rins, routs, rsems)

        body(*ins, *outs, *scr)

        @pl.when(last)
        def _():
            rider.finish(rins, routs, rsems)

    res = pl.pallas_call(
        wrapped, name=name, grid=grid, in_specs=list(in_specs) + [ANY] * r_in,
        out_specs=tuple(out_specs) + (ANY,) * r_out, out_shape=tuple(out_shape) + tuple(rider.out_shape),
        scratch_shapes=list(scratch_shapes) + rider.scratch,
        compiler_params=_params(("arbitrary",) * len(grid), vmem))(*args, *rider.inputs)
    return tuple(res[:n_out]), tuple(res[n_out:])


def _ffn_fwd(h, gain, wgt, wut, wd, rider=None):
    t = h.shape[0]
    fc = D_FF // FF_CHUNKS

    def body(h_ref, gain_ref, wg_hbm, wu_hbm, wd_hbm, hout_ref, n_ref, g_ref, u_ref, wg_v, wu_v, wd_v):
        @pl.when(pl.program_id(0) == 0)
        def _():
            pltpu.sync_copy(wg_hbm, wg_v)
            pltpu.sync_copy(wu_hbm, wu_v)
            pltpu.sync_copy(wd_hbm, wd_v)

        hh = h_ref[...]
        r = lax.rsqrt(jnp.mean(hh * hh, axis=-1, keepdims=True) + EPS)
        n = (hh * r * gain_ref[...]).astype(BF16)
        n_ref[...] = n
        acc = jnp.zeros((TM, D_MODEL), F32)
        for ci in range(FF_CHUNKS):
            sl = slice(ci * fc, (ci + 1) * fc)
            g = _dot_nt(n, wg_v[sl, :])
            u = _dot_nt(n, wu_v[sl, :])
            g_ref[:, sl] = g.astype(BF16)
            u_ref[:, sl] = u.astype(BF16)
            a = (g * _sigmoid(g) * u).astype(BF16)
            acc = acc + _dot(a, wd_v[sl, :])
        hout_ref[...] = hh + 0.5 * acc

    row = lambda w: pl.BlockSpec((TM, w), lambda i: (i, 0))
    wv = pltpu.VMEM((D_FF, D_MODEL), BF16)
    return _pallas(
        body, (h, gain, wgt, wut, wd), name="ffn_fwd", grid=(t // TM,),
        out_shape=(jax.ShapeDtypeStruct((t, D_MODEL), F32), jax.ShapeDtypeStruct((t, D_MODEL), BF16),
                   jax.ShapeDtypeStruct((t, D_FF), BF16), jax.ShapeDtypeStruct((t, D_FF), BF16)),
        in_specs=[row(D_MODEL), pl.BlockSpec((1, D_MODEL), lambda i: (0, 0)), ANY, ANY, ANY],
        out_specs=(row(D_MODEL), row(D_MODEL), row(D_FF), row(D_FF)),
        scratch_shapes=[wv, wv, wv], sem=("arbitrary",), vmem=VMEM_BIG, rider=rider)


def _ffn_bwd(dhout, h, gain, g, u, wgt, wut, wd):
    t = h.shape[0]
    tm = TM_BWD
    fc = D_FF // FF_CHUNKS

    def body(dho_ref, h_ref, gain_ref, g_ref, u_ref, wg_hbm, wu_hbm, wd_hbm,
             dh_ref, dg_ref, du_ref, a_ref, df_ref, gg_ref, wg_v, wu_v, wd_v):
        @pl.when(pl.program_id(0) == 0)
        def _():
            pltpu.sync_copy(wg_hbm, wg_v)
            pltpu.sync_copy(wu_hbm, wu_v)
            pltpu.sync_copy(wd_hbm, wd_v)
            gg_ref[...] = jnp.zeros_like(gg_ref)

        dho = dho_ref[...]
        df = (0.5 * dho).astype(BF16)
        df_ref[...] = df
        dn = jnp.zeros((tm, D_MODEL), F32)
        for ci in range(FF_CHUNKS):
            sl = slice(ci * fc, (ci + 1) * fc)
            da = _dot_nt(df, wd_v[sl, :])
            gv = g_ref[:, sl].astype(F32)
            uv = u_ref[:, sl].astype(F32)
            sg = _sigmoid(gv)
            silu = gv * sg
            dg = (da * uv * (sg * (1.0 + gv * (1.0 - sg)))).astype(BF16)
            du = (da * silu).astype(BF16)
            dg_ref[:, sl] = dg
            du_ref[:, sl] = du
            a_ref[:, sl] = (silu * uv).astype(BF16)
            dn = dn + _dot(dg, wg_v[sl, :]) + _dot(du, wu_v[sl, :])
        hh = h_ref[...]
        r = lax.rsqrt(jnp.mean(hh * hh, axis=-1, keepdims=True) + EPS)
        hn = hh * r
        gg_ref[...] += jnp.sum(dn * hn, axis=0, keepdims=True)
        dng = dn * gain_ref[...]
        dh_ref[...] = dho + r * (dng - hn * jnp.mean(dng * hn, axis=-1, keepdims=True))

    row = lambda w: pl.BlockSpec((tm, w), lambda i: (i, 0))
    vec = pl.BlockSpec((1, D_MODEL), lambda i: (0, 0))
    wv = pltpu.VMEM((D_FF, D_MODEL), BF16)
    return pl.pallas_call(
        body, name="ffn_bwd", grid=(t // tm,),
        out_shape=(jax.ShapeDtypeStruct((t, D_MODEL), F32), jax.ShapeDtypeStruct((t, D_FF), BF16),
                   jax.ShapeDtypeStruct((t, D_FF), BF16), jax.ShapeDtypeStruct((t, D_FF), BF16),
                   jax.ShapeDtypeStruct((t, D_MODEL), BF16), jax.ShapeDtypeStruct((1, D_MODEL), F32)),
        in_specs=[row(D_MODEL), row(D_MODEL), vec, row(D_FF), row(D_FF), ANY, ANY, ANY],
        out_specs=(row(D_MODEL), row(D_FF), row(D_FF), row(D_FF), row(D_MODEL), vec),
        scratch_shapes=[wv, wv, wv],
        compiler_params=_params(("arbitrary",), VMEM_BIG),
    )(dhout, h, gain, g, u, wgt, wut, wd)


def _wgrad(lhs, rhs, rb, with_colsum=False, name="wgrad", rider=None):
    t, k = lhs.shape
    n = rhs.shape[1]
    tk = 512
    nt = t // tk

    def body(l_ref, r_ref, o_ref, *rest):
        acc = rest[-1]
        ti = pl.program_id(1)

        @pl.when(ti == 0)
        def _():
            acc[...] = jnp.zeros_like(acc)
            if with_colsum:
                rest[0][...] = jnp.zeros_like(rest[0])

        acc[...] += _dot_tn(l_ref[...], r_ref[...])
        if with_colsum:
            rest[0][...] += jnp.sum(l_ref[...].astype(F32), axis=0, keepdims=True)

        @pl.when(ti == nt - 1)
        def _():
            o_ref[...] = acc[...].astype(BF16)

    out_shape = [jax.ShapeDtypeStruct((k, n), BF16)]
    out_specs = [pl.BlockSpec((rb, n), lambda j, i: (j, 0))]
    if with_colsum:
        out_shape.append(jax.ShapeDtypeStruct((1, k), F32))
        out_specs.append(pl.BlockSpec((1, rb), lambda j, i: (0, j)))
    res, ro = _pallas(
        body, (lhs, rhs), name=name, grid=(k // rb, nt), out_shape=tuple(out_shape),
        in_specs=[pl.BlockSpec((tk, rb), lambda j, i: (i, j)), pl.BlockSpec((tk, n), lambda j, i: (i, 0))],
        out_specs=tuple(out_specs), scratch_shapes=[pltpu.VMEM((rb, n), F32)],
        sem=("arbitrary", "arbitrary"), vmem=VMEM_BIG, rider=rider)
    if rider is not None:
        return res[0], ro
    return res if with_colsum else res[0]


def _lane_blocks(nseq, seq, nblk, tm=TM):
    spt = seq // tm
    return pl.BlockSpec((1, nblk, tm, 128), lambda i: (i // spt, 0, i % spt, 0))


def _inproj_fwd(h, gain, wint, b_in, nseq, rider=None):
    t = h.shape[0]
    seq = t // nseq
    half_a = ZA_W // 2
    pieces = ((0, half_a, 0, 0), (half_a, half_a, 0, half_a), (ZA_W, ZB_W, 1, 0), (ZA_W + ZB_W, 1024, 2, 0),
              (ZA_W + ZB_W + 1024, 1024, 2, 1024))

    def body(h_ref, gain_ref, w_hbm, b_ref, u_ref, za_ref, zb_ref, zg_ref, w_v):
        @pl.when(pl.program_id(0) == 0)
        def _():
            pltpu.sync_copy(w_hbm, w_v)

        hh = h_ref[...]
        r = lax.rsqrt(jnp.mean(hh * hh, axis=-1, keepdims=True) + EPS)
        un = (hh * r * gain_ref[...]).astype(BF16)
        u_ref[...] = un
        outs = (None, zb_ref, zg_ref)
        for c0, cw, oi, o0 in pieces:
            val = _dot_nt(un, w_v[c0:c0 + cw, :]) + b_ref[:, c0:c0 + cw]
            if oi == 0:
                for cb in range(cw // 128):
                    za_ref[0, o0 // 128 + cb] = val[:, cb * 128:(cb + 1) * 128]
            else:
                outs[oi][:, o0:o0 + cw] = val.astype(BF16)

    row = lambda w: pl.BlockSpec((TM, w), lambda i: (i, 0))
    return _pallas(
        body, (h, gain, wint, b_in), name="inproj_fwd", grid=(t // TM,),
        out_shape=(jax.ShapeDtypeStruct((t, D_MODEL), BF16), jax.ShapeDtypeStruct((nseq, ZA_W // 128, seq, 128), F32),
                   jax.ShapeDtypeStruct((t, ZB_W), BF16), jax.ShapeDtypeStruct((t, 2 * D_MODEL), BF16)),
        in_specs=[row(D_MODEL), pl.BlockSpec((1, D_MODEL), lambda i: (0, 0)), ANY,
                  pl.BlockSpec((1, D_IN), lambda i: (0, 0))],
        out_specs=(row(D_MODEL), _lane_blocks(nseq, seq, ZA_W // 128), row(ZB_W), row(2 * D_MODEL)),
        scratch_shapes=[pltpu.VMEM((D_IN, D_MODEL), BF16)], sem=("arbitrary",), vmem=VMEM_BIG, rider=rider)


def _inproj_bwd(dz, dh2, h, gain, wint, rider=None):
    t = h.shape[0]
    nc = 5
    cw = D_IN // nc

    def body(dz_ref, dh2_ref, h_ref, gain_ref, w_hbm, dh_ref, gg_ref, w_v):
        @pl.when(pl.program_id(0) == 0)
        def _():
            pltpu.sync_copy(w_hbm, w_v)
            gg_ref[...] = jnp.zeros_like(gg_ref)

        du = jnp.zeros((TM, D_MODEL), F32)
        for ci in range(nc):
            sl = slice(ci * cw, (ci + 1) * cw)
            du = du + _dot(dz_ref[:, sl], w_v[sl, :])
        hh = h_ref[...]
        r = lax.rsqrt(jnp.mean(hh * hh, axis=-1, keepdims=True) + EPS)
        hn = hh * r
        gg_ref[...] += jnp.sum(du * hn, axis=0, keepdims=True)
        dng = du * gain_ref[...]
        dh_ref[...] = dh2_ref[...] + r * (dng - hn * jnp.mean(dng * hn, axis=-1, keepdims=True))

    row = lambda w: pl.BlockSpec((TM, w), lambda i: (i, 0))
    vec = pl.BlockSpec((1, D_MODEL), lambda i: (0, 0))
    return _pallas(
        body, (dz, dh2, h, gain, wint), name="inproj_bwd", grid=(t // TM,),
        out_shape=(jax.ShapeDtypeStruct((t, D_MODEL), F32), jax.ShapeDtypeStruct((1, D_MODEL), F32)),
        in_specs=[row(D_IN), row(D_MODEL), row(D_MODEL), vec, ANY],
        out_specs=(row(D_MODEL), vec),
        scratch_shapes=[pltpu.VMEM((D_IN, D_MODEL), BF16)], sem=("arbitrary",), vmem=VMEM_BIG, rider=rider)


def _head_sum_matrix(w):
    i = lax.broadcasted_iota(jnp.int32, (w, w), 0) // HEAD_DIM
    j = lax.broadcasted_iota(jnp.int32, (w, w), 1) // HEAD_DIM
    return (i == j).astype(F32)


def _merge_fwd(o0, o1, o2, l0, l1, l2, yb, zg, h1, wat, wbt, wout):
    t = h1.shape[0]
    nseq, _, seq, _ = o0.shape

    def body(o0_ref, o1_ref, o2_ref, l0_ref, l1_ref, l2_ref, yb_ref, ga_ref, gb_ref, h1_ref, wa_ref, wb_ref, wo_ref,
             h2_ref, y_ref, lt_ref, pa_ref, pb_ref, mg_ref):
        wide = lambda ref: jnp.concatenate([ref[0, 0], ref[0, 1]], axis=1)
        la, lb, lc = wide(l0_ref), wide(l1_ref), wide(l2_ref)
        mx = jnp.maximum(jnp.maximum(la, lb), lc)
        ea, eb, ec = jnp.exp(la - mx), jnp.exp(lb - mx), jnp.exp(lc - mx)
        den = ea + eb + ec
        y = (ea * wide(o0_ref) + eb * wide(o1_ref) + ec * wide(o2_ref)) / den
        lt = mx + jnp.log(den)
        lt_ref[0, 0] = lt[:, :128]
        lt_ref[0, 1] = lt[:, 128:]
        yb16 = y.astype(BF16)
        y_ref[...] = yb16
        pa = _dot_nt(yb16, wa_ref[...])
        pb = _dot_nt(yb_ref[...], wb_ref[...])
        pa_ref[...] = pa.astype(BF16)
        pb_ref[...] = pb.astype(BF16)
        mg = (_sigmoid(ga_ref[...].astype(F32)) * pa + _sigmoid(gb_ref[...].astype(F32)) * pb).astype(BF16)
        mg_ref[...] = mg
        h2_ref[...] = h1_ref[...] + _dot(mg, wo_ref[...])

    row = lambda w: pl.BlockSpec((TM, w), lambda i: (i, 0))
    full = lambda a: pl.BlockSpec(a.shape, lambda i: (0, 0))
    gate = lambda cb: pl.BlockSpec((TM, D_MODEL), lambda i: (i, cb))
    return pl.pallas_call(
        body, name="merge_fwd", grid=(t // TM,),
        out_shape=(jax.ShapeDtypeStruct((t, D_MODEL), F32), jax.ShapeDtypeStruct((t, GW), BF16),
                   jax.ShapeDtypeStruct((nseq, 2, seq, 128), F32), jax.ShapeDtypeStruct((t, D_MODEL), BF16),
                   jax.ShapeDtypeStruct((t, D_MODEL), BF16), jax.ShapeDtypeStruct((t, D_MODEL), BF16)),
        in_specs=[_lane_blocks(nseq, seq, 2)] * 6 + [row(2 * GW), gate(0), gate(1), row(D_MODEL), full(wat), full(wbt),
                                                     full(wout)],
        out_specs=(row(D_MODEL), row(GW), _lane_blocks(nseq, seq, 2), row(D_MODEL), row(D_MODEL), row(D_MODEL)),
        compiler_params=_params(("parallel",), VMEM_BIG),
    )(o0, o1, o2, l0, l1, l2, yb, zg, zg, h1, wat, wbt, wout)


def _merge_bwd(dh2, pa, pb, zg, y, yb, wat, wbt, wout, nseq, rider=None):
    t = dh2.shape[0]

    def body(dh2_ref, pa_ref, pb_ref, ga_ref, gb_ref, y_ref, yb_ref, wa_ref, wb_ref, wo_ref,
             dpa_ref, dpb_ref, dga_ref, dgb_ref, dya_ref, dyb_ref, dh2b_ref, ca_ref, cb_ref):
        d16 = dh2_ref[...].astype(BF16)
        dh2b_ref[...] = d16
        dm = _dot_nt(d16, wo_ref[...])
        sa = _sigmoid(ga_ref[...].astype(F32))
        sb = _sigmoid(gb_ref[...].astype(F32))
        dpa = (dm * sa).astype(BF16)
        dpb = (dm * sb).astype(BF16)
        dpa_ref[...] = dpa
        dpb_ref[...] = dpb
        dga_ref[...] = (dm * pa_ref[...].astype(F32) * sa * (1.0 - sa)).astype(BF16)
        dgb_ref[...] = (dm * pb_ref[...].astype(F32) * sb * (1.0 - sb)).astype(BF16)
        dya = _dot(dpa, wa_ref[...])
        dyb = _dot(dpb, wb_ref[...])
        dya_ref[0, 0] = dya[:, :128]
        dya_ref[0, 1] = dya[:, 128:]
        dyb_ref[...] = dyb.astype(BF16)
        hp = lax.Precision.HIGHEST
        ca = jnp.dot(dya * y_ref[...].astype(F32), _head_sum_matrix(GW), precision=hp, preferred_element_type=F32)
        ca_ref[0, 0] = ca[:, :128]
        ca_ref[0, 1] = ca[:, 128:]
        cb_ref[...] = jnp.dot(dyb * yb_ref[...].astype(F32), _head_sum_matrix(2 * GW), precision=hp,
                              preferred_element_type=F32)

    row = lambda w: pl.BlockSpec((TM, w), lambda i: (i, 0))
    full = lambda a: pl.BlockSpec(a.shape, lambda i: (0, 0))
    gate = lambda cb: pl.BlockSpec((TM, D_MODEL), lambda i: (i, cb))
    bf = lambda w: jax.ShapeDtypeStruct((t, w), BF16)
    lanes = jax.ShapeDtypeStruct((nseq, 2, t // nseq, 128), F32)
    lane_spec = _lane_blocks(nseq, t // nseq, 2)
    return _pallas(
        body, (dh2, pa, pb, zg, zg, y, yb, wat, wbt, wout), name="merge_bwd", grid=(t // TM,),
        out_shape=(bf(D_MODEL), bf(D_MODEL), bf(D_MODEL), bf(D_MODEL), lanes, bf(2 * GW), bf(D_MODEL),
                   lanes, jax.ShapeDtypeStruct((t, 2 * GW), F32)),
        in_specs=[row(D_MODEL), row(D_MODEL), row(D_MODEL), gate(0), gate(1), row(GW), row(2 * GW),
                  full(wat), full(wbt), full(wout)],
        out_specs=(row(D_MODEL), row(D_MODEL), row(D_MODEL), row(D_MODEL), lane_spec, row(2 * GW), row(D_MODEL),
                   lane_spec, row(2 * GW)),
        sem=("parallel",), vmem=VMEM_BIG, rider=rider)


def _loss_head(h3, gain, tgt):
    t = h3.shape[0]

    def body(h_ref, gain_ref, t_ref, dh_ref, loss_ref, gg_ref):
        @pl.when(pl.program_id(0) == 0)
        def _():
            loss_ref[...] = jnp.zeros_like(loss_ref)
            gg_ref[...] = jnp.zeros_like(gg_ref)

        hh = h_ref[...]
        r = lax.rsqrt(jnp.mean(hh * hh, axis=-1, keepdims=True) + EPS)
        hn = hh * r
        err = hn * gain_ref[...] - t_ref[...]
        part = jnp.sum(jnp.sum(err * err, axis=1, keepdims=True), axis=0, keepdims=True)
        loss_ref[...] += (0.5 / D_MODEL) * part
        dy = err * (1.0 / D_MODEL)
        gg_ref[...] += jnp.sum(dy * hn, axis=0, keepdims=True)
        dng = dy * gain_ref[...]
        dh_ref[...] = r * (dng - hn * jnp.mean(dng * hn, axis=-1, keepdims=True))

    row = pl.BlockSpec((TM, D_MODEL), lambda i: (i, 0))
    vec = pl.BlockSpec((1, D_MODEL), lambda i: (0, 0))
    return pl.pallas_call(
        body, name="loss_head", grid=(t // TM,),
        out_shape=(jax.ShapeDtypeStruct((t, D_MODEL), F32), jax.ShapeDtypeStruct((8, 128), F32),
                   jax.ShapeDtypeStruct((1, D_MODEL), F32)),
        in_specs=[row, vec, row], out_specs=(row, pl.BlockSpec((8, 128), lambda i: (0, 0)), vec),
        compiler_params=_params(("arbitrary",)),
    )(h3, gain, tgt)


def _lane_head(rows):
    return lax.broadcasted_iota(jnp.int32, (rows, GW), 1) // HEAD_DIM


def _kv_expand_matrix(r):
    ci = lax.broadcasted_iota(jnp.int32, (2 * HEAD_DIM, GW), 0)
    ji = lax.broadcasted_iota(jnp.int32, (2 * HEAD_DIM, GW), 1)
    return (ci == (ji % HEAD_DIM) + HEAD_DIM * r).astype(BF16)


def _block_rows(row0, stride, ib):
    start = row0 + (stride * BLOCK) * ib
    if stride > 1:
        return pl.ds(start, BLOCK, stride=stride)
    return pl.ds(pl.multiple_of(start, BLOCK), BLOCK)


def _stack_heads(x, lane_head):
    return jnp.concatenate([jnp.where(lane_head == h, x, jnp.zeros_like(x)) for h in range(4)], axis=0)


def _unstack_heads(x4, lane_head):
    out = jnp.zeros((BLOCK, GW), F32)
    for h in range(4):
        out = jnp.where(lane_head == h, x4[h * BLOCK:(h + 1) * BLOCK], out)
    return out


def _load_rows(ref, rows, split):
    if split:
        return jnp.concatenate([ref[0, 0, rows, :], ref[0, 1, rows, :]], axis=1)
    return ref[0, rows, :]


def _store_rows(ref, rows, val, split):
    if split:
        ref[0, 0, rows, :] = val[:, :128]
        ref[0, 1, rows, :] = val[:, 128:]
    else:
        ref[0, rows, :] = val


def _attn_fwd(q_arr, k_arr, v_arr, bias, sink, *, grid, seq, stride, kvw, split, q_spec, k_spec, v_spec, bias_map,
              sink_map, o_spec, has_sink, o_shape, o_dtype, name, rider=None):
    nb = seq // stride // BLOCK
    scale = HEAD_DIM ** -0.5

    def body(q_ref, k_ref, v_ref, bias_ref, sink_ref, o_ref, lse_ref):
        rr = pl.program_id(1)
        row0 = rr if stride > 1 else 0
        lane_head = _lane_head(BLOCK)
        expand = _kv_expand_matrix(rr) if kvw != GW else None
        if has_sink:
            sk4 = jnp.concatenate([jnp.broadcast_to(sink_ref[0, h:h + 1, 0:1], (BLOCK, 1)) for h in range(4)], axis=0)

        def load(ref, ib):
            return _load_rows(ref, _block_rows(row0, stride, ib), split).astype(BF16)

        def block(ib, first):
            q4 = _stack_heads(load(q_ref, ib), lane_head)
            if first:
                kc, vc = load(k_ref, ib), load(v_ref, ib)
                b4 = bias_ref[:, :, BLOCK:].reshape(4 * BLOCK, BLOCK)
            else:
                kc = jnp.concatenate([load(k_ref, ib - 1), load(k_ref, ib)], axis=0)
                vc = jnp.concatenate([load(v_ref, ib - 1), load(v_ref, ib)], axis=0)
                b4 = bias_ref[...].reshape(4 * BLOCK, 2 * BLOCK)
            if expand is not None:
                kc = _dot(kc, expand).astype(BF16)
                vc = _dot(vc, expand).astype(BF16)
            s = _dot_nt(q4, kc) * scale + b4
            m = jnp.max(s, axis=-1, keepdims=True)
            if has_sink:
                m = jnp.maximum(m, sk4)
            p = jnp.exp(s - m)
            l = jnp.sum(p, axis=-1, keepdims=True)
            if has_sink:
                l = l + jnp.exp(sk4 - m)
            o4 = _dot(p.astype(BF16), vc) / l
            rows = _block_rows(row0, stride, ib)
            _store_rows(o_ref, rows, _unstack_heads(o4, lane_head).astype(o_dtype), split)
            _store_rows(lse_ref, rows, _unstack_heads(m + jnp.log(l), lane_head), split)

        block(0, True)
        if nb > 1:
            def step(i, carry):
                block(i, False)
                return carry
            lax.fori_loop(1, nb, step, 0)

    return _pallas(
        body, (q_arr, k_arr, v_arr, bias, sink), name=name, grid=grid,
        out_shape=(jax.ShapeDtypeStruct(o_shape, o_dtype), jax.ShapeDtypeStruct(o_shape, F32)),
        in_specs=[q_spec, k_spec, v_spec,
                  pl.BlockSpec((4, BLOCK, 2 * BLOCK), bias_map), pl.BlockSpec((1, 4, 128), sink_map)],
        out_specs=(o_spec, o_spec),
        sem=("arbitrary", "arbitrary"), vmem=VMEM_BIG, rider=rider)


def _attn_bwd(q_arr, k_arr, v_arr, bias, sink, dy, cc, lse, *, grid, seq, stride, kvw, split, q_spec, k_spec, v_spec,
              bias_map, sink_map, o_spec, kv_out_spec, has_sink, n_bias, dq_shape, dkv_shape, g_dtype, name):
    ln = seq // stride
    nb = ln // BLOCK
    scale = HEAD_DIM ** -0.5

    def body(q_ref, k_ref, v_ref, bias_ref, sink_ref, dy_ref, c_ref, lse_ref,
             dq_ref, dk_ref, dv_ref, db_ref, dsk_ref, dk_acc, dv_acc, dk_half, dv_half):
        rr = pl.program_id(1)
        row0 = rr if stride > 1 else 0

        @pl.when((pl.program_id(0) == 0) & (rr == 0))
        def _():
            db_ref[...] = jnp.zeros_like(db_ref)
            dsk_ref[...] = jnp.zeros_like(dsk_ref)

        dk_acc[...] = jnp.zeros_like(dk_acc)
        dv_acc[...] = jnp.zeros_like(dv_acc)
        lane_head = _lane_head(BLOCK)
        expand = _kv_expand_matrix(rr) if kvw != GW else None
        hb = 4 * rr if n_bias == 8 else 0

        def load(ref, ib):
            return _load_rows(ref, _block_rows(row0, stride, ib), split)

        def head_col(x):
            return jnp.concatenate([x[:, h * HEAD_DIM:h * HEAD_DIM + 1] for h in range(4)], axis=0)

        def block(ib, first):
            q4 = _stack_heads(load(q_ref, ib).astype(BF16), lane_head)
            dy4 = _stack_heads(load(dy_ref, ib).astype(BF16), lane_head)
            c4 = head_col(load(c_ref, ib))
            l4 = head_col(load(lse_ref, ib))
            if first:
                kc, vc = load(k_ref, ib).astype(BF16), load(v_ref, ib).astype(BF16)
                b4 = bias_ref[:, :, BLOCK:].reshape(4 * BLOCK, BLOCK)
                krows = pl.ds(0, BLOCK)
            else:
                kc = jnp.concatenate([load(k_ref, ib - 1), load(k_ref, ib)], axis=0).astype(BF16)
                vc = jnp.concatenate([load(v_ref, ib - 1), load(v_ref, ib)], axis=0).astype(BF16)
                b4 = bias_ref[...].reshape(4 * BLOCK, 2 * BLOCK)
                krows = pl.ds(pl.multiple_of((ib - 1) * BLOCK, BLOCK), 2 * BLOCK)
            if expand is not None:
                kc = _dot(kc, expand).astype(BF16)
                vc = _dot(vc, expand).astype(BF16)
            nk = BLOCK if first else 2 * BLOCK
            p = jnp.exp(_dot_nt(q4, kc) * scale + b4 - l4)
            ds = p * (_dot_nt(dy4, vc) - c4)
            ds3 = ds.reshape(4, BLOCK, nk)
            if n_bias == 8:
                if first:
                    db_ref[pl.ds(hb, 4), :, BLOCK:] += ds3
                else:
                    db_ref[pl.ds(hb, 4)] += ds3
            elif first:
                db_ref[:, :, BLOCK:] += ds3
            else:
                db_ref[...] += ds3
            ds16 = ds.astype(BF16)
            dq = _unstack_heads(_dot(ds16, kc), lane_head) * scale
            _store_rows(dq_ref, _block_rows(row0, stride, ib), dq.astype(g_dtype), split)
            dk_acc[krows, :] += _dot_tn(ds16, q4) * scale
            dv_acc[krows, :] += _dot_tn(p.astype(BF16), dy4)
            if has_sink:
                for h in range(4):
                    hs = slice(h * BLOCK, (h + 1) * BLOCK)
                    sk = sink_ref[0, h:h + 1, 0:1]
                    val = -jnp.sum(jnp.exp(sk - l4[hs]) * c4[hs], axis=0, keepdims=True)
                    dsk_ref[hb + h] += jnp.broadcast_to(val, (8, 128))

        block(0, True)
        if nb > 1:
            def step(i, carry):
                block(i, False)
                return carry
            lax.fori_loop(1, nb, step, 0)

        if kvw == GW:
            all_rows = pl.ds(row0, ln, stride=stride) if stride > 1 else pl.ds(0, ln)
            _store_rows(dk_ref, all_rows, dk_acc[...].astype(g_dtype), split)
            _store_rows(dv_ref, all_rows, dv_acc[...].astype(g_dtype), split)
        else:
            def fold(acc):
                t2 = acc[:, :2 * HEAD_DIM] + acc[:, 2 * HEAD_DIM:]
                t2 = t2 + pltpu.roll(t2, HEAD_DIM, 1)
                lane = lax.broadcasted_iota(jnp.int32, t2.shape, 1) // HEAD_DIM
                return jnp.where(lane == rr, t2, 0.0)

            @pl.when(rr == 0)
            def _():
                dk_half[...] = fold(dk_acc[...])
                dv_half[...] = fold(dv_acc[...])

            @pl.when(rr == 1)
            def _():
                dk_ref[0] = (dk_half[...] + fold(dk_acc[...])).astype(g_dtype)
                dv_ref[0] = (dv_half[...] + fold(dv_acc[...])).astype(g_dtype)

    return pl.pallas_call(
        body, name=name, grid=grid,
        out_shape=(jax.ShapeDtypeStruct(dq_shape, g_dtype), jax.ShapeDtypeStruct(dkv_shape, g_dtype),
                   jax.ShapeDtypeStruct(dkv_shape, g_dtype), jax.ShapeDtypeStruct((n_bias, BLOCK, 2 * BLOCK), F32),
                   jax.ShapeDtypeStruct((8, 8, 128), F32)),
        in_specs=[q_spec, k_spec, v_spec,
                  pl.BlockSpec((4, BLOCK, 2 * BLOCK), bias_map), pl.BlockSpec((1, 4, 128), sink_map),
                  o_spec, o_spec, o_spec],
        out_specs=(o_spec, kv_out_spec, kv_out_spec,
                   pl.BlockSpec((n_bias, BLOCK, 2 * BLOCK), lambda n, r: (0, 0, 0)),
                   pl.BlockSpec((8, 8, 128), lambda n, r: (0, 0, 0))),
        scratch_shapes=[pltpu.VMEM((ln, GW), F32), pltpu.VMEM((ln, GW), F32),
                        pltpu.VMEM((ln, 2 * HEAD_DIM), F32), pltpu.VMEM((ln, 2 * HEAD_DIM), F32)],
        compiler_params=_params(("arbitrary", "arbitrary"), VMEM_BIG),
    )(q_arr, k_arr, v_arr, bias, sink, dy, cc, lse)


def _bias_grad(ds_all, buckets):
    def body(ds_ref, bk_ref, o_ref):
        rows = lax.broadcasted_iota(jnp.int32, (N_BUCKETS, 128), 0)
        cols = lax.broadcasted_iota(jnp.int32, (N_BUCKETS, 128), 1)

        def per_bucket(b, acc):
            for h in range(20):
                gi = h // 4 if h < 12 else 3
                v = jnp.where(bk_ref[gi] == b, ds_ref[h], 0.0)
                v = jnp.sum(jnp.sum(v, axis=1, keepdims=True), axis=0, keepdims=True)
                acc = jnp.where((rows == b) & (cols == h), v, acc)
            return acc

        o_ref[...] = lax.fori_loop(0, N_BUCKETS, per_bucket, jnp.zeros((N_BUCKETS, 128), F32))

    vm = pl.BlockSpec(memory_space=pltpu.VMEM)
    return pl.pallas_call(body, name="bias_grad", out_shape=jax.ShapeDtypeStruct((N_BUCKETS, 128), F32),
                          in_specs=[vm, vm], out_specs=vm)(ds_all, buckets)


def _adamw(w, g, m, v, name):
    r, c = w.shape
    tr = r
    for cand in (256, 176, 128, 64, 32, 16, 8):
        if r % cand == 0:
            tr = cand
            break
    bc1 = 1.0 - ADAM_B1 ** ADAM_STEP
    bc2 = 1.0 - ADAM_B2 ** ADAM_STEP

    def body(w_ref, g_ref, m_ref, v_ref, d_ref, nm_ref, nv_ref):
        gv = g_ref[...]
        nm = ADAM_B1 * m_ref[...] + (1.0 - ADAM_B1) * gv
        nv = ADAM_B2 * v_ref[...] + (1.0 - ADAM_B2) * (gv * gv)
        nm_ref[...] = nm
        nv_ref[...] = nv
        d_ref[...] = -ADAM_LR * ((nm / bc1) / (jnp.sqrt(nv / bc2) + ADAM_EPS) + ADAM_WD * w_ref[...])

    spec = pl.BlockSpec((tr, c), lambda i: (i, 0))
    shp = jax.ShapeDtypeStruct((r, c), F32)
    return pl.pallas_call(body, name=name, grid=(r // tr,), out_shape=(shp, shp, shp),
                          in_specs=[spec] * 4, out_specs=(spec, spec, spec),
                          compiler_params=_params(("parallel",)))(w, g, m, v)


def _t5_bucket(dist):
    max_exact = N_BUCKETS // 2
    n = jnp.maximum(dist, 0)
    nf = jnp.maximum(n, 1).astype(F32)
    large = max_exact + (jnp.log(nf / max_exact) / math.log(MAX_DISTANCE / max_exact)
                         * (N_BUCKETS - max_exact)).astype(jnp.int32)
    large = jnp.minimum(large, N_BUCKETS - 1)
    return jnp.where(n < max_exact, n, large)


def _bias_tables(rel_bias):
    qi = jnp.arange(BLOCK)[:, None]
    ki = jnp.arange(2 * BLOCK)[None, :]
    dist = qi + BLOCK - ki
    specs = [(d, w // d, 4 * gi, 4 * gi + 4) for gi, (w, d) in enumerate(DIL_GROUPS)] + [(1, B_WINDOW - 1, 12, 20)]
    biases, buckets = [], []
    for stride, steps, h0, h1 in specs:
        valid = (dist >= 0) & (dist <= steps)
        bk = jnp.where(valid, _t5_bucket(dist * stride), -1).astype(jnp.int32)
        onehot = (bk[None, :, :] == jnp.arange(N_BUCKETS, dtype=jnp.int32)[:, None, None]).astype(F32)
        b = jnp.einsum("bqk,bh->hqk", onehot, rel_bias[:, h0:h1], precision=lax.Precision.HIGHEST)
        biases.append(jnp.where(valid[None], b, NEG))
        buckets.append(bk)
    return jnp.concatenate(biases, axis=0), jnp.stack(buckets, axis=0)


def _local_step(x, tgt, W, S, shards=None):
    nseq, seq, _ = x.shape
    t = nseq * seq
    xf = x.reshape(t, D_MODEL)
    bias_all, buckets = _bias_tables(S["rel_bias"])
    sink_b = jnp.broadcast_to(S["sinks"].reshape(2, 4, 1), (2, 4, 128)).astype(F32)
    sink_0 = jnp.zeros((1, 4, 128), F32)
    dist = shards is not None
    W = dict(W)
    G, GS, reduced = {}, {}, {}

    def put(keys, gathered):
        for k, g in zip(keys, gathered):
            W[k] = g.reshape(_FULL_SHAPE.get(k, (N_CHIPS * shards[k].shape[0], D_MODEL)))

    def gather_rider(keys):
        return _GatherRider([shards[k] for k in keys]) if dist else None

    def pair(keys):
        return _pair_reduce([G[k].reshape(N_CHIPS, 2, shards[k].shape[0] // 2, D_MODEL) for k in keys],
                            "grad_pair_reduce_" + keys[0])

    def finish(keys, own, rec):
        full = _final_reduce(own, rec, "grad_final_reduce_" + keys[0])
        off = 0
        for k in keys:
            r = shards[k].shape[0]
            reduced[k] = full[:, off:off + r // 2].reshape(r, D_MODEL)
            off += r // 2

    if dist:
        first = ("wgt1", "wut1", "wd1")
        put(first, _gather_rows([shards[k] for k in first]))
    keys = ("wint", "wgt2")
    (h1, n1, g1, u1), ro = _ffn_fwd(xf, S["ffn1_norm"], W["wgt1"], W["wut1"], W["wd1"], rider=gather_rider(keys))
    put(keys, ro)
    keys = ("wout", "wat", "wbt", "wut2")
    (un, za, zb, zg), ro = _inproj_fwd(h1, S["mix_norm"], W["wint"], S["b_in"], nseq, rider=gather_rider(keys))
    put(keys, ro)

    seq3 = lambda a: a.reshape(nseq, seq, a.shape[-1])
    zb3 = seq3(zb)
    pair_blk = lambda cb: pl.BlockSpec((1, 2, seq, 128), lambda n, r, cb=cb: (n, cb, 0, 0))
    a_cfg = []
    outs, lses = [], []
    for gi, (_, d) in enumerate(DIL_GROUPS):
        cfg = dict(grid=(nseq, d), seq=seq, stride=d, kvw=GW, split=True,
                   q_spec=pair_blk(gi), k_spec=pair_blk(3 + gi), v_spec=pair_blk(6 + gi), o_spec=pair_blk(0),
                   bias_map=lambda n, r: (0, 0, 0), sink_map=lambda n, r: (0, 0, 0), has_sink=False)
        a_cfg.append(cfg)
        (o, lse), _ = _attn_fwd(za, za, za, bias_all[4 * gi:4 * gi + 4], sink_0, o_shape=(nseq, 2, seq, 128),
                                o_dtype=F32, name=f"attn_a{gi}_fwd", **cfg)
        outs.append(o)
        lses.append(lse)
    wide_blk = lambda w, cmap: pl.BlockSpec((1, seq, w), cmap)
    b_cfg = dict(grid=(nseq, 2), seq=seq, stride=1, kvw=2 * HEAD_DIM, split=False,
                 q_spec=wide_blk(GW, lambda n, r: (n, 0, r)), k_spec=wide_blk(2 * HEAD_DIM, lambda n, r: (n, 0, 4)),
                 v_spec=wide_blk(2 * HEAD_DIM, lambda n, r: (n, 0, 5)), o_spec=wide_blk(GW, lambda n, r: (n, 0, r)),
                 bias_map=lambda n, r: (r, 0, 0), sink_map=lambda n, r: (r, 0, 0), has_sink=True)
    keys = ("wd2",)
    (yb, lse_b), ro = _attn_fwd(zb3, zb3, zb3, bias_all[12:20], sink_b, o_shape=(nseq, seq, 2 * GW), o_dtype=BF16,
                                name="attn_b_fwd", rider=gather_rider(keys), **b_cfg)
    put(keys, ro)
    yb = yb.reshape(t, 2 * GW)

    h2, y, lse_tot, pa, pb, merged = _merge_fwd(outs[0], outs[1], outs[2], lses[0], lses[1], lses[2], yb, zg, h1,
                                                W["wat"], W["wbt"], W["wout"])
    (h3, n2, g2, u2), _ = _ffn_fwd(h2, S["ffn2_norm"], W["wgt2"], W["wut2"], W["wd2"])
    dh3, loss_part, g_final = _loss_head(h3, S["final_norm"].reshape(1, D_MODEL), tgt.reshape(t, D_MODEL))

    GS["final_norm"] = g_final
    dh2, dg2, du2, a2, df2, GS["ffn2_norm"] = _ffn_bwd(dh3, h2, S["ffn2_norm"], g2, u2, W["wgt2"], W["wut2"], W["wd2"])
    G["wgt2"] = _wgrad(dg2, n2, D_FF, name="wgrad_gate2")
    G["wut2"] = _wgrad(du2, n2, D_FF, name="wgrad_up2")
    G["wd2"] = _wgrad(a2, df2, D_FF, name="wgrad_down2")

    keys = ("wgt2", "wut2", "wd2")
    rider = _ExchangeRider([pair(keys)]) if dist else None
    (dpa, dpb, dga, dgb, dya, dyb, dh2b, ca, cb), ro = _merge_bwd(dh2, pa, pb, zg, y, yb, W["wat"], W["wbt"], W["wout"],
                                                                  nseq, rider=rider)
    if dist:
        finish(keys, *ro)
    G["wout"] = _wgrad(merged, dh2b, D_MODEL, name="wgrad_out")
    G["wat"] = _wgrad(dpa, y, D_MODEL, name="wgrad_branch_a")
    G["wbt"] = _wgrad(dpb, yb, D_MODEL, name="wgrad_branch_b")

    dqs, dks, dvs, dbs = [], [], [], []
    shp = (nseq, 2, seq, 128)
    halves = lambda a: [a[:, hf].reshape(t, 128).astype(BF16) for hf in range(2)]
    for gi in range(len(DIL_GROUPS)):
        dq, dk, dv, db, _ = _attn_bwd(za, za, za, bias_all[4 * gi:4 * gi + 4], sink_0, dya, ca, lse_tot,
                                      n_bias=4, dq_shape=shp, dkv_shape=shp, g_dtype=F32,
                                      kv_out_spec=a_cfg[gi]["o_spec"], name=f"attn_a{gi}_bwd", **a_cfg[gi])
        dqs += halves(dq)
        dks += halves(dk)
        dvs += halves(dv)
        dbs.append(db)
    dqb, dkb, dvb, dbb, dsink = _attn_bwd(zb3, zb3, zb3, bias_all[12:20], sink_b, seq3(dyb), seq3(cb), lse_b,
                                          n_bias=8, dq_shape=(nseq, seq, 2 * GW),
                                          dkv_shape=(nseq, seq, 2 * HEAD_DIM), g_dtype=BF16,
                                          kv_out_spec=wide_blk(2 * HEAD_DIM, lambda n, r: (n, 0, 0)),
                                          name="attn_b_bwd", **b_cfg)
    dz = jnp.concatenate(dqs + dks + dvs + [dqb.reshape(t, 2 * GW), dkb.reshape(t, 2 * HEAD_DIM),
                                            dvb.reshape(t, 2 * HEAD_DIM), dga, dgb], axis=-1)
    gb_tab = _bias_grad(jnp.concatenate(dbs + [dbb], axis=0), buckets)
    GS["rel_bias"] = gb_tab[:, :20]
    GS["sinks"] = dsink[:, 0, 0].reshape(1, 8)

    G["wint"], GS["b_in"] = _wgrad(dz, un, D_IN // 2, with_colsum=True, name="wgrad_in")
    keys = ("wint", "wout", "wat", "wbt")
    rider = _ExchangeRider([pair(keys)]) if dist else None
    (dh1, GS["mix_norm"]), ro = _inproj_bwd(dz, dh2, h1, S["mix_norm"], W["wint"], rider=rider)
    if dist:
        finish(keys, *ro)

    dx, dg1, du1, a1, df1, GS["ffn1_norm"] = _ffn_bwd(dh1, xf, S["ffn1_norm"], g1, u1, W["wgt1"], W["wut1"], W["wd1"])
    G["wgt1"] = _wgrad(dg1, n1, D_FF, name="wgrad_gate1")
    if dist:
        G["wut1"], ro = _wgrad(du1, n1, D_FF, name="wgrad_up1", rider=_ExchangeRider([pair(("wgt1",))]))
        finish(("wgt1",), *ro)
        G["wd1"], ro = _wgrad(a1, df1, D_FF, name="wgrad_down1", rider=_ExchangeRider([pair(("wut1",))]))
        finish(("wut1",), *ro)
        finish(("wd1",), *_chip_exchange([pair(("wd1",))]))
    else:
        G["wut1"] = _wgrad(du1, n1, D_FF, name="wgrad_up1")
        G["wd1"] = _wgrad(a1, df1, D_FF, name="wgrad_down1")
    return loss_part, dx.reshape(x.shape), (reduced if dist else G), GS


_SMALL = ("ffn1_norm", "mix_norm", "ffn2_norm", "final_norm", "b_in", "sinks", "rel_bias")
_ORDER = ("ffn1_norm", "ffn1_w_gate", "ffn1_w_up", "ffn1_w_down", "mix_norm", "w_in", "b_in", "w_branch_a",
          "w_branch_b", "w_out", "sinks", "rel_bias", "ffn2_norm", "ffn2_w_gate", "ffn2_w_up", "ffn2_w_down",
          "final_norm")
_BIG = (("wgt1", "ffn1_w_gate", True, 704), ("wut1", "ffn1_w_up", True, 704), ("wd1", "ffn1_w_down", False, 704),
        ("wint", "w_in", True, 1280), ("wout", "w_out", False, 256), ("wat", "w_branch_a", True, 64),
        ("wbt", "w_branch_b", True, 128), ("wgt2", "ffn2_w_gate", True, 704), ("wut2", "ffn2_w_up", True, 704),
        ("wd2", "ffn2_w_down", False, 704))
_FULL_SHAPE = {"wat": (D_MODEL, GW), "wbt": (D_MODEL, 2 * GW)}


def _pack_small(p, extra=None):
    last = [p["sinks"].reshape(8), p["rel_bias"].reshape(640)]
    used = 648
    if extra is not None:
        last.append(extra.reshape(1))
        used += 1
    last.append(jnp.zeros((D_MODEL - used,), F32))
    rows = [p["ffn1_norm"].reshape(1, D_MODEL), p["mix_norm"].reshape(1, D_MODEL), p["ffn2_norm"].reshape(1, D_MODEL),
            p["final_norm"].reshape(1, D_MODEL), p["b_in"].reshape(5, D_MODEL), jnp.concatenate(last).reshape(1, D_MODEL),
            jnp.zeros((6, D_MODEL), F32)]
    return jnp.concatenate(rows, axis=0)


def _unpack_small(a):
    return {"ffn1_norm": a[0:1], "mix_norm": a[1:2], "ffn2_norm": a[2:3], "final_norm": a[3],
            "b_in": a[4:9].reshape(1, D_IN), "sinks": a[9, 0:8].reshape(1, 8), "rel_bias": a[9, 8:648].reshape(32, 20)}


def kernel(x, ffn1_norm, ffn1_w_gate, ffn1_w_up, ffn1_w_down, mix_norm, w_in, b_in, w_branch_a, w_branch_b, w_out, sinks, rel_bias, ffn2_norm, ffn2_w_gate, ffn2_w_up, ffn2_w_down, final_norm, loss_target, m_ffn1_norm, m_ffn1_w_gate, m_ffn1_w_up, m_ffn1_w_down, m_mix_norm, m_w_in, m_b_in, m_w_branch_a, m_w_branch_b, m_w_out, m_sinks, m_rel_bias, m_ffn2_norm, m_ffn2_w_gate, m_ffn2_w_up, m_ffn2_w_down, m_final_norm, v_ffn1_norm, v_ffn1_w_gate, v_ffn1_w_up, v_ffn1_w_down, v_mix_norm, v_w_in, v_b_in, v_w_branch_a, v_w_branch_b, v_w_out, v_sinks, v_rel_bias, v_ffn2_norm, v_ffn2_w_gate, v_ffn2_w_up, v_ffn2_w_down, v_final_norm):
    args = dict(locals())
    w = {n: args[n] for n in _ORDER}
    m = {n: args["m_" + n] for n in _ORDER}
    v = {n: args["v_" + n] for n in _ORDER}

    shards = {}
    for key, name, transposed, rows in _BIG:
        a = w[name][0]
        a = (a.T if transposed else a).astype(BF16)
        shards[key] = a.reshape(rows, D_MODEL)
    S = {n: w[n] for n in _SMALL}

    loss_part, grad_x, reduced, GS = _local_step(x, loss_target, {}, S, shards)

    grads = {}
    for key, name, transposed, rows in _BIG:
        nat = w[name][0].shape
        grads[name] = reduced[key].reshape(nat[1], nat[0]).T if transposed else reduced[key].reshape(nat)

    small = _allreduce_small(_pack_small(GS, extra=loss_part[0, 0]))
    loss = small[9, 648]

    out_g, out_d, out_m, out_v = {}, {}, {}, {}
    for _, n, _, _ in _BIG:
        d_, nm_, nv_ = _adamw(w[n][0], grads[n], m[n][0], v[n][0], "adamw_" + n)
        out_g[n], out_d[n], out_m[n], out_v[n] = grads[n][None], d_[None], nm_[None], nv_[None]
    d_s, m_s, v_s = _adamw(_pack_small(w), small, _pack_small(m), _pack_small(v), "adamw_small")
    for dst, src in ((out_g, small), (out_d, d_s), (out_m, m_s), (out_v, v_s)):
        dst.update(_unpack_small(src))

    return (loss, grad_x, *[out_g[n] for n in _ORDER], *[out_d[n] for n in _ORDER],
            *[out_m[n] for n in _ORDER], *[out_v[n] for n in _ORDER])
```

```python
import math

import jax
import jax.numpy as jnp
from jax import lax
from jax.experimental import pallas as pl
from jax.experimental.pallas import tpu as pltpu

F32, BF16 = jnp.float32, jnp.bfloat16
MESH = pl.DeviceIdType.MESH

D_MODEL = 1024
D_FF = 2816
D_IN = 5120
HEAD_DIM = 64
BLOCK = 128
DIL_GROUPS = ((128, 1), (512, 4), (2048, 16))
B_WINDOW = 128
N_BUCKETS = 32
MAX_DISTANCE = 2048
EPS = 1e-6
N_CHIPS = 4
GW = 256
ZA_W = 2304
ZB_W = 768
NEG = -1e30

ADAM_LR, ADAM_B1, ADAM_B2, ADAM_EPS, ADAM_WD, ADAM_STEP = 0.001, 0.9, 0.999, 1e-08, 0.01, 10

VMEM_BIG = 56 * 1024 * 1024
TM = 512
TM_BWD = 256
FF_CHUNKS = 2
DMA_SPLIT = 8
RESIDUES_PER_STEP = 4


def _dot(a, b):
    return jnp.dot(a, b, preferred_element_type=F32)


def _dot_nt(a, b):
    return lax.dot_general(a, b, (((1,), (1,)), ((), ())), preferred_element_type=F32)


def _dot_tn(a, b):
    return lax.dot_general(a, b, (((0,), (0,)), ((), ())), preferred_element_type=F32)


def _sigmoid(x):
    return 1.0 / (1.0 + jnp.exp(-x))


def _params(sem, vmem=None):
    return pltpu.CompilerParams(dimension_semantics=sem, vmem_limit_bytes=vmem)


ANY = pl.BlockSpec(memory_space=pl.ANY)


def _me():
    return lax.axis_index("x"), lax.axis_index("y"), lax.axis_index("c")


_CHIP_RELS = ((1, 0), (0, 1), (1, 1))


def _flip(v, f):
    return 1 - v if f else v


def _remote(src, dst, ssem, rsem, peer):
    return pltpu.make_async_remote_copy(src_ref=src, dst_ref=dst, send_sem=ssem, recv_sem=rsem,
                                        device_id=peer, device_id_type=MESH)


def _row_pieces(rows, n):
    step = max(16, -(-rows // n) // 16 * 16)
    out, s = [], 0
    while s < rows:
        out.append((s, min(step, rows - s)))
        s += step
    return out


def _gather_rows(shards):
    nt = len(shards)
    rows = [s.shape[0] for s in shards]

    def body(*refs):
        srcs, outs = refs[:nt], refs[nt:2 * nt]
        ici_s, ici_r, d2d_s, d2d_r, loc = refs[2 * nt:]
        x, y, c = _me()
        j = 2 * x + y
        sib = (x, y, 1 - c)
        local = [pltpu.make_async_copy(srcs[t], outs[t].at[j], loc.at[t]) for t in range(nt)]
        for cp in local:
            cp.start()
        sends = []
        for k, (fx, fy) in enumerate(_CHIP_RELS):
            peer = (_flip(x, fx), _flip(y, fy), c)
            for t in range(nt):
                half = pl.ds(c * (rows[t] // 2), rows[t] // 2)
                cp = _remote(srcs[t].at[half], outs[t].at[j, half], ici_s.at[3 * t + k], ici_r.at[3 * t + k], peer)
                cp.start()
                sends.append(cp)
        fwds = []
        for k, (fx, fy) in enumerate(_CHIP_RELS):
            pj = 2 * _flip(x, fx) + _flip(y, fy)
            for t in range(nt):
                half = pl.ds(c * (rows[t] // 2), rows[t] // 2)
                blk = outs[t].at[pj, half]
                _remote(blk, blk, ici_s.at[3 * t + k], ici_r.at[3 * t + k], sib).wait_recv()
                cp = _remote(blk, blk, d2d_s.at[3 * t + k], d2d_r.at[3 * t + k], sib)
                cp.start()
                fwds.append(cp)
        for cp in fwds:
            cp.wait()
        for cp in sends:
            cp.wait_send()
        for cp in local:
            cp.wait()

    sems = [pltpu.SemaphoreType.DMA((3 * nt,)) for _ in range(4)] + [pltpu.SemaphoreType.DMA((nt,))]
    return pl.pallas_call(
        body, name="gather_weights",
        out_shape=tuple(jax.ShapeDtypeStruct((N_CHIPS,) + s.shape, s.dtype) for s in shards),
        in_specs=[ANY] * nt, out_specs=tuple([ANY] * nt), scratch_shapes=sems,
    )(*shards)


VMEM_WHOLE = pl.BlockSpec(memory_space=pltpu.VMEM)


def _pair_reduce(grads, name):
    nt = len(grads)
    r2 = [g.shape[2] for g in grads]
    off = [sum(r2[:t]) for t in range(nt)]
    tot = sum(r2)

    def body(*refs):
        gs = refs[:nt]
        s_ref, got, ssem, rsem = refs[nt:]
        x, y, c = _me()
        sib = (x, y, 1 - c)
        for t in range(nt):
            for k in range(N_CHIPS):
                _remote(gs[t].at[k, 1 - c], got.at[k, pl.ds(off[t], r2[t])], ssem, rsem, sib).start()
        _remote(got, got, ssem, rsem, sib).wait()
        for t in range(nt):
            for k in range(N_CHIPS):
                rows = slice(off[t], off[t] + r2[t])
                s_ref[k, rows, :] = (gs[t][k, c].astype(F32) + got[k, rows, :].astype(F32)).astype(BF16)

    shp = jax.ShapeDtypeStruct((N_CHIPS, tot, D_MODEL), BF16)
    return pl.pallas_call(
        body, name=name, out_shape=shp, in_specs=[VMEM_WHOLE] * nt, out_specs=VMEM_WHOLE,
        scratch_shapes=[pltpu.VMEM((N_CHIPS, tot, D_MODEL), BF16), pltpu.SemaphoreType.DMA(()),
                        pltpu.SemaphoreType.DMA(())],
        compiler_params=pltpu.CompilerParams(vmem_limit_bytes=VMEM_BIG),
    )(*grads)


def _chip_exchange(parts):
    ng = len(parts)
    r2 = [p.shape[1] for p in parts]
    off = [sum(r2[:g]) for g in range(ng)]
    tot = sum(r2)

    def body(*refs):
        ps = refs[:ng]
        own_ref, rec_ref, ssems, rsems, lsem = refs[ng:]
        x, y, c = _me()
        j = 2 * x + y
        for g in range(ng):
            pltpu.make_async_copy(ps[g].at[j], own_ref.at[pl.ds(off[g], r2[g])], lsem).start()
        for k, (fx, fy) in enumerate(_CHIP_RELS):
            px, py = _flip(x, fx), _flip(y, fy)
            for g in range(ng):
                for st, sz in _row_pieces(r2[g], 2):
                    _remote(ps[g].at[2 * px + py, pl.ds(st, sz)], rec_ref.at[k, pl.ds(off[g] + st, sz)],
                            ssems.at[k], rsems.at[k], (px, py, c)).start()
        for k in range(3):
            _remote(rec_ref.at[k], rec_ref.at[k], ssems.at[k], rsems.at[k], (x, y, c)).wait()
        pltpu.make_async_copy(own_ref, own_ref, lsem).wait()

    return pl.pallas_call(
        body, name="grad_chip_exchange",
        out_shape=(jax.ShapeDtypeStruct((tot, D_MODEL), BF16), jax.ShapeDtypeStruct((3, tot, D_MODEL), BF16)),
        in_specs=[ANY] * ng, out_specs=(ANY, ANY),
        scratch_shapes=[pltpu.SemaphoreType.DMA((3,)), pltpu.SemaphoreType.DMA((3,)), pltpu.SemaphoreType.DMA(())],
    )(*parts)


def _final_reduce(own, rec, name):
    r2 = own.shape[0]
    pieces = _row_pieces(r2, DMA_SPLIT)

    def body(own_ref, rec_ref, o_ref, fbuf, ssem, rsem, lsem):
        x, y, c = _me()
        sib = (x, y, 1 - c)
        for st, sz in pieces:
            rows = slice(st, st + sz)
            fbuf[rows, :] = (own_ref[rows, :].astype(F32) + rec_ref[0, rows, :].astype(F32)
                             + rec_ref[1, rows, :].astype(F32) + rec_ref[2, rows, :].astype(F32))
            pltpu.make_async_copy(fbuf.at[pl.ds(st, sz)], o_ref.at[c, pl.ds(st, sz)], lsem).start()
            _remote(fbuf.at[pl.ds(st, sz)], o_ref.at[c, pl.ds(st, sz)], ssem, rsem, sib).start()
        _remote(fbuf, o_ref.at[c], ssem, rsem, sib).wait()
        pltpu.make_async_copy(fbuf, o_ref.at[c], lsem).wait()

    return pl.pallas_call(
        body, name=name, out_shape=jax.ShapeDtypeStruct((2, r2, D_MODEL), F32),
        in_specs=[VMEM_WHOLE, VMEM_WHOLE], out_specs=ANY,
        scratch_shapes=[pltpu.VMEM((r2, D_MODEL), F32), pltpu.SemaphoreType.DMA(()), pltpu.SemaphoreType.DMA(()),
                        pltpu.SemaphoreType.DMA(())],
        compiler_params=pltpu.CompilerParams(vmem_limit_bytes=VMEM_BIG),
    )(own, rec)


def _allreduce_small(vec):
    def body(v_ref, o_ref, buf, send_sems, recv_sems):
        x, y, c = _me()
        me = 4 * x + 2 * y + c
        buf[me] = v_ref[...]
        copies = []
        for k in range(1, 8):
            peer = (_flip(x, (k >> 2) & 1), _flip(y, (k >> 1) & 1), _flip(c, k & 1))
            cp = _remote(v_ref, buf.at[me], send_sems.at[k - 1], recv_sems.at[k - 1], peer)
            cp.start()
            copies.append(cp)
        for cp in copies:
            cp.wait()
        acc = buf[0]
        for i in range(1, 8):
            acc = acc + buf[i]
        o_ref[...] = acc

    vm = pl.BlockSpec(memory_space=pltpu.VMEM)
    return pl.pallas_call(
        body, name="allreduce_small", out_shape=jax.ShapeDtypeStruct(vec.shape, vec.dtype),
        in_specs=[vm], out_specs=vm,
        scratch_shapes=[pltpu.VMEM((8,) + vec.shape, vec.dtype), pltpu.SemaphoreType.DMA((7,)),
                        pltpu.SemaphoreType.DMA((7,))],
    )(vec)


class _GatherRider:
    def __init__(self, shards):
        self.inputs = list(shards)
        nt = len(shards)
        self.out_shape = [jax.ShapeDtypeStruct((N_CHIPS,) + s.shape, s.dtype) for s in shards]
        self.scratch = [pltpu.SemaphoreType.DMA((3 * nt,)), pltpu.SemaphoreType.DMA((3 * nt,)),
                        pltpu.SemaphoreType.DMA((nt,))]

    def _copies(self, srcs, outs, sems):
        ici_s, ici_r, loc = sems
        x, y, c = _me()
        j = 2 * x + y
        local = [pltpu.make_async_copy(srcs[t], outs[t].at[j], loc.at[t]) for t in range(len(srcs))]
        remote = []
        for k, (fx, fy) in enumerate(_CHIP_RELS):
            peer = (_flip(x, fx), _flip(y, fy), c)
            for t in range(len(srcs)):
                remote.append(_remote(srcs[t], outs[t].at[j], ici_s.at[3 * t + k], ici_r.at[3 * t + k], peer))
        return local, remote

    def start(self, srcs, outs, sems):
        local, remote = self._copies(srcs, outs, sems)
        for cp in local + remote:
            cp.start()

    def finish(self, srcs, outs, sems):
        local, remote = self._copies(srcs, outs, sems)
        for cp in remote + local:
            cp.wait()


class _ExchangeRider:
    def __init__(self, parts):
        self.inputs = list(parts)
        self.r2 = [p.shape[1] for p in parts]
        self.off = [sum(self.r2[:g]) for g in range(len(parts))]
        tot = sum(self.r2)
        self.out_shape = [jax.ShapeDtypeStruct((tot, D_MODEL), BF16), jax.ShapeDtypeStruct((3, tot, D_MODEL), BF16)]
        self.scratch = [pltpu.SemaphoreType.DMA((3,)), pltpu.SemaphoreType.DMA((3,)), pltpu.SemaphoreType.DMA(())]

    def start(self, ps, outs, sems):
        own_ref, rec_ref = outs
        ssems, rsems, lsem = sems
        x, y, c = _me()
        j = 2 * x + y
        for g in range(len(ps)):
            pltpu.make_async_copy(ps[g].at[j], own_ref.at[pl.ds(self.off[g], self.r2[g])], lsem).start()
        for k, (fx, fy) in enumerate(_CHIP_RELS):
            px, py = _flip(x, fx), _flip(y, fy)
            for g in range(len(ps)):
                for st, sz in _row_pieces(self.r2[g], 2):
                    _remote(ps[g].at[2 * px + py, pl.ds(st, sz)], rec_ref.at[k, pl.ds(self.off[g] + st, sz)],
                            ssems.at[k], rsems.at[k], (px, py, c)).start()

    def finish(self, ps, outs, sems):
        own_ref, rec_ref = outs
        ssems, rsems, lsem = sems
        x, y, c = _me()
        for k in range(3):
            _remote(rec_ref.at[k], rec_ref.at[k], ssems.at[k], rsems.at[k], (x, y, c)).wait()
        pltpu.make_async_copy(own_ref, own_ref, lsem).wait()


def _pallas(body, args, *, name, grid, in_specs, out_specs, out_shape, scratch_shapes=(), sem=None, vmem=None,
            rider=None):
    if rider is None:
        res = pl.pallas_call(body, name=name, grid=grid, in_specs=list(in_specs), out_specs=tuple(out_specs),
                             out_shape=tuple(out_shape), scratch_shapes=list(scratch_shapes),
                             compiler_params=_params(sem, vmem))(*args)
        return tuple(res), ()
    n_in, n_out, n_sc = len(in_specs), len(out_shape), len(scratch_shapes)
    r_in, r_out = len(rider.inputs), len(rider.out_shape)

    def wrapped(*refs):
        ins, rins = refs[:n_in], refs[n_in:n_in + r_in]
        p = n_in + r_in
        outs, routs = refs[p:p + n_out], refs[p + n_out:p + n_out + r_out]
        p += n_out + r_out
        scr, rsems = refs[p:p + n_sc], refs[p + n_sc:]
        first = pl.program_id(0) == 0
        last = pl.program_id(0) == grid[0] - 1
        for a in range(1, len(grid)):
            first = first & (pl.program_id(a) == 0)
            last = last & (pl.program_id(a) == grid[a] - 1)

        @pl.when(first)
        def _():
            rider.start(rins, routs, rsems)

        body(*ins, *outs, *scr)

        @pl.when(last)
        def _():
            rider.finish(rins, routs, rsems)

    res = pl.pallas_call(
        wrapped, name=name, grid=grid, in_specs=list(in_specs) + [ANY] * r_in,
        out_specs=tuple(out_specs) + (ANY,) * r_out, out_shape=tuple(out_shape) + tuple(rider.out_shape),
        scratch_shapes=list(scratch_shapes) + rider.scratch,
        compiler_params=_params(("arbitrary",) * len(grid), vmem))(*args, *rider.inputs)
    return tuple(res[:n_out]), tuple(res[n_out:])


def _ffn_fwd(h, gain, wgt, wut, wd, rider=None):
    t = h.shape[0]
    fc = D_FF // FF_CHUNKS

    def body(h_ref, gain_ref, wg_hbm, wu_hbm, wd_hbm, hout_ref, n_ref, g_ref, u_ref, wg_v, wu_v, wd_v):
        @pl.when(pl.program_id(0) == 0)
        def _():
            pltpu.sync_copy(wg_hbm, wg_v)
            pltpu.sync_copy(wu_hbm, wu_v)
            pltpu.sync_copy(wd_hbm, wd_v)

        hh = h_ref[...]
        r = lax.rsqrt(jnp.mean(hh * hh, axis=-1, keepdims=True) + EPS)
        n = (hh * r * gain_ref[...]).astype(BF16)
        n_ref[...] = n
        acc = jnp.zeros((TM, D_MODEL), F32)
        for ci in range(FF_CHUNKS):
            sl = slice(ci * fc, (ci + 1) * fc)
            g = _dot_nt(n, wg_v[sl, :])
            u = _dot_nt(n, wu_v[sl, :])
            g_ref[:, sl] = g.astype(BF16)
            u_ref[:, sl] = u.astype(BF16)
            a = (g * _sigmoid(g) * u).astype(BF16)
            acc = acc + _dot(a, wd_v[sl, :])
        hout_ref[...] = hh + 0.5 * acc

    row = lambda w: pl.BlockSpec((TM, w), lambda i: (i, 0))
    wv = pltpu.VMEM((D_FF, D_MODEL), BF16)
    return _pallas(
        body, (h, gain, wgt, wut, wd), name="ffn_fwd", grid=(t // TM,),
        out_shape=(jax.ShapeDtypeStruct((t, D_MODEL), F32), jax.ShapeDtypeStruct((t, D_MODEL), BF16),
                   jax.ShapeDtypeStruct((t, D_FF), BF16), jax.ShapeDtypeStruct((t, D_FF), BF16)),
        in_specs=[row(D_MODEL), pl.BlockSpec((1, D_MODEL), lambda i: (0, 0)), ANY, ANY, ANY],
        out_specs=(row(D_MODEL), row(D_MODEL), row(D_FF), row(D_FF)),
        scratch_shapes=[wv, wv, wv], sem=("arbitrary",), vmem=VMEM_BIG, rider=rider)


def _ffn_bwd(dhout, h, gain, g, u, wgt, wut, wd):
    t = h.shape[0]
    tm = TM_BWD
    fc = D_FF // FF_CHUNKS

    def body(dho_ref, h_ref, gain_ref, g_ref, u_ref, wg_hbm, wu_hbm, wd_hbm,
             dh_ref, dg_ref, du_ref, a_ref, df_ref, gg_ref, wg_v, wu_v, wd_v):
        @pl.when(pl.program_id(0) == 0)
        def _():
            pltpu.sync_copy(wg_hbm, wg_v)
            pltpu.sync_copy(wu_hbm, wu_v)
            pltpu.sync_copy(wd_hbm, wd_v)
            gg_ref[...] = jnp.zeros_like(gg_ref)

        dho = dho_ref[...]
        df = (0.5 * dho).astype(BF16)
        df_ref[...] = df
        dn = jnp.zeros((tm, D_MODEL), F32)
        for ci in range(FF_CHUNKS):
            sl = slice(ci * fc, (ci + 1) * fc)
            da = _dot_nt(df, wd_v[sl, :])
            gv = g_ref[:, sl].astype(F32)
            uv = u_ref[:, sl].astype(F32)
            sg = _sigmoid(gv)
            silu = gv * sg
            dg = (da * uv * (sg * (1.0 + gv * (1.0 - sg)))).astype(BF16)
            du = (da * silu).astype(BF16)
            dg_ref[:, sl] = dg
            du_ref[:, sl] = du
            a_ref[:, sl] = (silu * uv).astype(BF16)
            dn = dn + _dot(dg, wg_v[sl, :]) + _dot(du, wu_v[sl, :])
        hh = h_ref[...]
        r = lax.rsqrt(jnp.mean(hh * hh, axis=-1, keepdims=True) + EPS)
        hn = hh * r
        gg_ref[...] += jnp.sum(dn * hn, axis=0, keepdims=True)
        dng = dn * gain_ref[...]
        dh_ref[...] = dho + r * (dng - hn * jnp.mean(dng * hn, axis=-1, keepdims=True))

    row = lambda w: pl.BlockSpec((tm, w), lambda i: (i, 0))
    vec = pl.BlockSpec((1, D_MODEL), lambda i: (0, 0))
    wv = pltpu.VMEM((D_FF, D_MODEL), BF16)
    return pl.pallas_call(
        body, name="ffn_bwd", grid=(t // tm,),
        out_shape=(jax.ShapeDtypeStruct((t, D_MODEL), F32), jax.ShapeDtypeStruct((t, D_FF), BF16),
                   jax.ShapeDtypeStruct((t, D_FF), BF16), jax.ShapeDtypeStruct((t, D_FF), BF16),
                   jax.ShapeDtypeStruct((t, D_MODEL), BF16), jax.ShapeDtypeStruct((1, D_MODEL), F32)),
        in_specs=[row(D_MODEL), row(D_MODEL), vec, row(D_FF), row(D_FF), ANY, ANY, ANY],
        out_specs=(row(D_MODEL), row(D_FF), row(D_FF), row(D_FF), row(D_MODEL), vec),
        scratch_shapes=[wv, wv, wv],
        compiler_params=_params(("arbitrary",), VMEM_BIG),
    )(dhout, h, gain, g, u, wgt, wut, wd)


def _wgrad(lhs, rhs, rb, with_colsum=False, name="wgrad", rider=None):
    t, k = lhs.shape
    n = rhs.shape[1]
    tk = 512
    nt = t // tk

    def body(l_ref, r_ref, o_ref, *rest):
        acc = rest[-1]
        ti = pl.program_id(1)

        @pl.when(ti == 0)
        def _():
            acc[...] = jnp.zeros_like(acc)
            if with_colsum:
                rest[0][...] = jnp.zeros_like(rest[0])

        acc[...] += _dot_tn(l_ref[...], r_ref[...])
        if with_colsum:
            rest[0][...] += jnp.sum(l_ref[...].astype(F32), axis=0, keepdims=True)

        @pl.when(ti == nt - 1)
        def _():
            o_ref[...] = acc[...].astype(BF16)

    out_shape = [jax.ShapeDtypeStruct((k, n), BF16)]
    out_specs = [pl.BlockSpec((rb, n), lambda j, i: (j, 0))]
    if with_colsum:
        out_shape.append(jax.ShapeDtypeStruct((1, k), F32))
        out_specs.append(pl.BlockSpec((1, rb), lambda j, i: (0, j)))
    res, ro = _pallas(
        body, (lhs, rhs), name=name, grid=(k // rb, nt), out_shape=tuple(out_shape),
        in_specs=[pl.BlockSpec((tk, rb), lambda j, i: (i, j)), pl.BlockSpec((tk, n), lambda j, i: (i, 0))],
        out_specs=tuple(out_specs), scratch_shapes=[pltpu.VMEM((rb, n), F32)],
        sem=("arbitrary", "arbitrary"), vmem=VMEM_BIG, rider=rider)
    if rider is not None:
        return res[0], ro
    return res if with_colsum else res[0]


def _lane_blocks(nseq, seq, nblk, tm=TM):
    spt = seq // tm
    return pl.BlockSpec((1, nblk, tm, 128), lambda i: (i // spt, 0, i % spt, 0))


def _inproj_fwd(h, gain, wint, b_in, nseq, rider=None):
    t = h.shape[0]
    seq = t // nseq
    half_a = ZA_W // 2
    pieces = ((0, half_a, 0, 0), (half_a, half_a, 0, half_a), (ZA_W, ZB_W, 1, 0), (ZA_W + ZB_W, 1024, 2, 0),
              (ZA_W + ZB_W + 1024, 1024, 2, 1024))

    def body(h_ref, gain_ref, w_hbm, b_ref, u_ref, za_ref, zb_ref, zg_ref, w_v):
        @pl.when(pl.program_id(0) == 0)
        def _():
            pltpu.sync_copy(w_hbm, w_v)

        hh = h_ref[...]
        r = lax.rsqrt(jnp.mean(hh * hh, axis=-1, keepdims=True) + EPS)
        un = (hh * r * gain_ref[...]).astype(BF16)
        u_ref[...] = un
        outs = (None, zb_ref, zg_ref)
        for c0, cw, oi, o0 in pieces:
            val = _dot_nt(un, w_v[c0:c0 + cw, :]) + b_ref[:, c0:c0 + cw]
            if oi == 0:
                for cb in range(cw // 128):
                    za_ref[0, o0 // 128 + cb] = val[:, cb * 128:(cb + 1) * 128]
            else:
                outs[oi][:, o0:o0 + cw] = val.astype(BF16)

    row = lambda w: pl.BlockSpec((TM, w), lambda i: (i, 0))
    return _pallas(
        body, (h, gain, wint, b_in), name="inproj_fwd", grid=(t // TM,),
        out_shape=(jax.ShapeDtypeStruct((t, D_MODEL), BF16), jax.ShapeDtypeStruct((nseq, ZA_W // 128, seq, 128), F32),
                   jax.ShapeDtypeStruct((t, ZB_W), BF16), jax.ShapeDtypeStruct((t, 2 * D_MODEL), BF16)),
        in_specs=[row(D_MODEL), pl.BlockSpec((1, D_MODEL), lambda i: (0, 0)), ANY,
                  pl.BlockSpec((1, D_IN), lambda i: (0, 0))],
        out_specs=(row(D_MODEL), _lane_blocks(nseq, seq, ZA_W // 128), row(ZB_W), row(2 * D_MODEL)),
        scratch_shapes=[pltpu.VMEM((D_IN, D_MODEL), BF16)], sem=("arbitrary",), vmem=VMEM_BIG, rider=rider)


def _inproj_bwd(dz, dh2, h, gain, wint, rider=None):
    t = h.shape[0]
    nc = 5
    cw = D_IN // nc

    def body(dz_ref, dh2_ref, h_ref, gain_ref, w_hbm, dh_ref, gg_ref, w_v):
        @pl.when(pl.program_id(0) == 0)
        def _():
            pltpu.sync_copy(w_hbm, w_v)
            gg_ref[...] = jnp.zeros_like(gg_ref)

        du = jnp.zeros((TM, D_MODEL), F32)
        for ci in range(nc):
            sl = slice(ci * cw, (ci + 1) * cw)
            du = du + _dot(dz_ref[:, sl], w_v[sl, :])
        hh = h_ref[...]
        r = lax.rsqrt(jnp.mean(hh * hh, axis=-1, keepdims=True) + EPS)
        hn = hh * r
        gg_ref[...] += jnp.sum(du * hn, axis=0, keepdims=True)
        dng = du * gain_ref[...]
        dh_ref[...] = dh2_ref[...] + r * (dng - hn * jnp.mean(dng * hn, axis=-1, keepdims=True))

    row = lambda w: pl.BlockSpec((TM, w), lambda i: (i, 0))
    vec = pl.BlockSpec((1, D_MODEL), lambda i: (0, 0))
    return _pallas(
        body, (dz, dh2, h, gain, wint), name="inproj_bwd", grid=(t // TM,),
        out_shape=(jax.ShapeDtypeStruct((t, D_MODEL), F32), jax.ShapeDtypeStruct((1, D_MODEL), F32)),
        in_specs=[row(D_IN), row(D_MODEL), row(D_MODEL), vec, ANY],
        out_specs=(row(D_MODEL), vec),
        scratch_shapes=[pltpu.VMEM((D_IN, D_MODEL), BF16)], sem=("arbitrary",), vmem=VMEM_BIG, rider=rider)


def _head_sum_matrix(w):
    i = lax.broadcasted_iota(jnp.int32, (w, w), 0) // HEAD_DIM
    j = lax.broadcasted_iota(jnp.int32, (w, w), 1) // HEAD_DIM
    return (i == j).astype(F32)


def _merge_fwd(o0, o1, o2, l0, l1, l2, yb, zg, h1, wat, wbt, wout):
    t = h1.shape[0]
    nseq, _, seq, _ = o0.shape

    def body(o0_ref, o1_ref, o2_ref, l0_ref, l1_ref, l2_ref, yb_ref, ga_ref, gb_ref, h1_ref, wa_ref, wb_ref, wo_ref,
             h2_ref, y_ref, lt_ref, pa_ref, pb_ref, mg_ref):
        wide = lambda ref: jnp.concatenate([ref[0, 0], ref[0, 1]], axis=1)
        la, lb, lc = wide(l0_ref), wide(l1_ref), wide(l2_ref)
        mx = jnp.maximum(jnp.maximum(la, lb), lc)
        ea, eb, ec = jnp.exp(la - mx), jnp.exp(lb - mx), jnp.exp(lc - mx)
        den = ea + eb + ec
        y = (ea * wide(o0_ref) + eb * wide(o1_ref) + ec * wide(o2_ref)) / den
        lt = mx + jnp.log(den)
        lt_ref[0, 0] = lt[:, :128]
        lt_ref[0, 1] = lt[:, 128:]
        yb16 = y.astype(BF16)
        y_ref[...] = yb16
        pa = _dot_nt(yb16, wa_ref[...])
        pb = _dot_nt(yb_ref[...], wb_ref[...])
        pa_ref[...] = pa.astype(BF16)
        pb_ref[...] = pb.astype(BF16)
        mg = (_sigmoid(ga_ref[...].astype(F32)) * pa + _sigmoid(gb_ref[...].astype(F32)) * pb).astype(BF16)
        mg_ref[...] = mg
        h2_ref[...] = h1_ref[...] + _dot(mg, wo_ref[...])

    row = lambda w: pl.BlockSpec((TM, w), lambda i: (i, 0))
    full = lambda a: pl.BlockSpec(a.shape, lambda i: (0, 0))
    gate = lambda cb: pl.BlockSpec((TM, D_MODEL), lambda i: (i, cb))
    return pl.pallas_call(
        body, name="merge_fwd", grid=(t // TM,),
        out_shape=(jax.ShapeDtypeStruct((t, D_MODEL), F32), jax.ShapeDtypeStruct((t, GW), BF16),
                   jax.ShapeDtypeStruct((nseq, 2, seq, 128), F32), jax.ShapeDtypeStruct((t, D_MODEL), BF16),
                   jax.ShapeDtypeStruct((t, D_MODEL), BF16), jax.ShapeDtypeStruct((t, D_MODEL), BF16)),
        in_specs=[_lane_blocks(nseq, seq, 2)] * 6 + [row(2 * GW), gate(0), gate(1), row(D_MODEL), full(wat), full(wbt),
                                                     full(wout)],
        out_specs=(row(D_MODEL), row(GW), _lane_blocks(nseq, seq, 2), row(D_MODEL), row(D_MODEL), row(D_MODEL)),
        compiler_params=_params(("parallel",), VMEM_BIG),
    )(o0, o1, o2, l0, l1, l2, yb, zg, zg, h1, wat, wbt, wout)


def _merge_bwd(dh2, pa, pb, zg, y, yb, wat, wbt, wout, nseq, rider=None):
    t = dh2.shape[0]

    def body(dh2_ref, pa_ref, pb_ref, ga_ref, gb_ref, y_ref, yb_ref, wa_ref, wb_ref, wo_ref,
             dpa_ref, dpb_ref, dga_ref, dgb_ref, dya_ref, dyb_ref, dh2b_ref, ca_ref, cb_ref):
        d16 = dh2_ref[...].astype(BF16)
        dh2b_ref[...] = d16
        dm = _dot_nt(d16, wo_ref[...])
        sa = _sigmoid(ga_ref[...].astype(F32))
        sb = _sigmoid(gb_ref[...].astype(F32))
        dpa = (dm * sa).astype(BF16)
        dpb = (dm * sb).astype(BF16)
        dpa_ref[...] = dpa
        dpb_ref[...] = dpb
        dga_ref[...] = (dm * pa_ref[...].astype(F32) * sa * (1.0 - sa)).astype(BF16)
        dgb_ref[...] = (dm * pb_ref[...].astype(F32) * sb * (1.0 - sb)).astype(BF16)
        dya = _dot(dpa, wa_ref[...])
        dyb = _dot(dpb, wb_ref[...])
        dya_ref[0, 0] = dya[:, :128]
        dya_ref[0, 1] = dya[:, 128:]
        dyb_ref[...] = dyb.astype(BF16)
        hp = lax.Precision.HIGHEST
        ca = jnp.dot(dya * y_ref[...].astype(F32), _head_sum_matrix(GW), precision=hp, preferred_element_type=F32)
        ca_ref[0, 0] = ca[:, :128]
        ca_ref[0, 1] = ca[:, 128:]
        cb_ref[...] = jnp.dot(dyb * yb_ref[...].astype(F32), _head_sum_matrix(2 * GW), precision=hp,
                              preferred_element_type=F32)

    row = lambda w: pl.BlockSpec((TM, w), lambda i: (i, 0))
    full = lambda a: pl.BlockSpec(a.shape, lambda i: (0, 0))
    gate = lambda cb: pl.BlockSpec((TM, D_MODEL), lambda i: (i, cb))
    bf = lambda w: jax.ShapeDtypeStruct((t, w), BF16)
    lanes = jax.ShapeDtypeStruct((nseq, 2, t // nseq, 128), F32)
    lane_spec = _lane_blocks(nseq, t // nseq, 2)
    return _pallas(
        body, (dh2, pa, pb, zg, zg, y, yb, wat, wbt, wout), name="merge_bwd", grid=(t // TM,),
        out_shape=(bf(D_MODEL), bf(D_MODEL), bf(D_MODEL), bf(D_MODEL), lanes, bf(2 * GW), bf(D_MODEL),
                   lanes, jax.ShapeDtypeStruct((t, 2 * GW), F32)),
        in_specs=[row(D_MODEL), row(D_MODEL), row(D_MODEL), gate(0), gate(1), row(GW), row(2 * GW),
                  full(wat), full(wbt), full(wout)],
        out_specs=(row(D_MODEL), row(D_MODEL), row(D_MODEL), row(D_MODEL), lane_spec, row(2 * GW), row(D_MODEL),
                   lane_spec, row(2 * GW)),
        sem=("parallel",), vmem=VMEM_BIG, rider=rider)


def _loss_head(h3, gain, tgt):
    t = h3.shape[0]

    def body(h_ref, gain_ref, t_ref, dh_ref, loss_ref, gg_ref):
        @pl.when(pl.program_id(0) == 0)
        def _():
            loss_ref[...] = jnp.zeros_like(loss_ref)
            gg_ref[...] = jnp.zeros_like(gg_ref)

        hh = h_ref[...]
        r = lax.rsqrt(jnp.mean(hh * hh, axis=-1, keepdims=True) + EPS)
        hn = hh * r
        err = hn * gain_ref[...] - t_ref[...]
        part = jnp.sum(jnp.sum(err * err, axis=1, keepdims=True), axis=0, keepdims=True)
        loss_ref[...] += (0.5 / D_MODEL) * part
        dy = err * (1.0 / D_MODEL)
        gg_ref[...] += jnp.sum(dy * hn, axis=0, keepdims=True)
        dng = dy * gain_ref[...]
        dh_ref[...] = r * (dng - hn * jnp.mean(dng * hn, axis=-1, keepdims=True))

    row = pl.BlockSpec((TM, D_MODEL), lambda i: (i, 0))
    vec = pl.BlockSpec((1, D_MODEL), lambda i: (0, 0))
    return pl.pallas_call(
        body, name="loss_head", grid=(t // TM,),
        out_shape=(jax.ShapeDtypeStruct((t, D_MODEL), F32), jax.ShapeDtypeStruct((8, 128), F32),
                   jax.ShapeDtypeStruct((1, D_MODEL), F32)),
        in_specs=[row, vec, row], out_specs=(row, pl.BlockSpec((8, 128), lambda i: (0, 0)), vec),
        compiler_params=_params(("arbitrary",)),
    )(h3, gain, tgt)


def _lane_head(rows):
    return lax.broadcasted_iota(jnp.int32, (rows, GW), 1) // HEAD_DIM


def _kv_expand_matrix(r):
    ci = lax.broadcasted_iota(jnp.int32, (2 * HEAD_DIM, GW), 0)
    ji = lax.broadcasted_iota(jnp.int32, (2 * HEAD_DIM, GW), 1)
    return (ci == (ji % HEAD_DIM) + HEAD_DIM * r).astype(BF16)


def _block_rows(row0, stride, ib):
    start = row0 + (stride * BLOCK) * ib
    if stride > 1:
        return pl.ds(start, BLOCK, stride=stride)
    return pl.ds(pl.multiple_of(start, BLOCK), BLOCK)


def _stack_heads(x, lane_head):
    return jnp.concatenate([jnp.where(lane_head == h, x, jnp.zeros_like(x)) for h in range(4)], axis=0)


def _unstack_heads(x4, lane_head):
    out = jnp.zeros((BLOCK, GW), F32)
    for h in range(4):
        out = jnp.where(lane_head == h, x4[h * BLOCK:(h + 1) * BLOCK], out)
    return out


def _load_rows(ref, rows, split):
    if split:
        return jnp.concatenate([ref[0, 0, rows, :], ref[0, 1, rows, :]], axis=1)
    return ref[0, rows, :]


def _store_rows(ref, rows, val, split):
    if split:
        ref[0, 0, rows, :] = val[:, :128]
        ref[0, 1, rows, :] = val[:, 128:]
    else:
        ref[0, rows, :] = val


def _attn_fwd(q_arr, k_arr, v_arr, bias, sink, *, grid, seq, stride, kvw, split, q_spec, k_spec, v_spec, bias_map,
              sink_map, o_spec, has_sink, o_shape, o_dtype, name, rider=None):
    nb = seq // stride // BLOCK
    scale = HEAD_DIM ** -0.5
    expanded = kvw != GW
    rps = grid_rps = RESIDUES_PER_STEP if stride >= 4 * RESIDUES_PER_STEP else 1
    grid = (grid[0], grid[1] // grid_rps)

    def body(q_ref, k_ref, v_ref, bias_ref, sink_ref, o_ref, lse_ref, *kv_x):
        rr = pl.program_id(1)
        lane_head = _lane_head(BLOCK)
        if has_sink:
            sk4 = jnp.concatenate([jnp.broadcast_to(sink_ref[0, h:h + 1, 0:1], (BLOCK, 1)) for h in range(4)], axis=0)
        if expanded:
            expand = _kv_expand_matrix(rr)
            kv_x[0][...] = _dot(k_ref[0], expand).astype(BF16)
            kv_x[1][...] = _dot(v_ref[0], expand).astype(BF16)
        for j in range(rps):
            residue(rr * rps + j if stride > 1 else 0, q_ref, k_ref, v_ref, bias_ref, o_ref, lse_ref, kv_x,
                    lane_head, sk4 if has_sink else None)

    def residue(row0, q_ref, k_ref, v_ref, bias_ref, o_ref, lse_ref, kv_x, lane_head, sk4):
        def load(ref, ib):
            return _load_rows(ref, _block_rows(row0, stride, ib), split).astype(BF16)

        def load_kv(which, ib):
            if expanded:
                return kv_x[which][_block_rows(0, 1, ib), :]
            return load((k_ref, v_ref)[which], ib)

        def block(ib, first):
            q4 = _stack_heads(load(q_ref, ib), lane_head)
            if first:
                kc, vc = load_kv(0, ib), load_kv(1, ib)
                b4 = bias_ref[:, :, BLOCK:].reshape(4 * BLOCK, BLOCK)
            else:
                kc = jnp.concatenate([load_kv(0, ib - 1), load_kv(0, ib)], axis=0)
                vc = jnp.concatenate([load_kv(1, ib - 1), load_kv(1, ib)], axis=0)
                b4 = bias_ref[...].reshape(4 * BLOCK, 2 * BLOCK)
            s = _dot_nt(q4, kc) * scale + b4
            m = jnp.max(s, axis=-1, keepdims=True)
            if has_sink:
                m = jnp.maximum(m, sk4)
            p = jnp.exp(s - m)
            l = jnp.sum(p, axis=-1, keepdims=True)
            if has_sink:
                l = l + jnp.exp(sk4 - m)
            o4 = _dot(p.astype(BF16), vc) / l
            rows = _block_rows(row0, stride, ib)
            _store_rows(o_ref, rows, _unstack_heads(o4, lane_head).astype(o_dtype), split)
            _store_rows(lse_ref, rows, _unstack_heads(m + jnp.log(l), lane_head), split)

        block(0, True)
        if nb > 1:
            def step(i, carry):
                block(i, False)
                return carry
            lax.fori_loop(1, nb, step, 0)

    return _pallas(
        body, (q_arr, k_arr, v_arr, bias, sink), name=name, grid=grid,
        out_shape=(jax.ShapeDtypeStruct(o_shape, o_dtype), jax.ShapeDtypeStruct(o_shape, F32)),
        in_specs=[q_spec, k_spec, v_spec,
                  pl.BlockSpec((4, BLOCK, 2 * BLOCK), bias_map), pl.BlockSpec((1, 4, 128), sink_map)],
        out_specs=(o_spec, o_spec),
        scratch_shapes=[pltpu.VMEM((seq, GW), BF16)] * 2 if expanded else [],
        sem=("arbitrary", "arbitrary"), vmem=VMEM_BIG, rider=rider)


def _attn_bwd(q_arr, k_arr, v_arr, bias, sink, dy, cc, lse, *, grid, seq, stride, kvw, split, q_spec, k_spec, v_spec,
              bias_map, sink_map, o_spec, kv_out_spec, has_sink, n_bias, dq_shape, dkv_shape, g_dtype, name):
    ln = seq // stride
    nb = ln // BLOCK
    scale = HEAD_DIM ** -0.5
    expanded = kvw != GW
    rps = RESIDUES_PER_STEP if stride >= 4 * RESIDUES_PER_STEP else 1
    grid = (grid[0], grid[1] // rps)

    def body(q_ref, k_ref, v_ref, bias_ref, sink_ref, dy_ref, c_ref, lse_ref,
             dq_ref, dk_ref, dv_ref, db_ref, dsk_ref, dk_acc, dv_acc, dk_half, dv_half, *kv_x):
        rr = pl.program_id(1)

        @pl.when((pl.program_id(0) == 0) & (rr == 0))
        def _():
            db_ref[...] = jnp.zeros_like(db_ref)
            dsk_ref[...] = jnp.zeros_like(dsk_ref)

        if expanded:
            expand = _kv_expand_matrix(rr)
            kv_x[0][...] = _dot(k_ref[0], expand).astype(BF16)
            kv_x[1][...] = _dot(v_ref[0], expand).astype(BF16)
        refs = (q_ref, k_ref, v_ref, bias_ref, sink_ref, dy_ref, c_ref, lse_ref, dq_ref, dk_ref, dv_ref, db_ref,
                dsk_ref, dk_acc, dv_acc, dk_half, dv_half, kv_x)
        for j in range(rps):
            residue(rr, rr * rps + j if stride > 1 else 0, *refs)

    def residue(rr, row0, q_ref, k_ref, v_ref, bias_ref, sink_ref, dy_ref, c_ref, lse_ref,
                dq_ref, dk_ref, dv_ref, db_ref, dsk_ref, dk_acc, dv_acc, dk_half, dv_half, kv_x):
        dk_acc[...] = jnp.zeros_like(dk_acc)
        dv_acc[...] = jnp.zeros_like(dv_acc)
        lane_head = _lane_head(BLOCK)
        hb = 4 * rr if n_bias == 8 else 0

        def load(ref, ib):
            return _load_rows(ref, _block_rows(row0, stride, ib), split)

        def load_kv(which, ib):
            if expanded:
                return kv_x[which][_block_rows(0, 1, ib), :]
            return load((k_ref, v_ref)[which], ib).astype(BF16)

        def head_col(x):
            return jnp.concatenate([x[:, h * HEAD_DIM:h * HEAD_DIM + 1] for h in range(4)], axis=0)

        def block(ib, first):
            q4 = _stack_heads(load(q_ref, ib).astype(BF16), lane_head)
            dy4 = _stack_heads(load(dy_ref, ib).astype(BF16), lane_head)
            c4 = head_col(load(c_ref, ib))
            l4 = head_col(load(lse_ref, ib))
            if first:
                kc, vc = load_kv(0, ib), load_kv(1, ib)
                b4 = bias_ref[:, :, BLOCK:].reshape(4 * BLOCK, BLOCK)
                krows = pl.ds(0, BLOCK)
            else:
                kc = jnp.concatenate([load_kv(0, ib - 1), load_kv(0, ib)], axis=0)
                vc = jnp.concatenate([load_kv(1, ib - 1), load_kv(1, ib)], axis=0)
                b4 = bias_ref[...].reshape(4 * BLOCK, 2 * BLOCK)
                krows = pl.ds(pl.multiple_of((ib - 1) * BLOCK, BLOCK), 2 * BLOCK)
            nk = BLOCK if first else 2 * BLOCK
            p = jnp.exp(_dot_nt(q4, kc) * scale + b4 - l4)
            ds = p * (_dot_nt(dy4, vc) - c4)
            ds3 = ds.reshape(4, BLOCK, nk)
            if n_bias == 8:
                if first:
                    db_ref[pl.ds(hb, 4), :, BLOCK:] += ds3
                else:
                    db_ref[pl.ds(hb, 4)] += ds3
            elif first:
                db_ref[:, :, BLOCK:] += ds3
            else:
                db_ref[...] += ds3
            ds16 = ds.astype(BF16)
            dq = _unstack_heads(_dot(ds16, kc), lane_head) * scale
            _store_rows(dq_ref, _block_rows(row0, stride, ib), dq.astype(g_dtype), split)
            dk_acc[krows, :] += _dot_tn(ds16, q4) * scale
            dv_acc[krows, :] += _dot_tn(p.astype(BF16), dy4)
            if has_sink:
                for h in range(4):
                    hs = slice(h * BLOCK, (h + 1) * BLOCK)
                    sk = sink_ref[0, h:h + 1, 0:1]
                    val = -jnp.sum(jnp.exp(sk - l4[hs]) * c4[hs], axis=0, keepdims=True)
                    dsk_ref[hb + h] += jnp.broadcast_to(val, (8, 128))

        block(0, True)
        if nb > 1:
            def step(i, carry):
                block(i, False)
                return carry
            lax.fori_loop(1, nb, step, 0)

        if kvw == GW:
            all_rows = pl.ds(row0, ln, stride=stride) if stride > 1 else pl.ds(0, ln)
            _store_rows(dk_ref, all_rows, dk_acc[...].astype(g_dtype), split)
            _store_rows(dv_ref, all_rows, dv_acc[...].astype(g_dtype), split)
        else:
            def fold(acc):
                t2 = acc[:, :2 * HEAD_DIM] + acc[:, 2 * HEAD_DIM:]
                t2 = t2 + pltpu.roll(t2, HEAD_DIM, 1)
                lane = lax.broadcasted_iota(jnp.int32, t2.shape, 1) // HEAD_DIM
                return jnp.where(lane == rr, t2, 0.0)

            @pl.when(rr == 0)
            def _():
                dk_half[...] = fold(dk_acc[...])
                dv_half[...] = fold(dv_acc[...])

            @pl.when(rr == 1)
            def _():
                dk_ref[0] = (dk_half[...] + fold(dk_acc[...])).astype(g_dtype)
                dv_ref[0] = (dv_half[...] + fold(dv_acc[...])).astype(g_dtype)

    return pl.pallas_call(
        body, name=name, grid=grid,
        out_shape=(jax.ShapeDtypeStruct(dq_shape, g_dtype), jax.ShapeDtypeStruct(dkv_shape, g_dtype),
                   jax.ShapeDtypeStruct(dkv_shape, g_dtype), jax.ShapeDtypeStruct((n_bias, BLOCK, 2 * BLOCK), F32),
                   jax.ShapeDtypeStruct((8, 8, 128), F32)),
        in_specs=[q_spec, k_spec, v_spec,
                  pl.BlockSpec((4, BLOCK, 2 * BLOCK), bias_map), pl.BlockSpec((1, 4, 128), sink_map),
                  o_spec, o_spec, o_spec],
        out_specs=(o_spec, kv_out_spec, kv_out_spec,
                   pl.BlockSpec((n_bias, BLOCK, 2 * BLOCK), lambda n, r: (0, 0, 0)),
                   pl.BlockSpec((8, 8, 128), lambda n, r: (0, 0, 0))),
        scratch_shapes=[pltpu.VMEM((ln, GW), F32), pltpu.VMEM((ln, GW), F32),
                        pltpu.VMEM((ln, 2 * HEAD_DIM), F32), pltpu.VMEM((ln, 2 * HEAD_DIM), F32)]
        + ([pltpu.VMEM((seq, GW), BF16)] * 2 if expanded else []),
        compiler_params=_params(("arbitrary", "arbitrary"), VMEM_BIG),
    )(q_arr, k_arr, v_arr, bias, sink, dy, cc, lse)


def _bias_grad(ds_all, buckets):
    def body(ds_ref, bk_ref, o_ref):
        rows = lax.broadcasted_iota(jnp.int32, (N_BUCKETS, 128), 0)
        cols = lax.broadcasted_iota(jnp.int32, (N_BUCKETS, 128), 1)

        def per_bucket(b, acc):
            for h in range(20):
                gi = h // 4 if h < 12 else 3
                v = jnp.where(bk_ref[gi] == b, ds_ref[h], 0.0)
                v = jnp.sum(jnp.sum(v, axis=1, keepdims=True), axis=0, keepdims=True)
                acc = jnp.where((rows == b) & (cols == h), v, acc)
            return acc

        o_ref[...] = lax.fori_loop(0, N_BUCKETS, per_bucket, jnp.zeros((N_BUCKETS, 128), F32))

    vm = pl.BlockSpec(memory_space=pltpu.VMEM)
    return pl.pallas_call(body, name="bias_grad", out_shape=jax.ShapeDtypeStruct((N_BUCKETS, 128), F32),
                          in_specs=[vm, vm], out_specs=vm)(ds_all, buckets)


def _adamw(w, g, m, v, name):
    r, c = w.shape
    tr = r
    for cand in (256, 176, 128, 64, 32, 16, 8):
        if r % cand == 0:
            tr = cand
            break
    bc1 = 1.0 - ADAM_B1 ** ADAM_STEP
    bc2 = 1.0 - ADAM_B2 ** ADAM_STEP

    def body(w_ref, g_ref, m_ref, v_ref, d_ref, nm_ref, nv_ref):
        gv = g_ref[...]
        nm = ADAM_B1 * m_ref[...] + (1.0 - ADAM_B1) * gv
        nv = ADAM_B2 * v_ref[...] + (1.0 - ADAM_B2) * (gv * gv)
        nm_ref[...] = nm
        nv_ref[...] = nv
        d_ref[...] = -ADAM_LR * ((nm / bc1) / (jnp.sqrt(nv / bc2) + ADAM_EPS) + ADAM_WD * w_ref[...])

    spec = pl.BlockSpec((tr, c), lambda i: (i, 0))
    shp = jax.ShapeDtypeStruct((r, c), F32)
    return pl.pallas_call(body, name=name, grid=(r // tr,), out_shape=(shp, shp, shp),
                          in_specs=[spec] * 4, out_specs=(spec, spec, spec),
                          compiler_params=_params(("parallel",)))(w, g, m, v)


def _t5_bucket(dist):
    max_exact = N_BUCKETS // 2
    n = jnp.maximum(dist, 0)
    nf = jnp.maximum(n, 1).astype(F32)
    large = max_exact + (jnp.log(nf / max_exact) / math.log(MAX_DISTANCE / max_exact)
                         * (N_BUCKETS - max_exact)).astype(jnp.int32)
    large = jnp.minimum(large, N_BUCKETS - 1)
    return jnp.where(n < max_exact, n, large)


def _bias_tables(rel_bias):
    qi = jnp.arange(BLOCK)[:, None]
    ki = jnp.arange(2 * BLOCK)[None, :]
    dist = qi + BLOCK - ki
    specs = [(d, w // d, 4 * gi, 4 * gi + 4) for gi, (w, d) in enumerate(DIL_GROUPS)] + [(1, B_WINDOW - 1, 12, 20)]
    biases, buckets = [], []
    for stride, steps, h0, h1 in specs:
        valid = (dist >= 0) & (dist <= steps)
        bk = jnp.where(valid, _t5_bucket(dist * stride), -1).astype(jnp.int32)
        onehot = (bk[None, :, :] == jnp.arange(N_BUCKETS, dtype=jnp.int32)[:, None, None]).astype(F32)
        b = jnp.einsum("bqk,bh->hqk", onehot, rel_bias[:, h0:h1], precision=lax.Precision.HIGHEST)
        biases.append(jnp.where(valid[None], b, NEG))
        buckets.append(bk)
    return jnp.concatenate(biases, axis=0), jnp.stack(buckets, axis=0)


def _local_step(x, tgt, W, S, shards=None):
    nseq, seq, _ = x.shape
    t = nseq * seq
    xf = x.reshape(t, D_MODEL)
    bias_all, buckets = _bias_tables(S["rel_bias"])
    sink_b = jnp.broadcast_to(S["sinks"].reshape(2, 4, 1), (2, 4, 128)).astype(F32)
    sink_0 = jnp.zeros((1, 4, 128), F32)
    dist = shards is not None
    W = dict(W)
    G, GS, reduced = {}, {}, {}

    def put(keys, gathered):
        for k, g in zip(keys, gathered):
            W[k] = g.reshape(_FULL_SHAPE.get(k, (N_CHIPS * shards[k].shape[0], D_MODEL)))

    def gather_rider(keys):
        return _GatherRider([shards[k] for k in keys]) if dist else None

    def pair(keys):
        return _pair_reduce([G[k].reshape(N_CHIPS, 2, shards[k].shape[0] // 2, D_MODEL) for k in keys],
                            "grad_pair_reduce_" + keys[0])

    def finish(keys, own, rec):
        full = _final_reduce(own, rec, "grad_final_reduce_" + keys[0])
        off = 0
        for k in keys:
            r = shards[k].shape[0]
            reduced[k] = full[:, off:off + r // 2].reshape(r, D_MODEL)
            off += r // 2

    if dist:
        first = ("wgt1", "wut1", "wd1")
        put(first, _gather_rows([shards[k] for k in first]))
    keys = ("wint", "wgt2")
    (h1, n1, g1, u1), ro = _ffn_fwd(xf, S["ffn1_norm"], W["wgt1"], W["wut1"], W["wd1"], rider=gather_rider(keys))
    put(keys, ro)
    keys = ("wout", "wat", "wbt", "wut2")
    (un, za, zb, zg), ro = _inproj_fwd(h1, S["mix_norm"], W["wint"], S["b_in"], nseq, rider=gather_rider(keys))
    put(keys, ro)

    seq3 = lambda a: a.reshape(nseq, seq, a.shape[-1])
    zb3 = seq3(zb)
    pair_blk = lambda cb: pl.BlockSpec((1, 2, seq, 128), lambda n, r, cb=cb: (n, cb, 0, 0))
    a_cfg = []
    outs, lses = [], []
    for gi, (_, d) in enumerate(DIL_GROUPS):
        cfg = dict(grid=(nseq, d), seq=seq, stride=d, kvw=GW, split=True,
                   q_spec=pair_blk(gi), k_spec=pair_blk(3 + gi), v_spec=pair_blk(6 + gi), o_spec=pair_blk(0),
                   bias_map=lambda n, r: (0, 0, 0), sink_map=lambda n, r: (0, 0, 0), has_sink=False)
        a_cfg.append(cfg)
        (o, lse), _ = _attn_fwd(za, za, za, bias_all[4 * gi:4 * gi + 4], sink_0, o_shape=(nseq, 2, seq, 128),
                                o_dtype=F32, name=f"attn_a{gi}_fwd", **cfg)
        outs.append(o)
        lses.append(lse)
    wide_blk = lambda w, cmap: pl.BlockSpec((1, seq, w), cmap)
    b_cfg = dict(grid=(nseq, 2), seq=seq, stride=1, kvw=2 * HEAD_DIM, split=False,
                 q_spec=wide_blk(GW, lambda n, r: (n, 0, r)), k_spec=wide_blk(2 * HEAD_DIM, lambda n, r: (n, 0, 4)),
                 v_spec=wide_blk(2 * HEAD_DIM, lambda n, r: (n, 0, 5)), o_spec=wide_blk(GW, lambda n, r: (n, 0, r)),
                 bias_map=lambda n, r: (r, 0, 0), sink_map=lambda n, r: (r, 0, 0), has_sink=True)
    keys = ("wd2",)
    (yb, lse_b), ro = _attn_fwd(zb3, zb3, zb3, bias_all[12:20], sink_b, o_shape=(nseq, seq, 2 * GW), o_dtype=BF16,
                                name="attn_b_fwd", rider=gather_rider(keys), **b_cfg)
    put(keys, ro)
    yb = yb.reshape(t, 2 * GW)

    h2, y, lse_tot, pa, pb, merged = _merge_fwd(outs[0], outs[1], outs[2], lses[0], lses[1], lses[2], yb, zg, h1,
                                                W["wat"], W["wbt"], W["wout"])
    (h3, n2, g2, u2), _ = _ffn_fwd(h2, S["ffn2_norm"], W["wgt2"], W["wut2"], W["wd2"])
    dh3, loss_part, g_final = _loss_head(h3, S["final_norm"].reshape(1, D_MODEL), tgt.reshape(t, D_MODEL))

    GS["final_norm"] = g_final
    dh2, dg2, du2, a2, df2, GS["ffn2_norm"] = _ffn_bwd(dh3, h2, S["ffn2_norm"], g2, u2, W["wgt2"], W["wut2"], W["wd2"])
    G["wgt2"] = _wgrad(dg2, n2, D_FF, name="wgrad_gate2")
    G["wut2"] = _wgrad(du2, n2, D_FF, name="wgrad_up2")
    G["wd2"] = _wgrad(a2, df2, D_FF, name="wgrad_down2")

    keys = ("wgt2", "wut2", "wd2")
    rider = _ExchangeRider([pair(keys)]) if dist else None
    (dpa, dpb, dga, dgb, dya, dyb, dh2b, ca, cb), ro = _merge_bwd(dh2, pa, pb, zg, y, yb, W["wat"], W["wbt"], W["wout"],
                                                                  nseq, rider=rider)
    if dist:
        finish(keys, *ro)
    G["wout"] = _wgrad(merged, dh2b, D_MODEL, name="wgrad_out")
    G["wat"] = _wgrad(dpa, y, D_MODEL, name="wgrad_branch_a")
    G["wbt"] = _wgrad(dpb, yb, D_MODEL, name="wgrad_branch_b")

    dqs, dks, dvs, dbs = [], [], [], []
    shp = (nseq, 2, seq, 128)
    halves = lambda a: [a[:, hf].reshape(t, 128).astype(BF16) for hf in range(2)]
    for gi in range(len(DIL_GROUPS)):
        dq, dk, dv, db, _ = _attn_bwd(za, za, za, bias_all[4 * gi:4 * gi + 4], sink_0, dya, ca, lse_tot,
                                      n_bias=4, dq_shape=shp, dkv_shape=shp, g_dtype=F32,
                                      kv_out_spec=a_cfg[gi]["o_spec"], name=f"attn_a{gi}_bwd", **a_cfg[gi])
        dqs += halves(dq)
        dks += halves(dk)
        dvs += halves(dv)
        dbs.append(db)
    dqb, dkb, dvb, dbb, dsink = _attn_bwd(zb3, zb3, zb3, bias_all[12:20], sink_b, seq3(dyb), seq3(cb), lse_b,
                                          n_bias=8, dq_shape=(nseq, seq, 2 * GW),
                                          dkv_shape=(nseq, seq, 2 * HEAD_DIM), g_dtype=BF16,
                                          kv_out_spec=wide_blk(2 * HEAD_DIM, lambda n, r: (n, 0, 0)),
                                          name="attn_b_bwd", **b_cfg)
    dz = jnp.concatenate(dqs + dks + dvs + [dqb.reshape(t, 2 * GW), dkb.reshape(t, 2 * HEAD_DIM),
                                            dvb.reshape(t, 2 * HEAD_DIM), dga, dgb], axis=-1)
    gb_tab = _bias_grad(jnp.concatenate(dbs + [dbb], axis=0), buckets)
    GS["rel_bias"] = gb_tab[:, :20]
    GS["sinks"] = dsink[:, 0, 0].reshape(1, 8)

    G["wint"], GS["b_in"] = _wgrad(dz, un, D_IN // 2, with_colsum=True, name="wgrad_in")
    keys = ("wint", "wout", "wat", "wbt")
    rider = _ExchangeRider([pair(keys)]) if dist else None
    (dh1, GS["mix_norm"]), ro = _inproj_bwd(dz, dh2, h1, S["mix_norm"], W["wint"], rider=rider)
    if dist:
        finish(keys, *ro)

    dx, dg1, du1, a1, df1, GS["ffn1_norm"] = _ffn_bwd(dh1, xf, S["ffn1_norm"], g1, u1, W["wgt1"], W["wut1"], W["wd1"])
    G["wgt1"] = _wgrad(dg1, n1, D_FF, name="wgrad_gate1")
    if dist:
        G["wut1"], ro = _wgrad(du1, n1, D_FF, name="wgrad_up1", rider=_ExchangeRider([pair(("wgt1",))]))
        finish(("wgt1",), *ro)
        G["wd1"], ro = _wgrad(a1, df1, D_FF, name="wgrad_down1", rider=_ExchangeRider([pair(("wut1",))]))
        finish(("wut1",), *ro)
        finish(("wd1",), *_chip_exchange([pair(("wd1",))]))
    else:
        G["wut1"] = _wgrad(du1, n1, D_FF, name="wgrad_up1")
        G["wd1"] = _wgrad(a1, df1, D_FF, name="wgrad_down1")
    return loss_part, dx.reshape(x.shape), (reduced if dist else G), GS


_SMALL = ("ffn1_norm", "mix_norm", "ffn2_norm", "final_norm", "b_in", "sinks", "rel_bias")
_ORDER = ("ffn1_norm", "ffn1_w_gate", "ffn1_w_up", "ffn1_w_down", "mix_norm", "w_in", "b_in", "w_branch_a",
          "w_branch_b", "w_out", "sinks", "rel_bias", "ffn2_norm", "ffn2_w_gate", "ffn2_w_up", "ffn2_w_down",
          "final_norm")
_BIG = (("wgt1", "ffn1_w_gate", True, 704), ("wut1", "ffn1_w_up", True, 704), ("wd1", "ffn1_w_down", False, 704),
        ("wint", "w_in", True, 1280), ("wout", "w_out", False, 256), ("wat", "w_branch_a", True, 64),
        ("wbt", "w_branch_b", True, 128), ("wgt2", "ffn2_w_gate", True, 704), ("wut2", "ffn2_w_up", True, 704),
        ("wd2", "ffn2_w_down", False, 704))
_FULL_SHAPE = {"wat": (D_MODEL, GW), "wbt": (D_MODEL, 2 * GW)}


def _pack_small(p, extra=None):
    last = [p["sinks"].reshape(8), p["rel_bias"].reshape(640)]
    used = 648
    if extra is not None:
        last.append(extra.reshape(1))
        used += 1
    last.append(jnp.zeros((D_MODEL - used,), F32))
    rows = [p["ffn1_norm"].reshape(1, D_MODEL), p["mix_norm"].reshape(1, D_MODEL), p["ffn2_norm"].reshape(1, D_MODEL),
            p["final_norm"].reshape(1, D_MODEL), p["b_in"].reshape(5, D_MODEL), jnp.concatenate(last).reshape(1, D_MODEL),
            jnp.zeros((6, D_MODEL), F32)]
    return jnp.concatenate(rows, axis=0)


def _unpack_small(a):
    return {"ffn1_norm": a[0:1], "mix_norm": a[1:2], "ffn2_norm": a[2:3], "final_norm": a[3],
            "b_in": a[4:9].reshape(1, D_IN), "sinks": a[9, 0:8].reshape(1, 8), "rel_bias": a[9, 8:648].reshape(32, 20)}


def kernel(x, ffn1_norm, ffn1_w_gate, ffn1_w_up, ffn1_w_down, mix_norm, w_in, b_in, w_branch_a, w_branch_b, w_out, sinks, rel_bias, ffn2_norm, ffn2_w_gate, ffn2_w_up, ffn2_w_down, final_norm, loss_target, m_ffn1_norm, m_ffn1_w_gate, m_ffn1_w_up, m_ffn1_w_down, m_mix_norm, m_w_in, m_b_in, m_w_branch_a, m_w_branch_b, m_w_out, m_sinks, m_rel_bias, m_ffn2_norm, m_ffn2_w_gate, m_ffn2_w_up, m_ffn2_w_down, m_final_norm, v_ffn1_norm, v_ffn1_w_gate, v_ffn1_w_up, v_ffn1_w_down, v_mix_norm, v_w_in, v_b_in, v_w_branch_a, v_w_branch_b, v_w_out, v_sinks, v_rel_bias, v_ffn2_norm, v_ffn2_w_gate, v_ffn2_w_up, v_ffn2_w_down, v_final_norm):
    args = dict(locals())
    w = {n: args[n] for n in _ORDER}
    m = {n: args["m_" + n] for n in _ORDER}
    v = {n: args["v_" + n] for n in _ORDER}

    shards = {}
    for key, name, transposed, rows in _BIG:
        a = w[name][0]
        a = (a.T if transposed else a).astype(BF16)
        shards[key] = a.reshape(rows, D_MODEL)
    S = {n: w[n] for n in _SMALL}

    loss_part, grad_x, reduced, GS = _local_step(x, loss_target, {}, S, shards)

    grads = {}
    for key, name, transposed, rows in _BIG:
        nat = w[name][0].shape
        grads[name] = reduced[key].reshape(nat[1], nat[0]).T if transposed else reduced[key].reshape(nat)

    small = _allreduce_small(_pack_small(GS, extra=loss_part[0, 0]))
    loss = small[9, 648]

    out_g, out_d, out_m, out_v = {}, {}, {}, {}
    for _, n, _, _ in _BIG:
        d_, nm_, nv_ = _adamw(w[n][0], grads[n], m[n][0], v[n][0], "adamw_" + n)
        out_g[n], out_d[n], out_m[n], out_v[n] = grads[n][None], d_[None], nm_[None], nv_[None]
    d_s, m_s, v_s = _adamw(_pack_small(w), small, _pack_small(m), _pack_small(v), "adamw_small")
    for dst, src in ((out_g, small), (out_d, d_s), (out_m, m_s), (out_v, v_s)):
        dst.update(_unpack_small(src))

    return (loss, grad_x, *[out_g[n] for n in _ORDER], *[out_d[n] for n in _ORDER],
            *[out_m[n] for n in _ORDER], *[out_v[n] for n in _ORDER])
```

```python
import math

import jax
import jax.numpy as jnp
from jax import lax
from jax.experimental import pallas as pl
from jax.experimental.pallas import tpu as pltpu

F32, BF16 = jnp.float32, jnp.bfloat16
MESH = pl.DeviceIdType.MESH

D_MODEL = 1024
D_FF = 2816
D_IN = 5120
HEAD_DIM = 64
BLOCK = 128
DIL_GROUPS = ((128, 1), (512, 4), (2048, 16))
B_WINDOW = 128
N_BUCKETS = 32
MAX_DISTANCE = 2048
EPS = 1e-6
N_CHIPS = 4
GW = 256
ZA_W = 2304
ZB_W = 768
NEG = -1e30

ADAM_LR, ADAM_B1, ADAM_B2, ADAM_EPS, ADAM_WD, ADAM_STEP = 0.001, 0.9, 0.999, 1e-08, 0.01, 10

VMEM_BIG = 56 * 1024 * 1024
TM = 512
TM_BWD = 256
FF_CHUNKS = 2
DMA_SPLIT = 8
RESIDUES_PER_STEP = 4


def _dot(a, b):
    return jnp.dot(a, b, preferred_element_type=F32)


def _dot_nt(a, b):
    return lax.dot_general(a, b, (((1,), (1,)), ((), ())), preferred_element_type=F32)


def _dot_tn(a, b):
    return lax.dot_general(a, b, (((0,), (0,)), ((), ())), preferred_element_type=F32)


def _sigmoid(x):
    return 1.0 / (1.0 + jnp.exp(-x))


def _params(sem, vmem=None):
    return pltpu.CompilerParams(dimension_semantics=sem, vmem_limit_bytes=vmem)


ANY = pl.BlockSpec(memory_space=pl.ANY)


def _me():
    return lax.axis_index("x"), lax.axis_index("y"), lax.axis_index("c")


_CHIP_RELS = ((1, 0), (0, 1), (1, 1))


def _flip(v, f):
    return 1 - v if f else v


def _remote(src, dst, ssem, rsem, peer):
    return pltpu.make_async_remote_copy(src_ref=src, dst_ref=dst, send_sem=ssem, recv_sem=rsem,
                                        device_id=peer, device_id_type=MESH)


def _row_pieces(rows, n):
    step = max(16, -(-rows // n) // 16 * 16)
    out, s = [], 0
    while s < rows:
        out.append((s, min(step, rows - s)))
        s += step
    return out


def _gather_rows(shards):
    nt = len(shards)
    rows = [s.shape[0] for s in shards]

    def body(*refs):
        srcs, outs = refs[:nt], refs[nt:2 * nt]
        ici_s, ici_r, d2d_s, d2d_r, loc = refs[2 * nt:]
        x, y, c = _me()
        j = 2 * x + y
        sib = (x, y, 1 - c)
        local = [pltpu.make_async_copy(srcs[t], outs[t].at[j], loc.at[t]) for t in range(nt)]
        for cp in local:
            cp.start()
        sends = []
        for k, (fx, fy) in enumerate(_CHIP_RELS):
            peer = (_flip(x, fx), _flip(y, fy), c)
            for t in range(nt):
                half = pl.ds(c * (rows[t] // 2), rows[t] // 2)
                cp = _remote(srcs[t].at[half], outs[t].at[j, half], ici_s.at[3 * t + k], ici_r.at[3 * t + k], peer)
                cp.start()
                sends.append(cp)
        fwds = []
        for k, (fx, fy) in enumerate(_CHIP_RELS):
            pj = 2 * _flip(x, fx) + _flip(y, fy)
            for t in range(nt):
                half = pl.ds(c * (rows[t] // 2), rows[t] // 2)
                blk = outs[t].at[pj, half]
                _remote(blk, blk, ici_s.at[3 * t + k], ici_r.at[3 * t + k], sib).wait_recv()
                cp = _remote(blk, blk, d2d_s.at[3 * t + k], d2d_r.at[3 * t + k], sib)
                cp.start()
                fwds.append(cp)
        for cp in fwds:
            cp.wait()
        for cp in sends:
            cp.wait_send()
        for cp in local:
            cp.wait()

    sems = [pltpu.SemaphoreType.DMA((3 * nt,)) for _ in range(4)] + [pltpu.SemaphoreType.DMA((nt,))]
    return pl.pallas_call(
        body, name="gather_weights",
        out_shape=tuple(jax.ShapeDtypeStruct((N_CHIPS,) + s.shape, s.dtype) for s in shards),
        in_specs=[ANY] * nt, out_specs=tuple([ANY] * nt), scratch_shapes=sems,
    )(*shards)


VMEM_WHOLE = pl.BlockSpec(memory_space=pltpu.VMEM)


def _pair_reduce(grads, name):
    nt = len(grads)
    r2 = [g.shape[2] for g in grads]
    off = [sum(r2[:t]) for t in range(nt)]
    tot = sum(r2)

    def body(*refs):
        gs = refs[:nt]
        s_ref, got, ssem, rsem = refs[nt:]
        x, y, c = _me()
        sib = (x, y, 1 - c)
        for t in range(nt):
            for k in range(N_CHIPS):
                _remote(gs[t].at[k, 1 - c], got.at[k, pl.ds(off[t], r2[t])], ssem, rsem, sib).start()
        _remote(got, got, ssem, rsem, sib).wait()
        for t in range(nt):
            for k in range(N_CHIPS):
                rows = slice(off[t], off[t] + r2[t])
                s_ref[k, rows, :] = (gs[t][k, c].astype(F32) + got[k, rows, :].astype(F32)).astype(BF16)

    shp = jax.ShapeDtypeStruct((N_CHIPS, tot, D_MODEL), BF16)
    return pl.pallas_call(
        body, name=name, out_shape=shp, in_specs=[VMEM_WHOLE] * nt, out_specs=VMEM_WHOLE,
        scratch_shapes=[pltpu.VMEM((N_CHIPS, tot, D_MODEL), BF16), pltpu.SemaphoreType.DMA(()),
                        pltpu.SemaphoreType.DMA(())],
        compiler_params=pltpu.CompilerParams(vmem_limit_bytes=VMEM_BIG),
    )(*grads)


def _chip_exchange(parts):
    ng = len(parts)
    r2 = [p.shape[1] for p in parts]
    off = [sum(r2[:g]) for g in range(ng)]
    tot = sum(r2)

    def body(*refs):
        ps = refs[:ng]
        own_ref, rec_ref, ssems, rsems, lsem = refs[ng:]
        x, y, c = _me()
        j = 2 * x + y
        for g in range(ng):
            pltpu.make_async_copy(ps[g].at[j], own_ref.at[pl.ds(off[g], r2[g])], lsem).start()
        for k, (fx, fy) in enumerate(_CHIP_RELS):
            px, py = _flip(x, fx), _flip(y, fy)
            for g in range(ng):
                for st, sz in _row_pieces(r2[g], 2):
                    _remote(ps[g].at[2 * px + py, pl.ds(st, sz)], rec_ref.at[k, pl.ds(off[g] + st, sz)],
                            ssems.at[k], rsems.at[k], (px, py, c)).start()
        for k in range(3):
            _remote(rec_ref.at[k], rec_ref.at[k], ssems.at[k], rsems.at[k], (x, y, c)).wait()
        pltpu.make_async_copy(own_ref, own_ref, lsem).wait()

    return pl.pallas_call(
        body, name="grad_chip_exchange",
        out_shape=(jax.ShapeDtypeStruct((tot, D_MODEL), BF16), jax.ShapeDtypeStruct((3, tot, D_MODEL), BF16)),
        in_specs=[ANY] * ng, out_specs=(ANY, ANY),
        scratch_shapes=[pltpu.SemaphoreType.DMA((3,)), pltpu.SemaphoreType.DMA((3,)), pltpu.SemaphoreType.DMA(())],
    )(*parts)


def _final_reduce(own, rec, name):
    r2 = own.shape[0]
    pieces = _row_pieces(r2, DMA_SPLIT)

    def body(own_ref, rec_ref, o_ref, fbuf, ssem, rsem, lsem):
        x, y, c = _me()
        sib = (x, y, 1 - c)
        for st, sz in pieces:
            rows = slice(st, st + sz)
            fbuf[rows, :] = (own_ref[rows, :].astype(F32) + rec_ref[0, rows, :].astype(F32)
                             + rec_ref[1, rows, :].astype(F32) + rec_ref[2, rows, :].astype(F32))
            pltpu.make_async_copy(fbuf.at[pl.ds(st, sz)], o_ref.at[c, pl.ds(st, sz)], lsem).start()
            _remote(fbuf.at[pl.ds(st, sz)], o_ref.at[c, pl.ds(st, sz)], ssem, rsem, sib).start()
        _remote(fbuf, o_ref.at[c], ssem, rsem, sib).wait()
        pltpu.make_async_copy(fbuf, o_ref.at[c], lsem).wait()

    return pl.pallas_call(
        body, name=name, out_shape=jax.ShapeDtypeStruct((2, r2, D_MODEL), F32),
        in_specs=[VMEM_WHOLE, VMEM_WHOLE], out_specs=ANY,
        scratch_shapes=[pltpu.VMEM((r2, D_MODEL), F32), pltpu.SemaphoreType.DMA(()), pltpu.SemaphoreType.DMA(()),
                        pltpu.SemaphoreType.DMA(())],
        compiler_params=pltpu.CompilerParams(vmem_limit_bytes=VMEM_BIG),
    )(own, rec)


def _allreduce_small(vec):
    def body(v_ref, o_ref, buf, send_sems, recv_sems):
        x, y, c = _me()
        me = 4 * x + 2 * y + c
        buf[me] = v_ref[...]
        copies = []
        for k in range(1, 8):
            peer = (_flip(x, (k >> 2) & 1), _flip(y, (k >> 1) & 1), _flip(c, k & 1))
            cp = _remote(v_ref, buf.at[me], send_sems.at[k - 1], recv_sems.at[k - 1], peer)
            cp.start()
            copies.append(cp)
        for cp in copies:
            cp.wait()
        acc = buf[0]
        for i in range(1, 8):
            acc = acc + buf[i]
        o_ref[...] = acc

    vm = pl.BlockSpec(memory_space=pltpu.VMEM)
    return pl.pallas_call(
        body, name="allreduce_small", out_shape=jax.ShapeDtypeStruct(vec.shape, vec.dtype),
        in_specs=[vm], out_specs=vm,
        scratch_shapes=[pltpu.VMEM((8,) + vec.shape, vec.dtype), pltpu.SemaphoreType.DMA((7,)),
                        pltpu.SemaphoreType.DMA((7,))],
    )(vec)


class _GatherRider:
    def __init__(self, shards):
        self.inputs = list(shards)
        nt = len(shards)
        self.out_shape = [jax.ShapeDtypeStruct((N_CHIPS,) + s.shape, s.dtype) for s in shards]
        self.scratch = [pltpu.SemaphoreType.DMA((3 * nt,)), pltpu.SemaphoreType.DMA((3 * nt,)),
                        pltpu.SemaphoreType.DMA((nt,))]

    def _copies(self, srcs, outs, sems):
        ici_s, ici_r, loc = sems
        x, y, c = _me()
        j = 2 * x + y
        local = [pltpu.make_async_copy(srcs[t], outs[t].at[j], loc.at[t]) for t in range(len(srcs))]
        remote = []
        for k, (fx, fy) in enumerate(_CHIP_RELS):
            peer = (_flip(x, fx), _flip(y, fy), c)
            for t in range(len(srcs)):
                remote.append(_remote(srcs[t], outs[t].at[j], ici_s.at[3 * t + k], ici_r.at[3 * t + k], peer))
        return local, remote

    def start(self, srcs, outs, sems):
        local, remote = self._copies(srcs, outs, sems)
        for cp in local + remote:
            cp.start()

    def finish(self, srcs, outs, sems):
        local, remote = self._copies(srcs, outs, sems)
        for cp in remote + local:
            cp.wait()


class _ExchangeRider:
    def __init__(self, parts):
        self.inputs = list(parts)
        self.r2 = [p.shape[1] for p in parts]
        self.off = [sum(self.r2[:g]) for g in range(len(parts))]
        tot = sum(self.r2)
        self.out_shape = [jax.ShapeDtypeStruct((tot, D_MODEL), BF16), jax.ShapeDtypeStruct((3, tot, D_MODEL), BF16)]
        self.scratch = [pltpu.SemaphoreType.DMA((3,)), pltpu.SemaphoreType.DMA((3,)), pltpu.SemaphoreType.DMA(())]

    def start(self, ps, outs, sems):
        own_ref, rec_ref = outs
        ssems, rsems, lsem = sems
        x, y, c = _me()
        j = 2 * x + y
        for g in range(len(ps)):
            pltpu.make_async_copy(ps[g].at[j], own_ref.at[pl.ds(self.off[g], self.r2[g])], lsem).start()
        for k, (fx, fy) in enumerate(_CHIP_RELS):
            px, py = _flip(x, fx), _flip(y, fy)
            for g in range(len(ps)):
                for st, sz in _row_pieces(self.r2[g], 2):
                    _remote(ps[g].at[2 * px + py, pl.ds(st, sz)], rec_ref.at[k, pl.ds(self.off[g] + st, sz)],
                            ssems.at[k], rsems.at[k], (px, py, c)).start()

    def finish(self, ps, outs, sems):
        own_ref, rec_ref = outs
        ssems, rsems, lsem = sems
        x, y, c = _me()
        for k in range(3):
            _remote(rec_ref.at[k], rec_ref.at[k], ssems.at[k], rsems.at[k], (x, y, c)).wait()
        pltpu.make_async_copy(own_ref, own_ref, lsem).wait()


def _pallas(body, args, *, name, grid, in_specs, out_specs, out_shape, scratch_shapes=(), sem=None, vmem=None,
            rider=None):
    if rider is None:
        res = pl.pallas_call(body, name=name, grid=grid, in_specs=list(in_specs), out_specs=tuple(out_specs),
                             out_shape=tuple(out_shape), scratch_shapes=list(scratch_shapes),
                             compiler_params=_params(sem, vmem))(*args)
        return tuple(res), ()
    n_in, n_out, n_sc = len(in_specs), len(out_shape), len(scratch_shapes)
    r_in, r_out = len(rider.inputs), len(rider.out_shape)

    def wrapped(*refs):
        ins, rins = refs[:n_in], refs[n_in:n_in + r_in]
        p = n_in + r_in
        outs, routs = refs[p:p + n_out], refs[p + n_out:p + n_out + r_out]
        p += n_out + r_out
        scr, rsems = refs[p:p + n_sc], refs[p + n_sc:]
        first = pl.program_id(0) == 0
        last = pl.program_id(0) == grid[0] - 1
        for a in range(1, len(grid)):
            first = first & (pl.program_id(a) == 0)
            last = last & (pl.program_id(a) == grid[a] - 1)

        @pl.when(first)
        def _():
            rider.start(rins, routs, rsems)

        body(*ins, *outs, *scr)

        @pl.when(last)
        def _():
            rider.finish(rins, routs, rsems)

    res = pl.pallas_call(
        wrapped, name=name, grid=grid, in_specs=list(in_specs) + [ANY] * r_in,
        out_specs=tuple(out_specs) + (ANY,) * r_out, out_shape=tuple(out_shape) + tuple(rider.out_shape),
        scratch_shapes=list(scratch_shapes) + rider.scratch,
        compiler_params=_params(("arbitrary",) * len(grid), vmem))(*args, *rider.inputs)
    return tuple(res[:n_out]), tuple(res[n_out:])


def _ffn_fwd(h, gain, wgt, wut, wd, rider=None):
    t = h.shape[0]
    fc = D_FF // FF_CHUNKS

    def body(h_ref, gain_ref, wg_hbm, wu_hbm, wd_hbm, hout_ref, n_ref, g_ref, u_ref, wg_v, wu_v, wd_v):
        @pl.when(pl.program_id(0) == 0)
        def _():
            pltpu.sync_copy(wg_hbm, wg_v)
            pltpu.sync_copy(wu_hbm, wu_v)
            pltpu.sync_copy(wd_hbm, wd_v)

        hh = h_ref[...]
        r = lax.rsqrt(jnp.mean(hh * hh, axis=-1, keepdims=True) + EPS)
        n = (hh * r * gain_ref[...]).astype(BF16)
        n_ref[...] = n
        acc = jnp.zeros((TM, D_MODEL), F32)
        for ci in range(FF_CHUNKS):
            sl = slice(ci * fc, (ci + 1) * fc)
            g = _dot_nt(n, wg_v[sl, :])
            u = _dot_nt(n, wu_v[sl, :])
            g_ref[:, sl] = g.astype(BF16)
            u_ref[:, sl] = u.astype(BF16)
            a = (g * _sigmoid(g) * u).astype(BF16)
            acc = acc + _dot(a, wd_v[sl, :])
        hout_ref[...] = hh + 0.5 * acc

    row = lambda w: pl.BlockSpec((TM, w), lambda i: (i, 0))
    wv = pltpu.VMEM((D_FF, D_MODEL), BF16)
    return _pallas(
        body, (h, gain, wgt, wut, wd), name="ffn_fwd", grid=(t // TM,),
        out_shape=(jax.ShapeDtypeStruct((t, D_MODEL), F32), jax.ShapeDtypeStruct((t, D_MODEL), BF16),
                   jax.ShapeDtypeStruct((t, D_FF), BF16), jax.ShapeDtypeStruct((t, D_FF), BF16)),
        in_specs=[row(D_MODEL), pl.BlockSpec((1, D_MODEL), lambda i: (0, 0)), ANY, ANY, ANY],
        out_specs=(row(D_MODEL), row(D_MODEL), row(D_FF), row(D_FF)),
        scratch_shapes=[wv, wv, wv], sem=("arbitrary",), vmem=VMEM_BIG, rider=rider)


def _ffn_bwd(dhout, h, gain, g, u, wgt, wut, wd):
    t = h.shape[0]
    tm = TM_BWD
    fc = D_FF // FF_CHUNKS

    def body(dho_ref, h_ref, gain_ref, g_ref, u_ref, wg_hbm, wu_hbm, wd_hbm,
             dh_ref, dg_ref, du_ref, a_ref, df_ref, gg_ref, wg_v, wu_v, wd_v):
        @pl.when(pl.program_id(0) == 0)
        def _():
            pltpu.sync_copy(wg_hbm, wg_v)
            pltpu.sync_copy(wu_hbm, wu_v)
            pltpu.sync_copy(wd_hbm, wd_v)
            gg_ref[...] = jnp.zeros_like(gg_ref)

        dho = dho_ref[...]
        df = (0.5 * dho).astype(BF16)
        df_ref[...] = df
        dn = jnp.zeros((tm, D_MODEL), F32)
        for ci in range(FF_CHUNKS):
            sl = slice(ci * fc, (ci + 1) * fc)
            da = _dot_nt(df, wd_v[sl, :])
            gv = g_ref[:, sl].astype(F32)
            uv = u_ref[:, sl].astype(F32)
            sg = _sigmoid(gv)
            silu = gv * sg
            dg = (da * uv * (sg * (1.0 + gv * (1.0 - sg)))).astype(BF16)
            du = (da * silu).astype(BF16)
            dg_ref[:, sl] = dg
            du_ref[:, sl] = du
            a_ref[:, sl] = (silu * uv).astype(BF16)
            dn = dn + _dot(dg, wg_v[sl, :]) + _dot(du, wu_v[sl, :])
        hh = h_ref[...]
        r = lax.rsqrt(jnp.mean(hh * hh, axis=-1, keepdims=True) + EPS)
        hn = hh * r
        gg_ref[...] += jnp.sum(dn * hn, axis=0, keepdims=True)
        dng = dn * gain_ref[...]
        dh_ref[...] = dho + r * (dng - hn * jnp.mean(dng * hn, axis=-1, keepdims=True))

    row = lambda w: pl.BlockSpec((tm, w), lambda i: (i, 0))
    vec = pl.BlockSpec((1, D_MODEL), lambda i: (0, 0))
    wv = pltpu.VMEM((D_FF, D_MODEL), BF16)
    return pl.pallas_call(
        body, name="ffn_bwd", grid=(t // tm,),
        out_shape=(jax.ShapeDtypeStruct((t, D_MODEL), F32), jax.ShapeDtypeStruct((t, D_FF), BF16),
                   jax.ShapeDtypeStruct((t, D_FF), BF16), jax.ShapeDtypeStruct((t, D_FF), BF16),
                   jax.ShapeDtypeStruct((t, D_MODEL), BF16), jax.ShapeDtypeStruct((1, D_MODEL), F32)),
        in_specs=[row(D_MODEL), row(D_MODEL), vec, row(D_FF), row(D_FF), ANY, ANY, ANY],
        out_specs=(row(D_MODEL), row(D_FF), row(D_FF), row(D_FF), row(D_MODEL), vec),
        scratch_shapes=[wv, wv, wv],
        compiler_params=_params(("arbitrary",), VMEM_BIG),
    )(dhout, h, gain, g, u, wgt, wut, wd)


def _wgrad(lhs, rhs, rb, with_colsum=False, name="wgrad", rider=None):
    t, k = lhs.shape
    n = rhs.shape[1]
    tk = 512
    nt = t // tk

    def body(l_ref, r_ref, o_ref, *rest):
        acc = rest[-1]
        ti = pl.program_id(1)

        @pl.when(ti == 0)
        def _():
            acc[...] = jnp.zeros_like(acc)
            if with_colsum:
                rest[0][...] = jnp.zeros_like(rest[0])

        acc[...] += _dot_tn(l_ref[...], r_ref[...])
        if with_colsum:
            rest[0][...] += jnp.sum(l_ref[...].astype(F32), axis=0, keepdims=True)

        @pl.when(ti == nt - 1)
        def _():
            o_ref[...] = acc[...].astype(BF16)

    out_shape = [jax.ShapeDtypeStruct((k, n), BF16)]
    out_specs = [pl.BlockSpec((rb, n), lambda j, i: (j, 0))]
    if with_colsum:
        out_shape.append(jax.ShapeDtypeStruct((1, k), F32))
        out_specs.append(pl.BlockSpec((1, rb), lambda j, i: (0, j)))
    res, ro = _pallas(
        body, (lhs, rhs), name=name, grid=(k // rb, nt), out_shape=tuple(out_shape),
        in_specs=[pl.BlockSpec((tk, rb), lambda j, i: (i, j)), pl.BlockSpec((tk, n), lambda j, i: (i, 0))],
        out_specs=tuple(out_specs), scratch_shapes=[pltpu.VMEM((rb, n), F32)],
        sem=("arbitrary", "arbitrary"), vmem=VMEM_BIG, rider=rider)
    if rider is not None:
        return res[0], ro
    return res if with_colsum else res[0]


def _lane_blocks(nseq, seq, nblk, tm=TM):
    spt = seq // tm
    return pl.BlockSpec((1, nblk, tm, 128), lambda i: (i // spt, 0, i % spt, 0))


def _inproj_fwd(h, gain, wint, b_in, nseq, rider=None):
    t = h.shape[0]
    seq = t // nseq
    half_a = ZA_W // 2
    pieces = ((0, half_a, 0, 0), (half_a, half_a, 0, half_a), (ZA_W, ZB_W, 1, 0), (ZA_W + ZB_W, 1024, 2, 0),
              (ZA_W + ZB_W + 1024, 1024, 2, 1024))

    def body(h_ref, gain_ref, w_hbm, b_ref, u_ref, za_ref, zb_ref, zg_ref, w_v):
        @pl.when(pl.program_id(0) == 0)
        def _():
            pltpu.sync_copy(w_hbm, w_v)

        hh = h_ref[...]
        r = lax.rsqrt(jnp.mean(hh * hh, axis=-1, keepdims=True) + EPS)
        un = (hh * r * gain_ref[...]).astype(BF16)
        u_ref[...] = un
        outs = (None, zb_ref, zg_ref)
        for c0, cw, oi, o0 in pieces:
            val = _dot_nt(un, w_v[c0:c0 + cw, :]) + b_ref[:, c0:c0 + cw]
            if oi == 0:
                for cb in range(cw // 128):
                    za_ref[0, o0 // 128 + cb] = val[:, cb * 128:(cb + 1) * 128]
            else:
                outs[oi][:, o0:o0 + cw] = val.astype(BF16)

    row = lambda w: pl.BlockSpec((TM, w), lambda i: (i, 0))
    return _pallas(
        body, (h, gain, wint, b_in), name="inproj_fwd", grid=(t // TM,),
        out_shape=(jax.ShapeDtypeStruct((t, D_MODEL), BF16), jax.ShapeDtypeStruct((nseq, ZA_W // 128, seq, 128), F32),
                   jax.ShapeDtypeStruct((t, ZB_W), BF16), jax.ShapeDtypeStruct((t, 2 * D_MODEL), BF16)),
        in_specs=[row(D_MODEL), pl.BlockSpec((1, D_MODEL), lambda i: (0, 0)), ANY,
                  pl.BlockSpec((1, D_IN), lambda i: (0, 0))],
        out_specs=(row(D_MODEL), _lane_blocks(nseq, seq, ZA_W // 128), row(ZB_W), row(2 * D_MODEL)),
        scratch_shapes=[pltpu.VMEM((D_IN, D_MODEL), BF16)], sem=("arbitrary",), vmem=VMEM_BIG, rider=rider)


def _inproj_bwd(dz, dh2, h, gain, wint, rider=None):
    t = h.shape[0]
    nc = 5
    cw = D_IN // nc

    def body(dz_ref, dh2_ref, h_ref, gain_ref, w_hbm, dh_ref, gg_ref, w_v):
        @pl.when(pl.program_id(0) == 0)
        def _():
            pltpu.sync_copy(w_hbm, w_v)
            gg_ref[...] = jnp.zeros_like(gg_ref)

        du = jnp.zeros((TM, D_MODEL), F32)
        for ci in range(nc):
            sl = slice(ci * cw, (ci + 1) * cw)
            du = du + _dot(dz_ref[:, sl], w_v[sl, :])
        hh = h_ref[...]
        r = lax.rsqrt(jnp.mean(hh * hh, axis=-1, keepdims=True) + EPS)
        hn = hh * r
        gg_ref[...] += jnp.sum(du * hn, axis=0, keepdims=True)
        dng = du * gain_ref[...]
        dh_ref[...] = dh2_ref[...] + r * (dng - hn * jnp.mean(dng * hn, axis=-1, keepdims=True))

    row = lambda w: pl.BlockSpec((TM, w), lambda i: (i, 0))
    vec = pl.BlockSpec((1, D_MODEL), lambda i: (0, 0))
    return _pallas(
        body, (dz, dh2, h, gain, wint), name="inproj_bwd", grid=(t // TM,),
        out_shape=(jax.ShapeDtypeStruct((t, D_MODEL), F32), jax.ShapeDtypeStruct((1, D_MODEL), F32)),
        in_specs=[row(D_IN), row(D_MODEL), row(D_MODEL), vec, ANY],
        out_specs=(row(D_MODEL), vec),
        scratch_shapes=[pltpu.VMEM((D_IN, D_MODEL), BF16)], sem=("arbitrary",), vmem=VMEM_BIG, rider=rider)


def _head_sum_matrix(w):
    i = lax.broadcasted_iota(jnp.int32, (w, w), 0) // HEAD_DIM
    j = lax.broadcasted_iota(jnp.int32, (w, w), 1) // HEAD_DIM
    return (i == j).astype(F32)


def _merge_fwd(o0, o1, o2, l0, l1, l2, yb, zg, h1, wat, wbt, wout):
    t = h1.shape[0]
    nseq, _, seq, _ = o0.shape

    def body(o0_ref, o1_ref, o2_ref, l0_ref, l1_ref, l2_ref, yb_ref, ga_ref, gb_ref, h1_ref, wa_ref, wb_ref, wo_ref,
             h2_ref, y_ref, lt_ref, pa_ref, pb_ref, mg_ref):
        wide = lambda ref: jnp.concatenate([ref[0, 0], ref[0, 1]], axis=1)
        la, lb, lc = wide(l0_ref), wide(l1_ref), wide(l2_ref)
        mx = jnp.maximum(jnp.maximum(la, lb), lc)
        ea, eb, ec = jnp.exp(la - mx), jnp.exp(lb - mx), jnp.exp(lc - mx)
        den = ea + eb + ec
        y = (ea * wide(o0_ref) + eb * wide(o1_ref) + ec * wide(o2_ref)) / den
        lt = mx + jnp.log(den)
        lt_ref[0, 0] = lt[:, :128]
        lt_ref[0, 1] = lt[:, 128:]
        yb16 = y.astype(BF16)
        y_ref[...] = yb16
        pa = _dot_nt(yb16, wa_ref[...])
        pb = _dot_nt(yb_ref[...], wb_ref[...])
        pa_ref[...] = pa.astype(BF16)
        pb_ref[...] = pb.astype(BF16)
        mg = (_sigmoid(ga_ref[...].astype(F32)) * pa + _sigmoid(gb_ref[...].astype(F32)) * pb).astype(BF16)
        mg_ref[...] = mg
        h2_ref[...] = h1_ref[...] + _dot(mg, wo_ref[...])

    row = lambda w: pl.BlockSpec((TM, w), lambda i: (i, 0))
    full = lambda a: pl.BlockSpec(a.shape, lambda i: (0, 0))
    gate = lambda cb: pl.BlockSpec((TM, D_MODEL), lambda i: (i, cb))
    return pl.pallas_call(
        body, name="merge_fwd", grid=(t // TM,),
        out_shape=(jax.ShapeDtypeStruct((t, D_MODEL), F32), jax.ShapeDtypeStruct((t, GW), BF16),
                   jax.ShapeDtypeStruct((nseq, 2, seq, 128), F32), jax.ShapeDtypeStruct((t, D_MODEL), BF16),
                   jax.ShapeDtypeStruct((t, D_MODEL), BF16), jax.ShapeDtypeStruct((t, D_MODEL), BF16)),
        in_specs=[_lane_blocks(nseq, seq, 2)] * 6 + [row(2 * GW), gate(0), gate(1), row(D_MODEL), full(wat), full(wbt),
                                                     full(wout)],
        out_specs=(row(D_MODEL), row(GW), _lane_blocks(nseq, seq, 2), row(D_MODEL), row(D_MODEL), row(D_MODEL)),
        compiler_params=_params(("parallel",), VMEM_BIG),
    )(o0, o1, o2, l0, l1, l2, yb, zg, zg, h1, wat, wbt, wout)


def _merge_bwd(dh2, pa, pb, zg, y, yb, wat, wbt, wout, nseq, rider=None):
    t = dh2.shape[0]

    def body(dh2_ref, pa_ref, pb_ref, ga_ref, gb_ref, y_ref, yb_ref, wa_ref, wb_ref, wo_ref,
             dpa_ref, dpb_ref, dga_ref, dgb_ref, dya_ref, dyb_ref, dh2b_ref, ca_ref, cb_ref):
        d16 = dh2_ref[...].astype(BF16)
        dh2b_ref[...] = d16
        dm = _dot_nt(d16, wo_ref[...])
        sa = _sigmoid(ga_ref[...].astype(F32))
        sb = _sigmoid(gb_ref[...].astype(F32))
        dpa = (dm * sa).astype(BF16)
        dpb = (dm * sb).astype(BF16)
        dpa_ref[...] = dpa
        dpb_ref[...] = dpb
        dga_ref[...] = (dm * pa_ref[...].astype(F32) * sa * (1.0 - sa)).astype(BF16)
        dgb_ref[...] = (dm * pb_ref[...].astype(F32) * sb * (1.0 - sb)).astype(BF16)
        dya = _dot(dpa, wa_ref[...])
        dyb = _dot(dpb, wb_ref[...])
        dya_ref[0, 0] = dya[:, :128]
        dya_ref[0, 1] = dya[:, 128:]
        dyb_ref[...] = dyb.astype(BF16)
        hp = lax.Precision.HIGHEST
        ca = jnp.dot(dya * y_ref[...].astype(F32), _head_sum_matrix(GW), precision=hp, preferred_element_type=F32)
        ca_ref[0, 0] = ca[:, :128]
        ca_ref[0, 1] = ca[:, 128:]
        cb_ref[...] = jnp.dot(dyb * yb_ref[...].astype(F32), _head_sum_matrix(2 * GW), precision=hp,
                              preferred_element_type=F32)

    row = lambda w: pl.BlockSpec((TM, w), lambda i: (i, 0))
    full = lambda a: pl.BlockSpec(a.shape, lambda i: (0, 0))
    gate = lambda cb: pl.BlockSpec((TM, D_MODEL), lambda i: (i, cb))
    bf = lambda w: jax.ShapeDtypeStruct((t, w), BF16)
    lanes = jax.ShapeDtypeStruct((nseq, 2, t // nseq, 128), F32)
    lane_spec = _lane_blocks(nseq, t // nseq, 2)
    return _pallas(
        body, (dh2, pa, pb, zg, zg, y, yb, wat, wbt, wout), name="merge_bwd", grid=(t // TM,),
        out_shape=(bf(D_MODEL), bf(D_MODEL), bf(D_MODEL), bf(D_MODEL), lanes, bf(2 * GW), bf(D_MODEL),
                   lanes, jax.ShapeDtypeStruct((t, 2 * GW), F32)),
        in_specs=[row(D_MODEL), row(D_MODEL), row(D_MODEL), gate(0), gate(1), row(GW), row(2 * GW),
                  full(wat), full(wbt), full(wout)],
        out_specs=(row(D_MODEL), row(D_MODEL), row(D_MODEL), row(D_MODEL), lane_spec, row(2 * GW), row(D_MODEL),
                   lane_spec, row(2 * GW)),
        sem=("parallel",), vmem=VMEM_BIG, rider=rider)


def _loss_head(h3, gain, tgt):
    t = h3.shape[0]

    def body(h_ref, gain_ref, t_ref, dh_ref, loss_ref, gg_ref):
        @pl.when(pl.program_id(0) == 0)
        def _():
            loss_ref[...] = jnp.zeros_like(loss_ref)
            gg_ref[...] = jnp.zeros_like(gg_ref)

        hh = h_ref[...]
        r = lax.rsqrt(jnp.mean(hh * hh, axis=-1, keepdims=True) + EPS)
        hn = hh * r
        err = hn * gain_ref[...] - t_ref[...]
        part = jnp.sum(jnp.sum(err * err, axis=1, keepdims=True), axis=0, keepdims=True)
        loss_ref[...] += (0.5 / D_MODEL) * part
        dy = err * (1.0 / D_MODEL)
        gg_ref[...] += jnp.sum(dy * hn, axis=0, keepdims=True)
        dng = dy * gain_ref[...]
        dh_ref[...] = r * (dng - hn * jnp.mean(dng * hn, axis=-1, keepdims=True))

    row = pl.BlockSpec((TM, D_MODEL), lambda i: (i, 0))
    vec = pl.BlockSpec((1, D_MODEL), lambda i: (0, 0))
    return pl.pallas_call(
        body, name="loss_head", grid=(t // TM,),
        out_shape=(jax.ShapeDtypeStruct((t, D_MODEL), F32), jax.ShapeDtypeStruct((8, 128), F32),
                   jax.ShapeDtypeStruct((1, D_MODEL), F32)),
        in_specs=[row, vec, row], out_specs=(row, pl.BlockSpec((8, 128), lambda i: (0, 0)), vec),
        compiler_params=_params(("arbitrary",)),
    )(h3, gain, tgt)


def _lane_head(rows):
    return lax.broadcasted_iota(jnp.int32, (rows, GW), 1) // HEAD_DIM


def _kv_expand_matrix(r):
    ci = lax.broadcasted_iota(jnp.int32, (2 * HEAD_DIM, GW), 0)
    ji = lax.broadcasted_iota(jnp.int32, (2 * HEAD_DIM, GW), 1)
    return (ci == (ji % HEAD_DIM) + HEAD_DIM * r).astype(BF16)


def _block_rows(row0, stride, ib):
    start = row0 + (stride * BLOCK) * ib
    if stride > 1:
        return pl.ds(start, BLOCK, stride=stride)
    return pl.ds(pl.multiple_of(start, BLOCK), BLOCK)


def _stack_heads(x, lane_head):
    return jnp.concatenate([jnp.where(lane_head == h, x, jnp.zeros_like(x)) for h in range(4)], axis=0)


def _unstack_heads(x4, lane_head):
    out = jnp.zeros((BLOCK, GW), F32)
    for h in range(4):
        out = jnp.where(lane_head == h, x4[h * BLOCK:(h + 1) * BLOCK], out)
    return out


def _load_rows(ref, rows, split):
    if split:
        return jnp.concatenate([ref[0, 0, rows, :], ref[0, 1, rows, :]], axis=1)
    return ref[0, rows, :]


def _store_rows(ref, rows, val, split):
    if split:
        ref[0, 0, rows, :] = val[:, :128]
        ref[0, 1, rows, :] = val[:, 128:]
    else:
        ref[0, rows, :] = val


def _attn_fwd(q_arr, k_arr, v_arr, bias, sink, *, grid, seq, stride, kvw, split, q_spec, k_spec, v_spec, bias_map,
              sink_map, o_spec, has_sink, o_shape, o_dtype, name, rider=None):
    nb = seq // stride // BLOCK
    scale = HEAD_DIM ** -0.5
    expanded = kvw != GW
    rps = RESIDUES_PER_STEP if stride >= 4 * RESIDUES_PER_STEP else 1
    grid = (grid[0], grid[1] // rps)
    assert not has_sink or B_WINDOW - 1 < BLOCK

    def body(q_ref, k_ref, v_ref, bias_ref, sink_ref, o_ref, lse_ref, *kv_x):
        rr = pl.program_id(1)
        lane_head = _lane_head(BLOCK)
        if expanded:
            expand = _kv_expand_matrix(rr)
            kv_x[0][...] = _dot(k_ref[0], expand).astype(BF16)
            kv_x[1][...] = _dot(v_ref[0], expand).astype(BF16)
        for j in range(rps):
            residue(rr * rps + j if stride > 1 else 0, q_ref, k_ref, v_ref, bias_ref, sink_ref, o_ref, lse_ref, kv_x,
                    lane_head)

    def residue(row0, q_ref, k_ref, v_ref, bias_ref, sink_ref, o_ref, lse_ref, kv_x, lane_head):
        def per_head(fn, x):
            return jnp.concatenate([fn(sink_ref[0, h:h + 1, 0:1], x[h * BLOCK:(h + 1) * BLOCK]) for h in range(4)],
                                   axis=0)

        def load(ref, ib):
            return _load_rows(ref, _block_rows(row0, stride, ib), split).astype(BF16)

        def load_kv(which, ib):
            if expanded:
                return kv_x[which][_block_rows(0, 1, ib), :]
            return load((k_ref, v_ref)[which], ib)

        def block(ib, first):
            q4 = _stack_heads(load(q_ref, ib), lane_head)
            if first:
                kc, vc = load_kv(0, ib), load_kv(1, ib)
                b4 = bias_ref[:, :, BLOCK:].reshape(4 * BLOCK, BLOCK)
            else:
                kc = jnp.concatenate([load_kv(0, ib - 1), load_kv(0, ib)], axis=0)
                vc = jnp.concatenate([load_kv(1, ib - 1), load_kv(1, ib)], axis=0)
                b4 = bias_ref[...].reshape(4 * BLOCK, 2 * BLOCK)
                if has_sink:
                    oldest = lax.broadcasted_iota(jnp.int32, kc.shape, 0) == 0
                    kc = jnp.where(oldest, jnp.zeros_like(kc), kc)
                    vc = jnp.where(oldest, jnp.zeros_like(vc), vc)
            s = _dot_nt(q4, kc) * scale + b4
            m = jnp.max(s, axis=-1, keepdims=True)
            if has_sink and first:
                m = per_head(jnp.maximum, m)
            p = jnp.exp(s - m)
            l = jnp.sum(p, axis=-1, keepdims=True)
            if has_sink and first:
                l = l + per_head(lambda sk, mh: jnp.exp(sk - mh), m)
            o4 = _dot(p.astype(BF16), vc) / l
            rows = _block_rows(row0, stride, ib)
            _store_rows(o_ref, rows, _unstack_heads(o4, lane_head).astype(o_dtype), split)
            _store_rows(lse_ref, rows, _unstack_heads(m + jnp.log(l), lane_head), split)

        block(0, True)
        if nb > 1:
            def step(i, carry):
                block(i, False)
                return carry
            lax.fori_loop(1, nb, step, 0)

    return _pallas(
        body, (q_arr, k_arr, v_arr, bias, sink), name=name, grid=grid,
        out_shape=(jax.ShapeDtypeStruct(o_shape, o_dtype), jax.ShapeDtypeStruct(o_shape, F32)),
        in_specs=[q_spec, k_spec, v_spec,
                  pl.BlockSpec((4, BLOCK, 2 * BLOCK), bias_map), pl.BlockSpec((1, 4, 128), sink_map)],
        out_specs=(o_spec, o_spec),
        scratch_shapes=[pltpu.VMEM((seq, GW), BF16)] * 2 if expanded else [],
        sem=("arbitrary", "arbitrary"), vmem=VMEM_BIG, rider=rider)


def _attn_bwd(q_arr, k_arr, v_arr, bias, sink, dy, cc, lse, *, grid, seq, stride, kvw, split, q_spec, k_spec, v_spec,
              bias_map, sink_map, o_spec, kv_out_spec, has_sink, n_bias, dq_shape, dkv_shape, g_dtype, name):
    ln = seq // stride
    nb = ln // BLOCK
    scale = HEAD_DIM ** -0.5
    expanded = kvw != GW
    rps = RESIDUES_PER_STEP if stride >= 4 * RESIDUES_PER_STEP else 1
    grid = (grid[0], grid[1] // rps)

    def body(q_ref, k_ref, v_ref, bias_ref, sink_ref, dy_ref, c_ref, lse_ref,
             dq_ref, dk_ref, dv_ref, db_ref, dsk_ref, dk_acc, dv_acc, dk_half, dv_half, *kv_x):
        rr = pl.program_id(1)

        @pl.when((pl.program_id(0) == 0) & (rr == 0))
        def _():
            db_ref[...] = jnp.zeros_like(db_ref)
            dsk_ref[...] = jnp.zeros_like(dsk_ref)

        if expanded:
            expand = _kv_expand_matrix(rr)
            kv_x[0][...] = _dot(k_ref[0], expand).astype(BF16)
            kv_x[1][...] = _dot(v_ref[0], expand).astype(BF16)
        refs = (q_ref, k_ref, v_ref, bias_ref, sink_ref, dy_ref, c_ref, lse_ref, dq_ref, dk_ref, dv_ref, db_ref,
                dsk_ref, dk_acc, dv_acc, dk_half, dv_half, kv_x)
        for j in range(rps):
            residue(rr, rr * rps + j if stride > 1 else 0, *refs)

    def residue(rr, row0, q_ref, k_ref, v_ref, bias_ref, sink_ref, dy_ref, c_ref, lse_ref,
                dq_ref, dk_ref, dv_ref, db_ref, dsk_ref, dk_acc, dv_acc, dk_half, dv_half, kv_x):
        dk_acc[...] = jnp.zeros_like(dk_acc)
        dv_acc[...] = jnp.zeros_like(dv_acc)
        lane_head = _lane_head(BLOCK)
        hb = 4 * rr if n_bias == 8 else 0

        def load(ref, ib):
            return _load_rows(ref, _block_rows(row0, stride, ib), split)

        def load_kv(which, ib):
            if expanded:
                return kv_x[which][_block_rows(0, 1, ib), :]
            return load((k_ref, v_ref)[which], ib).astype(BF16)

        def head_col(x):
            return jnp.concatenate([x[:, h * HEAD_DIM:h * HEAD_DIM + 1] for h in range(4)], axis=0)

        def block(ib, first):
            q4 = _stack_heads(load(q_ref, ib).astype(BF16), lane_head)
            dy4 = _stack_heads(load(dy_ref, ib).astype(BF16), lane_head)
            c4 = head_col(load(c_ref, ib))
            l4 = head_col(load(lse_ref, ib))
            if first:
                kc, vc = load_kv(0, ib), load_kv(1, ib)
                b4 = bias_ref[:, :, BLOCK:].reshape(4 * BLOCK, BLOCK)
                krows = pl.ds(0, BLOCK)
            else:
                kc = jnp.concatenate([load_kv(0, ib - 1), load_kv(0, ib)], axis=0)
                vc = jnp.concatenate([load_kv(1, ib - 1), load_kv(1, ib)], axis=0)
                b4 = bias_ref[...].reshape(4 * BLOCK, 2 * BLOCK)
                krows = pl.ds(pl.multiple_of((ib - 1) * BLOCK, BLOCK), 2 * BLOCK)
            nk = BLOCK if first else 2 * BLOCK
            p = jnp.exp(_dot_nt(q4, kc) * scale + b4 - l4)
            ds = p * (_dot_nt(dy4, vc) - c4)
            ds3 = ds.reshape(4, BLOCK, nk)
            if n_bias == 8:
                if first:
                    db_ref[pl.ds(hb, 4), :, BLOCK:] += ds3
                else:
                    db_ref[pl.ds(hb, 4)] += ds3
            elif first:
                db_ref[:, :, BLOCK:] += ds3
            else:
                db_ref[...] += ds3
            ds16 = ds.astype(BF16)
            dq = _unstack_heads(_dot(ds16, kc), lane_head) * scale
            _store_rows(dq_ref, _block_rows(row0, stride, ib), dq.astype(g_dtype), split)
            dk_acc[krows, :] += _dot_tn(ds16, q4) * scale
            dv_acc[krows, :] += _dot_tn(p.astype(BF16), dy4)
            if has_sink:
                for h in range(4):
                    hs = slice(h * BLOCK, (h + 1) * BLOCK)
                    sk = sink_ref[0, h:h + 1, 0:1]
                    val = -jnp.sum(jnp.exp(sk - l4[hs]) * c4[hs], axis=0, keepdims=True)
                    dsk_ref[hb + h] += jnp.broadcast_to(val, (8, 128))

        block(0, True)
        if nb > 1:
            def step(i, carry):
                block(i, False)
                return carry
            lax.fori_loop(1, nb, step, 0)

        if kvw == GW:
            all_rows = pl.ds(row0, ln, stride=stride) if stride > 1 else pl.ds(0, ln)
            _store_rows(dk_ref, all_rows, dk_acc[...].astype(g_dtype), split)
            _store_rows(dv_ref, all_rows, dv_acc[...].astype(g_dtype), split)
        else:
            def fold(acc):
                t2 = acc[:, :2 * HEAD_DIM] + acc[:, 2 * HEAD_DIM:]
                t2 = t2 + pltpu.roll(t2, HEAD_DIM, 1)
                lane = lax.broadcasted_iota(jnp.int32, t2.shape, 1) // HEAD_DIM
                return jnp.where(lane == rr, t2, 0.0)

            @pl.when(rr == 0)
            def _():
                dk_half[...] = fold(dk_acc[...])
                dv_half[...] = fold(dv_acc[...])

            @pl.when(rr == 1)
            def _():
                dk_ref[0] = (dk_half[...] + fold(dk_acc[...])).astype(g_dtype)
                dv_ref[0] = (dv_half[...] + fold(dv_acc[...])).astype(g_dtype)

    return pl.pallas_call(
        body, name=name, grid=grid,
        out_shape=(jax.ShapeDtypeStruct(dq_shape, g_dtype), jax.ShapeDtypeStruct(dkv_shape, g_dtype),
                   jax.ShapeDtypeStruct(dkv_shape, g_dtype), jax.ShapeDtypeStruct((n_bias, BLOCK, 2 * BLOCK), F32),
                   jax.ShapeDtypeStruct((8, 8, 128), F32)),
        in_specs=[q_spec, k_spec, v_spec,
                  pl.BlockSpec((4, BLOCK, 2 * BLOCK), bias_map), pl.BlockSpec((1, 4, 128), sink_map),
                  o_spec, o_spec, o_spec],
        out_specs=(o_spec, kv_out_spec, kv_out_spec,
                   pl.BlockSpec((n_bias, BLOCK, 2 * BLOCK), lambda n, r: (0, 0, 0)),
                   pl.BlockSpec((8, 8, 128), lambda n, r: (0, 0, 0))),
        scratch_shapes=[pltpu.VMEM((ln, GW), F32), pltpu.VMEM((ln, GW), F32),
                        pltpu.VMEM((ln, 2 * HEAD_DIM), F32), pltpu.VMEM((ln, 2 * HEAD_DIM), F32)]
        + ([pltpu.VMEM((seq, GW), BF16)] * 2 if expanded else []),
        compiler_params=_params(("arbitrary", "arbitrary"), VMEM_BIG),
    )(q_arr, k_arr, v_arr, bias, sink, dy, cc, lse)


def _bias_grad(ds_all, buckets):
    def body(ds_ref, bk_ref, o_ref):
        rows = lax.broadcasted_iota(jnp.int32, (N_BUCKETS, 128), 0)
        cols = lax.broadcasted_iota(jnp.int32, (N_BUCKETS, 128), 1)

        def per_bucket(b, acc):
            for h in range(20):
                gi = h // 4 if h < 12 else 3
                v = jnp.where(bk_ref[gi] == b, ds_ref[h], 0.0)
                v = jnp.sum(jnp.sum(v, axis=1, keepdims=True), axis=0, keepdims=True)
                acc = jnp.where((rows == b) & (cols == h), v, acc)
            return acc

        o_ref[...] = lax.fori_loop(0, N_BUCKETS, per_bucket, jnp.zeros((N_BUCKETS, 128), F32))

    vm = pl.BlockSpec(memory_space=pltpu.VMEM)
    return pl.pallas_call(body, name="bias_grad", out_shape=jax.ShapeDtypeStruct((N_BUCKETS, 128), F32),
                          in_specs=[vm, vm], out_specs=vm)(ds_all, buckets)


def _adamw(w, g, m, v, name):
    r, c = w.shape
    tr = r
    for cand in (256, 176, 128, 64, 32, 16, 8):
        if r % cand == 0:
            tr = cand
            break
    bc1 = 1.0 - ADAM_B1 ** ADAM_STEP
    bc2 = 1.0 - ADAM_B2 ** ADAM_STEP

    def body(w_ref, g_ref, m_ref, v_ref, d_ref, nm_ref, nv_ref):
        gv = g_ref[...]
        nm = ADAM_B1 * m_ref[...] + (1.0 - ADAM_B1) * gv
        nv = ADAM_B2 * v_ref[...] + (1.0 - ADAM_B2) * (gv * gv)
        nm_ref[...] = nm
        nv_ref[...] = nv
        d_ref[...] = -ADAM_LR * ((nm / bc1) / (jnp.sqrt(nv / bc2) + ADAM_EPS) + ADAM_WD * w_ref[...])

    spec = pl.BlockSpec((tr, c), lambda i: (i, 0))
    shp = jax.ShapeDtypeStruct((r, c), F32)
    return pl.pallas_call(body, name=name, grid=(r // tr,), out_shape=(shp, shp, shp),
                          in_specs=[spec] * 4, out_specs=(spec, spec, spec),
                          compiler_params=_params(("parallel",)))(w, g, m, v)


def _t5_bucket(dist):
    max_exact = N_BUCKETS // 2
    n = jnp.maximum(dist, 0)
    nf = jnp.maximum(n, 1).astype(F32)
    large = max_exact + (jnp.log(nf / max_exact) / math.log(MAX_DISTANCE / max_exact)
                         * (N_BUCKETS - max_exact)).astype(jnp.int32)
    large = jnp.minimum(large, N_BUCKETS - 1)
    return jnp.where(n < max_exact, n, large)


def _bias_tables(rel_bias):
    qi = jnp.arange(BLOCK)[:, None]
    ki = jnp.arange(2 * BLOCK)[None, :]
    dist = qi + BLOCK - ki
    specs = [(d, w // d, 4 * gi, 4 * gi + 4) for gi, (w, d) in enumerate(DIL_GROUPS)] + [(1, B_WINDOW - 1, 12, 20)]
    biases, buckets = [], []
    for stride, steps, h0, h1 in specs:
        valid = (dist >= 0) & (dist <= steps)
        bk = jnp.where(valid, _t5_bucket(dist * stride), -1).astype(jnp.int32)
        onehot = (bk[None, :, :] == jnp.arange(N_BUCKETS, dtype=jnp.int32)[:, None, None]).astype(F32)
        b = jnp.einsum("bqk,bh->hqk", onehot, rel_bias[:, h0:h1], precision=lax.Precision.HIGHEST)
        biases.append(jnp.where(valid[None], b, NEG))
        buckets.append(bk)
    return jnp.concatenate(biases, axis=0), jnp.stack(buckets, axis=0)


def _local_step(x, tgt, W, S, shards=None):
    nseq, seq, _ = x.shape
    t = nseq * seq
    xf = x.reshape(t, D_MODEL)
    bias_all, buckets = _bias_tables(S["rel_bias"])
    sink_b = jnp.broadcast_to(S["sinks"].reshape(2, 4, 1), (2, 4, 128)).astype(F32)
    sink_0 = jnp.zeros((1, 4, 128), F32)
    dist = shards is not None
    W = dict(W)
    G, GS, reduced = {}, {}, {}

    def put(keys, gathered):
        for k, g in zip(keys, gathered):
            W[k] = g.reshape(_FULL_SHAPE.get(k, (N_CHIPS * shards[k].shape[0], D_MODEL)))

    def gather_rider(keys):
        return _GatherRider([shards[k] for k in keys]) if dist else None

    def pair(keys):
        return _pair_reduce([G[k].reshape(N_CHIPS, 2, shards[k].shape[0] // 2, D_MODEL) for k in keys],
                            "grad_pair_reduce_" + keys[0])

    def finish(keys, own, rec):
        full = _final_reduce(own, rec, "grad_final_reduce_" + keys[0])
        off = 0
        for k in keys:
            r = shards[k].shape[0]
            reduced[k] = full[:, off:off + r // 2].reshape(r, D_MODEL)
            off += r // 2

    if dist:
        first = ("wgt1", "wut1", "wd1")
        put(first, _gather_rows([shards[k] for k in first]))
    keys = ("wint", "wgt2")
    (h1, n1, g1, u1), ro = _ffn_fwd(xf, S["ffn1_norm"], W["wgt1"], W["wut1"], W["wd1"], rider=gather_rider(keys))
    put(keys, ro)
    keys = ("wout", "wat", "wbt", "wut2")
    (un, za, zb, zg), ro = _inproj_fwd(h1, S["mix_norm"], W["wint"], S["b_in"], nseq, rider=gather_rider(keys))
    put(keys, ro)

    seq3 = lambda a: a.reshape(nseq, seq, a.shape[-1])
    zb3 = seq3(zb)
    pair_blk = lambda cb: pl.BlockSpec((1, 2, seq, 128), lambda n, r, cb=cb: (n, cb, 0, 0))
    a_cfg = []
    outs, lses = [], []
    for gi, (_, d) in enumerate(DIL_GROUPS):
        cfg = dict(grid=(nseq, d), seq=seq, stride=d, kvw=GW, split=True,
                   q_spec=pair_blk(gi), k_spec=pair_blk(3 + gi), v_spec=pair_blk(6 + gi), o_spec=pair_blk(0),
                   bias_map=lambda n, r: (0, 0, 0), sink_map=lambda n, r: (0, 0, 0), has_sink=False)
        a_cfg.append(cfg)
        (o, lse), _ = _attn_fwd(za, za, za, bias_all[4 * gi:4 * gi + 4], sink_0, o_shape=(nseq, 2, seq, 128),
                                o_dtype=F32, name=f"attn_a{gi}_fwd", **cfg)
        outs.append(o)
        lses.append(lse)
    wide_blk = lambda w, cmap: pl.BlockSpec((1, seq, w), cmap)
    b_cfg = dict(grid=(nseq, 2), seq=seq, stride=1, kvw=2 * HEAD_DIM, split=False,
                 q_spec=wide_blk(GW, lambda n, r: (n, 0, r)), k_spec=wide_blk(2 * HEAD_DIM, lambda n, r: (n, 0, 4)),
                 v_spec=wide_blk(2 * HEAD_DIM, lambda n, r: (n, 0, 5)), o_spec=wide_blk(GW, lambda n, r: (n, 0, r)),
                 bias_map=lambda n, r: (r, 0, 0), sink_map=lambda n, r: (r, 0, 0), has_sink=True)
    keys = ("wd2",)
    bias_b_fwd = bias_all[12:20].at[:, :, 0].set(jnp.broadcast_to(S["sinks"].reshape(8, 1), (8, BLOCK)))
    (yb, lse_b), ro = _attn_fwd(zb3, zb3, zb3, bias_b_fwd, sink_b, o_shape=(nseq, seq, 2 * GW), o_dtype=BF16,
                                name="attn_b_fwd", rider=gather_rider(keys), **b_cfg)
    put(keys, ro)
    yb = yb.reshape(t, 2 * GW)

    h2, y, lse_tot, pa, pb, merged = _merge_fwd(outs[0], outs[1], outs[2], lses[0], lses[1], lses[2], yb, zg, h1,
                                                W["wat"], W["wbt"], W["wout"])
    (h3, n2, g2, u2), _ = _ffn_fwd(h2, S["ffn2_norm"], W["wgt2"], W["wut2"], W["wd2"])
    dh3, loss_part, g_final = _loss_head(h3, S["final_norm"].reshape(1, D_MODEL), tgt.reshape(t, D_MODEL))

    GS["final_norm"] = g_final
    dh2, dg2, du2, a2, df2, GS["ffn2_norm"] = _ffn_bwd(dh3, h2, S["ffn2_norm"], g2, u2, W["wgt2"], W["wut2"], W["wd2"])
    G["wgt2"] = _wgrad(dg2, n2, D_FF, name="wgrad_gate2")
    G["wut2"] = _wgrad(du2, n2, D_FF, name="wgrad_up2")
    G["wd2"] = _wgrad(a2, df2, D_FF, name="wgrad_down2")

    keys = ("wgt2", "wut2", "wd2")
    rider = _ExchangeRider([pair(keys)]) if dist else None
    (dpa, dpb, dga, dgb, dya, dyb, dh2b, ca, cb), ro = _merge_bwd(dh2, pa, pb, zg, y, yb, W["wat"], W["wbt"], W["wout"],
                                                                  nseq, rider=rider)
    if dist:
        finish(keys, *ro)
    G["wout"] = _wgrad(merged, dh2b, D_MODEL, name="wgrad_out")
    G["wat"] = _wgrad(dpa, y, D_MODEL, name="wgrad_branch_a")
    G["wbt"] = _wgrad(dpb, yb, D_MODEL, name="wgrad_branch_b")

    dqs, dks, dvs, dbs = [], [], [], []
    shp = (nseq, 2, seq, 128)
    halves = lambda a: [a[:, hf].reshape(t, 128).astype(BF16) for hf in range(2)]
    for gi in range(len(DIL_GROUPS)):
        dq, dk, dv, db, _ = _attn_bwd(za, za, za, bias_all[4 * gi:4 * gi + 4], sink_0, dya, ca, lse_tot,
                                      n_bias=4, dq_shape=shp, dkv_shape=shp, g_dtype=F32,
                                      kv_out_spec=a_cfg[gi]["o_spec"], name=f"attn_a{gi}_bwd", **a_cfg[gi])
        dqs += halves(dq)
        dks += halves(dk)
        dvs += halves(dv)
        dbs.append(db)
    dqb, dkb, dvb, dbb, dsink = _attn_bwd(zb3, zb3, zb3, bias_all[12:20], sink_b, seq3(dyb), seq3(cb), lse_b,
                                          n_bias=8, dq_shape=(nseq, seq, 2 * GW),
                                          dkv_shape=(nseq, seq, 2 * HEAD_DIM), g_dtype=BF16,
                                          kv_out_spec=wide_blk(2 * HEAD_DIM, lambda n, r: (n, 0, 0)),
                                          name="attn_b_bwd", **b_cfg)
    dz = jnp.concatenate(dqs + dks + dvs + [dqb.reshape(t, 2 * GW), dkb.reshape(t, 2 * HEAD_DIM),
                                            dvb.reshape(t, 2 * HEAD_DIM), dga, dgb], axis=-1)
    gb_tab = _bias_grad(jnp.concatenate(dbs + [dbb], axis=0), buckets)
    GS["rel_bias"] = gb_tab[:, :20]
    GS["sinks"] = dsink[:, 0, 0].reshape(1, 8)

    G["wint"], GS["b_in"] = _wgrad(dz, un, D_IN // 2, with_colsum=True, name="wgrad_in")
    keys = ("wint", "wout", "wat", "wbt")
    rider = _ExchangeRider([pair(keys)]) if dist else None
    (dh1, GS["mix_norm"]), ro = _inproj_bwd(dz, dh2, h1, S["mix_norm"], W["wint"], rider=rider)
    if dist:
        finish(keys, *ro)

    dx, dg1, du1, a1, df1, GS["ffn1_norm"] = _ffn_bwd(dh1, xf, S["ffn1_norm"], g1, u1, W["wgt1"], W["wut1"], W["wd1"])
    G["wgt1"] = _wgrad(dg1, n1, D_FF, name="wgrad_gate1")
    if dist:
        G["wut1"], ro = _wgrad(du1, n1, D_FF, name="wgrad_up1", rider=_ExchangeRider([pair(("wgt1",))]))
        finish(("wgt1",), *ro)
        G["wd1"], ro = _wgrad(a1, df1, D_FF, name="wgrad_down1", rider=_ExchangeRider([pair(("wut1",))]))
        finish(("wut1",), *ro)
        finish(("wd1",), *_chip_exchange([pair(("wd1",))]))
    else:
        G["wut1"] = _wgrad(du1, n1, D_FF, name="wgrad_up1")
        G["wd1"] = _wgrad(a1, df1, D_FF, name="wgrad_down1")
    return loss_part, dx.reshape(x.shape), (reduced if dist else G), GS


_SMALL = ("ffn1_norm", "mix_norm", "ffn2_norm", "final_norm", "b_in", "sinks", "rel_bias")
_ORDER = ("ffn1_norm", "ffn1_w_gate", "ffn1_w_up", "ffn1_w_down", "mix_norm", "w_in", "b_in", "w_branch_a",
          "w_branch_b", "w_out", "sinks", "rel_bias", "ffn2_norm", "ffn2_w_gate", "ffn2_w_up", "ffn2_w_down",
          "final_norm")
_BIG = (("wgt1", "ffn1_w_gate", True, 704), ("wut1", "ffn1_w_up", True, 704), ("wd1", "ffn1_w_down", False, 704),
        ("wint", "w_in", True, 1280), ("wout", "w_out", False, 256), ("wat", "w_branch_a", True, 64),
        ("wbt", "w_branch_b", True, 128), ("wgt2", "ffn2_w_gate", True, 704), ("wut2", "ffn2_w_up", True, 704),
        ("wd2", "ffn2_w_down", False, 704))
_FULL_SHAPE = {"wat": (D_MODEL, GW), "wbt": (D_MODEL, 2 * GW)}


def _pack_small(p, extra=None):
    last = [p["sinks"].reshape(8), p["rel_bias"].reshape(640)]
    used = 648
    if extra is not None:
        last.append(extra.reshape(1))
        used += 1
    last.append(jnp.zeros((D_MODEL - used,), F32))
    rows = [p["ffn1_norm"].reshape(1, D_MODEL), p["mix_norm"].reshape(1, D_MODEL), p["ffn2_norm"].reshape(1, D_MODEL),
            p["final_norm"].reshape(1, D_MODEL), p["b_in"].reshape(5, D_MODEL), jnp.concatenate(last).reshape(1, D_MODEL),
            jnp.zeros((6, D_MODEL), F32)]
    return jnp.concatenate(rows, axis=0)


def _unpack_small(a):
    return {"ffn1_norm": a[0:1], "mix_norm": a[1:2], "ffn2_norm": a[2:3], "final_norm": a[3],
            "b_in": a[4:9].reshape(1, D_IN), "sinks": a[9, 0:8].reshape(1, 8), "rel_bias": a[9, 8:648].reshape(32, 20)}


def kernel(x, ffn1_norm, ffn1_w_gate, ffn1_w_up, ffn1_w_down, mix_norm, w_in, b_in, w_branch_a, w_branch_b, w_out, sinks, rel_bias, ffn2_norm, ffn2_w_gate, ffn2_w_up, ffn2_w_down, final_norm, loss_target, m_ffn1_norm, m_ffn1_w_gate, m_ffn1_w_up, m_ffn1_w_down, m_mix_norm, m_w_in, m_b_in, m_w_branch_a, m_w_branch_b, m_w_out, m_sinks, m_rel_bias, m_ffn2_norm, m_ffn2_w_gate, m_ffn2_w_up, m_ffn2_w_down, m_final_norm, v_ffn1_norm, v_ffn1_w_gate, v_ffn1_w_up, v_ffn1_w_down, v_mix_norm, v_w_in, v_b_in, v_w_branch_a, v_w_branch_b, v_w_out, v_sinks, v_rel_bias, v_ffn2_norm, v_ffn2_w_gate, v_ffn2_w_up, v_ffn2_w_down, v_final_norm):
    args = dict(locals())
    w = {n: args[n] for n in _ORDER}
    m = {n: args["m_" + n] for n in _ORDER}
    v = {n: args["v_" + n] for n in _ORDER}

    shards = {}
    for key, name, transposed, rows in _BIG:
        a = w[name][0]
        a = (a.T if transposed else a).astype(BF16)
        shards[key] = a.reshape(rows, D_MODEL)
    S = {n: w[n] for n in _SMALL}

    loss_part, grad_x, reduced, GS = _local_step(x, loss_target, {}, S, shards)

    grads = {}
    for key, name, transposed, rows in _BIG:
        nat = w[name][0].shape
        grads[name] = reduced[key].reshape(nat[1], nat[0]).T if transposed else reduced[key].reshape(nat)

    small = _allreduce_small(_pack_small(GS, extra=loss_part[0, 0]))
    loss = small[9, 648]

    out_g, out_d, out_m, out_v = {}, {}, {}, {}
    for _, n, _, _ in _BIG:
        d_, nm_, nv_ = _adamw(w[n][0], grads[n], m[n][0], v[n][0], "adamw_" + n)
        out_g[n], out_d[n], out_m[n], out_v[n] = grads[n][None], d_[None], nm_[None], nv_[None]
    d_s, m_s, v_s = _adamw(_pack_small(w), small, _pack_small(m), _pack_small(v), "adamw_small")
    for dst, src in ((out_g, small), (out_d, d_s), (out_m, m_s), (out_v, v_s)):
        dst.update(_unpack_small(src))

    return (loss, grad_x, *[out_g[n] for n in _ORDER], *[out_d[n] for n in _ORDER],
            *[out_m[n] for n in _ORDER], *[out_v[n] for n in _ORDER])
```

```python
import math

import jax
import jax.numpy as jnp
from jax import lax
from jax.experimental import pallas as pl
from jax.experimental.pallas import tpu as pltpu

F32, BF16 = jnp.float32, jnp.bfloat16
MESH = pl.DeviceIdType.MESH

D_MODEL = 1024
D_FF = 2816
D_IN = 5120
HEAD_DIM = 64
BLOCK = 128
DIL_GROUPS = ((128, 1), (512, 4), (2048, 16))
B_WINDOW = 128
N_BUCKETS = 32
MAX_DISTANCE = 2048
EPS = 1e-6
N_CHIPS = 4
GW = 256
ZA_W = 2304
ZB_W = 768
NEG = -1e30

ADAM_LR, ADAM_B1, ADAM_B2, ADAM_EPS, ADAM_WD, ADAM_STEP = 0.001, 0.9, 0.999, 1e-08, 0.01, 10

VMEM_BIG = 56 * 1024 * 1024
TM = 512
TM_BWD = 256
FF_CHUNKS = 2
DMA_SPLIT = 8
RESIDUES_PER_STEP = 4


def _dot(a, b):
    return jnp.dot(a, b, preferred_element_type=F32)


def _dot_nt(a, b):
    return lax.dot_general(a, b, (((1,), (1,)), ((), ())), preferred_element_type=F32)


def _dot_tn(a, b):
    return lax.dot_general(a, b, (((0,), (0,)), ((), ())), preferred_element_type=F32)


def _sigmoid(x):
    return 1.0 / (1.0 + jnp.exp(-x))


def _params(sem, vmem=None):
    return pltpu.CompilerParams(dimension_semantics=sem, vmem_limit_bytes=vmem)


ANY = pl.BlockSpec(memory_space=pl.ANY)


def _me():
    return lax.axis_index("x"), lax.axis_index("y"), lax.axis_index("c")


_CHIP_RELS = ((1, 0), (0, 1), (1, 1))


def _flip(v, f):
    return 1 - v if f else v


def _remote(src, dst, ssem, rsem, peer):
    return pltpu.make_async_remote_copy(src_ref=src, dst_ref=dst, send_sem=ssem, recv_sem=rsem,
                                        device_id=peer, device_id_type=MESH)


def _row_pieces(rows, n):
    step = max(16, -(-rows // n) // 16 * 16)
    out, s = [], 0
    while s < rows:
        out.append((s, min(step, rows - s)))
        s += step
    return out


def _gather_rows(shards):
    nt = len(shards)
    rows = [s.shape[0] for s in shards]

    def body(*refs):
        srcs, outs = refs[:nt], refs[nt:2 * nt]
        ici_s, ici_r, d2d_s, d2d_r, loc = refs[2 * nt:]
        x, y, c = _me()
        j = 2 * x + y
        sib = (x, y, 1 - c)
        local = [pltpu.make_async_copy(srcs[t], outs[t].at[j], loc.at[t]) for t in range(nt)]
        for cp in local:
            cp.start()
        sends = []
        for k, (fx, fy) in enumerate(_CHIP_RELS):
            peer = (_flip(x, fx), _flip(y, fy), c)
            for t in range(nt):
                half = pl.ds(c * (rows[t] // 2), rows[t] // 2)
                cp = _remote(srcs[t].at[half], outs[t].at[j, half], ici_s.at[3 * t + k], ici_r.at[3 * t + k], peer)
                cp.start()
                sends.append(cp)
        fwds = []
        for k, (fx, fy) in enumerate(_CHIP_RELS):
            pj = 2 * _flip(x, fx) + _flip(y, fy)
            for t in range(nt):
                half = pl.ds(c * (rows[t] // 2), rows[t] // 2)
                blk = outs[t].at[pj, half]
                _remote(blk, blk, ici_s.at[3 * t + k], ici_r.at[3 * t + k], sib).wait_recv()
                cp = _remote(blk, blk, d2d_s.at[3 * t + k], d2d_r.at[3 * t + k], sib)
                cp.start()
                fwds.append(cp)
        for cp in fwds:
            cp.wait()
        for cp in sends:
            cp.wait_send()
        for cp in local:
            cp.wait()

    sems = [pltpu.SemaphoreType.DMA((3 * nt,)) for _ in range(4)] + [pltpu.SemaphoreType.DMA((nt,))]
    return pl.pallas_call(
        body, name="gather_weights",
        out_shape=tuple(jax.ShapeDtypeStruct((N_CHIPS,) + s.shape, s.dtype) for s in shards),
        in_specs=[ANY] * nt, out_specs=tuple([ANY] * nt), scratch_shapes=sems,
    )(*shards)


VMEM_WHOLE = pl.BlockSpec(memory_space=pltpu.VMEM)


def _pair_reduce(grads, name):
    nt = len(grads)
    r2 = [g.shape[2] for g in grads]
    off = [sum(r2[:t]) for t in range(nt)]
    tot = sum(r2)

    def body(*refs):
        gs = refs[:nt]
        s_ref, got, ssem, rsem = refs[nt:]
        x, y, c = _me()
        sib = (x, y, 1 - c)
        for t in range(nt):
            for k in range(N_CHIPS):
                _remote(gs[t].at[k, 1 - c], got.at[k, pl.ds(off[t], r2[t])], ssem, rsem, sib).start()
        _remote(got, got, ssem, rsem, sib).wait()
        for t in range(nt):
            for k in range(N_CHIPS):
                rows = slice(off[t], off[t] + r2[t])
                s_ref[k, rows, :] = (gs[t][k, c].astype(F32) + got[k, rows, :].astype(F32)).astype(BF16)

    shp = jax.ShapeDtypeStruct((N_CHIPS, tot, D_MODEL), BF16)
    return pl.pallas_call(
        body, name=name, out_shape=shp, in_specs=[VMEM_WHOLE] * nt, out_specs=VMEM_WHOLE,
        scratch_shapes=[pltpu.VMEM((N_CHIPS, tot, D_MODEL), BF16), pltpu.SemaphoreType.DMA(()),
                        pltpu.SemaphoreType.DMA(())],
        compiler_params=pltpu.CompilerParams(vmem_limit_bytes=VMEM_BIG),
    )(*grads)


def _chip_exchange(parts):
    ng = len(parts)
    r2 = [p.shape[1] for p in parts]
    off = [sum(r2[:g]) for g in range(ng)]
    tot = sum(r2)

    def body(*refs):
        ps = refs[:ng]
        own_ref, rec_ref, ssems, rsems, lsem = refs[ng:]
        x, y, c = _me()
        j = 2 * x + y
        for g in range(ng):
            pltpu.make_async_copy(ps[g].at[j], own_ref.at[pl.ds(off[g], r2[g])], lsem).start()
        for k, (fx, fy) in enumerate(_CHIP_RELS):
            px, py = _flip(x, fx), _flip(y, fy)
            for g in range(ng):
                for st, sz in _row_pieces(r2[g], 2):
                    _remote(ps[g].at[2 * px + py, pl.ds(st, sz)], rec_ref.at[k, pl.ds(off[g] + st, sz)],
                            ssems.at[k], rsems.at[k], (px, py, c)).start()
        for k in range(3):
            _remote(rec_ref.at[k], rec_ref.at[k], ssems.at[k], rsems.at[k], (x, y, c)).wait()
        pltpu.make_async_copy(own_ref, own_ref, lsem).wait()

    return pl.pallas_call(
        body, name="grad_chip_exchange",
        out_shape=(jax.ShapeDtypeStruct((tot, D_MODEL), BF16), jax.ShapeDtypeStruct((3, tot, D_MODEL), BF16)),
        in_specs=[ANY] * ng, out_specs=(ANY, ANY),
        scratch_shapes=[pltpu.SemaphoreType.DMA((3,)), pltpu.SemaphoreType.DMA((3,)), pltpu.SemaphoreType.DMA(())],
    )(*parts)


def _final_reduce(own, rec, name):
    r2 = own.shape[0]
    pieces = _row_pieces(r2, DMA_SPLIT)

    def body(own_ref, rec_ref, o_ref, fbuf, ssem, rsem, lsem):
        x, y, c = _me()
        sib = (x, y, 1 - c)
        for st, sz in pieces:
            rows = slice(st, st + sz)
            fbuf[rows, :] = (own_ref[rows, :].astype(F32) + rec_ref[0, rows, :].astype(F32)
                             + rec_ref[1, rows, :].astype(F32) + rec_ref[2, rows, :].astype(F32))
            pltpu.make_async_copy(fbuf.at[pl.ds(st, sz)], o_ref.at[c, pl.ds(st, sz)], lsem).start()
            _remote(fbuf.at[pl.ds(st, sz)], o_ref.at[c, pl.ds(st, sz)], ssem, rsem, sib).start()
        _remote(fbuf, o_ref.at[c], ssem, rsem, sib).wait()
        pltpu.make_async_copy(fbuf, o_ref.at[c], lsem).wait()

    return pl.pallas_call(
        body, name=name, out_shape=jax.ShapeDtypeStruct((2, r2, D_MODEL), F32),
        in_specs=[VMEM_WHOLE, VMEM_WHOLE], out_specs=ANY,
        scratch_shapes=[pltpu.VMEM((r2, D_MODEL), F32), pltpu.SemaphoreType.DMA(()), pltpu.SemaphoreType.DMA(()),
                        pltpu.SemaphoreType.DMA(())],
        compiler_params=pltpu.CompilerParams(vmem_limit_bytes=VMEM_BIG),
    )(own, rec)


def _allreduce_small(vec):
    def body(v_ref, o_ref, buf, send_sems, recv_sems):
        x, y, c = _me()
        me = 4 * x + 2 * y + c
        buf[me] = v_ref[...]
        copies = []
        for k in range(1, 8):
            peer = (_flip(x, (k >> 2) & 1), _flip(y, (k >> 1) & 1), _flip(c, k & 1))
            cp = _remote(v_ref, buf.at[me], send_sems.at[k - 1], recv_sems.at[k - 1], peer)
            cp.start()
            copies.append(cp)
        for cp in copies:
            cp.wait()
        acc = buf[0]
        for i in range(1, 8):
            acc = acc + buf[i]
        o_ref[...] = acc

    vm = pl.BlockSpec(memory_space=pltpu.VMEM)
    return pl.pallas_call(
        body, name="allreduce_small", out_shape=jax.ShapeDtypeStruct(vec.shape, vec.dtype),
        in_specs=[vm], out_specs=vm,
        scratch_shapes=[pltpu.VMEM((8,) + vec.shape, vec.dtype), pltpu.SemaphoreType.DMA((7,)),
                        pltpu.SemaphoreType.DMA((7,))],
    )(vec)


class _GatherRider:
    def __init__(self, shards):
        self.inputs = list(shards)
        nt = len(shards)
        self.out_shape = [jax.ShapeDtypeStruct((N_CHIPS,) + s.shape, s.dtype) for s in shards]
        self.scratch = [pltpu.SemaphoreType.DMA((3 * nt,)), pltpu.SemaphoreType.DMA((3 * nt,)),
                        pltpu.SemaphoreType.DMA((nt,))]

    def _copies(self, srcs, outs, sems):
        ici_s, ici_r, loc = sems
        x, y, c = _me()
        j = 2 * x + y
        local = [pltpu.make_async_copy(srcs[t], outs[t].at[j], loc.at[t]) for t in range(len(srcs))]
        remote = []
        for k, (fx, fy) in enumerate(_CHIP_RELS):
            peer = (_flip(x, fx), _flip(y, fy), c)
            for t in range(len(srcs)):
                remote.append(_remote(srcs[t], outs[t].at[j], ici_s.at[3 * t + k], ici_r.at[3 * t + k], peer))
        return local, remote

    def start(self, srcs, outs, sems):
        local, remote = self._copies(srcs, outs, sems)
        for cp in local + remote:
            cp.start()

    def finish(self, srcs, outs, sems):
        local, remote = self._copies(srcs, outs, sems)
        for cp in remote + local:
            cp.wait()


class _ExchangeRider:
    def __init__(self, parts):
        self.inputs = list(parts)
        self.r2 = [p.shape[1] for p in parts]
        self.off = [sum(self.r2[:g]) for g in range(len(parts))]
        tot = sum(self.r2)
        self.out_shape = [jax.ShapeDtypeStruct((tot, D_MODEL), BF16), jax.ShapeDtypeStruct((3, tot, D_MODEL), BF16)]
        self.scratch = [pltpu.SemaphoreType.DMA((3,)), pltpu.SemaphoreType.DMA((3,)), pltpu.SemaphoreType.DMA(())]

    def start(self, ps, outs, sems):
        own_ref, rec_ref = outs
        ssems, rsems, lsem = sems
        x, y, c = _me()
        j = 2 * x + y
        for g in range(len(ps)):
            pltpu.make_async_copy(ps[g].at[j], own_ref.at[pl.ds(self.off[g], self.r2[g])], lsem).start()
        for k, (fx, fy) in enumerate(_CHIP_RELS):
            px, py = _flip(x, fx), _flip(y, fy)
            for g in range(len(ps)):
                for st, sz in _row_pieces(self.r2[g], 2):
                    _remote(ps[g].at[2 * px + py, pl.ds(st, sz)], rec_ref.at[k, pl.ds(self.off[g] + st, sz)],
                            ssems.at[k], rsems.at[k], (px, py, c)).start()

    def finish(self, ps, outs, sems):
        own_ref, rec_ref = outs
        ssems, rsems, lsem = sems
        x, y, c = _me()
        for k in range(3):
            _remote(rec_ref.at[k], rec_ref.at[k], ssems.at[k], rsems.at[k], (x, y, c)).wait()
        pltpu.make_async_copy(own_ref, own_ref, lsem).wait()


def _pallas(body, args, *, name, grid, in_specs, out_specs, out_shape, scratch_shapes=(), sem=None, vmem=None,
            rider=None):
    if rider is None:
        res = pl.pallas_call(body, name=name, grid=grid, in_specs=list(in_specs), out_specs=tuple(out_specs),
                             out_shape=tuple(out_shape), scratch_shapes=list(scratch_shapes),
                             compiler_params=_params(sem, vmem))(*args)
        return tuple(res), ()
    n_in, n_out, n_sc = len(in_specs), len(out_shape), len(scratch_shapes)
    r_in, r_out = len(rider.inputs), len(rider.out_shape)

    def wrapped(*refs):
        ins, rins = refs[:n_in], refs[n_in:n_in + r_in]
        p = n_in + r_in
        outs, routs = refs[p:p + n_out], refs[p + n_out:p + n_out + r_out]
        p += n_out + r_out
        scr, rsems = refs[p:p + n_sc], refs[p + n_sc:]
        first = pl.program_id(0) == 0
        last = pl.program_id(0) == grid[0] - 1
        for a in range(1, len(grid)):
            first = first & (pl.program_id(a) == 0)
            last = last & (pl.program_id(a) == grid[a] - 1)

        @pl.when(first)
        def _():
            rider.start(rins, routs, rsems)

        body(*ins, *outs, *scr)

        @pl.when(last)
        def _():
            rider.finish(rins, routs, rsems)

    res = pl.pallas_call(
        wrapped, name=name, grid=grid, in_specs=list(in_specs) + [ANY] * r_in,
        out_specs=tuple(out_specs) + (ANY,) * r_out, out_shape=tuple(out_shape) + tuple(rider.out_shape),
        scratch_shapes=list(scratch_shapes) + rider.scratch,
        compiler_params=_params(("arbitrary",) * len(grid), vmem))(*args, *rider.inputs)
    return tuple(res[:n_out]), tuple(res[n_out:])


def _ffn_fwd(h, gain, wgt, wut, wd, rider=None):
    t = h.shape[0]
    fc = D_FF // FF_CHUNKS

    def body(h_ref, gain_ref, wg_hbm, wu_hbm, wd_hbm, hout_ref, n_ref, g_ref, u_ref, wg_v, wu_v, wd_v):
        @pl.when(pl.program_id(0) == 0)
        def _():
            pltpu.sync_copy(wg_hbm, wg_v)
            pltpu.sync_copy(wu_hbm, wu_v)
            pltpu.sync_copy(wd_hbm, wd_v)

        hh = h_ref[...]
        r = lax.rsqrt(jnp.mean(hh * hh, axis=-1, keepdims=True) + EPS)
        n = (hh * r * gain_ref[...]).astype(BF16)
        n_ref[...] = n
        acc = jnp.zeros((TM, D_MODEL), F32)
        for ci in range(FF_CHUNKS):
            sl = slice(ci * fc, (ci + 1) * fc)
            g = _dot_nt(n, wg_v[sl, :])
            u = _dot_nt(n, wu_v[sl, :])
            g_ref[:, sl] = g.astype(BF16)
            u_ref[:, sl] = u.astype(BF16)
            a = (g * _sigmoid(g) * u).astype(BF16)
            acc = acc + _dot(a, wd_v[sl, :])
        hout_ref[...] = hh + 0.5 * acc

    row = lambda w: pl.BlockSpec((TM, w), lambda i: (i, 0))
    wv = pltpu.VMEM((D_FF, D_MODEL), BF16)
    return _pallas(
        body, (h, gain, wgt, wut, wd), name="ffn_fwd", grid=(t // TM,),
        out_shape=(jax.ShapeDtypeStruct((t, D_MODEL), F32), jax.ShapeDtypeStruct((t, D_MODEL), BF16),
                   jax.ShapeDtypeStruct((t, D_FF), BF16), jax.ShapeDtypeStruct((t, D_FF), BF16)),
        in_specs=[row(D_MODEL), pl.BlockSpec((1, D_MODEL), lambda i: (0, 0)), ANY, ANY, ANY],
        out_specs=(row(D_MODEL), row(D_MODEL), row(D_FF), row(D_FF)),
        scratch_shapes=[wv, wv, wv], sem=("arbitrary",), vmem=VMEM_BIG, rider=rider)


def _ffn_bwd(dhout, h, gain, g, u, wgt, wut, wd):
    t = h.shape[0]
    tm = TM_BWD
    fc = D_FF // FF_CHUNKS

    def body(dho_ref, h_ref, gain_ref, g_ref, u_ref, wg_hbm, wu_hbm, wd_hbm,
             dh_ref, dg_ref, du_ref, a_ref, df_ref, gg_ref, wg_v, wu_v, wd_v):
        @pl.when(pl.program_id(0) == 0)
        def _():
            pltpu.sync_copy(wg_hbm, wg_v)
            pltpu.sync_copy(wu_hbm, wu_v)
            pltpu.sync_copy(wd_hbm, wd_v)
            gg_ref[...] = jnp.zeros_like(gg_ref)

        dho = dho_ref[...]
        df = (0.5 * dho).astype(BF16)
        df_ref[...] = df
        dn = jnp.zeros((tm, D_MODEL), F32)
        for ci in range(FF_CHUNKS):
            sl = slice(ci * fc, (ci + 1) * fc)
            da = _dot_nt(df, wd_v[sl, :])
            gv = g_ref[:, sl].astype(F32)
            uv = u_ref[:, sl].astype(F32)
            sg = _sigmoid(gv)
            silu = gv * sg
            dg = (da * uv * (sg * (1.0 + gv * (1.0 - sg)))).astype(BF16)
            du = (da * silu).astype(BF16)
            dg_ref[:, sl] = dg
            du_ref[:, sl] = du
            a_ref[:, sl] = (silu * uv).astype(BF16)
            dn = dn + _dot(dg, wg_v[sl, :]) + _dot(du, wu_v[sl, :])
        hh = h_ref[...]
        r = lax.rsqrt(jnp.mean(hh * hh, axis=-1, keepdims=True) + EPS)
        hn = hh * r
        gg_ref[...] += jnp.sum(dn * hn, axis=0, keepdims=True)
        dng = dn * gain_ref[...]
        dh_ref[...] = dho + r * (dng - hn * jnp.mean(dng * hn, axis=-1, keepdims=True))

    row = lambda w: pl.BlockSpec((tm, w), lambda i: (i, 0))
    vec = pl.BlockSpec((1, D_MODEL), lambda i: (0, 0))
    wv = pltpu.VMEM((D_FF, D_MODEL), BF16)
    return pl.pallas_call(
        body, name="ffn_bwd", grid=(t // tm,),
        out_shape=(jax.ShapeDtypeStruct((t, D_MODEL), F32), jax.ShapeDtypeStruct((t, D_FF), BF16),
                   jax.ShapeDtypeStruct((t, D_FF), BF16), jax.ShapeDtypeStruct((t, D_FF), BF16),
                   jax.ShapeDtypeStruct((t, D_MODEL), BF16), jax.ShapeDtypeStruct((1, D_MODEL), F32)),
        in_specs=[row(D_MODEL), row(D_MODEL), vec, row(D_FF), row(D_FF), ANY, ANY, ANY],
        out_specs=(row(D_MODEL), row(D_FF), row(D_FF), row(D_FF), row(D_MODEL), vec),
        scratch_shapes=[wv, wv, wv],
        compiler_params=_params(("arbitrary",), VMEM_BIG),
    )(dhout, h, gain, g, u, wgt, wut, wd)


def _wgrad(lhs, rhs, rb, with_colsum=False, name="wgrad", rider=None):
    t, k = lhs.shape
    n = rhs.shape[1]
    tk = 512
    nt = t // tk

    def body(l_ref, r_ref, o_ref, *rest):
        acc = rest[-1]
        ti = pl.program_id(1)

        @pl.when(ti == 0)
        def _():
            acc[...] = jnp.zeros_like(acc)
            if with_colsum:
                rest[0][...] = jnp.zeros_like(rest[0])

        acc[...] += _dot_tn(l_ref[...], r_ref[...])
        if with_colsum:
            rest[0][...] += jnp.sum(l_ref[...].astype(F32), axis=0, keepdims=True)

        @pl.when(ti == nt - 1)
        def _():
            o_ref[...] = acc[...].astype(BF16)

    out_shape = [jax.ShapeDtypeStruct((k, n), BF16)]
    out_specs = [pl.BlockSpec((rb, n), lambda j, i: (j, 0))]
    if with_colsum:
        out_shape.append(jax.ShapeDtypeStruct((1, k), F32))
        out_specs.append(pl.BlockSpec((1, rb), lambda j, i: (0, j)))
    res, ro = _pallas(
        body, (lhs, rhs), name=name, grid=(k // rb, nt), out_shape=tuple(out_shape),
        in_specs=[pl.BlockSpec((tk, rb), lambda j, i: (i, j)), pl.BlockSpec((tk, n), lambda j, i: (i, 0))],
        out_specs=tuple(out_specs), scratch_shapes=[pltpu.VMEM((rb, n), F32)],
        sem=("arbitrary", "arbitrary"), vmem=VMEM_BIG, rider=rider)
    if rider is not None:
        return res[0], ro
    return res if with_colsum else res[0]


def _lane_blocks(nseq, seq, nblk, tm=TM):
    spt = seq // tm
    return pl.BlockSpec((1, nblk, tm, 128), lambda i: (i // spt, 0, i % spt, 0))


def _inproj_fwd(h, gain, wint, b_in, nseq, rider=None):
    t = h.shape[0]
    seq = t // nseq
    half_a = ZA_W // 2
    pieces = ((0, half_a, 0, 0), (half_a, half_a, 0, half_a), (ZA_W, ZB_W, 1, 0), (ZA_W + ZB_W, 1024, 2, 0),
              (ZA_W + ZB_W + 1024, 1024, 2, 1024))

    def body(h_ref, gain_ref, w_hbm, b_ref, u_ref, za_ref, zb_ref, zg_ref, w_v):
        @pl.when(pl.program_id(0) == 0)
        def _():
            pltpu.sync_copy(w_hbm, w_v)

        hh = h_ref[...]
        r = lax.rsqrt(jnp.mean(hh * hh, axis=-1, keepdims=True) + EPS)
        un = (hh * r * gain_ref[...]).astype(BF16)
        u_ref[...] = un
        outs = (None, zb_ref, zg_ref)
        for c0, cw, oi, o0 in pieces:
            val = _dot_nt(un, w_v[c0:c0 + cw, :]) + b_ref[:, c0:c0 + cw]
            if oi == 0:
                for cb in range(cw // 128):
                    za_ref[0, o0 // 128 + cb] = val[:, cb * 128:(cb + 1) * 128]
            else:
                outs[oi][:, o0:o0 + cw] = val.astype(BF16)

    row = lambda w: pl.BlockSpec((TM, w), lambda i: (i, 0))
    return _pallas(
        body, (h, gain, wint, b_in), name="inproj_fwd", grid=(t // TM,),
        out_shape=(jax.ShapeDtypeStruct((t, D_MODEL), BF16), jax.ShapeDtypeStruct((nseq, ZA_W // 128, seq, 128), F32),
                   jax.ShapeDtypeStruct((t, ZB_W), BF16), jax.ShapeDtypeStruct((t, 2 * D_MODEL), BF16)),
        in_specs=[row(D_MODEL), pl.BlockSpec((1, D_MODEL), lambda i: (0, 0)), ANY,
                  pl.BlockSpec((1, D_IN), lambda i: (0, 0))],
        out_specs=(row(D_MODEL), _lane_blocks(nseq, seq, ZA_W // 128), row(ZB_W), row(2 * D_MODEL)),
        scratch_shapes=[pltpu.VMEM((D_IN, D_MODEL), BF16)], sem=("arbitrary",), vmem=VMEM_BIG, rider=rider)


def _inproj_bwd(dz, dh2, h, gain, wint, rider=None):
    t = h.shape[0]
    nc = 5
    cw = D_IN // nc

    def body(dz_ref, dh2_ref, h_ref, gain_ref, w_hbm, dh_ref, gg_ref, w_v):
        @pl.when(pl.program_id(0) == 0)
        def _():
            pltpu.sync_copy(w_hbm, w_v)
            gg_ref[...] = jnp.zeros_like(gg_ref)

        du = jnp.zeros((TM, D_MODEL), F32)
        for ci in range(nc):
            sl = slice(ci * cw, (ci + 1) * cw)
            du = du + _dot(dz_ref[:, sl], w_v[sl, :])
        hh = h_ref[...]
        r = lax.rsqrt(jnp.mean(hh * hh, axis=-1, keepdims=True) + EPS)
        hn = hh * r
        gg_ref[...] += jnp.sum(du * hn, axis=0, keepdims=True)
        dng = du * gain_ref[...]
        dh_ref[...] = dh2_ref[...] + r * (dng - hn * jnp.mean(dng * hn, axis=-1, keepdims=True))

    row = lambda w: pl.BlockSpec((TM, w), lambda i: (i, 0))
    vec = pl.BlockSpec((1, D_MODEL), lambda i: (0, 0))
    return _pallas(
        body, (dz, dh2, h, gain, wint), name="inproj_bwd", grid=(t // TM,),
        out_shape=(jax.ShapeDtypeStruct((t, D_MODEL), F32), jax.ShapeDtypeStruct((1, D_MODEL), F32)),
        in_specs=[row(D_IN), row(D_MODEL), row(D_MODEL), vec, ANY],
        out_specs=(row(D_MODEL), vec),
        scratch_shapes=[pltpu.VMEM((D_IN, D_MODEL), BF16)], sem=("arbitrary",), vmem=VMEM_BIG, rider=rider)


def _head_sums(x):
    w = x.shape[1]
    i = lax.broadcasted_iota(jnp.int32, (w, w), 0) // HEAD_DIM
    j = lax.broadcasted_iota(jnp.int32, (w, w), 1) // HEAD_DIM
    ones = (i == j).astype(BF16)
    hi = x.astype(BF16)
    r1 = x - hi.astype(F32)
    mid = r1.astype(BF16)
    lo = (r1 - mid.astype(F32)).astype(BF16)
    return _dot(hi, ones) + _dot(mid, ones) + _dot(lo, ones)


def _merge_fwd(o0, o1, o2, l0, l1, l2, yb, zg, h1, wat, wbt, wout):
    t = h1.shape[0]
    nseq, _, seq, _ = o0.shape

    def body(o0_ref, o1_ref, o2_ref, l0_ref, l1_ref, l2_ref, yb_ref, ga_ref, gb_ref, h1_ref, wa_ref, wb_ref, wo_ref,
             h2_ref, y_ref, lt_ref, pa_ref, pb_ref, mg_ref):
        wide = lambda ref: jnp.concatenate([ref[0, 0], ref[0, 1]], axis=1)
        la, lb, lc = wide(l0_ref), wide(l1_ref), wide(l2_ref)
        mx = jnp.maximum(jnp.maximum(la, lb), lc)
        ea, eb, ec = jnp.exp(la - mx), jnp.exp(lb - mx), jnp.exp(lc - mx)
        den = ea + eb + ec
        y = (ea * wide(o0_ref) + eb * wide(o1_ref) + ec * wide(o2_ref)) / den
        lt = mx + jnp.log(den)
        lt_ref[0, 0] = lt[:, :128]
        lt_ref[0, 1] = lt[:, 128:]
        yb16 = y.astype(BF16)
        y_ref[...] = yb16
        pa = _dot_nt(yb16, wa_ref[...])
        pb = _dot_nt(yb_ref[...], wb_ref[...])
        pa_ref[...] = pa.astype(BF16)
        pb_ref[...] = pb.astype(BF16)
        mg = (_sigmoid(ga_ref[...].astype(F32)) * pa + _sigmoid(gb_ref[...].astype(F32)) * pb).astype(BF16)
        mg_ref[...] = mg
        h2_ref[...] = h1_ref[...] + _dot(mg, wo_ref[...])

    row = lambda w: pl.BlockSpec((TM, w), lambda i: (i, 0))
    full = lambda a: pl.BlockSpec(a.shape, lambda i: (0, 0))
    gate = lambda cb: pl.BlockSpec((TM, D_MODEL), lambda i: (i, cb))
    return pl.pallas_call(
        body, name="merge_fwd", grid=(t // TM,),
        out_shape=(jax.ShapeDtypeStruct((t, D_MODEL), F32), jax.ShapeDtypeStruct((t, GW), BF16),
                   jax.ShapeDtypeStruct((nseq, 2, seq, 128), F32), jax.ShapeDtypeStruct((t, D_MODEL), BF16),
                   jax.ShapeDtypeStruct((t, D_MODEL), BF16), jax.ShapeDtypeStruct((t, D_MODEL), BF16)),
        in_specs=[_lane_blocks(nseq, seq, 2)] * 6 + [row(2 * GW), gate(0), gate(1), row(D_MODEL), full(wat), full(wbt),
                                                     full(wout)],
        out_specs=(row(D_MODEL), row(GW), _lane_blocks(nseq, seq, 2), row(D_MODEL), row(D_MODEL), row(D_MODEL)),
        compiler_params=_params(("parallel",), VMEM_BIG),
    )(o0, o1, o2, l0, l1, l2, yb, zg, zg, h1, wat, wbt, wout)


def _merge_bwd(dh2, pa, pb, zg, y, yb, wat, wbt, wout, nseq, rider=None):
    t = dh2.shape[0]

    def body(dh2_ref, pa_ref, pb_ref, ga_ref, gb_ref, y_ref, yb_ref, wa_ref, wb_ref, wo_ref,
             dpa_ref, dpb_ref, dga_ref, dgb_ref, dya_ref, dyb_ref, dh2b_ref, ca_ref, cb_ref):
        d16 = dh2_ref[...].astype(BF16)
        dh2b_ref[...] = d16
        dm = _dot_nt(d16, wo_ref[...])
        sa = _sigmoid(ga_ref[...].astype(F32))
        sb = _sigmoid(gb_ref[...].astype(F32))
        dpa = (dm * sa).astype(BF16)
        dpb = (dm * sb).astype(BF16)
        dpa_ref[...] = dpa
        dpb_ref[...] = dpb
        dga_ref[...] = (dm * pa_ref[...].astype(F32) * sa * (1.0 - sa)).astype(BF16)
        dgb_ref[...] = (dm * pb_ref[...].astype(F32) * sb * (1.0 - sb)).astype(BF16)
        dya = _dot(dpa, wa_ref[...])
        dyb = _dot(dpb, wb_ref[...])
        dya_ref[0, 0] = dya[:, :128]
        dya_ref[0, 1] = dya[:, 128:]
        dyb_ref[...] = dyb.astype(BF16)
        ca = _head_sums(dya * y_ref[...].astype(F32))
        ca_ref[0, 0] = ca[:, :128]
        ca_ref[0, 1] = ca[:, 128:]
        cb_ref[...] = _head_sums(dyb * yb_ref[...].astype(F32))

    row = lambda w: pl.BlockSpec((TM, w), lambda i: (i, 0))
    full = lambda a: pl.BlockSpec(a.shape, lambda i: (0, 0))
    gate = lambda cb: pl.BlockSpec((TM, D_MODEL), lambda i: (i, cb))
    bf = lambda w: jax.ShapeDtypeStruct((t, w), BF16)
    lanes = jax.ShapeDtypeStruct((nseq, 2, t // nseq, 128), F32)
    lane_spec = _lane_blocks(nseq, t // nseq, 2)
    return _pallas(
        body, (dh2, pa, pb, zg, zg, y, yb, wat, wbt, wout), name="merge_bwd", grid=(t // TM,),
        out_shape=(bf(D_MODEL), bf(D_MODEL), bf(D_MODEL), bf(D_MODEL), lanes, bf(2 * GW), bf(D_MODEL),
                   lanes, jax.ShapeDtypeStruct((t, 2 * GW), F32)),
        in_specs=[row(D_MODEL), row(D_MODEL), row(D_MODEL), gate(0), gate(1), row(GW), row(2 * GW),
                  full(wat), full(wbt), full(wout)],
        out_specs=(row(D_MODEL), row(D_MODEL), row(D_MODEL), row(D_MODEL), lane_spec, row(2 * GW), row(D_MODEL),
                   lane_spec, row(2 * GW)),
        sem=("parallel",), vmem=VMEM_BIG, rider=rider)


def _loss_head(h3, gain, tgt):
    t = h3.shape[0]

    def body(h_ref, gain_ref, t_ref, dh_ref, loss_ref, gg_ref):
        @pl.when(pl.program_id(0) == 0)
        def _():
            loss_ref[...] = jnp.zeros_like(loss_ref)
            gg_ref[...] = jnp.zeros_like(gg_ref)

        hh = h_ref[...]
        r = lax.rsqrt(jnp.mean(hh * hh, axis=-1, keepdims=True) + EPS)
        hn = hh * r
        err = hn * gain_ref[...] - t_ref[...]
        part = jnp.sum(jnp.sum(err * err, axis=1, keepdims=True), axis=0, keepdims=True)
        loss_ref[...] += (0.5 / D_MODEL) * part
        dy = err * (1.0 / D_MODEL)
        gg_ref[...] += jnp.sum(dy * hn, axis=0, keepdims=True)
        dng = dy * gain_ref[...]
        dh_ref[...] = r * (dng - hn * jnp.mean(dng * hn, axis=-1, keepdims=True))

    row = pl.BlockSpec((TM, D_MODEL), lambda i: (i, 0))
    vec = pl.BlockSpec((1, D_MODEL), lambda i: (0, 0))
    return pl.pallas_call(
        body, name="loss_head", grid=(t // TM,),
        out_shape=(jax.ShapeDtypeStruct((t, D_MODEL), F32), jax.ShapeDtypeStruct((8, 128), F32),
                   jax.ShapeDtypeStruct((1, D_MODEL), F32)),
        in_specs=[row, vec, row], out_specs=(row, pl.BlockSpec((8, 128), lambda i: (0, 0)), vec),
        compiler_params=_params(("arbitrary",)),
    )(h3, gain, tgt)


def _lane_head(rows):
    return lax.broadcasted_iota(jnp.int32, (rows, GW), 1) // HEAD_DIM


def _kv_expand_matrix(r):
    ci = lax.broadcasted_iota(jnp.int32, (2 * HEAD_DIM, GW), 0)
    ji = lax.broadcasted_iota(jnp.int32, (2 * HEAD_DIM, GW), 1)
    return (ci == (ji % HEAD_DIM) + HEAD_DIM * r).astype(BF16)


def _block_rows(row0, stride, ib):
    start = row0 + (stride * BLOCK) * ib
    if stride > 1:
        return pl.ds(start, BLOCK, stride=stride)
    return pl.ds(pl.multiple_of(start, BLOCK), BLOCK)


def _stack_heads(x, lane_head):
    return jnp.concatenate([jnp.where(lane_head == h, x, jnp.zeros_like(x)) for h in range(4)], axis=0)


def _unstack_heads(x4, lane_head):
    out = jnp.zeros((BLOCK, GW), F32)
    for h in range(4):
        out = jnp.where(lane_head == h, x4[h * BLOCK:(h + 1) * BLOCK], out)
    return out


def _load_rows(ref, rows, split):
    if split:
        return jnp.concatenate([ref[0, 0, rows, :], ref[0, 1, rows, :]], axis=1)
    return ref[0, rows, :]


def _store_rows(ref, rows, val, split):
    if split:
        ref[0, 0, rows, :] = val[:, :128]
        ref[0, 1, rows, :] = val[:, 128:]
    else:
        ref[0, rows, :] = val


def _attn_fwd(q_arr, k_arr, v_arr, bias, sink, *, grid, seq, stride, kvw, split, q_spec, k_spec, v_spec, bias_map,
              sink_map, o_spec, has_sink, o_shape, o_dtype, name, rider=None):
    nb = seq // stride // BLOCK
    scale = HEAD_DIM ** -0.5
    expanded = kvw != GW
    rps = RESIDUES_PER_STEP if stride >= 4 * RESIDUES_PER_STEP else 1
    grid = (grid[0], grid[1] // rps)
    assert not has_sink or B_WINDOW - 1 < BLOCK

    def body(q_ref, k_ref, v_ref, bias_ref, sink_ref, o_ref, lse_ref, *kv_x):
        rr = pl.program_id(1)
        lane_head = _lane_head(BLOCK)
        if expanded:
            expand = _kv_expand_matrix(rr)
            kv_x[0][...] = _dot(k_ref[0], expand).astype(BF16)
            kv_x[1][...] = _dot(v_ref[0], expand).astype(BF16)
        for j in range(rps):
            residue(rr * rps + j if stride > 1 else 0, q_ref, k_ref, v_ref, bias_ref, sink_ref, o_ref, lse_ref, kv_x,
                    lane_head)

    def residue(row0, q_ref, k_ref, v_ref, bias_ref, sink_ref, o_ref, lse_ref, kv_x, lane_head):
        def per_head(fn, x):
            return jnp.concatenate([fn(sink_ref[0, h:h + 1, 0:1], x[h * BLOCK:(h + 1) * BLOCK]) for h in range(4)],
                                   axis=0)

        def load(ref, ib):
            return _load_rows(ref, _block_rows(row0, stride, ib), split).astype(BF16)

        def load_kv(which, ib):
            if expanded:
                return kv_x[which][_block_rows(0, 1, ib), :]
            return load((k_ref, v_ref)[which], ib)

        def block(ib, first):
            q4 = _stack_heads(load(q_ref, ib), lane_head)
            if first:
                kc, vc = load_kv(0, ib), load_kv(1, ib)
                b4 = bias_ref[:, :, BLOCK:].reshape(4 * BLOCK, BLOCK)
            else:
                kc = jnp.concatenate([load_kv(0, ib - 1), load_kv(0, ib)], axis=0)
                vc = jnp.concatenate([load_kv(1, ib - 1), load_kv(1, ib)], axis=0)
                b4 = bias_ref[...].reshape(4 * BLOCK, 2 * BLOCK)
                if has_sink:
                    oldest = lax.broadcasted_iota(jnp.int32, kc.shape, 0) == 0
                    kc = jnp.where(oldest, jnp.zeros_like(kc), kc)
                    vc = jnp.where(oldest, jnp.zeros_like(vc), vc)
            s = _dot_nt(q4, kc) * scale + b4
            m = jnp.max(s, axis=-1, keepdims=True)
            if has_sink and first:
                m = per_head(jnp.maximum, m)
            p = jnp.exp(s - m)
            l = jnp.sum(p, axis=-1, keepdims=True)
            if has_sink and first:
                l = l + per_head(lambda sk, mh: jnp.exp(sk - mh), m)
            o4 = _dot(p.astype(BF16), vc) / l
            rows = _block_rows(row0, stride, ib)
            _store_rows(o_ref, rows, _unstack_heads(o4, lane_head).astype(o_dtype), split)
            _store_rows(lse_ref, rows, _unstack_heads(m + jnp.log(l), lane_head), split)

        block(0, True)
        if nb > 1:
            def step(i, carry):
                block(i, False)
                return carry
            lax.fori_loop(1, nb, step, 0)

    return _pallas(
        body, (q_arr, k_arr, v_arr, bias, sink), name=name, grid=grid,
        out_shape=(jax.ShapeDtypeStruct(o_shape, o_dtype), jax.ShapeDtypeStruct(o_shape, F32)),
        in_specs=[q_spec, k_spec, v_spec,
                  pl.BlockSpec((4, BLOCK, 2 * BLOCK), bias_map), pl.BlockSpec((1, 4, 128), sink_map)],
        out_specs=(o_spec, o_spec),
        scratch_shapes=[pltpu.VMEM((seq, GW), BF16)] * 2 if expanded else [],
        sem=("arbitrary", "arbitrary"), vmem=VMEM_BIG, rider=rider)


def _attn_bwd(q_arr, k_arr, v_arr, bias, sink, dy, cc, lse, *, grid, seq, stride, kvw, split, q_spec, k_spec, v_spec,
              bias_map, sink_map, o_spec, kv_out_spec, has_sink, n_bias, dq_shape, dkv_shape, g_dtype, name):
    ln = seq // stride
    nb = ln // BLOCK
    scale = HEAD_DIM ** -0.5
    expanded = kvw != GW
    rps = RESIDUES_PER_STEP if stride >= 4 * RESIDUES_PER_STEP else 1
    grid = (grid[0], grid[1] // rps)

    def body(q_ref, k_ref, v_ref, bias_ref, sink_ref, dy_ref, c_ref, lse_ref,
             dq_ref, dk_ref, dv_ref, db_ref, dsk_ref, dk_acc, dv_acc, dk_half, dv_half, *kv_x):
        rr = pl.program_id(1)

        @pl.when((pl.program_id(0) == 0) & (rr == 0))
        def _():
            db_ref[...] = jnp.zeros_like(db_ref)
            dsk_ref[...] = jnp.zeros_like(dsk_ref)

        if expanded:
            expand = _kv_expand_matrix(rr)
            kv_x[0][...] = _dot(k_ref[0], expand).astype(BF16)
            kv_x[1][...] = _dot(v_ref[0], expand).astype(BF16)
        refs = (q_ref, k_ref, v_ref, bias_ref, sink_ref, dy_ref, c_ref, lse_ref, dq_ref, dk_ref, dv_ref, db_ref,
                dsk_ref, dk_acc, dv_acc, dk_half, dv_half, kv_x)
        for j in range(rps):
            residue(rr, rr * rps + j if stride > 1 else 0, *refs)

    def residue(rr, row0, q_ref, k_ref, v_ref, bias_ref, sink_ref, dy_ref, c_ref, lse_ref,
                dq_ref, dk_ref, dv_ref, db_ref, dsk_ref, dk_acc, dv_acc, dk_half, dv_half, kv_x):
        dk_acc[...] = jnp.zeros_like(dk_acc)
        dv_acc[...] = jnp.zeros_like(dv_acc)
        lane_head = _lane_head(BLOCK)
        hb = 4 * rr if n_bias == 8 else 0

        def load(ref, ib):
            return _load_rows(ref, _block_rows(row0, stride, ib), split)

        def load_kv(which, ib):
            if expanded:
                return kv_x[which][_block_rows(0, 1, ib), :]
            return load((k_ref, v_ref)[which], ib).astype(BF16)

        def head_col(x):
            return jnp.concatenate([x[:, h * HEAD_DIM:h * HEAD_DIM + 1] for h in range(4)], axis=0)

        def block(ib, first):
            q4 = _stack_heads(load(q_ref, ib).astype(BF16), lane_head)
            dy4 = _stack_heads(load(dy_ref, ib).astype(BF16), lane_head)
            c4 = head_col(load(c_ref, ib))
            l4 = head_col(load(lse_ref, ib))
            if first:
                kc, vc = load_kv(0, ib), load_kv(1, ib)
                b4 = bias_ref[:, :, BLOCK:].reshape(4 * BLOCK, BLOCK)
                krows = pl.ds(0, BLOCK)
            else:
                kc = jnp.concatenate([load_kv(0, ib - 1), load_kv(0, ib)], axis=0)
                vc = jnp.concatenate([load_kv(1, ib - 1), load_kv(1, ib)], axis=0)
                b4 = bias_ref[...].reshape(4 * BLOCK, 2 * BLOCK)
                krows = pl.ds(pl.multiple_of((ib - 1) * BLOCK, BLOCK), 2 * BLOCK)
            nk = BLOCK if first else 2 * BLOCK
            p = jnp.exp(_dot_nt(q4, kc) * scale + b4 - l4)
            ds = p * (_dot_nt(dy4, vc) - c4)
            ds3 = ds.reshape(4, BLOCK, nk)
            if n_bias == 8:
                if first:
                    db_ref[pl.ds(hb, 4), :, BLOCK:] += ds3
                else:
                    db_ref[pl.ds(hb, 4)] += ds3
            elif first:
                db_ref[:, :, BLOCK:] += ds3
            else:
                db_ref[...] += ds3
            ds16 = ds.astype(BF16)
            dq = _unstack_heads(_dot(ds16, kc), lane_head) * scale
            _store_rows(dq_ref, _block_rows(row0, stride, ib), dq.astype(g_dtype), split)
            dk_acc[krows, :] += _dot_tn(ds16, q4) * scale
            dv_acc[krows, :] += _dot_tn(p.astype(BF16), dy4)
            if has_sink:
                for h in range(4):
                    hs = slice(h * BLOCK, (h + 1) * BLOCK)
                    sk = sink_ref[0, h:h + 1, 0:1]
                    val = -jnp.sum(jnp.exp(sk - l4[hs]) * c4[hs], axis=0, keepdims=True)
                    dsk_ref[hb + h] += jnp.broadcast_to(val, (8, 128))

        block(0, True)
        if nb > 1:
            def step(i, carry):
                block(i, False)
                return carry
            lax.fori_loop(1, nb, step, 0)

        if kvw == GW:
            all_rows = pl.ds(row0, ln, stride=stride) if stride > 1 else pl.ds(0, ln)
            _store_rows(dk_ref, all_rows, dk_acc[...].astype(g_dtype), split)
            _store_rows(dv_ref, all_rows, dv_acc[...].astype(g_dtype), split)
        else:
            def fold(acc):
                t2 = acc[:, :2 * HEAD_DIM] + acc[:, 2 * HEAD_DIM:]
                t2 = t2 + pltpu.roll(t2, HEAD_DIM, 1)
                lane = lax.broadcasted_iota(jnp.int32, t2.shape, 1) // HEAD_DIM
                return jnp.where(lane == rr, t2, 0.0)

            @pl.when(rr == 0)
            def _():
                dk_half[...] = fold(dk_acc[...])
                dv_half[...] = fold(dv_acc[...])

            @pl.when(rr == 1)
            def _():
                dk_ref[0] = (dk_half[...] + fold(dk_acc[...])).astype(g_dtype)
                dv_ref[0] = (dv_half[...] + fold(dv_acc[...])).astype(g_dtype)

    return pl.pallas_call(
        body, name=name, grid=grid,
        out_shape=(jax.ShapeDtypeStruct(dq_shape, g_dtype), jax.ShapeDtypeStruct(dkv_shape, g_dtype),
                   jax.ShapeDtypeStruct(dkv_shape, g_dtype), jax.ShapeDtypeStruct((n_bias, BLOCK, 2 * BLOCK), F32),
                   jax.ShapeDtypeStruct((8, 8, 128), F32)),
        in_specs=[q_spec, k_spec, v_spec,
                  pl.BlockSpec((4, BLOCK, 2 * BLOCK), bias_map), pl.BlockSpec((1, 4, 128), sink_map),
                  o_spec, o_spec, o_spec],
        out_specs=(o_spec, kv_out_spec, kv_out_spec,
                   pl.BlockSpec((n_bias, BLOCK, 2 * BLOCK), lambda n, r: (0, 0, 0)),
                   pl.BlockSpec((8, 8, 128), lambda n, r: (0, 0, 0))),
        scratch_shapes=[pltpu.VMEM((ln, GW), F32), pltpu.VMEM((ln, GW), F32),
                        pltpu.VMEM((ln, 2 * HEAD_DIM), F32), pltpu.VMEM((ln, 2 * HEAD_DIM), F32)]
        + ([pltpu.VMEM((seq, GW), BF16)] * 2 if expanded else []),
        compiler_params=_params(("arbitrary", "arbitrary"), VMEM_BIG),
    )(q_arr, k_arr, v_arr, bias, sink, dy, cc, lse)


def _bias_grad(ds_all, buckets):
    def body(ds_ref, bk_ref, o_ref):
        rows = lax.broadcasted_iota(jnp.int32, (N_BUCKETS, 128), 0)
        cols = lax.broadcasted_iota(jnp.int32, (N_BUCKETS, 128), 1)

        def per_bucket(b, acc):
            for h in range(20):
                gi = h // 4 if h < 12 else 3
                v = jnp.where(bk_ref[gi] == b, ds_ref[h], 0.0)
                v = jnp.sum(jnp.sum(v, axis=1, keepdims=True), axis=0, keepdims=True)
                acc = jnp.where((rows == b) & (cols == h), v, acc)
            return acc

        o_ref[...] = lax.fori_loop(0, N_BUCKETS, per_bucket, jnp.zeros((N_BUCKETS, 128), F32))

    vm = pl.BlockSpec(memory_space=pltpu.VMEM)
    return pl.pallas_call(body, name="bias_grad", out_shape=jax.ShapeDtypeStruct((N_BUCKETS, 128), F32),
                          in_specs=[vm, vm], out_specs=vm)(ds_all, buckets)


def _adamw(w, g, m, v, name):
    r, c = w.shape
    tr = r
    for cand in (256, 176, 128, 64, 32, 16, 8):
        if r % cand == 0:
            tr = cand
            break
    bc1 = 1.0 - ADAM_B1 ** ADAM_STEP
    bc2 = 1.0 - ADAM_B2 ** ADAM_STEP

    def body(w_ref, g_ref, m_ref, v_ref, d_ref, nm_ref, nv_ref):
        gv = g_ref[...]
        nm = ADAM_B1 * m_ref[...] + (1.0 - ADAM_B1) * gv
        nv = ADAM_B2 * v_ref[...] + (1.0 - ADAM_B2) * (gv * gv)
        nm_ref[...] = nm
        nv_ref[...] = nv
        d_ref[...] = -ADAM_LR * ((nm / bc1) / (jnp.sqrt(nv / bc2) + ADAM_EPS) + ADAM_WD * w_ref[...])

    spec = pl.BlockSpec((tr, c), lambda i: (i, 0))
    shp = jax.ShapeDtypeStruct((r, c), F32)
    return pl.pallas_call(body, name=name, grid=(r // tr,), out_shape=(shp, shp, shp),
                          in_specs=[spec] * 4, out_specs=(spec, spec, spec),
                          compiler_params=_params(("parallel",)))(w, g, m, v)


def _t5_bucket(dist):
    max_exact = N_BUCKETS // 2
    n = jnp.maximum(dist, 0)
    nf = jnp.maximum(n, 1).astype(F32)
    large = max_exact + (jnp.log(nf / max_exact) / math.log(MAX_DISTANCE / max_exact)
                         * (N_BUCKETS - max_exact)).astype(jnp.int32)
    large = jnp.minimum(large, N_BUCKETS - 1)
    return jnp.where(n < max_exact, n, large)


def _bias_tables(rel_bias):
    qi = jnp.arange(BLOCK)[:, None]
    ki = jnp.arange(2 * BLOCK)[None, :]
    dist = qi + BLOCK - ki
    specs = [(d, w // d, 4 * gi, 4 * gi + 4) for gi, (w, d) in enumerate(DIL_GROUPS)] + [(1, B_WINDOW - 1, 12, 20)]
    biases, buckets = [], []
    for stride, steps, h0, h1 in specs:
        valid = (dist >= 0) & (dist <= steps)
        bk = jnp.where(valid, _t5_bucket(dist * stride), -1).astype(jnp.int32)
        onehot = (bk[None, :, :] == jnp.arange(N_BUCKETS, dtype=jnp.int32)[:, None, None]).astype(F32)
        b = jnp.einsum("bqk,bh->hqk", onehot, rel_bias[:, h0:h1], precision=lax.Precision.HIGHEST)
        biases.append(jnp.where(valid[None], b, NEG))
        buckets.append(bk)
    return jnp.concatenate(biases, axis=0), jnp.stack(buckets, axis=0)


def _local_step(x, tgt, W, S, shards=None):
    nseq, seq, _ = x.shape
    t = nseq * seq
    xf = x.reshape(t, D_MODEL)
    bias_all, buckets = _bias_tables(S["rel_bias"])
    sink_b = jnp.broadcast_to(S["sinks"].reshape(2, 4, 1), (2, 4, 128)).astype(F32)
    sink_0 = jnp.zeros((1, 4, 128), F32)
    dist = shards is not None
    W = dict(W)
    G, GS, reduced = {}, {}, {}

    def put(keys, gathered):
        for k, g in zip(keys, gathered):
            W[k] = g.reshape(_FULL_SHAPE.get(k, (N_CHIPS * shards[k].shape[0], D_MODEL)))

    def gather_rider(keys):
        return _GatherRider([shards[k] for k in keys]) if dist else None

    def pair(keys):
        return _pair_reduce([G[k].reshape(N_CHIPS, 2, shards[k].shape[0] // 2, D_MODEL) for k in keys],
                            "grad_pair_reduce_" + keys[0])

    def finish(keys, own, rec):
        full = _final_reduce(own, rec, "grad_final_reduce_" + keys[0])
        off = 0
        for k in keys:
            r = shards[k].shape[0]
            reduced[k] = full[:, off:off + r // 2].reshape(r, D_MODEL)
            off += r // 2

    if dist:
        first = ("wgt1", "wut1", "wd1")
        put(first, _gather_rows([shards[k] for k in first]))
    keys = ("wint", "wgt2")
    (h1, n1, g1, u1), ro = _ffn_fwd(xf, S["ffn1_norm"], W["wgt1"], W["wut1"], W["wd1"], rider=gather_rider(keys))
    put(keys, ro)
    keys = ("wout", "wat", "wbt", "wut2")
    (un, za, zb, zg), ro = _inproj_fwd(h1, S["mix_norm"], W["wint"], S["b_in"], nseq, rider=gather_rider(keys))
    put(keys, ro)

    seq3 = lambda a: a.reshape(nseq, seq, a.shape[-1])
    zb3 = seq3(zb)
    pair_blk = lambda cb: pl.BlockSpec((1, 2, seq, 128), lambda n, r, cb=cb: (n, cb, 0, 0))
    a_cfg = []
    outs, lses = [], []
    for gi, (_, d) in enumerate(DIL_GROUPS):
        cfg = dict(grid=(nseq, d), seq=seq, stride=d, kvw=GW, split=True,
                   q_spec=pair_blk(gi), k_spec=pair_blk(3 + gi), v_spec=pair_blk(6 + gi), o_spec=pair_blk(0),
                   bias_map=lambda n, r: (0, 0, 0), sink_map=lambda n, r: (0, 0, 0), has_sink=False)
        a_cfg.append(cfg)
        (o, lse), _ = _attn_fwd(za, za, za, bias_all[4 * gi:4 * gi + 4], sink_0, o_shape=(nseq, 2, seq, 128),
                                o_dtype=F32, name=f"attn_a{gi}_fwd", **cfg)
        outs.append(o)
        lses.append(lse)
    wide_blk = lambda w, cmap: pl.BlockSpec((1, seq, w), cmap)
    b_cfg = dict(grid=(nseq, 2), seq=seq, stride=1, kvw=2 * HEAD_DIM, split=False,
                 q_spec=wide_blk(GW, lambda n, r: (n, 0, r)), k_spec=wide_blk(2 * HEAD_DIM, lambda n, r: (n, 0, 4)),
                 v_spec=wide_blk(2 * HEAD_DIM, lambda n, r: (n, 0, 5)), o_spec=wide_blk(GW, lambda n, r: (n, 0, r)),
                 bias_map=lambda n, r: (r, 0, 0), sink_map=lambda n, r: (r, 0, 0), has_sink=True)
    keys = ("wd2",)
    bias_b_fwd = bias_all[12:20].at[:, :, 0].set(jnp.broadcast_to(S["sinks"].reshape(8, 1), (8, BLOCK)))
    (yb, lse_b), ro = _attn_fwd(zb3, zb3, zb3, bias_b_fwd, sink_b, o_shape=(nseq, seq, 2 * GW), o_dtype=BF16,
                                name="attn_b_fwd", rider=gather_rider(keys), **b_cfg)
    put(keys, ro)
    yb = yb.reshape(t, 2 * GW)

    h2, y, lse_tot, pa, pb, merged = _merge_fwd(outs[0], outs[1], outs[2], lses[0], lses[1], lses[2], yb, zg, h1,
                                                W["wat"], W["wbt"], W["wout"])
    (h3, n2, g2, u2), _ = _ffn_fwd(h2, S["ffn2_norm"], W["wgt2"], W["wut2"], W["wd2"])
    dh3, loss_part, g_final = _loss_head(h3, S["final_norm"].reshape(1, D_MODEL), tgt.reshape(t, D_MODEL))

    GS["final_norm"] = g_final
    dh2, dg2, du2, a2, df2, GS["ffn2_norm"] = _ffn_bwd(dh3, h2, S["ffn2_norm"], g2, u2, W["wgt2"], W["wut2"], W["wd2"])
    G["wgt2"] = _wgrad(dg2, n2, D_FF, name="wgrad_gate2")
    G["wut2"] = _wgrad(du2, n2, D_FF, name="wgrad_up2")
    G["wd2"] = _wgrad(a2, df2, D_FF, name="wgrad_down2")

    keys = ("wgt2", "wut2", "wd2")
    rider = _ExchangeRider([pair(keys)]) if dist else None
    (dpa, dpb, dga, dgb, dya, dyb, dh2b, ca, cb), ro = _merge_bwd(dh2, pa, pb, zg, y, yb, W["wat"], W["wbt"], W["wout"],
                                                                  nseq, rider=rider)
    if dist:
        finish(keys, *ro)
    G["wout"] = _wgrad(merged, dh2b, D_MODEL, name="wgrad_out")
    G["wat"] = _wgrad(dpa, y, D_MODEL, name="wgrad_branch_a")
    G["wbt"] = _wgrad(dpb, yb, D_MODEL, name="wgrad_branch_b")

    dqs, dks, dvs, dbs = [], [], [], []
    shp = (nseq, 2, seq, 128)
    halves = lambda a: [a[:, hf].reshape(t, 128).astype(BF16) for hf in range(2)]
    for gi in range(len(DIL_GROUPS)):
        dq, dk, dv, db, _ = _attn_bwd(za, za, za, bias_all[4 * gi:4 * gi + 4], sink_0, dya, ca, lse_tot,
                                      n_bias=4, dq_shape=shp, dkv_shape=shp, g_dtype=F32,
                                      kv_out_spec=a_cfg[gi]["o_spec"], name=f"attn_a{gi}_bwd", **a_cfg[gi])
        dqs += halves(dq)
        dks += halves(dk)
        dvs += halves(dv)
        dbs.append(db)
    dqb, dkb, dvb, dbb, dsink = _attn_bwd(zb3, zb3, zb3, bias_all[12:20], sink_b, seq3(dyb), seq3(cb), lse_b,
                                          n_bias=8, dq_shape=(nseq, seq, 2 * GW),
                                          dkv_shape=(nseq, seq, 2 * HEAD_DIM), g_dtype=BF16,
                                          kv_out_spec=wide_blk(2 * HEAD_DIM, lambda n, r: (n, 0, 0)),
                                          name="attn_b_bwd", **b_cfg)
    dz = jnp.concatenate(dqs + dks + dvs + [dqb.reshape(t, 2 * GW), dkb.reshape(t, 2 * HEAD_DIM),
                                            dvb.reshape(t, 2 * HEAD_DIM), dga, dgb], axis=-1)
    gb_tab = _bias_grad(jnp.concatenate(dbs + [dbb], axis=0), buckets)
    GS["rel_bias"] = gb_tab[:, :20]
    GS["sinks"] = dsink[:, 0, 0].reshape(1, 8)

    G["wint"], GS["b_in"] = _wgrad(dz, un, D_IN // 2, with_colsum=True, name="wgrad_in")
    keys = ("wint", "wout", "wat", "wbt")
    rider = _ExchangeRider([pair(keys)]) if dist else None
    (dh1, GS["mix_norm"]), ro = _inproj_bwd(dz, dh2, h1, S["mix_norm"], W["wint"], rider=rider)
    if dist:
        finish(keys, *ro)

    dx, dg1, du1, a1, df1, GS["ffn1_norm"] = _ffn_bwd(dh1, xf, S["ffn1_norm"], g1, u1, W["wgt1"], W["wut1"], W["wd1"])
    G["wgt1"] = _wgrad(dg1, n1, D_FF, name="wgrad_gate1")
    if dist:
        G["wut1"], ro = _wgrad(du1, n1, D_FF, name="wgrad_up1", rider=_ExchangeRider([pair(("wgt1",))]))
        finish(("wgt1",), *ro)
        G["wd1"], ro = _wgrad(a1, df1, D_FF, name="wgrad_down1", rider=_ExchangeRider([pair(("wut1",))]))
        finish(("wut1",), *ro)
        finish(("wd1",), *_chip_exchange([pair(("wd1",))]))
    else:
        G["wut1"] = _wgrad(du1, n1, D_FF, name="wgrad_up1")
        G["wd1"] = _wgrad(a1, df1, D_FF, name="wgrad_down1")
    return loss_part, dx.reshape(x.shape), (reduced if dist else G), GS


_SMALL = ("ffn1_norm", "mix_norm", "ffn2_norm", "final_norm", "b_in", "sinks", "rel_bias")
_ORDER = ("ffn1_norm", "ffn1_w_gate", "ffn1_w_up", "ffn1_w_down", "mix_norm", "w_in", "b_in", "w_branch_a",
          "w_branch_b", "w_out", "sinks", "rel_bias", "ffn2_norm", "ffn2_w_gate", "ffn2_w_up", "ffn2_w_down",
          "final_norm")
_BIG = (("wgt1", "ffn1_w_gate", True, 704), ("wut1", "ffn1_w_up", True, 704), ("wd1", "ffn1_w_down", False, 704),
        ("wint", "w_in", True, 1280), ("wout", "w_out", False, 256), ("wat", "w_branch_a", True, 64),
        ("wbt", "w_branch_b", True, 128), ("wgt2", "ffn2_w_gate", True, 704), ("wut2", "ffn2_w_up", True, 704),
        ("wd2", "ffn2_w_down", False, 704))
_FULL_SHAPE = {"wat": (D_MODEL, GW), "wbt": (D_MODEL, 2 * GW)}


def _pack_small(p, extra=None):
    last = [p["sinks"].reshape(8), p["rel_bias"].reshape(640)]
    used = 648
    if extra is not None:
        last.append(extra.reshape(1))
        used += 1
    last.append(jnp.zeros((D_MODEL - used,), F32))
    rows = [p["ffn1_norm"].reshape(1, D_MODEL), p["mix_norm"].reshape(1, D_MODEL), p["ffn2_norm"].reshape(1, D_MODEL),
            p["final_norm"].reshape(1, D_MODEL), p["b_in"].reshape(5, D_MODEL), jnp.concatenate(last).reshape(1, D_MODEL),
            jnp.zeros((6, D_MODEL), F32)]
    return jnp.concatenate(rows, axis=0)


def _unpack_small(a):
    return {"ffn1_norm": a[0:1], "mix_norm": a[1:2], "ffn2_norm": a[2:3], "final_norm": a[3],
            "b_in": a[4:9].reshape(1, D_IN), "sinks": a[9, 0:8].reshape(1, 8), "rel_bias": a[9, 8:648].reshape(32, 20)}


def kernel(x, ffn1_norm, ffn1_w_gate, ffn1_w_up, ffn1_w_down, mix_norm, w_in, b_in, w_branch_a, w_branch_b, w_out, sinks, rel_bias, ffn2_norm, ffn2_w_gate, ffn2_w_up, ffn2_w_down, final_norm, loss_target, m_ffn1_norm, m_ffn1_w_gate, m_ffn1_w_up, m_ffn1_w_down, m_mix_norm, m_w_in, m_b_in, m_w_branch_a, m_w_branch_b, m_w_out, m_sinks, m_rel_bias, m_ffn2_norm, m_ffn2_w_gate, m_ffn2_w_up, m_ffn2_w_down, m_final_norm, v_ffn1_norm, v_ffn1_w_gate, v_ffn1_w_up, v_ffn1_w_down, v_mix_norm, v_w_in, v_b_in, v_w_branch_a, v_w_branch_b, v_w_out, v_sinks, v_rel_bias, v_ffn2_norm, v_ffn2_w_gate, v_ffn2_w_up, v_ffn2_w_down, v_final_norm):
    args = dict(locals())
    w = {n: args[n] for n in _ORDER}
    m = {n: args["m_" + n] for n in _ORDER}
    v = {n: args["v_" + n] for n in _ORDER}

    shards = {}
    for key, name, transposed, rows in _BIG:
        a = w[name][0]
        a = (a.T if transposed else a).astype(BF16)
        shards[key] = a.reshape(rows, D_MODEL)
    S = {n: w[n] for n in _SMALL}

    loss_part, grad_x, reduced, GS = _local_step(x, loss_target, {}, S, shards)

    small = _allreduce_small(_pack_small(GS, extra=loss_part[0, 0]))
    loss = small[9, 648]

    out_g, out_d, out_m, out_v = {}, {}, {}, {}
    for key, n, transposed, rows in _BIG:
        nat = w[n][0].shape
        if transposed and nat[1] % 128:
            res = _adamw(w[n][0].T, reduced[key], m[n][0].T, v[n][0].T, "adamw_" + n)
            res = [reduced[key].T] + [r.T for r in res]
        else:
            g = reduced[key].reshape(nat[1], nat[0]).T if transposed else reduced[key].reshape(nat)
            res = [g] + list(_adamw(w[n][0], g, m[n][0], v[n][0], "adamw_" + n))
        out_g[n], out_d[n], out_m[n], out_v[n] = [r[None] for r in res]
    d_s, m_s, v_s = _adamw(_pack_small(w), small, _pack_small(m), _pack_small(v), "adamw_small")
    for dst, src in ((out_g, small), (out_d, d_s), (out_m, m_s), (out_v, v_s)):
        dst.update(_unpack_small(src))

    return (loss, grad_x, *[out_g[n] for n in _ORDER], *[out_d[n] for n in _ORDER],
            *[out_m[n] for n in _ORDER], *[out_v[n] for n in _ORDER])
```

```python
import math

import jax
import jax.numpy as jnp
from jax import lax
from jax.experimental import pallas as pl
from jax.experimental.pallas import tpu as pltpu

F32, BF16 = jnp.float32, jnp.bfloat16
MESH = pl.DeviceIdType.MESH

D_MODEL = 1024
D_FF = 2816
D_IN = 5120
HEAD_DIM = 64
BLOCK = 128
DIL_GROUPS = ((128, 1), (512, 4), (2048, 16))
B_WINDOW = 128
N_BUCKETS = 32
MAX_DISTANCE = 2048
EPS = 1e-6
N_CHIPS = 4
GW = 256
ZA_W = 2304
ZB_W = 768
NEG = -1e30

ADAM_LR, ADAM_B1, ADAM_B2, ADAM_EPS, ADAM_WD, ADAM_STEP = 0.001, 0.9, 0.999, 1e-08, 0.01, 10

VMEM_BIG = 56 * 1024 * 1024
TM = 512
TM_BWD = 256
MXU_DIM = 256
FF_BOUNDS = (0, 6 * MXU_DIM, D_FF)
DMA_SPLIT = 8
RESIDUES_PER_STEP = 4


def _dot(a, b):
    return jnp.dot(a, b, preferred_element_type=F32)


def _dot_nt(a, b):
    return lax.dot_general(a, b, (((1,), (1,)), ((), ())), preferred_element_type=F32)


def _dot_tn(a, b):
    return lax.dot_general(a, b, (((0,), (0,)), ((), ())), preferred_element_type=F32)


def _sigmoid(x):
    return 0.5 * jnp.tanh(0.5 * x) + 0.5


def _params(sem, vmem=None):
    return pltpu.CompilerParams(dimension_semantics=sem, vmem_limit_bytes=vmem)


ANY = pl.BlockSpec(memory_space=pl.ANY)


def _me():
    return lax.axis_index("x"), lax.axis_index("y"), lax.axis_index("c")


_CHIP_RELS = ((1, 0), (0, 1), (1, 1))


def _flip(v, f):
    return 1 - v if f else v


def _remote(src, dst, ssem, rsem, peer):
    return pltpu.make_async_remote_copy(src_ref=src, dst_ref=dst, send_sem=ssem, recv_sem=rsem,
                                        device_id=peer, device_id_type=MESH)


def _row_pieces(rows, n):
    step = max(16, -(-rows // n) // 16 * 16)
    out, s = [], 0
    while s < rows:
        out.append((s, min(step, rows - s)))
        s += step
    return out


def _gather_rows(shards):
    nt = len(shards)
    rows = [s.shape[0] for s in shards]

    def body(*refs):
        srcs, outs = refs[:nt], refs[nt:2 * nt]
        ici_s, ici_r, d2d_s, d2d_r, loc = refs[2 * nt:]
        x, y, c = _me()
        j = 2 * x + y
        sib = (x, y, 1 - c)
        local = [pltpu.make_async_copy(srcs[t], outs[t].at[j], loc.at[t]) for t in range(nt)]
        for cp in local:
            cp.start()
        sends = []
        for k, (fx, fy) in enumerate(_CHIP_RELS):
            peer = (_flip(x, fx), _flip(y, fy), c)
            for t in range(nt):
                half = pl.ds(c * (rows[t] // 2), rows[t] // 2)
                cp = _remote(srcs[t].at[half], outs[t].at[j, half], ici_s.at[3 * t + k], ici_r.at[3 * t + k], peer)
                cp.start()
                sends.append(cp)
        fwds = []
        for k, (fx, fy) in enumerate(_CHIP_RELS):
            pj = 2 * _flip(x, fx) + _flip(y, fy)
            for t in range(nt):
                half = pl.ds(c * (rows[t] // 2), rows[t] // 2)
                blk = outs[t].at[pj, half]
                _remote(blk, blk, ici_s.at[3 * t + k], ici_r.at[3 * t + k], sib).wait_recv()
                cp = _remote(blk, blk, d2d_s.at[3 * t + k], d2d_r.at[3 * t + k], sib)
                cp.start()
                fwds.append(cp)
        for cp in fwds:
            cp.wait()
        for cp in sends:
            cp.wait_send()
        for cp in local:
            cp.wait()

    sems = [pltpu.SemaphoreType.DMA((3 * nt,)) for _ in range(4)] + [pltpu.SemaphoreType.DMA((nt,))]
    return pl.pallas_call(
        body, name="gather_weights",
        out_shape=tuple(jax.ShapeDtypeStruct((N_CHIPS,) + s.shape, s.dtype) for s in shards),
        in_specs=[ANY] * nt, out_specs=tuple([ANY] * nt), scratch_shapes=sems,
    )(*shards)


VMEM_WHOLE = pl.BlockSpec(memory_space=pltpu.VMEM)


def _pair_reduce(grads, name):
    nt = len(grads)
    r2 = [g.shape[2] for g in grads]
    off = [sum(r2[:t]) for t in range(nt)]
    tot = sum(r2)

    def body(*refs):
        gs = refs[:nt]
        s_ref, got, ssem, rsem = refs[nt:]
        x, y, c = _me()
        sib = (x, y, 1 - c)
        for t in range(nt):
            for k in range(N_CHIPS):
                _remote(gs[t].at[k, 1 - c], got.at[k, pl.ds(off[t], r2[t])], ssem, rsem, sib).start()
        _remote(got, got, ssem, rsem, sib).wait()
        for t in range(nt):
            for k in range(N_CHIPS):
                rows = slice(off[t], off[t] + r2[t])
                s_ref[k, rows, :] = (gs[t][k, c].astype(F32) + got[k, rows, :].astype(F32)).astype(BF16)

    shp = jax.ShapeDtypeStruct((N_CHIPS, tot, D_MODEL), BF16)
    return pl.pallas_call(
        body, name=name, out_shape=shp, in_specs=[VMEM_WHOLE] * nt, out_specs=VMEM_WHOLE,
        scratch_shapes=[pltpu.VMEM((N_CHIPS, tot, D_MODEL), BF16), pltpu.SemaphoreType.DMA(()),
                        pltpu.SemaphoreType.DMA(())],
        compiler_params=pltpu.CompilerParams(vmem_limit_bytes=VMEM_BIG),
    )(*grads)


def _chip_exchange(parts):
    ng = len(parts)
    r2 = [p.shape[1] for p in parts]
    off = [sum(r2[:g]) for g in range(ng)]
    tot = sum(r2)

    def body(*refs):
        ps = refs[:ng]
        own_ref, rec_ref, ssems, rsems, lsem = refs[ng:]
        x, y, c = _me()
        j = 2 * x + y
        for g in range(ng):
            pltpu.make_async_copy(ps[g].at[j], own_ref.at[pl.ds(off[g], r2[g])], lsem).start()
        for k, (fx, fy) in enumerate(_CHIP_RELS):
            px, py = _flip(x, fx), _flip(y, fy)
            for g in range(ng):
                for st, sz in _row_pieces(r2[g], 2):
                    _remote(ps[g].at[2 * px + py, pl.ds(st, sz)], rec_ref.at[k, pl.ds(off[g] + st, sz)],
                            ssems.at[k], rsems.at[k], (px, py, c)).start()
        for k in range(3):
            _remote(rec_ref.at[k], rec_ref.at[k], ssems.at[k], rsems.at[k], (x, y, c)).wait()
        pltpu.make_async_copy(own_ref, own_ref, lsem).wait()

    return pl.pallas_call(
        body, name="grad_chip_exchange",
        out_shape=(jax.ShapeDtypeStruct((tot, D_MODEL), BF16), jax.ShapeDtypeStruct((3, tot, D_MODEL), BF16)),
        in_specs=[ANY] * ng, out_specs=(ANY, ANY),
        scratch_shapes=[pltpu.SemaphoreType.DMA((3,)), pltpu.SemaphoreType.DMA((3,)), pltpu.SemaphoreType.DMA(())],
    )(*parts)


def _final_reduce(own, rec, name):
    r2 = own.shape[0]
    pieces = _row_pieces(r2, DMA_SPLIT)

    def body(own_ref, rec_ref, o_ref, fbuf, ssem, rsem, lsem):
        x, y, c = _me()
        sib = (x, y, 1 - c)
        for st, sz in pieces:
            rows = slice(st, st + sz)
            fbuf[rows, :] = (own_ref[rows, :].astype(F32) + rec_ref[0, rows, :].astype(F32)
                             + rec_ref[1, rows, :].astype(F32) + rec_ref[2, rows, :].astype(F32))
            pltpu.make_async_copy(fbuf.at[pl.ds(st, sz)], o_ref.at[c, pl.ds(st, sz)], lsem).start()
            _remote(fbuf.at[pl.ds(st, sz)], o_ref.at[c, pl.ds(st, sz)], ssem, rsem, sib).start()
        _remote(fbuf, o_ref.at[c], ssem, rsem, sib).wait()
        pltpu.make_async_copy(fbuf, o_ref.at[c], lsem).wait()

    return pl.pallas_call(
        body, name=name, out_shape=jax.ShapeDtypeStruct((2, r2, D_MODEL), F32),
        in_specs=[VMEM_WHOLE, VMEM_WHOLE], out_specs=ANY,
        scratch_shapes=[pltpu.VMEM((r2, D_MODEL), F32), pltpu.SemaphoreType.DMA(()), pltpu.SemaphoreType.DMA(()),
                        pltpu.SemaphoreType.DMA(())],
        compiler_params=pltpu.CompilerParams(vmem_limit_bytes=VMEM_BIG),
    )(own, rec)


def _allreduce_small(vec):
    def body(v_ref, o_ref, buf, send_sems, recv_sems):
        x, y, c = _me()
        me = 4 * x + 2 * y + c
        buf[me] = v_ref[...]
        copies = []
        for k in range(1, 8):
            peer = (_flip(x, (k >> 2) & 1), _flip(y, (k >> 1) & 1), _flip(c, k & 1))
            cp = _remote(v_ref, buf.at[me], send_sems.at[k - 1], recv_sems.at[k - 1], peer)
            cp.start()
            copies.append(cp)
        for cp in copies:
            cp.wait()
        acc = buf[0]
        for i in range(1, 8):
            acc = acc + buf[i]
        o_ref[...] = acc

    vm = pl.BlockSpec(memory_space=pltpu.VMEM)
    return pl.pallas_call(
        body, name="allreduce_small", out_shape=jax.ShapeDtypeStruct(vec.shape, vec.dtype),
        in_specs=[vm], out_specs=vm,
        scratch_shapes=[pltpu.VMEM((8,) + vec.shape, vec.dtype), pltpu.SemaphoreType.DMA((7,)),
                        pltpu.SemaphoreType.DMA((7,))],
    )(vec)


class _GatherRider:
    def __init__(self, shards):
        self.inputs = list(shards)
        nt = len(shards)
        self.out_shape = [jax.ShapeDtypeStruct((N_CHIPS,) + s.shape, s.dtype) for s in shards]
        self.scratch = [pltpu.SemaphoreType.DMA((3 * nt,)), pltpu.SemaphoreType.DMA((3 * nt,)),
                        pltpu.SemaphoreType.DMA((nt,))]

    def _copies(self, srcs, outs, sems):
        ici_s, ici_r, loc = sems
        x, y, c = _me()
        j = 2 * x + y
        local = [pltpu.make_async_copy(srcs[t], outs[t].at[j], loc.at[t]) for t in range(len(srcs))]
        remote = []
        for k, (fx, fy) in enumerate(_CHIP_RELS):
            peer = (_flip(x, fx), _flip(y, fy), c)
            for t in range(len(srcs)):
                remote.append(_remote(srcs[t], outs[t].at[j], ici_s.at[3 * t + k], ici_r.at[3 * t + k], peer))
        return local, remote

    def start(self, srcs, outs, sems):
        local, remote = self._copies(srcs, outs, sems)
        for cp in local + remote:
            cp.start()

    def finish(self, srcs, outs, sems):
        local, remote = self._copies(srcs, outs, sems)
        for cp in remote + local:
            cp.wait()


class _ExchangeRider:
    def __init__(self, parts):
        self.inputs = list(parts)
        self.r2 = [p.shape[1] for p in parts]
        self.off = [sum(self.r2[:g]) for g in range(len(parts))]
        tot = sum(self.r2)
        self.out_shape = [jax.ShapeDtypeStruct((tot, D_MODEL), BF16), jax.ShapeDtypeStruct((3, tot, D_MODEL), BF16)]
        self.scratch = [pltpu.SemaphoreType.DMA((3,)), pltpu.SemaphoreType.DMA((3,)), pltpu.SemaphoreType.DMA(())]

    def start(self, ps, outs, sems):
        own_ref, rec_ref = outs
        ssems, rsems, lsem = sems
        x, y, c = _me()
        j = 2 * x + y
        for g in range(len(ps)):
            pltpu.make_async_copy(ps[g].at[j], own_ref.at[pl.ds(self.off[g], self.r2[g])], lsem).start()
        for k, (fx, fy) in enumerate(_CHIP_RELS):
            px, py = _flip(x, fx), _flip(y, fy)
            for g in range(len(ps)):
                for st, sz in _row_pieces(self.r2[g], 2):
                    _remote(ps[g].at[2 * px + py, pl.ds(st, sz)], rec_ref.at[k, pl.ds(self.off[g] + st, sz)],
                            ssems.at[k], rsems.at[k], (px, py, c)).start()

    def finish(self, ps, outs, sems):
        own_ref, rec_ref = outs
        ssems, rsems, lsem = sems
        x, y, c = _me()
        for k in range(3):
            _remote(rec_ref.at[k], rec_ref.at[k], ssems.at[k], rsems.at[k], (x, y, c)).wait()
        pltpu.make_async_copy(own_ref, own_ref, lsem).wait()


def _pallas(body, args, *, name, grid, in_specs, out_specs, out_shape, scratch_shapes=(), sem=None, vmem=None,
            rider=None):
    if rider is None:
        res = pl.pallas_call(body, name=name, grid=grid, in_specs=list(in_specs), out_specs=tuple(out_specs),
                             out_shape=tuple(out_shape), scratch_shapes=list(scratch_shapes),
                             compiler_params=_params(sem, vmem))(*args)
        return tuple(res), ()
    n_in, n_out, n_sc = len(in_specs), len(out_shape), len(scratch_shapes)
    r_in, r_out = len(rider.inputs), len(rider.out_shape)

    def wrapped(*refs):
        ins, rins = refs[:n_in], refs[n_in:n_in + r_in]
        p = n_in + r_in
        outs, routs = refs[p:p + n_out], refs[p + n_out:p + n_out + r_out]
        p += n_out + r_out
        scr, rsems = refs[p:p + n_sc], refs[p + n_sc:]
        first = pl.program_id(0) == 0
        last = pl.program_id(0) == grid[0] - 1
        for a in range(1, len(grid)):
            first = first & (pl.program_id(a) == 0)
            last = last & (pl.program_id(a) == grid[a] - 1)

        @pl.when(first)
        def _():
            rider.start(rins, routs, rsems)

        body(*ins, *outs, *scr)

        @pl.when(last)
        def _():
            rider.finish(rins, routs, rsems)

    res = pl.pallas_call(
        wrapped, name=name, grid=grid, in_specs=list(in_specs) + [ANY] * r_in,
        out_specs=tuple(out_specs) + (ANY,) * r_out, out_shape=tuple(out_shape) + tuple(rider.out_shape),
        scratch_shapes=list(scratch_shapes) + rider.scratch,
        compiler_params=_params(("arbitrary",) * len(grid), vmem))(*args, *rider.inputs)
    return tuple(res[:n_out]), tuple(res[n_out:])


def _ffn_fwd(h, gain, wgt, wut, wd, rider=None):
    t = h.shape[0]

    def body(h_ref, gain_ref, wg_hbm, wu_hbm, wd_hbm, hout_ref, n_ref, g_ref, u_ref, wg_v, wu_v, wd_v):
        @pl.when(pl.program_id(0) == 0)
        def _():
            pltpu.sync_copy(wg_hbm, wg_v)
            pltpu.sync_copy(wu_hbm, wu_v)
            pltpu.sync_copy(wd_hbm, wd_v)

        hh = h_ref[...]
        r = lax.rsqrt(jnp.mean(hh * hh, axis=-1, keepdims=True) + EPS)
        n = (hh * r * gain_ref[...]).astype(BF16)
        n_ref[...] = n
        acc = jnp.zeros((TM, D_MODEL), F32)
        for c0, c1 in zip(FF_BOUNDS[:-1], FF_BOUNDS[1:]):
            sl = slice(c0, c1)
            g = _dot_nt(n, wg_v[sl, :])
            u = _dot_nt(n, wu_v[sl, :])
            g_ref[:, sl] = g.astype(BF16)
            u_ref[:, sl] = u.astype(BF16)
            a = (g * _sigmoid(g) * u).astype(BF16)
            acc = acc + _dot(a, wd_v[sl, :])
        hout_ref[...] = hh + 0.5 * acc

    row = lambda w: pl.BlockSpec((TM, w), lambda i: (i, 0))
    wv = pltpu.VMEM((D_FF, D_MODEL), BF16)
    return _pallas(
        body, (h, gain, wgt, wut, wd), name="ffn_fwd", grid=(t // TM,),
        out_shape=(jax.ShapeDtypeStruct((t, D_MODEL), F32), jax.ShapeDtypeStruct((t, D_MODEL), BF16),
                   jax.ShapeDtypeStruct((t, D_FF), BF16), jax.ShapeDtypeStruct((t, D_FF), BF16)),
        in_specs=[row(D_MODEL), pl.BlockSpec((1, D_MODEL), lambda i: (0, 0)), ANY, ANY, ANY],
        out_specs=(row(D_MODEL), row(D_MODEL), row(D_FF), row(D_FF)),
        scratch_shapes=[wv, wv, wv], sem=("arbitrary",), vmem=VMEM_BIG, rider=rider)


def _ffn_bwd(dhout, h, gain, g, u, wgt, wut, wd):
    t = h.shape[0]
    tm = TM_BWD

    def body(dho_ref, h_ref, gain_ref, g_ref, u_ref, wg_hbm, wu_hbm, wd_hbm,
             dh_ref, dg_ref, du_ref, a_ref, df_ref, gg_ref, wg_v, wu_v, wd_v):
        @pl.when(pl.program_id(0) == 0)
        def _():
            pltpu.sync_copy(wg_hbm, wg_v)
            pltpu.sync_copy(wu_hbm, wu_v)
            pltpu.sync_copy(wd_hbm, wd_v)
            gg_ref[...] = jnp.zeros_like(gg_ref)

        dho = dho_ref[...]
        df = (0.5 * dho).astype(BF16)
        df_ref[...] = df
        dn = jnp.zeros((tm, D_MODEL), F32)
        for c0, c1 in zip(FF_BOUNDS[:-1], FF_BOUNDS[1:]):
            sl = slice(c0, c1)
            da = _dot_nt(df, wd_v[sl, :])
            gv = g_ref[:, sl].astype(F32)
            uv = u_ref[:, sl].astype(F32)
            sg = _sigmoid(gv)
            silu = gv * sg
            dg = (da * uv * (sg * (1.0 + gv * (1.0 - sg)))).astype(BF16)
            du = (da * silu).astype(BF16)
            dg_ref[:, sl] = dg
            du_ref[:, sl] = du
            a_ref[:, sl] = (silu * uv).astype(BF16)
            dn = dn + _dot(dg, wg_v[sl, :]) + _dot(du, wu_v[sl, :])
        hh = h_ref[...]
        r = lax.rsqrt(jnp.mean(hh * hh, axis=-1, keepdims=True) + EPS)
        hn = hh * r
        gg_ref[...] += jnp.sum(dn * hn, axis=0, keepdims=True)
        dng = dn * gain_ref[...]
        dh_ref[...] = dho + r * (dng - hn * jnp.mean(dng * hn, axis=-1, keepdims=True))

    row = lambda w: pl.BlockSpec((tm, w), lambda i: (i, 0))
    vec = pl.BlockSpec((1, D_MODEL), lambda i: (0, 0))
    wv = pltpu.VMEM((D_FF, D_MODEL), BF16)
    return pl.pallas_call(
        body, name="ffn_bwd", grid=(t // tm,),
        out_shape=(jax.ShapeDtypeStruct((t, D_MODEL), F32), jax.ShapeDtypeStruct((t, D_FF), BF16),
                   jax.ShapeDtypeStruct((t, D_FF), BF16), jax.ShapeDtypeStruct((t, D_FF), BF16),
                   jax.ShapeDtypeStruct((t, D_MODEL), BF16), jax.ShapeDtypeStruct((1, D_MODEL), F32)),
        in_specs=[row(D_MODEL), row(D_MODEL), vec, row(D_FF), row(D_FF), ANY, ANY, ANY],
        out_specs=(row(D_MODEL), row(D_FF), row(D_FF), row(D_FF), row(D_MODEL), vec),
        scratch_shapes=[wv, wv, wv],
        compiler_params=_params(("arbitrary",), VMEM_BIG),
    )(dhout, h, gain, g, u, wgt, wut, wd)


def _wgrad(lhs, rhs, rb, with_colsum=False, name="wgrad", rider=None):
    t, k = lhs.shape
    n = rhs.shape[1]
    tk = 512
    nt = t // tk

    def body(l_ref, r_ref, o_ref, *rest):
        acc = rest[-1]
        ti = pl.program_id(1)

        @pl.when(ti == 0)
        def _():
            acc[...] = jnp.zeros_like(acc)
            if with_colsum:
                rest[0][...] = jnp.zeros_like(rest[0])

        acc[...] += _dot_tn(l_ref[...], r_ref[...])
        if with_colsum:
            rest[0][...] += jnp.sum(l_ref[...].astype(F32), axis=0, keepdims=True)

        @pl.when(ti == nt - 1)
        def _():
            o_ref[...] = acc[...].astype(BF16)

    out_shape = [jax.ShapeDtypeStruct((k, n), BF16)]
    out_specs = [pl.BlockSpec((rb, n), lambda j, i: (j, 0))]
    if with_colsum:
        out_shape.append(jax.ShapeDtypeStruct((1, k), F32))
        out_specs.append(pl.BlockSpec((1, rb), lambda j, i: (0, j)))
    res, ro = _pallas(
        body, (lhs, rhs), name=name, grid=(k // rb, nt), out_shape=tuple(out_shape),
        in_specs=[pl.BlockSpec((tk, rb), lambda j, i: (i, j)), pl.BlockSpec((tk, n), lambda j, i: (i, 0))],
        out_specs=tuple(out_specs), scratch_shapes=[pltpu.VMEM((rb, n), F32)],
        sem=("arbitrary", "arbitrary"), vmem=VMEM_BIG, rider=rider)
    if rider is not None:
        return res[0], ro
    return res if with_colsum else res[0]


def _lane_blocks(nseq, seq, nblk, tm=TM):
    spt = seq // tm
    return pl.BlockSpec((1, nblk, tm, 128), lambda i: (i // spt, 0, i % spt, 0))


def _inproj_fwd(h, gain, wint, b_in, nseq, rider=None):
    t = h.shape[0]
    seq = t // nseq
    cut_a = 5 * MXU_DIM
    pieces = ((0, cut_a, 0, 0), (cut_a, ZA_W - cut_a, 0, cut_a), (ZA_W, ZB_W, 1, 0), (ZA_W + ZB_W, 1024, 2, 0),
              (ZA_W + ZB_W + 1024, 1024, 2, 1024))

    def body(h_ref, gain_ref, w_hbm, b_ref, u_ref, za_ref, zb_ref, zg_ref, w_v):
        @pl.when(pl.program_id(0) == 0)
        def _():
            pltpu.sync_copy(w_hbm, w_v)

        hh = h_ref[...]
        r = lax.rsqrt(jnp.mean(hh * hh, axis=-1, keepdims=True) + EPS)
        un = (hh * r * gain_ref[...]).astype(BF16)
        u_ref[...] = un
        outs = (None, zb_ref, zg_ref)
        for c0, cw, oi, o0 in pieces:
            val = _dot_nt(un, w_v[c0:c0 + cw, :]) + b_ref[:, c0:c0 + cw]
            if oi == 0:
                for cb in range(cw // 128):
                    za_ref[0, o0 // 128 + cb] = val[:, cb * 128:(cb + 1) * 128]
            else:
                outs[oi][:, o0:o0 + cw] = val.astype(BF16)

    row = lambda w: pl.BlockSpec((TM, w), lambda i: (i, 0))
    return _pallas(
        body, (h, gain, wint, b_in), name="inproj_fwd", grid=(t // TM,),
        out_shape=(jax.ShapeDtypeStruct((t, D_MODEL), BF16), jax.ShapeDtypeStruct((nseq, ZA_W // 128, seq, 128), F32),
                   jax.ShapeDtypeStruct((t, ZB_W), BF16), jax.ShapeDtypeStruct((t, 2 * D_MODEL), BF16)),
        in_specs=[row(D_MODEL), pl.BlockSpec((1, D_MODEL), lambda i: (0, 0)), ANY,
                  pl.BlockSpec((1, D_IN), lambda i: (0, 0))],
        out_specs=(row(D_MODEL), _lane_blocks(nseq, seq, ZA_W // 128), row(ZB_W), row(2 * D_MODEL)),
        scratch_shapes=[pltpu.VMEM((D_IN, D_MODEL), BF16)], sem=("arbitrary",), vmem=VMEM_BIG, rider=rider)


def _inproj_bwd(dz, dh2, h, gain, wint, rider=None):
    t = h.shape[0]
    nc = 5
    cw = D_IN // nc

    def body(dz_ref, dh2_ref, h_ref, gain_ref, w_hbm, dh_ref, gg_ref, w_v):
        @pl.when(pl.program_id(0) == 0)
        def _():
            pltpu.sync_copy(w_hbm, w_v)
            gg_ref[...] = jnp.zeros_like(gg_ref)

        du = jnp.zeros((TM, D_MODEL), F32)
        for ci in range(nc):
            sl = slice(ci * cw, (ci + 1) * cw)
            du = du + _dot(dz_ref[:, sl], w_v[sl, :])
        hh = h_ref[...]
        r = lax.rsqrt(jnp.mean(hh * hh, axis=-1, keepdims=True) + EPS)
        hn = hh * r
        gg_ref[...] += jnp.sum(du * hn, axis=0, keepdims=True)
        dng = du * gain_ref[...]
        dh_ref[...] = dh2_ref[...] + r * (dng - hn * jnp.mean(dng * hn, axis=-1, keepdims=True))

    row = lambda w: pl.BlockSpec((TM, w), lambda i: (i, 0))
    vec = pl.BlockSpec((1, D_MODEL), lambda i: (0, 0))
    return _pallas(
        body, (dz, dh2, h, gain, wint), name="inproj_bwd", grid=(t // TM,),
        out_shape=(jax.ShapeDtypeStruct((t, D_MODEL), F32), jax.ShapeDtypeStruct((1, D_MODEL), F32)),
        in_specs=[row(D_IN), row(D_MODEL), row(D_MODEL), vec, ANY],
        out_specs=(row(D_MODEL), vec),
        scratch_shapes=[pltpu.VMEM((D_IN, D_MODEL), BF16)], sem=("arbitrary",), vmem=VMEM_BIG, rider=rider)


def _head_sums(x):
    w = x.shape[1]
    i = lax.broadcasted_iota(jnp.int32, (w, w), 0) // HEAD_DIM
    j = lax.broadcasted_iota(jnp.int32, (w, w), 1) // HEAD_DIM
    ones = (i == j).astype(BF16)
    hi = x.astype(BF16)
    r1 = x - hi.astype(F32)
    mid = r1.astype(BF16)
    lo = (r1 - mid.astype(F32)).astype(BF16)
    return _dot(hi, ones) + _dot(mid, ones) + _dot(lo, ones)


def _merge_fwd(o0, o1, o2, l0, l1, l2, yb, zg, h1, wat, wbt, wout):
    t = h1.shape[0]
    nseq, _, seq, _ = o0.shape

    def body(o0_ref, o1_ref, o2_ref, l0_ref, l1_ref, l2_ref, yb_ref, ga_ref, gb_ref, h1_ref, wa_ref, wb_ref, wo_ref,
             h2_ref, y_ref, lt_ref, pa_ref, pb_ref, mg_ref):
        wide = lambda ref: jnp.concatenate([ref[0, 0], ref[0, 1]], axis=1)
        la, lb, lc = wide(l0_ref), wide(l1_ref), wide(l2_ref)
        mx = jnp.maximum(jnp.maximum(la, lb), lc)
        ea, eb, ec = jnp.exp(la - mx), jnp.exp(lb - mx), jnp.exp(lc - mx)
        den = ea + eb + ec
        y = (ea * wide(o0_ref) + eb * wide(o1_ref) + ec * wide(o2_ref)) / den
        lt = mx + jnp.log(den)
        lt_ref[0, 0] = lt[:, :128]
        lt_ref[0, 1] = lt[:, 128:]
        yb16 = y.astype(BF16)
        y_ref[...] = yb16
        pa = _dot_nt(yb16, wa_ref[...])
        pb = _dot_nt(yb_ref[...], wb_ref[...])
        pa_ref[...] = pa.astype(BF16)
        pb_ref[...] = pb.astype(BF16)
        mg = (_sigmoid(ga_ref[...].astype(F32)) * pa + _sigmoid(gb_ref[...].astype(F32)) * pb).astype(BF16)
        mg_ref[...] = mg
        h2_ref[...] = h1_ref[...] + _dot(mg, wo_ref[...])

    row = lambda w: pl.BlockSpec((TM, w), lambda i: (i, 0))
    full = lambda a: pl.BlockSpec(a.shape, lambda i: (0, 0))
    gate = lambda cb: pl.BlockSpec((TM, D_MODEL), lambda i: (i, cb))
    return pl.pallas_call(
        body, name="merge_fwd", grid=(t // TM,),
        out_shape=(jax.ShapeDtypeStruct((t, D_MODEL), F32), jax.ShapeDtypeStruct((t, GW), BF16),
                   jax.ShapeDtypeStruct((nseq, 2, seq, 128), F32), jax.ShapeDtypeStruct((t, D_MODEL), BF16),
                   jax.ShapeDtypeStruct((t, D_MODEL), BF16), jax.ShapeDtypeStruct((t, D_MODEL), BF16)),
        in_specs=[_lane_blocks(nseq, seq, 2)] * 6 + [row(2 * GW), gate(0), gate(1), row(D_MODEL), full(wat), full(wbt),
                                                     full(wout)],
        out_specs=(row(D_MODEL), row(GW), _lane_blocks(nseq, seq, 2), row(D_MODEL), row(D_MODEL), row(D_MODEL)),
        compiler_params=_params(("parallel",), VMEM_BIG),
    )(o0, o1, o2, l0, l1, l2, yb, zg, zg, h1, wat, wbt, wout)


def _merge_bwd(dh2, pa, pb, zg, y, yb, wat, wbt, wout, nseq, rider=None):
    t = dh2.shape[0]

    def body(dh2_ref, pa_ref, pb_ref, ga_ref, gb_ref, y_ref, yb_ref, wa_ref, wb_ref, wo_ref,
             dpa_ref, dpb_ref, dga_ref, dgb_ref, dya_ref, dyb_ref, dh2b_ref, ca_ref, cb_ref):
        d16 = dh2_ref[...].astype(BF16)
        dh2b_ref[...] = d16
        dm = _dot_nt(d16, wo_ref[...])
        sa = _sigmoid(ga_ref[...].astype(F32))
        sb = _sigmoid(gb_ref[...].astype(F32))
        dpa = (dm * sa).astype(BF16)
        dpb = (dm * sb).astype(BF16)
        dpa_ref[...] = dpa
        dpb_ref[...] = dpb
        dga_ref[...] = (dm * pa_ref[...].astype(F32) * sa * (1.0 - sa)).astype(BF16)
        dgb_ref[...] = (dm * pb_ref[...].astype(F32) * sb * (1.0 - sb)).astype(BF16)
        dya = _dot(dpa, wa_ref[...])
        dyb = _dot(dpb, wb_ref[...])
        dya_ref[0, 0] = dya[:, :128]
        dya_ref[0, 1] = dya[:, 128:]
        dyb_ref[...] = dyb.astype(BF16)
        ca = _head_sums(dya * y_ref[...].astype(F32))
        ca_ref[0, 0] = ca[:, :128]
        ca_ref[0, 1] = ca[:, 128:]
        cb_ref[...] = _head_sums(dyb * yb_ref[...].astype(F32))

    row = lambda w: pl.BlockSpec((TM, w), lambda i: (i, 0))
    full = lambda a: pl.BlockSpec(a.shape, lambda i: (0, 0))
    gate = lambda cb: pl.BlockSpec((TM, D_MODEL), lambda i: (i, cb))
    bf = lambda w: jax.ShapeDtypeStruct((t, w), BF16)
    lanes = jax.ShapeDtypeStruct((nseq, 2, t // nseq, 128), F32)
    lane_spec = _lane_blocks(nseq, t // nseq, 2)
    return _pallas(
        body, (dh2, pa, pb, zg, zg, y, yb, wat, wbt, wout), name="merge_bwd", grid=(t // TM,),
        out_shape=(bf(D_MODEL), bf(D_MODEL), bf(D_MODEL), bf(D_MODEL), lanes, bf(2 * GW), bf(D_MODEL),
                   lanes, jax.ShapeDtypeStruct((t, 2 * GW), F32)),
        in_specs=[row(D_MODEL), row(D_MODEL), row(D_MODEL), gate(0), gate(1), row(GW), row(2 * GW),
                  full(wat), full(wbt), full(wout)],
        out_specs=(row(D_MODEL), row(D_MODEL), row(D_MODEL), row(D_MODEL), lane_spec, row(2 * GW), row(D_MODEL),
                   lane_spec, row(2 * GW)),
        sem=("parallel",), vmem=VMEM_BIG, rider=rider)


def _loss_head(h3, gain, tgt):
    t = h3.shape[0]

    def body(h_ref, gain_ref, t_ref, dh_ref, loss_ref, gg_ref):
        @pl.when(pl.program_id(0) == 0)
        def _():
            loss_ref[...] = jnp.zeros_like(loss_ref)
            gg_ref[...] = jnp.zeros_like(gg_ref)

        hh = h_ref[...]
        r = lax.rsqrt(jnp.mean(hh * hh, axis=-1, keepdims=True) + EPS)
        hn = hh * r
        err = hn * gain_ref[...] - t_ref[...]
        part = jnp.sum(jnp.sum(err * err, axis=1, keepdims=True), axis=0, keepdims=True)
        loss_ref[...] += (0.5 / D_MODEL) * part
        dy = err * (1.0 / D_MODEL)
        gg_ref[...] += jnp.sum(dy * hn, axis=0, keepdims=True)
        dng = dy * gain_ref[...]
        dh_ref[...] = r * (dng - hn * jnp.mean(dng * hn, axis=-1, keepdims=True))

    row = pl.BlockSpec((TM, D_MODEL), lambda i: (i, 0))
    vec = pl.BlockSpec((1, D_MODEL), lambda i: (0, 0))
    return pl.pallas_call(
        body, name="loss_head", grid=(t // TM,),
        out_shape=(jax.ShapeDtypeStruct((t, D_MODEL), F32), jax.ShapeDtypeStruct((8, 128), F32),
                   jax.ShapeDtypeStruct((1, D_MODEL), F32)),
        in_specs=[row, vec, row], out_specs=(row, pl.BlockSpec((8, 128), lambda i: (0, 0)), vec),
        compiler_params=_params(("arbitrary",)),
    )(h3, gain, tgt)


def _lane_head(rows):
    return lax.broadcasted_iota(jnp.int32, (rows, GW), 1) // HEAD_DIM


def _kv_expand_matrix(r):
    ci = lax.broadcasted_iota(jnp.int32, (2 * HEAD_DIM, GW), 0)
    ji = lax.broadcasted_iota(jnp.int32, (2 * HEAD_DIM, GW), 1)
    return (ci == (ji % HEAD_DIM) + HEAD_DIM * r).astype(BF16)


def _block_rows(row0, stride, ib):
    start = row0 + (stride * BLOCK) * ib
    if stride > 1:
        return pl.ds(start, BLOCK, stride=stride)
    return pl.ds(pl.multiple_of(start, BLOCK), BLOCK)


def _stack_heads(x, lane_head):
    return jnp.concatenate([jnp.where(lane_head == h, x, jnp.zeros_like(x)) for h in range(4)], axis=0)


def _unstack_heads(x4, lane_head):
    out = jnp.zeros((BLOCK, GW), F32)
    for h in range(4):
        out = jnp.where(lane_head == h, x4[h * BLOCK:(h + 1) * BLOCK], out)
    return out


def _load_rows(ref, rows, split):
    if split:
        return jnp.concatenate([ref[0, 0, rows, :], ref[0, 1, rows, :]], axis=1)
    return ref[0, rows, :]


def _store_rows(ref, rows, val, split):
    if split:
        ref[0, 0, rows, :] = val[:, :128]
        ref[0, 1, rows, :] = val[:, 128:]
    else:
        ref[0, rows, :] = val


def _attn_fwd(q_arr, k_arr, v_arr, bias, sink, *, grid, seq, stride, kvw, split, q_spec, k_spec, v_spec, bias_map,
              sink_map, o_spec, has_sink, o_shape, o_dtype, name, rider=None):
    nb = seq // stride // BLOCK
    scale = HEAD_DIM ** -0.5
    expanded = kvw != GW
    rps = RESIDUES_PER_STEP if stride >= 4 * RESIDUES_PER_STEP else 1
    grid = (grid[0], grid[1] // rps)
    assert not has_sink or B_WINDOW - 1 < BLOCK

    def body(q_ref, k_ref, v_ref, bias_ref, sink_ref, o_ref, lse_ref, *kv_x):
        rr = pl.program_id(1)
        lane_head = _lane_head(BLOCK)
        if expanded:
            expand = _kv_expand_matrix(rr)
            kv_x[0][...] = _dot(k_ref[0], expand).astype(BF16)
            kv_x[1][...] = _dot(v_ref[0], expand).astype(BF16)
        for j in range(rps):
            residue(rr * rps + j if stride > 1 else 0, q_ref, k_ref, v_ref, bias_ref, sink_ref, o_ref, lse_ref, kv_x,
                    lane_head)

    def residue(row0, q_ref, k_ref, v_ref, bias_ref, sink_ref, o_ref, lse_ref, kv_x, lane_head):
        def per_head(fn, x):
            return jnp.concatenate([fn(sink_ref[0, h:h + 1, 0:1], x[h * BLOCK:(h + 1) * BLOCK]) for h in range(4)],
                                   axis=0)

        def load(ref, ib):
            return _load_rows(ref, _block_rows(row0, stride, ib), split).astype(BF16)

        def load_kv(which, ib):
            if expanded:
                return kv_x[which][_block_rows(0, 1, ib), :]
            return load((k_ref, v_ref)[which], ib)

        def block(ib, first):
            q4 = _stack_heads(load(q_ref, ib), lane_head)
            if first:
                kc, vc = load_kv(0, ib), load_kv(1, ib)
                b4 = bias_ref[:, :, BLOCK:].reshape(4 * BLOCK, BLOCK)
            else:
                kc = jnp.concatenate([load_kv(0, ib - 1), load_kv(0, ib)], axis=0)
                vc = jnp.concatenate([load_kv(1, ib - 1), load_kv(1, ib)], axis=0)
                b4 = bias_ref[...].reshape(4 * BLOCK, 2 * BLOCK)
                if has_sink:
                    oldest = lax.broadcasted_iota(jnp.int32, kc.shape, 0) == 0
                    kc = jnp.where(oldest, jnp.zeros_like(kc), kc)
                    vc = jnp.where(oldest, jnp.zeros_like(vc), vc)
            s = _dot_nt(q4, kc) * scale + b4
            m = jnp.max(s, axis=-1, keepdims=True)
            if has_sink and first:
                m = per_head(jnp.maximum, m)
            p = jnp.exp(s - m)
            l = jnp.sum(p, axis=-1, keepdims=True)
            if has_sink and first:
                l = l + per_head(lambda sk, mh: jnp.exp(sk - mh), m)
            o4 = _dot(p.astype(BF16), vc) / l
            rows = _block_rows(row0, stride, ib)
            _store_rows(o_ref, rows, _unstack_heads(o4, lane_head).astype(o_dtype), split)
            _store_rows(lse_ref, rows, _unstack_heads(m + jnp.log(l), lane_head), split)

        block(0, True)
        if nb > 1:
            def step(i, carry):
                block(i, False)
                return carry
            lax.fori_loop(1, nb, step, 0)

    return _pallas(
        body, (q_arr, k_arr, v_arr, bias, sink), name=name, grid=grid,
        out_shape=(jax.ShapeDtypeStruct(o_shape, o_dtype), jax.ShapeDtypeStruct(o_shape, F32)),
        in_specs=[q_spec, k_spec, v_spec,
                  pl.BlockSpec((4, BLOCK, 2 * BLOCK), bias_map), pl.BlockSpec((1, 4, 128), sink_map)],
        out_specs=(o_spec, o_spec),
        scratch_shapes=[pltpu.VMEM((seq, GW), BF16)] * 2 if expanded else [],
        sem=("arbitrary", "arbitrary"), vmem=VMEM_BIG, rider=rider)


def _attn_bwd(q_arr, k_arr, v_arr, bias, sink, dy, cc, lse, *, grid, seq, stride, kvw, split, q_spec, k_spec, v_spec,
              bias_map, sink_map, o_spec, kv_out_spec, has_sink, n_bias, dq_shape, dkv_shape, g_dtype, name):
    ln = seq // stride
    nb = ln // BLOCK
    scale = HEAD_DIM ** -0.5
    expanded = kvw != GW
    rps = RESIDUES_PER_STEP if stride >= 4 * RESIDUES_PER_STEP else 1
    grid = (grid[0], grid[1] // rps)

    def body(q_ref, k_ref, v_ref, bias_ref, sink_ref, dy_ref, c_ref, lse_ref,
             dq_ref, dk_ref, dv_ref, db_ref, dsk_ref, dk_acc, dv_acc, dk_half, dv_half, *kv_x):
        rr = pl.program_id(1)

        @pl.when((pl.program_id(0) == 0) & (rr == 0))
        def _():
            db_ref[...] = jnp.zeros_like(db_ref)
            dsk_ref[...] = jnp.zeros_like(dsk_ref)

        if expanded:
            expand = _kv_expand_matrix(rr)
            kv_x[0][...] = _dot(k_ref[0], expand).astype(BF16)
            kv_x[1][...] = _dot(v_ref[0], expand).astype(BF16)
        refs = (q_ref, k_ref, v_ref, bias_ref, sink_ref, dy_ref, c_ref, lse_ref, dq_ref, dk_ref, dv_ref, db_ref,
                dsk_ref, dk_acc, dv_acc, dk_half, dv_half, kv_x)
        for j in range(rps):
            residue(rr, rr * rps + j if stride > 1 else 0, *refs)

    def residue(rr, row0, q_ref, k_ref, v_ref, bias_ref, sink_ref, dy_ref, c_ref, lse_ref,
                dq_ref, dk_ref, dv_ref, db_ref, dsk_ref, dk_acc, dv_acc, dk_half, dv_half, kv_x):
        dk_acc[...] = jnp.zeros_like(dk_acc)
        dv_acc[...] = jnp.zeros_like(dv_acc)
        lane_head = _lane_head(BLOCK)
        hb = 4 * rr if n_bias == 8 else 0

        def load(ref, ib):
            return _load_rows(ref, _block_rows(row0, stride, ib), split)

        def load_kv(which, ib):
            if expanded:
                return kv_x[which][_block_rows(0, 1, ib), :]
            return load((k_ref, v_ref)[which], ib).astype(BF16)

        def head_col(x):
            return jnp.concatenate([x[:, h * HEAD_DIM:h * HEAD_DIM + 1] for h in range(4)], axis=0)

        def block(ib, first):
            q4 = _stack_heads(load(q_ref, ib).astype(BF16), lane_head)
            dy4 = _stack_heads(load(dy_ref, ib).astype(BF16), lane_head)
            c4 = head_col(load(c_ref, ib))
            l4 = head_col(load(lse_ref, ib))
            if first:
                kc, vc = load_kv(0, ib), load_kv(1, ib)
                b4 = bias_ref[:, :, BLOCK:].reshape(4 * BLOCK, BLOCK)
                krows = pl.ds(0, BLOCK)
            else:
                kc = jnp.concatenate([load_kv(0, ib - 1), load_kv(0, ib)], axis=0)
                vc = jnp.concatenate([load_kv(1, ib - 1), load_kv(1, ib)], axis=0)
                b4 = bias_ref[...].reshape(4 * BLOCK, 2 * BLOCK)
                krows = pl.ds(pl.multiple_of((ib - 1) * BLOCK, BLOCK), 2 * BLOCK)
            nk = BLOCK if first else 2 * BLOCK
            p = jnp.exp(_dot_nt(q4, kc) * scale + b4 - l4)
            ds = p * (_dot_nt(dy4, vc) - c4)
            ds3 = ds.reshape(4, BLOCK, nk)
            if n_bias == 8:
                if first:
                    db_ref[pl.ds(hb, 4), :, BLOCK:] += ds3
                else:
                    db_ref[pl.ds(hb, 4)] += ds3
            elif first:
                db_ref[:, :, BLOCK:] += ds3
            else:
                db_ref[...] += ds3
            ds16 = ds.astype(BF16)
            dq = _unstack_heads(_dot(ds16, kc), lane_head) * scale
            _store_rows(dq_ref, _block_rows(row0, stride, ib), dq.astype(g_dtype), split)
            dk_acc[krows, :] += _dot_tn(ds16, q4) * scale
            dv_acc[krows, :] += _dot_tn(p.astype(BF16), dy4)
            if has_sink:
                for h in range(4):
                    hs = slice(h * BLOCK, (h + 1) * BLOCK)
                    sk = sink_ref[0, h:h + 1, 0:1]
                    val = -jnp.sum(jnp.exp(sk - l4[hs]) * c4[hs], axis=0, keepdims=True)
                    dsk_ref[hb + h] += jnp.broadcast_to(val, (8, 128))

        block(0, True)
        if nb > 1:
            def step(i, carry):
                block(i, False)
                return carry
            lax.fori_loop(1, nb, step, 0)

        if kvw == GW:
            all_rows = pl.ds(row0, ln, stride=stride) if stride > 1 else pl.ds(0, ln)
            _store_rows(dk_ref, all_rows, dk_acc[...].astype(g_dtype), split)
            _store_rows(dv_ref, all_rows, dv_acc[...].astype(g_dtype), split)
        else:
            def fold(acc):
                t2 = acc[:, :2 * HEAD_DIM] + acc[:, 2 * HEAD_DIM:]
                t2 = t2 + pltpu.roll(t2, HEAD_DIM, 1)
                lane = lax.broadcasted_iota(jnp.int32, t2.shape, 1) // HEAD_DIM
                return jnp.where(lane == rr, t2, 0.0)

            @pl.when(rr == 0)
            def _():
                dk_half[...] = fold(dk_acc[...])
                dv_half[...] = fold(dv_acc[...])

            @pl.when(rr == 1)
            def _():
                dk_ref[0] = (dk_half[...] + fold(dk_acc[...])).astype(g_dtype)
                dv_ref[0] = (dv_half[...] + fold(dv_acc[...])).astype(g_dtype)

    return pl.pallas_call(
        body, name=name, grid=grid,
        out_shape=(jax.ShapeDtypeStruct(dq_shape, g_dtype), jax.ShapeDtypeStruct(dkv_shape, g_dtype),
                   jax.ShapeDtypeStruct(dkv_shape, g_dtype), jax.ShapeDtypeStruct((n_bias, BLOCK, 2 * BLOCK), F32),
                   jax.ShapeDtypeStruct((8, 8, 128), F32)),
        in_specs=[q_spec, k_spec, v_spec,
                  pl.BlockSpec((4, BLOCK, 2 * BLOCK), bias_map), pl.BlockSpec((1, 4, 128), sink_map),
                  o_spec, o_spec, o_spec],
        out_specs=(o_spec, kv_out_spec, kv_out_spec,
                   pl.BlockSpec((n_bias, BLOCK, 2 * BLOCK), lambda n, r: (0, 0, 0)),
                   pl.BlockSpec((8, 8, 128), lambda n, r: (0, 0, 0))),
        scratch_shapes=[pltpu.VMEM((ln, GW), F32), pltpu.VMEM((ln, GW), F32),
                        pltpu.VMEM((ln, 2 * HEAD_DIM), F32), pltpu.VMEM((ln, 2 * HEAD_DIM), F32)]
        + ([pltpu.VMEM((seq, GW), BF16)] * 2 if expanded else []),
        compiler_params=_params(("arbitrary", "arbitrary"), VMEM_BIG),
    )(q_arr, k_arr, v_arr, bias, sink, dy, cc, lse)


def _bias_grad(ds_all, buckets):
    def body(ds_ref, bk_ref, o_ref):
        rows = lax.broadcasted_iota(jnp.int32, (N_BUCKETS, 128), 0)
        cols = lax.broadcasted_iota(jnp.int32, (N_BUCKETS, 128), 1)

        def per_bucket(b, acc):
            for h in range(20):
                gi = h // 4 if h < 12 else 3
                v = jnp.where(bk_ref[gi] == b, ds_ref[h], 0.0)
                v = jnp.sum(jnp.sum(v, axis=1, keepdims=True), axis=0, keepdims=True)
                acc = jnp.where((rows == b) & (cols == h), v, acc)
            return acc

        o_ref[...] = lax.fori_loop(0, N_BUCKETS, per_bucket, jnp.zeros((N_BUCKETS, 128), F32))

    vm = pl.BlockSpec(memory_space=pltpu.VMEM)
    return pl.pallas_call(body, name="bias_grad", out_shape=jax.ShapeDtypeStruct((N_BUCKETS, 128), F32),
                          in_specs=[vm, vm], out_specs=vm)(ds_all, buckets)


def _adamw(w, g, m, v, name):
    r, c = w.shape
    tr = r
    for cand in (256, 176, 128, 64, 32, 16, 8):
        if r % cand == 0:
            tr = cand
            break
    bc1 = 1.0 - ADAM_B1 ** ADAM_STEP
    bc2 = 1.0 - ADAM_B2 ** ADAM_STEP

    def body(w_ref, g_ref, m_ref, v_ref, d_ref, nm_ref, nv_ref):
        gv = g_ref[...]
        nm = ADAM_B1 * m_ref[...] + (1.0 - ADAM_B1) * gv
        nv = ADAM_B2 * v_ref[...] + (1.0 - ADAM_B2) * (gv * gv)
        nm_ref[...] = nm
        nv_ref[...] = nv
        d_ref[...] = -ADAM_LR * ((nm / bc1) / (jnp.sqrt(nv / bc2) + ADAM_EPS) + ADAM_WD * w_ref[...])

    spec = pl.BlockSpec((tr, c), lambda i: (i, 0))
    shp = jax.ShapeDtypeStruct((r, c), F32)
    return pl.pallas_call(body, name=name, grid=(r // tr,), out_shape=(shp, shp, shp),
                          in_specs=[spec] * 4, out_specs=(spec, spec, spec),
                          compiler_params=_params(("parallel",)))(w, g, m, v)


def _t5_bucket(dist):
    max_exact = N_BUCKETS // 2
    n = jnp.maximum(dist, 0)
    nf = jnp.maximum(n, 1).astype(F32)
    large = max_exact + (jnp.log(nf / max_exact) / math.log(MAX_DISTANCE / max_exact)
                         * (N_BUCKETS - max_exact)).astype(jnp.int32)
    large = jnp.minimum(large, N_BUCKETS - 1)
    return jnp.where(n < max_exact, n, large)


def _bias_tables(rel_bias):
    qi = jnp.arange(BLOCK)[:, None]
    ki = jnp.arange(2 * BLOCK)[None, :]
    dist = qi + BLOCK - ki
    specs = [(d, w // d, 4 * gi, 4 * gi + 4) for gi, (w, d) in enumerate(DIL_GROUPS)] + [(1, B_WINDOW - 1, 12, 20)]
    biases, buckets = [], []
    for stride, steps, h0, h1 in specs:
        valid = (dist >= 0) & (dist <= steps)
        bk = jnp.where(valid, _t5_bucket(dist * stride), -1).astype(jnp.int32)
        onehot = (bk[None, :, :] == jnp.arange(N_BUCKETS, dtype=jnp.int32)[:, None, None]).astype(F32)
        b = jnp.einsum("bqk,bh->hqk", onehot, rel_bias[:, h0:h1], precision=lax.Precision.HIGHEST)
        biases.append(jnp.where(valid[None], b, NEG))
        buckets.append(bk)
    return jnp.concatenate(biases, axis=0), jnp.stack(buckets, axis=0)


def _local_step(x, tgt, W, S, shards=None):
    nseq, seq, _ = x.shape
    t = nseq * seq
    xf = x.reshape(t, D_MODEL)
    bias_all, buckets = _bias_tables(S["rel_bias"])
    sink_b = jnp.broadcast_to(S["sinks"].reshape(2, 4, 1), (2, 4, 128)).astype(F32)
    sink_0 = jnp.zeros((1, 4, 128), F32)
    dist = shards is not None
    W = dict(W)
    G, GS, reduced = {}, {}, {}

    def put(keys, gathered):
        for k, g in zip(keys, gathered):
            W[k] = g.reshape(_FULL_SHAPE.get(k, (N_CHIPS * shards[k].shape[0], D_MODEL)))

    def gather_rider(keys):
        return _GatherRider([shards[k] for k in keys]) if dist else None

    def pair(keys):
        return _pair_reduce([G[k].reshape(N_CHIPS, 2, shards[k].shape[0] // 2, D_MODEL) for k in keys],
                            "grad_pair_reduce_" + keys[0])

    def finish(keys, own, rec):
        full = _final_reduce(own, rec, "grad_final_reduce_" + keys[0])
        off = 0
        for k in keys:
            r = shards[k].shape[0]
            reduced[k] = full[:, off:off + r // 2].reshape(r, D_MODEL)
            off += r // 2

    if dist:
        first = ("wgt1", "wut1", "wd1")
        put(first, _gather_rows([shards[k] for k in first]))
    keys = ("wint", "wgt2")
    (h1, n1, g1, u1), ro = _ffn_fwd(xf, S["ffn1_norm"], W["wgt1"], W["wut1"], W["wd1"], rider=gather_rider(keys))
    put(keys, ro)
    keys = ("wout", "wat", "wbt", "wut2")
    (un, za, zb, zg), ro = _inproj_fwd(h1, S["mix_norm"], W["wint"], S["b_in"], nseq, rider=gather_rider(keys))
    put(keys, ro)

    seq3 = lambda a: a.reshape(nseq, seq, a.shape[-1])
    zb3 = seq3(zb)
    pair_blk = lambda cb: pl.BlockSpec((1, 2, seq, 128), lambda n, r, cb=cb: (n, cb, 0, 0))
    a_cfg = []
    outs, lses = [], []
    for gi, (_, d) in enumerate(DIL_GROUPS):
        cfg = dict(grid=(nseq, d), seq=seq, stride=d, kvw=GW, split=True,
                   q_spec=pair_blk(gi), k_spec=pair_blk(3 + gi), v_spec=pair_blk(6 + gi), o_spec=pair_blk(0),
                   bias_map=lambda n, r: (0, 0, 0), sink_map=lambda n, r: (0, 0, 0), has_sink=False)
        a_cfg.append(cfg)
        (o, lse), _ = _attn_fwd(za, za, za, bias_all[4 * gi:4 * gi + 4], sink_0, o_shape=(nseq, 2, seq, 128),
                                o_dtype=F32, name=f"attn_a{gi}_fwd", **cfg)
        outs.append(o)
        lses.append(lse)
    wide_blk = lambda w, cmap: pl.BlockSpec((1, seq, w), cmap)
    b_cfg = dict(grid=(nseq, 2), seq=seq, stride=1, kvw=2 * HEAD_DIM, split=False,
                 q_spec=wide_blk(GW, lambda n, r: (n, 0, r)), k_spec=wide_blk(2 * HEAD_DIM, lambda n, r: (n, 0, 4)),
                 v_spec=wide_blk(2 * HEAD_DIM, lambda n, r: (n, 0, 5)), o_spec=wide_blk(GW, lambda n, r: (n, 0, r)),
                 bias_map=lambda n, r: (r, 0, 0), sink_map=lambda n, r: (r, 0, 0), has_sink=True)
    keys = ("wd2",)
    bias_b_fwd = bias_all[12:20].at[:, :, 0].set(jnp.broadcast_to(S["sinks"].reshape(8, 1), (8, BLOCK)))
    (yb, lse_b), ro = _attn_fwd(zb3, zb3, zb3, bias_b_fwd, sink_b, o_shape=(nseq, seq, 2 * GW), o_dtype=BF16,
                                name="attn_b_fwd", rider=gather_rider(keys), **b_cfg)
    put(keys, ro)
    yb = yb.reshape(t, 2 * GW)

    h2, y, lse_tot, pa, pb, merged = _merge_fwd(outs[0], outs[1], outs[2], lses[0], lses[1], lses[2], yb, zg, h1,
                                                W["wat"], W["wbt"], W["wout"])
    (h3, n2, g2, u2), _ = _ffn_fwd(h2, S["ffn2_norm"], W["wgt2"], W["wut2"], W["wd2"])
    dh3, loss_part, g_final = _loss_head(h3, S["final_norm"].reshape(1, D_MODEL), tgt.reshape(t, D_MODEL))

    GS["final_norm"] = g_final
    dh2, dg2, du2, a2, df2, GS["ffn2_norm"] = _ffn_bwd(dh3, h2, S["ffn2_norm"], g2, u2, W["wgt2"], W["wut2"], W["wd2"])
    G["wgt2"] = _wgrad(dg2, n2, D_FF, name="wgrad_gate2")
    G["wut2"] = _wgrad(du2, n2, D_FF, name="wgrad_up2")
    G["wd2"] = _wgrad(a2, df2, D_FF, name="wgrad_down2")

    keys = ("wgt2", "wut2", "wd2")
    rider = _ExchangeRider([pair(keys)]) if dist else None
    (dpa, dpb, dga, dgb, dya, dyb, dh2b, ca, cb), ro = _merge_bwd(dh2, pa, pb, zg, y, yb, W["wat"], W["wbt"], W["wout"],
                                                                  nseq, rider=rider)
    if dist:
        finish(keys, *ro)
    G["wout"] = _wgrad(merged, dh2b, D_MODEL, name="wgrad_out")
    G["wat"] = _wgrad(dpa, y, D_MODEL, name="wgrad_branch_a")
    G["wbt"] = _wgrad(dpb, yb, D_MODEL, name="wgrad_branch_b")

    dqs, dks, dvs, dbs = [], [], [], []
    shp = (nseq, 2, seq, 128)
    halves = lambda a: [a[:, hf].reshape(t, 128).astype(BF16) for hf in range(2)]
    for gi in range(len(DIL_GROUPS)):
        dq, dk, dv, db, _ = _attn_bwd(za, za, za, bias_all[4 * gi:4 * gi + 4], sink_0, dya, ca, lse_tot,
                                      n_bias=4, dq_shape=shp, dkv_shape=shp, g_dtype=F32,
                                      kv_out_spec=a_cfg[gi]["o_spec"], name=f"attn_a{gi}_bwd", **a_cfg[gi])
        dqs += halves(dq)
        dks += halves(dk)
        dvs += halves(dv)
        dbs.append(db)
    dqb, dkb, dvb, dbb, dsink = _attn_bwd(zb3, zb3, zb3, bias_all[12:20], sink_b, seq3(dyb), seq3(cb), lse_b,
                                          n_bias=8, dq_shape=(nseq, seq, 2 * GW),
                                          dkv_shape=(nseq, seq, 2 * HEAD_DIM), g_dtype=BF16,
                                          kv_out_spec=wide_blk(2 * HEAD_DIM, lambda n, r: (n, 0, 0)),
                                          name="attn_b_bwd", **b_cfg)
    dz = jnp.concatenate(dqs + dks + dvs + [dqb.reshape(t, 2 * GW), dkb.reshape(t, 2 * HEAD_DIM),
                                            dvb.reshape(t, 2 * HEAD_DIM), dga, dgb], axis=-1)
    gb_tab = _bias_grad(jnp.concatenate(dbs + [dbb], axis=0), buckets)
    GS["rel_bias"] = gb_tab[:, :20]
    GS["sinks"] = dsink[:, 0, 0].reshape(1, 8)

    G["wint"], GS["b_in"] = _wgrad(dz, un, D_IN // 2, with_colsum=True, name="wgrad_in")
    keys = ("wint", "wout", "wat", "wbt")
    rider = _ExchangeRider([pair(keys)]) if dist else None
    (dh1, GS["mix_norm"]), ro = _inproj_bwd(dz, dh2, h1, S["mix_norm"], W["wint"], rider=rider)
    if dist:
        finish(keys, *ro)

    dx, dg1, du1, a1, df1, GS["ffn1_norm"] = _ffn_bwd(dh1, xf, S["ffn1_norm"], g1, u1, W["wgt1"], W["wut1"], W["wd1"])
    G["wgt1"] = _wgrad(dg1, n1, D_FF, name="wgrad_gate1")
    if dist:
        G["wut1"], ro = _wgrad(du1, n1, D_FF, name="wgrad_up1", rider=_ExchangeRider([pair(("wgt1",))]))
        finish(("wgt1",), *ro)
        G["wd1"], ro = _wgrad(a1, df1, D_FF, name="wgrad_down1", rider=_ExchangeRider([pair(("wut1",))]))
        finish(("wut1",), *ro)
        finish(("wd1",), *_chip_exchange([pair(("wd1",))]))
    else:
        G["wut1"] = _wgrad(du1, n1, D_FF, name="wgrad_up1")
        G["wd1"] = _wgrad(a1, df1, D_FF, name="wgrad_down1")
    return loss_part, dx.reshape(x.shape), (reduced if dist else G), GS


_SMALL = ("ffn1_norm", "mix_norm", "ffn2_norm", "final_norm", "b_in", "sinks", "rel_bias")
_ORDER = ("ffn1_norm", "ffn1_w_gate", "ffn1_w_up", "ffn1_w_down", "mix_norm", "w_in", "b_in", "w_branch_a",
          "w_branch_b", "w_out", "sinks", "rel_bias", "ffn2_norm", "ffn2_w_gate", "ffn2_w_up", "ffn2_w_down",
          "final_norm")
_BIG = (("wgt1", "ffn1_w_gate", True, 704), ("wut1", "ffn1_w_up", True, 704), ("wd1", "ffn1_w_down", False, 704),
        ("wint", "w_in", True, 1280), ("wout", "w_out", False, 256), ("wat", "w_branch_a", True, 64),
        ("wbt", "w_branch_b", True, 128), ("wgt2", "ffn2_w_gate", True, 704), ("wut2", "ffn2_w_up", True, 704),
        ("wd2", "ffn2_w_down", False, 704))
_FULL_SHAPE = {"wat": (D_MODEL, GW), "wbt": (D_MODEL, 2 * GW)}


def _pack_small(p, extra=None):
    last = [p["sinks"].reshape(8), p["rel_bias"].reshape(640)]
    used = 648
    if extra is not None:
        last.append(extra.reshape(1))
        used += 1
    last.append(jnp.zeros((D_MODEL - used,), F32))
    rows = [p["ffn1_norm"].reshape(1, D_MODEL), p["mix_norm"].reshape(1, D_MODEL), p["ffn2_norm"].reshape(1, D_MODEL),
            p["final_norm"].reshape(1, D_MODEL), p["b_in"].reshape(5, D_MODEL), jnp.concatenate(last).reshape(1, D_MODEL),
            jnp.zeros((6, D_MODEL), F32)]
    return jnp.concatenate(rows, axis=0)


def _unpack_small(a):
    return {"ffn1_norm": a[0:1], "mix_norm": a[1:2], "ffn2_norm": a[2:3], "final_norm": a[3],
            "b_in": a[4:9].reshape(1, D_IN), "sinks": a[9, 0:8].reshape(1, 8), "rel_bias": a[9, 8:648].reshape(32, 20)}


def kernel(x, ffn1_norm, ffn1_w_gate, ffn1_w_up, ffn1_w_down, mix_norm, w_in, b_in, w_branch_a, w_branch_b, w_out, sinks, rel_bias, ffn2_norm, ffn2_w_gate, ffn2_w_up, ffn2_w_down, final_norm, loss_target, m_ffn1_norm, m_ffn1_w_gate, m_ffn1_w_up, m_ffn1_w_down, m_mix_norm, m_w_in, m_b_in, m_w_branch_a, m_w_branch_b, m_w_out, m_sinks, m_rel_bias, m_ffn2_norm, m_ffn2_w_gate, m_ffn2_w_up, m_ffn2_w_down, m_final_norm, v_ffn1_norm, v_ffn1_w_gate, v_ffn1_w_up, v_ffn1_w_down, v_mix_norm, v_w_in, v_b_in, v_w_branch_a, v_w_branch_b, v_w_out, v_sinks, v_rel_bias, v_ffn2_norm, v_ffn2_w_gate, v_ffn2_w_up, v_ffn2_w_down, v_final_norm):
    args = dict(locals())
    w = {n: args[n] for n in _ORDER}
    m = {n: args["m_" + n] for n in _ORDER}
    v = {n: args["v_" + n] for n in _ORDER}

    shards = {}
    for key, name, transposed, rows in _BIG:
        a = w[name][0]
        a = (a.T if transposed else a).astype(BF16)
        shards[key] = a.reshape(rows, D_MODEL)
    S = {n: w[n] for n in _SMALL}

    loss_part, grad_x, reduced, GS = _local_step(x, loss_target, {}, S, shards)

    small = _allreduce_small(_pack_small(GS, extra=loss_part[0, 0]))
    loss = small[9, 648]

    out_g, out_d, out_m, out_v = {}, {}, {}, {}
    for key, n, transposed, rows in _BIG:
        nat = w[n][0].shape
        if transposed and nat[1] % 128:
            res = _adamw(w[n][0].T, reduced[key], m[n][0].T, v[n][0].T, "adamw_" + n)
            res = [reduced[key].T] + [r.T for r in res]
        else:
            g = reduced[key].reshape(nat[1], nat[0]).T if transposed else reduced[key].reshape(nat)
            res = [g] + list(_adamw(w[n][0], g, m[n][0], v[n][0], "adamw_" + n))
        out_g[n], out_d[n], out_m[n], out_v[n] = [r[None] for r in res]
    d_s, m_s, v_s = _adamw(_pack_small(w), small, _pack_small(m), _pack_small(v), "adamw_small")
    for dst, src in ((out_g, small), (out_d, d_s), (out_m, m_s), (out_v, v_s)):
        dst.update(_unpack_small(src))

    return (loss, grad_x, *[out_g[n] for n in _ORDER], *[out_d[n] for n in _ORDER],
            *[out_m[n] for n in _ORDER], *[out_v[n] for n in _ORDER])
```

```python
import math

import jax
import jax.numpy as jnp
from jax import lax
from jax.experimental import pallas as pl
from jax.experimental.pallas import tpu as pltpu

F32, BF16 = jnp.float32, jnp.bfloat16
MESH = pl.DeviceIdType.MESH

D_MODEL = 1024
D_FF = 2816
D_IN = 5120
HEAD_DIM = 64
BLOCK = 128
DIL_GROUPS = ((128, 1), (512, 4), (2048, 16))
B_WINDOW = 128
N_BUCKETS = 32
MAX_DISTANCE = 2048
EPS = 1e-6
N_CHIPS = 4
GW = 256
ZA_W = 2304
ZB_W = 768
NEG = -1e30

ADAM_LR, ADAM_B1, ADAM_B2, ADAM_EPS, ADAM_WD, ADAM_STEP = 0.001, 0.9, 0.999, 1e-08, 0.01, 10

VMEM_BIG = 56 * 1024 * 1024
TM = 512
TM_BWD = 256
MXU_DIM = 256
FF_BOUNDS = (0, 6 * MXU_DIM, D_FF)
DMA_SPLIT = 8
RESIDUES_PER_STEP = 4


def _dot(a, b):
    return jnp.dot(a, b, preferred_element_type=F32)


def _dot_nt(a, b):
    return lax.dot_general(a, b, (((1,), (1,)), ((), ())), preferred_element_type=F32)


def _dot_tn(a, b):
    return lax.dot_general(a, b, (((0,), (0,)), ((), ())), preferred_element_type=F32)


def _sigmoid(x):
    return 0.5 * jnp.tanh(0.5 * x) + 0.5


def _params(sem, vmem=None):
    return pltpu.CompilerParams(dimension_semantics=sem, vmem_limit_bytes=vmem)


ANY = pl.BlockSpec(memory_space=pl.ANY)


def _me():
    return lax.axis_index("x"), lax.axis_index("y"), lax.axis_index("c")


_CHIP_RELS = ((1, 0), (0, 1), (1, 1))


def _flip(v, f):
    return 1 - v if f else v


def _remote(src, dst, ssem, rsem, peer):
    return pltpu.make_async_remote_copy(src_ref=src, dst_ref=dst, send_sem=ssem, recv_sem=rsem,
                                        device_id=peer, device_id_type=MESH)


def _row_pieces(rows, n):
    step = max(16, -(-rows // n) // 16 * 16)
    out, s = [], 0
    while s < rows:
        out.append((s, min(step, rows - s)))
        s += step
    return out


def _gather_rows(shards):
    nt = len(shards)
    rows = [s.shape[0] for s in shards]

    def body(*refs):
        srcs, outs = refs[:nt], refs[nt:2 * nt]
        ici_s, ici_r, d2d_s, d2d_r, loc = refs[2 * nt:]
        x, y, c = _me()
        j = 2 * x + y
        sib = (x, y, 1 - c)
        local = [pltpu.make_async_copy(srcs[t], outs[t].at[j], loc.at[t]) for t in range(nt)]
        for cp in local:
            cp.start()
        sends = []
        for k, (fx, fy) in enumerate(_CHIP_RELS):
            peer = (_flip(x, fx), _flip(y, fy), c)
            for t in range(nt):
                half = pl.ds(c * (rows[t] // 2), rows[t] // 2)
                cp = _remote(srcs[t].at[half], outs[t].at[j, half], ici_s.at[3 * t + k], ici_r.at[3 * t + k], peer)
                cp.start()
                sends.append(cp)
        fwds = []
        for k, (fx, fy) in enumerate(_CHIP_RELS):
            pj = 2 * _flip(x, fx) + _flip(y, fy)
            for t in range(nt):
                half = pl.ds(c * (rows[t] // 2), rows[t] // 2)
                blk = outs[t].at[pj, half]
                _remote(blk, blk, ici_s.at[3 * t + k], ici_r.at[3 * t + k], sib).wait_recv()
                cp = _remote(blk, blk, d2d_s.at[3 * t + k], d2d_r.at[3 * t + k], sib)
                cp.start()
                fwds.append(cp)
        for cp in fwds:
            cp.wait()
        for cp in sends:
            cp.wait_send()
        for cp in local:
            cp.wait()

    sems = [pltpu.SemaphoreType.DMA((3 * nt,)) for _ in range(4)] + [pltpu.SemaphoreType.DMA((nt,))]
    return pl.pallas_call(
        body, name="gather_weights",
        out_shape=tuple(jax.ShapeDtypeStruct((N_CHIPS,) + s.shape, s.dtype) for s in shards),
        in_specs=[ANY] * nt, out_specs=tuple([ANY] * nt), scratch_shapes=sems,
    )(*shards)


VMEM_WHOLE = pl.BlockSpec(memory_space=pltpu.VMEM)


def _pair_reduce(grads, name):
    nt = len(grads)
    r2 = [g.shape[2] for g in grads]
    off = [sum(r2[:t]) for t in range(nt)]
    tot = sum(r2)

    def body(*refs):
        gs = refs[:nt]
        s_ref, got, ssem, rsem = refs[nt:]
        x, y, c = _me()
        sib = (x, y, 1 - c)
        for t in range(nt):
            for k in range(N_CHIPS):
                _remote(gs[t].at[k, 1 - c], got.at[k, pl.ds(off[t], r2[t])], ssem, rsem, sib).start()
        _remote(got, got, ssem, rsem, sib).wait()
        for t in range(nt):
            for k in range(N_CHIPS):
                rows = slice(off[t], off[t] + r2[t])
                s_ref[k, rows, :] = (gs[t][k, c].astype(F32) + got[k, rows, :].astype(F32)).astype(BF16)

    shp = jax.ShapeDtypeStruct((N_CHIPS, tot, D_MODEL), BF16)
    return pl.pallas_call(
        body, name=name, out_shape=shp, in_specs=[VMEM_WHOLE] * nt, out_specs=VMEM_WHOLE,
        scratch_shapes=[pltpu.VMEM((N_CHIPS, tot, D_MODEL), BF16), pltpu.SemaphoreType.DMA(()),
                        pltpu.SemaphoreType.DMA(())],
        compiler_params=pltpu.CompilerParams(vmem_limit_bytes=VMEM_BIG),
    )(*grads)


def _chip_exchange(parts):
    ng = len(parts)
    r2 = [p.shape[1] for p in parts]
    off = [sum(r2[:g]) for g in range(ng)]
    tot = sum(r2)

    def body(*refs):
        ps = refs[:ng]
        own_ref, rec_ref, ssems, rsems, lsem = refs[ng:]
        x, y, c = _me()
        j = 2 * x + y
        for g in range(ng):
            pltpu.make_async_copy(ps[g].at[j], own_ref.at[pl.ds(off[g], r2[g])], lsem).start()
        for k, (fx, fy) in enumerate(_CHIP_RELS):
            px, py = _flip(x, fx), _flip(y, fy)
            for g in range(ng):
                for st, sz in _row_pieces(r2[g], 2):
                    _remote(ps[g].at[2 * px + py, pl.ds(st, sz)], rec_ref.at[k, pl.ds(off[g] + st, sz)],
                            ssems.at[k], rsems.at[k], (px, py, c)).start()
        for k in range(3):
            _remote(rec_ref.at[k], rec_ref.at[k], ssems.at[k], rsems.at[k], (x, y, c)).wait()
        pltpu.make_async_copy(own_ref, own_ref, lsem).wait()

    return pl.pallas_call(
        body, name="grad_chip_exchange",
        out_shape=(jax.ShapeDtypeStruct((tot, D_MODEL), BF16), jax.ShapeDtypeStruct((3, tot, D_MODEL), BF16)),
        in_specs=[ANY] * ng, out_specs=(ANY, ANY),
        scratch_shapes=[pltpu.SemaphoreType.DMA((3,)), pltpu.SemaphoreType.DMA((3,)), pltpu.SemaphoreType.DMA(())],
    )(*parts)


def _final_reduce(own, rec, name):
    r2 = own.shape[0]
    pieces = _row_pieces(r2, DMA_SPLIT)

    def body(own_ref, rec_ref, o_ref, fbuf, ssem, rsem, lsem):
        x, y, c = _me()
        sib = (x, y, 1 - c)
        for st, sz in pieces:
            rows = slice(st, st + sz)
            fbuf[rows, :] = (own_ref[rows, :].astype(F32) + rec_ref[0, rows, :].astype(F32)
                             + rec_ref[1, rows, :].astype(F32) + rec_ref[2, rows, :].astype(F32))
            pltpu.make_async_copy(fbuf.at[pl.ds(st, sz)], o_ref.at[c, pl.ds(st, sz)], lsem).start()
            _remote(fbuf.at[pl.ds(st, sz)], o_ref.at[c, pl.ds(st, sz)], ssem, rsem, sib).start()
        _remote(fbuf, o_ref.at[c], ssem, rsem, sib).wait()
        pltpu.make_async_copy(fbuf, o_ref.at[c], lsem).wait()

    return pl.pallas_call(
        body, name=name, out_shape=jax.ShapeDtypeStruct((2, r2, D_MODEL), F32),
        in_specs=[VMEM_WHOLE, VMEM_WHOLE], out_specs=ANY,
        scratch_shapes=[pltpu.VMEM((r2, D_MODEL), F32), pltpu.SemaphoreType.DMA(()), pltpu.SemaphoreType.DMA(()),
                        pltpu.SemaphoreType.DMA(())],
        compiler_params=pltpu.CompilerParams(vmem_limit_bytes=VMEM_BIG),
    )(own, rec)


def _allreduce_small(vec):
    def body(v_ref, o_ref, buf, send_sems, recv_sems):
        x, y, c = _me()
        me = 4 * x + 2 * y + c
        buf[me] = v_ref[...]
        copies = []
        for k in range(1, 8):
            peer = (_flip(x, (k >> 2) & 1), _flip(y, (k >> 1) & 1), _flip(c, k & 1))
            cp = _remote(v_ref, buf.at[me], send_sems.at[k - 1], recv_sems.at[k - 1], peer)
            cp.start()
            copies.append(cp)
        for cp in copies:
            cp.wait()
        acc = buf[0]
        for i in range(1, 8):
            acc = acc + buf[i]
        o_ref[...] = acc

    vm = pl.BlockSpec(memory_space=pltpu.VMEM)
    return pl.pallas_call(
        body, name="allreduce_small", out_shape=jax.ShapeDtypeStruct(vec.shape, vec.dtype),
        in_specs=[vm], out_specs=vm,
        scratch_shapes=[pltpu.VMEM((8,) + vec.shape, vec.dtype), pltpu.SemaphoreType.DMA((7,)),
                        pltpu.SemaphoreType.DMA((7,))],
    )(vec)


class _GatherRider:
    def __init__(self, shards):
        self.inputs = list(shards)
        nt = len(shards)
        self.out_shape = [jax.ShapeDtypeStruct((N_CHIPS,) + s.shape, s.dtype) for s in shards]
        self.scratch = [pltpu.SemaphoreType.DMA((3 * nt,)), pltpu.SemaphoreType.DMA((3 * nt,)),
                        pltpu.SemaphoreType.DMA((nt,))]

    def _copies(self, srcs, outs, sems):
        ici_s, ici_r, loc = sems
        x, y, c = _me()
        j = 2 * x + y
        local = [pltpu.make_async_copy(srcs[t], outs[t].at[j], loc.at[t]) for t in range(len(srcs))]
        remote = []
        for k, (fx, fy) in enumerate(_CHIP_RELS):
            peer = (_flip(x, fx), _flip(y, fy), c)
            for t in range(len(srcs)):
                remote.append(_remote(srcs[t], outs[t].at[j], ici_s.at[3 * t + k], ici_r.at[3 * t + k], peer))
        return local, remote

    def start(self, srcs, outs, sems):
        local, remote = self._copies(srcs, outs, sems)
        for cp in local + remote:
            cp.start()

    def finish(self, srcs, outs, sems):
        local, remote = self._copies(srcs, outs, sems)
        for cp in remote + local:
            cp.wait()


class _ExchangeRider:
    def __init__(self, parts):
        self.inputs = list(parts)
        self.r2 = [p.shape[1] for p in parts]
        self.off = [sum(self.r2[:g]) for g in range(len(parts))]
        tot = sum(self.r2)
        self.out_shape = [jax.ShapeDtypeStruct((tot, D_MODEL), BF16), jax.ShapeDtypeStruct((3, tot, D_MODEL), BF16)]
        self.scratch = [pltpu.SemaphoreType.DMA((3,)), pltpu.SemaphoreType.DMA((3,)), pltpu.SemaphoreType.DMA(())]

    def start(self, ps, outs, sems):
        own_ref, rec_ref = outs
        ssems, rsems, lsem = sems
        x, y, c = _me()
        j = 2 * x + y
        for g in range(len(ps)):
            pltpu.make_async_copy(ps[g].at[j], own_ref.at[pl.ds(self.off[g], self.r2[g])], lsem).start()
        for k, (fx, fy) in enumerate(_CHIP_RELS):
            px, py = _flip(x, fx), _flip(y, fy)
            for g in range(len(ps)):
                for st, sz in _row_pieces(self.r2[g], 2):
                    _remote(ps[g].at[2 * px + py, pl.ds(st, sz)], rec_ref.at[k, pl.ds(self.off[g] + st, sz)],
                            ssems.at[k], rsems.at[k], (px, py, c)).start()

    def finish(self, ps, outs, sems):
        own_ref, rec_ref = outs
        ssems, rsems, lsem = sems
        x, y, c = _me()
        for k in range(3):
            _remote(rec_ref.at[k], rec_ref.at[k], ssems.at[k], rsems.at[k], (x, y, c)).wait()
        pltpu.make_async_copy(own_ref, own_ref, lsem).wait()


def _pallas(body, args, *, name, grid, in_specs, out_specs, out_shape, scratch_shapes=(), sem=None, vmem=None,
            rider=None):
    if rider is None:
        res = pl.pallas_call(body, name=name, grid=grid, in_specs=list(in_specs), out_specs=tuple(out_specs),
                             out_shape=tuple(out_shape), scratch_shapes=list(scratch_shapes),
                             compiler_params=_params(sem, vmem))(*args)
        return tuple(res), ()
    n_in, n_out, n_sc = len(in_specs), len(out_shape), len(scratch_shapes)
    r_in, r_out = len(rider.inputs), len(rider.out_shape)

    def wrapped(*refs):
        ins, rins = refs[:n_in], refs[n_in:n_in + r_in]
        p = n_in + r_in
        outs, routs = refs[p:p + n_out], refs[p + n_out:p + n_out + r_out]
        p += n_out + r_out
        scr, rsems = refs[p:p + n_sc], refs[p + n_sc:]
        first = pl.program_id(0) == 0
        last = pl.program_id(0) == grid[0] - 1
        for a in range(1, len(grid)):
            first = first & (pl.program_id(a) == 0)
            last = last & (pl.program_id(a) == grid[a] - 1)

        @pl.when(first)
        def _():
            rider.start(rins, routs, rsems)

        body(*ins, *outs, *scr)

        @pl.when(last)
        def _():
            rider.finish(rins, routs, rsems)

    res = pl.pallas_call(
        wrapped, name=name, grid=grid, in_specs=list(in_specs) + [ANY] * r_in,
        out_specs=tuple(out_specs) + (ANY,) * r_out, out_shape=tuple(out_shape) + tuple(rider.out_shape),
        scratch_shapes=list(scratch_shapes) + rider.scratch,
        compiler_params=_params(("arbitrary",) * len(grid), vmem))(*args, *rider.inputs)
    return tuple(res[:n_out]), tuple(res[n_out:])


def _ffn_fwd(h, gain, wgt, wut, wd, rider=None):
    t = h.shape[0]

    def body(h_ref, gain_ref, wg_hbm, wu_hbm, wd_hbm, hout_ref, n_ref, g_ref, u_ref, wg_v, wu_v, wd_v):
        @pl.when(pl.program_id(0) == 0)
        def _():
            pltpu.sync_copy(wg_hbm, wg_v)
            pltpu.sync_copy(wu_hbm, wu_v)
            pltpu.sync_copy(wd_hbm, wd_v)

        hh = h_ref[...]
        r = lax.rsqrt(jnp.mean(hh * hh, axis=-1, keepdims=True) + EPS)
        n = (hh * r * gain_ref[...]).astype(BF16)
        n_ref[...] = n
        acc = jnp.zeros((TM, D_MODEL), F32)
        for c0, c1 in zip(FF_BOUNDS[:-1], FF_BOUNDS[1:]):
            sl = slice(c0, c1)
            g = _dot_nt(n, wg_v[sl, :])
            u = _dot_nt(n, wu_v[sl, :])
            g_ref[:, sl] = g.astype(BF16)
            u_ref[:, sl] = u.astype(BF16)
            a = (g * _sigmoid(g) * u).astype(BF16)
            acc = acc + _dot(a, wd_v[sl, :])
        hout_ref[...] = hh + 0.5 * acc

    row = lambda w: pl.BlockSpec((TM, w), lambda i: (i, 0))
    wv = pltpu.VMEM((D_FF, D_MODEL), BF16)
    return _pallas(
        body, (h, gain, wgt, wut, wd), name="ffn_fwd", grid=(t // TM,),
        out_shape=(jax.ShapeDtypeStruct((t, D_MODEL), F32), jax.ShapeDtypeStruct((t, D_MODEL), BF16),
                   jax.ShapeDtypeStruct((t, D_FF), BF16), jax.ShapeDtypeStruct((t, D_FF), BF16)),
        in_specs=[row(D_MODEL), pl.BlockSpec((1, D_MODEL), lambda i: (0, 0)), ANY, ANY, ANY],
        out_specs=(row(D_MODEL), row(D_MODEL), row(D_FF), row(D_FF)),
        scratch_shapes=[wv, wv, wv], sem=("arbitrary",), vmem=VMEM_BIG, rider=rider)


def _ffn_bwd(dhout, h, gain, g, u, wgt, wut, wd):
    t = h.shape[0]
    tm = TM_BWD

    def body(dho_ref, h_ref, gain_ref, g_ref, u_ref, wg_hbm, wu_hbm, wd_hbm,
             dh_ref, dg_ref, du_ref, a_ref, df_ref, gg_ref, wg_v, wu_v, wd_v):
        @pl.when(pl.program_id(0) == 0)
        def _():
            pltpu.sync_copy(wg_hbm, wg_v)
            pltpu.sync_copy(wu_hbm, wu_v)
            pltpu.sync_copy(wd_hbm, wd_v)
            gg_ref[...] = jnp.zeros_like(gg_ref)

        dho = dho_ref[...]
        df = (0.5 * dho).astype(BF16)
        df_ref[...] = df
        dn = jnp.zeros((tm, D_MODEL), F32)
        for c0, c1 in zip(FF_BOUNDS[:-1], FF_BOUNDS[1:]):
            sl = slice(c0, c1)
            da = _dot_nt(df, wd_v[sl, :])
            gv = g_ref[:, sl].astype(F32)
            uv = u_ref[:, sl].astype(F32)
            sg = _sigmoid(gv)
            silu = gv * sg
            dg = (da * uv * (sg * (1.0 + gv * (1.0 - sg)))).astype(BF16)
            du = (da * silu).astype(BF16)
            dg_ref[:, sl] = dg
            du_ref[:, sl] = du
            a_ref[:, sl] = (silu * uv).astype(BF16)
            dn = dn + _dot(dg, wg_v[sl, :]) + _dot(du, wu_v[sl, :])
        hh = h_ref[...]
        r = lax.rsqrt(jnp.mean(hh * hh, axis=-1, keepdims=True) + EPS)
        hn = hh * r
        gg_ref[...] += jnp.sum(dn * hn, axis=0, keepdims=True)
        dng = dn * gain_ref[...]
        dh_ref[...] = dho + r * (dng - hn * jnp.mean(dng * hn, axis=-1, keepdims=True))

    row = lambda w: pl.BlockSpec((tm, w), lambda i: (i, 0))
    vec = pl.BlockSpec((1, D_MODEL), lambda i: (0, 0))
    wv = pltpu.VMEM((D_FF, D_MODEL), BF16)
    return pl.pallas_call(
        body, name="ffn_bwd", grid=(t // tm,),
        out_shape=(jax.ShapeDtypeStruct((t, D_MODEL), F32), jax.ShapeDtypeStruct((t, D_FF), BF16),
                   jax.ShapeDtypeStruct((t, D_FF), BF16), jax.ShapeDtypeStruct((t, D_FF), BF16),
                   jax.ShapeDtypeStruct((t, D_MODEL), BF16), jax.ShapeDtypeStruct((1, D_MODEL), F32)),
        in_specs=[row(D_MODEL), row(D_MODEL), vec, row(D_FF), row(D_FF), ANY, ANY, ANY],
        out_specs=(row(D_MODEL), row(D_FF), row(D_FF), row(D_FF), row(D_MODEL), vec),
        scratch_shapes=[wv, wv, wv],
        compiler_params=_params(("arbitrary",), VMEM_BIG),
    )(dhout, h, gain, g, u, wgt, wut, wd)


def _wgrad(lhs, rhs, rb, with_colsum=False, name="wgrad", rider=None):
    t, k = lhs.shape
    n = rhs.shape[1]

    def body(l_ref, r_ref, o_ref, *rest):
        o_ref[...] = _dot_tn(l_ref[...], r_ref[...]).astype(BF16)
        if with_colsum:
            rest[0][...] = jnp.sum(l_ref[...].astype(F32), axis=0, keepdims=True)

    out_shape = [jax.ShapeDtypeStruct((k, n), BF16)]
    out_specs = [pl.BlockSpec((rb, n), lambda j: (j, 0))]
    if with_colsum:
        out_shape.append(jax.ShapeDtypeStruct((1, k), F32))
        out_specs.append(pl.BlockSpec((1, rb), lambda j: (0, j)))
    res, ro = _pallas(
        body, (lhs, rhs), name=name, grid=(k // rb,), out_shape=tuple(out_shape),
        in_specs=[pl.BlockSpec((t, rb), lambda j: (0, j)), pl.BlockSpec((t, n), lambda j: (0, 0))],
        out_specs=tuple(out_specs), sem=("arbitrary",), vmem=VMEM_BIG, rider=rider)
    if rider is not None:
        return res[0], ro
    return res if with_colsum else res[0]


def _lane_blocks(nseq, seq, nblk, tm=TM):
    spt = seq // tm
    return pl.BlockSpec((1, nblk, tm, 128), lambda i: (i // spt, 0, i % spt, 0))


def _inproj_fwd(h, gain, wint, b_in, nseq, rider=None):
    t = h.shape[0]
    seq = t // nseq
    cut_a = 5 * MXU_DIM
    pieces = ((0, cut_a, 0, 0), (cut_a, ZA_W - cut_a, 0, cut_a), (ZA_W, ZB_W, 1, 0), (ZA_W + ZB_W, 1024, 2, 0),
              (ZA_W + ZB_W + 1024, 1024, 2, 1024))

    def body(h_ref, gain_ref, w_hbm, b_ref, u_ref, za_ref, zb_ref, zg_ref, w_v):
        @pl.when(pl.program_id(0) == 0)
        def _():
            pltpu.sync_copy(w_hbm, w_v)

        hh = h_ref[...]
        r = lax.rsqrt(jnp.mean(hh * hh, axis=-1, keepdims=True) + EPS)
        un = (hh * r * gain_ref[...]).astype(BF16)
        u_ref[...] = un
        outs = (None, zb_ref, zg_ref)
        for c0, cw, oi, o0 in pieces:
            val = _dot_nt(un, w_v[c0:c0 + cw, :]) + b_ref[:, c0:c0 + cw]
            if oi == 0:
                for cb in range(cw // 128):
                    za_ref[0, o0 // 128 + cb] = val[:, cb * 128:(cb + 1) * 128]
            else:
                outs[oi][:, o0:o0 + cw] = val.astype(BF16)

    row = lambda w: pl.BlockSpec((TM, w), lambda i: (i, 0))
    return _pallas(
        body, (h, gain, wint, b_in), name="inproj_fwd", grid=(t // TM,),
        out_shape=(jax.ShapeDtypeStruct((t, D_MODEL), BF16), jax.ShapeDtypeStruct((nseq, ZA_W // 128, seq, 128), F32),
                   jax.ShapeDtypeStruct((t, ZB_W), BF16), jax.ShapeDtypeStruct((t, 2 * D_MODEL), BF16)),
        in_specs=[row(D_MODEL), pl.BlockSpec((1, D_MODEL), lambda i: (0, 0)), ANY,
                  pl.BlockSpec((1, D_IN), lambda i: (0, 0))],
        out_specs=(row(D_MODEL), _lane_blocks(nseq, seq, ZA_W // 128), row(ZB_W), row(2 * D_MODEL)),
        scratch_shapes=[pltpu.VMEM((D_IN, D_MODEL), BF16)], sem=("arbitrary",), vmem=VMEM_BIG, rider=rider)


def _inproj_bwd(dz, dh2, h, gain, wint, rider=None):
    t = h.shape[0]
    nc = 5
    cw = D_IN // nc

    def body(dz_ref, dh2_ref, h_ref, gain_ref, w_hbm, dh_ref, gg_ref, w_v):
        @pl.when(pl.program_id(0) == 0)
        def _():
            pltpu.sync_copy(w_hbm, w_v)
            gg_ref[...] = jnp.zeros_like(gg_ref)

        du = jnp.zeros((TM, D_MODEL), F32)
        for ci in range(nc):
            sl = slice(ci * cw, (ci + 1) * cw)
            du = du + _dot(dz_ref[:, sl], w_v[sl, :])
        hh = h_ref[...]
        r = lax.rsqrt(jnp.mean(hh * hh, axis=-1, keepdims=True) + EPS)
        hn = hh * r
        gg_ref[...] += jnp.sum(du * hn, axis=0, keepdims=True)
        dng = du * gain_ref[...]
        dh_ref[...] = dh2_ref[...] + r * (dng - hn * jnp.mean(dng * hn, axis=-1, keepdims=True))

    row = lambda w: pl.BlockSpec((TM, w), lambda i: (i, 0))
    vec = pl.BlockSpec((1, D_MODEL), lambda i: (0, 0))
    return _pallas(
        body, (dz, dh2, h, gain, wint), name="inproj_bwd", grid=(t // TM,),
        out_shape=(jax.ShapeDtypeStruct((t, D_MODEL), F32), jax.ShapeDtypeStruct((1, D_MODEL), F32)),
        in_specs=[row(D_IN), row(D_MODEL), row(D_MODEL), vec, ANY],
        out_specs=(row(D_MODEL), vec),
        scratch_shapes=[pltpu.VMEM((D_IN, D_MODEL), BF16)], sem=("arbitrary",), vmem=VMEM_BIG, rider=rider)


def _head_sums(x):
    w = x.shape[1]
    i = lax.broadcasted_iota(jnp.int32, (w, w), 0) // HEAD_DIM
    j = lax.broadcasted_iota(jnp.int32, (w, w), 1) // HEAD_DIM
    ones = (i == j).astype(BF16)
    hi = x.astype(BF16)
    r1 = x - hi.astype(F32)
    mid = r1.astype(BF16)
    lo = (r1 - mid.astype(F32)).astype(BF16)
    return _dot(hi, ones) + _dot(mid, ones) + _dot(lo, ones)


def _merge_fwd(o0, o1, o2, l0, l1, l2, yb, zg, h1, wat, wbt, wout, rider=None):
    t = h1.shape[0]
    nseq, _, seq, _ = o0.shape

    def body(o0_ref, o1_ref, o2_ref, l0_ref, l1_ref, l2_ref, yb_ref, ga_ref, gb_ref, h1_ref, wa_ref, wb_ref, wo_ref,
             h2_ref, y_ref, lt_ref, pa_ref, pb_ref, mg_ref):
        wide = lambda ref: jnp.concatenate([ref[0, 0], ref[0, 1]], axis=1)
        la, lb, lc = wide(l0_ref), wide(l1_ref), wide(l2_ref)
        mx = jnp.maximum(jnp.maximum(la, lb), lc)
        ea, eb, ec = jnp.exp(la - mx), jnp.exp(lb - mx), jnp.exp(lc - mx)
        den = ea + eb + ec
        y = (ea * wide(o0_ref) + eb * wide(o1_ref) + ec * wide(o2_ref)) / den
        lt = mx + jnp.log(den)
        lt_ref[0, 0] = lt[:, :128]
        lt_ref[0, 1] = lt[:, 128:]
        yb16 = y.astype(BF16)
        y_ref[...] = yb16
        pa = _dot_nt(yb16, wa_ref[...])
        pb = _dot_nt(yb_ref[...], wb_ref[...])
        pa_ref[...] = pa.astype(BF16)
        pb_ref[...] = pb.astype(BF16)
        mg = (_sigmoid(ga_ref[...].astype(F32)) * pa + _sigmoid(gb_ref[...].astype(F32)) * pb).astype(BF16)
        mg_ref[...] = mg
        h2_ref[...] = h1_ref[...] + _dot(mg, wo_ref[...])

    row = lambda w: pl.BlockSpec((TM, w), lambda i: (i, 0))
    full = lambda a: pl.BlockSpec(a.shape, lambda i: (0, 0))
    gate = lambda cb: pl.BlockSpec((TM, D_MODEL), lambda i: (i, cb))
    return _pallas(
        body, (o0, o1, o2, l0, l1, l2, yb, zg, zg, h1, wat, wbt, wout), name="merge_fwd", grid=(t // TM,),
        out_shape=(jax.ShapeDtypeStruct((t, D_MODEL), F32), jax.ShapeDtypeStruct((t, GW), BF16),
                   jax.ShapeDtypeStruct((nseq, 2, seq, 128), F32), jax.ShapeDtypeStruct((t, D_MODEL), BF16),
                   jax.ShapeDtypeStruct((t, D_MODEL), BF16), jax.ShapeDtypeStruct((t, D_MODEL), BF16)),
        in_specs=[_lane_blocks(nseq, seq, 2)] * 6 + [row(2 * GW), gate(0), gate(1), row(D_MODEL), full(wat), full(wbt),
                                                     full(wout)],
        out_specs=(row(D_MODEL), row(GW), _lane_blocks(nseq, seq, 2), row(D_MODEL), row(D_MODEL), row(D_MODEL)),
        sem=("parallel",), vmem=VMEM_BIG, rider=rider)


def _merge_bwd(dh2, pa, pb, zg, y, yb, wat, wbt, wout, nseq, rider=None):
    t = dh2.shape[0]

    def body(dh2_ref, pa_ref, pb_ref, ga_ref, gb_ref, y_ref, yb_ref, wa_ref, wb_ref, wo_ref,
             dpa_ref, dpb_ref, dga_ref, dgb_ref, dya_ref, dyb_ref, dh2b_ref, ca_ref, cb_ref):
        d16 = dh2_ref[...].astype(BF16)
        dh2b_ref[...] = d16
        dm = _dot_nt(d16, wo_ref[...])
        sa = _sigmoid(ga_ref[...].astype(F32))
        sb = _sigmoid(gb_ref[...].astype(F32))
        dpa = (dm * sa).astype(BF16)
        dpb = (dm * sb).astype(BF16)
        dpa_ref[...] = dpa
        dpb_ref[...] = dpb
        dga_ref[...] = (dm * pa_ref[...].astype(F32) * sa * (1.0 - sa)).astype(BF16)
        dgb_ref[...] = (dm * pb_ref[...].astype(F32) * sb * (1.0 - sb)).astype(BF16)
        dya = _dot(dpa, wa_ref[...])
        dyb = _dot(dpb, wb_ref[...])
        dya_ref[0, 0] = dya[:, :128]
        dya_ref[0, 1] = dya[:, 128:]
        dyb_ref[...] = dyb.astype(BF16)
        ca = _head_sums(dya * y_ref[...].astype(F32))
        ca_ref[0, 0] = ca[:, :128]
        ca_ref[0, 1] = ca[:, 128:]
        cb_ref[...] = _head_sums(dyb * yb_ref[...].astype(F32))

    row = lambda w: pl.BlockSpec((TM, w), lambda i: (i, 0))
    full = lambda a: pl.BlockSpec(a.shape, lambda i: (0, 0))
    gate = lambda cb: pl.BlockSpec((TM, D_MODEL), lambda i: (i, cb))
    bf = lambda w: jax.ShapeDtypeStruct((t, w), BF16)
    lanes = jax.ShapeDtypeStruct((nseq, 2, t // nseq, 128), F32)
    lane_spec = _lane_blocks(nseq, t // nseq, 2)
    return _pallas(
        body, (dh2, pa, pb, zg, zg, y, yb, wat, wbt, wout), name="merge_bwd", grid=(t // TM,),
        out_shape=(bf(D_MODEL), bf(D_MODEL), bf(D_MODEL), bf(D_MODEL), lanes, bf(2 * GW), bf(D_MODEL),
                   lanes, jax.ShapeDtypeStruct((t, 2 * GW), F32)),
        in_specs=[row(D_MODEL), row(D_MODEL), row(D_MODEL), gate(0), gate(1), row(GW), row(2 * GW),
                  full(wat), full(wbt), full(wout)],
        out_specs=(row(D_MODEL), row(D_MODEL), row(D_MODEL), row(D_MODEL), lane_spec, row(2 * GW), row(D_MODEL),
                   lane_spec, row(2 * GW)),
        sem=("parallel",), vmem=VMEM_BIG, rider=rider)


def _loss_head(h3, gain, tgt):
    t = h3.shape[0]

    def body(h_ref, gain_ref, t_ref, dh_ref, loss_ref, gg_ref):
        @pl.when(pl.program_id(0) == 0)
        def _():
            loss_ref[...] = jnp.zeros_like(loss_ref)
            gg_ref[...] = jnp.zeros_like(gg_ref)

        hh = h_ref[...]
        r = lax.rsqrt(jnp.mean(hh * hh, axis=-1, keepdims=True) + EPS)
        hn = hh * r
        err = hn * gain_ref[...] - t_ref[...]
        part = jnp.sum(jnp.sum(err * err, axis=1, keepdims=True), axis=0, keepdims=True)
        loss_ref[...] += (0.5 / D_MODEL) * part
        dy = err * (1.0 / D_MODEL)
        gg_ref[...] += jnp.sum(dy * hn, axis=0, keepdims=True)
        dng = dy * gain_ref[...]
        dh_ref[...] = r * (dng - hn * jnp.mean(dng * hn, axis=-1, keepdims=True))

    row = pl.BlockSpec((TM, D_MODEL), lambda i: (i, 0))
    vec = pl.BlockSpec((1, D_MODEL), lambda i: (0, 0))
    return pl.pallas_call(
        body, name="loss_head", grid=(t // TM,),
        out_shape=(jax.ShapeDtypeStruct((t, D_MODEL), F32), jax.ShapeDtypeStruct((8, 128), F32),
                   jax.ShapeDtypeStruct((1, D_MODEL), F32)),
        in_specs=[row, vec, row], out_specs=(row, pl.BlockSpec((8, 128), lambda i: (0, 0)), vec),
        compiler_params=_params(("arbitrary",)),
    )(h3, gain, tgt)


def _lane_head(rows):
    return lax.broadcasted_iota(jnp.int32, (rows, GW), 1) // HEAD_DIM


def _kv_expand_matrix(r):
    ci = lax.broadcasted_iota(jnp.int32, (2 * HEAD_DIM, GW), 0)
    ji = lax.broadcasted_iota(jnp.int32, (2 * HEAD_DIM, GW), 1)
    return (ci == (ji % HEAD_DIM) + HEAD_DIM * r).astype(BF16)


def _block_rows(row0, stride, ib):
    start = row0 + (stride * BLOCK) * ib
    if stride > 1:
        return pl.ds(start, BLOCK, stride=stride)
    return pl.ds(pl.multiple_of(start, BLOCK), BLOCK)


def _stack_heads(x, lane_head):
    return jnp.concatenate([jnp.where(lane_head == h, x, jnp.zeros_like(x)) for h in range(4)], axis=0)


def _unstack_heads(x4, lane_head):
    out = jnp.zeros((BLOCK, GW), F32)
    for h in range(4):
        out = jnp.where(lane_head == h, x4[h * BLOCK:(h + 1) * BLOCK], out)
    return out


def _load_rows(ref, rows, split):
    if split:
        return jnp.concatenate([ref[0, 0, rows, :], ref[0, 1, rows, :]], axis=1)
    return ref[0, rows, :]


def _store_rows(ref, rows, val, split):
    if split:
        ref[0, 0, rows, :] = val[:, :128]
        ref[0, 1, rows, :] = val[:, 128:]
    else:
        ref[0, rows, :] = val


def _attn_fwd(q_arr, k_arr, v_arr, bias, sink, *, grid, seq, stride, kvw, split, q_spec, k_spec, v_spec, bias_map,
              sink_map, o_spec, has_sink, o_shape, o_dtype, name, rider=None):
    nb = seq // stride // BLOCK
    scale = HEAD_DIM ** -0.5
    expanded = kvw != GW
    rps = RESIDUES_PER_STEP if stride >= 4 * RESIDUES_PER_STEP else 1
    grid = (grid[0], grid[1] // rps)
    assert not has_sink or B_WINDOW - 1 < BLOCK

    def body(q_ref, k_ref, v_ref, bias_ref, sink_ref, o_ref, lse_ref, *kv_x):
        rr = pl.program_id(1)
        lane_head = _lane_head(BLOCK)
        if expanded:
            expand = _kv_expand_matrix(rr)
            kv_x[0][...] = _dot(k_ref[0], expand).astype(BF16)
            kv_x[1][...] = _dot(v_ref[0], expand).astype(BF16)
        for j in range(rps):
            residue(rr * rps + j if stride > 1 else 0, q_ref, k_ref, v_ref, bias_ref, sink_ref, o_ref, lse_ref, kv_x,
                    lane_head)

    def residue(row0, q_ref, k_ref, v_ref, bias_ref, sink_ref, o_ref, lse_ref, kv_x, lane_head):
        def per_head(fn, x):
            return jnp.concatenate([fn(sink_ref[0, h:h + 1, 0:1], x[h * BLOCK:(h + 1) * BLOCK]) for h in range(4)],
                                   axis=0)

        def load(ref, ib):
            return _load_rows(ref, _block_rows(row0, stride, ib), split).astype(BF16)

        def load_kv(which, ib):
            if expanded:
                return kv_x[which][_block_rows(0, 1, ib), :]
            return load((k_ref, v_ref)[which], ib)

        def block(ib, first):
            q4 = _stack_heads(load(q_ref, ib), lane_head)
            if first:
                kc, vc = load_kv(0, ib), load_kv(1, ib)
                b4 = bias_ref[:, :, BLOCK:].reshape(4 * BLOCK, BLOCK)
            else:
                kc = jnp.concatenate([load_kv(0, ib - 1), load_kv(0, ib)], axis=0)
                vc = jnp.concatenate([load_kv(1, ib - 1), load_kv(1, ib)], axis=0)
                b4 = bias_ref[...].reshape(4 * BLOCK, 2 * BLOCK)
                if has_sink:
                    oldest = lax.broadcasted_iota(jnp.int32, kc.shape, 0) == 0
                    kc = jnp.where(oldest, jnp.zeros_like(kc), kc)
                    vc = jnp.where(oldest, jnp.zeros_like(vc), vc)
            s = _dot_nt(q4, kc) * scale + b4
            m = jnp.max(s, axis=-1, keepdims=True)
            if has_sink and first:
                m = per_head(jnp.maximum, m)
            p = jnp.exp(s - m)
            l = jnp.sum(p, axis=-1, keepdims=True)
            if has_sink and first:
                l = l + per_head(lambda sk, mh: jnp.exp(sk - mh), m)
            o4 = _dot(p.astype(BF16), vc) / l
            rows = _block_rows(row0, stride, ib)
            _store_rows(o_ref, rows, _unstack_heads(o4, lane_head).astype(o_dtype), split)
            _store_rows(lse_ref, rows, _unstack_heads(m + jnp.log(l), lane_head), split)

        block(0, True)
        if nb > 1:
            def step(i, carry):
                block(i, False)
                return carry
            lax.fori_loop(1, nb, step, 0)

    return _pallas(
        body, (q_arr, k_arr, v_arr, bias, sink), name=name, grid=grid,
        out_shape=(jax.ShapeDtypeStruct(o_shape, o_dtype), jax.ShapeDtypeStruct(o_shape, F32)),
        in_specs=[q_spec, k_spec, v_spec,
                  pl.BlockSpec((4, BLOCK, 2 * BLOCK), bias_map), pl.BlockSpec((1, 4, 128), sink_map)],
        out_specs=(o_spec, o_spec),
        scratch_shapes=[pltpu.VMEM((seq, GW), BF16)] * 2 if expanded else [],
        sem=("arbitrary", "arbitrary"), vmem=VMEM_BIG, rider=rider)


def _attn_bwd(q_arr, k_arr, v_arr, bias, sink, dy, cc, lse, *, grid, seq, stride, kvw, split, q_spec, k_spec, v_spec,
              bias_map, sink_map, o_spec, kv_out_spec, has_sink, n_bias, dq_shape, dkv_shape, g_dtype, name):
    ln = seq // stride
    nb = ln // BLOCK
    scale = HEAD_DIM ** -0.5
    expanded = kvw != GW
    rps = RESIDUES_PER_STEP if stride >= 4 * RESIDUES_PER_STEP else 1
    grid = (grid[0], grid[1] // rps)

    def body(q_ref, k_ref, v_ref, bias_ref, sink_ref, dy_ref, c_ref, lse_ref,
             dq_ref, dk_ref, dv_ref, db_ref, dsk_ref, dk_acc, dv_acc, dk_half, dv_half, *kv_x):
        rr = pl.program_id(1)

        @pl.when((pl.program_id(0) == 0) & (rr == 0))
        def _():
            db_ref[...] = jnp.zeros_like(db_ref)
            dsk_ref[...] = jnp.zeros_like(dsk_ref)

        if expanded:
            expand = _kv_expand_matrix(rr)
            kv_x[0][...] = _dot(k_ref[0], expand).astype(BF16)
            kv_x[1][...] = _dot(v_ref[0], expand).astype(BF16)
        refs = (q_ref, k_ref, v_ref, bias_ref, sink_ref, dy_ref, c_ref, lse_ref, dq_ref, dk_ref, dv_ref, db_ref,
                dsk_ref, dk_acc, dv_acc, dk_half, dv_half, kv_x)
        for j in range(rps):
            residue(rr, rr * rps + j if stride > 1 else 0, *refs)

    def residue(rr, row0, q_ref, k_ref, v_ref, bias_ref, sink_ref, dy_ref, c_ref, lse_ref,
                dq_ref, dk_ref, dv_ref, db_ref, dsk_ref, dk_acc, dv_acc, dk_half, dv_half, kv_x):
        dk_acc[...] = jnp.zeros_like(dk_acc)
        dv_acc[...] = jnp.zeros_like(dv_acc)
        lane_head = _lane_head(BLOCK)
        hb = 4 * rr if n_bias == 8 else 0

        def load(ref, ib):
            return _load_rows(ref, _block_rows(row0, stride, ib), split)

        def load_kv(which, ib):
            if expanded:
                return kv_x[which][_block_rows(0, 1, ib), :]
            return load((k_ref, v_ref)[which], ib).astype(BF16)

        def head_col(x):
            return jnp.concatenate([x[:, h * HEAD_DIM:h * HEAD_DIM + 1] for h in range(4)], axis=0)

        def block(ib, first):
            q4 = _stack_heads(load(q_ref, ib).astype(BF16), lane_head)
            dy4 = _stack_heads(load(dy_ref, ib).astype(BF16), lane_head)
            c4 = head_col(load(c_ref, ib))
            l4 = head_col(load(lse_ref, ib))
            if first:
                kc, vc = load_kv(0, ib), load_kv(1, ib)
                b4 = bias_ref[:, :, BLOCK:].reshape(4 * BLOCK, BLOCK)
                krows = pl.ds(0, BLOCK)
            else:
                kc = jnp.concatenate([load_kv(0, ib - 1), load_kv(0, ib)], axis=0)
                vc = jnp.concatenate([load_kv(1, ib - 1), load_kv(1, ib)], axis=0)
                b4 = bias_ref[...].reshape(4 * BLOCK, 2 * BLOCK)
                krows = pl.ds(pl.multiple_of((ib - 1) * BLOCK, BLOCK), 2 * BLOCK)
            nk = BLOCK if first else 2 * BLOCK
            p = jnp.exp(_dot_nt(q4, kc) * scale + b4 - l4)
            ds = p * (_dot_nt(dy4, vc) - c4)
            ds3 = ds.reshape(4, BLOCK, nk)
            if n_bias == 8:
                if first:
                    db_ref[pl.ds(hb, 4), :, BLOCK:] += ds3
                else:
                    db_ref[pl.ds(hb, 4)] += ds3
            elif first:
                db_ref[:, :, BLOCK:] += ds3
            else:
                db_ref[...] += ds3
            ds16 = ds.astype(BF16)
            dq = _unstack_heads(_dot(ds16, kc), lane_head) * scale
            _store_rows(dq_ref, _block_rows(row0, stride, ib), dq.astype(g_dtype), split)
            dk_acc[krows, :] += _dot_tn(ds16, q4) * scale
            dv_acc[krows, :] += _dot_tn(p.astype(BF16), dy4)
            if has_sink:
                for h in range(4):
                    hs = slice(h * BLOCK, (h + 1) * BLOCK)
                    sk = sink_ref[0, h:h + 1, 0:1]
                    val = -jnp.sum(jnp.exp(sk - l4[hs]) * c4[hs], axis=0, keepdims=True)
                    dsk_ref[hb + h] += jnp.broadcast_to(val, (8, 128))

        block(0, True)
        if nb > 1:
            def step(i, carry):
                block(i, False)
                return carry
            lax.fori_loop(1, nb, step, 0)

        if kvw == GW:
            all_rows = pl.ds(row0, ln, stride=stride) if stride > 1 else pl.ds(0, ln)
            _store_rows(dk_ref, all_rows, dk_acc[...].astype(g_dtype), split)
            _store_rows(dv_ref, all_rows, dv_acc[...].astype(g_dtype), split)
        else:
            def fold(acc):
                t2 = acc[:, :2 * HEAD_DIM] + acc[:, 2 * HEAD_DIM:]
                t2 = t2 + pltpu.roll(t2, HEAD_DIM, 1)
                lane = lax.broadcasted_iota(jnp.int32, t2.shape, 1) // HEAD_DIM
                return jnp.where(lane == rr, t2, 0.0)

            @pl.when(rr == 0)
            def _():
                dk_half[...] = fold(dk_acc[...])
                dv_half[...] = fold(dv_acc[...])

            @pl.when(rr == 1)
            def _():
                dk_ref[0] = (dk_half[...] + fold(dk_acc[...])).astype(g_dtype)
                dv_ref[0] = (dv_half[...] + fold(dv_acc[...])).astype(g_dtype)

    return pl.pallas_call(
        body, name=name, grid=grid,
        out_shape=(jax.ShapeDtypeStruct(dq_shape, g_dtype), jax.ShapeDtypeStruct(dkv_shape, g_dtype),
                   jax.ShapeDtypeStruct(dkv_shape, g_dtype), jax.ShapeDtypeStruct((n_bias, BLOCK, 2 * BLOCK), F32),
                   jax.ShapeDtypeStruct((8, 8, 128), F32)),
        in_specs=[q_spec, k_spec, v_spec,
                  pl.BlockSpec((4, BLOCK, 2 * BLOCK), bias_map), pl.BlockSpec((1, 4, 128), sink_map),
                  o_spec, o_spec, o_spec],
        out_specs=(o_spec, kv_out_spec, kv_out_spec,
                   pl.BlockSpec((n_bias, BLOCK, 2 * BLOCK), lambda n, r: (0, 0, 0)),
                   pl.BlockSpec((8, 8, 128), lambda n, r: (0, 0, 0))),
        scratch_shapes=[pltpu.VMEM((ln, GW), F32), pltpu.VMEM((ln, GW), F32),
                        pltpu.VMEM((ln, 2 * HEAD_DIM), F32), pltpu.VMEM((ln, 2 * HEAD_DIM), F32)]
        + ([pltpu.VMEM((seq, GW), BF16)] * 2 if expanded else []),
        compiler_params=_params(("arbitrary", "arbitrary"), VMEM_BIG),
    )(q_arr, k_arr, v_arr, bias, sink, dy, cc, lse)


def _bias_grad(ds_all, buckets):
    def body(ds_ref, bk_ref, o_ref):
        rows = lax.broadcasted_iota(jnp.int32, (N_BUCKETS, 128), 0)
        cols = lax.broadcasted_iota(jnp.int32, (N_BUCKETS, 128), 1)

        def per_bucket(b, acc):
            for h in range(20):
                gi = h // 4 if h < 12 else 3
                v = jnp.where(bk_ref[gi] == b, ds_ref[h], 0.0)
                v = jnp.sum(jnp.sum(v, axis=1, keepdims=True), axis=0, keepdims=True)
                acc = jnp.where((rows == b) & (cols == h), v, acc)
            return acc

        o_ref[...] = lax.fori_loop(0, N_BUCKETS, per_bucket, jnp.zeros((N_BUCKETS, 128), F32))

    vm = pl.BlockSpec(memory_space=pltpu.VMEM)
    return pl.pallas_call(body, name="bias_grad", out_shape=jax.ShapeDtypeStruct((N_BUCKETS, 128), F32),
                          in_specs=[vm, vm], out_specs=vm)(ds_all, buckets)


def _adamw(w, g, m, v, name):
    r, c = w.shape
    tr = r
    for cand in (256, 176, 128, 64, 32, 16, 8):
        if r % cand == 0:
            tr = cand
            break
    bc1 = 1.0 - ADAM_B1 ** ADAM_STEP
    bc2 = 1.0 - ADAM_B2 ** ADAM_STEP

    def body(w_ref, g_ref, m_ref, v_ref, d_ref, nm_ref, nv_ref):
        gv = g_ref[...]
        nm = ADAM_B1 * m_ref[...] + (1.0 - ADAM_B1) * gv
        nv = ADAM_B2 * v_ref[...] + (1.0 - ADAM_B2) * (gv * gv)
        nm_ref[...] = nm
        nv_ref[...] = nv
        d_ref[...] = -ADAM_LR * ((nm / bc1) / (jnp.sqrt(nv / bc2) + ADAM_EPS) + ADAM_WD * w_ref[...])

    spec = pl.BlockSpec((tr, c), lambda i: (i, 0))
    shp = jax.ShapeDtypeStruct((r, c), F32)
    return pl.pallas_call(body, name=name, grid=(r // tr,), out_shape=(shp, shp, shp),
                          in_specs=[spec] * 4, out_specs=(spec, spec, spec),
                          compiler_params=_params(("parallel",)))(w, g, m, v)


def _t5_bucket(dist):
    max_exact = N_BUCKETS // 2
    n = jnp.maximum(dist, 0)
    nf = jnp.maximum(n, 1).astype(F32)
    large = max_exact + (jnp.log(nf / max_exact) / math.log(MAX_DISTANCE / max_exact)
                         * (N_BUCKETS - max_exact)).astype(jnp.int32)
    large = jnp.minimum(large, N_BUCKETS - 1)
    return jnp.where(n < max_exact, n, large)


def _bias_tables(rel_bias):
    qi = jnp.arange(BLOCK)[:, None]
    ki = jnp.arange(2 * BLOCK)[None, :]
    dist = qi + BLOCK - ki
    specs = [(d, w // d, 4 * gi, 4 * gi + 4) for gi, (w, d) in enumerate(DIL_GROUPS)] + [(1, B_WINDOW - 1, 12, 20)]
    biases, buckets = [], []
    for stride, steps, h0, h1 in specs:
        valid = (dist >= 0) & (dist <= steps)
        bk = jnp.where(valid, _t5_bucket(dist * stride), -1).astype(jnp.int32)
        onehot = (bk[None, :, :] == jnp.arange(N_BUCKETS, dtype=jnp.int32)[:, None, None]).astype(F32)
        b = jnp.einsum("bqk,bh->hqk", onehot, rel_bias[:, h0:h1], precision=lax.Precision.HIGHEST)
        biases.append(jnp.where(valid[None], b, NEG))
        buckets.append(bk)
    return jnp.concatenate(biases, axis=0), jnp.stack(buckets, axis=0)


def _local_step(x, tgt, W, S, shards=None):
    nseq, seq, _ = x.shape
    t = nseq * seq
    xf = x.reshape(t, D_MODEL)
    bias_all, buckets = _bias_tables(S["rel_bias"])
    sink_b = jnp.broadcast_to(S["sinks"].reshape(2, 4, 1), (2, 4, 128)).astype(F32)
    sink_0 = jnp.zeros((1, 4, 128), F32)
    dist = shards is not None
    W = dict(W)
    G, GS, reduced = {}, {}, {}

    def put(keys, gathered):
        for k, g in zip(keys, gathered):
            W[k] = g.reshape(_FULL_SHAPE.get(k, (N_CHIPS * shards[k].shape[0], D_MODEL)))

    def gather_rider(keys):
        return _GatherRider([shards[k] for k in keys]) if dist else None

    def pair(keys):
        return _pair_reduce([G[k].reshape(N_CHIPS, 2, shards[k].shape[0] // 2, D_MODEL) for k in keys],
                            "grad_pair_reduce_" + keys[0])

    def finish(keys, own, rec):
        full = _final_reduce(own, rec, "grad_final_reduce_" + keys[0])
        off = 0
        for k in keys:
            r = shards[k].shape[0]
            reduced[k] = full[:, off:off + r // 2].reshape(r, D_MODEL)
            off += r // 2

    if dist:
        first = ("wgt1", "wut1", "wd1")
        put(first, _gather_rows([shards[k] for k in first]))
    keys = ("wint",)
    (h1, n1, g1, u1), ro = _ffn_fwd(xf, S["ffn1_norm"], W["wgt1"], W["wut1"], W["wd1"], rider=gather_rider(keys))
    put(keys, ro)
    keys = ("wout", "wat", "wbt", "wgt2")
    (un, za, zb, zg), ro = _inproj_fwd(h1, S["mix_norm"], W["wint"], S["b_in"], nseq, rider=gather_rider(keys))
    put(keys, ro)

    seq3 = lambda a: a.reshape(nseq, seq, a.shape[-1])
    zb3 = seq3(zb)
    pair_blk = lambda cb: pl.BlockSpec((1, 2, seq, 128), lambda n, r, cb=cb: (n, cb, 0, 0))
    a_cfg = []
    outs, lses = [], []
    for gi, (_, d) in enumerate(DIL_GROUPS):
        cfg = dict(grid=(nseq, d), seq=seq, stride=d, kvw=GW, split=True,
                   q_spec=pair_blk(gi), k_spec=pair_blk(3 + gi), v_spec=pair_blk(6 + gi), o_spec=pair_blk(0),
                   bias_map=lambda n, r: (0, 0, 0), sink_map=lambda n, r: (0, 0, 0), has_sink=False)
        a_cfg.append(cfg)
        (o, lse), _ = _attn_fwd(za, za, za, bias_all[4 * gi:4 * gi + 4], sink_0, o_shape=(nseq, 2, seq, 128),
                                o_dtype=F32, name=f"attn_a{gi}_fwd", **cfg)
        outs.append(o)
        lses.append(lse)
    wide_blk = lambda w, cmap: pl.BlockSpec((1, seq, w), cmap)
    b_cfg = dict(grid=(nseq, 2), seq=seq, stride=1, kvw=2 * HEAD_DIM, split=False,
                 q_spec=wide_blk(GW, lambda n, r: (n, 0, r)), k_spec=wide_blk(2 * HEAD_DIM, lambda n, r: (n, 0, 4)),
                 v_spec=wide_blk(2 * HEAD_DIM, lambda n, r: (n, 0, 5)), o_spec=wide_blk(GW, lambda n, r: (n, 0, r)),
                 bias_map=lambda n, r: (r, 0, 0), sink_map=lambda n, r: (r, 0, 0), has_sink=True)
    keys = ("wut2",)
    bias_b_fwd = bias_all[12:20].at[:, :, 0].set(jnp.broadcast_to(S["sinks"].reshape(8, 1), (8, BLOCK)))
    (yb, lse_b), ro = _attn_fwd(zb3, zb3, zb3, bias_b_fwd, sink_b, o_shape=(nseq, seq, 2 * GW), o_dtype=BF16,
                                name="attn_b_fwd", rider=gather_rider(keys), **b_cfg)
    put(keys, ro)
    yb = yb.reshape(t, 2 * GW)

    keys = ("wd2",)
    (h2, y, lse_tot, pa, pb, merged), ro = _merge_fwd(outs[0], outs[1], outs[2], lses[0], lses[1], lses[2], yb, zg, h1,
                                                      W["wat"], W["wbt"], W["wout"], rider=gather_rider(keys))
    put(keys, ro)
    (h3, n2, g2, u2), _ = _ffn_fwd(h2, S["ffn2_norm"], W["wgt2"], W["wut2"], W["wd2"])
    dh3, loss_part, g_final = _loss_head(h3, S["final_norm"].reshape(1, D_MODEL), tgt.reshape(t, D_MODEL))

    GS["final_norm"] = g_final
    dh2, dg2, du2, a2, df2, GS["ffn2_norm"] = _ffn_bwd(dh3, h2, S["ffn2_norm"], g2, u2, W["wgt2"], W["wut2"], W["wd2"])
    G["wgt2"] = _wgrad(dg2, n2, MXU_DIM, name="wgrad_gate2")
    G["wut2"] = _wgrad(du2, n2, MXU_DIM, name="wgrad_up2")
    G["wd2"] = _wgrad(a2, df2, MXU_DIM, name="wgrad_down2")

    keys = ("wgt2", "wut2", "wd2")
    rider = _ExchangeRider([pair(keys)]) if dist else None
    (dpa, dpb, dga, dgb, dya, dyb, dh2b, ca, cb), ro = _merge_bwd(dh2, pa, pb, zg, y, yb, W["wat"], W["wbt"], W["wout"],
                                                                  nseq, rider=rider)
    if dist:
        finish(keys, *ro)
    G["wout"] = _wgrad(merged, dh2b, MXU_DIM, name="wgrad_out")
    G["wat"] = _wgrad(dpa, y, MXU_DIM, name="wgrad_branch_a")
    G["wbt"] = _wgrad(dpb, yb, MXU_DIM, name="wgrad_branch_b")

    dqs, dks, dvs, dbs = [], [], [], []
    shp = (nseq, 2, seq, 128)
    halves = lambda a: [a[:, hf].reshape(t, 128).astype(BF16) for hf in range(2)]
    for gi in range(len(DIL_GROUPS)):
        dq, dk, dv, db, _ = _attn_bwd(za, za, za, bias_all[4 * gi:4 * gi + 4], sink_0, dya, ca, lse_tot,
                                      n_bias=4, dq_shape=shp, dkv_shape=shp, g_dtype=F32,
                                      kv_out_spec=a_cfg[gi]["o_spec"], name=f"attn_a{gi}_bwd", **a_cfg[gi])
        dqs += halves(dq)
        dks += halves(dk)
        dvs += halves(dv)
        dbs.append(db)
    dqb, dkb, dvb, dbb, dsink = _attn_bwd(zb3, zb3, zb3, bias_all[12:20], sink_b, seq3(dyb), seq3(cb), lse_b,
                                          n_bias=8, dq_shape=(nseq, seq, 2 * GW),
                                          dkv_shape=(nseq, seq, 2 * HEAD_DIM), g_dtype=BF16,
                                          kv_out_spec=wide_blk(2 * HEAD_DIM, lambda n, r: (n, 0, 0)),
                                          name="attn_b_bwd", **b_cfg)
    dz = jnp.concatenate(dqs + dks + dvs + [dqb.reshape(t, 2 * GW), dkb.reshape(t, 2 * HEAD_DIM),
                                            dvb.reshape(t, 2 * HEAD_DIM), dga, dgb], axis=-1)
    gb_tab = _bias_grad(jnp.concatenate(dbs + [dbb], axis=0), buckets)
    GS["rel_bias"] = gb_tab[:, :20]
    GS["sinks"] = dsink[:, 0, 0].reshape(1, 8)

    G["wint"], GS["b_in"] = _wgrad(dz, un, MXU_DIM, with_colsum=True, name="wgrad_in")
    keys = ("wint", "wout", "wat", "wbt")
    rider = _ExchangeRider([pair(keys)]) if dist else None
    (dh1, GS["mix_norm"]), ro = _inproj_bwd(dz, dh2, h1, S["mix_norm"], W["wint"], rider=rider)
    if dist:
        finish(keys, *ro)

    dx, dg1, du1, a1, df1, GS["ffn1_norm"] = _ffn_bwd(dh1, xf, S["ffn1_norm"], g1, u1, W["wgt1"], W["wut1"], W["wd1"])
    G["wgt1"] = _wgrad(dg1, n1, MXU_DIM, name="wgrad_gate1")
    if dist:
        G["wut1"], ro = _wgrad(du1, n1, MXU_DIM, name="wgrad_up1", rider=_ExchangeRider([pair(("wgt1",))]))
        finish(("wgt1",), *ro)
        G["wd1"], ro = _wgrad(a1, df1, MXU_DIM, name="wgrad_down1", rider=_ExchangeRider([pair(("wut1",))]))
        finish(("wut1",), *ro)
        finish(("wd1",), *_chip_exchange([pair(("wd1",))]))
    else:
        G["wut1"] = _wgrad(du1, n1, MXU_DIM, name="wgrad_up1")
        G["wd1"] = _wgrad(a1, df1, MXU_DIM, name="wgrad_down1")
    return loss_part, dx.reshape(x.shape), (reduced if dist else G), GS


_SMALL = ("ffn1_norm", "mix_norm", "ffn2_norm", "final_norm", "b_in", "sinks", "rel_bias")
_ORDER = ("ffn1_norm", "ffn1_w_gate", "ffn1_w_up", "ffn1_w_down", "mix_norm", "w_in", "b_in", "w_branch_a",
          "w_branch_b", "w_out", "sinks", "rel_bias", "ffn2_norm", "ffn2_w_gate", "ffn2_w_up", "ffn2_w_down",
          "final_norm")
_BIG = (("wgt1", "ffn1_w_gate", True, 704), ("wut1", "ffn1_w_up", True, 704), ("wd1", "ffn1_w_down", False, 704),
        ("wint", "w_in", True, 1280), ("wout", "w_out", False, 256), ("wat", "w_branch_a", True, 64),
        ("wbt", "w_branch_b", True, 128), ("wgt2", "ffn2_w_gate", True, 704), ("wut2", "ffn2_w_up", True, 704),
        ("wd2", "ffn2_w_down", False, 704))
_FULL_SHAPE = {"wat": (D_MODEL, GW), "wbt": (D_MODEL, 2 * GW)}


def _pack_small(p, extra=None):
    last = [p["sinks"].reshape(8), p["rel_bias"].reshape(640)]
    used = 648
    if extra is not None:
        last.append(extra.reshape(1))
        used += 1
    last.append(jnp.zeros((D_MODEL - used,), F32))
    rows = [p["ffn1_norm"].reshape(1, D_MODEL), p["mix_norm"].reshape(1, D_MODEL), p["ffn2_norm"].reshape(1, D_MODEL),
            p["final_norm"].reshape(1, D_MODEL), p["b_in"].reshape(5, D_MODEL), jnp.concatenate(last).reshape(1, D_MODEL),
            jnp.zeros((6, D_MODEL), F32)]
    return jnp.concatenate(rows, axis=0)


def _unpack_small(a):
    return {"ffn1_norm": a[0:1], "mix_norm": a[1:2], "ffn2_norm": a[2:3], "final_norm": a[3],
            "b_in": a[4:9].reshape(1, D_IN), "sinks": a[9, 0:8].reshape(1, 8), "rel_bias": a[9, 8:648].reshape(32, 20)}


def kernel(x, ffn1_norm, ffn1_w_gate, ffn1_w_up, ffn1_w_down, mix_norm, w_in, b_in, w_branch_a, w_branch_b, w_out, sinks, rel_bias, ffn2_norm, ffn2_w_gate, ffn2_w_up, ffn2_w_down, final_norm, loss_target, m_ffn1_norm, m_ffn1_w_gate, m_ffn1_w_up, m_ffn1_w_down, m_mix_norm, m_w_in, m_b_in, m_w_branch_a, m_w_branch_b, m_w_out, m_sinks, m_rel_bias, m_ffn2_norm, m_ffn2_w_gate, m_ffn2_w_up, m_ffn2_w_down, m_final_norm, v_ffn1_norm, v_ffn1_w_gate, v_ffn1_w_up, v_ffn1_w_down, v_mix_norm, v_w_in, v_b_in, v_w_branch_a, v_w_branch_b, v_w_out, v_sinks, v_rel_bias, v_ffn2_norm, v_ffn2_w_gate, v_ffn2_w_up, v_ffn2_w_down, v_final_norm):
    args = dict(locals())
    w = {n: args[n] for n in _ORDER}
    m = {n: args["m_" + n] for n in _ORDER}
    v = {n: args["v_" + n] for n in _ORDER}

    shards = {}
    for key, name, transposed, rows in _BIG:
        a = w[name][0]
        a = (a.T if transposed else a).astype(BF16)
        shards[key] = a.reshape(rows, D_MODEL)
    S = {n: w[n] for n in _SMALL}

    loss_part, grad_x, reduced, GS = _local_step(x, loss_target, {}, S, shards)

    small = _allreduce_small(_pack_small(GS, extra=loss_part[0, 0]))
    loss = small[9, 648]

    out_g, out_d, out_m, out_v = {}, {}, {}, {}
    for key, n, transposed, rows in _BIG:
        nat = w[n][0].shape
        if transposed and nat[1] % 128:
            res = _adamw(w[n][0].T, reduced[key], m[n][0].T, v[n][0].T, "adamw_" + n)
            res = [reduced[key].T] + [r.T for r in res]
        else:
            g = reduced[key].reshape(nat[1], nat[0]).T if transposed else reduced[key].reshape(nat)
            res = [g] + list(_adamw(w[n][0], g, m[n][0], v[n][0], "adamw_" + n))
        out_g[n], out_d[n], out_m[n], out_v[n] = [r[None] for r in res]
    d_s, m_s, v_s = _adamw(_pack_small(w), small, _pack_small(m), _pack_small(v), "adamw_small")
    for dst, src in ((out_g, small), (out_d, d_s), (out_m, m_s), (out_v, v_s)):
        dst.update(_unpack_small(src))

    return (loss, grad_x, *[out_g[n] for n in _ORDER], *[out_d[n] for n in _ORDER],
            *[out_m[n] for n in _ORDER], *[out_v[n] for n in _ORDER])
```

```python
import math

import jax
import jax.numpy as jnp
from jax import lax
from jax.experimental import pallas as pl
from jax.experimental.pallas import tpu as pltpu

F32, BF16 = jnp.float32, jnp.bfloat16
MESH = pl.DeviceIdType.MESH

D_MODEL = 1024
D_FF = 2816
D_IN = 5120
HEAD_DIM = 64
BLOCK = 128
DIL_GROUPS = ((128, 1), (512, 4), (2048, 16))
B_WINDOW = 128
N_BUCKETS = 32
MAX_DISTANCE = 2048
EPS = 1e-6
N_CHIPS = 4
GW = 256
ZA_W = 2304
ZB_W = 768
NEG = -1e30

ADAM_LR, ADAM_B1, ADAM_B2, ADAM_EPS, ADAM_WD, ADAM_STEP = 0.001, 0.9, 0.999, 1e-08, 0.01, 10

VMEM_BIG = 56 * 1024 * 1024
TM = 512
TM_BWD = 256
MXU_DIM = 256
FF_BOUNDS = (0, 6 * MXU_DIM, D_FF)
DMA_SPLIT = 8
RESIDUES_PER_STEP = 4


def _dot(a, b):
    return jnp.dot(a, b, preferred_element_type=F32)


def _dot_nt(a, b):
    return lax.dot_general(a, b, (((1,), (1,)), ((), ())), preferred_element_type=F32)


def _dot_tn(a, b):
    return lax.dot_general(a, b, (((0,), (0,)), ((), ())), preferred_element_type=F32)


def _sigmoid(x):
    return 0.5 * jnp.tanh(0.5 * x) + 0.5


def _params(sem, vmem=None):
    return pltpu.CompilerParams(dimension_semantics=sem, vmem_limit_bytes=vmem)


ANY = pl.BlockSpec(memory_space=pl.ANY)


def _me():
    return lax.axis_index("x"), lax.axis_index("y"), lax.axis_index("c")


_CHIP_RELS = ((1, 0), (0, 1), (1, 1))


def _flip(v, f):
    return 1 - v if f else v


def _remote(src, dst, ssem, rsem, peer):
    return pltpu.make_async_remote_copy(src_ref=src, dst_ref=dst, send_sem=ssem, recv_sem=rsem,
                                        device_id=peer, device_id_type=MESH)


def _row_pieces(rows, n):
    step = max(16, -(-rows // n) // 16 * 16)
    out, s = [], 0
    while s < rows:
        out.append((s, min(step, rows - s)))
        s += step
    return out


def _gather_rows(shards):
    nt = len(shards)
    rows = [s.shape[0] for s in shards]

    def body(*refs):
        srcs, outs = refs[:nt], refs[nt:2 * nt]
        ici_s, ici_r, fwd_s, fwd_r, d2d_s, d2d_r, loc = refs[2 * nt:]
        x, y, c = _me()
        j = 2 * x + y
        sib = (x, y, 1 - c)
        nbr = ((1 - x, y, c), (x, 1 - y, c))
        nbr_j = (2 * (1 - x) + y, 2 * x + (1 - y))
        diag_j = 2 * (1 - x) + (1 - y)
        local = [pltpu.make_async_copy(srcs[t], outs[t].at[j], loc.at[t]) for t in range(nt)]
        for cp in local:
            cp.start()
        pending = []
        for a in range(2):
            for t in range(nt):
                half = pl.ds(c * (rows[t] // 2), rows[t] // 2)
                cp = _remote(srcs[t].at[half], outs[t].at[j, half], ici_s.at[2 * t + a], ici_r.at[2 * t + a], nbr[a])
                cp.start()
                pending.append(cp)
        to_sibling = []

        def pass_on(blk, idx):
            cp = _remote(blk, blk, d2d_s.at[idx], d2d_r.at[idx], sib)
            cp.start()
            to_sibling.append(cp)

        for a in range(2):
            for t in range(nt):
                r2, r4 = rows[t] // 2, rows[t] // 4
                half = outs[t].at[nbr_j[a], pl.ds(c * r2, r2)]
                _remote(half, half, ici_s.at[2 * t + a], ici_r.at[2 * t + a], nbr[a]).wait_recv()
                quarter = outs[t].at[nbr_j[a], pl.ds(c * r2 + a * r4, r4)]
                cp = _remote(quarter, quarter, fwd_s.at[2 * t + a], fwd_r.at[2 * t + a], nbr[1 - a])
                cp.start()
                pending.append(cp)
                pass_on(half, 4 * t + a)
        for a in range(2):
            for t in range(nt):
                r2, r4 = rows[t] // 2, rows[t] // 4
                quarter = outs[t].at[diag_j, pl.ds(c * r2 + a * r4, r4)]
                _remote(quarter, quarter, fwd_s.at[2 * t + a], fwd_r.at[2 * t + a], nbr[1 - a]).wait_recv()
                pass_on(quarter, 4 * t + 2 + a)
        for cp in to_sibling:
            cp.wait()
        for cp in pending:
            cp.wait_send()
        for cp in local:
            cp.wait()

    sems = ([pltpu.SemaphoreType.DMA((2 * nt,)) for _ in range(4)] + [pltpu.SemaphoreType.DMA((4 * nt,))] * 2
            + [pltpu.SemaphoreType.DMA((nt,))])
    return pl.pallas_call(
        body, name="gather_weights",
        out_shape=tuple(jax.ShapeDtypeStruct((N_CHIPS,) + s.shape, s.dtype) for s in shards),
        in_specs=[ANY] * nt, out_specs=tuple([ANY] * nt), scratch_shapes=sems,
    )(*shards)


VMEM_WHOLE = pl.BlockSpec(memory_space=pltpu.VMEM)


def _pair_reduce(grads, name):
    nt = len(grads)
    r2 = [g.shape[2] for g in grads]
    off = [sum(r2[:t]) for t in range(nt)]
    tot = sum(r2)

    def body(*refs):
        gs = refs[:nt]
        s_ref, got, ssem, rsem = refs[nt:]
        x, y, c = _me()
        sib = (x, y, 1 - c)
        for t in range(nt):
            for k in range(N_CHIPS):
                _remote(gs[t].at[k, 1 - c], got.at[k, pl.ds(off[t], r2[t])], ssem, rsem, sib).start()
        _remote(got, got, ssem, rsem, sib).wait()
        for t in range(nt):
            for k in range(N_CHIPS):
                rows = slice(off[t], off[t] + r2[t])
                s_ref[k, rows, :] = (gs[t][k, c].astype(F32) + got[k, rows, :].astype(F32)).astype(BF16)

    shp = jax.ShapeDtypeStruct((N_CHIPS, tot, D_MODEL), BF16)
    return pl.pallas_call(
        body, name=name, out_shape=shp, in_specs=[VMEM_WHOLE] * nt, out_specs=VMEM_WHOLE,
        scratch_shapes=[pltpu.VMEM((N_CHIPS, tot, D_MODEL), BF16), pltpu.SemaphoreType.DMA(()),
                        pltpu.SemaphoreType.DMA(())],
        compiler_params=pltpu.CompilerParams(vmem_limit_bytes=VMEM_BIG),
    )(*grads)


def _chip_exchange(parts):
    ng = len(parts)
    r2 = [p.shape[1] for p in parts]
    off = [sum(r2[:g]) for g in range(ng)]
    tot = sum(r2)

    def body(*refs):
        ps = refs[:ng]
        own_ref, rec_ref, ssems, rsems, lsem = refs[ng:]
        x, y, c = _me()
        j = 2 * x + y
        for g in range(ng):
            pltpu.make_async_copy(ps[g].at[j], own_ref.at[pl.ds(off[g], r2[g])], lsem).start()
        for k, (fx, fy) in enumerate(_CHIP_RELS):
            px, py = _flip(x, fx), _flip(y, fy)
            for g in range(ng):
                for st, sz in _row_pieces(r2[g], 2):
                    _remote(ps[g].at[2 * px + py, pl.ds(st, sz)], rec_ref.at[k, pl.ds(off[g] + st, sz)],
                            ssems.at[k], rsems.at[k], (px, py, c)).start()
        for k in range(3):
            _remote(rec_ref.at[k], rec_ref.at[k], ssems.at[k], rsems.at[k], (x, y, c)).wait()
        pltpu.make_async_copy(own_ref, own_ref, lsem).wait()

    return pl.pallas_call(
        body, name="grad_chip_exchange",
        out_shape=(jax.ShapeDtypeStruct((tot, D_MODEL), BF16), jax.ShapeDtypeStruct((3, tot, D_MODEL), BF16)),
        in_specs=[ANY] * ng, out_specs=(ANY, ANY),
        scratch_shapes=[pltpu.SemaphoreType.DMA((3,)), pltpu.SemaphoreType.DMA((3,)), pltpu.SemaphoreType.DMA(())],
    )(*parts)


def _final_reduce(own, rec, name):
    r2 = own.shape[0]
    pieces = _row_pieces(r2, DMA_SPLIT)

    def body(own_ref, rec_ref, o_ref, fbuf, ssem, rsem, lsem):
        x, y, c = _me()
        sib = (x, y, 1 - c)
        for st, sz in pieces:
            rows = slice(st, st + sz)
            fbuf[rows, :] = (own_ref[rows, :].astype(F32) + rec_ref[0, rows, :].astype(F32)
                             + rec_ref[1, rows, :].astype(F32) + rec_ref[2, rows, :].astype(F32))
            pltpu.make_async_copy(fbuf.at[pl.ds(st, sz)], o_ref.at[c, pl.ds(st, sz)], lsem).start()
            _remote(fbuf.at[pl.ds(st, sz)], o_ref.at[c, pl.ds(st, sz)], ssem, rsem, sib).start()
        _remote(fbuf, o_ref.at[c], ssem, rsem, sib).wait()
        pltpu.make_async_copy(fbuf, o_ref.at[c], lsem).wait()

    return pl.pallas_call(
        body, name=name, out_shape=jax.ShapeDtypeStruct((2, r2, D_MODEL), F32),
        in_specs=[VMEM_WHOLE, VMEM_WHOLE], out_specs=ANY,
        scratch_shapes=[pltpu.VMEM((r2, D_MODEL), F32), pltpu.SemaphoreType.DMA(()), pltpu.SemaphoreType.DMA(()),
                        pltpu.SemaphoreType.DMA(())],
        compiler_params=pltpu.CompilerParams(vmem_limit_bytes=VMEM_BIG),
    )(own, rec)


SMALL_ROWS = 48


def _allreduce_small(g_ffn1, g_mix, g_ffn2, g_final, g_bin, dsink, bias_tab, loss_part):
    def body(f1_ref, mx_ref, f2_ref, fn_ref, bi_ref, sk_ref, bt_ref, ls_ref, o_ref, mine, buf, send_sems, recv_sems):
        x, y, c = _me()
        me = 4 * x + 2 * y + c
        mine[...] = jnp.zeros_like(mine)
        for r, ref in enumerate((f1_ref, mx_ref, f2_ref, fn_ref)):
            mine[r:r + 1, :] = ref[...]
        for k in range(D_IN // D_MODEL):
            mine[4 + k:5 + k, :] = bi_ref[:, k * D_MODEL:(k + 1) * D_MODEL]
        lane = lax.broadcasted_iota(jnp.int32, (1, 128), 1)
        row = jnp.where(lane == 8, ls_ref[0:1, :], 0.0)
        for h in range(8):
            row = jnp.where(lane == h, sk_ref[h, 0:1, :], row)
        mine[9:10, 0:128] = row
        mine[16:48, 0:128] = bt_ref[...]
        buf[me] = mine[...]
        copies = []
        for k in range(1, 8):
            peer = (_flip(x, (k >> 2) & 1), _flip(y, (k >> 1) & 1), _flip(c, k & 1))
            cp = _remote(mine, buf.at[me], send_sems.at[k - 1], recv_sems.at[k - 1], peer)
            cp.start()
            copies.append(cp)
        for cp in copies:
            cp.wait()
        acc = buf[0]
        for i in range(1, 8):
            acc = acc + buf[i]
        o_ref[...] = acc

    vm = pl.BlockSpec(memory_space=pltpu.VMEM)
    shape = (SMALL_ROWS, D_MODEL)
    return pl.pallas_call(
        body, name="allreduce_small", out_shape=jax.ShapeDtypeStruct(shape, F32),
        in_specs=[vm] * 8, out_specs=vm,
        scratch_shapes=[pltpu.VMEM(shape, F32), pltpu.VMEM((8,) + shape, F32), pltpu.SemaphoreType.DMA((7,)),
                        pltpu.SemaphoreType.DMA((7,))],
    )(g_ffn1, g_mix, g_ffn2, g_final, g_bin, dsink, bias_tab, loss_part)


def _adam_update(w, g, m, v):
    nm = ADAM_B1 * m + (1.0 - ADAM_B1) * g
    nv = ADAM_B2 * v + (1.0 - ADAM_B2) * (g * g)
    bc1 = 1.0 - ADAM_B1 ** ADAM_STEP
    bc2 = 1.0 - ADAM_B2 ** ADAM_STEP
    return -ADAM_LR * ((nm / bc1) / (jnp.sqrt(nv / bc2) + ADAM_EPS) + ADAM_WD * w), nm, nv


def _adamw_small(packed, w, m, v):
    names = ("ffn1_norm", "mix_norm", "ffn2_norm", "final_norm", "b_in", "sinks", "rel_bias")
    nn = len(names)

    def grad_of(p_ref, name, k=0):
        if name == "b_in":
            return p_ref[4 + k:5 + k, :]
        if name == "sinks":
            return p_ref[9:10, 0:8]
        if name == "rel_bias":
            return p_ref[16:48, 0:20]
        r = names.index(name)
        return p_ref[r:r + 1, :]

    def body(p_ref, *refs):
        ws, ms, vs = refs[:nn], refs[nn:2 * nn], refs[2 * nn:3 * nn]
        outs = refs[3 * nn:]
        for i, name in enumerate(names):
            og, od, om, ov = outs[i], outs[nn + i], outs[2 * nn + i], outs[3 * nn + i]
            pieces = range(D_IN // D_MODEL) if name == "b_in" else (0,)
            for k in pieces:
                sl = (slice(None), slice(k * D_MODEL, (k + 1) * D_MODEL)) if name == "b_in" else (Ellipsis,)
                g = grad_of(p_ref, name, k)
                d, nm, nv = _adam_update(ws[i][sl], g, ms[i][sl], vs[i][sl])
                og[sl], od[sl], om[sl], ov[sl] = g, d, nm, nv

    vm = pl.BlockSpec(memory_space=pltpu.VMEM)
    shapes = [jax.ShapeDtypeStruct(w[n].shape, F32) for n in names]
    res = pl.pallas_call(
        body, name="adamw_small", out_shape=tuple(shapes * 4), in_specs=[vm] * (1 + 3 * nn),
        out_specs=tuple([vm] * (4 * nn)),
    )(packed, *[w[n] for n in names], *[m[n] for n in names], *[v[n] for n in names])
    return [dict(zip(names, res[i * nn:(i + 1) * nn])) for i in range(4)]


class _GatherRider:
    def __init__(self, shards):
        self.inputs = list(shards)
        nt = len(shards)
        self.out_shape = [jax.ShapeDtypeStruct((N_CHIPS,) + s.shape, s.dtype) for s in shards]
        self.scratch = [pltpu.SemaphoreType.DMA((3 * nt,)), pltpu.SemaphoreType.DMA((3 * nt,)),
                        pltpu.SemaphoreType.DMA((nt,))]

    def _copies(self, srcs, outs, sems):
        ici_s, ici_r, loc = sems
        x, y, c = _me()
        j = 2 * x + y
        local = [pltpu.make_async_copy(srcs[t], outs[t].at[j], loc.at[t]) for t in range(len(srcs))]
        remote = []
        for k, (fx, fy) in enumerate(_CHIP_RELS):
            peer = (_flip(x, fx), _flip(y, fy), c)
            for t in range(len(srcs)):
                remote.append(_remote(srcs[t], outs[t].at[j], ici_s.at[3 * t + k], ici_r.at[3 * t + k], peer))
        return local, remote

    def start(self, srcs, outs, sems):
        local, remote = self._copies(srcs, outs, sems)
        for cp in local + remote:
            cp.start()

    def finish(self, srcs, outs, sems):
        local, remote = self._copies(srcs, outs, sems)
        for cp in remote + local:
            cp.wait()


class _ExchangeRider:
    def __init__(self, parts):
        self.inputs = list(parts)
        self.r2 = [p.shape[1] for p in parts]
        self.off = [sum(self.r2[:g]) for g in range(len(parts))]
        tot = sum(self.r2)
        self.out_shape = [jax.ShapeDtypeStruct((tot, D_MODEL), BF16), jax.ShapeDtypeStruct((3, tot, D_MODEL), BF16)]
        self.scratch = [pltpu.SemaphoreType.DMA((3,)), pltpu.SemaphoreType.DMA((3,)), pltpu.SemaphoreType.DMA(())]

    def start(self, ps, outs, sems):
        own_ref, rec_ref = outs
        ssems, rsems, lsem = sems
        x, y, c = _me()
        j = 2 * x + y
        for g in range(len(ps)):
            pltpu.make_async_copy(ps[g].at[j], own_ref.at[pl.ds(self.off[g], self.r2[g])], lsem).start()
        for k, (fx, fy) in enumerate(_CHIP_RELS):
            px, py = _flip(x, fx), _flip(y, fy)
            for g in range(len(ps)):
                for st, sz in _row_pieces(self.r2[g], 2):
                    _remote(ps[g].at[2 * px + py, pl.ds(st, sz)], rec_ref.at[k, pl.ds(self.off[g] + st, sz)],
                            ssems.at[k], rsems.at[k], (px, py, c)).start()

    def finish(self, ps, outs, sems):
        own_ref, rec_ref = outs
        ssems, rsems, lsem = sems
        x, y, c = _me()
        for k in range(3):
            _remote(rec_ref.at[k], rec_ref.at[k], ssems.at[k], rsems.at[k], (x, y, c)).wait()
        pltpu.make_async_copy(own_ref, own_ref, lsem).wait()


def _pallas(body, args, *, name, grid, in_specs, out_specs, out_shape, scratch_shapes=(), sem=None, vmem=None,
            rider=None):
    if rider is None:
        res = pl.pallas_call(body, name=name, grid=grid, in_specs=list(in_specs), out_specs=tuple(out_specs),
                             out_shape=tuple(out_shape), scratch_shapes=list(scratch_shapes),
                             compiler_params=_params(sem, vmem))(*args)
        return tuple(res), ()
    n_in, n_out, n_sc = len(in_specs), len(out_shape), len(scratch_shapes)
    r_in, r_out = len(rider.inputs), len(rider.out_shape)

    def wrapped(*refs):
        ins, rins = refs[:n_in], refs[n_in:n_in + r_in]
        p = n_in + r_in
        outs, routs = refs[p:p + n_out], refs[p + n_out:p + n_out + r_out]
        p += n_out + r_out
        scr, rsems = refs[p:p + n_sc], refs[p + n_sc:]
        first = pl.program_id(0) == 0
        last = pl.program_id(0) == grid[0] - 1
        for a in range(1, len(grid)):
            first = first & (pl.program_id(a) == 0)
            last = last & (pl.program_id(a) == grid[a] - 1)

        @pl.when(first)
        def _():
            rider.start(rins, routs, rsems)

        body(*ins, *outs, *scr)

        @pl.when(last)
        def _():
            rider.finish(rins, routs, rsems)

    res = pl.pallas_call(
        wrapped, name=name, grid=grid, in_specs=list(in_specs) + [ANY] * r_in,
        out_specs=tuple(out_specs) + (ANY,) * r_out, out_shape=tuple(out_shape) + tuple(rider.out_shape),
        scratch_shapes=list(scratch_shapes) + rider.scratch,
        compiler_params=_params(("arbitrary",) * len(grid), vmem))(*args, *rider.inputs)
    return tuple(res[:n_out]), tuple(res[n_out:])


def _ffn_fwd(h, gain, wgt, wut, wd, rider=None):
    t = h.shape[0]

    def body(h_ref, gain_ref, wg_hbm, wu_hbm, wd_hbm, hout_ref, n_ref, g_ref, u_ref, wg_v, wu_v, wd_v):
        @pl.when(pl.program_id(0) == 0)
        def _():
            pltpu.sync_copy(wg_hbm, wg_v)
            pltpu.sync_copy(wu_hbm, wu_v)
            pltpu.sync_copy(wd_hbm, wd_v)

        hh = h_ref[...]
        r = lax.rsqrt(jnp.mean(hh * hh, axis=-1, keepdims=True) + EPS)
        n = (hh * r * gain_ref[...]).astype(BF16)
        n_ref[...] = n
        acc = jnp.zeros((TM, D_MODEL), F32)
        for c0, c1 in zip(FF_BOUNDS[:-1], FF_BOUNDS[1:]):
            sl = slice(c0, c1)
            g = _dot_nt(n, wg_v[sl, :])
            u = _dot_nt(n, wu_v[sl, :])
            g_ref[:, sl] = g.astype(BF16)
            u_ref[:, sl] = u.astype(BF16)
            a = (g * _sigmoid(g) * u).astype(BF16)
            acc = acc + _dot(a, wd_v[sl, :])
        hout_ref[...] = hh + 0.5 * acc

    row = lambda w: pl.BlockSpec((TM, w), lambda i: (i, 0))
    wv = pltpu.VMEM((D_FF, D_MODEL), BF16)
    return _pallas(
        body, (h, gain, wgt, wut, wd), name="ffn_fwd", grid=(t // TM,),
        out_shape=(jax.ShapeDtypeStruct((t, D_MODEL), F32), jax.ShapeDtypeStruct((t, D_MODEL), BF16),
                   jax.ShapeDtypeStruct((t, D_FF), BF16), jax.ShapeDtypeStruct((t, D_FF), BF16)),
        in_specs=[row(D_MODEL), pl.BlockSpec((1, D_MODEL), lambda i: (0, 0)), ANY, ANY, ANY],
        out_specs=(row(D_MODEL), row(D_MODEL), row(D_FF), row(D_FF)),
        scratch_shapes=[wv, wv, wv], sem=("arbitrary",), vmem=VMEM_BIG, rider=rider)


def _ffn_bwd(dhout, h, gain, g, u, wgt, wut, wd):
    t = h.shape[0]
    tm = TM_BWD

    def body(dho_ref, h_ref, gain_ref, g_ref, u_ref, wg_hbm, wu_hbm, wd_hbm,
             dh_ref, dg_ref, du_ref, a_ref, df_ref, gg_ref, wg_v, wu_v, wd_v):
        @pl.when(pl.program_id(0) == 0)
        def _():
            pltpu.sync_copy(wg_hbm, wg_v)
            pltpu.sync_copy(wu_hbm, wu_v)
            pltpu.sync_copy(wd_hbm, wd_v)
            gg_ref[...] = jnp.zeros_like(gg_ref)

        dho = dho_ref[...]
        df = (0.5 * dho).astype(BF16)
        df_ref[...] = df
        dn = jnp.zeros((tm, D_MODEL), F32)
        for c0, c1 in zip(FF_BOUNDS[:-1], FF_BOUNDS[1:]):
            sl = slice(c0, c1)
            da = _dot_nt(df, wd_v[sl, :])
            gv = g_ref[:, sl].astype(F32)
            uv = u_ref[:, sl].astype(F32)
            sg = _sigmoid(gv)
            silu = gv * sg
            dg = (da * uv * (sg * (1.0 + gv * (1.0 - sg)))).astype(BF16)
            du = (da * silu).astype(BF16)
            dg_ref[:, sl] = dg
            du_ref[:, sl] = du
            a_ref[:, sl] = (silu * uv).astype(BF16)
            dn = dn + _dot(dg, wg_v[sl, :]) + _dot(du, wu_v[sl, :])
        hh = h_ref[...]
        r = lax.rsqrt(jnp.mean(hh * hh, axis=-1, keepdims=True) + EPS)
        hn = hh * r
        gg_ref[...] += jnp.sum(dn * hn, axis=0, keepdims=True)
        dng = dn * gain_ref[...]
        dh_ref[...] = dho + r * (dng - hn * jnp.mean(dng * hn, axis=-1, keepdims=True))

    row = lambda w: pl.BlockSpec((tm, w), lambda i: (i, 0))
    vec = pl.BlockSpec((1, D_MODEL), lambda i: (0, 0))
    wv = pltpu.VMEM((D_FF, D_MODEL), BF16)
    return pl.pallas_call(
        body, name="ffn_bwd", grid=(t // tm,),
        out_shape=(jax.ShapeDtypeStruct((t, D_MODEL), F32), jax.ShapeDtypeStruct((t, D_FF), BF16),
                   jax.ShapeDtypeStruct((t, D_FF), BF16), jax.ShapeDtypeStruct((t, D_FF), BF16),
                   jax.ShapeDtypeStruct((t, D_MODEL), BF16), jax.ShapeDtypeStruct((1, D_MODEL), F32)),
        in_specs=[row(D_MODEL), row(D_MODEL), vec, row(D_FF), row(D_FF), ANY, ANY, ANY],
        out_specs=(row(D_MODEL), row(D_FF), row(D_FF), row(D_FF), row(D_MODEL), vec),
        scratch_shapes=[wv, wv, wv],
        compiler_params=_params(("arbitrary",), VMEM_BIG),
    )(dhout, h, gain, g, u, wgt, wut, wd)


def _wgrad(lhs, rhs, rb, with_colsum=False, name="wgrad", rider=None):
    t, k = lhs.shape
    n = rhs.shape[1]

    def body(l_ref, r_ref, o_ref, *rest):
        o_ref[...] = _dot_tn(l_ref[...], r_ref[...]).astype(BF16)
        if with_colsum:
            rest[0][...] = jnp.sum(l_ref[...].astype(F32), axis=0, keepdims=True)

    out_shape = [jax.ShapeDtypeStruct((k, n), BF16)]
    out_specs = [pl.BlockSpec((rb, n), lambda j: (j, 0))]
    if with_colsum:
        out_shape.append(jax.ShapeDtypeStruct((1, k), F32))
        out_specs.append(pl.BlockSpec((1, rb), lambda j: (0, j)))
    res, ro = _pallas(
        body, (lhs, rhs), name=name, grid=(k // rb,), out_shape=tuple(out_shape),
        in_specs=[pl.BlockSpec((t, rb), lambda j: (0, j)), pl.BlockSpec((t, n), lambda j: (0, 0))],
        out_specs=tuple(out_specs), sem=("arbitrary",), vmem=VMEM_BIG, rider=rider)
    if rider is not None:
        return res[0], ro
    return res if with_colsum else res[0]


def _lane_blocks(nseq, seq, nblk, tm=TM):
    spt = seq // tm
    return pl.BlockSpec((1, nblk, tm, 128), lambda i: (i // spt, 0, i % spt, 0))


def _inproj_fwd(h, gain, wint, b_in, nseq, rider=None):
    t = h.shape[0]
    seq = t // nseq
    cut_a = 5 * MXU_DIM
    pieces = ((0, cut_a, 0, 0), (cut_a, ZA_W - cut_a, 0, cut_a), (ZA_W, ZB_W, 1, 0), (ZA_W + ZB_W, 1024, 2, 0),
              (ZA_W + ZB_W + 1024, 1024, 2, 1024))

    def body(h_ref, gain_ref, w_hbm, b_ref, u_ref, za_ref, zb_ref, zg_ref, w_v):
        @pl.when(pl.program_id(0) == 0)
        def _():
            pltpu.sync_copy(w_hbm, w_v)

        hh = h_ref[...]
        r = lax.rsqrt(jnp.mean(hh * hh, axis=-1, keepdims=True) + EPS)
        un = (hh * r * gain_ref[...]).astype(BF16)
        u_ref[...] = un
        outs = (None, zb_ref, zg_ref)
        for c0, cw, oi, o0 in pieces:
            val = _dot_nt(un, w_v[c0:c0 + cw, :]) + b_ref[:, c0:c0 + cw]
            if oi == 0:
                for cb in range(cw // 128):
                    za_ref[0, o0 // 128 + cb] = val[:, cb * 128:(cb + 1) * 128]
            else:
                outs[oi][:, o0:o0 + cw] = val.astype(BF16)

    row = lambda w: pl.BlockSpec((TM, w), lambda i: (i, 0))
    return _pallas(
        body, (h, gain, wint, b_in), name="inproj_fwd", grid=(t // TM,),
        out_shape=(jax.ShapeDtypeStruct((t, D_MODEL), BF16), jax.ShapeDtypeStruct((nseq, ZA_W // 128, seq, 128), F32),
                   jax.ShapeDtypeStruct((t, ZB_W), BF16), jax.ShapeDtypeStruct((t, 2 * D_MODEL), BF16)),
        in_specs=[row(D_MODEL), pl.BlockSpec((1, D_MODEL), lambda i: (0, 0)), ANY,
                  pl.BlockSpec((1, D_IN), lambda i: (0, 0))],
        out_specs=(row(D_MODEL), _lane_blocks(nseq, seq, ZA_W // 128), row(ZB_W), row(2 * D_MODEL)),
        scratch_shapes=[pltpu.VMEM((D_IN, D_MODEL), BF16)], sem=("arbitrary",), vmem=VMEM_BIG, rider=rider)


def _inproj_bwd(dz, dh2, h, gain, wint, rider=None):
    t = h.shape[0]
    nc = 5
    cw = D_IN // nc

    def body(dz_ref, dh2_ref, h_ref, gain_ref, w_hbm, dh_ref, gg_ref, w_v):
        @pl.when(pl.program_id(0) == 0)
        def _():
            pltpu.sync_copy(w_hbm, w_v)
            gg_ref[...] = jnp.zeros_like(gg_ref)

        du = jnp.zeros((TM, D_MODEL), F32)
        for ci in range(nc):
            sl = slice(ci * cw, (ci + 1) * cw)
            du = du + _dot(dz_ref[:, sl], w_v[sl, :])
        hh = h_ref[...]
        r = lax.rsqrt(jnp.mean(hh * hh, axis=-1, keepdims=True) + EPS)
        hn = hh * r
        gg_ref[...] += jnp.sum(du * hn, axis=0, keepdims=True)
        dng = du * gain_ref[...]
        dh_ref[...] = dh2_ref[...] + r * (dng - hn * jnp.mean(dng * hn, axis=-1, keepdims=True))

    row = lambda w: pl.BlockSpec((TM, w), lambda i: (i, 0))
    vec = pl.BlockSpec((1, D_MODEL), lambda i: (0, 0))
    return _pallas(
        body, (dz, dh2, h, gain, wint), name="inproj_bwd", grid=(t // TM,),
        out_shape=(jax.ShapeDtypeStruct((t, D_MODEL), F32), jax.ShapeDtypeStruct((1, D_MODEL), F32)),
        in_specs=[row(D_IN), row(D_MODEL), row(D_MODEL), vec, ANY],
        out_specs=(row(D_MODEL), vec),
        scratch_shapes=[pltpu.VMEM((D_IN, D_MODEL), BF16)], sem=("arbitrary",), vmem=VMEM_BIG, rider=rider)


def _head_sums(x):
    w = x.shape[1]
    i = lax.broadcasted_iota(jnp.int32, (w, w), 0) // HEAD_DIM
    j = lax.broadcasted_iota(jnp.int32, (w, w), 1) // HEAD_DIM
    ones = (i == j).astype(BF16)
    hi = x.astype(BF16)
    r1 = x - hi.astype(F32)
    mid = r1.astype(BF16)
    lo = (r1 - mid.astype(F32)).astype(BF16)
    return _dot(hi, ones) + _dot(mid, ones) + _dot(lo, ones)


def _merge_fwd(o0, o1, o2, l0, l1, l2, yb, zg, h1, wat, wbt, wout, rider=None):
    t = h1.shape[0]
    nseq, _, seq, _ = o0.shape

    def body(o0_ref, o1_ref, o2_ref, l0_ref, l1_ref, l2_ref, yb_ref, ga_ref, gb_ref, h1_ref, wa_ref, wb_ref, wo_ref,
             h2_ref, y_ref, lt_ref, pa_ref, pb_ref, mg_ref):
        wide = lambda ref: jnp.concatenate([ref[0, 0], ref[0, 1]], axis=1)
        la, lb, lc = wide(l0_ref), wide(l1_ref), wide(l2_ref)
        mx = jnp.maximum(jnp.maximum(la, lb), lc)
        ea, eb, ec = jnp.exp(la - mx), jnp.exp(lb - mx), jnp.exp(lc - mx)
        den = ea + eb + ec
        y = (ea * wide(o0_ref) + eb * wide(o1_ref) + ec * wide(o2_ref)) / den
        lt = mx + jnp.log(den)
        lt_ref[0, 0] = lt[:, :128]
        lt_ref[0, 1] = lt[:, 128:]
        yb16 = y.astype(BF16)
        y_ref[...] = yb16
        pa = _dot_nt(yb16, wa_ref[...])
        pb = _dot_nt(yb_ref[...], wb_ref[...])
        pa_ref[...] = pa.astype(BF16)
        pb_ref[...] = pb.astype(BF16)
        mg = (_sigmoid(ga_ref[...].astype(F32)) * pa + _sigmoid(gb_ref[...].astype(F32)) * pb).astype(BF16)
        mg_ref[...] = mg
        h2_ref[...] = h1_ref[...] + _dot(mg, wo_ref[...])

    row = lambda w: pl.BlockSpec((TM, w), lambda i: (i, 0))
    full = lambda a: pl.BlockSpec(a.shape, lambda i: (0, 0))
    gate = lambda cb: pl.BlockSpec((TM, D_MODEL), lambda i: (i, cb))
    return _pallas(
        body, (o0, o1, o2, l0, l1, l2, yb, zg, zg, h1, wat, wbt, wout), name="merge_fwd", grid=(t // TM,),
        out_shape=(jax.ShapeDtypeStruct((t, D_MODEL), F32), jax.ShapeDtypeStruct((t, GW), BF16),
                   jax.ShapeDtypeStruct((nseq, 2, seq, 128), F32), jax.ShapeDtypeStruct((t, D_MODEL), BF16),
                   jax.ShapeDtypeStruct((t, D_MODEL), BF16), jax.ShapeDtypeStruct((t, D_MODEL), BF16)),
        in_specs=[_lane_blocks(nseq, seq, 2)] * 6 + [row(2 * GW), gate(0), gate(1), row(D_MODEL), full(wat), full(wbt),
                                                     full(wout)],
        out_specs=(row(D_MODEL), row(GW), _lane_blocks(nseq, seq, 2), row(D_MODEL), row(D_MODEL), row(D_MODEL)),
        sem=("parallel",), vmem=VMEM_BIG, rider=rider)


def _merge_bwd(dh2, pa, pb, zg, y, yb, wat, wbt, wout, nseq, rider=None):
    t = dh2.shape[0]

    def body(dh2_ref, pa_ref, pb_ref, ga_ref, gb_ref, y_ref, yb_ref, wa_ref, wb_ref, wo_ref,
             dpa_ref, dpb_ref, dga_ref, dgb_ref, dya_ref, dyb_ref, dh2b_ref, ca_ref, cb_ref):
        d16 = dh2_ref[...].astype(BF16)
        dh2b_ref[...] = d16
        dm = _dot_nt(d16, wo_ref[...])
        sa = _sigmoid(ga_ref[...].astype(F32))
        sb = _sigmoid(gb_ref[...].astype(F32))
        dpa = (dm * sa).astype(BF16)
        dpb = (dm * sb).astype(BF16)
        dpa_ref[...] = dpa
        dpb_ref[...] = dpb
        dga_ref[...] = (dm * pa_ref[...].astype(F32) * sa * (1.0 - sa)).astype(BF16)
        dgb_ref[...] = (dm * pb_ref[...].astype(F32) * sb * (1.0 - sb)).astype(BF16)
        dya = _dot(dpa, wa_ref[...])
        dyb = _dot(dpb, wb_ref[...])
        dya_ref[0, 0] = dya[:, :128]
        dya_ref[0, 1] = dya[:, 128:]
        dyb_ref[...] = dyb.astype(BF16)
        ca = _head_sums(dya * y_ref[...].astype(F32))
        ca_ref[0, 0] = ca[:, :128]
        ca_ref[0, 1] = ca[:, 128:]
        cb_ref[...] = _head_sums(dyb * yb_ref[...].astype(F32))

    row = lambda w: pl.BlockSpec((TM, w), lambda i: (i, 0))
    full = lambda a: pl.BlockSpec(a.shape, lambda i: (0, 0))
    gate = lambda cb: pl.BlockSpec((TM, D_MODEL), lambda i: (i, cb))
    bf = lambda w: jax.ShapeDtypeStruct((t, w), BF16)
    lanes = jax.ShapeDtypeStruct((nseq, 2, t // nseq, 128), F32)
    lane_spec = _lane_blocks(nseq, t // nseq, 2)
    return _pallas(
        body, (dh2, pa, pb, zg, zg, y, yb, wat, wbt, wout), name="merge_bwd", grid=(t // TM,),
        out_shape=(bf(D_MODEL), bf(D_MODEL), bf(D_MODEL), bf(D_MODEL), lanes, bf(2 * GW), bf(D_MODEL),
                   lanes, jax.ShapeDtypeStruct((t, 2 * GW), F32)),
        in_specs=[row(D_MODEL), row(D_MODEL), row(D_MODEL), gate(0), gate(1), row(GW), row(2 * GW),
                  full(wat), full(wbt), full(wout)],
        out_specs=(row(D_MODEL), row(D_MODEL), row(D_MODEL), row(D_MODEL), lane_spec, row(2 * GW), row(D_MODEL),
                   lane_spec, row(2 * GW)),
        sem=("parallel",), vmem=VMEM_BIG, rider=rider)


def _loss_head(h3, gain, tgt):
    t = h3.shape[0]

    def body(h_ref, gain_ref, t_ref, dh_ref, loss_ref, gg_ref):
        @pl.when(pl.program_id(0) == 0)
        def _():
            loss_ref[...] = jnp.zeros_like(loss_ref)
            gg_ref[...] = jnp.zeros_like(gg_ref)

        hh = h_ref[...]
        r = lax.rsqrt(jnp.mean(hh * hh, axis=-1, keepdims=True) + EPS)
        hn = hh * r
        err = hn * gain_ref[...] - t_ref[...]
        part = jnp.sum(jnp.sum(err * err, axis=1, keepdims=True), axis=0, keepdims=True)
        loss_ref[...] += (0.5 / D_MODEL) * part
        dy = err * (1.0 / D_MODEL)
        gg_ref[...] += jnp.sum(dy * hn, axis=0, keepdims=True)
        dng = dy * gain_ref[...]
        dh_ref[...] = r * (dng - hn * jnp.mean(dng * hn, axis=-1, keepdims=True))

    row = pl.BlockSpec((TM, D_MODEL), lambda i: (i, 0))
    vec = pl.BlockSpec((1, D_MODEL), lambda i: (0, 0))
    return pl.pallas_call(
        body, name="loss_head", grid=(t // TM,),
        out_shape=(jax.ShapeDtypeStruct((t, D_MODEL), F32), jax.ShapeDtypeStruct((8, 128), F32),
                   jax.ShapeDtypeStruct((1, D_MODEL), F32)),
        in_specs=[row, vec, row], out_specs=(row, pl.BlockSpec((8, 128), lambda i: (0, 0)), vec),
        compiler_params=_params(("arbitrary",)),
    )(h3, gain, tgt)


def _lane_head(rows):
    return lax.broadcasted_iota(jnp.int32, (rows, GW), 1) // HEAD_DIM


def _kv_expand_matrix(r):
    ci = lax.broadcasted_iota(jnp.int32, (2 * HEAD_DIM, GW), 0)
    ji = lax.broadcasted_iota(jnp.int32, (2 * HEAD_DIM, GW), 1)
    return (ci == (ji % HEAD_DIM) + HEAD_DIM * r).astype(BF16)


def _block_rows(row0, stride, ib):
    start = row0 + (stride * BLOCK) * ib
    if stride > 1:
        return pl.ds(start, BLOCK, stride=stride)
    return pl.ds(pl.multiple_of(start, BLOCK), BLOCK)


def _stack_heads(x, lane_head):
    return jnp.concatenate([jnp.where(lane_head == h, x, jnp.zeros_like(x)) for h in range(4)], axis=0)


def _unstack_heads(x4, lane_head):
    out = jnp.zeros((BLOCK, GW), F32)
    for h in range(4):
        out = jnp.where(lane_head == h, x4[h * BLOCK:(h + 1) * BLOCK], out)
    return out


def _load_rows(ref, rows, split):
    if split:
        return jnp.concatenate([ref[0, 0, rows, :], ref[0, 1, rows, :]], axis=1)
    return ref[0, rows, :]


def _store_rows(ref, rows, val, split):
    if split:
        ref[0, 0, rows, :] = val[:, :128]
        ref[0, 1, rows, :] = val[:, 128:]
    else:
        ref[0, rows, :] = val


def _attn_fwd(q_arr, k_arr, v_arr, bias, sink, *, grid, seq, stride, kvw, split, q_spec, k_spec, v_spec, bias_map,
              sink_map, o_spec, has_sink, o_shape, o_dtype, name, rider=None):
    nb = seq // stride // BLOCK
    scale = HEAD_DIM ** -0.5
    expanded = kvw != GW
    rps = RESIDUES_PER_STEP if stride >= 4 * RESIDUES_PER_STEP else 1
    grid = (grid[0], grid[1] // rps)
    assert not has_sink or B_WINDOW - 1 < BLOCK

    def body(q_ref, k_ref, v_ref, bias_ref, sink_ref, o_ref, lse_ref, *kv_x):
        rr = pl.program_id(1)
        lane_head = _lane_head(BLOCK)
        if expanded:
            expand = _kv_expand_matrix(rr)
            kv_x[0][...] = _dot(k_ref[0], expand).astype(BF16)
            kv_x[1][...] = _dot(v_ref[0], expand).astype(BF16)
        for j in range(rps):
            residue(rr * rps + j if stride > 1 else 0, q_ref, k_ref, v_ref, bias_ref, sink_ref, o_ref, lse_ref, kv_x,
                    lane_head)

    def residue(row0, q_ref, k_ref, v_ref, bias_ref, sink_ref, o_ref, lse_ref, kv_x, lane_head):
        def per_head(fn, x):
            return jnp.concatenate([fn(sink_ref[0, h:h + 1, 0:1], x[h * BLOCK:(h + 1) * BLOCK]) for h in range(4)],
                                   axis=0)

        def load(ref, ib):
            return _load_rows(ref, _block_rows(row0, stride, ib), split).astype(BF16)

        def load_kv(which, ib):
            if expanded:
                return kv_x[which][_block_rows(0, 1, ib), :]
            return load((k_ref, v_ref)[which], ib)

        def block(ib, first):
            q4 = _stack_heads(load(q_ref, ib), lane_head)
            if first:
                kc, vc = load_kv(0, ib), load_kv(1, ib)
                b4 = bias_ref[:, :, BLOCK:].reshape(4 * BLOCK, BLOCK)
            else:
                kc = jnp.concatenate([load_kv(0, ib - 1), load_kv(0, ib)], axis=0)
                vc = jnp.concatenate([load_kv(1, ib - 1), load_kv(1, ib)], axis=0)
                b4 = bias_ref[...].reshape(4 * BLOCK, 2 * BLOCK)
                if has_sink:
                    oldest = lax.broadcasted_iota(jnp.int32, kc.shape, 0) == 0
                    kc = jnp.where(oldest, jnp.zeros_like(kc), kc)
                    vc = jnp.where(oldest, jnp.zeros_like(vc), vc)
            s = _dot_nt(q4, kc) * scale + b4
            m = jnp.max(s, axis=-1, keepdims=True)
            if has_sink and first:
                m = per_head(jnp.maximum, m)
            p = jnp.exp(s - m)
            l = jnp.sum(p, axis=-1, keepdims=True)
            if has_sink and first:
                l = l + per_head(lambda sk, mh: jnp.exp(sk - mh), m)
            o4 = _dot(p.astype(BF16), vc) / l
            rows = _block_rows(row0, stride, ib)
            _store_rows(o_ref, rows, _unstack_heads(o4, lane_head).astype(o_dtype), split)
            _store_rows(lse_ref, rows, _unstack_heads(m + jnp.log(l), lane_head), split)

        block(0, True)
        if nb > 1:
            def step(i, carry):
                block(i, False)
                return carry
            lax.fori_loop(1, nb, step, 0)

    return _pallas(
        body, (q_arr, k_arr, v_arr, bias, sink), name=name, grid=grid,
        out_shape=(jax.ShapeDtypeStruct(o_shape, o_dtype), jax.ShapeDtypeStruct(o_shape, F32)),
        in_specs=[q_spec, k_spec, v_spec,
                  pl.BlockSpec((4, BLOCK, 2 * BLOCK), bias_map), pl.BlockSpec((1, 4, 128), sink_map)],
        out_specs=(o_spec, o_spec),
        scratch_shapes=[pltpu.VMEM((seq, GW), BF16)] * 2 if expanded else [],
        sem=("arbitrary", "arbitrary"), vmem=VMEM_BIG, rider=rider)


def _attn_bwd(q_arr, k_arr, v_arr, bias, sink, dy, cc, lse, *, grid, seq, stride, kvw, split, q_spec, k_spec, v_spec,
              bias_map, sink_map, o_spec, kv_out_spec, has_sink, n_bias, dq_shape, dkv_shape, g_dtype, name):
    ln = seq // stride
    nb = ln // BLOCK
    scale = HEAD_DIM ** -0.5
    expanded = kvw != GW
    rps = RESIDUES_PER_STEP if stride >= 4 * RESIDUES_PER_STEP else 1
    grid = (grid[0], grid[1] // rps)

    def body(q_ref, k_ref, v_ref, bias_ref, sink_ref, dy_ref, c_ref, lse_ref,
             dq_ref, dk_ref, dv_ref, db_ref, dsk_ref, dk_acc, dv_acc, dk_half, dv_half, *kv_x):
        rr = pl.program_id(1)

        @pl.when((pl.program_id(0) == 0) & (rr == 0))
        def _():
            db_ref[...] = jnp.zeros_like(db_ref)
            dsk_ref[...] = jnp.zeros_like(dsk_ref)

        if expanded:
            expand = _kv_expand_matrix(rr)
            kv_x[0][...] = _dot(k_ref[0], expand).astype(BF16)
            kv_x[1][...] = _dot(v_ref[0], expand).astype(BF16)
        refs = (q_ref, k_ref, v_ref, bias_ref, sink_ref, dy_ref, c_ref, lse_ref, dq_ref, dk_ref, dv_ref, db_ref,
                dsk_ref, dk_acc, dv_acc, dk_half, dv_half, kv_x)
        for j in range(rps):
            residue(rr, rr * rps + j if stride > 1 else 0, *refs)

    def residue(rr, row0, q_ref, k_ref, v_ref, bias_ref, sink_ref, dy_ref, c_ref, lse_ref,
                dq_ref, dk_ref, dv_ref, db_ref, dsk_ref, dk_acc, dv_acc, dk_half, dv_half, kv_x):
        dk_acc[...] = jnp.zeros_like(dk_acc)
        dv_acc[...] = jnp.zeros_like(dv_acc)
        lane_head = _lane_head(BLOCK)
        hb = 4 * rr if n_bias == 8 else 0

        def load(ref, ib):
            return _load_rows(ref, _block_rows(row0, stride, ib), split)

        def load_kv(which, ib):
            if expanded:
                return kv_x[which][_block_rows(0, 1, ib), :]
            return load((k_ref, v_ref)[which], ib).astype(BF16)

        def head_col(x):
            return jnp.concatenate([x[:, h * HEAD_DIM:h * HEAD_DIM + 1] for h in range(4)], axis=0)

        def block(ib, first):
            q4 = _stack_heads(load(q_ref, ib).astype(BF16), lane_head)
            dy4 = _stack_heads(load(dy_ref, ib).astype(BF16), lane_head)
            c4 = head_col(load(c_ref, ib))
            l4 = head_col(load(lse_ref, ib))
            if first:
                kc, vc = load_kv(0, ib), load_kv(1, ib)
                b4 = bias_ref[:, :, BLOCK:].reshape(4 * BLOCK, BLOCK)
                krows = pl.ds(0, BLOCK)
            else:
                kc = jnp.concatenate([load_kv(0, ib - 1), load_kv(0, ib)], axis=0)
                vc = jnp.concatenate([load_kv(1, ib - 1), load_kv(1, ib)], axis=0)
                b4 = bias_ref[...].reshape(4 * BLOCK, 2 * BLOCK)
                krows = pl.ds(pl.multiple_of((ib - 1) * BLOCK, BLOCK), 2 * BLOCK)
            nk = BLOCK if first else 2 * BLOCK
            p = jnp.exp(_dot_nt(q4, kc) * scale + b4 - l4)
            ds = p * (_dot_nt(dy4, vc) - c4)
            ds3 = ds.reshape(4, BLOCK, nk)
            if n_bias == 8:
                if first:
                    db_ref[pl.ds(hb, 4), :, BLOCK:] += ds3
                else:
                    db_ref[pl.ds(hb, 4)] += ds3
            elif first:
                db_ref[:, :, BLOCK:] += ds3
            else:
                db_ref[...] += ds3
            ds16 = ds.astype(BF16)
            dq = _unstack_heads(_dot(ds16, kc), lane_head) * scale
            _store_rows(dq_ref, _block_rows(row0, stride, ib), dq.astype(g_dtype), split)
            dk_acc[krows, :] += _dot_tn(ds16, q4) * scale
            dv_acc[krows, :] += _dot_tn(p.astype(BF16), dy4)
            if has_sink:
                for h in range(4):
                    hs = slice(h * BLOCK, (h + 1) * BLOCK)
                    sk = sink_ref[0, h:h + 1, 0:1]
                    val = -jnp.sum(jnp.exp(sk - l4[hs]) * c4[hs], axis=0, keepdims=True)
                    dsk_ref[hb + h] += jnp.broadcast_to(val, (8, 128))

        block(0, True)
        if nb > 1:
            def step(i, carry):
                block(i, False)
                return carry
            lax.fori_loop(1, nb, step, 0)

        if kvw == GW:
            all_rows = pl.ds(row0, ln, stride=stride) if stride > 1 else pl.ds(0, ln)
            _store_rows(dk_ref, all_rows, dk_acc[...].astype(g_dtype), split)
            _store_rows(dv_ref, all_rows, dv_acc[...].astype(g_dtype), split)
        else:
            def fold(acc):
                t2 = acc[:, :2 * HEAD_DIM] + acc[:, 2 * HEAD_DIM:]
                t2 = t2 + pltpu.roll(t2, HEAD_DIM, 1)
                lane = lax.broadcasted_iota(jnp.int32, t2.shape, 1) // HEAD_DIM
                return jnp.where(lane == rr, t2, 0.0)

            @pl.when(rr == 0)
            def _():
                dk_half[...] = fold(dk_acc[...])
                dv_half[...] = fold(dv_acc[...])

            @pl.when(rr == 1)
            def _():
                dk_ref[0] = (dk_half[...] + fold(dk_acc[...])).astype(g_dtype)
                dv_ref[0] = (dv_half[...] + fold(dv_acc[...])).astype(g_dtype)

    return pl.pallas_call(
        body, name=name, grid=grid,
        out_shape=(jax.ShapeDtypeStruct(dq_shape, g_dtype), jax.ShapeDtypeStruct(dkv_shape, g_dtype),
                   jax.ShapeDtypeStruct(dkv_shape, g_dtype), jax.ShapeDtypeStruct((n_bias, BLOCK, 2 * BLOCK), F32),
                   jax.ShapeDtypeStruct((8, 8, 128), F32)),
        in_specs=[q_spec, k_spec, v_spec,
                  pl.BlockSpec((4, BLOCK, 2 * BLOCK), bias_map), pl.BlockSpec((1, 4, 128), sink_map),
                  o_spec, o_spec, o_spec],
        out_specs=(o_spec, kv_out_spec, kv_out_spec,
                   pl.BlockSpec((n_bias, BLOCK, 2 * BLOCK), lambda n, r: (0, 0, 0)),
                   pl.BlockSpec((8, 8, 128), lambda n, r: (0, 0, 0))),
        scratch_shapes=[pltpu.VMEM((ln, GW), F32), pltpu.VMEM((ln, GW), F32),
                        pltpu.VMEM((ln, 2 * HEAD_DIM), F32), pltpu.VMEM((ln, 2 * HEAD_DIM), F32)]
        + ([pltpu.VMEM((seq, GW), BF16)] * 2 if expanded else []),
        compiler_params=_params(("arbitrary", "arbitrary"), VMEM_BIG),
    )(q_arr, k_arr, v_arr, bias, sink, dy, cc, lse)


def _bias_grad(ds_all, buckets):
    def body(ds_ref, bk_ref, o_ref):
        rows = lax.broadcasted_iota(jnp.int32, (N_BUCKETS, 128), 0)
        cols = lax.broadcasted_iota(jnp.int32, (N_BUCKETS, 128), 1)

        def per_bucket(b, acc):
            for h in range(20):
                gi = h // 4 if h < 12 else 3
                v = jnp.where(bk_ref[gi] == b, ds_ref[h], 0.0)
                v = jnp.sum(jnp.sum(v, axis=1, keepdims=True), axis=0, keepdims=True)
                acc = jnp.where((rows == b) & (cols == h), v, acc)
            return acc

        o_ref[...] = lax.fori_loop(0, N_BUCKETS, per_bucket, jnp.zeros((N_BUCKETS, 128), F32))

    vm = pl.BlockSpec(memory_space=pltpu.VMEM)
    return pl.pallas_call(body, name="bias_grad", out_shape=jax.ShapeDtypeStruct((N_BUCKETS, 128), F32),
                          in_specs=[vm, vm], out_specs=vm)(ds_all, buckets)


def _adamw(w, g, m, v, name):
    r, c = w.shape
    tr = r
    for cand in (256, 176, 128, 64, 32, 16, 8):
        if r % cand == 0:
            tr = cand
            break
    bc1 = 1.0 - ADAM_B1 ** ADAM_STEP
    bc2 = 1.0 - ADAM_B2 ** ADAM_STEP

    def body(w_ref, g_ref, m_ref, v_ref, d_ref, nm_ref, nv_ref):
        gv = g_ref[...]
        nm = ADAM_B1 * m_ref[...] + (1.0 - ADAM_B1) * gv
        nv = ADAM_B2 * v_ref[...] + (1.0 - ADAM_B2) * (gv * gv)
        nm_ref[...] = nm
        nv_ref[...] = nv
        d_ref[...] = -ADAM_LR * ((nm / bc1) / (jnp.sqrt(nv / bc2) + ADAM_EPS) + ADAM_WD * w_ref[...])

    spec = pl.BlockSpec((tr, c), lambda i: (i, 0))
    shp = jax.ShapeDtypeStruct((r, c), F32)
    return pl.pallas_call(body, name=name, grid=(r // tr,), out_shape=(shp, shp, shp),
                          in_specs=[spec] * 4, out_specs=(spec, spec, spec),
                          compiler_params=_params(("parallel",)))(w, g, m, v)


def _t5_bucket(dist):
    max_exact = N_BUCKETS // 2
    n = jnp.maximum(dist, 0)
    nf = jnp.maximum(n, 1).astype(F32)
    large = max_exact + (jnp.log(nf / max_exact) / math.log(MAX_DISTANCE / max_exact)
                         * (N_BUCKETS - max_exact)).astype(jnp.int32)
    large = jnp.minimum(large, N_BUCKETS - 1)
    return jnp.where(n < max_exact, n, large)


def _bias_tables(rel_bias):
    qi = jnp.arange(BLOCK)[:, None]
    ki = jnp.arange(2 * BLOCK)[None, :]
    dist = qi + BLOCK - ki
    specs = [(d, w // d, 4 * gi, 4 * gi + 4) for gi, (w, d) in enumerate(DIL_GROUPS)] + [(1, B_WINDOW - 1, 12, 20)]
    biases, buckets = [], []
    for stride, steps, h0, h1 in specs:
        valid = (dist >= 0) & (dist <= steps)
        bk = jnp.where(valid, _t5_bucket(dist * stride), -1).astype(jnp.int32)
        onehot = (bk[None, :, :] == jnp.arange(N_BUCKETS, dtype=jnp.int32)[:, None, None]).astype(F32)
        b = jnp.einsum("bqk,bh->hqk", onehot, rel_bias[:, h0:h1], precision=lax.Precision.HIGHEST)
        biases.append(jnp.where(valid[None], b, NEG))
        buckets.append(bk)
    return jnp.concatenate(biases, axis=0), jnp.stack(buckets, axis=0)


def _local_step(x, tgt, W, S, shards=None):
    nseq, seq, _ = x.shape
    t = nseq * seq
    xf = x.reshape(t, D_MODEL)
    bias_all, buckets = _bias_tables(S["rel_bias"])
    sink_b = jnp.broadcast_to(S["sinks"].reshape(2, 4, 1), (2, 4, 128)).astype(F32)
    sink_0 = jnp.zeros((1, 4, 128), F32)
    dist = shards is not None
    W = dict(W)
    G, GS, reduced = {}, {}, {}

    def put(keys, gathered):
        for k, g in zip(keys, gathered):
            W[k] = g.reshape(_FULL_SHAPE.get(k, (N_CHIPS * shards[k].shape[0], D_MODEL)))

    def gather_rider(keys):
        return _GatherRider([shards[k] for k in keys]) if dist else None

    def pair(keys):
        return _pair_reduce([G[k].reshape(N_CHIPS, 2, shards[k].shape[0] // 2, D_MODEL) for k in keys],
                            "grad_pair_reduce_" + keys[0])

    def finish(keys, own, rec):
        full = _final_reduce(own, rec, "grad_final_reduce_" + keys[0])
        off = 0
        for k in keys:
            r = shards[k].shape[0]
            reduced[k] = full[:, off:off + r // 2].reshape(r, D_MODEL)
            off += r // 2

    if dist:
        first = ("wgt1", "wut1", "wd1")
        put(first, _gather_rows([shards[k] for k in first]))
    keys = ("wint",)
    (h1, n1, g1, u1), ro = _ffn_fwd(xf, S["ffn1_norm"], W["wgt1"], W["wut1"], W["wd1"], rider=gather_rider(keys))
    put(keys, ro)
    keys = ("wout", "wat", "wbt", "wgt2")
    (un, za, zb, zg), ro = _inproj_fwd(h1, S["mix_norm"], W["wint"], S["b_in"], nseq, rider=gather_rider(keys))
    put(keys, ro)

    seq3 = lambda a: a.reshape(nseq, seq, a.shape[-1])
    zb3 = seq3(zb)
    pair_blk = lambda cb: pl.BlockSpec((1, 2, seq, 128), lambda n, r, cb=cb: (n, cb, 0, 0))
    a_cfg = []
    outs, lses = [], []
    for gi, (_, d) in enumerate(DIL_GROUPS):
        cfg = dict(grid=(nseq, d), seq=seq, stride=d, kvw=GW, split=True,
                   q_spec=pair_blk(gi), k_spec=pair_blk(3 + gi), v_spec=pair_blk(6 + gi), o_spec=pair_blk(0),
                   bias_map=lambda n, r: (0, 0, 0), sink_map=lambda n, r: (0, 0, 0), has_sink=False)
        a_cfg.append(cfg)
        (o, lse), _ = _attn_fwd(za, za, za, bias_all[4 * gi:4 * gi + 4], sink_0, o_shape=(nseq, 2, seq, 128),
                                o_dtype=F32, name=f"attn_a{gi}_fwd", **cfg)
        outs.append(o)
        lses.append(lse)
    wide_blk = lambda w, cmap: pl.BlockSpec((1, seq, w), cmap)
    b_cfg = dict(grid=(nseq, 2), seq=seq, stride=1, kvw=2 * HEAD_DIM, split=False,
                 q_spec=wide_blk(GW, lambda n, r: (n, 0, r)), k_spec=wide_blk(2 * HEAD_DIM, lambda n, r: (n, 0, 4)),
                 v_spec=wide_blk(2 * HEAD_DIM, lambda n, r: (n, 0, 5)), o_spec=wide_blk(GW, lambda n, r: (n, 0, r)),
                 bias_map=lambda n, r: (r, 0, 0), sink_map=lambda n, r: (r, 0, 0), has_sink=True)
    keys = ("wut2",)
    bias_b_fwd = bias_all[12:20].at[:, :, 0].set(jnp.broadcast_to(S["sinks"].reshape(8, 1), (8, BLOCK)))
    (yb, lse_b), ro = _attn_fwd(zb3, zb3, zb3, bias_b_fwd, sink_b, o_shape=(nseq, seq, 2 * GW), o_dtype=BF16,
                                name="attn_b_fwd", rider=gather_rider(keys), **b_cfg)
    put(keys, ro)
    yb = yb.reshape(t, 2 * GW)

    keys = ("wd2",)
    (h2, y, lse_tot, pa, pb, merged), ro = _merge_fwd(outs[0], outs[1], outs[2], lses[0], lses[1], lses[2], yb, zg, h1,
                                                      W["wat"], W["wbt"], W["wout"], rider=gather_rider(keys))
    put(keys, ro)
    (h3, n2, g2, u2), _ = _ffn_fwd(h2, S["ffn2_norm"], W["wgt2"], W["wut2"], W["wd2"])
    dh3, loss_part, g_final = _loss_head(h3, S["final_norm"].reshape(1, D_MODEL), tgt.reshape(t, D_MODEL))

    GS["final_norm"] = g_final
    dh2, dg2, du2, a2, df2, GS["ffn2_norm"] = _ffn_bwd(dh3, h2, S["ffn2_norm"], g2, u2, W["wgt2"], W["wut2"], W["wd2"])
    G["wgt2"] = _wgrad(dg2, n2, MXU_DIM, name="wgrad_gate2")
    G["wut2"] = _wgrad(du2, n2, MXU_DIM, name="wgrad_up2")
    G["wd2"] = _wgrad(a2, df2, MXU_DIM, name="wgrad_down2")

    keys = ("wgt2", "wut2", "wd2")
    rider = _ExchangeRider([pair(keys)]) if dist else None
    (dpa, dpb, dga, dgb, dya, dyb, dh2b, ca, cb), ro = _merge_bwd(dh2, pa, pb, zg, y, yb, W["wat"], W["wbt"], W["wout"],
                                                                  nseq, rider=rider)
    if dist:
        finish(keys, *ro)
    G["wout"] = _wgrad(merged, dh2b, MXU_DIM, name="wgrad_out")
    G["wat"] = _wgrad(dpa, y, MXU_DIM, name="wgrad_branch_a")
    G["wbt"] = _wgrad(dpb, yb, MXU_DIM, name="wgrad_branch_b")

    dqs, dks, dvs, dbs = [], [], [], []
    shp = (nseq, 2, seq, 128)
    halves = lambda a: [a[:, hf].reshape(t, 128).astype(BF16) for hf in range(2)]
    for gi in range(len(DIL_GROUPS)):
        dq, dk, dv, db, _ = _attn_bwd(za, za, za, bias_all[4 * gi:4 * gi + 4], sink_0, dya, ca, lse_tot,
                                      n_bias=4, dq_shape=shp, dkv_shape=shp, g_dtype=F32,
                                      kv_out_spec=a_cfg[gi]["o_spec"], name=f"attn_a{gi}_bwd", **a_cfg[gi])
        dqs += halves(dq)
        dks += halves(dk)
        dvs += halves(dv)
        dbs.append(db)
    dqb, dkb, dvb, dbb, dsink = _attn_bwd(zb3, zb3, zb3, bias_all[12:20], sink_b, seq3(dyb), seq3(cb), lse_b,
                                          n_bias=8, dq_shape=(nseq, seq, 2 * GW),
                                          dkv_shape=(nseq, seq, 2 * HEAD_DIM), g_dtype=BF16,
                                          kv_out_spec=wide_blk(2 * HEAD_DIM, lambda n, r: (n, 0, 0)),
                                          name="attn_b_bwd", **b_cfg)
    dz = jnp.concatenate(dqs + dks + dvs + [dqb.reshape(t, 2 * GW), dkb.reshape(t, 2 * HEAD_DIM),
                                            dvb.reshape(t, 2 * HEAD_DIM), dga, dgb], axis=-1)
    gb_tab = _bias_grad(jnp.concatenate(dbs + [dbb], axis=0), buckets)
    if dist:
        GS["bias_tab"], GS["sink_tiles"] = gb_tab, dsink
    else:
        GS["rel_bias"] = gb_tab[:, :20]
        GS["sinks"] = dsink[:, 0, 0].reshape(1, 8)

    G["wint"], GS["b_in"] = _wgrad(dz, un, MXU_DIM, with_colsum=True, name="wgrad_in")
    keys = ("wint", "wout", "wat", "wbt")
    rider = _ExchangeRider([pair(keys)]) if dist else None
    (dh1, GS["mix_norm"]), ro = _inproj_bwd(dz, dh2, h1, S["mix_norm"], W["wint"], rider=rider)
    if dist:
        finish(keys, *ro)

    dx, dg1, du1, a1, df1, GS["ffn1_norm"] = _ffn_bwd(dh1, xf, S["ffn1_norm"], g1, u1, W["wgt1"], W["wut1"], W["wd1"])
    G["wgt1"] = _wgrad(dg1, n1, MXU_DIM, name="wgrad_gate1")
    if dist:
        G["wut1"], ro = _wgrad(du1, n1, MXU_DIM, name="wgrad_up1", rider=_ExchangeRider([pair(("wgt1",))]))
        finish(("wgt1",), *ro)
        G["wd1"], ro = _wgrad(a1, df1, MXU_DIM, name="wgrad_down1", rider=_ExchangeRider([pair(("wut1",))]))
        finish(("wut1",), *ro)
        finish(("wd1",), *_chip_exchange([pair(("wd1",))]))
    else:
        G["wut1"] = _wgrad(du1, n1, MXU_DIM, name="wgrad_up1")
        G["wd1"] = _wgrad(a1, df1, MXU_DIM, name="wgrad_down1")
    return loss_part, dx.reshape(x.shape), (reduced if dist else G), GS


_SMALL = ("ffn1_norm", "mix_norm", "ffn2_norm", "final_norm", "b_in", "sinks", "rel_bias")
_ORDER = ("ffn1_norm", "ffn1_w_gate", "ffn1_w_up", "ffn1_w_down", "mix_norm", "w_in", "b_in", "w_branch_a",
          "w_branch_b", "w_out", "sinks", "rel_bias", "ffn2_norm", "ffn2_w_gate", "ffn2_w_up", "ffn2_w_down",
          "final_norm")
_BIG = (("wgt1", "ffn1_w_gate", True, 704), ("wut1", "ffn1_w_up", True, 704), ("wd1", "ffn1_w_down", False, 704),
        ("wint", "w_in", True, 1280), ("wout", "w_out", False, 256), ("wat", "w_branch_a", True, 64),
        ("wbt", "w_branch_b", True, 128), ("wgt2", "ffn2_w_gate", True, 704), ("wut2", "ffn2_w_up", True, 704),
        ("wd2", "ffn2_w_down", False, 704))
_FULL_SHAPE = {"wat": (D_MODEL, GW), "wbt": (D_MODEL, 2 * GW)}


def kernel(x, ffn1_norm, ffn1_w_gate, ffn1_w_up, ffn1_w_down, mix_norm, w_in, b_in, w_branch_a, w_branch_b, w_out, sinks, rel_bias, ffn2_norm, ffn2_w_gate, ffn2_w_up, ffn2_w_down, final_norm, loss_target, m_ffn1_norm, m_ffn1_w_gate, m_ffn1_w_up, m_ffn1_w_down, m_mix_norm, m_w_in, m_b_in, m_w_branch_a, m_w_branch_b, m_w_out, m_sinks, m_rel_bias, m_ffn2_norm, m_ffn2_w_gate, m_ffn2_w_up, m_ffn2_w_down, m_final_norm, v_ffn1_norm, v_ffn1_w_gate, v_ffn1_w_up, v_ffn1_w_down, v_mix_norm, v_w_in, v_b_in, v_w_branch_a, v_w_branch_b, v_w_out, v_sinks, v_rel_bias, v_ffn2_norm, v_ffn2_w_gate, v_ffn2_w_up, v_ffn2_w_down, v_final_norm):
    args = dict(locals())
    w = {n: args[n] for n in _ORDER}
    m = {n: args["m_" + n] for n in _ORDER}
    v = {n: args["v_" + n] for n in _ORDER}

    shards = {}
    for key, name, transposed, rows in _BIG:
        a = w[name][0]
        a = (a.T if transposed else a).astype(BF16)
        shards[key] = a.reshape(rows, D_MODEL)
    S = {n: w[n] for n in _SMALL}

    loss_part, grad_x, reduced, GS = _local_step(x, loss_target, {}, S, shards)

    small = _allreduce_small(GS["ffn1_norm"], GS["mix_norm"], GS["ffn2_norm"], GS["final_norm"], GS["b_in"],
                             GS["sink_tiles"], GS["bias_tab"], loss_part)
    loss = small[9, 8]

    out_g, out_d, out_m, out_v = {}, {}, {}, {}
    for key, n, transposed, rows in _BIG:
        nat = w[n][0].shape
        if transposed and nat[1] % 128:
            res = _adamw(w[n][0].T, reduced[key], m[n][0].T, v[n][0].T, "adamw_" + n)
            res = [reduced[key].T] + [r.T for r in res]
        else:
            g = reduced[key].reshape(nat[1], nat[0]).T if transposed else reduced[key].reshape(nat)
            res = [g] + list(_adamw(w[n][0], g, m[n][0], v[n][0], "adamw_" + n))
        out_g[n], out_d[n], out_m[n], out_v[n] = [r[None] for r in res]
    row = lambda d: {n: (d[n].reshape(1, D_MODEL) if n == "final_norm" else d[n]) for n in _SMALL}
    for dst, src in zip((out_g, out_d, out_m, out_v), _adamw_small(small, row(w), row(m), row(v))):
        dst.update(src)
        dst["final_norm"] = src["final_norm"].reshape(D_MODEL)

    return (loss, grad_x, *[out_g[n] for n in _ORDER], *[out_d[n] for n in _ORDER],
            *[out_m[n] for n in _ORDER], *[out_v[n] for n in _ORDER])
```

```python
import math

import jax
import jax.numpy as jnp
from jax import lax
from jax.experimental import pallas as pl
from jax.experimental.pallas import tpu as pltpu

F32, BF16 = jnp.float32, jnp.bfloat16
MESH = pl.DeviceIdType.MESH

D_MODEL = 1024
D_FF = 2816
D_IN = 5120
HEAD_DIM = 64
BLOCK = 128
DIL_GROUPS = ((128, 1), (512, 4), (2048, 16))
B_WINDOW = 128
N_BUCKETS = 32
MAX_DISTANCE = 2048
EPS = 1e-6
N_CHIPS = 4
GW = 256
ZA_W = 2304
ZB_W = 768
NEG = -1e30

ADAM_LR, ADAM_B1, ADAM_B2, ADAM_EPS, ADAM_WD, ADAM_STEP = 0.001, 0.9, 0.999, 1e-08, 0.01, 10

VMEM_BIG = 56 * 1024 * 1024
TM = 512
TM_BWD = 256
MXU_DIM = 256
FF_BOUNDS = (0, 6 * MXU_DIM, D_FF)
DMA_SPLIT = 8
RESIDUES_PER_STEP = 4


def _dot(a, b):
    return jnp.dot(a, b, preferred_element_type=F32)


def _dot_nt(a, b):
    return lax.dot_general(a, b, (((1,), (1,)), ((), ())), preferred_element_type=F32)


def _dot_tn(a, b):
    return lax.dot_general(a, b, (((0,), (0,)), ((), ())), preferred_element_type=F32)


def _sigmoid(x):
    return 0.5 * jnp.tanh(0.5 * x) + 0.5


def _params(sem, vmem=None):
    return pltpu.CompilerParams(dimension_semantics=sem, vmem_limit_bytes=vmem)


ANY = pl.BlockSpec(memory_space=pl.ANY)


def _me():
    return lax.axis_index("x"), lax.axis_index("y"), lax.axis_index("c")


_CHIP_RELS = ((1, 0), (0, 1), (1, 1))


def _flip(v, f):
    return 1 - v if f else v


def _remote(src, dst, ssem, rsem, peer):
    return pltpu.make_async_remote_copy(src_ref=src, dst_ref=dst, send_sem=ssem, recv_sem=rsem,
                                        device_id=peer, device_id_type=MESH)


def _row_pieces(rows, n):
    step = max(16, -(-rows // n) // 16 * 16)
    out, s = [], 0
    while s < rows:
        out.append((s, min(step, rows - s)))
        s += step
    return out


def _gather_rows(shards):
    nt = len(shards)
    rows = [s.shape[0] for s in shards]

    def body(*refs):
        srcs, outs = refs[:nt], refs[nt:2 * nt]
        halves, quarters = refs[2 * nt:3 * nt], refs[3 * nt:4 * nt]
        ici_s, ici_r, fwd_s, fwd_r, d2d_s, d2d_r, keep, loc = refs[4 * nt:]
        x, y, c = _me()
        j = 2 * x + y
        sib = (x, y, 1 - c)
        nbr = ((1 - x, y, c), (x, 1 - y, c))
        nbr_j = (2 * (1 - x) + y, 2 * x + (1 - y))
        diag_j = 2 * (1 - x) + (1 - y)
        local = [pltpu.make_async_copy(srcs[t], outs[t].at[j], loc.at[t]) for t in range(nt)]
        for cp in local:
            cp.start()
        pending = []
        for a in range(2):
            for t in range(nt):
                half = pl.ds(c * (rows[t] // 2), rows[t] // 2)
                cp = _remote(srcs[t].at[half], halves[t].at[a], ici_s.at[2 * t + a], ici_r.at[2 * t + a], nbr[a])
                cp.start()
                pending.append(cp)
        placed = []

        def place(src, dst_of, idx):
            mine = pltpu.make_async_copy(src, dst_of, keep.at[idx])
            mine.start()
            cp = _remote(src, dst_of, d2d_s.at[idx], d2d_r.at[idx], sib)
            cp.start()
            placed.append((mine, cp))

        for a in range(2):
            for t in range(nt):
                r2, r4 = rows[t] // 2, rows[t] // 4
                got = halves[t].at[a]
                _remote(got, got, ici_s.at[2 * t + a], ici_r.at[2 * t + a], nbr[a]).wait_recv()
                cp = _remote(halves[t].at[a, pl.ds(a * r4, r4)], quarters[t].at[a], fwd_s.at[2 * t + a],
                             fwd_r.at[2 * t + a], nbr[1 - a])
                cp.start()
                pending.append(cp)
                place(got, outs[t].at[nbr_j[a], pl.ds(c * r2, r2)], 4 * t + a)
        for a in range(2):
            for t in range(nt):
                r2, r4 = rows[t] // 2, rows[t] // 4
                got = quarters[t].at[a]
                _remote(got, got, fwd_s.at[2 * t + a], fwd_r.at[2 * t + a], nbr[1 - a]).wait_recv()
                place(got, outs[t].at[diag_j, pl.ds(c * r2 + a * r4, r4)], 4 * t + 2 + a)
        for mine, cp in placed:
            mine.wait()
            cp.wait()
        for cp in pending:
            cp.wait_send()
        for cp in local:
            cp.wait()

    stage = ([pltpu.VMEM((2, r // 2, D_MODEL), BF16) for r in rows] + [pltpu.VMEM((2, r // 4, D_MODEL), BF16) for r in rows])
    sems = ([pltpu.SemaphoreType.DMA((2 * nt,)) for _ in range(4)] + [pltpu.SemaphoreType.DMA((4 * nt,))] * 3
            + [pltpu.SemaphoreType.DMA((nt,))])
    return pl.pallas_call(
        body, name="gather_weights",
        out_shape=tuple(jax.ShapeDtypeStruct((N_CHIPS,) + s.shape, s.dtype) for s in shards),
        in_specs=[ANY] * nt, out_specs=tuple([ANY] * nt), scratch_shapes=stage + sems,
    )(*shards)


VMEM_WHOLE = pl.BlockSpec(memory_space=pltpu.VMEM)


def _pair_reduce(grads, name):
    nt = len(grads)
    r2 = [g.shape[2] for g in grads]
    off = [sum(r2[:t]) for t in range(nt)]
    tot = sum(r2)

    def body(*refs):
        gs = refs[:nt]
        s_ref, got, ssem, rsem = refs[nt:]
        x, y, c = _me()
        sib = (x, y, 1 - c)
        for t in range(nt):
            for k in range(N_CHIPS):
                _remote(gs[t].at[k, 1 - c], got.at[k, pl.ds(off[t], r2[t])], ssem, rsem, sib).start()
        _remote(got, got, ssem, rsem, sib).wait()
        for t in range(nt):
            for k in range(N_CHIPS):
                rows = slice(off[t], off[t] + r2[t])
                s_ref[k, rows, :] = (gs[t][k, c].astype(F32) + got[k, rows, :].astype(F32)).astype(BF16)

    shp = jax.ShapeDtypeStruct((N_CHIPS, tot, D_MODEL), BF16)
    return pl.pallas_call(
        body, name=name, out_shape=shp, in_specs=[VMEM_WHOLE] * nt, out_specs=VMEM_WHOLE,
        scratch_shapes=[pltpu.VMEM((N_CHIPS, tot, D_MODEL), BF16), pltpu.SemaphoreType.DMA(()),
                        pltpu.SemaphoreType.DMA(())],
        compiler_params=pltpu.CompilerParams(vmem_limit_bytes=VMEM_BIG),
    )(*grads)


def _chip_exchange(parts):
    ng = len(parts)
    r2 = [p.shape[1] for p in parts]
    off = [sum(r2[:g]) for g in range(ng)]
    tot = sum(r2)

    def body(*refs):
        ps = refs[:ng]
        own_ref, rec_ref, ssems, rsems, lsem = refs[ng:]
        x, y, c = _me()
        j = 2 * x + y
        for g in range(ng):
            pltpu.make_async_copy(ps[g].at[j], own_ref.at[pl.ds(off[g], r2[g])], lsem).start()
        for k, (fx, fy) in enumerate(_CHIP_RELS):
            px, py = _flip(x, fx), _flip(y, fy)
            for g in range(ng):
                for st, sz in _row_pieces(r2[g], 2):
                    _remote(ps[g].at[2 * px + py, pl.ds(st, sz)], rec_ref.at[k, pl.ds(off[g] + st, sz)],
                            ssems.at[k], rsems.at[k], (px, py, c)).start()
        for k in range(3):
            _remote(rec_ref.at[k], rec_ref.at[k], ssems.at[k], rsems.at[k], (x, y, c)).wait()
        pltpu.make_async_copy(own_ref, own_ref, lsem).wait()

    return pl.pallas_call(
        body, name="grad_chip_exchange",
        out_shape=(jax.ShapeDtypeStruct((tot, D_MODEL), BF16), jax.ShapeDtypeStruct((3, tot, D_MODEL), BF16)),
        in_specs=[ANY] * ng, out_specs=(ANY, ANY),
        scratch_shapes=[pltpu.SemaphoreType.DMA((3,)), pltpu.SemaphoreType.DMA((3,)), pltpu.SemaphoreType.DMA(())],
    )(*parts)


def _final_reduce(own, rec, name):
    r2 = own.shape[0]
    pieces = _row_pieces(r2, DMA_SPLIT)

    def body(own_ref, rec_ref, o_ref, fbuf, ssem, rsem, lsem):
        x, y, c = _me()
        sib = (x, y, 1 - c)
        for st, sz in pieces:
            rows = slice(st, st + sz)
            fbuf[rows, :] = (own_ref[rows, :].astype(F32) + rec_ref[0, rows, :].astype(F32)
                             + rec_ref[1, rows, :].astype(F32) + rec_ref[2, rows, :].astype(F32))
            pltpu.make_async_copy(fbuf.at[pl.ds(st, sz)], o_ref.at[c, pl.ds(st, sz)], lsem).start()
            _remote(fbuf.at[pl.ds(st, sz)], o_ref.at[c, pl.ds(st, sz)], ssem, rsem, sib).start()
        _remote(fbuf, o_ref.at[c], ssem, rsem, sib).wait()
        pltpu.make_async_copy(fbuf, o_ref.at[c], lsem).wait()

    return pl.pallas_call(
        body, name=name, out_shape=jax.ShapeDtypeStruct((2, r2, D_MODEL), F32),
        in_specs=[VMEM_WHOLE, VMEM_WHOLE], out_specs=ANY,
        scratch_shapes=[pltpu.VMEM((r2, D_MODEL), F32), pltpu.SemaphoreType.DMA(()), pltpu.SemaphoreType.DMA(()),
                        pltpu.SemaphoreType.DMA(())],
        compiler_params=pltpu.CompilerParams(vmem_limit_bytes=VMEM_BIG),
    )(own, rec)


SMALL_ROWS = 48


def _allreduce_small(g_ffn1, g_mix, g_ffn2, g_final, g_bin, dsink, bias_tab, loss_part):
    def body(f1_ref, mx_ref, f2_ref, fn_ref, bi_ref, sk_ref, bt_ref, ls_ref, o_ref, mine, buf, send_sems, recv_sems):
        x, y, c = _me()
        me = 4 * x + 2 * y + c
        mine[...] = jnp.zeros_like(mine)
        for r, ref in enumerate((f1_ref, mx_ref, f2_ref, fn_ref)):
            mine[r:r + 1, :] = ref[...]
        for k in range(D_IN // D_MODEL):
            mine[4 + k:5 + k, :] = bi_ref[:, k * D_MODEL:(k + 1) * D_MODEL]
        lane = lax.broadcasted_iota(jnp.int32, (1, 128), 1)
        row = jnp.where(lane == 8, ls_ref[0:1, :], 0.0)
        for h in range(8):
            row = jnp.where(lane == h, sk_ref[h, 0:1, :], row)
        mine[9:10, 0:128] = row
        mine[16:48, 0:128] = bt_ref[...]
        buf[me] = mine[...]
        copies = []
        for k in range(1, 8):
            peer = (_flip(x, (k >> 2) & 1), _flip(y, (k >> 1) & 1), _flip(c, k & 1))
            cp = _remote(mine, buf.at[me], send_sems.at[k - 1], recv_sems.at[k - 1], peer)
            cp.start()
            copies.append(cp)
        for cp in copies:
            cp.wait()
        acc = buf[0]
        for i in range(1, 8):
            acc = acc + buf[i]
        o_ref[...] = acc

    vm = pl.BlockSpec(memory_space=pltpu.VMEM)
    shape = (SMALL_ROWS, D_MODEL)
    return pl.pallas_call(
        body, name="allreduce_small", out_shape=jax.ShapeDtypeStruct(shape, F32),
        in_specs=[vm] * 8, out_specs=vm,
        scratch_shapes=[pltpu.VMEM(shape, F32), pltpu.VMEM((8,) + shape, F32), pltpu.SemaphoreType.DMA((7,)),
                        pltpu.SemaphoreType.DMA((7,))],
    )(g_ffn1, g_mix, g_ffn2, g_final, g_bin, dsink, bias_tab, loss_part)


def _adam_update(w, g, m, v):
    nm = ADAM_B1 * m + (1.0 - ADAM_B1) * g
    nv = ADAM_B2 * v + (1.0 - ADAM_B2) * (g * g)
    bc1 = 1.0 - ADAM_B1 ** ADAM_STEP
    bc2 = 1.0 - ADAM_B2 ** ADAM_STEP
    return -ADAM_LR * ((nm / bc1) / (jnp.sqrt(nv / bc2) + ADAM_EPS) + ADAM_WD * w), nm, nv


def _adamw_small(packed, w, m, v):
    names = ("ffn1_norm", "mix_norm", "ffn2_norm", "final_norm", "b_in", "sinks", "rel_bias")
    nn = len(names)

    def grad_of(p_ref, name, k=0):
        if name == "b_in":
            return p_ref[4 + k:5 + k, :]
        if name == "sinks":
            return p_ref[9:10, 0:8]
        if name == "rel_bias":
            return p_ref[16:48, 0:20]
        r = names.index(name)
        return p_ref[r:r + 1, :]

    def body(p_ref, *refs):
        ws, ms, vs = refs[:nn], refs[nn:2 * nn], refs[2 * nn:3 * nn]
        outs = refs[3 * nn:]
        for i, name in enumerate(names):
            og, od, om, ov = outs[i], outs[nn + i], outs[2 * nn + i], outs[3 * nn + i]
            pieces = range(D_IN // D_MODEL) if name == "b_in" else (0,)
            for k in pieces:
                sl = (slice(None), slice(k * D_MODEL, (k + 1) * D_MODEL)) if name == "b_in" else (Ellipsis,)
                g = grad_of(p_ref, name, k)
                d, nm, nv = _adam_update(ws[i][sl], g, ms[i][sl], vs[i][sl])
                og[sl], od[sl], om[sl], ov[sl] = g, d, nm, nv

    vm = pl.BlockSpec(memory_space=pltpu.VMEM)
    shapes = [jax.ShapeDtypeStruct(w[n].shape, F32) for n in names]
    res = pl.pallas_call(
        body, name="adamw_small", out_shape=tuple(shapes * 4), in_specs=[vm] * (1 + 3 * nn),
        out_specs=tuple([vm] * (4 * nn)),
    )(packed, *[w[n] for n in names], *[m[n] for n in names], *[v[n] for n in names])
    return [dict(zip(names, res[i * nn:(i + 1) * nn])) for i in range(4)]


class _GatherRider:
    def __init__(self, shards):
        self.inputs = list(shards)
        nt = len(shards)
        self.out_shape = [jax.ShapeDtypeStruct((N_CHIPS,) + s.shape, s.dtype) for s in shards]
        self.scratch = [pltpu.SemaphoreType.DMA((3 * nt,)), pltpu.SemaphoreType.DMA((3 * nt,)),
                        pltpu.SemaphoreType.DMA((nt,))]

    def _copies(self, srcs, outs, sems):
        ici_s, ici_r, loc = sems
        x, y, c = _me()
        j = 2 * x + y
        local = [pltpu.make_async_copy(srcs[t], outs[t].at[j], loc.at[t]) for t in range(len(srcs))]
        remote = []
        for k, (fx, fy) in enumerate(_CHIP_RELS):
            peer = (_flip(x, fx), _flip(y, fy), c)
            for t in range(len(srcs)):
                remote.append(_remote(srcs[t], outs[t].at[j], ici_s.at[3 * t + k], ici_r.at[3 * t + k], peer))
        return local, remote

    def start(self, srcs, outs, sems):
        local, remote = self._copies(srcs, outs, sems)
        for cp in local + remote:
            cp.start()

    def finish(self, srcs, outs, sems):
        local, remote = self._copies(srcs, outs, sems)
        for cp in remote + local:
            cp.wait()


class _ExchangeRider:
    def __init__(self, parts):
        self.inputs = list(parts)
        self.r2 = [p.shape[1] for p in parts]
        self.off = [sum(self.r2[:g]) for g in range(len(parts))]
        tot = sum(self.r2)
        self.out_shape = [jax.ShapeDtypeStruct((tot, D_MODEL), BF16), jax.ShapeDtypeStruct((3, tot, D_MODEL), BF16)]
        self.scratch = [pltpu.SemaphoreType.DMA((3,)), pltpu.SemaphoreType.DMA((3,)), pltpu.SemaphoreType.DMA(())]

    def start(self, ps, outs, sems):
        own_ref, rec_ref = outs
        ssems, rsems, lsem = sems
        x, y, c = _me()
        j = 2 * x + y
        for g in range(len(ps)):
            pltpu.make_async_copy(ps[g].at[j], own_ref.at[pl.ds(self.off[g], self.r2[g])], lsem).start()
        for k, (fx, fy) in enumerate(_CHIP_RELS):
            px, py = _flip(x, fx), _flip(y, fy)
            for g in range(len(ps)):
                for st, sz in _row_pieces(self.r2[g], 2):
                    _remote(ps[g].at[2 * px + py, pl.ds(st, sz)], rec_ref.at[k, pl.ds(self.off[g] + st, sz)],
                            ssems.at[k], rsems.at[k], (px, py, c)).start()

    def finish(self, ps, outs, sems):
        own_ref, rec_ref = outs
        ssems, rsems, lsem = sems
        x, y, c = _me()
        for k in range(3):
            _remote(rec_ref.at[k], rec_ref.at[k], ssems.at[k], rsems.at[k], (x, y, c)).wait()
        pltpu.make_async_copy(own_ref, own_ref, lsem).wait()


def _pallas(body, args, *, name, grid, in_specs, out_specs, out_shape, scratch_shapes=(), sem=None, vmem=None,
            rider=None):
    if rider is None:
        res = pl.pallas_call(body, name=name, grid=grid, in_specs=list(in_specs), out_specs=tuple(out_specs),
                             out_shape=tuple(out_shape), scratch_shapes=list(scratch_shapes),
                             compiler_params=_params(sem, vmem))(*args)
        return tuple(res), ()
    n_in, n_out, n_sc = len(in_specs), len(out_shape), len(scratch_shapes)
    r_in, r_out = len(rider.inputs), len(rider.out_shape)

    def wrapped(*refs):
        ins, rins = refs[:n_in], refs[n_in:n_in + r_in]
        p = n_in + r_in
        outs, routs = refs[p:p + n_out], refs[p + n_out:p + n_out + r_out]
        p += n_out + r_out
        scr, rsems = refs[p:p + n_sc], refs[p + n_sc:]
        first = pl.program_id(0) == 0
        last = pl.program_id(0) == grid[0] - 1
        for a in range(1, len(grid)):
            first = first & (pl.program_id(a) == 0)
            last = last & (pl.program_id(a) == grid[a] - 1)

        @pl.when(first)
        def _():
            rider.start(rins, routs, rsems)

        body(*ins, *outs, *scr)

        @pl.when(last)
        def _():
            rider.finish(rins, routs, rsems)

    res = pl.pallas_call(
        wrapped, name=name, grid=grid, in_specs=list(in_specs) + [ANY] * r_in,
        out_specs=tuple(out_specs) + (ANY,) * r_out, out_shape=tuple(out_shape) + tuple(rider.out_shape),
        scratch_shapes=list(scratch_shapes) + rider.scratch,
        compiler_params=_params(("arbitrary",) * len(grid), vmem))(*args, *rider.inputs)
    return tuple(res[:n_out]), tuple(res[n_out:])


def _ffn_fwd(h, gain, wgt, wut, wd, rider=None):
    t = h.shape[0]

    def body(h_ref, gain_ref, wg_hbm, wu_hbm, wd_hbm, hout_ref, n_ref, g_ref, u_ref, wg_v, wu_v, wd_v):
        @pl.when(pl.program_id(0) == 0)
        def _():
            pltpu.sync_copy(wg_hbm, wg_v)
            pltpu.sync_copy(wu_hbm, wu_v)
            pltpu.sync_copy(wd_hbm, wd_v)

        hh = h_ref[...]
        r = lax.rsqrt(jnp.mean(hh * hh, axis=-1, keepdims=True) + EPS)
        n = (hh * r * gain_ref[...]).astype(BF16)
        n_ref[...] = n
        acc = jnp.zeros((TM, D_MODEL), F32)
        for c0, c1 in zip(FF_BOUNDS[:-1], FF_BOUNDS[1:]):
            sl = slice(c0, c1)
            g = _dot_nt(n, wg_v[sl, :])
            u = _dot_nt(n, wu_v[sl, :])
            g_ref[:, sl] = g.astype(BF16)
            u_ref[:, sl] = u.astype(BF16)
            a = (g * _sigmoid(g) * u).astype(BF16)
            acc = acc + _dot(a, wd_v[sl, :])
        hout_ref[...] = hh + 0.5 * acc

    row = lambda w: pl.BlockSpec((TM, w), lambda i: (i, 0))
    wv = pltpu.VMEM((D_FF, D_MODEL), BF16)
    return _pallas(
        body, (h, gain, wgt, wut, wd), name="ffn_fwd", grid=(t // TM,),
        out_shape=(jax.ShapeDtypeStruct((t, D_MODEL), F32), jax.ShapeDtypeStruct((t, D_MODEL), BF16),
                   jax.ShapeDtypeStruct((t, D_FF), BF16), jax.ShapeDtypeStruct((t, D_FF), BF16)),
        in_specs=[row(D_MODEL), pl.BlockSpec((1, D_MODEL), lambda i: (0, 0)), ANY, ANY, ANY],
        out_specs=(row(D_MODEL), row(D_MODEL), row(D_FF), row(D_FF)),
        scratch_shapes=[wv, wv, wv], sem=("arbitrary",), vmem=VMEM_BIG, rider=rider)


def _ffn_bwd(dhout, h, gain, g, u, wgt, wut, wd):
    t = h.shape[0]
    tm = TM_BWD

    def body(dho_ref, h_ref, gain_ref, g_ref, u_ref, wg_hbm, wu_hbm, wd_hbm,
             dh_ref, dg_ref, du_ref, a_ref, df_ref, gg_ref, wg_v, wu_v, wd_v):
        @pl.when(pl.program_id(0) == 0)
        def _():
            pltpu.sync_copy(wg_hbm, wg_v)
            pltpu.sync_copy(wu_hbm, wu_v)
            pltpu.sync_copy(wd_hbm, wd_v)
            gg_ref[...] = jnp.zeros_like(gg_ref)

        dho = dho_ref[...]
        df = (0.5 * dho).astype(BF16)
        df_ref[...] = df
        dn = jnp.zeros((tm, D_MODEL), F32)
        for c0, c1 in zip(FF_BOUNDS[:-1], FF_BOUNDS[1:]):
            sl = slice(c0, c1)
            da = _dot_nt(df, wd_v[sl, :])
            gv = g_ref[:, sl].astype(F32)
            uv = u_ref[:, sl].astype(F32)
            sg = _sigmoid(gv)
            silu = gv * sg
            dg = (da * uv * (sg * (1.0 + gv * (1.0 - sg)))).astype(BF16)
            du = (da * silu).astype(BF16)
            dg_ref[:, sl] = dg
            du_ref[:, sl] = du
            a_ref[:, sl] = (silu * uv).astype(BF16)
            dn = dn + _dot(dg, wg_v[sl, :]) + _dot(du, wu_v[sl, :])
        hh = h_ref[...]
        r = lax.rsqrt(jnp.mean(hh * hh, axis=-1, keepdims=True) + EPS)
        hn = hh * r
        gg_ref[...] += jnp.sum(dn * hn, axis=0, keepdims=True)
        dng = dn * gain_ref[...]
        dh_ref[...] = dho + r * (dng - hn * jnp.mean(dng * hn, axis=-1, keepdims=True))

    row = lambda w: pl.BlockSpec((tm, w), lambda i: (i, 0))
    vec = pl.BlockSpec((1, D_MODEL), lambda i: (0, 0))
    wv = pltpu.VMEM((D_FF, D_MODEL), BF16)
    return pl.pallas_call(
        body, name="ffn_bwd", grid=(t // tm,),
        out_shape=(jax.ShapeDtypeStruct((t, D_MODEL), F32), jax.ShapeDtypeStruct((t, D_FF), BF16),
                   jax.ShapeDtypeStruct((t, D_FF), BF16), jax.ShapeDtypeStruct((t, D_FF), BF16),
                   jax.ShapeDtypeStruct((t, D_MODEL), BF16), jax.ShapeDtypeStruct((1, D_MODEL), F32)),
        in_specs=[row(D_MODEL), row(D_MODEL), vec, row(D_FF), row(D_FF), ANY, ANY, ANY],
        out_specs=(row(D_MODEL), row(D_FF), row(D_FF), row(D_FF), row(D_MODEL), vec),
        scratch_shapes=[wv, wv, wv],
        compiler_params=_params(("arbitrary",), VMEM_BIG),
    )(dhout, h, gain, g, u, wgt, wut, wd)


def _wgrad(lhs, rhs, rb, with_colsum=False, name="wgrad", rider=None):
    t, k = lhs.shape
    n = rhs.shape[1]

    def body(l_ref, r_ref, o_ref, *rest):
        o_ref[...] = _dot_tn(l_ref[...], r_ref[...]).astype(BF16)
        if with_colsum:
            rest[0][...] = jnp.sum(l_ref[...].astype(F32), axis=0, keepdims=True)

    out_shape = [jax.ShapeDtypeStruct((k, n), BF16)]
    out_specs = [pl.BlockSpec((rb, n), lambda j: (j, 0))]
    if with_colsum:
        out_shape.append(jax.ShapeDtypeStruct((1, k), F32))
        out_specs.append(pl.BlockSpec((1, rb), lambda j: (0, j)))
    res, ro = _pallas(
        body, (lhs, rhs), name=name, grid=(k // rb,), out_shape=tuple(out_shape),
        in_specs=[pl.BlockSpec((t, rb), lambda j: (0, j)), pl.BlockSpec((t, n), lambda j: (0, 0))],
        out_specs=tuple(out_specs), sem=("arbitrary",), vmem=VMEM_BIG, rider=rider)
    if rider is not None:
        return res[0], ro
    return res if with_colsum else res[0]


def _lane_blocks(nseq, seq, nblk, tm=TM):
    spt = seq // tm
    return pl.BlockSpec((1, nblk, tm, 128), lambda i: (i // spt, 0, i % spt, 0))


def _inproj_fwd(h, gain, wint, b_in, nseq, rider=None):
    t = h.shape[0]
    seq = t // nseq
    cut_a = 5 * MXU_DIM
    pieces = ((0, cut_a, 0, 0), (cut_a, ZA_W - cut_a, 0, cut_a), (ZA_W, ZB_W, 1, 0), (ZA_W + ZB_W, 1024, 2, 0),
              (ZA_W + ZB_W + 1024, 1024, 2, 1024))

    def body(h_ref, gain_ref, w_hbm, b_ref, u_ref, za_ref, zb_ref, zg_ref, w_v):
        @pl.when(pl.program_id(0) == 0)
        def _():
            pltpu.sync_copy(w_hbm, w_v)

        hh = h_ref[...]
        r = lax.rsqrt(jnp.mean(hh * hh, axis=-1, keepdims=True) + EPS)
        un = (hh * r * gain_ref[...]).astype(BF16)
        u_ref[...] = un
        outs = (None, zb_ref, zg_ref)
        for c0, cw, oi, o0 in pieces:
            val = _dot_nt(un, w_v[c0:c0 + cw, :]) + b_ref[:, c0:c0 + cw]
            if oi == 0:
                for cb in range(cw // 128):
                    za_ref[0, o0 // 128 + cb] = val[:, cb * 128:(cb + 1) * 128]
            else:
                outs[oi][:, o0:o0 + cw] = val.astype(BF16)

    row = lambda w: pl.BlockSpec((TM, w), lambda i: (i, 0))
    return _pallas(
        body, (h, gain, wint, b_in), name="inproj_fwd", grid=(t // TM,),
        out_shape=(jax.ShapeDtypeStruct((t, D_MODEL), BF16), jax.ShapeDtypeStruct((nseq, ZA_W // 128, seq, 128), F32),
                   jax.ShapeDtypeStruct((t, ZB_W), BF16), jax.ShapeDtypeStruct((t, 2 * D_MODEL), BF16)),
        in_specs=[row(D_MODEL), pl.BlockSpec((1, D_MODEL), lambda i: (0, 0)), ANY,
                  pl.BlockSpec((1, D_IN), lambda i: (0, 0))],
        out_specs=(row(D_MODEL), _lane_blocks(nseq, seq, ZA_W // 128), row(ZB_W), row(2 * D_MODEL)),
        scratch_shapes=[pltpu.VMEM((D_IN, D_MODEL), BF16)], sem=("arbitrary",), vmem=VMEM_BIG, rider=rider)


def _inproj_bwd(dz, dh2, h, gain, wint, rider=None):
    t = h.shape[0]
    nc = 5
    cw = D_IN // nc

    def body(dz_ref, dh2_ref, h_ref, gain_ref, w_hbm, dh_ref, gg_ref, w_v):
        @pl.when(pl.program_id(0) == 0)
        def _():
            pltpu.sync_copy(w_hbm, w_v)
            gg_ref[...] = jnp.zeros_like(gg_ref)

        du = jnp.zeros((TM, D_MODEL), F32)
        for ci in range(nc):
            sl = slice(ci * cw, (ci + 1) * cw)
            du = du + _dot(dz_ref[:, sl], w_v[sl, :])
        hh = h_ref[...]
        r = lax.rsqrt(jnp.mean(hh * hh, axis=-1, keepdims=True) + EPS)
        hn = hh * r
        gg_ref[...] += jnp.sum(du * hn, axis=0, keepdims=True)
        dng = du * gain_ref[...]
        dh_ref[...] = dh2_ref[...] + r * (dng - hn * jnp.mean(dng * hn, axis=-1, keepdims=True))

    row = lambda w: pl.BlockSpec((TM, w), lambda i: (i, 0))
    vec = pl.BlockSpec((1, D_MODEL), lambda i: (0, 0))
    return _pallas(
        body, (dz, dh2, h, gain, wint), name="inproj_bwd", grid=(t // TM,),
        out_shape=(jax.ShapeDtypeStruct((t, D_MODEL), F32), jax.ShapeDtypeStruct((1, D_MODEL), F32)),
        in_specs=[row(D_IN), row(D_MODEL), row(D_MODEL), vec, ANY],
        out_specs=(row(D_MODEL), vec),
        scratch_shapes=[pltpu.VMEM((D_IN, D_MODEL), BF16)], sem=("arbitrary",), vmem=VMEM_BIG, rider=rider)


def _head_sums(x):
    w = x.shape[1]
    i = lax.broadcasted_iota(jnp.int32, (w, w), 0) // HEAD_DIM
    j = lax.broadcasted_iota(jnp.int32, (w, w), 1) // HEAD_DIM
    ones = (i == j).astype(BF16)
    hi = x.astype(BF16)
    r1 = x - hi.astype(F32)
    mid = r1.astype(BF16)
    lo = (r1 - mid.astype(F32)).astype(BF16)
    return _dot(hi, ones) + _dot(mid, ones) + _dot(lo, ones)


def _merge_fwd(o0, o1, o2, l0, l1, l2, yb, zg, h1, wat, wbt, wout, rider=None):
    t = h1.shape[0]
    nseq, _, seq, _ = o0.shape

    def body(o0_ref, o1_ref, o2_ref, l0_ref, l1_ref, l2_ref, yb_ref, ga_ref, gb_ref, h1_ref, wa_ref, wb_ref, wo_ref,
             h2_ref, y_ref, lt_ref, pa_ref, pb_ref, mg_ref):
        wide = lambda ref: jnp.concatenate([ref[0, 0], ref[0, 1]], axis=1)
        la, lb, lc = wide(l0_ref), wide(l1_ref), wide(l2_ref)
        mx = jnp.maximum(jnp.maximum(la, lb), lc)
        ea, eb, ec = jnp.exp(la - mx), jnp.exp(lb - mx), jnp.exp(lc - mx)
        den = ea + eb + ec
        y = (ea * wide(o0_ref) + eb * wide(o1_ref) + ec * wide(o2_ref)) / den
        lt = mx + jnp.log(den)
        lt_ref[0, 0] = lt[:, :128]
        lt_ref[0, 1] = lt[:, 128:]
        yb16 = y.astype(BF16)
        y_ref[...] = yb16
        pa = _dot_nt(yb16, wa_ref[...])
        pb = _dot_nt(yb_ref[...], wb_ref[...])
        pa_ref[...] = pa.astype(BF16)
        pb_ref[...] = pb.astype(BF16)
        mg = (_sigmoid(ga_ref[...].astype(F32)) * pa + _sigmoid(gb_ref[...].astype(F32)) * pb).astype(BF16)
        mg_ref[...] = mg
        h2_ref[...] = h1_ref[...] + _dot(mg, wo_ref[...])

    row = lambda w: pl.BlockSpec((TM, w), lambda i: (i, 0))
    full = lambda a: pl.BlockSpec(a.shape, lambda i: (0, 0))
    gate = lambda cb: pl.BlockSpec((TM, D_MODEL), lambda i: (i, cb))
    return _pallas(
        body, (o0, o1, o2, l0, l1, l2, yb, zg, zg, h1, wat, wbt, wout), name="merge_fwd", grid=(t // TM,),
        out_shape=(jax.ShapeDtypeStruct((t, D_MODEL), F32), jax.ShapeDtypeStruct((t, GW), BF16),
                   jax.ShapeDtypeStruct((nseq, 2, seq, 128), F32), jax.ShapeDtypeStruct((t, D_MODEL), BF16),
                   jax.ShapeDtypeStruct((t, D_MODEL), BF16), jax.ShapeDtypeStruct((t, D_MODEL), BF16)),
        in_specs=[_lane_blocks(nseq, seq, 2)] * 6 + [row(2 * GW), gate(0), gate(1), row(D_MODEL), full(wat), full(wbt),
                                                     full(wout)],
        out_specs=(row(D_MODEL), row(GW), _lane_blocks(nseq, seq, 2), row(D_MODEL), row(D_MODEL), row(D_MODEL)),
        sem=("parallel",), vmem=VMEM_BIG, rider=rider)


def _merge_bwd(dh2, pa, pb, zg, y, yb, wat, wbt, wout, nseq, rider=None):
    t = dh2.shape[0]

    def body(dh2_ref, pa_ref, pb_ref, ga_ref, gb_ref, y_ref, yb_ref, wa_ref, wb_ref, wo_ref,
             dpa_ref, dpb_ref, dga_ref, dgb_ref, dya_ref, dyb_ref, dh2b_ref, ca_ref, cb_ref):
        d16 = dh2_ref[...].astype(BF16)
        dh2b_ref[...] = d16
        dm = _dot_nt(d16, wo_ref[...])
        sa = _sigmoid(ga_ref[...].astype(F32))
        sb = _sigmoid(gb_ref[...].astype(F32))
        dpa = (dm * sa).astype(BF16)
        dpb = (dm * sb).astype(BF16)
        dpa_ref[...] = dpa
        dpb_ref[...] = dpb
        dga_ref[...] = (dm * pa_ref[...].astype(F32) * sa * (1.0 - sa)).astype(BF16)
        dgb_ref[...] = (dm * pb_ref[...].astype(F32) * sb * (1.0 - sb)).astype(BF16)
        dya = _dot(dpa, wa_ref[...])
        dyb = _dot(dpb, wb_ref[...])
        dya_ref[0, 0] = dya[:, :128]
        dya_ref[0, 1] = dya[:, 128:]
        dyb_ref[...] = dyb.astype(BF16)
        ca = _head_sums(dya * y_ref[...].astype(F32))
        ca_ref[0, 0] = ca[:, :128]
        ca_ref[0, 1] = ca[:, 128:]
        cb_ref[...] = _head_sums(dyb * yb_ref[...].astype(F32))

    row = lambda w: pl.BlockSpec((TM, w), lambda i: (i, 0))
    full = lambda a: pl.BlockSpec(a.shape, lambda i: (0, 0))
    gate = lambda cb: pl.BlockSpec((TM, D_MODEL), lambda i: (i, cb))
    bf = lambda w: jax.ShapeDtypeStruct((t, w), BF16)
    lanes = jax.ShapeDtypeStruct((nseq, 2, t // nseq, 128), F32)
    lane_spec = _lane_blocks(nseq, t // nseq, 2)
    return _pallas(
        body, (dh2, pa, pb, zg, zg, y, yb, wat, wbt, wout), name="merge_bwd", grid=(t // TM,),
        out_shape=(bf(D_MODEL), bf(D_MODEL), bf(D_MODEL), bf(D_MODEL), lanes, bf(2 * GW), bf(D_MODEL),
                   lanes, jax.ShapeDtypeStruct((t, 2 * GW), F32)),
        in_specs=[row(D_MODEL), row(D_MODEL), row(D_MODEL), gate(0), gate(1), row(GW), row(2 * GW),
                  full(wat), full(wbt), full(wout)],
        out_specs=(row(D_MODEL), row(D_MODEL), row(D_MODEL), row(D_MODEL), lane_spec, row(2 * GW), row(D_MODEL),
                   lane_spec, row(2 * GW)),
        sem=("parallel",), vmem=VMEM_BIG, rider=rider)


def _loss_head(h3, gain, tgt):
    t = h3.shape[0]

    def body(h_ref, gain_ref, t_ref, dh_ref, loss_ref, gg_ref):
        @pl.when(pl.program_id(0) == 0)
        def _():
            loss_ref[...] = jnp.zeros_like(loss_ref)
            gg_ref[...] = jnp.zeros_like(gg_ref)

        hh = h_ref[...]
        r = lax.rsqrt(jnp.mean(hh * hh, axis=-1, keepdims=True) + EPS)
        hn = hh * r
        err = hn * gain_ref[...] - t_ref[...]
        part = jnp.sum(jnp.sum(err * err, axis=1, keepdims=True), axis=0, keepdims=True)
        loss_ref[...] += (0.5 / D_MODEL) * part
        dy = err * (1.0 / D_MODEL)
        gg_ref[...] += jnp.sum(dy * hn, axis=0, keepdims=True)
        dng = dy * gain_ref[...]
        dh_ref[...] = r * (dng - hn * jnp.mean(dng * hn, axis=-1, keepdims=True))

    row = pl.BlockSpec((TM, D_MODEL), lambda i: (i, 0))
    vec = pl.BlockSpec((1, D_MODEL), lambda i: (0, 0))
    return pl.pallas_call(
        body, name="loss_head", grid=(t // TM,),
        out_shape=(jax.ShapeDtypeStruct((t, D_MODEL), F32), jax.ShapeDtypeStruct((8, 128), F32),
                   jax.ShapeDtypeStruct((1, D_MODEL), F32)),
        in_specs=[row, vec, row], out_specs=(row, pl.BlockSpec((8, 128), lambda i: (0, 0)), vec),
        compiler_params=_params(("arbitrary",)),
    )(h3, gain, tgt)


def _lane_head(rows):
    return lax.broadcasted_iota(jnp.int32, (rows, GW), 1) // HEAD_DIM


def _kv_expand_matrix(r):
    ci = lax.broadcasted_iota(jnp.int32, (2 * HEAD_DIM, GW), 0)
    ji = lax.broadcasted_iota(jnp.int32, (2 * HEAD_DIM, GW), 1)
    return (ci == (ji % HEAD_DIM) + HEAD_DIM * r).astype(BF16)


def _block_rows(row0, stride, ib):
    start = row0 + (stride * BLOCK) * ib
    if stride > 1:
        return pl.ds(start, BLOCK, stride=stride)
    return pl.ds(pl.multiple_of(start, BLOCK), BLOCK)


def _stack_heads(x, lane_head):
    return jnp.concatenate([jnp.where(lane_head == h, x, jnp.zeros_like(x)) for h in range(4)], axis=0)


def _unstack_heads(x4, lane_head):
    out = jnp.zeros((BLOCK, GW), F32)
    for h in range(4):
        out = jnp.where(lane_head == h, x4[h * BLOCK:(h + 1) * BLOCK], out)
    return out


def _load_rows(ref, rows, split):
    if split:
        return jnp.concatenate([ref[0, 0, rows, :], ref[0, 1, rows, :]], axis=1)
    return ref[0, rows, :]


def _store_rows(ref, rows, val, split):
    if split:
        ref[0, 0, rows, :] = val[:, :128]
        ref[0, 1, rows, :] = val[:, 128:]
    else:
        ref[0, rows, :] = val


def _attn_fwd(q_arr, k_arr, v_arr, bias, sink, *, grid, seq, stride, kvw, split, q_spec, k_spec, v_spec, bias_map,
              sink_map, o_spec, has_sink, o_shape, o_dtype, name, rider=None):
    nb = seq // stride // BLOCK
    scale = HEAD_DIM ** -0.5
    expanded = kvw != GW
    rps = RESIDUES_PER_STEP if stride >= 4 * RESIDUES_PER_STEP else 1
    grid = (grid[0], grid[1] // rps)
    assert not has_sink or B_WINDOW - 1 < BLOCK

    def body(q_ref, k_ref, v_ref, bias_ref, sink_ref, o_ref, lse_ref, *kv_x):
        rr = pl.program_id(1)
        lane_head = _lane_head(BLOCK)
        if expanded:
            expand = _kv_expand_matrix(rr)
            kv_x[0][...] = _dot(k_ref[0], expand).astype(BF16)
            kv_x[1][...] = _dot(v_ref[0], expand).astype(BF16)
        for j in range(rps):
            residue(rr * rps + j if stride > 1 else 0, q_ref, k_ref, v_ref, bias_ref, sink_ref, o_ref, lse_ref, kv_x,
                    lane_head)

    def residue(row0, q_ref, k_ref, v_ref, bias_ref, sink_ref, o_ref, lse_ref, kv_x, lane_head):
        def per_head(fn, x):
            return jnp.concatenate([fn(sink_ref[0, h:h + 1, 0:1], x[h * BLOCK:(h + 1) * BLOCK]) for h in range(4)],
                                   axis=0)

        def load(ref, ib):
            return _load_rows(ref, _block_rows(row0, stride, ib), split).astype(BF16)

        def load_kv(which, ib):
            if expanded:
                return kv_x[which][_block_rows(0, 1, ib), :]
            return load((k_ref, v_ref)[which], ib)

        def block(ib, first):
            q4 = _stack_heads(load(q_ref, ib), lane_head)
            if first:
                kc, vc = load_kv(0, ib), load_kv(1, ib)
                b4 = bias_ref[:, :, BLOCK:].reshape(4 * BLOCK, BLOCK)
            else:
                kc = jnp.concatenate([load_kv(0, ib - 1), load_kv(0, ib)], axis=0)
                vc = jnp.concatenate([load_kv(1, ib - 1), load_kv(1, ib)], axis=0)
                b4 = bias_ref[...].reshape(4 * BLOCK, 2 * BLOCK)
                if has_sink:
                    oldest = lax.broadcasted_iota(jnp.int32, kc.shape, 0) == 0
                    kc = jnp.where(oldest, jnp.zeros_like(kc), kc)
                    vc = jnp.where(oldest, jnp.zeros_like(vc), vc)
            s = _dot_nt(q4, kc) * scale + b4
            m = jnp.max(s, axis=-1, keepdims=True)
            if has_sink and first:
                m = per_head(jnp.maximum, m)
            p = jnp.exp(s - m)
            l = jnp.sum(p, axis=-1, keepdims=True)
            if has_sink and first:
                l = l + per_head(lambda sk, mh: jnp.exp(sk - mh), m)
            o4 = _dot(p.astype(BF16), vc) / l
            rows = _block_rows(row0, stride, ib)
            _store_rows(o_ref, rows, _unstack_heads(o4, lane_head).astype(o_dtype), split)
            _store_rows(lse_ref, rows, _unstack_heads(m + jnp.log(l), lane_head), split)

        block(0, True)
        if nb > 1:
            def step(i, carry):
                block(i, False)
                return carry
            lax.fori_loop(1, nb, step, 0)

    return _pallas(
        body, (q_arr, k_arr, v_arr, bias, sink), name=name, grid=grid,
        out_shape=(jax.ShapeDtypeStruct(o_shape, o_dtype), jax.ShapeDtypeStruct(o_shape, F32)),
        in_specs=[q_spec, k_spec, v_spec,
                  pl.BlockSpec((4, BLOCK, 2 * BLOCK), bias_map), pl.BlockSpec((1, 4, 128), sink_map)],
        out_specs=(o_spec, o_spec),
        scratch_shapes=[pltpu.VMEM((seq, GW), BF16)] * 2 if expanded else [],
        sem=("arbitrary", "arbitrary"), vmem=VMEM_BIG, rider=rider)


def _attn_bwd(q_arr, k_arr, v_arr, bias, sink, dy, cc, lse, *, grid, seq, stride, kvw, split, q_spec, k_spec, v_spec,
              bias_map, sink_map, o_spec, kv_out_spec, has_sink, n_bias, dq_shape, dkv_shape, g_dtype, name):
    ln = seq // stride
    nb = ln // BLOCK
    scale = HEAD_DIM ** -0.5
    expanded = kvw != GW
    rps = RESIDUES_PER_STEP if stride >= 4 * RESIDUES_PER_STEP else 1
    grid = (grid[0], grid[1] // rps)

    def body(q_ref, k_ref, v_ref, bias_ref, sink_ref, dy_ref, c_ref, lse_ref,
             dq_ref, dk_ref, dv_ref, db_ref, dsk_ref, dk_acc, dv_acc, dk_half, dv_half, *kv_x):
        rr = pl.program_id(1)

        @pl.when((pl.program_id(0) == 0) & (rr == 0))
        def _():
            db_ref[...] = jnp.zeros_like(db_ref)
            dsk_ref[...] = jnp.zeros_like(dsk_ref)

        if expanded:
            expand = _kv_expand_matrix(rr)
            kv_x[0][...] = _dot(k_ref[0], expand).astype(BF16)
            kv_x[1][...] = _dot(v_ref[0], expand).astype(BF16)
        refs = (q_ref, k_ref, v_ref, bias_ref, sink_ref, dy_ref, c_ref, lse_ref, dq_ref, dk_ref, dv_ref, db_ref,
                dsk_ref, dk_acc, dv_acc, dk_half, dv_half, kv_x)
        for j in range(rps):
            residue(rr, rr * rps + j if stride > 1 else 0, *refs)

    def residue(rr, row0, q_ref, k_ref, v_ref, bias_ref, sink_ref, dy_ref, c_ref, lse_ref,
                dq_ref, dk_ref, dv_ref, db_ref, dsk_ref, dk_acc, dv_acc, dk_half, dv_half, kv_x):
        dk_acc[...] = jnp.zeros_like(dk_acc)
        dv_acc[...] = jnp.zeros_like(dv_acc)
        lane_head = _lane_head(BLOCK)
        hb = 4 * rr if n_bias == 8 else 0

        def load(ref, ib):
            return _load_rows(ref, _block_rows(row0, stride, ib), split)

        def load_kv(which, ib):
            if expanded:
                return kv_x[which][_block_rows(0, 1, ib), :]
            return load((k_ref, v_ref)[which], ib).astype(BF16)

        def head_col(x):
            return jnp.concatenate([x[:, h * HEAD_DIM:h * HEAD_DIM + 1] for h in range(4)], axis=0)

        def block(ib, first):
            q4 = _stack_heads(load(q_ref, ib).astype(BF16), lane_head)
            dy4 = _stack_heads(load(dy_ref, ib).astype(BF16), lane_head)
            c4 = head_col(load(c_ref, ib))
            l4 = head_col(load(lse_ref, ib))
            if first:
                kc, vc = load_kv(0, ib), load_kv(1, ib)
                b4 = bias_ref[:, :, BLOCK:].reshape(4 * BLOCK, BLOCK)
                krows = pl.ds(0, BLOCK)
            else:
                kc = jnp.concatenate([load_kv(0, ib - 1), load_kv(0, ib)], axis=0)
                vc = jnp.concatenate([load_kv(1, ib - 1), load_kv(1, ib)], axis=0)
                b4 = bias_ref[...].reshape(4 * BLOCK, 2 * BLOCK)
                krows = pl.ds(pl.multiple_of((ib - 1) * BLOCK, BLOCK), 2 * BLOCK)
            nk = BLOCK if first else 2 * BLOCK
            p = jnp.exp(_dot_nt(q4, kc) * scale + b4 - l4)
            ds = p * (_dot_nt(dy4, vc) - c4)
            ds3 = ds.reshape(4, BLOCK, nk)
            if n_bias == 8:
                if first:
                    db_ref[pl.ds(hb, 4), :, BLOCK:] += ds3
                else:
                    db_ref[pl.ds(hb, 4)] += ds3
            elif first:
                db_ref[:, :, BLOCK:] += ds3
            else:
                db_ref[...] += ds3
            ds16 = ds.astype(BF16)
            dq = _unstack_heads(_dot(ds16, kc), lane_head) * scale
            _store_rows(dq_ref, _block_rows(row0, stride, ib), dq.astype(g_dtype), split)
            dk_acc[krows, :] += _dot_tn(ds16, q4) * scale
            dv_acc[krows, :] += _dot_tn(p.astype(BF16), dy4)
            if has_sink:
                for h in range(4):
                    hs = slice(h * BLOCK, (h + 1) * BLOCK)
                    sk = sink_ref[0, h:h + 1, 0:1]
                    val = -jnp.sum(jnp.exp(sk - l4[hs]) * c4[hs], axis=0, keepdims=True)
                    dsk_ref[hb + h] += jnp.broadcast_to(val, (8, 128))

        block(0, True)
        if nb > 1:
            def step(i, carry):
                block(i, False)
                return carry
            lax.fori_loop(1, nb, step, 0)

        if kvw == GW:
            all_rows = pl.ds(row0, ln, stride=stride) if stride > 1 else pl.ds(0, ln)
            _store_rows(dk_ref, all_rows, dk_acc[...].astype(g_dtype), split)
            _store_rows(dv_ref, all_rows, dv_acc[...].astype(g_dtype), split)
        else:
            def fold(acc):
                t2 = acc[:, :2 * HEAD_DIM] + acc[:, 2 * HEAD_DIM:]
                t2 = t2 + pltpu.roll(t2, HEAD_DIM, 1)
                lane = lax.broadcasted_iota(jnp.int32, t2.shape, 1) // HEAD_DIM
                return jnp.where(lane == rr, t2, 0.0)

            @pl.when(rr == 0)
            def _():
                dk_half[...] = fold(dk_acc[...])
                dv_half[...] = fold(dv_acc[...])

            @pl.when(rr == 1)
            def _():
                dk_ref[0] = (dk_half[...] + fold(dk_acc[...])).astype(g_dtype)
                dv_ref[0] = (dv_half[...] + fold(dv_acc[...])).astype(g_dtype)

    return pl.pallas_call(
        body, name=name, grid=grid,
        out_shape=(jax.ShapeDtypeStruct(dq_shape, g_dtype), jax.ShapeDtypeStruct(dkv_shape, g_dtype),
                   jax.ShapeDtypeStruct(dkv_shape, g_dtype), jax.ShapeDtypeStruct((n_bias, BLOCK, 2 * BLOCK), F32),
                   jax.ShapeDtypeStruct((8, 8, 128), F32)),
        in_specs=[q_spec, k_spec, v_spec,
                  pl.BlockSpec((4, BLOCK, 2 * BLOCK), bias_map), pl.BlockSpec((1, 4, 128), sink_map),
                  o_spec, o_spec, o_spec],
        out_specs=(o_spec, kv_out_spec, kv_out_spec,
                   pl.BlockSpec((n_bias, BLOCK, 2 * BLOCK), lambda n, r: (0, 0, 0)),
                   pl.BlockSpec((8, 8, 128), lambda n, r: (0, 0, 0))),
        scratch_shapes=[pltpu.VMEM((ln, GW), F32), pltpu.VMEM((ln, GW), F32),
                        pltpu.VMEM((ln, 2 * HEAD_DIM), F32), pltpu.VMEM((ln, 2 * HEAD_DIM), F32)]
        + ([pltpu.VMEM((seq, GW), BF16)] * 2 if expanded else []),
        compiler_params=_params(("arbitrary", "arbitrary"), VMEM_BIG),
    )(q_arr, k_arr, v_arr, bias, sink, dy, cc, lse)


def _bias_grad(ds_all, buckets):
    def body(ds_ref, bk_ref, o_ref):
        rows = lax.broadcasted_iota(jnp.int32, (N_BUCKETS, 128), 0)
        cols = lax.broadcasted_iota(jnp.int32, (N_BUCKETS, 128), 1)

        def per_bucket(b, acc):
            for h in range(20):
                gi = h // 4 if h < 12 else 3
                v = jnp.where(bk_ref[gi] == b, ds_ref[h], 0.0)
                v = jnp.sum(jnp.sum(v, axis=1, keepdims=True), axis=0, keepdims=True)
                acc = jnp.where((rows == b) & (cols == h), v, acc)
            return acc

        o_ref[...] = lax.fori_loop(0, N_BUCKETS, per_bucket, jnp.zeros((N_BUCKETS, 128), F32))

    vm = pl.BlockSpec(memory_space=pltpu.VMEM)
    return pl.pallas_call(body, name="bias_grad", out_shape=jax.ShapeDtypeStruct((N_BUCKETS, 128), F32),
                          in_specs=[vm, vm], out_specs=vm)(ds_all, buckets)


def _adamw(w, g, m, v, name):
    r, c = w.shape
    tr = r
    for cand in (256, 176, 128, 64, 32, 16, 8):
        if r % cand == 0:
            tr = cand
            break
    bc1 = 1.0 - ADAM_B1 ** ADAM_STEP
    bc2 = 1.0 - ADAM_B2 ** ADAM_STEP

    def body(w_ref, g_ref, m_ref, v_ref, d_ref, nm_ref, nv_ref):
        gv = g_ref[...]
        nm = ADAM_B1 * m_ref[...] + (1.0 - ADAM_B1) * gv
        nv = ADAM_B2 * v_ref[...] + (1.0 - ADAM_B2) * (gv * gv)
        nm_ref[...] = nm
        nv_ref[...] = nv
        d_ref[...] = -ADAM_LR * ((nm / bc1) / (jnp.sqrt(nv / bc2) + ADAM_EPS) + ADAM_WD * w_ref[...])

    spec = pl.BlockSpec((tr, c), lambda i: (i, 0))
    shp = jax.ShapeDtypeStruct((r, c), F32)
    return pl.pallas_call(body, name=name, grid=(r // tr,), out_shape=(shp, shp, shp),
                          in_specs=[spec] * 4, out_specs=(spec, spec, spec),
                          compiler_params=_params(("parallel",)))(w, g, m, v)


def _t5_bucket(dist):
    max_exact = N_BUCKETS // 2
    n = jnp.maximum(dist, 0)
    nf = jnp.maximum(n, 1).astype(F32)
    large = max_exact + (jnp.log(nf / max_exact) / math.log(MAX_DISTANCE / max_exact)
                         * (N_BUCKETS - max_exact)).astype(jnp.int32)
    large = jnp.minimum(large, N_BUCKETS - 1)
    return jnp.where(n < max_exact, n, large)


def _bias_tables(rel_bias):
    qi = jnp.arange(BLOCK)[:, None]
    ki = jnp.arange(2 * BLOCK)[None, :]
    dist = qi + BLOCK - ki
    specs = [(d, w // d, 4 * gi, 4 * gi + 4) for gi, (w, d) in enumerate(DIL_GROUPS)] + [(1, B_WINDOW - 1, 12, 20)]
    biases, buckets = [], []
    for stride, steps, h0, h1 in specs:
        valid = (dist >= 0) & (dist <= steps)
        bk = jnp.where(valid, _t5_bucket(dist * stride), -1).astype(jnp.int32)
        onehot = (bk[None, :, :] == jnp.arange(N_BUCKETS, dtype=jnp.int32)[:, None, None]).astype(F32)
        b = jnp.einsum("bqk,bh->hqk", onehot, rel_bias[:, h0:h1], precision=lax.Precision.HIGHEST)
        biases.append(jnp.where(valid[None], b, NEG))
        buckets.append(bk)
    return jnp.concatenate(biases, axis=0), jnp.stack(buckets, axis=0)


def _local_step(x, tgt, W, S, shards=None):
    nseq, seq, _ = x.shape
    t = nseq * seq
    xf = x.reshape(t, D_MODEL)
    bias_all, buckets = _bias_tables(S["rel_bias"])
    sink_b = jnp.broadcast_to(S["sinks"].reshape(2, 4, 1), (2, 4, 128)).astype(F32)
    sink_0 = jnp.zeros((1, 4, 128), F32)
    dist = shards is not None
    W = dict(W)
    G, GS, reduced = {}, {}, {}

    def put(keys, gathered):
        for k, g in zip(keys, gathered):
            W[k] = g.reshape(_FULL_SHAPE.get(k, (N_CHIPS * shards[k].shape[0], D_MODEL)))

    def gather_rider(keys):
        return _GatherRider([shards[k] for k in keys]) if dist else None

    def pair(keys):
        return _pair_reduce([G[k].reshape(N_CHIPS, 2, shards[k].shape[0] // 2, D_MODEL) for k in keys],
                            "grad_pair_reduce_" + keys[0])

    def finish(keys, own, rec):
        full = _final_reduce(own, rec, "grad_final_reduce_" + keys[0])
        off = 0
        for k in keys:
            r = shards[k].shape[0]
            reduced[k] = full[:, off:off + r // 2].reshape(r, D_MODEL)
            off += r // 2

    if dist:
        first = ("wgt1", "wut1", "wd1")
        put(first, _gather_rows([shards[k] for k in first]))
    keys = ("wint",)
    (h1, n1, g1, u1), ro = _ffn_fwd(xf, S["ffn1_norm"], W["wgt1"], W["wut1"], W["wd1"], rider=gather_rider(keys))
    put(keys, ro)
    keys = ("wout", "wat", "wbt", "wgt2")
    (un, za, zb, zg), ro = _inproj_fwd(h1, S["mix_norm"], W["wint"], S["b_in"], nseq, rider=gather_rider(keys))
    put(keys, ro)

    seq3 = lambda a: a.reshape(nseq, seq, a.shape[-1])
    zb3 = seq3(zb)
    pair_blk = lambda cb: pl.BlockSpec((1, 2, seq, 128), lambda n, r, cb=cb: (n, cb, 0, 0))
    a_cfg = []
    outs, lses = [], []
    for gi, (_, d) in enumerate(DIL_GROUPS):
        cfg = dict(grid=(nseq, d), seq=seq, stride=d, kvw=GW, split=True,
                   q_spec=pair_blk(gi), k_spec=pair_blk(3 + gi), v_spec=pair_blk(6 + gi), o_spec=pair_blk(0),
                   bias_map=lambda n, r: (0, 0, 0), sink_map=lambda n, r: (0, 0, 0), has_sink=False)
        a_cfg.append(cfg)
        (o, lse), _ = _attn_fwd(za, za, za, bias_all[4 * gi:4 * gi + 4], sink_0, o_shape=(nseq, 2, seq, 128),
                                o_dtype=F32, name=f"attn_a{gi}_fwd", **cfg)
        outs.append(o)
        lses.append(lse)
    wide_blk = lambda w, cmap: pl.BlockSpec((1, seq, w), cmap)
    b_cfg = dict(grid=(nseq, 2), seq=seq, stride=1, kvw=2 * HEAD_DIM, split=False,
                 q_spec=wide_blk(GW, lambda n, r: (n, 0, r)), k_spec=wide_blk(2 * HEAD_DIM, lambda n, r: (n, 0, 4)),
                 v_spec=wide_blk(2 * HEAD_DIM, lambda n, r: (n, 0, 5)), o_spec=wide_blk(GW, lambda n, r: (n, 0, r)),
                 bias_map=lambda n, r: (r, 0, 0), sink_map=lambda n, r: (r, 0, 0), has_sink=True)
    keys = ("wut2",)
    bias_b_fwd = bias_all[12:20].at[:, :, 0].set(jnp.broadcast_to(S["sinks"].reshape(8, 1), (8, BLOCK)))
    (yb, lse_b), ro = _attn_fwd(zb3, zb3, zb3, bias_b_fwd, sink_b, o_shape=(nseq, seq, 2 * GW), o_dtype=BF16,
                                name="attn_b_fwd", rider=gather_rider(keys), **b_cfg)
    put(keys, ro)
    yb = yb.reshape(t, 2 * GW)

    keys = ("wd2",)
    (h2, y, lse_tot, pa, pb, merged), ro = _merge_fwd(outs[0], outs[1], outs[2], lses[0], lses[1], lses[2], yb, zg, h1,
                                                      W["wat"], W["wbt"], W["wout"], rider=gather_rider(keys))
    put(keys, ro)
    (h3, n2, g2, u2), _ = _ffn_fwd(h2, S["ffn2_norm"], W["wgt2"], W["wut2"], W["wd2"])
    dh3, loss_part, g_final = _loss_head(h3, S["final_norm"].reshape(1, D_MODEL), tgt.reshape(t, D_MODEL))

    GS["final_norm"] = g_final
    dh2, dg2, du2, a2, df2, GS["ffn2_norm"] = _ffn_bwd(dh3, h2, S["ffn2_norm"], g2, u2, W["wgt2"], W["wut2"], W["wd2"])
    G["wgt2"] = _wgrad(dg2, n2, MXU_DIM, name="wgrad_gate2")
    G["wut2"] = _wgrad(du2, n2, MXU_DIM, name="wgrad_up2")
    G["wd2"] = _wgrad(a2, df2, MXU_DIM, name="wgrad_down2")

    keys = ("wgt2", "wut2", "wd2")
    rider = _ExchangeRider([pair(keys)]) if dist else None
    (dpa, dpb, dga, dgb, dya, dyb, dh2b, ca, cb), ro = _merge_bwd(dh2, pa, pb, zg, y, yb, W["wat"], W["wbt"], W["wout"],
                                                                  nseq, rider=rider)
    if dist:
        finish(keys, *ro)
    G["wout"] = _wgrad(merged, dh2b, MXU_DIM, name="wgrad_out")
    G["wat"] = _wgrad(dpa, y, MXU_DIM, name="wgrad_branch_a")
    G["wbt"] = _wgrad(dpb, yb, MXU_DIM, name="wgrad_branch_b")

    dqs, dks, dvs, dbs = [], [], [], []
    shp = (nseq, 2, seq, 128)
    halves = lambda a: [a[:, hf].reshape(t, 128).astype(BF16) for hf in range(2)]
    for gi in range(len(DIL_GROUPS)):
        dq, dk, dv, db, _ = _attn_bwd(za, za, za, bias_all[4 * gi:4 * gi + 4], sink_0, dya, ca, lse_tot,
                                      n_bias=4, dq_shape=shp, dkv_shape=shp, g_dtype=F32,
                                      kv_out_spec=a_cfg[gi]["o_spec"], name=f"attn_a{gi}_bwd", **a_cfg[gi])
        dqs += halves(dq)
        dks += halves(dk)
        dvs += halves(dv)
        dbs.append(db)
    dqb, dkb, dvb, dbb, dsink = _attn_bwd(zb3, zb3, zb3, bias_all[12:20], sink_b, seq3(dyb), seq3(cb), lse_b,
                                          n_bias=8, dq_shape=(nseq, seq, 2 * GW),
                                          dkv_shape=(nseq, seq, 2 * HEAD_DIM), g_dtype=BF16,
                                          kv_out_spec=wide_blk(2 * HEAD_DIM, lambda n, r: (n, 0, 0)),
                                          name="attn_b_bwd", **b_cfg)
    dz = jnp.concatenate(dqs + dks + dvs + [dqb.reshape(t, 2 * GW), dkb.reshape(t, 2 * HEAD_DIM),
                                            dvb.reshape(t, 2 * HEAD_DIM), dga, dgb], axis=-1)
    gb_tab = _bias_grad(jnp.concatenate(dbs + [dbb], axis=0), buckets)
    if dist:
        GS["bias_tab"], GS["sink_tiles"] = gb_tab, dsink
    else:
        GS["rel_bias"] = gb_tab[:, :20]
        GS["sinks"] = dsink[:, 0, 0].reshape(1, 8)

    G["wint"], GS["b_in"] = _wgrad(dz, un, MXU_DIM, with_colsum=True, name="wgrad_in")
    keys = ("wint", "wout", "wat", "wbt")
    rider = _ExchangeRider([pair(keys)]) if dist else None
    (dh1, GS["mix_norm"]), ro = _inproj_bwd(dz, dh2, h1, S["mix_norm"], W["wint"], rider=rider)
    if dist:
        finish(keys, *ro)

    dx, dg1, du1, a1, df1, GS["ffn1_norm"] = _ffn_bwd(dh1, xf, S["ffn1_norm"], g1, u1, W["wgt1"], W["wut1"], W["wd1"])
    G["wgt1"] = _wgrad(dg1, n1, MXU_DIM, name="wgrad_gate1")
    if dist:
        G["wut1"], ro = _wgrad(du1, n1, MXU_DIM, name="wgrad_up1", rider=_ExchangeRider([pair(("wgt1",))]))
        finish(("wgt1",), *ro)
        G["wd1"], ro = _wgrad(a1, df1, MXU_DIM, name="wgrad_down1", rider=_ExchangeRider([pair(("wut1",))]))
        finish(("wut1",), *ro)
        finish(("wd1",), *_chip_exchange([pair(("wd1",))]))
    else:
        G["wut1"] = _wgrad(du1, n1, MXU_DIM, name="wgrad_up1")
        G["wd1"] = _wgrad(a1, df1, MXU_DIM, name="wgrad_down1")
    return loss_part, dx.reshape(x.shape), (reduced if dist else G), GS


_SMALL = ("ffn1_norm", "mix_norm", "ffn2_norm", "final_norm", "b_in", "sinks", "rel_bias")
_ORDER = ("ffn1_norm", "ffn1_w_gate", "ffn1_w_up", "ffn1_w_down", "mix_norm", "w_in", "b_in", "w_branch_a",
          "w_branch_b", "w_out", "sinks", "rel_bias", "ffn2_norm", "ffn2_w_gate", "ffn2_w_up", "ffn2_w_down",
          "final_norm")
_BIG = (("wgt1", "ffn1_w_gate", True, 704), ("wut1", "ffn1_w_up", True, 704), ("wd1", "ffn1_w_down", False, 704),
        ("wint", "w_in", True, 1280), ("wout", "w_out", False, 256), ("wat", "w_branch_a", True, 64),
        ("wbt", "w_branch_b", True, 128), ("wgt2", "ffn2_w_gate", True, 704), ("wut2", "ffn2_w_up", True, 704),
        ("wd2", "ffn2_w_down", False, 704))
_FULL_SHAPE = {"wat": (D_MODEL, GW), "wbt": (D_MODEL, 2 * GW)}


def kernel(x, ffn1_norm, ffn1_w_gate, ffn1_w_up, ffn1_w_down, mix_norm, w_in, b_in, w_branch_a, w_branch_b, w_out, sinks, rel_bias, ffn2_norm, ffn2_w_gate, ffn2_w_up, ffn2_w_down, final_norm, loss_target, m_ffn1_norm, m_ffn1_w_gate, m_ffn1_w_up, m_ffn1_w_down, m_mix_norm, m_w_in, m_b_in, m_w_branch_a, m_w_branch_b, m_w_out, m_sinks, m_rel_bias, m_ffn2_norm, m_ffn2_w_gate, m_ffn2_w_up, m_ffn2_w_down, m_final_norm, v_ffn1_norm, v_ffn1_w_gate, v_ffn1_w_up, v_ffn1_w_down, v_mix_norm, v_w_in, v_b_in, v_w_branch_a, v_w_branch_b, v_w_out, v_sinks, v_rel_bias, v_ffn2_norm, v_ffn2_w_gate, v_ffn2_w_up, v_ffn2_w_down, v_final_norm):
    args = dict(locals())
    w = {n: args[n] for n in _ORDER}
    m = {n: args["m_" + n] for n in _ORDER}
    v = {n: args["v_" + n] for n in _ORDER}

    shards = {}
    for key, name, transposed, rows in _BIG:
        a = w[name][0]
        a = (a.T if transposed else a).astype(BF16)
        shards[key] = a.reshape(rows, D_MODEL)
    S = {n: w[n] for n in _SMALL}

    loss_part, grad_x, reduced, GS = _local_step(x, loss_target, {}, S, shards)

    small = _allreduce_small(GS["ffn1_norm"], GS["mix_norm"], GS["ffn2_norm"], GS["final_norm"], GS["b_in"],
                             GS["sink_tiles"], GS["bias_tab"], loss_part)
    loss = small[9, 8]

    out_g, out_d, out_m, out_v = {}, {}, {}, {}
    for key, n, transposed, rows in _BIG:
        nat = w[n][0].shape
        if transposed and nat[1] % 128:
            res = _adamw(w[n][0].T, reduced[key], m[n][0].T, v[n][0].T, "adamw_" + n)
            res = [reduced[key].T] + [r.T for r in res]
        else:
            g = reduced[key].reshape(nat[1], nat[0]).T if transposed else reduced[key].reshape(nat)
            res = [g] + list(_adamw(w[n][0], g, m[n][0], v[n][0], "adamw_" + n))
        out_g[n], out_d[n], out_m[n], out_v[n] = [r[None] for r in res]
    row = lambda d: {n: (d[n].reshape(1, D_MODEL) if n == "final_norm" else d[n]) for n in _SMALL}
    for dst, src in zip((out_g, out_d, out_m, out_v), _adamw_small(small, row(w), row(m), row(v))):
        dst.update(src)
        dst["final_norm"] = src["final_norm"].reshape(D_MODEL)

    return (loss, grad_x, *[out_g[n] for n in _ORDER], *[out_d[n] for n in _ORDER],
            *[out_m[n] for n in _ORDER], *[out_v[n] for n in _ORDER])
```

```python
import math

import jax
import jax.numpy as jnp
from jax import lax
from jax.experimental import pallas as pl
from jax.experimental.pallas import tpu as pltpu

F32, BF16 = jnp.float32, jnp.bfloat16
MESH = pl.DeviceIdType.MESH

D_MODEL = 1024
D_FF = 2816
D_IN = 5120
HEAD_DIM = 64
BLOCK = 128
DIL_GROUPS = ((128, 1), (512, 4), (2048, 16))
B_WINDOW = 128
N_BUCKETS = 32
MAX_DISTANCE = 2048
EPS = 1e-6
N_CHIPS = 4
GW = 256
ZA_W = 2304
ZB_W = 768
NEG = -1e30

ADAM_LR, ADAM_B1, ADAM_B2, ADAM_EPS, ADAM_WD, ADAM_STEP = 0.001, 0.9, 0.999, 1e-08, 0.01, 10

VMEM_BIG = 56 * 1024 * 1024
TM = 512
TM_BWD = 256
MXU_DIM = 256
FF_BOUNDS = (0, 6 * MXU_DIM, D_FF)
DMA_SPLIT = 8
RESIDUES_PER_STEP = 4


def _dot(a, b):
    return jnp.dot(a, b, preferred_element_type=F32)


def _dot_nt(a, b):
    return lax.dot_general(a, b, (((1,), (1,)), ((), ())), preferred_element_type=F32)


def _dot_tn(a, b):
    return lax.dot_general(a, b, (((0,), (0,)), ((), ())), preferred_element_type=F32)


def _sigmoid(x):
    return 0.5 * jnp.tanh(0.5 * x) + 0.5


def _params(sem, vmem=None):
    return pltpu.CompilerParams(dimension_semantics=sem, vmem_limit_bytes=vmem)


ANY = pl.BlockSpec(memory_space=pl.ANY)


def _me():
    return lax.axis_index("x"), lax.axis_index("y"), lax.axis_index("c")


_CHIP_RELS = ((1, 0), (0, 1), (1, 1))


def _flip(v, f):
    return 1 - v if f else v


def _remote(src, dst, ssem, rsem, peer):
    return pltpu.make_async_remote_copy(src_ref=src, dst_ref=dst, send_sem=ssem, recv_sem=rsem,
                                        device_id=peer, device_id_type=MESH)


def _row_pieces(rows, n):
    step = max(16, -(-rows // n) // 16 * 16)
    out, s = [], 0
    while s < rows:
        out.append((s, min(step, rows - s)))
        s += step
    return out


def _gather_rows(shards):
    nt = len(shards)
    rows = [s.shape[0] for s in shards]

    def body(*refs):
        srcs, outs = refs[:nt], refs[nt:2 * nt]
        halves, quarters = refs[2 * nt:3 * nt], refs[3 * nt:4 * nt]
        ici_s, ici_r, fwd_s, fwd_r, d2d_s, d2d_r, keep, loc = refs[4 * nt:]
        x, y, c = _me()
        j = 2 * x + y
        sib = (x, y, 1 - c)
        nbr = ((1 - x, y, c), (x, 1 - y, c))
        nbr_j = (2 * (1 - x) + y, 2 * x + (1 - y))
        diag_j = 2 * (1 - x) + (1 - y)
        local = [pltpu.make_async_copy(srcs[t], outs[t].at[j], loc.at[t]) for t in range(nt)]
        for cp in local:
            cp.start()
        pending = []
        for a in range(2):
            for t in range(nt):
                half = pl.ds(c * (rows[t] // 2), rows[t] // 2)
                cp = _remote(srcs[t].at[half], halves[t].at[a], ici_s.at[2 * t + a], ici_r.at[2 * t + a], nbr[a])
                cp.start()
                pending.append(cp)
        placed = []

        def place(src, dst_of, idx):
            mine = pltpu.make_async_copy(src, dst_of, keep.at[idx])
            mine.start()
            cp = _remote(src, dst_of, d2d_s.at[idx], d2d_r.at[idx], sib)
            cp.start()
            placed.append((mine, cp))

        for a in range(2):
            for t in range(nt):
                r2, r4 = rows[t] // 2, rows[t] // 4
                got = halves[t].at[a]
                _remote(got, got, ici_s.at[2 * t + a], ici_r.at[2 * t + a], nbr[a]).wait_recv()
                cp = _remote(halves[t].at[a, pl.ds(a * r4, r4)], quarters[t].at[a], fwd_s.at[2 * t + a],
                             fwd_r.at[2 * t + a], nbr[1 - a])
                cp.start()
                pending.append(cp)
                place(got, outs[t].at[nbr_j[a], pl.ds(c * r2, r2)], 4 * t + a)
        for a in range(2):
            for t in range(nt):
                r2, r4 = rows[t] // 2, rows[t] // 4
                got = quarters[t].at[a]
                _remote(got, got, fwd_s.at[2 * t + a], fwd_r.at[2 * t + a], nbr[1 - a]).wait_recv()
                place(got, outs[t].at[diag_j, pl.ds(c * r2 + a * r4, r4)], 4 * t + 2 + a)
        for mine, cp in placed:
            mine.wait()
            cp.wait()
        for cp in pending:
            cp.wait_send()
        for cp in local:
            cp.wait()

    stage = ([pltpu.VMEM((2, r // 2, D_MODEL), BF16) for r in rows] + [pltpu.VMEM((2, r // 4, D_MODEL), BF16) for r in rows])
    sems = ([pltpu.SemaphoreType.DMA((2 * nt,)) for _ in range(4)] + [pltpu.SemaphoreType.DMA((4 * nt,))] * 3
            + [pltpu.SemaphoreType.DMA((nt,))])
    return pl.pallas_call(
        body, name="gather_weights",
        out_shape=tuple(jax.ShapeDtypeStruct((N_CHIPS,) + s.shape, s.dtype) for s in shards),
        in_specs=[pl.BlockSpec(memory_space=pltpu.VMEM)] * nt,
        out_specs=tuple([ANY] * nt), scratch_shapes=stage + sems,
    )(*shards)


VMEM_WHOLE = pl.BlockSpec(memory_space=pltpu.VMEM)


def _pair_reduce(grads, name):
    nt = len(grads)
    r2 = [g.shape[2] for g in grads]
    off = [sum(r2[:t]) for t in range(nt)]
    tot = sum(r2)

    def body(*refs):
        gs = refs[:nt]
        s_ref, got, ssem, rsem = refs[nt:]
        x, y, c = _me()
        sib = (x, y, 1 - c)
        for t in range(nt):
            for k in range(N_CHIPS):
                _remote(gs[t].at[k, 1 - c], got.at[k, pl.ds(off[t], r2[t])], ssem, rsem, sib).start()
        _remote(got, got, ssem, rsem, sib).wait()
        for t in range(nt):
            for k in range(N_CHIPS):
                rows = slice(off[t], off[t] + r2[t])
                s_ref[k, rows, :] = (gs[t][k, c].astype(F32) + got[k, rows, :].astype(F32)).astype(BF16)

    shp = jax.ShapeDtypeStruct((N_CHIPS, tot, D_MODEL), BF16)
    return pl.pallas_call(
        body, name=name, out_shape=shp, in_specs=[VMEM_WHOLE] * nt, out_specs=VMEM_WHOLE,
        scratch_shapes=[pltpu.VMEM((N_CHIPS, tot, D_MODEL), BF16), pltpu.SemaphoreType.DMA(()),
                        pltpu.SemaphoreType.DMA(())],
        compiler_params=pltpu.CompilerParams(vmem_limit_bytes=VMEM_BIG),
    )(*grads)


def _chip_exchange(parts):
    ng = len(parts)
    r2 = [p.shape[1] for p in parts]
    off = [sum(r2[:g]) for g in range(ng)]
    tot = sum(r2)

    def body(*refs):
        ps = refs[:ng]
        own_ref, rec_ref, ssems, rsems, lsem = refs[ng:]
        x, y, c = _me()
        j = 2 * x + y
        for g in range(ng):
            pltpu.make_async_copy(ps[g].at[j], own_ref.at[pl.ds(off[g], r2[g])], lsem).start()
        for k, (fx, fy) in enumerate(_CHIP_RELS):
            px, py = _flip(x, fx), _flip(y, fy)
            for g in range(ng):
                for st, sz in _row_pieces(r2[g], 2):
                    _remote(ps[g].at[2 * px + py, pl.ds(st, sz)], rec_ref.at[k, pl.ds(off[g] + st, sz)],
                            ssems.at[k], rsems.at[k], (px, py, c)).start()
        for k in range(3):
            _remote(rec_ref.at[k], rec_ref.at[k], ssems.at[k], rsems.at[k], (x, y, c)).wait()
        pltpu.make_async_copy(own_ref, own_ref, lsem).wait()

    return pl.pallas_call(
        body, name="grad_chip_exchange",
        out_shape=(jax.ShapeDtypeStruct((tot, D_MODEL), BF16), jax.ShapeDtypeStruct((3, tot, D_MODEL), BF16)),
        in_specs=[VMEM_WHOLE] * ng, out_specs=(ANY, ANY),
        scratch_shapes=[pltpu.SemaphoreType.DMA((3,)), pltpu.SemaphoreType.DMA((3,)), pltpu.SemaphoreType.DMA(())],
    )(*parts)


def _final_reduce(own, rec, name):
    r2 = own.shape[0]
    pieces = _row_pieces(r2, DMA_SPLIT)

    def body(own_ref, rec_ref, o_ref, fbuf, ssem, rsem, lsem):
        x, y, c = _me()
        sib = (x, y, 1 - c)
        for st, sz in pieces:
            rows = slice(st, st + sz)
            fbuf[rows, :] = (own_ref[rows, :].astype(F32) + rec_ref[0, rows, :].astype(F32)
                             + rec_ref[1, rows, :].astype(F32) + rec_ref[2, rows, :].astype(F32))
            pltpu.make_async_copy(fbuf.at[pl.ds(st, sz)], o_ref.at[c, pl.ds(st, sz)], lsem).start()
            _remote(fbuf.at[pl.ds(st, sz)], o_ref.at[c, pl.ds(st, sz)], ssem, rsem, sib).start()
        _remote(fbuf, o_ref.at[c], ssem, rsem, sib).wait()
        pltpu.make_async_copy(fbuf, o_ref.at[c], lsem).wait()

    return pl.pallas_call(
        body, name=name, out_shape=jax.ShapeDtypeStruct((2, r2, D_MODEL), F32),
        in_specs=[VMEM_WHOLE, VMEM_WHOLE], out_specs=ANY,
        scratch_shapes=[pltpu.VMEM((r2, D_MODEL), F32), pltpu.SemaphoreType.DMA(()), pltpu.SemaphoreType.DMA(()),
                        pltpu.SemaphoreType.DMA(())],
        compiler_params=pltpu.CompilerParams(vmem_limit_bytes=VMEM_BIG),
    )(own, rec)


SMALL_ROWS = 48


def _allreduce_small(g_ffn1, g_mix, g_ffn2, g_final, g_bin, dsink, bias_tab, loss_part):
    def body(f1_ref, mx_ref, f2_ref, fn_ref, bi_ref, sk_ref, bt_ref, ls_ref, o_ref, mine, buf, send_sems, recv_sems):
        x, y, c = _me()
        me = 4 * x + 2 * y + c
        mine[...] = jnp.zeros_like(mine)
        for r, ref in enumerate((f1_ref, mx_ref, f2_ref, fn_ref)):
            mine[r:r + 1, :] = ref[...]
        for k in range(D_IN // D_MODEL):
            mine[4 + k:5 + k, :] = bi_ref[:, k * D_MODEL:(k + 1) * D_MODEL]
        lane = lax.broadcasted_iota(jnp.int32, (1, 128), 1)
        row = jnp.where(lane == 8, ls_ref[0:1, :], 0.0)
        for h in range(8):
            row = jnp.where(lane == h, sk_ref[h, 0:1, :], row)
        mine[9:10, 0:128] = row
        mine[16:48, 0:128] = bt_ref[...]
        buf[me] = mine[...]
        copies = []
        for k in range(1, 8):
            peer = (_flip(x, (k >> 2) & 1), _flip(y, (k >> 1) & 1), _flip(c, k & 1))
            cp = _remote(mine, buf.at[me], send_sems.at[k - 1], recv_sems.at[k - 1], peer)
            cp.start()
            copies.append(cp)
        for cp in copies:
            cp.wait()
        acc = buf[0]
        for i in range(1, 8):
            acc = acc + buf[i]
        o_ref[...] = acc

    vm = pl.BlockSpec(memory_space=pltpu.VMEM)
    shape = (SMALL_ROWS, D_MODEL)
    return pl.pallas_call(
        body, name="allreduce_small", out_shape=jax.ShapeDtypeStruct(shape, F32),
        in_specs=[vm] * 8, out_specs=vm,
        scratch_shapes=[pltpu.VMEM(shape, F32), pltpu.VMEM((8,) + shape, F32), pltpu.SemaphoreType.DMA((7,)),
                        pltpu.SemaphoreType.DMA((7,))],
    )(g_ffn1, g_mix, g_ffn2, g_final, g_bin, dsink, bias_tab, loss_part)


def _adam_update(w, g, m, v):
    nm = ADAM_B1 * m + (1.0 - ADAM_B1) * g
    nv = ADAM_B2 * v + (1.0 - ADAM_B2) * (g * g)
    bc1 = 1.0 - ADAM_B1 ** ADAM_STEP
    bc2 = 1.0 - ADAM_B2 ** ADAM_STEP
    return -ADAM_LR * ((nm / bc1) / (jnp.sqrt(nv / bc2) + ADAM_EPS) + ADAM_WD * w), nm, nv


def _adamw_small(packed, w, m, v):
    names = ("ffn1_norm", "mix_norm", "ffn2_norm", "final_norm", "b_in", "sinks", "rel_bias")
    nn = len(names)

    def grad_of(p_ref, name, k=0):
        if name == "b_in":
            return p_ref[4 + k:5 + k, :]
        if name == "sinks":
            return p_ref[9:10, 0:8]
        if name == "rel_bias":
            return p_ref[16:48, 0:20]
        r = names.index(name)
        return p_ref[r:r + 1, :]

    def body(p_ref, *refs):
        ws, ms, vs = refs[:nn], refs[nn:2 * nn], refs[2 * nn:3 * nn]
        outs = refs[3 * nn:]
        for i, name in enumerate(names):
            og, od, om, ov = outs[i], outs[nn + i], outs[2 * nn + i], outs[3 * nn + i]
            pieces = range(D_IN // D_MODEL) if name == "b_in" else (0,)
            for k in pieces:
                sl = (slice(None), slice(k * D_MODEL, (k + 1) * D_MODEL)) if name == "b_in" else (Ellipsis,)
                g = grad_of(p_ref, name, k)
                d, nm, nv = _adam_update(ws[i][sl], g, ms[i][sl], vs[i][sl])
                og[sl], od[sl], om[sl], ov[sl] = g, d, nm, nv

    vm = pl.BlockSpec(memory_space=pltpu.VMEM)
    shapes = [jax.ShapeDtypeStruct(w[n].shape, F32) for n in names]
    res = pl.pallas_call(
        body, name="adamw_small", out_shape=tuple(shapes * 4), in_specs=[vm] * (1 + 3 * nn),
        out_specs=tuple([vm] * (4 * nn)),
    )(packed, *[w[n] for n in names], *[m[n] for n in names], *[v[n] for n in names])
    return [dict(zip(names, res[i * nn:(i + 1) * nn])) for i in range(4)]


class _GatherRider:
    def __init__(self, shards):
        self.inputs = list(shards)
        nt = len(shards)
        self.out_shape = [jax.ShapeDtypeStruct((N_CHIPS,) + s.shape, s.dtype) for s in shards]
        self.scratch = [pltpu.SemaphoreType.DMA((3 * nt,)), pltpu.SemaphoreType.DMA((3 * nt,)),
                        pltpu.SemaphoreType.DMA((nt,))]

    def _copies(self, srcs, outs, sems):
        ici_s, ici_r, loc = sems
        x, y, c = _me()
        j = 2 * x + y
        local = [pltpu.make_async_copy(srcs[t], outs[t].at[j], loc.at[t]) for t in range(len(srcs))]
        remote = []
        for k, (fx, fy) in enumerate(_CHIP_RELS):
            peer = (_flip(x, fx), _flip(y, fy), c)
            for t in range(len(srcs)):
                remote.append(_remote(srcs[t], outs[t].at[j], ici_s.at[3 * t + k], ici_r.at[3 * t + k], peer))
        return local, remote

    def start(self, srcs, outs, sems):
        local, remote = self._copies(srcs, outs, sems)
        for cp in local + remote:
            cp.start()

    def finish(self, srcs, outs, sems):
        local, remote = self._copies(srcs, outs, sems)
        for cp in remote + local:
            cp.wait()


class _ExchangeRider:
    def __init__(self, parts):
        self.inputs = list(parts)
        self.r2 = [p.shape[1] for p in parts]
        self.off = [sum(self.r2[:g]) for g in range(len(parts))]
        tot = sum(self.r2)
        self.out_shape = [jax.ShapeDtypeStruct((tot, D_MODEL), BF16), jax.ShapeDtypeStruct((3, tot, D_MODEL), BF16)]
        self.scratch = [pltpu.SemaphoreType.DMA((3,)), pltpu.SemaphoreType.DMA((3,)), pltpu.SemaphoreType.DMA(())]

    def start(self, ps, outs, sems):
        own_ref, rec_ref = outs
        ssems, rsems, lsem = sems
        x, y, c = _me()
        j = 2 * x + y
        for g in range(len(ps)):
            pltpu.make_async_copy(ps[g].at[j], own_ref.at[pl.ds(self.off[g], self.r2[g])], lsem).start()
        for k, (fx, fy) in enumerate(_CHIP_RELS):
            px, py = _flip(x, fx), _flip(y, fy)
            for g in range(len(ps)):
                for st, sz in _row_pieces(self.r2[g], 2):
                    _remote(ps[g].at[2 * px + py, pl.ds(st, sz)], rec_ref.at[k, pl.ds(self.off[g] + st, sz)],
                            ssems.at[k], rsems.at[k], (px, py, c)).start()

    def finish(self, ps, outs, sems):
        own_ref, rec_ref = outs
        ssems, rsems, lsem = sems
        x, y, c = _me()
        for k in range(3):
            _remote(rec_ref.at[k], rec_ref.at[k], ssems.at[k], rsems.at[k], (x, y, c)).wait()
        pltpu.make_async_copy(own_ref, own_ref, lsem).wait()


def _pallas(body, args, *, name, grid, in_specs, out_specs, out_shape, scratch_shapes=(), sem=None, vmem=None,
            rider=None):
    if rider is None:
        res = pl.pallas_call(body, name=name, grid=grid, in_specs=list(in_specs), out_specs=tuple(out_specs),
                             out_shape=tuple(out_shape), scratch_shapes=list(scratch_shapes),
                             compiler_params=_params(sem, vmem))(*args)
        return tuple(res), ()
    n_in, n_out, n_sc = len(in_specs), len(out_shape), len(scratch_shapes)
    r_in, r_out = len(rider.inputs), len(rider.out_shape)

    def wrapped(*refs):
        ins, rins = refs[:n_in], refs[n_in:n_in + r_in]
        p = n_in + r_in
        outs, routs = refs[p:p + n_out], refs[p + n_out:p + n_out + r_out]
        p += n_out + r_out
        scr, rsems = refs[p:p + n_sc], refs[p + n_sc:]
        first = pl.program_id(0) == 0
        last = pl.program_id(0) == grid[0] - 1
        for a in range(1, len(grid)):
            first = first & (pl.program_id(a) == 0)
            last = last & (pl.program_id(a) == grid[a] - 1)

        @pl.when(first)
        def _():
            rider.start(rins, routs, rsems)

        body(*ins, *outs, *scr)

        @pl.when(last)
        def _():
            rider.finish(rins, routs, rsems)

    res = pl.pallas_call(
        wrapped, name=name, grid=grid, in_specs=list(in_specs) + [ANY] * r_in,
        out_specs=tuple(out_specs) + (ANY,) * r_out, out_shape=tuple(out_shape) + tuple(rider.out_shape),
        scratch_shapes=list(scratch_shapes) + rider.scratch,
        compiler_params=_params(("arbitrary",) * len(grid), vmem))(*args, *rider.inputs)
    return tuple(res[:n_out]), tuple(res[n_out:])


def _ffn_fwd(h, gain, wgt, wut, wd, rider=None):
    t = h.shape[0]

    def body(h_ref, gain_ref, wg_hbm, wu_hbm, wd_hbm, hout_ref, n_ref, g_ref, u_ref, wg_v, wu_v, wd_v):
        @pl.when(pl.program_id(0) == 0)
        def _():
            pltpu.sync_copy(wg_hbm, wg_v)
            pltpu.sync_copy(wu_hbm, wu_v)
            pltpu.sync_copy(wd_hbm, wd_v)

        hh = h_ref[...]
        r = lax.rsqrt(jnp.mean(hh * hh, axis=-1, keepdims=True) + EPS)
        n = (hh * r * gain_ref[...]).astype(BF16)
        n_ref[...] = n
        acc = jnp.zeros((TM, D_MODEL), F32)
        for c0, c1 in zip(FF_BOUNDS[:-1], FF_BOUNDS[1:]):
            sl = slice(c0, c1)
            g = _dot_nt(n, wg_v[sl, :])
            u = _dot_nt(n, wu_v[sl, :])
            g_ref[:, sl] = g.astype(BF16)
            u_ref[:, sl] = u.astype(BF16)
            a = (g * _sigmoid(g) * u).astype(BF16)
            acc = acc + _dot(a, wd_v[sl, :])
        hout_ref[...] = hh + 0.5 * acc

    row = lambda w: pl.BlockSpec((TM, w), lambda i: (i, 0))
    wv = pltpu.VMEM((D_FF, D_MODEL), BF16)
    return _pallas(
        body, (h, gain, wgt, wut, wd), name="ffn_fwd", grid=(t // TM,),
        out_shape=(jax.ShapeDtypeStruct((t, D_MODEL), F32), jax.ShapeDtypeStruct((t, D_MODEL), BF16),
                   jax.ShapeDtypeStruct((t, D_FF), BF16), jax.ShapeDtypeStruct((t, D_FF), BF16)),
        in_specs=[row(D_MODEL), pl.BlockSpec((1, D_MODEL), lambda i: (0, 0)), ANY, ANY, ANY],
        out_specs=(row(D_MODEL), row(D_MODEL), row(D_FF), row(D_FF)),
        scratch_shapes=[wv, wv, wv], sem=("arbitrary",), vmem=VMEM_BIG, rider=rider)


def _ffn_bwd(dhout, h, gain, g, u, wgt, wut, wd):
    t = h.shape[0]
    tm = TM_BWD

    def body(dho_ref, h_ref, gain_ref, g_ref, u_ref, wg_hbm, wu_hbm, wd_hbm,
             dh_ref, dg_ref, du_ref, a_ref, df_ref, gg_ref, wg_v, wu_v, wd_v):
        @pl.when(pl.program_id(0) == 0)
        def _():
            pltpu.sync_copy(wg_hbm, wg_v)
            pltpu.sync_copy(wu_hbm, wu_v)
            pltpu.sync_copy(wd_hbm, wd_v)
            gg_ref[...] = jnp.zeros_like(gg_ref)

        dho = dho_ref[...]
        df = (0.5 * dho).astype(BF16)
        df_ref[...] = df
        dn = jnp.zeros((tm, D_MODEL), F32)
        for c0, c1 in zip(FF_BOUNDS[:-1], FF_BOUNDS[1:]):
            sl = slice(c0, c1)
            da = _dot_nt(df, wd_v[sl, :])
            gv = g_ref[:, sl].astype(F32)
            uv = u_ref[:, sl].astype(F32)
            sg = _sigmoid(gv)
            silu = gv * sg
            dg = (da * uv * (sg * (1.0 + gv * (1.0 - sg)))).astype(BF16)
            du = (da * silu).astype(BF16)
            dg_ref[:, sl] = dg
            du_ref[:, sl] = du
            a_ref[:, sl] = (silu * uv).astype(BF16)
            dn = dn + _dot(dg, wg_v[sl, :]) + _dot(du, wu_v[sl, :])
        hh = h_ref[...]
        r = lax.rsqrt(jnp.mean(hh * hh, axis=-1, keepdims=True) + EPS)
        hn = hh * r
        gg_ref[...] += jnp.sum(dn * hn, axis=0, keepdims=True)
        dng = dn * gain_ref[...]
        dh_ref[...] = dho + r * (dng - hn * jnp.mean(dng * hn, axis=-1, keepdims=True))

    row = lambda w: pl.BlockSpec((tm, w), lambda i: (i, 0))
    vec = pl.BlockSpec((1, D_MODEL), lambda i: (0, 0))
    wv = pltpu.VMEM((D_FF, D_MODEL), BF16)
    return pl.pallas_call(
        body, name="ffn_bwd", grid=(t // tm,),
        out_shape=(jax.ShapeDtypeStruct((t, D_MODEL), F32), jax.ShapeDtypeStruct((t, D_FF), BF16),
                   jax.ShapeDtypeStruct((t, D_FF), BF16), jax.ShapeDtypeStruct((t, D_FF), BF16),
                   jax.ShapeDtypeStruct((t, D_MODEL), BF16), jax.ShapeDtypeStruct((1, D_MODEL), F32)),
        in_specs=[row(D_MODEL), row(D_MODEL), vec, row(D_FF), row(D_FF), ANY, ANY, ANY],
        out_specs=(row(D_MODEL), row(D_FF), row(D_FF), row(D_FF), row(D_MODEL), vec),
        scratch_shapes=[wv, wv, wv],
        compiler_params=_params(("arbitrary",), VMEM_BIG),
    )(dhout, h, gain, g, u, wgt, wut, wd)


def _wgrad(lhs, rhs, rb, with_colsum=False, name="wgrad", rider=None):
    t, k = lhs.shape
    n = rhs.shape[1]

    def body(l_ref, r_ref, o_ref, *rest):
        o_ref[...] = _dot_tn(l_ref[...], r_ref[...]).astype(BF16)
        if with_colsum:
            rest[0][...] = jnp.sum(l_ref[...].astype(F32), axis=0, keepdims=True)

    out_shape = [jax.ShapeDtypeStruct((k, n), BF16)]
    out_specs = [pl.BlockSpec((rb, n), lambda j: (j, 0))]
    if with_colsum:
        out_shape.append(jax.ShapeDtypeStruct((1, k), F32))
        out_specs.append(pl.BlockSpec((1, rb), lambda j: (0, j)))
    res, ro = _pallas(
        body, (lhs, rhs), name=name, grid=(k // rb,), out_shape=tuple(out_shape),
        in_specs=[pl.BlockSpec((t, rb), lambda j: (0, j)), pl.BlockSpec((t, n), lambda j: (0, 0))],
        out_specs=tuple(out_specs), sem=("arbitrary",), vmem=VMEM_BIG, rider=rider)
    if rider is not None:
        return res[0], ro
    return res if with_colsum else res[0]


def _lane_blocks(nseq, seq, nblk, tm=TM):
    spt = seq // tm
    return pl.BlockSpec((1, nblk, tm, 128), lambda i: (i // spt, 0, i % spt, 0))


def _inproj_fwd(h, gain, wint, b_in, nseq, rider=None):
    t = h.shape[0]
    seq = t // nseq
    cut_a = 5 * MXU_DIM
    pieces = ((0, cut_a, 0, 0), (cut_a, ZA_W - cut_a, 0, cut_a), (ZA_W, ZB_W, 1, 0), (ZA_W + ZB_W, 1024, 2, 0),
              (ZA_W + ZB_W + 1024, 1024, 2, 1024))

    def body(h_ref, gain_ref, w_hbm, b_ref, u_ref, za_ref, zb_ref, zg_ref, w_v):
        @pl.when(pl.program_id(0) == 0)
        def _():
            pltpu.sync_copy(w_hbm, w_v)

        hh = h_ref[...]
        r = lax.rsqrt(jnp.mean(hh * hh, axis=-1, keepdims=True) + EPS)
        un = (hh * r * gain_ref[...]).astype(BF16)
        u_ref[...] = un
        outs = (None, zb_ref, zg_ref)
        for c0, cw, oi, o0 in pieces:
            val = _dot_nt(un, w_v[c0:c0 + cw, :]) + b_ref[:, c0:c0 + cw]
            if oi == 0:
                for cb in range(cw // 128):
                    za_ref[0, o0 // 128 + cb] = val[:, cb * 128:(cb + 1) * 128]
            else:
                outs[oi][:, o0:o0 + cw] = val.astype(BF16)

    row = lambda w: pl.BlockSpec((TM, w), lambda i: (i, 0))
    return _pallas(
        body, (h, gain, wint, b_in), name="inproj_fwd", grid=(t // TM,),
        out_shape=(jax.ShapeDtypeStruct((t, D_MODEL), BF16), jax.ShapeDtypeStruct((nseq, ZA_W // 128, seq, 128), F32),
                   jax.ShapeDtypeStruct((t, ZB_W), BF16), jax.ShapeDtypeStruct((t, 2 * D_MODEL), BF16)),
        in_specs=[row(D_MODEL), pl.BlockSpec((1, D_MODEL), lambda i: (0, 0)), ANY,
                  pl.BlockSpec((1, D_IN), lambda i: (0, 0))],
        out_specs=(row(D_MODEL), _lane_blocks(nseq, seq, ZA_W // 128), row(ZB_W), row(2 * D_MODEL)),
        scratch_shapes=[pltpu.VMEM((D_IN, D_MODEL), BF16)], sem=("arbitrary",), vmem=VMEM_BIG, rider=rider)


def _inproj_bwd(dz, dh2, h, gain, wint, rider=None):
    t = h.shape[0]
    nc = 5
    cw = D_IN // nc

    def body(dz_ref, dh2_ref, h_ref, gain_ref, w_hbm, dh_ref, gg_ref, w_v):
        @pl.when(pl.program_id(0) == 0)
        def _():
            pltpu.sync_copy(w_hbm, w_v)
            gg_ref[...] = jnp.zeros_like(gg_ref)

        du = jnp.zeros((TM, D_MODEL), F32)
        for ci in range(nc):
            sl = slice(ci * cw, (ci + 1) * cw)
            du = du + _dot(dz_ref[:, sl], w_v[sl, :])
        hh = h_ref[...]
        r = lax.rsqrt(jnp.mean(hh * hh, axis=-1, keepdims=True) + EPS)
        hn = hh * r
        gg_ref[...] += jnp.sum(du * hn, axis=0, keepdims=True)
        dng = du * gain_ref[...]
        dh_ref[...] = dh2_ref[...] + r * (dng - hn * jnp.mean(dng * hn, axis=-1, keepdims=True))

    row = lambda w: pl.BlockSpec((TM, w), lambda i: (i, 0))
    vec = pl.BlockSpec((1, D_MODEL), lambda i: (0, 0))
    return _pallas(
        body, (dz, dh2, h, gain, wint), name="inproj_bwd", grid=(t // TM,),
        out_shape=(jax.ShapeDtypeStruct((t, D_MODEL), F32), jax.ShapeDtypeStruct((1, D_MODEL), F32)),
        in_specs=[row(D_IN), row(D_MODEL), row(D_MODEL), vec, ANY],
        out_specs=(row(D_MODEL), vec),
        scratch_shapes=[pltpu.VMEM((D_IN, D_MODEL), BF16)], sem=("arbitrary",), vmem=VMEM_BIG, rider=rider)


def _head_sums(x):
    w = x.shape[1]
    i = lax.broadcasted_iota(jnp.int32, (w, w), 0) // HEAD_DIM
    j = lax.broadcasted_iota(jnp.int32, (w, w), 1) // HEAD_DIM
    ones = (i == j).astype(BF16)
    hi = x.astype(BF16)
    r1 = x - hi.astype(F32)
    mid = r1.astype(BF16)
    lo = (r1 - mid.astype(F32)).astype(BF16)
    return _dot(hi, ones) + _dot(mid, ones) + _dot(lo, ones)


def _merge_fwd(o0, o1, o2, l0, l1, l2, yb, zg, h1, wat, wbt, wout, rider=None):
    t = h1.shape[0]
    nseq, _, seq, _ = o0.shape

    def body(o0_ref, o1_ref, o2_ref, l0_ref, l1_ref, l2_ref, yb_ref, ga_ref, gb_ref, h1_ref, wa_ref, wb_ref, wo_ref,
             h2_ref, y_ref, lt_ref, pa_ref, pb_ref, mg_ref):
        wide = lambda ref: jnp.concatenate([ref[0, 0], ref[0, 1]], axis=1)
        la, lb, lc = wide(l0_ref), wide(l1_ref), wide(l2_ref)
        mx = jnp.maximum(jnp.maximum(la, lb), lc)
        ea, eb, ec = jnp.exp(la - mx), jnp.exp(lb - mx), jnp.exp(lc - mx)
        den = ea + eb + ec
        y = (ea * wide(o0_ref) + eb * wide(o1_ref) + ec * wide(o2_ref)) / den
        lt = mx + jnp.log(den)
        lt_ref[0, 0] = lt[:, :128]
        lt_ref[0, 1] = lt[:, 128:]
        yb16 = y.astype(BF16)
        y_ref[...] = yb16
        pa = _dot_nt(yb16, wa_ref[...])
        pb = _dot_nt(yb_ref[...], wb_ref[...])
        pa_ref[...] = pa.astype(BF16)
        pb_ref[...] = pb.astype(BF16)
        mg = (_sigmoid(ga_ref[...].astype(F32)) * pa + _sigmoid(gb_ref[...].astype(F32)) * pb).astype(BF16)
        mg_ref[...] = mg
        h2_ref[...] = h1_ref[...] + _dot(mg, wo_ref[...])

    row = lambda w: pl.BlockSpec((TM, w), lambda i: (i, 0))
    full = lambda a: pl.BlockSpec(a.shape, lambda i: (0, 0))
    gate = lambda cb: pl.BlockSpec((TM, D_MODEL), lambda i: (i, cb))
    return _pallas(
        body, (o0, o1, o2, l0, l1, l2, yb, zg, zg, h1, wat, wbt, wout), name="merge_fwd", grid=(t // TM,),
        out_shape=(jax.ShapeDtypeStruct((t, D_MODEL), F32), jax.ShapeDtypeStruct((t, GW), BF16),
                   jax.ShapeDtypeStruct((nseq, 2, seq, 128), F32), jax.ShapeDtypeStruct((t, D_MODEL), BF16),
                   jax.ShapeDtypeStruct((t, D_MODEL), BF16), jax.ShapeDtypeStruct((t, D_MODEL), BF16)),
        in_specs=[_lane_blocks(nseq, seq, 2)] * 6 + [row(2 * GW), gate(0), gate(1), row(D_MODEL), full(wat), full(wbt),
                                                     full(wout)],
        out_specs=(row(D_MODEL), row(GW), _lane_blocks(nseq, seq, 2), row(D_MODEL), row(D_MODEL), row(D_MODEL)),
        sem=("parallel",), vmem=VMEM_BIG, rider=rider)


def _merge_bwd(dh2, pa, pb, zg, y, yb, wat, wbt, wout, nseq, rider=None):
    t = dh2.shape[0]

    def body(dh2_ref, pa_ref, pb_ref, ga_ref, gb_ref, y_ref, yb_ref, wa_ref, wb_ref, wo_ref,
             dpa_ref, dpb_ref, dga_ref, dgb_ref, dya_ref, dyb_ref, dh2b_ref, ca_ref, cb_ref):
        d16 = dh2_ref[...].astype(BF16)
        dh2b_ref[...] = d16
        dm = _dot_nt(d16, wo_ref[...])
        sa = _sigmoid(ga_ref[...].astype(F32))
        sb = _sigmoid(gb_ref[...].astype(F32))
        dpa = (dm * sa).astype(BF16)
        dpb = (dm * sb).astype(BF16)
        dpa_ref[...] = dpa
        dpb_ref[...] = dpb
        dga_ref[...] = (dm * pa_ref[...].astype(F32) * sa * (1.0 - sa)).astype(BF16)
        dgb_ref[...] = (dm * pb_ref[...].astype(F32) * sb * (1.0 - sb)).astype(BF16)
        dya = _dot(dpa, wa_ref[...])
        dyb = _dot(dpb, wb_ref[...])
        dya_ref[0, 0] = dya[:, :128]
        dya_ref[0, 1] = dya[:, 128:]
        dyb_ref[...] = dyb.astype(BF16)
        ca = _head_sums(dya * y_ref[...].astype(F32))
        ca_ref[0, 0] = ca[:, :128]
        ca_ref[0, 1] = ca[:, 128:]
        cb_ref[...] = _head_sums(dyb * yb_ref[...].astype(F32))

    row = lambda w: pl.BlockSpec((TM, w), lambda i: (i, 0))
    full = lambda a: pl.BlockSpec(a.shape, lambda i: (0, 0))
    gate = lambda cb: pl.BlockSpec((TM, D_MODEL), lambda i: (i, cb))
    bf = lambda w: jax.ShapeDtypeStruct((t, w), BF16)
    lanes = jax.ShapeDtypeStruct((nseq, 2, t // nseq, 128), F32)
    lane_spec = _lane_blocks(nseq, t // nseq, 2)
    return _pallas(
        body, (dh2, pa, pb, zg, zg, y, yb, wat, wbt, wout), name="merge_bwd", grid=(t // TM,),
        out_shape=(bf(D_MODEL), bf(D_MODEL), bf(D_MODEL), bf(D_MODEL), lanes, bf(2 * GW), bf(D_MODEL),
                   lanes, jax.ShapeDtypeStruct((t, 2 * GW), F32)),
        in_specs=[row(D_MODEL), row(D_MODEL), row(D_MODEL), gate(0), gate(1), row(GW), row(2 * GW),
                  full(wat), full(wbt), full(wout)],
        out_specs=(row(D_MODEL), row(D_MODEL), row(D_MODEL), row(D_MODEL), lane_spec, row(2 * GW), row(D_MODEL),
                   lane_spec, row(2 * GW)),
        sem=("parallel",), vmem=VMEM_BIG, rider=rider)


def _loss_head(h3, gain, tgt):
    t = h3.shape[0]

    def body(h_ref, gain_ref, t_ref, dh_ref, loss_ref, gg_ref):
        @pl.when(pl.program_id(0) == 0)
        def _():
            loss_ref[...] = jnp.zeros_like(loss_ref)
            gg_ref[...] = jnp.zeros_like(gg_ref)

        hh = h_ref[...]
        r = lax.rsqrt(jnp.mean(hh * hh, axis=-1, keepdims=True) + EPS)
        hn = hh * r
        err = hn * gain_ref[...] - t_ref[...]
        part = jnp.sum(jnp.sum(err * err, axis=1, keepdims=True), axis=0, keepdims=True)
        loss_ref[...] += (0.5 / D_MODEL) * part
        dy = err * (1.0 / D_MODEL)
        gg_ref[...] += jnp.sum(dy * hn, axis=0, keepdims=True)
        dng = dy * gain_ref[...]
        dh_ref[...] = r * (dng - hn * jnp.mean(dng * hn, axis=-1, keepdims=True))

    row = pl.BlockSpec((TM, D_MODEL), lambda i: (i, 0))
    vec = pl.BlockSpec((1, D_MODEL), lambda i: (0, 0))
    return pl.pallas_call(
        body, name="loss_head", grid=(t // TM,),
        out_shape=(jax.ShapeDtypeStruct((t, D_MODEL), F32), jax.ShapeDtypeStruct((8, 128), F32),
                   jax.ShapeDtypeStruct((1, D_MODEL), F32)),
        in_specs=[row, vec, row], out_specs=(row, pl.BlockSpec((8, 128), lambda i: (0, 0)), vec),
        compiler_params=_params(("arbitrary",)),
    )(h3, gain, tgt)


def _lane_head(rows):
    return lax.broadcasted_iota(jnp.int32, (rows, GW), 1) // HEAD_DIM


def _kv_expand_matrix(r):
    ci = lax.broadcasted_iota(jnp.int32, (2 * HEAD_DIM, GW), 0)
    ji = lax.broadcasted_iota(jnp.int32, (2 * HEAD_DIM, GW), 1)
    return (ci == (ji % HEAD_DIM) + HEAD_DIM * r).astype(BF16)


def _block_rows(row0, stride, ib):
    start = row0 + (stride * BLOCK) * ib
    if stride > 1:
        return pl.ds(start, BLOCK, stride=stride)
    return pl.ds(pl.multiple_of(start, BLOCK), BLOCK)


def _stack_heads(x, lane_head):
    return jnp.concatenate([jnp.where(lane_head == h, x, jnp.zeros_like(x)) for h in range(4)], axis=0)


def _unstack_heads(x4, lane_head):
    out = jnp.zeros((BLOCK, GW), F32)
    for h in range(4):
        out = jnp.where(lane_head == h, x4[h * BLOCK:(h + 1) * BLOCK], out)
    return out


def _load_rows(ref, rows, split):
    if split:
        return jnp.concatenate([ref[0, 0, rows, :], ref[0, 1, rows, :]], axis=1)
    return ref[0, rows, :]


def _store_rows(ref, rows, val, split):
    if split:
        ref[0, 0, rows, :] = val[:, :128]
        ref[0, 1, rows, :] = val[:, 128:]
    else:
        ref[0, rows, :] = val


def _attn_fwd(q_arr, k_arr, v_arr, bias, sink, *, grid, seq, stride, kvw, split, q_spec, k_spec, v_spec, bias_map,
              sink_map, o_spec, has_sink, o_shape, o_dtype, name, rider=None):
    nb = seq // stride // BLOCK
    scale = HEAD_DIM ** -0.5
    expanded = kvw != GW
    rps = RESIDUES_PER_STEP if stride >= 4 * RESIDUES_PER_STEP else 1
    grid = (grid[0], grid[1] // rps)
    assert not has_sink or B_WINDOW - 1 < BLOCK

    def body(q_ref, k_ref, v_ref, bias_ref, sink_ref, o_ref, lse_ref, *kv_x):
        rr = pl.program_id(1)
        lane_head = _lane_head(BLOCK)
        if expanded:
            expand = _kv_expand_matrix(rr)
            kv_x[0][...] = _dot(k_ref[0], expand).astype(BF16)
            kv_x[1][...] = _dot(v_ref[0], expand).astype(BF16)
        for j in range(rps):
            residue(rr * rps + j if stride > 1 else 0, q_ref, k_ref, v_ref, bias_ref, sink_ref, o_ref, lse_ref, kv_x,
                    lane_head)

    def residue(row0, q_ref, k_ref, v_ref, bias_ref, sink_ref, o_ref, lse_ref, kv_x, lane_head):
        def per_head(fn, x):
            return jnp.concatenate([fn(sink_ref[0, h:h + 1, 0:1], x[h * BLOCK:(h + 1) * BLOCK]) for h in range(4)],
                                   axis=0)

        def load(ref, ib):
            return _load_rows(ref, _block_rows(row0, stride, ib), split).astype(BF16)

        def load_kv(which, ib):
            if expanded:
                return kv_x[which][_block_rows(0, 1, ib), :]
            return load((k_ref, v_ref)[which], ib)

        def block(ib, first):
            q4 = _stack_heads(load(q_ref, ib), lane_head)
            if first:
                kc, vc = load_kv(0, ib), load_kv(1, ib)
                b4 = bias_ref[:, :, BLOCK:].reshape(4 * BLOCK, BLOCK)
            else:
                kc = jnp.concatenate([load_kv(0, ib - 1), load_kv(0, ib)], axis=0)
                vc = jnp.concatenate([load_kv(1, ib - 1), load_kv(1, ib)], axis=0)
                b4 = bias_ref[...].reshape(4 * BLOCK, 2 * BLOCK)
                if has_sink:
                    oldest = lax.broadcasted_iota(jnp.int32, kc.shape, 0) == 0
                    kc = jnp.where(oldest, jnp.zeros_like(kc), kc)
                    vc = jnp.where(oldest, jnp.zeros_like(vc), vc)
            s = _dot_nt(q4, kc) * scale + b4
            m = jnp.max(s, axis=-1, keepdims=True)
            if has_sink and first:
                m = per_head(jnp.maximum, m)
            p = jnp.exp(s - m)
            l = jnp.sum(p, axis=-1, keepdims=True)
            if has_sink and first:
                l = l + per_head(lambda sk, mh: jnp.exp(sk - mh), m)
            o4 = _dot(p.astype(BF16), vc) / l
            rows = _block_rows(row0, stride, ib)
            _store_rows(o_ref, rows, _unstack_heads(o4, lane_head).astype(o_dtype), split)
            _store_rows(lse_ref, rows, _unstack_heads(m + jnp.log(l), lane_head), split)

        block(0, True)
        if nb > 1:
            def step(i, carry):
                block(i, False)
                return carry
            lax.fori_loop(1, nb, step, 0)

    return _pallas(
        body, (q_arr, k_arr, v_arr, bias, sink), name=name, grid=grid,
        out_shape=(jax.ShapeDtypeStruct(o_shape, o_dtype), jax.ShapeDtypeStruct(o_shape, F32)),
        in_specs=[q_spec, k_spec, v_spec,
                  pl.BlockSpec((4, BLOCK, 2 * BLOCK), bias_map), pl.BlockSpec((1, 4, 128), sink_map)],
        out_specs=(o_spec, o_spec),
        scratch_shapes=[pltpu.VMEM((seq, GW), BF16)] * 2 if expanded else [],
        sem=("arbitrary", "arbitrary"), vmem=VMEM_BIG, rider=rider)


def _attn_bwd(q_arr, k_arr, v_arr, bias, sink, dy, cc, lse, *, grid, seq, stride, kvw, split, q_spec, k_spec, v_spec,
              bias_map, sink_map, o_spec, kv_out_spec, has_sink, n_bias, dq_shape, dkv_shape, g_dtype, name):
    ln = seq // stride
    nb = ln // BLOCK
    scale = HEAD_DIM ** -0.5
    expanded = kvw != GW
    rps = RESIDUES_PER_STEP if stride >= 4 * RESIDUES_PER_STEP else 1
    grid = (grid[0], grid[1] // rps)

    def body(q_ref, k_ref, v_ref, bias_ref, sink_ref, dy_ref, c_ref, lse_ref,
             dq_ref, dk_ref, dv_ref, db_ref, dsk_ref, dk_acc, dv_acc, dk_half, dv_half, *kv_x):
        rr = pl.program_id(1)

        @pl.when((pl.program_id(0) == 0) & (rr == 0))
        def _():
            db_ref[...] = jnp.zeros_like(db_ref)
            dsk_ref[...] = jnp.zeros_like(dsk_ref)

        if expanded:
            expand = _kv_expand_matrix(rr)
            kv_x[0][...] = _dot(k_ref[0], expand).astype(BF16)
            kv_x[1][...] = _dot(v_ref[0], expand).astype(BF16)
        refs = (q_ref, k_ref, v_ref, bias_ref, sink_ref, dy_ref, c_ref, lse_ref, dq_ref, dk_ref, dv_ref, db_ref,
                dsk_ref, dk_acc, dv_acc, dk_half, dv_half, kv_x)
        for j in range(rps):
            residue(rr, rr * rps + j if stride > 1 else 0, *refs)

    def residue(rr, row0, q_ref, k_ref, v_ref, bias_ref, sink_ref, dy_ref, c_ref, lse_ref,
                dq_ref, dk_ref, dv_ref, db_ref, dsk_ref, dk_acc, dv_acc, dk_half, dv_half, kv_x):
        dk_acc[...] = jnp.zeros_like(dk_acc)
        dv_acc[...] = jnp.zeros_like(dv_acc)
        lane_head = _lane_head(BLOCK)
        hb = 4 * rr if n_bias == 8 else 0

        def load(ref, ib):
            return _load_rows(ref, _block_rows(row0, stride, ib), split)

        def load_kv(which, ib):
            if expanded:
                return kv_x[which][_block_rows(0, 1, ib), :]
            return load((k_ref, v_ref)[which], ib).astype(BF16)

        def head_col(x):
            return jnp.concatenate([x[:, h * HEAD_DIM:h * HEAD_DIM + 1] for h in range(4)], axis=0)

        def block(ib, first):
            q4 = _stack_heads(load(q_ref, ib).astype(BF16), lane_head)
            dy4 = _stack_heads(load(dy_ref, ib).astype(BF16), lane_head)
            c4 = head_col(load(c_ref, ib))
            l4 = head_col(load(lse_ref, ib))
            if first:
                kc, vc = load_kv(0, ib), load_kv(1, ib)
                b4 = bias_ref[:, :, BLOCK:].reshape(4 * BLOCK, BLOCK)
                krows = pl.ds(0, BLOCK)
            else:
                kc = jnp.concatenate([load_kv(0, ib - 1), load_kv(0, ib)], axis=0)
                vc = jnp.concatenate([load_kv(1, ib - 1), load_kv(1, ib)], axis=0)
                b4 = bias_ref[...].reshape(4 * BLOCK, 2 * BLOCK)
                krows = pl.ds(pl.multiple_of((ib - 1) * BLOCK, BLOCK), 2 * BLOCK)
            nk = BLOCK if first else 2 * BLOCK
            p = jnp.exp(_dot_nt(q4, kc) * scale + b4 - l4)
            ds = p * (_dot_nt(dy4, vc) - c4)
            ds3 = ds.reshape(4, BLOCK, nk)
            if n_bias == 8:
                if first:
                    db_ref[pl.ds(hb, 4), :, BLOCK:] += ds3
                else:
                    db_ref[pl.ds(hb, 4)] += ds3
            elif first:
                db_ref[:, :, BLOCK:] += ds3
            else:
                db_ref[...] += ds3
            ds16 = ds.astype(BF16)
            dq = _unstack_heads(_dot(ds16, kc), lane_head) * scale
            _store_rows(dq_ref, _block_rows(row0, stride, ib), dq.astype(g_dtype), split)
            dk_acc[krows, :] += _dot_tn(ds16, q4) * scale
            dv_acc[krows, :] += _dot_tn(p.astype(BF16), dy4)
            if has_sink:
                for h in range(4):
                    hs = slice(h * BLOCK, (h + 1) * BLOCK)
                    sk = sink_ref[0, h:h + 1, 0:1]
                    val = -jnp.sum(jnp.exp(sk - l4[hs]) * c4[hs], axis=0, keepdims=True)
                    dsk_ref[hb + h] += jnp.broadcast_to(val, (8, 128))

        block(0, True)
        if nb > 1:
            def step(i, carry):
                block(i, False)
                return carry
            lax.fori_loop(1, nb, step, 0)

        if kvw == GW:
            all_rows = pl.ds(row0, ln, stride=stride) if stride > 1 else pl.ds(0, ln)
            _store_rows(dk_ref, all_rows, dk_acc[...].astype(g_dtype), split)
            _store_rows(dv_ref, all_rows, dv_acc[...].astype(g_dtype), split)
        else:
            def fold(acc):
                t2 = acc[:, :2 * HEAD_DIM] + acc[:, 2 * HEAD_DIM:]
                t2 = t2 + pltpu.roll(t2, HEAD_DIM, 1)
                lane = lax.broadcasted_iota(jnp.int32, t2.shape, 1) // HEAD_DIM
                return jnp.where(lane == rr, t2, 0.0)

            @pl.when(rr == 0)
            def _():
                dk_half[...] = fold(dk_acc[...])
                dv_half[...] = fold(dv_acc[...])

            @pl.when(rr == 1)
            def _():
                dk_ref[0] = (dk_half[...] + fold(dk_acc[...])).astype(g_dtype)
                dv_ref[0] = (dv_half[...] + fold(dv_acc[...])).astype(g_dtype)

    return pl.pallas_call(
        body, name=name, grid=grid,
        out_shape=(jax.ShapeDtypeStruct(dq_shape, g_dtype), jax.ShapeDtypeStruct(dkv_shape, g_dtype),
                   jax.ShapeDtypeStruct(dkv_shape, g_dtype), jax.ShapeDtypeStruct((n_bias, BLOCK, 2 * BLOCK), F32),
                   jax.ShapeDtypeStruct((8, 8, 128), F32)),
        in_specs=[q_spec, k_spec, v_spec,
                  pl.BlockSpec((4, BLOCK, 2 * BLOCK), bias_map), pl.BlockSpec((1, 4, 128), sink_map),
                  o_spec, o_spec, o_spec],
        out_specs=(o_spec, kv_out_spec, kv_out_spec,
                   pl.BlockSpec((n_bias, BLOCK, 2 * BLOCK), lambda n, r: (0, 0, 0)),
                   pl.BlockSpec((8, 8, 128), lambda n, r: (0, 0, 0))),
        scratch_shapes=[pltpu.VMEM((ln, GW), F32), pltpu.VMEM((ln, GW), F32),
                        pltpu.VMEM((ln, 2 * HEAD_DIM), F32), pltpu.VMEM((ln, 2 * HEAD_DIM), F32)]
        + ([pltpu.VMEM((seq, GW), BF16)] * 2 if expanded else []),
        compiler_params=_params(("arbitrary", "arbitrary"), VMEM_BIG),
    )(q_arr, k_arr, v_arr, bias, sink, dy, cc, lse)


def _bias_grad(ds_all, buckets):
    def body(ds_ref, bk_ref, o_ref):
        rows = lax.broadcasted_iota(jnp.int32, (N_BUCKETS, 128), 0)
        cols = lax.broadcasted_iota(jnp.int32, (N_BUCKETS, 128), 1)

        def per_bucket(b, acc):
            for h in range(20):
                gi = h // 4 if h < 12 else 3
                v = jnp.where(bk_ref[gi] == b, ds_ref[h], 0.0)
                v = jnp.sum(jnp.sum(v, axis=1, keepdims=True), axis=0, keepdims=True)
                acc = jnp.where((rows == b) & (cols == h), v, acc)
            return acc

        o_ref[...] = lax.fori_loop(0, N_BUCKETS, per_bucket, jnp.zeros((N_BUCKETS, 128), F32))

    vm = pl.BlockSpec(memory_space=pltpu.VMEM)
    return pl.pallas_call(body, name="bias_grad", out_shape=jax.ShapeDtypeStruct((N_BUCKETS, 128), F32),
                          in_specs=[vm, vm], out_specs=vm)(ds_all, buckets)


def _adamw(w, g, m, v, name):
    r, c = w.shape
    tr = r
    for cand in (256, 176, 128, 64, 32, 16, 8):
        if r % cand == 0:
            tr = cand
            break
    bc1 = 1.0 - ADAM_B1 ** ADAM_STEP
    bc2 = 1.0 - ADAM_B2 ** ADAM_STEP

    def body(w_ref, g_ref, m_ref, v_ref, d_ref, nm_ref, nv_ref):
        gv = g_ref[...]
        nm = ADAM_B1 * m_ref[...] + (1.0 - ADAM_B1) * gv
        nv = ADAM_B2 * v_ref[...] + (1.0 - ADAM_B2) * (gv * gv)
        nm_ref[...] = nm
        nv_ref[...] = nv
        d_ref[...] = -ADAM_LR * ((nm / bc1) / (jnp.sqrt(nv / bc2) + ADAM_EPS) + ADAM_WD * w_ref[...])

    spec = pl.BlockSpec((tr, c), lambda i: (i, 0))
    shp = jax.ShapeDtypeStruct((r, c), F32)
    return pl.pallas_call(body, name=name, grid=(r // tr,), out_shape=(shp, shp, shp),
                          in_specs=[spec] * 4, out_specs=(spec, spec, spec),
                          compiler_params=_params(("parallel",)))(w, g, m, v)


def _t5_bucket(dist):
    max_exact = N_BUCKETS // 2
    n = jnp.maximum(dist, 0)
    nf = jnp.maximum(n, 1).astype(F32)
    large = max_exact + (jnp.log(nf / max_exact) / math.log(MAX_DISTANCE / max_exact)
                         * (N_BUCKETS - max_exact)).astype(jnp.int32)
    large = jnp.minimum(large, N_BUCKETS - 1)
    return jnp.where(n < max_exact, n, large)


def _bias_tables(rel_bias):
    qi = jnp.arange(BLOCK)[:, None]
    ki = jnp.arange(2 * BLOCK)[None, :]
    dist = qi + BLOCK - ki
    specs = [(d, w // d, 4 * gi, 4 * gi + 4) for gi, (w, d) in enumerate(DIL_GROUPS)] + [(1, B_WINDOW - 1, 12, 20)]
    biases, buckets = [], []
    for stride, steps, h0, h1 in specs:
        valid = (dist >= 0) & (dist <= steps)
        bk = jnp.where(valid, _t5_bucket(dist * stride), -1).astype(jnp.int32)
        onehot = (bk[None, :, :] == jnp.arange(N_BUCKETS, dtype=jnp.int32)[:, None, None]).astype(F32)
        b = jnp.einsum("bqk,bh->hqk", onehot, rel_bias[:, h0:h1], precision=lax.Precision.HIGHEST)
        biases.append(jnp.where(valid[None], b, NEG))
        buckets.append(bk)
    return jnp.concatenate(biases, axis=0), jnp.stack(buckets, axis=0)


def _local_step(x, tgt, W, S, shards=None):
    nseq, seq, _ = x.shape
    t = nseq * seq
    xf = x.reshape(t, D_MODEL)
    bias_all, buckets = _bias_tables(S["rel_bias"])
    sink_b = jnp.broadcast_to(S["sinks"].reshape(2, 4, 1), (2, 4, 128)).astype(F32)
    sink_0 = jnp.zeros((1, 4, 128), F32)
    dist = shards is not None
    W = dict(W)
    G, GS, reduced = {}, {}, {}

    def put(keys, gathered):
        for k, g in zip(keys, gathered):
            W[k] = g.reshape(_FULL_SHAPE.get(k, (N_CHIPS * shards[k].shape[0], D_MODEL)))

    def gather_rider(keys):
        return _GatherRider([shards[k] for k in keys]) if dist else None

    def pair(keys):
        return _pair_reduce([G[k].reshape(N_CHIPS, 2, shards[k].shape[0] // 2, D_MODEL) for k in keys],
                            "grad_pair_reduce_" + keys[0])

    def finish(keys, own, rec):
        full = _final_reduce(own, rec, "grad_final_reduce_" + keys[0])
        off = 0
        for k in keys:
            r = shards[k].shape[0]
            reduced[k] = full[:, off:off + r // 2].reshape(r, D_MODEL)
            off += r // 2

    if dist:
        first = ("wgt1", "wut1", "wd1")
        put(first, _gather_rows([shards[k] for k in first]))
    keys = ("wint",)
    (h1, n1, g1, u1), ro = _ffn_fwd(xf, S["ffn1_norm"], W["wgt1"], W["wut1"], W["wd1"], rider=gather_rider(keys))
    put(keys, ro)
    keys = ("wout", "wat", "wbt", "wgt2")
    (un, za, zb, zg), ro = _inproj_fwd(h1, S["mix_norm"], W["wint"], S["b_in"], nseq, rider=gather_rider(keys))
    put(keys, ro)

    seq3 = lambda a: a.reshape(nseq, seq, a.shape[-1])
    zb3 = seq3(zb)
    pair_blk = lambda cb: pl.BlockSpec((1, 2, seq, 128), lambda n, r, cb=cb: (n, cb, 0, 0))
    a_cfg = []
    outs, lses = [], []
    for gi, (_, d) in enumerate(DIL_GROUPS):
        cfg = dict(grid=(nseq, d), seq=seq, stride=d, kvw=GW, split=True,
                   q_spec=pair_blk(gi), k_spec=pair_blk(3 + gi), v_spec=pair_blk(6 + gi), o_spec=pair_blk(0),
                   bias_map=lambda n, r: (0, 0, 0), sink_map=lambda n, r: (0, 0, 0), has_sink=False)
        a_cfg.append(cfg)
        (o, lse), _ = _attn_fwd(za, za, za, bias_all[4 * gi:4 * gi + 4], sink_0, o_shape=(nseq, 2, seq, 128),
                                o_dtype=F32, name=f"attn_a{gi}_fwd", **cfg)
        outs.append(o)
        lses.append(lse)
    wide_blk = lambda w, cmap: pl.BlockSpec((1, seq, w), cmap)
    b_cfg = dict(grid=(nseq, 2), seq=seq, stride=1, kvw=2 * HEAD_DIM, split=False,
                 q_spec=wide_blk(GW, lambda n, r: (n, 0, r)), k_spec=wide_blk(2 * HEAD_DIM, lambda n, r: (n, 0, 4)),
                 v_spec=wide_blk(2 * HEAD_DIM, lambda n, r: (n, 0, 5)), o_spec=wide_blk(GW, lambda n, r: (n, 0, r)),
                 bias_map=lambda n, r: (r, 0, 0), sink_map=lambda n, r: (r, 0, 0), has_sink=True)
    keys = ("wut2",)
    bias_b_fwd = bias_all[12:20].at[:, :, 0].set(jnp.broadcast_to(S["sinks"].reshape(8, 1), (8, BLOCK)))
    (yb, lse_b), ro = _attn_fwd(zb3, zb3, zb3, bias_b_fwd, sink_b, o_shape=(nseq, seq, 2 * GW), o_dtype=BF16,
                                name="attn_b_fwd", rider=gather_rider(keys), **b_cfg)
    put(keys, ro)
    yb = yb.reshape(t, 2 * GW)

    keys = ("wd2",)
    (h2, y, lse_tot, pa, pb, merged), ro = _merge_fwd(outs[0], outs[1], outs[2], lses[0], lses[1], lses[2], yb, zg, h1,
                                                      W["wat"], W["wbt"], W["wout"], rider=gather_rider(keys))
    put(keys, ro)
    (h3, n2, g2, u2), _ = _ffn_fwd(h2, S["ffn2_norm"], W["wgt2"], W["wut2"], W["wd2"])
    dh3, loss_part, g_final = _loss_head(h3, S["final_norm"].reshape(1, D_MODEL), tgt.reshape(t, D_MODEL))

    GS["final_norm"] = g_final
    dh2, dg2, du2, a2, df2, GS["ffn2_norm"] = _ffn_bwd(dh3, h2, S["ffn2_norm"], g2, u2, W["wgt2"], W["wut2"], W["wd2"])
    G["wgt2"] = _wgrad(dg2, n2, MXU_DIM, name="wgrad_gate2")
    G["wut2"] = _wgrad(du2, n2, MXU_DIM, name="wgrad_up2")
    G["wd2"] = _wgrad(a2, df2, MXU_DIM, name="wgrad_down2")

    keys = ("wgt2", "wut2", "wd2")
    rider = _ExchangeRider([pair(keys)]) if dist else None
    (dpa, dpb, dga, dgb, dya, dyb, dh2b, ca, cb), ro = _merge_bwd(dh2, pa, pb, zg, y, yb, W["wat"], W["wbt"], W["wout"],
                                                                  nseq, rider=rider)
    if dist:
        finish(keys, *ro)
    G["wout"] = _wgrad(merged, dh2b, MXU_DIM, name="wgrad_out")
    G["wat"] = _wgrad(dpa, y, MXU_DIM, name="wgrad_branch_a")
    G["wbt"] = _wgrad(dpb, yb, MXU_DIM, name="wgrad_branch_b")

    dqs, dks, dvs, dbs = [], [], [], []
    shp = (nseq, 2, seq, 128)
    halves = lambda a: [a[:, hf].reshape(t, 128).astype(BF16) for hf in range(2)]
    for gi in range(len(DIL_GROUPS)):
        dq, dk, dv, db, _ = _attn_bwd(za, za, za, bias_all[4 * gi:4 * gi + 4], sink_0, dya, ca, lse_tot,
                                      n_bias=4, dq_shape=shp, dkv_shape=shp, g_dtype=F32,
                                      kv_out_spec=a_cfg[gi]["o_spec"], name=f"attn_a{gi}_bwd", **a_cfg[gi])
        dqs += halves(dq)
        dks += halves(dk)
        dvs += halves(dv)
        dbs.append(db)
    dqb, dkb, dvb, dbb, dsink = _attn_bwd(zb3, zb3, zb3, bias_all[12:20], sink_b, seq3(dyb), seq3(cb), lse_b,
                                          n_bias=8, dq_shape=(nseq, seq, 2 * GW),
                                          dkv_shape=(nseq, seq, 2 * HEAD_DIM), g_dtype=BF16,
                                          kv_out_spec=wide_blk(2 * HEAD_DIM, lambda n, r: (n, 0, 0)),
                                          name="attn_b_bwd", **b_cfg)
    dz = jnp.concatenate(dqs + dks + dvs + [dqb.reshape(t, 2 * GW), dkb.reshape(t, 2 * HEAD_DIM),
                                            dvb.reshape(t, 2 * HEAD_DIM), dga, dgb], axis=-1)
    gb_tab = _bias_grad(jnp.concatenate(dbs + [dbb], axis=0), buckets)
    if dist:
        GS["bias_tab"], GS["sink_tiles"] = gb_tab, dsink
    else:
        GS["rel_bias"] = gb_tab[:, :20]
        GS["sinks"] = dsink[:, 0, 0].reshape(1, 8)

    G["wint"], GS["b_in"] = _wgrad(dz, un, MXU_DIM, with_colsum=True, name="wgrad_in")
    keys = ("wint", "wout", "wat", "wbt")
    rider = _ExchangeRider([pair(keys)]) if dist else None
    (dh1, GS["mix_norm"]), ro = _inproj_bwd(dz, dh2, h1, S["mix_norm"], W["wint"], rider=rider)
    if dist:
        finish(keys, *ro)

    dx, dg1, du1, a1, df1, GS["ffn1_norm"] = _ffn_bwd(dh1, xf, S["ffn1_norm"], g1, u1, W["wgt1"], W["wut1"], W["wd1"])
    G["wgt1"] = _wgrad(dg1, n1, MXU_DIM, name="wgrad_gate1")
    if dist:
        G["wut1"], ro = _wgrad(du1, n1, MXU_DIM, name="wgrad_up1", rider=_ExchangeRider([pair(("wgt1",))]))
        finish(("wgt1",), *ro)
        G["wd1"], ro = _wgrad(a1, df1, MXU_DIM, name="wgrad_down1", rider=_ExchangeRider([pair(("wut1",))]))
        finish(("wut1",), *ro)
        finish(("wd1",), *_chip_exchange([pair(("wd1",))]))
    else:
        G["wut1"] = _wgrad(du1, n1, MXU_DIM, name="wgrad_up1")
        G["wd1"] = _wgrad(a1, df1, MXU_DIM, name="wgrad_down1")
    return loss_part, dx.reshape(x.shape), (reduced if dist else G), GS


_SMALL = ("ffn1_norm", "mix_norm", "ffn2_norm", "final_norm", "b_in", "sinks", "rel_bias")
_ORDER = ("ffn1_norm", "ffn1_w_gate", "ffn1_w_up", "ffn1_w_down", "mix_norm", "w_in", "b_in", "w_branch_a",
          "w_branch_b", "w_out", "sinks", "rel_bias", "ffn2_norm", "ffn2_w_gate", "ffn2_w_up", "ffn2_w_down",
          "final_norm")
_BIG = (("wgt1", "ffn1_w_gate", True, 704), ("wut1", "ffn1_w_up", True, 704), ("wd1", "ffn1_w_down", False, 704),
        ("wint", "w_in", True, 1280), ("wout", "w_out", False, 256), ("wat", "w_branch_a", True, 64),
        ("wbt", "w_branch_b", True, 128), ("wgt2", "ffn2_w_gate", True, 704), ("wut2", "ffn2_w_up", True, 704),
        ("wd2", "ffn2_w_down", False, 704))
_FULL_SHAPE = {"wat": (D_MODEL, GW), "wbt": (D_MODEL, 2 * GW)}


def kernel(x, ffn1_norm, ffn1_w_gate, ffn1_w_up, ffn1_w_down, mix_norm, w_in, b_in, w_branch_a, w_branch_b, w_out, sinks, rel_bias, ffn2_norm, ffn2_w_gate, ffn2_w_up, ffn2_w_down, final_norm, loss_target, m_ffn1_norm, m_ffn1_w_gate, m_ffn1_w_up, m_ffn1_w_down, m_mix_norm, m_w_in, m_b_in, m_w_branch_a, m_w_branch_b, m_w_out, m_sinks, m_rel_bias, m_ffn2_norm, m_ffn2_w_gate, m_ffn2_w_up, m_ffn2_w_down, m_final_norm, v_ffn1_norm, v_ffn1_w_gate, v_ffn1_w_up, v_ffn1_w_down, v_mix_norm, v_w_in, v_b_in, v_w_branch_a, v_w_branch_b, v_w_out, v_sinks, v_rel_bias, v_ffn2_norm, v_ffn2_w_gate, v_ffn2_w_up, v_ffn2_w_down, v_final_norm):
    args = dict(locals())
    w = {n: args[n] for n in _ORDER}
    m = {n: args["m_" + n] for n in _ORDER}
    v = {n: args["v_" + n] for n in _ORDER}

    shards = {}
    for key, name, transposed, rows in _BIG:
        a = w[name][0]
        a = (a.T if transposed else a).astype(BF16)
        shards[key] = a.reshape(rows, D_MODEL)
    S = {n: w[n] for n in _SMALL}

    loss_part, grad_x, reduced, GS = _local_step(x, loss_target, {}, S, shards)

    small = _allreduce_small(GS["ffn1_norm"], GS["mix_norm"], GS["ffn2_norm"], GS["final_norm"], GS["b_in"],
                             GS["sink_tiles"], GS["bias_tab"], loss_part)
    loss = small[9, 8]

    out_g, out_d, out_m, out_v = {}, {}, {}, {}
    for key, n, transposed, rows in _BIG:
        nat = w[n][0].shape
        if transposed and nat[1] % 128:
            res = _adamw(w[n][0].T, reduced[key], m[n][0].T, v[n][0].T, "adamw_" + n)
            res = [reduced[key].T] + [r.T for r in res]
        else:
            g = reduced[key].reshape(nat[1], nat[0]).T if transposed else reduced[key].reshape(nat)
            res = [g] + list(_adamw(w[n][0], g, m[n][0], v[n][0], "adamw_" + n))
        out_g[n], out_d[n], out_m[n], out_v[n] = [r[None] for r in res]
    row = lambda d: {n: (d[n].reshape(1, D_MODEL) if n == "final_norm" else d[n]) for n in _SMALL}
    for dst, src in zip((out_g, out_d, out_m, out_v), _adamw_small(small, row(w), row(m), row(v))):
        dst.update(src)
        dst["final_norm"] = src["final_norm"].reshape(D_MODEL)

    return (loss, grad_x, *[out_g[n] for n in _ORDER], *[out_d[n] for n in _ORDER],
            *[out_m[n] for n in _ORDER], *[out_v[n] for n in _ORDER])
```

```python
import math

import jax
import jax.numpy as jnp
from jax import lax
from jax.experimental import pallas as pl
from jax.experimental.pallas import tpu as pltpu

F32, BF16 = jnp.float32, jnp.bfloat16
MESH = pl.DeviceIdType.MESH

D_MODEL = 1024
D_FF = 2816
D_IN = 5120
HEAD_DIM = 64
BLOCK = 128
DIL_GROUPS = ((128, 1), (512, 4), (2048, 16))
B_WINDOW = 128
N_BUCKETS = 32
MAX_DISTANCE = 2048
EPS = 1e-6
N_CHIPS = 4
GW = 256
ZA_W = 2304
ZB_W = 768
NEG = -1e30

ADAM_LR, ADAM_B1, ADAM_B2, ADAM_EPS, ADAM_WD, ADAM_STEP = 0.001, 0.9, 0.999, 1e-08, 0.01, 10

VMEM_BIG = 56 * 1024 * 1024
TM = 512
TM_BWD = 256
MXU_DIM = 256
FF_BOUNDS = (0, 6 * MXU_DIM, D_FF)
DMA_SPLIT = 8
RESIDUES_PER_STEP = 4
ATTN_UNROLL = 5


def _dot(a, b):
    return jnp.dot(a, b, preferred_element_type=F32)


def _dot_nt(a, b):
    return lax.dot_general(a, b, (((1,), (1,)), ((), ())), preferred_element_type=F32)


def _dot_tn(a, b):
    return lax.dot_general(a, b, (((0,), (0,)), ((), ())), preferred_element_type=F32)


def _sigmoid(x):
    return 0.5 * jnp.tanh(0.5 * x) + 0.5


def _params(sem, vmem=None):
    return pltpu.CompilerParams(dimension_semantics=sem, vmem_limit_bytes=vmem)


ANY = pl.BlockSpec(memory_space=pl.ANY)


def _me():
    return lax.axis_index("x"), lax.axis_index("y"), lax.axis_index("c")


_CHIP_RELS = ((1, 0), (0, 1), (1, 1))


def _flip(v, f):
    return 1 - v if f else v


def _remote(src, dst, ssem, rsem, peer):
    return pltpu.make_async_remote_copy(src_ref=src, dst_ref=dst, send_sem=ssem, recv_sem=rsem,
                                        device_id=peer, device_id_type=MESH)


def _row_pieces(rows, n):
    step = max(16, -(-rows // n) // 16 * 16)
    out, s = [], 0
    while s < rows:
        out.append((s, min(step, rows - s)))
        s += step
    return out


def _gather_rows(shards):
    nt = len(shards)
    rows = [s.shape[0] for s in shards]

    def body(*refs):
        srcs, outs = refs[:nt], refs[nt:2 * nt]
        halves, quarters = refs[2 * nt:3 * nt], refs[3 * nt:4 * nt]
        ici_s, ici_r, fwd_s, fwd_r, d2d_s, d2d_r, keep, loc = refs[4 * nt:]
        x, y, c = _me()
        j = 2 * x + y
        sib = (x, y, 1 - c)
        nbr = ((1 - x, y, c), (x, 1 - y, c))
        nbr_j = (2 * (1 - x) + y, 2 * x + (1 - y))
        diag_j = 2 * (1 - x) + (1 - y)
        local = [pltpu.make_async_copy(srcs[t], outs[t].at[j], loc.at[t]) for t in range(nt)]
        for cp in local:
            cp.start()
        pending = []
        for a in range(2):
            for t in range(nt):
                half = pl.ds(c * (rows[t] // 2), rows[t] // 2)
                cp = _remote(srcs[t].at[half], halves[t].at[a], ici_s.at[2 * t + a], ici_r.at[2 * t + a], nbr[a])
                cp.start()
                pending.append(cp)
        placed = []

        def place(src, dst_of, idx):
            mine = pltpu.make_async_copy(src, dst_of, keep.at[idx])
            mine.start()
            cp = _remote(src, dst_of, d2d_s.at[idx], d2d_r.at[idx], sib)
            cp.start()
            placed.append((mine, cp))

        for a in range(2):
            for t in range(nt):
                r2, r4 = rows[t] // 2, rows[t] // 4
                got = halves[t].at[a]
                _remote(got, got, ici_s.at[2 * t + a], ici_r.at[2 * t + a], nbr[a]).wait_recv()
                cp = _remote(halves[t].at[a, pl.ds(a * r4, r4)], quarters[t].at[a], fwd_s.at[2 * t + a],
                             fwd_r.at[2 * t + a], nbr[1 - a])
                cp.start()
                pending.append(cp)
                place(got, outs[t].at[nbr_j[a], pl.ds(c * r2, r2)], 4 * t + a)
        for a in range(2):
            for t in range(nt):
                r2, r4 = rows[t] // 2, rows[t] // 4
                got = quarters[t].at[a]
                _remote(got, got, fwd_s.at[2 * t + a], fwd_r.at[2 * t + a], nbr[1 - a]).wait_recv()
                place(got, outs[t].at[diag_j, pl.ds(c * r2 + a * r4, r4)], 4 * t + 2 + a)
        for mine, cp in placed:
            mine.wait()
            cp.wait()
        for cp in pending:
            cp.wait_send()
        for cp in local:
            cp.wait()

    stage = ([pltpu.VMEM((2, r // 2, D_MODEL), BF16) for r in rows] + [pltpu.VMEM((2, r // 4, D_MODEL), BF16) for r in rows])
    sems = ([pltpu.SemaphoreType.DMA((2 * nt,)) for _ in range(4)] + [pltpu.SemaphoreType.DMA((4 * nt,))] * 3
            + [pltpu.SemaphoreType.DMA((nt,))])
    return pl.pallas_call(
        body, name="gather_weights",
        out_shape=tuple(jax.ShapeDtypeStruct((N_CHIPS,) + s.shape, s.dtype) for s in shards),
        in_specs=[pl.BlockSpec(memory_space=pltpu.VMEM)] * nt,
        out_specs=tuple([ANY] * nt), scratch_shapes=stage + sems,
    )(*shards)


VMEM_WHOLE = pl.BlockSpec(memory_space=pltpu.VMEM)


def _pair_reduce(grads, name):
    nt = len(grads)
    r2 = [g.shape[2] for g in grads]
    off = [sum(r2[:t]) for t in range(nt)]
    tot = sum(r2)

    def body(*refs):
        gs = refs[:nt]
        s_ref, got, ssem, rsem = refs[nt:]
        x, y, c = _me()
        sib = (x, y, 1 - c)
        for t in range(nt):
            for k in range(N_CHIPS):
                _remote(gs[t].at[k, 1 - c], got.at[k, pl.ds(off[t], r2[t])], ssem, rsem, sib).start()
        _remote(got, got, ssem, rsem, sib).wait()
        for t in range(nt):
            for k in range(N_CHIPS):
                rows = slice(off[t], off[t] + r2[t])
                s_ref[k, rows, :] = (gs[t][k, c].astype(F32) + got[k, rows, :].astype(F32)).astype(BF16)

    shp = jax.ShapeDtypeStruct((N_CHIPS, tot, D_MODEL), BF16)
    return pl.pallas_call(
        body, name=name, out_shape=shp, in_specs=[VMEM_WHOLE] * nt, out_specs=VMEM_WHOLE,
        scratch_shapes=[pltpu.VMEM((N_CHIPS, tot, D_MODEL), BF16), pltpu.SemaphoreType.DMA(()),
                        pltpu.SemaphoreType.DMA(())],
        compiler_params=pltpu.CompilerParams(vmem_limit_bytes=VMEM_BIG),
    )(*grads)


def _chip_exchange(parts):
    ng = len(parts)
    r2 = [p.shape[1] for p in parts]
    off = [sum(r2[:g]) for g in range(ng)]
    tot = sum(r2)

    def body(*refs):
        ps = refs[:ng]
        own_ref, rec_ref, ssems, rsems, lsem = refs[ng:]
        x, y, c = _me()
        j = 2 * x + y
        for g in range(ng):
            pltpu.make_async_copy(ps[g].at[j], own_ref.at[pl.ds(off[g], r2[g])], lsem).start()
        for k, (fx, fy) in enumerate(_CHIP_RELS):
            px, py = _flip(x, fx), _flip(y, fy)
            for g in range(ng):
                for st, sz in _row_pieces(r2[g], 2):
                    _remote(ps[g].at[2 * px + py, pl.ds(st, sz)], rec_ref.at[k, pl.ds(off[g] + st, sz)],
                            ssems.at[k], rsems.at[k], (px, py, c)).start()
        for k in range(3):
            _remote(rec_ref.at[k], rec_ref.at[k], ssems.at[k], rsems.at[k], (x, y, c)).wait()
        pltpu.make_async_copy(own_ref, own_ref, lsem).wait()

    return pl.pallas_call(
        body, name="grad_chip_exchange",
        out_shape=(jax.ShapeDtypeStruct((tot, D_MODEL), BF16), jax.ShapeDtypeStruct((3, tot, D_MODEL), BF16)),
        in_specs=[VMEM_WHOLE] * ng, out_specs=(ANY, ANY),
        scratch_shapes=[pltpu.SemaphoreType.DMA((3,)), pltpu.SemaphoreType.DMA((3,)), pltpu.SemaphoreType.DMA(())],
    )(*parts)


def _final_reduce(own, rec, name):
    r2 = own.shape[0]
    pieces = _row_pieces(r2, DMA_SPLIT)

    def body(own_ref, rec_ref, o_ref, fbuf, ssem, rsem, lsem):
        x, y, c = _me()
        sib = (x, y, 1 - c)
        for st, sz in pieces:
            rows = slice(st, st + sz)
            fbuf[rows, :] = (own_ref[rows, :].astype(F32) + rec_ref[0, rows, :].astype(F32)
                             + rec_ref[1, rows, :].astype(F32) + rec_ref[2, rows, :].astype(F32))
            pltpu.make_async_copy(fbuf.at[pl.ds(st, sz)], o_ref.at[c, pl.ds(st, sz)], lsem).start()
            _remote(fbuf.at[pl.ds(st, sz)], o_ref.at[c, pl.ds(st, sz)], ssem, rsem, sib).start()
        _remote(fbuf, o_ref.at[c], ssem, rsem, sib).wait()
        pltpu.make_async_copy(fbuf, o_ref.at[c], lsem).wait()

    return pl.pallas_call(
        body, name=name, out_shape=jax.ShapeDtypeStruct((2, r2, D_MODEL), F32),
        in_specs=[VMEM_WHOLE, VMEM_WHOLE], out_specs=ANY,
        scratch_shapes=[pltpu.VMEM((r2, D_MODEL), F32), pltpu.SemaphoreType.DMA(()), pltpu.SemaphoreType.DMA(()),
                        pltpu.SemaphoreType.DMA(())],
        compiler_params=pltpu.CompilerParams(vmem_limit_bytes=VMEM_BIG),
    )(own, rec)


SMALL_ROWS = 48


def _allreduce_small(g_ffn1, g_mix, g_ffn2, g_final, g_bin, dsink, bias_tab, loss_part):
    def body(f1_ref, mx_ref, f2_ref, fn_ref, bi_ref, sk_ref, bt_ref, ls_ref, o_ref, mine, buf, send_sems, recv_sems):
        x, y, c = _me()
        me = 4 * x + 2 * y + c
        mine[...] = jnp.zeros_like(mine)
        for r, ref in enumerate((f1_ref, mx_ref, f2_ref, fn_ref)):
            mine[r:r + 1, :] = ref[...]
        for k in range(D_IN // D_MODEL):
            mine[4 + k:5 + k, :] = bi_ref[:, k * D_MODEL:(k + 1) * D_MODEL]
        lane = lax.broadcasted_iota(jnp.int32, (1, 128), 1)
        row = jnp.where(lane == 8, ls_ref[0:1, :], 0.0)
        for h in range(8):
            row = jnp.where(lane == h, sk_ref[h, 0:1, :], row)
        mine[9:10, 0:128] = row
        mine[16:48, 0:128] = bt_ref[...]
        buf[me] = mine[...]
        copies = []
        for k in range(1, 8):
            peer = (_flip(x, (k >> 2) & 1), _flip(y, (k >> 1) & 1), _flip(c, k & 1))
            cp = _remote(mine, buf.at[me], send_sems.at[k - 1], recv_sems.at[k - 1], peer)
            cp.start()
            copies.append(cp)
        for cp in copies:
            cp.wait()
        acc = buf[0]
        for i in range(1, 8):
            acc = acc + buf[i]
        o_ref[...] = acc

    vm = pl.BlockSpec(memory_space=pltpu.VMEM)
    shape = (SMALL_ROWS, D_MODEL)
    return pl.pallas_call(
        body, name="allreduce_small", out_shape=jax.ShapeDtypeStruct(shape, F32),
        in_specs=[vm] * 8, out_specs=vm,
        scratch_shapes=[pltpu.VMEM(shape, F32), pltpu.VMEM((8,) + shape, F32), pltpu.SemaphoreType.DMA((7,)),
                        pltpu.SemaphoreType.DMA((7,))],
    )(g_ffn1, g_mix, g_ffn2, g_final, g_bin, dsink, bias_tab, loss_part)


def _adam_update(w, g, m, v):
    nm = ADAM_B1 * m + (1.0 - ADAM_B1) * g
    nv = ADAM_B2 * v + (1.0 - ADAM_B2) * (g * g)
    bc1 = 1.0 - ADAM_B1 ** ADAM_STEP
    bc2 = 1.0 - ADAM_B2 ** ADAM_STEP
    return -ADAM_LR * ((nm / bc1) / (jnp.sqrt(nv / bc2) + ADAM_EPS) + ADAM_WD * w), nm, nv


def _adamw_small(packed, w, m, v):
    names = ("ffn1_norm", "mix_norm", "ffn2_norm", "final_norm", "b_in", "sinks", "rel_bias")
    nn = len(names)

    def grad_of(p_ref, name, k=0):
        if name == "b_in":
            return p_ref[4 + k:5 + k, :]
        if name == "sinks":
            return p_ref[9:10, 0:8]
        if name == "rel_bias":
            return p_ref[16:48, 0:20]
        r = names.index(name)
        return p_ref[r:r + 1, :]

    def body(p_ref, *refs):
        ws, ms, vs = refs[:nn], refs[nn:2 * nn], refs[2 * nn:3 * nn]
        outs = refs[3 * nn:]
        for i, name in enumerate(names):
            og, od, om, ov = outs[i], outs[nn + i], outs[2 * nn + i], outs[3 * nn + i]
            pieces = range(D_IN // D_MODEL) if name == "b_in" else (0,)
            for k in pieces:
                sl = (slice(None), slice(k * D_MODEL, (k + 1) * D_MODEL)) if name == "b_in" else (Ellipsis,)
                g = grad_of(p_ref, name, k)
                d, nm, nv = _adam_update(ws[i][sl], g, ms[i][sl], vs[i][sl])
                og[sl], od[sl], om[sl], ov[sl] = g, d, nm, nv

    vm = pl.BlockSpec(memory_space=pltpu.VMEM)
    shapes = [jax.ShapeDtypeStruct(w[n].shape, F32) for n in names]
    res = pl.pallas_call(
        body, name="adamw_small", out_shape=tuple(shapes * 4), in_specs=[vm] * (1 + 3 * nn),
        out_specs=tuple([vm] * (4 * nn)),
    )(packed, *[w[n] for n in names], *[m[n] for n in names], *[v[n] for n in names])
    return [dict(zip(names, res[i * nn:(i + 1) * nn])) for i in range(4)]


class _GatherRider:
    def __init__(self, shards):
        self.inputs = list(shards)
        nt = len(shards)
        self.out_shape = [jax.ShapeDtypeStruct((N_CHIPS,) + s.shape, s.dtype) for s in shards]
        self.scratch = [pltpu.SemaphoreType.DMA((3 * nt,)), pltpu.SemaphoreType.DMA((3 * nt,)),
                        pltpu.SemaphoreType.DMA((nt,))]

    def _copies(self, srcs, outs, sems):
        ici_s, ici_r, loc = sems
        x, y, c = _me()
        j = 2 * x + y
        local = [pltpu.make_async_copy(srcs[t], outs[t].at[j], loc.at[t]) for t in range(len(srcs))]
        remote = []
        for k, (fx, fy) in enumerate(_CHIP_RELS):
            peer = (_flip(x, fx), _flip(y, fy), c)
            for t in range(len(srcs)):
                remote.append(_remote(srcs[t], outs[t].at[j], ici_s.at[3 * t + k], ici_r.at[3 * t + k], peer))
        return local, remote

    def start(self, srcs, outs, sems):
        local, remote = self._copies(srcs, outs, sems)
        for cp in local + remote:
            cp.start()

    def finish(self, srcs, outs, sems):
        local, remote = self._copies(srcs, outs, sems)
        for cp in remote + local:
            cp.wait()


class _ExchangeRider:
    def __init__(self, parts):
        self.inputs = list(parts)
        self.r2 = [p.shape[1] for p in parts]
        self.off = [sum(self.r2[:g]) for g in range(len(parts))]
        tot = sum(self.r2)
        self.out_shape = [jax.ShapeDtypeStruct((tot, D_MODEL), BF16), jax.ShapeDtypeStruct((3, tot, D_MODEL), BF16)]
        self.scratch = [pltpu.SemaphoreType.DMA((3,)), pltpu.SemaphoreType.DMA((3,)), pltpu.SemaphoreType.DMA(())]

    def start(self, ps, outs, sems):
        own_ref, rec_ref = outs
        ssems, rsems, lsem = sems
        x, y, c = _me()
        j = 2 * x + y
        for g in range(len(ps)):
            pltpu.make_async_copy(ps[g].at[j], own_ref.at[pl.ds(self.off[g], self.r2[g])], lsem).start()
        for k, (fx, fy) in enumerate(_CHIP_RELS):
            px, py = _flip(x, fx), _flip(y, fy)
            for g in range(len(ps)):
                for st, sz in _row_pieces(self.r2[g], 2):
                    _remote(ps[g].at[2 * px + py, pl.ds(st, sz)], rec_ref.at[k, pl.ds(self.off[g] + st, sz)],
                            ssems.at[k], rsems.at[k], (px, py, c)).start()

    def finish(self, ps, outs, sems):
        own_ref, rec_ref = outs
        ssems, rsems, lsem = sems
        x, y, c = _me()
        for k in range(3):
            _remote(rec_ref.at[k], rec_ref.at[k], ssems.at[k], rsems.at[k], (x, y, c)).wait()
        pltpu.make_async_copy(own_ref, own_ref, lsem).wait()


def _pallas(body, args, *, name, grid, in_specs, out_specs, out_shape, scratch_shapes=(), sem=None, vmem=None,
            rider=None):
    if rider is None:
        res = pl.pallas_call(body, name=name, grid=grid, in_specs=list(in_specs), out_specs=tuple(out_specs),
                             out_shape=tuple(out_shape), scratch_shapes=list(scratch_shapes),
                             compiler_params=_params(sem, vmem))(*args)
        return tuple(res), ()
    n_in, n_out, n_sc = len(in_specs), len(out_shape), len(scratch_shapes)
    r_in, r_out = len(rider.inputs), len(rider.out_shape)

    def wrapped(*refs):
        ins, rins = refs[:n_in], refs[n_in:n_in + r_in]
        p = n_in + r_in
        outs, routs = refs[p:p + n_out], refs[p + n_out:p + n_out + r_out]
        p += n_out + r_out
        scr, rsems = refs[p:p + n_sc], refs[p + n_sc:]
        first = pl.program_id(0) == 0
        last = pl.program_id(0) == grid[0] - 1
        for a in range(1, len(grid)):
            first = first & (pl.program_id(a) == 0)
            last = last & (pl.program_id(a) == grid[a] - 1)

        @pl.when(first)
        def _():
            rider.start(rins, routs, rsems)

        body(*ins, *outs, *scr)

        @pl.when(last)
        def _():
            rider.finish(rins, routs, rsems)

    res = pl.pallas_call(
        wrapped, name=name, grid=grid, in_specs=list(in_specs) + [ANY] * r_in,
        out_specs=tuple(out_specs) + (ANY,) * r_out, out_shape=tuple(out_shape) + tuple(rider.out_shape),
        scratch_shapes=list(scratch_shapes) + rider.scratch,
        compiler_params=_params(("arbitrary",) * len(grid), vmem))(*args, *rider.inputs)
    return tuple(res[:n_out]), tuple(res[n_out:])


def _ffn_fwd(h, gain, wgt, wut, wd, rider=None):
    t = h.shape[0]

    def body(h_ref, gain_ref, wg_hbm, wu_hbm, wd_hbm, hout_ref, n_ref, g_ref, u_ref, wg_v, wu_v, wd_v):
        @pl.when(pl.program_id(0) == 0)
        def _():
            pltpu.sync_copy(wg_hbm, wg_v)
            pltpu.sync_copy(wu_hbm, wu_v)
            pltpu.sync_copy(wd_hbm, wd_v)

        hh = h_ref[...]
        r = lax.rsqrt(jnp.mean(hh * hh, axis=-1, keepdims=True) + EPS)
        n = (hh * r * gain_ref[...]).astype(BF16)
        n_ref[...] = n
        acc = jnp.zeros((TM, D_MODEL), F32)
        for c0, c1 in zip(FF_BOUNDS[:-1], FF_BOUNDS[1:]):
            sl = slice(c0, c1)
            g = _dot_nt(n, wg_v[sl, :])
            u = _dot_nt(n, wu_v[sl, :])
            g_ref[:, sl] = g.astype(BF16)
            u_ref[:, sl] = u.astype(BF16)
            a = (g * _sigmoid(g) * u).astype(BF16)
            acc = acc + _dot(a, wd_v[sl, :])
        hout_ref[...] = hh + 0.5 * acc

    row = lambda w: pl.BlockSpec((TM, w), lambda i: (i, 0))
    wv = pltpu.VMEM((D_FF, D_MODEL), BF16)
    return _pallas(
        body, (h, gain, wgt, wut, wd), name="ffn_fwd", grid=(t // TM,),
        out_shape=(jax.ShapeDtypeStruct((t, D_MODEL), F32), jax.ShapeDtypeStruct((t, D_MODEL), BF16),
                   jax.ShapeDtypeStruct((t, D_FF), BF16), jax.ShapeDtypeStruct((t, D_FF), BF16)),
        in_specs=[row(D_MODEL), pl.BlockSpec((1, D_MODEL), lambda i: (0, 0)), ANY, ANY, ANY],
        out_specs=(row(D_MODEL), row(D_MODEL), row(D_FF), row(D_FF)),
        scratch_shapes=[wv, wv, wv], sem=("arbitrary",), vmem=VMEM_BIG, rider=rider)


def _ffn_bwd(dhout, h, gain, g, u, wgt, wut, wd):
    t = h.shape[0]
    tm = TM_BWD

    def body(dho_ref, h_ref, gain_ref, g_ref, u_ref, wg_hbm, wu_hbm, wd_hbm,
             dh_ref, dg_ref, du_ref, a_ref, df_ref, gg_ref, wg_v, wu_v, wd_v):
        @pl.when(pl.program_id(0) == 0)
        def _():
            pltpu.sync_copy(wg_hbm, wg_v)
            pltpu.sync_copy(wu_hbm, wu_v)
            pltpu.sync_copy(wd_hbm, wd_v)
            gg_ref[...] = jnp.zeros_like(gg_ref)

        dho = dho_ref[...]
        df = (0.5 * dho).astype(BF16)
        df_ref[...] = df
        dn = jnp.zeros((tm, D_MODEL), F32)
        for c0, c1 in zip(FF_BOUNDS[:-1], FF_BOUNDS[1:]):
            sl = slice(c0, c1)
            da = _dot_nt(df, wd_v[sl, :])
            gv = g_ref[:, sl].astype(F32)
            uv = u_ref[:, sl].astype(F32)
            sg = _sigmoid(gv)
            silu = gv * sg
            dg = (da * uv * (sg * (1.0 + gv * (1.0 - sg)))).astype(BF16)
            du = (da * silu).astype(BF16)
            dg_ref[:, sl] = dg
            du_ref[:, sl] = du
            a_ref[:, sl] = (silu * uv).astype(BF16)
            dn = dn + _dot(dg, wg_v[sl, :]) + _dot(du, wu_v[sl, :])
        hh = h_ref[...]
        r = lax.rsqrt(jnp.mean(hh * hh, axis=-1, keepdims=True) + EPS)
        hn = hh * r
        gg_ref[...] += jnp.sum(dn * hn, axis=0, keepdims=True)
        dng = dn * gain_ref[...]
        dh_ref[...] = dho + r * (dng - hn * jnp.mean(dng * hn, axis=-1, keepdims=True))

    row = lambda w: pl.BlockSpec((tm, w), lambda i: (i, 0))
    vec = pl.BlockSpec((1, D_MODEL), lambda i: (0, 0))
    wv = pltpu.VMEM((D_FF, D_MODEL), BF16)
    return pl.pallas_call(
        body, name="ffn_bwd", grid=(t // tm,),
        out_shape=(jax.ShapeDtypeStruct((t, D_MODEL), F32), jax.ShapeDtypeStruct((t, D_FF), BF16),
                   jax.ShapeDtypeStruct((t, D_FF), BF16), jax.ShapeDtypeStruct((t, D_FF), BF16),
                   jax.ShapeDtypeStruct((t, D_MODEL), BF16), jax.ShapeDtypeStruct((1, D_MODEL), F32)),
        in_specs=[row(D_MODEL), row(D_MODEL), vec, row(D_FF), row(D_FF), ANY, ANY, ANY],
        out_specs=(row(D_MODEL), row(D_FF), row(D_FF), row(D_FF), row(D_MODEL), vec),
        scratch_shapes=[wv, wv, wv],
        compiler_params=_params(("arbitrary",), VMEM_BIG),
    )(dhout, h, gain, g, u, wgt, wut, wd)


def _wgrad(lhs, rhs, rb, with_colsum=False, name="wgrad", rider=None):
    t, k = lhs.shape
    n = rhs.shape[1]

    def body(l_ref, r_ref, o_ref, *rest):
        o_ref[...] = _dot_tn(l_ref[...], r_ref[...]).astype(BF16)
        if with_colsum:
            rest[0][...] = jnp.sum(l_ref[...].astype(F32), axis=0, keepdims=True)

    out_shape = [jax.ShapeDtypeStruct((k, n), BF16)]
    out_specs = [pl.BlockSpec((rb, n), lambda j: (j, 0))]
    if with_colsum:
        out_shape.append(jax.ShapeDtypeStruct((1, k), F32))
        out_specs.append(pl.BlockSpec((1, rb), lambda j: (0, j)))
    res, ro = _pallas(
        body, (lhs, rhs), name=name, grid=(k // rb,), out_shape=tuple(out_shape),
        in_specs=[pl.BlockSpec((t, rb), lambda j: (0, j)), pl.BlockSpec((t, n), lambda j: (0, 0))],
        out_specs=tuple(out_specs), sem=("arbitrary",), vmem=VMEM_BIG, rider=rider)
    if rider is not None:
        return res[0], ro
    return res if with_colsum else res[0]


def _lane_blocks(nseq, seq, nblk, tm=TM):
    spt = seq // tm
    return pl.BlockSpec((1, nblk, tm, 128), lambda i: (i // spt, 0, i % spt, 0))


def _inproj_fwd(h, gain, wint, b_in, nseq, rider=None):
    t = h.shape[0]
    seq = t // nseq
    cut_a = 5 * MXU_DIM
    pieces = ((0, cut_a, 0, 0), (cut_a, ZA_W - cut_a, 0, cut_a), (ZA_W, ZB_W, 1, 0), (ZA_W + ZB_W, 1024, 2, 0),
              (ZA_W + ZB_W + 1024, 1024, 2, 1024))

    def body(h_ref, gain_ref, w_hbm, b_ref, u_ref, za_ref, zb_ref, zg_ref, w_v):
        @pl.when(pl.program_id(0) == 0)
        def _():
            pltpu.sync_copy(w_hbm, w_v)

        hh = h_ref[...]
        r = lax.rsqrt(jnp.mean(hh * hh, axis=-1, keepdims=True) + EPS)
        un = (hh * r * gain_ref[...]).astype(BF16)
        u_ref[...] = un
        outs = (None, zb_ref, zg_ref)
        for c0, cw, oi, o0 in pieces:
            val = _dot_nt(un, w_v[c0:c0 + cw, :]) + b_ref[:, c0:c0 + cw]
            if oi == 0:
                for cb in range(cw // 128):
                    za_ref[0, o0 // 128 + cb] = val[:, cb * 128:(cb + 1) * 128]
            else:
                outs[oi][:, o0:o0 + cw] = val.astype(BF16)

    row = lambda w: pl.BlockSpec((TM, w), lambda i: (i, 0))
    return _pallas(
        body, (h, gain, wint, b_in), name="inproj_fwd", grid=(t // TM,),
        out_shape=(jax.ShapeDtypeStruct((t, D_MODEL), BF16), jax.ShapeDtypeStruct((nseq, ZA_W // 128, seq, 128), F32),
                   jax.ShapeDtypeStruct((t, ZB_W), BF16), jax.ShapeDtypeStruct((t, 2 * D_MODEL), BF16)),
        in_specs=[row(D_MODEL), pl.BlockSpec((1, D_MODEL), lambda i: (0, 0)), ANY,
                  pl.BlockSpec((1, D_IN), lambda i: (0, 0))],
        out_specs=(row(D_MODEL), _lane_blocks(nseq, seq, ZA_W // 128), row(ZB_W), row(2 * D_MODEL)),
        scratch_shapes=[pltpu.VMEM((D_IN, D_MODEL), BF16)], sem=("arbitrary",), vmem=VMEM_BIG, rider=rider)


def _inproj_bwd(dz, dh2, h, gain, wint, rider=None):
    t = h.shape[0]
    nc = 5
    cw = D_IN // nc

    def body(dz_ref, dh2_ref, h_ref, gain_ref, w_hbm, dh_ref, gg_ref, w_v):
        @pl.when(pl.program_id(0) == 0)
        def _():
            pltpu.sync_copy(w_hbm, w_v)
            gg_ref[...] = jnp.zeros_like(gg_ref)

        du = jnp.zeros((TM, D_MODEL), F32)
        for ci in range(nc):
            sl = slice(ci * cw, (ci + 1) * cw)
            du = du + _dot(dz_ref[:, sl], w_v[sl, :])
        hh = h_ref[...]
        r = lax.rsqrt(jnp.mean(hh * hh, axis=-1, keepdims=True) + EPS)
        hn = hh * r
        gg_ref[...] += jnp.sum(du * hn, axis=0, keepdims=True)
        dng = du * gain_ref[...]
        dh_ref[...] = dh2_ref[...] + r * (dng - hn * jnp.mean(dng * hn, axis=-1, keepdims=True))

    row = lambda w: pl.BlockSpec((TM, w), lambda i: (i, 0))
    vec = pl.BlockSpec((1, D_MODEL), lambda i: (0, 0))
    return _pallas(
        body, (dz, dh2, h, gain, wint), name="inproj_bwd", grid=(t // TM,),
        out_shape=(jax.ShapeDtypeStruct((t, D_MODEL), F32), jax.ShapeDtypeStruct((1, D_MODEL), F32)),
        in_specs=[row(D_IN), row(D_MODEL), row(D_MODEL), vec, ANY],
        out_specs=(row(D_MODEL), vec),
        scratch_shapes=[pltpu.VMEM((D_IN, D_MODEL), BF16)], sem=("arbitrary",), vmem=VMEM_BIG, rider=rider)


def _head_sums(x):
    w = x.shape[1]
    i = lax.broadcasted_iota(jnp.int32, (w, w), 0) // HEAD_DIM
    j = lax.broadcasted_iota(jnp.int32, (w, w), 1) // HEAD_DIM
    ones = (i == j).astype(BF16)
    hi = x.astype(BF16)
    r1 = x - hi.astype(F32)
    mid = r1.astype(BF16)
    lo = (r1 - mid.astype(F32)).astype(BF16)
    return _dot(hi, ones) + _dot(mid, ones) + _dot(lo, ones)


def _merge_fwd(o0, o1, o2, l0, l1, l2, yb, zg, h1, wat, wbt, wout, rider=None):
    t = h1.shape[0]
    nseq, _, seq, _ = o0.shape

    def body(o0_ref, o1_ref, o2_ref, l0_ref, l1_ref, l2_ref, yb_ref, ga_ref, gb_ref, h1_ref, wa_ref, wb_ref, wo_ref,
             h2_ref, y_ref, lt_ref, pa_ref, pb_ref, mg_ref):
        wide = lambda ref: jnp.concatenate([ref[0, 0], ref[0, 1]], axis=1)
        la, lb, lc = wide(l0_ref), wide(l1_ref), wide(l2_ref)
        mx = jnp.maximum(jnp.maximum(la, lb), lc)
        ea, eb, ec = jnp.exp(la - mx), jnp.exp(lb - mx), jnp.exp(lc - mx)
        den = ea + eb + ec
        y = (ea * wide(o0_ref) + eb * wide(o1_ref) + ec * wide(o2_ref)) / den
        lt = mx + jnp.log(den)
        lt_ref[0, 0] = lt[:, :128]
        lt_ref[0, 1] = lt[:, 128:]
        yb16 = y.astype(BF16)
        y_ref[...] = yb16
        pa = _dot_nt(yb16, wa_ref[...])
        pb = _dot_nt(yb_ref[...], wb_ref[...])
        pa_ref[...] = pa.astype(BF16)
        pb_ref[...] = pb.astype(BF16)
        mg = (_sigmoid(ga_ref[...].astype(F32)) * pa + _sigmoid(gb_ref[...].astype(F32)) * pb).astype(BF16)
        mg_ref[...] = mg
        h2_ref[...] = h1_ref[...] + _dot(mg, wo_ref[...])

    row = lambda w: pl.BlockSpec((TM, w), lambda i: (i, 0))
    full = lambda a: pl.BlockSpec(a.shape, lambda i: (0, 0))
    gate = lambda cb: pl.BlockSpec((TM, D_MODEL), lambda i: (i, cb))
    return _pallas(
        body, (o0, o1, o2, l0, l1, l2, yb, zg, zg, h1, wat, wbt, wout), name="merge_fwd", grid=(t // TM,),
        out_shape=(jax.ShapeDtypeStruct((t, D_MODEL), F32), jax.ShapeDtypeStruct((t, GW), BF16),
                   jax.ShapeDtypeStruct((nseq, 2, seq, 128), F32), jax.ShapeDtypeStruct((t, D_MODEL), BF16),
                   jax.ShapeDtypeStruct((t, D_MODEL), BF16), jax.ShapeDtypeStruct((t, D_MODEL), BF16)),
        in_specs=[_lane_blocks(nseq, seq, 2)] * 6 + [row(2 * GW), gate(0), gate(1), row(D_MODEL), full(wat), full(wbt),
                                                     full(wout)],
        out_specs=(row(D_MODEL), row(GW), _lane_blocks(nseq, seq, 2), row(D_MODEL), row(D_MODEL), row(D_MODEL)),
        sem=("parallel",), vmem=VMEM_BIG, rider=rider)


def _merge_bwd(dh2, pa, pb, zg, y, yb, wat, wbt, wout, nseq, rider=None):
    t = dh2.shape[0]

    def body(dh2_ref, pa_ref, pb_ref, ga_ref, gb_ref, y_ref, yb_ref, wa_ref, wb_ref, wo_ref,
             dpa_ref, dpb_ref, dga_ref, dgb_ref, dya_ref, dyb_ref, dh2b_ref, ca_ref, cb_ref):
        d16 = dh2_ref[...].astype(BF16)
        dh2b_ref[...] = d16
        dm = _dot_nt(d16, wo_ref[...])
        sa = _sigmoid(ga_ref[...].astype(F32))
        sb = _sigmoid(gb_ref[...].astype(F32))
        dpa = (dm * sa).astype(BF16)
        dpb = (dm * sb).astype(BF16)
        dpa_ref[...] = dpa
        dpb_ref[...] = dpb
        dga_ref[...] = (dm * pa_ref[...].astype(F32) * sa * (1.0 - sa)).astype(BF16)
        dgb_ref[...] = (dm * pb_ref[...].astype(F32) * sb * (1.0 - sb)).astype(BF16)
        dya = _dot(dpa, wa_ref[...])
        dyb = _dot(dpb, wb_ref[...])
        dya_ref[0, 0] = dya[:, :128]
        dya_ref[0, 1] = dya[:, 128:]
        dyb_ref[...] = dyb.astype(BF16)
        ca = _head_sums(dya * y_ref[...].astype(F32))
        ca_ref[0, 0] = ca[:, :128]
        ca_ref[0, 1] = ca[:, 128:]
        cb_ref[...] = _head_sums(dyb * yb_ref[...].astype(F32))

    row = lambda w: pl.BlockSpec((TM, w), lambda i: (i, 0))
    full = lambda a: pl.BlockSpec(a.shape, lambda i: (0, 0))
    gate = lambda cb: pl.BlockSpec((TM, D_MODEL), lambda i: (i, cb))
    bf = lambda w: jax.ShapeDtypeStruct((t, w), BF16)
    lanes = jax.ShapeDtypeStruct((nseq, 2, t // nseq, 128), F32)
    lane_spec = _lane_blocks(nseq, t // nseq, 2)
    return _pallas(
        body, (dh2, pa, pb, zg, zg, y, yb, wat, wbt, wout), name="merge_bwd", grid=(t // TM,),
        out_shape=(bf(D_MODEL), bf(D_MODEL), bf(D_MODEL), bf(D_MODEL), lanes, bf(2 * GW), bf(D_MODEL),
                   lanes, jax.ShapeDtypeStruct((t, 2 * GW), F32)),
        in_specs=[row(D_MODEL), row(D_MODEL), row(D_MODEL), gate(0), gate(1), row(GW), row(2 * GW),
                  full(wat), full(wbt), full(wout)],
        out_specs=(row(D_MODEL), row(D_MODEL), row(D_MODEL), row(D_MODEL), lane_spec, row(2 * GW), row(D_MODEL),
                   lane_spec, row(2 * GW)),
        sem=("parallel",), vmem=VMEM_BIG, rider=rider)


def _loss_head(h3, gain, tgt):
    t = h3.shape[0]

    def body(h_ref, gain_ref, t_ref, dh_ref, loss_ref, gg_ref):
        @pl.when(pl.program_id(0) == 0)
        def _():
            loss_ref[...] = jnp.zeros_like(loss_ref)
            gg_ref[...] = jnp.zeros_like(gg_ref)

        hh = h_ref[...]
        r = lax.rsqrt(jnp.mean(hh * hh, axis=-1, keepdims=True) + EPS)
        hn = hh * r
        err = hn * gain_ref[...] - t_ref[...]
        part = jnp.sum(jnp.sum(err * err, axis=1, keepdims=True), axis=0, keepdims=True)
        loss_ref[...] += (0.5 / D_MODEL) * part
        dy = err * (1.0 / D_MODEL)
        gg_ref[...] += jnp.sum(dy * hn, axis=0, keepdims=True)
        dng = dy * gain_ref[...]
        dh_ref[...] = r * (dng - hn * jnp.mean(dng * hn, axis=-1, keepdims=True))

    row = pl.BlockSpec((TM, D_MODEL), lambda i: (i, 0))
    vec = pl.BlockSpec((1, D_MODEL), lambda i: (0, 0))
    return pl.pallas_call(
        body, name="loss_head", grid=(t // TM,),
        out_shape=(jax.ShapeDtypeStruct((t, D_MODEL), F32), jax.ShapeDtypeStruct((8, 128), F32),
                   jax.ShapeDtypeStruct((1, D_MODEL), F32)),
        in_specs=[row, vec, row], out_specs=(row, pl.BlockSpec((8, 128), lambda i: (0, 0)), vec),
        compiler_params=_params(("arbitrary",)),
    )(h3, gain, tgt)


def _lane_head(rows):
    return lax.broadcasted_iota(jnp.int32, (rows, GW), 1) // HEAD_DIM


def _kv_expand_matrix(r):
    ci = lax.broadcasted_iota(jnp.int32, (2 * HEAD_DIM, GW), 0)
    ji = lax.broadcasted_iota(jnp.int32, (2 * HEAD_DIM, GW), 1)
    return (ci == (ji % HEAD_DIM) + HEAD_DIM * r).astype(BF16)


def _block_rows(row0, stride, ib):
    start = row0 + (stride * BLOCK) * ib
    if stride > 1:
        return pl.ds(start, BLOCK, stride=stride)
    return pl.ds(pl.multiple_of(start, BLOCK), BLOCK)


def _stack_heads(x, lane_head):
    return jnp.concatenate([jnp.where(lane_head == h, x, jnp.zeros_like(x)) for h in range(4)], axis=0)


def _unstack_heads(x4, lane_head):
    out = jnp.zeros((BLOCK, GW), F32)
    for h in range(4):
        out = jnp.where(lane_head == h, x4[h * BLOCK:(h + 1) * BLOCK], out)
    return out


def _load_rows(ref, rows, split):
    if split:
        return jnp.concatenate([ref[0, 0, rows, :], ref[0, 1, rows, :]], axis=1)
    return ref[0, rows, :]


def _store_rows(ref, rows, val, split):
    if split:
        ref[0, 0, rows, :] = val[:, :128]
        ref[0, 1, rows, :] = val[:, 128:]
    else:
        ref[0, rows, :] = val


def _attn_fwd(q_arr, k_arr, v_arr, bias, sink, *, grid, seq, stride, kvw, split, q_spec, k_spec, v_spec, bias_map,
              sink_map, o_spec, has_sink, o_shape, o_dtype, name, rider=None):
    nb = seq // stride // BLOCK
    scale = HEAD_DIM ** -0.5
    expanded = kvw != GW
    rps = RESIDUES_PER_STEP if stride >= 4 * RESIDUES_PER_STEP else 1
    grid = (grid[0], grid[1] // rps)
    assert not has_sink or B_WINDOW - 1 < BLOCK

    def body(q_ref, k_ref, v_ref, bias_ref, sink_ref, o_ref, lse_ref, *kv_x):
        rr = pl.program_id(1)
        lane_head = _lane_head(BLOCK)
        if expanded:
            expand = _kv_expand_matrix(rr)
            kv_x[0][...] = _dot(k_ref[0], expand).astype(BF16)
            kv_x[1][...] = _dot(v_ref[0], expand).astype(BF16)
        for j in range(rps):
            residue(rr * rps + j if stride > 1 else 0, q_ref, k_ref, v_ref, bias_ref, sink_ref, o_ref, lse_ref, kv_x,
                    lane_head)

    def residue(row0, q_ref, k_ref, v_ref, bias_ref, sink_ref, o_ref, lse_ref, kv_x, lane_head):
        def per_head(fn, x):
            return jnp.concatenate([fn(sink_ref[0, h:h + 1, 0:1], x[h * BLOCK:(h + 1) * BLOCK]) for h in range(4)],
                                   axis=0)

        def load(ref, ib):
            return _load_rows(ref, _block_rows(row0, stride, ib), split).astype(BF16)

        def load_kv(which, ib):
            if expanded:
                return kv_x[which][_block_rows(0, 1, ib), :]
            return load((k_ref, v_ref)[which], ib)

        def block(ib, first):
            q4 = _stack_heads(load(q_ref, ib), lane_head)
            if first:
                kc, vc = load_kv(0, ib), load_kv(1, ib)
                b4 = bias_ref[:, :, BLOCK:].reshape(4 * BLOCK, BLOCK)
            else:
                kc = jnp.concatenate([load_kv(0, ib - 1), load_kv(0, ib)], axis=0)
                vc = jnp.concatenate([load_kv(1, ib - 1), load_kv(1, ib)], axis=0)
                b4 = bias_ref[...].reshape(4 * BLOCK, 2 * BLOCK)
                if has_sink:
                    oldest = lax.broadcasted_iota(jnp.int32, kc.shape, 0) == 0
                    kc = jnp.where(oldest, jnp.zeros_like(kc), kc)
                    vc = jnp.where(oldest, jnp.zeros_like(vc), vc)
            s = _dot_nt(q4, kc) * scale + b4
            m = jnp.max(s, axis=-1, keepdims=True)
            if has_sink and first:
                m = per_head(jnp.maximum, m)
            p = jnp.exp(s - m)
            l = jnp.sum(p, axis=-1, keepdims=True)
            if has_sink and first:
                l = l + per_head(lambda sk, mh: jnp.exp(sk - mh), m)
            o4 = _dot(p.astype(BF16), vc) / l
            rows = _block_rows(row0, stride, ib)
            _store_rows(o_ref, rows, _unstack_heads(o4, lane_head).astype(o_dtype), split)
            _store_rows(lse_ref, rows, _unstack_heads(m + jnp.log(l), lane_head), split)

        block(0, True)
        if nb > 1:
            def step(i, carry):
                block(i, False)
                return carry
            lax.fori_loop(1, nb, step, 0, unroll=min(ATTN_UNROLL, nb - 1))

    return _pallas(
        body, (q_arr, k_arr, v_arr, bias, sink), name=name, grid=grid,
        out_shape=(jax.ShapeDtypeStruct(o_shape, o_dtype), jax.ShapeDtypeStruct(o_shape, F32)),
        in_specs=[q_spec, k_spec, v_spec,
                  pl.BlockSpec((4, BLOCK, 2 * BLOCK), bias_map), pl.BlockSpec((1, 4, 128), sink_map)],
        out_specs=(o_spec, o_spec),
        scratch_shapes=[pltpu.VMEM((seq, GW), BF16)] * 2 if expanded else [],
        sem=("arbitrary", "arbitrary"), vmem=VMEM_BIG, rider=rider)


def _attn_bwd(q_arr, k_arr, v_arr, bias, sink, dy, cc, lse, *, grid, seq, stride, kvw, split, q_spec, k_spec, v_spec,
              bias_map, sink_map, o_spec, kv_out_spec, has_sink, n_bias, dq_shape, dkv_shape, g_dtype, name):
    ln = seq // stride
    nb = ln // BLOCK
    scale = HEAD_DIM ** -0.5
    expanded = kvw != GW
    rps = RESIDUES_PER_STEP if stride >= 4 * RESIDUES_PER_STEP else 1
    grid = (grid[0], grid[1] // rps)

    def body(q_ref, k_ref, v_ref, bias_ref, sink_ref, dy_ref, c_ref, lse_ref,
             dq_ref, dk_ref, dv_ref, db_ref, dsk_ref, dk_acc, dv_acc, dk_half, dv_half, *kv_x):
        rr = pl.program_id(1)

        @pl.when((pl.program_id(0) == 0) & (rr == 0))
        def _():
            db_ref[...] = jnp.zeros_like(db_ref)
            dsk_ref[...] = jnp.zeros_like(dsk_ref)

        if expanded:
            expand = _kv_expand_matrix(rr)
            kv_x[0][...] = _dot(k_ref[0], expand).astype(BF16)
            kv_x[1][...] = _dot(v_ref[0], expand).astype(BF16)
        refs = (q_ref, k_ref, v_ref, bias_ref, sink_ref, dy_ref, c_ref, lse_ref, dq_ref, dk_ref, dv_ref, db_ref,
                dsk_ref, dk_acc, dv_acc, dk_half, dv_half, kv_x)
        for j in range(rps):
            residue(rr, rr * rps + j if stride > 1 else 0, *refs)

    def residue(rr, row0, q_ref, k_ref, v_ref, bias_ref, sink_ref, dy_ref, c_ref, lse_ref,
                dq_ref, dk_ref, dv_ref, db_ref, dsk_ref, dk_acc, dv_acc, dk_half, dv_half, kv_x):
        dk_acc[...] = jnp.zeros_like(dk_acc)
        dv_acc[...] = jnp.zeros_like(dv_acc)
        lane_head = _lane_head(BLOCK)
        hb = 4 * rr if n_bias == 8 else 0

        def load(ref, ib):
            return _load_rows(ref, _block_rows(row0, stride, ib), split)

        def load_kv(which, ib):
            if expanded:
                return kv_x[which][_block_rows(0, 1, ib), :]
            return load((k_ref, v_ref)[which], ib).astype(BF16)

        def head_col(x):
            return jnp.concatenate([x[:, h * HEAD_DIM:h * HEAD_DIM + 1] for h in range(4)], axis=0)

        def block(ib, first):
            q4 = _stack_heads(load(q_ref, ib).astype(BF16), lane_head)
            dy4 = _stack_heads(load(dy_ref, ib).astype(BF16), lane_head)
            c4 = head_col(load(c_ref, ib))
            l4 = head_col(load(lse_ref, ib))
            if first:
                kc, vc = load_kv(0, ib), load_kv(1, ib)
                b4 = bias_ref[:, :, BLOCK:].reshape(4 * BLOCK, BLOCK)
                krows = pl.ds(0, BLOCK)
            else:
                kc = jnp.concatenate([load_kv(0, ib - 1), load_kv(0, ib)], axis=0)
                vc = jnp.concatenate([load_kv(1, ib - 1), load_kv(1, ib)], axis=0)
                b4 = bias_ref[...].reshape(4 * BLOCK, 2 * BLOCK)
                krows = pl.ds(pl.multiple_of((ib - 1) * BLOCK, BLOCK), 2 * BLOCK)
            nk = BLOCK if first else 2 * BLOCK
            p = jnp.exp(_dot_nt(q4, kc) * scale + b4 - l4)
            ds = p * (_dot_nt(dy4, vc) - c4)
            ds3 = ds.reshape(4, BLOCK, nk)
            if n_bias == 8:
                if first:
                    db_ref[pl.ds(hb, 4), :, BLOCK:] += ds3
                else:
                    db_ref[pl.ds(hb, 4)] += ds3
            elif first:
                db_ref[:, :, BLOCK:] += ds3
            else:
                db_ref[...] += ds3
            ds16 = ds.astype(BF16)
            dq = _unstack_heads(_dot(ds16, kc), lane_head) * scale
            _store_rows(dq_ref, _block_rows(row0, stride, ib), dq.astype(g_dtype), split)
            dk_acc[krows, :] += _dot_tn(ds16, q4) * scale
            dv_acc[krows, :] += _dot_tn(p.astype(BF16), dy4)
            if has_sink:
                for h in range(4):
                    hs = slice(h * BLOCK, (h + 1) * BLOCK)
                    sk = sink_ref[0, h:h + 1, 0:1]
                    val = -jnp.sum(jnp.exp(sk - l4[hs]) * c4[hs], axis=0, keepdims=True)
                    dsk_ref[hb + h] += jnp.broadcast_to(val, (8, 128))

        block(0, True)
        if nb > 1:
            def step(i, carry):
                block(i, False)
                return carry
            lax.fori_loop(1, nb, step, 0, unroll=min(ATTN_UNROLL, nb - 1))

        if kvw == GW:
            all_rows = pl.ds(row0, ln, stride=stride) if stride > 1 else pl.ds(0, ln)
            _store_rows(dk_ref, all_rows, dk_acc[...].astype(g_dtype), split)
            _store_rows(dv_ref, all_rows, dv_acc[...].astype(g_dtype), split)
        else:
            def fold(acc):
                t2 = acc[:, :2 * HEAD_DIM] + acc[:, 2 * HEAD_DIM:]
                t2 = t2 + pltpu.roll(t2, HEAD_DIM, 1)
                lane = lax.broadcasted_iota(jnp.int32, t2.shape, 1) // HEAD_DIM
                return jnp.where(lane == rr, t2, 0.0)

            @pl.when(rr == 0)
            def _():
                dk_half[...] = fold(dk_acc[...])
                dv_half[...] = fold(dv_acc[...])

            @pl.when(rr == 1)
            def _():
                dk_ref[0] = (dk_half[...] + fold(dk_acc[...])).astype(g_dtype)
                dv_ref[0] = (dv_half[...] + fold(dv_acc[...])).astype(g_dtype)

    return pl.pallas_call(
        body, name=name, grid=grid,
        out_shape=(jax.ShapeDtypeStruct(dq_shape, g_dtype), jax.ShapeDtypeStruct(dkv_shape, g_dtype),
                   jax.ShapeDtypeStruct(dkv_shape, g_dtype), jax.ShapeDtypeStruct((n_bias, BLOCK, 2 * BLOCK), F32),
                   jax.ShapeDtypeStruct((8, 8, 128), F32)),
        in_specs=[q_spec, k_spec, v_spec,
                  pl.BlockSpec((4, BLOCK, 2 * BLOCK), bias_map), pl.BlockSpec((1, 4, 128), sink_map),
                  o_spec, o_spec, o_spec],
        out_specs=(o_spec, kv_out_spec, kv_out_spec,
                   pl.BlockSpec((n_bias, BLOCK, 2 * BLOCK), lambda n, r: (0, 0, 0)),
                   pl.BlockSpec((8, 8, 128), lambda n, r: (0, 0, 0))),
        scratch_shapes=[pltpu.VMEM((ln, GW), F32), pltpu.VMEM((ln, GW), F32),
                        pltpu.VMEM((ln, 2 * HEAD_DIM), F32), pltpu.VMEM((ln, 2 * HEAD_DIM), F32)]
        + ([pltpu.VMEM((seq, GW), BF16)] * 2 if expanded else []),
        compiler_params=_params(("arbitrary", "arbitrary"), VMEM_BIG),
    )(q_arr, k_arr, v_arr, bias, sink, dy, cc, lse)


def _bias_grad(ds_all, buckets):
    def body(ds_ref, bk_ref, o_ref):
        rows = lax.broadcasted_iota(jnp.int32, (N_BUCKETS, 128), 0)
        cols = lax.broadcasted_iota(jnp.int32, (N_BUCKETS, 128), 1)

        def per_bucket(b, acc):
            for h in range(20):
                gi = h // 4 if h < 12 else 3
                v = jnp.where(bk_ref[gi] == b, ds_ref[h], 0.0)
                v = jnp.sum(jnp.sum(v, axis=1, keepdims=True), axis=0, keepdims=True)
                acc = jnp.where((rows == b) & (cols == h), v, acc)
            return acc

        o_ref[...] = lax.fori_loop(0, N_BUCKETS, per_bucket, jnp.zeros((N_BUCKETS, 128), F32))

    vm = pl.BlockSpec(memory_space=pltpu.VMEM)
    return pl.pallas_call(body, name="bias_grad", out_shape=jax.ShapeDtypeStruct((N_BUCKETS, 128), F32),
                          in_specs=[vm, vm], out_specs=vm)(ds_all, buckets)


def _adamw(w, g, m, v, name):
    r, c = w.shape
    tr = r
    for cand in (256, 176, 128, 64, 32, 16, 8):
        if r % cand == 0:
            tr = cand
            break
    bc1 = 1.0 - ADAM_B1 ** ADAM_STEP
    bc2 = 1.0 - ADAM_B2 ** ADAM_STEP

    def body(w_ref, g_ref, m_ref, v_ref, d_ref, nm_ref, nv_ref):
        gv = g_ref[...]
        nm = ADAM_B1 * m_ref[...] + (1.0 - ADAM_B1) * gv
        nv = ADAM_B2 * v_ref[...] + (1.0 - ADAM_B2) * (gv * gv)
        nm_ref[...] = nm
        nv_ref[...] = nv
        d_ref[...] = -ADAM_LR * ((nm / bc1) / (jnp.sqrt(nv / bc2) + ADAM_EPS) + ADAM_WD * w_ref[...])

    spec = pl.BlockSpec((tr, c), lambda i: (i, 0))
    shp = jax.ShapeDtypeStruct((r, c), F32)
    return pl.pallas_call(body, name=name, grid=(r // tr,), out_shape=(shp, shp, shp),
                          in_specs=[spec] * 4, out_specs=(spec, spec, spec),
                          compiler_params=_params(("parallel",)))(w, g, m, v)


def _t5_bucket(dist):
    max_exact = N_BUCKETS // 2
    n = jnp.maximum(dist, 0)
    nf = jnp.maximum(n, 1).astype(F32)
    large = max_exact + (jnp.log(nf / max_exact) / math.log(MAX_DISTANCE / max_exact)
                         * (N_BUCKETS - max_exact)).astype(jnp.int32)
    large = jnp.minimum(large, N_BUCKETS - 1)
    return jnp.where(n < max_exact, n, large)


def _bias_tables(rel_bias):
    qi = jnp.arange(BLOCK)[:, None]
    ki = jnp.arange(2 * BLOCK)[None, :]
    dist = qi + BLOCK - ki
    specs = [(d, w // d, 4 * gi, 4 * gi + 4) for gi, (w, d) in enumerate(DIL_GROUPS)] + [(1, B_WINDOW - 1, 12, 20)]
    biases, buckets = [], []
    for stride, steps, h0, h1 in specs:
        valid = (dist >= 0) & (dist <= steps)
        bk = jnp.where(valid, _t5_bucket(dist * stride), -1).astype(jnp.int32)
        onehot = (bk[None, :, :] == jnp.arange(N_BUCKETS, dtype=jnp.int32)[:, None, None]).astype(F32)
        b = jnp.einsum("bqk,bh->hqk", onehot, rel_bias[:, h0:h1], precision=lax.Precision.HIGHEST)
        biases.append(jnp.where(valid[None], b, NEG))
        buckets.append(bk)
    return jnp.concatenate(biases, axis=0), jnp.stack(buckets, axis=0)


def _local_step(x, tgt, W, S, shards=None):
    nseq, seq, _ = x.shape
    t = nseq * seq
    xf = x.reshape(t, D_MODEL)
    bias_all, buckets = _bias_tables(S["rel_bias"])
    sink_b = jnp.broadcast_to(S["sinks"].reshape(2, 4, 1), (2, 4, 128)).astype(F32)
    sink_0 = jnp.zeros((1, 4, 128), F32)
    dist = shards is not None
    W = dict(W)
    G, GS, reduced = {}, {}, {}

    def put(keys, gathered):
        for k, g in zip(keys, gathered):
            W[k] = g.reshape(_FULL_SHAPE.get(k, (N_CHIPS * shards[k].shape[0], D_MODEL)))

    def gather_rider(keys):
        return _GatherRider([shards[k] for k in keys]) if dist else None

    def pair(keys):
        return _pair_reduce([G[k].reshape(N_CHIPS, 2, shards[k].shape[0] // 2, D_MODEL) for k in keys],
                            "grad_pair_reduce_" + keys[0])

    def finish(keys, own, rec):
        full = _final_reduce(own, rec, "grad_final_reduce_" + keys[0])
        off = 0
        for k in keys:
            r = shards[k].shape[0]
            reduced[k] = full[:, off:off + r // 2].reshape(r, D_MODEL)
            off += r // 2

    if dist:
        first = ("wgt1", "wut1", "wd1")
        put(first, _gather_rows([shards[k] for k in first]))
    keys = ("wint",)
    (h1, n1, g1, u1), ro = _ffn_fwd(xf, S["ffn1_norm"], W["wgt1"], W["wut1"], W["wd1"], rider=gather_rider(keys))
    put(keys, ro)
    keys = ("wout", "wat", "wbt", "wgt2")
    (un, za, zb, zg), ro = _inproj_fwd(h1, S["mix_norm"], W["wint"], S["b_in"], nseq, rider=gather_rider(keys))
    put(keys, ro)

    seq3 = lambda a: a.reshape(nseq, seq, a.shape[-1])
    zb3 = seq3(zb)
    pair_blk = lambda cb: pl.BlockSpec((1, 2, seq, 128), lambda n, r, cb=cb: (n, cb, 0, 0))
    a_cfg = []
    outs, lses = [], []
    for gi, (_, d) in enumerate(DIL_GROUPS):
        cfg = dict(grid=(nseq, d), seq=seq, stride=d, kvw=GW, split=True,
                   q_spec=pair_blk(gi), k_spec=pair_blk(3 + gi), v_spec=pair_blk(6 + gi), o_spec=pair_blk(0),
                   bias_map=lambda n, r: (0, 0, 0), sink_map=lambda n, r: (0, 0, 0), has_sink=False)
        a_cfg.append(cfg)
        (o, lse), _ = _attn_fwd(za, za, za, bias_all[4 * gi:4 * gi + 4], sink_0, o_shape=(nseq, 2, seq, 128),
                                o_dtype=F32, name=f"attn_a{gi}_fwd", **cfg)
        outs.append(o)
        lses.append(lse)
    wide_blk = lambda w, cmap: pl.BlockSpec((1, seq, w), cmap)
    b_cfg = dict(grid=(nseq, 2), seq=seq, stride=1, kvw=2 * HEAD_DIM, split=False,
                 q_spec=wide_blk(GW, lambda n, r: (n, 0, r)), k_spec=wide_blk(2 * HEAD_DIM, lambda n, r: (n, 0, 4)),
                 v_spec=wide_blk(2 * HEAD_DIM, lambda n, r: (n, 0, 5)), o_spec=wide_blk(GW, lambda n, r: (n, 0, r)),
                 bias_map=lambda n, r: (r, 0, 0), sink_map=lambda n, r: (r, 0, 0), has_sink=True)
    keys = ("wut2",)
    bias_b_fwd = bias_all[12:20].at[:, :, 0].set(jnp.broadcast_to(S["sinks"].reshape(8, 1), (8, BLOCK)))
    (yb, lse_b), ro = _attn_fwd(zb3, zb3, zb3, bias_b_fwd, sink_b, o_shape=(nseq, seq, 2 * GW), o_dtype=BF16,
                                name="attn_b_fwd", rider=gather_rider(keys), **b_cfg)
    put(keys, ro)
    yb = yb.reshape(t, 2 * GW)

    keys = ("wd2",)
    (h2, y, lse_tot, pa, pb, merged), ro = _merge_fwd(outs[0], outs[1], outs[2], lses[0], lses[1], lses[2], yb, zg, h1,
                                                      W["wat"], W["wbt"], W["wout"], rider=gather_rider(keys))
    put(keys, ro)
    (h3, n2, g2, u2), _ = _ffn_fwd(h2, S["ffn2_norm"], W["wgt2"], W["wut2"], W["wd2"])
    dh3, loss_part, g_final = _loss_head(h3, S["final_norm"].reshape(1, D_MODEL), tgt.reshape(t, D_MODEL))

    GS["final_norm"] = g_final
    dh2, dg2, du2, a2, df2, GS["ffn2_norm"] = _ffn_bwd(dh3, h2, S["ffn2_norm"], g2, u2, W["wgt2"], W["wut2"], W["wd2"])
    G["wgt2"] = _wgrad(dg2, n2, MXU_DIM, name="wgrad_gate2")
    G["wut2"] = _wgrad(du2, n2, MXU_DIM, name="wgrad_up2")
    G["wd2"] = _wgrad(a2, df2, MXU_DIM, name="wgrad_down2")

    keys = ("wgt2", "wut2", "wd2")
    rider = _ExchangeRider([pair(keys)]) if dist else None
    (dpa, dpb, dga, dgb, dya, dyb, dh2b, ca, cb), ro = _merge_bwd(dh2, pa, pb, zg, y, yb, W["wat"], W["wbt"], W["wout"],
                                                                  nseq, rider=rider)
    if dist:
        finish(keys, *ro)

    dqs, dks, dvs, dbs = [], [], [], []
    shp = (nseq, 2, seq, 128)
    halves = lambda a: [a[:, hf].reshape(t, 128).astype(BF16) for hf in range(2)]
    for gi in range(len(DIL_GROUPS)):
        dq, dk, dv, db, _ = _attn_bwd(za, za, za, bias_all[4 * gi:4 * gi + 4], sink_0, dya, ca, lse_tot,
                                      n_bias=4, dq_shape=shp, dkv_shape=shp, g_dtype=F32,
                                      kv_out_spec=a_cfg[gi]["o_spec"], name=f"attn_a{gi}_bwd", **a_cfg[gi])
        dqs += halves(dq)
        dks += halves(dk)
        dvs += halves(dv)
        dbs.append(db)
    dqb, dkb, dvb, dbb, dsink = _attn_bwd(zb3, zb3, zb3, bias_all[12:20], sink_b, seq3(dyb), seq3(cb), lse_b,
                                          n_bias=8, dq_shape=(nseq, seq, 2 * GW),
                                          dkv_shape=(nseq, seq, 2 * HEAD_DIM), g_dtype=BF16,
                                          kv_out_spec=wide_blk(2 * HEAD_DIM, lambda n, r: (n, 0, 0)),
                                          name="attn_b_bwd", **b_cfg)
    dz = jnp.concatenate(dqs + dks + dvs + [dqb.reshape(t, 2 * GW), dkb.reshape(t, 2 * HEAD_DIM),
                                            dvb.reshape(t, 2 * HEAD_DIM), dga, dgb], axis=-1)
    gb_tab = _bias_grad(jnp.concatenate(dbs + [dbb], axis=0), buckets)
    if dist:
        GS["bias_tab"], GS["sink_tiles"] = gb_tab, dsink
    else:
        GS["rel_bias"] = gb_tab[:, :20]
        GS["sinks"] = dsink[:, 0, 0].reshape(1, 8)

    G["wint"], GS["b_in"] = _wgrad(dz, un, MXU_DIM, with_colsum=True, name="wgrad_in")
    keys = ("wint",)
    rider = _ExchangeRider([pair(keys)]) if dist else None
    (dh1, GS["mix_norm"]), ro = _inproj_bwd(dz, dh2, h1, S["mix_norm"], W["wint"], rider=rider)
    if dist:
        finish(keys, *ro)

    dx, dg1, du1, a1, df1, GS["ffn1_norm"] = _ffn_bwd(dh1, xf, S["ffn1_norm"], g1, u1, W["wgt1"], W["wut1"], W["wd1"])
    G["wgt1"] = _wgrad(dg1, n1, MXU_DIM, name="wgrad_gate1")
    if dist:
        G["wut1"], ro = _wgrad(du1, n1, MXU_DIM, name="wgrad_up1", rider=_ExchangeRider([pair(("wgt1",))]))
        finish(("wgt1",), *ro)
        G["wd1"], ro = _wgrad(a1, df1, MXU_DIM, name="wgrad_down1", rider=_ExchangeRider([pair(("wut1",))]))
        finish(("wut1",), *ro)
        G["wout"], ro = _wgrad(merged, dh2b, MXU_DIM, name="wgrad_out", rider=_ExchangeRider([pair(("wd1",))]))
        finish(("wd1",), *ro)
    else:
        G["wut1"] = _wgrad(du1, n1, MXU_DIM, name="wgrad_up1")
        G["wd1"] = _wgrad(a1, df1, MXU_DIM, name="wgrad_down1")
        G["wout"] = _wgrad(merged, dh2b, MXU_DIM, name="wgrad_out")
    G["wat"] = _wgrad(dpa, y, MXU_DIM, name="wgrad_branch_a")
    G["wbt"] = _wgrad(dpb, yb, MXU_DIM, name="wgrad_branch_b")
    if dist:
        keys = ("wout", "wat", "wbt")
        finish(keys, *_chip_exchange([pair(keys)]))
    return loss_part, dx.reshape(x.shape), (reduced if dist else G), GS


_SMALL = ("ffn1_norm", "mix_norm", "ffn2_norm", "final_norm", "b_in", "sinks", "rel_bias")
_ORDER = ("ffn1_norm", "ffn1_w_gate", "ffn1_w_up", "ffn1_w_down", "mix_norm", "w_in", "b_in", "w_branch_a",
          "w_branch_b", "w_out", "sinks", "rel_bias", "ffn2_norm", "ffn2_w_gate", "ffn2_w_up", "ffn2_w_down",
          "final_norm")
_BIG = (("wgt1", "ffn1_w_gate", True, 704), ("wut1", "ffn1_w_up", True, 704), ("wd1", "ffn1_w_down", False, 704),
        ("wint", "w_in", True, 1280), ("wout", "w_out", False, 256), ("wat", "w_branch_a", True, 64),
        ("wbt", "w_branch_b", True, 128), ("wgt2", "ffn2_w_gate", True, 704), ("wut2", "ffn2_w_up", True, 704),
        ("wd2", "ffn2_w_down", False, 704))
_FULL_SHAPE = {"wat": (D_MODEL, GW), "wbt": (D_MODEL, 2 * GW)}


def kernel(x, ffn1_norm, ffn1_w_gate, ffn1_w_up, ffn1_w_down, mix_norm, w_in, b_in, w_branch_a, w_branch_b, w_out, sinks, rel_bias, ffn2_norm, ffn2_w_gate, ffn2_w_up, ffn2_w_down, final_norm, loss_target, m_ffn1_norm, m_ffn1_w_gate, m_ffn1_w_up, m_ffn1_w_down, m_mix_norm, m_w_in, m_b_in, m_w_branch_a, m_w_branch_b, m_w_out, m_sinks, m_rel_bias, m_ffn2_norm, m_ffn2_w_gate, m_ffn2_w_up, m_ffn2_w_down, m_final_norm, v_ffn1_norm, v_ffn1_w_gate, v_ffn1_w_up, v_ffn1_w_down, v_mix_norm, v_w_in, v_b_in, v_w_branch_a, v_w_branch_b, v_w_out, v_sinks, v_rel_bias, v_ffn2_norm, v_ffn2_w_gate, v_ffn2_w_up, v_ffn2_w_down, v_final_norm):
    args = dict(locals())
    w = {n: args[n] for n in _ORDER}
    m = {n: args["m_" + n] for n in _ORDER}
    v = {n: args["v_" + n] for n in _ORDER}

    shards = {}
    for key, name, transposed, rows in _BIG:
        a = w[name][0]
        a = (a.T if transposed else a).astype(BF16)
        shards[key] = a.reshape(rows, D_MODEL)
    S = {n: w[n] for n in _SMALL}

    loss_part, grad_x, reduced, GS = _local_step(x, loss_target, {}, S, shards)

    small = _allreduce_small(GS["ffn1_norm"], GS["mix_norm"], GS["ffn2_norm"], GS["final_norm"], GS["b_in"],
                             GS["sink_tiles"], GS["bias_tab"], loss_part)
    loss = small[9, 8]

    out_g, out_d, out_m, out_v = {}, {}, {}, {}
    for key, n, transposed, rows in _BIG:
        nat = w[n][0].shape
        if transposed and nat[1] % 128:
            res = _adamw(w[n][0].T, reduced[key], m[n][0].T, v[n][0].T, "adamw_" + n)
            res = [reduced[key].T] + [r.T for r in res]
        else:
            g = reduced[key].reshape(nat[1], nat[0]).T if transposed else reduced[key].reshape(nat)
            res = [g] + list(_adamw(w[n][0], g, m[n][0], v[n][0], "adamw_" + n))
        out_g[n], out_d[n], out_m[n], out_v[n] = [r[None] for r in res]
    row = lambda d: {n: (d[n].reshape(1, D_MODEL) if n == "final_norm" else d[n]) for n in _SMALL}
    for dst, src in zip((out_g, out_d, out_m, out_v), _adamw_small(small, row(w), row(m), row(v))):
        dst.update(src)
        dst["final_norm"] = src["final_norm"].reshape(D_MODEL)

    return (loss, grad_x, *[out_g[n] for n in _ORDER], *[out_d[n] for n in _ORDER],
            *[out_m[n] for n in _ORDER], *[out_v[n] for n in _ORDER])
```

```python
import math

import jax
import jax.numpy as jnp
from jax import lax
from jax.experimental import pallas as pl
from jax.experimental.pallas import tpu as pltpu

F32, BF16 = jnp.float32, jnp.bfloat16
MESH = pl.DeviceIdType.MESH

D_MODEL = 1024
D_FF = 2816
D_IN = 5120
HEAD_DIM = 64
BLOCK = 128
DIL_GROUPS = ((128, 1), (512, 4), (2048, 16))
B_WINDOW = 128
N_BUCKETS = 32
MAX_DISTANCE = 2048
EPS = 1e-6
N_CHIPS = 4
GW = 256
ZA_W = 2304
ZB_W = 768
NEG = -1e30

ADAM_LR, ADAM_B1, ADAM_B2, ADAM_EPS, ADAM_WD, ADAM_STEP = 0.001, 0.9, 0.999, 1e-08, 0.01, 10

VMEM_BIG = 56 * 1024 * 1024
TM = 512
TM_BWD = 256
MXU_DIM = 256
FF_BOUNDS = (0, 6 * MXU_DIM, D_FF)
DMA_SPLIT = 8
RESIDUES_PER_STEP = 8
ATTN_UNROLL = 5


def _dot(a, b):
    return jnp.dot(a, b, preferred_element_type=F32)


def _dot_nt(a, b):
    return lax.dot_general(a, b, (((1,), (1,)), ((), ())), preferred_element_type=F32)


def _dot_tn(a, b):
    return lax.dot_general(a, b, (((0,), (0,)), ((), ())), preferred_element_type=F32)


def _sigmoid(x):
    return 0.5 * jnp.tanh(0.5 * x) + 0.5


def _params(sem, vmem=None):
    return pltpu.CompilerParams(dimension_semantics=sem, vmem_limit_bytes=vmem)


ANY = pl.BlockSpec(memory_space=pl.ANY)


def _me():
    return lax.axis_index("x"), lax.axis_index("y"), lax.axis_index("c")


_CHIP_RELS = ((1, 0), (0, 1), (1, 1))


def _flip(v, f):
    return 1 - v if f else v


def _remote(src, dst, ssem, rsem, peer):
    return pltpu.make_async_remote_copy(src_ref=src, dst_ref=dst, send_sem=ssem, recv_sem=rsem,
                                        device_id=peer, device_id_type=MESH)


def _row_pieces(rows, n):
    step = max(16, -(-rows // n) // 16 * 16)
    out, s = [], 0
    while s < rows:
        out.append((s, min(step, rows - s)))
        s += step
    return out


def _gather_rows(shards):
    nt = len(shards)
    rows = [s.shape[0] for s in shards]

    def body(*refs):
        srcs, outs = refs[:nt], refs[nt:2 * nt]
        halves, quarters = refs[2 * nt:3 * nt], refs[3 * nt:4 * nt]
        ici_s, ici_r, fwd_s, fwd_r, d2d_s, d2d_r, keep, loc = refs[4 * nt:]
        x, y, c = _me()
        j = 2 * x + y
        sib = (x, y, 1 - c)
        nbr = ((1 - x, y, c), (x, 1 - y, c))
        nbr_j = (2 * (1 - x) + y, 2 * x + (1 - y))
        diag_j = 2 * (1 - x) + (1 - y)
        local = [pltpu.make_async_copy(srcs[t], outs[t].at[j], loc.at[t]) for t in range(nt)]
        for cp in local:
            cp.start()
        pending = []
        for a in range(2):
            for t in range(nt):
                half = pl.ds(c * (rows[t] // 2), rows[t] // 2)
                cp = _remote(srcs[t].at[half], halves[t].at[a], ici_s.at[2 * t + a], ici_r.at[2 * t + a], nbr[a])
                cp.start()
                pending.append(cp)
        placed = []

        def place(src, dst_of, idx):
            mine = pltpu.make_async_copy(src, dst_of, keep.at[idx])
            mine.start()
            cp = _remote(src, dst_of, d2d_s.at[idx], d2d_r.at[idx], sib)
            cp.start()
            placed.append((mine, cp))

        for a in range(2):
            for t in range(nt):
                r2, r4 = rows[t] // 2, rows[t] // 4
                got = halves[t].at[a]
                _remote(got, got, ici_s.at[2 * t + a], ici_r.at[2 * t + a], nbr[a]).wait_recv()
                cp = _remote(halves[t].at[a, pl.ds(a * r4, r4)], quarters[t].at[a], fwd_s.at[2 * t + a],
                             fwd_r.at[2 * t + a], nbr[1 - a])
                cp.start()
                pending.append(cp)
                place(got, outs[t].at[nbr_j[a], pl.ds(c * r2, r2)], 4 * t + a)
        for a in range(2):
            for t in range(nt):
                r2, r4 = rows[t] // 2, rows[t] // 4
                got = quarters[t].at[a]
                _remote(got, got, fwd_s.at[2 * t + a], fwd_r.at[2 * t + a], nbr[1 - a]).wait_recv()
                place(got, outs[t].at[diag_j, pl.ds(c * r2 + a * r4, r4)], 4 * t + 2 + a)
        for mine, cp in placed:
            mine.wait()
            cp.wait()
        for cp in pending:
            cp.wait_send()
        for cp in local:
            cp.wait()

    stage = ([pltpu.VMEM((2, r // 2, D_MODEL), BF16) for r in rows] + [pltpu.VMEM((2, r // 4, D_MODEL), BF16) for r in rows])
    sems = ([pltpu.SemaphoreType.DMA((2 * nt,)) for _ in range(4)] + [pltpu.SemaphoreType.DMA((4 * nt,))] * 3
            + [pltpu.SemaphoreType.DMA((nt,))])
    return pl.pallas_call(
        body, name="gather_weights",
        out_shape=tuple(jax.ShapeDtypeStruct((N_CHIPS,) + s.shape, s.dtype) for s in shards),
        in_specs=[pl.BlockSpec(memory_space=pltpu.VMEM)] * nt,
        out_specs=tuple([ANY] * nt), scratch_shapes=stage + sems,
    )(*shards)


VMEM_WHOLE = pl.BlockSpec(memory_space=pltpu.VMEM)


def _pair_reduce(grads, name):
    nt = len(grads)
    r2 = [g.shape[2] for g in grads]
    off = [sum(r2[:t]) for t in range(nt)]
    tot = sum(r2)

    def body(*refs):
        gs = refs[:nt]
        s_ref, got, ssem, rsem = refs[nt:]
        x, y, c = _me()
        sib = (x, y, 1 - c)
        for t in range(nt):
            for k in range(N_CHIPS):
                _remote(gs[t].at[k, 1 - c], got.at[k, pl.ds(off[t], r2[t])], ssem, rsem, sib).start()
        _remote(got, got, ssem, rsem, sib).wait()
        for t in range(nt):
            for k in range(N_CHIPS):
                rows = slice(off[t], off[t] + r2[t])
                s_ref[k, rows, :] = (gs[t][k, c].astype(F32) + got[k, rows, :].astype(F32)).astype(BF16)

    shp = jax.ShapeDtypeStruct((N_CHIPS, tot, D_MODEL), BF16)
    return pl.pallas_call(
        body, name=name, out_shape=shp, in_specs=[VMEM_WHOLE] * nt, out_specs=VMEM_WHOLE,
        scratch_shapes=[pltpu.VMEM((N_CHIPS, tot, D_MODEL), BF16), pltpu.SemaphoreType.DMA(()),
                        pltpu.SemaphoreType.DMA(())],
        compiler_params=pltpu.CompilerParams(vmem_limit_bytes=VMEM_BIG),
    )(*grads)


def _chip_exchange(parts):
    ng = len(parts)
    r2 = [p.shape[1] for p in parts]
    off = [sum(r2[:g]) for g in range(ng)]
    tot = sum(r2)

    def body(*refs):
        ps = refs[:ng]
        own_ref, rec_ref, ssems, rsems, lsem = refs[ng:]
        x, y, c = _me()
        j = 2 * x + y
        for g in range(ng):
            pltpu.make_async_copy(ps[g].at[j], own_ref.at[pl.ds(off[g], r2[g])], lsem).start()
        for k, (fx, fy) in enumerate(_CHIP_RELS):
            px, py = _flip(x, fx), _flip(y, fy)
            for g in range(ng):
                for st, sz in _row_pieces(r2[g], 2):
                    _remote(ps[g].at[2 * px + py, pl.ds(st, sz)], rec_ref.at[k, pl.ds(off[g] + st, sz)],
                            ssems.at[k], rsems.at[k], (px, py, c)).start()
        for k in range(3):
            _remote(rec_ref.at[k], rec_ref.at[k], ssems.at[k], rsems.at[k], (x, y, c)).wait()
        pltpu.make_async_copy(own_ref, own_ref, lsem).wait()

    return pl.pallas_call(
        body, name="grad_chip_exchange",
        out_shape=(jax.ShapeDtypeStruct((tot, D_MODEL), BF16), jax.ShapeDtypeStruct((3, tot, D_MODEL), BF16)),
        in_specs=[VMEM_WHOLE] * ng, out_specs=(ANY, ANY),
        scratch_shapes=[pltpu.SemaphoreType.DMA((3,)), pltpu.SemaphoreType.DMA((3,)), pltpu.SemaphoreType.DMA(())],
    )(*parts)


def _final_reduce(own, rec, name):
    r2 = own.shape[0]
    pieces = _row_pieces(r2, DMA_SPLIT)

    def body(own_ref, rec_ref, o_ref, fbuf, ssem, rsem, lsem):
        x, y, c = _me()
        sib = (x, y, 1 - c)
        for st, sz in pieces:
            rows = slice(st, st + sz)
            fbuf[rows, :] = (own_ref[rows, :].astype(F32) + rec_ref[0, rows, :].astype(F32)
                             + rec_ref[1, rows, :].astype(F32) + rec_ref[2, rows, :].astype(F32))
            pltpu.make_async_copy(fbuf.at[pl.ds(st, sz)], o_ref.at[c, pl.ds(st, sz)], lsem).start()
            _remote(fbuf.at[pl.ds(st, sz)], o_ref.at[c, pl.ds(st, sz)], ssem, rsem, sib).start()
        _remote(fbuf, o_ref.at[c], ssem, rsem, sib).wait()
        pltpu.make_async_copy(fbuf, o_ref.at[c], lsem).wait()

    return pl.pallas_call(
        body, name=name, out_shape=jax.ShapeDtypeStruct((2, r2, D_MODEL), F32),
        in_specs=[VMEM_WHOLE, VMEM_WHOLE], out_specs=ANY,
        scratch_shapes=[pltpu.VMEM((r2, D_MODEL), F32), pltpu.SemaphoreType.DMA(()), pltpu.SemaphoreType.DMA(()),
                        pltpu.SemaphoreType.DMA(())],
        compiler_params=pltpu.CompilerParams(vmem_limit_bytes=VMEM_BIG),
    )(own, rec)


SMALL_ROWS = 48


def _allreduce_small(g_ffn1, g_mix, g_ffn2, g_final, g_bin, dsink, bias_tab, loss_part):
    def body(f1_ref, mx_ref, f2_ref, fn_ref, bi_ref, sk_ref, bt_ref, ls_ref, o_ref, mine, buf, send_sems, recv_sems):
        x, y, c = _me()
        me = 4 * x + 2 * y + c
        mine[...] = jnp.zeros_like(mine)
        for r, ref in enumerate((f1_ref, mx_ref, f2_ref, fn_ref)):
            mine[r:r + 1, :] = ref[...]
        for k in range(D_IN // D_MODEL):
            mine[4 + k:5 + k, :] = bi_ref[:, k * D_MODEL:(k + 1) * D_MODEL]
        lane = lax.broadcasted_iota(jnp.int32, (1, 128), 1)
        row = jnp.where(lane == 8, ls_ref[0:1, :], 0.0)
        for h in range(8):
            row = jnp.where(lane == h, sk_ref[h, 0:1, :], row)
        mine[9:10, 0:128] = row
        mine[16:48, 0:128] = bt_ref[...]
        buf[me] = mine[...]
        copies = []
        for k in range(1, 8):
            peer = (_flip(x, (k >> 2) & 1), _flip(y, (k >> 1) & 1), _flip(c, k & 1))
            cp = _remote(mine, buf.at[me], send_sems.at[k - 1], recv_sems.at[k - 1], peer)
            cp.start()
            copies.append(cp)
        for cp in copies:
            cp.wait()
        acc = buf[0]
        for i in range(1, 8):
            acc = acc + buf[i]
        o_ref[...] = acc

    vm = pl.BlockSpec(memory_space=pltpu.VMEM)
    shape = (SMALL_ROWS, D_MODEL)
    return pl.pallas_call(
        body, name="allreduce_small", out_shape=jax.ShapeDtypeStruct(shape, F32),
        in_specs=[vm] * 8, out_specs=vm,
        scratch_shapes=[pltpu.VMEM(shape, F32), pltpu.VMEM((8,) + shape, F32), pltpu.SemaphoreType.DMA((7,)),
                        pltpu.SemaphoreType.DMA((7,))],
    )(g_ffn1, g_mix, g_ffn2, g_final, g_bin, dsink, bias_tab, loss_part)


def _adam_update(w, g, m, v):
    nm = ADAM_B1 * m + (1.0 - ADAM_B1) * g
    nv = ADAM_B2 * v + (1.0 - ADAM_B2) * (g * g)
    bc1 = 1.0 - ADAM_B1 ** ADAM_STEP
    bc2 = 1.0 - ADAM_B2 ** ADAM_STEP
    return -ADAM_LR * ((nm / bc1) / (jnp.sqrt(nv / bc2) + ADAM_EPS) + ADAM_WD * w), nm, nv


def _adamw_small(packed, w, m, v):
    names = ("ffn1_norm", "mix_norm", "ffn2_norm", "final_norm", "b_in", "sinks", "rel_bias")
    nn = len(names)

    def grad_of(p_ref, name, k=0):
        if name == "b_in":
            return p_ref[4 + k:5 + k, :]
        if name == "sinks":
            return p_ref[9:10, 0:8]
        if name == "rel_bias":
            return p_ref[16:48, 0:20]
        r = names.index(name)
        return p_ref[r:r + 1, :]

    def body(p_ref, *refs):
        ws, ms, vs = refs[:nn], refs[nn:2 * nn], refs[2 * nn:3 * nn]
        outs = refs[3 * nn:]
        for i, name in enumerate(names):
            og, od, om, ov = outs[i], outs[nn + i], outs[2 * nn + i], outs[3 * nn + i]
            pieces = range(D_IN // D_MODEL) if name == "b_in" else (0,)
            for k in pieces:
                sl = (slice(None), slice(k * D_MODEL, (k + 1) * D_MODEL)) if name == "b_in" else (Ellipsis,)
                g = grad_of(p_ref, name, k)
                d, nm, nv = _adam_update(ws[i][sl], g, ms[i][sl], vs[i][sl])
                og[sl], od[sl], om[sl], ov[sl] = g, d, nm, nv

    vm = pl.BlockSpec(memory_space=pltpu.VMEM)
    shapes = [jax.ShapeDtypeStruct(w[n].shape, F32) for n in names]
    res = pl.pallas_call(
        body, name="adamw_small", out_shape=tuple(shapes * 4), in_specs=[vm] * (1 + 3 * nn),
        out_specs=tuple([vm] * (4 * nn)),
    )(packed, *[w[n] for n in names], *[m[n] for n in names], *[v[n] for n in names])
    return [dict(zip(names, res[i * nn:(i + 1) * nn])) for i in range(4)]


class _GatherRider:
    def __init__(self, shards):
        self.inputs = list(shards)
        nt = len(shards)
        self.out_shape = [jax.ShapeDtypeStruct((N_CHIPS,) + s.shape, s.dtype) for s in shards]
        self.scratch = [pltpu.SemaphoreType.DMA((3 * nt,)), pltpu.SemaphoreType.DMA((3 * nt,)),
                        pltpu.SemaphoreType.DMA((nt,))]

    def _copies(self, srcs, outs, sems):
        ici_s, ici_r, loc = sems
        x, y, c = _me()
        j = 2 * x + y
        local = [pltpu.make_async_copy(srcs[t], outs[t].at[j], loc.at[t]) for t in range(len(srcs))]
        remote = []
        for k, (fx, fy) in enumerate(_CHIP_RELS):
            peer = (_flip(x, fx), _flip(y, fy), c)
            for t in range(len(srcs)):
                remote.append(_remote(srcs[t], outs[t].at[j], ici_s.at[3 * t + k], ici_r.at[3 * t + k], peer))
        return local, remote

    def start(self, srcs, outs, sems):
        local, remote = self._copies(srcs, outs, sems)
        for cp in local + remote:
            cp.start()

    def finish(self, srcs, outs, sems):
        local, remote = self._copies(srcs, outs, sems)
        for cp in remote + local:
            cp.wait()


class _ExchangeRider:
    def __init__(self, parts):
        self.inputs = list(parts)
        self.r2 = [p.shape[1] for p in parts]
        self.off = [sum(self.r2[:g]) for g in range(len(parts))]
        tot = sum(self.r2)
        self.out_shape = [jax.ShapeDtypeStruct((tot, D_MODEL), BF16), jax.ShapeDtypeStruct((3, tot, D_MODEL), BF16)]
        self.scratch = [pltpu.SemaphoreType.DMA((3,)), pltpu.SemaphoreType.DMA((3,)), pltpu.SemaphoreType.DMA(())]

    def start(self, ps, outs, sems):
        own_ref, rec_ref = outs
        ssems, rsems, lsem = sems
        x, y, c = _me()
        j = 2 * x + y
        for g in range(len(ps)):
            pltpu.make_async_copy(ps[g].at[j], own_ref.at[pl.ds(self.off[g], self.r2[g])], lsem).start()
        for k, (fx, fy) in enumerate(_CHIP_RELS):
            px, py = _flip(x, fx), _flip(y, fy)
            for g in range(len(ps)):
                for st, sz in _row_pieces(self.r2[g], 2):
                    _remote(ps[g].at[2 * px + py, pl.ds(st, sz)], rec_ref.at[k, pl.ds(self.off[g] + st, sz)],
                            ssems.at[k], rsems.at[k], (px, py, c)).start()

    def finish(self, ps, outs, sems):
        own_ref, rec_ref = outs
        ssems, rsems, lsem = sems
        x, y, c = _me()
        for k in range(3):
            _remote(rec_ref.at[k], rec_ref.at[k], ssems.at[k], rsems.at[k], (x, y, c)).wait()
        pltpu.make_async_copy(own_ref, own_ref, lsem).wait()


def _pallas(body, args, *, name, grid, in_specs, out_specs, out_shape, scratch_shapes=(), sem=None, vmem=None,
            rider=None):
    if rider is None:
        res = pl.pallas_call(body, name=name, grid=grid, in_specs=list(in_specs), out_specs=tuple(out_specs),
                             out_shape=tuple(out_shape), scratch_shapes=list(scratch_shapes),
                             compiler_params=_params(sem, vmem))(*args)
        return tuple(res), ()
    n_in, n_out, n_sc = len(in_specs), len(out_shape), len(scratch_shapes)
    r_in, r_out = len(rider.inputs), len(rider.out_shape)

    def wrapped(*refs):
        ins, rins = refs[:n_in], refs[n_in:n_in + r_in]
        p = n_in + r_in
        outs, routs = refs[p:p + n_out], refs[p + n_out:p + n_out + r_out]
        p += n_out + r_out
        scr, rsems = refs[p:p + n_sc], refs[p + n_sc:]
        first = pl.program_id(0) == 0
        last = pl.program_id(0) == grid[0] - 1
        for a in range(1, len(grid)):
            first = first & (pl.program_id(a) == 0)
            last = last & (pl.program_id(a) == grid[a] - 1)

        @pl.when(first)
        def _():
            rider.start(rins, routs, rsems)

        body(*ins, *outs, *scr)

        @pl.when(last)
        def _():
            rider.finish(rins, routs, rsems)

    res = pl.pallas_call(
        wrapped, name=name, grid=grid, in_specs=list(in_specs) + [ANY] * r_in,
        out_specs=tuple(out_specs) + (ANY,) * r_out, out_shape=tuple(out_shape) + tuple(rider.out_shape),
        scratch_shapes=list(scratch_shapes) + rider.scratch,
        compiler_params=_params(("arbitrary",) * len(grid), vmem))(*args, *rider.inputs)
    return tuple(res[:n_out]), tuple(res[n_out:])


def _ffn_fwd(h, gain, wgt, wut, wd, rider=None):
    t = h.shape[0]

    def body(h_ref, gain_ref, wg_hbm, wu_hbm, wd_hbm, hout_ref, n_ref, g_ref, u_ref, wg_v, wu_v, wd_v):
        @pl.when(pl.program_id(0) == 0)
        def _():
            pltpu.sync_copy(wg_hbm, wg_v)
            pltpu.sync_copy(wu_hbm, wu_v)
            pltpu.sync_copy(wd_hbm, wd_v)

        hh = h_ref[...]
        r = lax.rsqrt(jnp.mean(hh * hh, axis=-1, keepdims=True) + EPS)
        n = (hh * r * gain_ref[...]).astype(BF16)
        n_ref[...] = n
        acc = jnp.zeros((TM, D_MODEL), F32)
        for c0, c1 in zip(FF_BOUNDS[:-1], FF_BOUNDS[1:]):
            sl = slice(c0, c1)
            g = _dot_nt(n, wg_v[sl, :])
            u = _dot_nt(n, wu_v[sl, :])
            g_ref[:, sl] = g.astype(BF16)
            u_ref[:, sl] = u.astype(BF16)
            a = (g * _sigmoid(g) * u).astype(BF16)
            acc = acc + _dot(a, wd_v[sl, :])
        hout_ref[...] = hh + 0.5 * acc

    row = lambda w: pl.BlockSpec((TM, w), lambda i: (i, 0))
    wv = pltpu.VMEM((D_FF, D_MODEL), BF16)
    return _pallas(
        body, (h, gain, wgt, wut, wd), name="ffn_fwd", grid=(t // TM,),
        out_shape=(jax.ShapeDtypeStruct((t, D_MODEL), F32), jax.ShapeDtypeStruct((t, D_MODEL), BF16),
                   jax.ShapeDtypeStruct((t, D_FF), BF16), jax.ShapeDtypeStruct((t, D_FF), BF16)),
        in_specs=[row(D_MODEL), pl.BlockSpec((1, D_MODEL), lambda i: (0, 0)), ANY, ANY, ANY],
        out_specs=(row(D_MODEL), row(D_MODEL), row(D_FF), row(D_FF)),
        scratch_shapes=[wv, wv, wv], sem=("arbitrary",), vmem=VMEM_BIG, rider=rider)


def _ffn_bwd(dhout, h, gain, g, u, wgt, wut, wd):
    t = h.shape[0]
    tm = TM_BWD

    def body(dho_ref, h_ref, gain_ref, g_ref, u_ref, wg_hbm, wu_hbm, wd_hbm,
             dh_ref, dg_ref, du_ref, a_ref, df_ref, gg_ref, wg_v, wu_v, wd_v):
        @pl.when(pl.program_id(0) == 0)
        def _():
            pltpu.sync_copy(wg_hbm, wg_v)
            pltpu.sync_copy(wu_hbm, wu_v)
            pltpu.sync_copy(wd_hbm, wd_v)
            gg_ref[...] = jnp.zeros_like(gg_ref)

        dho = dho_ref[...]
        df = (0.5 * dho).astype(BF16)
        df_ref[...] = df
        dn = jnp.zeros((tm, D_MODEL), F32)
        for c0, c1 in zip(FF_BOUNDS[:-1], FF_BOUNDS[1:]):
            sl = slice(c0, c1)
            da = _dot_nt(df, wd_v[sl, :])
            gv = g_ref[:, sl].astype(F32)
            uv = u_ref[:, sl].astype(F32)
            sg = _sigmoid(gv)
            silu = gv * sg
            dg = (da * uv * (sg * (1.0 + gv * (1.0 - sg)))).astype(BF16)
            du = (da * silu).astype(BF16)
            dg_ref[:, sl] = dg
            du_ref[:, sl] = du
            a_ref[:, sl] = (silu * uv).astype(BF16)
            dn = dn + _dot(dg, wg_v[sl, :]) + _dot(du, wu_v[sl, :])
        hh = h_ref[...]
        r = lax.rsqrt(jnp.mean(hh * hh, axis=-1, keepdims=True) + EPS)
        hn = hh * r
        gg_ref[...] += jnp.sum(dn * hn, axis=0, keepdims=True)
        dng = dn * gain_ref[...]
        dh_ref[...] = dho + r * (dng - hn * jnp.mean(dng * hn, axis=-1, keepdims=True))

    row = lambda w: pl.BlockSpec((tm, w), lambda i: (i, 0))
    vec = pl.BlockSpec((1, D_MODEL), lambda i: (0, 0))
    wv = pltpu.VMEM((D_FF, D_MODEL), BF16)
    return pl.pallas_call(
        body, name="ffn_bwd", grid=(t // tm,),
        out_shape=(jax.ShapeDtypeStruct((t, D_MODEL), F32), jax.ShapeDtypeStruct((t, D_FF), BF16),
                   jax.ShapeDtypeStruct((t, D_FF), BF16), jax.ShapeDtypeStruct((t, D_FF), BF16),
                   jax.ShapeDtypeStruct((t, D_MODEL), BF16), jax.ShapeDtypeStruct((1, D_MODEL), F32)),
        in_specs=[row(D_MODEL), row(D_MODEL), vec, row(D_FF), row(D_FF), ANY, ANY, ANY],
        out_specs=(row(D_MODEL), row(D_FF), row(D_FF), row(D_FF), row(D_MODEL), vec),
        scratch_shapes=[wv, wv, wv],
        compiler_params=_params(("arbitrary",), VMEM_BIG),
    )(dhout, h, gain, g, u, wgt, wut, wd)


def _wgrad(lhs, rhs, rb, with_colsum=False, name="wgrad", rider=None):
    t, k = lhs.shape
    n = rhs.shape[1]

    def body(l_ref, r_ref, o_ref, *rest):
        o_ref[...] = _dot_tn(l_ref[...], r_ref[...]).astype(BF16)
        if with_colsum:
            rest[0][...] = jnp.sum(l_ref[...].astype(F32), axis=0, keepdims=True)

    out_shape = [jax.ShapeDtypeStruct((k, n), BF16)]
    out_specs = [pl.BlockSpec((rb, n), lambda j: (j, 0))]
    if with_colsum:
        out_shape.append(jax.ShapeDtypeStruct((1, k), F32))
        out_specs.append(pl.BlockSpec((1, rb), lambda j: (0, j)))
    res, ro = _pallas(
        body, (lhs, rhs), name=name, grid=(k // rb,), out_shape=tuple(out_shape),
        in_specs=[pl.BlockSpec((t, rb), lambda j: (0, j)), pl.BlockSpec((t, n), lambda j: (0, 0))],
        out_specs=tuple(out_specs), sem=("arbitrary",), vmem=VMEM_BIG, rider=rider)
    if rider is not None:
        return res[0], ro
    return res if with_colsum else res[0]


def _lane_blocks(nseq, seq, nblk, tm=TM):
    spt = seq // tm
    return pl.BlockSpec((1, nblk, tm, 128), lambda i: (i // spt, 0, i % spt, 0))


def _inproj_fwd(h, gain, wint, b_in, nseq, rider=None):
    t = h.shape[0]
    seq = t // nseq
    cut_a = 5 * MXU_DIM
    pieces = ((0, cut_a, 0, 0), (cut_a, ZA_W - cut_a, 0, cut_a), (ZA_W, ZB_W, 1, 0), (ZA_W + ZB_W, 1024, 2, 0),
              (ZA_W + ZB_W + 1024, 1024, 2, 1024))

    def body(h_ref, gain_ref, w_hbm, b_ref, u_ref, za_ref, zb_ref, zg_ref, w_v):
        @pl.when(pl.program_id(0) == 0)
        def _():
            pltpu.sync_copy(w_hbm, w_v)

        hh = h_ref[...]
        r = lax.rsqrt(jnp.mean(hh * hh, axis=-1, keepdims=True) + EPS)
        un = (hh * r * gain_ref[...]).astype(BF16)
        u_ref[...] = un
        outs = (None, zb_ref, zg_ref)
        for c0, cw, oi, o0 in pieces:
            val = _dot_nt(un, w_v[c0:c0 + cw, :]) + b_ref[:, c0:c0 + cw]
            if oi == 0:
                for cb in range(cw // 128):
                    za_ref[0, o0 // 128 + cb] = val[:, cb * 128:(cb + 1) * 128]
            else:
                outs[oi][:, o0:o0 + cw] = val.astype(BF16)

    row = lambda w: pl.BlockSpec((TM, w), lambda i: (i, 0))
    return _pallas(
        body, (h, gain, wint, b_in), name="inproj_fwd", grid=(t // TM,),
        out_shape=(jax.ShapeDtypeStruct((t, D_MODEL), BF16), jax.ShapeDtypeStruct((nseq, ZA_W // 128, seq, 128), F32),
                   jax.ShapeDtypeStruct((t, ZB_W), BF16), jax.ShapeDtypeStruct((t, 2 * D_MODEL), BF16)),
        in_specs=[row(D_MODEL), pl.BlockSpec((1, D_MODEL), lambda i: (0, 0)), ANY,
                  pl.BlockSpec((1, D_IN), lambda i: (0, 0))],
        out_specs=(row(D_MODEL), _lane_blocks(nseq, seq, ZA_W // 128), row(ZB_W), row(2 * D_MODEL)),
        scratch_shapes=[pltpu.VMEM((D_IN, D_MODEL), BF16)], sem=("arbitrary",), vmem=VMEM_BIG, rider=rider)


def _inproj_bwd(dz, dh2, h, gain, wint, rider=None):
    t = h.shape[0]
    nc = 5
    cw = D_IN // nc

    def body(dz_ref, dh2_ref, h_ref, gain_ref, w_hbm, dh_ref, gg_ref, w_v):
        @pl.when(pl.program_id(0) == 0)
        def _():
            pltpu.sync_copy(w_hbm, w_v)
            gg_ref[...] = jnp.zeros_like(gg_ref)

        du = jnp.zeros((TM, D_MODEL), F32)
        for ci in range(nc):
            sl = slice(ci * cw, (ci + 1) * cw)
            du = du + _dot(dz_ref[:, sl], w_v[sl, :])
        hh = h_ref[...]
        r = lax.rsqrt(jnp.mean(hh * hh, axis=-1, keepdims=True) + EPS)
        hn = hh * r
        gg_ref[...] += jnp.sum(du * hn, axis=0, keepdims=True)
        dng = du * gain_ref[...]
        dh_ref[...] = dh2_ref[...] + r * (dng - hn * jnp.mean(dng * hn, axis=-1, keepdims=True))

    row = lambda w: pl.BlockSpec((TM, w), lambda i: (i, 0))
    vec = pl.BlockSpec((1, D_MODEL), lambda i: (0, 0))
    return _pallas(
        body, (dz, dh2, h, gain, wint), name="inproj_bwd", grid=(t // TM,),
        out_shape=(jax.ShapeDtypeStruct((t, D_MODEL), F32), jax.ShapeDtypeStruct((1, D_MODEL), F32)),
        in_specs=[row(D_IN), row(D_MODEL), row(D_MODEL), vec, ANY],
        out_specs=(row(D_MODEL), vec),
        scratch_shapes=[pltpu.VMEM((D_IN, D_MODEL), BF16)], sem=("arbitrary",), vmem=VMEM_BIG, rider=rider)


def _head_sums(x):
    w = x.shape[1]
    i = lax.broadcasted_iota(jnp.int32, (w, w), 0) // HEAD_DIM
    j = lax.broadcasted_iota(jnp.int32, (w, w), 1) // HEAD_DIM
    ones = (i == j).astype(BF16)
    hi = x.astype(BF16)
    r1 = x - hi.astype(F32)
    mid = r1.astype(BF16)
    lo = (r1 - mid.astype(F32)).astype(BF16)
    return _dot(hi, ones) + _dot(mid, ones) + _dot(lo, ones)


def _merge_fwd(o0, o1, o2, l0, l1, l2, yb, zg, h1, wat, wbt, wout, rider=None):
    t = h1.shape[0]
    nseq, _, seq, _ = o0.shape

    def body(o0_ref, o1_ref, o2_ref, l0_ref, l1_ref, l2_ref, yb_ref, ga_ref, gb_ref, h1_ref, wa_ref, wb_ref, wo_ref,
             h2_ref, y_ref, lt_ref, pa_ref, pb_ref, mg_ref):
        wide = lambda ref: jnp.concatenate([ref[0, 0], ref[0, 1]], axis=1)
        la, lb, lc = wide(l0_ref), wide(l1_ref), wide(l2_ref)
        mx = jnp.maximum(jnp.maximum(la, lb), lc)
        ea, eb, ec = jnp.exp(la - mx), jnp.exp(lb - mx), jnp.exp(lc - mx)
        den = ea + eb + ec
        y = (ea * wide(o0_ref) + eb * wide(o1_ref) + ec * wide(o2_ref)) / den
        lt = mx + jnp.log(den)
        lt_ref[0, 0] = lt[:, :128]
        lt_ref[0, 1] = lt[:, 128:]
        yb16 = y.astype(BF16)
        y_ref[...] = yb16
        pa = _dot_nt(yb16, wa_ref[...])
        pb = _dot_nt(yb_ref[...], wb_ref[...])
        pa_ref[...] = pa.astype(BF16)
        pb_ref[...] = pb.astype(BF16)
        mg = (_sigmoid(ga_ref[...].astype(F32)) * pa + _sigmoid(gb_ref[...].astype(F32)) * pb).astype(BF16)
        mg_ref[...] = mg
        h2_ref[...] = h1_ref[...] + _dot(mg, wo_ref[...])

    row = lambda w: pl.BlockSpec((TM, w), lambda i: (i, 0))
    full = lambda a: pl.BlockSpec(a.shape, lambda i: (0, 0))
    gate = lambda cb: pl.BlockSpec((TM, D_MODEL), lambda i: (i, cb))
    return _pallas(
        body, (o0, o1, o2, l0, l1, l2, yb, zg, zg, h1, wat, wbt, wout), name="merge_fwd", grid=(t // TM,),
        out_shape=(jax.ShapeDtypeStruct((t, D_MODEL), F32), jax.ShapeDtypeStruct((t, GW), BF16),
                   jax.ShapeDtypeStruct((nseq, 2, seq, 128), F32), jax.ShapeDtypeStruct((t, D_MODEL), BF16),
                   jax.ShapeDtypeStruct((t, D_MODEL), BF16), jax.ShapeDtypeStruct((t, D_MODEL), BF16)),
        in_specs=[_lane_blocks(nseq, seq, 2)] * 6 + [row(2 * GW), gate(0), gate(1), row(D_MODEL), full(wat), full(wbt),
                                                     full(wout)],
        out_specs=(row(D_MODEL), row(GW), _lane_blocks(nseq, seq, 2), row(D_MODEL), row(D_MODEL), row(D_MODEL)),
        sem=("parallel",), vmem=VMEM_BIG, rider=rider)


def _merge_bwd(dh2, pa, pb, zg, y, yb, wat, wbt, wout, nseq, rider=None):
    t = dh2.shape[0]

    def body(dh2_ref, pa_ref, pb_ref, ga_ref, gb_ref, y_ref, yb_ref, wa_ref, wb_ref, wo_ref,
             dpa_ref, dpb_ref, dga_ref, dgb_ref, dya_ref, dyb_ref, dh2b_ref, ca_ref, cb_ref):
        d16 = dh2_ref[...].astype(BF16)
        dh2b_ref[...] = d16
        dm = _dot_nt(d16, wo_ref[...])
        sa = _sigmoid(ga_ref[...].astype(F32))
        sb = _sigmoid(gb_ref[...].astype(F32))
        dpa = (dm * sa).astype(BF16)
        dpb = (dm * sb).astype(BF16)
        dpa_ref[...] = dpa
        dpb_ref[...] = dpb
        dga_ref[...] = (dm * pa_ref[...].astype(F32) * sa * (1.0 - sa)).astype(BF16)
        dgb_ref[...] = (dm * pb_ref[...].astype(F32) * sb * (1.0 - sb)).astype(BF16)
        dya = _dot(dpa, wa_ref[...])
        dyb = _dot(dpb, wb_ref[...])
        dya_ref[0, 0] = dya[:, :128]
        dya_ref[0, 1] = dya[:, 128:]
        dyb_ref[...] = dyb.astype(BF16)
        ca = _head_sums(dya * y_ref[...].astype(F32))
        ca_ref[0, 0] = ca[:, :128]
        ca_ref[0, 1] = ca[:, 128:]
        cb_ref[...] = _head_sums(dyb * yb_ref[...].astype(F32))

    row = lambda w: pl.BlockSpec((TM, w), lambda i: (i, 0))
    full = lambda a: pl.BlockSpec(a.shape, lambda i: (0, 0))
    gate = lambda cb: pl.BlockSpec((TM, D_MODEL), lambda i: (i, cb))
    bf = lambda w: jax.ShapeDtypeStruct((t, w), BF16)
    lanes = jax.ShapeDtypeStruct((nseq, 2, t // nseq, 128), F32)
    lane_spec = _lane_blocks(nseq, t // nseq, 2)
    return _pallas(
        body, (dh2, pa, pb, zg, zg, y, yb, wat, wbt, wout), name="merge_bwd", grid=(t // TM,),
        out_shape=(bf(D_MODEL), bf(D_MODEL), bf(D_MODEL), bf(D_MODEL), lanes, bf(2 * GW), bf(D_MODEL),
                   lanes, jax.ShapeDtypeStruct((t, 2 * GW), F32)),
        in_specs=[row(D_MODEL), row(D_MODEL), row(D_MODEL), gate(0), gate(1), row(GW), row(2 * GW),
                  full(wat), full(wbt), full(wout)],
        out_specs=(row(D_MODEL), row(D_MODEL), row(D_MODEL), row(D_MODEL), lane_spec, row(2 * GW), row(D_MODEL),
                   lane_spec, row(2 * GW)),
        sem=("parallel",), vmem=VMEM_BIG, rider=rider)


def _loss_head(h3, gain, tgt):
    t = h3.shape[0]

    def body(h_ref, gain_ref, t_ref, dh_ref, loss_ref, gg_ref):
        @pl.when(pl.program_id(0) == 0)
        def _():
            loss_ref[...] = jnp.zeros_like(loss_ref)
            gg_ref[...] = jnp.zeros_like(gg_ref)

        hh = h_ref[...]
        r = lax.rsqrt(jnp.mean(hh * hh, axis=-1, keepdims=True) + EPS)
        hn = hh * r
        err = hn * gain_ref[...] - t_ref[...]
        part = jnp.sum(jnp.sum(err * err, axis=1, keepdims=True), axis=0, keepdims=True)
        loss_ref[...] += (0.5 / D_MODEL) * part
        dy = err * (1.0 / D_MODEL)
        gg_ref[...] += jnp.sum(dy * hn, axis=0, keepdims=True)
        dng = dy * gain_ref[...]
        dh_ref[...] = r * (dng - hn * jnp.mean(dng * hn, axis=-1, keepdims=True))

    row = pl.BlockSpec((TM, D_MODEL), lambda i: (i, 0))
    vec = pl.BlockSpec((1, D_MODEL), lambda i: (0, 0))
    return pl.pallas_call(
        body, name="loss_head", grid=(t // TM,),
        out_shape=(jax.ShapeDtypeStruct((t, D_MODEL), F32), jax.ShapeDtypeStruct((8, 128), F32),
                   jax.ShapeDtypeStruct((1, D_MODEL), F32)),
        in_specs=[row, vec, row], out_specs=(row, pl.BlockSpec((8, 128), lambda i: (0, 0)), vec),
        compiler_params=_params(("arbitrary",)),
    )(h3, gain, tgt)


def _lane_head(rows):
    return lax.broadcasted_iota(jnp.int32, (rows, GW), 1) // HEAD_DIM


def _kv_expand_matrix(r):
    ci = lax.broadcasted_iota(jnp.int32, (2 * HEAD_DIM, GW), 0)
    ji = lax.broadcasted_iota(jnp.int32, (2 * HEAD_DIM, GW), 1)
    return (ci == (ji % HEAD_DIM) + HEAD_DIM * r).astype(BF16)


def _block_rows(row0, stride, ib):
    start = row0 + (stride * BLOCK) * ib
    if stride > 1:
        return pl.ds(start, BLOCK, stride=stride)
    return pl.ds(pl.multiple_of(start, BLOCK), BLOCK)


def _stack_heads(x, lane_head):
    return jnp.concatenate([jnp.where(lane_head == h, x, jnp.zeros_like(x)) for h in range(4)], axis=0)


def _unstack_heads(x4, lane_head):
    out = jnp.zeros((BLOCK, GW), F32)
    for h in range(4):
        out = jnp.where(lane_head == h, x4[h * BLOCK:(h + 1) * BLOCK], out)
    return out


def _load_rows(ref, rows, split):
    if split:
        return jnp.concatenate([ref[0, 0, rows, :], ref[0, 1, rows, :]], axis=1)
    return ref[0, rows, :]


def _store_rows(ref, rows, val, split):
    if split:
        ref[0, 0, rows, :] = val[:, :128]
        ref[0, 1, rows, :] = val[:, 128:]
    else:
        ref[0, rows, :] = val


def _attn_fwd(q_arr, k_arr, v_arr, bias, sink, *, grid, seq, stride, kvw, split, q_spec, k_spec, v_spec, bias_map,
              sink_map, o_spec, has_sink, o_shape, o_dtype, name, rider=None):
    nb = seq // stride // BLOCK
    scale = HEAD_DIM ** -0.5
    expanded = kvw != GW
    rps = min(stride, RESIDUES_PER_STEP)
    grid = (grid[0], grid[1] // rps)
    assert not has_sink or B_WINDOW - 1 < BLOCK

    def body(q_ref, k_ref, v_ref, bias_ref, sink_ref, o_ref, lse_ref, *kv_x):
        rr = pl.program_id(1)
        lane_head = _lane_head(BLOCK)
        if expanded:
            expand = _kv_expand_matrix(rr)
            kv_x[0][...] = _dot(k_ref[0], expand).astype(BF16)
            kv_x[1][...] = _dot(v_ref[0], expand).astype(BF16)
        for j in range(rps):
            residue(rr * rps + j if stride > 1 else 0, q_ref, k_ref, v_ref, bias_ref, sink_ref, o_ref, lse_ref, kv_x,
                    lane_head)

    def residue(row0, q_ref, k_ref, v_ref, bias_ref, sink_ref, o_ref, lse_ref, kv_x, lane_head):
        def per_head(fn, x):
            return jnp.concatenate([fn(sink_ref[0, h:h + 1, 0:1], x[h * BLOCK:(h + 1) * BLOCK]) for h in range(4)],
                                   axis=0)

        def load(ref, ib):
            return _load_rows(ref, _block_rows(row0, stride, ib), split).astype(BF16)

        def load_kv(which, ib):
            if expanded:
                return kv_x[which][_block_rows(0, 1, ib), :]
            return load((k_ref, v_ref)[which], ib)

        def block(ib, first):
            q4 = _stack_heads(load(q_ref, ib), lane_head)
            if first:
                kc, vc = load_kv(0, ib), load_kv(1, ib)
                b4 = bias_ref[:, :, BLOCK:].reshape(4 * BLOCK, BLOCK)
            else:
                kc = jnp.concatenate([load_kv(0, ib - 1), load_kv(0, ib)], axis=0)
                vc = jnp.concatenate([load_kv(1, ib - 1), load_kv(1, ib)], axis=0)
                b4 = bias_ref[...].reshape(4 * BLOCK, 2 * BLOCK)
                if has_sink:
                    oldest = lax.broadcasted_iota(jnp.int32, kc.shape, 0) == 0
                    kc = jnp.where(oldest, jnp.zeros_like(kc), kc)
                    vc = jnp.where(oldest, jnp.zeros_like(vc), vc)
            s = _dot_nt(q4, kc) * scale + b4
            m = jnp.max(s, axis=-1, keepdims=True)
            if has_sink and first:
                m = per_head(jnp.maximum, m)
            p = jnp.exp(s - m)
            l = jnp.sum(p, axis=-1, keepdims=True)
            if has_sink and first:
                l = l + per_head(lambda sk, mh: jnp.exp(sk - mh), m)
            o4 = _dot(p.astype(BF16), vc) / l
            rows = _block_rows(row0, stride, ib)
            _store_rows(o_ref, rows, _unstack_heads(o4, lane_head).astype(o_dtype), split)
            _store_rows(lse_ref, rows, _unstack_heads(m + jnp.log(l), lane_head), split)

        block(0, True)
        if nb > 1:
            def step(i, carry):
                block(i, False)
                return carry
            lax.fori_loop(1, nb, step, 0, unroll=min(ATTN_UNROLL, nb - 1))

    return _pallas(
        body, (q_arr, k_arr, v_arr, bias, sink), name=name, grid=grid,
        out_shape=(jax.ShapeDtypeStruct(o_shape, o_dtype), jax.ShapeDtypeStruct(o_shape, F32)),
        in_specs=[q_spec, k_spec, v_spec,
                  pl.BlockSpec((4, BLOCK, 2 * BLOCK), bias_map), pl.BlockSpec((1, 4, 128), sink_map)],
        out_specs=(o_spec, o_spec),
        scratch_shapes=[pltpu.VMEM((seq, GW), BF16)] * 2 if expanded else [],
        sem=("arbitrary", "arbitrary"), vmem=VMEM_BIG, rider=rider)


def _attn_bwd(q_arr, k_arr, v_arr, bias, sink, dy, cc, lse, *, grid, seq, stride, kvw, split, q_spec, k_spec, v_spec,
              bias_map, sink_map, o_spec, kv_out_spec, has_sink, n_bias, dq_shape, dkv_shape, g_dtype, name):
    ln = seq // stride
    nb = ln // BLOCK
    scale = HEAD_DIM ** -0.5
    expanded = kvw != GW
    rps = min(stride, RESIDUES_PER_STEP)
    grid = (grid[0], grid[1] // rps)

    def body(q_ref, k_ref, v_ref, bias_ref, sink_ref, dy_ref, c_ref, lse_ref,
             dq_ref, dk_ref, dv_ref, db_ref, dsk_ref, dk_acc, dv_acc, dk_half, dv_half, *kv_x):
        rr = pl.program_id(1)

        @pl.when((pl.program_id(0) == 0) & (rr == 0))
        def _():
            db_ref[...] = jnp.zeros_like(db_ref)
            dsk_ref[...] = jnp.zeros_like(dsk_ref)

        if expanded:
            expand = _kv_expand_matrix(rr)
            kv_x[0][...] = _dot(k_ref[0], expand).astype(BF16)
            kv_x[1][...] = _dot(v_ref[0], expand).astype(BF16)
        refs = (q_ref, k_ref, v_ref, bias_ref, sink_ref, dy_ref, c_ref, lse_ref, dq_ref, dk_ref, dv_ref, db_ref,
                dsk_ref, dk_acc, dv_acc, dk_half, dv_half, kv_x)
        for j in range(rps):
            residue(rr, rr * rps + j if stride > 1 else 0, *refs)

    def residue(rr, row0, q_ref, k_ref, v_ref, bias_ref, sink_ref, dy_ref, c_ref, lse_ref,
                dq_ref, dk_ref, dv_ref, db_ref, dsk_ref, dk_acc, dv_acc, dk_half, dv_half, kv_x):
        dk_acc[...] = jnp.zeros_like(dk_acc)
        dv_acc[...] = jnp.zeros_like(dv_acc)
        lane_head = _lane_head(BLOCK)
        hb = 4 * rr if n_bias == 8 else 0

        def load(ref, ib):
            return _load_rows(ref, _block_rows(row0, stride, ib), split)

        def load_kv(which, ib):
            if expanded:
                return kv_x[which][_block_rows(0, 1, ib), :]
            return load((k_ref, v_ref)[which], ib).astype(BF16)

        def head_col(x):
            return jnp.concatenate([x[:, h * HEAD_DIM:h * HEAD_DIM + 1] for h in range(4)], axis=0)

        def block(ib, first):
            q4 = _stack_heads(load(q_ref, ib).astype(BF16), lane_head)
            dy4 = _stack_heads(load(dy_ref, ib).astype(BF16), lane_head)
            c4 = head_col(load(c_ref, ib))
            l4 = head_col(load(lse_ref, ib))
            if first:
                kc, vc = load_kv(0, ib), load_kv(1, ib)
                b4 = bias_ref[:, :, BLOCK:].reshape(4 * BLOCK, BLOCK)
                krows = pl.ds(0, BLOCK)
            else:
                kc = jnp.concatenate([load_kv(0, ib - 1), load_kv(0, ib)], axis=0)
                vc = jnp.concatenate([load_kv(1, ib - 1), load_kv(1, ib)], axis=0)
                b4 = bias_ref[...].reshape(4 * BLOCK, 2 * BLOCK)
                krows = pl.ds(pl.multiple_of((ib - 1) * BLOCK, BLOCK), 2 * BLOCK)
            nk = BLOCK if first else 2 * BLOCK
            p = jnp.exp(_dot_nt(q4, kc) * scale + b4 - l4)
            ds = p * (_dot_nt(dy4, vc) - c4)
            ds3 = ds.reshape(4, BLOCK, nk)
            if n_bias == 8:
                if first:
                    db_ref[pl.ds(hb, 4), :, BLOCK:] += ds3
                else:
                    db_ref[pl.ds(hb, 4)] += ds3
            elif first:
                db_ref[:, :, BLOCK:] += ds3
            else:
                db_ref[...] += ds3
            ds16 = ds.astype(BF16)
            dq = _unstack_heads(_dot(ds16, kc), lane_head) * scale
            _store_rows(dq_ref, _block_rows(row0, stride, ib), dq.astype(g_dtype), split)
            dk_acc[krows, :] += _dot_tn(ds16, q4) * scale
            dv_acc[krows, :] += _dot_tn(p.astype(BF16), dy4)
            if has_sink:
                for h in range(4):
                    hs = slice(h * BLOCK, (h + 1) * BLOCK)
                    sk = sink_ref[0, h:h + 1, 0:1]
                    val = -jnp.sum(jnp.exp(sk - l4[hs]) * c4[hs], axis=0, keepdims=True)
                    dsk_ref[hb + h] += jnp.broadcast_to(val, (8, 128))

        block(0, True)
        if nb > 1:
            def step(i, carry):
                block(i, False)
                return carry
            lax.fori_loop(1, nb, step, 0, unroll=min(ATTN_UNROLL, nb - 1))

        if kvw == GW:
            all_rows = pl.ds(row0, ln, stride=stride) if stride > 1 else pl.ds(0, ln)
            _store_rows(dk_ref, all_rows, dk_acc[...].astype(g_dtype), split)
            _store_rows(dv_ref, all_rows, dv_acc[...].astype(g_dtype), split)
        else:
            def fold(acc):
                t2 = acc[:, :2 * HEAD_DIM] + acc[:, 2 * HEAD_DIM:]
                t2 = t2 + pltpu.roll(t2, HEAD_DIM, 1)
                lane = lax.broadcasted_iota(jnp.int32, t2.shape, 1) // HEAD_DIM
                return jnp.where(lane == rr, t2, 0.0)

            @pl.when(rr == 0)
            def _():
                dk_half[...] = fold(dk_acc[...])
                dv_half[...] = fold(dv_acc[...])

            @pl.when(rr == 1)
            def _():
                dk_ref[0] = (dk_half[...] + fold(dk_acc[...])).astype(g_dtype)
                dv_ref[0] = (dv_half[...] + fold(dv_acc[...])).astype(g_dtype)

    return pl.pallas_call(
        body, name=name, grid=grid,
        out_shape=(jax.ShapeDtypeStruct(dq_shape, g_dtype), jax.ShapeDtypeStruct(dkv_shape, g_dtype),
                   jax.ShapeDtypeStruct(dkv_shape, g_dtype), jax.ShapeDtypeStruct((n_bias, BLOCK, 2 * BLOCK), F32),
                   jax.ShapeDtypeStruct((8, 8, 128), F32)),
        in_specs=[q_spec, k_spec, v_spec,
                  pl.BlockSpec((4, BLOCK, 2 * BLOCK), bias_map), pl.BlockSpec((1, 4, 128), sink_map),
                  o_spec, o_spec, o_spec],
        out_specs=(o_spec, kv_out_spec, kv_out_spec,
                   pl.BlockSpec((n_bias, BLOCK, 2 * BLOCK), lambda n, r: (0, 0, 0)),
                   pl.BlockSpec((8, 8, 128), lambda n, r: (0, 0, 0))),
        scratch_shapes=[pltpu.VMEM((ln, GW), F32), pltpu.VMEM((ln, GW), F32),
                        pltpu.VMEM((ln, 2 * HEAD_DIM), F32), pltpu.VMEM((ln, 2 * HEAD_DIM), F32)]
        + ([pltpu.VMEM((seq, GW), BF16)] * 2 if expanded else []),
        compiler_params=_params(("arbitrary", "arbitrary"), VMEM_BIG),
    )(q_arr, k_arr, v_arr, bias, sink, dy, cc, lse)


def _bias_grad(ds_all, buckets):
    def body(ds_ref, bk_ref, o_ref):
        rows = lax.broadcasted_iota(jnp.int32, (N_BUCKETS, 128), 0)
        cols = lax.broadcasted_iota(jnp.int32, (N_BUCKETS, 128), 1)

        def per_bucket(b, acc):
            for h in range(20):
                gi = h // 4 if h < 12 else 3
                v = jnp.where(bk_ref[gi] == b, ds_ref[h], 0.0)
                v = jnp.sum(jnp.sum(v, axis=1, keepdims=True), axis=0, keepdims=True)
                acc = jnp.where((rows == b) & (cols == h), v, acc)
            return acc

        o_ref[...] = lax.fori_loop(0, N_BUCKETS, per_bucket, jnp.zeros((N_BUCKETS, 128), F32))

    vm = pl.BlockSpec(memory_space=pltpu.VMEM)
    return pl.pallas_call(body, name="bias_grad", out_shape=jax.ShapeDtypeStruct((N_BUCKETS, 128), F32),
                          in_specs=[vm, vm], out_specs=vm)(ds_all, buckets)


def _adamw(w, g, m, v, name):
    r, c = w.shape
    tr = r
    for cand in (256, 176, 128, 64, 32, 16, 8):
        if r % cand == 0:
            tr = cand
            break
    bc1 = 1.0 - ADAM_B1 ** ADAM_STEP
    bc2 = 1.0 - ADAM_B2 ** ADAM_STEP

    def body(w_ref, g_ref, m_ref, v_ref, d_ref, nm_ref, nv_ref):
        gv = g_ref[...]
        nm = ADAM_B1 * m_ref[...] + (1.0 - ADAM_B1) * gv
        nv = ADAM_B2 * v_ref[...] + (1.0 - ADAM_B2) * (gv * gv)
        nm_ref[...] = nm
        nv_ref[...] = nv
        d_ref[...] = -ADAM_LR * ((nm / bc1) / (jnp.sqrt(nv / bc2) + ADAM_EPS) + ADAM_WD * w_ref[...])

    spec = pl.BlockSpec((tr, c), lambda i: (i, 0))
    shp = jax.ShapeDtypeStruct((r, c), F32)
    return pl.pallas_call(body, name=name, grid=(r // tr,), out_shape=(shp, shp, shp),
                          in_specs=[spec] * 4, out_specs=(spec, spec, spec),
                          compiler_params=_params(("parallel",)))(w, g, m, v)


def _t5_bucket(dist):
    max_exact = N_BUCKETS // 2
    n = jnp.maximum(dist, 0)
    nf = jnp.maximum(n, 1).astype(F32)
    large = max_exact + (jnp.log(nf / max_exact) / math.log(MAX_DISTANCE / max_exact)
                         * (N_BUCKETS - max_exact)).astype(jnp.int32)
    large = jnp.minimum(large, N_BUCKETS - 1)
    return jnp.where(n < max_exact, n, large)


def _bias_tables(rel_bias):
    qi = jnp.arange(BLOCK)[:, None]
    ki = jnp.arange(2 * BLOCK)[None, :]
    dist = qi + BLOCK - ki
    specs = [(d, w // d, 4 * gi, 4 * gi + 4) for gi, (w, d) in enumerate(DIL_GROUPS)] + [(1, B_WINDOW - 1, 12, 20)]
    biases, buckets = [], []
    for stride, steps, h0, h1 in specs:
        valid = (dist >= 0) & (dist <= steps)
        bk = jnp.where(valid, _t5_bucket(dist * stride), -1).astype(jnp.int32)
        onehot = (bk[None, :, :] == jnp.arange(N_BUCKETS, dtype=jnp.int32)[:, None, None]).astype(F32)
        b = jnp.einsum("bqk,bh->hqk", onehot, rel_bias[:, h0:h1], precision=lax.Precision.HIGHEST)
        biases.append(jnp.where(valid[None], b, NEG))
        buckets.append(bk)
    return jnp.concatenate(biases, axis=0), jnp.stack(buckets, axis=0)


def _local_step(x, tgt, W, S, shards=None):
    nseq, seq, _ = x.shape
    t = nseq * seq
    xf = x.reshape(t, D_MODEL)
    bias_all, buckets = _bias_tables(S["rel_bias"])
    sink_b = jnp.broadcast_to(S["sinks"].reshape(2, 4, 1), (2, 4, 128)).astype(F32)
    sink_0 = jnp.zeros((1, 4, 128), F32)
    dist = shards is not None
    W = dict(W)
    G, GS, reduced = {}, {}, {}

    def put(keys, gathered):
        for k, g in zip(keys, gathered):
            W[k] = g.reshape(_FULL_SHAPE.get(k, (N_CHIPS * shards[k].shape[0], D_MODEL)))

    def gather_rider(keys):
        return _GatherRider([shards[k] for k in keys]) if dist else None

    def pair(keys):
        return _pair_reduce([G[k].reshape(N_CHIPS, 2, shards[k].shape[0] // 2, D_MODEL) for k in keys],
                            "grad_pair_reduce_" + keys[0])

    def finish(keys, own, rec):
        full = _final_reduce(own, rec, "grad_final_reduce_" + keys[0])
        off = 0
        for k in keys:
            r = shards[k].shape[0]
            reduced[k] = full[:, off:off + r // 2].reshape(r, D_MODEL)
            off += r // 2

    if dist:
        first = ("wgt1", "wut1", "wd1")
        put(first, _gather_rows([shards[k] for k in first]))
    keys = ("wint",)
    (h1, n1, g1, u1), ro = _ffn_fwd(xf, S["ffn1_norm"], W["wgt1"], W["wut1"], W["wd1"], rider=gather_rider(keys))
    put(keys, ro)
    keys = ("wout", "wat", "wbt", "wgt2")
    (un, za, zb, zg), ro = _inproj_fwd(h1, S["mix_norm"], W["wint"], S["b_in"], nseq, rider=gather_rider(keys))
    put(keys, ro)

    seq3 = lambda a: a.reshape(nseq, seq, a.shape[-1])
    zb3 = seq3(zb)
    pair_blk = lambda cb: pl.BlockSpec((1, 2, seq, 128), lambda n, r, cb=cb: (n, cb, 0, 0))
    a_cfg = []
    outs, lses = [], []
    for gi, (_, d) in enumerate(DIL_GROUPS):
        cfg = dict(grid=(nseq, d), seq=seq, stride=d, kvw=GW, split=True,
                   q_spec=pair_blk(gi), k_spec=pair_blk(3 + gi), v_spec=pair_blk(6 + gi), o_spec=pair_blk(0),
                   bias_map=lambda n, r: (0, 0, 0), sink_map=lambda n, r: (0, 0, 0), has_sink=False)
        a_cfg.append(cfg)
        (o, lse), _ = _attn_fwd(za, za, za, bias_all[4 * gi:4 * gi + 4], sink_0, o_shape=(nseq, 2, seq, 128),
                                o_dtype=F32, name=f"attn_a{gi}_fwd", **cfg)
        outs.append(o)
        lses.append(lse)
    wide_blk = lambda w, cmap: pl.BlockSpec((1, seq, w), cmap)
    b_cfg = dict(grid=(nseq, 2), seq=seq, stride=1, kvw=2 * HEAD_DIM, split=False,
                 q_spec=wide_blk(GW, lambda n, r: (n, 0, r)), k_spec=wide_blk(2 * HEAD_DIM, lambda n, r: (n, 0, 4)),
                 v_spec=wide_blk(2 * HEAD_DIM, lambda n, r: (n, 0, 5)), o_spec=wide_blk(GW, lambda n, r: (n, 0, r)),
                 bias_map=lambda n, r: (r, 0, 0), sink_map=lambda n, r: (r, 0, 0), has_sink=True)
    keys = ("wut2",)
    bias_b_fwd = bias_all[12:20].at[:, :, 0].set(jnp.broadcast_to(S["sinks"].reshape(8, 1), (8, BLOCK)))
    (yb, lse_b), ro = _attn_fwd(zb3, zb3, zb3, bias_b_fwd, sink_b, o_shape=(nseq, seq, 2 * GW), o_dtype=BF16,
                                name="attn_b_fwd", rider=gather_rider(keys), **b_cfg)
    put(keys, ro)
    yb = yb.reshape(t, 2 * GW)

    keys = ("wd2",)
    (h2, y, lse_tot, pa, pb, merged), ro = _merge_fwd(outs[0], outs[1], outs[2], lses[0], lses[1], lses[2], yb, zg, h1,
                                                      W["wat"], W["wbt"], W["wout"], rider=gather_rider(keys))
    put(keys, ro)
    (h3, n2, g2, u2), _ = _ffn_fwd(h2, S["ffn2_norm"], W["wgt2"], W["wut2"], W["wd2"])
    dh3, loss_part, g_final = _loss_head(h3, S["final_norm"].reshape(1, D_MODEL), tgt.reshape(t, D_MODEL))

    GS["final_norm"] = g_final
    dh2, dg2, du2, a2, df2, GS["ffn2_norm"] = _ffn_bwd(dh3, h2, S["ffn2_norm"], g2, u2, W["wgt2"], W["wut2"], W["wd2"])
    G["wgt2"] = _wgrad(dg2, n2, MXU_DIM, name="wgrad_gate2")
    G["wut2"] = _wgrad(du2, n2, MXU_DIM, name="wgrad_up2")
    G["wd2"] = _wgrad(a2, df2, MXU_DIM, name="wgrad_down2")

    keys = ("wgt2", "wut2", "wd2")
    rider = _ExchangeRider([pair(keys)]) if dist else None
    (dpa, dpb, dga, dgb, dya, dyb, dh2b, ca, cb), ro = _merge_bwd(dh2, pa, pb, zg, y, yb, W["wat"], W["wbt"], W["wout"],
                                                                  nseq, rider=rider)
    if dist:
        finish(keys, *ro)

    dqs, dks, dvs, dbs = [], [], [], []
    shp = (nseq, 2, seq, 128)
    halves = lambda a: [a[:, hf].reshape(t, 128).astype(BF16) for hf in range(2)]
    for gi in range(len(DIL_GROUPS)):
        dq, dk, dv, db, _ = _attn_bwd(za, za, za, bias_all[4 * gi:4 * gi + 4], sink_0, dya, ca, lse_tot,
                                      n_bias=4, dq_shape=shp, dkv_shape=shp, g_dtype=F32,
                                      kv_out_spec=a_cfg[gi]["o_spec"], name=f"attn_a{gi}_bwd", **a_cfg[gi])
        dqs += halves(dq)
        dks += halves(dk)
        dvs += halves(dv)
        dbs.append(db)
    dqb, dkb, dvb, dbb, dsink = _attn_bwd(zb3, zb3, zb3, bias_all[12:20], sink_b, seq3(dyb), seq3(cb), lse_b,
                                          n_bias=8, dq_shape=(nseq, seq, 2 * GW),
                                          dkv_shape=(nseq, seq, 2 * HEAD_DIM), g_dtype=BF16,
                                          kv_out_spec=wide_blk(2 * HEAD_DIM, lambda n, r: (n, 0, 0)),
                                          name="attn_b_bwd", **b_cfg)
    dz = jnp.concatenate(dqs + dks + dvs + [dqb.reshape(t, 2 * GW), dkb.reshape(t, 2 * HEAD_DIM),
                                            dvb.reshape(t, 2 * HEAD_DIM), dga, dgb], axis=-1)
    gb_tab = _bias_grad(jnp.concatenate(dbs + [dbb], axis=0), buckets)
    if dist:
        GS["bias_tab"], GS["sink_tiles"] = gb_tab, dsink
    else:
        GS["rel_bias"] = gb_tab[:, :20]
        GS["sinks"] = dsink[:, 0, 0].reshape(1, 8)

    G["wint"], GS["b_in"] = _wgrad(dz, un, MXU_DIM, with_colsum=True, name="wgrad_in")
    keys = ("wint",)
    rider = _ExchangeRider([pair(keys)]) if dist else None
    (dh1, GS["mix_norm"]), ro = _inproj_bwd(dz, dh2, h1, S["mix_norm"], W["wint"], rider=rider)
    if dist:
        finish(keys, *ro)

    dx, dg1, du1, a1, df1, GS["ffn1_norm"] = _ffn_bwd(dh1, xf, S["ffn1_norm"], g1, u1, W["wgt1"], W["wut1"], W["wd1"])
    G["wgt1"] = _wgrad(dg1, n1, MXU_DIM, name="wgrad_gate1")
    if dist:
        G["wut1"], ro = _wgrad(du1, n1, MXU_DIM, name="wgrad_up1", rider=_ExchangeRider([pair(("wgt1",))]))
        finish(("wgt1",), *ro)
        G["wd1"], ro = _wgrad(a1, df1, MXU_DIM, name="wgrad_down1", rider=_ExchangeRider([pair(("wut1",))]))
        finish(("wut1",), *ro)
        G["wout"], ro = _wgrad(merged, dh2b, MXU_DIM, name="wgrad_out", rider=_ExchangeRider([pair(("wd1",))]))
        finish(("wd1",), *ro)
    else:
        G["wut1"] = _wgrad(du1, n1, MXU_DIM, name="wgrad_up1")
        G["wd1"] = _wgrad(a1, df1, MXU_DIM, name="wgrad_down1")
        G["wout"] = _wgrad(merged, dh2b, MXU_DIM, name="wgrad_out")
    G["wat"] = _wgrad(dpa, y, MXU_DIM, name="wgrad_branch_a")
    G["wbt"] = _wgrad(dpb, yb, MXU_DIM, name="wgrad_branch_b")
    if dist:
        keys = ("wout", "wat", "wbt")
        finish(keys, *_chip_exchange([pair(keys)]))
    return loss_part, dx.reshape(x.shape), (reduced if dist else G), GS


_SMALL = ("ffn1_norm", "mix_norm", "ffn2_norm", "final_norm", "b_in", "sinks", "rel_bias")
_ORDER = ("ffn1_norm", "ffn1_w_gate", "ffn1_w_up", "ffn1_w_down", "mix_norm", "w_in", "b_in", "w_branch_a",
          "w_branch_b", "w_out", "sinks", "rel_bias", "ffn2_norm", "ffn2_w_gate", "ffn2_w_up", "ffn2_w_down",
          "final_norm")
_BIG = (("wgt1", "ffn1_w_gate", True, 704), ("wut1", "ffn1_w_up", True, 704), ("wd1", "ffn1_w_down", False, 704),
        ("wint", "w_in", True, 1280), ("wout", "w_out", False, 256), ("wat", "w_branch_a", True, 64),
        ("wbt", "w_branch_b", True, 128), ("wgt2", "ffn2_w_gate", True, 704), ("wut2", "ffn2_w_up", True, 704),
        ("wd2", "ffn2_w_down", False, 704))
_FULL_SHAPE = {"wat": (D_MODEL, GW), "wbt": (D_MODEL, 2 * GW)}


def kernel(x, ffn1_norm, ffn1_w_gate, ffn1_w_up, ffn1_w_down, mix_norm, w_in, b_in, w_branch_a, w_branch_b, w_out, sinks, rel_bias, ffn2_norm, ffn2_w_gate, ffn2_w_up, ffn2_w_down, final_norm, loss_target, m_ffn1_norm, m_ffn1_w_gate, m_ffn1_w_up, m_ffn1_w_down, m_mix_norm, m_w_in, m_b_in, m_w_branch_a, m_w_branch_b, m_w_out, m_sinks, m_rel_bias, m_ffn2_norm, m_ffn2_w_gate, m_ffn2_w_up, m_ffn2_w_down, m_final_norm, v_ffn1_norm, v_ffn1_w_gate, v_ffn1_w_up, v_ffn1_w_down, v_mix_norm, v_w_in, v_b_in, v_w_branch_a, v_w_branch_b, v_w_out, v_sinks, v_rel_bias, v_ffn2_norm, v_ffn2_w_gate, v_ffn2_w_up, v_ffn2_w_down, v_final_norm):
    args = dict(locals())
    w = {n: args[n] for n in _ORDER}
    m = {n: args["m_" + n] for n in _ORDER}
    v = {n: args["v_" + n] for n in _ORDER}

    shards = {}
    for key, name, transposed, rows in _BIG:
        a = w[name][0]
        a = (a.T if transposed else a).astype(BF16)
        shards[key] = a.reshape(rows, D_MODEL)
    S = {n: w[n] for n in _SMALL}

    loss_part, grad_x, reduced, GS = _local_step(x, loss_target, {}, S, shards)

    small = _allreduce_small(GS["ffn1_norm"], GS["mix_norm"], GS["ffn2_norm"], GS["final_norm"], GS["b_in"],
                             GS["sink_tiles"], GS["bias_tab"], loss_part)
    loss = small[9, 8]

    out_g, out_d, out_m, out_v = {}, {}, {}, {}
    for key, n, transposed, rows in _BIG:
        nat = w[n][0].shape
        if transposed and nat[1] % 128:
            res = _adamw(w[n][0].T, reduced[key], m[n][0].T, v[n][0].T, "adamw_" + n)
            res = [reduced[key].T] + [r.T for r in res]
        else:
            g = reduced[key].reshape(nat[1], nat[0]).T if transposed else reduced[key].reshape(nat)
            res = [g] + list(_adamw(w[n][0], g, m[n][0], v[n][0], "adamw_" + n))
        out_g[n], out_d[n], out_m[n], out_v[n] = [r[None] for r in res]
    row = lambda d: {n: (d[n].reshape(1, D_MODEL) if n == "final_norm" else d[n]) for n in _SMALL}
    for dst, src in zip((out_g, out_d, out_m, out_v), _adamw_small(small, row(w), row(m), row(v))):
        dst.update(src)
        dst["final_norm"] = src["final_norm"].reshape(D_MODEL)

    return (loss, grad_x, *[out_g[n] for n in _ORDER], *[out_d[n] for n in _ORDER],
            *[out_m[n] for n in _ORDER], *[out_v[n] for n in _ORDER])
```

```python
import math

import jax
import jax.numpy as jnp
from jax import lax
from jax.experimental import pallas as pl
from jax.experimental.pallas import tpu as pltpu

F32, BF16 = jnp.float32, jnp.bfloat16
MESH = pl.DeviceIdType.MESH

D_MODEL = 1024
D_FF = 2816
D_IN = 5120
HEAD_DIM = 64
BLOCK = 128
DIL_GROUPS = ((128, 1), (512, 4), (2048, 16))
B_WINDOW = 128
N_BUCKETS = 32
MAX_DISTANCE = 2048
EPS = 1e-6
N_CHIPS = 4
GW = 256
ZA_W = 2304
ZB_W = 768
NEG = -1e30

ADAM_LR, ADAM_B1, ADAM_B2, ADAM_EPS, ADAM_WD, ADAM_STEP = 0.001, 0.9, 0.999, 1e-08, 0.01, 10

VMEM_BIG = 56 * 1024 * 1024
TM = 512
TM_BWD = 256
MXU_DIM = 256
FF_BOUNDS = (0, 6 * MXU_DIM, D_FF)
DMA_SPLIT = 8
RESIDUES_PER_STEP = 8
ATTN_UNROLL = 5


def _dot(a, b):
    return jnp.dot(a, b, preferred_element_type=F32)


def _dot_nt(a, b):
    return lax.dot_general(a, b, (((1,), (1,)), ((), ())), preferred_element_type=F32)


def _dot_tn(a, b):
    return lax.dot_general(a, b, (((0,), (0,)), ((), ())), preferred_element_type=F32)


def _sigmoid(x):
    return 0.5 * jnp.tanh(0.5 * x) + 0.5


def _params(sem, vmem=None):
    return pltpu.CompilerParams(dimension_semantics=sem, vmem_limit_bytes=vmem)


ANY = pl.BlockSpec(memory_space=pl.ANY)


def _me():
    return lax.axis_index("x"), lax.axis_index("y"), lax.axis_index("c")


_CHIP_RELS = ((1, 0), (0, 1), (1, 1))


def _flip(v, f):
    return 1 - v if f else v


def _remote(src, dst, ssem, rsem, peer):
    return pltpu.make_async_remote_copy(src_ref=src, dst_ref=dst, send_sem=ssem, recv_sem=rsem,
                                        device_id=peer, device_id_type=MESH)


def _row_pieces(rows, n):
    step = max(16, -(-rows // n) // 16 * 16)
    out, s = [], 0
    while s < rows:
        out.append((s, min(step, rows - s)))
        s += step
    return out


def _gather_rows(shards):
    nt = len(shards)
    rows = [s.shape[0] for s in shards]

    def body(*refs):
        srcs, outs = refs[:nt], refs[nt:2 * nt]
        halves, quarters = refs[2 * nt:3 * nt], refs[3 * nt:4 * nt]
        ici_s, ici_r, fwd_s, fwd_r, d2d_s, d2d_r, keep, loc = refs[4 * nt:]
        x, y, c = _me()
        j = 2 * x + y
        sib = (x, y, 1 - c)
        nbr = ((1 - x, y, c), (x, 1 - y, c))
        nbr_j = (2 * (1 - x) + y, 2 * x + (1 - y))
        diag_j = 2 * (1 - x) + (1 - y)
        local = [pltpu.make_async_copy(srcs[t], outs[t].at[j], loc.at[t]) for t in range(nt)]
        for cp in local:
            cp.start()
        pending = []
        for a in range(2):
            for t in range(nt):
                half = pl.ds(c * (rows[t] // 2), rows[t] // 2)
                cp = _remote(srcs[t].at[half], halves[t].at[a], ici_s.at[2 * t + a], ici_r.at[2 * t + a], nbr[a])
                cp.start()
                pending.append(cp)
        placed = []

        def place(src, dst_of, idx):
            mine = pltpu.make_async_copy(src, dst_of, keep.at[idx])
            mine.start()
            cp = _remote(src, dst_of, d2d_s.at[idx], d2d_r.at[idx], sib)
            cp.start()
            placed.append((mine, cp))

        for a in range(2):
            for t in range(nt):
                r2, r4 = rows[t] // 2, rows[t] // 4
                got = halves[t].at[a]
                _remote(got, got, ici_s.at[2 * t + a], ici_r.at[2 * t + a], nbr[a]).wait_recv()
                cp = _remote(halves[t].at[a, pl.ds(a * r4, r4)], quarters[t].at[a], fwd_s.at[2 * t + a],
                             fwd_r.at[2 * t + a], nbr[1 - a])
                cp.start()
                pending.append(cp)
                place(got, outs[t].at[nbr_j[a], pl.ds(c * r2, r2)], 4 * t + a)
        for a in range(2):
            for t in range(nt):
                r2, r4 = rows[t] // 2, rows[t] // 4
                got = quarters[t].at[a]
                _remote(got, got, fwd_s.at[2 * t + a], fwd_r.at[2 * t + a], nbr[1 - a]).wait_recv()
                place(got, outs[t].at[diag_j, pl.ds(c * r2 + a * r4, r4)], 4 * t + 2 + a)
        for mine, cp in placed:
            mine.wait()
            cp.wait()
        for cp in pending:
            cp.wait_send()
        for cp in local:
            cp.wait()

    stage = ([pltpu.VMEM((2, r // 2, D_MODEL), BF16) for r in rows] + [pltpu.VMEM((2, r // 4, D_MODEL), BF16) for r in rows])
    sems = ([pltpu.SemaphoreType.DMA((2 * nt,)) for _ in range(4)] + [pltpu.SemaphoreType.DMA((4 * nt,))] * 3
            + [pltpu.SemaphoreType.DMA((nt,))])
    return pl.pallas_call(
        body, name="gather_weights",
        out_shape=tuple(jax.ShapeDtypeStruct((N_CHIPS,) + s.shape, s.dtype) for s in shards),
        in_specs=[pl.BlockSpec(memory_space=pltpu.VMEM)] * nt,
        out_specs=tuple([ANY] * nt), scratch_shapes=stage + sems,
    )(*shards)


VMEM_WHOLE = pl.BlockSpec(memory_space=pltpu.VMEM)


def _pair_reduce(grads, name):
    nt = len(grads)
    r2 = [g.shape[2] for g in grads]
    off = [sum(r2[:t]) for t in range(nt)]
    tot = sum(r2)

    def body(*refs):
        gs = refs[:nt]
        s_ref, got, ssem, rsem = refs[nt:]
        x, y, c = _me()
        sib = (x, y, 1 - c)
        for t in range(nt):
            for k in range(N_CHIPS):
                _remote(gs[t].at[k, 1 - c], got.at[k, pl.ds(off[t], r2[t])], ssem, rsem, sib).start()
        _remote(got, got, ssem, rsem, sib).wait()
        for t in range(nt):
            for k in range(N_CHIPS):
                rows = slice(off[t], off[t] + r2[t])
                s_ref[k, rows, :] = (gs[t][k, c].astype(F32) + got[k, rows, :].astype(F32)).astype(BF16)

    shp = jax.ShapeDtypeStruct((N_CHIPS, tot, D_MODEL), BF16)
    return pl.pallas_call(
        body, name=name, out_shape=shp, in_specs=[VMEM_WHOLE] * nt, out_specs=VMEM_WHOLE,
        scratch_shapes=[pltpu.VMEM((N_CHIPS, tot, D_MODEL), BF16), pltpu.SemaphoreType.DMA(()),
                        pltpu.SemaphoreType.DMA(())],
        compiler_params=pltpu.CompilerParams(vmem_limit_bytes=VMEM_BIG),
    )(*grads)


def _chip_exchange(parts):
    ng = len(parts)
    r2 = [p.shape[1] for p in parts]
    off = [sum(r2[:g]) for g in range(ng)]
    tot = sum(r2)

    def body(*refs):
        ps = refs[:ng]
        own_ref, rec_ref, ssems, rsems, lsem = refs[ng:]
        x, y, c = _me()
        j = 2 * x + y
        for g in range(ng):
            pltpu.make_async_copy(ps[g].at[j], own_ref.at[pl.ds(off[g], r2[g])], lsem).start()
        for k, (fx, fy) in enumerate(_CHIP_RELS):
            px, py = _flip(x, fx), _flip(y, fy)
            for g in range(ng):
                for st, sz in _row_pieces(r2[g], 2):
                    _remote(ps[g].at[2 * px + py, pl.ds(st, sz)], rec_ref.at[k, pl.ds(off[g] + st, sz)],
                            ssems.at[k], rsems.at[k], (px, py, c)).start()
        for k in range(3):
            _remote(rec_ref.at[k], rec_ref.at[k], ssems.at[k], rsems.at[k], (x, y, c)).wait()
        pltpu.make_async_copy(own_ref, own_ref, lsem).wait()

    return pl.pallas_call(
        body, name="grad_chip_exchange",
        out_shape=(jax.ShapeDtypeStruct((tot, D_MODEL), BF16), jax.ShapeDtypeStruct((3, tot, D_MODEL), BF16)),
        in_specs=[VMEM_WHOLE] * ng, out_specs=(ANY, ANY),
        scratch_shapes=[pltpu.SemaphoreType.DMA((3,)), pltpu.SemaphoreType.DMA((3,)), pltpu.SemaphoreType.DMA(())],
    )(*parts)


def _final_reduce(own, rec, name):
    r2 = own.shape[0]
    pieces = _row_pieces(r2, DMA_SPLIT)

    def body(own_ref, rec_ref, o_ref, fbuf, ssem, rsem, lsem):
        x, y, c = _me()
        sib = (x, y, 1 - c)
        for st, sz in pieces:
            rows = slice(st, st + sz)
            fbuf[rows, :] = (own_ref[rows, :].astype(F32) + rec_ref[0, rows, :].astype(F32)
                             + rec_ref[1, rows, :].astype(F32) + rec_ref[2, rows, :].astype(F32))
            pltpu.make_async_copy(fbuf.at[pl.ds(st, sz)], o_ref.at[c, pl.ds(st, sz)], lsem).start()
            _remote(fbuf.at[pl.ds(st, sz)], o_ref.at[c, pl.ds(st, sz)], ssem, rsem, sib).start()
        _remote(fbuf, o_ref.at[c], ssem, rsem, sib).wait()
        pltpu.make_async_copy(fbuf, o_ref.at[c], lsem).wait()

    return pl.pallas_call(
        body, name=name, out_shape=jax.ShapeDtypeStruct((2, r2, D_MODEL), F32),
        in_specs=[VMEM_WHOLE, VMEM_WHOLE], out_specs=ANY,
        scratch_shapes=[pltpu.VMEM((r2, D_MODEL), F32), pltpu.SemaphoreType.DMA(()), pltpu.SemaphoreType.DMA(()),
                        pltpu.SemaphoreType.DMA(())],
        compiler_params=pltpu.CompilerParams(vmem_limit_bytes=VMEM_BIG),
    )(own, rec)


SMALL_ROWS = 48


def _allreduce_small(g_ffn1, g_mix, g_ffn2, g_final, g_bin, dsink, bias_tab, loss_part):
    def body(f1_ref, mx_ref, f2_ref, fn_ref, bi_ref, sk_ref, bt_ref, ls_ref, o_ref, mine, buf, send_sems, recv_sems):
        x, y, c = _me()
        me = 4 * x + 2 * y + c
        mine[...] = jnp.zeros_like(mine)
        for r, ref in enumerate((f1_ref, mx_ref, f2_ref, fn_ref)):
            mine[r:r + 1, :] = ref[...]
        for k in range(D_IN // D_MODEL):
            mine[4 + k:5 + k, :] = bi_ref[:, k * D_MODEL:(k + 1) * D_MODEL]
        lane = lax.broadcasted_iota(jnp.int32, (1, 128), 1)
        row = jnp.where(lane == 8, ls_ref[0:1, :], 0.0)
        for h in range(8):
            row = jnp.where(lane == h, sk_ref[h, 0:1, :], row)
        mine[9:10, 0:128] = row
        mine[16:48, 0:128] = bt_ref[...]
        buf[me] = mine[...]
        copies = []
        for k in range(1, 8):
            peer = (_flip(x, (k >> 2) & 1), _flip(y, (k >> 1) & 1), _flip(c, k & 1))
            cp = _remote(mine, buf.at[me], send_sems.at[k - 1], recv_sems.at[k - 1], peer)
            cp.start()
            copies.append(cp)
        for cp in copies:
            cp.wait()
        acc = buf[0]
        for i in range(1, 8):
            acc = acc + buf[i]
        o_ref[...] = acc

    vm = pl.BlockSpec(memory_space=pltpu.VMEM)
    shape = (SMALL_ROWS, D_MODEL)
    return pl.pallas_call(
        body, name="allreduce_small", out_shape=jax.ShapeDtypeStruct(shape, F32),
        in_specs=[vm] * 8, out_specs=vm,
        scratch_shapes=[pltpu.VMEM(shape, F32), pltpu.VMEM((8,) + shape, F32), pltpu.SemaphoreType.DMA((7,)),
                        pltpu.SemaphoreType.DMA((7,))],
    )(g_ffn1, g_mix, g_ffn2, g_final, g_bin, dsink, bias_tab, loss_part)


def _adam_update(w, g, m, v):
    nm = ADAM_B1 * m + (1.0 - ADAM_B1) * g
    nv = ADAM_B2 * v + (1.0 - ADAM_B2) * (g * g)
    bc1 = 1.0 - ADAM_B1 ** ADAM_STEP
    bc2 = 1.0 - ADAM_B2 ** ADAM_STEP
    return -ADAM_LR * ((nm / bc1) / (jnp.sqrt(nv / bc2) + ADAM_EPS) + ADAM_WD * w), nm, nv


def _adamw_small(packed, w, m, v):
    names = ("ffn1_norm", "mix_norm", "ffn2_norm", "final_norm", "b_in", "sinks", "rel_bias")
    nn = len(names)

    def grad_of(p_ref, name, k=0):
        if name == "b_in":
            return p_ref[4 + k:5 + k, :]
        if name == "sinks":
            return p_ref[9:10, 0:8]
        if name == "rel_bias":
            return p_ref[16:48, 0:20]
        r = names.index(name)
        return p_ref[r:r + 1, :]

    def body(p_ref, *refs):
        ws, ms, vs = refs[:nn], refs[nn:2 * nn], refs[2 * nn:3 * nn]
        outs = refs[3 * nn:]
        for i, name in enumerate(names):
            og, od, om, ov = outs[i], outs[nn + i], outs[2 * nn + i], outs[3 * nn + i]
            pieces = range(D_IN // D_MODEL) if name == "b_in" else (0,)
            for k in pieces:
                sl = (slice(None), slice(k * D_MODEL, (k + 1) * D_MODEL)) if name == "b_in" else (Ellipsis,)
                g = grad_of(p_ref, name, k)
                d, nm, nv = _adam_update(ws[i][sl], g, ms[i][sl], vs[i][sl])
                og[sl], od[sl], om[sl], ov[sl] = g, d, nm, nv

    vm = pl.BlockSpec(memory_space=pltpu.VMEM)
    shapes = [jax.ShapeDtypeStruct(w[n].shape, F32) for n in names]
    res = pl.pallas_call(
        body, name="adamw_small", out_shape=tuple(shapes * 4), in_specs=[vm] * (1 + 3 * nn),
        out_specs=tuple([vm] * (4 * nn)),
    )(packed, *[w[n] for n in names], *[m[n] for n in names], *[v[n] for n in names])
    return [dict(zip(names, res[i * nn:(i + 1) * nn])) for i in range(4)]


class _GatherRider:
    def __init__(self, shards):
        self.inputs = list(shards)
        nt = len(shards)
        self.out_shape = [jax.ShapeDtypeStruct((N_CHIPS,) + s.shape, s.dtype) for s in shards]
        self.scratch = [pltpu.SemaphoreType.DMA((3 * nt,)), pltpu.SemaphoreType.DMA((3 * nt,)),
                        pltpu.SemaphoreType.DMA((nt,))]

    def _copies(self, srcs, outs, sems):
        ici_s, ici_r, loc = sems
        x, y, c = _me()
        j = 2 * x + y
        local = [pltpu.make_async_copy(srcs[t], outs[t].at[j], loc.at[t]) for t in range(len(srcs))]
        remote = []
        for k, (fx, fy) in enumerate(_CHIP_RELS):
            peer = (_flip(x, fx), _flip(y, fy), c)
            for t in range(len(srcs)):
                remote.append(_remote(srcs[t], outs[t].at[j], ici_s.at[3 * t + k], ici_r.at[3 * t + k], peer))
        return local, remote

    def start(self, srcs, outs, sems):
        local, remote = self._copies(srcs, outs, sems)
        for cp in local + remote:
            cp.start()

    def finish(self, srcs, outs, sems):
        local, remote = self._copies(srcs, outs, sems)
        for cp in remote + local:
            cp.wait()


class _ExchangeRider:
    def __init__(self, parts):
        self.inputs = list(parts)
        self.r2 = [p.shape[1] for p in parts]
        self.off = [sum(self.r2[:g]) for g in range(len(parts))]
        tot = sum(self.r2)
        self.out_shape = [jax.ShapeDtypeStruct((tot, D_MODEL), BF16), jax.ShapeDtypeStruct((3, tot, D_MODEL), BF16)]
        self.scratch = [pltpu.SemaphoreType.DMA((3,)), pltpu.SemaphoreType.DMA((3,)), pltpu.SemaphoreType.DMA(())]

    def start(self, ps, outs, sems):
        own_ref, rec_ref = outs
        ssems, rsems, lsem = sems
        x, y, c = _me()
        j = 2 * x + y
        for g in range(len(ps)):
            pltpu.make_async_copy(ps[g].at[j], own_ref.at[pl.ds(self.off[g], self.r2[g])], lsem).start()
        for k, (fx, fy) in enumerate(_CHIP_RELS):
            px, py = _flip(x, fx), _flip(y, fy)
            for g in range(len(ps)):
                for st, sz in _row_pieces(self.r2[g], 2):
                    _remote(ps[g].at[2 * px + py, pl.ds(st, sz)], rec_ref.at[k, pl.ds(self.off[g] + st, sz)],
                            ssems.at[k], rsems.at[k], (px, py, c)).start()

    def finish(self, ps, outs, sems):
        own_ref, rec_ref = outs
        ssems, rsems, lsem = sems
        x, y, c = _me()
        for k in range(3):
            _remote(rec_ref.at[k], rec_ref.at[k], ssems.at[k], rsems.at[k], (x, y, c)).wait()
        pltpu.make_async_copy(own_ref, own_ref, lsem).wait()


def _pallas(body, args, *, name, grid, in_specs, out_specs, out_shape, scratch_shapes=(), sem=None, vmem=None,
            rider=None):
    if rider is None:
        res = pl.pallas_call(body, name=name, grid=grid, in_specs=list(in_specs), out_specs=tuple(out_specs),
                             out_shape=tuple(out_shape), scratch_shapes=list(scratch_shapes),
                             compiler_params=_params(sem, vmem))(*args)
        return tuple(res), ()
    n_in, n_out, n_sc = len(in_specs), len(out_shape), len(scratch_shapes)
    r_in, r_out = len(rider.inputs), len(rider.out_shape)

    def wrapped(*refs):
        ins, rins = refs[:n_in], refs[n_in:n_in + r_in]
        p = n_in + r_in
        outs, routs = refs[p:p + n_out], refs[p + n_out:p + n_out + r_out]
        p += n_out + r_out
        scr, rsems = refs[p:p + n_sc], refs[p + n_sc:]
        first = pl.program_id(0) == 0
        last = pl.program_id(0) == grid[0] - 1
        for a in range(1, len(grid)):
            first = first & (pl.program_id(a) == 0)
            last = last & (pl.program_id(a) == grid[a] - 1)

        @pl.when(first)
        def _():
            rider.start(rins, routs, rsems)

        body(*ins, *outs, *scr)

        @pl.when(last)
        def _():
            rider.finish(rins, routs, rsems)

    res = pl.pallas_call(
        wrapped, name=name, grid=grid, in_specs=list(in_specs) + [ANY] * r_in,
        out_specs=tuple(out_specs) + (ANY,) * r_out, out_shape=tuple(out_shape) + tuple(rider.out_shape),
        scratch_shapes=list(scratch_shapes) + rider.scratch,
        compiler_params=_params(("arbitrary",) * len(grid), vmem))(*args, *rider.inputs)
    return tuple(res[:n_out]), tuple(res[n_out:])


def _loss_tile(hh, gain, tgt):
    r = lax.rsqrt(jnp.mean(hh * hh, axis=-1, keepdims=True) + EPS)
    hn = hh * r
    err = hn * gain - tgt
    part = (0.5 / D_MODEL) * jnp.sum(jnp.sum(err * err, axis=1, keepdims=True), axis=0, keepdims=True)
    dy = err * (1.0 / D_MODEL)
    dng = dy * gain
    dh = r * (dng - hn * jnp.mean(dng * hn, axis=-1, keepdims=True))
    return dh, part, jnp.sum(dy * hn, axis=0, keepdims=True)


def _ffn_fwd(h, gain, wgt, wut, wd, rider=None, head=None):
    t = h.shape[0]

    def body(h_ref, gain_ref, wg_hbm, wu_hbm, wd_hbm, *rest):
        if head is None:
            hout_ref, n_ref, g_ref, u_ref, wg_v, wu_v, wd_v = rest
        else:
            fg_ref, tgt_ref, hout_ref, n_ref, g_ref, u_ref, loss_ref, gg_ref, wg_v, wu_v, wd_v = rest

        @pl.when(pl.program_id(0) == 0)
        def _():
            pltpu.sync_copy(wg_hbm, wg_v)
            pltpu.sync_copy(wu_hbm, wu_v)
            pltpu.sync_copy(wd_hbm, wd_v)
            if head is not None:
                loss_ref[...] = jnp.zeros_like(loss_ref)
                gg_ref[...] = jnp.zeros_like(gg_ref)

        hh = h_ref[...]
        r = lax.rsqrt(jnp.mean(hh * hh, axis=-1, keepdims=True) + EPS)
        n = (hh * r * gain_ref[...]).astype(BF16)
        n_ref[...] = n
        acc = jnp.zeros((TM, D_MODEL), F32)
        for c0, c1 in zip(FF_BOUNDS[:-1], FF_BOUNDS[1:]):
            sl = slice(c0, c1)
            g = _dot_nt(n, wg_v[sl, :])
            u = _dot_nt(n, wu_v[sl, :])
            g_ref[:, sl] = g.astype(BF16)
            u_ref[:, sl] = u.astype(BF16)
            a = (g * _sigmoid(g) * u).astype(BF16)
            acc = acc + _dot(a, wd_v[sl, :])
        hout = hh + 0.5 * acc
        if head is None:
            hout_ref[...] = hout
        else:
            dh, part, gpart = _loss_tile(hout, fg_ref[...], tgt_ref[...])
            hout_ref[...] = dh
            loss_ref[...] += part
            gg_ref[...] += gpart

    row = lambda w: pl.BlockSpec((TM, w), lambda i: (i, 0))
    vec = pl.BlockSpec((1, D_MODEL), lambda i: (0, 0))
    wv = pltpu.VMEM((D_FF, D_MODEL), BF16)
    args, in_specs = (h, gain, wgt, wut, wd), [row(D_MODEL), vec, ANY, ANY, ANY]
    out_shape = [jax.ShapeDtypeStruct((t, D_MODEL), F32), jax.ShapeDtypeStruct((t, D_MODEL), BF16),
                 jax.ShapeDtypeStruct((t, D_FF), BF16), jax.ShapeDtypeStruct((t, D_FF), BF16)]
    out_specs = [row(D_MODEL), row(D_MODEL), row(D_FF), row(D_FF)]
    if head is not None:
        args, in_specs = args + tuple(head), in_specs + [vec, row(D_MODEL)]
        out_shape += [jax.ShapeDtypeStruct((8, 128), F32), jax.ShapeDtypeStruct((1, D_MODEL), F32)]
        out_specs += [pl.BlockSpec((8, 128), lambda i: (0, 0)), vec]
    return _pallas(
        body, args, name="ffn_fwd", grid=(t // TM,), out_shape=tuple(out_shape), in_specs=in_specs,
        out_specs=tuple(out_specs), scratch_shapes=[wv, wv, wv], sem=("arbitrary",), vmem=VMEM_BIG, rider=rider)


def _ffn_bwd(dhout, h, gain, g, u, wgt, wut, wd):
    t = h.shape[0]
    tm = TM_BWD

    def body(dho_ref, h_ref, gain_ref, g_ref, u_ref, wg_hbm, wu_hbm, wd_hbm,
             dh_ref, dg_ref, du_ref, a_ref, df_ref, gg_ref, wg_v, wu_v, wd_v):
        @pl.when(pl.program_id(0) == 0)
        def _():
            pltpu.sync_copy(wg_hbm, wg_v)
            pltpu.sync_copy(wu_hbm, wu_v)
            pltpu.sync_copy(wd_hbm, wd_v)
            gg_ref[...] = jnp.zeros_like(gg_ref)

        dho = dho_ref[...]
        df = (0.5 * dho).astype(BF16)
        df_ref[...] = df
        dn = jnp.zeros((tm, D_MODEL), F32)
        for c0, c1 in zip(FF_BOUNDS[:-1], FF_BOUNDS[1:]):
            sl = slice(c0, c1)
            da = _dot_nt(df, wd_v[sl, :])
            gv = g_ref[:, sl].astype(F32)
            uv = u_ref[:, sl].astype(F32)
            sg = _sigmoid(gv)
            silu = gv * sg
            dg = (da * uv * (sg * (1.0 + gv * (1.0 - sg)))).astype(BF16)
            du = (da * silu).astype(BF16)
            dg_ref[:, sl] = dg
            du_ref[:, sl] = du
            a_ref[:, sl] = (silu * uv).astype(BF16)
            dn = dn + _dot(dg, wg_v[sl, :]) + _dot(du, wu_v[sl, :])
        hh = h_ref[...]
        r = lax.rsqrt(jnp.mean(hh * hh, axis=-1, keepdims=True) + EPS)
        hn = hh * r
        gg_ref[...] += jnp.sum(dn * hn, axis=0, keepdims=True)
        dng = dn * gain_ref[...]
        dh_ref[...] = dho + r * (dng - hn * jnp.mean(dng * hn, axis=-1, keepdims=True))

    row = lambda w: pl.BlockSpec((tm, w), lambda i: (i, 0))
    vec = pl.BlockSpec((1, D_MODEL), lambda i: (0, 0))
    wv = pltpu.VMEM((D_FF, D_MODEL), BF16)
    return pl.pallas_call(
        body, name="ffn_bwd", grid=(t // tm,),
        out_shape=(jax.ShapeDtypeStruct((t, D_MODEL), F32), jax.ShapeDtypeStruct((t, D_FF), BF16),
                   jax.ShapeDtypeStruct((t, D_FF), BF16), jax.ShapeDtypeStruct((t, D_FF), BF16),
                   jax.ShapeDtypeStruct((t, D_MODEL), BF16), jax.ShapeDtypeStruct((1, D_MODEL), F32)),
        in_specs=[row(D_MODEL), row(D_MODEL), vec, row(D_FF), row(D_FF), ANY, ANY, ANY],
        out_specs=(row(D_MODEL), row(D_FF), row(D_FF), row(D_FF), row(D_MODEL), vec),
        scratch_shapes=[wv, wv, wv],
        compiler_params=_params(("arbitrary",), VMEM_BIG),
    )(dhout, h, gain, g, u, wgt, wut, wd)


def _wgrad(lhs, rhs, rb, with_colsum=False, name="wgrad", rider=None):
    t, k = lhs.shape
    n = rhs.shape[1]

    def body(l_ref, r_ref, o_ref, *rest):
        o_ref[...] = _dot_tn(l_ref[...], r_ref[...]).astype(BF16)
        if with_colsum:
            rest[0][...] = jnp.sum(l_ref[...].astype(F32), axis=0, keepdims=True)

    out_shape = [jax.ShapeDtypeStruct((k, n), BF16)]
    out_specs = [pl.BlockSpec((rb, n), lambda j: (j, 0))]
    if with_colsum:
        out_shape.append(jax.ShapeDtypeStruct((1, k), F32))
        out_specs.append(pl.BlockSpec((1, rb), lambda j: (0, j)))
    res, ro = _pallas(
        body, (lhs, rhs), name=name, grid=(k // rb,), out_shape=tuple(out_shape),
        in_specs=[pl.BlockSpec((t, rb), lambda j: (0, j)), pl.BlockSpec((t, n), lambda j: (0, 0))],
        out_specs=tuple(out_specs), sem=("arbitrary",), vmem=VMEM_BIG, rider=rider)
    if rider is not None:
        return res[0], ro
    return res if with_colsum else res[0]


def _lane_blocks(nseq, seq, nblk, tm=TM):
    spt = seq // tm
    return pl.BlockSpec((1, nblk, tm, 128), lambda i: (i // spt, 0, i % spt, 0))


def _inproj_fwd(h, gain, wint, b_in, nseq, rider=None):
    t = h.shape[0]
    seq = t // nseq
    cut_a = 5 * MXU_DIM
    pieces = ((0, cut_a, 0, 0), (cut_a, ZA_W - cut_a, 0, cut_a), (ZA_W, ZB_W, 1, 0), (ZA_W + ZB_W, 1024, 2, 0),
              (ZA_W + ZB_W + 1024, 1024, 2, 1024))

    def body(h_ref, gain_ref, w_hbm, b_ref, u_ref, za_ref, zb_ref, zg_ref, w_v):
        @pl.when(pl.program_id(0) == 0)
        def _():
            pltpu.sync_copy(w_hbm, w_v)

        hh = h_ref[...]
        r = lax.rsqrt(jnp.mean(hh * hh, axis=-1, keepdims=True) + EPS)
        un = (hh * r * gain_ref[...]).astype(BF16)
        u_ref[...] = un
        outs = (None, zb_ref, zg_ref)
        for c0, cw, oi, o0 in pieces:
            val = _dot_nt(un, w_v[c0:c0 + cw, :]) + b_ref[:, c0:c0 + cw]
            if oi == 0:
                for cb in range(cw // 128):
                    za_ref[0, o0 // 128 + cb] = val[:, cb * 128:(cb + 1) * 128]
            else:
                outs[oi][:, o0:o0 + cw] = val.astype(BF16)

    row = lambda w: pl.BlockSpec((TM, w), lambda i: (i, 0))
    return _pallas(
        body, (h, gain, wint, b_in), name="inproj_fwd", grid=(t // TM,),
        out_shape=(jax.ShapeDtypeStruct((t, D_MODEL), BF16), jax.ShapeDtypeStruct((nseq, ZA_W // 128, seq, 128), F32),
                   jax.ShapeDtypeStruct((t, ZB_W), BF16), jax.ShapeDtypeStruct((t, 2 * D_MODEL), BF16)),
        in_specs=[row(D_MODEL), pl.BlockSpec((1, D_MODEL), lambda i: (0, 0)), ANY,
                  pl.BlockSpec((1, D_IN), lambda i: (0, 0))],
        out_specs=(row(D_MODEL), _lane_blocks(nseq, seq, ZA_W // 128), row(ZB_W), row(2 * D_MODEL)),
        scratch_shapes=[pltpu.VMEM((D_IN, D_MODEL), BF16)], sem=("arbitrary",), vmem=VMEM_BIG, rider=rider)


def _inproj_bwd(dz, dh2, h, gain, wint, rider=None):
    t = h.shape[0]
    nc = 5
    cw = D_IN // nc

    def body(dz_ref, dh2_ref, h_ref, gain_ref, w_hbm, dh_ref, gg_ref, w_v):
        @pl.when(pl.program_id(0) == 0)
        def _():
            pltpu.sync_copy(w_hbm, w_v)
            gg_ref[...] = jnp.zeros_like(gg_ref)

        du = jnp.zeros((TM, D_MODEL), F32)
        for ci in range(nc):
            sl = slice(ci * cw, (ci + 1) * cw)
            du = du + _dot(dz_ref[:, sl], w_v[sl, :])
        hh = h_ref[...]
        r = lax.rsqrt(jnp.mean(hh * hh, axis=-1, keepdims=True) + EPS)
        hn = hh * r
        gg_ref[...] += jnp.sum(du * hn, axis=0, keepdims=True)
        dng = du * gain_ref[...]
        dh_ref[...] = dh2_ref[...] + r * (dng - hn * jnp.mean(dng * hn, axis=-1, keepdims=True))

    row = lambda w: pl.BlockSpec((TM, w), lambda i: (i, 0))
    vec = pl.BlockSpec((1, D_MODEL), lambda i: (0, 0))
    return _pallas(
        body, (dz, dh2, h, gain, wint), name="inproj_bwd", grid=(t // TM,),
        out_shape=(jax.ShapeDtypeStruct((t, D_MODEL), F32), jax.ShapeDtypeStruct((1, D_MODEL), F32)),
        in_specs=[row(D_IN), row(D_MODEL), row(D_MODEL), vec, ANY],
        out_specs=(row(D_MODEL), vec),
        scratch_shapes=[pltpu.VMEM((D_IN, D_MODEL), BF16)], sem=("arbitrary",), vmem=VMEM_BIG, rider=rider)


def _head_sums(x):
    w = x.shape[1]
    i = lax.broadcasted_iota(jnp.int32, (w, w), 0) // HEAD_DIM
    j = lax.broadcasted_iota(jnp.int32, (w, w), 1) // HEAD_DIM
    ones = (i == j).astype(BF16)
    hi = x.astype(BF16)
    r1 = x - hi.astype(F32)
    mid = r1.astype(BF16)
    lo = (r1 - mid.astype(F32)).astype(BF16)
    return _dot(hi, ones) + _dot(mid, ones) + _dot(lo, ones)


def _merge_fwd(o0, o1, o2, l0, l1, l2, yb, zg, h1, wat, wbt, wout, rider=None):
    t = h1.shape[0]
    nseq, _, seq, _ = o0.shape

    def body(o0_ref, o1_ref, o2_ref, l0_ref, l1_ref, l2_ref, yb_ref, ga_ref, gb_ref, h1_ref, wa_ref, wb_ref, wo_ref,
             h2_ref, y_ref, lt_ref, pa_ref, pb_ref, mg_ref):
        wide = lambda ref: jnp.concatenate([ref[0, 0], ref[0, 1]], axis=1)
        la, lb, lc = wide(l0_ref), wide(l1_ref), wide(l2_ref)
        mx = jnp.maximum(jnp.maximum(la, lb), lc)
        ea, eb, ec = jnp.exp(la - mx), jnp.exp(lb - mx), jnp.exp(lc - mx)
        den = ea + eb + ec
        y = (ea * wide(o0_ref) + eb * wide(o1_ref) + ec * wide(o2_ref)) / den
        lt = mx + jnp.log(den)
        lt_ref[0, 0] = lt[:, :128]
        lt_ref[0, 1] = lt[:, 128:]
        yb16 = y.astype(BF16)
        y_ref[...] = yb16
        pa = _dot_nt(yb16, wa_ref[...])
        pb = _dot_nt(yb_ref[...], wb_ref[...])
        pa_ref[...] = pa.astype(BF16)
        pb_ref[...] = pb.astype(BF16)
        mg = (_sigmoid(ga_ref[...].astype(F32)) * pa + _sigmoid(gb_ref[...].astype(F32)) * pb).astype(BF16)
        mg_ref[...] = mg
        h2_ref[...] = h1_ref[...] + _dot(mg, wo_ref[...])

    row = lambda w: pl.BlockSpec((TM, w), lambda i: (i, 0))
    full = lambda a: pl.BlockSpec(a.shape, lambda i: (0, 0))
    gate = lambda cb: pl.BlockSpec((TM, D_MODEL), lambda i: (i, cb))
    return _pallas(
        body, (o0, o1, o2, l0, l1, l2, yb, zg, zg, h1, wat, wbt, wout), name="merge_fwd", grid=(t // TM,),
        out_shape=(jax.ShapeDtypeStruct((t, D_MODEL), F32), jax.ShapeDtypeStruct((t, GW), BF16),
                   jax.ShapeDtypeStruct((nseq, 2, seq, 128), F32), jax.ShapeDtypeStruct((t, D_MODEL), BF16),
                   jax.ShapeDtypeStruct((t, D_MODEL), BF16), jax.ShapeDtypeStruct((t, D_MODEL), BF16)),
        in_specs=[_lane_blocks(nseq, seq, 2)] * 6 + [row(2 * GW), gate(0), gate(1), row(D_MODEL), full(wat), full(wbt),
                                                     full(wout)],
        out_specs=(row(D_MODEL), row(GW), _lane_blocks(nseq, seq, 2), row(D_MODEL), row(D_MODEL), row(D_MODEL)),
        sem=("parallel",), vmem=VMEM_BIG, rider=rider)


def _merge_bwd(dh2, pa, pb, zg, y, yb, wat, wbt, wout, nseq, rider=None):
    t = dh2.shape[0]

    def body(dh2_ref, pa_ref, pb_ref, ga_ref, gb_ref, y_ref, yb_ref, wa_ref, wb_ref, wo_ref,
             dpa_ref, dpb_ref, dga_ref, dgb_ref, dya_ref, dyb_ref, dh2b_ref, ca_ref, cb_ref):
        d16 = dh2_ref[...].astype(BF16)
        dh2b_ref[...] = d16
        dm = _dot_nt(d16, wo_ref[...])
        sa = _sigmoid(ga_ref[...].astype(F32))
        sb = _sigmoid(gb_ref[...].astype(F32))
        dpa = (dm * sa).astype(BF16)
        dpb = (dm * sb).astype(BF16)
        dpa_ref[...] = dpa
        dpb_ref[...] = dpb
        dga_ref[...] = (dm * pa_ref[...].astype(F32) * sa * (1.0 - sa)).astype(BF16)
        dgb_ref[...] = (dm * pb_ref[...].astype(F32) * sb * (1.0 - sb)).astype(BF16)
        dya = _dot(dpa, wa_ref[...])
        dyb = _dot(dpb, wb_ref[...])
        dya_ref[0, 0] = dya[:, :128]
        dya_ref[0, 1] = dya[:, 128:]
        dyb_ref[...] = dyb.astype(BF16)
        ca = _head_sums(dya * y_ref[...].astype(F32))
        ca_ref[0, 0] = ca[:, :128]
        ca_ref[0, 1] = ca[:, 128:]
        cb_ref[...] = _head_sums(dyb * yb_ref[...].astype(F32))

    row = lambda w: pl.BlockSpec((TM, w), lambda i: (i, 0))
    full = lambda a: pl.BlockSpec(a.shape, lambda i: (0, 0))
    gate = lambda cb: pl.BlockSpec((TM, D_MODEL), lambda i: (i, cb))
    bf = lambda w: jax.ShapeDtypeStruct((t, w), BF16)
    lanes = jax.ShapeDtypeStruct((nseq, 2, t // nseq, 128), F32)
    lane_spec = _lane_blocks(nseq, t // nseq, 2)
    return _pallas(
        body, (dh2, pa, pb, zg, zg, y, yb, wat, wbt, wout), name="merge_bwd", grid=(t // TM,),
        out_shape=(bf(D_MODEL), bf(D_MODEL), bf(D_MODEL), bf(D_MODEL), lanes, bf(2 * GW), bf(D_MODEL),
                   lanes, jax.ShapeDtypeStruct((t, 2 * GW), F32)),
        in_specs=[row(D_MODEL), row(D_MODEL), row(D_MODEL), gate(0), gate(1), row(GW), row(2 * GW),
                  full(wat), full(wbt), full(wout)],
        out_specs=(row(D_MODEL), row(D_MODEL), row(D_MODEL), row(D_MODEL), lane_spec, row(2 * GW), row(D_MODEL),
                   lane_spec, row(2 * GW)),
        sem=("parallel",), vmem=VMEM_BIG, rider=rider)


def _lane_head(rows):
    return lax.broadcasted_iota(jnp.int32, (rows, GW), 1) // HEAD_DIM


def _kv_expand_matrix(r):
    ci = lax.broadcasted_iota(jnp.int32, (2 * HEAD_DIM, GW), 0)
    ji = lax.broadcasted_iota(jnp.int32, (2 * HEAD_DIM, GW), 1)
    return (ci == (ji % HEAD_DIM) + HEAD_DIM * r).astype(BF16)


def _block_rows(row0, stride, ib):
    start = row0 + (stride * BLOCK) * ib
    if stride > 1:
        return pl.ds(start, BLOCK, stride=stride)
    return pl.ds(pl.multiple_of(start, BLOCK), BLOCK)


def _stack_heads(x, lane_head):
    return jnp.concatenate([jnp.where(lane_head == h, x, jnp.zeros_like(x)) for h in range(4)], axis=0)


def _unstack_heads(x4, lane_head):
    out = jnp.zeros((BLOCK, GW), F32)
    for h in range(4):
        out = jnp.where(lane_head == h, x4[h * BLOCK:(h + 1) * BLOCK], out)
    return out


def _load_rows(ref, rows, split):
    if split:
        return jnp.concatenate([ref[0, 0, rows, :], ref[0, 1, rows, :]], axis=1)
    return ref[0, rows, :]


def _store_rows(ref, rows, val, split):
    if split:
        ref[0, 0, rows, :] = val[:, :128]
        ref[0, 1, rows, :] = val[:, 128:]
    else:
        ref[0, rows, :] = val


def _attn_fwd(q_arr, k_arr, v_arr, bias, sink, *, grid, seq, stride, kvw, split, q_spec, k_spec, v_spec, bias_map,
              sink_map, o_spec, has_sink, o_shape, o_dtype, name, rider=None):
    nb = seq // stride // BLOCK
    scale = HEAD_DIM ** -0.5
    expanded = kvw != GW
    rps = min(stride, RESIDUES_PER_STEP)
    grid = (grid[0], grid[1] // rps)
    assert not has_sink or B_WINDOW - 1 < BLOCK

    def body(q_ref, k_ref, v_ref, bias_ref, sink_ref, o_ref, lse_ref, *kv_x):
        rr = pl.program_id(1)
        lane_head = _lane_head(BLOCK)
        if expanded:
            expand = _kv_expand_matrix(rr)
            kv_x[0][...] = _dot(k_ref[0], expand).astype(BF16)
            kv_x[1][...] = _dot(v_ref[0], expand).astype(BF16)
        for j in range(rps):
            residue(rr * rps + j if stride > 1 else 0, q_ref, k_ref, v_ref, bias_ref, sink_ref, o_ref, lse_ref, kv_x,
                    lane_head)

    def residue(row0, q_ref, k_ref, v_ref, bias_ref, sink_ref, o_ref, lse_ref, kv_x, lane_head):
        def per_head(fn, x):
            return jnp.concatenate([fn(sink_ref[0, h:h + 1, 0:1], x[h * BLOCK:(h + 1) * BLOCK]) for h in range(4)],
                                   axis=0)

        def load(ref, ib):
            return _load_rows(ref, _block_rows(row0, stride, ib), split).astype(BF16)

        def load_kv(which, ib):
            if expanded:
                return kv_x[which][_block_rows(0, 1, ib), :]
            return load((k_ref, v_ref)[which], ib)

        def block(ib, first):
            q4 = _stack_heads(load(q_ref, ib), lane_head)
            if first:
                kc, vc = load_kv(0, ib), load_kv(1, ib)
                b4 = bias_ref[:, :, BLOCK:].reshape(4 * BLOCK, BLOCK)
            else:
                kc = jnp.concatenate([load_kv(0, ib - 1), load_kv(0, ib)], axis=0)
                vc = jnp.concatenate([load_kv(1, ib - 1), load_kv(1, ib)], axis=0)
                b4 = bias_ref[...].reshape(4 * BLOCK, 2 * BLOCK)
                if has_sink:
                    oldest = lax.broadcasted_iota(jnp.int32, kc.shape, 0) == 0
                    kc = jnp.where(oldest, jnp.zeros_like(kc), kc)
                    vc = jnp.where(oldest, jnp.zeros_like(vc), vc)
            s = _dot_nt(q4, kc) * scale + b4
            m = jnp.max(s, axis=-1, keepdims=True)
            if has_sink and first:
                m = per_head(jnp.maximum, m)
            p = jnp.exp(s - m)
            l = jnp.sum(p, axis=-1, keepdims=True)
            if has_sink and first:
                l = l + per_head(lambda sk, mh: jnp.exp(sk - mh), m)
            o4 = _dot(p.astype(BF16), vc) / l
            rows = _block_rows(row0, stride, ib)
            _store_rows(o_ref, rows, _unstack_heads(o4, lane_head).astype(o_dtype), split)
            _store_rows(lse_ref, rows, _unstack_heads(m + jnp.log(l), lane_head), split)

        block(0, True)
        if nb > 1:
            def step(i, carry):
                block(i, False)
                return carry
            lax.fori_loop(1, nb, step, 0, unroll=min(ATTN_UNROLL, nb - 1))

    return _pallas(
        body, (q_arr, k_arr, v_arr, bias, sink), name=name, grid=grid,
        out_shape=(jax.ShapeDtypeStruct(o_shape, o_dtype), jax.ShapeDtypeStruct(o_shape, F32)),
        in_specs=[q_spec, k_spec, v_spec,
                  pl.BlockSpec((4, BLOCK, 2 * BLOCK), bias_map), pl.BlockSpec((1, 4, 128), sink_map)],
        out_specs=(o_spec, o_spec),
        scratch_shapes=[pltpu.VMEM((seq, GW), BF16)] * 2 if expanded else [],
        sem=("arbitrary", "arbitrary"), vmem=VMEM_BIG, rider=rider)


def _attn_bwd(q_arr, k_arr, v_arr, bias, sink, dy, cc, lse, *, grid, seq, stride, kvw, split, q_spec, k_spec, v_spec,
              bias_map, sink_map, o_spec, kv_out_spec, has_sink, n_bias, dq_shape, dkv_shape, g_dtype, name):
    ln = seq // stride
    nb = ln // BLOCK
    scale = HEAD_DIM ** -0.5
    expanded = kvw != GW
    rps = min(stride, RESIDUES_PER_STEP)
    grid = (grid[0], grid[1] // rps)

    def body(q_ref, k_ref, v_ref, bias_ref, sink_ref, dy_ref, c_ref, lse_ref,
             dq_ref, dk_ref, dv_ref, db_ref, dsk_ref, dk_acc, dv_acc, dk_half, dv_half, *kv_x):
        rr = pl.program_id(1)

        @pl.when((pl.program_id(0) == 0) & (rr == 0))
        def _():
            db_ref[...] = jnp.zeros_like(db_ref)
            dsk_ref[...] = jnp.zeros_like(dsk_ref)

        if expanded:
            expand = _kv_expand_matrix(rr)
            kv_x[0][...] = _dot(k_ref[0], expand).astype(BF16)
            kv_x[1][...] = _dot(v_ref[0], expand).astype(BF16)
        refs = (q_ref, k_ref, v_ref, bias_ref, sink_ref, dy_ref, c_ref, lse_ref, dq_ref, dk_ref, dv_ref, db_ref,
                dsk_ref, dk_acc, dv_acc, dk_half, dv_half, kv_x)
        for j in range(rps):
            residue(rr, rr * rps + j if stride > 1 else 0, *refs)

    def residue(rr, row0, q_ref, k_ref, v_ref, bias_ref, sink_ref, dy_ref, c_ref, lse_ref,
                dq_ref, dk_ref, dv_ref, db_ref, dsk_ref, dk_acc, dv_acc, dk_half, dv_half, kv_x):
        dk_acc[...] = jnp.zeros_like(dk_acc)
        dv_acc[...] = jnp.zeros_like(dv_acc)
        lane_head = _lane_head(BLOCK)
        hb = 4 * rr if n_bias == 8 else 0

        def load(ref, ib):
            return _load_rows(ref, _block_rows(row0, stride, ib), split)

        def load_kv(which, ib):
            if expanded:
                return kv_x[which][_block_rows(0, 1, ib), :]
            return load((k_ref, v_ref)[which], ib).astype(BF16)

        def head_col(x):
            return jnp.concatenate([x[:, h * HEAD_DIM:h * HEAD_DIM + 1] for h in range(4)], axis=0)

        def block(ib, first):
            q4 = _stack_heads(load(q_ref, ib).astype(BF16), lane_head)
            dy4 = _stack_heads(load(dy_ref, ib).astype(BF16), lane_head)
            c4 = head_col(load(c_ref, ib))
            l4 = head_col(load(lse_ref, ib))
            if first:
                kc, vc = load_kv(0, ib), load_kv(1, ib)
                b4 = bias_ref[:, :, BLOCK:].reshape(4 * BLOCK, BLOCK)
                krows = pl.ds(0, BLOCK)
            else:
                kc = jnp.concatenate([load_kv(0, ib - 1), load_kv(0, ib)], axis=0)
                vc = jnp.concatenate([load_kv(1, ib - 1), load_kv(1, ib)], axis=0)
                b4 = bias_ref[...].reshape(4 * BLOCK, 2 * BLOCK)
                krows = pl.ds(pl.multiple_of((ib - 1) * BLOCK, BLOCK), 2 * BLOCK)
            nk = BLOCK if first else 2 * BLOCK
            p = jnp.exp(_dot_nt(q4, kc) * scale + b4 - l4)
            ds = p * (_dot_nt(dy4, vc) - c4)
            ds3 = ds.reshape(4, BLOCK, nk)
            if n_bias == 8:
                if first:
                    db_ref[pl.ds(hb, 4), :, BLOCK:] += ds3
                else:
                    db_ref[pl.ds(hb, 4)] += ds3
            elif first:
                db_ref[:, :, BLOCK:] += ds3
            else:
                db_ref[...] += ds3
            ds16 = ds.astype(BF16)
            dq = _unstack_heads(_dot(ds16, kc), lane_head) * scale
            _store_rows(dq_ref, _block_rows(row0, stride, ib), dq.astype(g_dtype), split)
            dk_acc[krows, :] += _dot_tn(ds16, q4) * scale
            dv_acc[krows, :] += _dot_tn(p.astype(BF16), dy4)
            if has_sink:
                for h in range(4):
                    hs = slice(h * BLOCK, (h + 1) * BLOCK)
                    sk = sink_ref[0, h:h + 1, 0:1]
                    val = -jnp.sum(jnp.exp(sk - l4[hs]) * c4[hs], axis=0, keepdims=True)
                    dsk_ref[hb + h] += jnp.broadcast_to(val, (8, 128))

        block(0, True)
        if nb > 1:
            def step(i, carry):
                block(i, False)
                return carry
            lax.fori_loop(1, nb, step, 0, unroll=min(ATTN_UNROLL, nb - 1))

        if kvw == GW:
            all_rows = pl.ds(row0, ln, stride=stride) if stride > 1 else pl.ds(0, ln)
            _store_rows(dk_ref, all_rows, dk_acc[...].astype(g_dtype), split)
            _store_rows(dv_ref, all_rows, dv_acc[...].astype(g_dtype), split)
        else:
            def fold(acc):
                t2 = acc[:, :2 * HEAD_DIM] + acc[:, 2 * HEAD_DIM:]
                t2 = t2 + pltpu.roll(t2, HEAD_DIM, 1)
                lane = lax.broadcasted_iota(jnp.int32, t2.shape, 1) // HEAD_DIM
                return jnp.where(lane == rr, t2, 0.0)

            @pl.when(rr == 0)
            def _():
                dk_half[...] = fold(dk_acc[...])
                dv_half[...] = fold(dv_acc[...])

            @pl.when(rr == 1)
            def _():
                dk_ref[0] = (dk_half[...] + fold(dk_acc[...])).astype(g_dtype)
                dv_ref[0] = (dv_half[...] + fold(dv_acc[...])).astype(g_dtype)

    return pl.pallas_call(
        body, name=name, grid=grid,
        out_shape=(jax.ShapeDtypeStruct(dq_shape, g_dtype), jax.ShapeDtypeStruct(dkv_shape, g_dtype),
                   jax.ShapeDtypeStruct(dkv_shape, g_dtype), jax.ShapeDtypeStruct((n_bias, BLOCK, 2 * BLOCK), F32),
                   jax.ShapeDtypeStruct((8, 8, 128), F32)),
        in_specs=[q_spec, k_spec, v_spec,
                  pl.BlockSpec((4, BLOCK, 2 * BLOCK), bias_map), pl.BlockSpec((1, 4, 128), sink_map),
                  o_spec, o_spec, o_spec],
        out_specs=(o_spec, kv_out_spec, kv_out_spec,
                   pl.BlockSpec((n_bias, BLOCK, 2 * BLOCK), lambda n, r: (0, 0, 0)),
                   pl.BlockSpec((8, 8, 128), lambda n, r: (0, 0, 0))),
        scratch_shapes=[pltpu.VMEM((ln, GW), F32), pltpu.VMEM((ln, GW), F32),
                        pltpu.VMEM((ln, 2 * HEAD_DIM), F32), pltpu.VMEM((ln, 2 * HEAD_DIM), F32)]
        + ([pltpu.VMEM((seq, GW), BF16)] * 2 if expanded else []),
        compiler_params=_params(("arbitrary", "arbitrary"), VMEM_BIG),
    )(q_arr, k_arr, v_arr, bias, sink, dy, cc, lse)


def _bias_grad(ds_all, buckets):
    def body(ds_ref, bk_ref, o_ref):
        rows = lax.broadcasted_iota(jnp.int32, (N_BUCKETS, 128), 0)
        cols = lax.broadcasted_iota(jnp.int32, (N_BUCKETS, 128), 1)

        def per_bucket(b, acc):
            for h in range(20):
                gi = h // 4 if h < 12 else 3
                v = jnp.where(bk_ref[gi] == b, ds_ref[h], 0.0)
                v = jnp.sum(jnp.sum(v, axis=1, keepdims=True), axis=0, keepdims=True)
                acc = jnp.where((rows == b) & (cols == h), v, acc)
            return acc

        o_ref[...] = lax.fori_loop(0, N_BUCKETS, per_bucket, jnp.zeros((N_BUCKETS, 128), F32))

    vm = pl.BlockSpec(memory_space=pltpu.VMEM)
    return pl.pallas_call(body, name="bias_grad", out_shape=jax.ShapeDtypeStruct((N_BUCKETS, 128), F32),
                          in_specs=[vm, vm], out_specs=vm)(ds_all, buckets)


def _adamw(w, g, m, v, name):
    r, c = w.shape
    tr = r
    for cand in (256, 176, 128, 64, 32, 16, 8):
        if r % cand == 0:
            tr = cand
            break
    bc1 = 1.0 - ADAM_B1 ** ADAM_STEP
    bc2 = 1.0 - ADAM_B2 ** ADAM_STEP

    def body(w_ref, g_ref, m_ref, v_ref, d_ref, nm_ref, nv_ref):
        gv = g_ref[...]
        nm = ADAM_B1 * m_ref[...] + (1.0 - ADAM_B1) * gv
        nv = ADAM_B2 * v_ref[...] + (1.0 - ADAM_B2) * (gv * gv)
        nm_ref[...] = nm
        nv_ref[...] = nv
        d_ref[...] = -ADAM_LR * ((nm / bc1) / (jnp.sqrt(nv / bc2) + ADAM_EPS) + ADAM_WD * w_ref[...])

    spec = pl.BlockSpec((tr, c), lambda i: (i, 0))
    shp = jax.ShapeDtypeStruct((r, c), F32)
    return pl.pallas_call(body, name=name, grid=(r // tr,), out_shape=(shp, shp, shp),
                          in_specs=[spec] * 4, out_specs=(spec, spec, spec),
                          compiler_params=_params(("parallel",)))(w, g, m, v)


def _t5_bucket(dist):
    max_exact = N_BUCKETS // 2
    n = jnp.maximum(dist, 0)
    nf = jnp.maximum(n, 1).astype(F32)
    large = max_exact + (jnp.log(nf / max_exact) / math.log(MAX_DISTANCE / max_exact)
                         * (N_BUCKETS - max_exact)).astype(jnp.int32)
    large = jnp.minimum(large, N_BUCKETS - 1)
    return jnp.where(n < max_exact, n, large)


def _bias_tables(rel_bias):
    qi = jnp.arange(BLOCK)[:, None]
    ki = jnp.arange(2 * BLOCK)[None, :]
    dist = qi + BLOCK - ki
    specs = [(d, w // d, 4 * gi, 4 * gi + 4) for gi, (w, d) in enumerate(DIL_GROUPS)] + [(1, B_WINDOW - 1, 12, 20)]
    biases, buckets = [], []
    for stride, steps, h0, h1 in specs:
        valid = (dist >= 0) & (dist <= steps)
        bk = jnp.where(valid, _t5_bucket(dist * stride), -1).astype(jnp.int32)
        onehot = (bk[None, :, :] == jnp.arange(N_BUCKETS, dtype=jnp.int32)[:, None, None]).astype(F32)
        b = jnp.einsum("bqk,bh->hqk", onehot, rel_bias[:, h0:h1], precision=lax.Precision.HIGHEST)
        biases.append(jnp.where(valid[None], b, NEG))
        buckets.append(bk)
    return jnp.concatenate(biases, axis=0), jnp.stack(buckets, axis=0)


def _local_step(x, tgt, W, S, shards=None):
    nseq, seq, _ = x.shape
    t = nseq * seq
    xf = x.reshape(t, D_MODEL)
    bias_all, buckets = _bias_tables(S["rel_bias"])
    sink_b = jnp.broadcast_to(S["sinks"].reshape(2, 4, 1), (2, 4, 128)).astype(F32)
    sink_0 = jnp.zeros((1, 4, 128), F32)
    dist = shards is not None
    W = dict(W)
    G, GS, reduced = {}, {}, {}

    def put(keys, gathered):
        for k, g in zip(keys, gathered):
            W[k] = g.reshape(_FULL_SHAPE.get(k, (N_CHIPS * shards[k].shape[0], D_MODEL)))

    def gather_rider(keys):
        return _GatherRider([shards[k] for k in keys]) if dist else None

    def pair(keys):
        return _pair_reduce([G[k].reshape(N_CHIPS, 2, shards[k].shape[0] // 2, D_MODEL) for k in keys],
                            "grad_pair_reduce_" + keys[0])

    def finish(keys, own, rec):
        full = _final_reduce(own, rec, "grad_final_reduce_" + keys[0])
        off = 0
        for k in keys:
            r = shards[k].shape[0]
            reduced[k] = full[:, off:off + r // 2].reshape(r, D_MODEL)
            off += r // 2

    if dist:
        first = ("wgt1", "wut1", "wd1")
        put(first, _gather_rows([shards[k] for k in first]))
    keys = ("wint",)
    (h1, n1, g1, u1), ro = _ffn_fwd(xf, S["ffn1_norm"], W["wgt1"], W["wut1"], W["wd1"], rider=gather_rider(keys))
    put(keys, ro)
    keys = ("wout", "wat", "wbt", "wgt2")
    (un, za, zb, zg), ro = _inproj_fwd(h1, S["mix_norm"], W["wint"], S["b_in"], nseq, rider=gather_rider(keys))
    put(keys, ro)

    seq3 = lambda a: a.reshape(nseq, seq, a.shape[-1])
    zb3 = seq3(zb)
    pair_blk = lambda cb: pl.BlockSpec((1, 2, seq, 128), lambda n, r, cb=cb: (n, cb, 0, 0))
    a_cfg = []
    outs, lses = [], []
    for gi, (_, d) in enumerate(DIL_GROUPS):
        cfg = dict(grid=(nseq, d), seq=seq, stride=d, kvw=GW, split=True,
                   q_spec=pair_blk(gi), k_spec=pair_blk(3 + gi), v_spec=pair_blk(6 + gi), o_spec=pair_blk(0),
                   bias_map=lambda n, r: (0, 0, 0), sink_map=lambda n, r: (0, 0, 0), has_sink=False)
        a_cfg.append(cfg)
        (o, lse), _ = _attn_fwd(za, za, za, bias_all[4 * gi:4 * gi + 4], sink_0, o_shape=(nseq, 2, seq, 128),
                                o_dtype=F32, name=f"attn_a{gi}_fwd", **cfg)
        outs.append(o)
        lses.append(lse)
    wide_blk = lambda w, cmap: pl.BlockSpec((1, seq, w), cmap)
    b_cfg = dict(grid=(nseq, 2), seq=seq, stride=1, kvw=2 * HEAD_DIM, split=False,
                 q_spec=wide_blk(GW, lambda n, r: (n, 0, r)), k_spec=wide_blk(2 * HEAD_DIM, lambda n, r: (n, 0, 4)),
                 v_spec=wide_blk(2 * HEAD_DIM, lambda n, r: (n, 0, 5)), o_spec=wide_blk(GW, lambda n, r: (n, 0, r)),
                 bias_map=lambda n, r: (r, 0, 0), sink_map=lambda n, r: (r, 0, 0), has_sink=True)
    keys = ("wut2",)
    bias_b_fwd = bias_all[12:20].at[:, :, 0].set(jnp.broadcast_to(S["sinks"].reshape(8, 1), (8, BLOCK)))
    (yb, lse_b), ro = _attn_fwd(zb3, zb3, zb3, bias_b_fwd, sink_b, o_shape=(nseq, seq, 2 * GW), o_dtype=BF16,
                                name="attn_b_fwd", rider=gather_rider(keys), **b_cfg)
    put(keys, ro)
    yb = yb.reshape(t, 2 * GW)

    keys = ("wd2",)
    (h2, y, lse_tot, pa, pb, merged), ro = _merge_fwd(outs[0], outs[1], outs[2], lses[0], lses[1], lses[2], yb, zg, h1,
                                                      W["wat"], W["wbt"], W["wout"], rider=gather_rider(keys))
    put(keys, ro)
    (dh3, n2, g2, u2, loss_part, g_final), _ = _ffn_fwd(
        h2, S["ffn2_norm"], W["wgt2"], W["wut2"], W["wd2"],
        head=(S["final_norm"].reshape(1, D_MODEL), tgt.reshape(t, D_MODEL)))

    GS["final_norm"] = g_final
    dh2, dg2, du2, a2, df2, GS["ffn2_norm"] = _ffn_bwd(dh3, h2, S["ffn2_norm"], g2, u2, W["wgt2"], W["wut2"], W["wd2"])
    G["wgt2"] = _wgrad(dg2, n2, MXU_DIM, name="wgrad_gate2")
    G["wut2"] = _wgrad(du2, n2, MXU_DIM, name="wgrad_up2")
    G["wd2"] = _wgrad(a2, df2, MXU_DIM, name="wgrad_down2")

    keys = ("wgt2", "wut2", "wd2")
    rider = _ExchangeRider([pair(keys)]) if dist else None
    (dpa, dpb, dga, dgb, dya, dyb, dh2b, ca, cb), ro = _merge_bwd(dh2, pa, pb, zg, y, yb, W["wat"], W["wbt"], W["wout"],
                                                                  nseq, rider=rider)
    if dist:
        finish(keys, *ro)

    dqs, dks, dvs, dbs = [], [], [], []
    shp = (nseq, 2, seq, 128)
    halves = lambda a: [a[:, hf].reshape(t, 128).astype(BF16) for hf in range(2)]
    for gi in range(len(DIL_GROUPS)):
        dq, dk, dv, db, _ = _attn_bwd(za, za, za, bias_all[4 * gi:4 * gi + 4], sink_0, dya, ca, lse_tot,
                                      n_bias=4, dq_shape=shp, dkv_shape=shp, g_dtype=F32,
                                      kv_out_spec=a_cfg[gi]["o_spec"], name=f"attn_a{gi}_bwd", **a_cfg[gi])
        dqs += halves(dq)
        dks += halves(dk)
        dvs += halves(dv)
        dbs.append(db)
    dqb, dkb, dvb, dbb, dsink = _attn_bwd(zb3, zb3, zb3, bias_all[12:20], sink_b, seq3(dyb), seq3(cb), lse_b,
                                          n_bias=8, dq_shape=(nseq, seq, 2 * GW),
                                          dkv_shape=(nseq, seq, 2 * HEAD_DIM), g_dtype=BF16,
                                          kv_out_spec=wide_blk(2 * HEAD_DIM, lambda n, r: (n, 0, 0)),
                                          name="attn_b_bwd", **b_cfg)
    dz = jnp.concatenate(dqs + dks + dvs + [dqb.reshape(t, 2 * GW), dkb.reshape(t, 2 * HEAD_DIM),
                                            dvb.reshape(t, 2 * HEAD_DIM), dga, dgb], axis=-1)
    gb_tab = _bias_grad(jnp.concatenate(dbs + [dbb], axis=0), buckets)
    if dist:
        GS["bias_tab"], GS["sink_tiles"] = gb_tab, dsink
    else:
        GS["rel_bias"] = gb_tab[:, :20]
        GS["sinks"] = dsink[:, 0, 0].reshape(1, 8)

    G["wint"], GS["b_in"] = _wgrad(dz, un, MXU_DIM, with_colsum=True, name="wgrad_in")
    keys = ("wint",)
    rider = _ExchangeRider([pair(keys)]) if dist else None
    (dh1, GS["mix_norm"]), ro = _inproj_bwd(dz, dh2, h1, S["mix_norm"], W["wint"], rider=rider)
    if dist:
        finish(keys, *ro)

    dx, dg1, du1, a1, df1, GS["ffn1_norm"] = _ffn_bwd(dh1, xf, S["ffn1_norm"], g1, u1, W["wgt1"], W["wut1"], W["wd1"])
    G["wgt1"] = _wgrad(dg1, n1, MXU_DIM, name="wgrad_gate1")
    if dist:
        G["wut1"], ro = _wgrad(du1, n1, MXU_DIM, name="wgrad_up1", rider=_ExchangeRider([pair(("wgt1",))]))
        finish(("wgt1",), *ro)
        G["wd1"], ro = _wgrad(a1, df1, MXU_DIM, name="wgrad_down1", rider=_ExchangeRider([pair(("wut1",))]))
        finish(("wut1",), *ro)
        G["wout"], ro = _wgrad(merged, dh2b, MXU_DIM, name="wgrad_out", rider=_ExchangeRider([pair(("wd1",))]))
        finish(("wd1",), *ro)
    else:
        G["wut1"] = _wgrad(du1, n1, MXU_DIM, name="wgrad_up1")
        G["wd1"] = _wgrad(a1, df1, MXU_DIM, name="wgrad_down1")
        G["wout"] = _wgrad(merged, dh2b, MXU_DIM, name="wgrad_out")
    G["wat"] = _wgrad(dpa, y, MXU_DIM, name="wgrad_branch_a")
    G["wbt"] = _wgrad(dpb, yb, MXU_DIM, name="wgrad_branch_b")
    if dist:
        keys = ("wout", "wat", "wbt")
        finish(keys, *_chip_exchange([pair(keys)]))
    return loss_part, dx.reshape(x.shape), (reduced if dist else G), GS


_SMALL = ("ffn1_norm", "mix_norm", "ffn2_norm", "final_norm", "b_in", "sinks", "rel_bias")
_ORDER = ("ffn1_norm", "ffn1_w_gate", "ffn1_w_up", "ffn1_w_down", "mix_norm", "w_in", "b_in", "w_branch_a",
          "w_branch_b", "w_out", "sinks", "rel_bias", "ffn2_norm", "ffn2_w_gate", "ffn2_w_up", "ffn2_w_down",
          "final_norm")
_BIG = (("wgt1", "ffn1_w_gate", True, 704), ("wut1", "ffn1_w_up", True, 704), ("wd1", "ffn1_w_down", False, 704),
        ("wint", "w_in", True, 1280), ("wout", "w_out", False, 256), ("wat", "w_branch_a", True, 64),
        ("wbt", "w_branch_b", True, 128), ("wgt2", "ffn2_w_gate", True, 704), ("wut2", "ffn2_w_up", True, 704),
        ("wd2", "ffn2_w_down", False, 704))
_FULL_SHAPE = {"wat": (D_MODEL, GW), "wbt": (D_MODEL, 2 * GW)}


def kernel(x, ffn1_norm, ffn1_w_gate, ffn1_w_up, ffn1_w_down, mix_norm, w_in, b_in, w_branch_a, w_branch_b, w_out, sinks, rel_bias, ffn2_norm, ffn2_w_gate, ffn2_w_up, ffn2_w_down, final_norm, loss_target, m_ffn1_norm, m_ffn1_w_gate, m_ffn1_w_up, m_ffn1_w_down, m_mix_norm, m_w_in, m_b_in, m_w_branch_a, m_w_branch_b, m_w_out, m_sinks, m_rel_bias, m_ffn2_norm, m_ffn2_w_gate, m_ffn2_w_up, m_ffn2_w_down, m_final_norm, v_ffn1_norm, v_ffn1_w_gate, v_ffn1_w_up, v_ffn1_w_down, v_mix_norm, v_w_in, v_b_in, v_w_branch_a, v_w_branch_b, v_w_out, v_sinks, v_rel_bias, v_ffn2_norm, v_ffn2_w_gate, v_ffn2_w_up, v_ffn2_w_down, v_final_norm):
    args = dict(locals())
    w = {n: args[n] for n in _ORDER}
    m = {n: args["m_" + n] for n in _ORDER}
    v = {n: args["v_" + n] for n in _ORDER}

    shards = {}
    for key, name, transposed, rows in _BIG:
        a = w[name][0]
        a = (a.T if transposed else a).astype(BF16)
        shards[key] = a.reshape(rows, D_MODEL)
    S = {n: w[n] for n in _SMALL}

    loss_part, grad_x, reduced, GS = _local_step(x, loss_target, {}, S, shards)

    small = _allreduce_small(GS["ffn1_norm"], GS["mix_norm"], GS["ffn2_norm"], GS["final_norm"], GS["b_in"],
                             GS["sink_tiles"], GS["bias_tab"], loss_part)
    loss = small[9, 8]

    out_g, out_d, out_m, out_v = {}, {}, {}, {}
    for key, n, transposed, rows in _BIG:
        nat = w[n][0].shape
        if transposed and nat[1] % 128:
            res = _adamw(w[n][0].T, reduced[key], m[n][0].T, v[n][0].T, "adamw_" + n)
            res = [reduced[key].T] + [r.T for r in res]
        else:
            g = reduced[key].reshape(nat[1], nat[0]).T if transposed else reduced[key].reshape(nat)
            res = [g] + list(_adamw(w[n][0], g, m[n][0], v[n][0], "adamw_" + n))
        out_g[n], out_d[n], out_m[n], out_v[n] = [r[None] for r in res]
    row = lambda d: {n: (d[n].reshape(1, D_MODEL) if n == "final_norm" else d[n]) for n in _SMALL}
    for dst, src in zip((out_g, out_d, out_m, out_v), _adamw_small(small, row(w), row(m), row(v))):
        dst.update(src)
        dst["final_norm"] = src["final_norm"].reshape(D_MODEL)

    return (loss, grad_x, *[out_g[n] for n in _ORDER], *[out_d[n] for n in _ORDER],
            *[out_m[n] for n in _ORDER], *[out_v[n] for n in _ORDER])
```

```python
import math

import jax
import jax.numpy as jnp
from jax import lax
from jax.experimental import pallas as pl
from jax.experimental.pallas import tpu as pltpu

F32, BF16 = jnp.float32, jnp.bfloat16
MESH = pl.DeviceIdType.MESH

D_MODEL = 1024
D_FF = 2816
D_IN = 5120
HEAD_DIM = 64
BLOCK = 128
DIL_GROUPS = ((128, 1), (512, 4), (2048, 16))
B_WINDOW = 128
N_BUCKETS = 32
MAX_DISTANCE = 2048
EPS = 1e-6
N_CHIPS = 4
GW = 256
ZA_W = 2304
ZB_W = 768
NEG = -1e30

ADAM_LR, ADAM_B1, ADAM_B2, ADAM_EPS, ADAM_WD, ADAM_STEP = 0.001, 0.9, 0.999, 1e-08, 0.01, 10

VMEM_BIG = 56 * 1024 * 1024
TM = 512
TM_BWD = 256
MXU_DIM = 256
FF_BOUNDS = (0, 4 * MXU_DIM, 8 * MXU_DIM, D_FF)
DMA_SPLIT = 8
RESIDUES_PER_STEP = 8
ATTN_UNROLL = 5


def _dot(a, b):
    return jnp.dot(a, b, preferred_element_type=F32)


def _dot_nt(a, b):
    return lax.dot_general(a, b, (((1,), (1,)), ((), ())), preferred_element_type=F32)


def _dot_tn(a, b):
    return lax.dot_general(a, b, (((0,), (0,)), ((), ())), preferred_element_type=F32)


def _sigmoid(x):
    return 0.5 * jnp.tanh(0.5 * x) + 0.5


def _params(sem, vmem=None):
    return pltpu.CompilerParams(dimension_semantics=sem, vmem_limit_bytes=vmem)


ANY = pl.BlockSpec(memory_space=pl.ANY)


def _me():
    return lax.axis_index("x"), lax.axis_index("y"), lax.axis_index("c")


_CHIP_RELS = ((1, 0), (0, 1), (1, 1))


def _flip(v, f):
    return 1 - v if f else v


def _remote(src, dst, ssem, rsem, peer):
    return pltpu.make_async_remote_copy(src_ref=src, dst_ref=dst, send_sem=ssem, recv_sem=rsem,
                                        device_id=peer, device_id_type=MESH)


def _row_pieces(rows, n):
    step = max(16, -(-rows // n) // 16 * 16)
    out, s = [], 0
    while s < rows:
        out.append((s, min(step, rows - s)))
        s += step
    return out


def _gather_rows(shards):
    nt = len(shards)
    rows = [s.shape[0] for s in shards]

    def body(*refs):
        srcs, outs = refs[:nt], refs[nt:2 * nt]
        halves, quarters = refs[2 * nt:3 * nt], refs[3 * nt:4 * nt]
        ici_s, ici_r, fwd_s, fwd_r, d2d_s, d2d_r, keep, loc = refs[4 * nt:]
        x, y, c = _me()
        j = 2 * x + y
        sib = (x, y, 1 - c)
        nbr = ((1 - x, y, c), (x, 1 - y, c))
        nbr_j = (2 * (1 - x) + y, 2 * x + (1 - y))
        diag_j = 2 * (1 - x) + (1 - y)
        local = [pltpu.make_async_copy(srcs[t], outs[t].at[j], loc.at[t]) for t in range(nt)]
        for cp in local:
            cp.start()
        pending = []
        for a in range(2):
            for t in range(nt):
                half = pl.ds(c * (rows[t] // 2), rows[t] // 2)
                cp = _remote(srcs[t].at[half], halves[t].at[a], ici_s.at[2 * t + a], ici_r.at[2 * t + a], nbr[a])
                cp.start()
                pending.append(cp)
        placed = []

        def place(src, dst_of, idx):
            mine = pltpu.make_async_copy(src, dst_of, keep.at[idx])
            mine.start()
            cp = _remote(src, dst_of, d2d_s.at[idx], d2d_r.at[idx], sib)
            cp.start()
            placed.append((mine, cp))

        for a in range(2):
            for t in range(nt):
                r2, r4 = rows[t] // 2, rows[t] // 4
                got = halves[t].at[a]
                _remote(got, got, ici_s.at[2 * t + a], ici_r.at[2 * t + a], nbr[a]).wait_recv()
                cp = _remote(halves[t].at[a, pl.ds(a * r4, r4)], quarters[t].at[a], fwd_s.at[2 * t + a],
                             fwd_r.at[2 * t + a], nbr[1 - a])
                cp.start()
                pending.append(cp)
                place(got, outs[t].at[nbr_j[a], pl.ds(c * r2, r2)], 4 * t + a)
        for a in range(2):
            for t in range(nt):
                r2, r4 = rows[t] // 2, rows[t] // 4
                got = quarters[t].at[a]
                _remote(got, got, fwd_s.at[2 * t + a], fwd_r.at[2 * t + a], nbr[1 - a]).wait_recv()
                place(got, outs[t].at[diag_j, pl.ds(c * r2 + a * r4, r4)], 4 * t + 2 + a)
        for mine, cp in placed:
            mine.wait()
            cp.wait()
        for cp in pending:
            cp.wait_send()
        for cp in local:
            cp.wait()

    stage = ([pltpu.VMEM((2, r // 2, D_MODEL), BF16) for r in rows] + [pltpu.VMEM((2, r // 4, D_MODEL), BF16) for r in rows])
    sems = ([pltpu.SemaphoreType.DMA((2 * nt,)) for _ in range(4)] + [pltpu.SemaphoreType.DMA((4 * nt,))] * 3
            + [pltpu.SemaphoreType.DMA((nt,))])
    return pl.pallas_call(
        body, name="gather_weights",
        out_shape=tuple(jax.ShapeDtypeStruct((N_CHIPS,) + s.shape, s.dtype) for s in shards),
        in_specs=[pl.BlockSpec(memory_space=pltpu.VMEM)] * nt,
        out_specs=tuple([ANY] * nt), scratch_shapes=stage + sems,
    )(*shards)


VMEM_WHOLE = pl.BlockSpec(memory_space=pltpu.VMEM)


def _pair_reduce(grads, name):
    nt = len(grads)
    r2 = [g.shape[2] for g in grads]
    off = [sum(r2[:t]) for t in range(nt)]
    tot = sum(r2)

    def body(*refs):
        gs = refs[:nt]
        s_ref, got, ssem, rsem = refs[nt:]
        x, y, c = _me()
        sib = (x, y, 1 - c)
        for t in range(nt):
            for k in range(N_CHIPS):
                _remote(gs[t].at[k, 1 - c], got.at[k, pl.ds(off[t], r2[t])], ssem, rsem, sib).start()
        _remote(got, got, ssem, rsem, sib).wait()
        for t in range(nt):
            for k in range(N_CHIPS):
                rows = slice(off[t], off[t] + r2[t])
                s_ref[k, rows, :] = (gs[t][k, c].astype(F32) + got[k, rows, :].astype(F32)).astype(BF16)

    shp = jax.ShapeDtypeStruct((N_CHIPS, tot, D_MODEL), BF16)
    return pl.pallas_call(
        body, name=name, out_shape=shp, in_specs=[VMEM_WHOLE] * nt, out_specs=VMEM_WHOLE,
        scratch_shapes=[pltpu.VMEM((N_CHIPS, tot, D_MODEL), BF16), pltpu.SemaphoreType.DMA(()),
                        pltpu.SemaphoreType.DMA(())],
        compiler_params=pltpu.CompilerParams(vmem_limit_bytes=VMEM_BIG),
    )(*grads)


def _chip_exchange(parts):
    ng = len(parts)
    r2 = [p.shape[1] for p in parts]
    off = [sum(r2[:g]) for g in range(ng)]
    tot = sum(r2)

    def body(*refs):
        ps = refs[:ng]
        own_ref, rec_ref, ssems, rsems, lsem = refs[ng:]
        x, y, c = _me()
        j = 2 * x + y
        for g in range(ng):
            pltpu.make_async_copy(ps[g].at[j], own_ref.at[pl.ds(off[g], r2[g])], lsem).start()
        for k, (fx, fy) in enumerate(_CHIP_RELS):
            px, py = _flip(x, fx), _flip(y, fy)
            for g in range(ng):
                for st, sz in _row_pieces(r2[g], 2):
                    _remote(ps[g].at[2 * px + py, pl.ds(st, sz)], rec_ref.at[k, pl.ds(off[g] + st, sz)],
                            ssems.at[k], rsems.at[k], (px, py, c)).start()
        for k in range(3):
            _remote(rec_ref.at[k], rec_ref.at[k], ssems.at[k], rsems.at[k], (x, y, c)).wait()
        pltpu.make_async_copy(own_ref, own_ref, lsem).wait()

    return pl.pallas_call(
        body, name="grad_chip_exchange",
        out_shape=(jax.ShapeDtypeStruct((tot, D_MODEL), BF16), jax.ShapeDtypeStruct((3, tot, D_MODEL), BF16)),
        in_specs=[VMEM_WHOLE] * ng, out_specs=(ANY, ANY),
        scratch_shapes=[pltpu.SemaphoreType.DMA((3,)), pltpu.SemaphoreType.DMA((3,)), pltpu.SemaphoreType.DMA(())],
    )(*parts)


def _final_reduce(own, rec, name):
    r2 = own.shape[0]
    pieces = _row_pieces(r2, DMA_SPLIT)

    def body(own_ref, rec_ref, o_ref, fbuf, ssem, rsem, lsem):
        x, y, c = _me()
        sib = (x, y, 1 - c)
        for st, sz in pieces:
            rows = slice(st, st + sz)
            fbuf[rows, :] = (own_ref[rows, :].astype(F32) + rec_ref[0, rows, :].astype(F32)
                             + rec_ref[1, rows, :].astype(F32) + rec_ref[2, rows, :].astype(F32))
            pltpu.make_async_copy(fbuf.at[pl.ds(st, sz)], o_ref.at[c, pl.ds(st, sz)], lsem).start()
            _remote(fbuf.at[pl.ds(st, sz)], o_ref.at[c, pl.ds(st, sz)], ssem, rsem, sib).start()
        _remote(fbuf, o_ref.at[c], ssem, rsem, sib).wait()
        pltpu.make_async_copy(fbuf, o_ref.at[c], lsem).wait()

    return pl.pallas_call(
        body, name=name, out_shape=jax.ShapeDtypeStruct((2, r2, D_MODEL), F32),
        in_specs=[VMEM_WHOLE, VMEM_WHOLE], out_specs=ANY,
        scratch_shapes=[pltpu.VMEM((r2, D_MODEL), F32), pltpu.SemaphoreType.DMA(()), pltpu.SemaphoreType.DMA(()),
                        pltpu.SemaphoreType.DMA(())],
        compiler_params=pltpu.CompilerParams(vmem_limit_bytes=VMEM_BIG),
    )(own, rec)


SMALL_ROWS = 48


def _allreduce_small(g_ffn1, g_mix, g_ffn2, g_final, g_bin, dsink, bias_tab, loss_part):
    def body(f1_ref, mx_ref, f2_ref, fn_ref, bi_ref, sk_ref, bt_ref, ls_ref, o_ref, mine, buf, send_sems, recv_sems):
        x, y, c = _me()
        me = 4 * x + 2 * y + c
        mine[...] = jnp.zeros_like(mine)
        for r, ref in enumerate((f1_ref, mx_ref, f2_ref, fn_ref)):
            mine[r:r + 1, :] = ref[...]
        for k in range(D_IN // D_MODEL):
            mine[4 + k:5 + k, :] = bi_ref[:, k * D_MODEL:(k + 1) * D_MODEL]
        lane = lax.broadcasted_iota(jnp.int32, (1, 128), 1)
        row = jnp.where(lane == 8, ls_ref[0:1, :], 0.0)
        for h in range(8):
            row = jnp.where(lane == h, sk_ref[h, 0:1, :], row)
        mine[9:10, 0:128] = row
        mine[16:48, 0:128] = bt_ref[...]
        buf[me] = mine[...]
        copies = []
        for k in range(1, 8):
            peer = (_flip(x, (k >> 2) & 1), _flip(y, (k >> 1) & 1), _flip(c, k & 1))
            cp = _remote(mine, buf.at[me], send_sems.at[k - 1], recv_sems.at[k - 1], peer)
            cp.start()
            copies.append(cp)
        for cp in copies:
            cp.wait()
        acc = buf[0]
        for i in range(1, 8):
            acc = acc + buf[i]
        o_ref[...] = acc

    vm = pl.BlockSpec(memory_space=pltpu.VMEM)
    shape = (SMALL_ROWS, D_MODEL)
    return pl.pallas_call(
        body, name="allreduce_small", out_shape=jax.ShapeDtypeStruct(shape, F32),
        in_specs=[vm] * 8, out_specs=vm,
        scratch_shapes=[pltpu.VMEM(shape, F32), pltpu.VMEM((8,) + shape, F32), pltpu.SemaphoreType.DMA((7,)),
                        pltpu.SemaphoreType.DMA((7,))],
    )(g_ffn1, g_mix, g_ffn2, g_final, g_bin, dsink, bias_tab, loss_part)


def _adam_update(w, g, m, v):
    nm = ADAM_B1 * m + (1.0 - ADAM_B1) * g
    nv = ADAM_B2 * v + (1.0 - ADAM_B2) * (g * g)
    bc1 = 1.0 - ADAM_B1 ** ADAM_STEP
    bc2 = 1.0 - ADAM_B2 ** ADAM_STEP
    return -ADAM_LR * ((nm / bc1) / (jnp.sqrt(nv / bc2) + ADAM_EPS) + ADAM_WD * w), nm, nv


def _adamw_small(packed, w, m, v):
    names = ("ffn1_norm", "mix_norm", "ffn2_norm", "final_norm", "b_in", "sinks", "rel_bias")
    nn = len(names)

    def grad_of(p_ref, name, k=0):
        if name == "b_in":
            return p_ref[4 + k:5 + k, :]
        if name == "sinks":
            return p_ref[9:10, 0:8]
        if name == "rel_bias":
            return p_ref[16:48, 0:20]
        r = names.index(name)
        return p_ref[r:r + 1, :]

    def body(p_ref, *refs):
        ws, ms, vs = refs[:nn], refs[nn:2 * nn], refs[2 * nn:3 * nn]
        outs = refs[3 * nn:]
        for i, name in enumerate(names):
            og, od, om, ov = outs[i], outs[nn + i], outs[2 * nn + i], outs[3 * nn + i]
            pieces = range(D_IN // D_MODEL) if name == "b_in" else (0,)
            for k in pieces:
                sl = (slice(None), slice(k * D_MODEL, (k + 1) * D_MODEL)) if name == "b_in" else (Ellipsis,)
                g = grad_of(p_ref, name, k)
                d, nm, nv = _adam_update(ws[i][sl], g, ms[i][sl], vs[i][sl])
                og[sl], od[sl], om[sl], ov[sl] = g, d, nm, nv

    vm = pl.BlockSpec(memory_space=pltpu.VMEM)
    shapes = [jax.ShapeDtypeStruct(w[n].shape, F32) for n in names]
    res = pl.pallas_call(
        body, name="adamw_small", out_shape=tuple(shapes * 4), in_specs=[vm] * (1 + 3 * nn),
        out_specs=tuple([vm] * (4 * nn)),
    )(packed, *[w[n] for n in names], *[m[n] for n in names], *[v[n] for n in names])
    return [dict(zip(names, res[i * nn:(i + 1) * nn])) for i in range(4)]


class _GatherRider:
    def __init__(self, shards):
        self.inputs = list(shards)
        nt = len(shards)
        self.out_shape = [jax.ShapeDtypeStruct((N_CHIPS,) + s.shape, s.dtype) for s in shards]
        self.scratch = [pltpu.SemaphoreType.DMA((3 * nt,)), pltpu.SemaphoreType.DMA((3 * nt,)),
                        pltpu.SemaphoreType.DMA((nt,))]

    def _copies(self, srcs, outs, sems):
        ici_s, ici_r, loc = sems
        x, y, c = _me()
        j = 2 * x + y
        local = [pltpu.make_async_copy(srcs[t], outs[t].at[j], loc.at[t]) for t in range(len(srcs))]
        remote = []
        for k, (fx, fy) in enumerate(_CHIP_RELS):
            peer = (_flip(x, fx), _flip(y, fy), c)
            for t in range(len(srcs)):
                remote.append(_remote(srcs[t], outs[t].at[j], ici_s.at[3 * t + k], ici_r.at[3 * t + k], peer))
        return local, remote

    def start(self, srcs, outs, sems):
        local, remote = self._copies(srcs, outs, sems)
        for cp in local + remote:
            cp.start()

    def finish(self, srcs, outs, sems):
        local, remote = self._copies(srcs, outs, sems)
        for cp in remote + local:
            cp.wait()


class _ExchangeRider:
    def __init__(self, parts):
        self.inputs = list(parts)
        self.r2 = [p.shape[1] for p in parts]
        self.off = [sum(self.r2[:g]) for g in range(len(parts))]
        tot = sum(self.r2)
        self.out_shape = [jax.ShapeDtypeStruct((tot, D_MODEL), BF16), jax.ShapeDtypeStruct((3, tot, D_MODEL), BF16)]
        self.scratch = [pltpu.SemaphoreType.DMA((3,)), pltpu.SemaphoreType.DMA((3,)), pltpu.SemaphoreType.DMA(())]

    def start(self, ps, outs, sems):
        own_ref, rec_ref = outs
        ssems, rsems, lsem = sems
        x, y, c = _me()
        j = 2 * x + y
        for g in range(len(ps)):
            pltpu.make_async_copy(ps[g].at[j], own_ref.at[pl.ds(self.off[g], self.r2[g])], lsem).start()
        for k, (fx, fy) in enumerate(_CHIP_RELS):
            px, py = _flip(x, fx), _flip(y, fy)
            for g in range(len(ps)):
                for st, sz in _row_pieces(self.r2[g], 2):
                    _remote(ps[g].at[2 * px + py, pl.ds(st, sz)], rec_ref.at[k, pl.ds(self.off[g] + st, sz)],
                            ssems.at[k], rsems.at[k], (px, py, c)).start()

    def finish(self, ps, outs, sems):
        own_ref, rec_ref = outs
        ssems, rsems, lsem = sems
        x, y, c = _me()
        for k in range(3):
            _remote(rec_ref.at[k], rec_ref.at[k], ssems.at[k], rsems.at[k], (x, y, c)).wait()
        pltpu.make_async_copy(own_ref, own_ref, lsem).wait()


def _pallas(body, args, *, name, grid, in_specs, out_specs, out_shape, scratch_shapes=(), sem=None, vmem=None,
            rider=None):
    if rider is None:
        res = pl.pallas_call(body, name=name, grid=grid, in_specs=list(in_specs), out_specs=tuple(out_specs),
                             out_shape=tuple(out_shape), scratch_shapes=list(scratch_shapes),
                             compiler_params=_params(sem, vmem))(*args)
        return tuple(res), ()
    n_in, n_out, n_sc = len(in_specs), len(out_shape), len(scratch_shapes)
    r_in, r_out = len(rider.inputs), len(rider.out_shape)

    def wrapped(*refs):
        ins, rins = refs[:n_in], refs[n_in:n_in + r_in]
        p = n_in + r_in
        outs, routs = refs[p:p + n_out], refs[p + n_out:p + n_out + r_out]
        p += n_out + r_out
        scr, rsems = refs[p:p + n_sc], refs[p + n_sc:]
        first = pl.program_id(0) == 0
        last = pl.program_id(0) == grid[0] - 1
        for a in range(1, len(grid)):
            first = first & (pl.program_id(a) == 0)
            last = last & (pl.program_id(a) == grid[a] - 1)

        @pl.when(first)
        def _():
            rider.start(rins, routs, rsems)

        body(*ins, *outs, *scr)

        @pl.when(last)
        def _():
            rider.finish(rins, routs, rsems)

    res = pl.pallas_call(
        wrapped, name=name, grid=grid, in_specs=list(in_specs) + [ANY] * r_in,
        out_specs=tuple(out_specs) + (ANY,) * r_out, out_shape=tuple(out_shape) + tuple(rider.out_shape),
        scratch_shapes=list(scratch_shapes) + rider.scratch,
        compiler_params=_params(("arbitrary",) * len(grid), vmem))(*args, *rider.inputs)
    return tuple(res[:n_out]), tuple(res[n_out:])


def _loss_tile(hh, gain, tgt):
    r = lax.rsqrt(jnp.mean(hh * hh, axis=-1, keepdims=True) + EPS)
    hn = hh * r
    err = hn * gain - tgt
    part = (0.5 / D_MODEL) * jnp.sum(jnp.sum(err * err, axis=1, keepdims=True), axis=0, keepdims=True)
    dy = err * (1.0 / D_MODEL)
    dng = dy * gain
    dh = r * (dng - hn * jnp.mean(dng * hn, axis=-1, keepdims=True))
    return dh, part, jnp.sum(dy * hn, axis=0, keepdims=True)


def _ffn_fwd(h, gain, wgt, wut, wd, rider=None, head=None):
    t = h.shape[0]

    def body(h_ref, gain_ref, wg_hbm, wu_hbm, wd_hbm, *rest):
        if head is None:
            hout_ref, n_ref, g_ref, u_ref, a_ref, wg_v, wu_v, wd_v = rest
        else:
            fg_ref, tgt_ref, hout_ref, n_ref, g_ref, u_ref, a_ref, loss_ref, gg_ref, wg_v, wu_v, wd_v = rest

        @pl.when(pl.program_id(0) == 0)
        def _():
            pltpu.sync_copy(wg_hbm, wg_v)
            pltpu.sync_copy(wu_hbm, wu_v)
            pltpu.sync_copy(wd_hbm, wd_v)
            if head is not None:
                loss_ref[...] = jnp.zeros_like(loss_ref)
                gg_ref[...] = jnp.zeros_like(gg_ref)

        hh = h_ref[...]
        r = lax.rsqrt(jnp.mean(hh * hh, axis=-1, keepdims=True) + EPS)
        n = (hh * r * gain_ref[...]).astype(BF16)
        n_ref[...] = n
        acc = jnp.zeros((TM, D_MODEL), F32)
        for c0, c1 in zip(FF_BOUNDS[:-1], FF_BOUNDS[1:]):
            sl = slice(c0, c1)
            g = _dot_nt(n, wg_v[sl, :])
            u = _dot_nt(n, wu_v[sl, :])
            sg = _sigmoid(g)
            silu = g * sg
            a = (silu * u).astype(BF16)
            a_ref[:, sl] = a
            g_ref[:, sl] = (u * (sg * (1.0 + g * (1.0 - sg)))).astype(BF16)
            u_ref[:, sl] = silu.astype(BF16)
            acc = acc + _dot(a, wd_v[sl, :])
        hout = hh + 0.5 * acc
        if head is None:
            hout_ref[...] = hout
        else:
            dh, part, gpart = _loss_tile(hout, fg_ref[...], tgt_ref[...])
            hout_ref[...] = dh
            loss_ref[...] += part
            gg_ref[...] += gpart

    row = lambda w: pl.BlockSpec((TM, w), lambda i: (i, 0))
    vec = pl.BlockSpec((1, D_MODEL), lambda i: (0, 0))
    wv = pltpu.VMEM((D_FF, D_MODEL), BF16)
    args, in_specs = (h, gain, wgt, wut, wd), [row(D_MODEL), vec, ANY, ANY, ANY]
    out_shape = [jax.ShapeDtypeStruct((t, D_MODEL), F32), jax.ShapeDtypeStruct((t, D_MODEL), BF16)] + [
        jax.ShapeDtypeStruct((t, D_FF), BF16)] * 3
    out_specs = [row(D_MODEL), row(D_MODEL), row(D_FF), row(D_FF), row(D_FF)]
    if head is not None:
        args, in_specs = args + tuple(head), in_specs + [vec, row(D_MODEL)]
        out_shape += [jax.ShapeDtypeStruct((8, 128), F32), jax.ShapeDtypeStruct((1, D_MODEL), F32)]
        out_specs += [pl.BlockSpec((8, 128), lambda i: (0, 0)), vec]
    return _pallas(
        body, args, name="ffn_fwd", grid=(t // TM,), out_shape=tuple(out_shape), in_specs=in_specs,
        out_specs=tuple(out_specs), scratch_shapes=[wv, wv, wv], sem=("arbitrary",), vmem=VMEM_BIG, rider=rider)


def _ffn_bwd(dhout, h, gain, dgf, duf, wgt, wut, wd):
    t = h.shape[0]
    tm = TM_BWD

    def body(dho_ref, h_ref, gain_ref, g_ref, u_ref, wg_hbm, wu_hbm, wd_hbm,
             dh_ref, dg_ref, du_ref, df_ref, gg_ref, wg_v, wu_v, wd_v):
        @pl.when(pl.program_id(0) == 0)
        def _():
            pltpu.sync_copy(wg_hbm, wg_v)
            pltpu.sync_copy(wu_hbm, wu_v)
            pltpu.sync_copy(wd_hbm, wd_v)
            gg_ref[...] = jnp.zeros_like(gg_ref)

        dho = dho_ref[...]
        df = (0.5 * dho).astype(BF16)
        df_ref[...] = df
        dn = jnp.zeros((tm, D_MODEL), F32)
        for c0, c1 in zip(FF_BOUNDS[:-1], FF_BOUNDS[1:]):
            sl = slice(c0, c1)
            da = _dot_nt(df, wd_v[sl, :])
            dg = (da * g_ref[:, sl].astype(F32)).astype(BF16)
            du = (da * u_ref[:, sl].astype(F32)).astype(BF16)
            dg_ref[:, sl] = dg
            du_ref[:, sl] = du
            dn = dn + _dot(dg, wg_v[sl, :]) + _dot(du, wu_v[sl, :])
        hh = h_ref[...]
        r = lax.rsqrt(jnp.mean(hh * hh, axis=-1, keepdims=True) + EPS)
        hn = hh * r
        gg_ref[...] += jnp.sum(dn * hn, axis=0, keepdims=True)
        dng = dn * gain_ref[...]
        dh_ref[...] = dho + r * (dng - hn * jnp.mean(dng * hn, axis=-1, keepdims=True))

    row = lambda w: pl.BlockSpec((tm, w), lambda i: (i, 0))
    vec = pl.BlockSpec((1, D_MODEL), lambda i: (0, 0))
    wv = pltpu.VMEM((D_FF, D_MODEL), BF16)
    return pl.pallas_call(
        body, name="ffn_bwd", grid=(t // tm,),
        out_shape=(jax.ShapeDtypeStruct((t, D_MODEL), F32), jax.ShapeDtypeStruct((t, D_FF), BF16),
                   jax.ShapeDtypeStruct((t, D_FF), BF16),
                   jax.ShapeDtypeStruct((t, D_MODEL), BF16), jax.ShapeDtypeStruct((1, D_MODEL), F32)),
        in_specs=[row(D_MODEL), row(D_MODEL), vec, row(D_FF), row(D_FF), ANY, ANY, ANY],
        out_specs=(row(D_MODEL), row(D_FF), row(D_FF), row(D_MODEL), vec),
        scratch_shapes=[wv, wv, wv],
        compiler_params=_params(("arbitrary",), VMEM_BIG),
    )(dhout, h, gain, dgf, duf, wgt, wut, wd)


def _wgrad(lhs, rhs, rb, with_colsum=False, name="wgrad", rider=None):
    t, k = lhs.shape
    n = rhs.shape[1]

    def body(l_ref, r_ref, o_ref, *rest):
        o_ref[...] = _dot_tn(l_ref[...], r_ref[...]).astype(BF16)
        if with_colsum:
            rest[0][...] = jnp.sum(l_ref[...].astype(F32), axis=0, keepdims=True)

    out_shape = [jax.ShapeDtypeStruct((k, n), BF16)]
    out_specs = [pl.BlockSpec((rb, n), lambda j: (j, 0))]
    if with_colsum:
        out_shape.append(jax.ShapeDtypeStruct((1, k), F32))
        out_specs.append(pl.BlockSpec((1, rb), lambda j: (0, j)))
    res, ro = _pallas(
        body, (lhs, rhs), name=name, grid=(k // rb,), out_shape=tuple(out_shape),
        in_specs=[pl.BlockSpec((t, rb), lambda j: (0, j)), pl.BlockSpec((t, n), lambda j: (0, 0))],
        out_specs=tuple(out_specs), sem=("arbitrary",), vmem=VMEM_BIG, rider=rider)
    if rider is not None:
        return res[0], ro
    return res if with_colsum else res[0]


def _lane_blocks(nseq, seq, nblk, tm=TM):
    spt = seq // tm
    return pl.BlockSpec((1, nblk, tm, 128), lambda i: (i // spt, 0, i % spt, 0))


def _inproj_fwd(h, gain, wint, b_in, nseq, rider=None):
    t = h.shape[0]
    seq = t // nseq
    cut_a = 5 * MXU_DIM
    pieces = ((0, cut_a, 0, 0), (cut_a, ZA_W - cut_a, 0, cut_a), (ZA_W, ZB_W, 1, 0), (ZA_W + ZB_W, 1024, 2, 0),
              (ZA_W + ZB_W + 1024, 1024, 2, 1024))

    def body(h_ref, gain_ref, w_hbm, b_ref, u_ref, za_ref, zb_ref, zg_ref, w_v):
        @pl.when(pl.program_id(0) == 0)
        def _():
            pltpu.sync_copy(w_hbm, w_v)

        hh = h_ref[...]
        r = lax.rsqrt(jnp.mean(hh * hh, axis=-1, keepdims=True) + EPS)
        un = (hh * r * gain_ref[...]).astype(BF16)
        u_ref[...] = un
        outs = (None, zb_ref, zg_ref)
        for c0, cw, oi, o0 in pieces:
            val = _dot_nt(un, w_v[c0:c0 + cw, :]) + b_ref[:, c0:c0 + cw]
            if oi == 0:
                for cb in range(cw // 128):
                    za_ref[0, o0 // 128 + cb] = val[:, cb * 128:(cb + 1) * 128]
            else:
                outs[oi][:, o0:o0 + cw] = val.astype(BF16)

    row = lambda w: pl.BlockSpec((TM, w), lambda i: (i, 0))
    return _pallas(
        body, (h, gain, wint, b_in), name="inproj_fwd", grid=(t // TM,),
        out_shape=(jax.ShapeDtypeStruct((t, D_MODEL), BF16), jax.ShapeDtypeStruct((nseq, ZA_W // 128, seq, 128), F32),
                   jax.ShapeDtypeStruct((t, ZB_W), BF16), jax.ShapeDtypeStruct((t, 2 * D_MODEL), BF16)),
        in_specs=[row(D_MODEL), pl.BlockSpec((1, D_MODEL), lambda i: (0, 0)), ANY,
                  pl.BlockSpec((1, D_IN), lambda i: (0, 0))],
        out_specs=(row(D_MODEL), _lane_blocks(nseq, seq, ZA_W // 128), row(ZB_W), row(2 * D_MODEL)),
        scratch_shapes=[pltpu.VMEM((D_IN, D_MODEL), BF16)], sem=("arbitrary",), vmem=VMEM_BIG, rider=rider)


def _inproj_bwd(dz, dh2, h, gain, wint, rider=None):
    t = h.shape[0]
    nc = 5
    cw = D_IN // nc

    def body(dz_ref, dh2_ref, h_ref, gain_ref, w_hbm, dh_ref, gg_ref, w_v):
        @pl.when(pl.program_id(0) == 0)
        def _():
            pltpu.sync_copy(w_hbm, w_v)
            gg_ref[...] = jnp.zeros_like(gg_ref)

        du = jnp.zeros((TM, D_MODEL), F32)
        for ci in range(nc):
            sl = slice(ci * cw, (ci + 1) * cw)
            du = du + _dot(dz_ref[:, sl], w_v[sl, :])
        hh = h_ref[...]
        r = lax.rsqrt(jnp.mean(hh * hh, axis=-1, keepdims=True) + EPS)
        hn = hh * r
        gg_ref[...] += jnp.sum(du * hn, axis=0, keepdims=True)
        dng = du * gain_ref[...]
        dh_ref[...] = dh2_ref[...] + r * (dng - hn * jnp.mean(dng * hn, axis=-1, keepdims=True))

    row = lambda w: pl.BlockSpec((TM, w), lambda i: (i, 0))
    vec = pl.BlockSpec((1, D_MODEL), lambda i: (0, 0))
    return _pallas(
        body, (dz, dh2, h, gain, wint), name="inproj_bwd", grid=(t // TM,),
        out_shape=(jax.ShapeDtypeStruct((t, D_MODEL), F32), jax.ShapeDtypeStruct((1, D_MODEL), F32)),
        in_specs=[row(D_IN), row(D_MODEL), row(D_MODEL), vec, ANY],
        out_specs=(row(D_MODEL), vec),
        scratch_shapes=[pltpu.VMEM((D_IN, D_MODEL), BF16)], sem=("arbitrary",), vmem=VMEM_BIG, rider=rider)


def _head_sums(x):
    w = x.shape[1]
    i = lax.broadcasted_iota(jnp.int32, (w, w), 0) // HEAD_DIM
    j = lax.broadcasted_iota(jnp.int32, (w, w), 1) // HEAD_DIM
    ones = (i == j).astype(BF16)
    hi = x.astype(BF16)
    r1 = x - hi.astype(F32)
    mid = r1.astype(BF16)
    lo = (r1 - mid.astype(F32)).astype(BF16)
    return _dot(hi, ones) + _dot(mid, ones) + _dot(lo, ones)


def _merge_fwd(o0, o1, o2, l0, l1, l2, yb, zg, h1, wat, wbt, wout, rider=None):
    t = h1.shape[0]
    nseq, _, seq, _ = o0.shape

    def body(o0_ref, o1_ref, o2_ref, l0_ref, l1_ref, l2_ref, yb_ref, ga_ref, gb_ref, h1_ref, wa_ref, wb_ref, wo_ref,
             h2_ref, y_ref, lt_ref, pa_ref, pb_ref, mg_ref):
        wide = lambda ref: jnp.concatenate([ref[0, 0], ref[0, 1]], axis=1)
        la, lb, lc = wide(l0_ref), wide(l1_ref), wide(l2_ref)
        mx = jnp.maximum(jnp.maximum(la, lb), lc)
        ea, eb, ec = jnp.exp(la - mx), jnp.exp(lb - mx), jnp.exp(lc - mx)
        den = ea + eb + ec
        y = (ea * wide(o0_ref) + eb * wide(o1_ref) + ec * wide(o2_ref)) / den
        lt = mx + jnp.log(den)
        lt_ref[0, 0] = lt[:, :128]
        lt_ref[0, 1] = lt[:, 128:]
        yb16 = y.astype(BF16)
        y_ref[...] = yb16
        pa = _dot_nt(yb16, wa_ref[...])
        pb = _dot_nt(yb_ref[...], wb_ref[...])
        pa_ref[...] = pa.astype(BF16)
        pb_ref[...] = pb.astype(BF16)
        mg = (_sigmoid(ga_ref[...].astype(F32)) * pa + _sigmoid(gb_ref[...].astype(F32)) * pb).astype(BF16)
        mg_ref[...] = mg
        h2_ref[...] = h1_ref[...] + _dot(mg, wo_ref[...])

    row = lambda w: pl.BlockSpec((TM, w), lambda i: (i, 0))
    full = lambda a: pl.BlockSpec(a.shape, lambda i: (0, 0))
    gate = lambda cb: pl.BlockSpec((TM, D_MODEL), lambda i: (i, cb))
    return _pallas(
        body, (o0, o1, o2, l0, l1, l2, yb, zg, zg, h1, wat, wbt, wout), name="merge_fwd", grid=(t // TM,),
        out_shape=(jax.ShapeDtypeStruct((t, D_MODEL), F32), jax.ShapeDtypeStruct((t, GW), BF16),
                   jax.ShapeDtypeStruct((nseq, 2, seq, 128), F32), jax.ShapeDtypeStruct((t, D_MODEL), BF16),
                   jax.ShapeDtypeStruct((t, D_MODEL), BF16), jax.ShapeDtypeStruct((t, D_MODEL), BF16)),
        in_specs=[_lane_blocks(nseq, seq, 2)] * 6 + [row(2 * GW), gate(0), gate(1), row(D_MODEL), full(wat), full(wbt),
                                                     full(wout)],
        out_specs=(row(D_MODEL), row(GW), _lane_blocks(nseq, seq, 2), row(D_MODEL), row(D_MODEL), row(D_MODEL)),
        sem=("parallel",), vmem=VMEM_BIG, rider=rider)


def _merge_bwd(dh2, pa, pb, zg, y, yb, wat, wbt, wout, nseq, rider=None):
    t = dh2.shape[0]

    def body(dh2_ref, pa_ref, pb_ref, ga_ref, gb_ref, y_ref, yb_ref, wa_ref, wb_ref, wo_ref,
             dpa_ref, dpb_ref, dga_ref, dgb_ref, dya_ref, dyb_ref, dh2b_ref, ca_ref, cb_ref):
        d16 = dh2_ref[...].astype(BF16)
        dh2b_ref[...] = d16
        dm = _dot_nt(d16, wo_ref[...])
        sa = _sigmoid(ga_ref[...].astype(F32))
        sb = _sigmoid(gb_ref[...].astype(F32))
        dpa = (dm * sa).astype(BF16)
        dpb = (dm * sb).astype(BF16)
        dpa_ref[...] = dpa
        dpb_ref[...] = dpb
        dga_ref[...] = (dm * pa_ref[...].astype(F32) * sa * (1.0 - sa)).astype(BF16)
        dgb_ref[...] = (dm * pb_ref[...].astype(F32) * sb * (1.0 - sb)).astype(BF16)
        dya = _dot(dpa, wa_ref[...])
        dyb = _dot(dpb, wb_ref[...])
        dya_ref[0, 0] = dya[:, :128]
        dya_ref[0, 1] = dya[:, 128:]
        dyb_ref[...] = dyb.astype(BF16)
        ca = _head_sums(dya * y_ref[...].astype(F32))
        ca_ref[0, 0] = ca[:, :128]
        ca_ref[0, 1] = ca[:, 128:]
        cb_ref[...] = _head_sums(dyb * yb_ref[...].astype(F32))

    row = lambda w: pl.BlockSpec((TM, w), lambda i: (i, 0))
    full = lambda a: pl.BlockSpec(a.shape, lambda i: (0, 0))
    gate = lambda cb: pl.BlockSpec((TM, D_MODEL), lambda i: (i, cb))
    bf = lambda w: jax.ShapeDtypeStruct((t, w), BF16)
    lanes = jax.ShapeDtypeStruct((nseq, 2, t // nseq, 128), F32)
    lane_spec = _lane_blocks(nseq, t // nseq, 2)
    return _pallas(
        body, (dh2, pa, pb, zg, zg, y, yb, wat, wbt, wout), name="merge_bwd", grid=(t // TM,),
        out_shape=(bf(D_MODEL), bf(D_MODEL), bf(D_MODEL), bf(D_MODEL), lanes, bf(2 * GW), bf(D_MODEL),
                   lanes, jax.ShapeDtypeStruct((t, 2 * GW), F32)),
        in_specs=[row(D_MODEL), row(D_MODEL), row(D_MODEL), gate(0), gate(1), row(GW), row(2 * GW),
                  full(wat), full(wbt), full(wout)],
        out_specs=(row(D_MODEL), row(D_MODEL), row(D_MODEL), row(D_MODEL), lane_spec, row(2 * GW), row(D_MODEL),
                   lane_spec, row(2 * GW)),
        sem=("parallel",), vmem=VMEM_BIG, rider=rider)


def _lane_head(rows):
    return lax.broadcasted_iota(jnp.int32, (rows, GW), 1) // HEAD_DIM


def _kv_expand_matrix(r):
    ci = lax.broadcasted_iota(jnp.int32, (2 * HEAD_DIM, GW), 0)
    ji = lax.broadcasted_iota(jnp.int32, (2 * HEAD_DIM, GW), 1)
    return (ci == (ji % HEAD_DIM) + HEAD_DIM * r).astype(BF16)


def _block_rows(row0, stride, ib):
    start = row0 + (stride * BLOCK) * ib
    if stride > 1:
        return pl.ds(start, BLOCK, stride=stride)
    return pl.ds(pl.multiple_of(start, BLOCK), BLOCK)


def _stack_heads(x, lane_head):
    return jnp.concatenate([jnp.where(lane_head == h, x, jnp.zeros_like(x)) for h in range(4)], axis=0)


def _unstack_heads(x4, lane_head):
    out = jnp.zeros((BLOCK, GW), F32)
    for h in range(4):
        out = jnp.where(lane_head == h, x4[h * BLOCK:(h + 1) * BLOCK], out)
    return out


def _load_rows(ref, rows, split):
    if split:
        return jnp.concatenate([ref[0, 0, rows, :], ref[0, 1, rows, :]], axis=1)
    return ref[0, rows, :]


def _store_rows(ref, rows, val, split):
    if split:
        ref[0, 0, rows, :] = val[:, :128]
        ref[0, 1, rows, :] = val[:, 128:]
    else:
        ref[0, rows, :] = val


def _attn_fwd(q_arr, k_arr, v_arr, bias, sink, *, grid, seq, stride, kvw, split, q_spec, k_spec, v_spec, bias_map,
              sink_map, o_spec, has_sink, o_shape, o_dtype, name, rider=None):
    nb = seq // stride // BLOCK
    scale = HEAD_DIM ** -0.5
    expanded = kvw != GW
    rps = min(stride, RESIDUES_PER_STEP)
    grid = (grid[0], grid[1] // rps)
    assert not has_sink or B_WINDOW - 1 < BLOCK

    def body(q_ref, k_ref, v_ref, bias_ref, sink_ref, o_ref, lse_ref, *kv_x):
        rr = pl.program_id(1)
        lane_head = _lane_head(BLOCK)
        if expanded:
            expand = _kv_expand_matrix(rr)
            kv_x[0][...] = _dot(k_ref[0], expand).astype(BF16)
            kv_x[1][...] = _dot(v_ref[0], expand).astype(BF16)
        for j in range(rps):
            residue(rr * rps + j if stride > 1 else 0, q_ref, k_ref, v_ref, bias_ref, sink_ref, o_ref, lse_ref, kv_x,
                    lane_head)

    def residue(row0, q_ref, k_ref, v_ref, bias_ref, sink_ref, o_ref, lse_ref, kv_x, lane_head):
        def per_head(fn, x):
            return jnp.concatenate([fn(sink_ref[0, h:h + 1, 0:1], x[h * BLOCK:(h + 1) * BLOCK]) for h in range(4)],
                                   axis=0)

        def load(ref, ib):
            return _load_rows(ref, _block_rows(row0, stride, ib), split).astype(BF16)

        def load_kv(which, ib):
            if expanded:
                return kv_x[which][_block_rows(0, 1, ib), :]
            return load((k_ref, v_ref)[which], ib)

        def block(ib, first):
            q4 = _stack_heads(load(q_ref, ib), lane_head)
            if first:
                kc, vc = load_kv(0, ib), load_kv(1, ib)
                b4 = bias_ref[:, :, BLOCK:].reshape(4 * BLOCK, BLOCK)
            else:
                kc = jnp.concatenate([load_kv(0, ib - 1), load_kv(0, ib)], axis=0)
                vc = jnp.concatenate([load_kv(1, ib - 1), load_kv(1, ib)], axis=0)
                b4 = bias_ref[...].reshape(4 * BLOCK, 2 * BLOCK)
                if has_sink:
                    oldest = lax.broadcasted_iota(jnp.int32, kc.shape, 0) == 0
                    kc = jnp.where(oldest, jnp.zeros_like(kc), kc)
                    vc = jnp.where(oldest, jnp.zeros_like(vc), vc)
            s = _dot_nt(q4, kc) * scale + b4
            m = jnp.max(s, axis=-1, keepdims=True)
            if has_sink and first:
                m = per_head(jnp.maximum, m)
            p = jnp.exp(s - m)
            l = jnp.sum(p, axis=-1, keepdims=True)
            if has_sink and first:
                l = l + per_head(lambda sk, mh: jnp.exp(sk - mh), m)
            o4 = _dot(p.astype(BF16), vc) / l
            rows = _block_rows(row0, stride, ib)
            _store_rows(o_ref, rows, _unstack_heads(o4, lane_head).astype(o_dtype), split)
            _store_rows(lse_ref, rows, _unstack_heads(m + jnp.log(l), lane_head), split)

        block(0, True)
        if nb > 1:
            def step(i, carry):
                block(i, False)
                return carry
            lax.fori_loop(1, nb, step, 0, unroll=min(ATTN_UNROLL, nb - 1))

    return _pallas(
        body, (q_arr, k_arr, v_arr, bias, sink), name=name, grid=grid,
        out_shape=(jax.ShapeDtypeStruct(o_shape, o_dtype), jax.ShapeDtypeStruct(o_shape, F32)),
        in_specs=[q_spec, k_spec, v_spec,
                  pl.BlockSpec((4, BLOCK, 2 * BLOCK), bias_map), pl.BlockSpec((1, 4, 128), sink_map)],
        out_specs=(o_spec, o_spec),
        scratch_shapes=[pltpu.VMEM((seq, GW), BF16)] * 2 if expanded else [],
        sem=("arbitrary", "arbitrary"), vmem=VMEM_BIG, rider=rider)


def _attn_bwd(q_arr, k_arr, v_arr, bias, sink, dy, cc, lse, *, grid, seq, stride, kvw, split, q_spec, k_spec, v_spec,
              bias_map, sink_map, o_spec, kv_out_spec, has_sink, n_bias, dq_shape, dkv_shape, g_dtype, name):
    ln = seq // stride
    nb = ln // BLOCK
    scale = HEAD_DIM ** -0.5
    expanded = kvw != GW
    rps = min(stride, RESIDUES_PER_STEP)
    grid = (grid[0], grid[1] // rps)

    def body(q_ref, k_ref, v_ref, bias_ref, sink_ref, dy_ref, c_ref, lse_ref,
             dq_ref, dk_ref, dv_ref, db_ref, dsk_ref, dk_acc, dv_acc, dk_half, dv_half, *kv_x):
        rr = pl.program_id(1)

        @pl.when((pl.program_id(0) == 0) & (rr == 0))
        def _():
            db_ref[...] = jnp.zeros_like(db_ref)
            dsk_ref[...] = jnp.zeros_like(dsk_ref)

        if expanded:
            expand = _kv_expand_matrix(rr)
            kv_x[0][...] = _dot(k_ref[0], expand).astype(BF16)
            kv_x[1][...] = _dot(v_ref[0], expand).astype(BF16)
        refs = (q_ref, k_ref, v_ref, bias_ref, sink_ref, dy_ref, c_ref, lse_ref, dq_ref, dk_ref, dv_ref, db_ref,
                dsk_ref, dk_acc, dv_acc, dk_half, dv_half, kv_x)
        for j in range(rps):
            residue(rr, rr * rps + j if stride > 1 else 0, *refs)

    def residue(rr, row0, q_ref, k_ref, v_ref, bias_ref, sink_ref, dy_ref, c_ref, lse_ref,
                dq_ref, dk_ref, dv_ref, db_ref, dsk_ref, dk_acc, dv_acc, dk_half, dv_half, kv_x):
        dk_acc[...] = jnp.zeros_like(dk_acc)
        dv_acc[...] = jnp.zeros_like(dv_acc)
        lane_head = _lane_head(BLOCK)
        hb = 4 * rr if n_bias == 8 else 0

        def load(ref, ib):
            return _load_rows(ref, _block_rows(row0, stride, ib), split)

        def load_kv(which, ib):
            if expanded:
                return kv_x[which][_block_rows(0, 1, ib), :]
            return load((k_ref, v_ref)[which], ib).astype(BF16)

        def head_col(x):
            return jnp.concatenate([x[:, h * HEAD_DIM:h * HEAD_DIM + 1] for h in range(4)], axis=0)

        def block(ib, first):
            q4 = _stack_heads(load(q_ref, ib).astype(BF16), lane_head)
            dy4 = _stack_heads(load(dy_ref, ib).astype(BF16), lane_head)
            c4 = head_col(load(c_ref, ib))
            l4 = head_col(load(lse_ref, ib))
            if first:
                kc, vc = load_kv(0, ib), load_kv(1, ib)
                b4 = bias_ref[:, :, BLOCK:].reshape(4 * BLOCK, BLOCK)
                krows = pl.ds(0, BLOCK)
            else:
                kc = jnp.concatenate([load_kv(0, ib - 1), load_kv(0, ib)], axis=0)
                vc = jnp.concatenate([load_kv(1, ib - 1), load_kv(1, ib)], axis=0)
                b4 = bias_ref[...].reshape(4 * BLOCK, 2 * BLOCK)
                krows = pl.ds(pl.multiple_of((ib - 1) * BLOCK, BLOCK), 2 * BLOCK)
            nk = BLOCK if first else 2 * BLOCK
            p = jnp.exp(_dot_nt(q4, kc) * scale + b4 - l4)
            ds = p * (_dot_nt(dy4, vc) - c4)
            ds3 = ds.reshape(4, BLOCK, nk)
            if n_bias == 8:
                if first:
                    db_ref[pl.ds(hb, 4), :, BLOCK:] += ds3
                else:
                    db_ref[pl.ds(hb, 4)] += ds3
            elif first:
                db_ref[:, :, BLOCK:] += ds3
            else:
                db_ref[...] += ds3
            ds16 = ds.astype(BF16)
            dq = _unstack_heads(_dot(ds16, kc), lane_head) * scale
            _store_rows(dq_ref, _block_rows(row0, stride, ib), dq.astype(g_dtype), split)
            dk_acc[krows, :] += _dot_tn(ds16, q4) * scale
            dv_acc[krows, :] += _dot_tn(p.astype(BF16), dy4)
            if has_sink:
                for h in range(4):
                    hs = slice(h * BLOCK, (h + 1) * BLOCK)
                    sk = sink_ref[0, h:h + 1, 0:1]
                    val = -jnp.sum(jnp.exp(sk - l4[hs]) * c4[hs], axis=0, keepdims=True)
                    dsk_ref[hb + h] += jnp.broadcast_to(val, (8, 128))

        block(0, True)
        if nb > 1:
            def step(i, carry):
                block(i, False)
                return carry
            lax.fori_loop(1, nb, step, 0, unroll=min(ATTN_UNROLL, nb - 1))

        if kvw == GW:
            all_rows = pl.ds(row0, ln, stride=stride) if stride > 1 else pl.ds(0, ln)
            _store_rows(dk_ref, all_rows, dk_acc[...].astype(g_dtype), split)
            _store_rows(dv_ref, all_rows, dv_acc[...].astype(g_dtype), split)
        else:
            def fold(acc):
                t2 = acc[:, :2 * HEAD_DIM] + acc[:, 2 * HEAD_DIM:]
                t2 = t2 + pltpu.roll(t2, HEAD_DIM, 1)
                lane = lax.broadcasted_iota(jnp.int32, t2.shape, 1) // HEAD_DIM
                return jnp.where(lane == rr, t2, 0.0)

            @pl.when(rr == 0)
            def _():
                dk_half[...] = fold(dk_acc[...])
                dv_half[...] = fold(dv_acc[...])

            @pl.when(rr == 1)
            def _():
                dk_ref[0] = (dk_half[...] + fold(dk_acc[...])).astype(g_dtype)
                dv_ref[0] = (dv_half[...] + fold(dv_acc[...])).astype(g_dtype)

    return pl.pallas_call(
        body, name=name, grid=grid,
        out_shape=(jax.ShapeDtypeStruct(dq_shape, g_dtype), jax.ShapeDtypeStruct(dkv_shape, g_dtype),
                   jax.ShapeDtypeStruct(dkv_shape, g_dtype), jax.ShapeDtypeStruct((n_bias, BLOCK, 2 * BLOCK), F32),
                   jax.ShapeDtypeStruct((8, 8, 128), F32)),
        in_specs=[q_spec, k_spec, v_spec,
                  pl.BlockSpec((4, BLOCK, 2 * BLOCK), bias_map), pl.BlockSpec((1, 4, 128), sink_map),
                  o_spec, o_spec, o_spec],
        out_specs=(o_spec, kv_out_spec, kv_out_spec,
                   pl.BlockSpec((n_bias, BLOCK, 2 * BLOCK), lambda n, r: (0, 0, 0)),
                   pl.BlockSpec((8, 8, 128), lambda n, r: (0, 0, 0))),
        scratch_shapes=[pltpu.VMEM((ln, GW), F32), pltpu.VMEM((ln, GW), F32),
                        pltpu.VMEM((ln, 2 * HEAD_DIM), F32), pltpu.VMEM((ln, 2 * HEAD_DIM), F32)]
        + ([pltpu.VMEM((seq, GW), BF16)] * 2 if expanded else []),
        compiler_params=_params(("arbitrary", "arbitrary"), VMEM_BIG),
    )(q_arr, k_arr, v_arr, bias, sink, dy, cc, lse)


def _bias_grad(ds_all, buckets):
    def body(ds_ref, bk_ref, o_ref):
        rows = lax.broadcasted_iota(jnp.int32, (N_BUCKETS, 128), 0)
        cols = lax.broadcasted_iota(jnp.int32, (N_BUCKETS, 128), 1)

        def per_bucket(b, acc):
            for h in range(20):
                gi = h // 4 if h < 12 else 3
                v = jnp.where(bk_ref[gi] == b, ds_ref[h], 0.0)
                v = jnp.sum(jnp.sum(v, axis=1, keepdims=True), axis=0, keepdims=True)
                acc = jnp.where((rows == b) & (cols == h), v, acc)
            return acc

        o_ref[...] = lax.fori_loop(0, N_BUCKETS, per_bucket, jnp.zeros((N_BUCKETS, 128), F32))

    vm = pl.BlockSpec(memory_space=pltpu.VMEM)
    return pl.pallas_call(body, name="bias_grad", out_shape=jax.ShapeDtypeStruct((N_BUCKETS, 128), F32),
                          in_specs=[vm, vm], out_specs=vm)(ds_all, buckets)


def _adamw(w, g, m, v, name):
    r, c = w.shape
    tr = r
    for cand in (256, 176, 128, 64, 32, 16, 8):
        if r % cand == 0:
            tr = cand
            break
    bc1 = 1.0 - ADAM_B1 ** ADAM_STEP
    bc2 = 1.0 - ADAM_B2 ** ADAM_STEP

    def body(w_ref, g_ref, m_ref, v_ref, d_ref, nm_ref, nv_ref):
        gv = g_ref[...]
        nm = ADAM_B1 * m_ref[...] + (1.0 - ADAM_B1) * gv
        nv = ADAM_B2 * v_ref[...] + (1.0 - ADAM_B2) * (gv * gv)
        nm_ref[...] = nm
        nv_ref[...] = nv
        d_ref[...] = -ADAM_LR * ((nm / bc1) / (jnp.sqrt(nv / bc2) + ADAM_EPS) + ADAM_WD * w_ref[...])

    spec = pl.BlockSpec((tr, c), lambda i: (i, 0))
    shp = jax.ShapeDtypeStruct((r, c), F32)
    return pl.pallas_call(body, name=name, grid=(r // tr,), out_shape=(shp, shp, shp),
                          in_specs=[spec] * 4, out_specs=(spec, spec, spec),
                          compiler_params=_params(("parallel",)))(w, g, m, v)


def _t5_bucket(dist):
    max_exact = N_BUCKETS // 2
    n = jnp.maximum(dist, 0)
    nf = jnp.maximum(n, 1).astype(F32)
    large = max_exact + (jnp.log(nf / max_exact) / math.log(MAX_DISTANCE / max_exact)
                         * (N_BUCKETS - max_exact)).astype(jnp.int32)
    large = jnp.minimum(large, N_BUCKETS - 1)
    return jnp.where(n < max_exact, n, large)


def _bias_tables(rel_bias):
    qi = jnp.arange(BLOCK)[:, None]
    ki = jnp.arange(2 * BLOCK)[None, :]
    dist = qi + BLOCK - ki
    specs = [(d, w // d, 4 * gi, 4 * gi + 4) for gi, (w, d) in enumerate(DIL_GROUPS)] + [(1, B_WINDOW - 1, 12, 20)]
    biases, buckets = [], []
    for stride, steps, h0, h1 in specs:
        valid = (dist >= 0) & (dist <= steps)
        bk = jnp.where(valid, _t5_bucket(dist * stride), -1).astype(jnp.int32)
        onehot = (bk[None, :, :] == jnp.arange(N_BUCKETS, dtype=jnp.int32)[:, None, None]).astype(F32)
        b = jnp.einsum("bqk,bh->hqk", onehot, rel_bias[:, h0:h1], precision=lax.Precision.HIGHEST)
        biases.append(jnp.where(valid[None], b, NEG))
        buckets.append(bk)
    return jnp.concatenate(biases, axis=0), jnp.stack(buckets, axis=0)


def _local_step(x, tgt, W, S, shards=None):
    nseq, seq, _ = x.shape
    t = nseq * seq
    xf = x.reshape(t, D_MODEL)
    bias_all, buckets = _bias_tables(S["rel_bias"])
    sink_b = jnp.broadcast_to(S["sinks"].reshape(2, 4, 1), (2, 4, 128)).astype(F32)
    sink_0 = jnp.zeros((1, 4, 128), F32)
    dist = shards is not None
    W = dict(W)
    G, GS, reduced = {}, {}, {}

    def put(keys, gathered):
        for k, g in zip(keys, gathered):
            W[k] = g.reshape(_FULL_SHAPE.get(k, (N_CHIPS * shards[k].shape[0], D_MODEL)))

    def gather_rider(keys):
        return _GatherRider([shards[k] for k in keys]) if dist else None

    def pair(keys):
        return _pair_reduce([G[k].reshape(N_CHIPS, 2, shards[k].shape[0] // 2, D_MODEL) for k in keys],
                            "grad_pair_reduce_" + keys[0])

    def finish(keys, own, rec):
        full = _final_reduce(own, rec, "grad_final_reduce_" + keys[0])
        off = 0
        for k in keys:
            r = shards[k].shape[0]
            reduced[k] = full[:, off:off + r // 2].reshape(r, D_MODEL)
            off += r // 2

    if dist:
        first = ("wgt1", "wut1", "wd1")
        put(first, _gather_rows([shards[k] for k in first]))
    keys = ("wint",)
    (h1, n1, g1, u1, a1), ro = _ffn_fwd(xf, S["ffn1_norm"], W["wgt1"], W["wut1"], W["wd1"], rider=gather_rider(keys))
    put(keys, ro)
    keys = ("wout", "wat", "wbt", "wgt2")
    (un, za, zb, zg), ro = _inproj_fwd(h1, S["mix_norm"], W["wint"], S["b_in"], nseq, rider=gather_rider(keys))
    put(keys, ro)

    seq3 = lambda a: a.reshape(nseq, seq, a.shape[-1])
    zb3 = seq3(zb)
    pair_blk = lambda cb: pl.BlockSpec((1, 2, seq, 128), lambda n, r, cb=cb: (n, cb, 0, 0))
    a_cfg = []
    outs, lses = [], []
    for gi, (_, d) in enumerate(DIL_GROUPS):
        cfg = dict(grid=(nseq, d), seq=seq, stride=d, kvw=GW, split=True,
                   q_spec=pair_blk(gi), k_spec=pair_blk(3 + gi), v_spec=pair_blk(6 + gi), o_spec=pair_blk(0),
                   bias_map=lambda n, r: (0, 0, 0), sink_map=lambda n, r: (0, 0, 0), has_sink=False)
        a_cfg.append(cfg)
        (o, lse), _ = _attn_fwd(za, za, za, bias_all[4 * gi:4 * gi + 4], sink_0, o_shape=(nseq, 2, seq, 128),
                                o_dtype=F32, name=f"attn_a{gi}_fwd", **cfg)
        outs.append(o)
        lses.append(lse)
    wide_blk = lambda w, cmap: pl.BlockSpec((1, seq, w), cmap)
    b_cfg = dict(grid=(nseq, 2), seq=seq, stride=1, kvw=2 * HEAD_DIM, split=False,
                 q_spec=wide_blk(GW, lambda n, r: (n, 0, r)), k_spec=wide_blk(2 * HEAD_DIM, lambda n, r: (n, 0, 4)),
                 v_spec=wide_blk(2 * HEAD_DIM, lambda n, r: (n, 0, 5)), o_spec=wide_blk(GW, lambda n, r: (n, 0, r)),
                 bias_map=lambda n, r: (r, 0, 0), sink_map=lambda n, r: (r, 0, 0), has_sink=True)
    keys = ("wut2",)
    bias_b_fwd = bias_all[12:20].at[:, :, 0].set(jnp.broadcast_to(S["sinks"].reshape(8, 1), (8, BLOCK)))
    (yb, lse_b), ro = _attn_fwd(zb3, zb3, zb3, bias_b_fwd, sink_b, o_shape=(nseq, seq, 2 * GW), o_dtype=BF16,
                                name="attn_b_fwd", rider=gather_rider(keys), **b_cfg)
    put(keys, ro)
    yb = yb.reshape(t, 2 * GW)

    keys = ("wd2",)
    (h2, y, lse_tot, pa, pb, merged), ro = _merge_fwd(outs[0], outs[1], outs[2], lses[0], lses[1], lses[2], yb, zg, h1,
                                                      W["wat"], W["wbt"], W["wout"], rider=gather_rider(keys))
    put(keys, ro)
    (dh3, n2, g2, u2, a2, loss_part, g_final), _ = _ffn_fwd(
        h2, S["ffn2_norm"], W["wgt2"], W["wut2"], W["wd2"],
        head=(S["final_norm"].reshape(1, D_MODEL), tgt.reshape(t, D_MODEL)))

    GS["final_norm"] = g_final
    dh2, dg2, du2, df2, GS["ffn2_norm"] = _ffn_bwd(dh3, h2, S["ffn2_norm"], g2, u2, W["wgt2"], W["wut2"], W["wd2"])
    G["wgt2"] = _wgrad(dg2, n2, MXU_DIM, name="wgrad_gate2")
    G["wut2"] = _wgrad(du2, n2, MXU_DIM, name="wgrad_up2")
    G["wd2"] = _wgrad(a2, df2, MXU_DIM, name="wgrad_down2")

    keys = ("wgt2", "wut2", "wd2")
    rider = _ExchangeRider([pair(keys)]) if dist else None
    (dpa, dpb, dga, dgb, dya, dyb, dh2b, ca, cb), ro = _merge_bwd(dh2, pa, pb, zg, y, yb, W["wat"], W["wbt"], W["wout"],
                                                                  nseq, rider=rider)
    if dist:
        finish(keys, *ro)

    dqs, dks, dvs, dbs = [], [], [], []
    shp = (nseq, 2, seq, 128)
    halves = lambda a: [a[:, hf].reshape(t, 128).astype(BF16) for hf in range(2)]
    for gi in range(len(DIL_GROUPS)):
        dq, dk, dv, db, _ = _attn_bwd(za, za, za, bias_all[4 * gi:4 * gi + 4], sink_0, dya, ca, lse_tot,
                                      n_bias=4, dq_shape=shp, dkv_shape=shp, g_dtype=F32,
                                      kv_out_spec=a_cfg[gi]["o_spec"], name=f"attn_a{gi}_bwd", **a_cfg[gi])
        dqs += halves(dq)
        dks += halves(dk)
        dvs += halves(dv)
        dbs.append(db)
    dqb, dkb, dvb, dbb, dsink = _attn_bwd(zb3, zb3, zb3, bias_all[12:20], sink_b, seq3(dyb), seq3(cb), lse_b,
                                          n_bias=8, dq_shape=(nseq, seq, 2 * GW),
                                          dkv_shape=(nseq, seq, 2 * HEAD_DIM), g_dtype=BF16,
                                          kv_out_spec=wide_blk(2 * HEAD_DIM, lambda n, r: (n, 0, 0)),
                                          name="attn_b_bwd", **b_cfg)
    dz = jnp.concatenate(dqs + dks + dvs + [dqb.reshape(t, 2 * GW), dkb.reshape(t, 2 * HEAD_DIM),
                                            dvb.reshape(t, 2 * HEAD_DIM), dga, dgb], axis=-1)
    gb_tab = _bias_grad(jnp.concatenate(dbs + [dbb], axis=0), buckets)
    if dist:
        GS["bias_tab"], GS["sink_tiles"] = gb_tab, dsink
    else:
        GS["rel_bias"] = gb_tab[:, :20]
        GS["sinks"] = dsink[:, 0, 0].reshape(1, 8)

    G["wint"], GS["b_in"] = _wgrad(dz, un, MXU_DIM, with_colsum=True, name="wgrad_in")
    keys = ("wint",)
    rider = _ExchangeRider([pair(keys)]) if dist else None
    (dh1, GS["mix_norm"]), ro = _inproj_bwd(dz, dh2, h1, S["mix_norm"], W["wint"], rider=rider)
    if dist:
        finish(keys, *ro)

    dx, dg1, du1, df1, GS["ffn1_norm"] = _ffn_bwd(dh1, xf, S["ffn1_norm"], g1, u1, W["wgt1"], W["wut1"], W["wd1"])
    G["wgt1"] = _wgrad(dg1, n1, MXU_DIM, name="wgrad_gate1")
    if dist:
        G["wut1"], ro = _wgrad(du1, n1, MXU_DIM, name="wgrad_up1", rider=_ExchangeRider([pair(("wgt1",))]))
        finish(("wgt1",), *ro)
        G["wd1"], ro = _wgrad(a1, df1, MXU_DIM, name="wgrad_down1", rider=_ExchangeRider([pair(("wut1",))]))
        finish(("wut1",), *ro)
        G["wout"], ro = _wgrad(merged, dh2b, MXU_DIM, name="wgrad_out", rider=_ExchangeRider([pair(("wd1",))]))
        finish(("wd1",), *ro)
    else:
        G["wut1"] = _wgrad(du1, n1, MXU_DIM, name="wgrad_up1")
        G["wd1"] = _wgrad(a1, df1, MXU_DIM, name="wgrad_down1")
        G["wout"] = _wgrad(merged, dh2b, MXU_DIM, name="wgrad_out")
    G["wat"] = _wgrad(dpa, y, MXU_DIM, name="wgrad_branch_a")
    G["wbt"] = _wgrad(dpb, yb, MXU_DIM, name="wgrad_branch_b")
    if dist:
        keys = ("wout", "wat", "wbt")
        finish(keys, *_chip_exchange([pair(keys)]))
    return loss_part, dx.reshape(x.shape), (reduced if dist else G), GS


_SMALL = ("ffn1_norm", "mix_norm", "ffn2_norm", "final_norm", "b_in", "sinks", "rel_bias")
_ORDER = ("ffn1_norm", "ffn1_w_gate", "ffn1_w_up", "ffn1_w_down", "mix_norm", "w_in", "b_in", "w_branch_a",
          "w_branch_b", "w_out", "sinks", "rel_bias", "ffn2_norm", "ffn2_w_gate", "ffn2_w_up", "ffn2_w_down",
          "final_norm")
_BIG = (("wgt1", "ffn1_w_gate", True, 704), ("wut1", "ffn1_w_up", True, 704), ("wd1", "ffn1_w_down", False, 704),
        ("wint", "w_in", True, 1280), ("wout", "w_out", False, 256), ("wat", "w_branch_a", True, 64),
        ("wbt", "w_branch_b", True, 128), ("wgt2", "ffn2_w_gate", True, 704), ("wut2", "ffn2_w_up", True, 704),
        ("wd2", "ffn2_w_down", False, 704))
_FULL_SHAPE = {"wat": (D_MODEL, GW), "wbt": (D_MODEL, 2 * GW)}


def kernel(x, ffn1_norm, ffn1_w_gate, ffn1_w_up, ffn1_w_down, mix_norm, w_in, b_in, w_branch_a, w_branch_b, w_out, sinks, rel_bias, ffn2_norm, ffn2_w_gate, ffn2_w_up, ffn2_w_down, final_norm, loss_target, m_ffn1_norm, m_ffn1_w_gate, m_ffn1_w_up, m_ffn1_w_down, m_mix_norm, m_w_in, m_b_in, m_w_branch_a, m_w_branch_b, m_w_out, m_sinks, m_rel_bias, m_ffn2_norm, m_ffn2_w_gate, m_ffn2_w_up, m_ffn2_w_down, m_final_norm, v_ffn1_norm, v_ffn1_w_gate, v_ffn1_w_up, v_ffn1_w_down, v_mix_norm, v_w_in, v_b_in, v_w_branch_a, v_w_branch_b, v_w_out, v_sinks, v_rel_bias, v_ffn2_norm, v_ffn2_w_gate, v_ffn2_w_up, v_ffn2_w_down, v_final_norm):
    args = dict(locals())
    w = {n: args[n] for n in _ORDER}
    m = {n: args["m_" + n] for n in _ORDER}
    v = {n: args["v_" + n] for n in _ORDER}

    shards = {}
    for key, name, transposed, rows in _BIG:
        a = w[name][0]
        a = (a.T if transposed else a).astype(BF16)
        shards[key] = a.reshape(rows, D_MODEL)
    S = {n: w[n] for n in _SMALL}

    loss_part, grad_x, reduced, GS = _local_step(x, loss_target, {}, S, shards)

    small = _allreduce_small(GS["ffn1_norm"], GS["mix_norm"], GS["ffn2_norm"], GS["final_norm"], GS["b_in"],
                             GS["sink_tiles"], GS["bias_tab"], loss_part)
    loss = small[9, 8]

    out_g, out_d, out_m, out_v = {}, {}, {}, {}
    for key, n, transposed, rows in _BIG:
        nat = w[n][0].shape
        if transposed and nat[1] % 128:
            res = _adamw(w[n][0].T, reduced[key], m[n][0].T, v[n][0].T, "adamw_" + n)
            res = [reduced[key].T] + [r.T for r in res]
        else:
            g = reduced[key].reshape(nat[1], nat[0]).T if transposed else reduced[key].reshape(nat)
            res = [g] + list(_adamw(w[n][0], g, m[n][0], v[n][0], "adamw_" + n))
        out_g[n], out_d[n], out_m[n], out_v[n] = [r[None] for r in res]
    row = lambda d: {n: (d[n].reshape(1, D_MODEL) if n == "final_norm" else d[n]) for n in _SMALL}
    for dst, src in zip((out_g, out_d, out_m, out_v), _adamw_small(small, row(w), row(m), row(v))):
        dst.update(src)
        dst["final_norm"] = src["final_norm"].reshape(D_MODEL)

    return (loss, grad_x, *[out_g[n] for n in _ORDER], *[out_d[n] for n in _ORDER],
            *[out_m[n] for n in _ORDER], *[out_v[n] for n in _ORDER])
```

```python
import math

import jax
import jax.numpy as jnp
from jax import lax
from jax.experimental import pallas as pl
from jax.experimental.pallas import tpu as pltpu

F32, BF16 = jnp.float32, jnp.bfloat16
MESH = pl.DeviceIdType.MESH

D_MODEL = 1024
D_FF = 2816
D_IN = 5120
HEAD_DIM = 64
BLOCK = 128
DIL_GROUPS = ((128, 1), (512, 4), (2048, 16))
B_WINDOW = 128
N_BUCKETS = 32
MAX_DISTANCE = 2048
EPS = 1e-6
N_CHIPS = 4
GW = 256
ZA_W = 2304
ZB_W = 768
NEG = -1e30

ADAM_LR, ADAM_B1, ADAM_B2, ADAM_EPS, ADAM_WD, ADAM_STEP = 0.001, 0.9, 0.999, 1e-08, 0.01, 10

VMEM_BIG = 56 * 1024 * 1024
TM = 512
TM_BWD = 256
MXU_DIM = 256
FF_BOUNDS = (0, 4 * MXU_DIM, 8 * MXU_DIM, D_FF)
DMA_SPLIT = 8
RESIDUES_PER_STEP = 8
ATTN_UNROLL = 5


def _dot(a, b):
    return jnp.dot(a, b, preferred_element_type=F32)


def _dot_nt(a, b):
    return lax.dot_general(a, b, (((1,), (1,)), ((), ())), preferred_element_type=F32)


def _dot_tn(a, b):
    return lax.dot_general(a, b, (((0,), (0,)), ((), ())), preferred_element_type=F32)


def _sigmoid(x):
    return 0.5 * jnp.tanh(0.5 * x) + 0.5


def _params(sem, vmem=None):
    return pltpu.CompilerParams(dimension_semantics=sem, vmem_limit_bytes=vmem)


ANY = pl.BlockSpec(memory_space=pl.ANY)


def _me():
    return lax.axis_index("x"), lax.axis_index("y"), lax.axis_index("c")


_CHIP_RELS = ((1, 0), (0, 1), (1, 1))


def _flip(v, f):
    return 1 - v if f else v


def _remote(src, dst, ssem, rsem, peer):
    return pltpu.make_async_remote_copy(src_ref=src, dst_ref=dst, send_sem=ssem, recv_sem=rsem,
                                        device_id=peer, device_id_type=MESH)


def _row_pieces(rows, n):
    step = max(16, -(-rows // n) // 16 * 16)
    out, s = [], 0
    while s < rows:
        out.append((s, min(step, rows - s)))
        s += step
    return out


def _gather_rows(shards):
    nt = len(shards)
    rows = [s.shape[0] for s in shards]

    def body(*refs):
        srcs, outs = refs[:nt], refs[nt:2 * nt]
        halves, quarters = refs[2 * nt:3 * nt], refs[3 * nt:4 * nt]
        ici_s, ici_r, fwd_s, fwd_r, d2d_s, d2d_r, keep, loc = refs[4 * nt:]
        x, y, c = _me()
        j = 2 * x + y
        sib = (x, y, 1 - c)
        nbr = ((1 - x, y, c), (x, 1 - y, c))
        nbr_j = (2 * (1 - x) + y, 2 * x + (1 - y))
        diag_j = 2 * (1 - x) + (1 - y)
        local = [pltpu.make_async_copy(srcs[t], outs[t].at[j], loc.at[t]) for t in range(nt)]
        for cp in local:
            cp.start()
        pending = []
        for a in range(2):
            for t in range(nt):
                half = pl.ds(c * (rows[t] // 2), rows[t] // 2)
                cp = _remote(srcs[t].at[half], halves[t].at[a], ici_s.at[2 * t + a], ici_r.at[2 * t + a], nbr[a])
                cp.start()
                pending.append(cp)
        placed = []

        def place(src, dst_of, idx):
            mine = pltpu.make_async_copy(src, dst_of, keep.at[idx])
            mine.start()
            cp = _remote(src, dst_of, d2d_s.at[idx], d2d_r.at[idx], sib)
            cp.start()
            placed.append((mine, cp))

        for a in range(2):
            for t in range(nt):
                r2, r4 = rows[t] // 2, rows[t] // 4
                got = halves[t].at[a]
                _remote(got, got, ici_s.at[2 * t + a], ici_r.at[2 * t + a], nbr[a]).wait_recv()
                cp = _remote(halves[t].at[a, pl.ds(a * r4, r4)], quarters[t].at[a], fwd_s.at[2 * t + a],
                             fwd_r.at[2 * t + a], nbr[1 - a])
                cp.start()
                pending.append(cp)
                place(got, outs[t].at[nbr_j[a], pl.ds(c * r2, r2)], 4 * t + a)
        for a in range(2):
            for t in range(nt):
                r2, r4 = rows[t] // 2, rows[t] // 4
                got = quarters[t].at[a]
                _remote(got, got, fwd_s.at[2 * t + a], fwd_r.at[2 * t + a], nbr[1 - a]).wait_recv()
                place(got, outs[t].at[diag_j, pl.ds(c * r2 + a * r4, r4)], 4 * t + 2 + a)
        for mine, cp in placed:
            mine.wait()
            cp.wait()
        for cp in pending:
            cp.wait_send()
        for cp in local:
            cp.wait()

    stage = ([pltpu.VMEM((2, r // 2, D_MODEL), BF16) for r in rows] + [pltpu.VMEM((2, r // 4, D_MODEL), BF16) for r in rows])
    sems = ([pltpu.SemaphoreType.DMA((2 * nt,)) for _ in range(4)] + [pltpu.SemaphoreType.DMA((4 * nt,))] * 3
            + [pltpu.SemaphoreType.DMA((nt,))])
    return pl.pallas_call(
        body, name="gather_weights",
        out_shape=tuple(jax.ShapeDtypeStruct((N_CHIPS,) + s.shape, s.dtype) for s in shards),
        in_specs=[pl.BlockSpec(memory_space=pltpu.VMEM)] * nt,
        out_specs=tuple([ANY] * nt), scratch_shapes=stage + sems,
    )(*shards)


VMEM_WHOLE = pl.BlockSpec(memory_space=pltpu.VMEM)


def _pair_reduce(grads, name):
    nt = len(grads)
    r2 = [g.shape[2] for g in grads]
    off = [sum(r2[:t]) for t in range(nt)]
    tot = sum(r2)

    def body(*refs):
        gs = refs[:nt]
        s_ref, mine, got, ssem, rsem, lsem = refs[nt:]
        x, y, c = _me()
        sib = (x, y, 1 - c)
        for t in range(nt):
            for k in range(N_CHIPS):
                rows = pl.ds(off[t], r2[t])
                _remote(gs[t].at[k, 1 - c], got.at[k, rows], ssem, rsem, sib).start()
                pltpu.make_async_copy(gs[t].at[k, c], mine.at[k, rows], lsem).start()
        pltpu.make_async_copy(mine, mine, lsem).wait()
        _remote(got, got, ssem, rsem, sib).wait()
        for k in range(N_CHIPS):
            for st, sz in _row_pieces(tot, 4):
                rows = slice(st, st + sz)
                s_ref[k, rows, :] = (mine[k, rows, :].astype(F32) + got[k, rows, :].astype(F32)).astype(BF16)

    shp = jax.ShapeDtypeStruct((N_CHIPS, tot, D_MODEL), BF16)
    buf = pltpu.VMEM((N_CHIPS, tot, D_MODEL), BF16)
    return pl.pallas_call(
        body, name=name, out_shape=shp, in_specs=[ANY] * nt, out_specs=VMEM_WHOLE,
        scratch_shapes=[buf, buf, pltpu.SemaphoreType.DMA(()), pltpu.SemaphoreType.DMA(()),
                        pltpu.SemaphoreType.DMA(())],
        compiler_params=pltpu.CompilerParams(vmem_limit_bytes=VMEM_BIG),
    )(*grads)


def _chip_exchange(parts):
    ng = len(parts)
    r2 = [p.shape[1] for p in parts]
    off = [sum(r2[:g]) for g in range(ng)]
    tot = sum(r2)

    def body(*refs):
        ps = refs[:ng]
        own_ref, rec_ref, ssems, rsems, lsem = refs[ng:]
        x, y, c = _me()
        j = 2 * x + y
        for g in range(ng):
            pltpu.make_async_copy(ps[g].at[j], own_ref.at[pl.ds(off[g], r2[g])], lsem).start()
        for k, (fx, fy) in enumerate(_CHIP_RELS):
            px, py = _flip(x, fx), _flip(y, fy)
            for g in range(ng):
                for st, sz in _row_pieces(r2[g], 2):
                    _remote(ps[g].at[2 * px + py, pl.ds(st, sz)], rec_ref.at[k, pl.ds(off[g] + st, sz)],
                            ssems.at[k], rsems.at[k], (px, py, c)).start()
        for k in range(3):
            _remote(rec_ref.at[k], rec_ref.at[k], ssems.at[k], rsems.at[k], (x, y, c)).wait()
        pltpu.make_async_copy(own_ref, own_ref, lsem).wait()

    return pl.pallas_call(
        body, name="grad_chip_exchange",
        out_shape=(jax.ShapeDtypeStruct((tot, D_MODEL), BF16), jax.ShapeDtypeStruct((3, tot, D_MODEL), BF16)),
        in_specs=[VMEM_WHOLE] * ng, out_specs=(ANY, ANY),
        scratch_shapes=[pltpu.SemaphoreType.DMA((3,)), pltpu.SemaphoreType.DMA((3,)), pltpu.SemaphoreType.DMA(())],
    )(*parts)


def _final_reduce(own, rec, name):
    r2 = own.shape[0]
    pieces = _row_pieces(r2, DMA_SPLIT)

    def body(own_ref, rec_ref, o_ref, fbuf, ssem, rsem, lsem):
        x, y, c = _me()
        sib = (x, y, 1 - c)
        for st, sz in pieces:
            rows = slice(st, st + sz)
            fbuf[rows, :] = (own_ref[rows, :].astype(F32) + rec_ref[0, rows, :].astype(F32)
                             + rec_ref[1, rows, :].astype(F32) + rec_ref[2, rows, :].astype(F32))
            pltpu.make_async_copy(fbuf.at[pl.ds(st, sz)], o_ref.at[c, pl.ds(st, sz)], lsem).start()
            _remote(fbuf.at[pl.ds(st, sz)], o_ref.at[c, pl.ds(st, sz)], ssem, rsem, sib).start()
        _remote(fbuf, o_ref.at[c], ssem, rsem, sib).wait()
        pltpu.make_async_copy(fbuf, o_ref.at[c], lsem).wait()

    return pl.pallas_call(
        body, name=name, out_shape=jax.ShapeDtypeStruct((2, r2, D_MODEL), F32),
        in_specs=[VMEM_WHOLE, VMEM_WHOLE], out_specs=ANY,
        scratch_shapes=[pltpu.VMEM((r2, D_MODEL), F32), pltpu.SemaphoreType.DMA(()), pltpu.SemaphoreType.DMA(()),
                        pltpu.SemaphoreType.DMA(())],
        compiler_params=pltpu.CompilerParams(vmem_limit_bytes=VMEM_BIG),
    )(own, rec)


SMALL_ROWS = 48


def _allreduce_small(g_ffn1, g_mix, g_ffn2, g_final, g_bin, dsink, bias_tab, loss_part):
    def body(f1_ref, mx_ref, f2_ref, fn_ref, bi_ref, sk_ref, bt_ref, ls_ref, o_ref, mine, buf, send_sems, recv_sems):
        x, y, c = _me()
        me = 4 * x + 2 * y + c
        mine[...] = jnp.zeros_like(mine)
        for r, ref in enumerate((f1_ref, mx_ref, f2_ref, fn_ref)):
            mine[r:r + 1, :] = ref[...]
        for k in range(D_IN // D_MODEL):
            mine[4 + k:5 + k, :] = bi_ref[:, k * D_MODEL:(k + 1) * D_MODEL]
        lane = lax.broadcasted_iota(jnp.int32, (1, 128), 1)
        row = jnp.where(lane == 8, ls_ref[0:1, :], 0.0)
        for h in range(8):
            row = jnp.where(lane == h, sk_ref[h, 0:1, :], row)
        mine[9:10, 0:128] = row
        mine[16:48, 0:128] = bt_ref[...]
        buf[me] = mine[...]
        copies = []
        for k in range(1, 8):
            peer = (_flip(x, (k >> 2) & 1), _flip(y, (k >> 1) & 1), _flip(c, k & 1))
            cp = _remote(mine, buf.at[me], send_sems.at[k - 1], recv_sems.at[k - 1], peer)
            cp.start()
            copies.append(cp)
        for cp in copies:
            cp.wait()
        acc = buf[0]
        for i in range(1, 8):
            acc = acc + buf[i]
        o_ref[...] = acc

    vm = pl.BlockSpec(memory_space=pltpu.VMEM)
    shape = (SMALL_ROWS, D_MODEL)
    return pl.pallas_call(
        body, name="allreduce_small", out_shape=jax.ShapeDtypeStruct(shape, F32),
        in_specs=[vm] * 8, out_specs=vm,
        scratch_shapes=[pltpu.VMEM(shape, F32), pltpu.VMEM((8,) + shape, F32), pltpu.SemaphoreType.DMA((7,)),
                        pltpu.SemaphoreType.DMA((7,))],
    )(g_ffn1, g_mix, g_ffn2, g_final, g_bin, dsink, bias_tab, loss_part)


def _adam_update(w, g, m, v):
    nm = ADAM_B1 * m + (1.0 - ADAM_B1) * g
    nv = ADAM_B2 * v + (1.0 - ADAM_B2) * (g * g)
    bc1 = 1.0 - ADAM_B1 ** ADAM_STEP
    bc2 = 1.0 - ADAM_B2 ** ADAM_STEP
    return -ADAM_LR * ((nm / bc1) / (jnp.sqrt(nv / bc2) + ADAM_EPS) + ADAM_WD * w), nm, nv


def _adamw_small(packed, w, m, v):
    names = ("ffn1_norm", "mix_norm", "ffn2_norm", "final_norm", "b_in", "sinks", "rel_bias")
    nn = len(names)

    def grad_of(p_ref, name, k=0):
        if name == "b_in":
            return p_ref[4 + k:5 + k, :]
        if name == "sinks":
            return p_ref[9:10, 0:8]
        if name == "rel_bias":
            return p_ref[16:48, 0:20]
        r = names.index(name)
        return p_ref[r:r + 1, :]

    def body(p_ref, *refs):
        ws, ms, vs = refs[:nn], refs[nn:2 * nn], refs[2 * nn:3 * nn]
        outs = refs[3 * nn:]
        for i, name in enumerate(names):
            og, od, om, ov = outs[i], outs[nn + i], outs[2 * nn + i], outs[3 * nn + i]
            pieces = range(D_IN // D_MODEL) if name == "b_in" else (0,)
            for k in pieces:
                sl = (slice(None), slice(k * D_MODEL, (k + 1) * D_MODEL)) if name == "b_in" else (Ellipsis,)
                g = grad_of(p_ref, name, k)
                d, nm, nv = _adam_update(ws[i][sl], g, ms[i][sl], vs[i][sl])
                og[sl], od[sl], om[sl], ov[sl] = g, d, nm, nv

    vm = pl.BlockSpec(memory_space=pltpu.VMEM)
    shapes = [jax.ShapeDtypeStruct(w[n].shape, F32) for n in names]
    res = pl.pallas_call(
        body, name="adamw_small", out_shape=tuple(shapes * 4), in_specs=[vm] * (1 + 3 * nn),
        out_specs=tuple([vm] * (4 * nn)),
    )(packed, *[w[n] for n in names], *[m[n] for n in names], *[v[n] for n in names])
    return [dict(zip(names, res[i * nn:(i + 1) * nn])) for i in range(4)]


class _GatherRider:
    def __init__(self, shards):
        self.inputs = list(shards)
        nt = len(shards)
        self.out_shape = [jax.ShapeDtypeStruct((N_CHIPS,) + s.shape, s.dtype) for s in shards]
        self.scratch = [pltpu.SemaphoreType.DMA((3 * nt,)), pltpu.SemaphoreType.DMA((3 * nt,)),
                        pltpu.SemaphoreType.DMA((nt,))]

    def _copies(self, srcs, outs, sems):
        ici_s, ici_r, loc = sems
        x, y, c = _me()
        j = 2 * x + y
        local = [pltpu.make_async_copy(srcs[t], outs[t].at[j], loc.at[t]) for t in range(len(srcs))]
        remote = []
        for k, (fx, fy) in enumerate(_CHIP_RELS):
            peer = (_flip(x, fx), _flip(y, fy), c)
            for t in range(len(srcs)):
                remote.append(_remote(srcs[t], outs[t].at[j], ici_s.at[3 * t + k], ici_r.at[3 * t + k], peer))
        return local, remote

    def start(self, srcs, outs, sems):
        local, remote = self._copies(srcs, outs, sems)
        for cp in local + remote:
            cp.start()

    def finish(self, srcs, outs, sems):
        local, remote = self._copies(srcs, outs, sems)
        for cp in remote + local:
            cp.wait()


class _ExchangeRider:
    def __init__(self, parts):
        self.inputs = list(parts)
        self.r2 = [p.shape[1] for p in parts]
        self.off = [sum(self.r2[:g]) for g in range(len(parts))]
        tot = sum(self.r2)
        self.out_shape = [jax.ShapeDtypeStruct((tot, D_MODEL), BF16), jax.ShapeDtypeStruct((3, tot, D_MODEL), BF16)]
        self.scratch = [pltpu.SemaphoreType.DMA((3,)), pltpu.SemaphoreType.DMA((3,)), pltpu.SemaphoreType.DMA(())]

    def start(self, ps, outs, sems):
        own_ref, rec_ref = outs
        ssems, rsems, lsem = sems
        x, y, c = _me()
        j = 2 * x + y
        for g in range(len(ps)):
            pltpu.make_async_copy(ps[g].at[j], own_ref.at[pl.ds(self.off[g], self.r2[g])], lsem).start()
        for k, (fx, fy) in enumerate(_CHIP_RELS):
            px, py = _flip(x, fx), _flip(y, fy)
            for g in range(len(ps)):
                for st, sz in _row_pieces(self.r2[g], 2):
                    _remote(ps[g].at[2 * px + py, pl.ds(st, sz)], rec_ref.at[k, pl.ds(self.off[g] + st, sz)],
                            ssems.at[k], rsems.at[k], (px, py, c)).start()

    def finish(self, ps, outs, sems):
        own_ref, rec_ref = outs
        ssems, rsems, lsem = sems
        x, y, c = _me()
        for k in range(3):
            _remote(rec_ref.at[k], rec_ref.at[k], ssems.at[k], rsems.at[k], (x, y, c)).wait()
        pltpu.make_async_copy(own_ref, own_ref, lsem).wait()


def _pallas(body, args, *, name, grid, in_specs, out_specs, out_shape, scratch_shapes=(), sem=None, vmem=None,
            rider=None):
    if rider is None:
        res = pl.pallas_call(body, name=name, grid=grid, in_specs=list(in_specs), out_specs=tuple(out_specs),
                             out_shape=tuple(out_shape), scratch_shapes=list(scratch_shapes),
                             compiler_params=_params(sem, vmem))(*args)
        return tuple(res), ()
    n_in, n_out, n_sc = len(in_specs), len(out_shape), len(scratch_shapes)
    r_in, r_out = len(rider.inputs), len(rider.out_shape)

    def wrapped(*refs):
        ins, rins = refs[:n_in], refs[n_in:n_in + r_in]
        p = n_in + r_in
        outs, routs = refs[p:p + n_out], refs[p + n_out:p + n_out + r_out]
        p += n_out + r_out
        scr, rsems = refs[p:p + n_sc], refs[p + n_sc:]
        first = pl.program_id(0) == 0
        last = pl.program_id(0) == grid[0] - 1
        for a in range(1, len(grid)):
            first = first & (pl.program_id(a) == 0)
            last = last & (pl.program_id(a) == grid[a] - 1)

        @pl.when(first)
        def _():
            rider.start(rins, routs, rsems)

        body(*ins, *outs, *scr)

        @pl.when(last)
        def _():
            rider.finish(rins, routs, rsems)

    res = pl.pallas_call(
        wrapped, name=name, grid=grid, in_specs=list(in_specs) + [ANY] * r_in,
        out_specs=tuple(out_specs) + (ANY,) * r_out, out_shape=tuple(out_shape) + tuple(rider.out_shape),
        scratch_shapes=list(scratch_shapes) + rider.scratch,
        compiler_params=_params(("arbitrary",) * len(grid), vmem))(*args, *rider.inputs)
    return tuple(res[:n_out]), tuple(res[n_out:])


def _loss_tile(hh, gain, tgt):
    r = lax.rsqrt(jnp.mean(hh * hh, axis=-1, keepdims=True) + EPS)
    hn = hh * r
    err = hn * gain - tgt
    part = (0.5 / D_MODEL) * jnp.sum(jnp.sum(err * err, axis=1, keepdims=True), axis=0, keepdims=True)
    dy = err * (1.0 / D_MODEL)
    dng = dy * gain
    dh = r * (dng - hn * jnp.mean(dng * hn, axis=-1, keepdims=True))
    return dh, part, jnp.sum(dy * hn, axis=0, keepdims=True)


def _ffn_fwd(h, gain, wgt, wut, wd, rider=None, head=None):
    t = h.shape[0]

    def body(h_ref, gain_ref, wg_hbm, wu_hbm, wd_hbm, *rest):
        if head is None:
            hout_ref, n_ref, g_ref, u_ref, a_ref, wg_v, wu_v, wd_v = rest
        else:
            fg_ref, tgt_ref, hout_ref, n_ref, g_ref, u_ref, a_ref, loss_ref, gg_ref, wg_v, wu_v, wd_v = rest

        @pl.when(pl.program_id(0) == 0)
        def _():
            pltpu.sync_copy(wg_hbm, wg_v)
            pltpu.sync_copy(wu_hbm, wu_v)
            pltpu.sync_copy(wd_hbm, wd_v)
            if head is not None:
                loss_ref[...] = jnp.zeros_like(loss_ref)
                gg_ref[...] = jnp.zeros_like(gg_ref)

        hh = h_ref[...]
        r = lax.rsqrt(jnp.mean(hh * hh, axis=-1, keepdims=True) + EPS)
        n = (hh * r * gain_ref[...]).astype(BF16)
        n_ref[...] = n
        acc = jnp.zeros((TM, D_MODEL), F32)
        for c0, c1 in zip(FF_BOUNDS[:-1], FF_BOUNDS[1:]):
            sl = slice(c0, c1)
            g = _dot_nt(n, wg_v[sl, :])
            u = _dot_nt(n, wu_v[sl, :])
            sg = _sigmoid(g)
            silu = g * sg
            a = (silu * u).astype(BF16)
            a_ref[:, sl] = a
            g_ref[:, sl] = (u * (sg * (1.0 + g * (1.0 - sg)))).astype(BF16)
            u_ref[:, sl] = silu.astype(BF16)
            acc = acc + _dot(a, wd_v[sl, :])
        hout = hh + 0.5 * acc
        if head is None:
            hout_ref[...] = hout
        else:
            dh, part, gpart = _loss_tile(hout, fg_ref[...], tgt_ref[...])
            hout_ref[...] = dh
            loss_ref[...] += part
            gg_ref[...] += gpart

    row = lambda w: pl.BlockSpec((TM, w), lambda i: (i, 0))
    vec = pl.BlockSpec((1, D_MODEL), lambda i: (0, 0))
    wv = pltpu.VMEM((D_FF, D_MODEL), BF16)
    args, in_specs = (h, gain, wgt, wut, wd), [row(D_MODEL), vec, ANY, ANY, ANY]
    out_shape = [jax.ShapeDtypeStruct((t, D_MODEL), F32), jax.ShapeDtypeStruct((t, D_MODEL), BF16)] + [
        jax.ShapeDtypeStruct((t, D_FF), BF16)] * 3
    out_specs = [row(D_MODEL), row(D_MODEL), row(D_FF), row(D_FF), row(D_FF)]
    if head is not None:
        args, in_specs = args + tuple(head), in_specs + [vec, row(D_MODEL)]
        out_shape += [jax.ShapeDtypeStruct((8, 128), F32), jax.ShapeDtypeStruct((1, D_MODEL), F32)]
        out_specs += [pl.BlockSpec((8, 128), lambda i: (0, 0)), vec]
    return _pallas(
        body, args, name="ffn_fwd", grid=(t // TM,), out_shape=tuple(out_shape), in_specs=in_specs,
        out_specs=tuple(out_specs), scratch_shapes=[wv, wv, wv], sem=("arbitrary",), vmem=VMEM_BIG, rider=rider)


def _ffn_bwd(dhout, h, gain, dgf, duf, wgt, wut, wd):
    t = h.shape[0]
    tm = TM_BWD

    def body(dho_ref, h_ref, gain_ref, g_ref, u_ref, wg_hbm, wu_hbm, wd_hbm,
             dh_ref, dg_ref, du_ref, df_ref, gg_ref, wg_v, wu_v, wd_v):
        @pl.when(pl.program_id(0) == 0)
        def _():
            pltpu.sync_copy(wg_hbm, wg_v)
            pltpu.sync_copy(wu_hbm, wu_v)
            pltpu.sync_copy(wd_hbm, wd_v)
            gg_ref[...] = jnp.zeros_like(gg_ref)

        dho = dho_ref[...]
        df = (0.5 * dho).astype(BF16)
        df_ref[...] = df
        dn = jnp.zeros((tm, D_MODEL), F32)
        for c0, c1 in zip(FF_BOUNDS[:-1], FF_BOUNDS[1:]):
            sl = slice(c0, c1)
            da = _dot_nt(df, wd_v[sl, :])
            dg = (da * g_ref[:, sl].astype(F32)).astype(BF16)
            du = (da * u_ref[:, sl].astype(F32)).astype(BF16)
            dg_ref[:, sl] = dg
            du_ref[:, sl] = du
            dn = dn + _dot(dg, wg_v[sl, :]) + _dot(du, wu_v[sl, :])
        hh = h_ref[...]
        r = lax.rsqrt(jnp.mean(hh * hh, axis=-1, keepdims=True) + EPS)
        hn = hh * r
        gg_ref[...] += jnp.sum(dn * hn, axis=0, keepdims=True)
        dng = dn * gain_ref[...]
        dh_ref[...] = dho + r * (dng - hn * jnp.mean(dng * hn, axis=-1, keepdims=True))

    row = lambda w: pl.BlockSpec((tm, w), lambda i: (i, 0))
    vec = pl.BlockSpec((1, D_MODEL), lambda i: (0, 0))
    wv = pltpu.VMEM((D_FF, D_MODEL), BF16)
    return pl.pallas_call(
        body, name="ffn_bwd", grid=(t // tm,),
        out_shape=(jax.ShapeDtypeStruct((t, D_MODEL), F32), jax.ShapeDtypeStruct((t, D_FF), BF16),
                   jax.ShapeDtypeStruct((t, D_FF), BF16),
                   jax.ShapeDtypeStruct((t, D_MODEL), BF16), jax.ShapeDtypeStruct((1, D_MODEL), F32)),
        in_specs=[row(D_MODEL), row(D_MODEL), vec, row(D_FF), row(D_FF), ANY, ANY, ANY],
        out_specs=(row(D_MODEL), row(D_FF), row(D_FF), row(D_MODEL), vec),
        scratch_shapes=[wv, wv, wv],
        compiler_params=_params(("arbitrary",), VMEM_BIG),
    )(dhout, h, gain, dgf, duf, wgt, wut, wd)


def _wgrad(lhs, rhs, rb, with_colsum=False, name="wgrad", rider=None):
    t, k = lhs.shape
    n = rhs.shape[1]

    def body(l_ref, r_ref, o_ref, *rest):
        o_ref[...] = _dot_tn(l_ref[...], r_ref[...]).astype(BF16)
        if with_colsum:
            rest[0][...] = jnp.sum(l_ref[...].astype(F32), axis=0, keepdims=True)

    out_shape = [jax.ShapeDtypeStruct((k, n), BF16)]
    out_specs = [pl.BlockSpec((rb, n), lambda j: (j, 0))]
    if with_colsum:
        out_shape.append(jax.ShapeDtypeStruct((1, k), F32))
        out_specs.append(pl.BlockSpec((1, rb), lambda j: (0, j)))
    res, ro = _pallas(
        body, (lhs, rhs), name=name, grid=(k // rb,), out_shape=tuple(out_shape),
        in_specs=[pl.BlockSpec((t, rb), lambda j: (0, j)), pl.BlockSpec((t, n), lambda j: (0, 0))],
        out_specs=tuple(out_specs), sem=("arbitrary",), vmem=VMEM_BIG, rider=rider)
    if rider is not None:
        return res[0], ro
    return res if with_colsum else res[0]


def _lane_blocks(nseq, seq, nblk, tm=TM):
    spt = seq // tm
    return pl.BlockSpec((1, nblk, tm, 128), lambda i: (i // spt, 0, i % spt, 0))


def _inproj_fwd(h, gain, wint, b_in, nseq, rider=None):
    t = h.shape[0]
    seq = t // nseq
    cut_a = 5 * MXU_DIM
    pieces = ((0, cut_a, 0, 0), (cut_a, ZA_W - cut_a, 0, cut_a), (ZA_W, ZB_W, 1, 0), (ZA_W + ZB_W, 1024, 2, 0),
              (ZA_W + ZB_W + 1024, 1024, 2, 1024))

    def body(h_ref, gain_ref, w_hbm, b_ref, u_ref, za_ref, zb_ref, zg_ref, w_v):
        @pl.when(pl.program_id(0) == 0)
        def _():
            pltpu.sync_copy(w_hbm, w_v)

        hh = h_ref[...]
        r = lax.rsqrt(jnp.mean(hh * hh, axis=-1, keepdims=True) + EPS)
        un = (hh * r * gain_ref[...]).astype(BF16)
        u_ref[...] = un
        outs = (None, zb_ref, zg_ref)
        for c0, cw, oi, o0 in pieces:
            val = _dot_nt(un, w_v[c0:c0 + cw, :]) + b_ref[:, c0:c0 + cw]
            if oi == 0:
                for cb in range(cw // 128):
                    za_ref[0, o0 // 128 + cb] = val[:, cb * 128:(cb + 1) * 128]
            else:
                outs[oi][:, o0:o0 + cw] = val.astype(BF16)

    row = lambda w: pl.BlockSpec((TM, w), lambda i: (i, 0))
    return _pallas(
        body, (h, gain, wint, b_in), name="inproj_fwd", grid=(t // TM,),
        out_shape=(jax.ShapeDtypeStruct((t, D_MODEL), BF16), jax.ShapeDtypeStruct((nseq, ZA_W // 128, seq, 128), F32),
                   jax.ShapeDtypeStruct((t, ZB_W), BF16), jax.ShapeDtypeStruct((t, 2 * D_MODEL), BF16)),
        in_specs=[row(D_MODEL), pl.BlockSpec((1, D_MODEL), lambda i: (0, 0)), ANY,
                  pl.BlockSpec((1, D_IN), lambda i: (0, 0))],
        out_specs=(row(D_MODEL), _lane_blocks(nseq, seq, ZA_W // 128), row(ZB_W), row(2 * D_MODEL)),
        scratch_shapes=[pltpu.VMEM((D_IN, D_MODEL), BF16)], sem=("arbitrary",), vmem=VMEM_BIG, rider=rider)


def _inproj_bwd(dz, dh2, h, gain, wint, rider=None):
    t = h.shape[0]
    nc = 5
    cw = D_IN // nc

    def body(dz_ref, dh2_ref, h_ref, gain_ref, w_hbm, dh_ref, gg_ref, w_v):
        @pl.when(pl.program_id(0) == 0)
        def _():
            pltpu.sync_copy(w_hbm, w_v)
            gg_ref[...] = jnp.zeros_like(gg_ref)

        du = jnp.zeros((TM, D_MODEL), F32)
        for ci in range(nc):
            sl = slice(ci * cw, (ci + 1) * cw)
            du = du + _dot(dz_ref[:, sl], w_v[sl, :])
        hh = h_ref[...]
        r = lax.rsqrt(jnp.mean(hh * hh, axis=-1, keepdims=True) + EPS)
        hn = hh * r
        gg_ref[...] += jnp.sum(du * hn, axis=0, keepdims=True)
        dng = du * gain_ref[...]
        dh_ref[...] = dh2_ref[...] + r * (dng - hn * jnp.mean(dng * hn, axis=-1, keepdims=True))

    row = lambda w: pl.BlockSpec((TM, w), lambda i: (i, 0))
    vec = pl.BlockSpec((1, D_MODEL), lambda i: (0, 0))
    return _pallas(
        body, (dz, dh2, h, gain, wint), name="inproj_bwd", grid=(t // TM,),
        out_shape=(jax.ShapeDtypeStruct((t, D_MODEL), F32), jax.ShapeDtypeStruct((1, D_MODEL), F32)),
        in_specs=[row(D_IN), row(D_MODEL), row(D_MODEL), vec, ANY],
        out_specs=(row(D_MODEL), vec),
        scratch_shapes=[pltpu.VMEM((D_IN, D_MODEL), BF16)], sem=("arbitrary",), vmem=VMEM_BIG, rider=rider)


def _head_sums(x):
    w = x.shape[1]
    i = lax.broadcasted_iota(jnp.int32, (w, w), 0) // HEAD_DIM
    j = lax.broadcasted_iota(jnp.int32, (w, w), 1) // HEAD_DIM
    ones = (i == j).astype(BF16)
    hi = x.astype(BF16)
    r1 = x - hi.astype(F32)
    mid = r1.astype(BF16)
    lo = (r1 - mid.astype(F32)).astype(BF16)
    return _dot(hi, ones) + _dot(mid, ones) + _dot(lo, ones)


def _merge_fwd(o0, o1, o2, l0, l1, l2, yb, zg, h1, wat, wbt, wout, rider=None):
    t = h1.shape[0]
    nseq, _, seq, _ = o0.shape

    def body(o0_ref, o1_ref, o2_ref, l0_ref, l1_ref, l2_ref, yb_ref, ga_ref, gb_ref, h1_ref, wa_ref, wb_ref, wo_ref,
             h2_ref, y_ref, lt_ref, pa_ref, pb_ref, mg_ref):
        wide = lambda ref: jnp.concatenate([ref[0, 0], ref[0, 1]], axis=1)
        la, lb, lc = wide(l0_ref), wide(l1_ref), wide(l2_ref)
        mx = jnp.maximum(jnp.maximum(la, lb), lc)
        ea, eb, ec = jnp.exp(la - mx), jnp.exp(lb - mx), jnp.exp(lc - mx)
        den = ea + eb + ec
        y = (ea * wide(o0_ref) + eb * wide(o1_ref) + ec * wide(o2_ref)) / den
        lt = mx + jnp.log(den)
        lt_ref[0, 0] = lt[:, :128]
        lt_ref[0, 1] = lt[:, 128:]
        yb16 = y.astype(BF16)
        y_ref[...] = yb16
        pa = _dot_nt(yb16, wa_ref[...])
        pb = _dot_nt(yb_ref[...], wb_ref[...])
        pa_ref[...] = pa.astype(BF16)
        pb_ref[...] = pb.astype(BF16)
        mg = (_sigmoid(ga_ref[...].astype(F32)) * pa + _sigmoid(gb_ref[...].astype(F32)) * pb).astype(BF16)
        mg_ref[...] = mg
        h2_ref[...] = h1_ref[...] + _dot(mg, wo_ref[...])

    row = lambda w: pl.BlockSpec((TM, w), lambda i: (i, 0))
    full = lambda a: pl.BlockSpec(a.shape, lambda i: (0, 0))
    gate = lambda cb: pl.BlockSpec((TM, D_MODEL), lambda i: (i, cb))
    return _pallas(
        body, (o0, o1, o2, l0, l1, l2, yb, zg, zg, h1, wat, wbt, wout), name="merge_fwd", grid=(t // TM,),
        out_shape=(jax.ShapeDtypeStruct((t, D_MODEL), F32), jax.ShapeDtypeStruct((t, GW), BF16),
                   jax.ShapeDtypeStruct((nseq, 2, seq, 128), F32), jax.ShapeDtypeStruct((t, D_MODEL), BF16),
                   jax.ShapeDtypeStruct((t, D_MODEL), BF16), jax.ShapeDtypeStruct((t, D_MODEL), BF16)),
        in_specs=[_lane_blocks(nseq, seq, 2)] * 6 + [row(2 * GW), gate(0), gate(1), row(D_MODEL), full(wat), full(wbt),
                                                     full(wout)],
        out_specs=(row(D_MODEL), row(GW), _lane_blocks(nseq, seq, 2), row(D_MODEL), row(D_MODEL), row(D_MODEL)),
        sem=("parallel",), vmem=VMEM_BIG, rider=rider)


def _merge_bwd(dh2, pa, pb, zg, y, yb, wat, wbt, wout, nseq, rider=None):
    t = dh2.shape[0]

    def body(dh2_ref, pa_ref, pb_ref, ga_ref, gb_ref, y_ref, yb_ref, wa_ref, wb_ref, wo_ref,
             dpa_ref, dpb_ref, dga_ref, dgb_ref, dya_ref, dyb_ref, dh2b_ref, ca_ref, cb_ref):
        d16 = dh2_ref[...].astype(BF16)
        dh2b_ref[...] = d16
        dm = _dot_nt(d16, wo_ref[...])
        sa = _sigmoid(ga_ref[...].astype(F32))
        sb = _sigmoid(gb_ref[...].astype(F32))
        dpa = (dm * sa).astype(BF16)
        dpb = (dm * sb).astype(BF16)
        dpa_ref[...] = dpa
        dpb_ref[...] = dpb
        dga_ref[...] = (dm * pa_ref[...].astype(F32) * sa * (1.0 - sa)).astype(BF16)
        dgb_ref[...] = (dm * pb_ref[...].astype(F32) * sb * (1.0 - sb)).astype(BF16)
        dya = _dot(dpa, wa_ref[...])
        dyb = _dot(dpb, wb_ref[...])
        dya_ref[0, 0] = dya[:, :128]
        dya_ref[0, 1] = dya[:, 128:]
        dyb_ref[...] = dyb.astype(BF16)
        ca = _head_sums(dya * y_ref[...].astype(F32))
        ca_ref[0, 0] = ca[:, :128]
        ca_ref[0, 1] = ca[:, 128:]
        cb_ref[...] = _head_sums(dyb * yb_ref[...].astype(F32))

    row = lambda w: pl.BlockSpec((TM, w), lambda i: (i, 0))
    full = lambda a: pl.BlockSpec(a.shape, lambda i: (0, 0))
    gate = lambda cb: pl.BlockSpec((TM, D_MODEL), lambda i: (i, cb))
    bf = lambda w: jax.ShapeDtypeStruct((t, w), BF16)
    lanes = jax.ShapeDtypeStruct((nseq, 2, t // nseq, 128), F32)
    lane_spec = _lane_blocks(nseq, t // nseq, 2)
    return _pallas(
        body, (dh2, pa, pb, zg, zg, y, yb, wat, wbt, wout), name="merge_bwd", grid=(t // TM,),
        out_shape=(bf(D_MODEL), bf(D_MODEL), bf(D_MODEL), bf(D_MODEL), lanes, bf(2 * GW), bf(D_MODEL),
                   lanes, jax.ShapeDtypeStruct((t, 2 * GW), F32)),
        in_specs=[row(D_MODEL), row(D_MODEL), row(D_MODEL), gate(0), gate(1), row(GW), row(2 * GW),
                  full(wat), full(wbt), full(wout)],
        out_specs=(row(D_MODEL), row(D_MODEL), row(D_MODEL), row(D_MODEL), lane_spec, row(2 * GW), row(D_MODEL),
                   lane_spec, row(2 * GW)),
        sem=("parallel",), vmem=VMEM_BIG, rider=rider)


def _lane_head(rows):
    return lax.broadcasted_iota(jnp.int32, (rows, GW), 1) // HEAD_DIM


def _kv_expand_matrix(r):
    ci = lax.broadcasted_iota(jnp.int32, (2 * HEAD_DIM, GW), 0)
    ji = lax.broadcasted_iota(jnp.int32, (2 * HEAD_DIM, GW), 1)
    return (ci == (ji % HEAD_DIM) + HEAD_DIM * r).astype(BF16)


def _block_rows(row0, stride, ib):
    start = row0 + (stride * BLOCK) * ib
    if stride > 1:
        return pl.ds(start, BLOCK, stride=stride)
    return pl.ds(pl.multiple_of(start, BLOCK), BLOCK)


def _stack_heads(x, lane_head):
    return jnp.concatenate([jnp.where(lane_head == h, x, jnp.zeros_like(x)) for h in range(4)], axis=0)


def _unstack_heads(x4, lane_head):
    out = jnp.zeros((BLOCK, GW), F32)
    for h in range(4):
        out = jnp.where(lane_head == h, x4[h * BLOCK:(h + 1) * BLOCK], out)
    return out


def _load_rows(ref, rows, split):
    if split:
        return jnp.concatenate([ref[0, 0, rows, :], ref[0, 1, rows, :]], axis=1)
    return ref[0, rows, :]


def _store_rows(ref, rows, val, split):
    if split:
        ref[0, 0, rows, :] = val[:, :128]
        ref[0, 1, rows, :] = val[:, 128:]
    else:
        ref[0, rows, :] = val


def _attn_fwd(q_arr, k_arr, v_arr, bias, sink, *, grid, seq, stride, kvw, split, q_spec, k_spec, v_spec, bias_map,
              sink_map, o_spec, has_sink, o_shape, o_dtype, name, rider=None):
    nb = seq // stride // BLOCK
    scale = HEAD_DIM ** -0.5
    expanded = kvw != GW
    rps = min(stride, RESIDUES_PER_STEP)
    grid = (grid[0], grid[1] // rps)
    assert not has_sink or B_WINDOW - 1 < BLOCK

    def body(q_ref, k_ref, v_ref, bias_ref, sink_ref, o_ref, lse_ref, *kv_x):
        rr = pl.program_id(1)
        lane_head = _lane_head(BLOCK)
        if expanded:
            expand = _kv_expand_matrix(rr)
            kv_x[0][...] = _dot(k_ref[0], expand).astype(BF16)
            kv_x[1][...] = _dot(v_ref[0], expand).astype(BF16)
        for j in range(rps):
            residue(rr * rps + j if stride > 1 else 0, q_ref, k_ref, v_ref, bias_ref, sink_ref, o_ref, lse_ref, kv_x,
                    lane_head)

    def residue(row0, q_ref, k_ref, v_ref, bias_ref, sink_ref, o_ref, lse_ref, kv_x, lane_head):
        def per_head(fn, x):
            return jnp.concatenate([fn(sink_ref[0, h:h + 1, 0:1], x[h * BLOCK:(h + 1) * BLOCK]) for h in range(4)],
                                   axis=0)

        def load(ref, ib):
            return _load_rows(ref, _block_rows(row0, stride, ib), split).astype(BF16)

        def load_kv(which, ib):
            if expanded:
                return kv_x[which][_block_rows(0, 1, ib), :]
            return load((k_ref, v_ref)[which], ib)

        def block(ib, first):
            q4 = _stack_heads(load(q_ref, ib), lane_head)
            if first:
                kc, vc = load_kv(0, ib), load_kv(1, ib)
                b4 = bias_ref[:, :, BLOCK:].reshape(4 * BLOCK, BLOCK)
            else:
                kc = jnp.concatenate([load_kv(0, ib - 1), load_kv(0, ib)], axis=0)
                vc = jnp.concatenate([load_kv(1, ib - 1), load_kv(1, ib)], axis=0)
                b4 = bias_ref[...].reshape(4 * BLOCK, 2 * BLOCK)
                if has_sink:
                    oldest = lax.broadcasted_iota(jnp.int32, kc.shape, 0) == 0
                    kc = jnp.where(oldest, jnp.zeros_like(kc), kc)
                    vc = jnp.where(oldest, jnp.zeros_like(vc), vc)
            s = _dot_nt(q4, kc) * scale + b4
            m = jnp.max(s, axis=-1, keepdims=True)
            if has_sink and first:
                m = per_head(jnp.maximum, m)
            p = jnp.exp(s - m)
            l = jnp.sum(p, axis=-1, keepdims=True)
            if has_sink and first:
                l = l + per_head(lambda sk, mh: jnp.exp(sk - mh), m)
            o4 = _dot(p.astype(BF16), vc) / l
            rows = _block_rows(row0, stride, ib)
            _store_rows(o_ref, rows, _unstack_heads(o4, lane_head).astype(o_dtype), split)
            _store_rows(lse_ref, rows, _unstack_heads(m + jnp.log(l), lane_head), split)

        block(0, True)
        if nb > 1:
            def step(i, carry):
                block(i, False)
                return carry
            lax.fori_loop(1, nb, step, 0, unroll=min(ATTN_UNROLL, nb - 1))

    return _pallas(
        body, (q_arr, k_arr, v_arr, bias, sink), name=name, grid=grid,
        out_shape=(jax.ShapeDtypeStruct(o_shape, o_dtype), jax.ShapeDtypeStruct(o_shape, F32)),
        in_specs=[q_spec, k_spec, v_spec,
                  pl.BlockSpec((4, BLOCK, 2 * BLOCK), bias_map), pl.BlockSpec((1, 4, 128), sink_map)],
        out_specs=(o_spec, o_spec),
        scratch_shapes=[pltpu.VMEM((seq, GW), BF16)] * 2 if expanded else [],
        sem=("arbitrary", "arbitrary"), vmem=VMEM_BIG, rider=rider)


def _attn_bwd(q_arr, k_arr, v_arr, bias, sink, dy, cc, lse, *, grid, seq, stride, kvw, split, q_spec, k_spec, v_spec,
              bias_map, sink_map, o_spec, kv_out_spec, has_sink, n_bias, dq_shape, dkv_shape, g_dtype, name):
    ln = seq // stride
    nb = ln // BLOCK
    scale = HEAD_DIM ** -0.5
    expanded = kvw != GW
    rps = min(stride, RESIDUES_PER_STEP)
    grid = (grid[0], grid[1] // rps)

    def body(q_ref, k_ref, v_ref, bias_ref, sink_ref, dy_ref, c_ref, lse_ref,
             dq_ref, dk_ref, dv_ref, db_ref, dsk_ref, dk_acc, dv_acc, dk_half, dv_half, *kv_x):
        rr = pl.program_id(1)

        @pl.when((pl.program_id(0) == 0) & (rr == 0))
        def _():
            db_ref[...] = jnp.zeros_like(db_ref)
            dsk_ref[...] = jnp.zeros_like(dsk_ref)

        if expanded:
            expand = _kv_expand_matrix(rr)
            kv_x[0][...] = _dot(k_ref[0], expand).astype(BF16)
            kv_x[1][...] = _dot(v_ref[0], expand).astype(BF16)
        refs = (q_ref, k_ref, v_ref, bias_ref, sink_ref, dy_ref, c_ref, lse_ref, dq_ref, dk_ref, dv_ref, db_ref,
                dsk_ref, dk_acc, dv_acc, dk_half, dv_half, kv_x)
        for j in range(rps):
            residue(rr, rr * rps + j if stride > 1 else 0, *refs)

    def residue(rr, row0, q_ref, k_ref, v_ref, bias_ref, sink_ref, dy_ref, c_ref, lse_ref,
                dq_ref, dk_ref, dv_ref, db_ref, dsk_ref, dk_acc, dv_acc, dk_half, dv_half, kv_x):
        dk_acc[...] = jnp.zeros_like(dk_acc)
        dv_acc[...] = jnp.zeros_like(dv_acc)
        lane_head = _lane_head(BLOCK)
        hb = 4 * rr if n_bias == 8 else 0

        def load(ref, ib):
            return _load_rows(ref, _block_rows(row0, stride, ib), split)

        def load_kv(which, ib):
            if expanded:
                return kv_x[which][_block_rows(0, 1, ib), :]
            return load((k_ref, v_ref)[which], ib).astype(BF16)

        def head_col(x):
            return jnp.concatenate([x[:, h * HEAD_DIM:h * HEAD_DIM + 1] for h in range(4)], axis=0)

        def block(ib, first):
            q4 = _stack_heads(load(q_ref, ib).astype(BF16), lane_head)
            dy4 = _stack_heads(load(dy_ref, ib).astype(BF16), lane_head)
            c4 = head_col(load(c_ref, ib))
            l4 = head_col(load(lse_ref, ib))
            if first:
                kc, vc = load_kv(0, ib), load_kv(1, ib)
                b4 = bias_ref[:, :, BLOCK:].reshape(4 * BLOCK, BLOCK)
                krows = pl.ds(0, BLOCK)
            else:
                kc = jnp.concatenate([load_kv(0, ib - 1), load_kv(0, ib)], axis=0)
                vc = jnp.concatenate([load_kv(1, ib - 1), load_kv(1, ib)], axis=0)
                b4 = bias_ref[...].reshape(4 * BLOCK, 2 * BLOCK)
                krows = pl.ds(pl.multiple_of((ib - 1) * BLOCK, BLOCK), 2 * BLOCK)
            nk = BLOCK if first else 2 * BLOCK
            p = jnp.exp(_dot_nt(q4, kc) * scale + b4 - l4)
            ds = p * (_dot_nt(dy4, vc) - c4)
            ds3 = ds.reshape(4, BLOCK, nk)
            if n_bias == 8:
                if first:
                    db_ref[pl.ds(hb, 4), :, BLOCK:] += ds3
                else:
                    db_ref[pl.ds(hb, 4)] += ds3
            elif first:
                db_ref[:, :, BLOCK:] += ds3
            else:
                db_ref[...] += ds3
            ds16 = ds.astype(BF16)
            dq = _unstack_heads(_dot(ds16, kc), lane_head) * scale
            _store_rows(dq_ref, _block_rows(row0, stride, ib), dq.astype(g_dtype), split)
            dk_acc[krows, :] += _dot_tn(ds16, q4) * scale
            dv_acc[krows, :] += _dot_tn(p.astype(BF16), dy4)
            if has_sink:
                for h in range(4):
                    hs = slice(h * BLOCK, (h + 1) * BLOCK)
                    sk = sink_ref[0, h:h + 1, 0:1]
                    val = -jnp.sum(jnp.exp(sk - l4[hs]) * c4[hs], axis=0, keepdims=True)
                    dsk_ref[hb + h] += jnp.broadcast_to(val, (8, 128))

        block(0, True)
        if nb > 1:
            def step(i, carry):
                block(i, False)
                return carry
            lax.fori_loop(1, nb, step, 0, unroll=min(ATTN_UNROLL, nb - 1))

        if kvw == GW:
            all_rows = pl.ds(row0, ln, stride=stride) if stride > 1 else pl.ds(0, ln)
            _store_rows(dk_ref, all_rows, dk_acc[...].astype(g_dtype), split)
            _store_rows(dv_ref, all_rows, dv_acc[...].astype(g_dtype), split)
        else:
            def fold(acc):
                t2 = acc[:, :2 * HEAD_DIM] + acc[:, 2 * HEAD_DIM:]
                t2 = t2 + pltpu.roll(t2, HEAD_DIM, 1)
                lane = lax.broadcasted_iota(jnp.int32, t2.shape, 1) // HEAD_DIM
                return jnp.where(lane == rr, t2, 0.0)

            @pl.when(rr == 0)
            def _():
                dk_half[...] = fold(dk_acc[...])
                dv_half[...] = fold(dv_acc[...])

            @pl.when(rr == 1)
            def _():
                dk_ref[0] = (dk_half[...] + fold(dk_acc[...])).astype(g_dtype)
                dv_ref[0] = (dv_half[...] + fold(dv_acc[...])).astype(g_dtype)

    return pl.pallas_call(
        body, name=name, grid=grid,
        out_shape=(jax.ShapeDtypeStruct(dq_shape, g_dtype), jax.ShapeDtypeStruct(dkv_shape, g_dtype),
                   jax.ShapeDtypeStruct(dkv_shape, g_dtype), jax.ShapeDtypeStruct((n_bias, BLOCK, 2 * BLOCK), F32),
                   jax.ShapeDtypeStruct((8, 8, 128), F32)),
        in_specs=[q_spec, k_spec, v_spec,
                  pl.BlockSpec((4, BLOCK, 2 * BLOCK), bias_map), pl.BlockSpec((1, 4, 128), sink_map),
                  o_spec, o_spec, o_spec],
        out_specs=(o_spec, kv_out_spec, kv_out_spec,
                   pl.BlockSpec((n_bias, BLOCK, 2 * BLOCK), lambda n, r: (0, 0, 0)),
                   pl.BlockSpec((8, 8, 128), lambda n, r: (0, 0, 0))),
        scratch_shapes=[pltpu.VMEM((ln, GW), F32), pltpu.VMEM((ln, GW), F32),
                        pltpu.VMEM((ln, 2 * HEAD_DIM), F32), pltpu.VMEM((ln, 2 * HEAD_DIM), F32)]
        + ([pltpu.VMEM((seq, GW), BF16)] * 2 if expanded else []),
        compiler_params=_params(("arbitrary", "arbitrary"), VMEM_BIG),
    )(q_arr, k_arr, v_arr, bias, sink, dy, cc, lse)


def _bias_grad(ds_all, buckets):
    def body(ds_ref, bk_ref, o_ref):
        rows = lax.broadcasted_iota(jnp.int32, (N_BUCKETS, 128), 0)
        cols = lax.broadcasted_iota(jnp.int32, (N_BUCKETS, 128), 1)

        def per_bucket(b, acc):
            for h in range(20):
                gi = h // 4 if h < 12 else 3
                v = jnp.where(bk_ref[gi] == b, ds_ref[h], 0.0)
                v = jnp.sum(jnp.sum(v, axis=1, keepdims=True), axis=0, keepdims=True)
                acc = jnp.where((rows == b) & (cols == h), v, acc)
            return acc

        o_ref[...] = lax.fori_loop(0, N_BUCKETS, per_bucket, jnp.zeros((N_BUCKETS, 128), F32))

    vm = pl.BlockSpec(memory_space=pltpu.VMEM)
    return pl.pallas_call(body, name="bias_grad", out_shape=jax.ShapeDtypeStruct((N_BUCKETS, 128), F32),
                          in_specs=[vm, vm], out_specs=vm)(ds_all, buckets)


def _adamw(w, g, m, v, name):
    r, c = w.shape
    tr = r
    for cand in (256, 176, 128, 64, 32, 16, 8):
        if r % cand == 0:
            tr = cand
            break
    bc1 = 1.0 - ADAM_B1 ** ADAM_STEP
    bc2 = 1.0 - ADAM_B2 ** ADAM_STEP

    def body(w_ref, g_ref, m_ref, v_ref, d_ref, nm_ref, nv_ref):
        gv = g_ref[...]
        nm = ADAM_B1 * m_ref[...] + (1.0 - ADAM_B1) * gv
        nv = ADAM_B2 * v_ref[...] + (1.0 - ADAM_B2) * (gv * gv)
        nm_ref[...] = nm
        nv_ref[...] = nv
        d_ref[...] = -ADAM_LR * ((nm / bc1) / (jnp.sqrt(nv / bc2) + ADAM_EPS) + ADAM_WD * w_ref[...])

    spec = pl.BlockSpec((tr, c), lambda i: (i, 0))
    shp = jax.ShapeDtypeStruct((r, c), F32)
    return pl.pallas_call(body, name=name, grid=(r // tr,), out_shape=(shp, shp, shp),
                          in_specs=[spec] * 4, out_specs=(spec, spec, spec),
                          compiler_params=_params(("parallel",)))(w, g, m, v)


def _t5_bucket(dist):
    max_exact = N_BUCKETS // 2
    n = jnp.maximum(dist, 0)
    nf = jnp.maximum(n, 1).astype(F32)
    large = max_exact + (jnp.log(nf / max_exact) / math.log(MAX_DISTANCE / max_exact)
                         * (N_BUCKETS - max_exact)).astype(jnp.int32)
    large = jnp.minimum(large, N_BUCKETS - 1)
    return jnp.where(n < max_exact, n, large)


def _bias_tables(rel_bias):
    qi = jnp.arange(BLOCK)[:, None]
    ki = jnp.arange(2 * BLOCK)[None, :]
    dist = qi + BLOCK - ki
    specs = [(d, w // d, 4 * gi, 4 * gi + 4) for gi, (w, d) in enumerate(DIL_GROUPS)] + [(1, B_WINDOW - 1, 12, 20)]
    biases, buckets = [], []
    for stride, steps, h0, h1 in specs:
        valid = (dist >= 0) & (dist <= steps)
        bk = jnp.where(valid, _t5_bucket(dist * stride), -1).astype(jnp.int32)
        onehot = (bk[None, :, :] == jnp.arange(N_BUCKETS, dtype=jnp.int32)[:, None, None]).astype(F32)
        b = jnp.einsum("bqk,bh->hqk", onehot, rel_bias[:, h0:h1], precision=lax.Precision.HIGHEST)
        biases.append(jnp.where(valid[None], b, NEG))
        buckets.append(bk)
    return jnp.concatenate(biases, axis=0), jnp.stack(buckets, axis=0)


def _local_step(x, tgt, W, S, shards=None):
    nseq, seq, _ = x.shape
    t = nseq * seq
    xf = x.reshape(t, D_MODEL)
    bias_all, buckets = _bias_tables(S["rel_bias"])
    sink_b = jnp.broadcast_to(S["sinks"].reshape(2, 4, 1), (2, 4, 128)).astype(F32)
    sink_0 = jnp.zeros((1, 4, 128), F32)
    dist = shards is not None
    W = dict(W)
    G, GS, reduced = {}, {}, {}

    def put(keys, gathered):
        for k, g in zip(keys, gathered):
            W[k] = g.reshape(_FULL_SHAPE.get(k, (N_CHIPS * shards[k].shape[0], D_MODEL)))

    def gather_rider(keys):
        return _GatherRider([shards[k] for k in keys]) if dist else None

    def pair(keys):
        return _pair_reduce([G[k].reshape(N_CHIPS, 2, shards[k].shape[0] // 2, D_MODEL) for k in keys],
                            "grad_pair_reduce_" + keys[0])

    def finish(keys, own, rec):
        full = _final_reduce(own, rec, "grad_final_reduce_" + keys[0])
        off = 0
        for k in keys:
            r = shards[k].shape[0]
            reduced[k] = full[:, off:off + r // 2].reshape(r, D_MODEL)
            off += r // 2

    if dist:
        first = ("wgt1", "wut1", "wd1")
        put(first, _gather_rows([shards[k] for k in first]))
    keys = ("wint",)
    (h1, n1, g1, u1, a1), ro = _ffn_fwd(xf, S["ffn1_norm"], W["wgt1"], W["wut1"], W["wd1"], rider=gather_rider(keys))
    put(keys, ro)
    keys = ("wout", "wat", "wbt", "wgt2")
    (un, za, zb, zg), ro = _inproj_fwd(h1, S["mix_norm"], W["wint"], S["b_in"], nseq, rider=gather_rider(keys))
    put(keys, ro)

    seq3 = lambda a: a.reshape(nseq, seq, a.shape[-1])
    zb3 = seq3(zb)
    pair_blk = lambda cb: pl.BlockSpec((1, 2, seq, 128), lambda n, r, cb=cb: (n, cb, 0, 0))
    a_cfg = []
    outs, lses = [], []
    for gi, (_, d) in enumerate(DIL_GROUPS):
        cfg = dict(grid=(nseq, d), seq=seq, stride=d, kvw=GW, split=True,
                   q_spec=pair_blk(gi), k_spec=pair_blk(3 + gi), v_spec=pair_blk(6 + gi), o_spec=pair_blk(0),
                   bias_map=lambda n, r: (0, 0, 0), sink_map=lambda n, r: (0, 0, 0), has_sink=False)
        a_cfg.append(cfg)
        (o, lse), _ = _attn_fwd(za, za, za, bias_all[4 * gi:4 * gi + 4], sink_0, o_shape=(nseq, 2, seq, 128),
                                o_dtype=F32, name=f"attn_a{gi}_fwd", **cfg)
        outs.append(o)
        lses.append(lse)
    wide_blk = lambda w, cmap: pl.BlockSpec((1, seq, w), cmap)
    b_cfg = dict(grid=(nseq, 2), seq=seq, stride=1, kvw=2 * HEAD_DIM, split=False,
                 q_spec=wide_blk(GW, lambda n, r: (n, 0, r)), k_spec=wide_blk(2 * HEAD_DIM, lambda n, r: (n, 0, 4)),
                 v_spec=wide_blk(2 * HEAD_DIM, lambda n, r: (n, 0, 5)), o_spec=wide_blk(GW, lambda n, r: (n, 0, r)),
                 bias_map=lambda n, r: (r, 0, 0), sink_map=lambda n, r: (r, 0, 0), has_sink=True)
    keys = ("wut2",)
    bias_b_fwd = bias_all[12:20].at[:, :, 0].set(jnp.broadcast_to(S["sinks"].reshape(8, 1), (8, BLOCK)))
    (yb, lse_b), ro = _attn_fwd(zb3, zb3, zb3, bias_b_fwd, sink_b, o_shape=(nseq, seq, 2 * GW), o_dtype=BF16,
                                name="attn_b_fwd", rider=gather_rider(keys), **b_cfg)
    put(keys, ro)
    yb = yb.reshape(t, 2 * GW)

    keys = ("wd2",)
    (h2, y, lse_tot, pa, pb, merged), ro = _merge_fwd(outs[0], outs[1], outs[2], lses[0], lses[1], lses[2], yb, zg, h1,
                                                      W["wat"], W["wbt"], W["wout"], rider=gather_rider(keys))
    put(keys, ro)
    (dh3, n2, g2, u2, a2, loss_part, g_final), _ = _ffn_fwd(
        h2, S["ffn2_norm"], W["wgt2"], W["wut2"], W["wd2"],
        head=(S["final_norm"].reshape(1, D_MODEL), tgt.reshape(t, D_MODEL)))

    GS["final_norm"] = g_final
    dh2, dg2, du2, df2, GS["ffn2_norm"] = _ffn_bwd(dh3, h2, S["ffn2_norm"], g2, u2, W["wgt2"], W["wut2"], W["wd2"])
    G["wgt2"] = _wgrad(dg2, n2, MXU_DIM, name="wgrad_gate2")
    G["wut2"] = _wgrad(du2, n2, MXU_DIM, name="wgrad_up2")
    G["wd2"] = _wgrad(a2, df2, MXU_DIM, name="wgrad_down2")

    keys = ("wgt2", "wut2", "wd2")
    rider = _ExchangeRider([pair(keys)]) if dist else None
    (dpa, dpb, dga, dgb, dya, dyb, dh2b, ca, cb), ro = _merge_bwd(dh2, pa, pb, zg, y, yb, W["wat"], W["wbt"], W["wout"],
                                                                  nseq, rider=rider)
    if dist:
        finish(keys, *ro)

    dqs, dks, dvs, dbs = [], [], [], []
    shp = (nseq, 2, seq, 128)
    halves = lambda a: [a[:, hf].reshape(t, 128).astype(BF16) for hf in range(2)]
    for gi in range(len(DIL_GROUPS)):
        dq, dk, dv, db, _ = _attn_bwd(za, za, za, bias_all[4 * gi:4 * gi + 4], sink_0, dya, ca, lse_tot,
                                      n_bias=4, dq_shape=shp, dkv_shape=shp, g_dtype=F32,
                                      kv_out_spec=a_cfg[gi]["o_spec"], name=f"attn_a{gi}_bwd", **a_cfg[gi])
        dqs += halves(dq)
        dks += halves(dk)
        dvs += halves(dv)
        dbs.append(db)
    dqb, dkb, dvb, dbb, dsink = _attn_bwd(zb3, zb3, zb3, bias_all[12:20], sink_b, seq3(dyb), seq3(cb), lse_b,
                                          n_bias=8, dq_shape=(nseq, seq, 2 * GW),
                                          dkv_shape=(nseq, seq, 2 * HEAD_DIM), g_dtype=BF16,
                                          kv_out_spec=wide_blk(2 * HEAD_DIM, lambda n, r: (n, 0, 0)),
                                          name="attn_b_bwd", **b_cfg)
    dz = jnp.concatenate(dqs + dks + dvs + [dqb.reshape(t, 2 * GW), dkb.reshape(t, 2 * HEAD_DIM),
                                            dvb.reshape(t, 2 * HEAD_DIM), dga, dgb], axis=-1)
    gb_tab = _bias_grad(jnp.concatenate(dbs + [dbb], axis=0), buckets)
    if dist:
        GS["bias_tab"], GS["sink_tiles"] = gb_tab, dsink
    else:
        GS["rel_bias"] = gb_tab[:, :20]
        GS["sinks"] = dsink[:, 0, 0].reshape(1, 8)

    G["wint"], GS["b_in"] = _wgrad(dz, un, MXU_DIM, with_colsum=True, name="wgrad_in")
    keys = ("wint",)
    rider = _ExchangeRider([pair(keys)]) if dist else None
    (dh1, GS["mix_norm"]), ro = _inproj_bwd(dz, dh2, h1, S["mix_norm"], W["wint"], rider=rider)
    if dist:
        finish(keys, *ro)

    dx, dg1, du1, df1, GS["ffn1_norm"] = _ffn_bwd(dh1, xf, S["ffn1_norm"], g1, u1, W["wgt1"], W["wut1"], W["wd1"])
    G["wgt1"] = _wgrad(dg1, n1, MXU_DIM, name="wgrad_gate1")
    if dist:
        G["wut1"], ro = _wgrad(du1, n1, MXU_DIM, name="wgrad_up1", rider=_ExchangeRider([pair(("wgt1",))]))
        finish(("wgt1",), *ro)
        G["wd1"], ro = _wgrad(a1, df1, MXU_DIM, name="wgrad_down1", rider=_ExchangeRider([pair(("wut1",))]))
        finish(("wut1",), *ro)
        G["wout"], ro = _wgrad(merged, dh2b, MXU_DIM, name="wgrad_out", rider=_ExchangeRider([pair(("wd1",))]))
        finish(("wd1",), *ro)
    else:
        G["wut1"] = _wgrad(du1, n1, MXU_DIM, name="wgrad_up1")
        G["wd1"] = _wgrad(a1, df1, MXU_DIM, name="wgrad_down1")
        G["wout"] = _wgrad(merged, dh2b, MXU_DIM, name="wgrad_out")
    G["wat"] = _wgrad(dpa, y, MXU_DIM, name="wgrad_branch_a")
    G["wbt"] = _wgrad(dpb, yb, MXU_DIM, name="wgrad_branch_b")
    if dist:
        keys = ("wout", "wat", "wbt")
        finish(keys, *_chip_exchange([pair(keys)]))
    return loss_part, dx.reshape(x.shape), (reduced if dist else G), GS


_SMALL = ("ffn1_norm", "mix_norm", "ffn2_norm", "final_norm", "b_in", "sinks", "rel_bias")
_ORDER = ("ffn1_norm", "ffn1_w_gate", "ffn1_w_up", "ffn1_w_down", "mix_norm", "w_in", "b_in", "w_branch_a",
          "w_branch_b", "w_out", "sinks", "rel_bias", "ffn2_norm", "ffn2_w_gate", "ffn2_w_up", "ffn2_w_down",
          "final_norm")
_BIG = (("wgt1", "ffn1_w_gate", True, 704), ("wut1", "ffn1_w_up", True, 704), ("wd1", "ffn1_w_down", False, 704),
        ("wint", "w_in", True, 1280), ("wout", "w_out", False, 256), ("wat", "w_branch_a", True, 64),
        ("wbt", "w_branch_b", True, 128), ("wgt2", "ffn2_w_gate", True, 704), ("wut2", "ffn2_w_up", True, 704),
        ("wd2", "ffn2_w_down", False, 704))
_FULL_SHAPE = {"wat": (D_MODEL, GW), "wbt": (D_MODEL, 2 * GW)}


def kernel(x, ffn1_norm, ffn1_w_gate, ffn1_w_up, ffn1_w_down, mix_norm, w_in, b_in, w_branch_a, w_branch_b, w_out, sinks, rel_bias, ffn2_norm, ffn2_w_gate, ffn2_w_up, ffn2_w_down, final_norm, loss_target, m_ffn1_norm, m_ffn1_w_gate, m_ffn1_w_up, m_ffn1_w_down, m_mix_norm, m_w_in, m_b_in, m_w_branch_a, m_w_branch_b, m_w_out, m_sinks, m_rel_bias, m_ffn2_norm, m_ffn2_w_gate, m_ffn2_w_up, m_ffn2_w_down, m_final_norm, v_ffn1_norm, v_ffn1_w_gate, v_ffn1_w_up, v_ffn1_w_down, v_mix_norm, v_w_in, v_b_in, v_w_branch_a, v_w_branch_b, v_w_out, v_sinks, v_rel_bias, v_ffn2_norm, v_ffn2_w_gate, v_ffn2_w_up, v_ffn2_w_down, v_final_norm):
    args = dict(locals())
    w = {n: args[n] for n in _ORDER}
    m = {n: args["m_" + n] for n in _ORDER}
    v = {n: args["v_" + n] for n in _ORDER}

    shards = {}
    for key, name, transposed, rows in _BIG:
        a = w[name][0]
        a = (a.T if transposed else a).astype(BF16)
        shards[key] = a.reshape(rows, D_MODEL)
    S = {n: w[n] for n in _SMALL}

    loss_part, grad_x, reduced, GS = _local_step(x, loss_target, {}, S, shards)

    small = _allreduce_small(GS["ffn1_norm"], GS["mix_norm"], GS["ffn2_norm"], GS["final_norm"], GS["b_in"],
                             GS["sink_tiles"], GS["bias_tab"], loss_part)
    loss = small[9, 8]

    out_g, out_d, out_m, out_v = {}, {}, {}, {}
    for key, n, transposed, rows in _BIG:
        nat = w[n][0].shape
        if transposed and nat[1] % 128:
            res = _adamw(w[n][0].T, reduced[key], m[n][0].T, v[n][0].T, "adamw_" + n)
            res = [reduced[key].T] + [r.T for r in res]
        else:
            g = reduced[key].reshape(nat[1], nat[0]).T if transposed else reduced[key].reshape(nat)
            res = [g] + list(_adamw(w[n][0], g, m[n][0], v[n][0], "adamw_" + n))
        out_g[n], out_d[n], out_m[n], out_v[n] = [r[None] for r in res]
    row = lambda d: {n: (d[n].reshape(1, D_MODEL) if n == "final_norm" else d[n]) for n in _SMALL}
    for dst, src in zip((out_g, out_d, out_m, out_v), _adamw_small(small, row(w), row(m), row(v))):
        dst.update(src)
        dst["final_norm"] = src["final_norm"].reshape(D_MODEL)

    return (loss, grad_x, *[out_g[n] for n in _ORDER], *[out_d[n] for n in _ORDER],
            *[out_m[n] for n in _ORDER], *[out_v[n] for n in _ORDER])
```

```python
import math

import jax
import jax.numpy as jnp
from jax import lax
from jax.experimental import pallas as pl
from jax.experimental.pallas import tpu as pltpu

F32, BF16 = jnp.float32, jnp.bfloat16
MESH = pl.DeviceIdType.MESH

D_MODEL = 1024
D_FF = 2816
D_IN = 5120
HEAD_DIM = 64
BLOCK = 128
DIL_GROUPS = ((128, 1), (512, 4), (2048, 16))
B_WINDOW = 128
N_BUCKETS = 32
MAX_DISTANCE = 2048
EPS = 1e-6
N_CHIPS = 4
GW = 256
ZA_W = 2304
ZB_W = 768
NEG = -1e30

ADAM_LR, ADAM_B1, ADAM_B2, ADAM_EPS, ADAM_WD, ADAM_STEP = 0.001, 0.9, 0.999, 1e-08, 0.01, 10

VMEM_BIG = 56 * 1024 * 1024
TM = 512
TM_BWD = 256
MXU_DIM = 256
FF_BOUNDS = (0, 4 * MXU_DIM, 8 * MXU_DIM, D_FF)
DMA_SPLIT = 8
RESIDUES_PER_STEP = 8
ATTN_UNROLL = 5


def _dot(a, b):
    return jnp.dot(a, b, preferred_element_type=F32)


def _dot_nt(a, b):
    return lax.dot_general(a, b, (((1,), (1,)), ((), ())), preferred_element_type=F32)


def _dot_tn(a, b):
    return lax.dot_general(a, b, (((0,), (0,)), ((), ())), preferred_element_type=F32)


def _sigmoid(x):
    return 0.5 * jnp.tanh(0.5 * x) + 0.5


def _params(sem, vmem=None):
    return pltpu.CompilerParams(dimension_semantics=sem, vmem_limit_bytes=vmem)


ANY = pl.BlockSpec(memory_space=pl.ANY)


def _me():
    return lax.axis_index("x"), lax.axis_index("y"), lax.axis_index("c")


_CHIP_RELS = ((1, 0), (0, 1), (1, 1))


def _flip(v, f):
    return 1 - v if f else v


def _remote(src, dst, ssem, rsem, peer):
    return pltpu.make_async_remote_copy(src_ref=src, dst_ref=dst, send_sem=ssem, recv_sem=rsem,
                                        device_id=peer, device_id_type=MESH)


def _row_pieces(rows, n):
    step = max(16, -(-rows // n) // 16 * 16)
    out, s = [], 0
    while s < rows:
        out.append((s, min(step, rows - s)))
        s += step
    return out


def _gather_rows(shards):
    nt = len(shards)
    rows = [s.shape[0] for s in shards]

    def body(*refs):
        srcs, outs = refs[:nt], refs[nt:2 * nt]
        halves, quarters = refs[2 * nt:3 * nt], refs[3 * nt:4 * nt]
        ici_s, ici_r, fwd_s, fwd_r, d2d_s, d2d_r, keep, loc = refs[4 * nt:]
        x, y, c = _me()
        j = 2 * x + y
        sib = (x, y, 1 - c)
        nbr = ((1 - x, y, c), (x, 1 - y, c))
        nbr_j = (2 * (1 - x) + y, 2 * x + (1 - y))
        diag_j = 2 * (1 - x) + (1 - y)
        local = [pltpu.make_async_copy(srcs[t], outs[t].at[j], loc.at[t]) for t in range(nt)]
        for cp in local:
            cp.start()
        pending = []
        for a in range(2):
            for t in range(nt):
                half = pl.ds(c * (rows[t] // 2), rows[t] // 2)
                cp = _remote(srcs[t].at[half], halves[t].at[a], ici_s.at[2 * t + a], ici_r.at[2 * t + a], nbr[a])
                cp.start()
                pending.append(cp)
        placed = []

        def place(src, dst_of, idx):
            mine = pltpu.make_async_copy(src, dst_of, keep.at[idx])
            mine.start()
            cp = _remote(src, dst_of, d2d_s.at[idx], d2d_r.at[idx], sib)
            cp.start()
            placed.append((mine, cp))

        for a in range(2):
            for t in range(nt):
                r2, r4 = rows[t] // 2, rows[t] // 4
                got = halves[t].at[a]
                _remote(got, got, ici_s.at[2 * t + a], ici_r.at[2 * t + a], nbr[a]).wait_recv()
                cp = _remote(halves[t].at[a, pl.ds(a * r4, r4)], quarters[t].at[a], fwd_s.at[2 * t + a],
                             fwd_r.at[2 * t + a], nbr[1 - a])
                cp.start()
                pending.append(cp)
                place(got, outs[t].at[nbr_j[a], pl.ds(c * r2, r2)], 4 * t + a)
        for a in range(2):
            for t in range(nt):
                r2, r4 = rows[t] // 2, rows[t] // 4
                got = quarters[t].at[a]
                _remote(got, got, fwd_s.at[2 * t + a], fwd_r.at[2 * t + a], nbr[1 - a]).wait_recv()
                place(got, outs[t].at[diag_j, pl.ds(c * r2 + a * r4, r4)], 4 * t + 2 + a)
        for mine, cp in placed:
            mine.wait()
            cp.wait()
        for cp in pending:
            cp.wait_send()
        for cp in local:
            cp.wait()

    stage = ([pltpu.VMEM((2, r // 2, D_MODEL), BF16) for r in rows] + [pltpu.VMEM((2, r // 4, D_MODEL), BF16) for r in rows])
    sems = ([pltpu.SemaphoreType.DMA((2 * nt,)) for _ in range(4)] + [pltpu.SemaphoreType.DMA((4 * nt,))] * 3
            + [pltpu.SemaphoreType.DMA((nt,))])
    return pl.pallas_call(
        body, name="gather_weights",
        out_shape=tuple(jax.ShapeDtypeStruct((N_CHIPS,) + s.shape, s.dtype) for s in shards),
        in_specs=[pl.BlockSpec(memory_space=pltpu.VMEM)] * nt,
        out_specs=tuple([ANY] * nt), scratch_shapes=stage + sems,
    )(*shards)


VMEM_WHOLE = pl.BlockSpec(memory_space=pltpu.VMEM)


def _pair_reduce(grads, name):
    nt = len(grads)
    r2 = [g.shape[2] for g in grads]
    off = [sum(r2[:t]) for t in range(nt)]
    tot = sum(r2)

    def body(*refs):
        gs = refs[:nt]
        s_ref, mine, got, ssem, rsem, lsem = refs[nt:]
        x, y, c = _me()
        sib = (x, y, 1 - c)
        for t in range(nt):
            for k in range(N_CHIPS):
                rows = pl.ds(off[t], r2[t])
                _remote(gs[t].at[k, 1 - c], got.at[k, rows], ssem, rsem, sib).start()
                pltpu.make_async_copy(gs[t].at[k, c], mine.at[k, rows], lsem).start()
        pltpu.make_async_copy(mine, mine, lsem).wait()
        _remote(got, got, ssem, rsem, sib).wait()
        for k in range(N_CHIPS):
            for st, sz in _row_pieces(tot, 4):
                rows = slice(st, st + sz)
                s_ref[k, rows, :] = (mine[k, rows, :].astype(F32) + got[k, rows, :].astype(F32)).astype(BF16)

    shp = jax.ShapeDtypeStruct((N_CHIPS, tot, D_MODEL), BF16)
    buf = pltpu.VMEM((N_CHIPS, tot, D_MODEL), BF16)
    return pl.pallas_call(
        body, name=name, out_shape=shp, in_specs=[ANY] * nt, out_specs=VMEM_WHOLE,
        scratch_shapes=[buf, buf, pltpu.SemaphoreType.DMA(()), pltpu.SemaphoreType.DMA(()),
                        pltpu.SemaphoreType.DMA(())],
        compiler_params=pltpu.CompilerParams(vmem_limit_bytes=VMEM_BIG),
    )(*grads)


def _chip_exchange(parts):
    ng = len(parts)
    r2 = [p.shape[1] for p in parts]
    off = [sum(r2[:g]) for g in range(ng)]
    tot = sum(r2)

    def body(*refs):
        ps = refs[:ng]
        own_ref, rec_ref, ssems, rsems, lsem = refs[ng:]
        x, y, c = _me()
        j = 2 * x + y
        for g in range(ng):
            pltpu.make_async_copy(ps[g].at[j], own_ref.at[pl.ds(off[g], r2[g])], lsem).start()
        for k, (fx, fy) in enumerate(_CHIP_RELS):
            px, py = _flip(x, fx), _flip(y, fy)
            for g in range(ng):
                for st, sz in _row_pieces(r2[g], 2):
                    _remote(ps[g].at[2 * px + py, pl.ds(st, sz)], rec_ref.at[k, pl.ds(off[g] + st, sz)],
                            ssems.at[k], rsems.at[k], (px, py, c)).start()
        for k in range(3):
            _remote(rec_ref.at[k], rec_ref.at[k], ssems.at[k], rsems.at[k], (x, y, c)).wait()
        pltpu.make_async_copy(own_ref, own_ref, lsem).wait()

    return pl.pallas_call(
        body, name="grad_chip_exchange",
        out_shape=(jax.ShapeDtypeStruct((tot, D_MODEL), BF16), jax.ShapeDtypeStruct((3, tot, D_MODEL), BF16)),
        in_specs=[VMEM_WHOLE] * ng, out_specs=(ANY, ANY),
        scratch_shapes=[pltpu.SemaphoreType.DMA((3,)), pltpu.SemaphoreType.DMA((3,)), pltpu.SemaphoreType.DMA(())],
    )(*parts)


def _final_reduce(own, rec, name):
    r2 = own.shape[0]
    pieces = _row_pieces(r2, DMA_SPLIT)

    def body(own_ref, rec_ref, o_ref, fbuf, ssem, rsem, lsem):
        x, y, c = _me()
        sib = (x, y, 1 - c)
        for st, sz in pieces:
            rows = slice(st, st + sz)
            fbuf[rows, :] = (own_ref[rows, :].astype(F32) + rec_ref[0, rows, :].astype(F32)
                             + rec_ref[1, rows, :].astype(F32) + rec_ref[2, rows, :].astype(F32))
            pltpu.make_async_copy(fbuf.at[pl.ds(st, sz)], o_ref.at[c, pl.ds(st, sz)], lsem).start()
            _remote(fbuf.at[pl.ds(st, sz)], o_ref.at[c, pl.ds(st, sz)], ssem, rsem, sib).start()
        _remote(fbuf, o_ref.at[c], ssem, rsem, sib).wait()
        pltpu.make_async_copy(fbuf, o_ref.at[c], lsem).wait()

    return pl.pallas_call(
        body, name=name, out_shape=jax.ShapeDtypeStruct((2, r2, D_MODEL), F32),
        in_specs=[VMEM_WHOLE, VMEM_WHOLE], out_specs=ANY,
        scratch_shapes=[pltpu.VMEM((r2, D_MODEL), F32), pltpu.SemaphoreType.DMA(()), pltpu.SemaphoreType.DMA(()),
                        pltpu.SemaphoreType.DMA(())],
        compiler_params=pltpu.CompilerParams(vmem_limit_bytes=VMEM_BIG),
    )(own, rec)


SMALL_ROWS = 48


def _allreduce_small(g_ffn1, g_mix, g_ffn2, g_final, g_bin, dsink, bias_tab, loss_part):
    def body(f1_ref, mx_ref, f2_ref, fn_ref, bi_ref, sk_ref, bt_ref, ls_ref, o_ref, mine, buf, send_sems, recv_sems):
        x, y, c = _me()
        me = 4 * x + 2 * y + c
        mine[...] = jnp.zeros_like(mine)
        for r, ref in enumerate((f1_ref, mx_ref, f2_ref, fn_ref)):
            mine[r:r + 1, :] = ref[...]
        for k in range(D_IN // D_MODEL):
            mine[4 + k:5 + k, :] = bi_ref[:, k * D_MODEL:(k + 1) * D_MODEL]
        lane = lax.broadcasted_iota(jnp.int32, (1, 128), 1)
        row = jnp.where(lane == 8, ls_ref[0:1, :], 0.0)
        for h in range(8):
            row = jnp.where(lane == h, sk_ref[h, 0:1, :], row)
        mine[9:10, 0:128] = row
        mine[16:48, 0:128] = bt_ref[...]
        buf[me] = mine[...]
        copies = []
        for k in range(1, 8):
            peer = (_flip(x, (k >> 2) & 1), _flip(y, (k >> 1) & 1), _flip(c, k & 1))
            cp = _remote(mine, buf.at[me], send_sems.at[k - 1], recv_sems.at[k - 1], peer)
            cp.start()
            copies.append(cp)
        for cp in copies:
            cp.wait()
        acc = buf[0]
        for i in range(1, 8):
            acc = acc + buf[i]
        o_ref[...] = acc

    vm = pl.BlockSpec(memory_space=pltpu.VMEM)
    shape = (SMALL_ROWS, D_MODEL)
    return pl.pallas_call(
        body, name="allreduce_small", out_shape=jax.ShapeDtypeStruct(shape, F32),
        in_specs=[vm] * 8, out_specs=vm,
        scratch_shapes=[pltpu.VMEM(shape, F32), pltpu.VMEM((8,) + shape, F32), pltpu.SemaphoreType.DMA((7,)),
                        pltpu.SemaphoreType.DMA((7,))],
    )(g_ffn1, g_mix, g_ffn2, g_final, g_bin, dsink, bias_tab, loss_part)


def _adam_update(w, g, m, v):
    nm = ADAM_B1 * m + (1.0 - ADAM_B1) * g
    nv = ADAM_B2 * v + (1.0 - ADAM_B2) * (g * g)
    bc1 = 1.0 - ADAM_B1 ** ADAM_STEP
    bc2 = 1.0 - ADAM_B2 ** ADAM_STEP
    return -ADAM_LR * ((nm / bc1) / (jnp.sqrt(nv / bc2) + ADAM_EPS) + ADAM_WD * w), nm, nv


def _adamw_small(packed, w, m, v):
    names = ("ffn1_norm", "mix_norm", "ffn2_norm", "final_norm", "b_in", "sinks", "rel_bias")
    nn = len(names)

    def grad_of(p_ref, name, k=0):
        if name == "b_in":
            return p_ref[4 + k:5 + k, :]
        if name == "sinks":
            return p_ref[9:10, 0:8]
        if name == "rel_bias":
            return p_ref[16:48, 0:20]
        r = names.index(name)
        return p_ref[r:r + 1, :]

    def body(p_ref, *refs):
        ws, ms, vs = refs[:nn], refs[nn:2 * nn], refs[2 * nn:3 * nn]
        outs = refs[3 * nn:]
        for i, name in enumerate(names):
            og, od, om, ov = outs[i], outs[nn + i], outs[2 * nn + i], outs[3 * nn + i]
            pieces = range(D_IN // D_MODEL) if name == "b_in" else (0,)
            for k in pieces:
                sl = (slice(None), slice(k * D_MODEL, (k + 1) * D_MODEL)) if name == "b_in" else (Ellipsis,)
                g = grad_of(p_ref, name, k)
                d, nm, nv = _adam_update(ws[i][sl], g, ms[i][sl], vs[i][sl])
                og[sl], od[sl], om[sl], ov[sl] = g, d, nm, nv

    vm = pl.BlockSpec(memory_space=pltpu.VMEM)
    shapes = [jax.ShapeDtypeStruct(w[n].shape, F32) for n in names]
    res = pl.pallas_call(
        body, name="adamw_small", out_shape=tuple(shapes * 4), in_specs=[vm] * (1 + 3 * nn),
        out_specs=tuple([vm] * (4 * nn)),
    )(packed, *[w[n] for n in names], *[m[n] for n in names], *[v[n] for n in names])
    return [dict(zip(names, res[i * nn:(i + 1) * nn])) for i in range(4)]


class _GatherRider:
    def __init__(self, shards):
        self.inputs = list(shards)
        nt = len(shards)
        self.out_shape = [jax.ShapeDtypeStruct((N_CHIPS,) + s.shape, s.dtype) for s in shards]
        self.scratch = [pltpu.SemaphoreType.DMA((3 * nt,)), pltpu.SemaphoreType.DMA((3 * nt,)),
                        pltpu.SemaphoreType.DMA((nt,))]

    def _copies(self, srcs, outs, sems):
        ici_s, ici_r, loc = sems
        x, y, c = _me()
        j = 2 * x + y
        local = [pltpu.make_async_copy(srcs[t], outs[t].at[j], loc.at[t]) for t in range(len(srcs))]
        remote = []
        for k, (fx, fy) in enumerate(_CHIP_RELS):
            peer = (_flip(x, fx), _flip(y, fy), c)
            for t in range(len(srcs)):
                remote.append(_remote(srcs[t], outs[t].at[j], ici_s.at[3 * t + k], ici_r.at[3 * t + k], peer))
        return local, remote

    def start(self, srcs, outs, sems):
        local, remote = self._copies(srcs, outs, sems)
        for cp in local + remote:
            cp.start()

    def finish(self, srcs, outs, sems):
        local, remote = self._copies(srcs, outs, sems)
        for cp in remote + local:
            cp.wait()


class _ExchangeRider:
    def __init__(self, parts):
        self.inputs = list(parts)
        self.r2 = [p.shape[1] for p in parts]
        self.off = [sum(self.r2[:g]) for g in range(len(parts))]
        tot = sum(self.r2)
        self.out_shape = [jax.ShapeDtypeStruct((tot, D_MODEL), BF16), jax.ShapeDtypeStruct((3, tot, D_MODEL), BF16)]
        self.scratch = [pltpu.SemaphoreType.DMA((3,)), pltpu.SemaphoreType.DMA((3,)), pltpu.SemaphoreType.DMA(())]

    def start(self, ps, outs, sems):
        own_ref, rec_ref = outs
        ssems, rsems, lsem = sems
        x, y, c = _me()
        j = 2 * x + y
        for g in range(len(ps)):
            pltpu.make_async_copy(ps[g].at[j], own_ref.at[pl.ds(self.off[g], self.r2[g])], lsem).start()
        for k, (fx, fy) in enumerate(_CHIP_RELS):
            px, py = _flip(x, fx), _flip(y, fy)
            for g in range(len(ps)):
                for st, sz in _row_pieces(self.r2[g], 2):
                    _remote(ps[g].at[2 * px + py, pl.ds(st, sz)], rec_ref.at[k, pl.ds(self.off[g] + st, sz)],
                            ssems.at[k], rsems.at[k], (px, py, c)).start()

    def finish(self, ps, outs, sems):
        own_ref, rec_ref = outs
        ssems, rsems, lsem = sems
        x, y, c = _me()
        for k in range(3):
            _remote(rec_ref.at[k], rec_ref.at[k], ssems.at[k], rsems.at[k], (x, y, c)).wait()
        pltpu.make_async_copy(own_ref, own_ref, lsem).wait()


def _pallas(body, args, *, name, grid, in_specs, out_specs, out_shape, scratch_shapes=(), sem=None, vmem=None,
            rider=None):
    if rider is None:
        res = pl.pallas_call(body, name=name, grid=grid, in_specs=list(in_specs), out_specs=tuple(out_specs),
                             out_shape=tuple(out_shape), scratch_shapes=list(scratch_shapes),
                             compiler_params=_params(sem, vmem))(*args)
        return tuple(res), ()
    n_in, n_out, n_sc = len(in_specs), len(out_shape), len(scratch_shapes)
    r_in, r_out = len(rider.inputs), len(rider.out_shape)

    def wrapped(*refs):
        ins, rins = refs[:n_in], refs[n_in:n_in + r_in]
        p = n_in + r_in
        outs, routs = refs[p:p + n_out], refs[p + n_out:p + n_out + r_out]
        p += n_out + r_out
        scr, rsems = refs[p:p + n_sc], refs[p + n_sc:]
        first = pl.program_id(0) == 0
        last = pl.program_id(0) == grid[0] - 1
        for a in range(1, len(grid)):
            first = first & (pl.program_id(a) == 0)
            last = last & (pl.program_id(a) == grid[a] - 1)

        @pl.when(first)
        def _():
            rider.start(rins, routs, rsems)

        body(*ins, *outs, *scr)

        @pl.when(last)
        def _():
            rider.finish(rins, routs, rsems)

    res = pl.pallas_call(
        wrapped, name=name, grid=grid, in_specs=list(in_specs) + [ANY] * r_in,
        out_specs=tuple(out_specs) + (ANY,) * r_out, out_shape=tuple(out_shape) + tuple(rider.out_shape),
        scratch_shapes=list(scratch_shapes) + rider.scratch,
        compiler_params=_params(("arbitrary",) * len(grid), vmem))(*args, *rider.inputs)
    return tuple(res[:n_out]), tuple(res[n_out:])


def _loss_tile(hh, gain, tgt):
    r = lax.rsqrt(jnp.mean(hh * hh, axis=-1, keepdims=True) + EPS)
    hn = hh * r
    err = hn * gain - tgt
    part = (0.5 / D_MODEL) * jnp.sum(jnp.sum(err * err, axis=1, keepdims=True), axis=0, keepdims=True)
    dy = err * (1.0 / D_MODEL)
    dng = dy * gain
    dh = r * (dng - hn * jnp.mean(dng * hn, axis=-1, keepdims=True))
    return dh, part, jnp.sum(dy * hn, axis=0, keepdims=True)


def _ffn_fwd(h, gain, wgt, wut, wd, rider=None, head=None):
    t = h.shape[0]

    def body(h_ref, gain_ref, wg_hbm, wu_hbm, wd_hbm, *rest):
        if head is None:
            hout_ref, n_ref, g_ref, u_ref, a_ref, wg_v, wu_v, wd_v = rest
        else:
            fg_ref, tgt_ref, hout_ref, n_ref, g_ref, u_ref, a_ref, loss_ref, gg_ref, wg_v, wu_v, wd_v = rest

        @pl.when(pl.program_id(0) == 0)
        def _():
            pltpu.sync_copy(wg_hbm, wg_v)
            pltpu.sync_copy(wu_hbm, wu_v)
            pltpu.sync_copy(wd_hbm, wd_v)
            if head is not None:
                loss_ref[...] = jnp.zeros_like(loss_ref)
                gg_ref[...] = jnp.zeros_like(gg_ref)

        hh = h_ref[...]
        r = lax.rsqrt(jnp.mean(hh * hh, axis=-1, keepdims=True) + EPS)
        n = (hh * r * gain_ref[...]).astype(BF16)
        n_ref[...] = n
        acc = jnp.zeros((TM, D_MODEL), F32)
        for c0, c1 in zip(FF_BOUNDS[:-1], FF_BOUNDS[1:]):
            sl = slice(c0, c1)
            g = _dot_nt(n, wg_v[sl, :])
            u = _dot_nt(n, wu_v[sl, :])
            sg = _sigmoid(g)
            silu = g * sg
            a = (silu * u).astype(BF16)
            a_ref[:, sl] = a
            g_ref[:, sl] = (u * (sg * (1.0 + g * (1.0 - sg)))).astype(BF16)
            u_ref[:, sl] = silu.astype(BF16)
            acc = acc + _dot(a, wd_v[sl, :])
        hout = hh + 0.5 * acc
        if head is None:
            hout_ref[...] = hout
        else:
            dh, part, gpart = _loss_tile(hout, fg_ref[...], tgt_ref[...])
            hout_ref[...] = dh
            loss_ref[...] += part
            gg_ref[...] += gpart

    row = lambda w: pl.BlockSpec((TM, w), lambda i: (i, 0))
    vec = pl.BlockSpec((1, D_MODEL), lambda i: (0, 0))
    wv = pltpu.VMEM((D_FF, D_MODEL), BF16)
    args, in_specs = (h, gain, wgt, wut, wd), [row(D_MODEL), vec, ANY, ANY, ANY]
    out_shape = [jax.ShapeDtypeStruct((t, D_MODEL), F32), jax.ShapeDtypeStruct((t, D_MODEL), BF16)] + [
        jax.ShapeDtypeStruct((t, D_FF), BF16)] * 3
    out_specs = [row(D_MODEL), row(D_MODEL), row(D_FF), row(D_FF), row(D_FF)]
    if head is not None:
        args, in_specs = args + tuple(head), in_specs + [vec, row(D_MODEL)]
        out_shape += [jax.ShapeDtypeStruct((8, 128), F32), jax.ShapeDtypeStruct((1, D_MODEL), F32)]
        out_specs += [pl.BlockSpec((8, 128), lambda i: (0, 0)), vec]
    return _pallas(
        body, args, name="ffn_fwd", grid=(t // TM,), out_shape=tuple(out_shape), in_specs=in_specs,
        out_specs=tuple(out_specs), scratch_shapes=[wv, wv, wv], sem=("arbitrary",), vmem=VMEM_BIG, rider=rider)


def _ffn_bwd(dhout, h, gain, dgf, duf, wgt, wut, wd):
    t = h.shape[0]
    tm = TM_BWD

    def body(dho_ref, h_ref, gain_ref, g_ref, u_ref, wg_hbm, wu_hbm, wd_hbm,
             dh_ref, dg_ref, du_ref, df_ref, gg_ref, wg_v, wu_v, wd_v):
        @pl.when(pl.program_id(0) == 0)
        def _():
            pltpu.sync_copy(wg_hbm, wg_v)
            pltpu.sync_copy(wu_hbm, wu_v)
            pltpu.sync_copy(wd_hbm, wd_v)
            gg_ref[...] = jnp.zeros_like(gg_ref)

        dho = dho_ref[...]
        df = (0.5 * dho).astype(BF16)
        df_ref[...] = df
        dn = jnp.zeros((tm, D_MODEL), F32)
        for c0, c1 in zip(FF_BOUNDS[:-1], FF_BOUNDS[1:]):
            sl = slice(c0, c1)
            da = _dot_nt(df, wd_v[sl, :])
            dg = (da * g_ref[:, sl].astype(F32)).astype(BF16)
            du = (da * u_ref[:, sl].astype(F32)).astype(BF16)
            dg_ref[:, sl] = dg
            du_ref[:, sl] = du
            dn = dn + _dot(dg, wg_v[sl, :]) + _dot(du, wu_v[sl, :])
        hh = h_ref[...]
        r = lax.rsqrt(jnp.mean(hh * hh, axis=-1, keepdims=True) + EPS)
        hn = hh * r
        gg_ref[...] += jnp.sum(dn * hn, axis=0, keepdims=True)
        dng = dn * gain_ref[...]
        dh_ref[...] = dho + r * (dng - hn * jnp.mean(dng * hn, axis=-1, keepdims=True))

    row = lambda w: pl.BlockSpec((tm, w), lambda i: (i, 0))
    vec = pl.BlockSpec((1, D_MODEL), lambda i: (0, 0))
    wv = pltpu.VMEM((D_FF, D_MODEL), BF16)
    return pl.pallas_call(
        body, name="ffn_bwd", grid=(t // tm,),
        out_shape=(jax.ShapeDtypeStruct((t, D_MODEL), F32), jax.ShapeDtypeStruct((t, D_FF), BF16),
                   jax.ShapeDtypeStruct((t, D_FF), BF16),
                   jax.ShapeDtypeStruct((t, D_MODEL), BF16), jax.ShapeDtypeStruct((1, D_MODEL), F32)),
        in_specs=[row(D_MODEL), row(D_MODEL), vec, row(D_FF), row(D_FF), ANY, ANY, ANY],
        out_specs=(row(D_MODEL), row(D_FF), row(D_FF), row(D_MODEL), vec),
        scratch_shapes=[wv, wv, wv],
        compiler_params=_params(("arbitrary",), VMEM_BIG),
    )(dhout, h, gain, dgf, duf, wgt, wut, wd)


def _wgrad(lhs, rhs, rb, with_colsum=False, name="wgrad", rider=None):
    t, k = lhs.shape
    n = rhs.shape[1]

    def body(l_ref, r_ref, o_ref, *rest):
        o_ref[...] = _dot_tn(l_ref[...], r_ref[...]).astype(BF16)
        if with_colsum:
            rest[0][...] = jnp.sum(l_ref[...].astype(F32), axis=0, keepdims=True)

    out_shape = [jax.ShapeDtypeStruct((k, n), BF16)]
    out_specs = [pl.BlockSpec((rb, n), lambda j: (j, 0))]
    if with_colsum:
        out_shape.append(jax.ShapeDtypeStruct((1, k), F32))
        out_specs.append(pl.BlockSpec((1, rb), lambda j: (0, j)))
    res, ro = _pallas(
        body, (lhs, rhs), name=name, grid=(k // rb,), out_shape=tuple(out_shape),
        in_specs=[pl.BlockSpec((t, rb), lambda j: (0, j)), pl.BlockSpec((t, n), lambda j: (0, 0))],
        out_specs=tuple(out_specs), sem=("arbitrary",), vmem=VMEM_BIG, rider=rider)
    if rider is not None:
        return res[0], ro
    return res if with_colsum else res[0]


def _lane_blocks(nseq, seq, nblk, tm=TM):
    spt = seq // tm
    return pl.BlockSpec((1, nblk, tm, 128), lambda i: (i // spt, 0, i % spt, 0))


def _inproj_fwd(h, gain, wint, b_in, nseq, rider=None):
    t = h.shape[0]
    seq = t // nseq
    cut_a = 5 * MXU_DIM
    pieces = ((0, cut_a, 0, 0), (cut_a, ZA_W - cut_a, 0, cut_a), (ZA_W, ZB_W, 1, 0), (ZA_W + ZB_W, 1024, 2, 0),
              (ZA_W + ZB_W + 1024, 1024, 2, 1024))

    def body(h_ref, gain_ref, w_hbm, b_ref, u_ref, za_ref, zb_ref, zg_ref, w_v):
        @pl.when(pl.program_id(0) == 0)
        def _():
            pltpu.sync_copy(w_hbm, w_v)

        hh = h_ref[...]
        r = lax.rsqrt(jnp.mean(hh * hh, axis=-1, keepdims=True) + EPS)
        un = (hh * r * gain_ref[...]).astype(BF16)
        u_ref[...] = un
        outs = (None, zb_ref, zg_ref)
        for c0, cw, oi, o0 in pieces:
            val = _dot_nt(un, w_v[c0:c0 + cw, :]) + b_ref[:, c0:c0 + cw]
            if oi == 0:
                for cb in range(cw // 128):
                    za_ref[0, o0 // 128 + cb] = val[:, cb * 128:(cb + 1) * 128]
            else:
                outs[oi][:, o0:o0 + cw] = val.astype(BF16)

    row = lambda w: pl.BlockSpec((TM, w), lambda i: (i, 0))
    return _pallas(
        body, (h, gain, wint, b_in), name="inproj_fwd", grid=(t // TM,),
        out_shape=(jax.ShapeDtypeStruct((t, D_MODEL), BF16), jax.ShapeDtypeStruct((nseq, ZA_W // 128, seq, 128), F32),
                   jax.ShapeDtypeStruct((t, ZB_W), BF16), jax.ShapeDtypeStruct((t, 2 * D_MODEL), BF16)),
        in_specs=[row(D_MODEL), pl.BlockSpec((1, D_MODEL), lambda i: (0, 0)), ANY,
                  pl.BlockSpec((1, D_IN), lambda i: (0, 0))],
        out_specs=(row(D_MODEL), _lane_blocks(nseq, seq, ZA_W // 128), row(ZB_W), row(2 * D_MODEL)),
        scratch_shapes=[pltpu.VMEM((D_IN, D_MODEL), BF16)], sem=("arbitrary",), vmem=VMEM_BIG, rider=rider)


def _inproj_bwd(dz, dh2, h, gain, wint, rider=None):
    t = h.shape[0]
    nc = 5
    cw = D_IN // nc

    def body(dz_ref, dh2_ref, h_ref, gain_ref, w_hbm, dh_ref, gg_ref, w_v):
        @pl.when(pl.program_id(0) == 0)
        def _():
            pltpu.sync_copy(w_hbm, w_v)
            gg_ref[...] = jnp.zeros_like(gg_ref)

        du = jnp.zeros((TM, D_MODEL), F32)
        for ci in range(nc):
            sl = slice(ci * cw, (ci + 1) * cw)
            du = du + _dot(dz_ref[:, sl], w_v[sl, :])
        hh = h_ref[...]
        r = lax.rsqrt(jnp.mean(hh * hh, axis=-1, keepdims=True) + EPS)
        hn = hh * r
        gg_ref[...] += jnp.sum(du * hn, axis=0, keepdims=True)
        dng = du * gain_ref[...]
        dh_ref[...] = dh2_ref[...] + r * (dng - hn * jnp.mean(dng * hn, axis=-1, keepdims=True))

    row = lambda w: pl.BlockSpec((TM, w), lambda i: (i, 0))
    vec = pl.BlockSpec((1, D_MODEL), lambda i: (0, 0))
    return _pallas(
        body, (dz, dh2, h, gain, wint), name="inproj_bwd", grid=(t // TM,),
        out_shape=(jax.ShapeDtypeStruct((t, D_MODEL), F32), jax.ShapeDtypeStruct((1, D_MODEL), F32)),
        in_specs=[row(D_IN), row(D_MODEL), row(D_MODEL), vec, ANY],
        out_specs=(row(D_MODEL), vec),
        scratch_shapes=[pltpu.VMEM((D_IN, D_MODEL), BF16)], sem=("arbitrary",), vmem=VMEM_BIG, rider=rider)


def _head_sums(x):
    w = x.shape[1]
    i = lax.broadcasted_iota(jnp.int32, (w, w), 0) // HEAD_DIM
    j = lax.broadcasted_iota(jnp.int32, (w, w), 1) // HEAD_DIM
    ones = (i == j).astype(BF16)
    hi = x.astype(BF16)
    r1 = x - hi.astype(F32)
    mid = r1.astype(BF16)
    lo = (r1 - mid.astype(F32)).astype(BF16)
    return _dot(hi, ones) + _dot(mid, ones) + _dot(lo, ones)


def _merge_fwd(o0, o1, o2, l0, l1, l2, yb, zg, h1, wat, wbt, wout, rider=None):
    t = h1.shape[0]
    nseq, _, seq, _ = o0.shape

    def body(o0_ref, o1_ref, o2_ref, l0_ref, l1_ref, l2_ref, yb_ref, ga_ref, gb_ref, h1_ref, wa_ref, wb_ref, wo_ref,
             h2_ref, y_ref, lt_ref, pa_ref, pb_ref, mg_ref):
        wide = lambda ref: jnp.concatenate([ref[0, 0], ref[0, 1]], axis=1)
        la, lb, lc = wide(l0_ref), wide(l1_ref), wide(l2_ref)
        mx = jnp.maximum(jnp.maximum(la, lb), lc)
        ea, eb, ec = jnp.exp(la - mx), jnp.exp(lb - mx), jnp.exp(lc - mx)
        den = ea + eb + ec
        y = (ea * wide(o0_ref) + eb * wide(o1_ref) + ec * wide(o2_ref)) / den
        lt = mx + jnp.log(den)
        lt_ref[0, 0] = lt[:, :128]
        lt_ref[0, 1] = lt[:, 128:]
        yb16 = y.astype(BF16)
        y_ref[...] = yb16
        pa = _dot_nt(yb16, wa_ref[...])
        pb = _dot_nt(yb_ref[...], wb_ref[...])
        pa_ref[...] = pa.astype(BF16)
        pb_ref[...] = pb.astype(BF16)
        mg = (_sigmoid(ga_ref[...].astype(F32)) * pa + _sigmoid(gb_ref[...].astype(F32)) * pb).astype(BF16)
        mg_ref[...] = mg
        h2_ref[...] = h1_ref[...] + _dot(mg, wo_ref[...])

    row = lambda w: pl.BlockSpec((TM, w), lambda i: (i, 0))
    full = lambda a: pl.BlockSpec(a.shape, lambda i: (0, 0))
    gate = lambda cb: pl.BlockSpec((TM, D_MODEL), lambda i: (i, cb))
    return _pallas(
        body, (o0, o1, o2, l0, l1, l2, yb, zg, zg, h1, wat, wbt, wout), name="merge_fwd", grid=(t // TM,),
        out_shape=(jax.ShapeDtypeStruct((t, D_MODEL), F32), jax.ShapeDtypeStruct((t, GW), BF16),
                   jax.ShapeDtypeStruct((nseq, 2, seq, 128), F32), jax.ShapeDtypeStruct((t, D_MODEL), BF16),
                   jax.ShapeDtypeStruct((t, D_MODEL), BF16), jax.ShapeDtypeStruct((t, D_MODEL), BF16)),
        in_specs=[_lane_blocks(nseq, seq, 2)] * 6 + [row(2 * GW), gate(0), gate(1), row(D_MODEL), full(wat), full(wbt),
                                                     full(wout)],
        out_specs=(row(D_MODEL), row(GW), _lane_blocks(nseq, seq, 2), row(D_MODEL), row(D_MODEL), row(D_MODEL)),
        sem=("parallel",), vmem=VMEM_BIG, rider=rider)


def _merge_bwd(dh2, pa, pb, zg, y, yb, wat, wbt, wout, nseq, rider=None):
    t = dh2.shape[0]

    def body(dh2_ref, pa_ref, pb_ref, ga_ref, gb_ref, y_ref, yb_ref, wa_ref, wb_ref, wo_ref,
             dpa_ref, dpb_ref, dga_ref, dgb_ref, dya_ref, dyb_ref, dh2b_ref, ca_ref, cb_ref):
        d16 = dh2_ref[...].astype(BF16)
        dh2b_ref[...] = d16
        dm = _dot_nt(d16, wo_ref[...])
        sa = _sigmoid(ga_ref[...].astype(F32))
        sb = _sigmoid(gb_ref[...].astype(F32))
        dpa = (dm * sa).astype(BF16)
        dpb = (dm * sb).astype(BF16)
        dpa_ref[...] = dpa
        dpb_ref[...] = dpb
        dga_ref[...] = (dm * pa_ref[...].astype(F32) * sa * (1.0 - sa)).astype(BF16)
        dgb_ref[...] = (dm * pb_ref[...].astype(F32) * sb * (1.0 - sb)).astype(BF16)
        dya = _dot(dpa, wa_ref[...])
        dyb = _dot(dpb, wb_ref[...])
        dya_ref[0, 0] = dya[:, :128]
        dya_ref[0, 1] = dya[:, 128:]
        dyb_ref[...] = dyb.astype(BF16)
        ca = _head_sums(dya * y_ref[...].astype(F32))
        ca_ref[0, 0] = ca[:, :128]
        ca_ref[0, 1] = ca[:, 128:]
        cb_ref[...] = _head_sums(dyb * yb_ref[...].astype(F32))

    row = lambda w: pl.BlockSpec((TM, w), lambda i: (i, 0))
    full = lambda a: pl.BlockSpec(a.shape, lambda i: (0, 0))
    gate = lambda cb: pl.BlockSpec((TM, D_MODEL), lambda i: (i, cb))
    bf = lambda w: jax.ShapeDtypeStruct((t, w), BF16)
    lanes = jax.ShapeDtypeStruct((nseq, 2, t // nseq, 128), F32)
    lane_spec = _lane_blocks(nseq, t // nseq, 2)
    return _pallas(
        body, (dh2, pa, pb, zg, zg, y, yb, wat, wbt, wout), name="merge_bwd", grid=(t // TM,),
        out_shape=(bf(D_MODEL), bf(D_MODEL), bf(D_MODEL), bf(D_MODEL), lanes, bf(2 * GW), bf(D_MODEL),
                   lanes, jax.ShapeDtypeStruct((t, 2 * GW), F32)),
        in_specs=[row(D_MODEL), row(D_MODEL), row(D_MODEL), gate(0), gate(1), row(GW), row(2 * GW),
                  full(wat), full(wbt), full(wout)],
        out_specs=(row(D_MODEL), row(D_MODEL), row(D_MODEL), row(D_MODEL), lane_spec, row(2 * GW), row(D_MODEL),
                   lane_spec, row(2 * GW)),
        sem=("parallel",), vmem=VMEM_BIG, rider=rider)


def _lane_head(rows):
    return lax.broadcasted_iota(jnp.int32, (rows, GW), 1) // HEAD_DIM


def _kv_expand_matrix(r):
    ci = lax.broadcasted_iota(jnp.int32, (2 * HEAD_DIM, GW), 0)
    ji = lax.broadcasted_iota(jnp.int32, (2 * HEAD_DIM, GW), 1)
    return (ci == (ji % HEAD_DIM) + HEAD_DIM * r).astype(BF16)


def _block_rows(row0, stride, ib):
    start = row0 + (stride * BLOCK) * ib
    if stride > 1:
        return pl.ds(start, BLOCK, stride=stride)
    return pl.ds(pl.multiple_of(start, BLOCK), BLOCK)


def _stack_heads(x, lane_head):
    return jnp.concatenate([jnp.where(lane_head == h, x, jnp.zeros_like(x)) for h in range(4)], axis=0)


def _unstack_heads(x4, lane_head):
    out = jnp.zeros((BLOCK, GW), F32)
    for h in range(4):
        out = jnp.where(lane_head == h, x4[h * BLOCK:(h + 1) * BLOCK], out)
    return out


def _load_rows(ref, rows, split):
    if split:
        return jnp.concatenate([ref[0, 0, rows, :], ref[0, 1, rows, :]], axis=1)
    return ref[0, rows, :]


def _store_rows(ref, rows, val, split):
    if split:
        ref[0, 0, rows, :] = val[:, :128]
        ref[0, 1, rows, :] = val[:, 128:]
    else:
        ref[0, rows, :] = val


def _attn_fwd(q_arr, k_arr, v_arr, bias, sink, *, grid, seq, stride, kvw, split, q_spec, k_spec, v_spec, bias_map,
              sink_map, o_spec, has_sink, o_shape, o_dtype, name, rider=None):
    nb = seq // stride // BLOCK
    scale = HEAD_DIM ** -0.5
    expanded = kvw != GW
    rps = min(stride, RESIDUES_PER_STEP)
    grid = (grid[0], grid[1] // rps)
    assert not has_sink or B_WINDOW - 1 < BLOCK

    def body(q_ref, k_ref, v_ref, bias_ref, sink_ref, o_ref, lse_ref, *kv_x):
        rr = pl.program_id(1)
        lane_head = _lane_head(BLOCK)
        if expanded:
            expand = _kv_expand_matrix(rr)
            kv_x[0][...] = _dot(k_ref[0], expand).astype(BF16)
            kv_x[1][...] = _dot(v_ref[0], expand).astype(BF16)
        for j in range(rps):
            residue(rr * rps + j if stride > 1 else 0, q_ref, k_ref, v_ref, bias_ref, sink_ref, o_ref, lse_ref, kv_x,
                    lane_head)

    def residue(row0, q_ref, k_ref, v_ref, bias_ref, sink_ref, o_ref, lse_ref, kv_x, lane_head):
        def per_head(fn, x):
            return jnp.concatenate([fn(sink_ref[0, h:h + 1, 0:1], x[h * BLOCK:(h + 1) * BLOCK]) for h in range(4)],
                                   axis=0)

        def load(ref, ib):
            return _load_rows(ref, _block_rows(row0, stride, ib), split).astype(BF16)

        def load_kv(which, ib):
            if expanded:
                return kv_x[which][_block_rows(0, 1, ib), :]
            return load((k_ref, v_ref)[which], ib)

        def block(ib, first):
            q4 = _stack_heads(load(q_ref, ib), lane_head)
            if first:
                kc, vc = load_kv(0, ib), load_kv(1, ib)
                b4 = bias_ref[:, :, BLOCK:].reshape(4 * BLOCK, BLOCK)
            else:
                kc = jnp.concatenate([load_kv(0, ib - 1), load_kv(0, ib)], axis=0)
                vc = jnp.concatenate([load_kv(1, ib - 1), load_kv(1, ib)], axis=0)
                b4 = bias_ref[...].reshape(4 * BLOCK, 2 * BLOCK)
                if has_sink:
                    oldest = lax.broadcasted_iota(jnp.int32, kc.shape, 0) == 0
                    kc = jnp.where(oldest, jnp.zeros_like(kc), kc)
                    vc = jnp.where(oldest, jnp.zeros_like(vc), vc)
            s = _dot_nt(q4, kc) * scale + b4
            m = jnp.max(s, axis=-1, keepdims=True)
            if has_sink and first:
                m = per_head(jnp.maximum, m)
            p = jnp.exp(s - m)
            l = jnp.sum(p, axis=-1, keepdims=True)
            if has_sink and first:
                l = l + per_head(lambda sk, mh: jnp.exp(sk - mh), m)
            o4 = _dot(p.astype(BF16), vc) / l
            rows = _block_rows(row0, stride, ib)
            _store_rows(o_ref, rows, _unstack_heads(o4, lane_head).astype(o_dtype), split)
            _store_rows(lse_ref, rows, _unstack_heads(m + jnp.log(l), lane_head), split)

        block(0, True)
        if nb > 1:
            def step(i, carry):
                block(i, False)
                return carry
            lax.fori_loop(1, nb, step, 0, unroll=min(ATTN_UNROLL, nb - 1))

    return _pallas(
        body, (q_arr, k_arr, v_arr, bias, sink), name=name, grid=grid,
        out_shape=(jax.ShapeDtypeStruct(o_shape, o_dtype), jax.ShapeDtypeStruct(o_shape, F32)),
        in_specs=[q_spec, k_spec, v_spec,
                  pl.BlockSpec((4, BLOCK, 2 * BLOCK), bias_map), pl.BlockSpec((1, 4, 128), sink_map)],
        out_specs=(o_spec, o_spec),
        scratch_shapes=[pltpu.VMEM((seq, GW), BF16)] * 2 if expanded else [],
        sem=("arbitrary", "arbitrary"), vmem=VMEM_BIG, rider=rider)


def _attn_bwd(q_arr, k_arr, v_arr, bias, sink, dy, cc, lse, *, grid, seq, stride, kvw, split, q_spec, k_spec, v_spec,
              bias_map, sink_map, o_spec, kv_out_spec, has_sink, n_bias, dq_shape, dkv_shape, g_dtype, name):
    ln = seq // stride
    nb = ln // BLOCK
    scale = HEAD_DIM ** -0.5
    expanded = kvw != GW
    rps = min(stride, RESIDUES_PER_STEP)
    grid = (grid[0], grid[1] // rps)

    def body(q_ref, k_ref, v_ref, bias_ref, sink_ref, dy_ref, c_ref, lse_ref,
             dq_ref, dk_ref, dv_ref, db_ref, dsk_ref, dk_acc, dv_acc, dk_half, dv_half, *kv_x):
        rr = pl.program_id(1)

        @pl.when((pl.program_id(0) == 0) & (rr == 0))
        def _():
            db_ref[...] = jnp.zeros_like(db_ref)
            dsk_ref[...] = jnp.zeros_like(dsk_ref)

        if expanded:
            expand = _kv_expand_matrix(rr)
            kv_x[0][...] = _dot(k_ref[0], expand).astype(BF16)
            kv_x[1][...] = _dot(v_ref[0], expand).astype(BF16)
        refs = (q_ref, k_ref, v_ref, bias_ref, sink_ref, dy_ref, c_ref, lse_ref, dq_ref, dk_ref, dv_ref, db_ref,
                dsk_ref, dk_acc, dv_acc, dk_half, dv_half, kv_x)
        for j in range(rps):
            residue(rr, rr * rps + j if stride > 1 else 0, *refs)

    def residue(rr, row0, q_ref, k_ref, v_ref, bias_ref, sink_ref, dy_ref, c_ref, lse_ref,
                dq_ref, dk_ref, dv_ref, db_ref, dsk_ref, dk_acc, dv_acc, dk_half, dv_half, kv_x):
        dk_acc[...] = jnp.zeros_like(dk_acc)
        dv_acc[...] = jnp.zeros_like(dv_acc)
        lane_head = _lane_head(BLOCK)
        hb = 4 * rr if n_bias == 8 else 0

        def load(ref, ib):
            return _load_rows(ref, _block_rows(row0, stride, ib), split)

        def load_kv(which, ib):
            if expanded:
                return kv_x[which][_block_rows(0, 1, ib), :]
            return load((k_ref, v_ref)[which], ib).astype(BF16)

        def head_col(x):
            return jnp.concatenate([x[:, h * HEAD_DIM:h * HEAD_DIM + 1] for h in range(4)], axis=0)

        def block(ib, first):
            q4 = _stack_heads(load(q_ref, ib).astype(BF16), lane_head)
            dy4 = _stack_heads(load(dy_ref, ib).astype(BF16), lane_head)
            c4 = head_col(load(c_ref, ib))
            l4 = head_col(load(lse_ref, ib))
            if first:
                kc, vc = load_kv(0, ib), load_kv(1, ib)
                b4 = bias_ref[:, :, BLOCK:].reshape(4 * BLOCK, BLOCK)
                krows = pl.ds(0, BLOCK)
            else:
                kc = jnp.concatenate([load_kv(0, ib - 1), load_kv(0, ib)], axis=0)
                vc = jnp.concatenate([load_kv(1, ib - 1), load_kv(1, ib)], axis=0)
                b4 = bias_ref[...].reshape(4 * BLOCK, 2 * BLOCK)
                krows = pl.ds(pl.multiple_of((ib - 1) * BLOCK, BLOCK), 2 * BLOCK)
            nk = BLOCK if first else 2 * BLOCK
            p = jnp.exp(_dot_nt(q4, kc) * scale + b4 - l4)
            ds = p * (_dot_nt(dy4, vc) - c4)
            ds3 = ds.reshape(4, BLOCK, nk)
            if n_bias == 8:
                if first:
                    db_ref[pl.ds(hb, 4), :, BLOCK:] += ds3
                else:
                    db_ref[pl.ds(hb, 4)] += ds3
            elif first:
                db_ref[:, :, BLOCK:] += ds3
            else:
                db_ref[...] += ds3
            ds16 = ds.astype(BF16)
            dq = _unstack_heads(_dot(ds16, kc), lane_head) * scale
            _store_rows(dq_ref, _block_rows(row0, stride, ib), dq.astype(g_dtype), split)
            dk_acc[krows, :] += _dot_tn(ds16, q4) * scale
            dv_acc[krows, :] += _dot_tn(p.astype(BF16), dy4)
            if has_sink:
                for h in range(4):
                    hs = slice(h * BLOCK, (h + 1) * BLOCK)
                    sk = sink_ref[0, h:h + 1, 0:1]
                    val = -jnp.sum(jnp.exp(sk - l4[hs]) * c4[hs], axis=0, keepdims=True)
                    dsk_ref[hb + h] += jnp.broadcast_to(val, (8, 128))

        block(0, True)
        if nb > 1:
            def step(i, carry):
                block(i, False)
                return carry
            lax.fori_loop(1, nb, step, 0, unroll=min(ATTN_UNROLL, nb - 1))

        if kvw == GW:
            all_rows = pl.ds(row0, ln, stride=stride) if stride > 1 else pl.ds(0, ln)
            _store_rows(dk_ref, all_rows, dk_acc[...].astype(g_dtype), split)
            _store_rows(dv_ref, all_rows, dv_acc[...].astype(g_dtype), split)
        else:
            def fold(acc):
                t2 = acc[:, :2 * HEAD_DIM] + acc[:, 2 * HEAD_DIM:]
                t2 = t2 + pltpu.roll(t2, HEAD_DIM, 1)
                lane = lax.broadcasted_iota(jnp.int32, t2.shape, 1) // HEAD_DIM
                return jnp.where(lane == rr, t2, 0.0)

            @pl.when(rr == 0)
            def _():
                dk_half[...] = fold(dk_acc[...])
                dv_half[...] = fold(dv_acc[...])

            @pl.when(rr == 1)
            def _():
                dk_ref[0] = (dk_half[...] + fold(dk_acc[...])).astype(g_dtype)
                dv_ref[0] = (dv_half[...] + fold(dv_acc[...])).astype(g_dtype)

    return pl.pallas_call(
        body, name=name, grid=grid,
        out_shape=(jax.ShapeDtypeStruct(dq_shape, g_dtype), jax.ShapeDtypeStruct(dkv_shape, g_dtype),
                   jax.ShapeDtypeStruct(dkv_shape, g_dtype), jax.ShapeDtypeStruct((n_bias, BLOCK, 2 * BLOCK), F32),
                   jax.ShapeDtypeStruct((8, 8, 128), F32)),
        in_specs=[q_spec, k_spec, v_spec,
                  pl.BlockSpec((4, BLOCK, 2 * BLOCK), bias_map), pl.BlockSpec((1, 4, 128), sink_map),
                  o_spec, o_spec, o_spec],
        out_specs=(o_spec, kv_out_spec, kv_out_spec,
                   pl.BlockSpec((n_bias, BLOCK, 2 * BLOCK), lambda n, r: (0, 0, 0)),
                   pl.BlockSpec((8, 8, 128), lambda n, r: (0, 0, 0))),
        scratch_shapes=[pltpu.VMEM((ln, GW), F32), pltpu.VMEM((ln, GW), F32),
                        pltpu.VMEM((ln, 2 * HEAD_DIM), F32), pltpu.VMEM((ln, 2 * HEAD_DIM), F32)]
        + ([pltpu.VMEM((seq, GW), BF16)] * 2 if expanded else []),
        compiler_params=_params(("arbitrary", "arbitrary"), VMEM_BIG),
    )(q_arr, k_arr, v_arr, bias, sink, dy, cc, lse)


def _bias_grad(ds_all, buckets):
    def body(ds_ref, bk_ref, o_ref):
        rows = lax.broadcasted_iota(jnp.int32, (N_BUCKETS, 128), 0)
        cols = lax.broadcasted_iota(jnp.int32, (N_BUCKETS, 128), 1)

        def per_bucket(b, acc):
            for h in range(20):
                gi = h // 4 if h < 12 else 3
                v = jnp.where(bk_ref[gi] == b, ds_ref[h], 0.0)
                v = jnp.sum(jnp.sum(v, axis=1, keepdims=True), axis=0, keepdims=True)
                acc = jnp.where((rows == b) & (cols == h), v, acc)
            return acc

        o_ref[...] = lax.fori_loop(0, N_BUCKETS, per_bucket, jnp.zeros((N_BUCKETS, 128), F32))

    vm = pl.BlockSpec(memory_space=pltpu.VMEM)
    return pl.pallas_call(body, name="bias_grad", out_shape=jax.ShapeDtypeStruct((N_BUCKETS, 128), F32),
                          in_specs=[vm, vm], out_specs=vm)(ds_all, buckets)


def _adamw(w, g, m, v, name):
    (res,), _ = _adamw_many([(w, g, m, v)], name)
    return res


def _adamw_many(tensors, name, rider=None):
    n = len(tensors)
    r, c = tensors[0][0].shape
    tr = r
    for cand in (256, 176, 128, 88, 64, 32, 16, 8):
        if r % cand == 0 and cand * c * 4 * 7 * n * 2 <= 24 * 1024 * 1024:
            tr = cand
            break

    def body(*refs):
        ins, outs = refs[:4 * n], refs[4 * n:]
        for i in range(n):
            w_ref, g_ref, m_ref, v_ref = ins[4 * i:4 * i + 4]
            d, nm, nv = _adam_update(w_ref[...], g_ref[...], m_ref[...], v_ref[...])
            outs[3 * i][...], outs[3 * i + 1][...], outs[3 * i + 2][...] = d, nm, nv

    spec = pl.BlockSpec((tr, c), lambda i: (i, 0))
    shp = jax.ShapeDtypeStruct((r, c), F32)
    res, ro = _pallas(body, tuple(a for t4 in tensors for a in t4), name=name, grid=(r // tr,),
                      out_shape=(shp,) * (3 * n), in_specs=[spec] * (4 * n), out_specs=(spec,) * (3 * n),
                      sem=("parallel",), vmem=VMEM_BIG, rider=rider)
    return [tuple(res[3 * i:3 * i + 3]) for i in range(n)], ro


def _t5_bucket(dist):
    max_exact = N_BUCKETS // 2
    n = jnp.maximum(dist, 0)
    nf = jnp.maximum(n, 1).astype(F32)
    large = max_exact + (jnp.log(nf / max_exact) / math.log(MAX_DISTANCE / max_exact)
                         * (N_BUCKETS - max_exact)).astype(jnp.int32)
    large = jnp.minimum(large, N_BUCKETS - 1)
    return jnp.where(n < max_exact, n, large)


def _bias_tables(rel_bias):
    qi = jnp.arange(BLOCK)[:, None]
    ki = jnp.arange(2 * BLOCK)[None, :]
    dist = qi + BLOCK - ki
    specs = [(d, w // d, 4 * gi, 4 * gi + 4) for gi, (w, d) in enumerate(DIL_GROUPS)] + [(1, B_WINDOW - 1, 12, 20)]
    biases, buckets = [], []
    for stride, steps, h0, h1 in specs:
        valid = (dist >= 0) & (dist <= steps)
        bk = jnp.where(valid, _t5_bucket(dist * stride), -1).astype(jnp.int32)
        onehot = (bk[None, :, :] == jnp.arange(N_BUCKETS, dtype=jnp.int32)[:, None, None]).astype(F32)
        b = jnp.einsum("bqk,bh->hqk", onehot, rel_bias[:, h0:h1], precision=lax.Precision.HIGHEST)
        biases.append(jnp.where(valid[None], b, NEG))
        buckets.append(bk)
    return jnp.concatenate(biases, axis=0), jnp.stack(buckets, axis=0)


def _local_step(x, tgt, W, S, shards=None, tail_host=None):
    nseq, seq, _ = x.shape
    t = nseq * seq
    xf = x.reshape(t, D_MODEL)
    bias_all, buckets = _bias_tables(S["rel_bias"])
    sink_b = jnp.broadcast_to(S["sinks"].reshape(2, 4, 1), (2, 4, 128)).astype(F32)
    sink_0 = jnp.zeros((1, 4, 128), F32)
    dist = shards is not None
    W = dict(W)
    G, GS, reduced = {}, {}, {}

    def put(keys, gathered):
        for k, g in zip(keys, gathered):
            W[k] = g.reshape(_FULL_SHAPE.get(k, (N_CHIPS * shards[k].shape[0], D_MODEL)))

    def gather_rider(keys):
        return _GatherRider([shards[k] for k in keys]) if dist else None

    def pair(keys):
        return _pair_reduce([G[k].reshape(N_CHIPS, 2, shards[k].shape[0] // 2, D_MODEL) for k in keys],
                            "grad_pair_reduce_" + keys[0])

    def finish(keys, own, rec):
        full = _final_reduce(own, rec, "grad_final_reduce_" + keys[0])
        off = 0
        for k in keys:
            r = shards[k].shape[0]
            reduced[k] = full[:, off:off + r // 2].reshape(r, D_MODEL)
            off += r // 2

    if dist:
        first = ("wgt1", "wut1", "wd1")
        put(first, _gather_rows([shards[k] for k in first]))
    keys = ("wint",)
    (h1, n1, g1, u1, a1), ro = _ffn_fwd(xf, S["ffn1_norm"], W["wgt1"], W["wut1"], W["wd1"], rider=gather_rider(keys))
    put(keys, ro)
    keys = ("wout", "wat", "wbt", "wgt2")
    (un, za, zb, zg), ro = _inproj_fwd(h1, S["mix_norm"], W["wint"], S["b_in"], nseq, rider=gather_rider(keys))
    put(keys, ro)

    seq3 = lambda a: a.reshape(nseq, seq, a.shape[-1])
    zb3 = seq3(zb)
    pair_blk = lambda cb: pl.BlockSpec((1, 2, seq, 128), lambda n, r, cb=cb: (n, cb, 0, 0))
    a_cfg = []
    outs, lses = [], []
    for gi, (_, d) in enumerate(DIL_GROUPS):
        cfg = dict(grid=(nseq, d), seq=seq, stride=d, kvw=GW, split=True,
                   q_spec=pair_blk(gi), k_spec=pair_blk(3 + gi), v_spec=pair_blk(6 + gi), o_spec=pair_blk(0),
                   bias_map=lambda n, r: (0, 0, 0), sink_map=lambda n, r: (0, 0, 0), has_sink=False)
        a_cfg.append(cfg)
        (o, lse), _ = _attn_fwd(za, za, za, bias_all[4 * gi:4 * gi + 4], sink_0, o_shape=(nseq, 2, seq, 128),
                                o_dtype=F32, name=f"attn_a{gi}_fwd", **cfg)
        outs.append(o)
        lses.append(lse)
    wide_blk = lambda w, cmap: pl.BlockSpec((1, seq, w), cmap)
    b_cfg = dict(grid=(nseq, 2), seq=seq, stride=1, kvw=2 * HEAD_DIM, split=False,
                 q_spec=wide_blk(GW, lambda n, r: (n, 0, r)), k_spec=wide_blk(2 * HEAD_DIM, lambda n, r: (n, 0, 4)),
                 v_spec=wide_blk(2 * HEAD_DIM, lambda n, r: (n, 0, 5)), o_spec=wide_blk(GW, lambda n, r: (n, 0, r)),
                 bias_map=lambda n, r: (r, 0, 0), sink_map=lambda n, r: (r, 0, 0), has_sink=True)
    keys = ("wut2",)
    bias_b_fwd = bias_all[12:20].at[:, :, 0].set(jnp.broadcast_to(S["sinks"].reshape(8, 1), (8, BLOCK)))
    (yb, lse_b), ro = _attn_fwd(zb3, zb3, zb3, bias_b_fwd, sink_b, o_shape=(nseq, seq, 2 * GW), o_dtype=BF16,
                                name="attn_b_fwd", rider=gather_rider(keys), **b_cfg)
    put(keys, ro)
    yb = yb.reshape(t, 2 * GW)

    keys = ("wd2",)
    (h2, y, lse_tot, pa, pb, merged), ro = _merge_fwd(outs[0], outs[1], outs[2], lses[0], lses[1], lses[2], yb, zg, h1,
                                                      W["wat"], W["wbt"], W["wout"], rider=gather_rider(keys))
    put(keys, ro)
    (dh3, n2, g2, u2, a2, loss_part, g_final), _ = _ffn_fwd(
        h2, S["ffn2_norm"], W["wgt2"], W["wut2"], W["wd2"],
        head=(S["final_norm"].reshape(1, D_MODEL), tgt.reshape(t, D_MODEL)))

    GS["final_norm"] = g_final
    dh2, dg2, du2, df2, GS["ffn2_norm"] = _ffn_bwd(dh3, h2, S["ffn2_norm"], g2, u2, W["wgt2"], W["wut2"], W["wd2"])
    G["wgt2"] = _wgrad(dg2, n2, MXU_DIM, name="wgrad_gate2")
    G["wut2"] = _wgrad(du2, n2, MXU_DIM, name="wgrad_up2")
    G["wd2"] = _wgrad(a2, df2, MXU_DIM, name="wgrad_down2")

    keys = ("wgt2", "wut2", "wd2")
    rider = _ExchangeRider([pair(keys)]) if dist else None
    (dpa, dpb, dga, dgb, dya, dyb, dh2b, ca, cb), ro = _merge_bwd(dh2, pa, pb, zg, y, yb, W["wat"], W["wbt"], W["wout"],
                                                                  nseq, rider=rider)
    if dist:
        finish(keys, *ro)

    dqs, dks, dvs, dbs = [], [], [], []
    shp = (nseq, 2, seq, 128)
    halves = lambda a: [a[:, hf].reshape(t, 128).astype(BF16) for hf in range(2)]
    for gi in range(len(DIL_GROUPS)):
        dq, dk, dv, db, _ = _attn_bwd(za, za, za, bias_all[4 * gi:4 * gi + 4], sink_0, dya, ca, lse_tot,
                                      n_bias=4, dq_shape=shp, dkv_shape=shp, g_dtype=F32,
                                      kv_out_spec=a_cfg[gi]["o_spec"], name=f"attn_a{gi}_bwd", **a_cfg[gi])
        dqs += halves(dq)
        dks += halves(dk)
        dvs += halves(dv)
        dbs.append(db)
    dqb, dkb, dvb, dbb, dsink = _attn_bwd(zb3, zb3, zb3, bias_all[12:20], sink_b, seq3(dyb), seq3(cb), lse_b,
                                          n_bias=8, dq_shape=(nseq, seq, 2 * GW),
                                          dkv_shape=(nseq, seq, 2 * HEAD_DIM), g_dtype=BF16,
                                          kv_out_spec=wide_blk(2 * HEAD_DIM, lambda n, r: (n, 0, 0)),
                                          name="attn_b_bwd", **b_cfg)
    dz = jnp.concatenate(dqs + dks + dvs + [dqb.reshape(t, 2 * GW), dkb.reshape(t, 2 * HEAD_DIM),
                                            dvb.reshape(t, 2 * HEAD_DIM), dga, dgb], axis=-1)
    gb_tab = _bias_grad(jnp.concatenate(dbs + [dbb], axis=0), buckets)
    if dist:
        GS["bias_tab"], GS["sink_tiles"] = gb_tab, dsink
    else:
        GS["rel_bias"] = gb_tab[:, :20]
        GS["sinks"] = dsink[:, 0, 0].reshape(1, 8)

    G["wint"], GS["b_in"] = _wgrad(dz, un, MXU_DIM, with_colsum=True, name="wgrad_in")
    G["wout"] = _wgrad(merged, dh2b, MXU_DIM, name="wgrad_out")
    G["wat"] = _wgrad(dpa, y, MXU_DIM, name="wgrad_branch_a")
    G["wbt"] = _wgrad(dpb, yb, MXU_DIM, name="wgrad_branch_b")
    keys = ("wint", "wout", "wat", "wbt")
    rider = _ExchangeRider([pair(keys)]) if dist else None
    (dh1, GS["mix_norm"]), ro = _inproj_bwd(dz, dh2, h1, S["mix_norm"], W["wint"], rider=rider)
    if dist:
        finish(keys, *ro)

    dx, dg1, du1, df1, GS["ffn1_norm"] = _ffn_bwd(dh1, xf, S["ffn1_norm"], g1, u1, W["wgt1"], W["wut1"], W["wd1"])
    G["wgt1"] = _wgrad(dg1, n1, MXU_DIM, name="wgrad_gate1")
    if dist:
        G["wut1"], ro = _wgrad(du1, n1, MXU_DIM, name="wgrad_up1", rider=_ExchangeRider([pair(("wgt1",))]))
        finish(("wgt1",), *ro)
        G["wd1"], ro = _wgrad(a1, df1, MXU_DIM, name="wgrad_down1", rider=_ExchangeRider([pair(("wut1",))]))
        finish(("wut1",), *ro)
        finish(("wd1",), *tail_host(_ExchangeRider([pair(("wd1",))]), reduced))
    else:
        G["wut1"] = _wgrad(du1, n1, MXU_DIM, name="wgrad_up1")
        G["wd1"] = _wgrad(a1, df1, MXU_DIM, name="wgrad_down1")
    return loss_part, dx.reshape(x.shape), (reduced if dist else G), GS


_SMALL = ("ffn1_norm", "mix_norm", "ffn2_norm", "final_norm", "b_in", "sinks", "rel_bias")
_ORDER = ("ffn1_norm", "ffn1_w_gate", "ffn1_w_up", "ffn1_w_down", "mix_norm", "w_in", "b_in", "w_branch_a",
          "w_branch_b", "w_out", "sinks", "rel_bias", "ffn2_norm", "ffn2_w_gate", "ffn2_w_up", "ffn2_w_down",
          "final_norm")
_BIG = (("wgt1", "ffn1_w_gate", True, 704), ("wut1", "ffn1_w_up", True, 704), ("wd1", "ffn1_w_down", False, 704),
        ("wint", "w_in", True, 1280), ("wout", "w_out", False, 256), ("wat", "w_branch_a", True, 64),
        ("wbt", "w_branch_b", True, 128), ("wgt2", "ffn2_w_gate", True, 704), ("wut2", "ffn2_w_up", True, 704),
        ("wd2", "ffn2_w_down", False, 704))
_FULL_SHAPE = {"wat": (D_MODEL, GW), "wbt": (D_MODEL, 2 * GW)}


def kernel(x, ffn1_norm, ffn1_w_gate, ffn1_w_up, ffn1_w_down, mix_norm, w_in, b_in, w_branch_a, w_branch_b, w_out, sinks, rel_bias, ffn2_norm, ffn2_w_gate, ffn2_w_up, ffn2_w_down, final_norm, loss_target, m_ffn1_norm, m_ffn1_w_gate, m_ffn1_w_up, m_ffn1_w_down, m_mix_norm, m_w_in, m_b_in, m_w_branch_a, m_w_branch_b, m_w_out, m_sinks, m_rel_bias, m_ffn2_norm, m_ffn2_w_gate, m_ffn2_w_up, m_ffn2_w_down, m_final_norm, v_ffn1_norm, v_ffn1_w_gate, v_ffn1_w_up, v_ffn1_w_down, v_mix_norm, v_w_in, v_b_in, v_w_branch_a, v_w_branch_b, v_w_out, v_sinks, v_rel_bias, v_ffn2_norm, v_ffn2_w_gate, v_ffn2_w_up, v_ffn2_w_down, v_final_norm):
    args = dict(locals())
    w = {n: args[n] for n in _ORDER}
    m = {n: args["m_" + n] for n in _ORDER}
    v = {n: args["v_" + n] for n in _ORDER}

    shards = {}
    for key, name, transposed, rows in _BIG:
        a = w[name][0]
        a = (a.T if transposed else a).astype(BF16)
        shards[key] = a.reshape(rows, D_MODEL)
    S = {n: w[n] for n in _SMALL}

    row_adam = lambda n: (w[n][0].T, m[n][0].T, v[n][0].T)
    early = {}

    def tail_host(rider, reduced):
        tensors = []
        for key, n in (("wgt2", "ffn2_w_gate"), ("wut2", "ffn2_w_up"), ("wd2", "ffn2_w_down")):
            wmv = row_adam(n) if key != "wd2" else (w[n][0], m[n][0], v[n][0])
            tensors.append((wmv[0], reduced[key], wmv[1], wmv[2]))
        res, ro = _adamw_many(tensors, "adamw_ffn2", rider=rider)
        early["ffn2_w_gate"], early["ffn2_w_up"], early["ffn2_w_down"] = res
        return ro

    loss_part, grad_x, reduced, GS = _local_step(x, loss_target, {}, S, shards, tail_host)

    small = _allreduce_small(GS["ffn1_norm"], GS["mix_norm"], GS["ffn2_norm"], GS["final_norm"], GS["b_in"],
                             GS["sink_tiles"], GS["bias_tab"], loss_part)
    loss = small[9, 8]

    out_g, out_d, out_m, out_v = {}, {}, {}, {}
    for key, n, transposed, rows in _BIG:
        nat = w[n][0].shape
        if transposed and nat[1] % 128:
            res = early[n] if n in early else _adamw(row_adam(n)[0], reduced[key], *row_adam(n)[1:], "adamw_" + n)
            res = [reduced[key].T] + [r.T for r in res]
        elif n in early:
            res = [reduced[key]] + list(early[n])
        else:
            g = reduced[key].reshape(nat[1], nat[0]).T if transposed else reduced[key].reshape(nat)
            res = [g] + list(_adamw(w[n][0], g, m[n][0], v[n][0], "adamw_" + n))
        out_g[n], out_d[n], out_m[n], out_v[n] = [r[None] for r in res]
    row = lambda d: {n: (d[n].reshape(1, D_MODEL) if n == "final_norm" else d[n]) for n in _SMALL}
    for dst, src in zip((out_g, out_d, out_m, out_v), _adamw_small(small, row(w), row(m), row(v))):
        dst.update(src)
        dst["final_norm"] = src["final_norm"].reshape(D_MODEL)

    return (loss, grad_x, *[out_g[n] for n in _ORDER], *[out_d[n] for n in _ORDER],
            *[out_m[n] for n in _ORDER], *[out_v[n] for n in _ORDER])
```

```python
import math

import jax
import jax.numpy as jnp
from jax import lax
from jax.experimental import pallas as pl
from jax.experimental.pallas import tpu as pltpu

F32, BF16 = jnp.float32, jnp.bfloat16
MESH = pl.DeviceIdType.MESH

D_MODEL = 1024
D_FF = 2816
D_IN = 5120
HEAD_DIM = 64
BLOCK = 128
DIL_GROUPS = ((128, 1), (512, 4), (2048, 16))
B_WINDOW = 128
N_BUCKETS = 32
MAX_DISTANCE = 2048
EPS = 1e-6
N_CHIPS = 4
GW = 256
ZA_W = 2304
ZB_W = 768
NEG = -1e30

ADAM_LR, ADAM_B1, ADAM_B2, ADAM_EPS, ADAM_WD, ADAM_STEP = 0.001, 0.9, 0.999, 1e-08, 0.01, 10

VMEM_BIG = 56 * 1024 * 1024
TM = 512
TM_BWD = 256
MXU_DIM = 256
FF_BOUNDS = (0, 4 * MXU_DIM, 8 * MXU_DIM, D_FF)
DMA_SPLIT = 8
RESIDUES_PER_STEP = 8
ATTN_UNROLL = 15


def _dot(a, b):
    return jnp.dot(a, b, preferred_element_type=F32)


def _dot_nt(a, b):
    return lax.dot_general(a, b, (((1,), (1,)), ((), ())), preferred_element_type=F32)


def _dot_tn(a, b):
    return lax.dot_general(a, b, (((0,), (0,)), ((), ())), preferred_element_type=F32)


def _sigmoid(x):
    return 0.5 * jnp.tanh(0.5 * x) + 0.5


def _params(sem, vmem=None):
    return pltpu.CompilerParams(dimension_semantics=sem, vmem_limit_bytes=vmem)


ANY = pl.BlockSpec(memory_space=pl.ANY)


def _me():
    return lax.axis_index("x"), lax.axis_index("y"), lax.axis_index("c")


_CHIP_RELS = ((1, 0), (0, 1), (1, 1))


def _flip(v, f):
    return 1 - v if f else v


def _remote(src, dst, ssem, rsem, peer):
    return pltpu.make_async_remote_copy(src_ref=src, dst_ref=dst, send_sem=ssem, recv_sem=rsem,
                                        device_id=peer, device_id_type=MESH)


def _row_pieces(rows, n):
    step = max(16, -(-rows // n) // 16 * 16)
    out, s = [], 0
    while s < rows:
        out.append((s, min(step, rows - s)))
        s += step
    return out


def _gather_rows(shards):
    nt = len(shards)
    rows = [s.shape[0] for s in shards]

    def body(*refs):
        srcs, outs = refs[:nt], refs[nt:2 * nt]
        halves, quarters = refs[2 * nt:3 * nt], refs[3 * nt:4 * nt]
        ici_s, ici_r, fwd_s, fwd_r, d2d_s, d2d_r, keep, loc = refs[4 * nt:]
        x, y, c = _me()
        j = 2 * x + y
        sib = (x, y, 1 - c)
        nbr = ((1 - x, y, c), (x, 1 - y, c))
        nbr_j = (2 * (1 - x) + y, 2 * x + (1 - y))
        diag_j = 2 * (1 - x) + (1 - y)
        local = [pltpu.make_async_copy(srcs[t], outs[t].at[j], loc.at[t]) for t in range(nt)]
        for cp in local:
            cp.start()
        pending = []
        for a in range(2):
            for t in range(nt):
                half = pl.ds(c * (rows[t] // 2), rows[t] // 2)
                cp = _remote(srcs[t].at[half], halves[t].at[a], ici_s.at[2 * t + a], ici_r.at[2 * t + a], nbr[a])
                cp.start()
                pending.append(cp)
        placed = []

        def place(src, dst_of, idx):
            mine = pltpu.make_async_copy(src, dst_of, keep.at[idx])
            mine.start()
            cp = _remote(src, dst_of, d2d_s.at[idx], d2d_r.at[idx], sib)
            cp.start()
            placed.append((mine, cp))

        for a in range(2):
            for t in range(nt):
                r2, r4 = rows[t] // 2, rows[t] // 4
                got = halves[t].at[a]
                _remote(got, got, ici_s.at[2 * t + a], ici_r.at[2 * t + a], nbr[a]).wait_recv()
                cp = _remote(halves[t].at[a, pl.ds(a * r4, r4)], quarters[t].at[a], fwd_s.at[2 * t + a],
                             fwd_r.at[2 * t + a], nbr[1 - a])
                cp.start()
                pending.append(cp)
                place(got, outs[t].at[nbr_j[a], pl.ds(c * r2, r2)], 4 * t + a)
        for a in range(2):
            for t in range(nt):
                r2, r4 = rows[t] // 2, rows[t] // 4
                got = quarters[t].at[a]
                _remote(got, got, fwd_s.at[2 * t + a], fwd_r.at[2 * t + a], nbr[1 - a]).wait_recv()
                place(got, outs[t].at[diag_j, pl.ds(c * r2 + a * r4, r4)], 4 * t + 2 + a)
        for mine, cp in placed:
            mine.wait()
            cp.wait()
        for cp in pending:
            cp.wait_send()
        for cp in local:
            cp.wait()

    stage = ([pltpu.VMEM((2, r // 2, D_MODEL), BF16) for r in rows] + [pltpu.VMEM((2, r // 4, D_MODEL), BF16) for r in rows])
    sems = ([pltpu.SemaphoreType.DMA((2 * nt,)) for _ in range(4)] + [pltpu.SemaphoreType.DMA((4 * nt,))] * 3
            + [pltpu.SemaphoreType.DMA((nt,))])
    return pl.pallas_call(
        body, name="gather_weights",
        out_shape=tuple(jax.ShapeDtypeStruct((N_CHIPS,) + s.shape, s.dtype) for s in shards),
        in_specs=[pl.BlockSpec(memory_space=pltpu.VMEM)] * nt,
        out_specs=tuple([ANY] * nt), scratch_shapes=stage + sems,
    )(*shards)


VMEM_WHOLE = pl.BlockSpec(memory_space=pltpu.VMEM)


def _pair_reduce(grads, name):
    nt = len(grads)
    r2 = [g.shape[2] for g in grads]
    off = [sum(r2[:t]) for t in range(nt)]
    tot = sum(r2)

    def body(*refs):
        gs = refs[:nt]
        s_ref, mine, got, ssem, rsem, lsem = refs[nt:]
        x, y, c = _me()
        sib = (x, y, 1 - c)
        for t in range(nt):
            for k in range(N_CHIPS):
                rows = pl.ds(off[t], r2[t])
                _remote(gs[t].at[k, 1 - c], got.at[k, rows], ssem, rsem, sib).start()
                pltpu.make_async_copy(gs[t].at[k, c], mine.at[k, rows], lsem).start()
        pltpu.make_async_copy(mine, mine, lsem).wait()
        _remote(got, got, ssem, rsem, sib).wait()
        for k in range(N_CHIPS):
            for st, sz in _row_pieces(tot, 4):
                rows = slice(st, st + sz)
                s_ref[k, rows, :] = (mine[k, rows, :].astype(F32) + got[k, rows, :].astype(F32)).astype(BF16)

    shp = jax.ShapeDtypeStruct((N_CHIPS, tot, D_MODEL), BF16)
    buf = pltpu.VMEM((N_CHIPS, tot, D_MODEL), BF16)
    return pl.pallas_call(
        body, name=name, out_shape=shp, in_specs=[ANY] * nt, out_specs=VMEM_WHOLE,
        scratch_shapes=[buf, buf, pltpu.SemaphoreType.DMA(()), pltpu.SemaphoreType.DMA(()),
                        pltpu.SemaphoreType.DMA(())],
        compiler_params=pltpu.CompilerParams(vmem_limit_bytes=VMEM_BIG),
    )(*grads)


def _chip_exchange(parts):
    ng = len(parts)
    r2 = [p.shape[1] for p in parts]
    off = [sum(r2[:g]) for g in range(ng)]
    tot = sum(r2)

    def body(*refs):
        ps = refs[:ng]
        own_ref, rec_ref, ssems, rsems, lsem = refs[ng:]
        x, y, c = _me()
        j = 2 * x + y
        for g in range(ng):
            pltpu.make_async_copy(ps[g].at[j], own_ref.at[pl.ds(off[g], r2[g])], lsem).start()
        for k, (fx, fy) in enumerate(_CHIP_RELS):
            px, py = _flip(x, fx), _flip(y, fy)
            for g in range(ng):
                for st, sz in _row_pieces(r2[g], 2):
                    _remote(ps[g].at[2 * px + py, pl.ds(st, sz)], rec_ref.at[k, pl.ds(off[g] + st, sz)],
                            ssems.at[k], rsems.at[k], (px, py, c)).start()
        for k in range(3):
            _remote(rec_ref.at[k], rec_ref.at[k], ssems.at[k], rsems.at[k], (x, y, c)).wait()
        pltpu.make_async_copy(own_ref, own_ref, lsem).wait()

    return pl.pallas_call(
        body, name="grad_chip_exchange",
        out_shape=(jax.ShapeDtypeStruct((tot, D_MODEL), BF16), jax.ShapeDtypeStruct((3, tot, D_MODEL), BF16)),
        in_specs=[VMEM_WHOLE] * ng, out_specs=(ANY, ANY),
        scratch_shapes=[pltpu.SemaphoreType.DMA((3,)), pltpu.SemaphoreType.DMA((3,)), pltpu.SemaphoreType.DMA(())],
    )(*parts)


def _final_reduce(own, rec, name):
    r2 = own.shape[0]
    pieces = _row_pieces(r2, DMA_SPLIT)

    def body(own_ref, rec_ref, o_ref, fbuf, ssem, rsem, lsem):
        x, y, c = _me()
        sib = (x, y, 1 - c)
        for st, sz in pieces:
            rows = slice(st, st + sz)
            fbuf[rows, :] = (own_ref[rows, :].astype(F32) + rec_ref[0, rows, :].astype(F32)
                             + rec_ref[1, rows, :].astype(F32) + rec_ref[2, rows, :].astype(F32))
            pltpu.make_async_copy(fbuf.at[pl.ds(st, sz)], o_ref.at[c, pl.ds(st, sz)], lsem).start()
            _remote(fbuf.at[pl.ds(st, sz)], o_ref.at[c, pl.ds(st, sz)], ssem, rsem, sib).start()
        _remote(fbuf, o_ref.at[c], ssem, rsem, sib).wait()
        pltpu.make_async_copy(fbuf, o_ref.at[c], lsem).wait()

    return pl.pallas_call(
        body, name=name, out_shape=jax.ShapeDtypeStruct((2, r2, D_MODEL), F32),
        in_specs=[VMEM_WHOLE, VMEM_WHOLE], out_specs=ANY,
        scratch_shapes=[pltpu.VMEM((r2, D_MODEL), F32), pltpu.SemaphoreType.DMA(()), pltpu.SemaphoreType.DMA(()),
                        pltpu.SemaphoreType.DMA(())],
        compiler_params=pltpu.CompilerParams(vmem_limit_bytes=VMEM_BIG),
    )(own, rec)


SMALL_ROWS = 48


def _allreduce_small(g_ffn1, g_mix, g_ffn2, g_final, g_bin, dsink, bias_tab, loss_part):
    def body(f1_ref, mx_ref, f2_ref, fn_ref, bi_ref, sk_ref, bt_ref, ls_ref, o_ref, mine, buf, send_sems, recv_sems):
        x, y, c = _me()
        me = 4 * x + 2 * y + c
        mine[...] = jnp.zeros_like(mine)
        for r, ref in enumerate((f1_ref, mx_ref, f2_ref, fn_ref)):
            mine[r:r + 1, :] = ref[...]
        for k in range(D_IN // D_MODEL):
            mine[4 + k:5 + k, :] = bi_ref[:, k * D_MODEL:(k + 1) * D_MODEL]
        lane = lax.broadcasted_iota(jnp.int32, (1, 128), 1)
        row = jnp.where(lane == 8, ls_ref[0:1, :], 0.0)
        for h in range(8):
            row = jnp.where(lane == h, sk_ref[h, 0:1, :], row)
        mine[9:10, 0:128] = row
        mine[16:48, 0:128] = bt_ref[...]
        buf[me] = mine[...]
        copies = []
        for k in range(1, 8):
            peer = (_flip(x, (k >> 2) & 1), _flip(y, (k >> 1) & 1), _flip(c, k & 1))
            cp = _remote(mine, buf.at[me], send_sems.at[k - 1], recv_sems.at[k - 1], peer)
            cp.start()
            copies.append(cp)
        for cp in copies:
            cp.wait()
        acc = buf[0]
        for i in range(1, 8):
            acc = acc + buf[i]
        o_ref[...] = acc

    vm = pl.BlockSpec(memory_space=pltpu.VMEM)
    shape = (SMALL_ROWS, D_MODEL)
    return pl.pallas_call(
        body, name="allreduce_small", out_shape=jax.ShapeDtypeStruct(shape, F32),
        in_specs=[vm] * 8, out_specs=vm,
        scratch_shapes=[pltpu.VMEM(shape, F32), pltpu.VMEM((8,) + shape, F32), pltpu.SemaphoreType.DMA((7,)),
                        pltpu.SemaphoreType.DMA((7,))],
    )(g_ffn1, g_mix, g_ffn2, g_final, g_bin, dsink, bias_tab, loss_part)


def _adam_update(w, g, m, v):
    nm = ADAM_B1 * m + (1.0 - ADAM_B1) * g
    nv = ADAM_B2 * v + (1.0 - ADAM_B2) * (g * g)
    bc1 = 1.0 - ADAM_B1 ** ADAM_STEP
    bc2 = 1.0 - ADAM_B2 ** ADAM_STEP
    return -ADAM_LR * ((nm / bc1) / (jnp.sqrt(nv / bc2) + ADAM_EPS) + ADAM_WD * w), nm, nv


def _adamw_small(packed, w, m, v):
    names = ("ffn1_norm", "mix_norm", "ffn2_norm", "final_norm", "b_in", "sinks", "rel_bias")
    nn = len(names)

    def grad_of(p_ref, name, k=0):
        if name == "b_in":
            return p_ref[4 + k:5 + k, :]
        if name == "sinks":
            return p_ref[9:10, 0:8]
        if name == "rel_bias":
            return p_ref[16:48, 0:20]
        r = names.index(name)
        return p_ref[r:r + 1, :]

    def body(p_ref, *refs):
        ws, ms, vs = refs[:nn], refs[nn:2 * nn], refs[2 * nn:3 * nn]
        outs = refs[3 * nn:]
        for i, name in enumerate(names):
            og, od, om, ov = outs[i], outs[nn + i], outs[2 * nn + i], outs[3 * nn + i]
            pieces = range(D_IN // D_MODEL) if name == "b_in" else (0,)
            for k in pieces:
                sl = (slice(None), slice(k * D_MODEL, (k + 1) * D_MODEL)) if name == "b_in" else (Ellipsis,)
                g = grad_of(p_ref, name, k)
                d, nm, nv = _adam_update(ws[i][sl], g, ms[i][sl], vs[i][sl])
                og[sl], od[sl], om[sl], ov[sl] = g, d, nm, nv

    vm = pl.BlockSpec(memory_space=pltpu.VMEM)
    shapes = [jax.ShapeDtypeStruct(w[n].shape, F32) for n in names]
    res = pl.pallas_call(
        body, name="adamw_small", out_shape=tuple(shapes * 4), in_specs=[vm] * (1 + 3 * nn),
        out_specs=tuple([vm] * (4 * nn)),
    )(packed, *[w[n] for n in names], *[m[n] for n in names], *[v[n] for n in names])
    return [dict(zip(names, res[i * nn:(i + 1) * nn])) for i in range(4)]


class _GatherRider:
    def __init__(self, shards):
        self.inputs = list(shards)
        nt = len(shards)
        self.out_shape = [jax.ShapeDtypeStruct((N_CHIPS,) + s.shape, s.dtype) for s in shards]
        self.scratch = [pltpu.SemaphoreType.DMA((3 * nt,)), pltpu.SemaphoreType.DMA((3 * nt,)),
                        pltpu.SemaphoreType.DMA((nt,))]

    def _copies(self, srcs, outs, sems):
        ici_s, ici_r, loc = sems
        x, y, c = _me()
        j = 2 * x + y
        local = [pltpu.make_async_copy(srcs[t], outs[t].at[j], loc.at[t]) for t in range(len(srcs))]
        remote = []
        for k, (fx, fy) in enumerate(_CHIP_RELS):
            peer = (_flip(x, fx), _flip(y, fy), c)
            for t in range(len(srcs)):
                remote.append(_remote(srcs[t], outs[t].at[j], ici_s.at[3 * t + k], ici_r.at[3 * t + k], peer))
        return local, remote

    def start(self, srcs, outs, sems):
        local, remote = self._copies(srcs, outs, sems)
        for cp in local + remote:
            cp.start()

    def finish(self, srcs, outs, sems):
        local, remote = self._copies(srcs, outs, sems)
        for cp in remote + local:
            cp.wait()


class _ExchangeRider:
    def __init__(self, parts):
        self.inputs = list(parts)
        self.r2 = [p.shape[1] for p in parts]
        self.off = [sum(self.r2[:g]) for g in range(len(parts))]
        tot = sum(self.r2)
        self.out_shape = [jax.ShapeDtypeStruct((tot, D_MODEL), BF16), jax.ShapeDtypeStruct((3, tot, D_MODEL), BF16)]
        self.scratch = [pltpu.SemaphoreType.DMA((3,)), pltpu.SemaphoreType.DMA((3,)), pltpu.SemaphoreType.DMA(())]

    def start(self, ps, outs, sems):
        own_ref, rec_ref = outs
        ssems, rsems, lsem = sems
        x, y, c = _me()
        j = 2 * x + y
        for g in range(len(ps)):
            pltpu.make_async_copy(ps[g].at[j], own_ref.at[pl.ds(self.off[g], self.r2[g])], lsem).start()
        for k, (fx, fy) in enumerate(_CHIP_RELS):
            px, py = _flip(x, fx), _flip(y, fy)
            for g in range(len(ps)):
                for st, sz in _row_pieces(self.r2[g], 2):
                    _remote(ps[g].at[2 * px + py, pl.ds(st, sz)], rec_ref.at[k, pl.ds(self.off[g] + st, sz)],
                            ssems.at[k], rsems.at[k], (px, py, c)).start()

    def finish(self, ps, outs, sems):
        own_ref, rec_ref = outs
        ssems, rsems, lsem = sems
        x, y, c = _me()
        for k in range(3):
            _remote(rec_ref.at[k], rec_ref.at[k], ssems.at[k], rsems.at[k], (x, y, c)).wait()
        pltpu.make_async_copy(own_ref, own_ref, lsem).wait()


def _pallas(body, args, *, name, grid, in_specs, out_specs, out_shape, scratch_shapes=(), sem=None, vmem=None,
            rider=None):
    if rider is None:
        res = pl.pallas_call(body, name=name, grid=grid, in_specs=list(in_specs), out_specs=tuple(out_specs),
                             out_shape=tuple(out_shape), scratch_shapes=list(scratch_shapes),
                             compiler_params=_params(sem, vmem))(*args)
        return tuple(res), ()
    n_in, n_out, n_sc = len(in_specs), len(out_shape), len(scratch_shapes)
    r_in, r_out = len(rider.inputs), len(rider.out_shape)

    def wrapped(*refs):
        ins, rins = refs[:n_in], refs[n_in:n_in + r_in]
        p = n_in + r_in
        outs, routs = refs[p:p + n_out], refs[p + n_out:p + n_out + r_out]
        p += n_out + r_out
        scr, rsems = refs[p:p + n_sc], refs[p + n_sc:]
        first = pl.program_id(0) == 0
        last = pl.program_id(0) == grid[0] - 1
        for a in range(1, len(grid)):
            first = first & (pl.program_id(a) == 0)
            last = last & (pl.program_id(a) == grid[a] - 1)

        @pl.when(first)
        def _():
            rider.start(rins, routs, rsems)

        body(*ins, *outs, *scr)

        @pl.when(last)
        def _():
            rider.finish(rins, routs, rsems)

    res = pl.pallas_call(
        wrapped, name=name, grid=grid, in_specs=list(in_specs) + [ANY] * r_in,
        out_specs=tuple(out_specs) + (ANY,) * r_out, out_shape=tuple(out_shape) + tuple(rider.out_shape),
        scratch_shapes=list(scratch_shapes) + rider.scratch,
        compiler_params=_params(("arbitrary",) * len(grid), vmem))(*args, *rider.inputs)
    return tuple(res[:n_out]), tuple(res[n_out:])


def _load_weights(pairs, sems):
    copies = [pltpu.make_async_copy(hbm, vmem, sems.at[i]) for i, (hbm, vmem) in enumerate(pairs)]
    for cp in copies:
        cp.start()
    for cp in copies:
        cp.wait()


def _loss_tile(hh, gain, tgt):
    r = lax.rsqrt(jnp.mean(hh * hh, axis=-1, keepdims=True) + EPS)
    hn = hh * r
    err = hn * gain - tgt
    part = (0.5 / D_MODEL) * jnp.sum(jnp.sum(err * err, axis=1, keepdims=True), axis=0, keepdims=True)
    dy = err * (1.0 / D_MODEL)
    dng = dy * gain
    dh = r * (dng - hn * jnp.mean(dng * hn, axis=-1, keepdims=True))
    return dh, part, jnp.sum(dy * hn, axis=0, keepdims=True)


def _ffn_fwd(h, gain, wgt, wut, wd, rider=None, head=None):
    t = h.shape[0]

    def body(h_ref, gain_ref, wg_hbm, wu_hbm, wd_hbm, *rest):
        if head is None:
            hout_ref, n_ref, g_ref, u_ref, a_ref, wg_v, wu_v, wd_v, wsem = rest
        else:
            fg_ref, tgt_ref, hout_ref, n_ref, g_ref, u_ref, a_ref, loss_ref, gg_ref, wg_v, wu_v, wd_v, wsem = rest
        @pl.when(pl.program_id(0) == 0)
        def _():
            _load_weights(((wg_hbm, wg_v), (wu_hbm, wu_v), (wd_hbm, wd_v)), wsem)
            if head is not None:
                loss_ref[...] = jnp.zeros_like(loss_ref)
                gg_ref[...] = jnp.zeros_like(gg_ref)

        hh = h_ref[...]
        r = lax.rsqrt(jnp.mean(hh * hh, axis=-1, keepdims=True) + EPS)
        n = (hh * r * gain_ref[...]).astype(BF16)
        n_ref[...] = n
        acc = jnp.zeros((TM, D_MODEL), F32)
        for c0, c1 in zip(FF_BOUNDS[:-1], FF_BOUNDS[1:]):
            sl = slice(c0, c1)
            g = _dot_nt(n, wg_v[sl, :])
            u = _dot_nt(n, wu_v[sl, :])
            sg = _sigmoid(g)
            silu = g * sg
            a = (silu * u).astype(BF16)
            a_ref[:, sl] = a
            g_ref[:, sl] = (u * (sg * (1.0 + g * (1.0 - sg)))).astype(BF16)
            u_ref[:, sl] = silu.astype(BF16)
            acc = acc + _dot(a, wd_v[sl, :])
        hout = hh + 0.5 * acc
        if head is None:
            hout_ref[...] = hout
        else:
            dh, part, gpart = _loss_tile(hout, fg_ref[...], tgt_ref[...])
            hout_ref[...] = dh
            loss_ref[...] += part
            gg_ref[...] += gpart

    row = lambda w: pl.BlockSpec((TM, w), lambda i: (i, 0))
    vec = pl.BlockSpec((1, D_MODEL), lambda i: (0, 0))
    wv = pltpu.VMEM((D_FF, D_MODEL), BF16)
    args, in_specs = (h, gain, wgt, wut, wd), [row(D_MODEL), vec, ANY, ANY, ANY]
    out_shape = [jax.ShapeDtypeStruct((t, D_MODEL), F32), jax.ShapeDtypeStruct((t, D_MODEL), BF16)] + [
        jax.ShapeDtypeStruct((t, D_FF), BF16)] * 3
    out_specs = [row(D_MODEL), row(D_MODEL), row(D_FF), row(D_FF), row(D_FF)]
    if head is not None:
        args, in_specs = args + tuple(head), in_specs + [vec, row(D_MODEL)]
        out_shape += [jax.ShapeDtypeStruct((8, 128), F32), jax.ShapeDtypeStruct((1, D_MODEL), F32)]
        out_specs += [pl.BlockSpec((8, 128), lambda i: (0, 0)), vec]
    return _pallas(
        body, args, name="ffn_fwd", grid=(t // TM,), out_shape=tuple(out_shape), in_specs=in_specs,
        out_specs=tuple(out_specs), scratch_shapes=[wv, wv, wv, pltpu.SemaphoreType.DMA((3,))],
        sem=("arbitrary",), vmem=VMEM_BIG, rider=rider)


def _ffn_bwd(dhout, h, gain, dgf, duf, wgt, wut, wd):
    t = h.shape[0]
    tm = TM_BWD

    def body(dho_ref, h_ref, gain_ref, g_ref, u_ref, wg_hbm, wu_hbm, wd_hbm,
             dh_ref, dg_ref, du_ref, df_ref, gg_ref, wg_v, wu_v, wd_v, wsem):
        @pl.when(pl.program_id(0) == 0)
        def _():
            _load_weights(((wd_hbm, wd_v), (wg_hbm, wg_v), (wu_hbm, wu_v)), wsem)
            gg_ref[...] = jnp.zeros_like(gg_ref)

        dho = dho_ref[...]
        df = (0.5 * dho).astype(BF16)
        df_ref[...] = df
        dn = jnp.zeros((tm, D_MODEL), F32)
        for c0, c1 in zip(FF_BOUNDS[:-1], FF_BOUNDS[1:]):
            sl = slice(c0, c1)
            da = _dot_nt(df, wd_v[sl, :])
            dg = (da * g_ref[:, sl].astype(F32)).astype(BF16)
            du = (da * u_ref[:, sl].astype(F32)).astype(BF16)
            dg_ref[:, sl] = dg
            du_ref[:, sl] = du
            dn = dn + _dot(dg, wg_v[sl, :]) + _dot(du, wu_v[sl, :])
        hh = h_ref[...]
        r = lax.rsqrt(jnp.mean(hh * hh, axis=-1, keepdims=True) + EPS)
        hn = hh * r
        gg_ref[...] += jnp.sum(dn * hn, axis=0, keepdims=True)
        dng = dn * gain_ref[...]
        dh_ref[...] = dho + r * (dng - hn * jnp.mean(dng * hn, axis=-1, keepdims=True))

    row = lambda w: pl.BlockSpec((tm, w), lambda i: (i, 0))
    vec = pl.BlockSpec((1, D_MODEL), lambda i: (0, 0))
    wv = pltpu.VMEM((D_FF, D_MODEL), BF16)
    return pl.pallas_call(
        body, name="ffn_bwd", grid=(t // tm,),
        out_shape=(jax.ShapeDtypeStruct((t, D_MODEL), F32), jax.ShapeDtypeStruct((t, D_FF), BF16),
                   jax.ShapeDtypeStruct((t, D_FF), BF16),
                   jax.ShapeDtypeStruct((t, D_MODEL), BF16), jax.ShapeDtypeStruct((1, D_MODEL), F32)),
        in_specs=[row(D_MODEL), row(D_MODEL), vec, row(D_FF), row(D_FF), ANY, ANY, ANY],
        out_specs=(row(D_MODEL), row(D_FF), row(D_FF), row(D_MODEL), vec),
        scratch_shapes=[wv, wv, wv, pltpu.SemaphoreType.DMA((3,))],
        compiler_params=_params(("arbitrary",), VMEM_BIG),
    )(dhout, h, gain, dgf, duf, wgt, wut, wd)


def _wgrad(lhs, rhs, rb, with_colsum=False, name="wgrad", rider=None):
    t, k = lhs.shape
    n = rhs.shape[1]

    def body(l_ref, r_ref, o_ref, *rest):
        o_ref[...] = _dot_tn(l_ref[...], r_ref[...]).astype(BF16)
        if with_colsum:
            rest[0][...] = jnp.sum(l_ref[...].astype(F32), axis=0, keepdims=True)

    out_shape = [jax.ShapeDtypeStruct((k, n), BF16)]
    out_specs = [pl.BlockSpec((rb, n), lambda j: (j, 0))]
    if with_colsum:
        out_shape.append(jax.ShapeDtypeStruct((1, k), F32))
        out_specs.append(pl.BlockSpec((1, rb), lambda j: (0, j)))
    res, ro = _pallas(
        body, (lhs, rhs), name=name, grid=(k // rb,), out_shape=tuple(out_shape),
        in_specs=[pl.BlockSpec((t, rb), lambda j: (0, j)), pl.BlockSpec((t, n), lambda j: (0, 0))],
        out_specs=tuple(out_specs), sem=("arbitrary",), vmem=VMEM_BIG, rider=rider)
    if rider is not None:
        return res[0], ro
    return res if with_colsum else res[0]


def _lane_blocks(nseq, seq, nblk, tm=TM):
    spt = seq // tm
    return pl.BlockSpec((1, nblk, tm, 128), lambda i: (i // spt, 0, i % spt, 0))


def _inproj_fwd(h, gain, wint, b_in, nseq, rider=None):
    t = h.shape[0]
    seq = t // nseq
    cut_a = 5 * MXU_DIM
    pieces = ((0, cut_a, 0, 0), (cut_a, ZA_W - cut_a, 0, cut_a), (ZA_W, ZB_W, 1, 0), (ZA_W + ZB_W, 1024, 2, 0),
              (ZA_W + ZB_W + 1024, 1024, 2, 1024))

    def body(h_ref, gain_ref, w_hbm, b_ref, u_ref, za_ref, zb_ref, zg_ref, w_v):
        @pl.when(pl.program_id(0) == 0)
        def _():
            pltpu.sync_copy(w_hbm, w_v)

        hh = h_ref[...]
        r = lax.rsqrt(jnp.mean(hh * hh, axis=-1, keepdims=True) + EPS)
        un = (hh * r * gain_ref[...]).astype(BF16)
        u_ref[...] = un
        outs = (None, zb_ref, zg_ref)
        for c0, cw, oi, o0 in pieces:
            val = _dot_nt(un, w_v[c0:c0 + cw, :]) + b_ref[:, c0:c0 + cw]
            if oi == 0:
                for cb in range(cw // 128):
                    za_ref[0, o0 // 128 + cb] = val[:, cb * 128:(cb + 1) * 128]
            else:
                outs[oi][:, o0:o0 + cw] = val.astype(BF16)

    row = lambda w: pl.BlockSpec((TM, w), lambda i: (i, 0))
    return _pallas(
        body, (h, gain, wint, b_in), name="inproj_fwd", grid=(t // TM,),
        out_shape=(jax.ShapeDtypeStruct((t, D_MODEL), BF16), jax.ShapeDtypeStruct((nseq, ZA_W // 128, seq, 128), F32),
                   jax.ShapeDtypeStruct((t, ZB_W), BF16), jax.ShapeDtypeStruct((t, 2 * D_MODEL), BF16)),
        in_specs=[row(D_MODEL), pl.BlockSpec((1, D_MODEL), lambda i: (0, 0)), ANY,
                  pl.BlockSpec((1, D_IN), lambda i: (0, 0))],
        out_specs=(row(D_MODEL), _lane_blocks(nseq, seq, ZA_W // 128), row(ZB_W), row(2 * D_MODEL)),
        scratch_shapes=[pltpu.VMEM((D_IN, D_MODEL), BF16)], sem=("arbitrary",), vmem=VMEM_BIG, rider=rider)


def _inproj_bwd(dz, dh2, h, gain, wint, rider=None):
    t = h.shape[0]
    nc = 5
    cw = D_IN // nc

    def body(dz_ref, dh2_ref, h_ref, gain_ref, w_hbm, dh_ref, gg_ref, w_v):
        @pl.when(pl.program_id(0) == 0)
        def _():
            pltpu.sync_copy(w_hbm, w_v)
            gg_ref[...] = jnp.zeros_like(gg_ref)

        du = jnp.zeros((TM, D_MODEL), F32)
        for ci in range(nc):
            sl = slice(ci * cw, (ci + 1) * cw)
            du = du + _dot(dz_ref[:, sl], w_v[sl, :])
        hh = h_ref[...]
        r = lax.rsqrt(jnp.mean(hh * hh, axis=-1, keepdims=True) + EPS)
        hn = hh * r
        gg_ref[...] += jnp.sum(du * hn, axis=0, keepdims=True)
        dng = du * gain_ref[...]
        dh_ref[...] = dh2_ref[...] + r * (dng - hn * jnp.mean(dng * hn, axis=-1, keepdims=True))

    row = lambda w: pl.BlockSpec((TM, w), lambda i: (i, 0))
    vec = pl.BlockSpec((1, D_MODEL), lambda i: (0, 0))
    return _pallas(
        body, (dz, dh2, h, gain, wint), name="inproj_bwd", grid=(t // TM,),
        out_shape=(jax.ShapeDtypeStruct((t, D_MODEL), F32), jax.ShapeDtypeStruct((1, D_MODEL), F32)),
        in_specs=[row(D_IN), row(D_MODEL), row(D_MODEL), vec, ANY],
        out_specs=(row(D_MODEL), vec),
        scratch_shapes=[pltpu.VMEM((D_IN, D_MODEL), BF16)], sem=("arbitrary",), vmem=VMEM_BIG, rider=rider)


def _head_sums(x):
    w = x.shape[1]
    i = lax.broadcasted_iota(jnp.int32, (w, w), 0) // HEAD_DIM
    j = lax.broadcasted_iota(jnp.int32, (w, w), 1) // HEAD_DIM
    ones = (i == j).astype(BF16)
    hi = x.astype(BF16)
    r1 = x - hi.astype(F32)
    mid = r1.astype(BF16)
    lo = (r1 - mid.astype(F32)).astype(BF16)
    return _dot(hi, ones) + _dot(mid, ones) + _dot(lo, ones)


def _merge_fwd(o0, o1, o2, l0, l1, l2, yb, zg, h1, wat, wbt, wout, rider=None):
    t = h1.shape[0]
    nseq, _, seq, _ = o0.shape

    def body(o0_ref, o1_ref, o2_ref, l0_ref, l1_ref, l2_ref, yb_ref, ga_ref, gb_ref, h1_ref, wa_ref, wb_ref, wo_ref,
             h2_ref, y_ref, lt_ref, pa_ref, pb_ref, mg_ref):
        wide = lambda ref: jnp.concatenate([ref[0, 0], ref[0, 1]], axis=1)
        la, lb, lc = wide(l0_ref), wide(l1_ref), wide(l2_ref)
        mx = jnp.maximum(jnp.maximum(la, lb), lc)
        ea, eb, ec = jnp.exp(la - mx), jnp.exp(lb - mx), jnp.exp(lc - mx)
        den = ea + eb + ec
        y = (ea * wide(o0_ref) + eb * wide(o1_ref) + ec * wide(o2_ref)) / den
        lt = mx + jnp.log(den)
        lt_ref[0, 0] = lt[:, :128]
        lt_ref[0, 1] = lt[:, 128:]
        yb16 = y.astype(BF16)
        y_ref[...] = yb16
        pa = _dot_nt(yb16, wa_ref[...])
        pb = _dot_nt(yb_ref[...], wb_ref[...])
        pa_ref[...] = pa.astype(BF16)
        pb_ref[...] = pb.astype(BF16)
        mg = (_sigmoid(ga_ref[...].astype(F32)) * pa + _sigmoid(gb_ref[...].astype(F32)) * pb).astype(BF16)
        mg_ref[...] = mg
        h2_ref[...] = h1_ref[...] + _dot(mg, wo_ref[...])

    row = lambda w: pl.BlockSpec((TM, w), lambda i: (i, 0))
    full = lambda a: pl.BlockSpec(a.shape, lambda i: (0, 0))
    gate = lambda cb: pl.BlockSpec((TM, D_MODEL), lambda i: (i, cb))
    return _pallas(
        body, (o0, o1, o2, l0, l1, l2, yb, zg, zg, h1, wat, wbt, wout), name="merge_fwd", grid=(t // TM,),
        out_shape=(jax.ShapeDtypeStruct((t, D_MODEL), F32), jax.ShapeDtypeStruct((t, GW), BF16),
                   jax.ShapeDtypeStruct((nseq, 2, seq, 128), F32), jax.ShapeDtypeStruct((t, D_MODEL), BF16),
                   jax.ShapeDtypeStruct((t, D_MODEL), BF16), jax.ShapeDtypeStruct((t, D_MODEL), BF16)),
        in_specs=[_lane_blocks(nseq, seq, 2)] * 6 + [row(2 * GW), gate(0), gate(1), row(D_MODEL), full(wat), full(wbt),
                                                     full(wout)],
        out_specs=(row(D_MODEL), row(GW), _lane_blocks(nseq, seq, 2), row(D_MODEL), row(D_MODEL), row(D_MODEL)),
        sem=("parallel",), vmem=VMEM_BIG, rider=rider)


def _merge_bwd(dh2, pa, pb, zg, y, yb, wat, wbt, wout, nseq, rider=None):
    t = dh2.shape[0]

    def body(dh2_ref, pa_ref, pb_ref, ga_ref, gb_ref, y_ref, yb_ref, wa_ref, wb_ref, wo_ref,
             dpa_ref, dpb_ref, dga_ref, dgb_ref, dya_ref, dyb_ref, dh2b_ref, ca_ref, cb_ref):
        d16 = dh2_ref[...].astype(BF16)
        dh2b_ref[...] = d16
        dm = _dot_nt(d16, wo_ref[...])
        sa = _sigmoid(ga_ref[...].astype(F32))
        sb = _sigmoid(gb_ref[...].astype(F32))
        dpa = (dm * sa).astype(BF16)
        dpb = (dm * sb).astype(BF16)
        dpa_ref[...] = dpa
        dpb_ref[...] = dpb
        dga_ref[...] = (dm * pa_ref[...].astype(F32) * sa * (1.0 - sa)).astype(BF16)
        dgb_ref[...] = (dm * pb_ref[...].astype(F32) * sb * (1.0 - sb)).astype(BF16)
        dya = _dot(dpa, wa_ref[...])
        dyb = _dot(dpb, wb_ref[...])
        dya_ref[0, 0] = dya[:, :128]
        dya_ref[0, 1] = dya[:, 128:]
        dyb_ref[...] = dyb.astype(BF16)
        ca = _head_sums(dya * y_ref[...].astype(F32))
        ca_ref[0, 0] = ca[:, :128]
        ca_ref[0, 1] = ca[:, 128:]
        cb_ref[...] = _head_sums(dyb * yb_ref[...].astype(F32))

    row = lambda w: pl.BlockSpec((TM, w), lambda i: (i, 0))
    full = lambda a: pl.BlockSpec(a.shape, lambda i: (0, 0))
    gate = lambda cb: pl.BlockSpec((TM, D_MODEL), lambda i: (i, cb))
    bf = lambda w: jax.ShapeDtypeStruct((t, w), BF16)
    lanes = jax.ShapeDtypeStruct((nseq, 2, t // nseq, 128), F32)
    lane_spec = _lane_blocks(nseq, t // nseq, 2)
    return _pallas(
        body, (dh2, pa, pb, zg, zg, y, yb, wat, wbt, wout), name="merge_bwd", grid=(t // TM,),
        out_shape=(bf(D_MODEL), bf(D_MODEL), bf(D_MODEL), bf(D_MODEL), lanes, bf(2 * GW), bf(D_MODEL),
                   lanes, jax.ShapeDtypeStruct((t, 2 * GW), F32)),
        in_specs=[row(D_MODEL), row(D_MODEL), row(D_MODEL), gate(0), gate(1), row(GW), row(2 * GW),
                  full(wat), full(wbt), full(wout)],
        out_specs=(row(D_MODEL), row(D_MODEL), row(D_MODEL), row(D_MODEL), lane_spec, row(2 * GW), row(D_MODEL),
                   lane_spec, row(2 * GW)),
        sem=("parallel",), vmem=VMEM_BIG, rider=rider)


def _lane_head(rows):
    return lax.broadcasted_iota(jnp.int32, (rows, GW), 1) // HEAD_DIM


def _kv_expand_matrix(r):
    ci = lax.broadcasted_iota(jnp.int32, (2 * HEAD_DIM, GW), 0)
    ji = lax.broadcasted_iota(jnp.int32, (2 * HEAD_DIM, GW), 1)
    return (ci == (ji % HEAD_DIM) + HEAD_DIM * r).astype(BF16)


def _block_rows(row0, stride, ib):
    start = row0 + (stride * BLOCK) * ib
    if stride > 1:
        return pl.ds(start, BLOCK, stride=stride)
    return pl.ds(pl.multiple_of(start, BLOCK), BLOCK)


def _stack_heads(x, lane_head):
    return jnp.concatenate([jnp.where(lane_head == h, x, jnp.zeros_like(x)) for h in range(4)], axis=0)


def _unstack_heads(x4, lane_head):
    out = jnp.zeros((BLOCK, GW), F32)
    for h in range(4):
        out = jnp.where(lane_head == h, x4[h * BLOCK:(h + 1) * BLOCK], out)
    return out


def _load_rows(ref, rows, split):
    if split:
        return jnp.concatenate([ref[0, 0, rows, :], ref[0, 1, rows, :]], axis=1)
    return ref[0, rows, :]


def _store_rows(ref, rows, val, split):
    if split:
        ref[0, 0, rows, :] = val[:, :128]
        ref[0, 1, rows, :] = val[:, 128:]
    else:
        ref[0, rows, :] = val


def _attn_fwd(q_arr, k_arr, v_arr, bias, sink, *, grid, seq, stride, kvw, split, q_spec, k_spec, v_spec, bias_map,
              sink_map, o_spec, has_sink, o_shape, o_dtype, name, rider=None):
    nb = seq // stride // BLOCK
    scale = HEAD_DIM ** -0.5
    expanded = kvw != GW
    rps = min(stride, RESIDUES_PER_STEP)
    grid = (grid[0], grid[1] // rps)
    assert not has_sink or B_WINDOW - 1 < BLOCK

    def body(q_ref, k_ref, v_ref, bias_ref, sink_ref, o_ref, lse_ref, *kv_x):
        rr = pl.program_id(1)
        lane_head = _lane_head(BLOCK)
        if expanded:
            expand = _kv_expand_matrix(rr)
            kv_x[0][...] = _dot(k_ref[0], expand).astype(BF16)
            kv_x[1][...] = _dot(v_ref[0], expand).astype(BF16)
        for j in range(rps):
            residue(rr * rps + j if stride > 1 else 0, q_ref, k_ref, v_ref, bias_ref, sink_ref, o_ref, lse_ref, kv_x,
                    lane_head)

    def residue(row0, q_ref, k_ref, v_ref, bias_ref, sink_ref, o_ref, lse_ref, kv_x, lane_head):
        def per_head(fn, x):
            return jnp.concatenate([fn(sink_ref[0, h:h + 1, 0:1], x[h * BLOCK:(h + 1) * BLOCK]) for h in range(4)],
                                   axis=0)

        def load(ref, ib):
            return _load_rows(ref, _block_rows(row0, stride, ib), split).astype(BF16)

        def load_kv(which, ib):
            if expanded:
                return kv_x[which][_block_rows(0, 1, ib), :]
            return load((k_ref, v_ref)[which], ib)

        def block(ib, first):
            q4 = _stack_heads(load(q_ref, ib), lane_head)
            if first:
                kc, vc = load_kv(0, ib), load_kv(1, ib)
                b4 = bias_ref[:, :, BLOCK:].reshape(4 * BLOCK, BLOCK)
            else:
                kc = jnp.concatenate([load_kv(0, ib - 1), load_kv(0, ib)], axis=0)
                vc = jnp.concatenate([load_kv(1, ib - 1), load_kv(1, ib)], axis=0)
                b4 = bias_ref[...].reshape(4 * BLOCK, 2 * BLOCK)
                if has_sink:
                    oldest = lax.broadcasted_iota(jnp.int32, kc.shape, 0) == 0
                    kc = jnp.where(oldest, jnp.zeros_like(kc), kc)
                    vc = jnp.where(oldest, jnp.zeros_like(vc), vc)
            s = _dot_nt(q4, kc) * scale + b4
            m = jnp.max(s, axis=-1, keepdims=True)
            if has_sink and first:
                m = per_head(jnp.maximum, m)
            p = jnp.exp(s - m)
            l = jnp.sum(p, axis=-1, keepdims=True)
            if has_sink and first:
                l = l + per_head(lambda sk, mh: jnp.exp(sk - mh), m)
            o4 = _dot(p.astype(BF16), vc) / l
            rows = _block_rows(row0, stride, ib)
            _store_rows(o_ref, rows, _unstack_heads(o4, lane_head).astype(o_dtype), split)
            _store_rows(lse_ref, rows, _unstack_heads(m + jnp.log(l), lane_head), split)

        block(0, True)
        if nb > 1:
            def step(i, carry):
                block(i, False)
                return carry
            lax.fori_loop(1, nb, step, 0, unroll=min(ATTN_UNROLL, nb - 1))

    return _pallas(
        body, (q_arr, k_arr, v_arr, bias, sink), name=name, grid=grid,
        out_shape=(jax.ShapeDtypeStruct(o_shape, o_dtype), jax.ShapeDtypeStruct(o_shape, F32)),
        in_specs=[q_spec, k_spec, v_spec,
                  pl.BlockSpec((4, BLOCK, 2 * BLOCK), bias_map), pl.BlockSpec((1, 4, 128), sink_map)],
        out_specs=(o_spec, o_spec),
        scratch_shapes=[pltpu.VMEM((seq, GW), BF16)] * 2 if expanded else [],
        sem=("arbitrary", "arbitrary"), vmem=VMEM_BIG, rider=rider)


def _attn_bwd(q_arr, k_arr, v_arr, bias, sink, dy, cc, lse, *, grid, seq, stride, kvw, split, q_spec, k_spec, v_spec,
              bias_map, sink_map, o_spec, kv_out_spec, has_sink, n_bias, dq_shape, dkv_shape, g_dtype, name):
    ln = seq // stride
    nb = ln // BLOCK
    scale = HEAD_DIM ** -0.5
    expanded = kvw != GW
    rps = min(stride, RESIDUES_PER_STEP)
    grid = (grid[0], grid[1] // rps)

    def body(q_ref, k_ref, v_ref, bias_ref, sink_ref, dy_ref, c_ref, lse_ref,
             dq_ref, dk_ref, dv_ref, db_ref, dsk_ref, dk_acc, dv_acc, dk_half, dv_half, *kv_x):
        rr = pl.program_id(1)

        @pl.when((pl.program_id(0) == 0) & (rr == 0))
        def _():
            db_ref[...] = jnp.zeros_like(db_ref)
            dsk_ref[...] = jnp.zeros_like(dsk_ref)

        if expanded:
            expand = _kv_expand_matrix(rr)
            kv_x[0][...] = _dot(k_ref[0], expand).astype(BF16)
            kv_x[1][...] = _dot(v_ref[0], expand).astype(BF16)
        refs = (q_ref, k_ref, v_ref, bias_ref, sink_ref, dy_ref, c_ref, lse_ref, dq_ref, dk_ref, dv_ref, db_ref,
                dsk_ref, dk_acc, dv_acc, dk_half, dv_half, kv_x)
        for j in range(rps):
            residue(rr, rr * rps + j if stride > 1 else 0, *refs)

    def residue(rr, row0, q_ref, k_ref, v_ref, bias_ref, sink_ref, dy_ref, c_ref, lse_ref,
                dq_ref, dk_ref, dv_ref, db_ref, dsk_ref, dk_acc, dv_acc, dk_half, dv_half, kv_x):
        dk_acc[...] = jnp.zeros_like(dk_acc)
        dv_acc[...] = jnp.zeros_like(dv_acc)
        lane_head = _lane_head(BLOCK)
        hb = 4 * rr if n_bias == 8 else 0

        def load(ref, ib):
            return _load_rows(ref, _block_rows(row0, stride, ib), split)

        def load_kv(which, ib):
            if expanded:
                return kv_x[which][_block_rows(0, 1, ib), :]
            return load((k_ref, v_ref)[which], ib).astype(BF16)

        def head_col(x):
            return jnp.concatenate([x[:, h * HEAD_DIM:h * HEAD_DIM + 1] for h in range(4)], axis=0)

        def block(ib, first):
            q4 = _stack_heads(load(q_ref, ib).astype(BF16), lane_head)
            dy4 = _stack_heads(load(dy_ref, ib).astype(BF16), lane_head)
            c4 = head_col(load(c_ref, ib))
            l4 = head_col(load(lse_ref, ib))
            if first:
                kc, vc = load_kv(0, ib), load_kv(1, ib)
                b4 = bias_ref[:, :, BLOCK:].reshape(4 * BLOCK, BLOCK)
                krows = pl.ds(0, BLOCK)
            else:
                kc = jnp.concatenate([load_kv(0, ib - 1), load_kv(0, ib)], axis=0)
                vc = jnp.concatenate([load_kv(1, ib - 1), load_kv(1, ib)], axis=0)
                b4 = bias_ref[...].reshape(4 * BLOCK, 2 * BLOCK)
                krows = pl.ds(pl.multiple_of((ib - 1) * BLOCK, BLOCK), 2 * BLOCK)
            nk = BLOCK if first else 2 * BLOCK
            p = jnp.exp(_dot_nt(q4, kc) * scale + b4 - l4)
            ds = p * (_dot_nt(dy4, vc) - c4)
            ds3 = ds.reshape(4, BLOCK, nk)
            if n_bias == 8:
                if first:
                    db_ref[pl.ds(hb, 4), :, BLOCK:] += ds3
                else:
                    db_ref[pl.ds(hb, 4)] += ds3
            elif first:
                db_ref[:, :, BLOCK:] += ds3
            else:
                db_ref[...] += ds3
            ds16 = ds.astype(BF16)
            dq = _unstack_heads(_dot(ds16, kc), lane_head) * scale
            _store_rows(dq_ref, _block_rows(row0, stride, ib), dq.astype(g_dtype), split)
            dk_acc[krows, :] += _dot_tn(ds16, q4) * scale
            dv_acc[krows, :] += _dot_tn(p.astype(BF16), dy4)
            if has_sink:
                for h in range(4):
                    hs = slice(h * BLOCK, (h + 1) * BLOCK)
                    sk = sink_ref[0, h:h + 1, 0:1]
                    val = -jnp.sum(jnp.exp(sk - l4[hs]) * c4[hs], axis=0, keepdims=True)
                    dsk_ref[hb + h] += jnp.broadcast_to(val, (8, 128))

        block(0, True)
        if nb > 1:
            def step(i, carry):
                block(i, False)
                return carry
            lax.fori_loop(1, nb, step, 0, unroll=min(ATTN_UNROLL, nb - 1))

        if kvw == GW:
            all_rows = pl.ds(row0, ln, stride=stride) if stride > 1 else pl.ds(0, ln)
            _store_rows(dk_ref, all_rows, dk_acc[...].astype(g_dtype), split)
            _store_rows(dv_ref, all_rows, dv_acc[...].astype(g_dtype), split)
        else:
            def fold(acc):
                t2 = acc[:, :2 * HEAD_DIM] + acc[:, 2 * HEAD_DIM:]
                t2 = t2 + pltpu.roll(t2, HEAD_DIM, 1)
                lane = lax.broadcasted_iota(jnp.int32, t2.shape, 1) // HEAD_DIM
                return jnp.where(lane == rr, t2, 0.0)

            @pl.when(rr == 0)
            def _():
                dk_half[...] = fold(dk_acc[...])
                dv_half[...] = fold(dv_acc[...])

            @pl.when(rr == 1)
            def _():
                dk_ref[0] = (dk_half[...] + fold(dk_acc[...])).astype(g_dtype)
                dv_ref[0] = (dv_half[...] + fold(dv_acc[...])).astype(g_dtype)

    return pl.pallas_call(
        body, name=name, grid=grid,
        out_shape=(jax.ShapeDtypeStruct(dq_shape, g_dtype), jax.ShapeDtypeStruct(dkv_shape, g_dtype),
                   jax.ShapeDtypeStruct(dkv_shape, g_dtype), jax.ShapeDtypeStruct((n_bias, BLOCK, 2 * BLOCK), F32),
                   jax.ShapeDtypeStruct((8, 8, 128), F32)),
        in_specs=[q_spec, k_spec, v_spec,
                  pl.BlockSpec((4, BLOCK, 2 * BLOCK), bias_map), pl.BlockSpec((1, 4, 128), sink_map),
                  o_spec, o_spec, o_spec],
        out_specs=(o_spec, kv_out_spec, kv_out_spec,
                   pl.BlockSpec((n_bias, BLOCK, 2 * BLOCK), lambda n, r: (0, 0, 0)),
                   pl.BlockSpec((8, 8, 128), lambda n, r: (0, 0, 0))),
        scratch_shapes=[pltpu.VMEM((ln, GW), F32), pltpu.VMEM((ln, GW), F32),
                        pltpu.VMEM((ln, 2 * HEAD_DIM), F32), pltpu.VMEM((ln, 2 * HEAD_DIM), F32)]
        + ([pltpu.VMEM((seq, GW), BF16)] * 2 if expanded else []),
        compiler_params=_params(("arbitrary", "arbitrary"), VMEM_BIG),
    )(q_arr, k_arr, v_arr, bias, sink, dy, cc, lse)


def _bias_grad(ds_all, buckets):
    def body(ds_ref, bk_ref, o_ref):
        rows = lax.broadcasted_iota(jnp.int32, (N_BUCKETS, 128), 0)
        cols = lax.broadcasted_iota(jnp.int32, (N_BUCKETS, 128), 1)

        def per_bucket(b, acc):
            for h in range(20):
                gi = h // 4 if h < 12 else 3
                v = jnp.where(bk_ref[gi] == b, ds_ref[h], 0.0)
                v = jnp.sum(jnp.sum(v, axis=1, keepdims=True), axis=0, keepdims=True)
                acc = jnp.where((rows == b) & (cols == h), v, acc)
            return acc

        o_ref[...] = lax.fori_loop(0, N_BUCKETS, per_bucket, jnp.zeros((N_BUCKETS, 128), F32))

    vm = pl.BlockSpec(memory_space=pltpu.VMEM)
    return pl.pallas_call(body, name="bias_grad", out_shape=jax.ShapeDtypeStruct((N_BUCKETS, 128), F32),
                          in_specs=[vm, vm], out_specs=vm)(ds_all, buckets)


def _adamw(w, g, m, v, name):
    (res,), _ = _adamw_many([(w, g, m, v)], name)
    return res


def _adamw_many(tensors, name, rider=None):
    n = len(tensors)
    r, c = tensors[0][0].shape
    tr = r
    for cand in (256, 176, 128, 88, 64, 32, 16, 8):
        if r % cand == 0 and cand * c * 4 * 7 * n * 2 <= 24 * 1024 * 1024:
            tr = cand
            break

    def body(*refs):
        ins, outs = refs[:4 * n], refs[4 * n:]
        for i in range(n):
            w_ref, g_ref, m_ref, v_ref = ins[4 * i:4 * i + 4]
            d, nm, nv = _adam_update(w_ref[...], g_ref[...], m_ref[...], v_ref[...])
            outs[3 * i][...], outs[3 * i + 1][...], outs[3 * i + 2][...] = d, nm, nv

    spec = pl.BlockSpec((tr, c), lambda i: (i, 0))
    shp = jax.ShapeDtypeStruct((r, c), F32)
    res, ro = _pallas(body, tuple(a for t4 in tensors for a in t4), name=name, grid=(r // tr,),
                      out_shape=(shp,) * (3 * n), in_specs=[spec] * (4 * n), out_specs=(spec,) * (3 * n),
                      sem=("parallel",), vmem=VMEM_BIG, rider=rider)
    return [tuple(res[3 * i:3 * i + 3]) for i in range(n)], ro


def _t5_bucket(dist):
    max_exact = N_BUCKETS // 2
    n = jnp.maximum(dist, 0)
    nf = jnp.maximum(n, 1).astype(F32)
    large = max_exact + (jnp.log(nf / max_exact) / math.log(MAX_DISTANCE / max_exact)
                         * (N_BUCKETS - max_exact)).astype(jnp.int32)
    large = jnp.minimum(large, N_BUCKETS - 1)
    return jnp.where(n < max_exact, n, large)


def _bias_tables(rel_bias):
    qi = jnp.arange(BLOCK)[:, None]
    ki = jnp.arange(2 * BLOCK)[None, :]
    dist = qi + BLOCK - ki
    specs = [(d, w // d, 4 * gi, 4 * gi + 4) for gi, (w, d) in enumerate(DIL_GROUPS)] + [(1, B_WINDOW - 1, 12, 20)]
    biases, buckets = [], []
    for stride, steps, h0, h1 in specs:
        valid = (dist >= 0) & (dist <= steps)
        bk = jnp.where(valid, _t5_bucket(dist * stride), -1).astype(jnp.int32)
        onehot = (bk[None, :, :] == jnp.arange(N_BUCKETS, dtype=jnp.int32)[:, None, None]).astype(F32)
        b = jnp.einsum("bqk,bh->hqk", onehot, rel_bias[:, h0:h1], precision=lax.Precision.HIGHEST)
        biases.append(jnp.where(valid[None], b, NEG))
        buckets.append(bk)
    return jnp.concatenate(biases, axis=0), jnp.stack(buckets, axis=0)


def _local_step(x, tgt, W, S, shards=None, tail_host=None):
    nseq, seq, _ = x.shape
    t = nseq * seq
    xf = x.reshape(t, D_MODEL)
    bias_all, buckets = _bias_tables(S["rel_bias"])
    sink_b = jnp.broadcast_to(S["sinks"].reshape(2, 4, 1), (2, 4, 128)).astype(F32)
    sink_0 = jnp.zeros((1, 4, 128), F32)
    dist = shards is not None
    W = dict(W)
    G, GS, reduced = {}, {}, {}

    def put(keys, gathered):
        for k, g in zip(keys, gathered):
            W[k] = g.reshape(_FULL_SHAPE.get(k, (N_CHIPS * shards[k].shape[0], D_MODEL)))

    def gather_rider(keys):
        return _GatherRider([shards[k] for k in keys]) if dist else None

    def pair(keys):
        return _pair_reduce([G[k].reshape(N_CHIPS, 2, shards[k].shape[0] // 2, D_MODEL) for k in keys],
                            "grad_pair_reduce_" + keys[0])

    def finish(keys, own, rec):
        full = _final_reduce(own, rec, "grad_final_reduce_" + keys[0])
        off = 0
        for k in keys:
            r = shards[k].shape[0]
            reduced[k] = full[:, off:off + r // 2].reshape(r, D_MODEL)
            off += r // 2

    if dist:
        first = ("wgt1", "wut1", "wd1")
        put(first, _gather_rows([shards[k] for k in first]))
    keys = ("wint",)
    (h1, n1, g1, u1, a1), ro = _ffn_fwd(xf, S["ffn1_norm"], W["wgt1"], W["wut1"], W["wd1"], rider=gather_rider(keys))
    put(keys, ro)
    keys = ("wout", "wat", "wbt", "wgt2")
    (un, za, zb, zg), ro = _inproj_fwd(h1, S["mix_norm"], W["wint"], S["b_in"], nseq, rider=gather_rider(keys))
    put(keys, ro)

    seq3 = lambda a: a.reshape(nseq, seq, a.shape[-1])
    zb3 = seq3(zb)
    pair_blk = lambda cb: pl.BlockSpec((1, 2, seq, 128), lambda n, r, cb=cb: (n, cb, 0, 0))
    a_cfg = []
    outs, lses = [], []
    for gi, (_, d) in enumerate(DIL_GROUPS):
        cfg = dict(grid=(nseq, d), seq=seq, stride=d, kvw=GW, split=True,
                   q_spec=pair_blk(gi), k_spec=pair_blk(3 + gi), v_spec=pair_blk(6 + gi), o_spec=pair_blk(0),
                   bias_map=lambda n, r: (0, 0, 0), sink_map=lambda n, r: (0, 0, 0), has_sink=False)
        a_cfg.append(cfg)
        (o, lse), _ = _attn_fwd(za, za, za, bias_all[4 * gi:4 * gi + 4], sink_0, o_shape=(nseq, 2, seq, 128),
                                o_dtype=F32, name=f"attn_a{gi}_fwd", **cfg)
        outs.append(o)
        lses.append(lse)
    wide_blk = lambda w, cmap: pl.BlockSpec((1, seq, w), cmap)
    b_cfg = dict(grid=(nseq, 2), seq=seq, stride=1, kvw=2 * HEAD_DIM, split=False,
                 q_spec=wide_blk(GW, lambda n, r: (n, 0, r)), k_spec=wide_blk(2 * HEAD_DIM, lambda n, r: (n, 0, 4)),
                 v_spec=wide_blk(2 * HEAD_DIM, lambda n, r: (n, 0, 5)), o_spec=wide_blk(GW, lambda n, r: (n, 0, r)),
                 bias_map=lambda n, r: (r, 0, 0), sink_map=lambda n, r: (r, 0, 0), has_sink=True)
    keys = ("wut2",)
    bias_b_fwd = bias_all[12:20].at[:, :, 0].set(jnp.broadcast_to(S["sinks"].reshape(8, 1), (8, BLOCK)))
    (yb, lse_b), ro = _attn_fwd(zb3, zb3, zb3, bias_b_fwd, sink_b, o_shape=(nseq, seq, 2 * GW), o_dtype=BF16,
                                name="attn_b_fwd", rider=gather_rider(keys), **b_cfg)
    put(keys, ro)
    yb = yb.reshape(t, 2 * GW)

    keys = ("wd2",)
    (h2, y, lse_tot, pa, pb, merged), ro = _merge_fwd(outs[0], outs[1], outs[2], lses[0], lses[1], lses[2], yb, zg, h1,
                                                      W["wat"], W["wbt"], W["wout"], rider=gather_rider(keys))
    put(keys, ro)
    (dh3, n2, g2, u2, a2, loss_part, g_final), _ = _ffn_fwd(
        h2, S["ffn2_norm"], W["wgt2"], W["wut2"], W["wd2"],
        head=(S["final_norm"].reshape(1, D_MODEL), tgt.reshape(t, D_MODEL)))

    GS["final_norm"] = g_final
    dh2, dg2, du2, df2, GS["ffn2_norm"] = _ffn_bwd(dh3, h2, S["ffn2_norm"], g2, u2, W["wgt2"], W["wut2"], W["wd2"])
    G["wgt2"] = _wgrad(dg2, n2, MXU_DIM, name="wgrad_gate2")
    G["wut2"] = _wgrad(du2, n2, MXU_DIM, name="wgrad_up2")
    G["wd2"] = _wgrad(a2, df2, MXU_DIM, name="wgrad_down2")

    keys = ("wgt2", "wut2", "wd2")
    rider = _ExchangeRider([pair(keys)]) if dist else None
    (dpa, dpb, dga, dgb, dya, dyb, dh2b, ca, cb), ro = _merge_bwd(dh2, pa, pb, zg, y, yb, W["wat"], W["wbt"], W["wout"],
                                                                  nseq, rider=rider)
    if dist:
        finish(keys, *ro)

    dqs, dks, dvs, dbs = [], [], [], []
    shp = (nseq, 2, seq, 128)
    halves = lambda a: [a[:, hf].reshape(t, 128).astype(BF16) for hf in range(2)]
    for gi in range(len(DIL_GROUPS)):
        dq, dk, dv, db, _ = _attn_bwd(za, za, za, bias_all[4 * gi:4 * gi + 4], sink_0, dya, ca, lse_tot,
                                      n_bias=4, dq_shape=shp, dkv_shape=shp, g_dtype=F32,
                                      kv_out_spec=a_cfg[gi]["o_spec"], name=f"attn_a{gi}_bwd", **a_cfg[gi])
        dqs += halves(dq)
        dks += halves(dk)
        dvs += halves(dv)
        dbs.append(db)
    dqb, dkb, dvb, dbb, dsink = _attn_bwd(zb3, zb3, zb3, bias_all[12:20], sink_b, seq3(dyb), seq3(cb), lse_b,
                                          n_bias=8, dq_shape=(nseq, seq, 2 * GW),
                                          dkv_shape=(nseq, seq, 2 * HEAD_DIM), g_dtype=BF16,
                                          kv_out_spec=wide_blk(2 * HEAD_DIM, lambda n, r: (n, 0, 0)),
                                          name="attn_b_bwd", **b_cfg)
    dz = jnp.concatenate(dqs + dks + dvs + [dqb.reshape(t, 2 * GW), dkb.reshape(t, 2 * HEAD_DIM),
                                            dvb.reshape(t, 2 * HEAD_DIM), dga, dgb], axis=-1)
    gb_tab = _bias_grad(jnp.concatenate(dbs + [dbb], axis=0), buckets)
    if dist:
        GS["bias_tab"], GS["sink_tiles"] = gb_tab, dsink
    else:
        GS["rel_bias"] = gb_tab[:, :20]
        GS["sinks"] = dsink[:, 0, 0].reshape(1, 8)

    G["wint"], GS["b_in"] = _wgrad(dz, un, MXU_DIM, with_colsum=True, name="wgrad_in")
    G["wout"] = _wgrad(merged, dh2b, MXU_DIM, name="wgrad_out")
    G["wat"] = _wgrad(dpa, y, MXU_DIM, name="wgrad_branch_a")
    G["wbt"] = _wgrad(dpb, yb, MXU_DIM, name="wgrad_branch_b")
    keys = ("wint", "wout", "wat", "wbt")
    rider = _ExchangeRider([pair(keys)]) if dist else None
    (dh1, GS["mix_norm"]), ro = _inproj_bwd(dz, dh2, h1, S["mix_norm"], W["wint"], rider=rider)
    if dist:
        finish(keys, *ro)

    dx, dg1, du1, df1, GS["ffn1_norm"] = _ffn_bwd(dh1, xf, S["ffn1_norm"], g1, u1, W["wgt1"], W["wut1"], W["wd1"])
    G["wgt1"] = _wgrad(dg1, n1, MXU_DIM, name="wgrad_gate1")
    if dist:
        G["wut1"], ro = _wgrad(du1, n1, MXU_DIM, name="wgrad_up1", rider=_ExchangeRider([pair(("wgt1",))]))
        finish(("wgt1",), *ro)
        G["wd1"], ro = _wgrad(a1, df1, MXU_DIM, name="wgrad_down1", rider=_ExchangeRider([pair(("wut1",))]))
        finish(("wut1",), *ro)
        finish(("wd1",), *tail_host(_ExchangeRider([pair(("wd1",))]), reduced))
    else:
        G["wut1"] = _wgrad(du1, n1, MXU_DIM, name="wgrad_up1")
        G["wd1"] = _wgrad(a1, df1, MXU_DIM, name="wgrad_down1")
    return loss_part, dx.reshape(x.shape), (reduced if dist else G), GS


_SMALL = ("ffn1_norm", "mix_norm", "ffn2_norm", "final_norm", "b_in", "sinks", "rel_bias")
_ORDER = ("ffn1_norm", "ffn1_w_gate", "ffn1_w_up", "ffn1_w_down", "mix_norm", "w_in", "b_in", "w_branch_a",
          "w_branch_b", "w_out", "sinks", "rel_bias", "ffn2_norm", "ffn2_w_gate", "ffn2_w_up", "ffn2_w_down",
          "final_norm")
_BIG = (("wgt1", "ffn1_w_gate", True, 704), ("wut1", "ffn1_w_up", True, 704), ("wd1", "ffn1_w_down", False, 704),
        ("wint", "w_in", True, 1280), ("wout", "w_out", False, 256), ("wat", "w_branch_a", True, 64),
        ("wbt", "w_branch_b", True, 128), ("wgt2", "ffn2_w_gate", True, 704), ("wut2", "ffn2_w_up", True, 704),
        ("wd2", "ffn2_w_down", False, 704))
_FULL_SHAPE = {"wat": (D_MODEL, GW), "wbt": (D_MODEL, 2 * GW)}


def kernel(x, ffn1_norm, ffn1_w_gate, ffn1_w_up, ffn1_w_down, mix_norm, w_in, b_in, w_branch_a, w_branch_b, w_out, sinks, rel_bias, ffn2_norm, ffn2_w_gate, ffn2_w_up, ffn2_w_down, final_norm, loss_target, m_ffn1_norm, m_ffn1_w_gate, m_ffn1_w_up, m_ffn1_w_down, m_mix_norm, m_w_in, m_b_in, m_w_branch_a, m_w_branch_b, m_w_out, m_sinks, m_rel_bias, m_ffn2_norm, m_ffn2_w_gate, m_ffn2_w_up, m_ffn2_w_down, m_final_norm, v_ffn1_norm, v_ffn1_w_gate, v_ffn1_w_up, v_ffn1_w_down, v_mix_norm, v_w_in, v_b_in, v_w_branch_a, v_w_branch_b, v_w_out, v_sinks, v_rel_bias, v_ffn2_norm, v_ffn2_w_gate, v_ffn2_w_up, v_ffn2_w_down, v_final_norm):
    args = dict(locals())
    w = {n: args[n] for n in _ORDER}
    m = {n: args["m_" + n] for n in _ORDER}
    v = {n: args["v_" + n] for n in _ORDER}

    shards = {}
    for key, name, transposed, rows in _BIG:
        a = w[name][0]
        a = (a.T if transposed else a).astype(BF16)
        shards[key] = a.reshape(rows, D_MODEL)
    S = {n: w[n] for n in _SMALL}

    row_adam = lambda n: (w[n][0].T, m[n][0].T, v[n][0].T)
    early = {}

    def tail_host(rider, reduced):
        tensors = []
        for key, n in (("wgt2", "ffn2_w_gate"), ("wut2", "ffn2_w_up"), ("wd2", "ffn2_w_down")):
            wmv = row_adam(n) if key != "wd2" else (w[n][0], m[n][0], v[n][0])
            tensors.append((wmv[0], reduced[key], wmv[1], wmv[2]))
        res, ro = _adamw_many(tensors, "adamw_ffn2", rider=rider)
        early["ffn2_w_gate"], early["ffn2_w_up"], early["ffn2_w_down"] = res
        return ro

    loss_part, grad_x, reduced, GS = _local_step(x, loss_target, {}, S, shards, tail_host)

    small = _allreduce_small(GS["ffn1_norm"], GS["mix_norm"], GS["ffn2_norm"], GS["final_norm"], GS["b_in"],
                             GS["sink_tiles"], GS["bias_tab"], loss_part)
    loss = small[9, 8]

    out_g, out_d, out_m, out_v = {}, {}, {}, {}
    for key, n, transposed, rows in _BIG:
        nat = w[n][0].shape
        if transposed and nat[1] % 128:
            res = early[n] if n in early else _adamw(row_adam(n)[0], reduced[key], *row_adam(n)[1:], "adamw_" + n)
            res = [reduced[key].T] + [r.T for r in res]
        elif n in early:
            res = [reduced[key]] + list(early[n])
        else:
            g = reduced[key].reshape(nat[1], nat[0]).T if transposed else reduced[key].reshape(nat)
            res = [g] + list(_adamw(w[n][0], g, m[n][0], v[n][0], "adamw_" + n))
        out_g[n], out_d[n], out_m[n], out_v[n] = [r[None] for r in res]
    row = lambda d: {n: (d[n].reshape(1, D_MODEL) if n == "final_norm" else d[n]) for n in _SMALL}
    for dst, src in zip((out_g, out_d, out_m, out_v), _adamw_small(small, row(w), row(m), row(v))):
        dst.update(src)
        dst["final_norm"] = src["final_norm"].reshape(D_MODEL)

    return (loss, grad_x, *[out_g[n] for n in _ORDER], *[out_d[n] for n in _ORDER],
            *[out_m[n] for n in _ORDER], *[out_v[n] for n in _ORDER])
```

```python
import math

import jax
import jax.numpy as jnp
from jax import lax
from jax.experimental import pallas as pl
from jax.experimental.pallas import tpu as pltpu

F32, BF16 = jnp.float32, jnp.bfloat16
MESH = pl.DeviceIdType.MESH

D_MODEL = 1024
D_FF = 2816
D_IN = 5120
HEAD_DIM = 64
BLOCK = 128
DIL_GROUPS = ((128, 1), (512, 4), (2048, 16))
B_WINDOW = 128
N_BUCKETS = 32
MAX_DISTANCE = 2048
EPS = 1e-6
N_CHIPS = 4
GW = 256
ZA_W = 2304
ZB_W = 768
NEG = -1e30

ADAM_LR, ADAM_B1, ADAM_B2, ADAM_EPS, ADAM_WD, ADAM_STEP = 0.001, 0.9, 0.999, 1e-08, 0.01, 10

VMEM_BIG = 56 * 1024 * 1024
TM = 512
TM_BWD = 256
MXU_DIM = 256
FF_BOUNDS = (0, 4 * MXU_DIM, 8 * MXU_DIM, D_FF)
DMA_SPLIT = 8
RESIDUES_PER_STEP = 16
ATTN_UNROLL = 15


def _dot(a, b):
    return jnp.dot(a, b, preferred_element_type=F32)


def _dot_nt(a, b):
    return lax.dot_general(a, b, (((1,), (1,)), ((), ())), preferred_element_type=F32)


def _dot_tn(a, b):
    return lax.dot_general(a, b, (((0,), (0,)), ((), ())), preferred_element_type=F32)


def _sigmoid(x):
    return 0.5 * jnp.tanh(0.5 * x) + 0.5


def _params(sem, vmem=None):
    return pltpu.CompilerParams(dimension_semantics=sem, vmem_limit_bytes=vmem)


ANY = pl.BlockSpec(memory_space=pl.ANY)


def _me():
    return lax.axis_index("x"), lax.axis_index("y"), lax.axis_index("c")


_CHIP_RELS = ((1, 0), (0, 1), (1, 1))


def _flip(v, f):
    return 1 - v if f else v


def _remote(src, dst, ssem, rsem, peer):
    return pltpu.make_async_remote_copy(src_ref=src, dst_ref=dst, send_sem=ssem, recv_sem=rsem,
                                        device_id=peer, device_id_type=MESH)


def _row_pieces(rows, n):
    step = max(16, -(-rows // n) // 16 * 16)
    out, s = [], 0
    while s < rows:
        out.append((s, min(step, rows - s)))
        s += step
    return out


def _gather_rows(shards):
    nt = len(shards)
    rows = [s.shape[0] for s in shards]

    def body(*refs):
        srcs, outs = refs[:nt], refs[nt:2 * nt]
        halves, quarters = refs[2 * nt:3 * nt], refs[3 * nt:4 * nt]
        ici_s, ici_r, fwd_s, fwd_r, d2d_s, d2d_r, keep, loc = refs[4 * nt:]
        x, y, c = _me()
        j = 2 * x + y
        sib = (x, y, 1 - c)
        nbr = ((1 - x, y, c), (x, 1 - y, c))
        nbr_j = (2 * (1 - x) + y, 2 * x + (1 - y))
        diag_j = 2 * (1 - x) + (1 - y)
        local = [pltpu.make_async_copy(srcs[t], outs[t].at[j], loc.at[t]) for t in range(nt)]
        for cp in local:
            cp.start()
        pending = []
        for a in range(2):
            for t in range(nt):
                half = pl.ds(c * (rows[t] // 2), rows[t] // 2)
                cp = _remote(srcs[t].at[half], halves[t].at[a], ici_s.at[2 * t + a], ici_r.at[2 * t + a], nbr[a])
                cp.start()
                pending.append(cp)
        placed = []

        def place(src, dst_of, idx):
            mine = pltpu.make_async_copy(src, dst_of, keep.at[idx])
            mine.start()
            cp = _remote(src, dst_of, d2d_s.at[idx], d2d_r.at[idx], sib)
            cp.start()
            placed.append((mine, cp))

        for a in range(2):
            for t in range(nt):
                r2, r4 = rows[t] // 2, rows[t] // 4
                got = halves[t].at[a]
                _remote(got, got, ici_s.at[2 * t + a], ici_r.at[2 * t + a], nbr[a]).wait_recv()
                cp = _remote(halves[t].at[a, pl.ds(a * r4, r4)], quarters[t].at[a], fwd_s.at[2 * t + a],
                             fwd_r.at[2 * t + a], nbr[1 - a])
                cp.start()
                pending.append(cp)
                place(got, outs[t].at[nbr_j[a], pl.ds(c * r2, r2)], 4 * t + a)
        for a in range(2):
            for t in range(nt):
                r2, r4 = rows[t] // 2, rows[t] // 4
                got = quarters[t].at[a]
                _remote(got, got, fwd_s.at[2 * t + a], fwd_r.at[2 * t + a], nbr[1 - a]).wait_recv()
                place(got, outs[t].at[diag_j, pl.ds(c * r2 + a * r4, r4)], 4 * t + 2 + a)
        for mine, cp in placed:
            mine.wait()
            cp.wait()
        for cp in pending:
            cp.wait_send()
        for cp in local:
            cp.wait()

    stage = ([pltpu.VMEM((2, r // 2, D_MODEL), BF16) for r in rows] + [pltpu.VMEM((2, r // 4, D_MODEL), BF16) for r in rows])
    sems = ([pltpu.SemaphoreType.DMA((2 * nt,)) for _ in range(4)] + [pltpu.SemaphoreType.DMA((4 * nt,))] * 3
            + [pltpu.SemaphoreType.DMA((nt,))])
    return pl.pallas_call(
        body, name="gather_weights",
        out_shape=tuple(jax.ShapeDtypeStruct((N_CHIPS,) + s.shape, s.dtype) for s in shards),
        in_specs=[pl.BlockSpec(memory_space=pltpu.VMEM)] * nt,
        out_specs=tuple([ANY] * nt), scratch_shapes=stage + sems,
    )(*shards)


VMEM_WHOLE = pl.BlockSpec(memory_space=pltpu.VMEM)


def _pair_reduce(grads, name):
    nt = len(grads)
    r2 = [g.shape[2] for g in grads]
    off = [sum(r2[:t]) for t in range(nt)]
    tot = sum(r2)

    def body(*refs):
        gs = refs[:nt]
        s_ref, mine, got, ssem, rsem, lsem = refs[nt:]
        x, y, c = _me()
        sib = (x, y, 1 - c)
        for t in range(nt):
            for k in range(N_CHIPS):
                rows = pl.ds(off[t], r2[t])
                _remote(gs[t].at[k, 1 - c], got.at[k, rows], ssem, rsem, sib).start()
                pltpu.make_async_copy(gs[t].at[k, c], mine.at[k, rows], lsem).start()
        pltpu.make_async_copy(mine, mine, lsem).wait()
        _remote(got, got, ssem, rsem, sib).wait()
        for k in range(N_CHIPS):
            for st, sz in _row_pieces(tot, 4):
                rows = slice(st, st + sz)
                s_ref[k, rows, :] = (mine[k, rows, :].astype(F32) + got[k, rows, :].astype(F32)).astype(BF16)

    shp = jax.ShapeDtypeStruct((N_CHIPS, tot, D_MODEL), BF16)
    buf = pltpu.VMEM((N_CHIPS, tot, D_MODEL), BF16)
    return pl.pallas_call(
        body, name=name, out_shape=shp, in_specs=[ANY] * nt, out_specs=VMEM_WHOLE,
        scratch_shapes=[buf, buf, pltpu.SemaphoreType.DMA(()), pltpu.SemaphoreType.DMA(()),
                        pltpu.SemaphoreType.DMA(())],
        compiler_params=pltpu.CompilerParams(vmem_limit_bytes=VMEM_BIG),
    )(*grads)


def _chip_exchange(parts):
    ng = len(parts)
    r2 = [p.shape[1] for p in parts]
    off = [sum(r2[:g]) for g in range(ng)]
    tot = sum(r2)

    def body(*refs):
        ps = refs[:ng]
        own_ref, rec_ref, ssems, rsems, lsem = refs[ng:]
        x, y, c = _me()
        j = 2 * x + y
        for g in range(ng):
            pltpu.make_async_copy(ps[g].at[j], own_ref.at[pl.ds(off[g], r2[g])], lsem).start()
        for k, (fx, fy) in enumerate(_CHIP_RELS):
            px, py = _flip(x, fx), _flip(y, fy)
            for g in range(ng):
                for st, sz in _row_pieces(r2[g], 2):
                    _remote(ps[g].at[2 * px + py, pl.ds(st, sz)], rec_ref.at[k, pl.ds(off[g] + st, sz)],
                            ssems.at[k], rsems.at[k], (px, py, c)).start()
        for k in range(3):
            _remote(rec_ref.at[k], rec_ref.at[k], ssems.at[k], rsems.at[k], (x, y, c)).wait()
        pltpu.make_async_copy(own_ref, own_ref, lsem).wait()

    return pl.pallas_call(
        body, name="grad_chip_exchange",
        out_shape=(jax.ShapeDtypeStruct((tot, D_MODEL), BF16), jax.ShapeDtypeStruct((3, tot, D_MODEL), BF16)),
        in_specs=[VMEM_WHOLE] * ng, out_specs=(ANY, ANY),
        scratch_shapes=[pltpu.SemaphoreType.DMA((3,)), pltpu.SemaphoreType.DMA((3,)), pltpu.SemaphoreType.DMA(())],
    )(*parts)


def _final_reduce(own, rec, name):
    r2 = own.shape[0]
    stages = _row_pieces(r2, 2)

    def body(own_hbm, rec_hbm, o_ref, parts, fbuf, ssem, rsem, lsem, insems):
        x, y, c = _me()
        sib = (x, y, 1 - c)
        for p, (st, sz) in enumerate(stages):
            rows = pl.ds(st, sz)
            pltpu.make_async_copy(own_hbm.at[rows], parts.at[0, rows], insems.at[p]).start()
            for k in range(3):
                pltpu.make_async_copy(rec_hbm.at[k, rows], parts.at[1 + k, rows], insems.at[p]).start()
        for p, (st, sz) in enumerate(stages):
            stage = parts.at[:, pl.ds(st, sz)]
            pltpu.make_async_copy(stage, stage, insems.at[p]).wait()
            for s0, ssz in _row_pieces(sz, DMA_SPLIT // 2):
                rows = slice(st + s0, st + s0 + ssz)
                fbuf[rows, :] = (parts[0, rows, :].astype(F32) + parts[1, rows, :].astype(F32)
                                 + parts[2, rows, :].astype(F32) + parts[3, rows, :].astype(F32))
                dst = o_ref.at[c, pl.ds(st + s0, ssz)]
                pltpu.make_async_copy(fbuf.at[pl.ds(st + s0, ssz)], dst, lsem).start()
                _remote(fbuf.at[pl.ds(st + s0, ssz)], dst, ssem, rsem, sib).start()
        _remote(fbuf, o_ref.at[c], ssem, rsem, sib).wait()
        pltpu.make_async_copy(fbuf, o_ref.at[c], lsem).wait()

    return pl.pallas_call(
        body, name=name, out_shape=jax.ShapeDtypeStruct((2, r2, D_MODEL), F32),
        in_specs=[ANY, ANY], out_specs=ANY,
        scratch_shapes=[pltpu.VMEM((4, r2, D_MODEL), BF16), pltpu.VMEM((r2, D_MODEL), F32),
                        pltpu.SemaphoreType.DMA(()), pltpu.SemaphoreType.DMA(()), pltpu.SemaphoreType.DMA(()),
                        pltpu.SemaphoreType.DMA((2,))],
        compiler_params=pltpu.CompilerParams(vmem_limit_bytes=VMEM_BIG),
    )(own, rec)


SMALL_ROWS = 48


def _allreduce_small(g_ffn1, g_mix, g_ffn2, g_final, g_bin, dsink, bias_tab, loss_part):
    def body(f1_ref, mx_ref, f2_ref, fn_ref, bi_ref, sk_ref, bt_ref, ls_ref, o_ref, mine, buf, send_sems, recv_sems):
        x, y, c = _me()
        me = 4 * x + 2 * y + c
        mine[...] = jnp.zeros_like(mine)
        for r, ref in enumerate((f1_ref, mx_ref, f2_ref, fn_ref)):
            mine[r:r + 1, :] = ref[...]
        for k in range(D_IN // D_MODEL):
            mine[4 + k:5 + k, :] = bi_ref[:, k * D_MODEL:(k + 1) * D_MODEL]
        lane = lax.broadcasted_iota(jnp.int32, (1, 128), 1)
        row = jnp.where(lane == 8, ls_ref[0:1, :], 0.0)
        for h in range(8):
            row = jnp.where(lane == h, sk_ref[h, 0:1, :], row)
        mine[9:10, 0:128] = row
        mine[16:48, 0:128] = bt_ref[...]
        buf[me] = mine[...]
        copies = []
        for k in range(1, 8):
            peer = (_flip(x, (k >> 2) & 1), _flip(y, (k >> 1) & 1), _flip(c, k & 1))
            cp = _remote(mine, buf.at[me], send_sems.at[k - 1], recv_sems.at[k - 1], peer)
            cp.start()
            copies.append(cp)
        for cp in copies:
            cp.wait()
        acc = buf[0]
        for i in range(1, 8):
            acc = acc + buf[i]
        o_ref[...] = acc

    vm = pl.BlockSpec(memory_space=pltpu.VMEM)
    shape = (SMALL_ROWS, D_MODEL)
    return pl.pallas_call(
        body, name="allreduce_small", out_shape=jax.ShapeDtypeStruct(shape, F32),
        in_specs=[vm] * 8, out_specs=vm,
        scratch_shapes=[pltpu.VMEM(shape, F32), pltpu.VMEM((8,) + shape, F32), pltpu.SemaphoreType.DMA((7,)),
                        pltpu.SemaphoreType.DMA((7,))],
    )(g_ffn1, g_mix, g_ffn2, g_final, g_bin, dsink, bias_tab, loss_part)


def _adam_update(w, g, m, v):
    nm = ADAM_B1 * m + (1.0 - ADAM_B1) * g
    nv = ADAM_B2 * v + (1.0 - ADAM_B2) * (g * g)
    bc1 = 1.0 - ADAM_B1 ** ADAM_STEP
    bc2 = 1.0 - ADAM_B2 ** ADAM_STEP
    return -ADAM_LR * ((nm / bc1) / (jnp.sqrt(nv / bc2) + ADAM_EPS) + ADAM_WD * w), nm, nv


def _adamw_small(packed, w, m, v):
    names = ("ffn1_norm", "mix_norm", "ffn2_norm", "final_norm", "b_in", "sinks", "rel_bias")
    nn = len(names)

    def grad_of(p_ref, name, k=0):
        if name == "b_in":
            return p_ref[4 + k:5 + k, :]
        if name == "sinks":
            return p_ref[9:10, 0:8]
        if name == "rel_bias":
            return p_ref[16:48, 0:20]
        r = names.index(name)
        return p_ref[r:r + 1, :]

    def body(p_ref, *refs):
        ws, ms, vs = refs[:nn], refs[nn:2 * nn], refs[2 * nn:3 * nn]
        outs = refs[3 * nn:]
        for i, name in enumerate(names):
            og, od, om, ov = outs[i], outs[nn + i], outs[2 * nn + i], outs[3 * nn + i]
            pieces = range(D_IN // D_MODEL) if name == "b_in" else (0,)
            for k in pieces:
                sl = (slice(None), slice(k * D_MODEL, (k + 1) * D_MODEL)) if name == "b_in" else (Ellipsis,)
                g = grad_of(p_ref, name, k)
                d, nm, nv = _adam_update(ws[i][sl], g, ms[i][sl], vs[i][sl])
                og[sl], od[sl], om[sl], ov[sl] = g, d, nm, nv

    vm = pl.BlockSpec(memory_space=pltpu.VMEM)
    shapes = [jax.ShapeDtypeStruct(w[n].shape, F32) for n in names]
    res = pl.pallas_call(
        body, name="adamw_small", out_shape=tuple(shapes * 4), in_specs=[vm] * (1 + 3 * nn),
        out_specs=tuple([vm] * (4 * nn)),
    )(packed, *[w[n] for n in names], *[m[n] for n in names], *[v[n] for n in names])
    return [dict(zip(names, res[i * nn:(i + 1) * nn])) for i in range(4)]


class _GatherRider:
    def __init__(self, shards):
        self.inputs = list(shards)
        nt = len(shards)
        self.out_shape = [jax.ShapeDtypeStruct((N_CHIPS,) + s.shape, s.dtype) for s in shards]
        self.scratch = [pltpu.SemaphoreType.DMA((3 * nt,)), pltpu.SemaphoreType.DMA((3 * nt,)),
                        pltpu.SemaphoreType.DMA((nt,))]

    def _copies(self, srcs, outs, sems):
        ici_s, ici_r, loc = sems
        x, y, c = _me()
        j = 2 * x + y
        local = [pltpu.make_async_copy(srcs[t], outs[t].at[j], loc.at[t]) for t in range(len(srcs))]
        remote = []
        for k, (fx, fy) in enumerate(_CHIP_RELS):
            peer = (_flip(x, fx), _flip(y, fy), c)
            for t in range(len(srcs)):
                remote.append(_remote(srcs[t], outs[t].at[j], ici_s.at[3 * t + k], ici_r.at[3 * t + k], peer))
        return local, remote

    def start(self, srcs, outs, sems):
        local, remote = self._copies(srcs, outs, sems)
        for cp in local + remote:
            cp.start()

    def finish(self, srcs, outs, sems):
        local, remote = self._copies(srcs, outs, sems)
        for cp in remote + local:
            cp.wait()


class _ExchangeRider:
    def __init__(self, parts):
        self.inputs = list(parts)
        self.r2 = [p.shape[1] for p in parts]
        self.off = [sum(self.r2[:g]) for g in range(len(parts))]
        tot = sum(self.r2)
        self.out_shape = [jax.ShapeDtypeStruct((tot, D_MODEL), BF16), jax.ShapeDtypeStruct((3, tot, D_MODEL), BF16)]
        self.scratch = [pltpu.SemaphoreType.DMA((3,)), pltpu.SemaphoreType.DMA((3,)), pltpu.SemaphoreType.DMA(())]

    def start(self, ps, outs, sems):
        own_ref, rec_ref = outs
        ssems, rsems, lsem = sems
        x, y, c = _me()
        j = 2 * x + y
        for g in range(len(ps)):
            pltpu.make_async_copy(ps[g].at[j], own_ref.at[pl.ds(self.off[g], self.r2[g])], lsem).start()
        for k, (fx, fy) in enumerate(_CHIP_RELS):
            px, py = _flip(x, fx), _flip(y, fy)
            for g in range(len(ps)):
                for st, sz in _row_pieces(self.r2[g], 2):
                    _remote(ps[g].at[2 * px + py, pl.ds(st, sz)], rec_ref.at[k, pl.ds(self.off[g] + st, sz)],
                            ssems.at[k], rsems.at[k], (px, py, c)).start()

    def finish(self, ps, outs, sems):
        own_ref, rec_ref = outs
        ssems, rsems, lsem = sems
        x, y, c = _me()
        for k in range(3):
            _remote(rec_ref.at[k], rec_ref.at[k], ssems.at[k], rsems.at[k], (x, y, c)).wait()
        pltpu.make_async_copy(own_ref, own_ref, lsem).wait()


def _pallas(body, args, *, name, grid, in_specs, out_specs, out_shape, scratch_shapes=(), sem=None, vmem=None,
            rider=None):
    if rider is None:
        res = pl.pallas_call(body, name=name, grid=grid, in_specs=list(in_specs), out_specs=tuple(out_specs),
                             out_shape=tuple(out_shape), scratch_shapes=list(scratch_shapes),
                             compiler_params=_params(sem, vmem))(*args)
        return tuple(res), ()
    n_in, n_out, n_sc = len(in_specs), len(out_shape), len(scratch_shapes)
    r_in, r_out = len(rider.inputs), len(rider.out_shape)

    def wrapped(*refs):
        ins, rins = refs[:n_in], refs[n_in:n_in + r_in]
        p = n_in + r_in
        outs, routs = refs[p:p + n_out], refs[p + n_out:p + n_out + r_out]
        p += n_out + r_out
        scr, rsems = refs[p:p + n_sc], refs[p + n_sc:]
        first = pl.program_id(0) == 0
        last = pl.program_id(0) == grid[0] - 1
        for a in range(1, len(grid)):
            first = first & (pl.program_id(a) == 0)
            last = last & (pl.program_id(a) == grid[a] - 1)

        @pl.when(first)
        def _():
            rider.start(rins, routs, rsems)

        body(*ins, *outs, *scr)

        @pl.when(last)
        def _():
            rider.finish(rins, routs, rsems)

    res = pl.pallas_call(
        wrapped, name=name, grid=grid, in_specs=list(in_specs) + [ANY] * r_in,
        out_specs=tuple(out_specs) + (ANY,) * r_out, out_shape=tuple(out_shape) + tuple(rider.out_shape),
        scratch_shapes=list(scratch_shapes) + rider.scratch,
        compiler_params=_params(("arbitrary",) * len(grid), vmem))(*args, *rider.inputs)
    return tuple(res[:n_out]), tuple(res[n_out:])


def _load_weights(pairs, sems):
    copies = [pltpu.make_async_copy(hbm, vmem, sems.at[i]) for i, (hbm, vmem) in enumerate(pairs)]
    for cp in copies:
        cp.start()
    for cp in copies:
        cp.wait()


def _loss_tile(hh, gain, tgt):
    r = lax.rsqrt(jnp.mean(hh * hh, axis=-1, keepdims=True) + EPS)
    hn = hh * r
    err = hn * gain - tgt
    part = (0.5 / D_MODEL) * jnp.sum(jnp.sum(err * err, axis=1, keepdims=True), axis=0, keepdims=True)
    dy = err * (1.0 / D_MODEL)
    dng = dy * gain
    dh = r * (dng - hn * jnp.mean(dng * hn, axis=-1, keepdims=True))
    return dh, part, jnp.sum(dy * hn, axis=0, keepdims=True)


def _ffn_fwd(h, gain, wgt, wut, wd, rider=None, head=None):
    t = h.shape[0]

    def body(h_ref, gain_ref, wg_hbm, wu_hbm, wd_hbm, *rest):
        if head is None:
            hout_ref, n_ref, g_ref, u_ref, a_ref, wg_v, wu_v, wd_v, wsem = rest
        else:
            fg_ref, tgt_ref, hout_ref, n_ref, g_ref, u_ref, a_ref, loss_ref, gg_ref, wg_v, wu_v, wd_v, wsem = rest
        @pl.when(pl.program_id(0) == 0)
        def _():
            _load_weights(((wg_hbm, wg_v), (wu_hbm, wu_v), (wd_hbm, wd_v)), wsem)
            if head is not None:
                loss_ref[...] = jnp.zeros_like(loss_ref)
                gg_ref[...] = jnp.zeros_like(gg_ref)

        hh = h_ref[...]
        r = lax.rsqrt(jnp.mean(hh * hh, axis=-1, keepdims=True) + EPS)
        n = (hh * r * gain_ref[...]).astype(BF16)
        n_ref[...] = n
        acc = jnp.zeros((TM, D_MODEL), F32)
        for c0, c1 in zip(FF_BOUNDS[:-1], FF_BOUNDS[1:]):
            sl = slice(c0, c1)
            g = _dot_nt(n, wg_v[sl, :])
            u = _dot_nt(n, wu_v[sl, :])
            sg = _sigmoid(g)
            silu = g * sg
            a = (silu * u).astype(BF16)
            a_ref[:, sl] = a
            g_ref[:, sl] = (u * (sg * (1.0 + g * (1.0 - sg)))).astype(BF16)
            u_ref[:, sl] = silu.astype(BF16)
            acc = acc + _dot(a, wd_v[sl, :])
        hout = hh + 0.5 * acc
        if head is None:
            hout_ref[...] = hout
        else:
            dh, part, gpart = _loss_tile(hout, fg_ref[...], tgt_ref[...])
            hout_ref[...] = dh
            loss_ref[...] += part
            gg_ref[...] += gpart

    row = lambda w: pl.BlockSpec((TM, w), lambda i: (i, 0))
    vec = pl.BlockSpec((1, D_MODEL), lambda i: (0, 0))
    wv = pltpu.VMEM((D_FF, D_MODEL), BF16)
    args, in_specs = (h, gain, wgt, wut, wd), [row(D_MODEL), vec, ANY, ANY, ANY]
    out_shape = [jax.ShapeDtypeStruct((t, D_MODEL), F32), jax.ShapeDtypeStruct((t, D_MODEL), BF16)] + [
        jax.ShapeDtypeStruct((t, D_FF), BF16)] * 3
    out_specs = [row(D_MODEL), row(D_MODEL), row(D_FF), row(D_FF), row(D_FF)]
    if head is not None:
        args, in_specs = args + tuple(head), in_specs + [vec, row(D_MODEL)]
        out_shape += [jax.ShapeDtypeStruct((8, 128), F32), jax.ShapeDtypeStruct((1, D_MODEL), F32)]
        out_specs += [pl.BlockSpec((8, 128), lambda i: (0, 0)), vec]
    return _pallas(
        body, args, name="ffn_fwd", grid=(t // TM,), out_shape=tuple(out_shape), in_specs=in_specs,
        out_specs=tuple(out_specs), scratch_shapes=[wv, wv, wv, pltpu.SemaphoreType.DMA((3,))],
        sem=("arbitrary",), vmem=VMEM_BIG, rider=rider)


def _ffn_bwd(dhout, h, gain, dgf, duf, wgt, wut, wd):
    t = h.shape[0]
    tm = TM_BWD

    def body(dho_ref, h_ref, gain_ref, g_ref, u_ref, wg_hbm, wu_hbm, wd_hbm,
             dh_ref, dg_ref, du_ref, df_ref, gg_ref, wg_v, wu_v, wd_v, wsem):
        @pl.when(pl.program_id(0) == 0)
        def _():
            _load_weights(((wd_hbm, wd_v), (wg_hbm, wg_v), (wu_hbm, wu_v)), wsem)
            gg_ref[...] = jnp.zeros_like(gg_ref)

        dho = dho_ref[...]
        df = (0.5 * dho).astype(BF16)
        df_ref[...] = df
        dn = jnp.zeros((tm, D_MODEL), F32)
        for c0, c1 in zip(FF_BOUNDS[:-1], FF_BOUNDS[1:]):
            sl = slice(c0, c1)
            da = _dot_nt(df, wd_v[sl, :])
            dg = (da * g_ref[:, sl].astype(F32)).astype(BF16)
            du = (da * u_ref[:, sl].astype(F32)).astype(BF16)
            dg_ref[:, sl] = dg
            du_ref[:, sl] = du
            dn = dn + _dot(dg, wg_v[sl, :]) + _dot(du, wu_v[sl, :])
        hh = h_ref[...]
        r = lax.rsqrt(jnp.mean(hh * hh, axis=-1, keepdims=True) + EPS)
        hn = hh * r
        gg_ref[...] += jnp.sum(dn * hn, axis=0, keepdims=True)
        dng = dn * gain_ref[...]
        dh_ref[...] = dho + r * (dng - hn * jnp.mean(dng * hn, axis=-1, keepdims=True))

    row = lambda w: pl.BlockSpec((tm, w), lambda i: (i, 0))
    vec = pl.BlockSpec((1, D_MODEL), lambda i: (0, 0))
    wv = pltpu.VMEM((D_FF, D_MODEL), BF16)
    return pl.pallas_call(
        body, name="ffn_bwd", grid=(t // tm,),
        out_shape=(jax.ShapeDtypeStruct((t, D_MODEL), F32), jax.ShapeDtypeStruct((t, D_FF), BF16),
                   jax.ShapeDtypeStruct((t, D_FF), BF16),
                   jax.ShapeDtypeStruct((t, D_MODEL), BF16), jax.ShapeDtypeStruct((1, D_MODEL), F32)),
        in_specs=[row(D_MODEL), row(D_MODEL), vec, row(D_FF), row(D_FF), ANY, ANY, ANY],
        out_specs=(row(D_MODEL), row(D_FF), row(D_FF), row(D_MODEL), vec),
        scratch_shapes=[wv, wv, wv, pltpu.SemaphoreType.DMA((3,))],
        compiler_params=_params(("arbitrary",), VMEM_BIG),
    )(dhout, h, gain, dgf, duf, wgt, wut, wd)


def _wgrad(lhs, rhs, rb, with_colsum=False, name="wgrad", rider=None):
    t, k = lhs.shape
    n = rhs.shape[1]

    def body(l_ref, r_ref, o_ref, *rest):
        o_ref[...] = _dot_tn(l_ref[...], r_ref[...]).astype(BF16)
        if with_colsum:
            rest[0][...] = jnp.sum(l_ref[...].astype(F32), axis=0, keepdims=True)

    out_shape = [jax.ShapeDtypeStruct((k, n), BF16)]
    out_specs = [pl.BlockSpec((rb, n), lambda j: (j, 0))]
    if with_colsum:
        out_shape.append(jax.ShapeDtypeStruct((1, k), F32))
        out_specs.append(pl.BlockSpec((1, rb), lambda j: (0, j)))
    res, ro = _pallas(
        body, (lhs, rhs), name=name, grid=(k // rb,), out_shape=tuple(out_shape),
        in_specs=[pl.BlockSpec((t, rb), lambda j: (0, j)), pl.BlockSpec((t, n), lambda j: (0, 0))],
        out_specs=tuple(out_specs), sem=("arbitrary",), vmem=VMEM_BIG, rider=rider)
    if rider is not None:
        return res[0], ro
    return res if with_colsum else res[0]


def _lane_blocks(nseq, seq, nblk, tm=TM):
    spt = seq // tm
    return pl.BlockSpec((1, nblk, tm, 128), lambda i: (i // spt, 0, i % spt, 0))


def _inproj_fwd(h, gain, wint, b_in, nseq, rider=None):
    t = h.shape[0]
    seq = t // nseq
    cut_a = 5 * MXU_DIM
    pieces = ((0, cut_a, 0, 0), (cut_a, ZA_W - cut_a, 0, cut_a), (ZA_W, ZB_W, 1, 0), (ZA_W + ZB_W, 1024, 2, 0),
              (ZA_W + ZB_W + 1024, 1024, 2, 1024))

    def body(h_ref, gain_ref, w_hbm, b_ref, u_ref, za_ref, zb_ref, zg_ref, w_v):
        @pl.when(pl.program_id(0) == 0)
        def _():
            pltpu.sync_copy(w_hbm, w_v)

        hh = h_ref[...]
        r = lax.rsqrt(jnp.mean(hh * hh, axis=-1, keepdims=True) + EPS)
        un = (hh * r * gain_ref[...]).astype(BF16)
        u_ref[...] = un
        outs = (None, zb_ref, zg_ref)
        for c0, cw, oi, o0 in pieces:
            val = _dot_nt(un, w_v[c0:c0 + cw, :]) + b_ref[:, c0:c0 + cw]
            if oi == 0:
                for cb in range(cw // 128):
                    za_ref[0, o0 // 128 + cb] = val[:, cb * 128:(cb + 1) * 128]
            else:
                outs[oi][:, o0:o0 + cw] = val.astype(BF16)

    row = lambda w: pl.BlockSpec((TM, w), lambda i: (i, 0))
    return _pallas(
        body, (h, gain, wint, b_in), name="inproj_fwd", grid=(t // TM,),
        out_shape=(jax.ShapeDtypeStruct((t, D_MODEL), BF16), jax.ShapeDtypeStruct((nseq, ZA_W // 128, seq, 128), F32),
                   jax.ShapeDtypeStruct((t, ZB_W), BF16), jax.ShapeDtypeStruct((t, 2 * D_MODEL), BF16)),
        in_specs=[row(D_MODEL), pl.BlockSpec((1, D_MODEL), lambda i: (0, 0)), ANY,
                  pl.BlockSpec((1, D_IN), lambda i: (0, 0))],
        out_specs=(row(D_MODEL), _lane_blocks(nseq, seq, ZA_W // 128), row(ZB_W), row(2 * D_MODEL)),
        scratch_shapes=[pltpu.VMEM((D_IN, D_MODEL), BF16)], sem=("arbitrary",), vmem=VMEM_BIG, rider=rider)


def _inproj_bwd(dz, dh2, h, gain, wint, rider=None):
    t = h.shape[0]
    nc = 5
    cw = D_IN // nc

    def body(dz_ref, dh2_ref, h_ref, gain_ref, w_hbm, dh_ref, gg_ref, w_v):
        @pl.when(pl.program_id(0) == 0)
        def _():
            pltpu.sync_copy(w_hbm, w_v)
            gg_ref[...] = jnp.zeros_like(gg_ref)

        du = jnp.zeros((TM, D_MODEL), F32)
        for ci in range(nc):
            sl = slice(ci * cw, (ci + 1) * cw)
            du = du + _dot(dz_ref[:, sl], w_v[sl, :])
        hh = h_ref[...]
        r = lax.rsqrt(jnp.mean(hh * hh, axis=-1, keepdims=True) + EPS)
        hn = hh * r
        gg_ref[...] += jnp.sum(du * hn, axis=0, keepdims=True)
        dng = du * gain_ref[...]
        dh_ref[...] = dh2_ref[...] + r * (dng - hn * jnp.mean(dng * hn, axis=-1, keepdims=True))

    row = lambda w: pl.BlockSpec((TM, w), lambda i: (i, 0))
    vec = pl.BlockSpec((1, D_MODEL), lambda i: (0, 0))
    return _pallas(
        body, (dz, dh2, h, gain, wint), name="inproj_bwd", grid=(t // TM,),
        out_shape=(jax.ShapeDtypeStruct((t, D_MODEL), F32), jax.ShapeDtypeStruct((1, D_MODEL), F32)),
        in_specs=[row(D_IN), row(D_MODEL), row(D_MODEL), vec, ANY],
        out_specs=(row(D_MODEL), vec),
        scratch_shapes=[pltpu.VMEM((D_IN, D_MODEL), BF16)], sem=("arbitrary",), vmem=VMEM_BIG, rider=rider)


def _head_sums(x):
    w = x.shape[1]
    i = lax.broadcasted_iota(jnp.int32, (w, w), 0) // HEAD_DIM
    j = lax.broadcasted_iota(jnp.int32, (w, w), 1) // HEAD_DIM
    ones = (i == j).astype(BF16)
    hi = x.astype(BF16)
    r1 = x - hi.astype(F32)
    mid = r1.astype(BF16)
    lo = (r1 - mid.astype(F32)).astype(BF16)
    return _dot(hi, ones) + _dot(mid, ones) + _dot(lo, ones)


def _merge_fwd(o0, o1, o2, l0, l1, l2, yb, zg, h1, wat, wbt, wout, rider=None):
    t = h1.shape[0]
    nseq, _, seq, _ = o0.shape

    def body(o0_ref, o1_ref, o2_ref, l0_ref, l1_ref, l2_ref, yb_ref, ga_ref, gb_ref, h1_ref, wa_ref, wb_ref, wo_ref,
             h2_ref, y_ref, lt_ref, pa_ref, pb_ref, mg_ref):
        wide = lambda ref: jnp.concatenate([ref[0, 0], ref[0, 1]], axis=1)
        la, lb, lc = wide(l0_ref), wide(l1_ref), wide(l2_ref)
        mx = jnp.maximum(jnp.maximum(la, lb), lc)
        ea, eb, ec = jnp.exp(la - mx), jnp.exp(lb - mx), jnp.exp(lc - mx)
        den = ea + eb + ec
        y = (ea * wide(o0_ref) + eb * wide(o1_ref) + ec * wide(o2_ref)) / den
        lt = mx + jnp.log(den)
        lt_ref[0, 0] = lt[:, :128]
        lt_ref[0, 1] = lt[:, 128:]
        yb16 = y.astype(BF16)
        y_ref[...] = yb16
        pa = _dot_nt(yb16, wa_ref[...])
        pb = _dot_nt(yb_ref[...], wb_ref[...])
        pa_ref[...] = pa.astype(BF16)
        pb_ref[...] = pb.astype(BF16)
        mg = (_sigmoid(ga_ref[...].astype(F32)) * pa + _sigmoid(gb_ref[...].astype(F32)) * pb).astype(BF16)
        mg_ref[...] = mg
        h2_ref[...] = h1_ref[...] + _dot(mg, wo_ref[...])

    row = lambda w: pl.BlockSpec((TM, w), lambda i: (i, 0))
    full = lambda a: pl.BlockSpec(a.shape, lambda i: (0, 0))
    gate = lambda cb: pl.BlockSpec((TM, D_MODEL), lambda i: (i, cb))
    return _pallas(
        body, (o0, o1, o2, l0, l1, l2, yb, zg, zg, h1, wat, wbt, wout), name="merge_fwd", grid=(t // TM,),
        out_shape=(jax.ShapeDtypeStruct((t, D_MODEL), F32), jax.ShapeDtypeStruct((t, GW), BF16),
                   jax.ShapeDtypeStruct((nseq, 2, seq, 128), F32), jax.ShapeDtypeStruct((t, D_MODEL), BF16),
                   jax.ShapeDtypeStruct((t, D_MODEL), BF16), jax.ShapeDtypeStruct((t, D_MODEL), BF16)),
        in_specs=[_lane_blocks(nseq, seq, 2)] * 6 + [row(2 * GW), gate(0), gate(1), row(D_MODEL), full(wat), full(wbt),
                                                     full(wout)],
        out_specs=(row(D_MODEL), row(GW), _lane_blocks(nseq, seq, 2), row(D_MODEL), row(D_MODEL), row(D_MODEL)),
        sem=("parallel",), vmem=VMEM_BIG, rider=rider)


def _merge_bwd(dh2, pa, pb, zg, y, yb, wat, wbt, wout, nseq, rider=None):
    t = dh2.shape[0]

    def body(dh2_ref, pa_ref, pb_ref, ga_ref, gb_ref, y_ref, yb_ref, wa_ref, wb_ref, wo_ref,
             dpa_ref, dpb_ref, dga_ref, dgb_ref, dya_ref, dyb_ref, dh2b_ref, ca_ref, cb_ref):
        d16 = dh2_ref[...].astype(BF16)
        dh2b_ref[...] = d16
        dm = _dot_nt(d16, wo_ref[...])
        sa = _sigmoid(ga_ref[...].astype(F32))
        sb = _sigmoid(gb_ref[...].astype(F32))
        dpa = (dm * sa).astype(BF16)
        dpb = (dm * sb).astype(BF16)
        dpa_ref[...] = dpa
        dpb_ref[...] = dpb
        dga_ref[...] = (dm * pa_ref[...].astype(F32) * sa * (1.0 - sa)).astype(BF16)
        dgb_ref[...] = (dm * pb_ref[...].astype(F32) * sb * (1.0 - sb)).astype(BF16)
        dya = _dot(dpa, wa_ref[...])
        dyb = _dot(dpb, wb_ref[...])
        dya_ref[0, 0] = dya[:, :128]
        dya_ref[0, 1] = dya[:, 128:]
        dyb_ref[...] = dyb.astype(BF16)
        ca = _head_sums(dya * y_ref[...].astype(F32))
        ca_ref[0, 0] = ca[:, :128]
        ca_ref[0, 1] = ca[:, 128:]
        cb_ref[...] = _head_sums(dyb * yb_ref[...].astype(F32))

    row = lambda w: pl.BlockSpec((TM, w), lambda i: (i, 0))
    full = lambda a: pl.BlockSpec(a.shape, lambda i: (0, 0))
    gate = lambda cb: pl.BlockSpec((TM, D_MODEL), lambda i: (i, cb))
    bf = lambda w: jax.ShapeDtypeStruct((t, w), BF16)
    lanes = jax.ShapeDtypeStruct((nseq, 2, t // nseq, 128), F32)
    lane_spec = _lane_blocks(nseq, t // nseq, 2)
    return _pallas(
        body, (dh2, pa, pb, zg, zg, y, yb, wat, wbt, wout), name="merge_bwd", grid=(t // TM,),
        out_shape=(bf(D_MODEL), bf(D_MODEL), bf(D_MODEL), bf(D_MODEL), lanes, bf(2 * GW), bf(D_MODEL),
                   lanes, jax.ShapeDtypeStruct((t, 2 * GW), F32)),
        in_specs=[row(D_MODEL), row(D_MODEL), row(D_MODEL), gate(0), gate(1), row(GW), row(2 * GW),
                  full(wat), full(wbt), full(wout)],
        out_specs=(row(D_MODEL), row(D_MODEL), row(D_MODEL), row(D_MODEL), lane_spec, row(2 * GW), row(D_MODEL),
                   lane_spec, row(2 * GW)),
        sem=("parallel",), vmem=VMEM_BIG, rider=rider)


def _lane_head(rows):
    return lax.broadcasted_iota(jnp.int32, (rows, GW), 1) // HEAD_DIM


def _kv_expand_matrix(r):
    ci = lax.broadcasted_iota(jnp.int32, (2 * HEAD_DIM, GW), 0)
    ji = lax.broadcasted_iota(jnp.int32, (2 * HEAD_DIM, GW), 1)
    return (ci == (ji % HEAD_DIM) + HEAD_DIM * r).astype(BF16)


def _block_rows(row0, stride, ib):
    start = row0 + (stride * BLOCK) * ib
    if stride > 1:
        return pl.ds(start, BLOCK, stride=stride)
    return pl.ds(pl.multiple_of(start, BLOCK), BLOCK)


def _stack_heads(x, lane_head):
    return jnp.concatenate([jnp.where(lane_head == h, x, jnp.zeros_like(x)) for h in range(4)], axis=0)


def _unstack_heads(x4, lane_head):
    out = jnp.zeros((BLOCK, GW), F32)
    for h in range(4):
        out = jnp.where(lane_head == h, x4[h * BLOCK:(h + 1) * BLOCK], out)
    return out


def _load_rows(ref, rows, split):
    if split:
        return jnp.concatenate([ref[0, 0, rows, :], ref[0, 1, rows, :]], axis=1)
    return ref[0, rows, :]


def _store_rows(ref, rows, val, split):
    if split:
        ref[0, 0, rows, :] = val[:, :128]
        ref[0, 1, rows, :] = val[:, 128:]
    else:
        ref[0, rows, :] = val


def _attn_fwd(q_arr, k_arr, v_arr, bias, sink, *, grid, seq, stride, kvw, split, q_spec, k_spec, v_spec, bias_map,
              sink_map, o_spec, has_sink, o_shape, o_dtype, name, rider=None):
    nb = seq // stride // BLOCK
    scale = HEAD_DIM ** -0.5
    expanded = kvw != GW
    rps = min(stride, RESIDUES_PER_STEP)
    grid = (grid[0], grid[1] // rps)
    assert not has_sink or B_WINDOW - 1 < BLOCK

    def body(q_ref, k_ref, v_ref, bias_ref, sink_ref, o_ref, lse_ref, *kv_x):
        rr = pl.program_id(1)
        lane_head = _lane_head(BLOCK)
        if expanded:
            expand = _kv_expand_matrix(rr)
            kv_x[0][...] = _dot(k_ref[0], expand).astype(BF16)
            kv_x[1][...] = _dot(v_ref[0], expand).astype(BF16)
        for j in range(rps):
            residue(rr * rps + j if stride > 1 else 0, q_ref, k_ref, v_ref, bias_ref, sink_ref, o_ref, lse_ref, kv_x,
                    lane_head)

    def residue(row0, q_ref, k_ref, v_ref, bias_ref, sink_ref, o_ref, lse_ref, kv_x, lane_head):
        def per_head(fn, x):
            return jnp.concatenate([fn(sink_ref[0, h:h + 1, 0:1], x[h * BLOCK:(h + 1) * BLOCK]) for h in range(4)],
                                   axis=0)

        def load(ref, ib):
            return _load_rows(ref, _block_rows(row0, stride, ib), split).astype(BF16)

        def load_kv(which, ib):
            if expanded:
                return kv_x[which][_block_rows(0, 1, ib), :]
            return load((k_ref, v_ref)[which], ib)

        def block(ib, first):
            q4 = _stack_heads(load(q_ref, ib), lane_head)
            if first:
                kc, vc = load_kv(0, ib), load_kv(1, ib)
                b4 = bias_ref[:, :, BLOCK:].reshape(4 * BLOCK, BLOCK)
            else:
                kc = jnp.concatenate([load_kv(0, ib - 1), load_kv(0, ib)], axis=0)
                vc = jnp.concatenate([load_kv(1, ib - 1), load_kv(1, ib)], axis=0)
                b4 = bias_ref[...].reshape(4 * BLOCK, 2 * BLOCK)
                if has_sink:
                    oldest = lax.broadcasted_iota(jnp.int32, kc.shape, 0) == 0
                    kc = jnp.where(oldest, jnp.zeros_like(kc), kc)
                    vc = jnp.where(oldest, jnp.zeros_like(vc), vc)
            s = _dot_nt(q4, kc) * scale + b4
            m = jnp.max(s, axis=-1, keepdims=True)
            if has_sink and first:
                m = per_head(jnp.maximum, m)
            p = jnp.exp(s - m)
            l = jnp.sum(p, axis=-1, keepdims=True)
            if has_sink and first:
                l = l + per_head(lambda sk, mh: jnp.exp(sk - mh), m)
            o4 = _dot(p.astype(BF16), vc) / l
            rows = _block_rows(row0, stride, ib)
            _store_rows(o_ref, rows, _unstack_heads(o4, lane_head).astype(o_dtype), split)
            _store_rows(lse_ref, rows, _unstack_heads(m + jnp.log(l), lane_head), split)

        block(0, True)
        if nb > 1:
            def step(i, carry):
                block(i, False)
                return carry
            lax.fori_loop(1, nb, step, 0, unroll=min(ATTN_UNROLL, nb - 1))

    return _pallas(
        body, (q_arr, k_arr, v_arr, bias, sink), name=name, grid=grid,
        out_shape=(jax.ShapeDtypeStruct(o_shape, o_dtype), jax.ShapeDtypeStruct(o_shape, F32)),
        in_specs=[q_spec, k_spec, v_spec,
                  pl.BlockSpec((4, BLOCK, 2 * BLOCK), bias_map), pl.BlockSpec((1, 4, 128), sink_map)],
        out_specs=(o_spec, o_spec),
        scratch_shapes=[pltpu.VMEM((seq, GW), BF16)] * 2 if expanded else [],
        sem=("arbitrary", "arbitrary"), vmem=VMEM_BIG, rider=rider)


def _attn_bwd(q_arr, k_arr, v_arr, bias, sink, dy, cc, lse, *, grid, seq, stride, kvw, split, q_spec, k_spec, v_spec,
              bias_map, sink_map, o_spec, kv_out_spec, has_sink, n_bias, dq_shape, dkv_shape, g_dtype, name):
    ln = seq // stride
    nb = ln // BLOCK
    scale = HEAD_DIM ** -0.5
    expanded = kvw != GW
    rps = min(stride, RESIDUES_PER_STEP)
    grid = (grid[0], grid[1] // rps)

    def body(q_ref, k_ref, v_ref, bias_ref, sink_ref, dy_ref, c_ref, lse_ref,
             dq_ref, dk_ref, dv_ref, db_ref, dsk_ref, dk_acc, dv_acc, dk_half, dv_half, *kv_x):
        rr = pl.program_id(1)

        @pl.when((pl.program_id(0) == 0) & (rr == 0))
        def _():
            db_ref[...] = jnp.zeros_like(db_ref)
            dsk_ref[...] = jnp.zeros_like(dsk_ref)

        if expanded:
            expand = _kv_expand_matrix(rr)
            kv_x[0][...] = _dot(k_ref[0], expand).astype(BF16)
            kv_x[1][...] = _dot(v_ref[0], expand).astype(BF16)
        refs = (q_ref, k_ref, v_ref, bias_ref, sink_ref, dy_ref, c_ref, lse_ref, dq_ref, dk_ref, dv_ref, db_ref,
                dsk_ref, dk_acc, dv_acc, dk_half, dv_half, kv_x)
        for j in range(rps):
            residue(rr, rr * rps + j if stride > 1 else 0, *refs)

    def residue(rr, row0, q_ref, k_ref, v_ref, bias_ref, sink_ref, dy_ref, c_ref, lse_ref,
                dq_ref, dk_ref, dv_ref, db_ref, dsk_ref, dk_acc, dv_acc, dk_half, dv_half, kv_x):
        dk_acc[...] = jnp.zeros_like(dk_acc)
        dv_acc[...] = jnp.zeros_like(dv_acc)
        lane_head = _lane_head(BLOCK)
        hb = 4 * rr if n_bias == 8 else 0

        def load(ref, ib):
            return _load_rows(ref, _block_rows(row0, stride, ib), split)

        def load_kv(which, ib):
            if expanded:
                return kv_x[which][_block_rows(0, 1, ib), :]
            return load((k_ref, v_ref)[which], ib).astype(BF16)

        def head_col(x):
            return jnp.concatenate([x[:, h * HEAD_DIM:h * HEAD_DIM + 1] for h in range(4)], axis=0)

        def block(ib, first):
            q4 = _stack_heads(load(q_ref, ib).astype(BF16), lane_head)
            dy4 = _stack_heads(load(dy_ref, ib).astype(BF16), lane_head)
            c4 = head_col(load(c_ref, ib))
            l4 = head_col(load(lse_ref, ib))
            if first:
                kc, vc = load_kv(0, ib), load_kv(1, ib)
                b4 = bias_ref[:, :, BLOCK:].reshape(4 * BLOCK, BLOCK)
                krows = pl.ds(0, BLOCK)
            else:
                kc = jnp.concatenate([load_kv(0, ib - 1), load_kv(0, ib)], axis=0)
                vc = jnp.concatenate([load_kv(1, ib - 1), load_kv(1, ib)], axis=0)
                b4 = bias_ref[...].reshape(4 * BLOCK, 2 * BLOCK)
                krows = pl.ds(pl.multiple_of((ib - 1) * BLOCK, BLOCK), 2 * BLOCK)
            nk = BLOCK if first else 2 * BLOCK
            p = jnp.exp(_dot_nt(q4, kc) * scale + b4 - l4)
            ds = p * (_dot_nt(dy4, vc) - c4)
            ds3 = ds.reshape(4, BLOCK, nk)
            if n_bias == 8:
                if first:
                    db_ref[pl.ds(hb, 4), :, BLOCK:] += ds3
                else:
                    db_ref[pl.ds(hb, 4)] += ds3
            elif first:
                db_ref[:, :, BLOCK:] += ds3
            else:
                db_ref[...] += ds3
            ds16 = ds.astype(BF16)
            dq = _unstack_heads(_dot(ds16, kc), lane_head) * scale
            _store_rows(dq_ref, _block_rows(row0, stride, ib), dq.astype(g_dtype), split)
            dk_acc[krows, :] += _dot_tn(ds16, q4) * scale
            dv_acc[krows, :] += _dot_tn(p.astype(BF16), dy4)
            if has_sink:
                for h in range(4):
                    hs = slice(h * BLOCK, (h + 1) * BLOCK)
                    sk = sink_ref[0, h:h + 1, 0:1]
                    val = -jnp.sum(jnp.exp(sk - l4[hs]) * c4[hs], axis=0, keepdims=True)
                    dsk_ref[hb + h] += jnp.broadcast_to(val, (8, 128))

        block(0, True)
        if nb > 1:
            def step(i, carry):
                block(i, False)
                return carry
            lax.fori_loop(1, nb, step, 0, unroll=min(ATTN_UNROLL, nb - 1))

        if kvw == GW:
            all_rows = pl.ds(row0, ln, stride=stride) if stride > 1 else pl.ds(0, ln)
            _store_rows(dk_ref, all_rows, dk_acc[...].astype(g_dtype), split)
            _store_rows(dv_ref, all_rows, dv_acc[...].astype(g_dtype), split)
        else:
            def fold(acc):
                t2 = acc[:, :2 * HEAD_DIM] + acc[:, 2 * HEAD_DIM:]
                t2 = t2 + pltpu.roll(t2, HEAD_DIM, 1)
                lane = lax.broadcasted_iota(jnp.int32, t2.shape, 1) // HEAD_DIM
                return jnp.where(lane == rr, t2, 0.0)

            @pl.when(rr == 0)
            def _():
                dk_half[...] = fold(dk_acc[...])
                dv_half[...] = fold(dv_acc[...])

            @pl.when(rr == 1)
            def _():
                dk_ref[0] = (dk_half[...] + fold(dk_acc[...])).astype(g_dtype)
                dv_ref[0] = (dv_half[...] + fold(dv_acc[...])).astype(g_dtype)

    return pl.pallas_call(
        body, name=name, grid=grid,
        out_shape=(jax.ShapeDtypeStruct(dq_shape, g_dtype), jax.ShapeDtypeStruct(dkv_shape, g_dtype),
                   jax.ShapeDtypeStruct(dkv_shape, g_dtype), jax.ShapeDtypeStruct((n_bias, BLOCK, 2 * BLOCK), F32),
                   jax.ShapeDtypeStruct((8, 8, 128), F32)),
        in_specs=[q_spec, k_spec, v_spec,
                  pl.BlockSpec((4, BLOCK, 2 * BLOCK), bias_map), pl.BlockSpec((1, 4, 128), sink_map),
                  o_spec, o_spec, o_spec],
        out_specs=(o_spec, kv_out_spec, kv_out_spec,
                   pl.BlockSpec((n_bias, BLOCK, 2 * BLOCK), lambda n, r: (0, 0, 0)),
                   pl.BlockSpec((8, 8, 128), lambda n, r: (0, 0, 0))),
        scratch_shapes=[pltpu.VMEM((ln, GW), F32), pltpu.VMEM((ln, GW), F32),
                        pltpu.VMEM((ln, 2 * HEAD_DIM), F32), pltpu.VMEM((ln, 2 * HEAD_DIM), F32)]
        + ([pltpu.VMEM((seq, GW), BF16)] * 2 if expanded else []),
        compiler_params=_params(("arbitrary", "arbitrary"), VMEM_BIG),
    )(q_arr, k_arr, v_arr, bias, sink, dy, cc, lse)


def _bias_grad(ds_all, buckets):
    def body(ds_ref, bk_ref, o_ref):
        rows = lax.broadcasted_iota(jnp.int32, (N_BUCKETS, 128), 0)
        cols = lax.broadcasted_iota(jnp.int32, (N_BUCKETS, 128), 1)

        def per_bucket(b, acc):
            for h in range(20):
                gi = h // 4 if h < 12 else 3
                v = jnp.where(bk_ref[gi] == b, ds_ref[h], 0.0)
                v = jnp.sum(jnp.sum(v, axis=1, keepdims=True), axis=0, keepdims=True)
                acc = jnp.where((rows == b) & (cols == h), v, acc)
            return acc

        o_ref[...] = lax.fori_loop(0, N_BUCKETS, per_bucket, jnp.zeros((N_BUCKETS, 128), F32))

    vm = pl.BlockSpec(memory_space=pltpu.VMEM)
    return pl.pallas_call(body, name="bias_grad", out_shape=jax.ShapeDtypeStruct((N_BUCKETS, 128), F32),
                          in_specs=[vm, vm], out_specs=vm)(ds_all, buckets)


def _adamw(w, g, m, v, name):
    (res,), _ = _adamw_many([(w, g, m, v)], name)
    return res


def _adamw_many(tensors, name, rider=None):
    n = len(tensors)
    r, c = tensors[0][0].shape
    tr = r
    for cand in (256, 176, 128, 88, 64, 32, 16, 8):
        if r % cand == 0 and cand * c * 4 * 7 * n * 2 <= 24 * 1024 * 1024:
            tr = cand
            break

    def body(*refs):
        ins, outs = refs[:4 * n], refs[4 * n:]
        for i in range(n):
            w_ref, g_ref, m_ref, v_ref = ins[4 * i:4 * i + 4]
            d, nm, nv = _adam_update(w_ref[...], g_ref[...], m_ref[...], v_ref[...])
            outs[3 * i][...], outs[3 * i + 1][...], outs[3 * i + 2][...] = d, nm, nv

    spec = pl.BlockSpec((tr, c), lambda i: (i, 0))
    shp = jax.ShapeDtypeStruct((r, c), F32)
    res, ro = _pallas(body, tuple(a for t4 in tensors for a in t4), name=name, grid=(r // tr,),
                      out_shape=(shp,) * (3 * n), in_specs=[spec] * (4 * n), out_specs=(spec,) * (3 * n),
                      sem=("parallel",), vmem=VMEM_BIG, rider=rider)
    return [tuple(res[3 * i:3 * i + 3]) for i in range(n)], ro


def _t5_bucket(dist):
    max_exact = N_BUCKETS // 2
    n = jnp.maximum(dist, 0)
    nf = jnp.maximum(n, 1).astype(F32)
    large = max_exact + (jnp.log(nf / max_exact) / math.log(MAX_DISTANCE / max_exact)
                         * (N_BUCKETS - max_exact)).astype(jnp.int32)
    large = jnp.minimum(large, N_BUCKETS - 1)
    return jnp.where(n < max_exact, n, large)


def _bias_tables(rel_bias):
    qi = jnp.arange(BLOCK)[:, None]
    ki = jnp.arange(2 * BLOCK)[None, :]
    dist = qi + BLOCK - ki
    specs = [(d, w // d, 4 * gi, 4 * gi + 4) for gi, (w, d) in enumerate(DIL_GROUPS)] + [(1, B_WINDOW - 1, 12, 20)]
    biases, buckets = [], []
    for stride, steps, h0, h1 in specs:
        valid = (dist >= 0) & (dist <= steps)
        bk = jnp.where(valid, _t5_bucket(dist * stride), -1).astype(jnp.int32)
        onehot = (bk[None, :, :] == jnp.arange(N_BUCKETS, dtype=jnp.int32)[:, None, None]).astype(F32)
        b = jnp.einsum("bqk,bh->hqk", onehot, rel_bias[:, h0:h1], precision=lax.Precision.HIGHEST)
        biases.append(jnp.where(valid[None], b, NEG))
        buckets.append(bk)
    return jnp.concatenate(biases, axis=0), jnp.stack(buckets, axis=0)


def _local_step(x, tgt, W, S, shards=None, tail_host=None):
    nseq, seq, _ = x.shape
    t = nseq * seq
    xf = x.reshape(t, D_MODEL)
    bias_all, buckets = _bias_tables(S["rel_bias"])
    sink_b = jnp.broadcast_to(S["sinks"].reshape(2, 4, 1), (2, 4, 128)).astype(F32)
    sink_0 = jnp.zeros((1, 4, 128), F32)
    dist = shards is not None
    W = dict(W)
    G, GS, reduced = {}, {}, {}

    def put(keys, gathered):
        for k, g in zip(keys, gathered):
            W[k] = g.reshape(_FULL_SHAPE.get(k, (N_CHIPS * shards[k].shape[0], D_MODEL)))

    def gather_rider(keys):
        return _GatherRider([shards[k] for k in keys]) if dist else None

    def pair(keys):
        return _pair_reduce([G[k].reshape(N_CHIPS, 2, shards[k].shape[0] // 2, D_MODEL) for k in keys],
                            "grad_pair_reduce_" + keys[0])

    def finish(keys, own, rec):
        full = _final_reduce(own, rec, "grad_final_reduce_" + keys[0])
        off = 0
        for k in keys:
            r = shards[k].shape[0]
            reduced[k] = full[:, off:off + r // 2].reshape(r, D_MODEL)
            off += r // 2

    if dist:
        first = ("wgt1", "wut1", "wd1")
        put(first, _gather_rows([shards[k] for k in first]))
    keys = ("wint",)
    (h1, n1, g1, u1, a1), ro = _ffn_fwd(xf, S["ffn1_norm"], W["wgt1"], W["wut1"], W["wd1"], rider=gather_rider(keys))
    put(keys, ro)
    keys = ("wout", "wat", "wbt", "wgt2")
    (un, za, zb, zg), ro = _inproj_fwd(h1, S["mix_norm"], W["wint"], S["b_in"], nseq, rider=gather_rider(keys))
    put(keys, ro)

    seq3 = lambda a: a.reshape(nseq, seq, a.shape[-1])
    zb3 = seq3(zb)
    pair_blk = lambda cb: pl.BlockSpec((1, 2, seq, 128), lambda n, r, cb=cb: (n, cb, 0, 0))
    a_cfg = []
    outs, lses = [], []
    for gi, (_, d) in enumerate(DIL_GROUPS):
        cfg = dict(grid=(nseq, d), seq=seq, stride=d, kvw=GW, split=True,
                   q_spec=pair_blk(gi), k_spec=pair_blk(3 + gi), v_spec=pair_blk(6 + gi), o_spec=pair_blk(0),
                   bias_map=lambda n, r: (0, 0, 0), sink_map=lambda n, r: (0, 0, 0), has_sink=False)
        a_cfg.append(cfg)
        (o, lse), _ = _attn_fwd(za, za, za, bias_all[4 * gi:4 * gi + 4], sink_0, o_shape=(nseq, 2, seq, 128),
                                o_dtype=F32, name=f"attn_a{gi}_fwd", **cfg)
        outs.append(o)
        lses.append(lse)
    wide_blk = lambda w, cmap: pl.BlockSpec((1, seq, w), cmap)
    b_cfg = dict(grid=(nseq, 2), seq=seq, stride=1, kvw=2 * HEAD_DIM, split=False,
                 q_spec=wide_blk(GW, lambda n, r: (n, 0, r)), k_spec=wide_blk(2 * HEAD_DIM, lambda n, r: (n, 0, 4)),
                 v_spec=wide_blk(2 * HEAD_DIM, lambda n, r: (n, 0, 5)), o_spec=wide_blk(GW, lambda n, r: (n, 0, r)),
                 bias_map=lambda n, r: (r, 0, 0), sink_map=lambda n, r: (r, 0, 0), has_sink=True)
    keys = ("wut2",)
    bias_b_fwd = bias_all[12:20].at[:, :, 0].set(jnp.broadcast_to(S["sinks"].reshape(8, 1), (8, BLOCK)))
    (yb, lse_b), ro = _attn_fwd(zb3, zb3, zb3, bias_b_fwd, sink_b, o_shape=(nseq, seq, 2 * GW), o_dtype=BF16,
                                name="attn_b_fwd", rider=gather_rider(keys), **b_cfg)
    put(keys, ro)
    yb = yb.reshape(t, 2 * GW)

    keys = ("wd2",)
    (h2, y, lse_tot, pa, pb, merged), ro = _merge_fwd(outs[0], outs[1], outs[2], lses[0], lses[1], lses[2], yb, zg, h1,
                                                      W["wat"], W["wbt"], W["wout"], rider=gather_rider(keys))
    put(keys, ro)
    (dh3, n2, g2, u2, a2, loss_part, g_final), _ = _ffn_fwd(
        h2, S["ffn2_norm"], W["wgt2"], W["wut2"], W["wd2"],
        head=(S["final_norm"].reshape(1, D_MODEL), tgt.reshape(t, D_MODEL)))

    GS["final_norm"] = g_final
    dh2, dg2, du2, df2, GS["ffn2_norm"] = _ffn_bwd(dh3, h2, S["ffn2_norm"], g2, u2, W["wgt2"], W["wut2"], W["wd2"])
    G["wgt2"] = _wgrad(dg2, n2, MXU_DIM, name="wgrad_gate2")
    G["wut2"] = _wgrad(du2, n2, MXU_DIM, name="wgrad_up2")
    G["wd2"] = _wgrad(a2, df2, MXU_DIM, name="wgrad_down2")

    keys = ("wgt2", "wut2", "wd2")
    rider = _ExchangeRider([pair(keys)]) if dist else None
    (dpa, dpb, dga, dgb, dya, dyb, dh2b, ca, cb), ro = _merge_bwd(dh2, pa, pb, zg, y, yb, W["wat"], W["wbt"], W["wout"],
                                                                  nseq, rider=rider)
    if dist:
        finish(keys, *ro)

    dqs, dks, dvs, dbs = [], [], [], []
    shp = (nseq, 2, seq, 128)
    halves = lambda a: [a[:, hf].reshape(t, 128).astype(BF16) for hf in range(2)]
    for gi in range(len(DIL_GROUPS)):
        dq, dk, dv, db, _ = _attn_bwd(za, za, za, bias_all[4 * gi:4 * gi + 4], sink_0, dya, ca, lse_tot,
                                      n_bias=4, dq_shape=shp, dkv_shape=shp, g_dtype=F32,
                                      kv_out_spec=a_cfg[gi]["o_spec"], name=f"attn_a{gi}_bwd", **a_cfg[gi])
        dqs += halves(dq)
        dks += halves(dk)
        dvs += halves(dv)
        dbs.append(db)
    dqb, dkb, dvb, dbb, dsink = _attn_bwd(zb3, zb3, zb3, bias_all[12:20], sink_b, seq3(dyb), seq3(cb), lse_b,
                                          n_bias=8, dq_shape=(nseq, seq, 2 * GW),
                                          dkv_shape=(nseq, seq, 2 * HEAD_DIM), g_dtype=BF16,
                                          kv_out_spec=wide_blk(2 * HEAD_DIM, lambda n, r: (n, 0, 0)),
                                          name="attn_b_bwd", **b_cfg)
    dz = jnp.concatenate(dqs + dks + dvs + [dqb.reshape(t, 2 * GW), dkb.reshape(t, 2 * HEAD_DIM),
                                            dvb.reshape(t, 2 * HEAD_DIM), dga, dgb], axis=-1)
    gb_tab = _bias_grad(jnp.concatenate(dbs + [dbb], axis=0), buckets)
    if dist:
        GS["bias_tab"], GS["sink_tiles"] = gb_tab, dsink
    else:
        GS["rel_bias"] = gb_tab[:, :20]
        GS["sinks"] = dsink[:, 0, 0].reshape(1, 8)

    G["wint"], GS["b_in"] = _wgrad(dz, un, MXU_DIM, with_colsum=True, name="wgrad_in")
    G["wout"] = _wgrad(merged, dh2b, MXU_DIM, name="wgrad_out")
    G["wat"] = _wgrad(dpa, y, MXU_DIM, name="wgrad_branch_a")
    G["wbt"] = _wgrad(dpb, yb, MXU_DIM, name="wgrad_branch_b")
    keys = ("wint", "wout", "wat", "wbt")
    rider = _ExchangeRider([pair(keys)]) if dist else None
    (dh1, GS["mix_norm"]), ro = _inproj_bwd(dz, dh2, h1, S["mix_norm"], W["wint"], rider=rider)
    if dist:
        finish(keys, *ro)

    dx, dg1, du1, df1, GS["ffn1_norm"] = _ffn_bwd(dh1, xf, S["ffn1_norm"], g1, u1, W["wgt1"], W["wut1"], W["wd1"])
    G["wgt1"] = _wgrad(dg1, n1, MXU_DIM, name="wgrad_gate1")
    if dist:
        G["wut1"], ro = _wgrad(du1, n1, MXU_DIM, name="wgrad_up1", rider=_ExchangeRider([pair(("wgt1",))]))
        finish(("wgt1",), *ro)
        G["wd1"], ro = _wgrad(a1, df1, MXU_DIM, name="wgrad_down1", rider=_ExchangeRider([pair(("wut1",))]))
        finish(("wut1",), *ro)
        finish(("wd1",), *tail_host(_ExchangeRider([pair(("wd1",))]), reduced))
    else:
        G["wut1"] = _wgrad(du1, n1, MXU_DIM, name="wgrad_up1")
        G["wd1"] = _wgrad(a1, df1, MXU_DIM, name="wgrad_down1")
    return loss_part, dx.reshape(x.shape), (reduced if dist else G), GS


_SMALL = ("ffn1_norm", "mix_norm", "ffn2_norm", "final_norm", "b_in", "sinks", "rel_bias")
_ORDER = ("ffn1_norm", "ffn1_w_gate", "ffn1_w_up", "ffn1_w_down", "mix_norm", "w_in", "b_in", "w_branch_a",
          "w_branch_b", "w_out", "sinks", "rel_bias", "ffn2_norm", "ffn2_w_gate", "ffn2_w_up", "ffn2_w_down",
          "final_norm")
_BIG = (("wgt1", "ffn1_w_gate", True, 704), ("wut1", "ffn1_w_up", True, 704), ("wd1", "ffn1_w_down", False, 704),
        ("wint", "w_in", True, 1280), ("wout", "w_out", False, 256), ("wat", "w_branch_a", True, 64),
        ("wbt", "w_branch_b", True, 128), ("wgt2", "ffn2_w_gate", True, 704), ("wut2", "ffn2_w_up", True, 704),
        ("wd2", "ffn2_w_down", False, 704))
_FULL_SHAPE = {"wat": (D_MODEL, GW), "wbt": (D_MODEL, 2 * GW)}


def kernel(x, ffn1_norm, ffn1_w_gate, ffn1_w_up, ffn1_w_down, mix_norm, w_in, b_in, w_branch_a, w_branch_b, w_out, sinks, rel_bias, ffn2_norm, ffn2_w_gate, ffn2_w_up, ffn2_w_down, final_norm, loss_target, m_ffn1_norm, m_ffn1_w_gate, m_ffn1_w_up, m_ffn1_w_down, m_mix_norm, m_w_in, m_b_in, m_w_branch_a, m_w_branch_b, m_w_out, m_sinks, m_rel_bias, m_ffn2_norm, m_ffn2_w_gate, m_ffn2_w_up, m_ffn2_w_down, m_final_norm, v_ffn1_norm, v_ffn1_w_gate, v_ffn1_w_up, v_ffn1_w_down, v_mix_norm, v_w_in, v_b_in, v_w_branch_a, v_w_branch_b, v_w_out, v_sinks, v_rel_bias, v_ffn2_norm, v_ffn2_w_gate, v_ffn2_w_up, v_ffn2_w_down, v_final_norm):
    args = dict(locals())
    w = {n: args[n] for n in _ORDER}
    m = {n: args["m_" + n] for n in _ORDER}
    v = {n: args["v_" + n] for n in _ORDER}

    shards = {}
    for key, name, transposed, rows in _BIG:
        a = w[name][0]
        a = (a.T if transposed else a).astype(BF16)
        shards[key] = a.reshape(rows, D_MODEL)
    S = {n: w[n] for n in _SMALL}

    row_adam = lambda n: (w[n][0].T, m[n][0].T, v[n][0].T)
    early = {}

    def tail_host(rider, reduced):
        tensors = []
        for key, n in (("wgt2", "ffn2_w_gate"), ("wut2", "ffn2_w_up"), ("wd2", "ffn2_w_down")):
            wmv = row_adam(n) if key != "wd2" else (w[n][0], m[n][0], v[n][0])
            tensors.append((wmv[0], reduced[key], wmv[1], wmv[2]))
        res, ro = _adamw_many(tensors, "adamw_ffn2", rider=rider)
        early["ffn2_w_gate"], early["ffn2_w_up"], early["ffn2_w_down"] = res
        return ro

    loss_part, grad_x, reduced, GS = _local_step(x, loss_target, {}, S, shards, tail_host)

    small = _allreduce_small(GS["ffn1_norm"], GS["mix_norm"], GS["ffn2_norm"], GS["final_norm"], GS["b_in"],
                             GS["sink_tiles"], GS["bias_tab"], loss_part)
    loss = small[9, 8]

    out_g, out_d, out_m, out_v = {}, {}, {}, {}
    for key, n, transposed, rows in _BIG:
        nat = w[n][0].shape
        if transposed and nat[1] % 128:
            res = early[n] if n in early else _adamw(row_adam(n)[0], reduced[key], *row_adam(n)[1:], "adamw_" + n)
            res = [reduced[key].T] + [r.T for r in res]
        elif n in early:
            res = [reduced[key]] + list(early[n])
        else:
            g = reduced[key].reshape(nat[1], nat[0]).T if transposed else reduced[key].reshape(nat)
            res = [g] + list(_adamw(w[n][0], g, m[n][0], v[n][0], "adamw_" + n))
        out_g[n], out_d[n], out_m[n], out_v[n] = [r[None] for r in res]
    row = lambda d: {n: (d[n].reshape(1, D_MODEL) if n == "final_norm" else d[n]) for n in _SMALL}
    for dst, src in zip((out_g, out_d, out_m, out_v), _adamw_small(small, row(w), row(m), row(v))):
        dst.update(src)
        dst["final_norm"] = src["final_norm"].reshape(D_MODEL)

    return (loss, grad_x, *[out_g[n] for n in _ORDER], *[out_d[n] for n in _ORDER],
            *[out_m[n] for n in _ORDER], *[out_v[n] for n in _ORDER])
```

```python
import math

import jax
import jax.numpy as jnp
from jax import lax
from jax.experimental import pallas as pl
from jax.experimental.pallas import tpu as pltpu

F32, BF16 = jnp.float32, jnp.bfloat16
MESH = pl.DeviceIdType.MESH

D_MODEL = 1024
D_FF = 2816
D_IN = 5120
HEAD_DIM = 64
BLOCK = 128
DIL_GROUPS = ((128, 1), (512, 4), (2048, 16))
B_WINDOW = 128
N_BUCKETS = 32
MAX_DISTANCE = 2048
EPS = 1e-6
N_CHIPS = 4
GW = 256
ZA_W = 2304
ZB_W = 768
NEG = -1e30

ADAM_LR, ADAM_B1, ADAM_B2, ADAM_EPS, ADAM_WD, ADAM_STEP = 0.001, 0.9, 0.999, 1e-08, 0.01, 10

VMEM_BIG = 56 * 1024 * 1024
TM = 512
TM_BWD = 256
MXU_DIM = 256
FF_BOUNDS = (0, 4 * MXU_DIM, 8 * MXU_DIM, D_FF)
DMA_SPLIT = 8
RESIDUES_PER_STEP = 16
ATTN_UNROLL = 15


def _dot(a, b):
    return jnp.dot(a, b, preferred_element_type=F32)


def _dot_nt(a, b):
    return lax.dot_general(a, b, (((1,), (1,)), ((), ())), preferred_element_type=F32)


def _dot_tn(a, b):
    return lax.dot_general(a, b, (((0,), (0,)), ((), ())), preferred_element_type=F32)


def _sigmoid(x):
    return 0.5 * jnp.tanh(0.5 * x) + 0.5


def _params(sem, vmem=None):
    return pltpu.CompilerParams(dimension_semantics=sem, vmem_limit_bytes=vmem)


ANY = pl.BlockSpec(memory_space=pl.ANY)


def _me():
    return lax.axis_index("x"), lax.axis_index("y"), lax.axis_index("c")


_CHIP_RELS = ((1, 0), (0, 1), (1, 1))


def _flip(v, f):
    return 1 - v if f else v


def _remote(src, dst, ssem, rsem, peer):
    return pltpu.make_async_remote_copy(src_ref=src, dst_ref=dst, send_sem=ssem, recv_sem=rsem,
                                        device_id=peer, device_id_type=MESH)


def _row_pieces(rows, n):
    step = max(16, -(-rows // n) // 16 * 16)
    out, s = [], 0
    while s < rows:
        out.append((s, min(step, rows - s)))
        s += step
    return out


def _gather_rows(shards):
    nt = len(shards)
    rows = [s.shape[0] for s in shards]

    def body(*refs):
        srcs, outs = refs[:nt], refs[nt:2 * nt]
        halves, quarters = refs[2 * nt:3 * nt], refs[3 * nt:4 * nt]
        ici_s, ici_r, fwd_s, fwd_r, d2d_s, d2d_r, keep, loc = refs[4 * nt:]
        x, y, c = _me()
        j = 2 * x + y
        sib = (x, y, 1 - c)
        nbr = ((1 - x, y, c), (x, 1 - y, c))
        nbr_j = (2 * (1 - x) + y, 2 * x + (1 - y))
        diag_j = 2 * (1 - x) + (1 - y)
        local = [pltpu.make_async_copy(srcs[t], outs[t].at[j], loc.at[t]) for t in range(nt)]
        for cp in local:
            cp.start()
        pending = []
        for a in range(2):
            for t in range(nt):
                half = pl.ds(c * (rows[t] // 2), rows[t] // 2)
                cp = _remote(srcs[t].at[half], halves[t].at[a], ici_s.at[2 * t + a], ici_r.at[2 * t + a], nbr[a])
                cp.start()
                pending.append(cp)
        placed = []

        def place(src, dst_of, idx):
            mine = pltpu.make_async_copy(src, dst_of, keep.at[idx])
            mine.start()
            cp = _remote(src, dst_of, d2d_s.at[idx], d2d_r.at[idx], sib)
            cp.start()
            placed.append((mine, cp))

        for a in range(2):
            for t in range(nt):
                r2, r4 = rows[t] // 2, rows[t] // 4
                got = halves[t].at[a]
                _remote(got, got, ici_s.at[2 * t + a], ici_r.at[2 * t + a], nbr[a]).wait_recv()
                cp = _remote(halves[t].at[a, pl.ds(a * r4, r4)], quarters[t].at[a], fwd_s.at[2 * t + a],
                             fwd_r.at[2 * t + a], nbr[1 - a])
                cp.start()
                pending.append(cp)
                place(got, outs[t].at[nbr_j[a], pl.ds(c * r2, r2)], 4 * t + a)
        for a in range(2):
            for t in range(nt):
                r2, r4 = rows[t] // 2, rows[t] // 4
                got = quarters[t].at[a]
                _remote(got, got, fwd_s.at[2 * t + a], fwd_r.at[2 * t + a], nbr[1 - a]).wait_recv()
                place(got, outs[t].at[diag_j, pl.ds(c * r2 + a * r4, r4)], 4 * t + 2 + a)
        for mine, cp in placed:
            mine.wait()
            cp.wait()
        for cp in pending:
            cp.wait_send()
        for cp in local:
            cp.wait()

    stage = ([pltpu.VMEM((2, r // 2, D_MODEL), BF16) for r in rows] + [pltpu.VMEM((2, r // 4, D_MODEL), BF16) for r in rows])
    sems = ([pltpu.SemaphoreType.DMA((2 * nt,)) for _ in range(4)] + [pltpu.SemaphoreType.DMA((4 * nt,))] * 3
            + [pltpu.SemaphoreType.DMA((nt,))])
    return pl.pallas_call(
        body, name="gather_weights",
        out_shape=tuple(jax.ShapeDtypeStruct((N_CHIPS,) + s.shape, s.dtype) for s in shards),
        in_specs=[pl.BlockSpec(memory_space=pltpu.VMEM)] * nt,
        out_specs=tuple([ANY] * nt), scratch_shapes=stage + sems,
    )(*shards)


VMEM_WHOLE = pl.BlockSpec(memory_space=pltpu.VMEM)


def _pair_reduce(grads, name):
    nt = len(grads)
    r2 = [g.shape[2] for g in grads]
    off = [sum(r2[:t]) for t in range(nt)]
    tot = sum(r2)

    def body(*refs):
        gs = refs[:nt]
        s_ref, mine, got, ssem, rsem, lsem = refs[nt:]
        x, y, c = _me()
        sib = (x, y, 1 - c)
        for t in range(nt):
            for k in range(N_CHIPS):
                rows = pl.ds(off[t], r2[t])
                _remote(gs[t].at[k, 1 - c], got.at[k, rows], ssem, rsem, sib).start()
                pltpu.make_async_copy(gs[t].at[k, c], mine.at[k, rows], lsem).start()
        pltpu.make_async_copy(mine, mine, lsem).wait()
        _remote(got, got, ssem, rsem, sib).wait()
        for k in range(N_CHIPS):
            for st, sz in _row_pieces(tot, 4):
                rows = slice(st, st + sz)
                s_ref[k, rows, :] = (mine[k, rows, :].astype(F32) + got[k, rows, :].astype(F32)).astype(BF16)

    shp = jax.ShapeDtypeStruct((N_CHIPS, tot, D_MODEL), BF16)
    buf = pltpu.VMEM((N_CHIPS, tot, D_MODEL), BF16)
    return pl.pallas_call(
        body, name=name, out_shape=shp, in_specs=[ANY] * nt, out_specs=VMEM_WHOLE,
        scratch_shapes=[buf, buf, pltpu.SemaphoreType.DMA(()), pltpu.SemaphoreType.DMA(()),
                        pltpu.SemaphoreType.DMA(())],
        compiler_params=pltpu.CompilerParams(vmem_limit_bytes=VMEM_BIG),
    )(*grads)


def _final_reduce(own, rec, name):
    r2 = own.shape[0]
    stages = _row_pieces(r2, 2)

    def body(own_hbm, rec_hbm, o_ref, parts, fbuf, ssem, rsem, lsem, insems):
        x, y, c = _me()
        sib = (x, y, 1 - c)
        for p, (st, sz) in enumerate(stages):
            rows = pl.ds(st, sz)
            pltpu.make_async_copy(own_hbm.at[rows], parts.at[0, rows], insems.at[p]).start()
            for k in range(3):
                pltpu.make_async_copy(rec_hbm.at[k, rows], parts.at[1 + k, rows], insems.at[p]).start()
        for p, (st, sz) in enumerate(stages):
            stage = parts.at[:, pl.ds(st, sz)]
            pltpu.make_async_copy(stage, stage, insems.at[p]).wait()
            for s0, ssz in _row_pieces(sz, DMA_SPLIT // 2):
                rows = slice(st + s0, st + s0 + ssz)
                fbuf[rows, :] = (parts[0, rows, :].astype(F32) + parts[1, rows, :].astype(F32)
                                 + parts[2, rows, :].astype(F32) + parts[3, rows, :].astype(F32))
                dst = o_ref.at[c, pl.ds(st + s0, ssz)]
                pltpu.make_async_copy(fbuf.at[pl.ds(st + s0, ssz)], dst, lsem).start()
                _remote(fbuf.at[pl.ds(st + s0, ssz)], dst, ssem, rsem, sib).start()
        _remote(fbuf, o_ref.at[c], ssem, rsem, sib).wait()
        pltpu.make_async_copy(fbuf, o_ref.at[c], lsem).wait()

    return pl.pallas_call(
        body, name=name, out_shape=jax.ShapeDtypeStruct((2, r2, D_MODEL), F32),
        in_specs=[ANY, ANY], out_specs=ANY,
        scratch_shapes=[pltpu.VMEM((4, r2, D_MODEL), BF16), pltpu.VMEM((r2, D_MODEL), F32),
                        pltpu.SemaphoreType.DMA(()), pltpu.SemaphoreType.DMA(()), pltpu.SemaphoreType.DMA(()),
                        pltpu.SemaphoreType.DMA((2,))],
        compiler_params=pltpu.CompilerParams(vmem_limit_bytes=VMEM_BIG),
    )(own, rec)


SMALL_ROWS = 48


def _allreduce_small(g_ffn1, g_mix, g_ffn2, g_final, g_bin, dsink, bias_tab, loss_part):
    def body(f1_ref, mx_ref, f2_ref, fn_ref, bi_ref, sk_ref, bt_ref, ls_ref, o_ref, mine, buf, tabs, send_sems,
             recv_sems):
        x, y, c = _me()
        me = 4 * x + 2 * y + c
        mine[...] = jnp.zeros_like(mine)
        for r, ref in enumerate((f1_ref, mx_ref, f2_ref, fn_ref)):
            mine[r:r + 1, :] = ref[...]
        for k in range(D_IN // D_MODEL):
            mine[4 + k:5 + k, :] = bi_ref[:, k * D_MODEL:(k + 1) * D_MODEL]
        lane = lax.broadcasted_iota(jnp.int32, (1, 128), 1)
        row = jnp.where(lane == 8, ls_ref[0:1, :], 0.0)
        for h in range(8):
            row = jnp.where(lane == h, sk_ref[h, 0:1, :], row)
        mine[9:10, 0:128] = row
        buf[me] = mine[...]
        tabs[me] = bt_ref[...]
        copies = []
        for k in range(1, 8):
            peer = (_flip(x, (k >> 2) & 1), _flip(y, (k >> 1) & 1), _flip(c, k & 1))
            for t, (src, dst) in enumerate(((mine, buf), (bt_ref, tabs))):
                cp = _remote(src, dst.at[me], send_sems.at[2 * (k - 1) + t], recv_sems.at[2 * (k - 1) + t], peer)
                cp.start()
                copies.append(cp)
        for cp in copies:
            cp.wait()
        acc, tab = buf[0], tabs[0]
        for i in range(1, 8):
            acc, tab = acc + buf[i], tab + tabs[i]
        o_ref[...] = jnp.zeros_like(o_ref)
        o_ref[0:16, :] = acc
        o_ref[16:48, 0:128] = tab

    vm = pl.BlockSpec(memory_space=pltpu.VMEM)
    return pl.pallas_call(
        body, name="allreduce_small", out_shape=jax.ShapeDtypeStruct((SMALL_ROWS, D_MODEL), F32),
        in_specs=[vm] * 8, out_specs=vm,
        scratch_shapes=[pltpu.VMEM((16, D_MODEL), F32), pltpu.VMEM((8, 16, D_MODEL), F32),
                        pltpu.VMEM((8, N_BUCKETS, 128), F32), pltpu.SemaphoreType.DMA((14,)),
                        pltpu.SemaphoreType.DMA((14,))],
    )(g_ffn1, g_mix, g_ffn2, g_final, g_bin, dsink, bias_tab, loss_part)


def _adam_update(w, g, m, v):
    nm = ADAM_B1 * m + (1.0 - ADAM_B1) * g
    nv = ADAM_B2 * v + (1.0 - ADAM_B2) * (g * g)
    bc1 = 1.0 - ADAM_B1 ** ADAM_STEP
    bc2 = 1.0 - ADAM_B2 ** ADAM_STEP
    return -ADAM_LR * ((nm / bc1) / (jnp.sqrt(nv / bc2) + ADAM_EPS) + ADAM_WD * w), nm, nv


def _adamw_small(packed, w, m, v):
    names = ("ffn1_norm", "mix_norm", "ffn2_norm", "final_norm", "b_in", "sinks", "rel_bias")
    nn = len(names)

    def grad_of(p_ref, name, k=0):
        if name == "b_in":
            return p_ref[4 + k:5 + k, :]
        if name == "sinks":
            return p_ref[9:10, 0:8]
        if name == "rel_bias":
            return p_ref[16:48, 0:20]
        r = names.index(name)
        return p_ref[r:r + 1, :]

    def body(p_ref, *refs):
        ws, ms, vs = refs[:nn], refs[nn:2 * nn], refs[2 * nn:3 * nn]
        outs = refs[3 * nn:]
        for i, name in enumerate(names):
            og, od, om, ov = outs[i], outs[nn + i], outs[2 * nn + i], outs[3 * nn + i]
            pieces = range(D_IN // D_MODEL) if name == "b_in" else (0,)
            for k in pieces:
                sl = (slice(None), slice(k * D_MODEL, (k + 1) * D_MODEL)) if name == "b_in" else (Ellipsis,)
                g = grad_of(p_ref, name, k)
                d, nm, nv = _adam_update(ws[i][sl], g, ms[i][sl], vs[i][sl])
                og[sl], od[sl], om[sl], ov[sl] = g, d, nm, nv

    vm = pl.BlockSpec(memory_space=pltpu.VMEM)
    shapes = [jax.ShapeDtypeStruct(w[n].shape, F32) for n in names]
    res = pl.pallas_call(
        body, name="adamw_small", out_shape=tuple(shapes * 4), in_specs=[vm] * (1 + 3 * nn),
        out_specs=tuple([vm] * (4 * nn)),
    )(packed, *[w[n] for n in names], *[m[n] for n in names], *[v[n] for n in names])
    return [dict(zip(names, res[i * nn:(i + 1) * nn])) for i in range(4)]


class _GatherRider:
    def __init__(self, shards):
        self.inputs = list(shards)
        nt = len(shards)
        self.out_shape = [jax.ShapeDtypeStruct((N_CHIPS,) + s.shape, s.dtype) for s in shards]
        self.scratch = [pltpu.SemaphoreType.DMA((3 * nt,)), pltpu.SemaphoreType.DMA((3 * nt,)),
                        pltpu.SemaphoreType.DMA((nt,))]

    def _copies(self, srcs, outs, sems):
        ici_s, ici_r, loc = sems
        x, y, c = _me()
        j = 2 * x + y
        local = [pltpu.make_async_copy(srcs[t], outs[t].at[j], loc.at[t]) for t in range(len(srcs))]
        remote = []
        for k, (fx, fy) in enumerate(_CHIP_RELS):
            peer = (_flip(x, fx), _flip(y, fy), c)
            for t in range(len(srcs)):
                remote.append(_remote(srcs[t], outs[t].at[j], ici_s.at[3 * t + k], ici_r.at[3 * t + k], peer))
        return local, remote

    def start(self, srcs, outs, sems):
        local, remote = self._copies(srcs, outs, sems)
        for cp in local + remote:
            cp.start()

    def finish(self, srcs, outs, sems):
        local, remote = self._copies(srcs, outs, sems)
        for cp in remote + local:
            cp.wait()


class _ExchangeRider:
    def __init__(self, parts):
        self.inputs = list(parts)
        self.r2 = [p.shape[1] for p in parts]
        self.off = [sum(self.r2[:g]) for g in range(len(parts))]
        tot = sum(self.r2)
        self.out_shape = [jax.ShapeDtypeStruct((tot, D_MODEL), BF16), jax.ShapeDtypeStruct((3, tot, D_MODEL), BF16)]
        self.scratch = [pltpu.SemaphoreType.DMA((3,)), pltpu.SemaphoreType.DMA((3,)), pltpu.SemaphoreType.DMA(())]

    def start(self, ps, outs, sems):
        own_ref, rec_ref = outs
        ssems, rsems, lsem = sems
        x, y, c = _me()
        j = 2 * x + y
        for g in range(len(ps)):
            pltpu.make_async_copy(ps[g].at[j], own_ref.at[pl.ds(self.off[g], self.r2[g])], lsem).start()
        for k, (fx, fy) in enumerate(_CHIP_RELS):
            px, py = _flip(x, fx), _flip(y, fy)
            for g in range(len(ps)):
                for st, sz in _row_pieces(self.r2[g], 2):
                    _remote(ps[g].at[2 * px + py, pl.ds(st, sz)], rec_ref.at[k, pl.ds(self.off[g] + st, sz)],
                            ssems.at[k], rsems.at[k], (px, py, c)).start()

    def finish(self, ps, outs, sems):
        own_ref, rec_ref = outs
        ssems, rsems, lsem = sems
        x, y, c = _me()
        for k in range(3):
            _remote(rec_ref.at[k], rec_ref.at[k], ssems.at[k], rsems.at[k], (x, y, c)).wait()
        pltpu.make_async_copy(own_ref, own_ref, lsem).wait()


def _pallas(body, args, *, name, grid, in_specs, out_specs, out_shape, scratch_shapes=(), sem=None, vmem=None,
            rider=None):
    if rider is None:
        res = pl.pallas_call(body, name=name, grid=grid, in_specs=list(in_specs), out_specs=tuple(out_specs),
                             out_shape=tuple(out_shape), scratch_shapes=list(scratch_shapes),
                             compiler_params=_params(sem, vmem))(*args)
        return tuple(res), ()
    n_in, n_out, n_sc = len(in_specs), len(out_shape), len(scratch_shapes)
    r_in, r_out = len(rider.inputs), len(rider.out_shape)

    def wrapped(*refs):
        ins, rins = refs[:n_in], refs[n_in:n_in + r_in]
        p = n_in + r_in
        outs, routs = refs[p:p + n_out], refs[p + n_out:p + n_out + r_out]
        p += n_out + r_out
        scr, rsems = refs[p:p + n_sc], refs[p + n_sc:]
        first = pl.program_id(0) == 0
        last = pl.program_id(0) == grid[0] - 1
        for a in range(1, len(grid)):
            first = first & (pl.program_id(a) == 0)
            last = last & (pl.program_id(a) == grid[a] - 1)

        @pl.when(first)
        def _():
            rider.start(rins, routs, rsems)

        body(*ins, *outs, *scr)

        @pl.when(last)
        def _():
            rider.finish(rins, routs, rsems)

    res = pl.pallas_call(
        wrapped, name=name, grid=grid, in_specs=list(in_specs) + [ANY] * r_in,
        out_specs=tuple(out_specs) + (ANY,) * r_out, out_shape=tuple(out_shape) + tuple(rider.out_shape),
        scratch_shapes=list(scratch_shapes) + rider.scratch,
        compiler_params=_params(("arbitrary",) * len(grid), vmem))(*args, *rider.inputs)
    return tuple(res[:n_out]), tuple(res[n_out:])


def _load_weights(pairs, sems):
    copies = [pltpu.make_async_copy(hbm, vmem, sems.at[i]) for i, (hbm, vmem) in enumerate(pairs)]
    for cp in copies:
        cp.start()
    for cp in copies:
        cp.wait()


def _loss_tile(hh, gain, tgt):
    r = lax.rsqrt(jnp.mean(hh * hh, axis=-1, keepdims=True) + EPS)
    hn = hh * r
    err = hn * gain - tgt
    part = (0.5 / D_MODEL) * jnp.sum(jnp.sum(err * err, axis=1, keepdims=True), axis=0, keepdims=True)
    dy = err * (1.0 / D_MODEL)
    dng = dy * gain
    dh = r * (dng - hn * jnp.mean(dng * hn, axis=-1, keepdims=True))
    return dh, part, jnp.sum(dy * hn, axis=0, keepdims=True)


def _ffn_fwd(h, gain, wgt, wut, wd, rider=None, head=None):
    t = h.shape[0]

    def body(h_ref, gain_ref, wg_hbm, wu_hbm, wd_hbm, *rest):
        if head is None:
            hout_ref, n_ref, g_ref, u_ref, a_ref, wg_v, wu_v, wd_v, wsem = rest
        else:
            fg_ref, tgt_ref, hout_ref, n_ref, g_ref, u_ref, a_ref, loss_ref, gg_ref, wg_v, wu_v, wd_v, wsem = rest
        @pl.when(pl.program_id(0) == 0)
        def _():
            _load_weights(((wg_hbm, wg_v), (wu_hbm, wu_v), (wd_hbm, wd_v)), wsem)
            if head is not None:
                loss_ref[...] = jnp.zeros_like(loss_ref)
                gg_ref[...] = jnp.zeros_like(gg_ref)

        hh = h_ref[...]
        r = lax.rsqrt(jnp.mean(hh * hh, axis=-1, keepdims=True) + EPS)
        n = (hh * r * gain_ref[...]).astype(BF16)
        n_ref[...] = n
        acc = jnp.zeros((TM, D_MODEL), F32)
        for c0, c1 in zip(FF_BOUNDS[:-1], FF_BOUNDS[1:]):
            sl = slice(c0, c1)
            g = _dot_nt(n, wg_v[sl, :])
            u = _dot_nt(n, wu_v[sl, :])
            sg = _sigmoid(g)
            silu = g * sg
            a = (silu * u).astype(BF16)
            a_ref[:, sl] = a
            g_ref[:, sl] = (u * (sg * (1.0 + g * (1.0 - sg)))).astype(BF16)
            u_ref[:, sl] = silu.astype(BF16)
            acc = acc + _dot(a, wd_v[sl, :])
        hout = hh + 0.5 * acc
        if head is None:
            hout_ref[...] = hout
        else:
            dh, part, gpart = _loss_tile(hout, fg_ref[...], tgt_ref[...])
            hout_ref[...] = dh
            loss_ref[...] += part
            gg_ref[...] += gpart

    row = lambda w: pl.BlockSpec((TM, w), lambda i: (i, 0))
    vec = pl.BlockSpec((1, D_MODEL), lambda i: (0, 0))
    wv = pltpu.VMEM((D_FF, D_MODEL), BF16)
    args, in_specs = (h, gain, wgt, wut, wd), [row(D_MODEL), vec, ANY, ANY, ANY]
    out_shape = [jax.ShapeDtypeStruct((t, D_MODEL), F32), jax.ShapeDtypeStruct((t, D_MODEL), BF16)] + [
        jax.ShapeDtypeStruct((t, D_FF), BF16)] * 3
    out_specs = [row(D_MODEL), row(D_MODEL), row(D_FF), row(D_FF), row(D_FF)]
    if head is not None:
        args, in_specs = args + tuple(head), in_specs + [vec, row(D_MODEL)]
        out_shape += [jax.ShapeDtypeStruct((8, 128), F32), jax.ShapeDtypeStruct((1, D_MODEL), F32)]
        out_specs += [pl.BlockSpec((8, 128), lambda i: (0, 0)), vec]
    return _pallas(
        body, args, name="ffn_fwd", grid=(t // TM,), out_shape=tuple(out_shape), in_specs=in_specs,
        out_specs=tuple(out_specs), scratch_shapes=[wv, wv, wv, pltpu.SemaphoreType.DMA((3,))],
        sem=("arbitrary",), vmem=VMEM_BIG, rider=rider)


def _ffn_bwd(dhout, h, gain, dgf, duf, wgt, wut, wd):
    t = h.shape[0]
    tm = TM_BWD

    def body(dho_ref, h_ref, gain_ref, g_ref, u_ref, wg_hbm, wu_hbm, wd_hbm,
             dh_ref, dg_ref, du_ref, df_ref, gg_ref, wg_v, wu_v, wd_v, wsem):
        @pl.when(pl.program_id(0) == 0)
        def _():
            _load_weights(((wd_hbm, wd_v), (wg_hbm, wg_v), (wu_hbm, wu_v)), wsem)
            gg_ref[...] = jnp.zeros_like(gg_ref)

        dho = dho_ref[...]
        df = (0.5 * dho).astype(BF16)
        df_ref[...] = df
        dn = jnp.zeros((tm, D_MODEL), F32)
        for c0, c1 in zip(FF_BOUNDS[:-1], FF_BOUNDS[1:]):
            sl = slice(c0, c1)
            da = _dot_nt(df, wd_v[sl, :])
            dg = (da * g_ref[:, sl].astype(F32)).astype(BF16)
            du = (da * u_ref[:, sl].astype(F32)).astype(BF16)
            dg_ref[:, sl] = dg
            du_ref[:, sl] = du
            dn = dn + _dot(dg, wg_v[sl, :]) + _dot(du, wu_v[sl, :])
        hh = h_ref[...]
        r = lax.rsqrt(jnp.mean(hh * hh, axis=-1, keepdims=True) + EPS)
        hn = hh * r
        gg_ref[...] += jnp.sum(dn * hn, axis=0, keepdims=True)
        dng = dn * gain_ref[...]
        dh_ref[...] = dho + r * (dng - hn * jnp.mean(dng * hn, axis=-1, keepdims=True))

    row = lambda w: pl.BlockSpec((tm, w), lambda i: (i, 0))
    vec = pl.BlockSpec((1, D_MODEL), lambda i: (0, 0))
    wv = pltpu.VMEM((D_FF, D_MODEL), BF16)
    return pl.pallas_call(
        body, name="ffn_bwd", grid=(t // tm,),
        out_shape=(jax.ShapeDtypeStruct((t, D_MODEL), F32), jax.ShapeDtypeStruct((t, D_FF), BF16),
                   jax.ShapeDtypeStruct((t, D_FF), BF16),
                   jax.ShapeDtypeStruct((t, D_MODEL), BF16), jax.ShapeDtypeStruct((1, D_MODEL), F32)),
        in_specs=[row(D_MODEL), row(D_MODEL), vec, row(D_FF), row(D_FF), ANY, ANY, ANY],
        out_specs=(row(D_MODEL), row(D_FF), row(D_FF), row(D_MODEL), vec),
        scratch_shapes=[wv, wv, wv, pltpu.SemaphoreType.DMA((3,))],
        compiler_params=_params(("arbitrary",), VMEM_BIG),
    )(dhout, h, gain, dgf, duf, wgt, wut, wd)


def _wgrad(lhs, rhs, rb, with_colsum=False, name="wgrad", rider=None):
    t, k = lhs.shape
    n = rhs.shape[1]

    def body(l_ref, r_ref, o_ref, *rest):
        o_ref[...] = _dot_tn(l_ref[...], r_ref[...]).astype(BF16)
        if with_colsum:
            rest[0][...] = jnp.sum(l_ref[...].astype(F32), axis=0, keepdims=True)

    out_shape = [jax.ShapeDtypeStruct((k, n), BF16)]
    out_specs = [pl.BlockSpec((rb, n), lambda j: (j, 0))]
    if with_colsum:
        out_shape.append(jax.ShapeDtypeStruct((1, k), F32))
        out_specs.append(pl.BlockSpec((1, rb), lambda j: (0, j)))
    res, ro = _pallas(
        body, (lhs, rhs), name=name, grid=(k // rb,), out_shape=tuple(out_shape),
        in_specs=[pl.BlockSpec((t, rb), lambda j: (0, j)), pl.BlockSpec((t, n), lambda j: (0, 0))],
        out_specs=tuple(out_specs), sem=("arbitrary",), vmem=VMEM_BIG, rider=rider)
    if rider is not None:
        return res[0], ro
    return res if with_colsum else res[0]


def _lane_blocks(nseq, seq, nblk, tm=TM):
    spt = seq // tm
    return pl.BlockSpec((1, nblk, tm, 128), lambda i: (i // spt, 0, i % spt, 0))


def _inproj_fwd(h, gain, wint, b_in, nseq, rider=None):
    t = h.shape[0]
    seq = t // nseq
    cut_a = 5 * MXU_DIM
    pieces = ((0, cut_a, 0, 0), (cut_a, ZA_W - cut_a, 0, cut_a), (ZA_W, ZB_W, 1, 0), (ZA_W + ZB_W, 1024, 2, 0),
              (ZA_W + ZB_W + 1024, 1024, 2, 1024))

    def body(h_ref, gain_ref, w_hbm, b_ref, u_ref, za_ref, zb_ref, zg_ref, w_v):
        @pl.when(pl.program_id(0) == 0)
        def _():
            pltpu.sync_copy(w_hbm, w_v)

        hh = h_ref[...]
        r = lax.rsqrt(jnp.mean(hh * hh, axis=-1, keepdims=True) + EPS)
        un = (hh * r * gain_ref[...]).astype(BF16)
        u_ref[...] = un
        outs = (None, zb_ref, zg_ref)
        for c0, cw, oi, o0 in pieces:
            val = _dot_nt(un, w_v[c0:c0 + cw, :]) + b_ref[:, c0:c0 + cw]
            if oi == 0:
                for cb in range(cw // 128):
                    za_ref[0, o0 // 128 + cb] = val[:, cb * 128:(cb + 1) * 128]
            else:
                outs[oi][:, o0:o0 + cw] = val.astype(BF16)

    row = lambda w: pl.BlockSpec((TM, w), lambda i: (i, 0))
    return _pallas(
        body, (h, gain, wint, b_in), name="inproj_fwd", grid=(t // TM,),
        out_shape=(jax.ShapeDtypeStruct((t, D_MODEL), BF16), jax.ShapeDtypeStruct((nseq, ZA_W // 128, seq, 128), F32),
                   jax.ShapeDtypeStruct((t, ZB_W), BF16), jax.ShapeDtypeStruct((t, 2 * D_MODEL), BF16)),
        in_specs=[row(D_MODEL), pl.BlockSpec((1, D_MODEL), lambda i: (0, 0)), ANY,
                  pl.BlockSpec((1, D_IN), lambda i: (0, 0))],
        out_specs=(row(D_MODEL), _lane_blocks(nseq, seq, ZA_W // 128), row(ZB_W), row(2 * D_MODEL)),
        scratch_shapes=[pltpu.VMEM((D_IN, D_MODEL), BF16)], sem=("arbitrary",), vmem=VMEM_BIG, rider=rider)


def _inproj_bwd(dz, dh2, h, gain, wint, rider=None):
    t = h.shape[0]
    nc = 5
    cw = D_IN // nc

    def body(dz_ref, dh2_ref, h_ref, gain_ref, w_hbm, dh_ref, gg_ref, w_v):
        @pl.when(pl.program_id(0) == 0)
        def _():
            pltpu.sync_copy(w_hbm, w_v)
            gg_ref[...] = jnp.zeros_like(gg_ref)

        du = jnp.zeros((TM, D_MODEL), F32)
        for ci in range(nc):
            sl = slice(ci * cw, (ci + 1) * cw)
            du = du + _dot(dz_ref[:, sl], w_v[sl, :])
        hh = h_ref[...]
        r = lax.rsqrt(jnp.mean(hh * hh, axis=-1, keepdims=True) + EPS)
        hn = hh * r
        gg_ref[...] += jnp.sum(du * hn, axis=0, keepdims=True)
        dng = du * gain_ref[...]
        dh_ref[...] = dh2_ref[...] + r * (dng - hn * jnp.mean(dng * hn, axis=-1, keepdims=True))

    row = lambda w: pl.BlockSpec((TM, w), lambda i: (i, 0))
    vec = pl.BlockSpec((1, D_MODEL), lambda i: (0, 0))
    return _pallas(
        body, (dz, dh2, h, gain, wint), name="inproj_bwd", grid=(t // TM,),
        out_shape=(jax.ShapeDtypeStruct((t, D_MODEL), F32), jax.ShapeDtypeStruct((1, D_MODEL), F32)),
        in_specs=[row(D_IN), row(D_MODEL), row(D_MODEL), vec, ANY],
        out_specs=(row(D_MODEL), vec),
        scratch_shapes=[pltpu.VMEM((D_IN, D_MODEL), BF16)], sem=("arbitrary",), vmem=VMEM_BIG, rider=rider)


def _head_sums(x):
    w = x.shape[1]
    i = lax.broadcasted_iota(jnp.int32, (w, w), 0) // HEAD_DIM
    j = lax.broadcasted_iota(jnp.int32, (w, w), 1) // HEAD_DIM
    ones = (i == j).astype(BF16)
    hi = x.astype(BF16)
    r1 = x - hi.astype(F32)
    mid = r1.astype(BF16)
    lo = (r1 - mid.astype(F32)).astype(BF16)
    return _dot(hi, ones) + _dot(mid, ones) + _dot(lo, ones)


def _merge_fwd(o0, o1, o2, l0, l1, l2, yb, zg, h1, wat, wbt, wout, rider=None):
    t = h1.shape[0]
    nseq, _, seq, _ = o0.shape

    def body(o0_ref, o1_ref, o2_ref, l0_ref, l1_ref, l2_ref, yb_ref, ga_ref, gb_ref, h1_ref, wa_ref, wb_ref, wo_ref,
             h2_ref, y_ref, lt_ref, pa_ref, pb_ref, mg_ref):
        wide = lambda ref: jnp.concatenate([ref[0, 0], ref[0, 1]], axis=1)
        la, lb, lc = wide(l0_ref), wide(l1_ref), wide(l2_ref)
        mx = jnp.maximum(jnp.maximum(la, lb), lc)
        ea, eb, ec = jnp.exp(la - mx), jnp.exp(lb - mx), jnp.exp(lc - mx)
        den = ea + eb + ec
        y = (ea * wide(o0_ref) + eb * wide(o1_ref) + ec * wide(o2_ref)) / den
        lt = mx + jnp.log(den)
        lt_ref[0, 0] = lt[:, :128]
        lt_ref[0, 1] = lt[:, 128:]
        yb16 = y.astype(BF16)
        y_ref[...] = yb16
        pa = _dot_nt(yb16, wa_ref[...])
        pb = _dot_nt(yb_ref[...], wb_ref[...])
        pa_ref[...] = pa.astype(BF16)
        pb_ref[...] = pb.astype(BF16)
        mg = (_sigmoid(ga_ref[...].astype(F32)) * pa + _sigmoid(gb_ref[...].astype(F32)) * pb).astype(BF16)
        mg_ref[...] = mg
        h2_ref[...] = h1_ref[...] + _dot(mg, wo_ref[...])

    row = lambda w: pl.BlockSpec((TM, w), lambda i: (i, 0))
    full = lambda a: pl.BlockSpec(a.shape, lambda i: (0, 0))
    gate = lambda cb: pl.BlockSpec((TM, D_MODEL), lambda i: (i, cb))
    return _pallas(
        body, (o0, o1, o2, l0, l1, l2, yb, zg, zg, h1, wat, wbt, wout), name="merge_fwd", grid=(t // TM,),
        out_shape=(jax.ShapeDtypeStruct((t, D_MODEL), F32), jax.ShapeDtypeStruct((t, GW), BF16),
                   jax.ShapeDtypeStruct((nseq, 2, seq, 128), F32), jax.ShapeDtypeStruct((t, D_MODEL), BF16),
                   jax.ShapeDtypeStruct((t, D_MODEL), BF16), jax.ShapeDtypeStruct((t, D_MODEL), BF16)),
        in_specs=[_lane_blocks(nseq, seq, 2)] * 6 + [row(2 * GW), gate(0), gate(1), row(D_MODEL), full(wat), full(wbt),
                                                     full(wout)],
        out_specs=(row(D_MODEL), row(GW), _lane_blocks(nseq, seq, 2), row(D_MODEL), row(D_MODEL), row(D_MODEL)),
        sem=("parallel",), vmem=VMEM_BIG, rider=rider)


def _merge_bwd(dh2, pa, pb, zg, y, yb, wat, wbt, wout, nseq, rider=None):
    t = dh2.shape[0]

    def body(dh2_ref, pa_ref, pb_ref, ga_ref, gb_ref, y_ref, yb_ref, wa_ref, wb_ref, wo_ref,
             dpa_ref, dpb_ref, dga_ref, dgb_ref, dya_ref, dyb_ref, dh2b_ref, ca_ref, cb_ref):
        d16 = dh2_ref[...].astype(BF16)
        dh2b_ref[...] = d16
        dm = _dot_nt(d16, wo_ref[...])
        sa = _sigmoid(ga_ref[...].astype(F32))
        sb = _sigmoid(gb_ref[...].astype(F32))
        dpa = (dm * sa).astype(BF16)
        dpb = (dm * sb).astype(BF16)
        dpa_ref[...] = dpa
        dpb_ref[...] = dpb
        dga_ref[...] = (dm * pa_ref[...].astype(F32) * sa * (1.0 - sa)).astype(BF16)
        dgb_ref[...] = (dm * pb_ref[...].astype(F32) * sb * (1.0 - sb)).astype(BF16)
        dya = _dot(dpa, wa_ref[...])
        dyb = _dot(dpb, wb_ref[...])
        dya_ref[0, 0] = dya[:, :128]
        dya_ref[0, 1] = dya[:, 128:]
        dyb_ref[...] = dyb.astype(BF16)
        ca = _head_sums(dya * y_ref[...].astype(F32))
        ca_ref[0, 0] = ca[:, :128]
        ca_ref[0, 1] = ca[:, 128:]
        cb_ref[...] = _head_sums(dyb * yb_ref[...].astype(F32))

    row = lambda w: pl.BlockSpec((TM, w), lambda i: (i, 0))
    full = lambda a: pl.BlockSpec(a.shape, lambda i: (0, 0))
    gate = lambda cb: pl.BlockSpec((TM, D_MODEL), lambda i: (i, cb))
    bf = lambda w: jax.ShapeDtypeStruct((t, w), BF16)
    lanes = jax.ShapeDtypeStruct((nseq, 2, t // nseq, 128), F32)
    lane_spec = _lane_blocks(nseq, t // nseq, 2)
    return _pallas(
        body, (dh2, pa, pb, zg, zg, y, yb, wat, wbt, wout), name="merge_bwd", grid=(t // TM,),
        out_shape=(bf(D_MODEL), bf(D_MODEL), bf(D_MODEL), bf(D_MODEL), lanes, bf(2 * GW), bf(D_MODEL),
                   lanes, jax.ShapeDtypeStruct((t, 2 * GW), F32)),
        in_specs=[row(D_MODEL), row(D_MODEL), row(D_MODEL), gate(0), gate(1), row(GW), row(2 * GW),
                  full(wat), full(wbt), full(wout)],
        out_specs=(row(D_MODEL), row(D_MODEL), row(D_MODEL), row(D_MODEL), lane_spec, row(2 * GW), row(D_MODEL),
                   lane_spec, row(2 * GW)),
        sem=("parallel",), vmem=VMEM_BIG, rider=rider)


def _lane_head(rows):
    return lax.broadcasted_iota(jnp.int32, (rows, GW), 1) // HEAD_DIM


def _kv_expand_matrix(r):
    ci = lax.broadcasted_iota(jnp.int32, (2 * HEAD_DIM, GW), 0)
    ji = lax.broadcasted_iota(jnp.int32, (2 * HEAD_DIM, GW), 1)
    return (ci == (ji % HEAD_DIM) + HEAD_DIM * r).astype(BF16)


def _block_rows(row0, stride, ib):
    start = row0 + (stride * BLOCK) * ib
    if stride > 1:
        return pl.ds(start, BLOCK, stride=stride)
    return pl.ds(pl.multiple_of(start, BLOCK), BLOCK)


def _stack_heads(x, lane_head):
    return jnp.concatenate([jnp.where(lane_head == h, x, jnp.zeros_like(x)) for h in range(4)], axis=0)


def _unstack_heads(x4, lane_head):
    out = jnp.zeros((BLOCK, GW), F32)
    for h in range(4):
        out = jnp.where(lane_head == h, x4[h * BLOCK:(h + 1) * BLOCK], out)
    return out


def _load_rows(ref, rows, split):
    if split:
        return jnp.concatenate([ref[0, 0, rows, :], ref[0, 1, rows, :]], axis=1)
    return ref[0, rows, :]


def _store_rows(ref, rows, val, split):
    if split:
        ref[0, 0, rows, :] = val[:, :128]
        ref[0, 1, rows, :] = val[:, 128:]
    else:
        ref[0, rows, :] = val


def _attn_fwd(q_arr, k_arr, v_arr, bias, sink, *, grid, seq, stride, kvw, split, q_spec, k_spec, v_spec, bias_map,
              sink_map, o_spec, has_sink, o_shape, o_dtype, name, rider=None):
    nb = seq // stride // BLOCK
    scale = HEAD_DIM ** -0.5
    expanded = kvw != GW
    rps = min(stride, RESIDUES_PER_STEP)
    grid = (grid[0], grid[1] // rps)
    assert not has_sink or B_WINDOW - 1 < BLOCK

    def body(q_ref, k_ref, v_ref, bias_ref, sink_ref, o_ref, lse_ref, *kv_x):
        rr = pl.program_id(1)
        lane_head = _lane_head(BLOCK)
        if expanded:
            expand = _kv_expand_matrix(rr)
            kv_x[0][...] = _dot(k_ref[0], expand).astype(BF16)
            kv_x[1][...] = _dot(v_ref[0], expand).astype(BF16)
        for j in range(rps):
            residue(rr * rps + j if stride > 1 else 0, q_ref, k_ref, v_ref, bias_ref, sink_ref, o_ref, lse_ref, kv_x,
                    lane_head)

    def residue(row0, q_ref, k_ref, v_ref, bias_ref, sink_ref, o_ref, lse_ref, kv_x, lane_head):
        def per_head(fn, x):
            return jnp.concatenate([fn(sink_ref[0, h:h + 1, 0:1], x[h * BLOCK:(h + 1) * BLOCK]) for h in range(4)],
                                   axis=0)

        def load(ref, ib):
            return _load_rows(ref, _block_rows(row0, stride, ib), split).astype(BF16)

        def load_kv(which, ib):
            if expanded:
                return kv_x[which][_block_rows(0, 1, ib), :]
            return load((k_ref, v_ref)[which], ib)

        def block(ib, first):
            q4 = _stack_heads(load(q_ref, ib), lane_head)
            if first:
                kc, vc = load_kv(0, ib), load_kv(1, ib)
                b4 = bias_ref[:, :, BLOCK:].reshape(4 * BLOCK, BLOCK)
            else:
                kc = jnp.concatenate([load_kv(0, ib - 1), load_kv(0, ib)], axis=0)
                vc = jnp.concatenate([load_kv(1, ib - 1), load_kv(1, ib)], axis=0)
                b4 = bias_ref[...].reshape(4 * BLOCK, 2 * BLOCK)
                if has_sink:
                    oldest = lax.broadcasted_iota(jnp.int32, kc.shape, 0) == 0
                    kc = jnp.where(oldest, jnp.zeros_like(kc), kc)
                    vc = jnp.where(oldest, jnp.zeros_like(vc), vc)
            s = _dot_nt(q4, kc) * scale + b4
            m = jnp.max(s, axis=-1, keepdims=True)
            if has_sink and first:
                m = per_head(jnp.maximum, m)
            p = jnp.exp(s - m)
            l = jnp.sum(p, axis=-1, keepdims=True)
            if has_sink and first:
                l = l + per_head(lambda sk, mh: jnp.exp(sk - mh), m)
            o4 = _dot(p.astype(BF16), vc) / l
            rows = _block_rows(row0, stride, ib)
            _store_rows(o_ref, rows, _unstack_heads(o4, lane_head).astype(o_dtype), split)
            _store_rows(lse_ref, rows, _unstack_heads(m + jnp.log(l), lane_head), split)

        block(0, True)
        if nb > 1:
            def step(i, carry):
                block(i, False)
                return carry
            lax.fori_loop(1, nb, step, 0, unroll=min(ATTN_UNROLL, nb - 1))

    return _pallas(
        body, (q_arr, k_arr, v_arr, bias, sink), name=name, grid=grid,
        out_shape=(jax.ShapeDtypeStruct(o_shape, o_dtype), jax.ShapeDtypeStruct(o_shape, F32)),
        in_specs=[q_spec, k_spec, v_spec,
                  pl.BlockSpec((4, BLOCK, 2 * BLOCK), bias_map), pl.BlockSpec((1, 4, 128), sink_map)],
        out_specs=(o_spec, o_spec),
        scratch_shapes=[pltpu.VMEM((seq, GW), BF16)] * 2 if expanded else [],
        sem=("arbitrary", "arbitrary"), vmem=VMEM_BIG, rider=rider)


def _attn_bwd(q_arr, k_arr, v_arr, bias, sink, dy, cc, lse, *, grid, seq, stride, kvw, split, q_spec, k_spec, v_spec,
              bias_map, sink_map, o_spec, kv_out_spec, has_sink, n_bias, dq_shape, dkv_shape, g_dtype, name):
    ln = seq // stride
    nb = ln // BLOCK
    scale = HEAD_DIM ** -0.5
    expanded = kvw != GW
    rps = min(stride, RESIDUES_PER_STEP)
    grid = (grid[0], grid[1] // rps)

    def body(q_ref, k_ref, v_ref, bias_ref, sink_ref, dy_ref, c_ref, lse_ref,
             dq_ref, dk_ref, dv_ref, db_ref, dsk_ref, dk_acc, dv_acc, dk_half, dv_half, *kv_x):
        rr = pl.program_id(1)

        @pl.when((pl.program_id(0) == 0) & (rr == 0))
        def _():
            db_ref[...] = jnp.zeros_like(db_ref)
            dsk_ref[...] = jnp.zeros_like(dsk_ref)

        if expanded:
            expand = _kv_expand_matrix(rr)
            kv_x[0][...] = _dot(k_ref[0], expand).astype(BF16)
            kv_x[1][...] = _dot(v_ref[0], expand).astype(BF16)
        refs = (q_ref, k_ref, v_ref, bias_ref, sink_ref, dy_ref, c_ref, lse_ref, dq_ref, dk_ref, dv_ref, db_ref,
                dsk_ref, dk_acc, dv_acc, dk_half, dv_half, kv_x)
        for j in range(rps):
            residue(rr, rr * rps + j if stride > 1 else 0, *refs)

    def residue(rr, row0, q_ref, k_ref, v_ref, bias_ref, sink_ref, dy_ref, c_ref, lse_ref,
                dq_ref, dk_ref, dv_ref, db_ref, dsk_ref, dk_acc, dv_acc, dk_half, dv_half, kv_x):
        dk_acc[...] = jnp.zeros_like(dk_acc)
        dv_acc[...] = jnp.zeros_like(dv_acc)
        lane_head = _lane_head(BLOCK)
        hb = 4 * rr if n_bias == 8 else 0

        def load(ref, ib):
            return _load_rows(ref, _block_rows(row0, stride, ib), split)

        def load_kv(which, ib):
            if expanded:
                return kv_x[which][_block_rows(0, 1, ib), :]
            return load((k_ref, v_ref)[which], ib).astype(BF16)

        def head_col(x):
            return jnp.concatenate([x[:, h * HEAD_DIM:h * HEAD_DIM + 1] for h in range(4)], axis=0)

        def block(ib, first):
            q4 = _stack_heads(load(q_ref, ib).astype(BF16), lane_head)
            dy4 = _stack_heads(load(dy_ref, ib).astype(BF16), lane_head)
            c4 = head_col(load(c_ref, ib))
            l4 = head_col(load(lse_ref, ib))
            if first:
                kc, vc = load_kv(0, ib), load_kv(1, ib)
                b4 = bias_ref[:, :, BLOCK:].reshape(4 * BLOCK, BLOCK)
                krows = pl.ds(0, BLOCK)
            else:
                kc = jnp.concatenate([load_kv(0, ib - 1), load_kv(0, ib)], axis=0)
                vc = jnp.concatenate([load_kv(1, ib - 1), load_kv(1, ib)], axis=0)
                b4 = bias_ref[...].reshape(4 * BLOCK, 2 * BLOCK)
                krows = pl.ds(pl.multiple_of((ib - 1) * BLOCK, BLOCK), 2 * BLOCK)
            nk = BLOCK if first else 2 * BLOCK
            p = jnp.exp(_dot_nt(q4, kc) * scale + b4 - l4)
            ds = p * (_dot_nt(dy4, vc) - c4)
            ds3 = ds.reshape(4, BLOCK, nk)
            if n_bias == 8:
                if first:
                    db_ref[pl.ds(hb, 4), :, BLOCK:] += ds3
                else:
                    db_ref[pl.ds(hb, 4)] += ds3
            elif first:
                db_ref[:, :, BLOCK:] += ds3
            else:
                db_ref[...] += ds3
            ds16 = ds.astype(BF16)
            dq = _unstack_heads(_dot(ds16, kc), lane_head) * scale
            _store_rows(dq_ref, _block_rows(row0, stride, ib), dq.astype(g_dtype), split)
            dk_acc[krows, :] += _dot_tn(ds16, q4) * scale
            dv_acc[krows, :] += _dot_tn(p.astype(BF16), dy4)
            if has_sink:
                for h in range(4):
                    hs = slice(h * BLOCK, (h + 1) * BLOCK)
                    sk = sink_ref[0, h:h + 1, 0:1]
                    val = -jnp.sum(jnp.exp(sk - l4[hs]) * c4[hs], axis=0, keepdims=True)
                    dsk_ref[hb + h] += jnp.broadcast_to(val, (8, 128))

        block(0, True)
        if nb > 1:
            def step(i, carry):
                block(i, False)
                return carry
            lax.fori_loop(1, nb, step, 0, unroll=min(ATTN_UNROLL, nb - 1))

        if kvw == GW:
            all_rows = pl.ds(row0, ln, stride=stride) if stride > 1 else pl.ds(0, ln)
            _store_rows(dk_ref, all_rows, dk_acc[...].astype(g_dtype), split)
            _store_rows(dv_ref, all_rows, dv_acc[...].astype(g_dtype), split)
        else:
            def fold(acc):
                t2 = acc[:, :2 * HEAD_DIM] + acc[:, 2 * HEAD_DIM:]
                t2 = t2 + pltpu.roll(t2, HEAD_DIM, 1)
                lane = lax.broadcasted_iota(jnp.int32, t2.shape, 1) // HEAD_DIM
                return jnp.where(lane == rr, t2, 0.0)

            @pl.when(rr == 0)
            def _():
                dk_half[...] = fold(dk_acc[...])
                dv_half[...] = fold(dv_acc[...])

            @pl.when(rr == 1)
            def _():
                dk_ref[0] = (dk_half[...] + fold(dk_acc[...])).astype(g_dtype)
                dv_ref[0] = (dv_half[...] + fold(dv_acc[...])).astype(g_dtype)

    return pl.pallas_call(
        body, name=name, grid=grid,
        out_shape=(jax.ShapeDtypeStruct(dq_shape, g_dtype), jax.ShapeDtypeStruct(dkv_shape, g_dtype),
                   jax.ShapeDtypeStruct(dkv_shape, g_dtype), jax.ShapeDtypeStruct((n_bias, BLOCK, 2 * BLOCK), F32),
                   jax.ShapeDtypeStruct((8, 8, 128), F32)),
        in_specs=[q_spec, k_spec, v_spec,
                  pl.BlockSpec((4, BLOCK, 2 * BLOCK), bias_map), pl.BlockSpec((1, 4, 128), sink_map),
                  o_spec, o_spec, o_spec],
        out_specs=(o_spec, kv_out_spec, kv_out_spec,
                   pl.BlockSpec((n_bias, BLOCK, 2 * BLOCK), lambda n, r: (0, 0, 0)),
                   pl.BlockSpec((8, 8, 128), lambda n, r: (0, 0, 0))),
        scratch_shapes=[pltpu.VMEM((ln, GW), F32), pltpu.VMEM((ln, GW), F32),
                        pltpu.VMEM((ln, 2 * HEAD_DIM), F32), pltpu.VMEM((ln, 2 * HEAD_DIM), F32)]
        + ([pltpu.VMEM((seq, GW), BF16)] * 2 if expanded else []),
        compiler_params=_params(("arbitrary", "arbitrary"), VMEM_BIG),
    )(q_arr, k_arr, v_arr, bias, sink, dy, cc, lse)


def _bias_grad(ds_all, buckets):
    def body(ds_ref, bk_ref, o_ref):
        rows = lax.broadcasted_iota(jnp.int32, (N_BUCKETS, 128), 0)
        cols = lax.broadcasted_iota(jnp.int32, (N_BUCKETS, 128), 1)

        def per_bucket(b, acc):
            for h in range(20):
                gi = h // 4 if h < 12 else 3
                v = jnp.where(bk_ref[gi] == b, ds_ref[h], 0.0)
                v = jnp.sum(jnp.sum(v, axis=1, keepdims=True), axis=0, keepdims=True)
                acc = jnp.where((rows == b) & (cols == h), v, acc)
            return acc

        o_ref[...] = lax.fori_loop(0, N_BUCKETS, per_bucket, jnp.zeros((N_BUCKETS, 128), F32))

    vm = pl.BlockSpec(memory_space=pltpu.VMEM)
    return pl.pallas_call(body, name="bias_grad", out_shape=jax.ShapeDtypeStruct((N_BUCKETS, 128), F32),
                          in_specs=[vm, vm], out_specs=vm)(ds_all, buckets)


def _adamw(w, g, m, v, name):
    (res,), _ = _adamw_many([(w, g, m, v)], name)
    return res


def _adamw_many(tensors, name, rider=None):
    n = len(tensors)
    r, c = tensors[0][0].shape
    tr = r
    for cand in (256, 176, 128, 88, 64, 32, 16, 8):
        if r % cand == 0 and cand * c * 4 * 7 * n * 2 <= 24 * 1024 * 1024:
            tr = cand
            break

    def body(*refs):
        ins, outs = refs[:4 * n], refs[4 * n:]
        for i in range(n):
            w_ref, g_ref, m_ref, v_ref = ins[4 * i:4 * i + 4]
            d, nm, nv = _adam_update(w_ref[...], g_ref[...], m_ref[...], v_ref[...])
            outs[3 * i][...], outs[3 * i + 1][...], outs[3 * i + 2][...] = d, nm, nv

    spec = pl.BlockSpec((tr, c), lambda i: (i, 0))
    shp = jax.ShapeDtypeStruct((r, c), F32)
    res, ro = _pallas(body, tuple(a for t4 in tensors for a in t4), name=name, grid=(r // tr,),
                      out_shape=(shp,) * (3 * n), in_specs=[spec] * (4 * n), out_specs=(spec,) * (3 * n),
                      sem=("parallel",), vmem=VMEM_BIG, rider=rider)
    return [tuple(res[3 * i:3 * i + 3]) for i in range(n)], ro


def _t5_bucket(dist):
    max_exact = N_BUCKETS // 2
    n = jnp.maximum(dist, 0)
    nf = jnp.maximum(n, 1).astype(F32)
    large = max_exact + (jnp.log(nf / max_exact) / math.log(MAX_DISTANCE / max_exact)
                         * (N_BUCKETS - max_exact)).astype(jnp.int32)
    large = jnp.minimum(large, N_BUCKETS - 1)
    return jnp.where(n < max_exact, n, large)


def _bias_tables(rel_bias):
    qi = jnp.arange(BLOCK)[:, None]
    ki = jnp.arange(2 * BLOCK)[None, :]
    dist = qi + BLOCK - ki
    specs = [(d, w // d, 4 * gi, 4 * gi + 4) for gi, (w, d) in enumerate(DIL_GROUPS)] + [(1, B_WINDOW - 1, 12, 20)]
    biases, buckets = [], []
    for stride, steps, h0, h1 in specs:
        valid = (dist >= 0) & (dist <= steps)
        bk = jnp.where(valid, _t5_bucket(dist * stride), -1).astype(jnp.int32)
        onehot = (bk[None, :, :] == jnp.arange(N_BUCKETS, dtype=jnp.int32)[:, None, None]).astype(F32)
        b = jnp.einsum("bqk,bh->hqk", onehot, rel_bias[:, h0:h1], precision=lax.Precision.HIGHEST)
        biases.append(jnp.where(valid[None], b, NEG))
        buckets.append(bk)
    return jnp.concatenate(biases, axis=0), jnp.stack(buckets, axis=0)


def _local_step(x, tgt, W, S, shards=None, tail_host=None):
    nseq, seq, _ = x.shape
    t = nseq * seq
    xf = x.reshape(t, D_MODEL)
    bias_all, buckets = _bias_tables(S["rel_bias"])
    sink_b = jnp.broadcast_to(S["sinks"].reshape(2, 4, 1), (2, 4, 128)).astype(F32)
    sink_0 = jnp.zeros((1, 4, 128), F32)
    dist = shards is not None
    W = dict(W)
    G, GS, reduced = {}, {}, {}

    def put(keys, gathered):
        for k, g in zip(keys, gathered):
            W[k] = g.reshape(_FULL_SHAPE.get(k, (N_CHIPS * shards[k].shape[0], D_MODEL)))

    def gather_rider(keys):
        return _GatherRider([shards[k] for k in keys]) if dist else None

    def pair(keys):
        return _pair_reduce([G[k].reshape(N_CHIPS, 2, shards[k].shape[0] // 2, D_MODEL) for k in keys],
                            "grad_pair_reduce_" + keys[0])

    def finish(keys, own, rec):
        full = _final_reduce(own, rec, "grad_final_reduce_" + keys[0])
        off = 0
        for k in keys:
            r = shards[k].shape[0]
            reduced[k] = full[:, off:off + r // 2].reshape(r, D_MODEL)
            off += r // 2

    if dist:
        first = ("wgt1", "wut1", "wd1")
        put(first, _gather_rows([shards[k] for k in first]))
    keys = ("wint",)
    (h1, n1, g1, u1, a1), ro = _ffn_fwd(xf, S["ffn1_norm"], W["wgt1"], W["wut1"], W["wd1"], rider=gather_rider(keys))
    put(keys, ro)
    keys = ("wout", "wat", "wbt", "wgt2")
    (un, za, zb, zg), ro = _inproj_fwd(h1, S["mix_norm"], W["wint"], S["b_in"], nseq, rider=gather_rider(keys))
    put(keys, ro)

    seq3 = lambda a: a.reshape(nseq, seq, a.shape[-1])
    zb3 = seq3(zb)
    pair_blk = lambda cb: pl.BlockSpec((1, 2, seq, 128), lambda n, r, cb=cb: (n, cb, 0, 0))
    a_cfg = []
    outs, lses = [], []
    for gi, (_, d) in enumerate(DIL_GROUPS):
        cfg = dict(grid=(nseq, d), seq=seq, stride=d, kvw=GW, split=True,
                   q_spec=pair_blk(gi), k_spec=pair_blk(3 + gi), v_spec=pair_blk(6 + gi), o_spec=pair_blk(0),
                   bias_map=lambda n, r: (0, 0, 0), sink_map=lambda n, r: (0, 0, 0), has_sink=False)
        a_cfg.append(cfg)
        (o, lse), _ = _attn_fwd(za, za, za, bias_all[4 * gi:4 * gi + 4], sink_0, o_shape=(nseq, 2, seq, 128),
                                o_dtype=F32, name=f"attn_a{gi}_fwd", **cfg)
        outs.append(o)
        lses.append(lse)
    wide_blk = lambda w, cmap: pl.BlockSpec((1, seq, w), cmap)
    b_cfg = dict(grid=(nseq, 2), seq=seq, stride=1, kvw=2 * HEAD_DIM, split=False,
                 q_spec=wide_blk(GW, lambda n, r: (n, 0, r)), k_spec=wide_blk(2 * HEAD_DIM, lambda n, r: (n, 0, 4)),
                 v_spec=wide_blk(2 * HEAD_DIM, lambda n, r: (n, 0, 5)), o_spec=wide_blk(GW, lambda n, r: (n, 0, r)),
                 bias_map=lambda n, r: (r, 0, 0), sink_map=lambda n, r: (r, 0, 0), has_sink=True)
    keys = ("wut2",)
    bias_b_fwd = bias_all[12:20].at[:, :, 0].set(jnp.broadcast_to(S["sinks"].reshape(8, 1), (8, BLOCK)))
    (yb, lse_b), ro = _attn_fwd(zb3, zb3, zb3, bias_b_fwd, sink_b, o_shape=(nseq, seq, 2 * GW), o_dtype=BF16,
                                name="attn_b_fwd", rider=gather_rider(keys), **b_cfg)
    put(keys, ro)
    yb = yb.reshape(t, 2 * GW)

    keys = ("wd2",)
    (h2, y, lse_tot, pa, pb, merged), ro = _merge_fwd(outs[0], outs[1], outs[2], lses[0], lses[1], lses[2], yb, zg, h1,
                                                      W["wat"], W["wbt"], W["wout"], rider=gather_rider(keys))
    put(keys, ro)
    (dh3, n2, g2, u2, a2, loss_part, g_final), _ = _ffn_fwd(
        h2, S["ffn2_norm"], W["wgt2"], W["wut2"], W["wd2"],
        head=(S["final_norm"].reshape(1, D_MODEL), tgt.reshape(t, D_MODEL)))

    GS["final_norm"] = g_final
    dh2, dg2, du2, df2, GS["ffn2_norm"] = _ffn_bwd(dh3, h2, S["ffn2_norm"], g2, u2, W["wgt2"], W["wut2"], W["wd2"])
    G["wgt2"] = _wgrad(dg2, n2, MXU_DIM, name="wgrad_gate2")
    G["wut2"] = _wgrad(du2, n2, MXU_DIM, name="wgrad_up2")
    G["wd2"] = _wgrad(a2, df2, MXU_DIM, name="wgrad_down2")

    keys = ("wgt2", "wut2", "wd2")
    rider = _ExchangeRider([pair(keys)]) if dist else None
    (dpa, dpb, dga, dgb, dya, dyb, dh2b, ca, cb), ro = _merge_bwd(dh2, pa, pb, zg, y, yb, W["wat"], W["wbt"], W["wout"],
                                                                  nseq, rider=rider)
    if dist:
        finish(keys, *ro)

    dqs, dks, dvs, dbs = [], [], [], []
    shp = (nseq, 2, seq, 128)
    halves = lambda a: [a[:, hf].reshape(t, 128).astype(BF16) for hf in range(2)]
    for gi in range(len(DIL_GROUPS)):
        dq, dk, dv, db, _ = _attn_bwd(za, za, za, bias_all[4 * gi:4 * gi + 4], sink_0, dya, ca, lse_tot,
                                      n_bias=4, dq_shape=shp, dkv_shape=shp, g_dtype=F32,
                                      kv_out_spec=a_cfg[gi]["o_spec"], name=f"attn_a{gi}_bwd", **a_cfg[gi])
        dqs += halves(dq)
        dks += halves(dk)
        dvs += halves(dv)
        dbs.append(db)
    dqb, dkb, dvb, dbb, dsink = _attn_bwd(zb3, zb3, zb3, bias_all[12:20], sink_b, seq3(dyb), seq3(cb), lse_b,
                                          n_bias=8, dq_shape=(nseq, seq, 2 * GW),
                                          dkv_shape=(nseq, seq, 2 * HEAD_DIM), g_dtype=BF16,
                                          kv_out_spec=wide_blk(2 * HEAD_DIM, lambda n, r: (n, 0, 0)),
                                          name="attn_b_bwd", **b_cfg)
    dz = jnp.concatenate(dqs + dks + dvs + [dqb.reshape(t, 2 * GW), dkb.reshape(t, 2 * HEAD_DIM),
                                            dvb.reshape(t, 2 * HEAD_DIM), dga, dgb], axis=-1)
    gb_tab = _bias_grad(jnp.concatenate(dbs + [dbb], axis=0), buckets)
    if dist:
        GS["bias_tab"], GS["sink_tiles"] = gb_tab, dsink
    else:
        GS["rel_bias"] = gb_tab[:, :20]
        GS["sinks"] = dsink[:, 0, 0].reshape(1, 8)

    G["wint"], GS["b_in"] = _wgrad(dz, un, MXU_DIM, with_colsum=True, name="wgrad_in")
    G["wout"] = _wgrad(merged, dh2b, MXU_DIM, name="wgrad_out")
    G["wat"] = _wgrad(dpa, y, MXU_DIM, name="wgrad_branch_a")
    G["wbt"] = _wgrad(dpb, yb, MXU_DIM, name="wgrad_branch_b")
    keys = ("wint", "wout", "wat", "wbt")
    rider = _ExchangeRider([pair(keys)]) if dist else None
    (dh1, GS["mix_norm"]), ro = _inproj_bwd(dz, dh2, h1, S["mix_norm"], W["wint"], rider=rider)
    if dist:
        finish(keys, *ro)

    dx, dg1, du1, df1, GS["ffn1_norm"] = _ffn_bwd(dh1, xf, S["ffn1_norm"], g1, u1, W["wgt1"], W["wut1"], W["wd1"])
    G["wgt1"] = _wgrad(dg1, n1, MXU_DIM, name="wgrad_gate1")
    if dist:
        G["wut1"], ro = _wgrad(du1, n1, MXU_DIM, name="wgrad_up1", rider=_ExchangeRider([pair(("wgt1",))]))
        finish(("wgt1",), *ro)
        G["wd1"], ro = _wgrad(a1, df1, MXU_DIM, name="wgrad_down1", rider=_ExchangeRider([pair(("wut1",))]))
        finish(("wut1",), *ro)
        finish(("wd1",), *tail_host(_ExchangeRider([pair(("wd1",))]), reduced))
    else:
        G["wut1"] = _wgrad(du1, n1, MXU_DIM, name="wgrad_up1")
        G["wd1"] = _wgrad(a1, df1, MXU_DIM, name="wgrad_down1")
    return loss_part, dx.reshape(x.shape), (reduced if dist else G), GS


_SMALL = ("ffn1_norm", "mix_norm", "ffn2_norm", "final_norm", "b_in", "sinks", "rel_bias")
_ORDER = ("ffn1_norm", "ffn1_w_gate", "ffn1_w_up", "ffn1_w_down", "mix_norm", "w_in", "b_in", "w_branch_a",
          "w_branch_b", "w_out", "sinks", "rel_bias", "ffn2_norm", "ffn2_w_gate", "ffn2_w_up", "ffn2_w_down",
          "final_norm")
_BIG = (("wgt1", "ffn1_w_gate", True, 704), ("wut1", "ffn1_w_up", True, 704), ("wd1", "ffn1_w_down", False, 704),
        ("wint", "w_in", True, 1280), ("wout", "w_out", False, 256), ("wat", "w_branch_a", True, 64),
        ("wbt", "w_branch_b", True, 128), ("wgt2", "ffn2_w_gate", True, 704), ("wut2", "ffn2_w_up", True, 704),
        ("wd2", "ffn2_w_down", False, 704))
_FULL_SHAPE = {"wat": (D_MODEL, GW), "wbt": (D_MODEL, 2 * GW)}


def kernel(x, ffn1_norm, ffn1_w_gate, ffn1_w_up, ffn1_w_down, mix_norm, w_in, b_in, w_branch_a, w_branch_b, w_out, sinks, rel_bias, ffn2_norm, ffn2_w_gate, ffn2_w_up, ffn2_w_down, final_norm, loss_target, m_ffn1_norm, m_ffn1_w_gate, m_ffn1_w_up, m_ffn1_w_down, m_mix_norm, m_w_in, m_b_in, m_w_branch_a, m_w_branch_b, m_w_out, m_sinks, m_rel_bias, m_ffn2_norm, m_ffn2_w_gate, m_ffn2_w_up, m_ffn2_w_down, m_final_norm, v_ffn1_norm, v_ffn1_w_gate, v_ffn1_w_up, v_ffn1_w_down, v_mix_norm, v_w_in, v_b_in, v_w_branch_a, v_w_branch_b, v_w_out, v_sinks, v_rel_bias, v_ffn2_norm, v_ffn2_w_gate, v_ffn2_w_up, v_ffn2_w_down, v_final_norm):
    args = dict(locals())
    w = {n: args[n] for n in _ORDER}
    m = {n: args["m_" + n] for n in _ORDER}
    v = {n: args["v_" + n] for n in _ORDER}

    shards = {}
    for key, name, transposed, rows in _BIG:
        a = w[name][0]
        a = (a.T if transposed else a).astype(BF16)
        shards[key] = a.reshape(rows, D_MODEL)
    S = {n: w[n] for n in _SMALL}

    row_adam = lambda n: (w[n][0].T, m[n][0].T, v[n][0].T)
    early = {}

    def tail_host(rider, reduced):
        tensors = []
        for key, n in (("wgt2", "ffn2_w_gate"), ("wut2", "ffn2_w_up"), ("wd2", "ffn2_w_down")):
            wmv = row_adam(n) if key != "wd2" else (w[n][0], m[n][0], v[n][0])
            tensors.append((wmv[0], reduced[key], wmv[1], wmv[2]))
        res, ro = _adamw_many(tensors, "adamw_ffn2", rider=rider)
        early["ffn2_w_gate"], early["ffn2_w_up"], early["ffn2_w_down"] = res
        return ro

    loss_part, grad_x, reduced, GS = _local_step(x, loss_target, {}, S, shards, tail_host)

    small = _allreduce_small(GS["ffn1_norm"], GS["mix_norm"], GS["ffn2_norm"], GS["final_norm"], GS["b_in"],
                             GS["sink_tiles"], GS["bias_tab"], loss_part)
    loss = small[9, 8]

    out_g, out_d, out_m, out_v = {}, {}, {}, {}
    for key, n, transposed, rows in _BIG:
        nat = w[n][0].shape
        if transposed and nat[1] % 128:
            res = early[n] if n in early else _adamw(row_adam(n)[0], reduced[key], *row_adam(n)[1:], "adamw_" + n)
            res = [reduced[key].T] + [r.T for r in res]
        elif n in early:
            res = [reduced[key]] + list(early[n])
        else:
            g = reduced[key].reshape(nat[1], nat[0]).T if transposed else reduced[key].reshape(nat)
            res = [g] + list(_adamw(w[n][0], g, m[n][0], v[n][0], "adamw_" + n))
        out_g[n], out_d[n], out_m[n], out_v[n] = [r[None] for r in res]
    row = lambda d: {n: (d[n].reshape(1, D_MODEL) if n == "final_norm" else d[n]) for n in _SMALL}
    for dst, src in zip((out_g, out_d, out_m, out_v), _adamw_small(small, row(w), row(m), row(v))):
        dst.update(src)
        dst["final_norm"] = src["final_norm"].reshape(D_MODEL)

    return (loss, grad_x, *[out_g[n] for n in _ORDER], *[out_d[n] for n in _ORDER],
            *[out_m[n] for n in _ORDER], *[out_v[n] for n in _ORDER])
```

```python
import math

import jax
import jax.numpy as jnp
from jax import lax
from jax.experimental import pallas as pl
from jax.experimental.pallas import tpu as pltpu

F32, BF16 = jnp.float32, jnp.bfloat16
MESH = pl.DeviceIdType.MESH

D_MODEL = 1024
D_FF = 2816
D_IN = 5120
HEAD_DIM = 64
BLOCK = 128
DIL_GROUPS = ((128, 1), (512, 4), (2048, 16))
B_WINDOW = 128
N_BUCKETS = 32
MAX_DISTANCE = 2048
EPS = 1e-6
N_CHIPS = 4
GW = 256
ZA_W = 2304
ZB_W = 768
NEG = -1e30

ADAM_LR, ADAM_B1, ADAM_B2, ADAM_EPS, ADAM_WD, ADAM_STEP = 0.001, 0.9, 0.999, 1e-08, 0.01, 10

VMEM_BIG = 56 * 1024 * 1024
TM = 512
TM_BWD = 256
MXU_DIM = 256
FF_BOUNDS = (0, 4 * MXU_DIM, 8 * MXU_DIM, D_FF)
DMA_SPLIT = 8
RESIDUES_PER_STEP = 16
ATTN_UNROLL = 15


def _dot(a, b):
    return jnp.dot(a, b, preferred_element_type=F32)


def _dot_nt(a, b):
    return lax.dot_general(a, b, (((1,), (1,)), ((), ())), preferred_element_type=F32)


def _dot_tn(a, b):
    return lax.dot_general(a, b, (((0,), (0,)), ((), ())), preferred_element_type=F32)


def _sigmoid(x):
    return 0.5 * jnp.tanh(0.5 * x) + 0.5


def _params(sem, vmem=None):
    return pltpu.CompilerParams(dimension_semantics=sem, vmem_limit_bytes=vmem)


ANY = pl.BlockSpec(memory_space=pl.ANY)


def _me():
    return lax.axis_index("x"), lax.axis_index("y"), lax.axis_index("c")


_CHIP_RELS = ((1, 0), (0, 1), (1, 1))


def _flip(v, f):
    return 1 - v if f else v


def _remote(src, dst, ssem, rsem, peer):
    return pltpu.make_async_remote_copy(src_ref=src, dst_ref=dst, send_sem=ssem, recv_sem=rsem,
                                        device_id=peer, device_id_type=MESH)


def _row_pieces(rows, n):
    step = max(16, -(-rows // n) // 16 * 16)
    out, s = [], 0
    while s < rows:
        out.append((s, min(step, rows - s)))
        s += step
    return out


def _gather_rows(shards):
    nt = len(shards)
    rows = [s.shape[0] for s in shards]

    def body(*refs):
        srcs, outs = refs[:nt], refs[nt:2 * nt]
        halves, quarters = refs[2 * nt:3 * nt], refs[3 * nt:4 * nt]
        ici_s, ici_r, fwd_s, fwd_r, d2d_s, d2d_r, keep, loc = refs[4 * nt:]
        x, y, c = _me()
        j = 2 * x + y
        sib = (x, y, 1 - c)
        nbr = ((1 - x, y, c), (x, 1 - y, c))
        nbr_j = (2 * (1 - x) + y, 2 * x + (1 - y))
        diag_j = 2 * (1 - x) + (1 - y)
        local = [pltpu.make_async_copy(srcs[t], outs[t].at[j], loc.at[t]) for t in range(nt)]
        for cp in local:
            cp.start()
        pending = []
        for a in range(2):
            for t in range(nt):
                half = pl.ds(c * (rows[t] // 2), rows[t] // 2)
                cp = _remote(srcs[t].at[half], halves[t].at[a], ici_s.at[2 * t + a], ici_r.at[2 * t + a], nbr[a])
                cp.start()
                pending.append(cp)
        placed = []

        def place(src, dst_of, idx):
            mine = pltpu.make_async_copy(src, dst_of, keep.at[idx])
            mine.start()
            cp = _remote(src, dst_of, d2d_s.at[idx], d2d_r.at[idx], sib)
            cp.start()
            placed.append((mine, cp))

        for a in range(2):
            for t in range(nt):
                r2, r4 = rows[t] // 2, rows[t] // 4
                got = halves[t].at[a]
                _remote(got, got, ici_s.at[2 * t + a], ici_r.at[2 * t + a], nbr[a]).wait_recv()
                cp = _remote(halves[t].at[a, pl.ds(a * r4, r4)], quarters[t].at[a], fwd_s.at[2 * t + a],
                             fwd_r.at[2 * t + a], nbr[1 - a])
                cp.start()
                pending.append(cp)
                place(got, outs[t].at[nbr_j[a], pl.ds(c * r2, r2)], 4 * t + a)
        for a in range(2):
            for t in range(nt):
                r2, r4 = rows[t] // 2, rows[t] // 4
                got = quarters[t].at[a]
                _remote(got, got, fwd_s.at[2 * t + a], fwd_r.at[2 * t + a], nbr[1 - a]).wait_recv()
                place(got, outs[t].at[diag_j, pl.ds(c * r2 + a * r4, r4)], 4 * t + 2 + a)
        for mine, cp in placed:
            mine.wait()
            cp.wait()
        for cp in pending:
            cp.wait_send()
        for cp in local:
            cp.wait()

    stage = ([pltpu.VMEM((2, r // 2, D_MODEL), BF16) for r in rows] + [pltpu.VMEM((2, r // 4, D_MODEL), BF16) for r in rows])
    sems = ([pltpu.SemaphoreType.DMA((2 * nt,)) for _ in range(4)] + [pltpu.SemaphoreType.DMA((4 * nt,))] * 3
            + [pltpu.SemaphoreType.DMA((nt,))])
    return pl.pallas_call(
        body, name="gather_weights",
        out_shape=tuple(jax.ShapeDtypeStruct((N_CHIPS,) + s.shape, s.dtype) for s in shards),
        in_specs=[pl.BlockSpec(memory_space=pltpu.VMEM)] * nt,
        out_specs=tuple([ANY] * nt), scratch_shapes=stage + sems,
    )(*shards)


VMEM_WHOLE = pl.BlockSpec(memory_space=pltpu.VMEM)


def _pair_reduce(grads, name):
    nt = len(grads)
    r2 = [g.shape[2] for g in grads]
    off = [sum(r2[:t]) for t in range(nt)]
    tot = sum(r2)

    def body(*refs):
        gs = refs[:nt]
        s_ref, mine, got, ssem, rsem, lsem = refs[nt:]
        x, y, c = _me()
        sib = (x, y, 1 - c)
        for t in range(nt):
            for k in range(N_CHIPS):
                rows = pl.ds(off[t], r2[t])
                _remote(gs[t].at[k, 1 - c], got.at[k, rows], ssem, rsem, sib).start()
                pltpu.make_async_copy(gs[t].at[k, c], mine.at[k, rows], lsem).start()
        pltpu.make_async_copy(mine, mine, lsem).wait()
        _remote(got, got, ssem, rsem, sib).wait()
        for k in range(N_CHIPS):
            for st, sz in _row_pieces(tot, 4):
                rows = slice(st, st + sz)
                s_ref[k, rows, :] = (mine[k, rows, :].astype(F32) + got[k, rows, :].astype(F32)).astype(BF16)

    shp = jax.ShapeDtypeStruct((N_CHIPS, tot, D_MODEL), BF16)
    buf = pltpu.VMEM((N_CHIPS, tot, D_MODEL), BF16)
    return pl.pallas_call(
        body, name=name, out_shape=shp, in_specs=[ANY] * nt, out_specs=VMEM_WHOLE,
        scratch_shapes=[buf, buf, pltpu.SemaphoreType.DMA(()), pltpu.SemaphoreType.DMA(()),
                        pltpu.SemaphoreType.DMA(())],
        compiler_params=pltpu.CompilerParams(vmem_limit_bytes=VMEM_BIG),
    )(*grads)


def _final_reduce(own, rec, name):
    r2 = own.shape[0]
    stages = _row_pieces(r2, 2)

    def body(own_hbm, rec_hbm, o_ref, parts, fbuf, ssem, rsem, lsem, insems):
        x, y, c = _me()
        sib = (x, y, 1 - c)
        for p, (st, sz) in enumerate(stages):
            rows = pl.ds(st, sz)
            pltpu.make_async_copy(own_hbm.at[rows], parts.at[0, rows], insems.at[p]).start()
            for k in range(3):
                pltpu.make_async_copy(rec_hbm.at[k, rows], parts.at[1 + k, rows], insems.at[p]).start()
        for p, (st, sz) in enumerate(stages):
            stage = parts.at[:, pl.ds(st, sz)]
            pltpu.make_async_copy(stage, stage, insems.at[p]).wait()
            for s0, ssz in _row_pieces(sz, DMA_SPLIT // 2):
                rows = slice(st + s0, st + s0 + ssz)
                fbuf[rows, :] = (parts[0, rows, :].astype(F32) + parts[1, rows, :].astype(F32)
                                 + parts[2, rows, :].astype(F32) + parts[3, rows, :].astype(F32))
                dst = o_ref.at[c, pl.ds(st + s0, ssz)]
                pltpu.make_async_copy(fbuf.at[pl.ds(st + s0, ssz)], dst, lsem).start()
                _remote(fbuf.at[pl.ds(st + s0, ssz)], dst, ssem, rsem, sib).start()
        _remote(fbuf, o_ref.at[c], ssem, rsem, sib).wait()
        pltpu.make_async_copy(fbuf, o_ref.at[c], lsem).wait()

    return pl.pallas_call(
        body, name=name, out_shape=jax.ShapeDtypeStruct((2, r2, D_MODEL), F32),
        in_specs=[ANY, ANY], out_specs=ANY,
        scratch_shapes=[pltpu.VMEM((4, r2, D_MODEL), BF16), pltpu.VMEM((r2, D_MODEL), F32),
                        pltpu.SemaphoreType.DMA(()), pltpu.SemaphoreType.DMA(()), pltpu.SemaphoreType.DMA(()),
                        pltpu.SemaphoreType.DMA((2,))],
        compiler_params=pltpu.CompilerParams(vmem_limit_bytes=VMEM_BIG),
    )(own, rec)


SMALL_ROWS = 48


def _allreduce_small(g_ffn1, g_mix, g_ffn2, g_final, g_bin, dsink, bias_tab, loss_part):
    def body(f1_ref, mx_ref, f2_ref, fn_ref, bi_ref, sk_ref, bt_ref, ls_ref, o_ref, mine, buf, tabs, send_sems,
             recv_sems):
        x, y, c = _me()
        me = 4 * x + 2 * y + c
        mine[...] = jnp.zeros_like(mine)
        for r, ref in enumerate((f1_ref, mx_ref, f2_ref, fn_ref)):
            mine[r:r + 1, :] = ref[...]
        for k in range(D_IN // D_MODEL):
            mine[4 + k:5 + k, :] = bi_ref[:, k * D_MODEL:(k + 1) * D_MODEL]
        lane = lax.broadcasted_iota(jnp.int32, (1, 128), 1)
        row = jnp.where(lane == 8, ls_ref[0:1, :], 0.0)
        for h in range(8):
            row = jnp.where(lane == h, sk_ref[h, 0:1, :], row)
        mine[9:10, 0:128] = row
        buf[me] = mine[...]
        tabs[me] = bt_ref[...]
        copies = []
        for k in range(1, 8):
            peer = (_flip(x, (k >> 2) & 1), _flip(y, (k >> 1) & 1), _flip(c, k & 1))
            for t, (src, dst) in enumerate(((mine, buf), (bt_ref, tabs))):
                cp = _remote(src, dst.at[me], send_sems.at[2 * (k - 1) + t], recv_sems.at[2 * (k - 1) + t], peer)
                cp.start()
                copies.append(cp)
        for cp in copies:
            cp.wait()
        acc, tab = buf[0], tabs[0]
        for i in range(1, 8):
            acc, tab = acc + buf[i], tab + tabs[i]
        o_ref[...] = jnp.zeros_like(o_ref)
        o_ref[0:16, :] = acc
        o_ref[16:48, 0:128] = tab

    vm = pl.BlockSpec(memory_space=pltpu.VMEM)
    return pl.pallas_call(
        body, name="allreduce_small", out_shape=jax.ShapeDtypeStruct((SMALL_ROWS, D_MODEL), F32),
        in_specs=[vm] * 8, out_specs=vm,
        scratch_shapes=[pltpu.VMEM((16, D_MODEL), F32), pltpu.VMEM((8, 16, D_MODEL), F32),
                        pltpu.VMEM((8, N_BUCKETS, 128), F32), pltpu.SemaphoreType.DMA((14,)),
                        pltpu.SemaphoreType.DMA((14,))],
    )(g_ffn1, g_mix, g_ffn2, g_final, g_bin, dsink, bias_tab, loss_part)


def _adam_update(w, g, m, v):
    nm = ADAM_B1 * m + (1.0 - ADAM_B1) * g
    nv = ADAM_B2 * v + (1.0 - ADAM_B2) * (g * g)
    bc1 = 1.0 - ADAM_B1 ** ADAM_STEP
    bc2 = 1.0 - ADAM_B2 ** ADAM_STEP
    return -ADAM_LR * ((nm / bc1) / (jnp.sqrt(nv / bc2) + ADAM_EPS) + ADAM_WD * w), nm, nv


def _adamw_small(packed, w, m, v):
    names = ("ffn1_norm", "mix_norm", "ffn2_norm", "final_norm", "b_in", "sinks", "rel_bias")
    nn = len(names)

    def grad_of(p_ref, name, k=0):
        if name == "b_in":
            return p_ref[4 + k:5 + k, :]
        if name == "sinks":
            return p_ref[9:10, 0:8]
        if name == "rel_bias":
            return p_ref[16:48, 0:20]
        r = names.index(name)
        return p_ref[r:r + 1, :]

    def body(p_ref, *refs):
        ws, ms, vs = refs[:nn], refs[nn:2 * nn], refs[2 * nn:3 * nn]
        outs = refs[3 * nn:]
        for i, name in enumerate(names):
            og, od, om, ov = outs[i], outs[nn + i], outs[2 * nn + i], outs[3 * nn + i]
            pieces = range(D_IN // D_MODEL) if name == "b_in" else (0,)
            for k in pieces:
                sl = (slice(None), slice(k * D_MODEL, (k + 1) * D_MODEL)) if name == "b_in" else (Ellipsis,)
                g = grad_of(p_ref, name, k)
                d, nm, nv = _adam_update(ws[i][sl], g, ms[i][sl], vs[i][sl])
                og[sl], od[sl], om[sl], ov[sl] = g, d, nm, nv

    vm = pl.BlockSpec(memory_space=pltpu.VMEM)
    shapes = [jax.ShapeDtypeStruct(w[n].shape, F32) for n in names]
    res = pl.pallas_call(
        body, name="adamw_small", out_shape=tuple(shapes * 4), in_specs=[vm] * (1 + 3 * nn),
        out_specs=tuple([vm] * (4 * nn)),
    )(packed, *[w[n] for n in names], *[m[n] for n in names], *[v[n] for n in names])
    return [dict(zip(names, res[i * nn:(i + 1) * nn])) for i in range(4)]


class _GatherRider:
    def __init__(self, shards):
        self.inputs = list(shards)
        nt = len(shards)
        self.out_shape = [jax.ShapeDtypeStruct((N_CHIPS,) + s.shape, s.dtype) for s in shards]
        self.scratch = [pltpu.SemaphoreType.DMA((3 * nt,)), pltpu.SemaphoreType.DMA((3 * nt,)),
                        pltpu.SemaphoreType.DMA((nt,))]

    def _copies(self, srcs, outs, sems):
        ici_s, ici_r, loc = sems
        x, y, c = _me()
        j = 2 * x + y
        local = [pltpu.make_async_copy(srcs[t], outs[t].at[j], loc.at[t]) for t in range(len(srcs))]
        remote = []
        for k, (fx, fy) in enumerate(_CHIP_RELS):
            peer = (_flip(x, fx), _flip(y, fy), c)
            for t in range(len(srcs)):
                remote.append(_remote(srcs[t], outs[t].at[j], ici_s.at[3 * t + k], ici_r.at[3 * t + k], peer))
        return local, remote

    def start(self, srcs, outs, sems):
        local, remote = self._copies(srcs, outs, sems)
        for cp in local + remote:
            cp.start()

    def finish(self, srcs, outs, sems):
        local, remote = self._copies(srcs, outs, sems)
        for cp in remote + local:
            cp.wait()


class _ExchangeRider:
    def __init__(self, parts):
        self.inputs = list(parts)
        self.r2 = [p.shape[1] for p in parts]
        self.off = [sum(self.r2[:g]) for g in range(len(parts))]
        tot = sum(self.r2)
        self.out_shape = [jax.ShapeDtypeStruct((tot, D_MODEL), BF16), jax.ShapeDtypeStruct((3, tot, D_MODEL), BF16)]
        self.scratch = [pltpu.SemaphoreType.DMA((3,)), pltpu.SemaphoreType.DMA((3,)), pltpu.SemaphoreType.DMA(())]

    def start(self, ps, outs, sems):
        own_ref, rec_ref = outs
        ssems, rsems, lsem = sems
        x, y, c = _me()
        j = 2 * x + y
        for g in range(len(ps)):
            pltpu.make_async_copy(ps[g].at[j], own_ref.at[pl.ds(self.off[g], self.r2[g])], lsem).start()
        for k, (fx, fy) in enumerate(_CHIP_RELS):
            px, py = _flip(x, fx), _flip(y, fy)
            for g in range(len(ps)):
                for st, sz in _row_pieces(self.r2[g], 2):
                    _remote(ps[g].at[2 * px + py, pl.ds(st, sz)], rec_ref.at[k, pl.ds(self.off[g] + st, sz)],
                            ssems.at[k], rsems.at[k], (px, py, c)).start()

    def finish(self, ps, outs, sems):
        own_ref, rec_ref = outs
        ssems, rsems, lsem = sems
        x, y, c = _me()
        for k in range(3):
            _remote(rec_ref.at[k], rec_ref.at[k], ssems.at[k], rsems.at[k], (x, y, c)).wait()
        pltpu.make_async_copy(own_ref, own_ref, lsem).wait()


def _pallas(body, args, *, name, grid, in_specs, out_specs, out_shape, scratch_shapes=(), sem=None, vmem=None,
            rider=None):
    if rider is None:
        res = pl.pallas_call(body, name=name, grid=grid, in_specs=list(in_specs), out_specs=tuple(out_specs),
                             out_shape=tuple(out_shape), scratch_shapes=list(scratch_shapes),
                             compiler_params=_params(sem, vmem))(*args)
        return tuple(res), ()
    n_in, n_out, n_sc = len(in_specs), len(out_shape), len(scratch_shapes)
    r_in, r_out = len(rider.inputs), len(rider.out_shape)

    def wrapped(*refs):
        ins, rins = refs[:n_in], refs[n_in:n_in + r_in]
        p = n_in + r_in
        outs, routs = refs[p:p + n_out], refs[p + n_out:p + n_out + r_out]
        p += n_out + r_out
        scr, rsems = refs[p:p + n_sc], refs[p + n_sc:]
        first = pl.program_id(0) == 0
        last = pl.program_id(0) == grid[0] - 1
        for a in range(1, len(grid)):
            first = first & (pl.program_id(a) == 0)
            last = last & (pl.program_id(a) == grid[a] - 1)

        @pl.when(first)
        def _():
            rider.start(rins, routs, rsems)

        body(*ins, *outs, *scr)

        @pl.when(last)
        def _():
            rider.finish(rins, routs, rsems)

    res = pl.pallas_call(
        wrapped, name=name, grid=grid, in_specs=list(in_specs) + [ANY] * r_in,
        out_specs=tuple(out_specs) + (ANY,) * r_out, out_shape=tuple(out_shape) + tuple(rider.out_shape),
        scratch_shapes=list(scratch_shapes) + rider.scratch,
        compiler_params=_params(("arbitrary",) * len(grid), vmem))(*args, *rider.inputs)
    return tuple(res[:n_out]), tuple(res[n_out:])


def _load_weights(pairs, sems):
    copies = [pltpu.make_async_copy(hbm, vmem, sems.at[i]) for i, (hbm, vmem) in enumerate(pairs)]
    for cp in copies:
        cp.start()
    for cp in copies:
        cp.wait()


def _loss_tile(hh, gain, tgt):
    r = lax.rsqrt(jnp.mean(hh * hh, axis=-1, keepdims=True) + EPS)
    hn = hh * r
    err = hn * gain - tgt
    part = (0.5 / D_MODEL) * jnp.sum(jnp.sum(err * err, axis=1, keepdims=True), axis=0, keepdims=True)
    dy = err * (1.0 / D_MODEL)
    dng = dy * gain
    dh = r * (dng - hn * jnp.mean(dng * hn, axis=-1, keepdims=True))
    return dh, part, jnp.sum(dy * hn, axis=0, keepdims=True)


def _ffn_fwd(h, gain, wgt, wut, wd, rider=None, head=None):
    t = h.shape[0]

    def body(h_ref, gain_ref, wg_hbm, wu_hbm, wd_hbm, *rest):
        if head is None:
            hout_ref, n_ref, g_ref, u_ref, a_ref, wg_v, wu_v, wd_v, wsem = rest
        else:
            fg_ref, tgt_ref, hout_ref, n_ref, g_ref, u_ref, a_ref, loss_ref, gg_ref, wg_v, wu_v, wd_v, wsem = rest
        @pl.when(pl.program_id(0) == 0)
        def _():
            _load_weights(((wg_hbm, wg_v), (wu_hbm, wu_v), (wd_hbm, wd_v)), wsem)
            if head is not None:
                loss_ref[...] = jnp.zeros_like(loss_ref)
                gg_ref[...] = jnp.zeros_like(gg_ref)

        hh = h_ref[...]
        r = lax.rsqrt(jnp.mean(hh * hh, axis=-1, keepdims=True) + EPS)
        n = (hh * r * gain_ref[...]).astype(BF16)
        n_ref[...] = n
        acc = jnp.zeros((TM, D_MODEL), F32)
        for c0, c1 in zip(FF_BOUNDS[:-1], FF_BOUNDS[1:]):
            sl = slice(c0, c1)
            g = _dot_nt(n, wg_v[sl, :])
            u = _dot_nt(n, wu_v[sl, :])
            sg = _sigmoid(g)
            silu = g * sg
            a = (silu * u).astype(BF16)
            a_ref[:, sl] = a
            g_ref[:, sl] = (u * (sg * (1.0 + g * (1.0 - sg)))).astype(BF16)
            u_ref[:, sl] = silu.astype(BF16)
            acc = acc + _dot(a, wd_v[sl, :])
        hout = hh + 0.5 * acc
        if head is None:
            hout_ref[...] = hout
        else:
            dh, part, gpart = _loss_tile(hout, fg_ref[...], tgt_ref[...])
            hout_ref[...] = dh
            loss_ref[...] += part
            gg_ref[...] += gpart

    row = lambda w: pl.BlockSpec((TM, w), lambda i: (i, 0))
    vec = pl.BlockSpec((1, D_MODEL), lambda i: (0, 0))
    wv = pltpu.VMEM((D_FF, D_MODEL), BF16)
    args, in_specs = (h, gain, wgt, wut, wd), [row(D_MODEL), vec, ANY, ANY, ANY]
    out_shape = [jax.ShapeDtypeStruct((t, D_MODEL), F32), jax.ShapeDtypeStruct((t, D_MODEL), BF16)] + [
        jax.ShapeDtypeStruct((t, D_FF), BF16)] * 3
    out_specs = [row(D_MODEL), row(D_MODEL), row(D_FF), row(D_FF), row(D_FF)]
    if head is not None:
        args, in_specs = args + tuple(head), in_specs + [vec, row(D_MODEL)]
        out_shape += [jax.ShapeDtypeStruct((8, 128), F32), jax.ShapeDtypeStruct((1, D_MODEL), F32)]
        out_specs += [pl.BlockSpec((8, 128), lambda i: (0, 0)), vec]
    return _pallas(
        body, args, name="ffn_fwd", grid=(t // TM,), out_shape=tuple(out_shape), in_specs=in_specs,
        out_specs=tuple(out_specs), scratch_shapes=[wv, wv, wv, pltpu.SemaphoreType.DMA((3,))],
        sem=("arbitrary",), vmem=VMEM_BIG, rider=rider)


def _ffn_bwd(dhout, h, gain, dgf, duf, wgt, wut, wd):
    t = h.shape[0]
    tm = TM_BWD

    def body(dho_ref, h_ref, gain_ref, g_ref, u_ref, wg_hbm, wu_hbm, wd_hbm,
             dh_ref, dg_ref, du_ref, df_ref, gg_ref, wg_v, wu_v, wd_v, wsem):
        @pl.when(pl.program_id(0) == 0)
        def _():
            _load_weights(((wd_hbm, wd_v), (wg_hbm, wg_v), (wu_hbm, wu_v)), wsem)
            gg_ref[...] = jnp.zeros_like(gg_ref)

        dho = dho_ref[...]
        df = (0.5 * dho).astype(BF16)
        df_ref[...] = df
        dn = jnp.zeros((tm, D_MODEL), F32)
        for c0, c1 in zip(FF_BOUNDS[:-1], FF_BOUNDS[1:]):
            sl = slice(c0, c1)
            da = _dot_nt(df, wd_v[sl, :])
            dg = (da * g_ref[:, sl].astype(F32)).astype(BF16)
            du = (da * u_ref[:, sl].astype(F32)).astype(BF16)
            dg_ref[:, sl] = dg
            du_ref[:, sl] = du
            dn = dn + _dot(dg, wg_v[sl, :]) + _dot(du, wu_v[sl, :])
        hh = h_ref[...]
        r = lax.rsqrt(jnp.mean(hh * hh, axis=-1, keepdims=True) + EPS)
        hn = hh * r
        gg_ref[...] += jnp.sum(dn * hn, axis=0, keepdims=True)
        dng = dn * gain_ref[...]
        dh_ref[...] = dho + r * (dng - hn * jnp.mean(dng * hn, axis=-1, keepdims=True))

    row = lambda w: pl.BlockSpec((tm, w), lambda i: (i, 0))
    vec = pl.BlockSpec((1, D_MODEL), lambda i: (0, 0))
    wv = pltpu.VMEM((D_FF, D_MODEL), BF16)
    return pl.pallas_call(
        body, name="ffn_bwd", grid=(t // tm,),
        out_shape=(jax.ShapeDtypeStruct((t, D_MODEL), F32), jax.ShapeDtypeStruct((t, D_FF), BF16),
                   jax.ShapeDtypeStruct((t, D_FF), BF16),
                   jax.ShapeDtypeStruct((t, D_MODEL), BF16), jax.ShapeDtypeStruct((1, D_MODEL), F32)),
        in_specs=[row(D_MODEL), row(D_MODEL), vec, row(D_FF), row(D_FF), ANY, ANY, ANY],
        out_specs=(row(D_MODEL), row(D_FF), row(D_FF), row(D_MODEL), vec),
        scratch_shapes=[wv, wv, wv, pltpu.SemaphoreType.DMA((3,))],
        compiler_params=_params(("arbitrary",), VMEM_BIG),
    )(dhout, h, gain, dgf, duf, wgt, wut, wd)


def _wgrad(lhs, rhs, rb, with_colsum=False, name="wgrad", rider=None):
    t, k = lhs.shape
    n = rhs.shape[1]

    def body(l_ref, r_ref, o_ref, *rest):
        o_ref[...] = _dot_tn(l_ref[...], r_ref[...]).astype(BF16)
        if with_colsum:
            rest[0][...] = jnp.sum(l_ref[...].astype(F32), axis=0, keepdims=True)

    out_shape = [jax.ShapeDtypeStruct((k, n), BF16)]
    out_specs = [pl.BlockSpec((rb, n), lambda j: (j, 0))]
    if with_colsum:
        out_shape.append(jax.ShapeDtypeStruct((1, k), F32))
        out_specs.append(pl.BlockSpec((1, rb), lambda j: (0, j)))
    res, ro = _pallas(
        body, (lhs, rhs), name=name, grid=(k // rb,), out_shape=tuple(out_shape),
        in_specs=[pl.BlockSpec((t, rb), lambda j: (0, j)), pl.BlockSpec((t, n), lambda j: (0, 0))],
        out_specs=tuple(out_specs), sem=("arbitrary",), vmem=VMEM_BIG, rider=rider)
    if rider is not None:
        return res[0], ro
    return res if with_colsum else res[0]


def _lane_blocks(nseq, seq, nblk, tm=TM):
    spt = seq // tm
    return pl.BlockSpec((1, nblk, tm, 128), lambda i: (i // spt, 0, i % spt, 0))


def _inproj_fwd(h, gain, wint, b_in, nseq, rider=None):
    t = h.shape[0]
    seq = t // nseq
    cut_a = 5 * MXU_DIM
    pieces = ((0, cut_a, 0, 0), (cut_a, ZA_W - cut_a, 0, cut_a), (ZA_W, ZB_W, 1, 0), (ZA_W + ZB_W, 1024, 2, 0),
              (ZA_W + ZB_W + 1024, 1024, 2, 1024))

    def body(h_ref, gain_ref, w_hbm, b_ref, u_ref, za_ref, zb_ref, zg_ref, w_v):
        @pl.when(pl.program_id(0) == 0)
        def _():
            pltpu.sync_copy(w_hbm, w_v)

        hh = h_ref[...]
        r = lax.rsqrt(jnp.mean(hh * hh, axis=-1, keepdims=True) + EPS)
        un = (hh * r * gain_ref[...]).astype(BF16)
        u_ref[...] = un
        outs = (None, zb_ref, zg_ref)
        for c0, cw, oi, o0 in pieces:
            val = _dot_nt(un, w_v[c0:c0 + cw, :]) + b_ref[:, c0:c0 + cw]
            if oi == 0:
                for cb in range(cw // 128):
                    za_ref[0, o0 // 128 + cb] = val[:, cb * 128:(cb + 1) * 128]
            else:
                outs[oi][:, o0:o0 + cw] = val.astype(BF16)

    row = lambda w: pl.BlockSpec((TM, w), lambda i: (i, 0))
    return _pallas(
        body, (h, gain, wint, b_in), name="inproj_fwd", grid=(t // TM,),
        out_shape=(jax.ShapeDtypeStruct((t, D_MODEL), BF16), jax.ShapeDtypeStruct((nseq, ZA_W // 128, seq, 128), F32),
                   jax.ShapeDtypeStruct((t, ZB_W), BF16), jax.ShapeDtypeStruct((t, 2 * D_MODEL), BF16)),
        in_specs=[row(D_MODEL), pl.BlockSpec((1, D_MODEL), lambda i: (0, 0)), ANY,
                  pl.BlockSpec((1, D_IN), lambda i: (0, 0))],
        out_specs=(row(D_MODEL), _lane_blocks(nseq, seq, ZA_W // 128), row(ZB_W), row(2 * D_MODEL)),
        scratch_shapes=[pltpu.VMEM((D_IN, D_MODEL), BF16)], sem=("arbitrary",), vmem=VMEM_BIG, rider=rider)


def _inproj_bwd(dz, dh2, h, gain, wint, rider=None):
    t = h.shape[0]
    nc = 5
    cw = D_IN // nc

    def body(dz_ref, dh2_ref, h_ref, gain_ref, w_hbm, dh_ref, gg_ref, w_v):
        @pl.when(pl.program_id(0) == 0)
        def _():
            pltpu.sync_copy(w_hbm, w_v)
            gg_ref[...] = jnp.zeros_like(gg_ref)

        du = jnp.zeros((TM, D_MODEL), F32)
        for ci in range(nc):
            sl = slice(ci * cw, (ci + 1) * cw)
            du = du + _dot(dz_ref[:, sl], w_v[sl, :])
        hh = h_ref[...]
        r = lax.rsqrt(jnp.mean(hh * hh, axis=-1, keepdims=True) + EPS)
        hn = hh * r
        gg_ref[...] += jnp.sum(du * hn, axis=0, keepdims=True)
        dng = du * gain_ref[...]
        dh_ref[...] = dh2_ref[...] + r * (dng - hn * jnp.mean(dng * hn, axis=-1, keepdims=True))

    row = lambda w: pl.BlockSpec((TM, w), lambda i: (i, 0))
    vec = pl.BlockSpec((1, D_MODEL), lambda i: (0, 0))
    return _pallas(
        body, (dz, dh2, h, gain, wint), name="inproj_bwd", grid=(t // TM,),
        out_shape=(jax.ShapeDtypeStruct((t, D_MODEL), F32), jax.ShapeDtypeStruct((1, D_MODEL), F32)),
        in_specs=[row(D_IN), row(D_MODEL), row(D_MODEL), vec, ANY],
        out_specs=(row(D_MODEL), vec),
        scratch_shapes=[pltpu.VMEM((D_IN, D_MODEL), BF16)], sem=("arbitrary",), vmem=VMEM_BIG, rider=rider)


def _head_sums(x):
    w = x.shape[1]
    i = lax.broadcasted_iota(jnp.int32, (w, w), 0) // HEAD_DIM
    j = lax.broadcasted_iota(jnp.int32, (w, w), 1) // HEAD_DIM
    ones = (i == j).astype(BF16)
    hi = x.astype(BF16)
    r1 = x - hi.astype(F32)
    mid = r1.astype(BF16)
    lo = (r1 - mid.astype(F32)).astype(BF16)
    return _dot(hi, ones) + _dot(mid, ones) + _dot(lo, ones)


def _merge_fwd(o0, o1, o2, l0, l1, l2, yb, zg, h1, wat, wbt, wout, rider=None):
    t = h1.shape[0]
    nseq, _, seq, _ = o0.shape

    def body(o0_ref, o1_ref, o2_ref, l0_ref, l1_ref, l2_ref, yb_ref, ga_ref, gb_ref, h1_ref, wa_ref, wb_ref, wo_ref,
             h2_ref, y_ref, lt_ref, pa_ref, pb_ref, mg_ref):
        wide = lambda ref: jnp.concatenate([ref[0, 0], ref[0, 1]], axis=1)
        la, lb, lc = wide(l0_ref), wide(l1_ref), wide(l2_ref)
        mx = jnp.maximum(jnp.maximum(la, lb), lc)
        ea, eb, ec = jnp.exp(la - mx), jnp.exp(lb - mx), jnp.exp(lc - mx)
        den = ea + eb + ec
        y = (ea * wide(o0_ref) + eb * wide(o1_ref) + ec * wide(o2_ref)) / den
        lt = mx + jnp.log(den)
        lt_ref[0, 0] = lt[:, :128]
        lt_ref[0, 1] = lt[:, 128:]
        yb16 = y.astype(BF16)
        y_ref[...] = yb16
        pa = _dot_nt(yb16, wa_ref[...])
        pb = _dot_nt(yb_ref[...], wb_ref[...])
        pa_ref[...] = pa.astype(BF16)
        pb_ref[...] = pb.astype(BF16)
        mg = (_sigmoid(ga_ref[...].astype(F32)) * pa + _sigmoid(gb_ref[...].astype(F32)) * pb).astype(BF16)
        mg_ref[...] = mg
        h2_ref[...] = h1_ref[...] + _dot(mg, wo_ref[...])

    row = lambda w: pl.BlockSpec((TM, w), lambda i: (i, 0))
    full = lambda a: pl.BlockSpec(a.shape, lambda i: (0, 0))
    gate = lambda cb: pl.BlockSpec((TM, D_MODEL), lambda i: (i, cb))
    return _pallas(
        body, (o0, o1, o2, l0, l1, l2, yb, zg, zg, h1, wat, wbt, wout), name="merge_fwd", grid=(t // TM,),
        out_shape=(jax.ShapeDtypeStruct((t, D_MODEL), F32), jax.ShapeDtypeStruct((t, GW), BF16),
                   jax.ShapeDtypeStruct((nseq, 2, seq, 128), F32), jax.ShapeDtypeStruct((t, D_MODEL), BF16),
                   jax.ShapeDtypeStruct((t, D_MODEL), BF16), jax.ShapeDtypeStruct((t, D_MODEL), BF16)),
        in_specs=[_lane_blocks(nseq, seq, 2)] * 6 + [row(2 * GW), gate(0), gate(1), row(D_MODEL), full(wat), full(wbt),
                                                     full(wout)],
        out_specs=(row(D_MODEL), row(GW), _lane_blocks(nseq, seq, 2), row(D_MODEL), row(D_MODEL), row(D_MODEL)),
        sem=("parallel",), vmem=VMEM_BIG, rider=rider)


def _merge_bwd(dh2, pa, pb, zg, y, yb, wat, wbt, wout, nseq, rider=None):
    t = dh2.shape[0]

    def body(dh2_ref, pa_ref, pb_ref, ga_ref, gb_ref, y_ref, yb_ref, wa_ref, wb_ref, wo_ref,
             dpa_ref, dpb_ref, dga_ref, dgb_ref, dya_ref, dyb_ref, dh2b_ref, ca_ref, cb_ref):
        d16 = dh2_ref[...].astype(BF16)
        dh2b_ref[...] = d16
        dm = _dot_nt(d16, wo_ref[...])
        sa = _sigmoid(ga_ref[...].astype(F32))
        sb = _sigmoid(gb_ref[...].astype(F32))
        dpa = (dm * sa).astype(BF16)
        dpb = (dm * sb).astype(BF16)
        dpa_ref[...] = dpa
        dpb_ref[...] = dpb
        dga_ref[...] = (dm * pa_ref[...].astype(F32) * sa * (1.0 - sa)).astype(BF16)
        dgb_ref[...] = (dm * pb_ref[...].astype(F32) * sb * (1.0 - sb)).astype(BF16)
        dya = _dot(dpa, wa_ref[...])
        dyb = _dot(dpb, wb_ref[...])
        dya_ref[0, 0] = dya[:, :128]
        dya_ref[0, 1] = dya[:, 128:]
        dyb_ref[...] = dyb.astype(BF16)
        ca = _head_sums(dya * y_ref[...].astype(F32))
        ca_ref[0, 0] = ca[:, :128]
        ca_ref[0, 1] = ca[:, 128:]
        cb_ref[...] = _head_sums(dyb * yb_ref[...].astype(F32))

    row = lambda w: pl.BlockSpec((TM, w), lambda i: (i, 0))
    full = lambda a: pl.BlockSpec(a.shape, lambda i: (0, 0))
    gate = lambda cb: pl.BlockSpec((TM, D_MODEL), lambda i: (i, cb))
    bf = lambda w: jax.ShapeDtypeStruct((t, w), BF16)
    lanes = jax.ShapeDtypeStruct((nseq, 2, t // nseq, 128), F32)
    lane_spec = _lane_blocks(nseq, t // nseq, 2)
    return _pallas(
        body, (dh2, pa, pb, zg, zg, y, yb, wat, wbt, wout), name="merge_bwd", grid=(t // TM,),
        out_shape=(bf(D_MODEL), bf(D_MODEL), bf(D_MODEL), bf(D_MODEL), lanes, bf(2 * GW), bf(D_MODEL),
                   lanes, jax.ShapeDtypeStruct((t, 2 * GW), F32)),
        in_specs=[row(D_MODEL), row(D_MODEL), row(D_MODEL), gate(0), gate(1), row(GW), row(2 * GW),
                  full(wat), full(wbt), full(wout)],
        out_specs=(row(D_MODEL), row(D_MODEL), row(D_MODEL), row(D_MODEL), lane_spec, row(2 * GW), row(D_MODEL),
                   lane_spec, row(2 * GW)),
        sem=("parallel",), vmem=VMEM_BIG, rider=rider)


def _lane_head(rows):
    return lax.broadcasted_iota(jnp.int32, (rows, GW), 1) // HEAD_DIM


def _kv_expand_matrix(r):
    ci = lax.broadcasted_iota(jnp.int32, (2 * HEAD_DIM, GW), 0)
    ji = lax.broadcasted_iota(jnp.int32, (2 * HEAD_DIM, GW), 1)
    return (ci == (ji % HEAD_DIM) + HEAD_DIM * r).astype(BF16)


def _block_rows(row0, stride, ib):
    start = row0 + (stride * BLOCK) * ib
    if stride > 1:
        return pl.ds(start, BLOCK, stride=stride)
    return pl.ds(pl.multiple_of(start, BLOCK), BLOCK)


def _stack_heads(x, lane_head):
    return jnp.concatenate([jnp.where(lane_head == h, x, jnp.zeros_like(x)) for h in range(4)], axis=0)


def _unstack_heads(x4, lane_head):
    out = jnp.zeros((BLOCK, GW), F32)
    for h in range(4):
        out = jnp.where(lane_head == h, x4[h * BLOCK:(h + 1) * BLOCK], out)
    return out


def _load_rows(ref, rows, split):
    if split:
        return jnp.concatenate([ref[0, 0, rows, :], ref[0, 1, rows, :]], axis=1)
    return ref[0, rows, :]


def _store_rows(ref, rows, val, split):
    if split:
        ref[0, 0, rows, :] = val[:, :128]
        ref[0, 1, rows, :] = val[:, 128:]
    else:
        ref[0, rows, :] = val


def _attn_fwd(q_arr, k_arr, v_arr, bias, sink, *, grid, seq, stride, kvw, split, q_spec, k_spec, v_spec, bias_map,
              sink_map, o_spec, has_sink, o_shape, o_dtype, name, rider=None):
    nb = seq // stride // BLOCK
    scale = HEAD_DIM ** -0.5
    expanded = kvw != GW
    rps = min(stride, RESIDUES_PER_STEP)
    grid = (grid[0], grid[1] // rps)
    assert not has_sink or B_WINDOW - 1 < BLOCK

    def body(q_ref, k_ref, v_ref, bias_ref, sink_ref, o_ref, lse_ref, *kv_x):
        rr = pl.program_id(1)
        lane_head = _lane_head(BLOCK)
        if expanded:
            expand = _kv_expand_matrix(rr)
            kv_x[0][...] = _dot(k_ref[0], expand).astype(BF16)
            kv_x[1][...] = _dot(v_ref[0], expand).astype(BF16)
        for j in range(rps):
            residue(rr * rps + j if stride > 1 else 0, q_ref, k_ref, v_ref, bias_ref, sink_ref, o_ref, lse_ref, kv_x,
                    lane_head)

    def residue(row0, q_ref, k_ref, v_ref, bias_ref, sink_ref, o_ref, lse_ref, kv_x, lane_head):
        def per_head(fn, x):
            return jnp.concatenate([fn(sink_ref[0, h:h + 1, 0:1], x[h * BLOCK:(h + 1) * BLOCK]) for h in range(4)],
                                   axis=0)

        def load(ref, ib):
            return _load_rows(ref, _block_rows(row0, stride, ib), split).astype(BF16)

        def load_kv(which, ib):
            if expanded:
                return kv_x[which][_block_rows(0, 1, ib), :]
            return load((k_ref, v_ref)[which], ib)

        def block(ib, first):
            q4 = _stack_heads(load(q_ref, ib), lane_head)
            if first:
                kc, vc = load_kv(0, ib), load_kv(1, ib)
                b4 = bias_ref[:, :, BLOCK:].reshape(4 * BLOCK, BLOCK)
            else:
                kc = jnp.concatenate([load_kv(0, ib - 1), load_kv(0, ib)], axis=0)
                vc = jnp.concatenate([load_kv(1, ib - 1), load_kv(1, ib)], axis=0)
                b4 = bias_ref[...].reshape(4 * BLOCK, 2 * BLOCK)
                if has_sink:
                    oldest = lax.broadcasted_iota(jnp.int32, kc.shape, 0) == 0
                    kc = jnp.where(oldest, jnp.zeros_like(kc), kc)
                    vc = jnp.where(oldest, jnp.zeros_like(vc), vc)
            s = _dot_nt(q4, kc) * scale + b4
            m = jnp.max(s, axis=-1, keepdims=True)
            if has_sink and first:
                m = per_head(jnp.maximum, m)
            p = jnp.exp(s - m)
            l = jnp.sum(p, axis=-1, keepdims=True)
            if has_sink and first:
                l = l + per_head(lambda sk, mh: jnp.exp(sk - mh), m)
            o4 = _dot(p.astype(BF16), vc) / l
            rows = _block_rows(row0, stride, ib)
            _store_rows(o_ref, rows, _unstack_heads(o4, lane_head).astype(o_dtype), split)
            _store_rows(lse_ref, rows, _unstack_heads(m + jnp.log(l), lane_head), split)

        block(0, True)
        if nb > 1:
            def step(i, carry):
                block(i, False)
                return carry
            lax.fori_loop(1, nb, step, 0, unroll=min(ATTN_UNROLL, nb - 1))

    return _pallas(
        body, (q_arr, k_arr, v_arr, bias, sink), name=name, grid=grid,
        out_shape=(jax.ShapeDtypeStruct(o_shape, o_dtype), jax.ShapeDtypeStruct(o_shape, F32)),
        in_specs=[q_spec, k_spec, v_spec,
                  pl.BlockSpec((4, BLOCK, 2 * BLOCK), bias_map), pl.BlockSpec((1, 4, 128), sink_map)],
        out_specs=(o_spec, o_spec),
        scratch_shapes=[pltpu.VMEM((seq, GW), BF16)] * 2 if expanded else [],
        sem=("arbitrary", "arbitrary"), vmem=VMEM_BIG, rider=rider)


def _attn_bwd(q_arr, k_arr, v_arr, bias, sink, dy, cc, lse, *, grid, seq, stride, kvw, split, q_spec, k_spec, v_spec,
              bias_map, sink_map, o_spec, kv_out_spec, has_sink, n_bias, dq_shape, dkv_shape, g_dtype, name):
    ln = seq // stride
    nb = ln // BLOCK
    scale = HEAD_DIM ** -0.5
    expanded = kvw != GW
    rps = min(stride, RESIDUES_PER_STEP)
    grid = (grid[0], grid[1] // rps)

    def body(q_ref, k_ref, v_ref, bias_ref, sink_ref, dy_ref, c_ref, lse_ref,
             dq_ref, dk_ref, dv_ref, db_ref, dsk_ref, dk_acc, dv_acc, dk_half, dv_half, *kv_x):
        rr = pl.program_id(1)

        @pl.when((pl.program_id(0) == 0) & (rr == 0))
        def _():
            db_ref[...] = jnp.zeros_like(db_ref)
            dsk_ref[...] = jnp.zeros_like(dsk_ref)

        if expanded:
            expand = _kv_expand_matrix(rr)
            kv_x[0][...] = _dot(k_ref[0], expand).astype(BF16)
            kv_x[1][...] = _dot(v_ref[0], expand).astype(BF16)
        refs = (q_ref, k_ref, v_ref, bias_ref, sink_ref, dy_ref, c_ref, lse_ref, dq_ref, dk_ref, dv_ref, db_ref,
                dsk_ref, dk_acc, dv_acc, dk_half, dv_half, kv_x)
        for j in range(rps):
            residue(rr, rr * rps + j if stride > 1 else 0, *refs)

    def residue(rr, row0, q_ref, k_ref, v_ref, bias_ref, sink_ref, dy_ref, c_ref, lse_ref,
                dq_ref, dk_ref, dv_ref, db_ref, dsk_ref, dk_acc, dv_acc, dk_half, dv_half, kv_x):
        dk_acc[...] = jnp.zeros_like(dk_acc)
        dv_acc[...] = jnp.zeros_like(dv_acc)
        lane_head = _lane_head(BLOCK)
        hb = 4 * rr if n_bias == 8 else 0

        def load(ref, ib):
            return _load_rows(ref, _block_rows(row0, stride, ib), split)

        def load_kv(which, ib):
            if expanded:
                return kv_x[which][_block_rows(0, 1, ib), :]
            return load((k_ref, v_ref)[which], ib).astype(BF16)

        def head_col(x):
            return jnp.concatenate([x[:, h * HEAD_DIM:h * HEAD_DIM + 1] for h in range(4)], axis=0)

        def block(ib, first):
            q4 = _stack_heads(load(q_ref, ib).astype(BF16), lane_head)
            dy4 = _stack_heads(load(dy_ref, ib).astype(BF16), lane_head)
            c4 = head_col(load(c_ref, ib))
            l4 = head_col(load(lse_ref, ib))
            if first:
                kc, vc = load_kv(0, ib), load_kv(1, ib)
                b4 = bias_ref[:, :, BLOCK:].reshape(4 * BLOCK, BLOCK)
                krows = pl.ds(0, BLOCK)
            else:
                kc = jnp.concatenate([load_kv(0, ib - 1), load_kv(0, ib)], axis=0)
                vc = jnp.concatenate([load_kv(1, ib - 1), load_kv(1, ib)], axis=0)
                b4 = bias_ref[...].reshape(4 * BLOCK, 2 * BLOCK)
                krows = pl.ds(pl.multiple_of((ib - 1) * BLOCK, BLOCK), 2 * BLOCK)
            nk = BLOCK if first else 2 * BLOCK
            p = jnp.exp(_dot_nt(q4, kc) * scale + b4 - l4)
            ds = p * (_dot_nt(dy4, vc) - c4)
            ds3 = ds.reshape(4, BLOCK, nk)
            if n_bias == 8:
                if first:
                    db_ref[pl.ds(hb, 4), :, BLOCK:] += ds3
                else:
                    db_ref[pl.ds(hb, 4)] += ds3
            elif first:
                db_ref[:, :, BLOCK:] += ds3
            else:
                db_ref[...] += ds3
            ds16 = ds.astype(BF16)
            dq = _unstack_heads(_dot(ds16, kc), lane_head) * scale
            _store_rows(dq_ref, _block_rows(row0, stride, ib), dq.astype(g_dtype), split)
            dk_acc[krows, :] += _dot_tn(ds16, q4) * scale
            dv_acc[krows, :] += _dot_tn(p.astype(BF16), dy4)
            if has_sink:
                for h in range(4):
                    hs = slice(h * BLOCK, (h + 1) * BLOCK)
                    sk = sink_ref[0, h:h + 1, 0:1]
                    val = -jnp.sum(jnp.exp(sk - l4[hs]) * c4[hs], axis=0, keepdims=True)
                    dsk_ref[hb + h] += jnp.broadcast_to(val, (8, 128))

        block(0, True)
        if nb > 1:
            def step(i, carry):
                block(i, False)
                return carry
            lax.fori_loop(1, nb, step, 0, unroll=min(ATTN_UNROLL, nb - 1))

        if kvw == GW:
            all_rows = pl.ds(row0, ln, stride=stride) if stride > 1 else pl.ds(0, ln)
            _store_rows(dk_ref, all_rows, dk_acc[...].astype(g_dtype), split)
            _store_rows(dv_ref, all_rows, dv_acc[...].astype(g_dtype), split)
        else:
            def fold(acc):
                t2 = acc[:, :2 * HEAD_DIM] + acc[:, 2 * HEAD_DIM:]
                t2 = t2 + pltpu.roll(t2, HEAD_DIM, 1)
                lane = lax.broadcasted_iota(jnp.int32, t2.shape, 1) // HEAD_DIM
                return jnp.where(lane == rr, t2, 0.0)

            @pl.when(rr == 0)
            def _():
                dk_half[...] = fold(dk_acc[...])
                dv_half[...] = fold(dv_acc[...])

            @pl.when(rr == 1)
            def _():
                dk_ref[0] = (dk_half[...] + fold(dk_acc[...])).astype(g_dtype)
                dv_ref[0] = (dv_half[...] + fold(dv_acc[...])).astype(g_dtype)

    return pl.pallas_call(
        body, name=name, grid=grid,
        out_shape=(jax.ShapeDtypeStruct(dq_shape, g_dtype), jax.ShapeDtypeStruct(dkv_shape, g_dtype),
                   jax.ShapeDtypeStruct(dkv_shape, g_dtype), jax.ShapeDtypeStruct((n_bias, BLOCK, 2 * BLOCK), F32),
                   jax.ShapeDtypeStruct((8, 8, 128), F32)),
        in_specs=[q_spec, k_spec, v_spec,
                  pl.BlockSpec((4, BLOCK, 2 * BLOCK), bias_map), pl.BlockSpec((1, 4, 128), sink_map),
                  o_spec, o_spec, o_spec],
        out_specs=(o_spec, kv_out_spec, kv_out_spec,
                   pl.BlockSpec((n_bias, BLOCK, 2 * BLOCK), lambda n, r: (0, 0, 0)),
                   pl.BlockSpec((8, 8, 128), lambda n, r: (0, 0, 0))),
        scratch_shapes=[pltpu.VMEM((ln, GW), F32), pltpu.VMEM((ln, GW), F32),
                        pltpu.VMEM((ln, 2 * HEAD_DIM), F32), pltpu.VMEM((ln, 2 * HEAD_DIM), F32)]
        + ([pltpu.VMEM((seq, GW), BF16)] * 2 if expanded else []),
        compiler_params=_params(("arbitrary", "arbitrary"), VMEM_BIG),
    )(q_arr, k_arr, v_arr, bias, sink, dy, cc, lse)


def _bias_grad(ds_all, buckets):
    def body(ds_ref, bk_ref, o_ref):
        rows = lax.broadcasted_iota(jnp.int32, (N_BUCKETS, 128), 0)
        cols = lax.broadcasted_iota(jnp.int32, (N_BUCKETS, 128), 1)

        def per_bucket(b, acc):
            for h in range(20):
                gi = h // 4 if h < 12 else 3
                v = jnp.where(bk_ref[gi] == b, ds_ref[h], 0.0)
                v = jnp.sum(jnp.sum(v, axis=1, keepdims=True), axis=0, keepdims=True)
                acc = jnp.where((rows == b) & (cols == h), v, acc)
            return acc

        o_ref[...] = lax.fori_loop(0, N_BUCKETS, per_bucket, jnp.zeros((N_BUCKETS, 128), F32))

    vm = pl.BlockSpec(memory_space=pltpu.VMEM)
    return pl.pallas_call(body, name="bias_grad", out_shape=jax.ShapeDtypeStruct((N_BUCKETS, 128), F32),
                          in_specs=[vm, vm], out_specs=vm)(ds_all, buckets)


def _adamw(w, g, m, v, name):
    (res,), _ = _adamw_many([(w, g, m, v)], name)
    return res


def _adamw_many(tensors, name, rider=None):
    n = len(tensors)
    r, c = tensors[0][0].shape
    tr = r
    for cand in (256, 176, 128, 88, 64, 32, 16, 8):
        if r % cand == 0 and cand * c * 4 * 7 * n * 2 <= 24 * 1024 * 1024:
            tr = cand
            break

    def body(*refs):
        ins, outs = refs[:4 * n], refs[4 * n:]
        for i in range(n):
            w_ref, g_ref, m_ref, v_ref = ins[4 * i:4 * i + 4]
            d, nm, nv = _adam_update(w_ref[...], g_ref[...], m_ref[...], v_ref[...])
            outs[3 * i][...], outs[3 * i + 1][...], outs[3 * i + 2][...] = d, nm, nv

    spec = pl.BlockSpec((tr, c), lambda i: (i, 0))
    shp = jax.ShapeDtypeStruct((r, c), F32)
    res, ro = _pallas(body, tuple(a for t4 in tensors for a in t4), name=name, grid=(r // tr,),
                      out_shape=(shp,) * (3 * n), in_specs=[spec] * (4 * n), out_specs=(spec,) * (3 * n),
                      sem=("parallel",), vmem=VMEM_BIG, rider=rider)
    return [tuple(res[3 * i:3 * i + 3]) for i in range(n)], ro


def _t5_bucket(dist):
    max_exact = N_BUCKETS // 2
    n = jnp.maximum(dist, 0)
    nf = jnp.maximum(n, 1).astype(F32)
    large = max_exact + (jnp.log(nf / max_exact) / math.log(MAX_DISTANCE / max_exact)
                         * (N_BUCKETS - max_exact)).astype(jnp.int32)
    large = jnp.minimum(large, N_BUCKETS - 1)
    return jnp.where(n < max_exact, n, large)


def _bias_tables(rel_bias):
    qi = jnp.arange(BLOCK)[:, None]
    ki = jnp.arange(2 * BLOCK)[None, :]
    dist = qi + BLOCK - ki
    specs = [(d, w // d, 4 * gi, 4 * gi + 4) for gi, (w, d) in enumerate(DIL_GROUPS)] + [(1, B_WINDOW - 1, 12, 20)]
    biases, buckets = [], []
    for stride, steps, h0, h1 in specs:
        valid = (dist >= 0) & (dist <= steps)
        bk = jnp.where(valid, _t5_bucket(dist * stride), -1).astype(jnp.int32)
        onehot = (bk[None, :, :] == jnp.arange(N_BUCKETS, dtype=jnp.int32)[:, None, None]).astype(F32)
        b = jnp.einsum("bqk,bh->hqk", onehot, rel_bias[:, h0:h1], precision=lax.Precision.HIGHEST)
        biases.append(jnp.where(valid[None], b, NEG))
        buckets.append(bk)
    return jnp.concatenate(biases, axis=0), jnp.stack(buckets, axis=0)


def _local_step(x, tgt, W, S, shards=None, tail_host=None):
    nseq, seq, _ = x.shape
    t = nseq * seq
    xf = x.reshape(t, D_MODEL)
    bias_all, buckets = _bias_tables(S["rel_bias"])
    sink_b = jnp.broadcast_to(S["sinks"].reshape(2, 4, 1), (2, 4, 128)).astype(F32)
    sink_0 = jnp.zeros((1, 4, 128), F32)
    dist = shards is not None
    W = dict(W)
    G, GS, reduced = {}, {}, {}

    def put(keys, gathered):
        for k, g in zip(keys, gathered):
            W[k] = g.reshape(_FULL_SHAPE.get(k, (N_CHIPS * shards[k].shape[0], D_MODEL)))

    def gather_rider(keys):
        return _GatherRider([shards[k] for k in keys]) if dist else None

    def pair(keys):
        return _pair_reduce([G[k].reshape(N_CHIPS, 2, shards[k].shape[0] // 2, D_MODEL) for k in keys],
                            "grad_pair_reduce_" + keys[0])

    def finish(keys, own, rec):
        full = _final_reduce(own, rec, "grad_final_reduce_" + keys[0])
        off = 0
        for k in keys:
            r = shards[k].shape[0]
            reduced[k] = full[:, off:off + r // 2].reshape(r, D_MODEL)
            off += r // 2

    if dist:
        first = ("wgt1", "wut1", "wd1")
        put(first, _gather_rows([shards[k] for k in first]))
    keys = ("wint",)
    (h1, n1, g1, u1, a1), ro = _ffn_fwd(xf, S["ffn1_norm"], W["wgt1"], W["wut1"], W["wd1"], rider=gather_rider(keys))
    put(keys, ro)
    keys = ("wout", "wat", "wbt", "wgt2")
    (un, za, zb, zg), ro = _inproj_fwd(h1, S["mix_norm"], W["wint"], S["b_in"], nseq, rider=gather_rider(keys))
    put(keys, ro)

    seq3 = lambda a: a.reshape(nseq, seq, a.shape[-1])
    zb3 = seq3(zb)
    pair_blk = lambda cb: pl.BlockSpec((1, 2, seq, 128), lambda n, r, cb=cb: (n, cb, 0, 0))
    a_cfg = []
    outs, lses = [], []
    for gi, (_, d) in enumerate(DIL_GROUPS):
        cfg = dict(grid=(nseq, d), seq=seq, stride=d, kvw=GW, split=True,
                   q_spec=pair_blk(gi), k_spec=pair_blk(3 + gi), v_spec=pair_blk(6 + gi), o_spec=pair_blk(0),
                   bias_map=lambda n, r: (0, 0, 0), sink_map=lambda n, r: (0, 0, 0), has_sink=False)
        a_cfg.append(cfg)
        (o, lse), _ = _attn_fwd(za, za, za, bias_all[4 * gi:4 * gi + 4], sink_0, o_shape=(nseq, 2, seq, 128),
                                o_dtype=F32, name=f"attn_a{gi}_fwd", **cfg)
        outs.append(o)
        lses.append(lse)
    wide_blk = lambda w, cmap: pl.BlockSpec((1, seq, w), cmap)
    b_cfg = dict(grid=(nseq, 2), seq=seq, stride=1, kvw=2 * HEAD_DIM, split=False,
                 q_spec=wide_blk(GW, lambda n, r: (n, 0, r)), k_spec=wide_blk(2 * HEAD_DIM, lambda n, r: (n, 0, 4)),
                 v_spec=wide_blk(2 * HEAD_DIM, lambda n, r: (n, 0, 5)), o_spec=wide_blk(GW, lambda n, r: (n, 0, r)),
                 bias_map=lambda n, r: (r, 0, 0), sink_map=lambda n, r: (r, 0, 0), has_sink=True)
    keys = ("wut2",)
    bias_b_fwd = bias_all[12:20].at[:, :, 0].set(jnp.broadcast_to(S["sinks"].reshape(8, 1), (8, BLOCK)))
    (yb, lse_b), ro = _attn_fwd(zb3, zb3, zb3, bias_b_fwd, sink_b, o_shape=(nseq, seq, 2 * GW), o_dtype=BF16,
                                name="attn_b_fwd", rider=gather_rider(keys), **b_cfg)
    put(keys, ro)
    yb = yb.reshape(t, 2 * GW)

    keys = ("wd2",)
    (h2, y, lse_tot, pa, pb, merged), ro = _merge_fwd(outs[0], outs[1], outs[2], lses[0], lses[1], lses[2], yb, zg, h1,
                                                      W["wat"], W["wbt"], W["wout"], rider=gather_rider(keys))
    put(keys, ro)
    (dh3, n2, g2, u2, a2, loss_part, g_final), _ = _ffn_fwd(
        h2, S["ffn2_norm"], W["wgt2"], W["wut2"], W["wd2"],
        head=(S["final_norm"].reshape(1, D_MODEL), tgt.reshape(t, D_MODEL)))

    GS["final_norm"] = g_final
    dh2, dg2, du2, df2, GS["ffn2_norm"] = _ffn_bwd(dh3, h2, S["ffn2_norm"], g2, u2, W["wgt2"], W["wut2"], W["wd2"])
    G["wgt2"] = _wgrad(dg2, n2, MXU_DIM, name="wgrad_gate2")
    G["wut2"] = _wgrad(du2, n2, MXU_DIM, name="wgrad_up2")
    G["wd2"] = _wgrad(a2, df2, MXU_DIM, name="wgrad_down2")

    keys = ("wgt2", "wut2", "wd2")
    rider = _ExchangeRider([pair(keys)]) if dist else None
    (dpa, dpb, dga, dgb, dya, dyb, dh2b, ca, cb), ro = _merge_bwd(dh2, pa, pb, zg, y, yb, W["wat"], W["wbt"], W["wout"],
                                                                  nseq, rider=rider)
    if dist:
        finish(keys, *ro)

    dqs, dks, dvs, dbs = [], [], [], []
    shp = (nseq, 2, seq, 128)
    halves = lambda a: [a[:, hf].reshape(t, 128).astype(BF16) for hf in range(2)]
    for gi in range(len(DIL_GROUPS)):
        dq, dk, dv, db, _ = _attn_bwd(za, za, za, bias_all[4 * gi:4 * gi + 4], sink_0, dya, ca, lse_tot,
                                      n_bias=4, dq_shape=shp, dkv_shape=shp, g_dtype=F32,
                                      kv_out_spec=a_cfg[gi]["o_spec"], name=f"attn_a{gi}_bwd", **a_cfg[gi])
        dqs += halves(dq)
        dks += halves(dk)
        dvs += halves(dv)
        dbs.append(db)
    dqb, dkb, dvb, dbb, dsink = _attn_bwd(zb3, zb3, zb3, bias_all[12:20], sink_b, seq3(dyb), seq3(cb), lse_b,
                                          n_bias=8, dq_shape=(nseq, seq, 2 * GW),
                                          dkv_shape=(nseq, seq, 2 * HEAD_DIM), g_dtype=BF16,
                                          kv_out_spec=wide_blk(2 * HEAD_DIM, lambda n, r: (n, 0, 0)),
                                          name="attn_b_bwd", **b_cfg)
    dz = jnp.concatenate(dqs + dks + dvs + [dqb.reshape(t, 2 * GW), dkb.reshape(t, 2 * HEAD_DIM),
                                            dvb.reshape(t, 2 * HEAD_DIM), dga, dgb], axis=-1)
    gb_tab = _bias_grad(jnp.concatenate(dbs + [dbb], axis=0), buckets)
    if dist:
        GS["bias_tab"], GS["sink_tiles"] = gb_tab, dsink
    else:
        GS["rel_bias"] = gb_tab[:, :20]
        GS["sinks"] = dsink[:, 0, 0].reshape(1, 8)

    G["wint"], GS["b_in"] = _wgrad(dz, un, MXU_DIM, with_colsum=True, name="wgrad_in")
    G["wout"] = _wgrad(merged, dh2b, MXU_DIM, name="wgrad_out")
    G["wat"] = _wgrad(dpa, y, MXU_DIM, name="wgrad_branch_a")
    G["wbt"] = _wgrad(dpb, yb, MXU_DIM, name="wgrad_branch_b")
    keys = ("wint", "wout", "wat", "wbt")
    rider = _ExchangeRider([pair(keys)]) if dist else None
    (dh1, GS["mix_norm"]), ro = _inproj_bwd(dz, dh2, h1, S["mix_norm"], W["wint"], rider=rider)
    if dist:
        finish(keys, *ro)

    dx, dg1, du1, df1, GS["ffn1_norm"] = _ffn_bwd(dh1, xf, S["ffn1_norm"], g1, u1, W["wgt1"], W["wut1"], W["wd1"])
    G["wgt1"] = _wgrad(dg1, n1, MXU_DIM, name="wgrad_gate1")
    if dist:
        G["wut1"], ro = _wgrad(du1, n1, MXU_DIM, name="wgrad_up1", rider=_ExchangeRider([pair(("wgt1",))]))
        finish(("wgt1",), *ro)
        G["wd1"], ro = _wgrad(a1, df1, MXU_DIM, name="wgrad_down1", rider=_ExchangeRider([pair(("wut1",))]))
        finish(("wut1",), *ro)
        finish(("wd1",), *tail_host(_ExchangeRider([pair(("wd1",))]), reduced))
    else:
        G["wut1"] = _wgrad(du1, n1, MXU_DIM, name="wgrad_up1")
        G["wd1"] = _wgrad(a1, df1, MXU_DIM, name="wgrad_down1")
    return loss_part, dx.reshape(x.shape), (reduced if dist else G), GS


_SMALL = ("ffn1_norm", "mix_norm", "ffn2_norm", "final_norm", "b_in", "sinks", "rel_bias")
_ORDER = ("ffn1_norm", "ffn1_w_gate", "ffn1_w_up", "ffn1_w_down", "mix_norm", "w_in", "b_in", "w_branch_a",
          "w_branch_b", "w_out", "sinks", "rel_bias", "ffn2_norm", "ffn2_w_gate", "ffn2_w_up", "ffn2_w_down",
          "final_norm")
_BIG = (("wgt1", "ffn1_w_gate", True, 704), ("wut1", "ffn1_w_up", True, 704), ("wd1", "ffn1_w_down", False, 704),
        ("wint", "w_in", True, 1280), ("wout", "w_out", False, 256), ("wat", "w_branch_a", True, 64),
        ("wbt", "w_branch_b", True, 128), ("wgt2", "ffn2_w_gate", True, 704), ("wut2", "ffn2_w_up", True, 704),
        ("wd2", "ffn2_w_down", False, 704))
_FULL_SHAPE = {"wat": (D_MODEL, GW), "wbt": (D_MODEL, 2 * GW)}


def kernel(x, ffn1_norm, ffn1_w_gate, ffn1_w_up, ffn1_w_down, mix_norm, w_in, b_in, w_branch_a, w_branch_b, w_out, sinks, rel_bias, ffn2_norm, ffn2_w_gate, ffn2_w_up, ffn2_w_down, final_norm, loss_target, m_ffn1_norm, m_ffn1_w_gate, m_ffn1_w_up, m_ffn1_w_down, m_mix_norm, m_w_in, m_b_in, m_w_branch_a, m_w_branch_b, m_w_out, m_sinks, m_rel_bias, m_ffn2_norm, m_ffn2_w_gate, m_ffn2_w_up, m_ffn2_w_down, m_final_norm, v_ffn1_norm, v_ffn1_w_gate, v_ffn1_w_up, v_ffn1_w_down, v_mix_norm, v_w_in, v_b_in, v_w_branch_a, v_w_branch_b, v_w_out, v_sinks, v_rel_bias, v_ffn2_norm, v_ffn2_w_gate, v_ffn2_w_up, v_ffn2_w_down, v_final_norm):
    args = dict(locals())
    w = {n: args[n] for n in _ORDER}
    m = {n: args["m_" + n] for n in _ORDER}
    v = {n: args["v_" + n] for n in _ORDER}

    shards = {}
    for key, name, transposed, rows in _BIG:
        a = w[name][0]
        a = (a.T if transposed else a).astype(BF16)
        shards[key] = a.reshape(rows, D_MODEL)
    S = {n: w[n] for n in _SMALL}

    row_adam = lambda n: (w[n][0].T, m[n][0].T, v[n][0].T)
    early = {}

    def tail_host(rider, reduced):
        done = (("wgt2", "ffn2_w_gate"), ("wut2", "ffn2_w_up"), ("wd2", "ffn2_w_down"), ("wgt1", "ffn1_w_gate"),
                ("wut1", "ffn1_w_up"))
        tensors = []
        for key, n in done:
            wmv = row_adam(n) if key != "wd2" else (w[n][0], m[n][0], v[n][0])
            tensors.append((wmv[0], reduced[key], wmv[1], wmv[2]))
        res, ro = _adamw_many(tensors, "adamw_ffn", rider=rider)
        for (_, n), r in zip(done, res):
            early[n] = r
        return ro

    loss_part, grad_x, reduced, GS = _local_step(x, loss_target, {}, S, shards, tail_host)

    small = _allreduce_small(GS["ffn1_norm"], GS["mix_norm"], GS["ffn2_norm"], GS["final_norm"], GS["b_in"],
                             GS["sink_tiles"], GS["bias_tab"], loss_part)
    loss = small[9, 8]

    out_g, out_d, out_m, out_v = {}, {}, {}, {}
    for key, n, transposed, rows in _BIG:
        nat = w[n][0].shape
        if transposed and nat[1] % 128:
            res = early[n] if n in early else _adamw(row_adam(n)[0], reduced[key], *row_adam(n)[1:], "adamw_" + n)
            res = [reduced[key].T] + [r.T for r in res]
        elif n in early:
            res = [reduced[key]] + list(early[n])
        else:
            g = reduced[key].reshape(nat[1], nat[0]).T if transposed else reduced[key].reshape(nat)
            res = [g] + list(_adamw(w[n][0], g, m[n][0], v[n][0], "adamw_" + n))
        out_g[n], out_d[n], out_m[n], out_v[n] = [r[None] for r in res]
    row = lambda d: {n: (d[n].reshape(1, D_MODEL) if n == "final_norm" else d[n]) for n in _SMALL}
    for dst, src in zip((out_g, out_d, out_m, out_v), _adamw_small(small, row(w), row(m), row(v))):
        dst.update(src)
        dst["final_norm"] = src["final_norm"].reshape(D_MODEL)

    return (loss, grad_x, *[out_g[n] for n in _ORDER], *[out_d[n] for n in _ORDER],
            *[out_m[n] for n in _ORDER], *[out_v[n] for n in _ORDER])
```

```python
import math

import jax
import jax.numpy as jnp
from jax import lax
from jax.experimental import pallas as pl
from jax.experimental.pallas import tpu as pltpu

F32, BF16 = jnp.float32, jnp.bfloat16
MESH = pl.DeviceIdType.MESH

D_MODEL = 1024
D_FF = 2816
D_IN = 5120
HEAD_DIM = 64
BLOCK = 128
DIL_GROUPS = ((128, 1), (512, 4), (2048, 16))
B_WINDOW = 128
N_BUCKETS = 32
MAX_DISTANCE = 2048
EPS = 1e-6
N_CHIPS = 4
GW = 256
ZA_W = 2304
ZB_W = 768
NEG = -1e30

ADAM_LR, ADAM_B1, ADAM_B2, ADAM_EPS, ADAM_WD, ADAM_STEP = 0.001, 0.9, 0.999, 1e-08, 0.01, 10

VMEM_BIG = 56 * 1024 * 1024
TM = 512
TM_BWD = 256
MXU_DIM = 256
FF_BOUNDS = (0, 4 * MXU_DIM, 8 * MXU_DIM, D_FF)
DMA_SPLIT = 8
RESIDUES_PER_STEP = 16
ATTN_UNROLL = 15


def _dot(a, b):
    return jnp.dot(a, b, preferred_element_type=F32)


def _dot_nt(a, b):
    return lax.dot_general(a, b, (((1,), (1,)), ((), ())), preferred_element_type=F32)


def _dot_tn(a, b):
    return lax.dot_general(a, b, (((0,), (0,)), ((), ())), preferred_element_type=F32)


def _sigmoid(x):
    return 0.5 * jnp.tanh(0.5 * x) + 0.5


def _params(sem, vmem=None):
    return pltpu.CompilerParams(dimension_semantics=sem, vmem_limit_bytes=vmem)


ANY = pl.BlockSpec(memory_space=pl.ANY)


def _me():
    return lax.axis_index("x"), lax.axis_index("y"), lax.axis_index("c")


_CHIP_RELS = ((1, 0), (0, 1), (1, 1))


def _flip(v, f):
    return 1 - v if f else v


def _remote(src, dst, ssem, rsem, peer):
    return pltpu.make_async_remote_copy(src_ref=src, dst_ref=dst, send_sem=ssem, recv_sem=rsem,
                                        device_id=peer, device_id_type=MESH)


def _row_pieces(rows, n):
    step = max(16, -(-rows // n) // 16 * 16)
    out, s = [], 0
    while s < rows:
        out.append((s, min(step, rows - s)))
        s += step
    return out


def _gather_rows(shards):
    nt = len(shards)
    rows = [s.shape[0] for s in shards]

    def body(*refs):
        srcs, outs = refs[:nt], refs[nt:2 * nt]
        halves, quarters = refs[2 * nt:3 * nt], refs[3 * nt:4 * nt]
        ici_s, ici_r, fwd_s, fwd_r, d2d_s, d2d_r, keep, loc = refs[4 * nt:]
        x, y, c = _me()
        j = 2 * x + y
        sib = (x, y, 1 - c)
        nbr = ((1 - x, y, c), (x, 1 - y, c))
        nbr_j = (2 * (1 - x) + y, 2 * x + (1 - y))
        diag_j = 2 * (1 - x) + (1 - y)
        local = [pltpu.make_async_copy(srcs[t], outs[t].at[j], loc.at[t]) for t in range(nt)]
        for cp in local:
            cp.start()
        pending = []
        for a in range(2):
            for t in range(nt):
                half = pl.ds(c * (rows[t] // 2), rows[t] // 2)
                cp = _remote(srcs[t].at[half], halves[t].at[a], ici_s.at[2 * t + a], ici_r.at[2 * t + a], nbr[a])
                cp.start()
                pending.append(cp)
        placed = []

        def place(src, dst_of, idx):
            mine = pltpu.make_async_copy(src, dst_of, keep.at[idx])
            mine.start()
            cp = _remote(src, dst_of, d2d_s.at[idx], d2d_r.at[idx], sib)
            cp.start()
            placed.append((mine, cp))

        for a in range(2):
            for t in range(nt):
                r2, r4 = rows[t] // 2, rows[t] // 4
                got = halves[t].at[a]
                _remote(got, got, ici_s.at[2 * t + a], ici_r.at[2 * t + a], nbr[a]).wait_recv()
                cp = _remote(halves[t].at[a, pl.ds(a * r4, r4)], quarters[t].at[a], fwd_s.at[2 * t + a],
                             fwd_r.at[2 * t + a], nbr[1 - a])
                cp.start()
                pending.append(cp)
                place(got, outs[t].at[nbr_j[a], pl.ds(c * r2, r2)], 4 * t + a)
        for a in range(2):
            for t in range(nt):
                r2, r4 = rows[t] // 2, rows[t] // 4
                got = quarters[t].at[a]
                _remote(got, got, fwd_s.at[2 * t + a], fwd_r.at[2 * t + a], nbr[1 - a]).wait_recv()
                place(got, outs[t].at[diag_j, pl.ds(c * r2 + a * r4, r4)], 4 * t + 2 + a)
        for mine, cp in placed:
            mine.wait()
            cp.wait()
        for cp in pending:
            cp.wait_send()
        for cp in local:
            cp.wait()

    stage = ([pltpu.VMEM((2, r // 2, D_MODEL), BF16) for r in rows] + [pltpu.VMEM((2, r // 4, D_MODEL), BF16) for r in rows])
    sems = ([pltpu.SemaphoreType.DMA((2 * nt,)) for _ in range(4)] + [pltpu.SemaphoreType.DMA((4 * nt,))] * 3
            + [pltpu.SemaphoreType.DMA((nt,))])
    return pl.pallas_call(
        body, name="gather_weights",
        out_shape=tuple(jax.ShapeDtypeStruct((N_CHIPS,) + s.shape, s.dtype) for s in shards),
        in_specs=[pl.BlockSpec(memory_space=pltpu.VMEM)] * nt,
        out_specs=tuple([ANY] * nt), scratch_shapes=stage + sems,
    )(*shards)


VMEM_WHOLE = pl.BlockSpec(memory_space=pltpu.VMEM)


def _pair_reduce(grads, name):
    nt = len(grads)
    r2 = [g.shape[2] for g in grads]
    off = [sum(r2[:t]) for t in range(nt)]
    tot = sum(r2)

    def body(*refs):
        gs = refs[:nt]
        s_ref, mine, got, ssem, rsem, lsem = refs[nt:]
        x, y, c = _me()
        sib = (x, y, 1 - c)
        for t in range(nt):
            for k in range(N_CHIPS):
                rows = pl.ds(off[t], r2[t])
                _remote(gs[t].at[k, 1 - c], got.at[k, rows], ssem, rsem, sib).start()
                pltpu.make_async_copy(gs[t].at[k, c], mine.at[k, rows], lsem).start()
        pltpu.make_async_copy(mine, mine, lsem).wait()
        _remote(got, got, ssem, rsem, sib).wait()
        for k in range(N_CHIPS):
            for st, sz in _row_pieces(tot, 4):
                rows = slice(st, st + sz)
                s_ref[k, rows, :] = (mine[k, rows, :].astype(F32) + got[k, rows, :].astype(F32)).astype(BF16)

    shp = jax.ShapeDtypeStruct((N_CHIPS, tot, D_MODEL), BF16)
    buf = pltpu.VMEM((N_CHIPS, tot, D_MODEL), BF16)
    return pl.pallas_call(
        body, name=name, out_shape=shp, in_specs=[ANY] * nt, out_specs=VMEM_WHOLE,
        scratch_shapes=[buf, buf, pltpu.SemaphoreType.DMA(()), pltpu.SemaphoreType.DMA(()),
                        pltpu.SemaphoreType.DMA(())],
        compiler_params=pltpu.CompilerParams(vmem_limit_bytes=VMEM_BIG),
    )(*grads)


def _final_reduce(own, rec, name):
    r2 = own.shape[0]
    stages = _row_pieces(r2, 2)

    def body(own_hbm, rec_hbm, o_ref, parts, fbuf, ssem, rsem, lsem, insems):
        x, y, c = _me()
        sib = (x, y, 1 - c)
        for p, (st, sz) in enumerate(stages):
            rows = pl.ds(st, sz)
            pltpu.make_async_copy(own_hbm.at[rows], parts.at[0, rows], insems.at[p]).start()
            for k in range(3):
                pltpu.make_async_copy(rec_hbm.at[k, rows], parts.at[1 + k, rows], insems.at[p]).start()
        for p, (st, sz) in enumerate(stages):
            stage = parts.at[:, pl.ds(st, sz)]
            pltpu.make_async_copy(stage, stage, insems.at[p]).wait()
            for s0, ssz in _row_pieces(sz, DMA_SPLIT // 2):
                rows = slice(st + s0, st + s0 + ssz)
                fbuf[rows, :] = (parts[0, rows, :].astype(F32) + parts[1, rows, :].astype(F32)
                                 + parts[2, rows, :].astype(F32) + parts[3, rows, :].astype(F32))
                dst = o_ref.at[c, pl.ds(st + s0, ssz)]
                pltpu.make_async_copy(fbuf.at[pl.ds(st + s0, ssz)], dst, lsem).start()
                _remote(fbuf.at[pl.ds(st + s0, ssz)], dst, ssem, rsem, sib).start()
        _remote(fbuf, o_ref.at[c], ssem, rsem, sib).wait()
        pltpu.make_async_copy(fbuf, o_ref.at[c], lsem).wait()

    return pl.pallas_call(
        body, name=name, out_shape=jax.ShapeDtypeStruct((2, r2, D_MODEL), F32),
        in_specs=[ANY, ANY], out_specs=ANY,
        scratch_shapes=[pltpu.VMEM((4, r2, D_MODEL), BF16), pltpu.VMEM((r2, D_MODEL), F32),
                        pltpu.SemaphoreType.DMA(()), pltpu.SemaphoreType.DMA(()), pltpu.SemaphoreType.DMA(()),
                        pltpu.SemaphoreType.DMA((2,))],
        compiler_params=pltpu.CompilerParams(vmem_limit_bytes=VMEM_BIG),
    )(own, rec)


SMALL_ROWS = 48


def _allreduce_small(g_ffn1, g_mix, g_ffn2, g_final, g_bin, dsink, bias_tab, loss_part):
    def body(f1_ref, mx_ref, f2_ref, fn_ref, bi_ref, sk_ref, bt_ref, ls_ref, o_ref, mine, buf, tabs, send_sems,
             recv_sems):
        x, y, c = _me()
        me = 4 * x + 2 * y + c
        mine[...] = jnp.zeros_like(mine)
        for r, ref in enumerate((f1_ref, mx_ref, f2_ref, fn_ref)):
            mine[r:r + 1, :] = ref[...]
        for k in range(D_IN // D_MODEL):
            mine[4 + k:5 + k, :] = bi_ref[:, k * D_MODEL:(k + 1) * D_MODEL]
        lane = lax.broadcasted_iota(jnp.int32, (1, 128), 1)
        row = jnp.where(lane == 8, ls_ref[0:1, :], 0.0)
        for h in range(8):
            row = jnp.where(lane == h, sk_ref[h, 0:1, :], row)
        mine[9:10, 0:128] = row
        buf[me] = mine[...]
        tabs[me] = bt_ref[...]
        copies = []
        for k in range(1, 8):
            peer = (_flip(x, (k >> 2) & 1), _flip(y, (k >> 1) & 1), _flip(c, k & 1))
            for t, (src, dst) in enumerate(((mine, buf), (bt_ref, tabs))):
                cp = _remote(src, dst.at[me], send_sems.at[2 * (k - 1) + t], recv_sems.at[2 * (k - 1) + t], peer)
                cp.start()
                copies.append(cp)
        for cp in copies:
            cp.wait()
        acc, tab = buf[0], tabs[0]
        for i in range(1, 8):
            acc, tab = acc + buf[i], tab + tabs[i]
        o_ref[...] = jnp.zeros_like(o_ref)
        o_ref[0:16, :] = acc
        o_ref[16:48, 0:128] = tab

    vm = pl.BlockSpec(memory_space=pltpu.VMEM)
    return pl.pallas_call(
        body, name="allreduce_small", out_shape=jax.ShapeDtypeStruct((SMALL_ROWS, D_MODEL), F32),
        in_specs=[vm] * 8, out_specs=vm,
        scratch_shapes=[pltpu.VMEM((16, D_MODEL), F32), pltpu.VMEM((8, 16, D_MODEL), F32),
                        pltpu.VMEM((8, N_BUCKETS, 128), F32), pltpu.SemaphoreType.DMA((14,)),
                        pltpu.SemaphoreType.DMA((14,))],
    )(g_ffn1, g_mix, g_ffn2, g_final, g_bin, dsink, bias_tab, loss_part)


def _adam_update(w, g, m, v):
    nm = ADAM_B1 * m + (1.0 - ADAM_B1) * g
    nv = ADAM_B2 * v + (1.0 - ADAM_B2) * (g * g)
    bc1 = 1.0 - ADAM_B1 ** ADAM_STEP
    bc2 = 1.0 - ADAM_B2 ** ADAM_STEP
    return -ADAM_LR * ((nm / bc1) / (jnp.sqrt(nv / bc2) + ADAM_EPS) + ADAM_WD * w), nm, nv


def _adamw_small(packed, w, m, v):
    names = ("ffn1_norm", "mix_norm", "ffn2_norm", "final_norm", "b_in", "sinks", "rel_bias")
    nn = len(names)

    def grad_of(p_ref, name, k=0):
        if name == "b_in":
            return p_ref[4 + k:5 + k, :]
        if name == "sinks":
            return p_ref[9:10, 0:8]
        if name == "rel_bias":
            return p_ref[16:48, 0:20]
        r = names.index(name)
        return p_ref[r:r + 1, :]

    def body(p_ref, *refs):
        ws, ms, vs = refs[:nn], refs[nn:2 * nn], refs[2 * nn:3 * nn]
        outs = refs[3 * nn:]
        for i, name in enumerate(names):
            og, od, om, ov = outs[i], outs[nn + i], outs[2 * nn + i], outs[3 * nn + i]
            pieces = range(D_IN // D_MODEL) if name == "b_in" else (0,)
            for k in pieces:
                sl = (slice(None), slice(k * D_MODEL, (k + 1) * D_MODEL)) if name == "b_in" else (Ellipsis,)
                g = grad_of(p_ref, name, k)
                d, nm, nv = _adam_update(ws[i][sl], g, ms[i][sl], vs[i][sl])
                og[sl], od[sl], om[sl], ov[sl] = g, d, nm, nv

    vm = pl.BlockSpec(memory_space=pltpu.VMEM)
    shapes = [jax.ShapeDtypeStruct(w[n].shape, F32) for n in names]
    res = pl.pallas_call(
        body, name="adamw_small", out_shape=tuple(shapes * 4), in_specs=[vm] * (1 + 3 * nn),
        out_specs=tuple([vm] * (4 * nn)),
    )(packed, *[w[n] for n in names], *[m[n] for n in names], *[v[n] for n in names])
    return [dict(zip(names, res[i * nn:(i + 1) * nn])) for i in range(4)]


class _GatherRider:
    def __init__(self, shards):
        self.inputs = list(shards)
        nt = len(shards)
        self.out_shape = [jax.ShapeDtypeStruct((N_CHIPS,) + s.shape, s.dtype) for s in shards]
        self.scratch = [pltpu.SemaphoreType.DMA((3 * nt,)), pltpu.SemaphoreType.DMA((3 * nt,)),
                        pltpu.SemaphoreType.DMA((nt,))]

    def _copies(self, srcs, outs, sems):
        ici_s, ici_r, loc = sems
        x, y, c = _me()
        j = 2 * x + y
        local = [pltpu.make_async_copy(srcs[t], outs[t].at[j], loc.at[t]) for t in range(len(srcs))]
        remote = []
        for k, (fx, fy) in enumerate(_CHIP_RELS):
            peer = (_flip(x, fx), _flip(y, fy), c)
            for t in range(len(srcs)):
                remote.append(_remote(srcs[t], outs[t].at[j], ici_s.at[3 * t + k], ici_r.at[3 * t + k], peer))
        return local, remote

    def start(self, srcs, outs, sems):
        local, remote = self._copies(srcs, outs, sems)
        for cp in local + remote:
            cp.start()

    def finish(self, srcs, outs, sems):
        local, remote = self._copies(srcs, outs, sems)
        for cp in remote + local:
            cp.wait()


class _PairRider:
    def __init__(self, grad):
        self.inputs = [grad]
        shape = (N_CHIPS,) + grad.shape[2:]
        self.out_shape = [jax.ShapeDtypeStruct(shape, BF16)]
        self.scratch = [pltpu.VMEM(shape, BF16), pltpu.VMEM(shape, BF16)] + [pltpu.SemaphoreType.DMA(())] * 4

    def start(self, ins, outs, scr):
        mine, got, ssem, rsem, lsem, _ = scr
        x, y, c = _me()
        for k in range(N_CHIPS):
            _remote(ins[0].at[k, 1 - c], got.at[k], ssem, rsem, (x, y, 1 - c)).start()
            pltpu.make_async_copy(ins[0].at[k, c], mine.at[k], lsem).start()

    def finish(self, ins, outs, scr):
        mine, got, ssem, rsem, lsem, osem = scr
        x, y, c = _me()
        pltpu.make_async_copy(mine, mine, lsem).wait()
        _remote(got, got, ssem, rsem, (x, y, 1 - c)).wait()
        for k in range(N_CHIPS):
            for st, sz in _row_pieces(mine.shape[1], 4):
                rows = slice(st, st + sz)
                mine[k, rows, :] = (mine[k, rows, :].astype(F32) + got[k, rows, :].astype(F32)).astype(BF16)
        out = pltpu.make_async_copy(mine, outs[0], osem)
        out.start()
        out.wait()


class _ExchangeRider:
    def __init__(self, parts):
        self.inputs = list(parts)
        self.r2 = [p.shape[1] for p in parts]
        self.off = [sum(self.r2[:g]) for g in range(len(parts))]
        tot = sum(self.r2)
        self.out_shape = [jax.ShapeDtypeStruct((tot, D_MODEL), BF16), jax.ShapeDtypeStruct((3, tot, D_MODEL), BF16)]
        self.scratch = [pltpu.SemaphoreType.DMA((3,)), pltpu.SemaphoreType.DMA((3,)), pltpu.SemaphoreType.DMA(())]

    def start(self, ps, outs, sems):
        own_ref, rec_ref = outs
        ssems, rsems, lsem = sems
        x, y, c = _me()
        j = 2 * x + y
        for g in range(len(ps)):
            pltpu.make_async_copy(ps[g].at[j], own_ref.at[pl.ds(self.off[g], self.r2[g])], lsem).start()
        for k, (fx, fy) in enumerate(_CHIP_RELS):
            px, py = _flip(x, fx), _flip(y, fy)
            for g in range(len(ps)):
                for st, sz in _row_pieces(self.r2[g], 2):
                    _remote(ps[g].at[2 * px + py, pl.ds(st, sz)], rec_ref.at[k, pl.ds(self.off[g] + st, sz)],
                            ssems.at[k], rsems.at[k], (px, py, c)).start()

    def finish(self, ps, outs, sems):
        own_ref, rec_ref = outs
        ssems, rsems, lsem = sems
        x, y, c = _me()
        for k in range(3):
            _remote(rec_ref.at[k], rec_ref.at[k], ssems.at[k], rsems.at[k], (x, y, c)).wait()
        pltpu.make_async_copy(own_ref, own_ref, lsem).wait()


def _pallas(body, args, *, name, grid, in_specs, out_specs, out_shape, scratch_shapes=(), sem=None, vmem=None,
            rider=None):
    if rider is None:
        res = pl.pallas_call(body, name=name, grid=grid, in_specs=list(in_specs), out_specs=tuple(out_specs),
                             out_shape=tuple(out_shape), scratch_shapes=list(scratch_shapes),
                             compiler_params=_params(sem, vmem))(*args)
        return tuple(res), ()
    n_in, n_out, n_sc = len(in_specs), len(out_shape), len(scratch_shapes)
    r_in, r_out = len(rider.inputs), len(rider.out_shape)

    def wrapped(*refs):
        ins, rins = refs[:n_in], refs[n_in:n_in + r_in]
        p = n_in + r_in
        outs, routs = refs[p:p + n_out], refs[p + n_out:p + n_out + r_out]
        p += n_out + r_out
        scr, rsems = refs[p:p + n_sc], refs[p + n_sc:]
        first = pl.program_id(0) == 0
        last = pl.program_id(0) == grid[0] - 1
        for a in range(1, len(grid)):
            first = first & (pl.program_id(a) == 0)
            last = last & (pl.program_id(a) == grid[a] - 1)

        @pl.when(first)
        def _():
            rider.start(rins, routs, rsems)

        body(*ins, *outs, *scr)

        @pl.when(last)
        def _():
            rider.finish(rins, routs, rsems)

    res = pl.pallas_call(
        wrapped, name=name, grid=grid, in_specs=list(in_specs) + [ANY] * r_in,
        out_specs=tuple(out_specs) + (ANY,) * r_out, out_shape=tuple(out_shape) + tuple(rider.out_shape),
        scratch_shapes=list(scratch_shapes) + rider.scratch,
        compiler_params=_params(("arbitrary",) * len(grid), vmem))(*args, *rider.inputs)
    return tuple(res[:n_out]), tuple(res[n_out:])


def _load_weights(pairs, sems):
    copies = [pltpu.make_async_copy(hbm, vmem, sems.at[i]) for i, (hbm, vmem) in enumerate(pairs)]
    for cp in copies:
        cp.start()
    for cp in copies:
        cp.wait()


def _loss_tile(hh, gain, tgt):
    r = lax.rsqrt(jnp.mean(hh * hh, axis=-1, keepdims=True) + EPS)
    hn = hh * r
    err = hn * gain - tgt
    part = (0.5 / D_MODEL) * jnp.sum(jnp.sum(err * err, axis=1, keepdims=True), axis=0, keepdims=True)
    dy = err * (1.0 / D_MODEL)
    dng = dy * gain
    dh = r * (dng - hn * jnp.mean(dng * hn, axis=-1, keepdims=True))
    return dh, part, jnp.sum(dy * hn, axis=0, keepdims=True)


def _ffn_fwd(h, gain, wgt, wut, wd, rider=None, head=None):
    t = h.shape[0]

    def body(h_ref, gain_ref, wg_hbm, wu_hbm, wd_hbm, *rest):
        if head is None:
            hout_ref, n_ref, g_ref, u_ref, a_ref, wg_v, wu_v, wd_v, wsem = rest
        else:
            fg_ref, tgt_ref, hout_ref, n_ref, g_ref, u_ref, a_ref, loss_ref, gg_ref, wg_v, wu_v, wd_v, wsem = rest
        @pl.when(pl.program_id(0) == 0)
        def _():
            _load_weights(((wg_hbm, wg_v), (wu_hbm, wu_v), (wd_hbm, wd_v)), wsem)
            if head is not None:
                loss_ref[...] = jnp.zeros_like(loss_ref)
                gg_ref[...] = jnp.zeros_like(gg_ref)

        hh = h_ref[...]
        r = lax.rsqrt(jnp.mean(hh * hh, axis=-1, keepdims=True) + EPS)
        n = (hh * r * gain_ref[...]).astype(BF16)
        n_ref[...] = n
        acc = jnp.zeros((TM, D_MODEL), F32)
        for c0, c1 in zip(FF_BOUNDS[:-1], FF_BOUNDS[1:]):
            sl = slice(c0, c1)
            g = _dot_nt(n, wg_v[sl, :])
            u = _dot_nt(n, wu_v[sl, :])
            sg = _sigmoid(g)
            silu = g * sg
            a = (silu * u).astype(BF16)
            a_ref[:, sl] = a
            g_ref[:, sl] = (u * (sg * (1.0 + g * (1.0 - sg)))).astype(BF16)
            u_ref[:, sl] = silu.astype(BF16)
            acc = acc + _dot(a, wd_v[sl, :])
        hout = hh + 0.5 * acc
        if head is None:
            hout_ref[...] = hout
        else:
            dh, part, gpart = _loss_tile(hout, fg_ref[...], tgt_ref[...])
            hout_ref[...] = dh
            loss_ref[...] += part
            gg_ref[...] += gpart

    row = lambda w: pl.BlockSpec((TM, w), lambda i: (i, 0))
    vec = pl.BlockSpec((1, D_MODEL), lambda i: (0, 0))
    wv = pltpu.VMEM((D_FF, D_MODEL), BF16)
    args, in_specs = (h, gain, wgt, wut, wd), [row(D_MODEL), vec, ANY, ANY, ANY]
    out_shape = [jax.ShapeDtypeStruct((t, D_MODEL), F32), jax.ShapeDtypeStruct((t, D_MODEL), BF16)] + [
        jax.ShapeDtypeStruct((t, D_FF), BF16)] * 3
    out_specs = [row(D_MODEL), row(D_MODEL), row(D_FF), row(D_FF), row(D_FF)]
    if head is not None:
        args, in_specs = args + tuple(head), in_specs + [vec, row(D_MODEL)]
        out_shape += [jax.ShapeDtypeStruct((8, 128), F32), jax.ShapeDtypeStruct((1, D_MODEL), F32)]
        out_specs += [pl.BlockSpec((8, 128), lambda i: (0, 0)), vec]
    return _pallas(
        body, args, name="ffn_fwd", grid=(t // TM,), out_shape=tuple(out_shape), in_specs=in_specs,
        out_specs=tuple(out_specs), scratch_shapes=[wv, wv, wv, pltpu.SemaphoreType.DMA((3,))],
        sem=("arbitrary",), vmem=VMEM_BIG, rider=rider)


def _ffn_bwd(dhout, h, gain, dgf, duf, wgt, wut, wd):
    t = h.shape[0]
    tm = TM_BWD

    def body(dho_ref, h_ref, gain_ref, g_ref, u_ref, wg_hbm, wu_hbm, wd_hbm,
             dh_ref, dg_ref, du_ref, df_ref, gg_ref, wg_v, wu_v, wd_v, wsem):
        @pl.when(pl.program_id(0) == 0)
        def _():
            _load_weights(((wd_hbm, wd_v), (wg_hbm, wg_v), (wu_hbm, wu_v)), wsem)
            gg_ref[...] = jnp.zeros_like(gg_ref)

        dho = dho_ref[...]
        df = (0.5 * dho).astype(BF16)
        df_ref[...] = df
        dn = jnp.zeros((tm, D_MODEL), F32)
        for c0, c1 in zip(FF_BOUNDS[:-1], FF_BOUNDS[1:]):
            sl = slice(c0, c1)
            da = _dot_nt(df, wd_v[sl, :])
            dg = (da * g_ref[:, sl].astype(F32)).astype(BF16)
            du = (da * u_ref[:, sl].astype(F32)).astype(BF16)
            dg_ref[:, sl] = dg
            du_ref[:, sl] = du
            dn = dn + _dot(dg, wg_v[sl, :]) + _dot(du, wu_v[sl, :])
        hh = h_ref[...]
        r = lax.rsqrt(jnp.mean(hh * hh, axis=-1, keepdims=True) + EPS)
        hn = hh * r
        gg_ref[...] += jnp.sum(dn * hn, axis=0, keepdims=True)
        dng = dn * gain_ref[...]
        dh_ref[...] = dho + r * (dng - hn * jnp.mean(dng * hn, axis=-1, keepdims=True))

    row = lambda w: pl.BlockSpec((tm, w), lambda i: (i, 0))
    vec = pl.BlockSpec((1, D_MODEL), lambda i: (0, 0))
    wv = pltpu.VMEM((D_FF, D_MODEL), BF16)
    return pl.pallas_call(
        body, name="ffn_bwd", grid=(t // tm,),
        out_shape=(jax.ShapeDtypeStruct((t, D_MODEL), F32), jax.ShapeDtypeStruct((t, D_FF), BF16),
                   jax.ShapeDtypeStruct((t, D_FF), BF16),
                   jax.ShapeDtypeStruct((t, D_MODEL), BF16), jax.ShapeDtypeStruct((1, D_MODEL), F32)),
        in_specs=[row(D_MODEL), row(D_MODEL), vec, row(D_FF), row(D_FF), ANY, ANY, ANY],
        out_specs=(row(D_MODEL), row(D_FF), row(D_FF), row(D_MODEL), vec),
        scratch_shapes=[wv, wv, wv, pltpu.SemaphoreType.DMA((3,))],
        compiler_params=_params(("arbitrary",), VMEM_BIG),
    )(dhout, h, gain, dgf, duf, wgt, wut, wd)


def _wgrad(lhs, rhs, rb, with_colsum=False, name="wgrad", rider=None):
    t, k = lhs.shape
    n = rhs.shape[1]

    def body(l_ref, r_ref, o_ref, *rest):
        o_ref[...] = _dot_tn(l_ref[...], r_ref[...]).astype(BF16)
        if with_colsum:
            rest[0][...] = jnp.sum(l_ref[...].astype(F32), axis=0, keepdims=True)

    out_shape = [jax.ShapeDtypeStruct((k, n), BF16)]
    out_specs = [pl.BlockSpec((rb, n), lambda j: (j, 0))]
    if with_colsum:
        out_shape.append(jax.ShapeDtypeStruct((1, k), F32))
        out_specs.append(pl.BlockSpec((1, rb), lambda j: (0, j)))
    res, ro = _pallas(
        body, (lhs, rhs), name=name, grid=(k // rb,), out_shape=tuple(out_shape),
        in_specs=[pl.BlockSpec((t, rb), lambda j: (0, j)), pl.BlockSpec((t, n), lambda j: (0, 0))],
        out_specs=tuple(out_specs), sem=("arbitrary",), vmem=VMEM_BIG, rider=rider)
    if rider is not None:
        return res[0], ro
    return res if with_colsum else res[0]


def _lane_blocks(nseq, seq, nblk, tm=TM):
    spt = seq // tm
    return pl.BlockSpec((1, nblk, tm, 128), lambda i: (i // spt, 0, i % spt, 0))


def _inproj_fwd(h, gain, wint, b_in, nseq, rider=None):
    t = h.shape[0]
    seq = t // nseq
    cut_a = 5 * MXU_DIM
    pieces = ((0, cut_a, 0, 0), (cut_a, ZA_W - cut_a, 0, cut_a), (ZA_W, ZB_W, 1, 0), (ZA_W + ZB_W, 1024, 2, 0),
              (ZA_W + ZB_W + 1024, 1024, 2, 1024))

    def body(h_ref, gain_ref, w_hbm, b_ref, u_ref, za_ref, zb_ref, zg_ref, w_v):
        @pl.when(pl.program_id(0) == 0)
        def _():
            pltpu.sync_copy(w_hbm, w_v)

        hh = h_ref[...]
        r = lax.rsqrt(jnp.mean(hh * hh, axis=-1, keepdims=True) + EPS)
        un = (hh * r * gain_ref[...]).astype(BF16)
        u_ref[...] = un
        outs = (None, zb_ref, zg_ref)
        for c0, cw, oi, o0 in pieces:
            val = _dot_nt(un, w_v[c0:c0 + cw, :]) + b_ref[:, c0:c0 + cw]
            if oi == 0:
                for cb in range(cw // 128):
                    za_ref[0, o0 // 128 + cb] = val[:, cb * 128:(cb + 1) * 128]
            else:
                outs[oi][:, o0:o0 + cw] = val.astype(BF16)

    row = lambda w: pl.BlockSpec((TM, w), lambda i: (i, 0))
    return _pallas(
        body, (h, gain, wint, b_in), name="inproj_fwd", grid=(t // TM,),
        out_shape=(jax.ShapeDtypeStruct((t, D_MODEL), BF16), jax.ShapeDtypeStruct((nseq, ZA_W // 128, seq, 128), F32),
                   jax.ShapeDtypeStruct((t, ZB_W), BF16), jax.ShapeDtypeStruct((t, 2 * D_MODEL), BF16)),
        in_specs=[row(D_MODEL), pl.BlockSpec((1, D_MODEL), lambda i: (0, 0)), ANY,
                  pl.BlockSpec((1, D_IN), lambda i: (0, 0))],
        out_specs=(row(D_MODEL), _lane_blocks(nseq, seq, ZA_W // 128), row(ZB_W), row(2 * D_MODEL)),
        scratch_shapes=[pltpu.VMEM((D_IN, D_MODEL), BF16)], sem=("arbitrary",), vmem=VMEM_BIG, rider=rider)


def _inproj_bwd(dz, dh2, h, gain, wint, rider=None):
    t = h.shape[0]
    nc = 5
    cw = D_IN // nc

    def body(dz_ref, dh2_ref, h_ref, gain_ref, w_hbm, dh_ref, gg_ref, w_v):
        @pl.when(pl.program_id(0) == 0)
        def _():
            pltpu.sync_copy(w_hbm, w_v)
            gg_ref[...] = jnp.zeros_like(gg_ref)

        du = jnp.zeros((TM, D_MODEL), F32)
        for ci in range(nc):
            sl = slice(ci * cw, (ci + 1) * cw)
            du = du + _dot(dz_ref[:, sl], w_v[sl, :])
        hh = h_ref[...]
        r = lax.rsqrt(jnp.mean(hh * hh, axis=-1, keepdims=True) + EPS)
        hn = hh * r
        gg_ref[...] += jnp.sum(du * hn, axis=0, keepdims=True)
        dng = du * gain_ref[...]
        dh_ref[...] = dh2_ref[...] + r * (dng - hn * jnp.mean(dng * hn, axis=-1, keepdims=True))

    row = lambda w: pl.BlockSpec((TM, w), lambda i: (i, 0))
    vec = pl.BlockSpec((1, D_MODEL), lambda i: (0, 0))
    return _pallas(
        body, (dz, dh2, h, gain, wint), name="inproj_bwd", grid=(t // TM,),
        out_shape=(jax.ShapeDtypeStruct((t, D_MODEL), F32), jax.ShapeDtypeStruct((1, D_MODEL), F32)),
        in_specs=[row(D_IN), row(D_MODEL), row(D_MODEL), vec, ANY],
        out_specs=(row(D_MODEL), vec),
        scratch_shapes=[pltpu.VMEM((D_IN, D_MODEL), BF16)], sem=("arbitrary",), vmem=VMEM_BIG, rider=rider)


def _head_sums(x):
    w = x.shape[1]
    i = lax.broadcasted_iota(jnp.int32, (w, w), 0) // HEAD_DIM
    j = lax.broadcasted_iota(jnp.int32, (w, w), 1) // HEAD_DIM
    ones = (i == j).astype(BF16)
    hi = x.astype(BF16)
    r1 = x - hi.astype(F32)
    mid = r1.astype(BF16)
    lo = (r1 - mid.astype(F32)).astype(BF16)
    return _dot(hi, ones) + _dot(mid, ones) + _dot(lo, ones)


def _merge_fwd(o0, o1, o2, l0, l1, l2, yb, zg, h1, wat, wbt, wout, rider=None):
    t = h1.shape[0]
    nseq, _, seq, _ = o0.shape

    def body(o0_ref, o1_ref, o2_ref, l0_ref, l1_ref, l2_ref, yb_ref, ga_ref, gb_ref, h1_ref, wa_ref, wb_ref, wo_ref,
             h2_ref, y_ref, lt_ref, pa_ref, pb_ref, mg_ref):
        wide = lambda ref: jnp.concatenate([ref[0, 0], ref[0, 1]], axis=1)
        la, lb, lc = wide(l0_ref), wide(l1_ref), wide(l2_ref)
        mx = jnp.maximum(jnp.maximum(la, lb), lc)
        ea, eb, ec = jnp.exp(la - mx), jnp.exp(lb - mx), jnp.exp(lc - mx)
        den = ea + eb + ec
        y = (ea * wide(o0_ref) + eb * wide(o1_ref) + ec * wide(o2_ref)) / den
        lt = mx + jnp.log(den)
        lt_ref[0, 0] = lt[:, :128]
        lt_ref[0, 1] = lt[:, 128:]
        yb16 = y.astype(BF16)
        y_ref[...] = yb16
        pa = _dot_nt(yb16, wa_ref[...])
        pb = _dot_nt(yb_ref[...], wb_ref[...])
        pa_ref[...] = pa.astype(BF16)
        pb_ref[...] = pb.astype(BF16)
        mg = (_sigmoid(ga_ref[...].astype(F32)) * pa + _sigmoid(gb_ref[...].astype(F32)) * pb).astype(BF16)
        mg_ref[...] = mg
        h2_ref[...] = h1_ref[...] + _dot(mg, wo_ref[...])

    row = lambda w: pl.BlockSpec((TM, w), lambda i: (i, 0))
    full = lambda a: pl.BlockSpec(a.shape, lambda i: (0, 0))
    gate = lambda cb: pl.BlockSpec((TM, D_MODEL), lambda i: (i, cb))
    return _pallas(
        body, (o0, o1, o2, l0, l1, l2, yb, zg, zg, h1, wat, wbt, wout), name="merge_fwd", grid=(t // TM,),
        out_shape=(jax.ShapeDtypeStruct((t, D_MODEL), F32), jax.ShapeDtypeStruct((t, GW), BF16),
                   jax.ShapeDtypeStruct((nseq, 2, seq, 128), F32), jax.ShapeDtypeStruct((t, D_MODEL), BF16),
                   jax.ShapeDtypeStruct((t, D_MODEL), BF16), jax.ShapeDtypeStruct((t, D_MODEL), BF16)),
        in_specs=[_lane_blocks(nseq, seq, 2)] * 6 + [row(2 * GW), gate(0), gate(1), row(D_MODEL), full(wat), full(wbt),
                                                     full(wout)],
        out_specs=(row(D_MODEL), row(GW), _lane_blocks(nseq, seq, 2), row(D_MODEL), row(D_MODEL), row(D_MODEL)),
        sem=("parallel",), vmem=VMEM_BIG, rider=rider)


def _merge_bwd(dh2, pa, pb, zg, y, yb, wat, wbt, wout, nseq, rider=None):
    t = dh2.shape[0]

    def body(dh2_ref, pa_ref, pb_ref, ga_ref, gb_ref, y_ref, yb_ref, wa_ref, wb_ref, wo_ref,
             dpa_ref, dpb_ref, dga_ref, dgb_ref, dya_ref, dyb_ref, dh2b_ref, ca_ref, cb_ref):
        d16 = dh2_ref[...].astype(BF16)
        dh2b_ref[...] = d16
        dm = _dot_nt(d16, wo_ref[...])
        sa = _sigmoid(ga_ref[...].astype(F32))
        sb = _sigmoid(gb_ref[...].astype(F32))
        dpa = (dm * sa).astype(BF16)
        dpb = (dm * sb).astype(BF16)
        dpa_ref[...] = dpa
        dpb_ref[...] = dpb
        dga_ref[...] = (dm * pa_ref[...].astype(F32) * sa * (1.0 - sa)).astype(BF16)
        dgb_ref[...] = (dm * pb_ref[...].astype(F32) * sb * (1.0 - sb)).astype(BF16)
        dya = _dot(dpa, wa_ref[...])
        dyb = _dot(dpb, wb_ref[...])
        dya_ref[0, 0] = dya[:, :128]
        dya_ref[0, 1] = dya[:, 128:]
        dyb_ref[...] = dyb.astype(BF16)
        ca = _head_sums(dya * y_ref[...].astype(F32))
        ca_ref[0, 0] = ca[:, :128]
        ca_ref[0, 1] = ca[:, 128:]
        cb_ref[...] = _head_sums(dyb * yb_ref[...].astype(F32))

    row = lambda w: pl.BlockSpec((TM, w), lambda i: (i, 0))
    full = lambda a: pl.BlockSpec(a.shape, lambda i: (0, 0))
    gate = lambda cb: pl.BlockSpec((TM, D_MODEL), lambda i: (i, cb))
    bf = lambda w: jax.ShapeDtypeStruct((t, w), BF16)
    lanes = jax.ShapeDtypeStruct((nseq, 2, t // nseq, 128), F32)
    lane_spec = _lane_blocks(nseq, t // nseq, 2)
    return _pallas(
        body, (dh2, pa, pb, zg, zg, y, yb, wat, wbt, wout), name="merge_bwd", grid=(t // TM,),
        out_shape=(bf(D_MODEL), bf(D_MODEL), bf(D_MODEL), bf(D_MODEL), lanes, bf(2 * GW), bf(D_MODEL),
                   lanes, jax.ShapeDtypeStruct((t, 2 * GW), F32)),
        in_specs=[row(D_MODEL), row(D_MODEL), row(D_MODEL), gate(0), gate(1), row(GW), row(2 * GW),
                  full(wat), full(wbt), full(wout)],
        out_specs=(row(D_MODEL), row(D_MODEL), row(D_MODEL), row(D_MODEL), lane_spec, row(2 * GW), row(D_MODEL),
                   lane_spec, row(2 * GW)),
        sem=("parallel",), vmem=VMEM_BIG, rider=rider)


def _lane_head(rows):
    return lax.broadcasted_iota(jnp.int32, (rows, GW), 1) // HEAD_DIM


def _kv_expand_matrix(r):
    ci = lax.broadcasted_iota(jnp.int32, (2 * HEAD_DIM, GW), 0)
    ji = lax.broadcasted_iota(jnp.int32, (2 * HEAD_DIM, GW), 1)
    return (ci == (ji % HEAD_DIM) + HEAD_DIM * r).astype(BF16)


def _block_rows(row0, stride, ib):
    start = row0 + (stride * BLOCK) * ib
    if stride > 1:
        return pl.ds(start, BLOCK, stride=stride)
    return pl.ds(pl.multiple_of(start, BLOCK), BLOCK)


def _stack_heads(x, lane_head):
    return jnp.concatenate([jnp.where(lane_head == h, x, jnp.zeros_like(x)) for h in range(4)], axis=0)


def _unstack_heads(x4, lane_head):
    out = jnp.zeros((BLOCK, GW), F32)
    for h in range(4):
        out = jnp.where(lane_head == h, x4[h * BLOCK:(h + 1) * BLOCK], out)
    return out


def _load_rows(ref, rows, split):
    if split:
        return jnp.concatenate([ref[0, 0, rows, :], ref[0, 1, rows, :]], axis=1)
    return ref[0, rows, :]


def _store_rows(ref, rows, val, split):
    if split:
        ref[0, 0, rows, :] = val[:, :128]
        ref[0, 1, rows, :] = val[:, 128:]
    else:
        ref[0, rows, :] = val


def _attn_fwd(q_arr, k_arr, v_arr, bias, sink, *, grid, seq, stride, kvw, split, q_spec, k_spec, v_spec, bias_map,
              sink_map, o_spec, has_sink, o_shape, o_dtype, name, rider=None):
    nb = seq // stride // BLOCK
    scale = HEAD_DIM ** -0.5
    expanded = kvw != GW
    rps = min(stride, RESIDUES_PER_STEP)
    grid = (grid[0], grid[1] // rps)
    assert not has_sink or B_WINDOW - 1 < BLOCK

    def body(q_ref, k_ref, v_ref, bias_ref, sink_ref, o_ref, lse_ref, *kv_x):
        rr = pl.program_id(1)
        lane_head = _lane_head(BLOCK)
        if expanded:
            expand = _kv_expand_matrix(rr)
            kv_x[0][...] = _dot(k_ref[0], expand).astype(BF16)
            kv_x[1][...] = _dot(v_ref[0], expand).astype(BF16)
        for j in range(rps):
            residue(rr * rps + j if stride > 1 else 0, q_ref, k_ref, v_ref, bias_ref, sink_ref, o_ref, lse_ref, kv_x,
                    lane_head)

    def residue(row0, q_ref, k_ref, v_ref, bias_ref, sink_ref, o_ref, lse_ref, kv_x, lane_head):
        def per_head(fn, x):
            return jnp.concatenate([fn(sink_ref[0, h:h + 1, 0:1], x[h * BLOCK:(h + 1) * BLOCK]) for h in range(4)],
                                   axis=0)

        def load(ref, ib):
            return _load_rows(ref, _block_rows(row0, stride, ib), split).astype(BF16)

        def load_kv(which, ib):
            if expanded:
                return kv_x[which][_block_rows(0, 1, ib), :]
            return load((k_ref, v_ref)[which], ib)

        def block(ib, first):
            q4 = _stack_heads(load(q_ref, ib), lane_head)
            if first:
                kc, vc = load_kv(0, ib), load_kv(1, ib)
                b4 = bias_ref[:, :, BLOCK:].reshape(4 * BLOCK, BLOCK)
            else:
                kc = jnp.concatenate([load_kv(0, ib - 1), load_kv(0, ib)], axis=0)
                vc = jnp.concatenate([load_kv(1, ib - 1), load_kv(1, ib)], axis=0)
                b4 = bias_ref[...].reshape(4 * BLOCK, 2 * BLOCK)
                if has_sink:
                    oldest = lax.broadcasted_iota(jnp.int32, kc.shape, 0) == 0
                    kc = jnp.where(oldest, jnp.zeros_like(kc), kc)
                    vc = jnp.where(oldest, jnp.zeros_like(vc), vc)
            s = _dot_nt(q4, kc) * scale + b4
            m = jnp.max(s, axis=-1, keepdims=True)
            if has_sink and first:
                m = per_head(jnp.maximum, m)
            p = jnp.exp(s - m)
            l = jnp.sum(p, axis=-1, keepdims=True)
            if has_sink and first:
                l = l + per_head(lambda sk, mh: jnp.exp(sk - mh), m)
            o4 = _dot(p.astype(BF16), vc) / l
            rows = _block_rows(row0, stride, ib)
            _store_rows(o_ref, rows, _unstack_heads(o4, lane_head).astype(o_dtype), split)
            _store_rows(lse_ref, rows, _unstack_heads(m + jnp.log(l), lane_head), split)

        block(0, True)
        if nb > 1:
            def step(i, carry):
                block(i, False)
                return carry
            lax.fori_loop(1, nb, step, 0, unroll=min(ATTN_UNROLL, nb - 1))

    return _pallas(
        body, (q_arr, k_arr, v_arr, bias, sink), name=name, grid=grid,
        out_shape=(jax.ShapeDtypeStruct(o_shape, o_dtype), jax.ShapeDtypeStruct(o_shape, F32)),
        in_specs=[q_spec, k_spec, v_spec,
                  pl.BlockSpec((4, BLOCK, 2 * BLOCK), bias_map), pl.BlockSpec((1, 4, 128), sink_map)],
        out_specs=(o_spec, o_spec),
        scratch_shapes=[pltpu.VMEM((seq, GW), BF16)] * 2 if expanded else [],
        sem=("arbitrary", "arbitrary"), vmem=VMEM_BIG, rider=rider)


def _attn_bwd(q_arr, k_arr, v_arr, bias, sink, dy, cc, lse, *, grid, seq, stride, kvw, split, q_spec, k_spec, v_spec,
              bias_map, sink_map, o_spec, kv_out_spec, has_sink, n_bias, dq_shape, dkv_shape, g_dtype, name):
    ln = seq // stride
    nb = ln // BLOCK
    scale = HEAD_DIM ** -0.5
    expanded = kvw != GW
    rps = min(stride, RESIDUES_PER_STEP)
    grid = (grid[0], grid[1] // rps)

    def body(q_ref, k_ref, v_ref, bias_ref, sink_ref, dy_ref, c_ref, lse_ref,
             dq_ref, dk_ref, dv_ref, db_ref, dsk_ref, dk_acc, dv_acc, dk_half, dv_half, *kv_x):
        rr = pl.program_id(1)

        @pl.when((pl.program_id(0) == 0) & (rr == 0))
        def _():
            db_ref[...] = jnp.zeros_like(db_ref)
            dsk_ref[...] = jnp.zeros_like(dsk_ref)

        if expanded:
            expand = _kv_expand_matrix(rr)
            kv_x[0][...] = _dot(k_ref[0], expand).astype(BF16)
            kv_x[1][...] = _dot(v_ref[0], expand).astype(BF16)
        refs = (q_ref, k_ref, v_ref, bias_ref, sink_ref, dy_ref, c_ref, lse_ref, dq_ref, dk_ref, dv_ref, db_ref,
                dsk_ref, dk_acc, dv_acc, dk_half, dv_half, kv_x)
        for j in range(rps):
            residue(rr, rr * rps + j if stride > 1 else 0, *refs)

    def residue(rr, row0, q_ref, k_ref, v_ref, bias_ref, sink_ref, dy_ref, c_ref, lse_ref,
                dq_ref, dk_ref, dv_ref, db_ref, dsk_ref, dk_acc, dv_acc, dk_half, dv_half, kv_x):
        dk_acc[...] = jnp.zeros_like(dk_acc)
        dv_acc[...] = jnp.zeros_like(dv_acc)
        lane_head = _lane_head(BLOCK)
        hb = 4 * rr if n_bias == 8 else 0

        def load(ref, ib):
            return _load_rows(ref, _block_rows(row0, stride, ib), split)

        def load_kv(which, ib):
            if expanded:
                return kv_x[which][_block_rows(0, 1, ib), :]
            return load((k_ref, v_ref)[which], ib).astype(BF16)

        def head_col(x):
            return jnp.concatenate([x[:, h * HEAD_DIM:h * HEAD_DIM + 1] for h in range(4)], axis=0)

        def block(ib, first):
            q4 = _stack_heads(load(q_ref, ib).astype(BF16), lane_head)
            dy4 = _stack_heads(load(dy_ref, ib).astype(BF16), lane_head)
            c4 = head_col(load(c_ref, ib))
            l4 = head_col(load(lse_ref, ib))
            if first:
                kc, vc = load_kv(0, ib), load_kv(1, ib)
                b4 = bias_ref[:, :, BLOCK:].reshape(4 * BLOCK, BLOCK)
                krows = pl.ds(0, BLOCK)
            else:
                kc = jnp.concatenate([load_kv(0, ib - 1), load_kv(0, ib)], axis=0)
                vc = jnp.concatenate([load_kv(1, ib - 1), load_kv(1, ib)], axis=0)
                b4 = bias_ref[...].reshape(4 * BLOCK, 2 * BLOCK)
                krows = pl.ds(pl.multiple_of((ib - 1) * BLOCK, BLOCK), 2 * BLOCK)
            nk = BLOCK if first else 2 * BLOCK
            p = jnp.exp(_dot_nt(q4, kc) * scale + b4 - l4)
            ds = p * (_dot_nt(dy4, vc) - c4)
            ds3 = ds.reshape(4, BLOCK, nk)
            if n_bias == 8:
                if first:
                    db_ref[pl.ds(hb, 4), :, BLOCK:] += ds3
                else:
                    db_ref[pl.ds(hb, 4)] += ds3
            elif first:
                db_ref[:, :, BLOCK:] += ds3
            else:
                db_ref[...] += ds3
            ds16 = ds.astype(BF16)
            dq = _unstack_heads(_dot(ds16, kc), lane_head) * scale
            _store_rows(dq_ref, _block_rows(row0, stride, ib), dq.astype(g_dtype), split)
            dk_acc[krows, :] += _dot_tn(ds16, q4) * scale
            dv_acc[krows, :] += _dot_tn(p.astype(BF16), dy4)
            if has_sink:
                for h in range(4):
                    hs = slice(h * BLOCK, (h + 1) * BLOCK)
                    sk = sink_ref[0, h:h + 1, 0:1]
                    val = -jnp.sum(jnp.exp(sk - l4[hs]) * c4[hs], axis=0, keepdims=True)
                    dsk_ref[hb + h] += jnp.broadcast_to(val, (8, 128))

        block(0, True)
        if nb > 1:
            def step(i, carry):
                block(i, False)
                return carry
            lax.fori_loop(1, nb, step, 0, unroll=min(ATTN_UNROLL, nb - 1))

        if kvw == GW:
            all_rows = pl.ds(row0, ln, stride=stride) if stride > 1 else pl.ds(0, ln)
            _store_rows(dk_ref, all_rows, dk_acc[...].astype(g_dtype), split)
            _store_rows(dv_ref, all_rows, dv_acc[...].astype(g_dtype), split)
        else:
            def fold(acc):
                t2 = acc[:, :2 * HEAD_DIM] + acc[:, 2 * HEAD_DIM:]
                t2 = t2 + pltpu.roll(t2, HEAD_DIM, 1)
                lane = lax.broadcasted_iota(jnp.int32, t2.shape, 1) // HEAD_DIM
                return jnp.where(lane == rr, t2, 0.0)

            @pl.when(rr == 0)
            def _():
                dk_half[...] = fold(dk_acc[...])
                dv_half[...] = fold(dv_acc[...])

            @pl.when(rr == 1)
            def _():
                dk_ref[0] = (dk_half[...] + fold(dk_acc[...])).astype(g_dtype)
                dv_ref[0] = (dv_half[...] + fold(dv_acc[...])).astype(g_dtype)

    return pl.pallas_call(
        body, name=name, grid=grid,
        out_shape=(jax.ShapeDtypeStruct(dq_shape, g_dtype), jax.ShapeDtypeStruct(dkv_shape, g_dtype),
                   jax.ShapeDtypeStruct(dkv_shape, g_dtype), jax.ShapeDtypeStruct((n_bias, BLOCK, 2 * BLOCK), F32),
                   jax.ShapeDtypeStruct((8, 8, 128), F32)),
        in_specs=[q_spec, k_spec, v_spec,
                  pl.BlockSpec((4, BLOCK, 2 * BLOCK), bias_map), pl.BlockSpec((1, 4, 128), sink_map),
                  o_spec, o_spec, o_spec],
        out_specs=(o_spec, kv_out_spec, kv_out_spec,
                   pl.BlockSpec((n_bias, BLOCK, 2 * BLOCK), lambda n, r: (0, 0, 0)),
                   pl.BlockSpec((8, 8, 128), lambda n, r: (0, 0, 0))),
        scratch_shapes=[pltpu.VMEM((ln, GW), F32), pltpu.VMEM((ln, GW), F32),
                        pltpu.VMEM((ln, 2 * HEAD_DIM), F32), pltpu.VMEM((ln, 2 * HEAD_DIM), F32)]
        + ([pltpu.VMEM((seq, GW), BF16)] * 2 if expanded else []),
        compiler_params=_params(("arbitrary", "arbitrary"), VMEM_BIG),
    )(q_arr, k_arr, v_arr, bias, sink, dy, cc, lse)


def _bias_grad(ds_all, buckets):
    def body(ds_ref, bk_ref, o_ref):
        rows = lax.broadcasted_iota(jnp.int32, (N_BUCKETS, 128), 0)
        cols = lax.broadcasted_iota(jnp.int32, (N_BUCKETS, 128), 1)

        def per_bucket(b, acc):
            for h in range(20):
                gi = h // 4 if h < 12 else 3
                v = jnp.where(bk_ref[gi] == b, ds_ref[h], 0.0)
                v = jnp.sum(jnp.sum(v, axis=1, keepdims=True), axis=0, keepdims=True)
                acc = jnp.where((rows == b) & (cols == h), v, acc)
            return acc

        o_ref[...] = lax.fori_loop(0, N_BUCKETS, per_bucket, jnp.zeros((N_BUCKETS, 128), F32))

    vm = pl.BlockSpec(memory_space=pltpu.VMEM)
    return pl.pallas_call(body, name="bias_grad", out_shape=jax.ShapeDtypeStruct((N_BUCKETS, 128), F32),
                          in_specs=[vm, vm], out_specs=vm)(ds_all, buckets)


def _adamw(w, g, m, v, name):
    (res,), _ = _adamw_many([(w, g, m, v)], name)
    return res


def _adamw_many(tensors, name, rider=None):
    n = len(tensors)
    r, c = tensors[0][0].shape
    tr = r
    for cand in (256, 176, 128, 88, 64, 32, 16, 8):
        if r % cand == 0 and cand * c * 4 * 7 * n * 2 <= 24 * 1024 * 1024:
            tr = cand
            break

    def body(*refs):
        ins, outs = refs[:4 * n], refs[4 * n:]
        for i in range(n):
            w_ref, g_ref, m_ref, v_ref = ins[4 * i:4 * i + 4]
            d, nm, nv = _adam_update(w_ref[...], g_ref[...], m_ref[...], v_ref[...])
            outs[3 * i][...], outs[3 * i + 1][...], outs[3 * i + 2][...] = d, nm, nv

    spec = pl.BlockSpec((tr, c), lambda i: (i, 0))
    shp = jax.ShapeDtypeStruct((r, c), F32)
    res, ro = _pallas(body, tuple(a for t4 in tensors for a in t4), name=name, grid=(r // tr,),
                      out_shape=(shp,) * (3 * n), in_specs=[spec] * (4 * n), out_specs=(spec,) * (3 * n),
                      sem=("parallel",), vmem=VMEM_BIG, rider=rider)
    return [tuple(res[3 * i:3 * i + 3]) for i in range(n)], ro


def _t5_bucket(dist):
    max_exact = N_BUCKETS // 2
    n = jnp.maximum(dist, 0)
    nf = jnp.maximum(n, 1).astype(F32)
    large = max_exact + (jnp.log(nf / max_exact) / math.log(MAX_DISTANCE / max_exact)
                         * (N_BUCKETS - max_exact)).astype(jnp.int32)
    large = jnp.minimum(large, N_BUCKETS - 1)
    return jnp.where(n < max_exact, n, large)


def _bias_tables(rel_bias):
    qi = jnp.arange(BLOCK)[:, None]
    ki = jnp.arange(2 * BLOCK)[None, :]
    dist = qi + BLOCK - ki
    specs = [(d, w // d, 4 * gi, 4 * gi + 4) for gi, (w, d) in enumerate(DIL_GROUPS)] + [(1, B_WINDOW - 1, 12, 20)]
    biases, buckets = [], []
    for stride, steps, h0, h1 in specs:
        valid = (dist >= 0) & (dist <= steps)
        bk = jnp.where(valid, _t5_bucket(dist * stride), -1).astype(jnp.int32)
        onehot = (bk[None, :, :] == jnp.arange(N_BUCKETS, dtype=jnp.int32)[:, None, None]).astype(F32)
        b = jnp.einsum("bqk,bh->hqk", onehot, rel_bias[:, h0:h1], precision=lax.Precision.HIGHEST)
        biases.append(jnp.where(valid[None], b, NEG))
        buckets.append(bk)
    return jnp.concatenate(biases, axis=0), jnp.stack(buckets, axis=0)


def _local_step(x, tgt, W, S, shards=None, tail_host=None):
    nseq, seq, _ = x.shape
    t = nseq * seq
    xf = x.reshape(t, D_MODEL)
    bias_all, buckets = _bias_tables(S["rel_bias"])
    sink_b = jnp.broadcast_to(S["sinks"].reshape(2, 4, 1), (2, 4, 128)).astype(F32)
    sink_0 = jnp.zeros((1, 4, 128), F32)
    dist = shards is not None
    W = dict(W)
    G, GS, reduced = {}, {}, {}

    def put(keys, gathered):
        for k, g in zip(keys, gathered):
            W[k] = g.reshape(_FULL_SHAPE.get(k, (N_CHIPS * shards[k].shape[0], D_MODEL)))

    def gather_rider(keys):
        return _GatherRider([shards[k] for k in keys]) if dist else None

    def pair(keys):
        return _pair_reduce([G[k].reshape(N_CHIPS, 2, shards[k].shape[0] // 2, D_MODEL) for k in keys],
                            "grad_pair_reduce_" + keys[0])

    def finish(keys, own, rec):
        full = _final_reduce(own, rec, "grad_final_reduce_" + keys[0])
        off = 0
        for k in keys:
            r = shards[k].shape[0]
            reduced[k] = full[:, off:off + r // 2].reshape(r, D_MODEL)
            off += r // 2

    if dist:
        first = ("wgt1", "wut1", "wd1")
        put(first, _gather_rows([shards[k] for k in first]))
    keys = ("wint",)
    (h1, n1, g1, u1, a1), ro = _ffn_fwd(xf, S["ffn1_norm"], W["wgt1"], W["wut1"], W["wd1"], rider=gather_rider(keys))
    put(keys, ro)
    keys = ("wout", "wat", "wbt", "wgt2")
    (un, za, zb, zg), ro = _inproj_fwd(h1, S["mix_norm"], W["wint"], S["b_in"], nseq, rider=gather_rider(keys))
    put(keys, ro)

    seq3 = lambda a: a.reshape(nseq, seq, a.shape[-1])
    zb3 = seq3(zb)
    pair_blk = lambda cb: pl.BlockSpec((1, 2, seq, 128), lambda n, r, cb=cb: (n, cb, 0, 0))
    a_cfg = []
    outs, lses = [], []
    for gi, (_, d) in enumerate(DIL_GROUPS):
        cfg = dict(grid=(nseq, d), seq=seq, stride=d, kvw=GW, split=True,
                   q_spec=pair_blk(gi), k_spec=pair_blk(3 + gi), v_spec=pair_blk(6 + gi), o_spec=pair_blk(0),
                   bias_map=lambda n, r: (0, 0, 0), sink_map=lambda n, r: (0, 0, 0), has_sink=False)
        a_cfg.append(cfg)
        (o, lse), _ = _attn_fwd(za, za, za, bias_all[4 * gi:4 * gi + 4], sink_0, o_shape=(nseq, 2, seq, 128),
                                o_dtype=F32, name=f"attn_a{gi}_fwd", **cfg)
        outs.append(o)
        lses.append(lse)
    wide_blk = lambda w, cmap: pl.BlockSpec((1, seq, w), cmap)
    b_cfg = dict(grid=(nseq, 2), seq=seq, stride=1, kvw=2 * HEAD_DIM, split=False,
                 q_spec=wide_blk(GW, lambda n, r: (n, 0, r)), k_spec=wide_blk(2 * HEAD_DIM, lambda n, r: (n, 0, 4)),
                 v_spec=wide_blk(2 * HEAD_DIM, lambda n, r: (n, 0, 5)), o_spec=wide_blk(GW, lambda n, r: (n, 0, r)),
                 bias_map=lambda n, r: (r, 0, 0), sink_map=lambda n, r: (r, 0, 0), has_sink=True)
    keys = ("wut2",)
    bias_b_fwd = bias_all[12:20].at[:, :, 0].set(jnp.broadcast_to(S["sinks"].reshape(8, 1), (8, BLOCK)))
    (yb, lse_b), ro = _attn_fwd(zb3, zb3, zb3, bias_b_fwd, sink_b, o_shape=(nseq, seq, 2 * GW), o_dtype=BF16,
                                name="attn_b_fwd", rider=gather_rider(keys), **b_cfg)
    put(keys, ro)
    yb = yb.reshape(t, 2 * GW)

    keys = ("wd2",)
    (h2, y, lse_tot, pa, pb, merged), ro = _merge_fwd(outs[0], outs[1], outs[2], lses[0], lses[1], lses[2], yb, zg, h1,
                                                      W["wat"], W["wbt"], W["wout"], rider=gather_rider(keys))
    put(keys, ro)
    (dh3, n2, g2, u2, a2, loss_part, g_final), _ = _ffn_fwd(
        h2, S["ffn2_norm"], W["wgt2"], W["wut2"], W["wd2"],
        head=(S["final_norm"].reshape(1, D_MODEL), tgt.reshape(t, D_MODEL)))

    GS["final_norm"] = g_final
    dh2, dg2, du2, df2, GS["ffn2_norm"] = _ffn_bwd(dh3, h2, S["ffn2_norm"], g2, u2, W["wgt2"], W["wut2"], W["wd2"])
    G["wgt2"] = _wgrad(dg2, n2, MXU_DIM, name="wgrad_gate2")
    keys = ("wgt2", "wut2", "wd2")
    if dist:
        halves4 = lambda k: G[k].reshape(N_CHIPS, 2, shards[k].shape[0] // 2, D_MODEL)
        G["wut2"], (s_g2,) = _wgrad(du2, n2, MXU_DIM, name="wgrad_up2", rider=_PairRider(halves4("wgt2")))
        G["wd2"], (s_u2,) = _wgrad(a2, df2, MXU_DIM, name="wgrad_down2", rider=_PairRider(halves4("wut2")))
        rider = _ExchangeRider([s_g2, s_u2, pair(("wd2",))])
    else:
        G["wut2"] = _wgrad(du2, n2, MXU_DIM, name="wgrad_up2")
        G["wd2"] = _wgrad(a2, df2, MXU_DIM, name="wgrad_down2")
        rider = None
    (dpa, dpb, dga, dgb, dya, dyb, dh2b, ca, cb), ro = _merge_bwd(dh2, pa, pb, zg, y, yb, W["wat"], W["wbt"], W["wout"],
                                                                  nseq, rider=rider)
    if dist:
        finish(keys, *ro)

    dqs, dks, dvs, dbs = [], [], [], []
    shp = (nseq, 2, seq, 128)
    halves = lambda a: [a[:, hf].reshape(t, 128).astype(BF16) for hf in range(2)]
    for gi in range(len(DIL_GROUPS)):
        dq, dk, dv, db, _ = _attn_bwd(za, za, za, bias_all[4 * gi:4 * gi + 4], sink_0, dya, ca, lse_tot,
                                      n_bias=4, dq_shape=shp, dkv_shape=shp, g_dtype=F32,
                                      kv_out_spec=a_cfg[gi]["o_spec"], name=f"attn_a{gi}_bwd", **a_cfg[gi])
        dqs += halves(dq)
        dks += halves(dk)
        dvs += halves(dv)
        dbs.append(db)
    dqb, dkb, dvb, dbb, dsink = _attn_bwd(zb3, zb3, zb3, bias_all[12:20], sink_b, seq3(dyb), seq3(cb), lse_b,
                                          n_bias=8, dq_shape=(nseq, seq, 2 * GW),
                                          dkv_shape=(nseq, seq, 2 * HEAD_DIM), g_dtype=BF16,
                                          kv_out_spec=wide_blk(2 * HEAD_DIM, lambda n, r: (n, 0, 0)),
                                          name="attn_b_bwd", **b_cfg)
    dz = jnp.concatenate(dqs + dks + dvs + [dqb.reshape(t, 2 * GW), dkb.reshape(t, 2 * HEAD_DIM),
                                            dvb.reshape(t, 2 * HEAD_DIM), dga, dgb], axis=-1)
    gb_tab = _bias_grad(jnp.concatenate(dbs + [dbb], axis=0), buckets)
    if dist:
        GS["bias_tab"], GS["sink_tiles"] = gb_tab, dsink
    else:
        GS["rel_bias"] = gb_tab[:, :20]
        GS["sinks"] = dsink[:, 0, 0].reshape(1, 8)

    G["wint"], GS["b_in"] = _wgrad(dz, un, MXU_DIM, with_colsum=True, name="wgrad_in")
    G["wout"] = _wgrad(merged, dh2b, MXU_DIM, name="wgrad_out")
    G["wat"] = _wgrad(dpa, y, MXU_DIM, name="wgrad_branch_a")
    G["wbt"] = _wgrad(dpb, yb, MXU_DIM, name="wgrad_branch_b")
    keys = ("wint", "wout", "wat", "wbt")
    rider = _ExchangeRider([pair(keys)]) if dist else None
    (dh1, GS["mix_norm"]), ro = _inproj_bwd(dz, dh2, h1, S["mix_norm"], W["wint"], rider=rider)
    if dist:
        finish(keys, *ro)

    dx, dg1, du1, df1, GS["ffn1_norm"] = _ffn_bwd(dh1, xf, S["ffn1_norm"], g1, u1, W["wgt1"], W["wut1"], W["wd1"])
    G["wgt1"] = _wgrad(dg1, n1, MXU_DIM, name="wgrad_gate1")
    if dist:
        G["wut1"], ro = _wgrad(du1, n1, MXU_DIM, name="wgrad_up1", rider=_ExchangeRider([pair(("wgt1",))]))
        finish(("wgt1",), *ro)
        G["wd1"], ro = _wgrad(a1, df1, MXU_DIM, name="wgrad_down1", rider=_ExchangeRider([pair(("wut1",))]))
        finish(("wut1",), *ro)
        finish(("wd1",), *tail_host(_ExchangeRider([pair(("wd1",))]), reduced))
    else:
        G["wut1"] = _wgrad(du1, n1, MXU_DIM, name="wgrad_up1")
        G["wd1"] = _wgrad(a1, df1, MXU_DIM, name="wgrad_down1")
    return loss_part, dx.reshape(x.shape), (reduced if dist else G), GS


_SMALL = ("ffn1_norm", "mix_norm", "ffn2_norm", "final_norm", "b_in", "sinks", "rel_bias")
_ORDER = ("ffn1_norm", "ffn1_w_gate", "ffn1_w_up", "ffn1_w_down", "mix_norm", "w_in", "b_in", "w_branch_a",
          "w_branch_b", "w_out", "sinks", "rel_bias", "ffn2_norm", "ffn2_w_gate", "ffn2_w_up", "ffn2_w_down",
          "final_norm")
_BIG = (("wgt1", "ffn1_w_gate", True, 704), ("wut1", "ffn1_w_up", True, 704), ("wd1", "ffn1_w_down", False, 704),
        ("wint", "w_in", True, 1280), ("wout", "w_out", False, 256), ("wat", "w_branch_a", True, 64),
        ("wbt", "w_branch_b", True, 128), ("wgt2", "ffn2_w_gate", True, 704), ("wut2", "ffn2_w_up", True, 704),
        ("wd2", "ffn2_w_down", False, 704))
_FULL_SHAPE = {"wat": (D_MODEL, GW), "wbt": (D_MODEL, 2 * GW)}


def kernel(x, ffn1_norm, ffn1_w_gate, ffn1_w_up, ffn1_w_down, mix_norm, w_in, b_in, w_branch_a, w_branch_b, w_out, sinks, rel_bias, ffn2_norm, ffn2_w_gate, ffn2_w_up, ffn2_w_down, final_norm, loss_target, m_ffn1_norm, m_ffn1_w_gate, m_ffn1_w_up, m_ffn1_w_down, m_mix_norm, m_w_in, m_b_in, m_w_branch_a, m_w_branch_b, m_w_out, m_sinks, m_rel_bias, m_ffn2_norm, m_ffn2_w_gate, m_ffn2_w_up, m_ffn2_w_down, m_final_norm, v_ffn1_norm, v_ffn1_w_gate, v_ffn1_w_up, v_ffn1_w_down, v_mix_norm, v_w_in, v_b_in, v_w_branch_a, v_w_branch_b, v_w_out, v_sinks, v_rel_bias, v_ffn2_norm, v_ffn2_w_gate, v_ffn2_w_up, v_ffn2_w_down, v_final_norm):
    args = dict(locals())
    w = {n: args[n] for n in _ORDER}
    m = {n: args["m_" + n] for n in _ORDER}
    v = {n: args["v_" + n] for n in _ORDER}

    shards = {}
    for key, name, transposed, rows in _BIG:
        a = w[name][0]
        a = (a.T if transposed else a).astype(BF16)
        shards[key] = a.reshape(rows, D_MODEL)
    S = {n: w[n] for n in _SMALL}

    row_adam = lambda n: (w[n][0].T, m[n][0].T, v[n][0].T)
    early = {}

    def tail_host(rider, reduced):
        tensors = []
        for key, n in (("wgt2", "ffn2_w_gate"), ("wut2", "ffn2_w_up"), ("wd2", "ffn2_w_down")):
            wmv = row_adam(n) if key != "wd2" else (w[n][0], m[n][0], v[n][0])
            tensors.append((wmv[0], reduced[key], wmv[1], wmv[2]))
        res, ro = _adamw_many(tensors, "adamw_ffn2", rider=rider)
        early["ffn2_w_gate"], early["ffn2_w_up"], early["ffn2_w_down"] = res
        return ro

    loss_part, grad_x, reduced, GS = _local_step(x, loss_target, {}, S, shards, tail_host)

    small = _allreduce_small(GS["ffn1_norm"], GS["mix_norm"], GS["ffn2_norm"], GS["final_norm"], GS["b_in"],
                             GS["sink_tiles"], GS["bias_tab"], loss_part)
    loss = small[9, 8]

    out_g, out_d, out_m, out_v = {}, {}, {}, {}
    for key, n, transposed, rows in _BIG:
        nat = w[n][0].shape
        if transposed and nat[1] % 128:
            res = early[n] if n in early else _adamw(row_adam(n)[0], reduced[key], *row_adam(n)[1:], "adamw_" + n)
            res = [reduced[key].T] + [r.T for r in res]
        elif n in early:
            res = [reduced[key]] + list(early[n])
        else:
            g = reduced[key].reshape(nat[1], nat[0]).T if transposed else reduced[key].reshape(nat)
            res = [g] + list(_adamw(w[n][0], g, m[n][0], v[n][0], "adamw_" + n))
        out_g[n], out_d[n], out_m[n], out_v[n] = [r[None] for r in res]
    row = lambda d: {n: (d[n].reshape(1, D_MODEL) if n == "final_norm" else d[n]) for n in _SMALL}
    for dst, src in zip((out_g, out_d, out_m, out_v), _adamw_small(small, row(w), row(m), row(v))):
        dst.update(src)
        dst["final_norm"] = src["final_norm"].reshape(D_MODEL)

    return (loss, grad_x, *[out_g[n] for n in _ORDER], *[out_d[n] for n in _ORDER],
            *[out_m[n] for n in _ORDER], *[out_v[n] for n in _ORDER])
```

```python
import math

import jax
import jax.numpy as jnp
from jax import lax
from jax.experimental import pallas as pl
from jax.experimental.pallas import tpu as pltpu

F32, BF16 = jnp.float32, jnp.bfloat16
MESH = pl.DeviceIdType.MESH

D_MODEL = 1024
D_FF = 2816
D_IN = 5120
HEAD_DIM = 64
BLOCK = 128
DIL_GROUPS = ((128, 1), (512, 4), (2048, 16))
B_WINDOW = 128
N_BUCKETS = 32
MAX_DISTANCE = 2048
EPS = 1e-6
N_CHIPS = 4
GW = 256
ZA_W = 2304
ZB_W = 768
NEG = -1e30

ADAM_LR, ADAM_B1, ADAM_B2, ADAM_EPS, ADAM_WD, ADAM_STEP = 0.001, 0.9, 0.999, 1e-08, 0.01, 10

VMEM_BIG = 56 * 1024 * 1024
TM = 512
TM_BWD = 256
MXU_DIM = 256
FF_BOUNDS = (0, 4 * MXU_DIM, 8 * MXU_DIM, D_FF)
DMA_SPLIT = 8
RESIDUES_PER_STEP = 16
ATTN_UNROLL = 15


def _dot(a, b):
    return jnp.dot(a, b, preferred_element_type=F32)


def _dot_nt(a, b):
    return lax.dot_general(a, b, (((1,), (1,)), ((), ())), preferred_element_type=F32)


def _dot_tn(a, b):
    return lax.dot_general(a, b, (((0,), (0,)), ((), ())), preferred_element_type=F32)


def _sigmoid(x):
    return 0.5 * jnp.tanh(0.5 * x) + 0.5


def _params(sem, vmem=None):
    return pltpu.CompilerParams(dimension_semantics=sem, vmem_limit_bytes=vmem)


ANY = pl.BlockSpec(memory_space=pl.ANY)


def _me():
    return lax.axis_index("x"), lax.axis_index("y"), lax.axis_index("c")


_CHIP_RELS = ((1, 0), (0, 1), (1, 1))


def _flip(v, f):
    return 1 - v if f else v


def _remote(src, dst, ssem, rsem, peer):
    return pltpu.make_async_remote_copy(src_ref=src, dst_ref=dst, send_sem=ssem, recv_sem=rsem,
                                        device_id=peer, device_id_type=MESH)


def _row_pieces(rows, n):
    step = max(16, -(-rows // n) // 16 * 16)
    out, s = [], 0
    while s < rows:
        out.append((s, min(step, rows - s)))
        s += step
    return out


def _gather_rows(shards):
    nt = len(shards)
    rows = [s.shape[0] for s in shards]

    def body(*refs):
        srcs, outs = refs[:nt], refs[nt:2 * nt]
        halves, quarters = refs[2 * nt:3 * nt], refs[3 * nt:4 * nt]
        ici_s, ici_r, fwd_s, fwd_r, d2d_s, d2d_r, keep, loc = refs[4 * nt:]
        x, y, c = _me()
        j = 2 * x + y
        sib = (x, y, 1 - c)
        nbr = ((1 - x, y, c), (x, 1 - y, c))
        nbr_j = (2 * (1 - x) + y, 2 * x + (1 - y))
        diag_j = 2 * (1 - x) + (1 - y)
        local = [pltpu.make_async_copy(srcs[t], outs[t].at[j], loc.at[t]) for t in range(nt)]
        for cp in local:
            cp.start()
        pending = []
        for a in range(2):
            for t in range(nt):
                half = pl.ds(c * (rows[t] // 2), rows[t] // 2)
                cp = _remote(srcs[t].at[half], halves[t].at[a], ici_s.at[2 * t + a], ici_r.at[2 * t + a], nbr[a])
                cp.start()
                pending.append(cp)
        placed = []

        def place(src, dst_of, idx):
            mine = pltpu.make_async_copy(src, dst_of, keep.at[idx])
            mine.start()
            cp = _remote(src, dst_of, d2d_s.at[idx], d2d_r.at[idx], sib)
            cp.start()
            placed.append((mine, cp))

        for a in range(2):
            for t in range(nt):
                r2, r4 = rows[t] // 2, rows[t] // 4
                got = halves[t].at[a]
                _remote(got, got, ici_s.at[2 * t + a], ici_r.at[2 * t + a], nbr[a]).wait_recv()
                cp = _remote(halves[t].at[a, pl.ds(a * r4, r4)], quarters[t].at[a], fwd_s.at[2 * t + a],
                             fwd_r.at[2 * t + a], nbr[1 - a])
                cp.start()
                pending.append(cp)
                place(got, outs[t].at[nbr_j[a], pl.ds(c * r2, r2)], 4 * t + a)
        for a in range(2):
            for t in range(nt):
                r2, r4 = rows[t] // 2, rows[t] // 4
                got = quarters[t].at[a]
                _remote(got, got, fwd_s.at[2 * t + a], fwd_r.at[2 * t + a], nbr[1 - a]).wait_recv()
                place(got, outs[t].at[diag_j, pl.ds(c * r2 + a * r4, r4)], 4 * t + 2 + a)
        for mine, cp in placed:
            mine.wait()
            cp.wait()
        for cp in pending:
            cp.wait_send()
        for cp in local:
            cp.wait()

    stage = ([pltpu.VMEM((2, r // 2, D_MODEL), BF16) for r in rows] + [pltpu.VMEM((2, r // 4, D_MODEL), BF16) for r in rows])
    sems = ([pltpu.SemaphoreType.DMA((2 * nt,)) for _ in range(4)] + [pltpu.SemaphoreType.DMA((4 * nt,))] * 3
            + [pltpu.SemaphoreType.DMA((nt,))])
    return pl.pallas_call(
        body, name="gather_weights",
        out_shape=tuple(jax.ShapeDtypeStruct((N_CHIPS,) + s.shape, s.dtype) for s in shards),
        in_specs=[pl.BlockSpec(memory_space=pltpu.VMEM)] * nt,
        out_specs=tuple([ANY] * nt), scratch_shapes=stage + sems,
    )(*shards)


VMEM_WHOLE = pl.BlockSpec(memory_space=pltpu.VMEM)


def _pair_reduce(grads, name):
    nt = len(grads)
    r2 = [g.shape[2] for g in grads]
    off = [sum(r2[:t]) for t in range(nt)]
    tot = sum(r2)

    def body(*refs):
        gs = refs[:nt]
        s_ref, mine, got, ssem, rsem, lsem = refs[nt:]
        x, y, c = _me()
        sib = (x, y, 1 - c)
        for t in range(nt):
            for k in range(N_CHIPS):
                rows = pl.ds(off[t], r2[t])
                _remote(gs[t].at[k, 1 - c], got.at[k, rows], ssem, rsem, sib).start()
                pltpu.make_async_copy(gs[t].at[k, c], mine.at[k, rows], lsem).start()
        pltpu.make_async_copy(mine, mine, lsem).wait()
        _remote(got, got, ssem, rsem, sib).wait()
        for k in range(N_CHIPS):
            for st, sz in _row_pieces(tot, 4):
                rows = slice(st, st + sz)
                s_ref[k, rows, :] = (mine[k, rows, :].astype(F32) + got[k, rows, :].astype(F32)).astype(BF16)

    shp = jax.ShapeDtypeStruct((N_CHIPS, tot, D_MODEL), BF16)
    buf = pltpu.VMEM((N_CHIPS, tot, D_MODEL), BF16)
    return pl.pallas_call(
        body, name=name, out_shape=shp, in_specs=[ANY] * nt, out_specs=VMEM_WHOLE,
        scratch_shapes=[buf, buf, pltpu.SemaphoreType.DMA(()), pltpu.SemaphoreType.DMA(()),
                        pltpu.SemaphoreType.DMA(())],
        compiler_params=pltpu.CompilerParams(vmem_limit_bytes=VMEM_BIG),
    )(*grads)


def _final_reduce(own, rec, name):
    r2 = own.shape[0]
    stages = _row_pieces(r2, 2)

    def body(own_hbm, rec_hbm, o_ref, parts, fbuf, ssem, rsem, lsem, insems):
        x, y, c = _me()
        sib = (x, y, 1 - c)
        for p, (st, sz) in enumerate(stages):
            rows = pl.ds(st, sz)
            pltpu.make_async_copy(own_hbm.at[rows], parts.at[0, rows], insems.at[p]).start()
            for k in range(3):
                pltpu.make_async_copy(rec_hbm.at[k, rows], parts.at[1 + k, rows], insems.at[p]).start()
        for p, (st, sz) in enumerate(stages):
            stage = parts.at[:, pl.ds(st, sz)]
            pltpu.make_async_copy(stage, stage, insems.at[p]).wait()
            for s0, ssz in _row_pieces(sz, DMA_SPLIT // 2):
                rows = slice(st + s0, st + s0 + ssz)
                fbuf[rows, :] = (parts[0, rows, :].astype(F32) + parts[1, rows, :].astype(F32)
                                 + parts[2, rows, :].astype(F32) + parts[3, rows, :].astype(F32))
                dst = o_ref.at[c, pl.ds(st + s0, ssz)]
                pltpu.make_async_copy(fbuf.at[pl.ds(st + s0, ssz)], dst, lsem).start()
                _remote(fbuf.at[pl.ds(st + s0, ssz)], dst, ssem, rsem, sib).start()
        _remote(fbuf, o_ref.at[c], ssem, rsem, sib).wait()
        pltpu.make_async_copy(fbuf, o_ref.at[c], lsem).wait()

    return pl.pallas_call(
        body, name=name, out_shape=jax.ShapeDtypeStruct((2, r2, D_MODEL), F32),
        in_specs=[ANY, ANY], out_specs=ANY,
        scratch_shapes=[pltpu.VMEM((4, r2, D_MODEL), BF16), pltpu.VMEM((r2, D_MODEL), F32),
                        pltpu.SemaphoreType.DMA(()), pltpu.SemaphoreType.DMA(()), pltpu.SemaphoreType.DMA(()),
                        pltpu.SemaphoreType.DMA((2,))],
        compiler_params=pltpu.CompilerParams(vmem_limit_bytes=VMEM_BIG),
    )(own, rec)


SMALL_ROWS = 48


def _allreduce_small(g_ffn1, g_mix, g_ffn2, g_final, g_bin, dsink, bias_tab, loss_part):
    def body(f1_ref, mx_ref, f2_ref, fn_ref, bi_ref, sk_ref, bt_ref, ls_ref, o_ref, mine, buf, tabs, send_sems,
             recv_sems):
        x, y, c = _me()
        me = 4 * x + 2 * y + c
        mine[...] = jnp.zeros_like(mine)
        for r, ref in enumerate((f1_ref, mx_ref, f2_ref, fn_ref)):
            mine[r:r + 1, :] = ref[...]
        for k in range(D_IN // D_MODEL):
            mine[4 + k:5 + k, :] = bi_ref[:, k * D_MODEL:(k + 1) * D_MODEL]
        lane = lax.broadcasted_iota(jnp.int32, (1, 128), 1)
        row = jnp.where(lane == 8, ls_ref[0:1, :], 0.0)
        for h in range(8):
            row = jnp.where(lane == h, sk_ref[h, 0:1, :], row)
        mine[9:10, 0:128] = row
        buf[me] = mine[...]
        tabs[me] = bt_ref[...]
        copies = []
        for k in range(1, 8):
            peer = (_flip(x, (k >> 2) & 1), _flip(y, (k >> 1) & 1), _flip(c, k & 1))
            for t, (src, dst) in enumerate(((mine, buf), (bt_ref, tabs))):
                cp = _remote(src, dst.at[me], send_sems.at[2 * (k - 1) + t], recv_sems.at[2 * (k - 1) + t], peer)
                cp.start()
                copies.append(cp)
        for cp in copies:
            cp.wait()
        acc, tab = buf[0], tabs[0]
        for i in range(1, 8):
            acc, tab = acc + buf[i], tab + tabs[i]
        o_ref[...] = jnp.zeros_like(o_ref)
        o_ref[0:16, :] = acc
        o_ref[16:48, 0:128] = tab

    vm = pl.BlockSpec(memory_space=pltpu.VMEM)
    return pl.pallas_call(
        body, name="allreduce_small", out_shape=jax.ShapeDtypeStruct((SMALL_ROWS, D_MODEL), F32),
        in_specs=[vm] * 8, out_specs=vm,
        scratch_shapes=[pltpu.VMEM((16, D_MODEL), F32), pltpu.VMEM((8, 16, D_MODEL), F32),
                        pltpu.VMEM((8, N_BUCKETS, 128), F32), pltpu.SemaphoreType.DMA((14,)),
                        pltpu.SemaphoreType.DMA((14,))],
    )(g_ffn1, g_mix, g_ffn2, g_final, g_bin, dsink, bias_tab, loss_part)


def _adam_update(w, g, m, v):
    nm = ADAM_B1 * m + (1.0 - ADAM_B1) * g
    nv = ADAM_B2 * v + (1.0 - ADAM_B2) * (g * g)
    bc1 = 1.0 - ADAM_B1 ** ADAM_STEP
    bc2 = 1.0 - ADAM_B2 ** ADAM_STEP
    return -ADAM_LR * ((nm / bc1) / (jnp.sqrt(nv / bc2) + ADAM_EPS) + ADAM_WD * w), nm, nv


def _adamw_small(packed, w, m, v):
    names = ("ffn1_norm", "mix_norm", "ffn2_norm", "final_norm", "b_in", "sinks", "rel_bias")
    nn = len(names)

    def grad_of(p_ref, name, k=0):
        if name == "b_in":
            return p_ref[4 + k:5 + k, :]
        if name == "sinks":
            return p_ref[9:10, 0:8]
        if name == "rel_bias":
            return p_ref[16:48, 0:20]
        r = names.index(name)
        return p_ref[r:r + 1, :]

    def body(p_ref, *refs):
        ws, ms, vs = refs[:nn], refs[nn:2 * nn], refs[2 * nn:3 * nn]
        outs = refs[3 * nn:]
        for i, name in enumerate(names):
            og, od, om, ov = outs[i], outs[nn + i], outs[2 * nn + i], outs[3 * nn + i]
            pieces = range(D_IN // D_MODEL) if name == "b_in" else (0,)
            for k in pieces:
                sl = (slice(None), slice(k * D_MODEL, (k + 1) * D_MODEL)) if name == "b_in" else (Ellipsis,)
                g = grad_of(p_ref, name, k)
                d, nm, nv = _adam_update(ws[i][sl], g, ms[i][sl], vs[i][sl])
                og[sl], od[sl], om[sl], ov[sl] = g, d, nm, nv

    vm = pl.BlockSpec(memory_space=pltpu.VMEM)
    shapes = [jax.ShapeDtypeStruct(w[n].shape, F32) for n in names]
    res = pl.pallas_call(
        body, name="adamw_small", out_shape=tuple(shapes * 4), in_specs=[vm] * (1 + 3 * nn),
        out_specs=tuple([vm] * (4 * nn)),
    )(packed, *[w[n] for n in names], *[m[n] for n in names], *[v[n] for n in names])
    return [dict(zip(names, res[i * nn:(i + 1) * nn])) for i in range(4)]


class _GatherRider:
    def __init__(self, shards):
        self.inputs = list(shards)
        nt = len(shards)
        self.out_shape = [jax.ShapeDtypeStruct((N_CHIPS,) + s.shape, s.dtype) for s in shards]
        self.scratch = [pltpu.SemaphoreType.DMA((3 * nt,)), pltpu.SemaphoreType.DMA((3 * nt,)),
                        pltpu.SemaphoreType.DMA((nt,))]

    def _copies(self, srcs, outs, sems):
        ici_s, ici_r, loc = sems
        x, y, c = _me()
        j = 2 * x + y
        local = [pltpu.make_async_copy(srcs[t], outs[t].at[j], loc.at[t]) for t in range(len(srcs))]
        remote = []
        for k, (fx, fy) in enumerate(_CHIP_RELS):
            peer = (_flip(x, fx), _flip(y, fy), c)
            for t in range(len(srcs)):
                remote.append(_remote(srcs[t], outs[t].at[j], ici_s.at[3 * t + k], ici_r.at[3 * t + k], peer))
        return local, remote

    def start(self, srcs, outs, sems):
        local, remote = self._copies(srcs, outs, sems)
        for cp in local + remote:
            cp.start()

    def finish(self, srcs, outs, sems):
        local, remote = self._copies(srcs, outs, sems)
        for cp in remote + local:
            cp.wait()


class _ExchangeRider:
    def __init__(self, parts):
        self.inputs = list(parts)
        self.r2 = [p.shape[1] for p in parts]
        self.off = [sum(self.r2[:g]) for g in range(len(parts))]
        tot = sum(self.r2)
        self.out_shape = [jax.ShapeDtypeStruct((tot, D_MODEL), BF16), jax.ShapeDtypeStruct((3, tot, D_MODEL), BF16)]
        self.scratch = [pltpu.SemaphoreType.DMA((3,)), pltpu.SemaphoreType.DMA((3,)), pltpu.SemaphoreType.DMA(())]

    def start(self, ps, outs, sems):
        own_ref, rec_ref = outs
        ssems, rsems, lsem = sems
        x, y, c = _me()
        j = 2 * x + y
        for g in range(len(ps)):
            pltpu.make_async_copy(ps[g].at[j], own_ref.at[pl.ds(self.off[g], self.r2[g])], lsem).start()
        for k, (fx, fy) in enumerate(_CHIP_RELS):
            px, py = _flip(x, fx), _flip(y, fy)
            for g in range(len(ps)):
                for st, sz in _row_pieces(self.r2[g], 2):
                    _remote(ps[g].at[2 * px + py, pl.ds(st, sz)], rec_ref.at[k, pl.ds(self.off[g] + st, sz)],
                            ssems.at[k], rsems.at[k], (px, py, c)).start()

    def finish(self, ps, outs, sems):
        own_ref, rec_ref = outs
        ssems, rsems, lsem = sems
        x, y, c = _me()
        for k in range(3):
            _remote(rec_ref.at[k], rec_ref.at[k], ssems.at[k], rsems.at[k], (x, y, c)).wait()
        pltpu.make_async_copy(own_ref, own_ref, lsem).wait()


def _pallas(body, args, *, name, grid, in_specs, out_specs, out_shape, scratch_shapes=(), sem=None, vmem=None,
            rider=None, stream=False):
    if stream:
        n_in, n_out = len(in_specs), len(out_shape)
        r_in, r_out = (len(rider.inputs), len(rider.out_shape)) if rider is not None else (0, 0)

        def outer(*refs):
            ins, rins = refs[:n_in], refs[n_in:n_in + r_in]
            p = n_in + r_in
            outs, routs, rsems = refs[p:p + n_out], refs[p + n_out:p + n_out + r_out], refs[p + n_out + r_out:]
            if rider is not None:
                rider.start(rins, routs, rsems)
            pltpu.emit_pipeline(body, grid=grid, in_specs=list(in_specs), out_specs=list(out_specs))(*ins, *outs)
            if rider is not None:
                rider.finish(rins, routs, rsems)

        res = pl.pallas_call(
            outer, name=name, in_specs=[ANY] * (n_in + r_in), out_specs=(ANY,) * (n_out + r_out),
            out_shape=tuple(out_shape) + tuple(rider.out_shape if rider is not None else ()),
            scratch_shapes=list(rider.scratch) if rider is not None else [],
            compiler_params=pltpu.CompilerParams(vmem_limit_bytes=vmem),
        )(*args, *(rider.inputs if rider is not None else ()))
        return tuple(res[:n_out]), tuple(res[n_out:])
    if rider is None:
        res = pl.pallas_call(body, name=name, grid=grid, in_specs=list(in_specs), out_specs=tuple(out_specs),
                             out_shape=tuple(out_shape), scratch_shapes=list(scratch_shapes),
                             compiler_params=_params(sem, vmem))(*args)
        return tuple(res), ()
    n_in, n_out, n_sc = len(in_specs), len(out_shape), len(scratch_shapes)
    r_in, r_out = len(rider.inputs), len(rider.out_shape)

    def wrapped(*refs):
        ins, rins = refs[:n_in], refs[n_in:n_in + r_in]
        p = n_in + r_in
        outs, routs = refs[p:p + n_out], refs[p + n_out:p + n_out + r_out]
        p += n_out + r_out
        scr, rsems = refs[p:p + n_sc], refs[p + n_sc:]
        first = pl.program_id(0) == 0
        last = pl.program_id(0) == grid[0] - 1
        for a in range(1, len(grid)):
            first = first & (pl.program_id(a) == 0)
            last = last & (pl.program_id(a) == grid[a] - 1)

        @pl.when(first)
        def _():
            rider.start(rins, routs, rsems)

        body(*ins, *outs, *scr)

        @pl.when(last)
        def _():
            rider.finish(rins, routs, rsems)

    res = pl.pallas_call(
        wrapped, name=name, grid=grid, in_specs=list(in_specs) + [ANY] * r_in,
        out_specs=tuple(out_specs) + (ANY,) * r_out, out_shape=tuple(out_shape) + tuple(rider.out_shape),
        scratch_shapes=list(scratch_shapes) + rider.scratch,
        compiler_params=_params(("arbitrary",) * len(grid), vmem))(*args, *rider.inputs)
    return tuple(res[:n_out]), tuple(res[n_out:])


def _load_weights(pairs, sems):
    copies = [pltpu.make_async_copy(hbm, vmem, sems.at[i]) for i, (hbm, vmem) in enumerate(pairs)]
    for cp in copies:
        cp.start()
    for cp in copies:
        cp.wait()


def _loss_tile(hh, gain, tgt):
    r = lax.rsqrt(jnp.mean(hh * hh, axis=-1, keepdims=True) + EPS)
    hn = hh * r
    err = hn * gain - tgt
    part = (0.5 / D_MODEL) * jnp.sum(jnp.sum(err * err, axis=1, keepdims=True), axis=0, keepdims=True)
    dy = err * (1.0 / D_MODEL)
    dng = dy * gain
    dh = r * (dng - hn * jnp.mean(dng * hn, axis=-1, keepdims=True))
    return dh, part, jnp.sum(dy * hn, axis=0, keepdims=True)


def _ffn_fwd(h, gain, wgt, wut, wd, rider=None, head=None):
    t = h.shape[0]

    def body(h_ref, gain_ref, wg_hbm, wu_hbm, wd_hbm, *rest):
        if head is None:
            hout_ref, n_ref, g_ref, u_ref, a_ref, wg_v, wu_v, wd_v, wsem = rest
        else:
            fg_ref, tgt_ref, hout_ref, n_ref, g_ref, u_ref, a_ref, loss_ref, gg_ref, wg_v, wu_v, wd_v, wsem = rest
        @pl.when(pl.program_id(0) == 0)
        def _():
            _load_weights(((wg_hbm, wg_v), (wu_hbm, wu_v), (wd_hbm, wd_v)), wsem)
            if head is not None:
                loss_ref[...] = jnp.zeros_like(loss_ref)
                gg_ref[...] = jnp.zeros_like(gg_ref)

        hh = h_ref[...]
        r = lax.rsqrt(jnp.mean(hh * hh, axis=-1, keepdims=True) + EPS)
        n = (hh * r * gain_ref[...]).astype(BF16)
        n_ref[...] = n
        acc = jnp.zeros((TM, D_MODEL), F32)
        for c0, c1 in zip(FF_BOUNDS[:-1], FF_BOUNDS[1:]):
            sl = slice(c0, c1)
            g = _dot_nt(n, wg_v[sl, :])
            u = _dot_nt(n, wu_v[sl, :])
            sg = _sigmoid(g)
            silu = g * sg
            a = (silu * u).astype(BF16)
            a_ref[:, sl] = a
            g_ref[:, sl] = (u * (sg * (1.0 + g * (1.0 - sg)))).astype(BF16)
            u_ref[:, sl] = silu.astype(BF16)
            acc = acc + _dot(a, wd_v[sl, :])
        hout = hh + 0.5 * acc
        if head is None:
            hout_ref[...] = hout
        else:
            dh, part, gpart = _loss_tile(hout, fg_ref[...], tgt_ref[...])
            hout_ref[...] = dh
            loss_ref[...] += part
            gg_ref[...] += gpart

    row = lambda w: pl.BlockSpec((TM, w), lambda i: (i, 0))
    vec = pl.BlockSpec((1, D_MODEL), lambda i: (0, 0))
    wv = pltpu.VMEM((D_FF, D_MODEL), BF16)
    args, in_specs = (h, gain, wgt, wut, wd), [row(D_MODEL), vec, ANY, ANY, ANY]
    out_shape = [jax.ShapeDtypeStruct((t, D_MODEL), F32), jax.ShapeDtypeStruct((t, D_MODEL), BF16)] + [
        jax.ShapeDtypeStruct((t, D_FF), BF16)] * 3
    out_specs = [row(D_MODEL), row(D_MODEL), row(D_FF), row(D_FF), row(D_FF)]
    if head is not None:
        args, in_specs = args + tuple(head), in_specs + [vec, row(D_MODEL)]
        out_shape += [jax.ShapeDtypeStruct((8, 128), F32), jax.ShapeDtypeStruct((1, D_MODEL), F32)]
        out_specs += [pl.BlockSpec((8, 128), lambda i: (0, 0)), vec]
    return _pallas(
        body, args, name="ffn_fwd", grid=(t // TM,), out_shape=tuple(out_shape), in_specs=in_specs,
        out_specs=tuple(out_specs), scratch_shapes=[wv, wv, wv, pltpu.SemaphoreType.DMA((3,))],
        sem=("arbitrary",), vmem=VMEM_BIG, rider=rider)


def _ffn_bwd(dhout, h, gain, dgf, duf, wgt, wut, wd):
    t = h.shape[0]
    tm = TM_BWD

    def body(dho_ref, h_ref, gain_ref, g_ref, u_ref, wg_hbm, wu_hbm, wd_hbm,
             dh_ref, dg_ref, du_ref, df_ref, gg_ref, wg_v, wu_v, wd_v, wsem):
        @pl.when(pl.program_id(0) == 0)
        def _():
            _load_weights(((wd_hbm, wd_v), (wg_hbm, wg_v), (wu_hbm, wu_v)), wsem)
            gg_ref[...] = jnp.zeros_like(gg_ref)

        dho = dho_ref[...]
        df = (0.5 * dho).astype(BF16)
        df_ref[...] = df
        dn = jnp.zeros((tm, D_MODEL), F32)
        for c0, c1 in zip(FF_BOUNDS[:-1], FF_BOUNDS[1:]):
            sl = slice(c0, c1)
            da = _dot_nt(df, wd_v[sl, :])
            dg = (da * g_ref[:, sl].astype(F32)).astype(BF16)
            du = (da * u_ref[:, sl].astype(F32)).astype(BF16)
            dg_ref[:, sl] = dg
            du_ref[:, sl] = du
            dn = dn + _dot(dg, wg_v[sl, :]) + _dot(du, wu_v[sl, :])
        hh = h_ref[...]
        r = lax.rsqrt(jnp.mean(hh * hh, axis=-1, keepdims=True) + EPS)
        hn = hh * r
        gg_ref[...] += jnp.sum(dn * hn, axis=0, keepdims=True)
        dng = dn * gain_ref[...]
        dh_ref[...] = dho + r * (dng - hn * jnp.mean(dng * hn, axis=-1, keepdims=True))

    row = lambda w: pl.BlockSpec((tm, w), lambda i: (i, 0))
    vec = pl.BlockSpec((1, D_MODEL), lambda i: (0, 0))
    wv = pltpu.VMEM((D_FF, D_MODEL), BF16)
    return pl.pallas_call(
        body, name="ffn_bwd", grid=(t // tm,),
        out_shape=(jax.ShapeDtypeStruct((t, D_MODEL), F32), jax.ShapeDtypeStruct((t, D_FF), BF16),
                   jax.ShapeDtypeStruct((t, D_FF), BF16),
                   jax.ShapeDtypeStruct((t, D_MODEL), BF16), jax.ShapeDtypeStruct((1, D_MODEL), F32)),
        in_specs=[row(D_MODEL), row(D_MODEL), vec, row(D_FF), row(D_FF), ANY, ANY, ANY],
        out_specs=(row(D_MODEL), row(D_FF), row(D_FF), row(D_MODEL), vec),
        scratch_shapes=[wv, wv, wv, pltpu.SemaphoreType.DMA((3,))],
        compiler_params=_params(("arbitrary",), VMEM_BIG),
    )(dhout, h, gain, dgf, duf, wgt, wut, wd)


def _wgrad(lhs, rhs, rb, with_colsum=False, name="wgrad", rider=None):
    t, k = lhs.shape
    n = rhs.shape[1]

    def body(l_ref, r_ref, o_ref, *rest):
        o_ref[...] = _dot_tn(l_ref[...], r_ref[...]).astype(BF16)
        if with_colsum:
            rest[0][...] = jnp.sum(l_ref[...].astype(F32), axis=0, keepdims=True)

    out_shape = [jax.ShapeDtypeStruct((k, n), BF16)]
    out_specs = [pl.BlockSpec((rb, n), lambda j: (j, 0))]
    if with_colsum:
        out_shape.append(jax.ShapeDtypeStruct((1, k), F32))
        out_specs.append(pl.BlockSpec((1, rb), lambda j: (0, j)))
    res, ro = _pallas(
        body, (lhs, rhs), name=name, grid=(k // rb,), out_shape=tuple(out_shape),
        in_specs=[pl.BlockSpec((t, rb), lambda j: (0, j)), pl.BlockSpec((t, n), lambda j: (0, 0))],
        out_specs=tuple(out_specs), sem=("arbitrary",), vmem=VMEM_BIG, rider=rider)
    if rider is not None:
        return res[0], ro
    return res if with_colsum else res[0]


STREAM_IN = pl.Buffered(3)


def _lane_blocks(nseq, seq, nblk, tm=TM, mode=None):
    spt = seq // tm
    return pl.BlockSpec((1, nblk, tm, 128), lambda i: (i // spt, 0, i % spt, 0), pipeline_mode=mode)


def _inproj_fwd(h, gain, wint, b_in, nseq, rider=None):
    t = h.shape[0]
    seq = t // nseq
    cut_a = 5 * MXU_DIM
    pieces = ((0, cut_a, 0, 0), (cut_a, ZA_W - cut_a, 0, cut_a), (ZA_W, ZB_W, 1, 0), (ZA_W + ZB_W, 1024, 2, 0),
              (ZA_W + ZB_W + 1024, 1024, 2, 1024))

    def body(h_ref, gain_ref, w_hbm, b_ref, u_ref, za_ref, zb_ref, zg_ref, w_v):
        @pl.when(pl.program_id(0) == 0)
        def _():
            pltpu.sync_copy(w_hbm, w_v)

        hh = h_ref[...]
        r = lax.rsqrt(jnp.mean(hh * hh, axis=-1, keepdims=True) + EPS)
        un = (hh * r * gain_ref[...]).astype(BF16)
        u_ref[...] = un
        outs = (None, zb_ref, zg_ref)
        for c0, cw, oi, o0 in pieces:
            val = _dot_nt(un, w_v[c0:c0 + cw, :]) + b_ref[:, c0:c0 + cw]
            if oi == 0:
                for cb in range(cw // 128):
                    za_ref[0, o0 // 128 + cb] = val[:, cb * 128:(cb + 1) * 128]
            else:
                outs[oi][:, o0:o0 + cw] = val.astype(BF16)

    row = lambda w: pl.BlockSpec((TM, w), lambda i: (i, 0))
    return _pallas(
        body, (h, gain, wint, b_in), name="inproj_fwd", grid=(t // TM,),
        out_shape=(jax.ShapeDtypeStruct((t, D_MODEL), BF16), jax.ShapeDtypeStruct((nseq, ZA_W // 128, seq, 128), F32),
                   jax.ShapeDtypeStruct((t, ZB_W), BF16), jax.ShapeDtypeStruct((t, 2 * D_MODEL), BF16)),
        in_specs=[row(D_MODEL), pl.BlockSpec((1, D_MODEL), lambda i: (0, 0)), ANY,
                  pl.BlockSpec((1, D_IN), lambda i: (0, 0))],
        out_specs=(row(D_MODEL), _lane_blocks(nseq, seq, ZA_W // 128), row(ZB_W), row(2 * D_MODEL)),
        scratch_shapes=[pltpu.VMEM((D_IN, D_MODEL), BF16)], sem=("arbitrary",), vmem=VMEM_BIG, rider=rider)


def _inproj_bwd(dz, dh2, h, gain, wint, rider=None):
    t = h.shape[0]
    nc = 5
    cw = D_IN // nc

    def body(dz_ref, dh2_ref, h_ref, gain_ref, w_hbm, dh_ref, gg_ref, w_v):
        @pl.when(pl.program_id(0) == 0)
        def _():
            pltpu.sync_copy(w_hbm, w_v)
            gg_ref[...] = jnp.zeros_like(gg_ref)

        du = jnp.zeros((TM, D_MODEL), F32)
        for ci in range(nc):
            sl = slice(ci * cw, (ci + 1) * cw)
            du = du + _dot(dz_ref[:, sl], w_v[sl, :])
        hh = h_ref[...]
        r = lax.rsqrt(jnp.mean(hh * hh, axis=-1, keepdims=True) + EPS)
        hn = hh * r
        gg_ref[...] += jnp.sum(du * hn, axis=0, keepdims=True)
        dng = du * gain_ref[...]
        dh_ref[...] = dh2_ref[...] + r * (dng - hn * jnp.mean(dng * hn, axis=-1, keepdims=True))

    row = lambda w: pl.BlockSpec((TM, w), lambda i: (i, 0))
    vec = pl.BlockSpec((1, D_MODEL), lambda i: (0, 0))
    return _pallas(
        body, (dz, dh2, h, gain, wint), name="inproj_bwd", grid=(t // TM,),
        out_shape=(jax.ShapeDtypeStruct((t, D_MODEL), F32), jax.ShapeDtypeStruct((1, D_MODEL), F32)),
        in_specs=[row(D_IN), row(D_MODEL), row(D_MODEL), vec, ANY],
        out_specs=(row(D_MODEL), vec),
        scratch_shapes=[pltpu.VMEM((D_IN, D_MODEL), BF16)], sem=("arbitrary",), vmem=VMEM_BIG, rider=rider)


def _head_sums(x):
    w = x.shape[1]
    i = lax.broadcasted_iota(jnp.int32, (w, w), 0) // HEAD_DIM
    j = lax.broadcasted_iota(jnp.int32, (w, w), 1) // HEAD_DIM
    ones = (i == j).astype(BF16)
    hi = x.astype(BF16)
    r1 = x - hi.astype(F32)
    mid = r1.astype(BF16)
    lo = (r1 - mid.astype(F32)).astype(BF16)
    return _dot(hi, ones) + _dot(mid, ones) + _dot(lo, ones)


def _merge_fwd(o0, o1, o2, l0, l1, l2, yb, zg, h1, wat, wbt, wout, rider=None):
    t = h1.shape[0]
    nseq, _, seq, _ = o0.shape

    def body(o0_ref, o1_ref, o2_ref, l0_ref, l1_ref, l2_ref, yb_ref, ga_ref, gb_ref, h1_ref, wa_ref, wb_ref, wo_ref,
             h2_ref, y_ref, lt_ref, pa_ref, pb_ref, mg_ref):
        wide = lambda ref: jnp.concatenate([ref[0, 0], ref[0, 1]], axis=1)
        la, lb, lc = wide(l0_ref), wide(l1_ref), wide(l2_ref)
        mx = jnp.maximum(jnp.maximum(la, lb), lc)
        ea, eb, ec = jnp.exp(la - mx), jnp.exp(lb - mx), jnp.exp(lc - mx)
        den = ea + eb + ec
        y = (ea * wide(o0_ref) + eb * wide(o1_ref) + ec * wide(o2_ref)) / den
        lt = mx + jnp.log(den)
        lt_ref[0, 0] = lt[:, :128]
        lt_ref[0, 1] = lt[:, 128:]
        yb16 = y.astype(BF16)
        y_ref[...] = yb16
        pa = _dot_nt(yb16, wa_ref[...])
        pb = _dot_nt(yb_ref[...], wb_ref[...])
        pa_ref[...] = pa.astype(BF16)
        pb_ref[...] = pb.astype(BF16)
        mg = (_sigmoid(ga_ref[...].astype(F32)) * pa + _sigmoid(gb_ref[...].astype(F32)) * pb).astype(BF16)
        mg_ref[...] = mg
        h2_ref[...] = h1_ref[...] + _dot(mg, wo_ref[...])

    row = lambda w: pl.BlockSpec((TM, w), lambda i: (i, 0))
    full = lambda a: pl.BlockSpec(a.shape, lambda i: (0, 0))
    gate = lambda cb: pl.BlockSpec((TM, D_MODEL), lambda i: (i, cb), pipeline_mode=STREAM_IN)
    rin = lambda w: pl.BlockSpec((TM, w), lambda i: (i, 0), pipeline_mode=STREAM_IN)
    return _pallas(
        body, (o0, o1, o2, l0, l1, l2, yb, zg, zg, h1, wat, wbt, wout), name="merge_fwd", grid=(t // TM,),
        out_shape=(jax.ShapeDtypeStruct((t, D_MODEL), F32), jax.ShapeDtypeStruct((t, GW), BF16),
                   jax.ShapeDtypeStruct((nseq, 2, seq, 128), F32), jax.ShapeDtypeStruct((t, D_MODEL), BF16),
                   jax.ShapeDtypeStruct((t, D_MODEL), BF16), jax.ShapeDtypeStruct((t, D_MODEL), BF16)),
        in_specs=[_lane_blocks(nseq, seq, 2, mode=STREAM_IN)] * 6 + [rin(2 * GW), gate(0), gate(1), rin(D_MODEL),
                                                                     full(wat), full(wbt), full(wout)],
        out_specs=(row(D_MODEL), row(GW), _lane_blocks(nseq, seq, 2), row(D_MODEL), row(D_MODEL), row(D_MODEL)),
        sem=("parallel",), vmem=VMEM_BIG, rider=rider, stream=True)


def _merge_bwd(dh2, pa, pb, zg, y, yb, wat, wbt, wout, nseq, rider=None):
    t = dh2.shape[0]

    def body(dh2_ref, pa_ref, pb_ref, ga_ref, gb_ref, y_ref, yb_ref, wa_ref, wb_ref, wo_ref,
             dpa_ref, dpb_ref, dga_ref, dgb_ref, dya_ref, dyb_ref, dh2b_ref, ca_ref, cb_ref):
        d16 = dh2_ref[...].astype(BF16)
        dh2b_ref[...] = d16
        dm = _dot_nt(d16, wo_ref[...])
        sa = _sigmoid(ga_ref[...].astype(F32))
        sb = _sigmoid(gb_ref[...].astype(F32))
        dpa = (dm * sa).astype(BF16)
        dpb = (dm * sb).astype(BF16)
        dpa_ref[...] = dpa
        dpb_ref[...] = dpb
        dga_ref[...] = (dm * pa_ref[...].astype(F32) * sa * (1.0 - sa)).astype(BF16)
        dgb_ref[...] = (dm * pb_ref[...].astype(F32) * sb * (1.0 - sb)).astype(BF16)
        dya = _dot(dpa, wa_ref[...])
        dyb = _dot(dpb, wb_ref[...])
        dya_ref[0, 0] = dya[:, :128]
        dya_ref[0, 1] = dya[:, 128:]
        dyb_ref[...] = dyb.astype(BF16)
        ca = _head_sums(dya * y_ref[...].astype(F32))
        ca_ref[0, 0] = ca[:, :128]
        ca_ref[0, 1] = ca[:, 128:]
        cb_ref[...] = _head_sums(dyb * yb_ref[...].astype(F32))

    row = lambda w: pl.BlockSpec((TM, w), lambda i: (i, 0))
    full = lambda a: pl.BlockSpec(a.shape, lambda i: (0, 0))
    gate = lambda cb: pl.BlockSpec((TM, D_MODEL), lambda i: (i, cb), pipeline_mode=STREAM_IN)
    rin = lambda w: pl.BlockSpec((TM, w), lambda i: (i, 0), pipeline_mode=STREAM_IN)
    bf = lambda w: jax.ShapeDtypeStruct((t, w), BF16)
    lanes = jax.ShapeDtypeStruct((nseq, 2, t // nseq, 128), F32)
    lane_spec = _lane_blocks(nseq, t // nseq, 2)
    return _pallas(
        body, (dh2, pa, pb, zg, zg, y, yb, wat, wbt, wout), name="merge_bwd", grid=(t // TM,),
        out_shape=(bf(D_MODEL), bf(D_MODEL), bf(D_MODEL), bf(D_MODEL), lanes, bf(2 * GW), bf(D_MODEL),
                   lanes, jax.ShapeDtypeStruct((t, 2 * GW), F32)),
        in_specs=[rin(D_MODEL), rin(D_MODEL), rin(D_MODEL), gate(0), gate(1), rin(GW), rin(2 * GW),
                  full(wat), full(wbt), full(wout)],
        out_specs=(row(D_MODEL), row(D_MODEL), row(D_MODEL), row(D_MODEL), lane_spec, row(2 * GW), row(D_MODEL),
                   lane_spec, row(2 * GW)),
        sem=("parallel",), vmem=VMEM_BIG, rider=rider, stream=True)


def _lane_head(rows):
    return lax.broadcasted_iota(jnp.int32, (rows, GW), 1) // HEAD_DIM


def _kv_expand_matrix(r):
    ci = lax.broadcasted_iota(jnp.int32, (2 * HEAD_DIM, GW), 0)
    ji = lax.broadcasted_iota(jnp.int32, (2 * HEAD_DIM, GW), 1)
    return (ci == (ji % HEAD_DIM) + HEAD_DIM * r).astype(BF16)


def _block_rows(row0, stride, ib):
    start = row0 + (stride * BLOCK) * ib
    if stride > 1:
        return pl.ds(start, BLOCK, stride=stride)
    return pl.ds(pl.multiple_of(start, BLOCK), BLOCK)


def _stack_heads(x, lane_head):
    return jnp.concatenate([jnp.where(lane_head == h, x, jnp.zeros_like(x)) for h in range(4)], axis=0)


def _unstack_heads(x4, lane_head):
    out = jnp.zeros((BLOCK, GW), F32)
    for h in range(4):
        out = jnp.where(lane_head == h, x4[h * BLOCK:(h + 1) * BLOCK], out)
    return out


def _load_rows(ref, rows, split):
    if split:
        return jnp.concatenate([ref[0, 0, rows, :], ref[0, 1, rows, :]], axis=1)
    return ref[0, rows, :]


def _store_rows(ref, rows, val, split):
    if split:
        ref[0, 0, rows, :] = val[:, :128]
        ref[0, 1, rows, :] = val[:, 128:]
    else:
        ref[0, rows, :] = val


def _attn_fwd(q_arr, k_arr, v_arr, bias, sink, *, grid, seq, stride, kvw, split, q_spec, k_spec, v_spec, bias_map,
              sink_map, o_spec, has_sink, o_shape, o_dtype, name, rider=None):
    nb = seq // stride // BLOCK
    scale = HEAD_DIM ** -0.5
    expanded = kvw != GW
    rps = min(stride, RESIDUES_PER_STEP)
    grid = (grid[0], grid[1] // rps)
    assert not has_sink or B_WINDOW - 1 < BLOCK

    def body(q_ref, k_ref, v_ref, bias_ref, sink_ref, o_ref, lse_ref, *kv_x):
        rr = pl.program_id(1)
        lane_head = _lane_head(BLOCK)
        if expanded:
            expand = _kv_expand_matrix(rr)
            kv_x[0][...] = _dot(k_ref[0], expand).astype(BF16)
            kv_x[1][...] = _dot(v_ref[0], expand).astype(BF16)
        for j in range(rps):
            residue(rr * rps + j if stride > 1 else 0, q_ref, k_ref, v_ref, bias_ref, sink_ref, o_ref, lse_ref, kv_x,
                    lane_head)

    def residue(row0, q_ref, k_ref, v_ref, bias_ref, sink_ref, o_ref, lse_ref, kv_x, lane_head):
        def per_head(fn, x):
            return jnp.concatenate([fn(sink_ref[0, h:h + 1, 0:1], x[h * BLOCK:(h + 1) * BLOCK]) for h in range(4)],
                                   axis=0)

        def load(ref, ib):
            return _load_rows(ref, _block_rows(row0, stride, ib), split).astype(BF16)

        def load_kv(which, ib):
            if expanded:
                return kv_x[which][_block_rows(0, 1, ib), :]
            return load((k_ref, v_ref)[which], ib)

        def block(ib, first):
            q4 = _stack_heads(load(q_ref, ib), lane_head)
            if first:
                kc, vc = load_kv(0, ib), load_kv(1, ib)
                b4 = bias_ref[:, :, BLOCK:].reshape(4 * BLOCK, BLOCK)
            else:
                kc = jnp.concatenate([load_kv(0, ib - 1), load_kv(0, ib)], axis=0)
                vc = jnp.concatenate([load_kv(1, ib - 1), load_kv(1, ib)], axis=0)
                b4 = bias_ref[...].reshape(4 * BLOCK, 2 * BLOCK)
                if has_sink:
                    oldest = lax.broadcasted_iota(jnp.int32, kc.shape, 0) == 0
                    kc = jnp.where(oldest, jnp.zeros_like(kc), kc)
                    vc = jnp.where(oldest, jnp.zeros_like(vc), vc)
            s = _dot_nt(q4, kc) * scale + b4
            m = jnp.max(s, axis=-1, keepdims=True)
            if has_sink and first:
                m = per_head(jnp.maximum, m)
            p = jnp.exp(s - m)
            l = jnp.sum(p, axis=-1, keepdims=True)
            if has_sink and first:
                l = l + per_head(lambda sk, mh: jnp.exp(sk - mh), m)
            o4 = _dot(p.astype(BF16), vc) / l
            rows = _block_rows(row0, stride, ib)
            _store_rows(o_ref, rows, _unstack_heads(o4, lane_head).astype(o_dtype), split)
            _store_rows(lse_ref, rows, _unstack_heads(m + jnp.log(l), lane_head), split)

        block(0, True)
        if nb > 1:
            def step(i, carry):
                block(i, False)
                return carry
            lax.fori_loop(1, nb, step, 0, unroll=min(ATTN_UNROLL, nb - 1))

    return _pallas(
        body, (q_arr, k_arr, v_arr, bias, sink), name=name, grid=grid,
        out_shape=(jax.ShapeDtypeStruct(o_shape, o_dtype), jax.ShapeDtypeStruct(o_shape, F32)),
        in_specs=[q_spec, k_spec, v_spec,
                  pl.BlockSpec((4, BLOCK, 2 * BLOCK), bias_map), pl.BlockSpec((1, 4, 128), sink_map)],
        out_specs=(o_spec, o_spec),
        scratch_shapes=[pltpu.VMEM((seq, GW), BF16)] * 2 if expanded else [],
        sem=("arbitrary", "arbitrary"), vmem=VMEM_BIG, rider=rider)


def _attn_bwd(q_arr, k_arr, v_arr, bias, sink, dy, cc, lse, *, grid, seq, stride, kvw, split, q_spec, k_spec, v_spec,
              bias_map, sink_map, o_spec, kv_out_spec, has_sink, n_bias, dq_shape, dkv_shape, g_dtype, name):
    ln = seq // stride
    nb = ln // BLOCK
    scale = HEAD_DIM ** -0.5
    expanded = kvw != GW
    rps = min(stride, RESIDUES_PER_STEP)
    grid = (grid[0], grid[1] // rps)

    def body(q_ref, k_ref, v_ref, bias_ref, sink_ref, dy_ref, c_ref, lse_ref,
             dq_ref, dk_ref, dv_ref, db_ref, dsk_ref, dk_acc, dv_acc, dk_half, dv_half, *kv_x):
        rr = pl.program_id(1)

        @pl.when((pl.program_id(0) == 0) & (rr == 0))
        def _():
            db_ref[...] = jnp.zeros_like(db_ref)
            dsk_ref[...] = jnp.zeros_like(dsk_ref)

        if expanded:
            expand = _kv_expand_matrix(rr)
            kv_x[0][...] = _dot(k_ref[0], expand).astype(BF16)
            kv_x[1][...] = _dot(v_ref[0], expand).astype(BF16)
        refs = (q_ref, k_ref, v_ref, bias_ref, sink_ref, dy_ref, c_ref, lse_ref, dq_ref, dk_ref, dv_ref, db_ref,
                dsk_ref, dk_acc, dv_acc, dk_half, dv_half, kv_x)
        for j in range(rps):
            residue(rr, rr * rps + j if stride > 1 else 0, *refs)

    def residue(rr, row0, q_ref, k_ref, v_ref, bias_ref, sink_ref, dy_ref, c_ref, lse_ref,
                dq_ref, dk_ref, dv_ref, db_ref, dsk_ref, dk_acc, dv_acc, dk_half, dv_half, kv_x):
        dk_acc[...] = jnp.zeros_like(dk_acc)
        dv_acc[...] = jnp.zeros_like(dv_acc)
        lane_head = _lane_head(BLOCK)
        hb = 4 * rr if n_bias == 8 else 0

        def load(ref, ib):
            return _load_rows(ref, _block_rows(row0, stride, ib), split)

        def load_kv(which, ib):
            if expanded:
                return kv_x[which][_block_rows(0, 1, ib), :]
            return load((k_ref, v_ref)[which], ib).astype(BF16)

        def head_col(x):
            return jnp.concatenate([x[:, h * HEAD_DIM:h * HEAD_DIM + 1] for h in range(4)], axis=0)

        def block(ib, first):
            q4 = _stack_heads(load(q_ref, ib).astype(BF16), lane_head)
            dy4 = _stack_heads(load(dy_ref, ib).astype(BF16), lane_head)
            c4 = head_col(load(c_ref, ib))
            l4 = head_col(load(lse_ref, ib))
            if first:
                kc, vc = load_kv(0, ib), load_kv(1, ib)
                b4 = bias_ref[:, :, BLOCK:].reshape(4 * BLOCK, BLOCK)
                krows = pl.ds(0, BLOCK)
            else:
                kc = jnp.concatenate([load_kv(0, ib - 1), load_kv(0, ib)], axis=0)
                vc = jnp.concatenate([load_kv(1, ib - 1), load_kv(1, ib)], axis=0)
                b4 = bias_ref[...].reshape(4 * BLOCK, 2 * BLOCK)
                krows = pl.ds(pl.multiple_of((ib - 1) * BLOCK, BLOCK), 2 * BLOCK)
            nk = BLOCK if first else 2 * BLOCK
            p = jnp.exp(_dot_nt(q4, kc) * scale + b4 - l4)
            ds = p * (_dot_nt(dy4, vc) - c4)
            ds3 = ds.reshape(4, BLOCK, nk)
            if n_bias == 8:
                if first:
                    db_ref[pl.ds(hb, 4), :, BLOCK:] += ds3
                else:
                    db_ref[pl.ds(hb, 4)] += ds3
            elif first:
                db_ref[:, :, BLOCK:] += ds3
            else:
                db_ref[...] += ds3
            ds16 = ds.astype(BF16)
            dq = _unstack_heads(_dot(ds16, kc), lane_head) * scale
            _store_rows(dq_ref, _block_rows(row0, stride, ib), dq.astype(g_dtype), split)
            dk_acc[krows, :] += _dot_tn(ds16, q4) * scale
            dv_acc[krows, :] += _dot_tn(p.astype(BF16), dy4)
            if has_sink:
                for h in range(4):
                    hs = slice(h * BLOCK, (h + 1) * BLOCK)
                    sk = sink_ref[0, h:h + 1, 0:1]
                    val = -jnp.sum(jnp.exp(sk - l4[hs]) * c4[hs], axis=0, keepdims=True)
                    dsk_ref[hb + h] += jnp.broadcast_to(val, (8, 128))

        block(0, True)
        if nb > 1:
            def step(i, carry):
                block(i, False)
                return carry
            lax.fori_loop(1, nb, step, 0, unroll=min(ATTN_UNROLL, nb - 1))

        if kvw == GW:
            all_rows = pl.ds(row0, ln, stride=stride) if stride > 1 else pl.ds(0, ln)
            _store_rows(dk_ref, all_rows, dk_acc[...].astype(g_dtype), split)
            _store_rows(dv_ref, all_rows, dv_acc[...].astype(g_dtype), split)
        else:
            def fold(acc):
                t2 = acc[:, :2 * HEAD_DIM] + acc[:, 2 * HEAD_DIM:]
                t2 = t2 + pltpu.roll(t2, HEAD_DIM, 1)
                lane = lax.broadcasted_iota(jnp.int32, t2.shape, 1) // HEAD_DIM
                return jnp.where(lane == rr, t2, 0.0)

            @pl.when(rr == 0)
            def _():
                dk_half[...] = fold(dk_acc[...])
                dv_half[...] = fold(dv_acc[...])

            @pl.when(rr == 1)
            def _():
                dk_ref[0] = (dk_half[...] + fold(dk_acc[...])).astype(g_dtype)
                dv_ref[0] = (dv_half[...] + fold(dv_acc[...])).astype(g_dtype)

    return pl.pallas_call(
        body, name=name, grid=grid,
        out_shape=(jax.ShapeDtypeStruct(dq_shape, g_dtype), jax.ShapeDtypeStruct(dkv_shape, g_dtype),
                   jax.ShapeDtypeStruct(dkv_shape, g_dtype), jax.ShapeDtypeStruct((n_bias, BLOCK, 2 * BLOCK), F32),
                   jax.ShapeDtypeStruct((8, 8, 128), F32)),
        in_specs=[q_spec, k_spec, v_spec,
                  pl.BlockSpec((4, BLOCK, 2 * BLOCK), bias_map), pl.BlockSpec((1, 4, 128), sink_map),
                  o_spec, o_spec, o_spec],
        out_specs=(o_spec, kv_out_spec, kv_out_spec,
                   pl.BlockSpec((n_bias, BLOCK, 2 * BLOCK), lambda n, r: (0, 0, 0)),
                   pl.BlockSpec((8, 8, 128), lambda n, r: (0, 0, 0))),
        scratch_shapes=[pltpu.VMEM((ln, GW), F32), pltpu.VMEM((ln, GW), F32),
                        pltpu.VMEM((ln, 2 * HEAD_DIM), F32), pltpu.VMEM((ln, 2 * HEAD_DIM), F32)]
        + ([pltpu.VMEM((seq, GW), BF16)] * 2 if expanded else []),
        compiler_params=_params(("arbitrary", "arbitrary"), VMEM_BIG),
    )(q_arr, k_arr, v_arr, bias, sink, dy, cc, lse)


def _bias_grad(ds_all, buckets):
    def body(ds_ref, bk_ref, o_ref):
        rows = lax.broadcasted_iota(jnp.int32, (N_BUCKETS, 128), 0)
        cols = lax.broadcasted_iota(jnp.int32, (N_BUCKETS, 128), 1)

        def per_bucket(b, acc):
            for h in range(20):
                gi = h // 4 if h < 12 else 3
                v = jnp.where(bk_ref[gi] == b, ds_ref[h], 0.0)
                v = jnp.sum(jnp.sum(v, axis=1, keepdims=True), axis=0, keepdims=True)
                acc = jnp.where((rows == b) & (cols == h), v, acc)
            return acc

        o_ref[...] = lax.fori_loop(0, N_BUCKETS, per_bucket, jnp.zeros((N_BUCKETS, 128), F32))

    vm = pl.BlockSpec(memory_space=pltpu.VMEM)
    return pl.pallas_call(body, name="bias_grad", out_shape=jax.ShapeDtypeStruct((N_BUCKETS, 128), F32),
                          in_specs=[vm, vm], out_specs=vm)(ds_all, buckets)


def _adamw(w, g, m, v, name):
    (res,), _ = _adamw_many([(w, g, m, v)], name)
    return res


def _adamw_many(tensors, name, rider=None):
    n = len(tensors)
    r, c = tensors[0][0].shape
    tr = r
    for cand in (256, 176, 128, 88, 64, 32, 16, 8):
        if r % cand == 0 and cand * c * 4 * 7 * n * 2 <= 24 * 1024 * 1024:
            tr = cand
            break

    def body(*refs):
        ins, outs = refs[:4 * n], refs[4 * n:]
        for i in range(n):
            w_ref, g_ref, m_ref, v_ref = ins[4 * i:4 * i + 4]
            d, nm, nv = _adam_update(w_ref[...], g_ref[...], m_ref[...], v_ref[...])
            outs[3 * i][...], outs[3 * i + 1][...], outs[3 * i + 2][...] = d, nm, nv

    spec = pl.BlockSpec((tr, c), lambda i: (i, 0))
    shp = jax.ShapeDtypeStruct((r, c), F32)
    res, ro = _pallas(body, tuple(a for t4 in tensors for a in t4), name=name, grid=(r // tr,),
                      out_shape=(shp,) * (3 * n), in_specs=[spec] * (4 * n), out_specs=(spec,) * (3 * n),
                      sem=("parallel",), vmem=VMEM_BIG, rider=rider)
    return [tuple(res[3 * i:3 * i + 3]) for i in range(n)], ro


def _t5_bucket(dist):
    max_exact = N_BUCKETS // 2
    n = jnp.maximum(dist, 0)
    nf = jnp.maximum(n, 1).astype(F32)
    large = max_exact + (jnp.log(nf / max_exact) / math.log(MAX_DISTANCE / max_exact)
                         * (N_BUCKETS - max_exact)).astype(jnp.int32)
    large = jnp.minimum(large, N_BUCKETS - 1)
    return jnp.where(n < max_exact, n, large)


def _bias_tables(rel_bias):
    qi = jnp.arange(BLOCK)[:, None]
    ki = jnp.arange(2 * BLOCK)[None, :]
    dist = qi + BLOCK - ki
    specs = [(d, w // d, 4 * gi, 4 * gi + 4) for gi, (w, d) in enumerate(DIL_GROUPS)] + [(1, B_WINDOW - 1, 12, 20)]
    biases, buckets = [], []
    for stride, steps, h0, h1 in specs:
        valid = (dist >= 0) & (dist <= steps)
        bk = jnp.where(valid, _t5_bucket(dist * stride), -1).astype(jnp.int32)
        onehot = (bk[None, :, :] == jnp.arange(N_BUCKETS, dtype=jnp.int32)[:, None, None]).astype(F32)
        b = jnp.einsum("bqk,bh->hqk", onehot, rel_bias[:, h0:h1], precision=lax.Precision.HIGHEST)
        biases.append(jnp.where(valid[None], b, NEG))
        buckets.append(bk)
    return jnp.concatenate(biases, axis=0), jnp.stack(buckets, axis=0)


def _local_step(x, tgt, W, S, shards=None, tail_host=None):
    nseq, seq, _ = x.shape
    t = nseq * seq
    xf = x.reshape(t, D_MODEL)
    bias_all, buckets = _bias_tables(S["rel_bias"])
    sink_b = jnp.broadcast_to(S["sinks"].reshape(2, 4, 1), (2, 4, 128)).astype(F32)
    sink_0 = jnp.zeros((1, 4, 128), F32)
    dist = shards is not None
    W = dict(W)
    G, GS, reduced = {}, {}, {}

    def put(keys, gathered):
        for k, g in zip(keys, gathered):
            W[k] = g.reshape(_FULL_SHAPE.get(k, (N_CHIPS * shards[k].shape[0], D_MODEL)))

    def gather_rider(keys):
        return _GatherRider([shards[k] for k in keys]) if dist else None

    def pair(keys):
        return _pair_reduce([G[k].reshape(N_CHIPS, 2, shards[k].shape[0] // 2, D_MODEL) for k in keys],
                            "grad_pair_reduce_" + keys[0])

    def finish(keys, own, rec):
        full = _final_reduce(own, rec, "grad_final_reduce_" + keys[0])
        off = 0
        for k in keys:
            r = shards[k].shape[0]
            reduced[k] = full[:, off:off + r // 2].reshape(r, D_MODEL)
            off += r // 2

    if dist:
        first = ("wgt1", "wut1", "wd1")
        put(first, _gather_rows([shards[k] for k in first]))
    keys = ("wint",)
    (h1, n1, g1, u1, a1), ro = _ffn_fwd(xf, S["ffn1_norm"], W["wgt1"], W["wut1"], W["wd1"], rider=gather_rider(keys))
    put(keys, ro)
    keys = ("wout", "wat", "wbt", "wgt2")
    (un, za, zb, zg), ro = _inproj_fwd(h1, S["mix_norm"], W["wint"], S["b_in"], nseq, rider=gather_rider(keys))
    put(keys, ro)

    seq3 = lambda a: a.reshape(nseq, seq, a.shape[-1])
    zb3 = seq3(zb)
    pair_blk = lambda cb: pl.BlockSpec((1, 2, seq, 128), lambda n, r, cb=cb: (n, cb, 0, 0))
    a_cfg = []
    outs, lses = [], []
    for gi, (_, d) in enumerate(DIL_GROUPS):
        cfg = dict(grid=(nseq, d), seq=seq, stride=d, kvw=GW, split=True,
                   q_spec=pair_blk(gi), k_spec=pair_blk(3 + gi), v_spec=pair_blk(6 + gi), o_spec=pair_blk(0),
                   bias_map=lambda n, r: (0, 0, 0), sink_map=lambda n, r: (0, 0, 0), has_sink=False)
        a_cfg.append(cfg)
        (o, lse), _ = _attn_fwd(za, za, za, bias_all[4 * gi:4 * gi + 4], sink_0, o_shape=(nseq, 2, seq, 128),
                                o_dtype=F32, name=f"attn_a{gi}_fwd", **cfg)
        outs.append(o)
        lses.append(lse)
    wide_blk = lambda w, cmap: pl.BlockSpec((1, seq, w), cmap)
    b_cfg = dict(grid=(nseq, 2), seq=seq, stride=1, kvw=2 * HEAD_DIM, split=False,
                 q_spec=wide_blk(GW, lambda n, r: (n, 0, r)), k_spec=wide_blk(2 * HEAD_DIM, lambda n, r: (n, 0, 4)),
                 v_spec=wide_blk(2 * HEAD_DIM, lambda n, r: (n, 0, 5)), o_spec=wide_blk(GW, lambda n, r: (n, 0, r)),
                 bias_map=lambda n, r: (r, 0, 0), sink_map=lambda n, r: (r, 0, 0), has_sink=True)
    keys = ("wut2",)
    bias_b_fwd = bias_all[12:20].at[:, :, 0].set(jnp.broadcast_to(S["sinks"].reshape(8, 1), (8, BLOCK)))
    (yb, lse_b), ro = _attn_fwd(zb3, zb3, zb3, bias_b_fwd, sink_b, o_shape=(nseq, seq, 2 * GW), o_dtype=BF16,
                                name="attn_b_fwd", rider=gather_rider(keys), **b_cfg)
    put(keys, ro)
    yb = yb.reshape(t, 2 * GW)

    keys = ("wd2",)
    (h2, y, lse_tot, pa, pb, merged), ro = _merge_fwd(outs[0], outs[1], outs[2], lses[0], lses[1], lses[2], yb, zg, h1,
                                                      W["wat"], W["wbt"], W["wout"], rider=gather_rider(keys))
    put(keys, ro)
    (dh3, n2, g2, u2, a2, loss_part, g_final), _ = _ffn_fwd(
        h2, S["ffn2_norm"], W["wgt2"], W["wut2"], W["wd2"],
        head=(S["final_norm"].reshape(1, D_MODEL), tgt.reshape(t, D_MODEL)))

    GS["final_norm"] = g_final
    dh2, dg2, du2, df2, GS["ffn2_norm"] = _ffn_bwd(dh3, h2, S["ffn2_norm"], g2, u2, W["wgt2"], W["wut2"], W["wd2"])
    G["wgt2"] = _wgrad(dg2, n2, MXU_DIM, name="wgrad_gate2")
    G["wut2"] = _wgrad(du2, n2, MXU_DIM, name="wgrad_up2")
    G["wd2"] = _wgrad(a2, df2, MXU_DIM, name="wgrad_down2")

    keys = ("wgt2", "wut2", "wd2")
    rider = _ExchangeRider([pair(keys)]) if dist else None
    (dpa, dpb, dga, dgb, dya, dyb, dh2b, ca, cb), ro = _merge_bwd(dh2, pa, pb, zg, y, yb, W["wat"], W["wbt"], W["wout"],
                                                                  nseq, rider=rider)
    if dist:
        finish(keys, *ro)

    dqs, dks, dvs, dbs = [], [], [], []
    shp = (nseq, 2, seq, 128)
    halves = lambda a: [a[:, hf].reshape(t, 128).astype(BF16) for hf in range(2)]
    for gi in range(len(DIL_GROUPS)):
        dq, dk, dv, db, _ = _attn_bwd(za, za, za, bias_all[4 * gi:4 * gi + 4], sink_0, dya, ca, lse_tot,
                                      n_bias=4, dq_shape=shp, dkv_shape=shp, g_dtype=F32,
                                      kv_out_spec=a_cfg[gi]["o_spec"], name=f"attn_a{gi}_bwd", **a_cfg[gi])
        dqs += halves(dq)
        dks += halves(dk)
        dvs += halves(dv)
        dbs.append(db)
    dqb, dkb, dvb, dbb, dsink = _attn_bwd(zb3, zb3, zb3, bias_all[12:20], sink_b, seq3(dyb), seq3(cb), lse_b,
                                          n_bias=8, dq_shape=(nseq, seq, 2 * GW),
                                          dkv_shape=(nseq, seq, 2 * HEAD_DIM), g_dtype=BF16,
                                          kv_out_spec=wide_blk(2 * HEAD_DIM, lambda n, r: (n, 0, 0)),
                                          name="attn_b_bwd", **b_cfg)
    dz = jnp.concatenate(dqs + dks + dvs + [dqb.reshape(t, 2 * GW), dkb.reshape(t, 2 * HEAD_DIM),
                                            dvb.reshape(t, 2 * HEAD_DIM), dga, dgb], axis=-1)
    gb_tab = _bias_grad(jnp.concatenate(dbs + [dbb], axis=0), buckets)
    if dist:
        GS["bias_tab"], GS["sink_tiles"] = gb_tab, dsink
    else:
        GS["rel_bias"] = gb_tab[:, :20]
        GS["sinks"] = dsink[:, 0, 0].reshape(1, 8)

    G["wint"], GS["b_in"] = _wgrad(dz, un, MXU_DIM, with_colsum=True, name="wgrad_in")
    G["wout"] = _wgrad(merged, dh2b, MXU_DIM, name="wgrad_out")
    G["wat"] = _wgrad(dpa, y, MXU_DIM, name="wgrad_branch_a")
    G["wbt"] = _wgrad(dpb, yb, MXU_DIM, name="wgrad_branch_b")
    keys = ("wint", "wout", "wat", "wbt")
    rider = _ExchangeRider([pair(keys)]) if dist else None
    (dh1, GS["mix_norm"]), ro = _inproj_bwd(dz, dh2, h1, S["mix_norm"], W["wint"], rider=rider)
    if dist:
        finish(keys, *ro)

    dx, dg1, du1, df1, GS["ffn1_norm"] = _ffn_bwd(dh1, xf, S["ffn1_norm"], g1, u1, W["wgt1"], W["wut1"], W["wd1"])
    G["wgt1"] = _wgrad(dg1, n1, MXU_DIM, name="wgrad_gate1")
    if dist:
        G["wut1"], ro = _wgrad(du1, n1, MXU_DIM, name="wgrad_up1", rider=_ExchangeRider([pair(("wgt1",))]))
        finish(("wgt1",), *ro)
        G["wd1"], ro = _wgrad(a1, df1, MXU_DIM, name="wgrad_down1", rider=_ExchangeRider([pair(("wut1",))]))
        finish(("wut1",), *ro)
        finish(("wd1",), *tail_host(_ExchangeRider([pair(("wd1",))]), reduced))
    else:
        G["wut1"] = _wgrad(du1, n1, MXU_DIM, name="wgrad_up1")
        G["wd1"] = _wgrad(a1, df1, MXU_DIM, name="wgrad_down1")
    return loss_part, dx.reshape(x.shape), (reduced if dist else G), GS


_SMALL = ("ffn1_norm", "mix_norm", "ffn2_norm", "final_norm", "b_in", "sinks", "rel_bias")
_ORDER = ("ffn1_norm", "ffn1_w_gate", "ffn1_w_up", "ffn1_w_down", "mix_norm", "w_in", "b_in", "w_branch_a",
          "w_branch_b", "w_out", "sinks", "rel_bias", "ffn2_norm", "ffn2_w_gate", "ffn2_w_up", "ffn2_w_down",
          "final_norm")
_BIG = (("wgt1", "ffn1_w_gate", True, 704), ("wut1", "ffn1_w_up", True, 704), ("wd1", "ffn1_w_down", False, 704),
        ("wint", "w_in", True, 1280), ("wout", "w_out", False, 256), ("wat", "w_branch_a", True, 64),
        ("wbt", "w_branch_b", True, 128), ("wgt2", "ffn2_w_gate", True, 704), ("wut2", "ffn2_w_up", True, 704),
        ("wd2", "ffn2_w_down", False, 704))
_FULL_SHAPE = {"wat": (D_MODEL, GW), "wbt": (D_MODEL, 2 * GW)}


def kernel(x, ffn1_norm, ffn1_w_gate, ffn1_w_up, ffn1_w_down, mix_norm, w_in, b_in, w_branch_a, w_branch_b, w_out, sinks, rel_bias, ffn2_norm, ffn2_w_gate, ffn2_w_up, ffn2_w_down, final_norm, loss_target, m_ffn1_norm, m_ffn1_w_gate, m_ffn1_w_up, m_ffn1_w_down, m_mix_norm, m_w_in, m_b_in, m_w_branch_a, m_w_branch_b, m_w_out, m_sinks, m_rel_bias, m_ffn2_norm, m_ffn2_w_gate, m_ffn2_w_up, m_ffn2_w_down, m_final_norm, v_ffn1_norm, v_ffn1_w_gate, v_ffn1_w_up, v_ffn1_w_down, v_mix_norm, v_w_in, v_b_in, v_w_branch_a, v_w_branch_b, v_w_out, v_sinks, v_rel_bias, v_ffn2_norm, v_ffn2_w_gate, v_ffn2_w_up, v_ffn2_w_down, v_final_norm):
    args = dict(locals())
    w = {n: args[n] for n in _ORDER}
    m = {n: args["m_" + n] for n in _ORDER}
    v = {n: args["v_" + n] for n in _ORDER}

    shards = {}
    for key, name, transposed, rows in _BIG:
        a = w[name][0]
        a = (a.T if transposed else a).astype(BF16)
        shards[key] = a.reshape(rows, D_MODEL)
    S = {n: w[n] for n in _SMALL}

    row_adam = lambda n: (w[n][0].T, m[n][0].T, v[n][0].T)
    early = {}

    def tail_host(rider, reduced):
        tensors = []
        for key, n in (("wgt2", "ffn2_w_gate"), ("wut2", "ffn2_w_up"), ("wd2", "ffn2_w_down")):
            wmv = row_adam(n) if key != "wd2" else (w[n][0], m[n][0], v[n][0])
            tensors.append((wmv[0], reduced[key], wmv[1], wmv[2]))
        res, ro = _adamw_many(tensors, "adamw_ffn2", rider=rider)
        early["ffn2_w_gate"], early["ffn2_w_up"], early["ffn2_w_down"] = res
        return ro

    loss_part, grad_x, reduced, GS = _local_step(x, loss_target, {}, S, shards, tail_host)

    small = _allreduce_small(GS["ffn1_norm"], GS["mix_norm"], GS["ffn2_norm"], GS["final_norm"], GS["b_in"],
                             GS["sink_tiles"], GS["bias_tab"], loss_part)
    loss = small[9, 8]

    out_g, out_d, out_m, out_v = {}, {}, {}, {}
    for key, n, transposed, rows in _BIG:
        nat = w[n][0].shape
        if transposed and nat[1] % 128:
            res = early[n] if n in early else _adamw(row_adam(n)[0], reduced[key], *row_adam(n)[1:], "adamw_" + n)
            res = [reduced[key].T] + [r.T for r in res]
        elif n in early:
            res = [reduced[key]] + list(early[n])
        else:
            g = reduced[key].reshape(nat[1], nat[0]).T if transposed else reduced[key].reshape(nat)
            res = [g] + list(_adamw(w[n][0], g, m[n][0], v[n][0], "adamw_" + n))
        out_g[n], out_d[n], out_m[n], out_v[n] = [r[None] for r in res]
    row = lambda d: {n: (d[n].reshape(1, D_MODEL) if n == "final_norm" else d[n]) for n in _SMALL}
    for dst, src in zip((out_g, out_d, out_m, out_v), _adamw_small(small, row(w), row(m), row(v))):
        dst.update(src)
        dst["final_norm"] = src["final_norm"].reshape(D_MODEL)

    return (loss, grad_x, *[out_g[n] for n in _ORDER], *[out_d[n] for n in _ORDER],
            *[out_m[n] for n in _ORDER], *[out_v[n] for n in _ORDER])
```

```python
import math

import jax
import jax.numpy as jnp
from jax import lax
from jax.experimental import pallas as pl
from jax.experimental.pallas import tpu as pltpu

F32, BF16 = jnp.float32, jnp.bfloat16
MESH = pl.DeviceIdType.MESH

D_MODEL = 1024
D_FF = 2816
D_IN = 5120
HEAD_DIM = 64
BLOCK = 128
DIL_GROUPS = ((128, 1), (512, 4), (2048, 16))
B_WINDOW = 128
N_BUCKETS = 32
MAX_DISTANCE = 2048
EPS = 1e-6
N_CHIPS = 4
GW = 256
ZA_W = 2304
ZB_W = 768
NEG = -1e30

ADAM_LR, ADAM_B1, ADAM_B2, ADAM_EPS, ADAM_WD, ADAM_STEP = 0.001, 0.9, 0.999, 1e-08, 0.01, 10

VMEM_BIG = 56 * 1024 * 1024
TM = 512
TM_BWD = 256
MXU_DIM = 256
FF_BOUNDS = (0, 4 * MXU_DIM, 8 * MXU_DIM, D_FF)
DMA_SPLIT = 8
RESIDUES_PER_STEP = 16
ATTN_UNROLL = 15


def _dot(a, b):
    return jnp.dot(a, b, preferred_element_type=F32)


def _dot_nt(a, b):
    return lax.dot_general(a, b, (((1,), (1,)), ((), ())), preferred_element_type=F32)


def _dot_tn(a, b):
    return lax.dot_general(a, b, (((0,), (0,)), ((), ())), preferred_element_type=F32)


def _sigmoid(x):
    return 0.5 * jnp.tanh(0.5 * x) + 0.5


def _params(sem, vmem=None):
    return pltpu.CompilerParams(dimension_semantics=sem, vmem_limit_bytes=vmem)


ANY = pl.BlockSpec(memory_space=pl.ANY)


def _me():
    return lax.axis_index("x"), lax.axis_index("y"), lax.axis_index("c")


_CHIP_RELS = ((1, 0), (0, 1), (1, 1))


def _flip(v, f):
    return 1 - v if f else v


def _remote(src, dst, ssem, rsem, peer):
    return pltpu.make_async_remote_copy(src_ref=src, dst_ref=dst, send_sem=ssem, recv_sem=rsem,
                                        device_id=peer, device_id_type=MESH)


def _row_pieces(rows, n):
    step = max(16, -(-rows // n) // 16 * 16)
    out, s = [], 0
    while s < rows:
        out.append((s, min(step, rows - s)))
        s += step
    return out


def _gather_rows(shards):
    nt = len(shards)
    rows = [s.shape[0] for s in shards]

    def body(*refs):
        srcs, outs = refs[:nt], refs[nt:2 * nt]
        halves, quarters = refs[2 * nt:3 * nt], refs[3 * nt:4 * nt]
        ici_s, ici_r, fwd_s, fwd_r, d2d_s, d2d_r, keep, loc = refs[4 * nt:]
        x, y, c = _me()
        j = 2 * x + y
        sib = (x, y, 1 - c)
        nbr = ((1 - x, y, c), (x, 1 - y, c))
        nbr_j = (2 * (1 - x) + y, 2 * x + (1 - y))
        diag_j = 2 * (1 - x) + (1 - y)
        local = [pltpu.make_async_copy(srcs[t], outs[t].at[j], loc.at[t]) for t in range(nt)]
        for cp in local:
            cp.start()
        pending = []
        for a in range(2):
            for t in range(nt):
                half = pl.ds(c * (rows[t] // 2), rows[t] // 2)
                cp = _remote(srcs[t].at[half], halves[t].at[a], ici_s.at[2 * t + a], ici_r.at[2 * t + a], nbr[a])
                cp.start()
                pending.append(cp)
        placed = []

        def place(src, dst_of, idx):
            mine = pltpu.make_async_copy(src, dst_of, keep.at[idx])
            mine.start()
            cp = _remote(src, dst_of, d2d_s.at[idx], d2d_r.at[idx], sib)
            cp.start()
            placed.append((mine, cp))

        for a in range(2):
            for t in range(nt):
                r2, r4 = rows[t] // 2, rows[t] // 4
                got = halves[t].at[a]
                _remote(got, got, ici_s.at[2 * t + a], ici_r.at[2 * t + a], nbr[a]).wait_recv()
                cp = _remote(halves[t].at[a, pl.ds(a * r4, r4)], quarters[t].at[a], fwd_s.at[2 * t + a],
                             fwd_r.at[2 * t + a], nbr[1 - a])
                cp.start()
                pending.append(cp)
                place(got, outs[t].at[nbr_j[a], pl.ds(c * r2, r2)], 4 * t + a)
        for a in range(2):
            for t in range(nt):
                r2, r4 = rows[t] // 2, rows[t] // 4
                got = quarters[t].at[a]
                _remote(got, got, fwd_s.at[2 * t + a], fwd_r.at[2 * t + a], nbr[1 - a]).wait_recv()
                place(got, outs[t].at[diag_j, pl.ds(c * r2 + a * r4, r4)], 4 * t + 2 + a)
        for mine, cp in placed:
            mine.wait()
            cp.wait()
        for cp in pending:
            cp.wait_send()
        for cp in local:
            cp.wait()

    stage = ([pltpu.VMEM((2, r // 2, D_MODEL), BF16) for r in rows] + [pltpu.VMEM((2, r // 4, D_MODEL), BF16) for r in rows])
    sems = ([pltpu.SemaphoreType.DMA((2 * nt,)) for _ in range(4)] + [pltpu.SemaphoreType.DMA((4 * nt,))] * 3
            + [pltpu.SemaphoreType.DMA((nt,))])
    return pl.pallas_call(
        body, name="gather_weights",
        out_shape=tuple(jax.ShapeDtypeStruct((N_CHIPS,) + s.shape, s.dtype) for s in shards),
        in_specs=[pl.BlockSpec(memory_space=pltpu.VMEM)] * nt,
        out_specs=tuple([ANY] * nt), scratch_shapes=stage + sems,
    )(*shards)


VMEM_WHOLE = pl.BlockSpec(memory_space=pltpu.VMEM)


def _pair_reduce(grads, name):
    nt = len(grads)
    r2 = [g.shape[2] for g in grads]
    off = [sum(r2[:t]) for t in range(nt)]
    tot = sum(r2)

    def body(*refs):
        gs = refs[:nt]
        s_ref, mine, got, ssem, rsem, lsem = refs[nt:]
        x, y, c = _me()
        sib = (x, y, 1 - c)
        for t in range(nt):
            for k in range(N_CHIPS):
                rows = pl.ds(off[t], r2[t])
                _remote(gs[t].at[k, 1 - c], got.at[k, rows], ssem, rsem, sib).start()
                pltpu.make_async_copy(gs[t].at[k, c], mine.at[k, rows], lsem).start()
        pltpu.make_async_copy(mine, mine, lsem).wait()
        _remote(got, got, ssem, rsem, sib).wait()
        for k in range(N_CHIPS):
            for st, sz in _row_pieces(tot, 4):
                rows = slice(st, st + sz)
                s_ref[k, rows, :] = (mine[k, rows, :].astype(F32) + got[k, rows, :].astype(F32)).astype(BF16)

    shp = jax.ShapeDtypeStruct((N_CHIPS, tot, D_MODEL), BF16)
    buf = pltpu.VMEM((N_CHIPS, tot, D_MODEL), BF16)
    return pl.pallas_call(
        body, name=name, out_shape=shp, in_specs=[ANY] * nt, out_specs=VMEM_WHOLE,
        scratch_shapes=[buf, buf, pltpu.SemaphoreType.DMA(()), pltpu.SemaphoreType.DMA(()),
                        pltpu.SemaphoreType.DMA(())],
        compiler_params=pltpu.CompilerParams(vmem_limit_bytes=VMEM_BIG),
    )(*grads)


def _final_reduce(own, rec, name):
    r2 = own.shape[0]
    stages = _row_pieces(r2, 2)

    def body(own_hbm, rec_hbm, o_ref, parts, fbuf, ssem, rsem, lsem, insems):
        x, y, c = _me()
        sib = (x, y, 1 - c)
        for p, (st, sz) in enumerate(stages):
            rows = pl.ds(st, sz)
            pltpu.make_async_copy(own_hbm.at[rows], parts.at[0, rows], insems.at[p]).start()
            for k in range(3):
                pltpu.make_async_copy(rec_hbm.at[k, rows], parts.at[1 + k, rows], insems.at[p]).start()
        for p, (st, sz) in enumerate(stages):
            stage = parts.at[:, pl.ds(st, sz)]
            pltpu.make_async_copy(stage, stage, insems.at[p]).wait()
            for s0, ssz in _row_pieces(sz, DMA_SPLIT // 2):
                rows = slice(st + s0, st + s0 + ssz)
                fbuf[rows, :] = (parts[0, rows, :].astype(F32) + parts[1, rows, :].astype(F32)
                                 + parts[2, rows, :].astype(F32) + parts[3, rows, :].astype(F32))
                dst = o_ref.at[c, pl.ds(st + s0, ssz)]
                pltpu.make_async_copy(fbuf.at[pl.ds(st + s0, ssz)], dst, lsem).start()
                _remote(fbuf.at[pl.ds(st + s0, ssz)], dst, ssem, rsem, sib).start()
        _remote(fbuf, o_ref.at[c], ssem, rsem, sib).wait()
        pltpu.make_async_copy(fbuf, o_ref.at[c], lsem).wait()

    return pl.pallas_call(
        body, name=name, out_shape=jax.ShapeDtypeStruct((2, r2, D_MODEL), F32),
        in_specs=[ANY, ANY], out_specs=ANY,
        scratch_shapes=[pltpu.VMEM((4, r2, D_MODEL), BF16), pltpu.VMEM((r2, D_MODEL), F32),
                        pltpu.SemaphoreType.DMA(()), pltpu.SemaphoreType.DMA(()), pltpu.SemaphoreType.DMA(()),
                        pltpu.SemaphoreType.DMA((2,))],
        compiler_params=pltpu.CompilerParams(vmem_limit_bytes=VMEM_BIG),
    )(own, rec)


SMALL_ROWS = 48


def _allreduce_small(g_ffn1, g_mix, g_ffn2, g_final, g_bin, dsink, bias_tab, loss_part):
    def body(f1_ref, mx_ref, f2_ref, fn_ref, bi_ref, sk_ref, bt_ref, ls_ref, o_ref, mine, buf, tabs, send_sems,
             recv_sems):
        x, y, c = _me()
        me = 4 * x + 2 * y + c
        mine[...] = jnp.zeros_like(mine)
        for r, ref in enumerate((f1_ref, mx_ref, f2_ref, fn_ref)):
            mine[r:r + 1, :] = ref[...]
        for k in range(D_IN // D_MODEL):
            mine[4 + k:5 + k, :] = bi_ref[:, k * D_MODEL:(k + 1) * D_MODEL]
        lane = lax.broadcasted_iota(jnp.int32, (1, 128), 1)
        row = jnp.where(lane == 8, ls_ref[0:1, :], 0.0)
        for h in range(8):
            row = jnp.where(lane == h, sk_ref[h, 0:1, :], row)
        mine[9:10, 0:128] = row
        buf[me] = mine[...]
        tabs[me] = bt_ref[...]
        copies = []
        for k in range(1, 8):
            peer = (_flip(x, (k >> 2) & 1), _flip(y, (k >> 1) & 1), _flip(c, k & 1))
            for t, (src, dst) in enumerate(((mine, buf), (bt_ref, tabs))):
                cp = _remote(src, dst.at[me], send_sems.at[2 * (k - 1) + t], recv_sems.at[2 * (k - 1) + t], peer)
                cp.start()
                copies.append(cp)
        for cp in copies:
            cp.wait()
        acc, tab = buf[0], tabs[0]
        for i in range(1, 8):
            acc, tab = acc + buf[i], tab + tabs[i]
        o_ref[...] = jnp.zeros_like(o_ref)
        o_ref[0:16, :] = acc
        o_ref[16:48, 0:128] = tab

    vm = pl.BlockSpec(memory_space=pltpu.VMEM)
    return pl.pallas_call(
        body, name="allreduce_small", out_shape=jax.ShapeDtypeStruct((SMALL_ROWS, D_MODEL), F32),
        in_specs=[vm] * 8, out_specs=vm,
        scratch_shapes=[pltpu.VMEM((16, D_MODEL), F32), pltpu.VMEM((8, 16, D_MODEL), F32),
                        pltpu.VMEM((8, N_BUCKETS, 128), F32), pltpu.SemaphoreType.DMA((14,)),
                        pltpu.SemaphoreType.DMA((14,))],
    )(g_ffn1, g_mix, g_ffn2, g_final, g_bin, dsink, bias_tab, loss_part)


def _adam_update(w, g, m, v):
    nm = ADAM_B1 * m + (1.0 - ADAM_B1) * g
    nv = ADAM_B2 * v + (1.0 - ADAM_B2) * (g * g)
    bc1 = 1.0 - ADAM_B1 ** ADAM_STEP
    bc2 = 1.0 - ADAM_B2 ** ADAM_STEP
    return -ADAM_LR * ((nm / bc1) / (jnp.sqrt(nv / bc2) + ADAM_EPS) + ADAM_WD * w), nm, nv


def _adamw_small(packed, w, m, v):
    names = ("ffn1_norm", "mix_norm", "ffn2_norm", "final_norm", "b_in", "sinks", "rel_bias")
    nn = len(names)

    def grad_of(p_ref, name, k=0):
        if name == "b_in":
            return p_ref[4 + k:5 + k, :]
        if name == "sinks":
            return p_ref[9:10, 0:8]
        if name == "rel_bias":
            return p_ref[16:48, 0:20]
        r = names.index(name)
        return p_ref[r:r + 1, :]

    def body(p_ref, *refs):
        ws, ms, vs = refs[:nn], refs[nn:2 * nn], refs[2 * nn:3 * nn]
        outs = refs[3 * nn:]
        for i, name in enumerate(names):
            og, od, om, ov = outs[i], outs[nn + i], outs[2 * nn + i], outs[3 * nn + i]
            pieces = range(D_IN // D_MODEL) if name == "b_in" else (0,)
            for k in pieces:
                sl = (slice(None), slice(k * D_MODEL, (k + 1) * D_MODEL)) if name == "b_in" else (Ellipsis,)
                g = grad_of(p_ref, name, k)
                d, nm, nv = _adam_update(ws[i][sl], g, ms[i][sl], vs[i][sl])
                og[sl], od[sl], om[sl], ov[sl] = g, d, nm, nv

    vm = pl.BlockSpec(memory_space=pltpu.VMEM)
    shapes = [jax.ShapeDtypeStruct(w[n].shape, F32) for n in names]
    res = pl.pallas_call(
        body, name="adamw_small", out_shape=tuple(shapes * 4), in_specs=[vm] * (1 + 3 * nn),
        out_specs=tuple([vm] * (4 * nn)),
    )(packed, *[w[n] for n in names], *[m[n] for n in names], *[v[n] for n in names])
    return [dict(zip(names, res[i * nn:(i + 1) * nn])) for i in range(4)]


class _GatherRider:
    def __init__(self, shards):
        self.inputs = list(shards)
        nt = len(shards)
        self.out_shape = [jax.ShapeDtypeStruct((N_CHIPS,) + s.shape, s.dtype) for s in shards]
        self.scratch = [pltpu.SemaphoreType.DMA((3 * nt,)), pltpu.SemaphoreType.DMA((3 * nt,)),
                        pltpu.SemaphoreType.DMA((nt,))]

    def _copies(self, srcs, outs, sems):
        ici_s, ici_r, loc = sems
        x, y, c = _me()
        j = 2 * x + y
        local = [pltpu.make_async_copy(srcs[t], outs[t].at[j], loc.at[t]) for t in range(len(srcs))]
        remote = []
        for k, (fx, fy) in enumerate(_CHIP_RELS):
            peer = (_flip(x, fx), _flip(y, fy), c)
            for t in range(len(srcs)):
                remote.append(_remote(srcs[t], outs[t].at[j], ici_s.at[3 * t + k], ici_r.at[3 * t + k], peer))
        return local, remote

    def start(self, srcs, outs, sems):
        local, remote = self._copies(srcs, outs, sems)
        for cp in local + remote:
            cp.start()

    def finish(self, srcs, outs, sems):
        local, remote = self._copies(srcs, outs, sems)
        for cp in remote + local:
            cp.wait()


class _ExchangeRider:
    def __init__(self, parts):
        self.inputs = list(parts)
        self.r2 = [p.shape[1] for p in parts]
        self.off = [sum(self.r2[:g]) for g in range(len(parts))]
        tot = sum(self.r2)
        self.out_shape = [jax.ShapeDtypeStruct((tot, D_MODEL), BF16), jax.ShapeDtypeStruct((3, tot, D_MODEL), BF16)]
        self.scratch = [pltpu.SemaphoreType.DMA((3,)), pltpu.SemaphoreType.DMA((3,)), pltpu.SemaphoreType.DMA(())]

    def start(self, ps, outs, sems):
        own_ref, rec_ref = outs
        ssems, rsems, lsem = sems
        x, y, c = _me()
        j = 2 * x + y
        for g in range(len(ps)):
            pltpu.make_async_copy(ps[g].at[j], own_ref.at[pl.ds(self.off[g], self.r2[g])], lsem).start()
        for k, (fx, fy) in enumerate(_CHIP_RELS):
            px, py = _flip(x, fx), _flip(y, fy)
            for g in range(len(ps)):
                for st, sz in _row_pieces(self.r2[g], 2):
                    _remote(ps[g].at[2 * px + py, pl.ds(st, sz)], rec_ref.at[k, pl.ds(self.off[g] + st, sz)],
                            ssems.at[k], rsems.at[k], (px, py, c)).start()

    def finish(self, ps, outs, sems):
        own_ref, rec_ref = outs
        ssems, rsems, lsem = sems
        x, y, c = _me()
        for k in range(3):
            _remote(rec_ref.at[k], rec_ref.at[k], ssems.at[k], rsems.at[k], (x, y, c)).wait()
        pltpu.make_async_copy(own_ref, own_ref, lsem).wait()


def _pallas(body, args, *, name, grid, in_specs, out_specs, out_shape, scratch_shapes=(), sem=None, vmem=None,
            rider=None, stream=False):
    if stream:
        n_in, n_out = len(in_specs), len(out_shape)
        r_in, r_out = (len(rider.inputs), len(rider.out_shape)) if rider is not None else (0, 0)

        def outer(*refs):
            ins, rins = refs[:n_in], refs[n_in:n_in + r_in]
            p = n_in + r_in
            outs, routs, rsems = refs[p:p + n_out], refs[p + n_out:p + n_out + r_out], refs[p + n_out + r_out:]
            if rider is not None:
                rider.start(rins, routs, rsems)
            pltpu.emit_pipeline(body, grid=grid, in_specs=list(in_specs), out_specs=list(out_specs))(*ins, *outs)
            if rider is not None:
                rider.finish(rins, routs, rsems)

        res = pl.pallas_call(
            outer, name=name, in_specs=[ANY] * (n_in + r_in), out_specs=(ANY,) * (n_out + r_out),
            out_shape=tuple(out_shape) + tuple(rider.out_shape if rider is not None else ()),
            scratch_shapes=list(rider.scratch) if rider is not None else [],
            compiler_params=pltpu.CompilerParams(vmem_limit_bytes=vmem),
        )(*args, *(rider.inputs if rider is not None else ()))
        return tuple(res[:n_out]), tuple(res[n_out:])
    if rider is None:
        res = pl.pallas_call(body, name=name, grid=grid, in_specs=list(in_specs), out_specs=tuple(out_specs),
                             out_shape=tuple(out_shape), scratch_shapes=list(scratch_shapes),
                             compiler_params=_params(sem, vmem))(*args)
        return tuple(res), ()
    n_in, n_out, n_sc = len(in_specs), len(out_shape), len(scratch_shapes)
    r_in, r_out = len(rider.inputs), len(rider.out_shape)

    def wrapped(*refs):
        ins, rins = refs[:n_in], refs[n_in:n_in + r_in]
        p = n_in + r_in
        outs, routs = refs[p:p + n_out], refs[p + n_out:p + n_out + r_out]
        p += n_out + r_out
        scr, rsems = refs[p:p + n_sc], refs[p + n_sc:]
        first = pl.program_id(0) == 0
        last = pl.program_id(0) == grid[0] - 1
        for a in range(1, len(grid)):
            first = first & (pl.program_id(a) == 0)
            last = last & (pl.program_id(a) == grid[a] - 1)

        @pl.when(first)
        def _():
            rider.start(rins, routs, rsems)

        body(*ins, *outs, *scr)

        @pl.when(last)
        def _():
            rider.finish(rins, routs, rsems)

    res = pl.pallas_call(
        wrapped, name=name, grid=grid, in_specs=list(in_specs) + [ANY] * r_in,
        out_specs=tuple(out_specs) + (ANY,) * r_out, out_shape=tuple(out_shape) + tuple(rider.out_shape),
        scratch_shapes=list(scratch_shapes) + rider.scratch,
        compiler_params=_params(("arbitrary",) * len(grid), vmem))(*args, *rider.inputs)
    return tuple(res[:n_out]), tuple(res[n_out:])


def _load_weights(pairs, sems):
    copies = [pltpu.make_async_copy(hbm, vmem, sems.at[i]) for i, (hbm, vmem) in enumerate(pairs)]
    for cp in copies:
        cp.start()
    for cp in copies:
        cp.wait()


def _loss_tile(hh, gain, tgt):
    r = lax.rsqrt(jnp.mean(hh * hh, axis=-1, keepdims=True) + EPS)
    hn = hh * r
    err = hn * gain - tgt
    part = (0.5 / D_MODEL) * jnp.sum(jnp.sum(err * err, axis=1, keepdims=True), axis=0, keepdims=True)
    dy = err * (1.0 / D_MODEL)
    dng = dy * gain
    dh = r * (dng - hn * jnp.mean(dng * hn, axis=-1, keepdims=True))
    return dh, part, jnp.sum(dy * hn, axis=0, keepdims=True)


def _ffn_fwd(h, gain, wgt, wut, wd, rider=None, head=None):
    t = h.shape[0]

    def body(h_ref, gain_ref, wg_hbm, wu_hbm, wd_hbm, *rest):
        if head is None:
            hout_ref, n_ref, g_ref, u_ref, a_ref, wg_v, wu_v, wd_v, wsem = rest
        else:
            fg_ref, tgt_ref, hout_ref, n_ref, g_ref, u_ref, a_ref, loss_ref, gg_ref, wg_v, wu_v, wd_v, wsem = rest
        @pl.when(pl.program_id(0) == 0)
        def _():
            _load_weights(((wg_hbm, wg_v), (wu_hbm, wu_v), (wd_hbm, wd_v)), wsem)
            if head is not None:
                loss_ref[...] = jnp.zeros_like(loss_ref)
                gg_ref[...] = jnp.zeros_like(gg_ref)

        hh = h_ref[...]
        r = lax.rsqrt(jnp.mean(hh * hh, axis=-1, keepdims=True) + EPS)
        n = (hh * r * gain_ref[...]).astype(BF16)
        n_ref[...] = n
        acc = jnp.zeros((TM, D_MODEL), F32)
        for c0, c1 in zip(FF_BOUNDS[:-1], FF_BOUNDS[1:]):
            sl = slice(c0, c1)
            g = _dot_nt(n, wg_v[sl, :])
            u = _dot_nt(n, wu_v[sl, :])
            sg = _sigmoid(g)
            silu = g * sg
            a = (silu * u).astype(BF16)
            a_ref[:, sl] = a
            g_ref[:, sl] = (u * (sg * (1.0 + g * (1.0 - sg)))).astype(BF16)
            u_ref[:, sl] = silu.astype(BF16)
            acc = acc + _dot(a, wd_v[sl, :])
        hout = hh + 0.5 * acc
        if head is None:
            hout_ref[...] = hout
        else:
            dh, part, gpart = _loss_tile(hout, fg_ref[...], tgt_ref[...])
            hout_ref[...] = dh
            loss_ref[...] += part
            gg_ref[...] += gpart

    row = lambda w: pl.BlockSpec((TM, w), lambda i: (i, 0))
    vec = pl.BlockSpec((1, D_MODEL), lambda i: (0, 0))
    wv = pltpu.VMEM((D_FF, D_MODEL), BF16)
    args, in_specs = (h, gain, wgt, wut, wd), [row(D_MODEL), vec, ANY, ANY, ANY]
    out_shape = [jax.ShapeDtypeStruct((t, D_MODEL), F32), jax.ShapeDtypeStruct((t, D_MODEL), BF16)] + [
        jax.ShapeDtypeStruct((t, D_FF), BF16)] * 3
    out_specs = [row(D_MODEL), row(D_MODEL), row(D_FF), row(D_FF), row(D_FF)]
    if head is not None:
        args, in_specs = args + tuple(head), in_specs + [vec, row(D_MODEL)]
        out_shape += [jax.ShapeDtypeStruct((8, 128), F32), jax.ShapeDtypeStruct((1, D_MODEL), F32)]
        out_specs += [pl.BlockSpec((8, 128), lambda i: (0, 0)), vec]
    return _pallas(
        body, args, name="ffn_fwd", grid=(t // TM,), out_shape=tuple(out_shape), in_specs=in_specs,
        out_specs=tuple(out_specs), scratch_shapes=[wv, wv, wv, pltpu.SemaphoreType.DMA((3,))],
        sem=("arbitrary",), vmem=VMEM_BIG, rider=rider)


def _ffn_bwd(dhout, h, gain, dgf, duf, wgt, wut, wd):
    t = h.shape[0]
    tm = TM_BWD

    def body(dho_ref, h_ref, gain_ref, g_ref, u_ref, wg_hbm, wu_hbm, wd_hbm,
             dh_ref, dg_ref, du_ref, df_ref, gg_ref, wg_v, wu_v, wd_v, wsem):
        @pl.when(pl.program_id(0) == 0)
        def _():
            _load_weights(((wd_hbm, wd_v), (wg_hbm, wg_v), (wu_hbm, wu_v)), wsem)
            gg_ref[...] = jnp.zeros_like(gg_ref)

        dho = dho_ref[...]
        df = (0.5 * dho).astype(BF16)
        df_ref[...] = df
        dn = jnp.zeros((tm, D_MODEL), F32)
        for c0, c1 in zip(FF_BOUNDS[:-1], FF_BOUNDS[1:]):
            sl = slice(c0, c1)
            da = _dot_nt(df, wd_v[sl, :])
            dg = (da * g_ref[:, sl].astype(F32)).astype(BF16)
            du = (da * u_ref[:, sl].astype(F32)).astype(BF16)
            dg_ref[:, sl] = dg
            du_ref[:, sl] = du
            dn = dn + _dot(dg, wg_v[sl, :]) + _dot(du, wu_v[sl, :])
        hh = h_ref[...]
        r = lax.rsqrt(jnp.mean(hh * hh, axis=-1, keepdims=True) + EPS)
        hn = hh * r
        gg_ref[...] += jnp.sum(dn * hn, axis=0, keepdims=True)
        dng = dn * gain_ref[...]
        dh_ref[...] = dho + r * (dng - hn * jnp.mean(dng * hn, axis=-1, keepdims=True))

    row = lambda w: pl.BlockSpec((tm, w), lambda i: (i, 0))
    vec = pl.BlockSpec((1, D_MODEL), lambda i: (0, 0))
    wv = pltpu.VMEM((D_FF, D_MODEL), BF16)
    return pl.pallas_call(
        body, name="ffn_bwd", grid=(t // tm,),
        out_shape=(jax.ShapeDtypeStruct((t, D_MODEL), F32), jax.ShapeDtypeStruct((t, D_FF), BF16),
                   jax.ShapeDtypeStruct((t, D_FF), BF16),
                   jax.ShapeDtypeStruct((t, D_MODEL), BF16), jax.ShapeDtypeStruct((1, D_MODEL), F32)),
        in_specs=[row(D_MODEL), row(D_MODEL), vec, row(D_FF), row(D_FF), ANY, ANY, ANY],
        out_specs=(row(D_MODEL), row(D_FF), row(D_FF), row(D_MODEL), vec),
        scratch_shapes=[wv, wv, wv, pltpu.SemaphoreType.DMA((3,))],
        compiler_params=_params(("arbitrary",), VMEM_BIG),
    )(dhout, h, gain, dgf, duf, wgt, wut, wd)


def _wgrad(lhs, rhs, rb, with_colsum=False, name="wgrad", rider=None):
    t, k = lhs.shape
    n = rhs.shape[1]

    def body(l_ref, r_ref, o_ref, *rest):
        o_ref[...] = _dot_tn(l_ref[...], r_ref[...]).astype(BF16)
        if with_colsum:
            rest[0][...] = jnp.sum(l_ref[...].astype(F32), axis=0, keepdims=True)

    out_shape = [jax.ShapeDtypeStruct((k, n), BF16)]
    out_specs = [pl.BlockSpec((rb, n), lambda j: (j, 0))]
    if with_colsum:
        out_shape.append(jax.ShapeDtypeStruct((1, k), F32))
        out_specs.append(pl.BlockSpec((1, rb), lambda j: (0, j)))
    res, ro = _pallas(
        body, (lhs, rhs), name=name, grid=(k // rb,), out_shape=tuple(out_shape),
        in_specs=[pl.BlockSpec((t, rb), lambda j: (0, j)), pl.BlockSpec((t, n), lambda j: (0, 0))],
        out_specs=tuple(out_specs), sem=("arbitrary",), vmem=VMEM_BIG, rider=rider)
    if rider is not None:
        return res[0], ro
    return res if with_colsum else res[0]


STREAM_IN = pl.Buffered(3)


def _lane_blocks(nseq, seq, nblk, tm=TM, mode=None):
    spt = seq // tm
    return pl.BlockSpec((1, nblk, tm, 128), lambda i: (i // spt, 0, i % spt, 0), pipeline_mode=mode)


def _inproj_fwd(h, gain, wint, b_in, nseq, rider=None):
    t = h.shape[0]
    seq = t // nseq
    cut_a = 5 * MXU_DIM
    pieces = ((0, cut_a, 0, 0), (cut_a, ZA_W - cut_a, 0, cut_a), (ZA_W, ZB_W, 1, 0), (ZA_W + ZB_W, 1024, 2, 0),
              (ZA_W + ZB_W + 1024, 1024, 2, 1024))

    def body(h_ref, gain_ref, w_hbm, b_ref, u_ref, za_ref, zb_ref, zg_ref, w_v):
        @pl.when(pl.program_id(0) == 0)
        def _():
            pltpu.sync_copy(w_hbm, w_v)

        hh = h_ref[...]
        r = lax.rsqrt(jnp.mean(hh * hh, axis=-1, keepdims=True) + EPS)
        un = (hh * r * gain_ref[...]).astype(BF16)
        u_ref[...] = un
        outs = (None, zb_ref, zg_ref)
        for c0, cw, oi, o0 in pieces:
            val = _dot_nt(un, w_v[c0:c0 + cw, :]) + b_ref[:, c0:c0 + cw]
            if oi == 0:
                for cb in range(cw // 128):
                    za_ref[0, o0 // 128 + cb] = val[:, cb * 128:(cb + 1) * 128]
            else:
                outs[oi][:, o0:o0 + cw] = val.astype(BF16)

    row = lambda w: pl.BlockSpec((TM, w), lambda i: (i, 0))
    return _pallas(
        body, (h, gain, wint, b_in), name="inproj_fwd", grid=(t // TM,),
        out_shape=(jax.ShapeDtypeStruct((t, D_MODEL), BF16), jax.ShapeDtypeStruct((nseq, ZA_W // 128, seq, 128), F32),
                   jax.ShapeDtypeStruct((t, ZB_W), BF16), jax.ShapeDtypeStruct((t, 2 * D_MODEL), BF16)),
        in_specs=[row(D_MODEL), pl.BlockSpec((1, D_MODEL), lambda i: (0, 0)), ANY,
                  pl.BlockSpec((1, D_IN), lambda i: (0, 0))],
        out_specs=(row(D_MODEL), _lane_blocks(nseq, seq, ZA_W // 128), row(ZB_W), row(2 * D_MODEL)),
        scratch_shapes=[pltpu.VMEM((D_IN, D_MODEL), BF16)], sem=("arbitrary",), vmem=VMEM_BIG, rider=rider)


def _inproj_bwd(dz, dh2, h, gain, wint, rider=None):
    t = h.shape[0]
    nc = 5
    cw = D_IN // nc

    def body(dz_ref, dh2_ref, h_ref, gain_ref, w_hbm, dh_ref, gg_ref, w_v):
        @pl.when(pl.program_id(0) == 0)
        def _():
            pltpu.sync_copy(w_hbm, w_v)
            gg_ref[...] = jnp.zeros_like(gg_ref)

        du = jnp.zeros((TM, D_MODEL), F32)
        for ci in range(nc):
            sl = slice(ci * cw, (ci + 1) * cw)
            du = du + _dot(dz_ref[:, sl], w_v[sl, :])
        hh = h_ref[...]
        r = lax.rsqrt(jnp.mean(hh * hh, axis=-1, keepdims=True) + EPS)
        hn = hh * r
        gg_ref[...] += jnp.sum(du * hn, axis=0, keepdims=True)
        dng = du * gain_ref[...]
        dh_ref[...] = dh2_ref[...] + r * (dng - hn * jnp.mean(dng * hn, axis=-1, keepdims=True))

    row = lambda w: pl.BlockSpec((TM, w), lambda i: (i, 0))
    vec = pl.BlockSpec((1, D_MODEL), lambda i: (0, 0))
    return _pallas(
        body, (dz, dh2, h, gain, wint), name="inproj_bwd", grid=(t // TM,),
        out_shape=(jax.ShapeDtypeStruct((t, D_MODEL), F32), jax.ShapeDtypeStruct((1, D_MODEL), F32)),
        in_specs=[row(D_IN), row(D_MODEL), row(D_MODEL), vec, ANY],
        out_specs=(row(D_MODEL), vec),
        scratch_shapes=[pltpu.VMEM((D_IN, D_MODEL), BF16)], sem=("arbitrary",), vmem=VMEM_BIG, rider=rider)


def _head_sums(x):
    w = x.shape[1]
    i = lax.broadcasted_iota(jnp.int32, (w, w), 0) // HEAD_DIM
    j = lax.broadcasted_iota(jnp.int32, (w, w), 1) // HEAD_DIM
    ones = (i == j).astype(BF16)
    hi = x.astype(BF16)
    r1 = x - hi.astype(F32)
    mid = r1.astype(BF16)
    lo = (r1 - mid.astype(F32)).astype(BF16)
    return _dot(hi, ones) + _dot(mid, ones) + _dot(lo, ones)


def _merge_fwd(o0, o1, o2, l0, l1, l2, yb, zg, h1, wat, wbt, wout, rider=None):
    t = h1.shape[0]
    nseq, _, seq, _ = o0.shape

    def body(o0_ref, o1_ref, o2_ref, l0_ref, l1_ref, l2_ref, yb_ref, ga_ref, gb_ref, h1_ref, wa_ref, wb_ref, wo_ref,
             h2_ref, y_ref, lt_ref, pa_ref, pb_ref, mg_ref):
        wide = lambda ref: jnp.concatenate([ref[0, 0], ref[0, 1]], axis=1)
        la, lb, lc = wide(l0_ref), wide(l1_ref), wide(l2_ref)
        mx = jnp.maximum(jnp.maximum(la, lb), lc)
        ea, eb, ec = jnp.exp(la - mx), jnp.exp(lb - mx), jnp.exp(lc - mx)
        den = ea + eb + ec
        y = (ea * wide(o0_ref) + eb * wide(o1_ref) + ec * wide(o2_ref)) / den
        lt = mx + jnp.log(den)
        lt_ref[0, 0] = lt[:, :128]
        lt_ref[0, 1] = lt[:, 128:]
        yb16 = y.astype(BF16)
        y_ref[...] = yb16
        pa = _dot_nt(yb16, wa_ref[...])
        pb = _dot_nt(yb_ref[...], wb_ref[...])
        pa_ref[...] = pa.astype(BF16)
        pb_ref[...] = pb.astype(BF16)
        mg = (_sigmoid(ga_ref[...].astype(F32)) * pa + _sigmoid(gb_ref[...].astype(F32)) * pb).astype(BF16)
        mg_ref[...] = mg
        h2_ref[...] = h1_ref[...] + _dot(mg, wo_ref[...])

    row = lambda w: pl.BlockSpec((TM, w), lambda i: (i, 0))
    full = lambda a: pl.BlockSpec(a.shape, lambda i: (0, 0))
    gate = lambda cb: pl.BlockSpec((TM, D_MODEL), lambda i: (i, cb), pipeline_mode=STREAM_IN)
    rin = lambda w: pl.BlockSpec((TM, w), lambda i: (i, 0), pipeline_mode=STREAM_IN)
    return _pallas(
        body, (o0, o1, o2, l0, l1, l2, yb, zg, zg, h1, wat, wbt, wout), name="merge_fwd", grid=(t // TM,),
        out_shape=(jax.ShapeDtypeStruct((t, D_MODEL), F32), jax.ShapeDtypeStruct((t, GW), BF16),
                   jax.ShapeDtypeStruct((nseq, 2, seq, 128), F32), jax.ShapeDtypeStruct((t, D_MODEL), BF16),
                   jax.ShapeDtypeStruct((t, D_MODEL), BF16), jax.ShapeDtypeStruct((t, D_MODEL), BF16)),
        in_specs=[_lane_blocks(nseq, seq, 2, mode=STREAM_IN)] * 6 + [rin(2 * GW), gate(0), gate(1), rin(D_MODEL),
                                                                     full(wat), full(wbt), full(wout)],
        out_specs=(row(D_MODEL), row(GW), _lane_blocks(nseq, seq, 2), row(D_MODEL), row(D_MODEL), row(D_MODEL)),
        sem=("parallel",), vmem=VMEM_BIG, rider=rider, stream=True)


def _merge_bwd(dh2, pa, pb, zg, y, yb, wat, wbt, wout, nseq, rider=None):
    t = dh2.shape[0]

    def body(dh2_ref, pa_ref, pb_ref, ga_ref, gb_ref, y_ref, yb_ref, wa_ref, wb_ref, wo_ref,
             dpa_ref, dpb_ref, dga_ref, dgb_ref, dya_ref, dyb_ref, dh2b_ref, ca_ref, cb_ref):
        d16 = dh2_ref[...].astype(BF16)
        dh2b_ref[...] = d16
        dm = _dot_nt(d16, wo_ref[...])
        sa = _sigmoid(ga_ref[...].astype(F32))
        sb = _sigmoid(gb_ref[...].astype(F32))
        dpa = (dm * sa).astype(BF16)
        dpb = (dm * sb).astype(BF16)
        dpa_ref[...] = dpa
        dpb_ref[...] = dpb
        dga_ref[...] = (dm * pa_ref[...].astype(F32) * sa * (1.0 - sa)).astype(BF16)
        dgb_ref[...] = (dm * pb_ref[...].astype(F32) * sb * (1.0 - sb)).astype(BF16)
        dya = _dot(dpa, wa_ref[...])
        dyb = _dot(dpb, wb_ref[...])
        dya_ref[0, 0] = dya[:, :128]
        dya_ref[0, 1] = dya[:, 128:]
        dyb_ref[...] = dyb.astype(BF16)
        ca = _head_sums(dya * y_ref[...].astype(F32))
        ca_ref[0, 0] = ca[:, :128]
        ca_ref[0, 1] = ca[:, 128:]
        cb_ref[...] = _head_sums(dyb * yb_ref[...].astype(F32))

    row = lambda w: pl.BlockSpec((TM, w), lambda i: (i, 0))
    full = lambda a: pl.BlockSpec(a.shape, lambda i: (0, 0))
    gate = lambda cb: pl.BlockSpec((TM, D_MODEL), lambda i: (i, cb), pipeline_mode=STREAM_IN)
    rin = lambda w: pl.BlockSpec((TM, w), lambda i: (i, 0), pipeline_mode=STREAM_IN)
    bf = lambda w: jax.ShapeDtypeStruct((t, w), BF16)
    lanes = jax.ShapeDtypeStruct((nseq, 2, t // nseq, 128), F32)
    lane_spec = _lane_blocks(nseq, t // nseq, 2)
    return _pallas(
        body, (dh2, pa, pb, zg, zg, y, yb, wat, wbt, wout), name="merge_bwd", grid=(t // TM,),
        out_shape=(bf(D_MODEL), bf(D_MODEL), bf(D_MODEL), bf(D_MODEL), lanes, bf(2 * GW), bf(D_MODEL),
                   lanes, jax.ShapeDtypeStruct((t, 2 * GW), F32)),
        in_specs=[rin(D_MODEL), rin(D_MODEL), rin(D_MODEL), gate(0), gate(1), rin(GW), rin(2 * GW),
                  full(wat), full(wbt), full(wout)],
        out_specs=(row(D_MODEL), row(D_MODEL), row(D_MODEL), row(D_MODEL), lane_spec, row(2 * GW), row(D_MODEL),
                   lane_spec, row(2 * GW)),
        sem=("parallel",), vmem=VMEM_BIG, rider=rider, stream=True)


def _lane_head(rows):
    return lax.broadcasted_iota(jnp.int32, (rows, GW), 1) // HEAD_DIM


def _kv_expand_matrix(r):
    ci = lax.broadcasted_iota(jnp.int32, (2 * HEAD_DIM, GW), 0)
    ji = lax.broadcasted_iota(jnp.int32, (2 * HEAD_DIM, GW), 1)
    return (ci == (ji % HEAD_DIM) + HEAD_DIM * r).astype(BF16)


def _block_rows(row0, stride, ib):
    start = row0 + (stride * BLOCK) * ib
    if stride > 1:
        return pl.ds(start, BLOCK, stride=stride)
    return pl.ds(pl.multiple_of(start, BLOCK), BLOCK)


def _stack_heads(x, lane_head):
    return jnp.concatenate([jnp.where(lane_head == h, x, jnp.zeros_like(x)) for h in range(4)], axis=0)


def _unstack_heads(x4, lane_head):
    out = jnp.zeros((BLOCK, GW), F32)
    for h in range(4):
        out = jnp.where(lane_head == h, x4[h * BLOCK:(h + 1) * BLOCK], out)
    return out


def _load_rows(ref, rows, split):
    if split:
        return jnp.concatenate([ref[0, 0, rows, :], ref[0, 1, rows, :]], axis=1)
    return ref[0, rows, :]


def _store_rows(ref, rows, val, split):
    if split:
        ref[0, 0, rows, :] = val[:, :128]
        ref[0, 1, rows, :] = val[:, 128:]
    else:
        ref[0, rows, :] = val


def _attn_fwd(q_arr, k_arr, v_arr, bias, sink, *, grid, seq, stride, kvw, split, q_spec, k_spec, v_spec, bias_map,
              sink_map, o_spec, has_sink, o_shape, o_dtype, name, rider=None):
    nb = seq // stride // BLOCK
    scale = HEAD_DIM ** -0.5
    expanded = kvw != GW
    rps = min(stride, RESIDUES_PER_STEP)
    grid = (grid[0], grid[1] // rps)
    assert not has_sink or B_WINDOW - 1 < BLOCK

    def body(q_ref, k_ref, v_ref, bias_ref, sink_ref, o_ref, lse_ref, *kv_x):
        rr = pl.program_id(1)
        lane_head = _lane_head(BLOCK)
        if expanded:
            expand = _kv_expand_matrix(rr)
            kv_x[0][...] = _dot(k_ref[0], expand).astype(BF16)
            kv_x[1][...] = _dot(v_ref[0], expand).astype(BF16)
        for j in range(rps):
            residue(rr * rps + j if stride > 1 else 0, q_ref, k_ref, v_ref, bias_ref, sink_ref, o_ref, lse_ref, kv_x,
                    lane_head)

    def residue(row0, q_ref, k_ref, v_ref, bias_ref, sink_ref, o_ref, lse_ref, kv_x, lane_head):
        def per_head(fn, x):
            return jnp.concatenate([fn(sink_ref[0, h:h + 1, 0:1], x[h * BLOCK:(h + 1) * BLOCK]) for h in range(4)],
                                   axis=0)

        def load(ref, ib):
            return _load_rows(ref, _block_rows(row0, stride, ib), split).astype(BF16)

        def load_kv(which, ib):
            if expanded:
                return kv_x[which][_block_rows(0, 1, ib), :]
            return load((k_ref, v_ref)[which], ib)

        def block(ib, first):
            q4 = _stack_heads(load(q_ref, ib), lane_head)
            if first:
                kc, vc = load_kv(0, ib), load_kv(1, ib)
                b4 = bias_ref[:, :, BLOCK:].reshape(4 * BLOCK, BLOCK)
            else:
                kc = jnp.concatenate([load_kv(0, ib - 1), load_kv(0, ib)], axis=0)
                vc = jnp.concatenate([load_kv(1, ib - 1), load_kv(1, ib)], axis=0)
                b4 = bias_ref[...].reshape(4 * BLOCK, 2 * BLOCK)
                if has_sink:
                    oldest = lax.broadcasted_iota(jnp.int32, kc.shape, 0) == 0
                    kc = jnp.where(oldest, jnp.zeros_like(kc), kc)
                    vc = jnp.where(oldest, jnp.zeros_like(vc), vc)
            s = _dot_nt(q4, kc) * scale + b4
            m = jnp.max(s, axis=-1, keepdims=True)
            if has_sink and first:
                m = per_head(jnp.maximum, m)
            p = jnp.exp(s - m)
            l = jnp.sum(p, axis=-1, keepdims=True)
            if has_sink and first:
                l = l + per_head(lambda sk, mh: jnp.exp(sk - mh), m)
            o4 = _dot(p.astype(BF16), vc) / l
            rows = _block_rows(row0, stride, ib)
            _store_rows(o_ref, rows, _unstack_heads(o4, lane_head).astype(o_dtype), split)
            _store_rows(lse_ref, rows, _unstack_heads(m + jnp.log(l), lane_head), split)

        block(0, True)
        if nb > 1:
            def step(i, carry):
                block(i, False)
                return carry
            lax.fori_loop(1, nb, step, 0, unroll=min(ATTN_UNROLL, nb - 1))

    return _pallas(
        body, (q_arr, k_arr, v_arr, bias, sink), name=name, grid=grid,
        out_shape=(jax.ShapeDtypeStruct(o_shape, o_dtype), jax.ShapeDtypeStruct(o_shape, F32)),
        in_specs=[q_spec, k_spec, v_spec,
                  pl.BlockSpec((4, BLOCK, 2 * BLOCK), bias_map), pl.BlockSpec((1, 4, 128), sink_map)],
        out_specs=(o_spec, o_spec),
        scratch_shapes=[pltpu.VMEM((seq, GW), BF16)] * 2 if expanded else [],
        sem=("arbitrary", "arbitrary"), vmem=VMEM_BIG, rider=rider)


def _attn_bwd(q_arr, k_arr, v_arr, bias, sink, dy, cc, lse, *, grid, seq, stride, kvw, split, q_spec, k_spec, v_spec,
              bias_map, sink_map, o_spec, kv_out_spec, has_sink, n_bias, dq_shape, dkv_shape, g_dtype, name):
    ln = seq // stride
    nb = ln // BLOCK
    scale = HEAD_DIM ** -0.5
    expanded = kvw != GW
    rps = min(stride, RESIDUES_PER_STEP)
    grid = (grid[0], grid[1] // rps)

    def body(q_ref, k_ref, v_ref, bias_ref, sink_ref, dy_ref, c_ref, lse_ref,
             dq_ref, dk_ref, dv_ref, db_ref, dsk_ref, dk_acc, dv_acc, dk_half, dv_half, *kv_x):
        rr = pl.program_id(1)

        @pl.when((pl.program_id(0) == 0) & (rr == 0))
        def _():
            db_ref[...] = jnp.zeros_like(db_ref)
            dsk_ref[...] = jnp.zeros_like(dsk_ref)

        if expanded:
            expand = _kv_expand_matrix(rr)
            kv_x[0][...] = _dot(k_ref[0], expand).astype(BF16)
            kv_x[1][...] = _dot(v_ref[0], expand).astype(BF16)
        refs = (q_ref, k_ref, v_ref, bias_ref, sink_ref, dy_ref, c_ref, lse_ref, dq_ref, dk_ref, dv_ref, db_ref,
                dsk_ref, dk_acc, dv_acc, dk_half, dv_half, kv_x)
        for j in range(rps):
            residue(rr, rr * rps + j if stride > 1 else 0, *refs)

    def residue(rr, row0, q_ref, k_ref, v_ref, bias_ref, sink_ref, dy_ref, c_ref, lse_ref,
                dq_ref, dk_ref, dv_ref, db_ref, dsk_ref, dk_acc, dv_acc, dk_half, dv_half, kv_x):
        dk_acc[...] = jnp.zeros_like(dk_acc)
        dv_acc[...] = jnp.zeros_like(dv_acc)
        lane_head = _lane_head(BLOCK)
        hb = 4 * rr if n_bias == 8 else 0

        def load(ref, ib):
            return _load_rows(ref, _block_rows(row0, stride, ib), split)

        def load_kv(which, ib):
            if expanded:
                return kv_x[which][_block_rows(0, 1, ib), :]
            return load((k_ref, v_ref)[which], ib).astype(BF16)

        def head_col(x):
            return jnp.concatenate([x[:, h * HEAD_DIM:h * HEAD_DIM + 1] for h in range(4)], axis=0)

        def block(ib, first):
            q4 = _stack_heads(load(q_ref, ib).astype(BF16), lane_head)
            dy4 = _stack_heads(load(dy_ref, ib).astype(BF16), lane_head)
            c4 = head_col(load(c_ref, ib))
            l4 = head_col(load(lse_ref, ib))
            if first:
                kc, vc = load_kv(0, ib), load_kv(1, ib)
                b4 = bias_ref[:, :, BLOCK:].reshape(4 * BLOCK, BLOCK)
                krows = pl.ds(0, BLOCK)
            else:
                kc = jnp.concatenate([load_kv(0, ib - 1), load_kv(0, ib)], axis=0)
                vc = jnp.concatenate([load_kv(1, ib - 1), load_kv(1, ib)], axis=0)
                b4 = bias_ref[...].reshape(4 * BLOCK, 2 * BLOCK)
                krows = pl.ds(pl.multiple_of((ib - 1) * BLOCK, BLOCK), 2 * BLOCK)
            nk = BLOCK if first else 2 * BLOCK
            p = jnp.exp(_dot_nt(q4, kc) * scale + b4 - l4)
            ds = p * (_dot_nt(dy4, vc) - c4)
            ds3 = ds.reshape(4, BLOCK, nk)
            if n_bias == 8:
                if first:
                    db_ref[pl.ds(hb, 4), :, BLOCK:] += ds3
                else:
                    db_ref[pl.ds(hb, 4)] += ds3
            elif first:
                db_ref[:, :, BLOCK:] += ds3
            else:
                db_ref[...] += ds3
            ds16 = ds.astype(BF16)
            dq = _unstack_heads(_dot(ds16, kc), lane_head) * scale
            _store_rows(dq_ref, _block_rows(row0, stride, ib), dq.astype(g_dtype), split)
            dk_acc[krows, :] += _dot_tn(ds16, q4) * scale
            dv_acc[krows, :] += _dot_tn(p.astype(BF16), dy4)
            if has_sink:
                for h in range(4):
                    hs = slice(h * BLOCK, (h + 1) * BLOCK)
                    sk = sink_ref[0, h:h + 1, 0:1]
                    val = -jnp.sum(jnp.exp(sk - l4[hs]) * c4[hs], axis=0, keepdims=True)
                    dsk_ref[hb + h] += jnp.broadcast_to(val, (8, 128))

        block(0, True)
        if nb > 1:
            def step(i, carry):
                block(i, False)
                return carry
            lax.fori_loop(1, nb, step, 0, unroll=min(ATTN_UNROLL, nb - 1))

        if kvw == GW:
            all_rows = pl.ds(row0, ln, stride=stride) if stride > 1 else pl.ds(0, ln)
            _store_rows(dk_ref, all_rows, dk_acc[...].astype(g_dtype), split)
            _store_rows(dv_ref, all_rows, dv_acc[...].astype(g_dtype), split)
        else:
            def fold(acc):
                t2 = acc[:, :2 * HEAD_DIM] + acc[:, 2 * HEAD_DIM:]
                t2 = t2 + pltpu.roll(t2, HEAD_DIM, 1)
                lane = lax.broadcasted_iota(jnp.int32, t2.shape, 1) // HEAD_DIM
                return jnp.where(lane == rr, t2, 0.0)

            @pl.when(rr == 0)
            def _():
                dk_half[...] = fold(dk_acc[...])
                dv_half[...] = fold(dv_acc[...])

            @pl.when(rr == 1)
            def _():
                dk_ref[0] = (dk_half[...] + fold(dk_acc[...])).astype(g_dtype)
                dv_ref[0] = (dv_half[...] + fold(dv_acc[...])).astype(g_dtype)

    return pl.pallas_call(
        body, name=name, grid=grid,
        out_shape=(jax.ShapeDtypeStruct(dq_shape, g_dtype), jax.ShapeDtypeStruct(dkv_shape, g_dtype),
                   jax.ShapeDtypeStruct(dkv_shape, g_dtype), jax.ShapeDtypeStruct((n_bias, BLOCK, 2 * BLOCK), F32),
                   jax.ShapeDtypeStruct((8, 8, 128), F32)),
        in_specs=[q_spec, k_spec, v_spec,
                  pl.BlockSpec((4, BLOCK, 2 * BLOCK), bias_map), pl.BlockSpec((1, 4, 128), sink_map),
                  o_spec, o_spec, o_spec],
        out_specs=(o_spec, kv_out_spec, kv_out_spec,
                   pl.BlockSpec((n_bias, BLOCK, 2 * BLOCK), lambda n, r: (0, 0, 0)),
                   pl.BlockSpec((8, 8, 128), lambda n, r: (0, 0, 0))),
        scratch_shapes=[pltpu.VMEM((ln, GW), F32), pltpu.VMEM((ln, GW), F32),
                        pltpu.VMEM((ln, 2 * HEAD_DIM), F32), pltpu.VMEM((ln, 2 * HEAD_DIM), F32)]
        + ([pltpu.VMEM((seq, GW), BF16)] * 2 if expanded else []),
        compiler_params=_params(("arbitrary", "arbitrary"), VMEM_BIG),
    )(q_arr, k_arr, v_arr, bias, sink, dy, cc, lse)


def _bias_grad(ds_all, buckets):
    def body(ds_ref, bk_ref, o_ref):
        rows = lax.broadcasted_iota(jnp.int32, (N_BUCKETS, 128), 0)
        cols = lax.broadcasted_iota(jnp.int32, (N_BUCKETS, 128), 1)

        def per_bucket(b, acc):
            for h in range(20):
                gi = h // 4 if h < 12 else 3
                v = jnp.where(bk_ref[gi] == b, ds_ref[h], 0.0)
                v = jnp.sum(jnp.sum(v, axis=1, keepdims=True), axis=0, keepdims=True)
                acc = jnp.where((rows == b) & (cols == h), v, acc)
            return acc

        o_ref[...] = lax.fori_loop(0, N_BUCKETS, per_bucket, jnp.zeros((N_BUCKETS, 128), F32))

    vm = pl.BlockSpec(memory_space=pltpu.VMEM)
    return pl.pallas_call(body, name="bias_grad", out_shape=jax.ShapeDtypeStruct((N_BUCKETS, 128), F32),
                          in_specs=[vm, vm], out_specs=vm)(ds_all, buckets)


def _adamw(w, g, m, v, name):
    (res,), _ = _adamw_many([(w, g, m, v)], name)
    return res


def _adamw_many(tensors, name, rider=None):
    n = len(tensors)
    r, c = tensors[0][0].shape
    tr = r
    for cand in (256, 176, 128, 88, 64, 32, 16, 8):
        if r % cand == 0 and cand * c * 4 * (4 * 3 + 3 * 2) * n <= 24 * 1024 * 1024:
            tr = cand
            break

    def body(*refs):
        ins, outs = refs[:4 * n], refs[4 * n:]
        for i in range(n):
            w_ref, g_ref, m_ref, v_ref = ins[4 * i:4 * i + 4]
            d, nm, nv = _adam_update(w_ref[...], g_ref[...], m_ref[...], v_ref[...])
            outs[3 * i][...], outs[3 * i + 1][...], outs[3 * i + 2][...] = d, nm, nv

    spec = pl.BlockSpec((tr, c), lambda i: (i, 0))
    spec_in = pl.BlockSpec((tr, c), lambda i: (i, 0), pipeline_mode=STREAM_IN)
    shp = jax.ShapeDtypeStruct((r, c), F32)
    res, ro = _pallas(body, tuple(a for t4 in tensors for a in t4), name=name, grid=(r // tr,),
                      out_shape=(shp,) * (3 * n), in_specs=[spec_in] * (4 * n), out_specs=(spec,) * (3 * n),
                      sem=("parallel",), vmem=VMEM_BIG, rider=rider, stream=True)
    return [tuple(res[3 * i:3 * i + 3]) for i in range(n)], ro


def _t5_bucket(dist):
    max_exact = N_BUCKETS // 2
    n = jnp.maximum(dist, 0)
    nf = jnp.maximum(n, 1).astype(F32)
    large = max_exact + (jnp.log(nf / max_exact) / math.log(MAX_DISTANCE / max_exact)
                         * (N_BUCKETS - max_exact)).astype(jnp.int32)
    large = jnp.minimum(large, N_BUCKETS - 1)
    return jnp.where(n < max_exact, n, large)


def _bias_tables(rel_bias):
    qi = jnp.arange(BLOCK)[:, None]
    ki = jnp.arange(2 * BLOCK)[None, :]
    dist = qi + BLOCK - ki
    specs = [(d, w // d, 4 * gi, 4 * gi + 4) for gi, (w, d) in enumerate(DIL_GROUPS)] + [(1, B_WINDOW - 1, 12, 20)]
    biases, buckets = [], []
    for stride, steps, h0, h1 in specs:
        valid = (dist >= 0) & (dist <= steps)
        bk = jnp.where(valid, _t5_bucket(dist * stride), -1).astype(jnp.int32)
        onehot = (bk[None, :, :] == jnp.arange(N_BUCKETS, dtype=jnp.int32)[:, None, None]).astype(F32)
        b = jnp.einsum("bqk,bh->hqk", onehot, rel_bias[:, h0:h1], precision=lax.Precision.HIGHEST)
        biases.append(jnp.where(valid[None], b, NEG))
        buckets.append(bk)
    return jnp.concatenate(biases, axis=0), jnp.stack(buckets, axis=0)


def _local_step(x, tgt, W, S, shards=None, tail_host=None):
    nseq, seq, _ = x.shape
    t = nseq * seq
    xf = x.reshape(t, D_MODEL)
    bias_all, buckets = _bias_tables(S["rel_bias"])
    sink_b = jnp.broadcast_to(S["sinks"].reshape(2, 4, 1), (2, 4, 128)).astype(F32)
    sink_0 = jnp.zeros((1, 4, 128), F32)
    dist = shards is not None
    W = dict(W)
    G, GS, reduced = {}, {}, {}

    def put(keys, gathered):
        for k, g in zip(keys, gathered):
            W[k] = g.reshape(_FULL_SHAPE.get(k, (N_CHIPS * shards[k].shape[0], D_MODEL)))

    def gather_rider(keys):
        return _GatherRider([shards[k] for k in keys]) if dist else None

    def pair(keys):
        return _pair_reduce([G[k].reshape(N_CHIPS, 2, shards[k].shape[0] // 2, D_MODEL) for k in keys],
                            "grad_pair_reduce_" + keys[0])

    def finish(keys, own, rec):
        full = _final_reduce(own, rec, "grad_final_reduce_" + keys[0])
        off = 0
        for k in keys:
            r = shards[k].shape[0]
            reduced[k] = full[:, off:off + r // 2].reshape(r, D_MODEL)
            off += r // 2

    if dist:
        first = ("wgt1", "wut1", "wd1")
        put(first, _gather_rows([shards[k] for k in first]))
    keys = ("wint",)
    (h1, n1, g1, u1, a1), ro = _ffn_fwd(xf, S["ffn1_norm"], W["wgt1"], W["wut1"], W["wd1"], rider=gather_rider(keys))
    put(keys, ro)
    keys = ("wout", "wat", "wbt", "wgt2")
    (un, za, zb, zg), ro = _inproj_fwd(h1, S["mix_norm"], W["wint"], S["b_in"], nseq, rider=gather_rider(keys))
    put(keys, ro)

    seq3 = lambda a: a.reshape(nseq, seq, a.shape[-1])
    zb3 = seq3(zb)
    pair_blk = lambda cb: pl.BlockSpec((1, 2, seq, 128), lambda n, r, cb=cb: (n, cb, 0, 0))
    a_cfg = []
    outs, lses = [], []
    for gi, (_, d) in enumerate(DIL_GROUPS):
        cfg = dict(grid=(nseq, d), seq=seq, stride=d, kvw=GW, split=True,
                   q_spec=pair_blk(gi), k_spec=pair_blk(3 + gi), v_spec=pair_blk(6 + gi), o_spec=pair_blk(0),
                   bias_map=lambda n, r: (0, 0, 0), sink_map=lambda n, r: (0, 0, 0), has_sink=False)
        a_cfg.append(cfg)
        (o, lse), _ = _attn_fwd(za, za, za, bias_all[4 * gi:4 * gi + 4], sink_0, o_shape=(nseq, 2, seq, 128),
                                o_dtype=F32, name=f"attn_a{gi}_fwd", **cfg)
        outs.append(o)
        lses.append(lse)
    wide_blk = lambda w, cmap: pl.BlockSpec((1, seq, w), cmap)
    b_cfg = dict(grid=(nseq, 2), seq=seq, stride=1, kvw=2 * HEAD_DIM, split=False,
                 q_spec=wide_blk(GW, lambda n, r: (n, 0, r)), k_spec=wide_blk(2 * HEAD_DIM, lambda n, r: (n, 0, 4)),
                 v_spec=wide_blk(2 * HEAD_DIM, lambda n, r: (n, 0, 5)), o_spec=wide_blk(GW, lambda n, r: (n, 0, r)),
                 bias_map=lambda n, r: (r, 0, 0), sink_map=lambda n, r: (r, 0, 0), has_sink=True)
    keys = ("wut2",)
    bias_b_fwd = bias_all[12:20].at[:, :, 0].set(jnp.broadcast_to(S["sinks"].reshape(8, 1), (8, BLOCK)))
    (yb, lse_b), ro = _attn_fwd(zb3, zb3, zb3, bias_b_fwd, sink_b, o_shape=(nseq, seq, 2 * GW), o_dtype=BF16,
                                name="attn_b_fwd", rider=gather_rider(keys), **b_cfg)
    put(keys, ro)
    yb = yb.reshape(t, 2 * GW)

    keys = ("wd2",)
    (h2, y, lse_tot, pa, pb, merged), ro = _merge_fwd(outs[0], outs[1], outs[2], lses[0], lses[1], lses[2], yb, zg, h1,
                                                      W["wat"], W["wbt"], W["wout"], rider=gather_rider(keys))
    put(keys, ro)
    (dh3, n2, g2, u2, a2, loss_part, g_final), _ = _ffn_fwd(
        h2, S["ffn2_norm"], W["wgt2"], W["wut2"], W["wd2"],
        head=(S["final_norm"].reshape(1, D_MODEL), tgt.reshape(t, D_MODEL)))

    GS["final_norm"] = g_final
    dh2, dg2, du2, df2, GS["ffn2_norm"] = _ffn_bwd(dh3, h2, S["ffn2_norm"], g2, u2, W["wgt2"], W["wut2"], W["wd2"])
    G["wgt2"] = _wgrad(dg2, n2, MXU_DIM, name="wgrad_gate2")
    G["wut2"] = _wgrad(du2, n2, MXU_DIM, name="wgrad_up2")
    G["wd2"] = _wgrad(a2, df2, MXU_DIM, name="wgrad_down2")

    keys = ("wgt2", "wut2", "wd2")
    rider = _ExchangeRider([pair(keys)]) if dist else None
    (dpa, dpb, dga, dgb, dya, dyb, dh2b, ca, cb), ro = _merge_bwd(dh2, pa, pb, zg, y, yb, W["wat"], W["wbt"], W["wout"],
                                                                  nseq, rider=rider)
    if dist:
        finish(keys, *ro)

    dqs, dks, dvs, dbs = [], [], [], []
    shp = (nseq, 2, seq, 128)
    halves = lambda a: [a[:, hf].reshape(t, 128).astype(BF16) for hf in range(2)]
    for gi in range(len(DIL_GROUPS)):
        dq, dk, dv, db, _ = _attn_bwd(za, za, za, bias_all[4 * gi:4 * gi + 4], sink_0, dya, ca, lse_tot,
                                      n_bias=4, dq_shape=shp, dkv_shape=shp, g_dtype=F32,
                                      kv_out_spec=a_cfg[gi]["o_spec"], name=f"attn_a{gi}_bwd", **a_cfg[gi])
        dqs += halves(dq)
        dks += halves(dk)
        dvs += halves(dv)
        dbs.append(db)
    dqb, dkb, dvb, dbb, dsink = _attn_bwd(zb3, zb3, zb3, bias_all[12:20], sink_b, seq3(dyb), seq3(cb), lse_b,
                                          n_bias=8, dq_shape=(nseq, seq, 2 * GW),
                                          dkv_shape=(nseq, seq, 2 * HEAD_DIM), g_dtype=BF16,
                                          kv_out_spec=wide_blk(2 * HEAD_DIM, lambda n, r: (n, 0, 0)),
                                          name="attn_b_bwd", **b_cfg)
    dz = jnp.concatenate(dqs + dks + dvs + [dqb.reshape(t, 2 * GW), dkb.reshape(t, 2 * HEAD_DIM),
                                            dvb.reshape(t, 2 * HEAD_DIM), dga, dgb], axis=-1)
    gb_tab = _bias_grad(jnp.concatenate(dbs + [dbb], axis=0), buckets)
    if dist:
        GS["bias_tab"], GS["sink_tiles"] = gb_tab, dsink
    else:
        GS["rel_bias"] = gb_tab[:, :20]
        GS["sinks"] = dsink[:, 0, 0].reshape(1, 8)

    G["wint"], GS["b_in"] = _wgrad(dz, un, MXU_DIM, with_colsum=True, name="wgrad_in")
    G["wout"] = _wgrad(merged, dh2b, MXU_DIM, name="wgrad_out")
    G["wat"] = _wgrad(dpa, y, MXU_DIM, name="wgrad_branch_a")
    G["wbt"] = _wgrad(dpb, yb, MXU_DIM, name="wgrad_branch_b")
    keys = ("wint", "wout", "wat", "wbt")
    rider = _ExchangeRider([pair(keys)]) if dist else None
    (dh1, GS["mix_norm"]), ro = _inproj_bwd(dz, dh2, h1, S["mix_norm"], W["wint"], rider=rider)
    if dist:
        finish(keys, *ro)

    dx, dg1, du1, df1, GS["ffn1_norm"] = _ffn_bwd(dh1, xf, S["ffn1_norm"], g1, u1, W["wgt1"], W["wut1"], W["wd1"])
    G["wgt1"] = _wgrad(dg1, n1, MXU_DIM, name="wgrad_gate1")
    if dist:
        G["wut1"], ro = _wgrad(du1, n1, MXU_DIM, name="wgrad_up1", rider=_ExchangeRider([pair(("wgt1",))]))
        finish(("wgt1",), *ro)
        G["wd1"], ro = _wgrad(a1, df1, MXU_DIM, name="wgrad_down1", rider=_ExchangeRider([pair(("wut1",))]))
        finish(("wut1",), *ro)
        finish(("wd1",), *tail_host(_ExchangeRider([pair(("wd1",))]), reduced))
    else:
        G["wut1"] = _wgrad(du1, n1, MXU_DIM, name="wgrad_up1")
        G["wd1"] = _wgrad(a1, df1, MXU_DIM, name="wgrad_down1")
    return loss_part, dx.reshape(x.shape), (reduced if dist else G), GS


_SMALL = ("ffn1_norm", "mix_norm", "ffn2_norm", "final_norm", "b_in", "sinks", "rel_bias")
_ORDER = ("ffn1_norm", "ffn1_w_gate", "ffn1_w_up", "ffn1_w_down", "mix_norm", "w_in", "b_in", "w_branch_a",
          "w_branch_b", "w_out", "sinks", "rel_bias", "ffn2_norm", "ffn2_w_gate", "ffn2_w_up", "ffn2_w_down",
          "final_norm")
_BIG = (("wgt1", "ffn1_w_gate", True, 704), ("wut1", "ffn1_w_up", True, 704), ("wd1", "ffn1_w_down", False, 704),
        ("wint", "w_in", True, 1280), ("wout", "w_out", False, 256), ("wat", "w_branch_a", True, 64),
        ("wbt", "w_branch_b", True, 128), ("wgt2", "ffn2_w_gate", True, 704), ("wut2", "ffn2_w_up", True, 704),
        ("wd2", "ffn2_w_down", False, 704))
_FULL_SHAPE = {"wat": (D_MODEL, GW), "wbt": (D_MODEL, 2 * GW)}


def kernel(x, ffn1_norm, ffn1_w_gate, ffn1_w_up, ffn1_w_down, mix_norm, w_in, b_in, w_branch_a, w_branch_b, w_out, sinks, rel_bias, ffn2_norm, ffn2_w_gate, ffn2_w_up, ffn2_w_down, final_norm, loss_target, m_ffn1_norm, m_ffn1_w_gate, m_ffn1_w_up, m_ffn1_w_down, m_mix_norm, m_w_in, m_b_in, m_w_branch_a, m_w_branch_b, m_w_out, m_sinks, m_rel_bias, m_ffn2_norm, m_ffn2_w_gate, m_ffn2_w_up, m_ffn2_w_down, m_final_norm, v_ffn1_norm, v_ffn1_w_gate, v_ffn1_w_up, v_ffn1_w_down, v_mix_norm, v_w_in, v_b_in, v_w_branch_a, v_w_branch_b, v_w_out, v_sinks, v_rel_bias, v_ffn2_norm, v_ffn2_w_gate, v_ffn2_w_up, v_ffn2_w_down, v_final_norm):
    args = dict(locals())
    w = {n: args[n] for n in _ORDER}
    m = {n: args["m_" + n] for n in _ORDER}
    v = {n: args["v_" + n] for n in _ORDER}

    shards = {}
    for key, name, transposed, rows in _BIG:
        a = w[name][0]
        a = (a.T if transposed else a).astype(BF16)
        shards[key] = a.reshape(rows, D_MODEL)
    S = {n: w[n] for n in _SMALL}

    row_adam = lambda n: (w[n][0].T, m[n][0].T, v[n][0].T)
    early = {}

    def tail_host(rider, reduced):
        tensors = []
        for key, n in (("wgt2", "ffn2_w_gate"), ("wut2", "ffn2_w_up"), ("wd2", "ffn2_w_down")):
            wmv = row_adam(n) if key != "wd2" else (w[n][0], m[n][0], v[n][0])
            tensors.append((wmv[0], reduced[key], wmv[1], wmv[2]))
        res, ro = _adamw_many(tensors, "adamw_ffn2", rider=rider)
        early["ffn2_w_gate"], early["ffn2_w_up"], early["ffn2_w_down"] = res
        return ro

    loss_part, grad_x, reduced, GS = _local_step(x, loss_target, {}, S, shards, tail_host)

    small = _allreduce_small(GS["ffn1_norm"], GS["mix_norm"], GS["ffn2_norm"], GS["final_norm"], GS["b_in"],
                             GS["sink_tiles"], GS["bias_tab"], loss_part)
    loss = small[9, 8]

    out_g, out_d, out_m, out_v = {}, {}, {}, {}
    for key, n, transposed, rows in _BIG:
        nat = w[n][0].shape
        if transposed and nat[1] % 128:
            res = early[n] if n in early else _adamw(row_adam(n)[0], reduced[key], *row_adam(n)[1:], "adamw_" + n)
            res = [reduced[key].T] + [r.T for r in res]
        elif n in early:
            res = [reduced[key]] + list(early[n])
        else:
            g = reduced[key].reshape(nat[1], nat[0]).T if transposed else reduced[key].reshape(nat)
            res = [g] + list(_adamw(w[n][0], g, m[n][0], v[n][0], "adamw_" + n))
        out_g[n], out_d[n], out_m[n], out_v[n] = [r[None] for r in res]
    row = lambda d: {n: (d[n].reshape(1, D_MODEL) if n == "final_norm" else d[n]) for n in _SMALL}
    for dst, src in zip((out_g, out_d, out_m, out_v), _adamw_small(small, row(w), row(m), row(v))):
        dst.update(src)
        dst["final_norm"] = src["final_norm"].reshape(D_MODEL)

    return (loss, grad_x, *[out_g[n] for n in _ORDER], *[out_d[n] for n in _ORDER],
            *[out_m[n] for n in _ORDER], *[out_v[n] for n in _ORDER])
```
